```python
import jax, jax.numpy as jnp
from jax import lax
import numpy as np

D_MODEL = 1024
BATCH = 8
SEQ = 2048
DEPTH = 1

D_FF = 2816
POOL_WIDTH = D_MODEL // 2
POOL_WINDOWS = (2, 4, 8, 16)
N_POOL_GROUPS = len(POOL_WINDOWS)
POOL_GROUP = POOL_WIDTH // N_POOL_GROUPS
N_SB_HEADS = 8
SB_HEAD_DIM = 64
SB_WIDTH = N_SB_HEADS * SB_HEAD_DIM
Q_BLOCK = 128
IN_WIDTH = POOL_WIDTH + 3 * SB_WIDTH + 2 * D_MODEL
RMS_EPS = 1e-6

kernel_name = "macaron_pool_stickbreaking_gated_hybrid"


def rmsnorm(x, g):
    xf = x.astype(jnp.float32)
    r = lax.rsqrt(jnp.mean(xf * xf, axis=-1, keepdims=True) + RMS_EPS)
    return (xf * r * g.astype(jnp.float32)).astype(x.dtype)


def swiglu(x, w_gate_up, w_down):
    gu = x @ w_gate_up
    g, u = jnp.split(gu, 2, axis=-1)
    return (jax.nn.silu(g) * u) @ w_down


def causal_pool_mixer(xp, w_group, pool_scale):
    b, s, _ = xp.shape
    xg = xp.reshape(b, s, N_POOL_GROUPS, POOL_GROUP)
    pos = jnp.arange(s, dtype=jnp.int32)
    outs = []
    for gi, w in enumerate(POOL_WINDOWS):
        xi = xg[:, :, gi, :].astype(jnp.float32)
        cs = jnp.cumsum(xi, axis=1)
        lower = jnp.concatenate([jnp.zeros((b, w, POOL_GROUP), cs.dtype), cs[:, : s - w]], axis=1)
        count = jnp.minimum(pos + 1, w).astype(jnp.float32)[None, :, None]
        mean = (cs - lower) / count
        outs.append((mean - xi).astype(xp.dtype))
    y = jnp.stack(outs, axis=2)
    y = jnp.einsum('bsgc,gcd->bsgd', y, w_group)
    return y.reshape(b, s, POOL_WIDTH) * pool_scale


def stick_breaking_attention(q, k, v):
    s_len = q.shape[2]
    scale = 1.0 / np.sqrt(SB_HEAD_DIM)
    outs = []
    for blk in range(s_len // Q_BLOCK):
        i0, i1 = blk * Q_BLOCK, (blk + 1) * Q_BLOCK
        qb = q[:, :, i0:i1]
        kb = k[:, :, :i1]
        vb = v[:, :, :i1]
        z = jnp.einsum('bhqd,bhkd->bhqk', qb, kb).astype(jnp.float32) * scale
        qpos = jnp.arange(i0, i1)[:, None]
        kpos = jnp.arange(i1)[None, :]
        mask = kpos < qpos
        log_beta = jax.nn.log_sigmoid(z)
        log_1m_beta = jnp.where(mask, jax.nn.log_sigmoid(-z), 0.0)
        log_a = log_beta + lax.cumsum(log_1m_beta, axis=3, reverse=True) - log_1m_beta
        a = jnp.where(mask, jnp.exp(log_a), 0.0)
        outs.append(jnp.einsum('bhqk,bhkd->bhqd', a.astype(vb.dtype), vb))
    return jnp.concatenate(outs, axis=2)


def gated_mixer_block(u, w_in, pool_w_group, pool_scale, w_branch_pool, w_branch_attn, w_out):
    b, s, _ = u.shape
    proj = u @ w_in
    o1 = POOL_WIDTH
    o2 = o1 + SB_WIDTH
    o3 = o2 + SB_WIDTH
    o4 = o3 + SB_WIDTH
    xp = proj[..., :o1]
    q = proj[..., o1:o2].reshape(b, s, N_SB_HEADS, SB_HEAD_DIM).transpose(0, 2, 1, 3)
    k = proj[..., o2:o3].reshape(b, s, N_SB_HEADS, SB_HEAD_DIM).transpose(0, 2, 1, 3)
    v = proj[..., o3:o4].reshape(b, s, N_SB_HEADS, SB_HEAD_DIM).transpose(0, 2, 1, 3)
    gate_logits = proj[..., o4:]
    y_pool = causal_pool_mixer(xp, pool_w_group, pool_scale) @ w_branch_pool
    o_sb = stick_breaking_attention(q, k, v).transpose(0, 2, 1, 3).reshape(b, s, SB_WIDTH)
    y_sb = o_sb @ w_branch_attn
    g = jax.nn.sigmoid(gate_logits.astype(jnp.float32)).astype(u.dtype)
    g_pool, g_sb = jnp.split(g, 2, axis=-1)
    return (g_pool * y_pool + g_sb * y_sb) @ w_out


def _fwd_setup_inputs(seed: int = 0) -> dict:
    key = jax.random.key(seed)
    ks = jax.random.split(key, 20)
    f32 = jnp.float32

    def w(k, shape, fan_in):
        return jax.random.normal(k, shape, f32) * (fan_in ** -0.5)

    def gain(k, shape):
        return 1.0 + 0.05 * jax.random.normal(k, shape, f32)

    L = DEPTH
    return {
        "x": jax.random.normal(ks[0], (BATCH, SEQ, D_MODEL), f32),
        "ffn1_norm": gain(ks[1], (L, D_MODEL)),
        "ffn1_w_gate_up": w(ks[2], (L, D_MODEL, 2 * D_FF), D_MODEL),
        "ffn1_w_down": w(ks[3], (L, D_FF, D_MODEL), D_FF),
        "mix_norm": gain(ks[4], (L, D_MODEL)),
        "w_in": w(ks[5], (L, D_MODEL, IN_WIDTH), D_MODEL),
        "pool_w_group": w(ks[6], (L, N_POOL_GROUPS, POOL_GROUP, POOL_GROUP), POOL_GROUP),
        "pool_scale": gain(ks[7], (L, POOL_WIDTH)),
        "w_branch_pool": w(ks[8], (L, POOL_WIDTH, D_MODEL), POOL_WIDTH),
        "w_branch_attn": w(ks[9], (L, SB_WIDTH, D_MODEL), SB_WIDTH),
        "w_out": w(ks[10], (L, D_MODEL, D_MODEL), D_MODEL),
        "ffn2_norm": gain(ks[11], (L, D_MODEL)),
        "ffn2_w_gate_up": w(ks[12], (L, D_MODEL, 2 * D_FF), D_MODEL),
        "ffn2_w_down": w(ks[13], (L, D_FF, D_MODEL), D_FF),
        "final_norm": gain(ks[14], (D_MODEL,)),
    }


def _fwd_reference(x, ffn1_norm, ffn1_w_gate_up, ffn1_w_down, mix_norm, w_in, pool_w_group,
              pool_scale, w_branch_pool, w_branch_attn, w_out, ffn2_norm, ffn2_w_gate_up,
              ffn2_w_down, final_norm):
    h = x
    for l in range(DEPTH):
        h = h + 0.5 * swiglu(rmsnorm(h, ffn1_norm[l]), ffn1_w_gate_up[l], ffn1_w_down[l])
        u = rmsnorm(h, mix_norm[l])
        h = h + gated_mixer_block(u, w_in[l], pool_w_group[l], pool_scale[l],
                                  w_branch_pool[l], w_branch_attn[l], w_out[l])
        h = h + 0.5 * swiglu(rmsnorm(h, ffn2_norm[l]), ffn2_w_gate_up[l], ffn2_w_down[l])
    return rmsnorm(h, final_norm)


import jax as _jax
import jax.numpy as _jnp

TWIN_FORMAT = 'train_step'
FWD_PARAMS = ['x', 'ffn1_norm', 'ffn1_w_gate_up', 'ffn1_w_down', 'mix_norm', 'w_in', 'pool_w_group', 'pool_scale', 'w_branch_pool', 'w_branch_attn', 'w_out', 'ffn2_norm', 'ffn2_w_gate_up', 'ffn2_w_down', 'final_norm']
TWIN_WEIGHTS = ['ffn1_norm', 'ffn1_w_gate_up', 'ffn1_w_down', 'mix_norm', 'w_in', 'pool_w_group', 'pool_scale', 'w_branch_pool', 'w_branch_attn', 'w_out', 'ffn2_norm', 'ffn2_w_gate_up', 'ffn2_w_down', 'final_norm']
TWIN_DIFF_INPUT = 'x'
TWIN_INPUTS = ['x', 'ffn1_norm', 'ffn1_w_gate_up', 'ffn1_w_down', 'mix_norm', 'w_in', 'pool_w_group', 'pool_scale', 'w_branch_pool', 'w_branch_attn', 'w_out', 'ffn2_norm', 'ffn2_w_gate_up', 'ffn2_w_down', 'final_norm', 'loss_target', 'm_ffn1_norm', 'm_ffn1_w_gate_up', 'm_ffn1_w_down', 'm_mix_norm', 'm_w_in', 'm_pool_w_group', 'm_pool_scale', 'm_w_branch_pool', 'm_w_branch_attn', 'm_w_out', 'm_ffn2_norm', 'm_ffn2_w_gate_up', 'm_ffn2_w_down', 'm_final_norm', 'v_ffn1_norm', 'v_ffn1_w_gate_up', 'v_ffn1_w_down', 'v_mix_norm', 'v_w_in', 'v_pool_w_group', 'v_pool_scale', 'v_w_branch_pool', 'v_w_branch_attn', 'v_w_out', 'v_ffn2_norm', 'v_ffn2_w_gate_up', 'v_ffn2_w_down', 'v_final_norm']
TWIN_OUTPUTS = ['loss', 'grad_x', 'grad_ffn1_norm', 'grad_ffn1_w_gate_up', 'grad_ffn1_w_down', 'grad_mix_norm', 'grad_w_in', 'grad_pool_w_group', 'grad_pool_scale', 'grad_w_branch_pool', 'grad_w_branch_attn', 'grad_w_out', 'grad_ffn2_norm', 'grad_ffn2_w_gate_up', 'grad_ffn2_w_down', 'grad_final_norm', 'delta_ffn1_norm', 'delta_ffn1_w_gate_up', 'delta_ffn1_w_down', 'delta_mix_norm', 'delta_w_in', 'delta_pool_w_group', 'delta_pool_scale', 'delta_w_branch_pool', 'delta_w_branch_attn', 'delta_w_out', 'delta_ffn2_norm', 'delta_ffn2_w_gate_up', 'delta_ffn2_w_down', 'delta_final_norm', 'new_m_ffn1_norm', 'new_m_ffn1_w_gate_up', 'new_m_ffn1_w_down', 'new_m_mix_norm', 'new_m_w_in', 'new_m_pool_w_group', 'new_m_pool_scale', 'new_m_w_branch_pool', 'new_m_w_branch_attn', 'new_m_w_out', 'new_m_ffn2_norm', 'new_m_ffn2_w_gate_up', 'new_m_ffn2_w_down', 'new_m_final_norm', 'new_v_ffn1_norm', 'new_v_ffn1_w_gate_up', 'new_v_ffn1_w_down', 'new_v_mix_norm', 'new_v_w_in', 'new_v_pool_w_group', 'new_v_pool_scale', 'new_v_w_branch_pool', 'new_v_w_branch_attn', 'new_v_w_out', 'new_v_ffn2_norm', 'new_v_ffn2_w_gate_up', 'new_v_ffn2_w_down', 'new_v_final_norm']
TWIN_LEAF_KINDS = {'loss': 'loss', 'grad_x': 'grad_x', 'grad_ffn1_norm': 'grad_w', 'grad_ffn1_w_gate_up': 'grad_w', 'grad_ffn1_w_down': 'grad_w', 'grad_mix_norm': 'grad_w', 'grad_w_in': 'grad_w', 'grad_pool_w_group': 'grad_w', 'grad_pool_scale': 'grad_w', 'grad_w_branch_pool': 'grad_w', 'grad_w_branch_attn': 'grad_w', 'grad_w_out': 'grad_w', 'grad_ffn2_norm': 'grad_w', 'grad_ffn2_w_gate_up': 'grad_w', 'grad_ffn2_w_down': 'grad_w', 'grad_final_norm': 'grad_w', 'delta_ffn1_norm': 'delta_w', 'delta_ffn1_w_gate_up': 'delta_w', 'delta_ffn1_w_down': 'delta_w', 'delta_mix_norm': 'delta_w', 'delta_w_in': 'delta_w', 'delta_pool_w_group': 'delta_w', 'delta_pool_scale': 'delta_w', 'delta_w_branch_pool': 'delta_w', 'delta_w_branch_attn': 'delta_w', 'delta_w_out': 'delta_w', 'delta_ffn2_norm': 'delta_w', 'delta_ffn2_w_gate_up': 'delta_w', 'delta_ffn2_w_down': 'delta_w', 'delta_final_norm': 'delta_w', 'new_m_ffn1_norm': 'new_m', 'new_m_ffn1_w_gate_up': 'new_m', 'new_m_ffn1_w_down': 'new_m', 'new_m_mix_norm': 'new_m', 'new_m_w_in': 'new_m', 'new_m_pool_w_group': 'new_m', 'new_m_pool_scale': 'new_m', 'new_m_w_branch_pool': 'new_m', 'new_m_w_branch_attn': 'new_m', 'new_m_w_out': 'new_m', 'new_m_ffn2_norm': 'new_m', 'new_m_ffn2_w_gate_up': 'new_m', 'new_m_ffn2_w_down': 'new_m', 'new_m_final_norm': 'new_m', 'new_v_ffn1_norm': 'new_v', 'new_v_ffn1_w_gate_up': 'new_v', 'new_v_ffn1_w_down': 'new_v', 'new_v_mix_norm': 'new_v', 'new_v_w_in': 'new_v', 'new_v_pool_w_group': 'new_v', 'new_v_pool_scale': 'new_v', 'new_v_w_branch_pool': 'new_v', 'new_v_w_branch_attn': 'new_v', 'new_v_w_out': 'new_v', 'new_v_ffn2_norm': 'new_v', 'new_v_ffn2_w_gate_up': 'new_v', 'new_v_ffn2_w_down': 'new_v', 'new_v_final_norm': 'new_v'}


def _forward(args):
    return _fwd_reference(*[args[k] for k in FWD_PARAMS])


def _output_shape():
    out = _jax.eval_shape(lambda: _forward(_fwd_setup_inputs(0)))
    return out.shape, out.dtype

N_MICROBATCH = 1
ADAM_LR = 0.001
ADAM_B1 = 0.9
ADAM_B2 = 0.999
ADAM_EPS = 1e-08
ADAM_WD = 0.01
ADAM_STEP = 10
PER_EXAMPLE_BATCH_AXIS = {'x': 0, 'loss_target': 0}
SHARED_INPUTS = []
_WEIGHT_DTYPES = {'ffn1_norm': _jnp.float32, 'ffn1_w_gate_up': _jnp.float32, 'ffn1_w_down': _jnp.float32, 'mix_norm': _jnp.float32, 'w_in': _jnp.float32, 'pool_w_group': _jnp.float32, 'pool_scale': _jnp.float32, 'w_branch_pool': _jnp.float32, 'w_branch_attn': _jnp.float32, 'w_out': _jnp.float32, 'ffn2_norm': _jnp.float32, 'ffn2_w_gate_up': _jnp.float32, 'ffn2_w_down': _jnp.float32, 'final_norm': _jnp.float32}
MOMENT_SCALE = {'ffn1_norm': 5.793593e-02, 'ffn1_w_gate_up': 2.359454e-02, 'ffn1_w_down': 3.847654e-02, 'mix_norm': 7.359712e-02, 'w_in': 3.682282e-02, 'pool_w_group': 7.536098e-02, 'pool_scale': 7.597528e-02, 'w_branch_pool': 5.263804e-02, 'w_branch_attn': 3.808030e-02, 'w_out': 6.529205e-02, 'ffn2_norm': 4.397347e-02, 'ffn2_w_gate_up': 1.876771e-02, 'ffn2_w_down': 3.065990e-02, 'final_norm': 1.603285e+01}


def _to_microbatches(a, axis):
    t = _jnp.moveaxis(a, axis, 0)
    t = t.reshape((N_MICROBATCH, t.shape[0] // N_MICROBATCH) + t.shape[1:])
    return _jnp.moveaxis(t, 1, axis + 1)


def setup_inputs(seed: int = 0) -> dict:
    inp = _fwd_setup_inputs(seed)
    key = _jax.random.fold_in(_jax.random.key(seed), 7919)
    shape, _ = _output_shape()
    out = dict(inp)
    out["loss_target"] = _jax.random.normal(_jax.random.fold_in(key, 0), shape, _jnp.float32)
    for i, name in enumerate(TWIN_WEIGHTS):
        w = inp[name].astype(_jnp.float32)
        if MOMENT_SCALE is None:
            s = _jnp.sqrt(_jnp.mean(_jnp.square(w)) + 1e-30)
        else:
            s = MOMENT_SCALE[name]
        km, kv = _jax.random.split(_jax.random.fold_in(key, i + 1))
        out[name] = w
        out["m_" + name] = s * _jax.random.normal(km, w.shape, _jnp.float32)
        out["v_" + name] = (s * s) * _jax.random.uniform(kv, w.shape, _jnp.float32, 0.5, 1.5)
    if N_MICROBATCH > 1:
        for name, axis in PER_EXAMPLE_BATCH_AXIS.items():
            out[name] = _to_microbatches(out[name], axis)
    return {'x': out['x'], 'ffn1_norm': out['ffn1_norm'], 'ffn1_w_gate_up': out['ffn1_w_gate_up'], 'ffn1_w_down': out['ffn1_w_down'], 'mix_norm': out['mix_norm'], 'w_in': out['w_in'], 'pool_w_group': out['pool_w_group'], 'pool_scale': out['pool_scale'], 'w_branch_pool': out['w_branch_pool'], 'w_branch_attn': out['w_branch_attn'], 'w_out': out['w_out'], 'ffn2_norm': out['ffn2_norm'], 'ffn2_w_gate_up': out['ffn2_w_gate_up'], 'ffn2_w_down': out['ffn2_w_down'], 'final_norm': out['final_norm'], 'loss_target': out['loss_target'], 'm_ffn1_norm': out['m_ffn1_norm'], 'm_ffn1_w_gate_up': out['m_ffn1_w_gate_up'], 'm_ffn1_w_down': out['m_ffn1_w_down'], 'm_mix_norm': out['m_mix_norm'], 'm_w_in': out['m_w_in'], 'm_pool_w_group': out['m_pool_w_group'], 'm_pool_scale': out['m_pool_scale'], 'm_w_branch_pool': out['m_w_branch_pool'], 'm_w_branch_attn': out['m_w_branch_attn'], 'm_w_out': out['m_w_out'], 'm_ffn2_norm': out['m_ffn2_norm'], 'm_ffn2_w_gate_up': out['m_ffn2_w_gate_up'], 'm_ffn2_w_down': out['m_ffn2_w_down'], 'm_final_norm': out['m_final_norm'], 'v_ffn1_norm': out['v_ffn1_norm'], 'v_ffn1_w_gate_up': out['v_ffn1_w_gate_up'], 'v_ffn1_w_down': out['v_ffn1_w_down'], 'v_mix_norm': out['v_mix_norm'], 'v_w_in': out['v_w_in'], 'v_pool_w_group': out['v_pool_w_group'], 'v_pool_scale': out['v_pool_scale'], 'v_w_branch_pool': out['v_w_branch_pool'], 'v_w_branch_attn': out['v_w_branch_attn'], 'v_w_out': out['v_w_out'], 'v_ffn2_norm': out['v_ffn2_norm'], 'v_ffn2_w_gate_up': out['v_ffn2_w_gate_up'], 'v_ffn2_w_down': out['v_ffn2_w_down'], 'v_final_norm': out['v_final_norm']}


def _loss(weights, diff, rest, loss_target):
    with _jax.named_scope("forward"):
        args = {**rest, TWIN_DIFF_INPUT: diff, **{k: w.astype(_WEIGHT_DTYPES[k]) for k, w in weights.items()}}
        y = _forward(args)
    with _jax.named_scope("loss_head"):
        err = _jnp.square(y.astype(_jnp.float32) - loss_target)
        return 0.5 * _jnp.sum(_jnp.mean(err, axis=-1)) if err.ndim else 0.5 * err


def _adamw(w, g, m, v):
    m = ADAM_B1 * m + (1.0 - ADAM_B1) * g
    v = ADAM_B2 * v + (1.0 - ADAM_B2) * _jnp.square(g)
    m_hat = m / (1.0 - ADAM_B1 ** ADAM_STEP)
    v_hat = v / (1.0 - ADAM_B2 ** ADAM_STEP)
    delta = -ADAM_LR * (m_hat / (_jnp.sqrt(v_hat) + ADAM_EPS) + ADAM_WD * w)
    return delta, m, v


def reference(x, ffn1_norm, ffn1_w_gate_up, ffn1_w_down, mix_norm, w_in, pool_w_group, pool_scale, w_branch_pool, w_branch_attn, w_out, ffn2_norm, ffn2_w_gate_up, ffn2_w_down, final_norm, loss_target, m_ffn1_norm, m_ffn1_w_gate_up, m_ffn1_w_down, m_mix_norm, m_w_in, m_pool_w_group, m_pool_scale, m_w_branch_pool, m_w_branch_attn, m_w_out, m_ffn2_norm, m_ffn2_w_gate_up, m_ffn2_w_down, m_final_norm, v_ffn1_norm, v_ffn1_w_gate_up, v_ffn1_w_down, v_mix_norm, v_w_in, v_pool_w_group, v_pool_scale, v_w_branch_pool, v_w_branch_attn, v_w_out, v_ffn2_norm, v_ffn2_w_gate_up, v_ffn2_w_down, v_final_norm):
    given = dict(x=x, ffn1_norm=ffn1_norm, ffn1_w_gate_up=ffn1_w_gate_up, ffn1_w_down=ffn1_w_down, mix_norm=mix_norm, w_in=w_in, pool_w_group=pool_w_group, pool_scale=pool_scale, w_branch_pool=w_branch_pool, w_branch_attn=w_branch_attn, w_out=w_out, ffn2_norm=ffn2_norm, ffn2_w_gate_up=ffn2_w_gate_up, ffn2_w_down=ffn2_w_down, final_norm=final_norm, loss_target=loss_target, m_ffn1_norm=m_ffn1_norm, m_ffn1_w_gate_up=m_ffn1_w_gate_up, m_ffn1_w_down=m_ffn1_w_down, m_mix_norm=m_mix_norm, m_w_in=m_w_in, m_pool_w_group=m_pool_w_group, m_pool_scale=m_pool_scale, m_w_branch_pool=m_w_branch_pool, m_w_branch_attn=m_w_branch_attn, m_w_out=m_w_out, m_ffn2_norm=m_ffn2_norm, m_ffn2_w_gate_up=m_ffn2_w_gate_up, m_ffn2_w_down=m_ffn2_w_down, m_final_norm=m_final_norm, v_ffn1_norm=v_ffn1_norm, v_ffn1_w_gate_up=v_ffn1_w_gate_up, v_ffn1_w_down=v_ffn1_w_down, v_mix_norm=v_mix_norm, v_w_in=v_w_in, v_pool_w_group=v_pool_w_group, v_pool_scale=v_pool_scale, v_w_branch_pool=v_w_branch_pool, v_w_branch_attn=v_w_branch_attn, v_w_out=v_w_out, v_ffn2_norm=v_ffn2_norm, v_ffn2_w_gate_up=v_ffn2_w_gate_up, v_ffn2_w_down=v_ffn2_w_down, v_final_norm=v_final_norm)
    weights = {n: given[n] for n in TWIN_WEIGHTS}
    shared = {n: given[n] for n in SHARED_INPUTS}
    per_example = {n: given[n] for n in ['x']}
    grad_fn = _jax.value_and_grad(_loss, argnums=(0, 1))

    def one_microbatch(ex, loss_target):
        ex = dict(ex)
        diff = ex.pop(TWIN_DIFF_INPUT)
        return grad_fn(weights, diff, {**shared, **ex}, loss_target)

    if N_MICROBATCH == 1:
        loss, (grad_w, grad_x) = one_microbatch(per_example, given["loss_target"])
    else:
        def body(carry, xs):
            loss_sum, grad_sum = carry
            l_k, (gw_k, gx_k) = one_microbatch(xs[0], xs[1])
            with _jax.named_scope("update"):
                return (loss_sum + l_k, _jax.tree.map(_jnp.add, grad_sum, gw_k)), gx_k

        init = (_jnp.zeros((), _jnp.float32), _jax.tree.map(_jnp.zeros_like, weights))
        (loss, grad_w), grad_x = _jax.lax.scan(body, init, (per_example, given["loss_target"]))
    with _jax.named_scope("update"):
        delta_w, new_m, new_v = {}, {}, {}
        for n in TWIN_WEIGHTS:
            delta_w[n], new_m[n], new_v[n] = _adamw(weights[n], grad_w[n], given["m_" + n], given["v_" + n])
    return (loss, grad_x, *[grad_w[n] for n in TWIN_WEIGHTS], *[delta_w[n] for n in TWIN_WEIGHTS],
            *[new_m[n] for n in TWIN_WEIGHTS], *[new_v[n] for n in TWIN_WEIGHTS])
```

```python
import functools

import jax
import jax.numpy as jnp
from jax import lax
from jax.experimental import pallas as pl
from jax.experimental.pallas import tpu as pltpu

F32 = jnp.float32
BF16 = jnp.bfloat16
MESH = pl.DeviceIdType.MESH

RMS_EPS = 1e-6
N_DEV = 8
N_HEADS = 8
HEAD_DIM = 64
HEAD_PAIR = 2 * HEAD_DIM
POOL_WINDOWS = (2, 4, 8, 16)
POOL_GROUP = 128
POOL_WIDTH = 512
SB_WIDTH = 512
FF_SHARD = 352
FF_SHARD_PAD = 384
ATTN_BLOCK = 128

ADAM_LR = 0.001
ADAM_B1 = 0.9
ADAM_B2 = 0.999
ADAM_EPS = 1e-08
ADAM_WD = 0.01
ADAM_STEP = 10

VMEM_LIMIT = 48 << 20


def _params(dims=None):
    return pltpu.CompilerParams(dimension_semantics=dims, vmem_limit_bytes=VMEM_LIMIT)


def _mm(a, b):
    return jnp.dot(a, b, preferred_element_type=F32)


def _mm_nt(a, b):
    return lax.dot_general(a, b, (((1,), (1,)), ((), ())), preferred_element_type=F32)


def _mm_tn(a, b):
    return lax.dot_general(a, b, (((0,), (0,)), ((), ())), preferred_element_type=F32)


def _rstd(xf):
    return lax.rsqrt(jnp.mean(xf * xf, axis=-1, keepdims=True) + RMS_EPS)


def _rms_bwd(xf, gain, dn):
    r = _rstd(xf)
    xh = xf * r
    dgain = jnp.sum(dn * xh, axis=0, keepdims=True)
    dxh = dn * gain
    dx = r * (dxh - xh * jnp.mean(dxh * xh, axis=-1, keepdims=True))
    return dx, dgain


def _ffn_fwd(x, gain, wgu, wd, *, tm, name):
    T, D = x.shape
    tm = min(tm, T)
    nb, bw = wgu.shape[0] // 2, wgu.shape[2]

    def body(x_ref, gain_ref, wg_ref, wu_ref, wd_ref, h_ref, gu_ref, n_scr, acc):
        j = pl.program_id(1)

        @pl.when(j == 0)
        def _():
            xf = x_ref[...]
            n_scr[...] = (xf * _rstd(xf) * gain_ref[...]).astype(BF16)
            acc[...] = jnp.zeros_like(acc)

        n = n_scr[...]
        g = _mm(n, wg_ref[...])
        u = _mm(n, wu_ref[...])
        gu_ref[0] = g.astype(BF16)
        gu_ref[1] = u.astype(BF16)
        hid = (g * jax.nn.sigmoid(g) * u).astype(BF16)
        acc[...] += _mm(hid, wd_ref[...])

        @pl.when(j == nb - 1)
        def _():
            h_ref[...] = x_ref[...] + 0.5 * acc[...]

    return pl.pallas_call(
        body, name=name, grid=(T // tm, nb),
        in_specs=[
            pl.BlockSpec((tm, D), lambda i, j: (i, 0)),
            pl.BlockSpec((1, D), lambda i, j: (0, 0)),
            pl.BlockSpec((None, D, bw), lambda i, j: (j, 0, 0)),
            pl.BlockSpec((None, D, bw), lambda i, j: (j + nb, 0, 0)),
            pl.BlockSpec((bw, D), lambda i, j: (j, 0)),
        ],
        out_specs=[
            pl.BlockSpec((tm, D), lambda i, j: (i, 0)),
            pl.BlockSpec((2, tm, bw), lambda i, j: (0, i, j)),
        ],
        out_shape=[jax.ShapeDtypeStruct((T, D), F32), jax.ShapeDtypeStruct((2, T, nb * bw), BF16)],
        scratch_shapes=[pltpu.VMEM((tm, D), BF16), pltpu.VMEM((tm, D), F32)],
        compiler_params=_params(("arbitrary", "arbitrary")),
    )(x, gain, wgu, wgu, wd)


def _ffn_bwd(dh, x, gain, gu, wgu, wd, *, tm, name):
    T, D = x.shape
    tm = min(tm, T)
    nb, bw = wgu.shape[0] // 2, wgu.shape[2]

    def body(dh_ref, x_ref, gain_ref, gu_ref, wg_ref, wu_ref, wd_ref,
             dx_ref, dgain_ref, n_ref, df_ref, dgu_ref, hid_ref, dn_acc):
        i, j = pl.program_id(0), pl.program_id(1)

        @pl.when(j == 0)
        def _():
            xf = x_ref[...]
            n_ref[...] = (xf * _rstd(xf) * gain_ref[...]).astype(BF16)
            df_ref[...] = (0.5 * dh_ref[...]).astype(BF16)
            dn_acc[...] = jnp.zeros_like(dn_acc)

        @pl.when((i == 0) & (j == 0))
        def _():
            dgain_ref[...] = jnp.zeros_like(dgain_ref)

        dhid = _mm_nt(df_ref[...], wd_ref[...])
        g = gu_ref[0].astype(F32)
        u = gu_ref[1].astype(F32)
        s = jax.nn.sigmoid(g)
        silu = g * s
        hid_ref[...] = (silu * u).astype(BF16)
        dg = (dhid * u * (s * (1.0 + g * (1.0 - s)))).astype(BF16)
        du = (dhid * silu).astype(BF16)
        dgu_ref[0] = dg
        dgu_ref[1] = du
        dn_acc[...] += _mm_nt(dg, wg_ref[...]) + _mm_nt(du, wu_ref[...])

        @pl.when(j == nb - 1)
        def _():
            dx, dgain = _rms_bwd(x_ref[...], gain_ref[...], dn_acc[...])
            dx_ref[...] = dh_ref[...] + dx
            dgain_ref[...] += dgain

    row = lambda i, j: (i, 0)
    return pl.pallas_call(
        body, name=name, grid=(T // tm, nb),
        in_specs=[
            pl.BlockSpec((tm, D), row),
            pl.BlockSpec((tm, D), row),
            pl.BlockSpec((1, D), lambda i, j: (0, 0)),
            pl.BlockSpec((2, tm, bw), lambda i, j: (0, i, j)),
            pl.BlockSpec((None, D, bw), lambda i, j: (j, 0, 0)),
            pl.BlockSpec((None, D, bw), lambda i, j: (j + nb, 0, 0)),
            pl.BlockSpec((bw, D), lambda i, j: (j, 0)),
        ],
        out_specs=[
            pl.BlockSpec((tm, D), row),
            pl.BlockSpec((1, D), lambda i, j: (0, 0)),
            pl.BlockSpec((tm, D), row),
            pl.BlockSpec((tm, D), row),
            pl.BlockSpec((2, tm, bw), lambda i, j: (0, i, j)),
            pl.BlockSpec((tm, bw), lambda i, j: (i, j)),
        ],
        out_shape=[
            jax.ShapeDtypeStruct((T, D), F32),
            jax.ShapeDtypeStruct((1, D), F32),
            jax.ShapeDtypeStruct((T, D), BF16),
            jax.ShapeDtypeStruct((T, D), BF16),
            jax.ShapeDtypeStruct((2, T, nb * bw), BF16),
            jax.ShapeDtypeStruct((T, nb * bw), BF16),
        ],
        scratch_shapes=[pltpu.VMEM((tm, D), F32)],
        compiler_params=_params(("arbitrary", "arbitrary")),
    )(dh, x, gain, gu, wgu, wgu, wd)


def _wgrad(a, b, *, grid, a_spec, b_spec, out_spec, out_shape, acc_shape, name, split_lanes=0):
    nk = grid[2]

    def body(a_ref, b_ref, o_ref, acc):
        k = pl.program_id(2)

        @pl.when(k == 0)
        def _():
            acc[...] = jnp.zeros_like(acc)

        acc[...] += _mm_tn(a_ref[...].astype(BF16), b_ref[...].astype(BF16))

        @pl.when(k == nk - 1)
        def _():
            if split_lanes:
                for e in range(o_ref.shape[0]):
                    o_ref[e] = acc[:, e * split_lanes:(e + 1) * split_lanes].astype(o_ref.dtype)
            else:
                o_ref[...] = acc[...].astype(o_ref.dtype)

    return pl.pallas_call(
        body, name=name, grid=grid, in_specs=[a_spec, b_spec], out_specs=out_spec,
        out_shape=jax.ShapeDtypeStruct(out_shape, BF16),
        scratch_shapes=[pltpu.VMEM(acc_shape, F32)],
        compiler_params=_params(("arbitrary", "arbitrary", "arbitrary")),
    )(a, b)


def _wgrad_gate_up(n, dgu, *, tk, name):
    T, D = n.shape
    tk = min(tk, T)
    bw = FF_SHARD_PAD * 2
    nb = dgu.shape[2] // bw
    return _wgrad(
        n, dgu, grid=(1, 2 * nb, T // tk), name=name,
        a_spec=pl.BlockSpec((tk, D), lambda m, c, k: (k, 0)),
        b_spec=pl.BlockSpec((None, tk, bw), lambda m, c, k: (c // nb, k, c % nb)),
        out_spec=pl.BlockSpec((None, D, bw), lambda m, c, k: (c, 0, 0)),
        out_shape=(2 * nb, D, bw), acc_shape=(D, bw))


def _wgrad_down(hid, df, *, tk, name):
    T, D = df.shape
    tk = min(tk, T)
    bw = FF_SHARD_PAD * 2
    nb = hid.shape[1] // bw
    return _wgrad(
        hid, df, grid=(nb, 1, T // tk), name=name,
        a_spec=pl.BlockSpec((tk, bw), lambda m, c, k: (k, m)),
        b_spec=pl.BlockSpec((tk, D), lambda m, c, k: (k, 0)),
        out_spec=pl.BlockSpec((bw, D), lambda m, c, k: (m, 0)),
        out_shape=(nb * bw, D), acc_shape=(bw, D))


def _wgrad_in(un, dproj, *, tk, name):
    T, D = un.shape
    tk = min(tk, T)
    bw = dproj.shape[1] // N_DEV
    return _wgrad(
        un, dproj, grid=(1, N_DEV, T // tk), name=name,
        a_spec=pl.BlockSpec((tk, D), lambda m, c, k: (k, 0)),
        b_spec=pl.BlockSpec((tk, bw), lambda m, c, k: (k, c)),
        out_spec=pl.BlockSpec((None, D, bw), lambda m, c, k: (c, 0, 0)),
        out_shape=(N_DEV, D, bw), acc_shape=(D, bw))


def _wgrad_full(a, b, *, tk, name, split_lanes=0):
    T, M = a.shape
    tk = min(tk, T)
    N = b.shape[1]
    if split_lanes:
        out_shape = (N // split_lanes, M, split_lanes)
        out_spec = pl.BlockSpec(out_shape, lambda m, c, k: (0, 0, 0))
    else:
        out_shape = (M, N)
        out_spec = pl.BlockSpec(out_shape, lambda m, c, k: (0, 0))
    return _wgrad(
        a, b, grid=(1, 1, T // tk), name=name,
        a_spec=pl.BlockSpec((tk, M), lambda m, c, k: (k, 0)),
        b_spec=pl.BlockSpec((tk, N), lambda m, c, k: (k, 0)),
        out_spec=out_spec, out_shape=out_shape, acc_shape=(M, N), split_lanes=split_lanes)


def _loss_bwd(h, target, gain, *, tm, name):
    T, D = h.shape
    tm = min(tm, T)

    def body(h_ref, t_ref, gain_ref, dh_ref, loss_ref, dgain_ref):
        @pl.when(pl.program_id(0) == 0)
        def _():
            loss_ref[...] = jnp.zeros_like(loss_ref)
            dgain_ref[...] = jnp.zeros_like(dgain_ref)

        xf = h_ref[...]
        gain = gain_ref[...]
        err = xf * _rstd(xf) * gain - t_ref[...]
        loss_ref[...] += 0.5 * jnp.sum(jnp.mean(err * err, axis=-1, keepdims=True), axis=0, keepdims=True)
        dx, dgain = _rms_bwd(xf, gain, err * (1.0 / D))
        dh_ref[...] = dx
        dgain_ref[...] += dgain

    row = lambda i: (i, 0)
    fixed = lambda i: (0, 0)
    return pl.pallas_call(
        body, name=name, grid=(T // tm,),
        in_specs=[pl.BlockSpec((tm, D), row), pl.BlockSpec((tm, D), row), pl.BlockSpec((1, D), fixed)],
        out_specs=[pl.BlockSpec((tm, D), row), pl.BlockSpec((1, 128), fixed), pl.BlockSpec((1, D), fixed)],
        out_shape=[jax.ShapeDtypeStruct((T, D), F32), jax.ShapeDtypeStruct((1, 128), F32),
                   jax.ShapeDtypeStruct((1, D), F32)],
        compiler_params=_params(("arbitrary",)),
    )(h, target, gain)


def _inproj_fwd(h, gain, w_in, *, tm, name):
    T, D = h.shape
    tm = min(tm, T)
    nb, bw = w_in.shape[0], w_in.shape[2]

    def body(h_ref, gain_ref, w_ref, un_ref, proj_ref):
        @pl.when(pl.program_id(1) == 0)
        def _():
            xf = h_ref[...]
            un_ref[...] = (xf * _rstd(xf) * gain_ref[...]).astype(BF16)

        proj_ref[...] = _mm(un_ref[...], w_ref[...])

    return pl.pallas_call(
        body, name=name, grid=(T // tm, nb),
        in_specs=[
            pl.BlockSpec((tm, D), lambda i, j: (i, 0)),
            pl.BlockSpec((1, D), lambda i, j: (0, 0)),
            pl.BlockSpec((None, D, bw), lambda i, j: (j, 0, 0)),
        ],
        out_specs=[pl.BlockSpec((tm, D), lambda i, j: (i, 0)), pl.BlockSpec((tm, bw), lambda i, j: (i, j))],
        out_shape=[jax.ShapeDtypeStruct((T, D), BF16), jax.ShapeDtypeStruct((T, nb * bw), F32)],
        compiler_params=_params(("arbitrary", "arbitrary")),
    )(h, gain, w_in)


def _inproj_bwd(dproj, dh, h, gain, w_in, *, tm, name):
    T, D = h.shape
    tm = min(tm, T)
    nb, bw = w_in.shape[0], w_in.shape[2]

    def body(dp_ref, dh_ref, h_ref, gain_ref, w_ref, dx_ref, dgain_ref, acc):
        i, j = pl.program_id(0), pl.program_id(1)

        @pl.when(j == 0)
        def _():
            acc[...] = jnp.zeros_like(acc)

        @pl.when((i == 0) & (j == 0))
        def _():
            dgain_ref[...] = jnp.zeros_like(dgain_ref)

        acc[...] += _mm_nt(dp_ref[...], w_ref[...])

        @pl.when(j == nb - 1)
        def _():
            dx, dgain = _rms_bwd(h_ref[...], gain_ref[...], acc[...])
            dx_ref[...] = dh_ref[...] + dx
            dgain_ref[...] += dgain

    row = lambda i, j: (i, 0)
    return pl.pallas_call(
        body, name=name, grid=(T // tm, nb),
        in_specs=[
            pl.BlockSpec((tm, bw), lambda i, j: (i, j)),
            pl.BlockSpec((tm, D), row),
            pl.BlockSpec((tm, D), row),
            pl.BlockSpec((1, D), lambda i, j: (0, 0)),
            pl.BlockSpec((None, D, bw), lambda i, j: (j, 0, 0)),
        ],
        out_specs=[pl.BlockSpec((tm, D), row), pl.BlockSpec((1, D), lambda i, j: (0, 0))],
        out_shape=[jax.ShapeDtypeStruct((T, D), F32), jax.ShapeDtypeStruct((1, D), F32)],
        scratch_shapes=[pltpu.VMEM((tm, D), F32)],
        compiler_params=_params(("arbitrary", "arbitrary")),
    )(dproj, dh, h, gain, w_in)


def _window_sum(x, row, doublings, *, backward):
    T = x.shape[0]
    s = x
    for k in range(doublings):
        sh = 1 << k
        if backward:
            s = s + jnp.where(row < T - sh, pltpu.roll(s, T - sh, 0), 0.0)
        else:
            s = s + jnp.where(row >= sh, pltpu.roll(s, sh, 0), 0.0)
    return s


def _pool_fwd(proj, w_group, scale, *, name):
    T = proj.shape[0]

    def body(xp_ref, w_ref, scale_ref, p_ref):
        row = lax.broadcasted_iota(jnp.int32, (T, POOL_GROUP), 0)
        for gi, window in enumerate(POOL_WINDOWS):
            cols = slice(gi * POOL_GROUP, (gi + 1) * POOL_GROUP)
            x = xp_ref[:, cols]
            inv_count = 1.0 / jnp.minimum(row + 1, window).astype(F32)
            yc = _window_sum(x, row, gi + 1, backward=False) * inv_count - x
            pre = _mm(yc.astype(BF16), w_ref[gi].astype(BF16))
            p_ref[:, cols] = pre * scale_ref[:, cols]

    return pl.pallas_call(
        body, name=name, grid=(1,),
        in_specs=[
            pl.BlockSpec((T, POOL_WIDTH), lambda i: (0, 0)),
            pl.BlockSpec(w_group.shape, lambda i: (0, 0, 0)),
            pl.BlockSpec((1, POOL_WIDTH), lambda i: (0, 0)),
        ],
        out_specs=pl.BlockSpec((T, POOL_WIDTH), lambda i: (0, 0)),
        out_shape=jax.ShapeDtypeStruct((T, POOL_WIDTH), F32),
        compiler_params=_params(("arbitrary",)),
    )(proj, w_group, scale)


def _pool_bwd(dp, proj, w_group, scale, *, name):
    T = proj.shape[0]

    def body(dp_ref, xp_ref, w_ref, scale_ref, dxp_ref, dw_ref, dscale_ref):
        row = lax.broadcasted_iota(jnp.int32, (T, POOL_GROUP), 0)
        for gi, window in enumerate(POOL_WINDOWS):
            cols = slice(gi * POOL_GROUP, (gi + 1) * POOL_GROUP)
            x = xp_ref[:, cols]
            inv_count = 1.0 / jnp.minimum(row + 1, window).astype(F32)
            yc = (_window_sum(x, row, gi + 1, backward=False) * inv_count - x).astype(BF16)
            w = w_ref[gi].astype(BF16)
            pre = _mm(yc, w)
            dpg = dp_ref[:, cols]
            dscale_ref[:, cols] = jnp.sum(dpg * pre, axis=0, keepdims=True)
            dpre = (dpg * scale_ref[:, cols]).astype(BF16)
            dw_ref[gi] = _mm_tn(yc, dpre)
            dyc = _mm_nt(dpre, w)
            dxp_ref[:, cols] = _window_sum(dyc * inv_count, row, gi + 1, backward=True) - dyc

    return pl.pallas_call(
        body, name=name, grid=(1,),
        in_specs=[
            pl.BlockSpec((T, POOL_WIDTH), lambda i: (0, 0)),
            pl.BlockSpec((T, POOL_WIDTH), lambda i: (0, 0)),
            pl.BlockSpec(w_group.shape, lambda i: (0, 0, 0)),
            pl.BlockSpec((1, POOL_WIDTH), lambda i: (0, 0)),
        ],
        out_specs=[
            pl.BlockSpec((T, POOL_WIDTH), lambda i: (0, 0)),
            pl.BlockSpec(w_group.shape, lambda i: (0, 0, 0)),
            pl.BlockSpec((1, POOL_WIDTH), lambda i: (0, 0)),
        ],
        out_shape=[jax.ShapeDtypeStruct((T, POOL_WIDTH), F32), jax.ShapeDtypeStruct(w_group.shape, F32),
                   jax.ShapeDtypeStruct((1, POOL_WIDTH), F32)],
        compiler_params=_params(("arbitrary",)),
    )(dp, proj, w_group, scale)


def _log_sigmoids(z):
    t = jnp.log(1.0 + jnp.exp(-jnp.abs(z)))
    return jnp.minimum(z, 0.0) - t, -jnp.maximum(z, 0.0) - t


def _tri_sum(x, tri):
    hi = x.astype(BF16)
    lo = (x - hi.astype(F32)).astype(BF16)
    return _mm(hi, tri) + _mm(lo, tri)


def _attn_specs(T, tq):
    q_col = POOL_WIDTH // HEAD_PAIR
    k_col = q_col + SB_WIDTH // HEAD_PAIR
    v_col = k_col + SB_WIDTH // HEAD_PAIR
    return [
        pl.BlockSpec((tq, HEAD_PAIR), lambda p, i: (i, q_col + p)),
        pl.BlockSpec((T, HEAD_PAIR), lambda p, i: (0, k_col + p)),
        pl.BlockSpec((T, HEAD_PAIR), lambda p, i: (0, v_col + p)),
    ]


def _attn_fwd(proj, *, name):
    T = proj.shape[0]
    tq = ATTN_BLOCK
    scale = 1.0 / (HEAD_DIM ** 0.5)

    def body(q_ref, k_ref, v_ref, o_ref, lt_ref, kb_scr, vb_scr):
        qi = pl.program_id(1)

        @pl.when(qi == 0)
        def _():
            kb_scr[...] = k_ref[...].astype(BF16)
            vb_scr[...] = v_ref[...].astype(BF16)

        head0 = lax.broadcasted_iota(jnp.int32, (tq, HEAD_PAIR), 1) < HEAD_DIM
        q = q_ref[...]
        qs = (jnp.where(head0, q, 0.0).astype(BF16), jnp.where(head0, 0.0, q).astype(BF16))
        r = lax.broadcasted_iota(jnp.int32, (tq, tq), 0)
        c = lax.broadcasted_iota(jnp.int32, (tq, tq), 1)
        later = (r > c).astype(BF16)
        causal = c < r

        def block(kj, carry, valid):
            off = pl.multiple_of(kj * tq, tq)
            kb = kb_scr[pl.ds(off, tq), :]
            vb = vb_scr[pl.ds(off, tq), :]
            out = []
            for h in range(2):
                run, o = carry[2 * h], carry[2 * h + 1]
                z = _mm_nt(qs[h], kb) * scale
                lb, lm = _log_sigmoids(z)
                if valid is not None:
                    lm = jnp.where(valid, lm, 0.0)
                a = jnp.exp(lb + run + _tri_sum(lm, later))
                if valid is not None:
                    a = jnp.where(valid, a, 0.0)
                out += [run + jnp.sum(lm, axis=1, keepdims=True), o + _mm(a.astype(BF16), vb)]
            return tuple(out)

        zero = (jnp.zeros((tq, 1), F32), jnp.zeros((tq, HEAD_PAIR), F32))
        carry = block(qi, zero + zero, causal)
        carry = lax.fori_loop(0, qi, lambda it, cr: block(qi - 1 - it, cr, None), carry)
        o_ref[...] = jnp.where(head0, carry[1], carry[3])
        lt_ref[...] = jnp.where(head0, carry[0], carry[2])

    out_spec = pl.BlockSpec((tq, HEAD_PAIR), lambda p, i: (i, p))
    return pl.pallas_call(
        body, name=name, grid=(N_HEADS // 2, T // tq),
        in_specs=_attn_specs(T, tq), out_specs=[out_spec, out_spec],
        out_shape=[jax.ShapeDtypeStruct((T, SB_WIDTH), F32), jax.ShapeDtypeStruct((T, SB_WIDTH), F32)],
        scratch_shapes=[pltpu.VMEM((T, HEAD_PAIR), BF16), pltpu.VMEM((T, HEAD_PAIR), BF16)],
        compiler_params=_params(("arbitrary", "arbitrary")),
    )(proj, proj, proj)


def _attn_bwd(proj, do, ltot, *, name):
    T = proj.shape[0]
    tq = ATTN_BLOCK
    scale = 1.0 / (HEAD_DIM ** 0.5)

    def body(q_ref, k_ref, v_ref, do_ref, lt_ref, dq_ref, dk_ref, dv_ref, kb_scr, vb_scr):
        qi = pl.program_id(1)

        @pl.when(qi == 0)
        def _():
            kb_scr[...] = k_ref[...].astype(BF16)
            vb_scr[...] = v_ref[...].astype(BF16)
            dk_ref[...] = jnp.zeros_like(dk_ref)
            dv_ref[...] = jnp.zeros_like(dv_ref)

        head0 = lax.broadcasted_iota(jnp.int32, (tq, HEAD_PAIR), 1) < HEAD_DIM
        q, do_, lt = q_ref[...], do_ref[...], lt_ref[...]
        qs = (jnp.where(head0, q, 0.0).astype(BF16), jnp.where(head0, 0.0, q).astype(BF16))
        dos = (jnp.where(head0, do_, 0.0).astype(BF16), jnp.where(head0, 0.0, do_).astype(BF16))
        lts = (jnp.max(jnp.where(head0, lt, -jnp.inf), axis=1, keepdims=True),
               jnp.max(jnp.where(head0, -jnp.inf, lt), axis=1, keepdims=True))
        r = lax.broadcasted_iota(jnp.int32, (tq, tq), 0)
        c = lax.broadcasted_iota(jnp.int32, (tq, tq), 1)
        upto = (r <= c).astype(BF16)
        before = (r < c).astype(BF16)
        causal = c < r

        def block(kj, carry, valid):
            off = pl.multiple_of(kj * tq, tq)
            kb = kb_scr[pl.ds(off, tq), :]
            vb = vb_scr[pl.ds(off, tq), :]
            dk_blk = jnp.zeros((tq, HEAD_PAIR), F32)
            dv_blk = jnp.zeros((tq, HEAD_PAIR), F32)
            out = []
            for h in range(2):
                run_lm, run_e, dq = carry[3 * h:3 * h + 3]
                z = _mm_nt(qs[h], kb) * scale
                lb, lm = _log_sigmoids(z)
                if valid is not None:
                    lm = jnp.where(valid, lm, 0.0)
                a = jnp.exp(lb + (lts[h] - run_lm - _tri_sum(lm, upto)))
                if valid is not None:
                    a = jnp.where(valid, a, 0.0)
                e = _mm_nt(dos[h], vb) * a
                beta = jnp.exp(lb)
                dz = e * (1.0 - beta) - (run_e + _tri_sum(e, before)) * beta
                if valid is not None:
                    dz = jnp.where(valid, dz, 0.0)
                dz = (dz * scale).astype(BF16)
                dk_blk += _mm_tn(dz, qs[h])
                dv_blk += _mm_tn(a.astype(BF16), dos[h])
                out += [run_lm + jnp.sum(lm, axis=1, keepdims=True), run_e + jnp.sum(e, axis=1, keepdims=True),
                        dq + _mm(dz, kb)]
            dk_ref[pl.ds(off, tq), :] += dk_blk
            dv_ref[pl.ds(off, tq), :] += dv_blk
            return tuple(out)

        zero = (jnp.zeros((tq, 1), F32), jnp.zeros((tq, 1), F32), jnp.zeros((tq, HEAD_PAIR), F32))
        carry = lax.fori_loop(0, qi, lambda kj, cr: block(kj, cr, None), zero + zero)
        carry = block(qi, carry, causal)
        dq_ref[...] = jnp.where(head0, carry[2], carry[5])

    blk = pl.BlockSpec((tq, HEAD_PAIR), lambda p, i: (i, p))
    seq = pl.BlockSpec((T, HEAD_PAIR), lambda p, i: (0, p))
    return pl.pallas_call(
        body, name=name, grid=(N_HEADS // 2, T // tq),
        in_specs=_attn_specs(T, tq) + [blk, blk], out_specs=[blk, seq, seq],
        out_shape=[jax.ShapeDtypeStruct((T, SB_WIDTH), F32)] * 3,
        scratch_shapes=[pltpu.VMEM((T, HEAD_PAIR), BF16), pltpu.VMEM((T, HEAD_PAIR), BF16)],
        compiler_params=_params(("arbitrary", "arbitrary")),
    )(proj, proj, proj, do, ltot)


def _branch(act_bf16, w_ref):
    return jnp.concatenate([_mm(act_bf16, w_ref[e]) for e in range(w_ref.shape[0])], axis=1)


def _mix_specs(T, D, tm, wbp, w_out):
    gate_col = (POOL_WIDTH + 3 * SB_WIDTH) // D
    row = lambda i: (i, 0)
    return [
        pl.BlockSpec((tm, D), row),
        pl.BlockSpec((tm, POOL_WIDTH), row),
        pl.BlockSpec((tm, SB_WIDTH), row),
        pl.BlockSpec((tm, D), lambda i: (i, gate_col)),
        pl.BlockSpec((tm, D), lambda i: (i, gate_col + 1)),
        pl.BlockSpec(wbp.shape, lambda i: (0, 0, 0)),
        pl.BlockSpec(wbp.shape, lambda i: (0, 0, 0)),
        pl.BlockSpec(w_out.shape, lambda i: (0, 0)),
    ]


def _mix_fwd(h, p, o, proj, wbp, wba, w_out, *, tm, name):
    T, D = h.shape
    tm = min(tm, T)

    def body(h_ref, p_ref, o_ref, glp_ref, gls_ref, wbp_ref, wba_ref, wout_ref, hout_ref, m_ref):
        yp = _branch(p_ref[...].astype(BF16), wbp_ref)
        ys = _branch(o_ref[...].astype(BF16), wba_ref)
        m = (jax.nn.sigmoid(glp_ref[...]) * yp + jax.nn.sigmoid(gls_ref[...]) * ys).astype(BF16)
        m_ref[...] = m
        hout_ref[...] = h_ref[...] + _mm(m, wout_ref[...])

    row = lambda i: (i, 0)
    return pl.pallas_call(
        body, name=name, grid=(T // tm,),
        in_specs=_mix_specs(T, D, tm, wbp, w_out),
        out_specs=[pl.BlockSpec((tm, D), row), pl.BlockSpec((tm, D), row)],
        out_shape=[jax.ShapeDtypeStruct((T, D), F32), jax.ShapeDtypeStruct((T, D), BF16)],
        compiler_params=_params(("arbitrary",)),
    )(h, p, o, proj, proj, wbp, wba, w_out)


def _mix_bwd(dh, p, o, proj, wbp, wba, w_out, *, tm, name):
    T, D = dh.shape
    tm = min(tm, T)
    bw = wbp.shape[2]

    def body(dh_ref, p_ref, o_ref, glp_ref, gls_ref, wbp_ref, wba_ref, wout_ref,
             dyp_ref, dys_ref, dp_ref, do_ref, dgl_ref):
        dm = _mm_nt(dh_ref[...].astype(BF16), wout_ref[...])
        yp = _branch(p_ref[...].astype(BF16), wbp_ref)
        ys = _branch(o_ref[...].astype(BF16), wba_ref)
        gp = jax.nn.sigmoid(glp_ref[...])
        gs = jax.nn.sigmoid(gls_ref[...])
        dyp = (dm * gp).astype(BF16)
        dys = (dm * gs).astype(BF16)
        dyp_ref[...] = dyp
        dys_ref[...] = dys
        dgl_ref[:, :D] = (dm * yp * gp * (1.0 - gp)).astype(BF16)
        dgl_ref[:, D:] = (dm * ys * gs * (1.0 - gs)).astype(BF16)
        dp = jnp.zeros(dp_ref.shape, F32)
        do_ = jnp.zeros(do_ref.shape, F32)
        for e in range(wbp_ref.shape[0]):
            dp += _mm_nt(dyp[:, e * bw:(e + 1) * bw], wbp_ref[e])
            do_ += _mm_nt(dys[:, e * bw:(e + 1) * bw], wba_ref[e])
        dp_ref[...] = dp
        do_ref[...] = do_

    row = lambda i: (i, 0)
    return pl.pallas_call(
        body, name=name, grid=(T // tm,),
        in_specs=_mix_specs(T, D, tm, wbp, w_out),
        out_specs=[pl.BlockSpec((tm, D), row), pl.BlockSpec((tm, D), row), pl.BlockSpec((tm, POOL_WIDTH), row),
                   pl.BlockSpec((tm, SB_WIDTH), row), pl.BlockSpec((tm, 2 * D), row)],
        out_shape=[jax.ShapeDtypeStruct((T, D), BF16), jax.ShapeDtypeStruct((T, D), BF16),
                   jax.ShapeDtypeStruct((T, POOL_WIDTH), F32), jax.ShapeDtypeStruct((T, SB_WIDTH), F32),
                   jax.ShapeDtypeStruct((T, 2 * D), BF16)],
        compiler_params=_params(("arbitrary",)),
    )(dh, p, o, proj, proj, wbp, wba, w_out)


def _adamw(w, g, m, v, *, name):
    R, C = w.shape
    tr = R if R * C <= 512 * 1024 else 256

    def body(w_ref, g_ref, m_ref, v_ref, d_ref, nm_ref, nv_ref):
        g_ = g_ref[...]
        m_ = ADAM_B1 * m_ref[...] + (1.0 - ADAM_B1) * g_
        v_ = ADAM_B2 * v_ref[...] + (1.0 - ADAM_B2) * (g_ * g_)
        m_hat = m_ / (1.0 - ADAM_B1 ** ADAM_STEP)
        v_hat = v_ / (1.0 - ADAM_B2 ** ADAM_STEP)
        d_ref[...] = -ADAM_LR * (m_hat / (jnp.sqrt(v_hat) + ADAM_EPS) + ADAM_WD * w_ref[...])
        nm_ref[...] = m_
        nv_ref[...] = v_

    spec = pl.BlockSpec((tr, C), lambda i: (i, 0))
    return pl.pallas_call(
        body, name=name, grid=(R // tr,), in_specs=[spec] * 4, out_specs=[spec] * 3,
        out_shape=[jax.ShapeDtypeStruct((R, C), F32)] * 3,
        compiler_params=_params(("arbitrary",)),
    )(w, g, m, v)


def _position():
    return lax.axis_index("x"), lax.axis_index("y"), lax.axis_index("c")


def _all_gather(shards, *, name):
    n = len(shards)

    def body(*refs):
        ins, outs = refs[:n], refs[n:2 * n]
        send_sems, recv_sems, local_sems = refs[2 * n:]
        x, y, c = _position()
        me, sibling = (x, y, c), (x, y, 1 - c)
        chips = [(1 - x, y), (x, 1 - y), (1 - x, 1 - y)]

        def block(a, pos):
            return outs[a].at[4 * pos[0] + 2 * pos[1] + pos[2]]

        def copy(a, k, pos, to, src=None):
            return pltpu.make_async_remote_copy(
                src_ref=block(a, pos) if src is None else src, dst_ref=block(a, pos),
                send_sem=send_sems.at[7 * a + k], recv_sem=recv_sems.at[7 * a + k],
                device_id=to, device_id_type=MESH)

        started = []
        for a in range(n):
            mine = pltpu.make_async_copy(ins[a], block(a, me), local_sems.at[a])
            mine.start()
            started.append(mine)
        sends = []
        for a in range(n):
            sends += [copy(a, 1 + j, me, (*chip, c), src=ins[a]) for j, chip in enumerate(chips)]
            sends.append(copy(a, 0, me, sibling, src=ins[a]))
        for cp in sends:
            cp.start()
        for j, chip in enumerate(chips):
            for a in range(n):
                copy(a, 1 + j, (*chip, c), me).wait_recv()
                passed = copy(a, 4 + j, (*chip, c), sibling)
                passed.start()
                sends.append(passed)
        for a in range(n):
            copy(a, 0, sibling, me).wait_recv()
            for j, chip in enumerate(chips):
                copy(a, 4 + j, (*chip, 1 - c), me).wait_recv()
        for cp in sends:
            cp.wait_send()
        for cp in started:
            cp.wait()

    any_spec = pl.BlockSpec(memory_space=pl.ANY)
    return pl.pallas_call(
        body, name=name,
        in_specs=[any_spec] * n, out_specs=[any_spec] * n,
        out_shape=[jax.ShapeDtypeStruct((N_DEV,) + s.shape, s.dtype) for s in shards],
        scratch_shapes=[pltpu.SemaphoreType.DMA((7 * n,)), pltpu.SemaphoreType.DMA((7 * n,)),
                        pltpu.SemaphoreType.DMA((n,))],
    )(*shards)


def _reduce_scatter(grads, *, name):
    _, R, C = grads.shape
    rc = 128 if R % 128 == 0 else R

    def body(g_ref, out_ref, mine, theirs, partial, landed, send_sems, recv_sems, local_sems):
        x, y, c = _position()
        my_chip = 2 * x + y

        def swap(s):
            return pltpu.make_async_remote_copy(
                src_ref=g_ref.at[2 * s + (1 - c)], dst_ref=theirs.at[s],
                send_sem=send_sems.at[s], recv_sem=recv_sems.at[s],
                device_id=(x, y, 1 - c), device_id_type=MESH)

        def load(s):
            return pltpu.make_async_copy(g_ref.at[2 * s + c], mine.at[s], local_sems.at[s])

        def cross(j):
            chip = my_chip ^ j
            return pltpu.make_async_remote_copy(
                src_ref=partial.at[j - 1], dst_ref=landed.at[j - 1],
                send_sem=send_sems.at[3 + j], recv_sem=recv_sems.at[3 + j],
                device_id=(chip // 2, chip % 2, c), device_id_type=MESH)

        for s in range(4):
            swap(s).start()
            load(s).start()
        for s in range(4):
            load(s).wait()
            swap(s).wait_recv()

        def chip_sum(chip, rows):
            return mine[chip, rows, :].astype(F32) + theirs[chip, rows, :].astype(F32)

        for j in (1, 2, 3):
            @pl.loop(0, R // rc)
            def _(t):
                rows = pl.ds(pl.multiple_of(t * rc, rc), rc)
                partial[j - 1, rows, :] = chip_sum(my_chip ^ j, rows).astype(BF16)
            cross(j).start()

        @pl.loop(0, R // rc)
        def _(t):
            rows = pl.ds(pl.multiple_of(t * rc, rc), rc)
            out_ref[rows, :] = chip_sum(my_chip, rows)

        for j in (1, 2, 3):
            cross(j).wait_recv()

            @pl.loop(0, R // rc)
            def _(t):
                rows = pl.ds(pl.multiple_of(t * rc, rc), rc)
                out_ref[rows, :] += landed[j - 1, rows, :].astype(F32)

        for s in range(4):
            swap(s).wait_send()
        for j in (1, 2, 3):
            cross(j).wait_send()

    return pl.pallas_call(
        body, name=name,
        in_specs=[pl.BlockSpec(memory_space=pl.ANY)],
        out_specs=pl.BlockSpec(memory_space=pltpu.VMEM),
        out_shape=jax.ShapeDtypeStruct((R, C), F32),
        scratch_shapes=[
            pltpu.VMEM((4, R, C), BF16), pltpu.VMEM((4, R, C), BF16),
            pltpu.VMEM((3, R, C), BF16), pltpu.VMEM((3, R, C), BF16),
            pltpu.SemaphoreType.DMA((7,)), pltpu.SemaphoreType.DMA((7,)), pltpu.SemaphoreType.DMA((4,)),
        ],
        compiler_params=_params(),
    )(grads)


def _all_reduce_small(slab, *, name):
    R, C = slab.shape

    def body(in_ref, out_ref, gathered, send_sems, recv_sems):
        x, y, c = _position()
        me = 4 * x + 2 * y + c

        def copy(k):
            peer = me ^ k
            return pltpu.make_async_remote_copy(
                src_ref=in_ref, dst_ref=gathered.at[me],
                send_sem=send_sems.at[k - 1], recv_sem=recv_sems.at[k - 1],
                device_id=(peer // 4, (peer // 2) % 2, peer % 2), device_id_type=MESH)

        def arrival(k):
            return pltpu.make_async_remote_copy(
                src_ref=in_ref, dst_ref=gathered.at[me ^ k],
                send_sem=send_sems.at[k - 1], recv_sem=recv_sems.at[k - 1],
                device_id=(x, y, c), device_id_type=MESH)

        for k in range(1, N_DEV):
            copy(k).start()
        gathered[me] = in_ref[...]
        for k in range(1, N_DEV):
            arrival(k).wait_recv()
        total = gathered[0]
        for d in range(1, N_DEV):
            total = total + gathered[d]
        out_ref[...] = total
        for k in range(1, N_DEV):
            copy(k).wait_send()

    return pl.pallas_call(
        body, name=name,
        in_specs=[pl.BlockSpec(memory_space=pltpu.VMEM)],
        out_specs=pl.BlockSpec(memory_space=pltpu.VMEM),
        out_shape=jax.ShapeDtypeStruct((R, C), F32),
        scratch_shapes=[pltpu.VMEM((N_DEV, R, C), F32),
                        pltpu.SemaphoreType.DMA((N_DEV - 1,)), pltpu.SemaphoreType.DMA((N_DEV - 1,))],
        compiler_params=_params(),
    )(slab)


def _local_step(x, target, norms, pool_w_group, pool_scale, wgu1, wd1, w_in, wbp, wba, w_out, wgu2, wd2):
    n1g, nmg, n2g, nfg = norms
    h1, gu1 = _ffn_fwd(x, n1g, wgu1, wd1, tm=512, name="ffn1_fwd")
    un, proj = _inproj_fwd(h1, nmg, w_in, tm=512, name="inproj_fwd")
    p = _pool_fwd(proj, pool_w_group, pool_scale, name="pool_fwd")
    o, ltot = _attn_fwd(proj, name="attn_fwd")
    h2, m = _mix_fwd(h1, p, o, proj, wbp, wba, w_out, tm=256, name="mix_fwd")
    h3, gu2 = _ffn_fwd(h2, n2g, wgu2, wd2, tm=512, name="ffn2_fwd")
    dh3, loss, d_nf = _loss_bwd(h3, target, nfg, tm=256, name="loss_bwd")

    dh2, d_n2, n2, df2, dgu2, hid2 = _ffn_bwd(dh3, h2, n2g, gu2, wgu2, wd2, tm=256, name="ffn2_bwd")
    d_wgu2 = _wgrad_gate_up(n2, dgu2, tk=512, name="ffn2_wgrad_gate_up")
    d_wd2 = _wgrad_down(hid2, df2, tk=512, name="ffn2_wgrad_down")

    dyp, dys, dp, do, dgl = _mix_bwd(dh2, p, o, proj, wbp, wba, w_out, tm=256, name="mix_bwd")
    d_wout = _wgrad_full(m, dh2, tk=512, name="wgrad_out")
    d_wbp = _wgrad_full(p, dyp, tk=512, name="wgrad_branch_pool", split_lanes=wbp.shape[2])
    d_wba = _wgrad_full(o, dys, tk=512, name="wgrad_branch_attn", split_lanes=wba.shape[2])
    dxp, d_wgroup, d_scale = _pool_bwd(dp, proj, pool_w_group, pool_scale, name="pool_bwd")
    dq, dk, dv = _attn_bwd(proj, do, ltot, name="attn_bwd")
    dproj = jnp.concatenate([dxp.astype(BF16), dq.astype(BF16), dk.astype(BF16), dv.astype(BF16), dgl], axis=1)
    dh1, d_nm = _inproj_bwd(dproj, dh2, h1, nmg, w_in, tm=256, name="inproj_bwd")
    d_win = _wgrad_in(un, dproj, tk=512, name="wgrad_in")

    dx, d_n1, n1, df1, dgu1, hid1 = _ffn_bwd(dh1, x, n1g, gu1, wgu1, wd1, tm=256, name="ffn1_bwd")
    d_wgu1 = _wgrad_gate_up(n1, dgu1, tk=512, name="ffn1_wgrad_gate_up")
    d_wd1 = _wgrad_down(hid1, df1, tk=512, name="ffn1_wgrad_down")

    sharded = (d_wgu1, d_wd1, d_win, d_wbp, d_wba, d_wout, d_wgu2, d_wd2)
    replicated = (d_n1, d_nm, d_n2, d_nf, d_scale, d_wgroup)
    return loss, dx, sharded, replicated


def _pad_gate_up(w):
    d = w.shape[0]
    w = w.astype(BF16).reshape(d, 2, FF_SHARD)
    return jnp.pad(w, ((0, 0), (0, 0), (0, FF_SHARD_PAD - FF_SHARD))).reshape(d, 2 * FF_SHARD_PAD)


def _unpad_gate_up(g):
    d = g.shape[0]
    return g.reshape(d, 2, FF_SHARD_PAD)[:, :, :FF_SHARD].reshape(d, 2 * FF_SHARD)


def _pad_down(w):
    return jnp.pad(w.astype(BF16), ((0, FF_SHARD_PAD - FF_SHARD), (0, 0)))


def kernel(x, ffn1_norm, ffn1_w_gate_up, ffn1_w_down, mix_norm, w_in, pool_w_group, pool_scale, w_branch_pool, w_branch_attn, w_out, ffn2_norm, ffn2_w_gate_up, ffn2_w_down, final_norm, loss_target, m_ffn1_norm, m_ffn1_w_gate_up, m_ffn1_w_down, m_mix_norm, m_w_in, m_pool_w_group, m_pool_scale, m_w_branch_pool, m_w_branch_attn, m_w_out, m_ffn2_norm, m_ffn2_w_gate_up, m_ffn2_w_down, m_final_norm, v_ffn1_norm, v_ffn1_w_gate_up, v_ffn1_w_down, v_mix_norm, v_w_in, v_pool_w_group, v_pool_scale, v_w_branch_pool, v_w_branch_attn, v_w_out, v_ffn2_norm, v_ffn2_w_gate_up, v_ffn2_w_down, v_final_norm):
    D = x.shape[-1]
    weights = dict(ffn1_norm=ffn1_norm, ffn1_w_gate_up=ffn1_w_gate_up, ffn1_w_down=ffn1_w_down, mix_norm=mix_norm,
                   w_in=w_in, pool_w_group=pool_w_group, pool_scale=pool_scale, w_branch_pool=w_branch_pool,
                   w_branch_attn=w_branch_attn, w_out=w_out, ffn2_norm=ffn2_norm, ffn2_w_gate_up=ffn2_w_gate_up,
                   ffn2_w_down=ffn2_w_down, final_norm=final_norm)
    first = dict(ffn1_norm=m_ffn1_norm, ffn1_w_gate_up=m_ffn1_w_gate_up, ffn1_w_down=m_ffn1_w_down,
                 mix_norm=m_mix_norm, w_in=m_w_in, pool_w_group=m_pool_w_group, pool_scale=m_pool_scale,
                 w_branch_pool=m_w_branch_pool, w_branch_attn=m_w_branch_attn, w_out=m_w_out,
                 ffn2_norm=m_ffn2_norm, ffn2_w_gate_up=m_ffn2_w_gate_up, ffn2_w_down=m_ffn2_w_down,
                 final_norm=m_final_norm)
    second = dict(ffn1_norm=v_ffn1_norm, ffn1_w_gate_up=v_ffn1_w_gate_up, ffn1_w_down=v_ffn1_w_down,
                  mix_norm=v_mix_norm, w_in=v_w_in, pool_w_group=v_pool_w_group, pool_scale=v_pool_scale,
                  w_branch_pool=v_w_branch_pool, w_branch_attn=v_w_branch_attn, w_out=v_w_out,
                  ffn2_norm=v_ffn2_norm, ffn2_w_gate_up=v_ffn2_w_gate_up, ffn2_w_down=v_ffn2_w_down,
                  final_norm=v_final_norm)
    order = list(weights)

    shards = [
        _pad_gate_up(ffn1_w_gate_up[0]), _pad_down(ffn1_w_down[0]), w_in[0].astype(BF16),
        w_branch_pool[0].astype(BF16), w_branch_attn[0].astype(BF16), w_out[0].astype(BF16),
        _pad_gate_up(ffn2_w_gate_up[0]), _pad_down(ffn2_w_down[0]),
    ]
    wgu1, wd1, win_g, wbp_g, wba_g, wout_g, wgu2, wd2 = _all_gather(shards, name="all_gather_weights")
    wd1 = wd1.reshape(N_DEV * FF_SHARD_PAD, D)
    wd2 = wd2.reshape(N_DEV * FF_SHARD_PAD, D)
    wout_g = wout_g.reshape(D, D)

    norms = (ffn1_norm, mix_norm, ffn2_norm, final_norm.reshape(1, D))
    loss, dx, sharded, replicated = _local_step(
        x[0], loss_target[0], norms, pool_w_group[0], pool_scale, wgu1, wd1, win_g, wbp_g, wba_g, wout_g, wgu2, wd2)
    d_wgu1, d_wd1, d_win, d_wbp, d_wba, d_wout, d_wgu2, d_wd2 = sharded
    d_wd1 = d_wd1.reshape(N_DEV, FF_SHARD_PAD, D)
    d_wd2 = d_wd2.reshape(N_DEV, FF_SHARD_PAD, D)
    d_wout = d_wout.reshape(N_DEV, D // N_DEV, D)

    grads = dict(
        ffn1_w_gate_up=_unpad_gate_up(_reduce_scatter(d_wgu1, name="reduce_scatter_ffn1_gate_up")),
        ffn1_w_down=_reduce_scatter(d_wd1, name="reduce_scatter_ffn1_down")[:FF_SHARD],
        w_in=_reduce_scatter(d_win, name="reduce_scatter_w_in"),
        w_branch_pool=_reduce_scatter(d_wbp, name="reduce_scatter_branch_pool"),
        w_branch_attn=_reduce_scatter(d_wba, name="reduce_scatter_branch_attn"),
        w_out=_reduce_scatter(d_wout, name="reduce_scatter_w_out"),
        ffn2_w_gate_up=_unpad_gate_up(_reduce_scatter(d_wgu2, name="reduce_scatter_ffn2_gate_up")),
        ffn2_w_down=_reduce_scatter(d_wd2, name="reduce_scatter_ffn2_down")[:FF_SHARD],
    )

    d_n1, d_nm, d_n2, d_nf, d_scale, d_wgroup = replicated
    small = ["ffn1_norm", "mix_norm", "ffn2_norm", "final_norm", "pool_scale", "pool_w_group"]
    pieces = [d.reshape(-1, 128) for d in (d_n1, d_nm, d_n2, d_nf, d_scale, d_wgroup)]
    pieces.append(jnp.broadcast_to(loss, (8, 128)))
    rows = [p.shape[0] for p in pieces]
    slab = jnp.concatenate(pieces, axis=0)
    slab = jnp.pad(slab, ((0, -slab.shape[0] % 8), (0, 0)))
    total = _all_reduce_small(slab, name="all_reduce_replicated")
    start = 0
    for name_, n_rows in zip(small, rows):
        grads[name_] = total[start:start + n_rows]
        start += n_rows
    loss_out = total[start, 0]

    def flat(a):
        return a.reshape(-1, 128)

    small_w = jnp.concatenate([flat(weights[k]) for k in small], axis=0)
    small_m = jnp.concatenate([flat(first[k]) for k in small], axis=0)
    small_v = jnp.concatenate([flat(second[k]) for k in small], axis=0)
    small_g = total[:start]
    pad = -start % 8
    padded = [jnp.pad(a, ((0, pad), (0, 0))) for a in (small_w, small_g, small_m, small_v)]
    small_out = _adamw(*padded, name="adamw_replicated")
    delta, new_m, new_v = {}, {}, {}
    start = 0
    for name_, n_rows in zip(small, rows):
        shape = weights[name_].shape
        delta[name_], new_m[name_], new_v[name_] = (a[start:start + n_rows].reshape(shape) for a in small_out)
        grads[name_] = grads[name_].reshape(shape)
        start += n_rows
    for name_ in order:
        if name_ in small:
            continue
        shape = weights[name_].shape
        two_d = shape[1:]
        out = _adamw(weights[name_].reshape(two_d), grads[name_], first[name_].reshape(two_d),
                     second[name_].reshape(two_d), name="adamw_" + name_)
        delta[name_], new_m[name_], new_v[name_] = (a.reshape(shape) for a in out)
        grads[name_] = grads[name_].reshape(shape)

    return (loss_out, dx[None], *[grads[k] for k in order], *[delta[k] for k in order],
            *[new_m[k] for k in order], *[new_v[k] for k in order])
```

```python
import functools

import jax
import jax.numpy as jnp
from jax import lax
from jax.experimental import pallas as pl
from jax.experimental.pallas import tpu as pltpu

F32 = jnp.float32
BF16 = jnp.bfloat16
MESH = pl.DeviceIdType.MESH

RMS_EPS = 1e-6
N_DEV = 8
N_HEADS = 8
HEAD_DIM = 64
HEAD_PAIR = 2 * HEAD_DIM
POOL_WINDOWS = (2, 4, 8, 16)
POOL_GROUP = 128
POOL_WIDTH = 512
SB_WIDTH = 512
FF_SHARD = 352
FF_SHARD_PAD = 384
ATTN_BLOCK = 256
ATTN_SCALE = 0.125

ADAM_LR = 0.001
ADAM_B1 = 0.9
ADAM_B2 = 0.999
ADAM_EPS = 1e-08
ADAM_WD = 0.01
ADAM_STEP = 10

VMEM_LIMIT = 48 << 20


def _params(dims=None):
    return pltpu.CompilerParams(dimension_semantics=dims, vmem_limit_bytes=VMEM_LIMIT)


def _mm(a, b):
    return jnp.dot(a, b, preferred_element_type=F32)


def _mm_nt(a, b):
    return lax.dot_general(a, b, (((1,), (1,)), ((), ())), preferred_element_type=F32)


def _mm_tn(a, b):
    return lax.dot_general(a, b, (((0,), (0,)), ((), ())), preferred_element_type=F32)


def _rstd(xf):
    return lax.rsqrt(jnp.mean(xf * xf, axis=-1, keepdims=True) + RMS_EPS)


def _rms_bwd(xf, gain, dn):
    r = _rstd(xf)
    xh = xf * r
    dgain = jnp.sum(dn * xh, axis=0, keepdims=True)
    dxh = dn * gain
    dx = r * (dxh - xh * jnp.mean(dxh * xh, axis=-1, keepdims=True))
    return dx, dgain


def _ffn_fwd(x, gain, wgu, wd, *, tm, name):
    T, D = x.shape
    tm = min(tm, T)
    nb, bw = wgu.shape[0] // 2, wgu.shape[2]

    def body(x_ref, gain_ref, wg_ref, wu_ref, wd_ref, h_ref, gu_ref, n_scr, acc):
        j = pl.program_id(1)

        @pl.when(j == 0)
        def _():
            xf = x_ref[...]
            n_scr[...] = (xf * _rstd(xf) * gain_ref[...]).astype(BF16)
            acc[...] = jnp.zeros_like(acc)

        n = n_scr[...]
        g = _mm(n, wg_ref[...])
        u = _mm(n, wu_ref[...])
        gu_ref[0] = g.astype(BF16)
        gu_ref[1] = u.astype(BF16)
        hid = (g * jax.nn.sigmoid(g) * u).astype(BF16)
        acc[...] += _mm(hid, wd_ref[...])

        @pl.when(j == nb - 1)
        def _():
            h_ref[...] = x_ref[...] + 0.5 * acc[...]

    return pl.pallas_call(
        body, name=name, grid=(T // tm, nb),
        in_specs=[
            pl.BlockSpec((tm, D), lambda i, j: (i, 0)),
            pl.BlockSpec((1, D), lambda i, j: (0, 0)),
            pl.BlockSpec((None, D, bw), lambda i, j: (j, 0, 0)),
            pl.BlockSpec((None, D, bw), lambda i, j: (j + nb, 0, 0)),
            pl.BlockSpec((bw, D), lambda i, j: (j, 0)),
        ],
        out_specs=[
            pl.BlockSpec((tm, D), lambda i, j: (i, 0)),
            pl.BlockSpec((2, tm, bw), lambda i, j: (0, i, j)),
        ],
        out_shape=[jax.ShapeDtypeStruct((T, D), F32), jax.ShapeDtypeStruct((2, T, nb * bw), BF16)],
        scratch_shapes=[pltpu.VMEM((tm, D), BF16), pltpu.VMEM((tm, D), F32)],
        compiler_params=_params(("arbitrary", "arbitrary")),
    )(x, gain, wgu, wgu, wd)


def _ffn_bwd(dh, x, gain, gu, wgu, wd, *, tm, name):
    T, D = x.shape
    tm = min(tm, T)
    nb, bw = wgu.shape[0] // 2, wgu.shape[2]

    def body(dh_ref, x_ref, gain_ref, gu_ref, wg_ref, wu_ref, wd_ref,
             dx_ref, dgain_ref, n_ref, df_ref, dgu_ref, hid_ref, dn_acc):
        i, j = pl.program_id(0), pl.program_id(1)

        @pl.when(j == 0)
        def _():
            xf = x_ref[...]
            n_ref[...] = (xf * _rstd(xf) * gain_ref[...]).astype(BF16)
            df_ref[...] = (0.5 * dh_ref[...]).astype(BF16)
            dn_acc[...] = jnp.zeros_like(dn_acc)

        @pl.when((i == 0) & (j == 0))
        def _():
            dgain_ref[...] = jnp.zeros_like(dgain_ref)

        dhid = _mm_nt(df_ref[...], wd_ref[...])
        g = gu_ref[0].astype(F32)
        u = gu_ref[1].astype(F32)
        s = jax.nn.sigmoid(g)
        silu = g * s
        hid_ref[...] = (silu * u).astype(BF16)
        dg = (dhid * u * (s * (1.0 + g * (1.0 - s)))).astype(BF16)
        du = (dhid * silu).astype(BF16)
        dgu_ref[0] = dg
        dgu_ref[1] = du
        dn_acc[...] += _mm_nt(dg, wg_ref[...]) + _mm_nt(du, wu_ref[...])

        @pl.when(j == nb - 1)
        def _():
            dx, dgain = _rms_bwd(x_ref[...], gain_ref[...], dn_acc[...])
            dx_ref[...] = dh_ref[...] + dx
            dgain_ref[...] += dgain

    row = lambda i, j: (i, 0)
    return pl.pallas_call(
        body, name=name, grid=(T // tm, nb),
        in_specs=[
            pl.BlockSpec((tm, D), row),
            pl.BlockSpec((tm, D), row),
            pl.BlockSpec((1, D), lambda i, j: (0, 0)),
            pl.BlockSpec((2, tm, bw), lambda i, j: (0, i, j)),
            pl.BlockSpec((None, D, bw), lambda i, j: (j, 0, 0)),
            pl.BlockSpec((None, D, bw), lambda i, j: (j + nb, 0, 0)),
            pl.BlockSpec((bw, D), lambda i, j: (j, 0)),
        ],
        out_specs=[
            pl.BlockSpec((tm, D), row),
            pl.BlockSpec((1, D), lambda i, j: (0, 0)),
            pl.BlockSpec((tm, D), row),
            pl.BlockSpec((tm, D), row),
            pl.BlockSpec((2, tm, bw), lambda i, j: (0, i, j)),
            pl.BlockSpec((tm, bw), lambda i, j: (i, j)),
        ],
        out_shape=[
            jax.ShapeDtypeStruct((T, D), F32),
            jax.ShapeDtypeStruct((1, D), F32),
            jax.ShapeDtypeStruct((T, D), BF16),
            jax.ShapeDtypeStruct((T, D), BF16),
            jax.ShapeDtypeStruct((2, T, nb * bw), BF16),
            jax.ShapeDtypeStruct((T, nb * bw), BF16),
        ],
        scratch_shapes=[pltpu.VMEM((tm, D), F32)],
        compiler_params=_params(("arbitrary", "arbitrary")),
    )(dh, x, gain, gu, wgu, wgu, wd)


def _wgrad(a, b, *, grid, a_spec, b_spec, out_spec, out_shape, acc_shape, name, split_lanes=0):
    nk = grid[2]

    def body(a_ref, b_ref, o_ref, acc):
        k = pl.program_id(2)

        @pl.when(k == 0)
        def _():
            acc[...] = jnp.zeros_like(acc)

        acc[...] += _mm_tn(a_ref[...].astype(BF16), b_ref[...].astype(BF16))

        @pl.when(k == nk - 1)
        def _():
            if split_lanes:
                for e in range(o_ref.shape[0]):
                    o_ref[e] = acc[:, e * split_lanes:(e + 1) * split_lanes].astype(o_ref.dtype)
            else:
                o_ref[...] = acc[...].astype(o_ref.dtype)

    return pl.pallas_call(
        body, name=name, grid=grid, in_specs=[a_spec, b_spec], out_specs=out_spec,
        out_shape=jax.ShapeDtypeStruct(out_shape, BF16),
        scratch_shapes=[pltpu.VMEM(acc_shape, F32)],
        compiler_params=_params(("arbitrary", "arbitrary", "arbitrary")),
    )(a, b)


def _wgrad_gate_up(n, dgu, *, tk, name):
    T, D = n.shape
    tk = min(tk, T)
    bw = FF_SHARD_PAD * 2
    nb = dgu.shape[2] // bw
    return _wgrad(
        n, dgu, grid=(1, 2 * nb, T // tk), name=name,
        a_spec=pl.BlockSpec((tk, D), lambda m, c, k: (k, 0)),
        b_spec=pl.BlockSpec((None, tk, bw), lambda m, c, k: (c // nb, k, c % nb)),
        out_spec=pl.BlockSpec((None, D, bw), lambda m, c, k: (c, 0, 0)),
        out_shape=(2 * nb, D, bw), acc_shape=(D, bw))


def _wgrad_down(hid, df, *, tk, name):
    T, D = df.shape
    tk = min(tk, T)
    bw = FF_SHARD_PAD * 2
    nb = hid.shape[1] // bw
    return _wgrad(
        hid, df, grid=(nb, 1, T // tk), name=name,
        a_spec=pl.BlockSpec((tk, bw), lambda m, c, k: (k, m)),
        b_spec=pl.BlockSpec((tk, D), lambda m, c, k: (k, 0)),
        out_spec=pl.BlockSpec((bw, D), lambda m, c, k: (m, 0)),
        out_shape=(nb * bw, D), acc_shape=(bw, D))


def _wgrad_in(un, dproj, *, tk, name):
    T, D = un.shape
    tk = min(tk, T)
    bw = dproj.shape[1] // N_DEV
    return _wgrad(
        un, dproj, grid=(1, N_DEV, T // tk), name=name,
        a_spec=pl.BlockSpec((tk, D), lambda m, c, k: (k, 0)),
        b_spec=pl.BlockSpec((tk, bw), lambda m, c, k: (k, c)),
        out_spec=pl.BlockSpec((None, D, bw), lambda m, c, k: (c, 0, 0)),
        out_shape=(N_DEV, D, bw), acc_shape=(D, bw))


def _wgrad_full(a, b, *, tk, name, split_lanes=0):
    T, M = a.shape
    tk = min(tk, T)
    N = b.shape[1]
    if split_lanes:
        out_shape = (N // split_lanes, M, split_lanes)
        out_spec = pl.BlockSpec(out_shape, lambda m, c, k: (0, 0, 0))
    else:
        out_shape = (M, N)
        out_spec = pl.BlockSpec(out_shape, lambda m, c, k: (0, 0))
    return _wgrad(
        a, b, grid=(1, 1, T // tk), name=name,
        a_spec=pl.BlockSpec((tk, M), lambda m, c, k: (k, 0)),
        b_spec=pl.BlockSpec((tk, N), lambda m, c, k: (k, 0)),
        out_spec=out_spec, out_shape=out_shape, acc_shape=(M, N), split_lanes=split_lanes)


def _loss_bwd(h, target, gain, *, tm, name):
    T, D = h.shape
    tm = min(tm, T)

    def body(h_ref, t_ref, gain_ref, dh_ref, loss_ref, dgain_ref):
        @pl.when(pl.program_id(0) == 0)
        def _():
            loss_ref[...] = jnp.zeros_like(loss_ref)
            dgain_ref[...] = jnp.zeros_like(dgain_ref)

        xf = h_ref[...]
        gain = gain_ref[...]
        err = xf * _rstd(xf) * gain - t_ref[...]
        loss_ref[...] += 0.5 * jnp.sum(jnp.mean(err * err, axis=-1, keepdims=True), axis=0, keepdims=True)
        dx, dgain = _rms_bwd(xf, gain, err * (1.0 / D))
        dh_ref[...] = dx
        dgain_ref[...] += dgain

    row = lambda i: (i, 0)
    fixed = lambda i: (0, 0)
    return pl.pallas_call(
        body, name=name, grid=(T // tm,),
        in_specs=[pl.BlockSpec((tm, D), row), pl.BlockSpec((tm, D), row), pl.BlockSpec((1, D), fixed)],
        out_specs=[pl.BlockSpec((tm, D), row), pl.BlockSpec((1, 128), fixed), pl.BlockSpec((1, D), fixed)],
        out_shape=[jax.ShapeDtypeStruct((T, D), F32), jax.ShapeDtypeStruct((1, 128), F32),
                   jax.ShapeDtypeStruct((1, D), F32)],
        compiler_params=_params(("arbitrary",)),
    )(h, target, gain)


def _inproj_fwd(h, gain, w_in, *, tm, name):
    T, D = h.shape
    tm = min(tm, T)
    nb, bw = w_in.shape[0], w_in.shape[2]

    def body(h_ref, gain_ref, w_ref, un_ref, proj_ref):
        @pl.when(pl.program_id(1) == 0)
        def _():
            xf = h_ref[...]
            un_ref[...] = (xf * _rstd(xf) * gain_ref[...]).astype(BF16)

        proj_ref[...] = _mm(un_ref[...], w_ref[...])

    return pl.pallas_call(
        body, name=name, grid=(T // tm, nb),
        in_specs=[
            pl.BlockSpec((tm, D), lambda i, j: (i, 0)),
            pl.BlockSpec((1, D), lambda i, j: (0, 0)),
            pl.BlockSpec((None, D, bw), lambda i, j: (j, 0, 0)),
        ],
        out_specs=[pl.BlockSpec((tm, D), lambda i, j: (i, 0)), pl.BlockSpec((tm, bw), lambda i, j: (i, j))],
        out_shape=[jax.ShapeDtypeStruct((T, D), BF16), jax.ShapeDtypeStruct((T, nb * bw), F32)],
        compiler_params=_params(("arbitrary", "arbitrary")),
    )(h, gain, w_in)


def _inproj_bwd(dproj, dh, h, gain, w_in, *, tm, name):
    T, D = h.shape
    tm = min(tm, T)
    nb, bw = w_in.shape[0], w_in.shape[2]

    def body(dp_ref, dh_ref, h_ref, gain_ref, w_ref, dx_ref, dgain_ref, acc):
        i, j = pl.program_id(0), pl.program_id(1)

        @pl.when(j == 0)
        def _():
            acc[...] = jnp.zeros_like(acc)

        @pl.when((i == 0) & (j == 0))
        def _():
            dgain_ref[...] = jnp.zeros_like(dgain_ref)

        acc[...] += _mm_nt(dp_ref[...], w_ref[...])

        @pl.when(j == nb - 1)
        def _():
            dx, dgain = _rms_bwd(h_ref[...], gain_ref[...], acc[...])
            dx_ref[...] = dh_ref[...] + dx
            dgain_ref[...] += dgain

    row = lambda i, j: (i, 0)
    return pl.pallas_call(
        body, name=name, grid=(T // tm, nb),
        in_specs=[
            pl.BlockSpec((tm, bw), lambda i, j: (i, j)),
            pl.BlockSpec((tm, D), row),
            pl.BlockSpec((tm, D), row),
            pl.BlockSpec((1, D), lambda i, j: (0, 0)),
            pl.BlockSpec((None, D, bw), lambda i, j: (j, 0, 0)),
        ],
        out_specs=[pl.BlockSpec((tm, D), row), pl.BlockSpec((1, D), lambda i, j: (0, 0))],
        out_shape=[jax.ShapeDtypeStruct((T, D), F32), jax.ShapeDtypeStruct((1, D), F32)],
        scratch_shapes=[pltpu.VMEM((tm, D), F32)],
        compiler_params=_params(("arbitrary", "arbitrary")),
    )(dproj, dh, h, gain, w_in)


def _window_sum(x, row, doublings, *, backward):
    T = x.shape[0]
    s = x
    for k in range(doublings):
        sh = 1 << k
        if backward:
            s = s + jnp.where(row < T - sh, pltpu.roll(s, T - sh, 0), 0.0)
        else:
            s = s + jnp.where(row >= sh, pltpu.roll(s, sh, 0), 0.0)
    return s


def _pool_fwd(proj, w_group, scale, *, name):
    T = proj.shape[0]

    def body(xp_ref, w_ref, scale_ref, p_ref):
        row = lax.broadcasted_iota(jnp.int32, (T, POOL_GROUP), 0)
        for gi, window in enumerate(POOL_WINDOWS):
            cols = slice(gi * POOL_GROUP, (gi + 1) * POOL_GROUP)
            x = xp_ref[:, cols]
            inv_count = 1.0 / jnp.minimum(row + 1, window).astype(F32)
            yc = _window_sum(x, row, gi + 1, backward=False) * inv_count - x
            pre = _mm(yc.astype(BF16), w_ref[gi].astype(BF16))
            p_ref[:, cols] = pre * scale_ref[:, cols]

    return pl.pallas_call(
        body, name=name, grid=(1,),
        in_specs=[
            pl.BlockSpec((T, POOL_WIDTH), lambda i: (0, 0)),
            pl.BlockSpec(w_group.shape, lambda i: (0, 0, 0)),
            pl.BlockSpec((1, POOL_WIDTH), lambda i: (0, 0)),
        ],
        out_specs=pl.BlockSpec((T, POOL_WIDTH), lambda i: (0, 0)),
        out_shape=jax.ShapeDtypeStruct((T, POOL_WIDTH), F32),
        compiler_params=_params(("arbitrary",)),
    )(proj, w_group, scale)


def _pool_bwd(dp, proj, w_group, scale, *, name):
    T = proj.shape[0]

    def body(dp_ref, xp_ref, w_ref, scale_ref, dxp_ref, dw_ref, dscale_ref):
        row = lax.broadcasted_iota(jnp.int32, (T, POOL_GROUP), 0)
        for gi, window in enumerate(POOL_WINDOWS):
            cols = slice(gi * POOL_GROUP, (gi + 1) * POOL_GROUP)
            x = xp_ref[:, cols]
            inv_count = 1.0 / jnp.minimum(row + 1, window).astype(F32)
            yc = (_window_sum(x, row, gi + 1, backward=False) * inv_count - x).astype(BF16)
            w = w_ref[gi].astype(BF16)
            pre = _mm(yc, w)
            dpg = dp_ref[:, cols]
            dscale_ref[:, cols] = jnp.sum(dpg * pre, axis=0, keepdims=True)
            dpre = (dpg * scale_ref[:, cols]).astype(BF16)
            dw_ref[gi] = _mm_tn(yc, dpre)
            dyc = _mm_nt(dpre, w)
            dxp_ref[:, cols] = _window_sum(dyc * inv_count, row, gi + 1, backward=True) - dyc

    return pl.pallas_call(
        body, name=name, grid=(1,),
        in_specs=[
            pl.BlockSpec((T, POOL_WIDTH), lambda i: (0, 0)),
            pl.BlockSpec((T, POOL_WIDTH), lambda i: (0, 0)),
            pl.BlockSpec(w_group.shape, lambda i: (0, 0, 0)),
            pl.BlockSpec((1, POOL_WIDTH), lambda i: (0, 0)),
        ],
        out_specs=[
            pl.BlockSpec((T, POOL_WIDTH), lambda i: (0, 0)),
            pl.BlockSpec(w_group.shape, lambda i: (0, 0, 0)),
            pl.BlockSpec((1, POOL_WIDTH), lambda i: (0, 0)),
        ],
        out_shape=[jax.ShapeDtypeStruct((T, POOL_WIDTH), F32), jax.ShapeDtypeStruct(w_group.shape, F32),
                   jax.ShapeDtypeStruct((1, POOL_WIDTH), F32)],
        compiler_params=_params(("arbitrary",)),
    )(dp, proj, w_group, scale)


def _log_sigmoids(z):
    t = jnp.log(1.0 + jnp.exp(-jnp.abs(z)))
    return jnp.minimum(z, 0.0) - t, -jnp.maximum(z, 0.0) - t


def _tri_sum(x, tri):
    hi = x.astype(BF16)
    lo = (x - hi.astype(F32)).astype(BF16)
    return _mm(hi, tri) + _mm(lo, tri)


def _attn_specs(T, tq):
    q_col = POOL_WIDTH // HEAD_PAIR
    k_col = q_col + SB_WIDTH // HEAD_PAIR
    v_col = k_col + SB_WIDTH // HEAD_PAIR
    return [
        pl.BlockSpec((tq, HEAD_PAIR), lambda p, i: (i, q_col + p)),
        pl.BlockSpec((T, HEAD_PAIR), lambda p, i: (0, k_col + p)),
        pl.BlockSpec((T, HEAD_PAIR), lambda p, i: (0, v_col + p)),
    ]


def _attn_fwd(proj, *, name):
    T = proj.shape[0]
    tq = ATTN_BLOCK

    def body(q_ref, k_ref, v_ref, o_ref, lt_ref, kb_scr, vb_scr):
        qi = pl.program_id(1)

        @pl.when(qi == 0)
        def _():
            kb_scr[...] = k_ref[...].astype(BF16)
            vb_scr[...] = v_ref[...].astype(BF16)

        head0 = lax.broadcasted_iota(jnp.int32, (tq, HEAD_PAIR), 1) < HEAD_DIM
        q = q_ref[...] * ATTN_SCALE
        qs = (jnp.where(head0, q, 0.0).astype(BF16), jnp.where(head0, 0.0, q).astype(BF16))
        r = lax.broadcasted_iota(jnp.int32, (tq, tq), 0)
        c = lax.broadcasted_iota(jnp.int32, (tq, tq), 1)
        later = (r > c).astype(BF16)
        causal = c < r

        def block(kj, carry, valid):
            off = pl.multiple_of(kj * tq, tq)
            kb = kb_scr[pl.ds(off, tq), :]
            vb = vb_scr[pl.ds(off, tq), :]
            out = []
            for h in range(2):
                run, o = carry[2 * h], carry[2 * h + 1]
                z = _mm_nt(qs[h], kb)
                lb, lm = _log_sigmoids(z)
                if valid is not None:
                    lm = jnp.where(valid, lm, 0.0)
                a = jnp.exp(lb + run + _tri_sum(lm, later))
                if valid is not None:
                    a = jnp.where(valid, a, 0.0)
                out += [run + jnp.sum(lm, axis=1, keepdims=True), o + _mm(a.astype(BF16), vb)]
            return tuple(out)

        zero = (jnp.zeros((tq, 1), F32), jnp.zeros((tq, HEAD_PAIR), F32))
        carry = block(qi, zero + zero, causal)
        carry = lax.fori_loop(0, qi, lambda it, cr: block(qi - 1 - it, cr, None), carry)
        o_ref[...] = jnp.where(head0, carry[1], carry[3])
        lt_ref[...] = jnp.where(head0, carry[0], carry[2])

    out_spec = pl.BlockSpec((tq, HEAD_PAIR), lambda p, i: (i, p))
    return pl.pallas_call(
        body, name=name, grid=(N_HEADS // 2, T // tq),
        in_specs=_attn_specs(T, tq), out_specs=[out_spec, out_spec],
        out_shape=[jax.ShapeDtypeStruct((T, SB_WIDTH), F32), jax.ShapeDtypeStruct((T, SB_WIDTH), F32)],
        scratch_shapes=[pltpu.VMEM((T, HEAD_PAIR), BF16), pltpu.VMEM((T, HEAD_PAIR), BF16)],
        compiler_params=_params(("arbitrary", "arbitrary")),
    )(proj, proj, proj)


def _attn_bwd(proj, do, ltot, *, name):
    T = proj.shape[0]
    tq = ATTN_BLOCK

    def body(q_ref, k_ref, v_ref, do_ref, lt_ref, dq_ref, dk_ref, dv_ref, kb_scr, vb_scr):
        qi = pl.program_id(1)

        @pl.when(qi == 0)
        def _():
            kb_scr[...] = k_ref[...].astype(BF16)
            vb_scr[...] = v_ref[...].astype(BF16)
            dk_ref[...] = jnp.zeros_like(dk_ref)
            dv_ref[...] = jnp.zeros_like(dv_ref)

        head0 = lax.broadcasted_iota(jnp.int32, (tq, HEAD_PAIR), 1) < HEAD_DIM
        q, do_, lt = q_ref[...] * ATTN_SCALE, do_ref[...], lt_ref[...]
        qs = (jnp.where(head0, q, 0.0).astype(BF16), jnp.where(head0, 0.0, q).astype(BF16))
        dos = (jnp.where(head0, do_, 0.0).astype(BF16), jnp.where(head0, 0.0, do_).astype(BF16))
        lts = (jnp.max(jnp.where(head0, lt, -jnp.inf), axis=1, keepdims=True),
               jnp.max(jnp.where(head0, -jnp.inf, lt), axis=1, keepdims=True))
        r = lax.broadcasted_iota(jnp.int32, (tq, tq), 0)
        c = lax.broadcasted_iota(jnp.int32, (tq, tq), 1)
        upto = (r <= c).astype(BF16)
        before = (r < c).astype(BF16)
        causal = c < r

        def block(kj, carry, valid):
            off = pl.multiple_of(kj * tq, tq)
            kb = kb_scr[pl.ds(off, tq), :]
            vb = vb_scr[pl.ds(off, tq), :]
            dk_blk = jnp.zeros((tq, HEAD_PAIR), F32)
            dv_blk = jnp.zeros((tq, HEAD_PAIR), F32)
            out = []
            for h in range(2):
                run_lm, run_e, dq = carry[3 * h:3 * h + 3]
                z = _mm_nt(qs[h], kb)
                lb, lm = _log_sigmoids(z)
                if valid is not None:
                    lm = jnp.where(valid, lm, 0.0)
                a = jnp.exp(lb + (lts[h] - run_lm - _tri_sum(lm, upto)))
                if valid is not None:
                    a = jnp.where(valid, a, 0.0)
                e = _mm_nt(dos[h], vb) * a
                beta = jnp.exp(lb)
                dz = e * (1.0 - beta) - (run_e + _tri_sum(e, before)) * beta
                if valid is not None:
                    dz = jnp.where(valid, dz, 0.0)
                dz = dz.astype(BF16)
                dk_blk += _mm_tn(dz, qs[h])
                dv_blk += _mm_tn(a.astype(BF16), dos[h])
                out += [run_lm + jnp.sum(lm, axis=1, keepdims=True), run_e + jnp.sum(e, axis=1, keepdims=True),
                        dq + _mm(dz, kb)]
            dk_ref[pl.ds(off, tq), :] += dk_blk
            dv_ref[pl.ds(off, tq), :] += dv_blk
            return tuple(out)

        zero = (jnp.zeros((tq, 1), F32), jnp.zeros((tq, 1), F32), jnp.zeros((tq, HEAD_PAIR), F32))
        carry = lax.fori_loop(0, qi, lambda kj, cr: block(kj, cr, None), zero + zero)
        carry = block(qi, carry, causal)
        dq_ref[...] = jnp.where(head0, carry[2], carry[5]) * ATTN_SCALE

    blk = pl.BlockSpec((tq, HEAD_PAIR), lambda p, i: (i, p))
    seq = pl.BlockSpec((T, HEAD_PAIR), lambda p, i: (0, p))
    return pl.pallas_call(
        body, name=name, grid=(N_HEADS // 2, T // tq),
        in_specs=_attn_specs(T, tq) + [blk, blk], out_specs=[blk, seq, seq],
        out_shape=[jax.ShapeDtypeStruct((T, SB_WIDTH), F32)] * 3,
        scratch_shapes=[pltpu.VMEM((T, HEAD_PAIR), BF16), pltpu.VMEM((T, HEAD_PAIR), BF16)],
        compiler_params=_params(("arbitrary", "arbitrary")),
    )(proj, proj, proj, do, ltot)


def _branch(act_bf16, w_ref):
    return jnp.concatenate([_mm(act_bf16, w_ref[e]) for e in range(w_ref.shape[0])], axis=1)


def _mix_specs(T, D, tm, wbp, w_out):
    gate_col = (POOL_WIDTH + 3 * SB_WIDTH) // D
    row = lambda i: (i, 0)
    return [
        pl.BlockSpec((tm, D), row),
        pl.BlockSpec((tm, POOL_WIDTH), row),
        pl.BlockSpec((tm, SB_WIDTH), row),
        pl.BlockSpec((tm, D), lambda i: (i, gate_col)),
        pl.BlockSpec((tm, D), lambda i: (i, gate_col + 1)),
        pl.BlockSpec(wbp.shape, lambda i: (0, 0, 0)),
        pl.BlockSpec(wbp.shape, lambda i: (0, 0, 0)),
        pl.BlockSpec(w_out.shape, lambda i: (0, 0)),
    ]


def _mix_fwd(h, p, o, proj, wbp, wba, w_out, *, tm, name):
    T, D = h.shape
    tm = min(tm, T)

    def body(h_ref, p_ref, o_ref, glp_ref, gls_ref, wbp_ref, wba_ref, wout_ref, hout_ref, m_ref):
        yp = _branch(p_ref[...].astype(BF16), wbp_ref)
        ys = _branch(o_ref[...].astype(BF16), wba_ref)
        m = (jax.nn.sigmoid(glp_ref[...]) * yp + jax.nn.sigmoid(gls_ref[...]) * ys).astype(BF16)
        m_ref[...] = m
        hout_ref[...] = h_ref[...] + _mm(m, wout_ref[...])

    row = lambda i: (i, 0)
    return pl.pallas_call(
        body, name=name, grid=(T // tm,),
        in_specs=_mix_specs(T, D, tm, wbp, w_out),
        out_specs=[pl.BlockSpec((tm, D), row), pl.BlockSpec((tm, D), row)],
        out_shape=[jax.ShapeDtypeStruct((T, D), F32), jax.ShapeDtypeStruct((T, D), BF16)],
        compiler_params=_params(("arbitrary",)),
    )(h, p, o, proj, proj, wbp, wba, w_out)


def _mix_bwd(dh, p, o, proj, wbp, wba, w_out, *, tm, name):
    T, D = dh.shape
    tm = min(tm, T)
    bw = wbp.shape[2]

    def body(dh_ref, p_ref, o_ref, glp_ref, gls_ref, wbp_ref, wba_ref, wout_ref,
             dyp_ref, dys_ref, dp_ref, do_ref, dgl_ref):
        dm = _mm_nt(dh_ref[...].astype(BF16), wout_ref[...])
        yp = _branch(p_ref[...].astype(BF16), wbp_ref)
        ys = _branch(o_ref[...].astype(BF16), wba_ref)
        gp = jax.nn.sigmoid(glp_ref[...])
        gs = jax.nn.sigmoid(gls_ref[...])
        dyp = (dm * gp).astype(BF16)
        dys = (dm * gs).astype(BF16)
        dyp_ref[...] = dyp
        dys_ref[...] = dys
        dgl_ref[:, :D] = (dm * yp * gp * (1.0 - gp)).astype(BF16)
        dgl_ref[:, D:] = (dm * ys * gs * (1.0 - gs)).astype(BF16)
        dp = jnp.zeros(dp_ref.shape, F32)
        do_ = jnp.zeros(do_ref.shape, F32)
        for e in range(wbp_ref.shape[0]):
            dp += _mm_nt(dyp[:, e * bw:(e + 1) * bw], wbp_ref[e])
            do_ += _mm_nt(dys[:, e * bw:(e + 1) * bw], wba_ref[e])
        dp_ref[...] = dp
        do_ref[...] = do_

    row = lambda i: (i, 0)
    return pl.pallas_call(
        body, name=name, grid=(T // tm,),
        in_specs=_mix_specs(T, D, tm, wbp, w_out),
        out_specs=[pl.BlockSpec((tm, D), row), pl.BlockSpec((tm, D), row), pl.BlockSpec((tm, POOL_WIDTH), row),
                   pl.BlockSpec((tm, SB_WIDTH), row), pl.BlockSpec((tm, 2 * D), row)],
        out_shape=[jax.ShapeDtypeStruct((T, D), BF16), jax.ShapeDtypeStruct((T, D), BF16),
                   jax.ShapeDtypeStruct((T, POOL_WIDTH), F32), jax.ShapeDtypeStruct((T, SB_WIDTH), F32),
                   jax.ShapeDtypeStruct((T, 2 * D), BF16)],
        compiler_params=_params(("arbitrary",)),
    )(dh, p, o, proj, proj, wbp, wba, w_out)


def _adamw(w, g, m, v, *, name):
    R, C = w.shape
    tr = R if R * C <= 512 * 1024 else 256

    def body(w_ref, g_ref, m_ref, v_ref, d_ref, nm_ref, nv_ref):
        g_ = g_ref[...]
        m_ = ADAM_B1 * m_ref[...] + (1.0 - ADAM_B1) * g_
        v_ = ADAM_B2 * v_ref[...] + (1.0 - ADAM_B2) * (g_ * g_)
        m_hat = m_ / (1.0 - ADAM_B1 ** ADAM_STEP)
        v_hat = v_ / (1.0 - ADAM_B2 ** ADAM_STEP)
        d_ref[...] = -ADAM_LR * (m_hat / (jnp.sqrt(v_hat) + ADAM_EPS) + ADAM_WD * w_ref[...])
        nm_ref[...] = m_
        nv_ref[...] = v_

    spec = pl.BlockSpec((tr, C), lambda i: (i, 0))
    return pl.pallas_call(
        body, name=name, grid=(R // tr,), in_specs=[spec] * 4, out_specs=[spec] * 3,
        out_shape=[jax.ShapeDtypeStruct((R, C), F32)] * 3,
        compiler_params=_params(("arbitrary",)),
    )(w, g, m, v)


def _position():
    return lax.axis_index("x"), lax.axis_index("y"), lax.axis_index("c")


def _all_gather(shards, *, name):
    n = len(shards)

    def body(*refs):
        ins, outs = refs[:n], refs[n:2 * n]
        send_sems, recv_sems, local_sems = refs[2 * n:]
        x, y, c = _position()
        me, sibling = (x, y, c), (x, y, 1 - c)
        chips = [(1 - x, y), (x, 1 - y), (1 - x, 1 - y)]

        def block(a, pos):
            return outs[a].at[4 * pos[0] + 2 * pos[1] + pos[2]]

        def copy(a, k, pos, to, src=None):
            return pltpu.make_async_remote_copy(
                src_ref=block(a, pos) if src is None else src, dst_ref=block(a, pos),
                send_sem=send_sems.at[7 * a + k], recv_sem=recv_sems.at[7 * a + k],
                device_id=to, device_id_type=MESH)

        started = []
        for a in range(n):
            mine = pltpu.make_async_copy(ins[a], block(a, me), local_sems.at[a])
            mine.start()
            started.append(mine)
        sends = []
        for a in range(n):
            sends += [copy(a, 1 + j, me, (*chip, c), src=ins[a]) for j, chip in enumerate(chips)]
            sends.append(copy(a, 0, me, sibling, src=ins[a]))
        for cp in sends:
            cp.start()
        for j, chip in enumerate(chips):
            for a in range(n):
                copy(a, 1 + j, (*chip, c), me).wait_recv()
                passed = copy(a, 4 + j, (*chip, c), sibling)
                passed.start()
                sends.append(passed)
        for a in range(n):
            copy(a, 0, sibling, me).wait_recv()
            for j, chip in enumerate(chips):
                copy(a, 4 + j, (*chip, 1 - c), me).wait_recv()
        for cp in sends:
            cp.wait_send()
        for cp in started:
            cp.wait()

    any_spec = pl.BlockSpec(memory_space=pl.ANY)
    return pl.pallas_call(
        body, name=name,
        in_specs=[any_spec] * n, out_specs=[any_spec] * n,
        out_shape=[jax.ShapeDtypeStruct((N_DEV,) + s.shape, s.dtype) for s in shards],
        scratch_shapes=[pltpu.SemaphoreType.DMA((7 * n,)), pltpu.SemaphoreType.DMA((7 * n,)),
                        pltpu.SemaphoreType.DMA((n,))],
    )(*shards)


def _reduce_scatter(grads, *, name):
    _, R, C = grads.shape
    rc = 128 if R % 128 == 0 else R

    def body(g_ref, out_ref, mine, theirs, partial, landed, send_sems, recv_sems, local_sems):
        x, y, c = _position()
        my_chip = 2 * x + y

        def swap(s):
            return pltpu.make_async_remote_copy(
                src_ref=g_ref.at[2 * s + (1 - c)], dst_ref=theirs.at[s],
                send_sem=send_sems.at[s], recv_sem=recv_sems.at[s],
                device_id=(x, y, 1 - c), device_id_type=MESH)

        def load(s):
            return pltpu.make_async_copy(g_ref.at[2 * s + c], mine.at[s], local_sems.at[s])

        def cross(j):
            chip = my_chip ^ j
            return pltpu.make_async_remote_copy(
                src_ref=partial.at[j - 1], dst_ref=landed.at[j - 1],
                send_sem=send_sems.at[3 + j], recv_sem=recv_sems.at[3 + j],
                device_id=(chip // 2, chip % 2, c), device_id_type=MESH)

        for s in range(4):
            swap(s).start()
            load(s).start()
        for s in range(4):
            load(s).wait()
            swap(s).wait_recv()

        def chip_sum(chip, rows):
            return mine[chip, rows, :].astype(F32) + theirs[chip, rows, :].astype(F32)

        for j in (1, 2, 3):
            @pl.loop(0, R // rc)
            def _(t):
                rows = pl.ds(pl.multiple_of(t * rc, rc), rc)
                partial[j - 1, rows, :] = chip_sum(my_chip ^ j, rows).astype(BF16)
            cross(j).start()

        @pl.loop(0, R // rc)
        def _(t):
            rows = pl.ds(pl.multiple_of(t * rc, rc), rc)
            out_ref[rows, :] = chip_sum(my_chip, rows)

        for j in (1, 2, 3):
            cross(j).wait_recv()

            @pl.loop(0, R // rc)
            def _(t):
                rows = pl.ds(pl.multiple_of(t * rc, rc), rc)
                out_ref[rows, :] += landed[j - 1, rows, :].astype(F32)

        for s in range(4):
            swap(s).wait_send()
        for j in (1, 2, 3):
            cross(j).wait_send()

    return pl.pallas_call(
        body, name=name,
        in_specs=[pl.BlockSpec(memory_space=pl.ANY)],
        out_specs=pl.BlockSpec(memory_space=pltpu.VMEM),
        out_shape=jax.ShapeDtypeStruct((R, C), F32),
        scratch_shapes=[
            pltpu.VMEM((4, R, C), BF16), pltpu.VMEM((4, R, C), BF16),
            pltpu.VMEM((3, R, C), BF16), pltpu.VMEM((3, R, C), BF16),
            pltpu.SemaphoreType.DMA((7,)), pltpu.SemaphoreType.DMA((7,)), pltpu.SemaphoreType.DMA((4,)),
        ],
        compiler_params=_params(),
    )(grads)


def _all_reduce_small(slab, *, name):
    R, C = slab.shape

    def body(in_ref, out_ref, gathered, send_sems, recv_sems):
        x, y, c = _position()
        me = 4 * x + 2 * y + c

        def copy(k):
            peer = me ^ k
            return pltpu.make_async_remote_copy(
                src_ref=in_ref, dst_ref=gathered.at[me],
                send_sem=send_sems.at[k - 1], recv_sem=recv_sems.at[k - 1],
                device_id=(peer // 4, (peer // 2) % 2, peer % 2), device_id_type=MESH)

        def arrival(k):
            return pltpu.make_async_remote_copy(
                src_ref=in_ref, dst_ref=gathered.at[me ^ k],
                send_sem=send_sems.at[k - 1], recv_sem=recv_sems.at[k - 1],
                device_id=(x, y, c), device_id_type=MESH)

        for k in range(1, N_DEV):
            copy(k).start()
        gathered[me] = in_ref[...]
        for k in range(1, N_DEV):
            arrival(k).wait_recv()
        total = gathered[0]
        for d in range(1, N_DEV):
            total = total + gathered[d]
        out_ref[...] = total
        for k in range(1, N_DEV):
            copy(k).wait_send()

    return pl.pallas_call(
        body, name=name,
        in_specs=[pl.BlockSpec(memory_space=pltpu.VMEM)],
        out_specs=pl.BlockSpec(memory_space=pltpu.VMEM),
        out_shape=jax.ShapeDtypeStruct((R, C), F32),
        scratch_shapes=[pltpu.VMEM((N_DEV, R, C), F32),
                        pltpu.SemaphoreType.DMA((N_DEV - 1,)), pltpu.SemaphoreType.DMA((N_DEV - 1,))],
        compiler_params=_params(),
    )(slab)


def _local_step(x, target, norms, pool_w_group, pool_scale, wgu1, wd1, w_in, wbp, wba, w_out, wgu2, wd2):
    n1g, nmg, n2g, nfg = norms
    h1, gu1 = _ffn_fwd(x, n1g, wgu1, wd1, tm=512, name="ffn1_fwd")
    un, proj = _inproj_fwd(h1, nmg, w_in, tm=512, name="inproj_fwd")
    p = _pool_fwd(proj, pool_w_group, pool_scale, name="pool_fwd")
    o, ltot = _attn_fwd(proj, name="attn_fwd")
    h2, m = _mix_fwd(h1, p, o, proj, wbp, wba, w_out, tm=256, name="mix_fwd")
    h3, gu2 = _ffn_fwd(h2, n2g, wgu2, wd2, tm=512, name="ffn2_fwd")
    dh3, loss, d_nf = _loss_bwd(h3, target, nfg, tm=256, name="loss_bwd")

    dh2, d_n2, n2, df2, dgu2, hid2 = _ffn_bwd(dh3, h2, n2g, gu2, wgu2, wd2, tm=256, name="ffn2_bwd")
    d_wgu2 = _wgrad_gate_up(n2, dgu2, tk=512, name="ffn2_wgrad_gate_up")
    d_wd2 = _wgrad_down(hid2, df2, tk=512, name="ffn2_wgrad_down")

    dyp, dys, dp, do, dgl = _mix_bwd(dh2, p, o, proj, wbp, wba, w_out, tm=256, name="mix_bwd")
    d_wout = _wgrad_full(m, dh2, tk=512, name="wgrad_out")
    d_wbp = _wgrad_full(p, dyp, tk=512, name="wgrad_branch_pool", split_lanes=wbp.shape[2])
    d_wba = _wgrad_full(o, dys, tk=512, name="wgrad_branch_attn", split_lanes=wba.shape[2])
    dxp, d_wgroup, d_scale = _pool_bwd(dp, proj, pool_w_group, pool_scale, name="pool_bwd")
    dq, dk, dv = _attn_bwd(proj, do, ltot, name="attn_bwd")
    dproj = jnp.concatenate([dxp.astype(BF16), dq.astype(BF16), dk.astype(BF16), dv.astype(BF16), dgl], axis=1)
    dh1, d_nm = _inproj_bwd(dproj, dh2, h1, nmg, w_in, tm=256, name="inproj_bwd")
    d_win = _wgrad_in(un, dproj, tk=512, name="wgrad_in")

    dx, d_n1, n1, df1, dgu1, hid1 = _ffn_bwd(dh1, x, n1g, gu1, wgu1, wd1, tm=256, name="ffn1_bwd")
    d_wgu1 = _wgrad_gate_up(n1, dgu1, tk=512, name="ffn1_wgrad_gate_up")
    d_wd1 = _wgrad_down(hid1, df1, tk=512, name="ffn1_wgrad_down")

    sharded = (d_wgu1, d_wd1, d_win, d_wbp, d_wba, d_wout, d_wgu2, d_wd2)
    replicated = (d_n1, d_nm, d_n2, d_nf, d_scale, d_wgroup)
    return loss, dx, sharded, replicated


def _pad_gate_up(w):
    d = w.shape[0]
    w = w.astype(BF16).reshape(d, 2, FF_SHARD)
    return jnp.pad(w, ((0, 0), (0, 0), (0, FF_SHARD_PAD - FF_SHARD))).reshape(d, 2 * FF_SHARD_PAD)


def _unpad_gate_up(g):
    d = g.shape[0]
    return g.reshape(d, 2, FF_SHARD_PAD)[:, :, :FF_SHARD].reshape(d, 2 * FF_SHARD)


def _pad_down(w):
    return jnp.pad(w.astype(BF16), ((0, FF_SHARD_PAD - FF_SHARD), (0, 0)))


def kernel(x, ffn1_norm, ffn1_w_gate_up, ffn1_w_down, mix_norm, w_in, pool_w_group, pool_scale, w_branch_pool, w_branch_attn, w_out, ffn2_norm, ffn2_w_gate_up, ffn2_w_down, final_norm, loss_target, m_ffn1_norm, m_ffn1_w_gate_up, m_ffn1_w_down, m_mix_norm, m_w_in, m_pool_w_group, m_pool_scale, m_w_branch_pool, m_w_branch_attn, m_w_out, m_ffn2_norm, m_ffn2_w_gate_up, m_ffn2_w_down, m_final_norm, v_ffn1_norm, v_ffn1_w_gate_up, v_ffn1_w_down, v_mix_norm, v_w_in, v_pool_w_group, v_pool_scale, v_w_branch_pool, v_w_branch_attn, v_w_out, v_ffn2_norm, v_ffn2_w_gate_up, v_ffn2_w_down, v_final_norm):
    D = x.shape[-1]
    weights = dict(ffn1_norm=ffn1_norm, ffn1_w_gate_up=ffn1_w_gate_up, ffn1_w_down=ffn1_w_down, mix_norm=mix_norm,
                   w_in=w_in, pool_w_group=pool_w_group, pool_scale=pool_scale, w_branch_pool=w_branch_pool,
                   w_branch_attn=w_branch_attn, w_out=w_out, ffn2_norm=ffn2_norm, ffn2_w_gate_up=ffn2_w_gate_up,
                   ffn2_w_down=ffn2_w_down, final_norm=final_norm)
    first = dict(ffn1_norm=m_ffn1_norm, ffn1_w_gate_up=m_ffn1_w_gate_up, ffn1_w_down=m_ffn1_w_down,
                 mix_norm=m_mix_norm, w_in=m_w_in, pool_w_group=m_pool_w_group, pool_scale=m_pool_scale,
                 w_branch_pool=m_w_branch_pool, w_branch_attn=m_w_branch_attn, w_out=m_w_out,
                 ffn2_norm=m_ffn2_norm, ffn2_w_gate_up=m_ffn2_w_gate_up, ffn2_w_down=m_ffn2_w_down,
                 final_norm=m_final_norm)
    second = dict(ffn1_norm=v_ffn1_norm, ffn1_w_gate_up=v_ffn1_w_gate_up, ffn1_w_down=v_ffn1_w_down,
                  mix_norm=v_mix_norm, w_in=v_w_in, pool_w_group=v_pool_w_group, pool_scale=v_pool_scale,
                  w_branch_pool=v_w_branch_pool, w_branch_attn=v_w_branch_attn, w_out=v_w_out,
                  ffn2_norm=v_ffn2_norm, ffn2_w_gate_up=v_ffn2_w_gate_up, ffn2_w_down=v_ffn2_w_down,
                  final_norm=v_final_norm)
    order = list(weights)

    shards = [
        _pad_gate_up(ffn1_w_gate_up[0]), _pad_down(ffn1_w_down[0]), w_in[0].astype(BF16),
        w_branch_pool[0].astype(BF16), w_branch_attn[0].astype(BF16), w_out[0].astype(BF16),
        _pad_gate_up(ffn2_w_gate_up[0]), _pad_down(ffn2_w_down[0]),
    ]
    wgu1, wd1, win_g, wbp_g, wba_g, wout_g, wgu2, wd2 = _all_gather(shards, name="all_gather_weights")
    wd1 = wd1.reshape(N_DEV * FF_SHARD_PAD, D)
    wd2 = wd2.reshape(N_DEV * FF_SHARD_PAD, D)
    wout_g = wout_g.reshape(D, D)

    norms = (ffn1_norm, mix_norm, ffn2_norm, final_norm.reshape(1, D))
    loss, dx, sharded, replicated = _local_step(
        x[0], loss_target[0], norms, pool_w_group[0], pool_scale, wgu1, wd1, win_g, wbp_g, wba_g, wout_g, wgu2, wd2)
    d_wgu1, d_wd1, d_win, d_wbp, d_wba, d_wout, d_wgu2, d_wd2 = sharded
    d_wd1 = d_wd1.reshape(N_DEV, FF_SHARD_PAD, D)
    d_wd2 = d_wd2.reshape(N_DEV, FF_SHARD_PAD, D)
    d_wout = d_wout.reshape(N_DEV, D // N_DEV, D)

    grads = dict(
        ffn1_w_gate_up=_unpad_gate_up(_reduce_scatter(d_wgu1, name="reduce_scatter_ffn1_gate_up")),
        ffn1_w_down=_reduce_scatter(d_wd1, name="reduce_scatter_ffn1_down")[:FF_SHARD],
        w_in=_reduce_scatter(d_win, name="reduce_scatter_w_in"),
        w_branch_pool=_reduce_scatter(d_wbp, name="reduce_scatter_branch_pool"),
        w_branch_attn=_reduce_scatter(d_wba, name="reduce_scatter_branch_attn"),
        w_out=_reduce_scatter(d_wout, name="reduce_scatter_w_out"),
        ffn2_w_gate_up=_unpad_gate_up(_reduce_scatter(d_wgu2, name="reduce_scatter_ffn2_gate_up")),
        ffn2_w_down=_reduce_scatter(d_wd2, name="reduce_scatter_ffn2_down")[:FF_SHARD],
    )

    d_n1, d_nm, d_n2, d_nf, d_scale, d_wgroup = replicated
    small = ["ffn1_norm", "mix_norm", "ffn2_norm", "final_norm", "pool_scale", "pool_w_group"]
    pieces = [d.reshape(-1, 128) for d in (d_n1, d_nm, d_n2, d_nf, d_scale, d_wgroup)]
    pieces.append(jnp.broadcast_to(loss, (8, 128)))
    rows = [p.shape[0] for p in pieces]
    slab = jnp.concatenate(pieces, axis=0)
    slab = jnp.pad(slab, ((0, -slab.shape[0] % 8), (0, 0)))
    total = _all_reduce_small(slab, name="all_reduce_replicated")
    start = 0
    for name_, n_rows in zip(small, rows):
        grads[name_] = total[start:start + n_rows]
        start += n_rows
    loss_out = total[start, 0]

    def flat(a):
        return a.reshape(-1, 128)

    small_w = jnp.concatenate([flat(weights[k]) for k in small], axis=0)
    small_m = jnp.concatenate([flat(first[k]) for k in small], axis=0)
    small_v = jnp.concatenate([flat(second[k]) for k in small], axis=0)
    small_g = total[:start]
    pad = -start % 8
    padded = [jnp.pad(a, ((0, pad), (0, 0))) for a in (small_w, small_g, small_m, small_v)]
    small_out = _adamw(*padded, name="adamw_replicated")
    delta, new_m, new_v = {}, {}, {}
    start = 0
    for name_, n_rows in zip(small, rows):
        shape = weights[name_].shape
        delta[name_], new_m[name_], new_v[name_] = (a[start:start + n_rows].reshape(shape) for a in small_out)
        grads[name_] = grads[name_].reshape(shape)
        start += n_rows
    for name_ in order:
        if name_ in small:
            continue
        shape = weights[name_].shape
        two_d = shape[1:]
        out = _adamw(weights[name_].reshape(two_d), grads[name_], first[name_].reshape(two_d),
                     second[name_].reshape(two_d), name="adamw_" + name_)
        delta[name_], new_m[name_], new_v[name_] = (a.reshape(shape) for a in out)
        grads[name_] = grads[name_].reshape(shape)

    return (loss_out, dx[None], *[grads[k] for k in order], *[delta[k] for k in order],
            *[new_m[k] for k in order], *[new_v[k] for k in order])
```

```python
import functools

import jax
import jax.numpy as jnp
from jax import lax
from jax.experimental import pallas as pl
from jax.experimental.pallas import tpu as pltpu
from jax.experimental.pallas import tpu_sc as plsc

F32 = jnp.float32
BF16 = jnp.bfloat16
MESH = pl.DeviceIdType.MESH

RMS_EPS = 1e-6
N_DEV = 8
N_HEADS = 8
HEAD_DIM = 64
HEAD_PAIR = 2 * HEAD_DIM
POOL_WINDOWS = (2, 4, 8, 16)
POOL_GROUP = 128
POOL_WIDTH = 512
SB_WIDTH = 512
FF_SHARD = 352
FF_SHARD_PAD = 384
ATTN_BLOCK = 256
ATTN_SCALE = 0.125

ADAM_LR = 0.001
ADAM_B1 = 0.9
ADAM_B2 = 0.999
ADAM_EPS = 1e-08
ADAM_WD = 0.01
ADAM_STEP = 10

VMEM_LIMIT = 48 << 20


def _params(dims=None):
    return pltpu.CompilerParams(dimension_semantics=dims, vmem_limit_bytes=VMEM_LIMIT)


def _mm(a, b):
    return jnp.dot(a, b, preferred_element_type=F32)


def _mm_nt(a, b):
    return lax.dot_general(a, b, (((1,), (1,)), ((), ())), preferred_element_type=F32)


def _mm_tn(a, b):
    return lax.dot_general(a, b, (((0,), (0,)), ((), ())), preferred_element_type=F32)


def _rstd(xf):
    return lax.rsqrt(jnp.mean(xf * xf, axis=-1, keepdims=True) + RMS_EPS)


def _rms_bwd(xf, gain, dn):
    r = _rstd(xf)
    xh = xf * r
    dgain = jnp.sum(dn * xh, axis=0, keepdims=True)
    dxh = dn * gain
    dx = r * (dxh - xh * jnp.mean(dxh * xh, axis=-1, keepdims=True))
    return dx, dgain


def _ffn_fwd(x, gain, wgu, wd, *, tm, name):
    T, D = x.shape
    tm = min(tm, T)
    nb, bw = wgu.shape[0] // 2, wgu.shape[2]

    def body(x_ref, gain_ref, wg_ref, wu_ref, wd_ref, h_ref, gu_ref, n_scr, acc):
        j = pl.program_id(1)

        @pl.when(j == 0)
        def _():
            xf = x_ref[...]
            n_scr[...] = (xf * _rstd(xf) * gain_ref[...]).astype(BF16)
            acc[...] = jnp.zeros_like(acc)

        n = n_scr[...]
        g = _mm(n, wg_ref[...])
        u = _mm(n, wu_ref[...])
        gu_ref[0] = g.astype(BF16)
        gu_ref[1] = u.astype(BF16)
        hid = (g * jax.nn.sigmoid(g) * u).astype(BF16)
        acc[...] += _mm(hid, wd_ref[...])

        @pl.when(j == nb - 1)
        def _():
            h_ref[...] = x_ref[...] + 0.5 * acc[...]

    return pl.pallas_call(
        body, name=name, grid=(T // tm, nb),
        in_specs=[
            pl.BlockSpec((tm, D), lambda i, j: (i, 0)),
            pl.BlockSpec((1, D), lambda i, j: (0, 0)),
            pl.BlockSpec((None, D, bw), lambda i, j: (j, 0, 0)),
            pl.BlockSpec((None, D, bw), lambda i, j: (j + nb, 0, 0)),
            pl.BlockSpec((bw, D), lambda i, j: (j, 0)),
        ],
        out_specs=[
            pl.BlockSpec((tm, D), lambda i, j: (i, 0)),
            pl.BlockSpec((2, tm, bw), lambda i, j: (0, i, j)),
        ],
        out_shape=[jax.ShapeDtypeStruct((T, D), F32), jax.ShapeDtypeStruct((2, T, nb * bw), BF16)],
        scratch_shapes=[pltpu.VMEM((tm, D), BF16), pltpu.VMEM((tm, D), F32)],
        compiler_params=_params(("arbitrary", "arbitrary")),
    )(x, gain, wgu, wgu, wd)


def _ffn_bwd(dh, x, gain, gu, wgu, wd, *, tm, name):
    T, D = x.shape
    tm = min(tm, T)
    nb, bw = wgu.shape[0] // 2, wgu.shape[2]

    def body(dh_ref, x_ref, gain_ref, gu_ref, wg_ref, wu_ref, wd_ref,
             dx_ref, dgain_ref, n_ref, df_ref, dgu_ref, hid_ref, dn_acc):
        i, j = pl.program_id(0), pl.program_id(1)

        @pl.when(j == 0)
        def _():
            xf = x_ref[...]
            n_ref[...] = (xf * _rstd(xf) * gain_ref[...]).astype(BF16)
            df_ref[...] = (0.5 * dh_ref[...]).astype(BF16)
            dn_acc[...] = jnp.zeros_like(dn_acc)

        @pl.when((i == 0) & (j == 0))
        def _():
            dgain_ref[...] = jnp.zeros_like(dgain_ref)

        dhid = _mm_nt(df_ref[...], wd_ref[...])
        g = gu_ref[0].astype(F32)
        u = gu_ref[1].astype(F32)
        s = jax.nn.sigmoid(g)
        silu = g * s
        hid_ref[...] = (silu * u).astype(BF16)
        dg = (dhid * u * (s * (1.0 + g * (1.0 - s)))).astype(BF16)
        du = (dhid * silu).astype(BF16)
        dgu_ref[0] = dg
        dgu_ref[1] = du
        dn_acc[...] += _mm_nt(dg, wg_ref[...]) + _mm_nt(du, wu_ref[...])

        @pl.when(j == nb - 1)
        def _():
            dx, dgain = _rms_bwd(x_ref[...], gain_ref[...], dn_acc[...])
            dx_ref[...] = dh_ref[...] + dx
            dgain_ref[...] += dgain

    row = lambda i, j: (i, 0)
    return pl.pallas_call(
        body, name=name, grid=(T // tm, nb),
        in_specs=[
            pl.BlockSpec((tm, D), row),
            pl.BlockSpec((tm, D), row),
            pl.BlockSpec((1, D), lambda i, j: (0, 0)),
            pl.BlockSpec((2, tm, bw), lambda i, j: (0, i, j)),
            pl.BlockSpec((None, D, bw), lambda i, j: (j, 0, 0)),
            pl.BlockSpec((None, D, bw), lambda i, j: (j + nb, 0, 0)),
            pl.BlockSpec((bw, D), lambda i, j: (j, 0)),
        ],
        out_specs=[
            pl.BlockSpec((tm, D), row),
            pl.BlockSpec((1, D), lambda i, j: (0, 0)),
            pl.BlockSpec((tm, D), row),
            pl.BlockSpec((tm, D), row),
            pl.BlockSpec((2, tm, bw), lambda i, j: (0, i, j)),
            pl.BlockSpec((tm, bw), lambda i, j: (i, j)),
        ],
        out_shape=[
            jax.ShapeDtypeStruct((T, D), F32),
            jax.ShapeDtypeStruct((1, D), F32),
            jax.ShapeDtypeStruct((T, D), BF16),
            jax.ShapeDtypeStruct((T, D), BF16),
            jax.ShapeDtypeStruct((2, T, nb * bw), BF16),
            jax.ShapeDtypeStruct((T, nb * bw), BF16),
        ],
        scratch_shapes=[pltpu.VMEM((tm, D), F32)],
        compiler_params=_params(("arbitrary", "arbitrary")),
    )(dh, x, gain, gu, wgu, wgu, wd)


def _wgrad(a, b, *, grid, a_spec, b_spec, out_spec, out_shape, acc_shape, name, split_lanes=0):
    nk = grid[2]

    def body(a_ref, b_ref, o_ref, acc):
        k = pl.program_id(2)

        @pl.when(k == 0)
        def _():
            acc[...] = jnp.zeros_like(acc)

        acc[...] += _mm_tn(a_ref[...].astype(BF16), b_ref[...].astype(BF16))

        @pl.when(k == nk - 1)
        def _():
            if split_lanes:
                for e in range(o_ref.shape[0]):
                    o_ref[e] = acc[:, e * split_lanes:(e + 1) * split_lanes].astype(o_ref.dtype)
            else:
                o_ref[...] = acc[...].astype(o_ref.dtype)

    return pl.pallas_call(
        body, name=name, grid=grid, in_specs=[a_spec, b_spec], out_specs=out_spec,
        out_shape=jax.ShapeDtypeStruct(out_shape, BF16),
        scratch_shapes=[pltpu.VMEM(acc_shape, F32)],
        compiler_params=_params(("arbitrary", "arbitrary", "arbitrary")),
    )(a, b)


def _wgrad_gate_up(n, dgu, *, tk, name):
    T, D = n.shape
    tk = min(tk, T)
    bw = FF_SHARD_PAD * 2
    nb = dgu.shape[2] // bw
    return _wgrad(
        n, dgu, grid=(1, 2 * nb, T // tk), name=name,
        a_spec=pl.BlockSpec((tk, D), lambda m, c, k: (k, 0)),
        b_spec=pl.BlockSpec((None, tk, bw), lambda m, c, k: (c // nb, k, c % nb)),
        out_spec=pl.BlockSpec((None, D, bw), lambda m, c, k: (c, 0, 0)),
        out_shape=(2 * nb, D, bw), acc_shape=(D, bw))


def _wgrad_down(hid, df, *, tk, name):
    T, D = df.shape
    tk = min(tk, T)
    bw = FF_SHARD_PAD * 2
    nb = hid.shape[1] // bw
    return _wgrad(
        hid, df, grid=(nb, 1, T // tk), name=name,
        a_spec=pl.BlockSpec((tk, bw), lambda m, c, k: (k, m)),
        b_spec=pl.BlockSpec((tk, D), lambda m, c, k: (k, 0)),
        out_spec=pl.BlockSpec((bw, D), lambda m, c, k: (m, 0)),
        out_shape=(nb * bw, D), acc_shape=(bw, D))


def _wgrad_in(un, dproj, *, tk, name):
    T, D = un.shape
    tk = min(tk, T)
    bw = dproj.shape[1] // N_DEV
    return _wgrad(
        un, dproj, grid=(1, N_DEV, T // tk), name=name,
        a_spec=pl.BlockSpec((tk, D), lambda m, c, k: (k, 0)),
        b_spec=pl.BlockSpec((tk, bw), lambda m, c, k: (k, c)),
        out_spec=pl.BlockSpec((None, D, bw), lambda m, c, k: (c, 0, 0)),
        out_shape=(N_DEV, D, bw), acc_shape=(D, bw))


def _wgrad_full(a, b, *, tk, name, split_lanes=0):
    T, M = a.shape
    tk = min(tk, T)
    N = b.shape[1]
    if split_lanes:
        out_shape = (N // split_lanes, M, split_lanes)
        out_spec = pl.BlockSpec(out_shape, lambda m, c, k: (0, 0, 0))
    else:
        out_shape = (M, N)
        out_spec = pl.BlockSpec(out_shape, lambda m, c, k: (0, 0))
    return _wgrad(
        a, b, grid=(1, 1, T // tk), name=name,
        a_spec=pl.BlockSpec((tk, M), lambda m, c, k: (k, 0)),
        b_spec=pl.BlockSpec((tk, N), lambda m, c, k: (k, 0)),
        out_spec=out_spec, out_shape=out_shape, acc_shape=(M, N), split_lanes=split_lanes)


def _loss_bwd(h, target, gain, *, tm, name):
    T, D = h.shape
    tm = min(tm, T)

    def body(h_ref, t_ref, gain_ref, dh_ref, loss_ref, dgain_ref):
        @pl.when(pl.program_id(0) == 0)
        def _():
            loss_ref[...] = jnp.zeros_like(loss_ref)
            dgain_ref[...] = jnp.zeros_like(dgain_ref)

        xf = h_ref[...]
        gain = gain_ref[...]
        err = xf * _rstd(xf) * gain - t_ref[...]
        loss_ref[...] += 0.5 * jnp.sum(jnp.mean(err * err, axis=-1, keepdims=True), axis=0, keepdims=True)
        dx, dgain = _rms_bwd(xf, gain, err * (1.0 / D))
        dh_ref[...] = dx
        dgain_ref[...] += dgain

    row = lambda i: (i, 0)
    fixed = lambda i: (0, 0)
    return pl.pallas_call(
        body, name=name, grid=(T // tm,),
        in_specs=[pl.BlockSpec((tm, D), row), pl.BlockSpec((tm, D), row), pl.BlockSpec((1, D), fixed)],
        out_specs=[pl.BlockSpec((tm, D), row), pl.BlockSpec((1, 128), fixed), pl.BlockSpec((1, D), fixed)],
        out_shape=[jax.ShapeDtypeStruct((T, D), F32), jax.ShapeDtypeStruct((1, 128), F32),
                   jax.ShapeDtypeStruct((1, D), F32)],
        compiler_params=_params(("arbitrary",)),
    )(h, target, gain)


def _inproj_fwd(h, gain, w_in, *, tm, name):
    T, D = h.shape
    tm = min(tm, T)
    nb, bw = w_in.shape[0], w_in.shape[2]

    def body(h_ref, gain_ref, w_ref, un_ref, proj_ref):
        @pl.when(pl.program_id(1) == 0)
        def _():
            xf = h_ref[...]
            un_ref[...] = (xf * _rstd(xf) * gain_ref[...]).astype(BF16)

        proj_ref[...] = _mm(un_ref[...], w_ref[...])

    return pl.pallas_call(
        body, name=name, grid=(T // tm, nb),
        in_specs=[
            pl.BlockSpec((tm, D), lambda i, j: (i, 0)),
            pl.BlockSpec((1, D), lambda i, j: (0, 0)),
            pl.BlockSpec((None, D, bw), lambda i, j: (j, 0, 0)),
        ],
        out_specs=[pl.BlockSpec((tm, D), lambda i, j: (i, 0)), pl.BlockSpec((tm, bw), lambda i, j: (i, j))],
        out_shape=[jax.ShapeDtypeStruct((T, D), BF16), jax.ShapeDtypeStruct((T, nb * bw), F32)],
        compiler_params=_params(("arbitrary", "arbitrary")),
    )(h, gain, w_in)


def _inproj_bwd(dproj, dh, h, gain, w_in, *, tm, name):
    T, D = h.shape
    tm = min(tm, T)
    nb, bw = w_in.shape[0], w_in.shape[2]

    def body(dp_ref, dh_ref, h_ref, gain_ref, w_ref, dx_ref, dgain_ref, acc):
        i, j = pl.program_id(0), pl.program_id(1)

        @pl.when(j == 0)
        def _():
            acc[...] = jnp.zeros_like(acc)

        @pl.when((i == 0) & (j == 0))
        def _():
            dgain_ref[...] = jnp.zeros_like(dgain_ref)

        acc[...] += _mm_nt(dp_ref[...], w_ref[...])

        @pl.when(j == nb - 1)
        def _():
            dx, dgain = _rms_bwd(h_ref[...], gain_ref[...], acc[...])
            dx_ref[...] = dh_ref[...] + dx
            dgain_ref[...] += dgain

    row = lambda i, j: (i, 0)
    return pl.pallas_call(
        body, name=name, grid=(T // tm, nb),
        in_specs=[
            pl.BlockSpec((tm, bw), lambda i, j: (i, j)),
            pl.BlockSpec((tm, D), row),
            pl.BlockSpec((tm, D), row),
            pl.BlockSpec((1, D), lambda i, j: (0, 0)),
            pl.BlockSpec((None, D, bw), lambda i, j: (j, 0, 0)),
        ],
        out_specs=[pl.BlockSpec((tm, D), row), pl.BlockSpec((1, D), lambda i, j: (0, 0))],
        out_shape=[jax.ShapeDtypeStruct((T, D), F32), jax.ShapeDtypeStruct((1, D), F32)],
        scratch_shapes=[pltpu.VMEM((tm, D), F32)],
        compiler_params=_params(("arbitrary", "arbitrary")),
    )(dproj, dh, h, gain, w_in)


def _window_sum(x, row, doublings, *, backward):
    T = x.shape[0]
    s = x
    for k in range(doublings):
        sh = 1 << k
        if backward:
            s = s + jnp.where(row < T - sh, pltpu.roll(s, T - sh, 0), 0.0)
        else:
            s = s + jnp.where(row >= sh, pltpu.roll(s, sh, 0), 0.0)
    return s


def _pool_fwd(proj, w_group, scale, *, name):
    T = proj.shape[0]

    def body(xp_ref, w_ref, scale_ref, p_ref):
        row = lax.broadcasted_iota(jnp.int32, (T, POOL_GROUP), 0)
        for gi, window in enumerate(POOL_WINDOWS):
            cols = slice(gi * POOL_GROUP, (gi + 1) * POOL_GROUP)
            x = xp_ref[:, cols]
            inv_count = 1.0 / jnp.minimum(row + 1, window).astype(F32)
            yc = _window_sum(x, row, gi + 1, backward=False) * inv_count - x
            pre = _mm(yc.astype(BF16), w_ref[gi].astype(BF16))
            p_ref[:, cols] = pre * scale_ref[:, cols]

    return pl.pallas_call(
        body, name=name, grid=(1,),
        in_specs=[
            pl.BlockSpec((T, POOL_WIDTH), lambda i: (0, 0)),
            pl.BlockSpec(w_group.shape, lambda i: (0, 0, 0)),
            pl.BlockSpec((1, POOL_WIDTH), lambda i: (0, 0)),
        ],
        out_specs=pl.BlockSpec((T, POOL_WIDTH), lambda i: (0, 0)),
        out_shape=jax.ShapeDtypeStruct((T, POOL_WIDTH), F32),
        compiler_params=_params(("arbitrary",)),
    )(proj, w_group, scale)


def _pool_bwd(dp, proj, w_group, scale, *, name):
    T = proj.shape[0]

    def body(dp_ref, xp_ref, w_ref, scale_ref, dxp_ref, dw_ref, dscale_ref):
        row = lax.broadcasted_iota(jnp.int32, (T, POOL_GROUP), 0)
        for gi, window in enumerate(POOL_WINDOWS):
            cols = slice(gi * POOL_GROUP, (gi + 1) * POOL_GROUP)
            x = xp_ref[:, cols]
            inv_count = 1.0 / jnp.minimum(row + 1, window).astype(F32)
            yc = (_window_sum(x, row, gi + 1, backward=False) * inv_count - x).astype(BF16)
            w = w_ref[gi].astype(BF16)
            pre = _mm(yc, w)
            dpg = dp_ref[:, cols]
            dscale_ref[:, cols] = jnp.sum(dpg * pre, axis=0, keepdims=True)
            dpre = (dpg * scale_ref[:, cols]).astype(BF16)
            dw_ref[gi] = _mm_tn(yc, dpre)
            dyc = _mm_nt(dpre, w)
            dxp_ref[:, cols] = _window_sum(dyc * inv_count, row, gi + 1, backward=True) - dyc

    return pl.pallas_call(
        body, name=name, grid=(1,),
        in_specs=[
            pl.BlockSpec((T, POOL_WIDTH), lambda i: (0, 0)),
            pl.BlockSpec((T, POOL_WIDTH), lambda i: (0, 0)),
            pl.BlockSpec(w_group.shape, lambda i: (0, 0, 0)),
            pl.BlockSpec((1, POOL_WIDTH), lambda i: (0, 0)),
        ],
        out_specs=[
            pl.BlockSpec((T, POOL_WIDTH), lambda i: (0, 0)),
            pl.BlockSpec(w_group.shape, lambda i: (0, 0, 0)),
            pl.BlockSpec((1, POOL_WIDTH), lambda i: (0, 0)),
        ],
        out_shape=[jax.ShapeDtypeStruct((T, POOL_WIDTH), F32), jax.ShapeDtypeStruct(w_group.shape, F32),
                   jax.ShapeDtypeStruct((1, POOL_WIDTH), F32)],
        compiler_params=_params(("arbitrary",)),
    )(dp, proj, w_group, scale)


def _log_sigmoids(z):
    t = jnp.log(1.0 + jnp.exp(-jnp.abs(z)))
    return jnp.minimum(z, 0.0) - t, -jnp.maximum(z, 0.0) - t


def _tri_sum(x, tri):
    hi = x.astype(BF16)
    lo = (x - hi.astype(F32)).astype(BF16)
    return _mm(hi, tri) + _mm(lo, tri)


def _attn_specs(T, tq):
    q_col = POOL_WIDTH // HEAD_PAIR
    k_col = q_col + SB_WIDTH // HEAD_PAIR
    v_col = k_col + SB_WIDTH // HEAD_PAIR
    return [
        pl.BlockSpec((tq, HEAD_PAIR), lambda p, i: (i, q_col + p)),
        pl.BlockSpec((T, HEAD_PAIR), lambda p, i: (0, k_col + p)),
        pl.BlockSpec((T, HEAD_PAIR), lambda p, i: (0, v_col + p)),
    ]


def _attn_fwd(proj, *, name):
    T = proj.shape[0]
    tq = ATTN_BLOCK

    def body(q_ref, k_ref, v_ref, o_ref, lt_ref, kb_scr, vb_scr):
        qi = pl.program_id(1)

        @pl.when(qi == 0)
        def _():
            kb_scr[...] = k_ref[...].astype(BF16)
            vb_scr[...] = v_ref[...].astype(BF16)

        head0 = lax.broadcasted_iota(jnp.int32, (tq, HEAD_PAIR), 1) < HEAD_DIM
        q = q_ref[...] * ATTN_SCALE
        qs = (jnp.where(head0, q, 0.0).astype(BF16), jnp.where(head0, 0.0, q).astype(BF16))
        r = lax.broadcasted_iota(jnp.int32, (tq, tq), 0)
        c = lax.broadcasted_iota(jnp.int32, (tq, tq), 1)
        later = (r > c).astype(BF16)
        causal = c < r

        def block(kj, carry, valid):
            off = pl.multiple_of(kj * tq, tq)
            kb = kb_scr[pl.ds(off, tq), :]
            vb = vb_scr[pl.ds(off, tq), :]
            out = []
            for h in range(2):
                run, o = carry[2 * h], carry[2 * h + 1]
                z = _mm_nt(qs[h], kb)
                lb, lm = _log_sigmoids(z)
                if valid is not None:
                    lm = jnp.where(valid, lm, 0.0)
                a = jnp.exp(lb + run + _tri_sum(lm, later))
                if valid is not None:
                    a = jnp.where(valid, a, 0.0)
                out += [run + jnp.sum(lm, axis=1, keepdims=True), o + _mm(a.astype(BF16), vb)]
            return tuple(out)

        zero = (jnp.zeros((tq, 1), F32), jnp.zeros((tq, HEAD_PAIR), F32))
        carry = block(qi, zero + zero, causal)
        carry = lax.fori_loop(0, qi, lambda it, cr: block(qi - 1 - it, cr, None), carry)
        o_ref[...] = jnp.where(head0, carry[1], carry[3])
        lt_ref[...] = jnp.where(head0, carry[0], carry[2])

    out_spec = pl.BlockSpec((tq, HEAD_PAIR), lambda p, i: (i, p))
    return pl.pallas_call(
        body, name=name, grid=(N_HEADS // 2, T // tq),
        in_specs=_attn_specs(T, tq), out_specs=[out_spec, out_spec],
        out_shape=[jax.ShapeDtypeStruct((T, SB_WIDTH), F32), jax.ShapeDtypeStruct((T, SB_WIDTH), F32)],
        scratch_shapes=[pltpu.VMEM((T, HEAD_PAIR), BF16), pltpu.VMEM((T, HEAD_PAIR), BF16)],
        compiler_params=_params(("arbitrary", "arbitrary")),
    )(proj, proj, proj)


def _attn_bwd(proj, do, ltot, *, name):
    T = proj.shape[0]
    tq = ATTN_BLOCK

    def body(q_ref, k_ref, v_ref, do_ref, lt_ref, dq_ref, dk_ref, dv_ref, kb_scr, vb_scr):
        qi = pl.program_id(1)

        @pl.when(qi == 0)
        def _():
            kb_scr[...] = k_ref[...].astype(BF16)
            vb_scr[...] = v_ref[...].astype(BF16)
            dk_ref[...] = jnp.zeros_like(dk_ref)
            dv_ref[...] = jnp.zeros_like(dv_ref)

        head0 = lax.broadcasted_iota(jnp.int32, (tq, HEAD_PAIR), 1) < HEAD_DIM
        q, do_, lt = q_ref[...] * ATTN_SCALE, do_ref[...], lt_ref[...]
        qs = (jnp.where(head0, q, 0.0).astype(BF16), jnp.where(head0, 0.0, q).astype(BF16))
        dos = (jnp.where(head0, do_, 0.0).astype(BF16), jnp.where(head0, 0.0, do_).astype(BF16))
        lts = (jnp.max(jnp.where(head0, lt, -jnp.inf), axis=1, keepdims=True),
               jnp.max(jnp.where(head0, -jnp.inf, lt), axis=1, keepdims=True))
        r = lax.broadcasted_iota(jnp.int32, (tq, tq), 0)
        c = lax.broadcasted_iota(jnp.int32, (tq, tq), 1)
        upto = (r <= c).astype(BF16)
        before = (r < c).astype(BF16)
        causal = c < r

        def block(kj, carry, valid):
            off = pl.multiple_of(kj * tq, tq)
            kb = kb_scr[pl.ds(off, tq), :]
            vb = vb_scr[pl.ds(off, tq), :]
            dk_blk = jnp.zeros((tq, HEAD_PAIR), F32)
            dv_blk = jnp.zeros((tq, HEAD_PAIR), F32)
            out = []
            for h in range(2):
                run_lm, run_e, dq = carry[3 * h:3 * h + 3]
                z = _mm_nt(qs[h], kb)
                lb, lm = _log_sigmoids(z)
                if valid is not None:
                    lm = jnp.where(valid, lm, 0.0)
                a = jnp.exp(lb + (lts[h] - run_lm - _tri_sum(lm, upto)))
                if valid is not None:
                    a = jnp.where(valid, a, 0.0)
                e = _mm_nt(dos[h], vb) * a
                beta = jnp.exp(lb)
                dz = e * (1.0 - beta) - (run_e + _tri_sum(e, before)) * beta
                if valid is not None:
                    dz = jnp.where(valid, dz, 0.0)
                dz = dz.astype(BF16)
                dk_blk += _mm_tn(dz, qs[h])
                dv_blk += _mm_tn(a.astype(BF16), dos[h])
                out += [run_lm + jnp.sum(lm, axis=1, keepdims=True), run_e + jnp.sum(e, axis=1, keepdims=True),
                        dq + _mm(dz, kb)]
            dk_ref[pl.ds(off, tq), :] += dk_blk
            dv_ref[pl.ds(off, tq), :] += dv_blk
            return tuple(out)

        zero = (jnp.zeros((tq, 1), F32), jnp.zeros((tq, 1), F32), jnp.zeros((tq, HEAD_PAIR), F32))
        carry = lax.fori_loop(0, qi, lambda kj, cr: block(kj, cr, None), zero + zero)
        carry = block(qi, carry, causal)
        dq_ref[...] = jnp.where(head0, carry[2], carry[5]) * ATTN_SCALE

    blk = pl.BlockSpec((tq, HEAD_PAIR), lambda p, i: (i, p))
    seq = pl.BlockSpec((T, HEAD_PAIR), lambda p, i: (0, p))
    return pl.pallas_call(
        body, name=name, grid=(N_HEADS // 2, T // tq),
        in_specs=_attn_specs(T, tq) + [blk, blk], out_specs=[blk, seq, seq],
        out_shape=[jax.ShapeDtypeStruct((T, SB_WIDTH), F32)] * 3,
        scratch_shapes=[pltpu.VMEM((T, HEAD_PAIR), BF16), pltpu.VMEM((T, HEAD_PAIR), BF16)],
        compiler_params=_params(("arbitrary", "arbitrary")),
    )(proj, proj, proj, do, ltot)


def _branch(act_bf16, w_ref):
    return jnp.concatenate([_mm(act_bf16, w_ref[e]) for e in range(w_ref.shape[0])], axis=1)


def _mix_specs(T, D, tm, wbp, w_out):
    gate_col = (POOL_WIDTH + 3 * SB_WIDTH) // D
    row = lambda i: (i, 0)
    return [
        pl.BlockSpec((tm, D), row),
        pl.BlockSpec((tm, POOL_WIDTH), row),
        pl.BlockSpec((tm, SB_WIDTH), row),
        pl.BlockSpec((tm, D), lambda i: (i, gate_col)),
        pl.BlockSpec((tm, D), lambda i: (i, gate_col + 1)),
        pl.BlockSpec(wbp.shape, lambda i: (0, 0, 0)),
        pl.BlockSpec(wbp.shape, lambda i: (0, 0, 0)),
        pl.BlockSpec(w_out.shape, lambda i: (0, 0)),
    ]


def _mix_fwd(h, p, o, proj, wbp, wba, w_out, *, tm, name):
    T, D = h.shape
    tm = min(tm, T)

    def body(h_ref, p_ref, o_ref, glp_ref, gls_ref, wbp_ref, wba_ref, wout_ref, hout_ref, m_ref):
        yp = _branch(p_ref[...].astype(BF16), wbp_ref)
        ys = _branch(o_ref[...].astype(BF16), wba_ref)
        m = (jax.nn.sigmoid(glp_ref[...]) * yp + jax.nn.sigmoid(gls_ref[...]) * ys).astype(BF16)
        m_ref[...] = m
        hout_ref[...] = h_ref[...] + _mm(m, wout_ref[...])

    row = lambda i: (i, 0)
    return pl.pallas_call(
        body, name=name, grid=(T // tm,),
        in_specs=_mix_specs(T, D, tm, wbp, w_out),
        out_specs=[pl.BlockSpec((tm, D), row), pl.BlockSpec((tm, D), row)],
        out_shape=[jax.ShapeDtypeStruct((T, D), F32), jax.ShapeDtypeStruct((T, D), BF16)],
        compiler_params=_params(("arbitrary",)),
    )(h, p, o, proj, proj, wbp, wba, w_out)


def _mix_bwd(dh, p, o, proj, wbp, wba, w_out, *, tm, name):
    T, D = dh.shape
    tm = min(tm, T)
    bw = wbp.shape[2]

    def body(dh_ref, p_ref, o_ref, glp_ref, gls_ref, wbp_ref, wba_ref, wout_ref,
             dyp_ref, dys_ref, dp_ref, do_ref, dgl_ref):
        dm = _mm_nt(dh_ref[...].astype(BF16), wout_ref[...])
        yp = _branch(p_ref[...].astype(BF16), wbp_ref)
        ys = _branch(o_ref[...].astype(BF16), wba_ref)
        gp = jax.nn.sigmoid(glp_ref[...])
        gs = jax.nn.sigmoid(gls_ref[...])
        dyp = (dm * gp).astype(BF16)
        dys = (dm * gs).astype(BF16)
        dyp_ref[...] = dyp
        dys_ref[...] = dys
        dgl_ref[:, :D] = (dm * yp * gp * (1.0 - gp)).astype(BF16)
        dgl_ref[:, D:] = (dm * ys * gs * (1.0 - gs)).astype(BF16)
        dp = jnp.zeros(dp_ref.shape, F32)
        do_ = jnp.zeros(do_ref.shape, F32)
        for e in range(wbp_ref.shape[0]):
            dp += _mm_nt(dyp[:, e * bw:(e + 1) * bw], wbp_ref[e])
            do_ += _mm_nt(dys[:, e * bw:(e + 1) * bw], wba_ref[e])
        dp_ref[...] = dp
        do_ref[...] = do_

    row = lambda i: (i, 0)
    return pl.pallas_call(
        body, name=name, grid=(T // tm,),
        in_specs=_mix_specs(T, D, tm, wbp, w_out),
        out_specs=[pl.BlockSpec((tm, D), row), pl.BlockSpec((tm, D), row), pl.BlockSpec((tm, POOL_WIDTH), row),
                   pl.BlockSpec((tm, SB_WIDTH), row), pl.BlockSpec((tm, 2 * D), row)],
        out_shape=[jax.ShapeDtypeStruct((T, D), BF16), jax.ShapeDtypeStruct((T, D), BF16),
                   jax.ShapeDtypeStruct((T, POOL_WIDTH), F32), jax.ShapeDtypeStruct((T, SB_WIDTH), F32),
                   jax.ShapeDtypeStruct((T, 2 * D), BF16)],
        compiler_params=_params(("arbitrary",)),
    )(dh, p, o, proj, proj, wbp, wba, w_out)


def _adamw(w, g, m, v, *, name):
    R, C = w.shape
    tr = R if R * C <= 512 * 1024 else 256

    def body(w_ref, g_ref, m_ref, v_ref, d_ref, nm_ref, nv_ref):
        g_ = g_ref[...]
        m_ = ADAM_B1 * m_ref[...] + (1.0 - ADAM_B1) * g_
        v_ = ADAM_B2 * v_ref[...] + (1.0 - ADAM_B2) * (g_ * g_)
        m_hat = m_ / (1.0 - ADAM_B1 ** ADAM_STEP)
        v_hat = v_ / (1.0 - ADAM_B2 ** ADAM_STEP)
        d_ref[...] = -ADAM_LR * (m_hat / (jnp.sqrt(v_hat) + ADAM_EPS) + ADAM_WD * w_ref[...])
        nm_ref[...] = m_
        nv_ref[...] = v_

    spec = pl.BlockSpec((tr, C), lambda i: (i, 0))
    return pl.pallas_call(
        body, name=name, grid=(R // tr,), in_specs=[spec] * 4, out_specs=[spec] * 3,
        out_shape=[jax.ShapeDtypeStruct((R, C), F32)] * 3,
        compiler_params=_params(("arbitrary",)),
    )(w, g, m, v)


def _position():
    return lax.axis_index("x"), lax.axis_index("y"), lax.axis_index("c")


def _all_gather(shards, *, name, collective_id):
    n = len(shards)

    def body(*refs):
        ins, outs = refs[:n], refs[n:2 * n]
        send_sems, recv_sems, local_sems = refs[2 * n:]
        x, y, c = _position()
        me, sibling = (x, y, c), (x, y, 1 - c)
        chips = [(1 - x, y), (x, 1 - y), (1 - x, 1 - y)]

        barrier = pltpu.get_barrier_semaphore()
        for peer in [sibling] + [(*chip, c) for chip in chips]:
            pl.semaphore_signal(barrier, inc=1, device_id=peer, device_id_type=MESH)
        pl.semaphore_wait(barrier, 4)

        def block(a, pos):
            return outs[a].at[4 * pos[0] + 2 * pos[1] + pos[2]]

        def copy(a, k, pos, to, src=None):
            return pltpu.make_async_remote_copy(
                src_ref=block(a, pos) if src is None else src, dst_ref=block(a, pos),
                send_sem=send_sems.at[7 * a + k], recv_sem=recv_sems.at[7 * a + k],
                device_id=to, device_id_type=MESH)

        started = []
        for a in range(n):
            mine = pltpu.make_async_copy(ins[a], block(a, me), local_sems.at[a])
            mine.start()
            started.append(mine)
        sends = []
        for a in range(n):
            sends += [copy(a, 1 + j, me, (*chip, c), src=ins[a]) for j, chip in enumerate(chips)]
            sends.append(copy(a, 0, me, sibling, src=ins[a]))
        for cp in sends:
            cp.start()
        for j, chip in enumerate(chips):
            for a in range(n):
                copy(a, 1 + j, (*chip, c), me).wait_recv()
                passed = copy(a, 4 + j, (*chip, c), sibling)
                passed.start()
                sends.append(passed)
        for a in range(n):
            copy(a, 0, sibling, me).wait_recv()
            for j, chip in enumerate(chips):
                copy(a, 4 + j, (*chip, 1 - c), me).wait_recv()
        for cp in sends:
            cp.wait_send()
        for cp in started:
            cp.wait()

    return pl.kernel(
        body, name=name,
        out_type=[jax.ShapeDtypeStruct((N_DEV,) + s.shape, s.dtype) for s in shards],
        mesh=plsc.ScalarSubcoreMesh(axis_name="sequencer", num_cores=1),
        scratch_types=[pltpu.SemaphoreType.DMA((7 * n,)), pltpu.SemaphoreType.DMA((7 * n,)),
                       pltpu.SemaphoreType.DMA((n,))],
        compiler_params=pltpu.CompilerParams(collective_id=collective_id),
    )(*shards)


def _reduce_scatter(grads, *, name):
    _, R, C = grads.shape
    rc = 128 if R % 128 == 0 else R

    def body(g_ref, out_ref, mine, theirs, partial, landed, send_sems, recv_sems, local_sems):
        x, y, c = _position()
        my_chip = 2 * x + y

        def swap(s):
            return pltpu.make_async_remote_copy(
                src_ref=g_ref.at[2 * s + (1 - c)], dst_ref=theirs.at[s],
                send_sem=send_sems.at[s], recv_sem=recv_sems.at[s],
                device_id=(x, y, 1 - c), device_id_type=MESH)

        def load(s):
            return pltpu.make_async_copy(g_ref.at[2 * s + c], mine.at[s], local_sems.at[s])

        def cross(j):
            chip = my_chip ^ j
            return pltpu.make_async_remote_copy(
                src_ref=partial.at[j - 1], dst_ref=landed.at[j - 1],
                send_sem=send_sems.at[3 + j], recv_sem=recv_sems.at[3 + j],
                device_id=(chip // 2, chip % 2, c), device_id_type=MESH)

        for s in range(4):
            swap(s).start()
            load(s).start()
        for s in range(4):
            load(s).wait()
            swap(s).wait_recv()

        def chip_sum(chip, rows):
            return mine[chip, rows, :].astype(F32) + theirs[chip, rows, :].astype(F32)

        for j in (1, 2, 3):
            @pl.loop(0, R // rc)
            def _(t):
                rows = pl.ds(pl.multiple_of(t * rc, rc), rc)
                partial[j - 1, rows, :] = chip_sum(my_chip ^ j, rows).astype(BF16)
            cross(j).start()

        @pl.loop(0, R // rc)
        def _(t):
            rows = pl.ds(pl.multiple_of(t * rc, rc), rc)
            out_ref[rows, :] = chip_sum(my_chip, rows)

        for j in (1, 2, 3):
            cross(j).wait_recv()

            @pl.loop(0, R // rc)
            def _(t):
                rows = pl.ds(pl.multiple_of(t * rc, rc), rc)
                out_ref[rows, :] += landed[j - 1, rows, :].astype(F32)

        for s in range(4):
            swap(s).wait_send()
        for j in (1, 2, 3):
            cross(j).wait_send()

    return pl.pallas_call(
        body, name=name,
        in_specs=[pl.BlockSpec(memory_space=pl.ANY)],
        out_specs=pl.BlockSpec(memory_space=pltpu.VMEM),
        out_shape=jax.ShapeDtypeStruct((R, C), F32),
        scratch_shapes=[
            pltpu.VMEM((4, R, C), BF16), pltpu.VMEM((4, R, C), BF16),
            pltpu.VMEM((3, R, C), BF16), pltpu.VMEM((3, R, C), BF16),
            pltpu.SemaphoreType.DMA((7,)), pltpu.SemaphoreType.DMA((7,)), pltpu.SemaphoreType.DMA((4,)),
        ],
        compiler_params=_params(),
    )(grads)


def _all_reduce_small(slab, *, name):
    R, C = slab.shape

    def body(in_ref, out_ref, gathered, send_sems, recv_sems):
        x, y, c = _position()
        me = 4 * x + 2 * y + c

        def copy(k):
            peer = me ^ k
            return pltpu.make_async_remote_copy(
                src_ref=in_ref, dst_ref=gathered.at[me],
                send_sem=send_sems.at[k - 1], recv_sem=recv_sems.at[k - 1],
                device_id=(peer // 4, (peer // 2) % 2, peer % 2), device_id_type=MESH)

        def arrival(k):
            return pltpu.make_async_remote_copy(
                src_ref=in_ref, dst_ref=gathered.at[me ^ k],
                send_sem=send_sems.at[k - 1], recv_sem=recv_sems.at[k - 1],
                device_id=(x, y, c), device_id_type=MESH)

        for k in range(1, N_DEV):
            copy(k).start()
        gathered[me] = in_ref[...]
        for k in range(1, N_DEV):
            arrival(k).wait_recv()
        total = gathered[0]
        for d in range(1, N_DEV):
            total = total + gathered[d]
        out_ref[...] = total
        for k in range(1, N_DEV):
            copy(k).wait_send()

    return pl.pallas_call(
        body, name=name,
        in_specs=[pl.BlockSpec(memory_space=pltpu.VMEM)],
        out_specs=pl.BlockSpec(memory_space=pltpu.VMEM),
        out_shape=jax.ShapeDtypeStruct((R, C), F32),
        scratch_shapes=[pltpu.VMEM((N_DEV, R, C), F32),
                        pltpu.SemaphoreType.DMA((N_DEV - 1,)), pltpu.SemaphoreType.DMA((N_DEV - 1,))],
        compiler_params=_params(),
    )(slab)


def _local_step(x, target, norms, pool_w_group, pool_scale, wgu1, wd1, w_in, wbp, wba, w_out, wgu2, wd2):
    n1g, nmg, n2g, nfg = norms
    h1, gu1 = _ffn_fwd(x, n1g, wgu1, wd1, tm=512, name="ffn1_fwd")
    un, proj = _inproj_fwd(h1, nmg, w_in, tm=512, name="inproj_fwd")
    p = _pool_fwd(proj, pool_w_group, pool_scale, name="pool_fwd")
    o, ltot = _attn_fwd(proj, name="attn_fwd")
    h2, m = _mix_fwd(h1, p, o, proj, wbp, wba, w_out, tm=256, name="mix_fwd")
    h3, gu2 = _ffn_fwd(h2, n2g, wgu2, wd2, tm=512, name="ffn2_fwd")
    dh3, loss, d_nf = _loss_bwd(h3, target, nfg, tm=256, name="loss_bwd")

    dh2, d_n2, n2, df2, dgu2, hid2 = _ffn_bwd(dh3, h2, n2g, gu2, wgu2, wd2, tm=256, name="ffn2_bwd")
    d_wgu2 = _wgrad_gate_up(n2, dgu2, tk=512, name="ffn2_wgrad_gate_up")
    d_wd2 = _wgrad_down(hid2, df2, tk=512, name="ffn2_wgrad_down")

    dyp, dys, dp, do, dgl = _mix_bwd(dh2, p, o, proj, wbp, wba, w_out, tm=256, name="mix_bwd")
    d_wout = _wgrad_full(m, dh2, tk=512, name="wgrad_out")
    d_wbp = _wgrad_full(p, dyp, tk=512, name="wgrad_branch_pool", split_lanes=wbp.shape[2])
    d_wba = _wgrad_full(o, dys, tk=512, name="wgrad_branch_attn", split_lanes=wba.shape[2])
    dxp, d_wgroup, d_scale = _pool_bwd(dp, proj, pool_w_group, pool_scale, name="pool_bwd")
    dq, dk, dv = _attn_bwd(proj, do, ltot, name="attn_bwd")
    dproj = jnp.concatenate([dxp.astype(BF16), dq.astype(BF16), dk.astype(BF16), dv.astype(BF16), dgl], axis=1)
    dh1, d_nm = _inproj_bwd(dproj, dh2, h1, nmg, w_in, tm=256, name="inproj_bwd")
    d_win = _wgrad_in(un, dproj, tk=512, name="wgrad_in")

    dx, d_n1, n1, df1, dgu1, hid1 = _ffn_bwd(dh1, x, n1g, gu1, wgu1, wd1, tm=256, name="ffn1_bwd")
    d_wgu1 = _wgrad_gate_up(n1, dgu1, tk=512, name="ffn1_wgrad_gate_up")
    d_wd1 = _wgrad_down(hid1, df1, tk=512, name="ffn1_wgrad_down")

    sharded = (d_wgu1, d_wd1, d_win, d_wbp, d_wba, d_wout, d_wgu2, d_wd2)
    replicated = (d_n1, d_nm, d_n2, d_nf, d_scale, d_wgroup)
    return loss, dx, sharded, replicated


def _pad_gate_up(w):
    d = w.shape[0]
    w = w.astype(BF16).reshape(d, 2, FF_SHARD)
    return jnp.pad(w, ((0, 0), (0, 0), (0, FF_SHARD_PAD - FF_SHARD))).reshape(d, 2 * FF_SHARD_PAD)


def _unpad_gate_up(g):
    d = g.shape[0]
    return g.reshape(d, 2, FF_SHARD_PAD)[:, :, :FF_SHARD].reshape(d, 2 * FF_SHARD)


def _pad_down(w):
    return jnp.pad(w.astype(BF16), ((0, FF_SHARD_PAD - FF_SHARD), (0, 0)))


def kernel(x, ffn1_norm, ffn1_w_gate_up, ffn1_w_down, mix_norm, w_in, pool_w_group, pool_scale, w_branch_pool, w_branch_attn, w_out, ffn2_norm, ffn2_w_gate_up, ffn2_w_down, final_norm, loss_target, m_ffn1_norm, m_ffn1_w_gate_up, m_ffn1_w_down, m_mix_norm, m_w_in, m_pool_w_group, m_pool_scale, m_w_branch_pool, m_w_branch_attn, m_w_out, m_ffn2_norm, m_ffn2_w_gate_up, m_ffn2_w_down, m_final_norm, v_ffn1_norm, v_ffn1_w_gate_up, v_ffn1_w_down, v_mix_norm, v_w_in, v_pool_w_group, v_pool_scale, v_w_branch_pool, v_w_branch_attn, v_w_out, v_ffn2_norm, v_ffn2_w_gate_up, v_ffn2_w_down, v_final_norm):
    D = x.shape[-1]
    weights = dict(ffn1_norm=ffn1_norm, ffn1_w_gate_up=ffn1_w_gate_up, ffn1_w_down=ffn1_w_down, mix_norm=mix_norm,
                   w_in=w_in, pool_w_group=pool_w_group, pool_scale=pool_scale, w_branch_pool=w_branch_pool,
                   w_branch_attn=w_branch_attn, w_out=w_out, ffn2_norm=ffn2_norm, ffn2_w_gate_up=ffn2_w_gate_up,
                   ffn2_w_down=ffn2_w_down, final_norm=final_norm)
    first = dict(ffn1_norm=m_ffn1_norm, ffn1_w_gate_up=m_ffn1_w_gate_up, ffn1_w_down=m_ffn1_w_down,
                 mix_norm=m_mix_norm, w_in=m_w_in, pool_w_group=m_pool_w_group, pool_scale=m_pool_scale,
                 w_branch_pool=m_w_branch_pool, w_branch_attn=m_w_branch_attn, w_out=m_w_out,
                 ffn2_norm=m_ffn2_norm, ffn2_w_gate_up=m_ffn2_w_gate_up, ffn2_w_down=m_ffn2_w_down,
                 final_norm=m_final_norm)
    second = dict(ffn1_norm=v_ffn1_norm, ffn1_w_gate_up=v_ffn1_w_gate_up, ffn1_w_down=v_ffn1_w_down,
                  mix_norm=v_mix_norm, w_in=v_w_in, pool_w_group=v_pool_w_group, pool_scale=v_pool_scale,
                  w_branch_pool=v_w_branch_pool, w_branch_attn=v_w_branch_attn, w_out=v_w_out,
                  ffn2_norm=v_ffn2_norm, ffn2_w_gate_up=v_ffn2_w_gate_up, ffn2_w_down=v_ffn2_w_down,
                  final_norm=v_final_norm)
    order = list(weights)

    wgu1, wd1 = _all_gather([_pad_gate_up(ffn1_w_gate_up[0]), _pad_down(ffn1_w_down[0])],
                            name="all_gather_ffn1", collective_id=0)
    win_g, = _all_gather([w_in[0].astype(BF16)], name="all_gather_w_in", collective_id=1)
    wbp_g, wba_g, wout_g = _all_gather(
        [w_branch_pool[0].astype(BF16), w_branch_attn[0].astype(BF16), w_out[0].astype(BF16)],
        name="all_gather_mix", collective_id=2)
    wgu2, wd2 = _all_gather([_pad_gate_up(ffn2_w_gate_up[0]), _pad_down(ffn2_w_down[0])],
                            name="all_gather_ffn2", collective_id=3)
    wd1 = wd1.reshape(N_DEV * FF_SHARD_PAD, D)
    wd2 = wd2.reshape(N_DEV * FF_SHARD_PAD, D)
    wout_g = wout_g.reshape(D, D)

    norms = (ffn1_norm, mix_norm, ffn2_norm, final_norm.reshape(1, D))
    loss, dx, sharded, replicated = _local_step(
        x[0], loss_target[0], norms, pool_w_group[0], pool_scale, wgu1, wd1, win_g, wbp_g, wba_g, wout_g, wgu2, wd2)
    d_wgu1, d_wd1, d_win, d_wbp, d_wba, d_wout, d_wgu2, d_wd2 = sharded
    d_wd1 = d_wd1.reshape(N_DEV, FF_SHARD_PAD, D)
    d_wd2 = d_wd2.reshape(N_DEV, FF_SHARD_PAD, D)
    d_wout = d_wout.reshape(N_DEV, D // N_DEV, D)

    grads = dict(
        ffn1_w_gate_up=_unpad_gate_up(_reduce_scatter(d_wgu1, name="reduce_scatter_ffn1_gate_up")),
        ffn1_w_down=_reduce_scatter(d_wd1, name="reduce_scatter_ffn1_down")[:FF_SHARD],
        w_in=_reduce_scatter(d_win, name="reduce_scatter_w_in"),
        w_branch_pool=_reduce_scatter(d_wbp, name="reduce_scatter_branch_pool"),
        w_branch_attn=_reduce_scatter(d_wba, name="reduce_scatter_branch_attn"),
        w_out=_reduce_scatter(d_wout, name="reduce_scatter_w_out"),
        ffn2_w_gate_up=_unpad_gate_up(_reduce_scatter(d_wgu2, name="reduce_scatter_ffn2_gate_up")),
        ffn2_w_down=_reduce_scatter(d_wd2, name="reduce_scatter_ffn2_down")[:FF_SHARD],
    )

    d_n1, d_nm, d_n2, d_nf, d_scale, d_wgroup = replicated
    small = ["ffn1_norm", "mix_norm", "ffn2_norm", "final_norm", "pool_scale", "pool_w_group"]
    def tile_rows(a):
        a = a.reshape(-1, 128)
        return jnp.pad(a, ((0, -a.shape[0] % 8), (0, 0)))

    pieces = [tile_rows(d) for d in (d_n1, d_nm, d_n2, d_nf, d_scale, d_wgroup)]
    rows = [weights[k].size // 128 for k in small]
    starts = [sum(p.shape[0] for p in pieces[:i]) for i in range(len(pieces) + 1)]
    slab = jnp.concatenate(pieces + [jnp.broadcast_to(loss, (8, 128))], axis=0)
    total = _all_reduce_small(slab, name="all_reduce_replicated")
    loss_out = total[starts[-1], 0]

    small_w = jnp.concatenate([tile_rows(weights[k]) for k in small], axis=0)
    small_m = jnp.concatenate([tile_rows(first[k]) for k in small], axis=0)
    small_v = jnp.concatenate([tile_rows(second[k]) for k in small], axis=0)
    small_out = _adamw(small_w, total[:starts[-1]], small_m, small_v, name="adamw_replicated")
    delta, new_m, new_v = {}, {}, {}
    for name_, start, n_rows in zip(small, starts, rows):
        shape = weights[name_].shape
        grads[name_] = total[start:start + n_rows].reshape(shape)
        delta[name_], new_m[name_], new_v[name_] = (a[start:start + n_rows].reshape(shape) for a in small_out)
    for name_ in order:
        if name_ in small:
            continue
        shape = weights[name_].shape
        two_d = shape[1:]
        out = _adamw(weights[name_].reshape(two_d), grads[name_], first[name_].reshape(two_d),
                     second[name_].reshape(two_d), name="adamw_" + name_)
        delta[name_], new_m[name_], new_v[name_] = (a.reshape(shape) for a in out)
        grads[name_] = grads[name_].reshape(shape)

    return (loss_out, dx[None], *[grads[k] for k in order], *[delta[k] for k in order],
            *[new_m[k] for k in order], *[new_v[k] for k in order])
```

```python
import functools

import jax
import jax.numpy as jnp
from jax import lax
from jax.experimental import pallas as pl
from jax.experimental.pallas import tpu as pltpu
from jax.experimental.pallas import tpu_sc as plsc

F32 = jnp.float32
BF16 = jnp.bfloat16
MESH = pl.DeviceIdType.MESH

RMS_EPS = 1e-6
N_DEV = 8
N_HEADS = 8
HEAD_DIM = 64
HEAD_PAIR = 2 * HEAD_DIM
POOL_WINDOWS = (2, 4, 8, 16)
POOL_GROUP = 128
POOL_WIDTH = 512
SB_WIDTH = 512
FF_SHARD = 352
FF_SHARD_PAD = 384
ATTN_BLOCK = 256
ATTN_SCALE = 0.125

ADAM_LR = 0.001
ADAM_B1 = 0.9
ADAM_B2 = 0.999
ADAM_EPS = 1e-08
ADAM_WD = 0.01
ADAM_STEP = 10

VMEM_LIMIT = 48 << 20


def _params(dims=None):
    return pltpu.CompilerParams(dimension_semantics=dims, vmem_limit_bytes=VMEM_LIMIT)


def _mm(a, b):
    return jnp.dot(a, b, preferred_element_type=F32)


def _mm_nt(a, b):
    return lax.dot_general(a, b, (((1,), (1,)), ((), ())), preferred_element_type=F32)


def _mm_tn(a, b):
    return lax.dot_general(a, b, (((0,), (0,)), ((), ())), preferred_element_type=F32)


def _rstd(xf):
    return lax.rsqrt(jnp.mean(xf * xf, axis=-1, keepdims=True) + RMS_EPS)


def _rms_bwd(xf, gain, dn):
    r = _rstd(xf)
    xh = xf * r
    dgain = jnp.sum(dn * xh, axis=0, keepdims=True)
    dxh = dn * gain
    dx = r * (dxh - xh * jnp.mean(dxh * xh, axis=-1, keepdims=True))
    return dx, dgain


def _ffn_fwd(x, gain, wgu, wd, *, tm, name):
    T, D = x.shape
    tm = min(tm, T)
    nb, bw = wgu.shape[0] // 2, wgu.shape[2]

    def body(x_ref, gain_ref, wg_ref, wu_ref, wd_ref, h_ref, gu_ref, n_scr, acc):
        j = pl.program_id(1)

        @pl.when(j == 0)
        def _():
            xf = x_ref[...]
            n_scr[...] = (xf * _rstd(xf) * gain_ref[...]).astype(BF16)
            acc[...] = jnp.zeros_like(acc)

        n = n_scr[...]
        g = _mm(n, wg_ref[...])
        u = _mm(n, wu_ref[...])
        gu_ref[0] = g.astype(BF16)
        gu_ref[1] = u.astype(BF16)
        hid = (g * jax.nn.sigmoid(g) * u).astype(BF16)
        acc[...] += _mm(hid, wd_ref[...])

        @pl.when(j == nb - 1)
        def _():
            h_ref[...] = x_ref[...] + 0.5 * acc[...]

    return pl.pallas_call(
        body, name=name, grid=(T // tm, nb),
        in_specs=[
            pl.BlockSpec((tm, D), lambda i, j: (i, 0)),
            pl.BlockSpec((1, D), lambda i, j: (0, 0)),
            pl.BlockSpec((None, D, bw), lambda i, j: (j, 0, 0)),
            pl.BlockSpec((None, D, bw), lambda i, j: (j + nb, 0, 0)),
            pl.BlockSpec((bw, D), lambda i, j: (j, 0)),
        ],
        out_specs=[
            pl.BlockSpec((tm, D), lambda i, j: (i, 0)),
            pl.BlockSpec((2, tm, bw), lambda i, j: (0, i, j)),
        ],
        out_shape=[jax.ShapeDtypeStruct((T, D), F32), jax.ShapeDtypeStruct((2, T, nb * bw), BF16)],
        scratch_shapes=[pltpu.VMEM((tm, D), BF16), pltpu.VMEM((tm, D), F32)],
        compiler_params=_params(("arbitrary", "arbitrary")),
    )(x, gain, wgu, wgu, wd)


def _ffn_bwd(dh, x, gain, gu, wgu, wd, *, tm, name):
    T, D = x.shape
    tm = min(tm, T)
    nb, bw = wgu.shape[0] // 2, wgu.shape[2]

    def body(dh_ref, x_ref, gain_ref, gu_ref, wg_ref, wu_ref, wd_ref,
             dx_ref, dgain_ref, n_ref, df_ref, dgu_ref, hid_ref, dn_acc):
        i, j = pl.program_id(0), pl.program_id(1)

        @pl.when(j == 0)
        def _():
            xf = x_ref[...]
            n_ref[...] = (xf * _rstd(xf) * gain_ref[...]).astype(BF16)
            df_ref[...] = (0.5 * dh_ref[...]).astype(BF16)
            dn_acc[...] = jnp.zeros_like(dn_acc)

        @pl.when((i == 0) & (j == 0))
        def _():
            dgain_ref[...] = jnp.zeros_like(dgain_ref)

        dhid = _mm_nt(df_ref[...], wd_ref[...])
        g = gu_ref[0].astype(F32)
        u = gu_ref[1].astype(F32)
        s = jax.nn.sigmoid(g)
        silu = g * s
        hid_ref[...] = (silu * u).astype(BF16)
        dg = (dhid * u * (s * (1.0 + g * (1.0 - s)))).astype(BF16)
        du = (dhid * silu).astype(BF16)
        dgu_ref[0] = dg
        dgu_ref[1] = du
        dn_acc[...] += _mm_nt(dg, wg_ref[...]) + _mm_nt(du, wu_ref[...])

        @pl.when(j == nb - 1)
        def _():
            dx, dgain = _rms_bwd(x_ref[...], gain_ref[...], dn_acc[...])
            dx_ref[...] = dh_ref[...] + dx
            dgain_ref[...] += dgain

    row = lambda i, j: (i, 0)
    return pl.pallas_call(
        body, name=name, grid=(T // tm, nb),
        in_specs=[
            pl.BlockSpec((tm, D), row),
            pl.BlockSpec((tm, D), row),
            pl.BlockSpec((1, D), lambda i, j: (0, 0)),
            pl.BlockSpec((2, tm, bw), lambda i, j: (0, i, j)),
            pl.BlockSpec((None, D, bw), lambda i, j: (j, 0, 0)),
            pl.BlockSpec((None, D, bw), lambda i, j: (j + nb, 0, 0)),
            pl.BlockSpec((bw, D), lambda i, j: (j, 0)),
        ],
        out_specs=[
            pl.BlockSpec((tm, D), row),
            pl.BlockSpec((1, D), lambda i, j: (0, 0)),
            pl.BlockSpec((tm, D), row),
            pl.BlockSpec((tm, D), row),
            pl.BlockSpec((2, tm, bw), lambda i, j: (0, i, j)),
            pl.BlockSpec((tm, bw), lambda i, j: (i, j)),
        ],
        out_shape=[
            jax.ShapeDtypeStruct((T, D), F32),
            jax.ShapeDtypeStruct((1, D), F32),
            jax.ShapeDtypeStruct((T, D), BF16),
            jax.ShapeDtypeStruct((T, D), BF16),
            jax.ShapeDtypeStruct((2, T, nb * bw), BF16),
            jax.ShapeDtypeStruct((T, nb * bw), BF16),
        ],
        scratch_shapes=[pltpu.VMEM((tm, D), F32)],
        compiler_params=_params(("arbitrary", "arbitrary")),
    )(dh, x, gain, gu, wgu, wgu, wd)


def _wgrad(a, b, *, grid, a_spec, b_spec, out_spec, out_shape, acc_shape, name, split_lanes=0):
    nk = grid[2]

    def body(a_ref, b_ref, o_ref, acc):
        k = pl.program_id(2)

        @pl.when(k == 0)
        def _():
            acc[...] = jnp.zeros_like(acc)

        acc[...] += _mm_tn(a_ref[...].astype(BF16), b_ref[...].astype(BF16))

        @pl.when(k == nk - 1)
        def _():
            if split_lanes:
                for e in range(o_ref.shape[0]):
                    o_ref[e] = acc[:, e * split_lanes:(e + 1) * split_lanes].astype(o_ref.dtype)
            else:
                o_ref[...] = acc[...].astype(o_ref.dtype)

    return pl.pallas_call(
        body, name=name, grid=grid, in_specs=[a_spec, b_spec], out_specs=out_spec,
        out_shape=jax.ShapeDtypeStruct(out_shape, BF16),
        scratch_shapes=[pltpu.VMEM(acc_shape, F32)],
        compiler_params=_params(("arbitrary", "arbitrary", "arbitrary")),
    )(a, b)


def _wgrad_gate_up(n, dgu, *, tk, name):
    T, D = n.shape
    tk = min(tk, T)
    bw = FF_SHARD_PAD * 2
    nb = dgu.shape[2] // bw
    return _wgrad(
        n, dgu, grid=(1, 2 * nb, T // tk), name=name,
        a_spec=pl.BlockSpec((tk, D), lambda m, c, k: (k, 0)),
        b_spec=pl.BlockSpec((None, tk, bw), lambda m, c, k: (c // nb, k, c % nb)),
        out_spec=pl.BlockSpec((None, D, bw), lambda m, c, k: (c, 0, 0)),
        out_shape=(2 * nb, D, bw), acc_shape=(D, bw))


def _wgrad_down(hid, df, *, tk, name):
    T, D = df.shape
    tk = min(tk, T)
    bw = FF_SHARD_PAD * 2
    nb = hid.shape[1] // bw
    return _wgrad(
        hid, df, grid=(nb, 1, T // tk), name=name,
        a_spec=pl.BlockSpec((tk, bw), lambda m, c, k: (k, m)),
        b_spec=pl.BlockSpec((tk, D), lambda m, c, k: (k, 0)),
        out_spec=pl.BlockSpec((bw, D), lambda m, c, k: (m, 0)),
        out_shape=(nb * bw, D), acc_shape=(bw, D))


def _wgrad_in(un, dproj, *, tk, name):
    T, D = un.shape
    tk = min(tk, T)
    bw = dproj.shape[1] // N_DEV
    return _wgrad(
        un, dproj, grid=(1, N_DEV, T // tk), name=name,
        a_spec=pl.BlockSpec((tk, D), lambda m, c, k: (k, 0)),
        b_spec=pl.BlockSpec((tk, bw), lambda m, c, k: (k, c)),
        out_spec=pl.BlockSpec((None, D, bw), lambda m, c, k: (c, 0, 0)),
        out_shape=(N_DEV, D, bw), acc_shape=(D, bw))


def _wgrad_full(a, b, *, tk, name, split_lanes=0):
    T, M = a.shape
    tk = min(tk, T)
    N = b.shape[1]
    if split_lanes:
        out_shape = (N // split_lanes, M, split_lanes)
        out_spec = pl.BlockSpec(out_shape, lambda m, c, k: (0, 0, 0))
    else:
        out_shape = (M, N)
        out_spec = pl.BlockSpec(out_shape, lambda m, c, k: (0, 0))
    return _wgrad(
        a, b, grid=(1, 1, T // tk), name=name,
        a_spec=pl.BlockSpec((tk, M), lambda m, c, k: (k, 0)),
        b_spec=pl.BlockSpec((tk, N), lambda m, c, k: (k, 0)),
        out_spec=out_spec, out_shape=out_shape, acc_shape=(M, N), split_lanes=split_lanes)


def _loss_bwd(h, target, gain, *, tm, name):
    T, D = h.shape
    tm = min(tm, T)

    def body(h_ref, t_ref, gain_ref, dh_ref, loss_ref, dgain_ref):
        @pl.when(pl.program_id(0) == 0)
        def _():
            loss_ref[...] = jnp.zeros_like(loss_ref)
            dgain_ref[...] = jnp.zeros_like(dgain_ref)

        xf = h_ref[...]
        gain = gain_ref[...]
        err = xf * _rstd(xf) * gain - t_ref[...]
        loss_ref[...] += 0.5 * jnp.sum(jnp.mean(err * err, axis=-1, keepdims=True), axis=0, keepdims=True)
        dx, dgain = _rms_bwd(xf, gain, err * (1.0 / D))
        dh_ref[...] = dx
        dgain_ref[...] += dgain

    row = lambda i: (i, 0)
    fixed = lambda i: (0, 0)
    return pl.pallas_call(
        body, name=name, grid=(T // tm,),
        in_specs=[pl.BlockSpec((tm, D), row), pl.BlockSpec((tm, D), row), pl.BlockSpec((1, D), fixed)],
        out_specs=[pl.BlockSpec((tm, D), row), pl.BlockSpec((1, 128), fixed), pl.BlockSpec((1, D), fixed)],
        out_shape=[jax.ShapeDtypeStruct((T, D), F32), jax.ShapeDtypeStruct((1, 128), F32),
                   jax.ShapeDtypeStruct((1, D), F32)],
        compiler_params=_params(("arbitrary",)),
    )(h, target, gain)


def _inproj_fwd(h, gain, w_in, *, tm, name):
    T, D = h.shape
    tm = min(tm, T)
    nb, bw = w_in.shape[0], w_in.shape[2]

    def body(h_ref, gain_ref, w_ref, un_ref, proj_ref):
        @pl.when(pl.program_id(1) == 0)
        def _():
            xf = h_ref[...]
            un_ref[...] = (xf * _rstd(xf) * gain_ref[...]).astype(BF16)

        proj_ref[...] = _mm(un_ref[...], w_ref[...])

    return pl.pallas_call(
        body, name=name, grid=(T // tm, nb),
        in_specs=[
            pl.BlockSpec((tm, D), lambda i, j: (i, 0)),
            pl.BlockSpec((1, D), lambda i, j: (0, 0)),
            pl.BlockSpec((None, D, bw), lambda i, j: (j, 0, 0)),
        ],
        out_specs=[pl.BlockSpec((tm, D), lambda i, j: (i, 0)), pl.BlockSpec((tm, bw), lambda i, j: (i, j))],
        out_shape=[jax.ShapeDtypeStruct((T, D), BF16), jax.ShapeDtypeStruct((T, nb * bw), F32)],
        compiler_params=_params(("arbitrary", "arbitrary")),
    )(h, gain, w_in)


def _inproj_bwd(dproj, dh, h, gain, w_in, *, tm, name):
    T, D = h.shape
    tm = min(tm, T)
    nb, bw = w_in.shape[0], w_in.shape[2]

    def body(dp_ref, dh_ref, h_ref, gain_ref, w_ref, dx_ref, dgain_ref, acc):
        i, j = pl.program_id(0), pl.program_id(1)

        @pl.when(j == 0)
        def _():
            acc[...] = jnp.zeros_like(acc)

        @pl.when((i == 0) & (j == 0))
        def _():
            dgain_ref[...] = jnp.zeros_like(dgain_ref)

        acc[...] += _mm_nt(dp_ref[...], w_ref[...])

        @pl.when(j == nb - 1)
        def _():
            dx, dgain = _rms_bwd(h_ref[...], gain_ref[...], acc[...])
            dx_ref[...] = dh_ref[...] + dx
            dgain_ref[...] += dgain

    row = lambda i, j: (i, 0)
    return pl.pallas_call(
        body, name=name, grid=(T // tm, nb),
        in_specs=[
            pl.BlockSpec((tm, bw), lambda i, j: (i, j)),
            pl.BlockSpec((tm, D), row),
            pl.BlockSpec((tm, D), row),
            pl.BlockSpec((1, D), lambda i, j: (0, 0)),
            pl.BlockSpec((None, D, bw), lambda i, j: (j, 0, 0)),
        ],
        out_specs=[pl.BlockSpec((tm, D), row), pl.BlockSpec((1, D), lambda i, j: (0, 0))],
        out_shape=[jax.ShapeDtypeStruct((T, D), F32), jax.ShapeDtypeStruct((1, D), F32)],
        scratch_shapes=[pltpu.VMEM((tm, D), F32)],
        compiler_params=_params(("arbitrary", "arbitrary")),
    )(dproj, dh, h, gain, w_in)


def _window_sum(x, row, doublings, *, backward):
    T = x.shape[0]
    s = x
    for k in range(doublings):
        sh = 1 << k
        if backward:
            s = s + jnp.where(row < T - sh, pltpu.roll(s, T - sh, 0), 0.0)
        else:
            s = s + jnp.where(row >= sh, pltpu.roll(s, sh, 0), 0.0)
    return s


def _pool_fwd(proj, w_group, scale, *, name):
    T = proj.shape[0]

    def body(xp_ref, w_ref, scale_ref, p_ref):
        row = lax.broadcasted_iota(jnp.int32, (T, POOL_GROUP), 0)
        for gi, window in enumerate(POOL_WINDOWS):
            cols = slice(gi * POOL_GROUP, (gi + 1) * POOL_GROUP)
            x = xp_ref[:, cols]
            inv_count = 1.0 / jnp.minimum(row + 1, window).astype(F32)
            yc = _window_sum(x, row, gi + 1, backward=False) * inv_count - x
            pre = _mm(yc.astype(BF16), w_ref[gi].astype(BF16))
            p_ref[:, cols] = pre * scale_ref[:, cols]

    return pl.pallas_call(
        body, name=name, grid=(1,),
        in_specs=[
            pl.BlockSpec((T, POOL_WIDTH), lambda i: (0, 0)),
            pl.BlockSpec(w_group.shape, lambda i: (0, 0, 0)),
            pl.BlockSpec((1, POOL_WIDTH), lambda i: (0, 0)),
        ],
        out_specs=pl.BlockSpec((T, POOL_WIDTH), lambda i: (0, 0)),
        out_shape=jax.ShapeDtypeStruct((T, POOL_WIDTH), F32),
        compiler_params=_params(("arbitrary",)),
    )(proj, w_group, scale)


def _pool_bwd(dp, proj, w_group, scale, *, name):
    T = proj.shape[0]

    def body(dp_ref, xp_ref, w_ref, scale_ref, dxp_ref, dw_ref, dscale_ref):
        row = lax.broadcasted_iota(jnp.int32, (T, POOL_GROUP), 0)
        for gi, window in enumerate(POOL_WINDOWS):
            cols = slice(gi * POOL_GROUP, (gi + 1) * POOL_GROUP)
            x = xp_ref[:, cols]
            inv_count = 1.0 / jnp.minimum(row + 1, window).astype(F32)
            yc = (_window_sum(x, row, gi + 1, backward=False) * inv_count - x).astype(BF16)
            w = w_ref[gi].astype(BF16)
            pre = _mm(yc, w)
            dpg = dp_ref[:, cols]
            dscale_ref[:, cols] = jnp.sum(dpg * pre, axis=0, keepdims=True)
            dpre = (dpg * scale_ref[:, cols]).astype(BF16)
            dw_ref[gi] = _mm_tn(yc, dpre)
            dyc = _mm_nt(dpre, w)
            dxp_ref[:, cols] = _window_sum(dyc * inv_count, row, gi + 1, backward=True) - dyc

    return pl.pallas_call(
        body, name=name, grid=(1,),
        in_specs=[
            pl.BlockSpec((T, POOL_WIDTH), lambda i: (0, 0)),
            pl.BlockSpec((T, POOL_WIDTH), lambda i: (0, 0)),
            pl.BlockSpec(w_group.shape, lambda i: (0, 0, 0)),
            pl.BlockSpec((1, POOL_WIDTH), lambda i: (0, 0)),
        ],
        out_specs=[
            pl.BlockSpec((T, POOL_WIDTH), lambda i: (0, 0)),
            pl.BlockSpec(w_group.shape, lambda i: (0, 0, 0)),
            pl.BlockSpec((1, POOL_WIDTH), lambda i: (0, 0)),
        ],
        out_shape=[jax.ShapeDtypeStruct((T, POOL_WIDTH), F32), jax.ShapeDtypeStruct(w_group.shape, F32),
                   jax.ShapeDtypeStruct((1, POOL_WIDTH), F32)],
        compiler_params=_params(("arbitrary",)),
    )(dp, proj, w_group, scale)


def _log_sigmoids(z):
    t = jnp.log(1.0 + jnp.exp(-jnp.abs(z)))
    return jnp.minimum(z, 0.0) - t, -jnp.maximum(z, 0.0) - t


def _tri_sum(x, tri):
    hi = x.astype(BF16)
    lo = (x - hi.astype(F32)).astype(BF16)
    return _mm(hi, tri) + _mm(lo, tri)


def _attn_specs(T, tq):
    q_col = POOL_WIDTH // HEAD_PAIR
    k_col = q_col + SB_WIDTH // HEAD_PAIR
    v_col = k_col + SB_WIDTH // HEAD_PAIR
    return [
        pl.BlockSpec((tq, HEAD_PAIR), lambda p, i: (i, q_col + p)),
        pl.BlockSpec((T, HEAD_PAIR), lambda p, i: (0, k_col + p)),
        pl.BlockSpec((T, HEAD_PAIR), lambda p, i: (0, v_col + p)),
    ]


def _attn_fwd(proj, *, name):
    T = proj.shape[0]
    tq = ATTN_BLOCK

    def body(q_ref, k_ref, v_ref, o_ref, lt_ref, kb_scr, vb_scr):
        qi = pl.program_id(1)

        @pl.when(qi == 0)
        def _():
            kb_scr[...] = k_ref[...].astype(BF16)
            vb_scr[...] = v_ref[...].astype(BF16)

        head0 = lax.broadcasted_iota(jnp.int32, (tq, HEAD_PAIR), 1) < HEAD_DIM
        q = q_ref[...] * ATTN_SCALE
        qs = (jnp.where(head0, q, 0.0).astype(BF16), jnp.where(head0, 0.0, q).astype(BF16))
        r = lax.broadcasted_iota(jnp.int32, (tq, tq), 0)
        c = lax.broadcasted_iota(jnp.int32, (tq, tq), 1)
        later = (r > c).astype(BF16)
        causal = c < r

        def block(kj, carry, valid):
            off = pl.multiple_of(kj * tq, tq)
            kb = kb_scr[pl.ds(off, tq), :]
            vb = vb_scr[pl.ds(off, tq), :]
            out = []
            for h in range(2):
                run, o = carry[2 * h], carry[2 * h + 1]
                z = _mm_nt(qs[h], kb)
                lb, lm = _log_sigmoids(z)
                if valid is not None:
                    lm = jnp.where(valid, lm, 0.0)
                a = jnp.exp(lb + run + _tri_sum(lm, later))
                if valid is not None:
                    a = jnp.where(valid, a, 0.0)
                out += [run + jnp.sum(lm, axis=1, keepdims=True), o + _mm(a.astype(BF16), vb)]
            return tuple(out)

        zero = (jnp.zeros((tq, 1), F32), jnp.zeros((tq, HEAD_PAIR), F32))
        carry = block(qi, zero + zero, causal)
        carry = lax.fori_loop(0, qi, lambda it, cr: block(qi - 1 - it, cr, None), carry)
        o_ref[...] = jnp.where(head0, carry[1], carry[3])
        lt_ref[...] = jnp.where(head0, carry[0], carry[2])

    out_spec = pl.BlockSpec((tq, HEAD_PAIR), lambda p, i: (i, p))
    return pl.pallas_call(
        body, name=name, grid=(N_HEADS // 2, T // tq),
        in_specs=_attn_specs(T, tq), out_specs=[out_spec, out_spec],
        out_shape=[jax.ShapeDtypeStruct((T, SB_WIDTH), F32), jax.ShapeDtypeStruct((T, SB_WIDTH), F32)],
        scratch_shapes=[pltpu.VMEM((T, HEAD_PAIR), BF16), pltpu.VMEM((T, HEAD_PAIR), BF16)],
        compiler_params=_params(("arbitrary", "arbitrary")),
    )(proj, proj, proj)


def _attn_bwd(proj, do, ltot, *, name):
    T = proj.shape[0]
    tq = ATTN_BLOCK

    def body(q_ref, k_ref, v_ref, do_ref, lt_ref, dq_ref, dk_ref, dv_ref, kb_scr, vb_scr):
        qi = pl.program_id(1)

        @pl.when(qi == 0)
        def _():
            kb_scr[...] = k_ref[...].astype(BF16)
            vb_scr[...] = v_ref[...].astype(BF16)
            dk_ref[...] = jnp.zeros_like(dk_ref)
            dv_ref[...] = jnp.zeros_like(dv_ref)

        head0 = lax.broadcasted_iota(jnp.int32, (tq, HEAD_PAIR), 1) < HEAD_DIM
        q, do_, lt = q_ref[...] * ATTN_SCALE, do_ref[...], lt_ref[...]
        qs = (jnp.where(head0, q, 0.0).astype(BF16), jnp.where(head0, 0.0, q).astype(BF16))
        dos = (jnp.where(head0, do_, 0.0).astype(BF16), jnp.where(head0, 0.0, do_).astype(BF16))
        lts = (jnp.max(jnp.where(head0, lt, -jnp.inf), axis=1, keepdims=True),
               jnp.max(jnp.where(head0, -jnp.inf, lt), axis=1, keepdims=True))
        r = lax.broadcasted_iota(jnp.int32, (tq, tq), 0)
        c = lax.broadcasted_iota(jnp.int32, (tq, tq), 1)
        upto = (r <= c).astype(BF16)
        before = (r < c).astype(BF16)
        causal = c < r

        def block(kj, carry, valid):
            off = pl.multiple_of(kj * tq, tq)
            kb = kb_scr[pl.ds(off, tq), :]
            vb = vb_scr[pl.ds(off, tq), :]
            dk_blk = jnp.zeros((tq, HEAD_PAIR), F32)
            dv_blk = jnp.zeros((tq, HEAD_PAIR), F32)
            out = []
            for h in range(2):
                run_lm, run_e, dq = carry[3 * h:3 * h + 3]
                z = _mm_nt(qs[h], kb)
                lb, lm = _log_sigmoids(z)
                if valid is not None:
                    lm = jnp.where(valid, lm, 0.0)
                a = jnp.exp(lb + (lts[h] - run_lm - _tri_sum(lm, upto)))
                if valid is not None:
                    a = jnp.where(valid, a, 0.0)
                e = _mm_nt(dos[h], vb) * a
                beta = jnp.exp(lb)
                dz = e * (1.0 - beta) - (run_e + _tri_sum(e, before)) * beta
                if valid is not None:
                    dz = jnp.where(valid, dz, 0.0)
                dz = dz.astype(BF16)
                dk_blk += _mm_tn(dz, qs[h])
                dv_blk += _mm_tn(a.astype(BF16), dos[h])
                out += [run_lm + jnp.sum(lm, axis=1, keepdims=True), run_e + jnp.sum(e, axis=1, keepdims=True),
                        dq + _mm(dz, kb)]
            dk_ref[pl.ds(off, tq), :] += dk_blk
            dv_ref[pl.ds(off, tq), :] += dv_blk
            return tuple(out)

        zero = (jnp.zeros((tq, 1), F32), jnp.zeros((tq, 1), F32), jnp.zeros((tq, HEAD_PAIR), F32))
        carry = lax.fori_loop(0, qi, lambda kj, cr: block(kj, cr, None), zero + zero)
        carry = block(qi, carry, causal)
        dq_ref[...] = jnp.where(head0, carry[2], carry[5]) * ATTN_SCALE

    blk = pl.BlockSpec((tq, HEAD_PAIR), lambda p, i: (i, p))
    seq = pl.BlockSpec((T, HEAD_PAIR), lambda p, i: (0, p))
    return pl.pallas_call(
        body, name=name, grid=(N_HEADS // 2, T // tq),
        in_specs=_attn_specs(T, tq) + [blk, blk], out_specs=[blk, seq, seq],
        out_shape=[jax.ShapeDtypeStruct((T, SB_WIDTH), F32)] * 3,
        scratch_shapes=[pltpu.VMEM((T, HEAD_PAIR), BF16), pltpu.VMEM((T, HEAD_PAIR), BF16)],
        compiler_params=_params(("arbitrary", "arbitrary")),
    )(proj, proj, proj, do, ltot)


def _branch(act_bf16, w_ref):
    return jnp.concatenate([_mm(act_bf16, w_ref[e]) for e in range(w_ref.shape[0])], axis=1)


def _mix_specs(T, D, tm, wbp, w_out):
    gate_col = (POOL_WIDTH + 3 * SB_WIDTH) // D
    row = lambda i: (i, 0)
    return [
        pl.BlockSpec((tm, D), row),
        pl.BlockSpec((tm, POOL_WIDTH), row),
        pl.BlockSpec((tm, SB_WIDTH), row),
        pl.BlockSpec((tm, D), lambda i: (i, gate_col)),
        pl.BlockSpec((tm, D), lambda i: (i, gate_col + 1)),
        pl.BlockSpec(wbp.shape, lambda i: (0, 0, 0)),
        pl.BlockSpec(wbp.shape, lambda i: (0, 0, 0)),
        pl.BlockSpec(w_out.shape, lambda i: (0, 0)),
    ]


def _mix_fwd(h, p, o, proj, wbp, wba, w_out, *, tm, name):
    T, D = h.shape
    tm = min(tm, T)

    def body(h_ref, p_ref, o_ref, glp_ref, gls_ref, wbp_ref, wba_ref, wout_ref, hout_ref, m_ref):
        yp = _branch(p_ref[...].astype(BF16), wbp_ref)
        ys = _branch(o_ref[...].astype(BF16), wba_ref)
        m = (jax.nn.sigmoid(glp_ref[...]) * yp + jax.nn.sigmoid(gls_ref[...]) * ys).astype(BF16)
        m_ref[...] = m
        hout_ref[...] = h_ref[...] + _mm(m, wout_ref[...])

    row = lambda i: (i, 0)
    return pl.pallas_call(
        body, name=name, grid=(T // tm,),
        in_specs=_mix_specs(T, D, tm, wbp, w_out),
        out_specs=[pl.BlockSpec((tm, D), row), pl.BlockSpec((tm, D), row)],
        out_shape=[jax.ShapeDtypeStruct((T, D), F32), jax.ShapeDtypeStruct((T, D), BF16)],
        compiler_params=_params(("arbitrary",)),
    )(h, p, o, proj, proj, wbp, wba, w_out)


def _mix_bwd(dh, p, o, proj, wbp, wba, w_out, *, tm, name):
    T, D = dh.shape
    tm = min(tm, T)
    bw = wbp.shape[2]

    def body(dh_ref, p_ref, o_ref, glp_ref, gls_ref, wbp_ref, wba_ref, wout_ref,
             dyp_ref, dys_ref, dp_ref, do_ref, dgl_ref):
        dm = _mm_nt(dh_ref[...].astype(BF16), wout_ref[...])
        yp = _branch(p_ref[...].astype(BF16), wbp_ref)
        ys = _branch(o_ref[...].astype(BF16), wba_ref)
        gp = jax.nn.sigmoid(glp_ref[...])
        gs = jax.nn.sigmoid(gls_ref[...])
        dyp = (dm * gp).astype(BF16)
        dys = (dm * gs).astype(BF16)
        dyp_ref[...] = dyp
        dys_ref[...] = dys
        dgl_ref[:, :D] = (dm * yp * gp * (1.0 - gp)).astype(BF16)
        dgl_ref[:, D:] = (dm * ys * gs * (1.0 - gs)).astype(BF16)
        dp = jnp.zeros(dp_ref.shape, F32)
        do_ = jnp.zeros(do_ref.shape, F32)
        for e in range(wbp_ref.shape[0]):
            dp += _mm_nt(dyp[:, e * bw:(e + 1) * bw], wbp_ref[e])
            do_ += _mm_nt(dys[:, e * bw:(e + 1) * bw], wba_ref[e])
        dp_ref[...] = dp
        do_ref[...] = do_

    row = lambda i: (i, 0)
    return pl.pallas_call(
        body, name=name, grid=(T // tm,),
        in_specs=_mix_specs(T, D, tm, wbp, w_out),
        out_specs=[pl.BlockSpec((tm, D), row), pl.BlockSpec((tm, D), row), pl.BlockSpec((tm, POOL_WIDTH), row),
                   pl.BlockSpec((tm, SB_WIDTH), row), pl.BlockSpec((tm, 2 * D), row)],
        out_shape=[jax.ShapeDtypeStruct((T, D), BF16), jax.ShapeDtypeStruct((T, D), BF16),
                   jax.ShapeDtypeStruct((T, POOL_WIDTH), F32), jax.ShapeDtypeStruct((T, SB_WIDTH), F32),
                   jax.ShapeDtypeStruct((T, 2 * D), BF16)],
        compiler_params=_params(("arbitrary",)),
    )(dh, p, o, proj, proj, wbp, wba, w_out)


def _adamw(w, g, m, v, *, name):
    R, C = w.shape
    tr = R if R * C <= 512 * 1024 else 256

    def body(w_ref, g_ref, m_ref, v_ref, d_ref, nm_ref, nv_ref):
        g_ = g_ref[...]
        m_ = ADAM_B1 * m_ref[...] + (1.0 - ADAM_B1) * g_
        v_ = ADAM_B2 * v_ref[...] + (1.0 - ADAM_B2) * (g_ * g_)
        m_hat = m_ / (1.0 - ADAM_B1 ** ADAM_STEP)
        v_hat = v_ / (1.0 - ADAM_B2 ** ADAM_STEP)
        d_ref[...] = -ADAM_LR * (m_hat / (jnp.sqrt(v_hat) + ADAM_EPS) + ADAM_WD * w_ref[...])
        nm_ref[...] = m_
        nv_ref[...] = v_

    spec = pl.BlockSpec((tr, C), lambda i: (i, 0))
    return pl.pallas_call(
        body, name=name, grid=(R // tr,), in_specs=[spec] * 4, out_specs=[spec] * 3,
        out_shape=[jax.ShapeDtypeStruct((R, C), F32)] * 3,
        compiler_params=_params(("arbitrary",)),
    )(w, g, m, v)


def _position():
    return lax.axis_index("x"), lax.axis_index("y"), lax.axis_index("c")


def _all_gather(shards, *, name, collective_id):
    n = len(shards)

    def body(*refs):
        ins, outs = refs[:n], refs[n:2 * n]
        send_sems, recv_sems, local_sems = refs[2 * n:]
        x, y, c = _position()
        me, sibling = (x, y, c), (x, y, 1 - c)
        chips = [(1 - x, y), (x, 1 - y), (1 - x, 1 - y)]

        barrier = pltpu.get_barrier_semaphore()
        for peer in [sibling] + [(*chip, c) for chip in chips]:
            pl.semaphore_signal(barrier, inc=1, device_id=peer, device_id_type=MESH)
        pl.semaphore_wait(barrier, 4)

        def block(a, pos):
            return outs[a].at[4 * pos[0] + 2 * pos[1] + pos[2]]

        def copy(a, k, pos, to, src=None):
            return pltpu.make_async_remote_copy(
                src_ref=block(a, pos) if src is None else src, dst_ref=block(a, pos),
                send_sem=send_sems.at[7 * a + k], recv_sem=recv_sems.at[7 * a + k],
                device_id=to, device_id_type=MESH)

        started = []
        for a in range(n):
            mine = pltpu.make_async_copy(ins[a], block(a, me), local_sems.at[a])
            mine.start()
            started.append(mine)
        sends = []
        for a in range(n):
            sends += [copy(a, 1 + j, me, (*chip, c), src=ins[a]) for j, chip in enumerate(chips)]
            sends.append(copy(a, 0, me, sibling, src=ins[a]))
        for cp in sends:
            cp.start()
        for j, chip in enumerate(chips):
            for a in range(n):
                copy(a, 1 + j, (*chip, c), me).wait_recv()
                passed = copy(a, 4 + j, (*chip, c), sibling)
                passed.start()
                sends.append(passed)
        for a in range(n):
            copy(a, 0, sibling, me).wait_recv()
            for j, chip in enumerate(chips):
                copy(a, 4 + j, (*chip, 1 - c), me).wait_recv()
        for cp in sends:
            cp.wait_send()
        for cp in started:
            cp.wait()

    return pl.kernel(
        body, name=name,
        out_type=[jax.ShapeDtypeStruct((N_DEV,) + s.shape, s.dtype) for s in shards],
        mesh=plsc.ScalarSubcoreMesh(axis_name="sequencer", num_cores=1),
        scratch_types=[pltpu.SemaphoreType.DMA((7 * n,)), pltpu.SemaphoreType.DMA((7 * n,)),
                       pltpu.SemaphoreType.DMA((n,))],
        compiler_params=pltpu.CompilerParams(collective_id=collective_id),
    )(*shards)


def _chip_sums(grads, *, name):
    _, R, C = grads.shape
    rc = 128 if R % 128 == 0 else R

    def body(g_ref, partial, out_ref, mine, theirs, send_sems, recv_sems, local_sems):
        x, y, c = _position()
        my_chip = 2 * x + y

        def swap(s):
            return pltpu.make_async_remote_copy(
                src_ref=g_ref.at[2 * s + (1 - c)], dst_ref=theirs.at[s],
                send_sem=send_sems.at[s], recv_sem=recv_sems.at[s],
                device_id=(x, y, 1 - c), device_id_type=MESH)

        def load(s):
            return pltpu.make_async_copy(g_ref.at[2 * s + c], mine.at[s], local_sems.at[s])

        for s in range(4):
            swap(s).start()
            load(s).start()
        for s in range(4):
            load(s).wait()
            swap(s).wait_recv()

        def chip_sum(chip, rows):
            return mine[chip, rows, :].astype(F32) + theirs[chip, rows, :].astype(F32)

        for j in (1, 2, 3):
            @pl.loop(0, R // rc)
            def _(t):
                rows = pl.ds(pl.multiple_of(t * rc, rc), rc)
                partial[j - 1, rows, :] = chip_sum(my_chip ^ j, rows).astype(BF16)

        @pl.loop(0, R // rc)
        def _(t):
            rows = pl.ds(pl.multiple_of(t * rc, rc), rc)
            out_ref[rows, :] = chip_sum(my_chip, rows)

        for s in range(4):
            swap(s).wait_send()

    vmem = pl.BlockSpec(memory_space=pltpu.VMEM)
    return pl.pallas_call(
        body, name=name,
        in_specs=[pl.BlockSpec(memory_space=pl.ANY)], out_specs=[vmem, vmem],
        out_shape=[jax.ShapeDtypeStruct((3, R, C), BF16), jax.ShapeDtypeStruct((R, C), F32)],
        scratch_shapes=[
            pltpu.VMEM((4, R, C), BF16), pltpu.VMEM((4, R, C), BF16),
            pltpu.SemaphoreType.DMA((4,)), pltpu.SemaphoreType.DMA((4,)), pltpu.SemaphoreType.DMA((4,)),
        ],
        compiler_params=_params(),
    )(grads)


def _cross_chips(partials, *, name, collective_id):
    n = len(partials)

    def body(*refs):
        ins, outs = refs[:n], refs[n:2 * n]
        send_sems, recv_sems = refs[2 * n:]
        x, y, c = _position()
        my_chip = 2 * x + y
        peers = [((my_chip ^ j) // 2, (my_chip ^ j) % 2, c) for j in (1, 2, 3)]

        barrier = pltpu.get_barrier_semaphore()
        for peer in peers:
            pl.semaphore_signal(barrier, inc=1, device_id=peer, device_id_type=MESH)
        pl.semaphore_wait(barrier, 3)

        copies = [
            pltpu.make_async_remote_copy(
                src_ref=ins[a].at[j], dst_ref=outs[a].at[j],
                send_sem=send_sems.at[3 * a + j], recv_sem=recv_sems.at[3 * a + j],
                device_id=peers[j], device_id_type=MESH)
            for a in range(n) for j in range(3)]
        for cp in copies:
            cp.start()
        for cp in copies:
            cp.wait_recv()
        for cp in copies:
            cp.wait_send()

    return pl.kernel(
        body, name=name,
        out_type=[jax.ShapeDtypeStruct(p.shape, p.dtype) for p in partials],
        mesh=plsc.ScalarSubcoreMesh(axis_name="sequencer", num_cores=1),
        scratch_types=[pltpu.SemaphoreType.DMA((3 * n,)), pltpu.SemaphoreType.DMA((3 * n,))],
        compiler_params=pltpu.CompilerParams(collective_id=collective_id),
    )(*partials)


def _owner_sum(own, landed, *, name):
    R, C = own.shape
    tr = R if R * C <= 512 * 1024 else 256

    def body(own_ref, landed_ref, out_ref):
        total = own_ref[...]
        for j in range(3):
            total = total + landed_ref[j].astype(F32)
        out_ref[...] = total

    return pl.pallas_call(
        body, name=name, grid=(R // tr,),
        in_specs=[pl.BlockSpec((tr, C), lambda i: (i, 0)), pl.BlockSpec((3, tr, C), lambda i: (0, i, 0))],
        out_specs=pl.BlockSpec((tr, C), lambda i: (i, 0)),
        out_shape=jax.ShapeDtypeStruct((R, C), F32),
        compiler_params=_params(("arbitrary",)),
    )(own, landed)


def _all_reduce_small(slab, *, name):
    R, C = slab.shape

    def body(in_ref, out_ref, gathered, send_sems, recv_sems):
        x, y, c = _position()
        me = 4 * x + 2 * y + c

        def copy(k):
            peer = me ^ k
            return pltpu.make_async_remote_copy(
                src_ref=in_ref, dst_ref=gathered.at[me],
                send_sem=send_sems.at[k - 1], recv_sem=recv_sems.at[k - 1],
                device_id=(peer // 4, (peer // 2) % 2, peer % 2), device_id_type=MESH)

        def arrival(k):
            return pltpu.make_async_remote_copy(
                src_ref=in_ref, dst_ref=gathered.at[me ^ k],
                send_sem=send_sems.at[k - 1], recv_sem=recv_sems.at[k - 1],
                device_id=(x, y, c), device_id_type=MESH)

        for k in range(1, N_DEV):
            copy(k).start()
        gathered[me] = in_ref[...]
        for k in range(1, N_DEV):
            arrival(k).wait_recv()
        total = gathered[0]
        for d in range(1, N_DEV):
            total = total + gathered[d]
        out_ref[...] = total
        for k in range(1, N_DEV):
            copy(k).wait_send()

    return pl.pallas_call(
        body, name=name,
        in_specs=[pl.BlockSpec(memory_space=pltpu.VMEM)],
        out_specs=pl.BlockSpec(memory_space=pltpu.VMEM),
        out_shape=jax.ShapeDtypeStruct((R, C), F32),
        scratch_shapes=[pltpu.VMEM((N_DEV, R, C), F32),
                        pltpu.SemaphoreType.DMA((N_DEV - 1,)), pltpu.SemaphoreType.DMA((N_DEV - 1,))],
        compiler_params=_params(),
    )(slab)


def _local_step(x, target, norms, pool_w_group, pool_scale, wgu1, wd1, w_in, wbp, wba, w_out, wgu2, wd2, exchange):
    n1g, nmg, n2g, nfg = norms
    D = x.shape[1]
    h1, gu1 = _ffn_fwd(x, n1g, wgu1, wd1, tm=512, name="ffn1_fwd")
    un, proj = _inproj_fwd(h1, nmg, w_in, tm=512, name="inproj_fwd")
    p = _pool_fwd(proj, pool_w_group, pool_scale, name="pool_fwd")
    o, ltot = _attn_fwd(proj, name="attn_fwd")
    h2, m = _mix_fwd(h1, p, o, proj, wbp, wba, w_out, tm=256, name="mix_fwd")
    h3, gu2 = _ffn_fwd(h2, n2g, wgu2, wd2, tm=512, name="ffn2_fwd")
    dh3, loss, d_nf = _loss_bwd(h3, target, nfg, tm=256, name="loss_bwd")

    dh2, d_n2, n2, df2, dgu2, hid2 = _ffn_bwd(dh3, h2, n2g, gu2, wgu2, wd2, tm=256, name="ffn2_bwd")
    d_wgu2 = _wgrad_gate_up(n2, dgu2, tk=512, name="ffn2_wgrad_gate_up")
    d_wd2 = _wgrad_down(hid2, df2, tk=512, name="ffn2_wgrad_down")
    g_wgu2, g_wd2 = exchange("ffn2", [d_wgu2, d_wd2.reshape(N_DEV, FF_SHARD_PAD, D)])

    dyp, dys, dp, do, dgl = _mix_bwd(dh2, p, o, proj, wbp, wba, w_out, tm=256, name="mix_bwd")
    d_wout = _wgrad_full(m, dh2, tk=512, name="wgrad_out")
    d_wbp = _wgrad_full(p, dyp, tk=512, name="wgrad_branch_pool", split_lanes=wbp.shape[2])
    d_wba = _wgrad_full(o, dys, tk=512, name="wgrad_branch_attn", split_lanes=wba.shape[2])
    g_wbp, g_wba, g_wout = exchange("mix", [d_wbp, d_wba, d_wout.reshape(N_DEV, D // N_DEV, D)])
    dxp, d_wgroup, d_scale = _pool_bwd(dp, proj, pool_w_group, pool_scale, name="pool_bwd")
    dq, dk, dv = _attn_bwd(proj, do, ltot, name="attn_bwd")
    dproj = jnp.concatenate([dxp.astype(BF16), dq.astype(BF16), dk.astype(BF16), dv.astype(BF16), dgl], axis=1)
    dh1, d_nm = _inproj_bwd(dproj, dh2, h1, nmg, w_in, tm=256, name="inproj_bwd")
    d_win = _wgrad_in(un, dproj, tk=512, name="wgrad_in")
    g_win, = exchange("w_in", [d_win])

    dx, d_n1, n1, df1, dgu1, hid1 = _ffn_bwd(dh1, x, n1g, gu1, wgu1, wd1, tm=256, name="ffn1_bwd")
    d_wgu1 = _wgrad_gate_up(n1, dgu1, tk=512, name="ffn1_wgrad_gate_up")
    d_wd1 = _wgrad_down(hid1, df1, tk=512, name="ffn1_wgrad_down")
    g_wgu1, g_wd1 = exchange("ffn1", [d_wgu1, d_wd1.reshape(N_DEV, FF_SHARD_PAD, D)])

    sharded = (g_wgu1, g_wd1, g_win, g_wbp, g_wba, g_wout, g_wgu2, g_wd2)
    replicated = (d_n1, d_nm, d_n2, d_nf, d_scale, d_wgroup)
    return loss, dx, sharded, replicated


def _pad_gate_up(w):
    d = w.shape[0]
    w = w.astype(BF16).reshape(d, 2, FF_SHARD)
    return jnp.pad(w, ((0, 0), (0, 0), (0, FF_SHARD_PAD - FF_SHARD))).reshape(d, 2 * FF_SHARD_PAD)


def _unpad_gate_up(g):
    d = g.shape[0]
    return g.reshape(d, 2, FF_SHARD_PAD)[:, :, :FF_SHARD].reshape(d, 2 * FF_SHARD)


def _pad_down(w):
    return jnp.pad(w.astype(BF16), ((0, FF_SHARD_PAD - FF_SHARD), (0, 0)))


def kernel(x, ffn1_norm, ffn1_w_gate_up, ffn1_w_down, mix_norm, w_in, pool_w_group, pool_scale, w_branch_pool, w_branch_attn, w_out, ffn2_norm, ffn2_w_gate_up, ffn2_w_down, final_norm, loss_target, m_ffn1_norm, m_ffn1_w_gate_up, m_ffn1_w_down, m_mix_norm, m_w_in, m_pool_w_group, m_pool_scale, m_w_branch_pool, m_w_branch_attn, m_w_out, m_ffn2_norm, m_ffn2_w_gate_up, m_ffn2_w_down, m_final_norm, v_ffn1_norm, v_ffn1_w_gate_up, v_ffn1_w_down, v_mix_norm, v_w_in, v_pool_w_group, v_pool_scale, v_w_branch_pool, v_w_branch_attn, v_w_out, v_ffn2_norm, v_ffn2_w_gate_up, v_ffn2_w_down, v_final_norm):
    D = x.shape[-1]
    weights = dict(ffn1_norm=ffn1_norm, ffn1_w_gate_up=ffn1_w_gate_up, ffn1_w_down=ffn1_w_down, mix_norm=mix_norm,
                   w_in=w_in, pool_w_group=pool_w_group, pool_scale=pool_scale, w_branch_pool=w_branch_pool,
                   w_branch_attn=w_branch_attn, w_out=w_out, ffn2_norm=ffn2_norm, ffn2_w_gate_up=ffn2_w_gate_up,
                   ffn2_w_down=ffn2_w_down, final_norm=final_norm)
    first = dict(ffn1_norm=m_ffn1_norm, ffn1_w_gate_up=m_ffn1_w_gate_up, ffn1_w_down=m_ffn1_w_down,
                 mix_norm=m_mix_norm, w_in=m_w_in, pool_w_group=m_pool_w_group, pool_scale=m_pool_scale,
                 w_branch_pool=m_w_branch_pool, w_branch_attn=m_w_branch_attn, w_out=m_w_out,
                 ffn2_norm=m_ffn2_norm, ffn2_w_gate_up=m_ffn2_w_gate_up, ffn2_w_down=m_ffn2_w_down,
                 final_norm=m_final_norm)
    second = dict(ffn1_norm=v_ffn1_norm, ffn1_w_gate_up=v_ffn1_w_gate_up, ffn1_w_down=v_ffn1_w_down,
                  mix_norm=v_mix_norm, w_in=v_w_in, pool_w_group=v_pool_w_group, pool_scale=v_pool_scale,
                  w_branch_pool=v_w_branch_pool, w_branch_attn=v_w_branch_attn, w_out=v_w_out,
                  ffn2_norm=v_ffn2_norm, ffn2_w_gate_up=v_ffn2_w_gate_up, ffn2_w_down=v_ffn2_w_down,
                  final_norm=v_final_norm)
    order = list(weights)

    wgu1, wd1 = _all_gather([_pad_gate_up(ffn1_w_gate_up[0]), _pad_down(ffn1_w_down[0])],
                            name="all_gather_ffn1", collective_id=0)
    win_g, = _all_gather([w_in[0].astype(BF16)], name="all_gather_w_in", collective_id=1)
    wbp_g, wba_g, wout_g = _all_gather(
        [w_branch_pool[0].astype(BF16), w_branch_attn[0].astype(BF16), w_out[0].astype(BF16)],
        name="all_gather_mix", collective_id=2)
    wgu2, wd2 = _all_gather([_pad_gate_up(ffn2_w_gate_up[0]), _pad_down(ffn2_w_down[0])],
                            name="all_gather_ffn2", collective_id=3)
    wd1 = wd1.reshape(N_DEV * FF_SHARD_PAD, D)
    wd2 = wd2.reshape(N_DEV * FF_SHARD_PAD, D)
    wout_g = wout_g.reshape(D, D)

    cross_ids = {"ffn2": 4, "mix": 5, "w_in": 6, "ffn1": 7}

    def exchange(tag, group):
        sums = [_chip_sums(g, name=f"chip_sums_{tag}_{i}") for i, g in enumerate(group)]
        landed = _cross_chips([s[0] for s in sums], name="cross_chips_" + tag, collective_id=cross_ids[tag])
        return [(s[1], l) for s, l in zip(sums, landed)]

    norms = (ffn1_norm, mix_norm, ffn2_norm, final_norm.reshape(1, D))
    loss, dx, sharded, replicated = _local_step(
        x[0], loss_target[0], norms, pool_w_group[0], pool_scale, wgu1, wd1, win_g, wbp_g, wba_g, wout_g, wgu2, wd2,
        exchange)
    names = ["ffn1_w_gate_up", "ffn1_w_down", "w_in", "w_branch_pool", "w_branch_attn", "w_out",
             "ffn2_w_gate_up", "ffn2_w_down"]
    grads = {k: _owner_sum(own, landed, name="owner_sum_" + k)
             for k, (own, landed) in reversed(list(zip(names, sharded)))}
    for k in ("ffn1_w_gate_up", "ffn2_w_gate_up"):
        grads[k] = _unpad_gate_up(grads[k])
    for k in ("ffn1_w_down", "ffn2_w_down"):
        grads[k] = grads[k][:FF_SHARD]

    d_n1, d_nm, d_n2, d_nf, d_scale, d_wgroup = replicated
    small = ["ffn1_norm", "mix_norm", "ffn2_norm", "final_norm", "pool_scale", "pool_w_group"]
    def tile_rows(a):
        a = a.reshape(-1, 128)
        return jnp.pad(a, ((0, -a.shape[0] % 8), (0, 0)))

    pieces = [tile_rows(d) for d in (d_n1, d_nm, d_n2, d_nf, d_scale, d_wgroup)]
    rows = [weights[k].size // 128 for k in small]
    starts = [sum(p.shape[0] for p in pieces[:i]) for i in range(len(pieces) + 1)]
    slab = jnp.concatenate(pieces + [jnp.broadcast_to(loss, (8, 128))], axis=0)
    total = _all_reduce_small(slab, name="all_reduce_replicated")
    loss_out = total[starts[-1], 0]

    small_w = jnp.concatenate([tile_rows(weights[k]) for k in small], axis=0)
    small_m = jnp.concatenate([tile_rows(first[k]) for k in small], axis=0)
    small_v = jnp.concatenate([tile_rows(second[k]) for k in small], axis=0)
    small_out = _adamw(small_w, total[:starts[-1]], small_m, small_v, name="adamw_replicated")
    delta, new_m, new_v = {}, {}, {}
    for name_, start, n_rows in zip(small, starts, rows):
        shape = weights[name_].shape
        grads[name_] = total[start:start + n_rows].reshape(shape)
        delta[name_], new_m[name_], new_v[name_] = (a[start:start + n_rows].reshape(shape) for a in small_out)
    for name_ in order:
        if name_ in small:
            continue
        shape = weights[name_].shape
        two_d = shape[1:]
        out = _adamw(weights[name_].reshape(two_d), grads[name_], first[name_].reshape(two_d),
                     second[name_].reshape(two_d), name="adamw_" + name_)
        delta[name_], new_m[name_], new_v[name_] = (a.reshape(shape) for a in out)
        grads[name_] = grads[name_].reshape(shape)

    return (loss_out, dx[None], *[grads[k] for k in order], *[delta[k] for k in order],
            *[new_m[k] for k in order], *[new_v[k] for k in order])
```

```python
import functools

import jax
import jax.numpy as jnp
from jax import lax
from jax.experimental import pallas as pl
from jax.experimental.pallas import tpu as pltpu
from jax.experimental.pallas import tpu_sc as plsc

F32 = jnp.float32
BF16 = jnp.bfloat16
MESH = pl.DeviceIdType.MESH

RMS_EPS = 1e-6
N_DEV = 8
N_HEADS = 8
HEAD_DIM = 64
HEAD_PAIR = 2 * HEAD_DIM
POOL_WINDOWS = (2, 4, 8, 16)
POOL_GROUP = 128
POOL_WIDTH = 512
SB_WIDTH = 512
FF_SHARD = 352
FF_SHARD_PAD = 384
ATTN_BLOCK = 256
ATTN_SCALE = 0.125

ADAM_LR = 0.001
ADAM_B1 = 0.9
ADAM_B2 = 0.999
ADAM_EPS = 1e-08
ADAM_WD = 0.01
ADAM_STEP = 10

VMEM_LIMIT = 48 << 20


def _params(dims=None):
    return pltpu.CompilerParams(dimension_semantics=dims, vmem_limit_bytes=VMEM_LIMIT)


def _mm(a, b):
    return jnp.dot(a, b, preferred_element_type=F32)


def _mm_nt(a, b):
    return lax.dot_general(a, b, (((1,), (1,)), ((), ())), preferred_element_type=F32)


def _mm_tn(a, b):
    return lax.dot_general(a, b, (((0,), (0,)), ((), ())), preferred_element_type=F32)


def _rstd(xf):
    return lax.rsqrt(jnp.mean(xf * xf, axis=-1, keepdims=True) + RMS_EPS)


def _rms_bwd(xf, gain, dn):
    r = _rstd(xf)
    xh = xf * r
    dgain = jnp.sum(dn * xh, axis=0, keepdims=True)
    dxh = dn * gain
    dx = r * (dxh - xh * jnp.mean(dxh * xh, axis=-1, keepdims=True))
    return dx, dgain


def _ffn_fwd(x, gain, wgu, wd, *, tm, name):
    T, D = x.shape
    tm = min(tm, T)
    nb, bw = wgu.shape[0] // 2, wgu.shape[2]

    def body(x_ref, gain_ref, wg_ref, wu_ref, wd_ref, h_ref, gu_ref, n_scr, acc):
        j = pl.program_id(1)

        @pl.when(j == 0)
        def _():
            xf = x_ref[...]
            n_scr[...] = (xf * _rstd(xf) * gain_ref[...]).astype(BF16)
            acc[...] = jnp.zeros_like(acc)

        n = n_scr[...]
        g = _mm(n, wg_ref[...])
        u = _mm(n, wu_ref[...])
        gu_ref[0] = g.astype(BF16)
        gu_ref[1] = u.astype(BF16)
        hid = (g * jax.nn.sigmoid(g) * u).astype(BF16)
        acc[...] += _mm(hid, wd_ref[...])

        @pl.when(j == nb - 1)
        def _():
            h_ref[...] = x_ref[...] + 0.5 * acc[...]

    return pl.pallas_call(
        body, name=name, grid=(T // tm, nb),
        in_specs=[
            pl.BlockSpec((tm, D), lambda i, j: (i, 0)),
            pl.BlockSpec((1, D), lambda i, j: (0, 0)),
            pl.BlockSpec((None, D, bw), lambda i, j: (j, 0, 0)),
            pl.BlockSpec((None, D, bw), lambda i, j: (j + nb, 0, 0)),
            pl.BlockSpec((bw, D), lambda i, j: (j, 0)),
        ],
        out_specs=[
            pl.BlockSpec((tm, D), lambda i, j: (i, 0)),
            pl.BlockSpec((2, tm, bw), lambda i, j: (0, i, j)),
        ],
        out_shape=[jax.ShapeDtypeStruct((T, D), F32), jax.ShapeDtypeStruct((2, T, nb * bw), BF16)],
        scratch_shapes=[pltpu.VMEM((tm, D), BF16), pltpu.VMEM((tm, D), F32)],
        compiler_params=_params(("arbitrary", "arbitrary")),
    )(x, gain, wgu, wgu, wd)


def _ffn_bwd(dh, x, gain, gu, wgu, wd, *, tm, name):
    T, D = x.shape
    tm = min(tm, T)
    nb, bw = wgu.shape[0] // 2, wgu.shape[2]

    def body(dh_ref, x_ref, gain_ref, gu_ref, wg_ref, wu_ref, wd_ref,
             dx_ref, dgain_ref, n_ref, df_ref, dgu_ref, hid_ref, dn_acc):
        i, j = pl.program_id(0), pl.program_id(1)

        @pl.when(j == 0)
        def _():
            xf = x_ref[...]
            n_ref[...] = (xf * _rstd(xf) * gain_ref[...]).astype(BF16)
            df_ref[...] = (0.5 * dh_ref[...]).astype(BF16)
            dn_acc[...] = jnp.zeros_like(dn_acc)

        @pl.when((i == 0) & (j == 0))
        def _():
            dgain_ref[...] = jnp.zeros_like(dgain_ref)

        dhid = _mm_nt(df_ref[...], wd_ref[...])
        g = gu_ref[0].astype(F32)
        u = gu_ref[1].astype(F32)
        s = jax.nn.sigmoid(g)
        silu = g * s
        hid_ref[...] = (silu * u).astype(BF16)
        dg = (dhid * u * (s * (1.0 + g * (1.0 - s)))).astype(BF16)
        du = (dhid * silu).astype(BF16)
        dgu_ref[0] = dg
        dgu_ref[1] = du
        dn_acc[...] += _mm_nt(dg, wg_ref[...]) + _mm_nt(du, wu_ref[...])

        @pl.when(j == nb - 1)
        def _():
            dx, dgain = _rms_bwd(x_ref[...], gain_ref[...], dn_acc[...])
            dx_ref[...] = dh_ref[...] + dx
            dgain_ref[...] += dgain

    row = lambda i, j: (i, 0)
    return pl.pallas_call(
        body, name=name, grid=(T // tm, nb),
        in_specs=[
            pl.BlockSpec((tm, D), row),
            pl.BlockSpec((tm, D), row),
            pl.BlockSpec((1, D), lambda i, j: (0, 0)),
            pl.BlockSpec((2, tm, bw), lambda i, j: (0, i, j)),
            pl.BlockSpec((None, D, bw), lambda i, j: (j, 0, 0)),
            pl.BlockSpec((None, D, bw), lambda i, j: (j + nb, 0, 0)),
            pl.BlockSpec((bw, D), lambda i, j: (j, 0)),
        ],
        out_specs=[
            pl.BlockSpec((tm, D), row),
            pl.BlockSpec((1, D), lambda i, j: (0, 0)),
            pl.BlockSpec((tm, D), row),
            pl.BlockSpec((tm, D), row),
            pl.BlockSpec((2, tm, bw), lambda i, j: (0, i, j)),
            pl.BlockSpec((tm, bw), lambda i, j: (i, j)),
        ],
        out_shape=[
            jax.ShapeDtypeStruct((T, D), F32),
            jax.ShapeDtypeStruct((1, D), F32),
            jax.ShapeDtypeStruct((T, D), BF16),
            jax.ShapeDtypeStruct((T, D), BF16),
            jax.ShapeDtypeStruct((2, T, nb * bw), BF16),
            jax.ShapeDtypeStruct((T, nb * bw), BF16),
        ],
        scratch_shapes=[pltpu.VMEM((tm, D), F32)],
        compiler_params=_params(("arbitrary", "arbitrary")),
    )(dh, x, gain, gu, wgu, wgu, wd)


def _wgrad(a, b, *, grid, a_spec, b_spec, out_spec, out_shape, acc_shape, name, split_lanes=0):
    nk = grid[2]

    def body(a_ref, b_ref, o_ref, acc):
        k = pl.program_id(2)

        @pl.when(k == 0)
        def _():
            acc[...] = jnp.zeros_like(acc)

        acc[...] += _mm_tn(a_ref[...].astype(BF16), b_ref[...].astype(BF16))

        @pl.when(k == nk - 1)
        def _():
            if split_lanes:
                for e in range(o_ref.shape[0]):
                    o_ref[e] = acc[:, e * split_lanes:(e + 1) * split_lanes].astype(o_ref.dtype)
            else:
                o_ref[...] = acc[...].astype(o_ref.dtype)

    return pl.pallas_call(
        body, name=name, grid=grid, in_specs=[a_spec, b_spec], out_specs=out_spec,
        out_shape=jax.ShapeDtypeStruct(out_shape, BF16),
        scratch_shapes=[pltpu.VMEM(acc_shape, F32)],
        compiler_params=_params(("arbitrary", "arbitrary", "arbitrary")),
    )(a, b)


def _wgrad_gate_up(n, dgu, *, tk, name):
    T, D = n.shape
    tk = min(tk, T)
    bw = FF_SHARD_PAD * 2
    nb = dgu.shape[2] // bw
    return _wgrad(
        n, dgu, grid=(1, 2 * nb, T // tk), name=name,
        a_spec=pl.BlockSpec((tk, D), lambda m, c, k: (k, 0)),
        b_spec=pl.BlockSpec((None, tk, bw), lambda m, c, k: (c // nb, k, c % nb)),
        out_spec=pl.BlockSpec((None, D, bw), lambda m, c, k: (c, 0, 0)),
        out_shape=(2 * nb, D, bw), acc_shape=(D, bw))


def _wgrad_down(hid, df, *, tk, name):
    T, D = df.shape
    tk = min(tk, T)
    bw = FF_SHARD_PAD * 2
    nb = hid.shape[1] // bw
    return _wgrad(
        hid, df, grid=(nb, 1, T // tk), name=name,
        a_spec=pl.BlockSpec((tk, bw), lambda m, c, k: (k, m)),
        b_spec=pl.BlockSpec((tk, D), lambda m, c, k: (k, 0)),
        out_spec=pl.BlockSpec((bw, D), lambda m, c, k: (m, 0)),
        out_shape=(nb * bw, D), acc_shape=(bw, D))


def _wgrad_in(un, dproj, *, tk, name):
    T, D = un.shape
    tk = min(tk, T)
    bw = dproj.shape[1] // N_DEV
    return _wgrad(
        un, dproj, grid=(1, N_DEV, T // tk), name=name,
        a_spec=pl.BlockSpec((tk, D), lambda m, c, k: (k, 0)),
        b_spec=pl.BlockSpec((tk, bw), lambda m, c, k: (k, c)),
        out_spec=pl.BlockSpec((None, D, bw), lambda m, c, k: (c, 0, 0)),
        out_shape=(N_DEV, D, bw), acc_shape=(D, bw))


def _wgrad_full(a, b, *, tk, name, split_lanes=0):
    T, M = a.shape
    tk = min(tk, T)
    N = b.shape[1]
    if split_lanes:
        out_shape = (N // split_lanes, M, split_lanes)
        out_spec = pl.BlockSpec(out_shape, lambda m, c, k: (0, 0, 0))
    else:
        out_shape = (M, N)
        out_spec = pl.BlockSpec(out_shape, lambda m, c, k: (0, 0))
    return _wgrad(
        a, b, grid=(1, 1, T // tk), name=name,
        a_spec=pl.BlockSpec((tk, M), lambda m, c, k: (k, 0)),
        b_spec=pl.BlockSpec((tk, N), lambda m, c, k: (k, 0)),
        out_spec=out_spec, out_shape=out_shape, acc_shape=(M, N), split_lanes=split_lanes)


def _loss_bwd(h, target, gain, *, tm, name):
    T, D = h.shape
    tm = min(tm, T)

    def body(h_ref, t_ref, gain_ref, dh_ref, loss_ref, dgain_ref):
        @pl.when(pl.program_id(0) == 0)
        def _():
            loss_ref[...] = jnp.zeros_like(loss_ref)
            dgain_ref[...] = jnp.zeros_like(dgain_ref)

        xf = h_ref[...]
        gain = gain_ref[...]
        err = xf * _rstd(xf) * gain - t_ref[...]
        loss_ref[...] += 0.5 * jnp.sum(jnp.mean(err * err, axis=-1, keepdims=True), axis=0, keepdims=True)
        dx, dgain = _rms_bwd(xf, gain, err * (1.0 / D))
        dh_ref[...] = dx
        dgain_ref[...] += dgain

    row = lambda i: (i, 0)
    fixed = lambda i: (0, 0)
    return pl.pallas_call(
        body, name=name, grid=(T // tm,),
        in_specs=[pl.BlockSpec((tm, D), row), pl.BlockSpec((tm, D), row), pl.BlockSpec((1, D), fixed)],
        out_specs=[pl.BlockSpec((tm, D), row), pl.BlockSpec((1, 128), fixed), pl.BlockSpec((1, D), fixed)],
        out_shape=[jax.ShapeDtypeStruct((T, D), F32), jax.ShapeDtypeStruct((1, 128), F32),
                   jax.ShapeDtypeStruct((1, D), F32)],
        compiler_params=_params(("arbitrary",)),
    )(h, target, gain)


def _inproj_fwd(h, gain, w_in, *, tm, name):
    T, D = h.shape
    tm = min(tm, T)
    nb, bw = w_in.shape[0], w_in.shape[2]

    def body(h_ref, gain_ref, w_ref, un_ref, proj_ref):
        @pl.when(pl.program_id(1) == 0)
        def _():
            xf = h_ref[...]
            un_ref[...] = (xf * _rstd(xf) * gain_ref[...]).astype(BF16)

        proj_ref[...] = _mm(un_ref[...], w_ref[...])

    return pl.pallas_call(
        body, name=name, grid=(T // tm, nb),
        in_specs=[
            pl.BlockSpec((tm, D), lambda i, j: (i, 0)),
            pl.BlockSpec((1, D), lambda i, j: (0, 0)),
            pl.BlockSpec((None, D, bw), lambda i, j: (j, 0, 0)),
        ],
        out_specs=[pl.BlockSpec((tm, D), lambda i, j: (i, 0)), pl.BlockSpec((tm, bw), lambda i, j: (i, j))],
        out_shape=[jax.ShapeDtypeStruct((T, D), BF16), jax.ShapeDtypeStruct((T, nb * bw), F32)],
        compiler_params=_params(("arbitrary", "arbitrary")),
    )(h, gain, w_in)


def _inproj_bwd(dproj, dh, h, gain, w_in, *, tm, name):
    T, D = h.shape
    tm = min(tm, T)
    nb, bw = w_in.shape[0], w_in.shape[2]

    def body(dp_ref, dh_ref, h_ref, gain_ref, w_ref, dx_ref, dgain_ref, acc):
        i, j = pl.program_id(0), pl.program_id(1)

        @pl.when(j == 0)
        def _():
            acc[...] = jnp.zeros_like(acc)

        @pl.when((i == 0) & (j == 0))
        def _():
            dgain_ref[...] = jnp.zeros_like(dgain_ref)

        acc[...] += _mm_nt(dp_ref[...], w_ref[...])

        @pl.when(j == nb - 1)
        def _():
            dx, dgain = _rms_bwd(h_ref[...], gain_ref[...], acc[...])
            dx_ref[...] = dh_ref[...] + dx
            dgain_ref[...] += dgain

    row = lambda i, j: (i, 0)
    return pl.pallas_call(
        body, name=name, grid=(T // tm, nb),
        in_specs=[
            pl.BlockSpec((tm, bw), lambda i, j: (i, j)),
            pl.BlockSpec((tm, D), row),
            pl.BlockSpec((tm, D), row),
            pl.BlockSpec((1, D), lambda i, j: (0, 0)),
            pl.BlockSpec((None, D, bw), lambda i, j: (j, 0, 0)),
        ],
        out_specs=[pl.BlockSpec((tm, D), row), pl.BlockSpec((1, D), lambda i, j: (0, 0))],
        out_shape=[jax.ShapeDtypeStruct((T, D), F32), jax.ShapeDtypeStruct((1, D), F32)],
        scratch_shapes=[pltpu.VMEM((tm, D), F32)],
        compiler_params=_params(("arbitrary", "arbitrary")),
    )(dproj, dh, h, gain, w_in)


def _window_sum(x, row, doublings, *, backward):
    T = x.shape[0]
    s = x
    for k in range(doublings):
        sh = 1 << k
        if backward:
            s = s + jnp.where(row < T - sh, pltpu.roll(s, T - sh, 0), 0.0)
        else:
            s = s + jnp.where(row >= sh, pltpu.roll(s, sh, 0), 0.0)
    return s


def _pool_fwd(proj, w_group, scale, *, name):
    T = proj.shape[0]

    def body(xp_ref, w_ref, scale_ref, p_ref):
        row = lax.broadcasted_iota(jnp.int32, (T, POOL_GROUP), 0)
        for gi, window in enumerate(POOL_WINDOWS):
            cols = slice(gi * POOL_GROUP, (gi + 1) * POOL_GROUP)
            x = xp_ref[:, cols]
            inv_count = 1.0 / jnp.minimum(row + 1, window).astype(F32)
            yc = _window_sum(x, row, gi + 1, backward=False) * inv_count - x
            pre = _mm(yc.astype(BF16), w_ref[gi].astype(BF16))
            p_ref[:, cols] = pre * scale_ref[:, cols]

    return pl.pallas_call(
        body, name=name, grid=(1,),
        in_specs=[
            pl.BlockSpec((T, POOL_WIDTH), lambda i: (0, 0)),
            pl.BlockSpec(w_group.shape, lambda i: (0, 0, 0)),
            pl.BlockSpec((1, POOL_WIDTH), lambda i: (0, 0)),
        ],
        out_specs=pl.BlockSpec((T, POOL_WIDTH), lambda i: (0, 0)),
        out_shape=jax.ShapeDtypeStruct((T, POOL_WIDTH), F32),
        compiler_params=_params(("arbitrary",)),
    )(proj, w_group, scale)


def _pool_bwd(dp, proj, w_group, scale, *, name):
    T = proj.shape[0]

    def body(dp_ref, xp_ref, w_ref, scale_ref, dxp_ref, dw_ref, dscale_ref):
        row = lax.broadcasted_iota(jnp.int32, (T, POOL_GROUP), 0)
        for gi, window in enumerate(POOL_WINDOWS):
            cols = slice(gi * POOL_GROUP, (gi + 1) * POOL_GROUP)
            x = xp_ref[:, cols]
            inv_count = 1.0 / jnp.minimum(row + 1, window).astype(F32)
            yc = (_window_sum(x, row, gi + 1, backward=False) * inv_count - x).astype(BF16)
            w = w_ref[gi].astype(BF16)
            pre = _mm(yc, w)
            dpg = dp_ref[:, cols]
            dscale_ref[:, cols] = jnp.sum(dpg * pre, axis=0, keepdims=True)
            dpre = (dpg * scale_ref[:, cols]).astype(BF16)
            dw_ref[gi] = _mm_tn(yc, dpre)
            dyc = _mm_nt(dpre, w)
            dxp_ref[:, cols] = _window_sum(dyc * inv_count, row, gi + 1, backward=True) - dyc

    return pl.pallas_call(
        body, name=name, grid=(1,),
        in_specs=[
            pl.BlockSpec((T, POOL_WIDTH), lambda i: (0, 0)),
            pl.BlockSpec((T, POOL_WIDTH), lambda i: (0, 0)),
            pl.BlockSpec(w_group.shape, lambda i: (0, 0, 0)),
            pl.BlockSpec((1, POOL_WIDTH), lambda i: (0, 0)),
        ],
        out_specs=[
            pl.BlockSpec((T, POOL_WIDTH), lambda i: (0, 0)),
            pl.BlockSpec(w_group.shape, lambda i: (0, 0, 0)),
            pl.BlockSpec((1, POOL_WIDTH), lambda i: (0, 0)),
        ],
        out_shape=[jax.ShapeDtypeStruct((T, POOL_WIDTH), F32), jax.ShapeDtypeStruct(w_group.shape, F32),
                   jax.ShapeDtypeStruct((1, POOL_WIDTH), F32)],
        compiler_params=_params(("arbitrary",)),
    )(dp, proj, w_group, scale)


def _log_sigmoids(z):
    t = jnp.log(1.0 + jnp.exp(-jnp.abs(z)))
    return jnp.minimum(z, 0.0) - t, -jnp.maximum(z, 0.0) - t


def _tri_sum(x, tri):
    hi = x.astype(BF16)
    lo = (x - hi.astype(F32)).astype(BF16)
    return _mm(hi, tri) + _mm(lo, tri)


def _attn_specs(T, tq):
    q_col = POOL_WIDTH // HEAD_PAIR
    k_col = q_col + SB_WIDTH // HEAD_PAIR
    v_col = k_col + SB_WIDTH // HEAD_PAIR
    return [
        pl.BlockSpec((tq, HEAD_PAIR), lambda p, i: (i, q_col + p)),
        pl.BlockSpec((T, HEAD_PAIR), lambda p, i: (0, k_col + p)),
        pl.BlockSpec((T, HEAD_PAIR), lambda p, i: (0, v_col + p)),
    ]


def _attn_fwd(proj, *, name):
    T = proj.shape[0]
    tq = ATTN_BLOCK

    def body(q_ref, k_ref, v_ref, o_ref, lt_ref, kb_scr, vb_scr):
        qi = pl.program_id(1)

        @pl.when(qi == 0)
        def _():
            kb_scr[...] = k_ref[...].astype(BF16)
            vb_scr[...] = v_ref[...].astype(BF16)

        head0 = lax.broadcasted_iota(jnp.int32, (tq, HEAD_PAIR), 1) < HEAD_DIM
        q = q_ref[...] * ATTN_SCALE
        qs = (jnp.where(head0, q, 0.0).astype(BF16), jnp.where(head0, 0.0, q).astype(BF16))
        r = lax.broadcasted_iota(jnp.int32, (tq, tq), 0)
        c = lax.broadcasted_iota(jnp.int32, (tq, tq), 1)
        later = (r > c).astype(BF16)
        causal = c < r

        def block(kj, carry, valid):
            off = pl.multiple_of(kj * tq, tq)
            kb = kb_scr[pl.ds(off, tq), :]
            vb = vb_scr[pl.ds(off, tq), :]
            out = []
            for h in range(2):
                run, o = carry[2 * h], carry[2 * h + 1]
                z = _mm_nt(qs[h], kb)
                lb, lm = _log_sigmoids(z)
                if valid is not None:
                    lm = jnp.where(valid, lm, 0.0)
                a = jnp.exp(lb + run + _tri_sum(lm, later))
                if valid is not None:
                    a = jnp.where(valid, a, 0.0)
                out += [run + jnp.sum(lm, axis=1, keepdims=True), o + _mm(a.astype(BF16), vb)]
            return tuple(out)

        zero = (jnp.zeros((tq, 1), F32), jnp.zeros((tq, HEAD_PAIR), F32))
        carry = block(qi, zero + zero, causal)
        carry = lax.fori_loop(0, qi, lambda it, cr: block(qi - 1 - it, cr, None), carry)
        o_ref[...] = jnp.where(head0, carry[1], carry[3])
        lt_ref[...] = jnp.where(head0, carry[0], carry[2])

    out_spec = pl.BlockSpec((tq, HEAD_PAIR), lambda p, i: (i, p))
    return pl.pallas_call(
        body, name=name, grid=(N_HEADS // 2, T // tq),
        in_specs=_attn_specs(T, tq), out_specs=[out_spec, out_spec],
        out_shape=[jax.ShapeDtypeStruct((T, SB_WIDTH), F32), jax.ShapeDtypeStruct((T, SB_WIDTH), F32)],
        scratch_shapes=[pltpu.VMEM((T, HEAD_PAIR), BF16), pltpu.VMEM((T, HEAD_PAIR), BF16)],
        compiler_params=_params(("arbitrary", "arbitrary")),
    )(proj, proj, proj)


def _attn_bwd(proj, do, ltot, *, name):
    T = proj.shape[0]
    tq = ATTN_BLOCK

    def body(q_ref, k_ref, v_ref, do_ref, lt_ref, dq_ref, dk_ref, dv_ref, kb_scr, vb_scr):
        qi = pl.program_id(1)

        @pl.when(qi == 0)
        def _():
            kb_scr[...] = k_ref[...].astype(BF16)
            vb_scr[...] = v_ref[...].astype(BF16)
            dk_ref[...] = jnp.zeros_like(dk_ref)
            dv_ref[...] = jnp.zeros_like(dv_ref)

        head0 = lax.broadcasted_iota(jnp.int32, (tq, HEAD_PAIR), 1) < HEAD_DIM
        q, do_, lt = q_ref[...] * ATTN_SCALE, do_ref[...], lt_ref[...]
        qs = (jnp.where(head0, q, 0.0).astype(BF16), jnp.where(head0, 0.0, q).astype(BF16))
        dos = (jnp.where(head0, do_, 0.0).astype(BF16), jnp.where(head0, 0.0, do_).astype(BF16))
        lts = (jnp.max(jnp.where(head0, lt, -jnp.inf), axis=1, keepdims=True),
               jnp.max(jnp.where(head0, -jnp.inf, lt), axis=1, keepdims=True))
        r = lax.broadcasted_iota(jnp.int32, (tq, tq), 0)
        c = lax.broadcasted_iota(jnp.int32, (tq, tq), 1)
        upto = (r <= c).astype(BF16)
        before = (r < c).astype(BF16)
        causal = c < r

        def block(kj, carry, valid):
            off = pl.multiple_of(kj * tq, tq)
            kb = kb_scr[pl.ds(off, tq), :]
            vb = vb_scr[pl.ds(off, tq), :]
            dk_blk = jnp.zeros((tq, HEAD_PAIR), F32)
            dv_blk = jnp.zeros((tq, HEAD_PAIR), F32)
            out = []
            for h in range(2):
                run_lm, run_e, dq = carry[3 * h:3 * h + 3]
                z = _mm_nt(qs[h], kb)
                lb, lm = _log_sigmoids(z)
                if valid is not None:
                    lm = jnp.where(valid, lm, 0.0)
                a = jnp.exp(lb + (lts[h] - run_lm - _tri_sum(lm, upto)))
                if valid is not None:
                    a = jnp.where(valid, a, 0.0)
                e = _mm_nt(dos[h], vb) * a
                beta = jnp.exp(lb)
                dz = e * (1.0 - beta) - (run_e + _tri_sum(e, before)) * beta
                if valid is not None:
                    dz = jnp.where(valid, dz, 0.0)
                dz = dz.astype(BF16)
                dk_blk += _mm_tn(dz, qs[h])
                dv_blk += _mm_tn(a.astype(BF16), dos[h])
                out += [run_lm + jnp.sum(lm, axis=1, keepdims=True), run_e + jnp.sum(e, axis=1, keepdims=True),
                        dq + _mm(dz, kb)]
            dk_ref[pl.ds(off, tq), :] += dk_blk
            dv_ref[pl.ds(off, tq), :] += dv_blk
            return tuple(out)

        zero = (jnp.zeros((tq, 1), F32), jnp.zeros((tq, 1), F32), jnp.zeros((tq, HEAD_PAIR), F32))
        carry = lax.fori_loop(0, qi, lambda kj, cr: block(kj, cr, None), zero + zero)
        carry = block(qi, carry, causal)
        dq_ref[...] = jnp.where(head0, carry[2], carry[5]) * ATTN_SCALE

    blk = pl.BlockSpec((tq, HEAD_PAIR), lambda p, i: (i, p))
    seq = pl.BlockSpec((T, HEAD_PAIR), lambda p, i: (0, p))
    return pl.pallas_call(
        body, name=name, grid=(N_HEADS // 2, T // tq),
        in_specs=_attn_specs(T, tq) + [blk, blk], out_specs=[blk, seq, seq],
        out_shape=[jax.ShapeDtypeStruct((T, SB_WIDTH), F32)] * 3,
        scratch_shapes=[pltpu.VMEM((T, HEAD_PAIR), BF16), pltpu.VMEM((T, HEAD_PAIR), BF16)],
        compiler_params=_params(("arbitrary", "arbitrary")),
    )(proj, proj, proj, do, ltot)


def _branch(act_bf16, w_ref):
    return jnp.concatenate([_mm(act_bf16, w_ref[e]) for e in range(w_ref.shape[0])], axis=1)


def _mix_specs(T, D, tm, wbp, w_out):
    gate_col = (POOL_WIDTH + 3 * SB_WIDTH) // D
    row = lambda i: (i, 0)
    return [
        pl.BlockSpec((tm, D), row),
        pl.BlockSpec((tm, POOL_WIDTH), row),
        pl.BlockSpec((tm, SB_WIDTH), row),
        pl.BlockSpec((tm, D), lambda i: (i, gate_col)),
        pl.BlockSpec((tm, D), lambda i: (i, gate_col + 1)),
        pl.BlockSpec(wbp.shape, lambda i: (0, 0, 0)),
        pl.BlockSpec(wbp.shape, lambda i: (0, 0, 0)),
        pl.BlockSpec(w_out.shape, lambda i: (0, 0)),
    ]


def _mix_fwd(h, p, o, proj, wbp, wba, w_out, *, tm, name):
    T, D = h.shape
    tm = min(tm, T)

    def body(h_ref, p_ref, o_ref, glp_ref, gls_ref, wbp_ref, wba_ref, wout_ref, hout_ref, m_ref):
        yp = _branch(p_ref[...].astype(BF16), wbp_ref)
        ys = _branch(o_ref[...].astype(BF16), wba_ref)
        m = (jax.nn.sigmoid(glp_ref[...]) * yp + jax.nn.sigmoid(gls_ref[...]) * ys).astype(BF16)
        m_ref[...] = m
        hout_ref[...] = h_ref[...] + _mm(m, wout_ref[...])

    row = lambda i: (i, 0)
    return pl.pallas_call(
        body, name=name, grid=(T // tm,),
        in_specs=_mix_specs(T, D, tm, wbp, w_out),
        out_specs=[pl.BlockSpec((tm, D), row), pl.BlockSpec((tm, D), row)],
        out_shape=[jax.ShapeDtypeStruct((T, D), F32), jax.ShapeDtypeStruct((T, D), BF16)],
        compiler_params=_params(("arbitrary",)),
    )(h, p, o, proj, proj, wbp, wba, w_out)


def _mix_bwd(dh, p, o, proj, wbp, wba, w_out, *, tm, name):
    T, D = dh.shape
    tm = min(tm, T)
    bw = wbp.shape[2]

    def body(dh_ref, p_ref, o_ref, glp_ref, gls_ref, wbp_ref, wba_ref, wout_ref,
             dyp_ref, dys_ref, dp_ref, do_ref, dgl_ref):
        dm = _mm_nt(dh_ref[...].astype(BF16), wout_ref[...])
        yp = _branch(p_ref[...].astype(BF16), wbp_ref)
        ys = _branch(o_ref[...].astype(BF16), wba_ref)
        gp = jax.nn.sigmoid(glp_ref[...])
        gs = jax.nn.sigmoid(gls_ref[...])
        dyp = (dm * gp).astype(BF16)
        dys = (dm * gs).astype(BF16)
        dyp_ref[...] = dyp
        dys_ref[...] = dys
        dgl_ref[:, :D] = (dm * yp * gp * (1.0 - gp)).astype(BF16)
        dgl_ref[:, D:] = (dm * ys * gs * (1.0 - gs)).astype(BF16)
        dp = jnp.zeros(dp_ref.shape, F32)
        do_ = jnp.zeros(do_ref.shape, F32)
        for e in range(wbp_ref.shape[0]):
            dp += _mm_nt(dyp[:, e * bw:(e + 1) * bw], wbp_ref[e])
            do_ += _mm_nt(dys[:, e * bw:(e + 1) * bw], wba_ref[e])
        dp_ref[...] = dp
        do_ref[...] = do_

    row = lambda i: (i, 0)
    return pl.pallas_call(
        body, name=name, grid=(T // tm,),
        in_specs=_mix_specs(T, D, tm, wbp, w_out),
        out_specs=[pl.BlockSpec((tm, D), row), pl.BlockSpec((tm, D), row), pl.BlockSpec((tm, POOL_WIDTH), row),
                   pl.BlockSpec((tm, SB_WIDTH), row), pl.BlockSpec((tm, 2 * D), row)],
        out_shape=[jax.ShapeDtypeStruct((T, D), BF16), jax.ShapeDtypeStruct((T, D), BF16),
                   jax.ShapeDtypeStruct((T, POOL_WIDTH), F32), jax.ShapeDtypeStruct((T, SB_WIDTH), F32),
                   jax.ShapeDtypeStruct((T, 2 * D), BF16)],
        compiler_params=_params(("arbitrary",)),
    )(dh, p, o, proj, proj, wbp, wba, w_out)


def _adamw(w, g, m, v, *, name):
    R, C = w.shape
    tr = R if R * C <= 512 * 1024 else 256

    def body(w_ref, g_ref, m_ref, v_ref, d_ref, nm_ref, nv_ref):
        g_ = g_ref[...]
        m_ = ADAM_B1 * m_ref[...] + (1.0 - ADAM_B1) * g_
        v_ = ADAM_B2 * v_ref[...] + (1.0 - ADAM_B2) * (g_ * g_)
        m_hat = m_ / (1.0 - ADAM_B1 ** ADAM_STEP)
        v_hat = v_ / (1.0 - ADAM_B2 ** ADAM_STEP)
        d_ref[...] = -ADAM_LR * (m_hat / (jnp.sqrt(v_hat) + ADAM_EPS) + ADAM_WD * w_ref[...])
        nm_ref[...] = m_
        nv_ref[...] = v_

    spec = pl.BlockSpec((tr, C), lambda i: (i, 0))
    return pl.pallas_call(
        body, name=name, grid=(R // tr,), in_specs=[spec] * 4, out_specs=[spec] * 3,
        out_shape=[jax.ShapeDtypeStruct((R, C), F32)] * 3,
        compiler_params=_params(("arbitrary",)),
    )(w, g, m, v)


def _position():
    return lax.axis_index("x"), lax.axis_index("y"), lax.axis_index("c")


def _all_gather(shards, *, name, collective_id):
    n = len(shards)

    def body(*refs):
        ins, outs = refs[:n], refs[n:2 * n]
        send_sems, recv_sems, local_sems = refs[2 * n:]
        x, y, c = _position()
        me, sibling = (x, y, c), (x, y, 1 - c)
        chips = [(1 - x, y), (x, 1 - y), (1 - x, 1 - y)]

        barrier = pltpu.get_barrier_semaphore()
        for peer in [sibling] + [(*chip, c) for chip in chips]:
            pl.semaphore_signal(barrier, inc=1, device_id=peer, device_id_type=MESH)
        pl.semaphore_wait(barrier, 4)

        def block(a, pos):
            return outs[a].at[4 * pos[0] + 2 * pos[1] + pos[2]]

        def copy(a, k, pos, to, src=None):
            return pltpu.make_async_remote_copy(
                src_ref=block(a, pos) if src is None else src, dst_ref=block(a, pos),
                send_sem=send_sems.at[7 * a + k], recv_sem=recv_sems.at[7 * a + k],
                device_id=to, device_id_type=MESH)

        started = []
        for a in range(n):
            mine = pltpu.make_async_copy(ins[a], block(a, me), local_sems.at[a])
            mine.start()
            started.append(mine)
        sends = []
        for a in range(n):
            sends += [copy(a, 1 + j, me, (*chip, c), src=ins[a]) for j, chip in enumerate(chips)]
            sends.append(copy(a, 0, me, sibling, src=ins[a]))
        for cp in sends:
            cp.start()
        for j, chip in enumerate(chips):
            for a in range(n):
                copy(a, 1 + j, (*chip, c), me).wait_recv()
                passed = copy(a, 4 + j, (*chip, c), sibling)
                passed.start()
                sends.append(passed)
        for a in range(n):
            copy(a, 0, sibling, me).wait_recv()
            for j, chip in enumerate(chips):
                copy(a, 4 + j, (*chip, 1 - c), me).wait_recv()
        for cp in sends:
            cp.wait_send()
        for cp in started:
            cp.wait()

    return pl.kernel(
        body, name=name,
        out_type=[jax.ShapeDtypeStruct((N_DEV,) + s.shape, s.dtype) for s in shards],
        mesh=plsc.ScalarSubcoreMesh(axis_name="sequencer", num_cores=1),
        scratch_types=[pltpu.SemaphoreType.DMA((7 * n,)), pltpu.SemaphoreType.DMA((7 * n,)),
                       pltpu.SemaphoreType.DMA((n,))],
        compiler_params=pltpu.CompilerParams(collective_id=collective_id),
    )(*shards)


def _chip_sums(grads, *, name):
    _, R, C = grads.shape
    rc = 128 if R % 128 == 0 else R

    def body(g_ref, partial, out_ref, mine, theirs, send_sems, recv_sems, local_sems):
        x, y, c = _position()
        my_chip = 2 * x + y

        def swap(s):
            return pltpu.make_async_remote_copy(
                src_ref=g_ref.at[2 * s + (1 - c)], dst_ref=theirs.at[s],
                send_sem=send_sems.at[s], recv_sem=recv_sems.at[s],
                device_id=(x, y, 1 - c), device_id_type=MESH)

        def load(s):
            return pltpu.make_async_copy(g_ref.at[2 * s + c], mine.at[s], local_sems.at[s])

        for s in range(4):
            swap(s).start()
            load(s).start()
        for s in range(4):
            load(s).wait()
            swap(s).wait_recv()

        def chip_sum(chip, rows):
            return mine[chip, rows, :].astype(F32) + theirs[chip, rows, :].astype(F32)

        for j in (1, 2, 3):
            @pl.loop(0, R // rc)
            def _(t):
                rows = pl.ds(pl.multiple_of(t * rc, rc), rc)
                partial[j - 1, rows, :] = chip_sum(my_chip ^ j, rows).astype(BF16)

        @pl.loop(0, R // rc)
        def _(t):
            rows = pl.ds(pl.multiple_of(t * rc, rc), rc)
            out_ref[rows, :] = chip_sum(my_chip, rows)

        for s in range(4):
            swap(s).wait_send()

    vmem = pl.BlockSpec(memory_space=pltpu.VMEM)
    return pl.pallas_call(
        body, name=name,
        in_specs=[pl.BlockSpec(memory_space=pl.ANY)], out_specs=[vmem, vmem],
        out_shape=[jax.ShapeDtypeStruct((3, R, C), BF16), jax.ShapeDtypeStruct((R, C), F32)],
        scratch_shapes=[
            pltpu.VMEM((4, R, C), BF16), pltpu.VMEM((4, R, C), BF16),
            pltpu.SemaphoreType.DMA((4,)), pltpu.SemaphoreType.DMA((4,)), pltpu.SemaphoreType.DMA((4,)),
        ],
        compiler_params=_params(),
    )(grads)


def _cross_chips(partials, *, name, collective_id):
    n = len(partials)

    def body(*refs):
        ins, outs = refs[:n], refs[n:2 * n]
        send_sems, recv_sems = refs[2 * n:]
        x, y, c = _position()
        my_chip = 2 * x + y
        peers = [((my_chip ^ j) // 2, (my_chip ^ j) % 2, c) for j in (1, 2, 3)]

        barrier = pltpu.get_barrier_semaphore()
        for peer in peers:
            pl.semaphore_signal(barrier, inc=1, device_id=peer, device_id_type=MESH)
        pl.semaphore_wait(barrier, 3)

        copies = [
            pltpu.make_async_remote_copy(
                src_ref=ins[a].at[j], dst_ref=outs[a].at[j],
                send_sem=send_sems.at[3 * a + j], recv_sem=recv_sems.at[3 * a + j],
                device_id=peers[j], device_id_type=MESH)
            for a in range(n) for j in range(3)]
        for cp in copies:
            cp.start()
        for cp in copies:
            cp.wait_recv()
        for cp in copies:
            cp.wait_send()

    return pl.kernel(
        body, name=name,
        out_type=[jax.ShapeDtypeStruct(p.shape, p.dtype) for p in partials],
        mesh=plsc.ScalarSubcoreMesh(axis_name="sequencer", num_cores=1),
        scratch_types=[pltpu.SemaphoreType.DMA((3 * n,)), pltpu.SemaphoreType.DMA((3 * n,))],
        compiler_params=pltpu.CompilerParams(collective_id=collective_id),
    )(*partials)


def _owner_sum(own, landed, *, name):
    R, C = own.shape
    tr = R if R * C <= 512 * 1024 else 256

    def body(own_ref, landed_ref, out_ref):
        total = own_ref[...]
        for j in range(3):
            total = total + landed_ref[j].astype(F32)
        out_ref[...] = total

    return pl.pallas_call(
        body, name=name, grid=(R // tr,),
        in_specs=[pl.BlockSpec((tr, C), lambda i: (i, 0)), pl.BlockSpec((3, tr, C), lambda i: (0, i, 0))],
        out_specs=pl.BlockSpec((tr, C), lambda i: (i, 0)),
        out_shape=jax.ShapeDtypeStruct((R, C), F32),
        compiler_params=_params(("arbitrary",)),
    )(own, landed)


def _sum_devices(gathered, *, name):
    _, R, C = gathered.shape

    def body(in_ref, out_ref):
        total = in_ref[0]
        for d in range(1, N_DEV):
            total = total + in_ref[d]
        out_ref[...] = total

    return pl.pallas_call(
        body, name=name, grid=(1,),
        in_specs=[pl.BlockSpec((N_DEV, R, C), lambda i: (0, 0, 0))],
        out_specs=pl.BlockSpec((R, C), lambda i: (0, 0)),
        out_shape=jax.ShapeDtypeStruct((R, C), F32),
        compiler_params=_params(("arbitrary",)),
    )(gathered)


def _local_step(x, target, norms, pool_w_group, pool_scale, wgu1, wd1, w_in, wbp, wba, w_out, wgu2, wd2, exchange):
    n1g, nmg, n2g, nfg = norms
    D = x.shape[1]
    h1, gu1 = _ffn_fwd(x, n1g, wgu1, wd1, tm=512, name="ffn1_fwd")
    un, proj = _inproj_fwd(h1, nmg, w_in, tm=1024, name="inproj_fwd")
    p = _pool_fwd(proj, pool_w_group, pool_scale, name="pool_fwd")
    o, ltot = _attn_fwd(proj, name="attn_fwd")
    h2, m = _mix_fwd(h1, p, o, proj, wbp, wba, w_out, tm=256, name="mix_fwd")
    h3, gu2 = _ffn_fwd(h2, n2g, wgu2, wd2, tm=512, name="ffn2_fwd")
    dh3, loss, d_nf = _loss_bwd(h3, target, nfg, tm=256, name="loss_bwd")

    dh2, d_n2, n2, df2, dgu2, hid2 = _ffn_bwd(dh3, h2, n2g, gu2, wgu2, wd2, tm=256, name="ffn2_bwd")
    d_wgu2 = _wgrad_gate_up(n2, dgu2, tk=512, name="ffn2_wgrad_gate_up")
    d_wd2 = _wgrad_down(hid2, df2, tk=512, name="ffn2_wgrad_down")
    g_wgu2, g_wd2 = exchange("ffn2", [d_wgu2, d_wd2.reshape(N_DEV, FF_SHARD_PAD, D)])

    dyp, dys, dp, do, dgl = _mix_bwd(dh2, p, o, proj, wbp, wba, w_out, tm=256, name="mix_bwd")
    d_wout = _wgrad_full(m, dh2, tk=512, name="wgrad_out")
    d_wbp = _wgrad_full(p, dyp, tk=512, name="wgrad_branch_pool", split_lanes=wbp.shape[2])
    d_wba = _wgrad_full(o, dys, tk=512, name="wgrad_branch_attn", split_lanes=wba.shape[2])
    g_wbp, g_wba, g_wout = exchange("mix", [d_wbp, d_wba, d_wout.reshape(N_DEV, D // N_DEV, D)])
    dxp, d_wgroup, d_scale = _pool_bwd(dp, proj, pool_w_group, pool_scale, name="pool_bwd")
    dq, dk, dv = _attn_bwd(proj, do, ltot, name="attn_bwd")
    dproj = jnp.concatenate([dxp.astype(BF16), dq.astype(BF16), dk.astype(BF16), dv.astype(BF16), dgl], axis=1)
    dh1, d_nm = _inproj_bwd(dproj, dh2, h1, nmg, w_in, tm=1024, name="inproj_bwd")
    d_win = _wgrad_in(un, dproj, tk=512, name="wgrad_in")
    g_win, = exchange("w_in", [d_win])

    dx, d_n1, n1, df1, dgu1, hid1 = _ffn_bwd(dh1, x, n1g, gu1, wgu1, wd1, tm=256, name="ffn1_bwd")
    d_wgu1 = _wgrad_gate_up(n1, dgu1, tk=512, name="ffn1_wgrad_gate_up")
    d_wd1 = _wgrad_down(hid1, df1, tk=512, name="ffn1_wgrad_down")
    g_wgu1, g_wd1 = exchange("ffn1", [d_wgu1, d_wd1.reshape(N_DEV, FF_SHARD_PAD, D)])

    sharded = (g_wgu1, g_wd1, g_win, g_wbp, g_wba, g_wout, g_wgu2, g_wd2)
    replicated = (d_n1, d_nm, d_n2, d_nf, d_scale, d_wgroup)
    return loss, dx, sharded, replicated


def _pad_gate_up(w):
    d = w.shape[0]
    w = w.astype(BF16).reshape(d, 2, FF_SHARD)
    return jnp.pad(w, ((0, 0), (0, 0), (0, FF_SHARD_PAD - FF_SHARD))).reshape(d, 2 * FF_SHARD_PAD)


def _unpad_gate_up(g):
    d = g.shape[0]
    return g.reshape(d, 2, FF_SHARD_PAD)[:, :, :FF_SHARD].reshape(d, 2 * FF_SHARD)


def _pad_down(w):
    return jnp.pad(w.astype(BF16), ((0, FF_SHARD_PAD - FF_SHARD), (0, 0)))


def kernel(x, ffn1_norm, ffn1_w_gate_up, ffn1_w_down, mix_norm, w_in, pool_w_group, pool_scale, w_branch_pool, w_branch_attn, w_out, ffn2_norm, ffn2_w_gate_up, ffn2_w_down, final_norm, loss_target, m_ffn1_norm, m_ffn1_w_gate_up, m_ffn1_w_down, m_mix_norm, m_w_in, m_pool_w_group, m_pool_scale, m_w_branch_pool, m_w_branch_attn, m_w_out, m_ffn2_norm, m_ffn2_w_gate_up, m_ffn2_w_down, m_final_norm, v_ffn1_norm, v_ffn1_w_gate_up, v_ffn1_w_down, v_mix_norm, v_w_in, v_pool_w_group, v_pool_scale, v_w_branch_pool, v_w_branch_attn, v_w_out, v_ffn2_norm, v_ffn2_w_gate_up, v_ffn2_w_down, v_final_norm):
    D = x.shape[-1]
    weights = dict(ffn1_norm=ffn1_norm, ffn1_w_gate_up=ffn1_w_gate_up, ffn1_w_down=ffn1_w_down, mix_norm=mix_norm,
                   w_in=w_in, pool_w_group=pool_w_group, pool_scale=pool_scale, w_branch_pool=w_branch_pool,
                   w_branch_attn=w_branch_attn, w_out=w_out, ffn2_norm=ffn2_norm, ffn2_w_gate_up=ffn2_w_gate_up,
                   ffn2_w_down=ffn2_w_down, final_norm=final_norm)
    first = dict(ffn1_norm=m_ffn1_norm, ffn1_w_gate_up=m_ffn1_w_gate_up, ffn1_w_down=m_ffn1_w_down,
                 mix_norm=m_mix_norm, w_in=m_w_in, pool_w_group=m_pool_w_group, pool_scale=m_pool_scale,
                 w_branch_pool=m_w_branch_pool, w_branch_attn=m_w_branch_attn, w_out=m_w_out,
                 ffn2_norm=m_ffn2_norm, ffn2_w_gate_up=m_ffn2_w_gate_up, ffn2_w_down=m_ffn2_w_down,
                 final_norm=m_final_norm)
    second = dict(ffn1_norm=v_ffn1_norm, ffn1_w_gate_up=v_ffn1_w_gate_up, ffn1_w_down=v_ffn1_w_down,
                  mix_norm=v_mix_norm, w_in=v_w_in, pool_w_group=v_pool_w_group, pool_scale=v_pool_scale,
                  w_branch_pool=v_w_branch_pool, w_branch_attn=v_w_branch_attn, w_out=v_w_out,
                  ffn2_norm=v_ffn2_norm, ffn2_w_gate_up=v_ffn2_w_gate_up, ffn2_w_down=v_ffn2_w_down,
                  final_norm=v_final_norm)
    order = list(weights)

    wgu1, wd1 = _all_gather([_pad_gate_up(ffn1_w_gate_up[0]), _pad_down(ffn1_w_down[0])],
                            name="all_gather_ffn1", collective_id=0)
    win_g, = _all_gather([w_in[0].astype(BF16)], name="all_gather_w_in", collective_id=1)
    wbp_g, wba_g, wout_g = _all_gather(
        [w_branch_pool[0].astype(BF16), w_branch_attn[0].astype(BF16), w_out[0].astype(BF16)],
        name="all_gather_mix", collective_id=2)
    wgu2, wd2 = _all_gather([_pad_gate_up(ffn2_w_gate_up[0]), _pad_down(ffn2_w_down[0])],
                            name="all_gather_ffn2", collective_id=3)
    wd1 = wd1.reshape(N_DEV * FF_SHARD_PAD, D)
    wd2 = wd2.reshape(N_DEV * FF_SHARD_PAD, D)
    wout_g = wout_g.reshape(D, D)

    cross_ids = {"ffn2": 4, "mix": 5, "w_in": 6, "ffn1": 7}

    def exchange(tag, group):
        sums = [_chip_sums(g, name=f"chip_sums_{tag}_{i}") for i, g in enumerate(group)]
        landed = _cross_chips([s[0] for s in sums], name="cross_chips_" + tag, collective_id=cross_ids[tag])
        return [(s[1], l) for s, l in zip(sums, landed)]

    norms = (ffn1_norm, mix_norm, ffn2_norm, final_norm.reshape(1, D))
    loss, dx, sharded, replicated = _local_step(
        x[0], loss_target[0], norms, pool_w_group[0], pool_scale, wgu1, wd1, win_g, wbp_g, wba_g, wout_g, wgu2, wd2,
        exchange)
    names = ["ffn1_w_gate_up", "ffn1_w_down", "w_in", "w_branch_pool", "w_branch_attn", "w_out",
             "ffn2_w_gate_up", "ffn2_w_down"]
    grads = {k: _owner_sum(own, landed, name="owner_sum_" + k)
             for k, (own, landed) in reversed(list(zip(names, sharded)))}
    for k in ("ffn1_w_gate_up", "ffn2_w_gate_up"):
        grads[k] = _unpad_gate_up(grads[k])
    for k in ("ffn1_w_down", "ffn2_w_down"):
        grads[k] = grads[k][:FF_SHARD]

    d_n1, d_nm, d_n2, d_nf, d_scale, d_wgroup = replicated
    small = ["ffn1_norm", "mix_norm", "ffn2_norm", "final_norm", "pool_scale", "pool_w_group"]
    def tile_rows(a):
        a = a.reshape(-1, 128)
        return jnp.pad(a, ((0, -a.shape[0] % 8), (0, 0)))

    pieces = [tile_rows(d) for d in (d_n1, d_nm, d_n2, d_nf, d_scale, d_wgroup)]
    rows = [weights[k].size // 128 for k in small]
    starts = [sum(p.shape[0] for p in pieces[:i]) for i in range(len(pieces) + 1)]
    slab = jnp.concatenate(pieces + [jnp.broadcast_to(loss, (8, 128))], axis=0)
    slabs, = _all_gather([slab], name="all_gather_replicated", collective_id=8)
    total = _sum_devices(slabs, name="sum_replicated")
    loss_out = total[starts[-1], 0]

    small_w = jnp.concatenate([tile_rows(weights[k]) for k in small], axis=0)
    small_m = jnp.concatenate([tile_rows(first[k]) for k in small], axis=0)
    small_v = jnp.concatenate([tile_rows(second[k]) for k in small], axis=0)
    small_out = _adamw(small_w, total[:starts[-1]], small_m, small_v, name="adamw_replicated")
    delta, new_m, new_v = {}, {}, {}
    for name_, start, n_rows in zip(small, starts, rows):
        shape = weights[name_].shape
        grads[name_] = total[start:start + n_rows].reshape(shape)
        delta[name_], new_m[name_], new_v[name_] = (a[start:start + n_rows].reshape(shape) for a in small_out)
    for name_ in order:
        if name_ in small:
            continue
        shape = weights[name_].shape
        two_d = shape[1:]
        out = _adamw(weights[name_].reshape(two_d), grads[name_], first[name_].reshape(two_d),
                     second[name_].reshape(two_d), name="adamw_" + name_)
        delta[name_], new_m[name_], new_v[name_] = (a.reshape(shape) for a in out)
        grads[name_] = grads[name_].reshape(shape)

    return (loss_out, dx[None], *[grads[k] for k in order], *[delta[k] for k in order],
            *[new_m[k] for k in order], *[new_v[k] for k in order])
```

```python
import functools

import jax
import jax.numpy as jnp
from jax import lax
from jax.experimental import pallas as pl
from jax.experimental.pallas import tpu as pltpu
from jax.experimental.pallas import tpu_sc as plsc

F32 = jnp.float32
BF16 = jnp.bfloat16
MESH = pl.DeviceIdType.MESH

RMS_EPS = 1e-6
N_DEV = 8
N_HEADS = 8
HEAD_DIM = 64
HEAD_PAIR = 2 * HEAD_DIM
POOL_WINDOWS = (2, 4, 8, 16)
POOL_GROUP = 128
POOL_WIDTH = 512
SB_WIDTH = 512
FF_SHARD = 352
FF_SHARD_PAD = 384
ATTN_BLOCK = 256
ATTN_SCALE = 0.125

ADAM_LR = 0.001
ADAM_B1 = 0.9
ADAM_B2 = 0.999
ADAM_EPS = 1e-08
ADAM_WD = 0.01
ADAM_STEP = 10

VMEM_LIMIT = 48 << 20


def _params(dims=None):
    return pltpu.CompilerParams(dimension_semantics=dims, vmem_limit_bytes=VMEM_LIMIT)


def _mm(a, b):
    return jnp.dot(a, b, preferred_element_type=F32)


def _mm_nt(a, b):
    return lax.dot_general(a, b, (((1,), (1,)), ((), ())), preferred_element_type=F32)


def _mm_tn(a, b):
    return lax.dot_general(a, b, (((0,), (0,)), ((), ())), preferred_element_type=F32)


def _rstd(xf):
    return lax.rsqrt(jnp.mean(xf * xf, axis=-1, keepdims=True) + RMS_EPS)


def _rms_bwd(xf, gain, dn):
    r = _rstd(xf)
    xh = xf * r
    dgain = jnp.sum(dn * xh, axis=0, keepdims=True)
    dxh = dn * gain
    dx = r * (dxh - xh * jnp.mean(dxh * xh, axis=-1, keepdims=True))
    return dx, dgain


def _ffn_fwd(x, gain, wgu, wd, *, tm, name):
    T, D = x.shape
    tm = min(tm, T)
    nb, bw = wgu.shape[0] // 2, wgu.shape[2]

    def body(x_ref, gain_ref, wg_ref, wu_ref, wd_ref, h_ref, gu_ref, n_scr, acc):
        j = pl.program_id(1)

        @pl.when(j == 0)
        def _():
            xf = x_ref[...]
            n_scr[...] = (xf * _rstd(xf) * gain_ref[...]).astype(BF16)
            acc[...] = jnp.zeros_like(acc)

        n = n_scr[...]
        g = _mm(n, wg_ref[...])
        u = _mm(n, wu_ref[...])
        gu_ref[0] = g.astype(BF16)
        gu_ref[1] = u.astype(BF16)
        hid = (g * jax.nn.sigmoid(g) * u).astype(BF16)
        acc[...] += _mm(hid, wd_ref[...])

        @pl.when(j == nb - 1)
        def _():
            h_ref[...] = x_ref[...] + 0.5 * acc[...]

    return pl.pallas_call(
        body, name=name, grid=(T // tm, nb),
        in_specs=[
            pl.BlockSpec((tm, D), lambda i, j: (i, 0)),
            pl.BlockSpec((1, D), lambda i, j: (0, 0)),
            pl.BlockSpec((None, D, bw), lambda i, j: (j, 0, 0)),
            pl.BlockSpec((None, D, bw), lambda i, j: (j + nb, 0, 0)),
            pl.BlockSpec((bw, D), lambda i, j: (j, 0)),
        ],
        out_specs=[
            pl.BlockSpec((tm, D), lambda i, j: (i, 0)),
            pl.BlockSpec((2, tm, bw), lambda i, j: (0, i, j)),
        ],
        out_shape=[jax.ShapeDtypeStruct((T, D), F32), jax.ShapeDtypeStruct((2, T, nb * bw), BF16)],
        scratch_shapes=[pltpu.VMEM((tm, D), BF16), pltpu.VMEM((tm, D), F32)],
        compiler_params=_params(("arbitrary", "arbitrary")),
    )(x, gain, wgu, wgu, wd)


def _ffn_bwd(dh, x, gain, gu, wgu, wd, *, tm, name):
    T, D = x.shape
    tm = min(tm, T)
    nb, bw = wgu.shape[0] // 2, wgu.shape[2]

    def body(dh_ref, x_ref, gain_ref, gu_ref, wg_ref, wu_ref, wd_ref,
             dx_ref, dgain_ref, n_ref, df_ref, dgu_ref, hid_ref, dn_acc):
        i, j = pl.program_id(0), pl.program_id(1)

        @pl.when(j == 0)
        def _():
            xf = x_ref[...]
            n_ref[...] = (xf * _rstd(xf) * gain_ref[...]).astype(BF16)
            df_ref[...] = (0.5 * dh_ref[...]).astype(BF16)
            dn_acc[...] = jnp.zeros_like(dn_acc)

        @pl.when((i == 0) & (j == 0))
        def _():
            dgain_ref[...] = jnp.zeros_like(dgain_ref)

        dhid = _mm_nt(df_ref[...], wd_ref[...])
        g = gu_ref[0].astype(F32)
        u = gu_ref[1].astype(F32)
        s = jax.nn.sigmoid(g)
        silu = g * s
        hid_ref[...] = (silu * u).astype(BF16)
        dg = (dhid * u * (s * (1.0 + g * (1.0 - s)))).astype(BF16)
        du = (dhid * silu).astype(BF16)
        dgu_ref[0] = dg
        dgu_ref[1] = du
        dn_acc[...] += _mm_nt(dg, wg_ref[...]) + _mm_nt(du, wu_ref[...])

        @pl.when(j == nb - 1)
        def _():
            dx, dgain = _rms_bwd(x_ref[...], gain_ref[...], dn_acc[...])
            dx_ref[...] = dh_ref[...] + dx
            dgain_ref[...] += dgain

    row = lambda i, j: (i, 0)
    return pl.pallas_call(
        body, name=name, grid=(T // tm, nb),
        in_specs=[
            pl.BlockSpec((tm, D), row),
            pl.BlockSpec((tm, D), row),
            pl.BlockSpec((1, D), lambda i, j: (0, 0)),
            pl.BlockSpec((2, tm, bw), lambda i, j: (0, i, j)),
            pl.BlockSpec((None, D, bw), lambda i, j: (j, 0, 0)),
            pl.BlockSpec((None, D, bw), lambda i, j: (j + nb, 0, 0)),
            pl.BlockSpec((bw, D), lambda i, j: (j, 0)),
        ],
        out_specs=[
            pl.BlockSpec((tm, D), row),
            pl.BlockSpec((1, D), lambda i, j: (0, 0)),
            pl.BlockSpec((tm, D), row),
            pl.BlockSpec((tm, D), row),
            pl.BlockSpec((2, tm, bw), lambda i, j: (0, i, j)),
            pl.BlockSpec((tm, bw), lambda i, j: (i, j)),
        ],
        out_shape=[
            jax.ShapeDtypeStruct((T, D), F32),
            jax.ShapeDtypeStruct((1, D), F32),
            jax.ShapeDtypeStruct((T, D), BF16),
            jax.ShapeDtypeStruct((T, D), BF16),
            jax.ShapeDtypeStruct((2, T, nb * bw), BF16),
            jax.ShapeDtypeStruct((T, nb * bw), BF16),
        ],
        scratch_shapes=[pltpu.VMEM((tm, D), F32)],
        compiler_params=_params(("arbitrary", "arbitrary")),
    )(dh, x, gain, gu, wgu, wgu, wd)


def _wgrad(a, b, *, grid, a_spec, b_spec, out_spec, out_shape, acc_shape, name, split_lanes=0):
    nk = grid[2]

    def body(a_ref, b_ref, o_ref, acc):
        k = pl.program_id(2)

        @pl.when(k == 0)
        def _():
            acc[...] = jnp.zeros_like(acc)

        acc[...] += _mm_tn(a_ref[...].astype(BF16), b_ref[...].astype(BF16))

        @pl.when(k == nk - 1)
        def _():
            if split_lanes:
                for e in range(o_ref.shape[0]):
                    o_ref[e] = acc[:, e * split_lanes:(e + 1) * split_lanes].astype(o_ref.dtype)
            else:
                o_ref[...] = acc[...].astype(o_ref.dtype)

    return pl.pallas_call(
        body, name=name, grid=grid, in_specs=[a_spec, b_spec], out_specs=out_spec,
        out_shape=jax.ShapeDtypeStruct(out_shape, BF16),
        scratch_shapes=[pltpu.VMEM(acc_shape, F32)],
        compiler_params=_params(("arbitrary", "arbitrary", "arbitrary")),
    )(a, b)


def _wgrad_gate_up(n, dgu, *, tk, name):
    T, D = n.shape
    tk = min(tk, T)
    bw = FF_SHARD_PAD * 2
    nb = dgu.shape[2] // bw
    return _wgrad(
        n, dgu, grid=(1, 2 * nb, T // tk), name=name,
        a_spec=pl.BlockSpec((tk, D), lambda m, c, k: (k, 0)),
        b_spec=pl.BlockSpec((None, tk, bw), lambda m, c, k: (c // nb, k, c % nb)),
        out_spec=pl.BlockSpec((None, D, bw), lambda m, c, k: (c, 0, 0)),
        out_shape=(2 * nb, D, bw), acc_shape=(D, bw))


def _wgrad_down(hid, df, *, tk, name):
    T, D = df.shape
    tk = min(tk, T)
    bw = FF_SHARD_PAD * 2
    nb = hid.shape[1] // bw
    return _wgrad(
        hid, df, grid=(nb, 1, T // tk), name=name,
        a_spec=pl.BlockSpec((tk, bw), lambda m, c, k: (k, m)),
        b_spec=pl.BlockSpec((tk, D), lambda m, c, k: (k, 0)),
        out_spec=pl.BlockSpec((bw, D), lambda m, c, k: (m, 0)),
        out_shape=(nb * bw, D), acc_shape=(bw, D))


def _wgrad_in(un, dproj, *, tk, name):
    T, D = un.shape
    tk = min(tk, T)
    bw = dproj.shape[1] // N_DEV
    return _wgrad(
        un, dproj, grid=(1, N_DEV, T // tk), name=name,
        a_spec=pl.BlockSpec((tk, D), lambda m, c, k: (k, 0)),
        b_spec=pl.BlockSpec((tk, bw), lambda m, c, k: (k, c)),
        out_spec=pl.BlockSpec((None, D, bw), lambda m, c, k: (c, 0, 0)),
        out_shape=(N_DEV, D, bw), acc_shape=(D, bw))


def _wgrad_full(a, b, *, tk, name, split_lanes=0):
    T, M = a.shape
    tk = min(tk, T)
    N = b.shape[1]
    if split_lanes:
        out_shape = (N // split_lanes, M, split_lanes)
        out_spec = pl.BlockSpec(out_shape, lambda m, c, k: (0, 0, 0))
    else:
        out_shape = (M, N)
        out_spec = pl.BlockSpec(out_shape, lambda m, c, k: (0, 0))
    return _wgrad(
        a, b, grid=(1, 1, T // tk), name=name,
        a_spec=pl.BlockSpec((tk, M), lambda m, c, k: (k, 0)),
        b_spec=pl.BlockSpec((tk, N), lambda m, c, k: (k, 0)),
        out_spec=out_spec, out_shape=out_shape, acc_shape=(M, N), split_lanes=split_lanes)


def _loss_bwd(h, target, gain, *, tm, name):
    T, D = h.shape
    tm = min(tm, T)

    def body(h_ref, t_ref, gain_ref, dh_ref, loss_ref, dgain_ref):
        @pl.when(pl.program_id(0) == 0)
        def _():
            loss_ref[...] = jnp.zeros_like(loss_ref)
            dgain_ref[...] = jnp.zeros_like(dgain_ref)

        xf = h_ref[...]
        gain = gain_ref[...]
        err = xf * _rstd(xf) * gain - t_ref[...]
        loss_ref[...] += 0.5 * jnp.sum(jnp.mean(err * err, axis=-1, keepdims=True), axis=0, keepdims=True)
        dx, dgain = _rms_bwd(xf, gain, err * (1.0 / D))
        dh_ref[...] = dx
        dgain_ref[...] += dgain

    row = lambda i: (i, 0)
    fixed = lambda i: (0, 0)
    return pl.pallas_call(
        body, name=name, grid=(T // tm,),
        in_specs=[pl.BlockSpec((tm, D), row), pl.BlockSpec((tm, D), row), pl.BlockSpec((1, D), fixed)],
        out_specs=[pl.BlockSpec((tm, D), row), pl.BlockSpec((1, 128), fixed), pl.BlockSpec((1, D), fixed)],
        out_shape=[jax.ShapeDtypeStruct((T, D), F32), jax.ShapeDtypeStruct((1, 128), F32),
                   jax.ShapeDtypeStruct((1, D), F32)],
        compiler_params=_params(("arbitrary",)),
    )(h, target, gain)


def _inproj_fwd(h, gain, w_in, *, tm, name):
    T, D = h.shape
    tm = min(tm, T)
    nb, bw = w_in.shape[0], w_in.shape[2]

    def body(h_ref, gain_ref, w_ref, un_ref, proj_ref):
        @pl.when(pl.program_id(1) == 0)
        def _():
            xf = h_ref[...]
            un_ref[...] = (xf * _rstd(xf) * gain_ref[...]).astype(BF16)

        proj_ref[...] = _mm(un_ref[...], w_ref[...])

    return pl.pallas_call(
        body, name=name, grid=(T // tm, nb),
        in_specs=[
            pl.BlockSpec((tm, D), lambda i, j: (i, 0)),
            pl.BlockSpec((1, D), lambda i, j: (0, 0)),
            pl.BlockSpec((None, D, bw), lambda i, j: (j, 0, 0)),
        ],
        out_specs=[pl.BlockSpec((tm, D), lambda i, j: (i, 0)), pl.BlockSpec((tm, bw), lambda i, j: (i, j))],
        out_shape=[jax.ShapeDtypeStruct((T, D), BF16), jax.ShapeDtypeStruct((T, nb * bw), F32)],
        compiler_params=_params(("arbitrary", "arbitrary")),
    )(h, gain, w_in)


def _inproj_bwd(dproj, dh, h, gain, w_in, *, tm, name):
    T, D = h.shape
    tm = min(tm, T)
    nb, bw = w_in.shape[0], w_in.shape[2]

    def body(dp_ref, dh_ref, h_ref, gain_ref, w_ref, dx_ref, dgain_ref, acc):
        i, j = pl.program_id(0), pl.program_id(1)

        @pl.when(j == 0)
        def _():
            acc[...] = jnp.zeros_like(acc)

        @pl.when((i == 0) & (j == 0))
        def _():
            dgain_ref[...] = jnp.zeros_like(dgain_ref)

        acc[...] += _mm_nt(dp_ref[...], w_ref[...])

        @pl.when(j == nb - 1)
        def _():
            dx, dgain = _rms_bwd(h_ref[...], gain_ref[...], acc[...])
            dx_ref[...] = dh_ref[...] + dx
            dgain_ref[...] += dgain

    row = lambda i, j: (i, 0)
    return pl.pallas_call(
        body, name=name, grid=(T // tm, nb),
        in_specs=[
            pl.BlockSpec((tm, bw), lambda i, j: (i, j)),
            pl.BlockSpec((tm, D), row),
            pl.BlockSpec((tm, D), row),
            pl.BlockSpec((1, D), lambda i, j: (0, 0)),
            pl.BlockSpec((None, D, bw), lambda i, j: (j, 0, 0)),
        ],
        out_specs=[pl.BlockSpec((tm, D), row), pl.BlockSpec((1, D), lambda i, j: (0, 0))],
        out_shape=[jax.ShapeDtypeStruct((T, D), F32), jax.ShapeDtypeStruct((1, D), F32)],
        scratch_shapes=[pltpu.VMEM((tm, D), F32)],
        compiler_params=_params(("arbitrary", "arbitrary")),
    )(dproj, dh, h, gain, w_in)


def _window_sum(x, row, doublings, *, backward):
    T = x.shape[0]
    s = x
    for k in range(doublings):
        sh = 1 << k
        if backward:
            s = s + jnp.where(row < T - sh, pltpu.roll(s, T - sh, 0), 0.0)
        else:
            s = s + jnp.where(row >= sh, pltpu.roll(s, sh, 0), 0.0)
    return s


def _pool_fwd(proj, w_group, scale, *, name):
    T = proj.shape[0]

    def body(xp_ref, w_ref, scale_ref, p_ref):
        row = lax.broadcasted_iota(jnp.int32, (T, POOL_GROUP), 0)
        for gi, window in enumerate(POOL_WINDOWS):
            cols = slice(gi * POOL_GROUP, (gi + 1) * POOL_GROUP)
            x = xp_ref[:, cols]
            inv_count = 1.0 / jnp.minimum(row + 1, window).astype(F32)
            yc = _window_sum(x, row, gi + 1, backward=False) * inv_count - x
            pre = _mm(yc.astype(BF16), w_ref[gi].astype(BF16))
            p_ref[:, cols] = pre * scale_ref[:, cols]

    return pl.pallas_call(
        body, name=name, grid=(1,),
        in_specs=[
            pl.BlockSpec((T, POOL_WIDTH), lambda i: (0, 0)),
            pl.BlockSpec(w_group.shape, lambda i: (0, 0, 0)),
            pl.BlockSpec((1, POOL_WIDTH), lambda i: (0, 0)),
        ],
        out_specs=pl.BlockSpec((T, POOL_WIDTH), lambda i: (0, 0)),
        out_shape=jax.ShapeDtypeStruct((T, POOL_WIDTH), F32),
        compiler_params=_params(("arbitrary",)),
    )(proj, w_group, scale)


def _pool_bwd(dp, proj, w_group, scale, *, name):
    T = proj.shape[0]

    def body(dp_ref, xp_ref, w_ref, scale_ref, dxp_ref, dw_ref, dscale_ref):
        row = lax.broadcasted_iota(jnp.int32, (T, POOL_GROUP), 0)
        for gi, window in enumerate(POOL_WINDOWS):
            cols = slice(gi * POOL_GROUP, (gi + 1) * POOL_GROUP)
            x = xp_ref[:, cols]
            inv_count = 1.0 / jnp.minimum(row + 1, window).astype(F32)
            yc = (_window_sum(x, row, gi + 1, backward=False) * inv_count - x).astype(BF16)
            w = w_ref[gi].astype(BF16)
            pre = _mm(yc, w)
            dpg = dp_ref[:, cols]
            dscale_ref[:, cols] = jnp.sum(dpg * pre, axis=0, keepdims=True)
            dpre = (dpg * scale_ref[:, cols]).astype(BF16)
            dw_ref[gi] = _mm_tn(yc, dpre)
            dyc = _mm_nt(dpre, w)
            dxp_ref[:, cols] = _window_sum(dyc * inv_count, row, gi + 1, backward=True) - dyc

    return pl.pallas_call(
        body, name=name, grid=(1,),
        in_specs=[
            pl.BlockSpec((T, POOL_WIDTH), lambda i: (0, 0)),
            pl.BlockSpec((T, POOL_WIDTH), lambda i: (0, 0)),
            pl.BlockSpec(w_group.shape, lambda i: (0, 0, 0)),
            pl.BlockSpec((1, POOL_WIDTH), lambda i: (0, 0)),
        ],
        out_specs=[
            pl.BlockSpec((T, POOL_WIDTH), lambda i: (0, 0)),
            pl.BlockSpec(w_group.shape, lambda i: (0, 0, 0)),
            pl.BlockSpec((1, POOL_WIDTH), lambda i: (0, 0)),
        ],
        out_shape=[jax.ShapeDtypeStruct((T, POOL_WIDTH), F32), jax.ShapeDtypeStruct(w_group.shape, F32),
                   jax.ShapeDtypeStruct((1, POOL_WIDTH), F32)],
        compiler_params=_params(("arbitrary",)),
    )(dp, proj, w_group, scale)


ATTN_STRIP = 32


def _log_sigmoids(z):
    lb = jnp.minimum(z, 0.0) - jnp.log(1.0 + jnp.exp(-jnp.abs(z)))
    return lb, lb - z


def _transposed_blocks(x_ref, blocks_scr, tq):
    for b in range(blocks_scr.shape[0]):
        blocks_scr[b] = x_ref[b * tq:(b + 1) * tq, :].T.astype(BF16)


def _split_bf16(x):
    hi = x.astype(BF16)
    return hi, (x - hi.astype(F32)).astype(BF16)


def _strips(n):
    return [slice(i, i + ATTN_STRIP) for i in range(0, n, ATTN_STRIP)]


def _rows(parts):
    return jnp.concatenate(parts, axis=0)


def _attn_specs(T, tq):
    q_col = POOL_WIDTH // HEAD_PAIR
    k_col = q_col + SB_WIDTH // HEAD_PAIR
    v_col = k_col + SB_WIDTH // HEAD_PAIR
    return [
        pl.BlockSpec((tq, HEAD_PAIR), lambda p, i: (i, q_col + p)),
        pl.BlockSpec((T, HEAD_PAIR), lambda p, i: (0, k_col + p)),
        pl.BlockSpec((T, HEAD_PAIR), lambda p, i: (0, v_col + p)),
    ]


def _attn_fwd(proj, *, name):
    T = proj.shape[0]
    tq = ATTN_BLOCK

    def body(q_ref, k_ref, v_ref, o_ref, lt_ref, kt_scr, vb_scr):
        qi = pl.program_id(1)

        @pl.when(qi == 0)
        def _():
            _transposed_blocks(k_ref, kt_scr, tq)
            vb_scr[...] = v_ref[...].astype(BF16)

        head0 = lax.broadcasted_iota(jnp.int32, (tq, HEAD_PAIR), 1) < HEAD_DIM
        q = q_ref[...] * ATTN_SCALE
        qs = (jnp.where(head0, q, 0.0).astype(BF16), jnp.where(head0, 0.0, q).astype(BF16))
        r = lax.broadcasted_iota(jnp.int32, (tq, tq), 0)
        c = lax.broadcasted_iota(jnp.int32, (tq, tq), 1)
        later = (r > c).astype(BF16)
        later2 = _rows([later, later])
        causal = lambda rows: c[rows] < r[rows]
        strips = _strips(tq)

        def log_terms(z, valid):
            lbs, his, los, sums = [], [], [], []
            for rows in strips:
                lb, lm = _log_sigmoids(z[rows])
                if valid is not None:
                    lm = jnp.where(valid(rows), lm, 0.0)
                hi, lo = _split_bf16(lm)
                lbs.append(lb)
                his.append(hi)
                los.append(lo)
                sums.append(jnp.sum(lm, axis=1, keepdims=True))
            return lbs, jnp.concatenate([_rows(his), _rows(los)], axis=1), _rows(sums)

        def weights(lbs, run, after, valid):
            parts = []
            for rows, lb in zip(strips, lbs):
                a = jnp.exp(lb + run[rows] + after[rows])
                if valid is not None:
                    a = jnp.where(valid(rows), a, 0.0)
                parts.append(a.astype(BF16))
            return _rows(parts)

        def block(kj, carry, valid):
            kt = kt_scr[kj]
            vb = vb_scr[pl.ds(pl.multiple_of(kj * tq, tq), tq), :]
            run0, o0, run1, o1 = carry
            z0 = _mm(qs[0], kt)
            z1 = _mm(qs[1], kt)
            lbs0, split0, sums0 = log_terms(z0, valid)
            after0 = _mm(split0, later2)
            lbs1, split1, sums1 = log_terms(z1, valid)
            after1 = _mm(split1, later2)
            o0 = o0 + _mm(weights(lbs0, run0, after0, valid), vb)
            o1 = o1 + _mm(weights(lbs1, run1, after1, valid), vb)
            return run0 + sums0, o0, run1 + sums1, o1

        zero = (jnp.zeros((tq, 1), F32), jnp.zeros((tq, HEAD_PAIR), F32))
        carry = block(qi, zero + zero, causal)
        carry = lax.fori_loop(0, qi, lambda it, cr: block(qi - 1 - it, cr, None), carry)
        o_ref[...] = jnp.where(head0, carry[1], carry[3])
        lt_ref[...] = jnp.where(head0, carry[0], carry[2])

    out_spec = pl.BlockSpec((tq, HEAD_PAIR), lambda p, i: (i, p))
    return pl.pallas_call(
        body, name=name, grid=(N_HEADS // 2, T // tq),
        in_specs=_attn_specs(T, tq), out_specs=[out_spec, out_spec],
        out_shape=[jax.ShapeDtypeStruct((T, SB_WIDTH), F32), jax.ShapeDtypeStruct((T, SB_WIDTH), F32)],
        scratch_shapes=[pltpu.VMEM((T // tq, HEAD_PAIR, tq), BF16), pltpu.VMEM((T, HEAD_PAIR), BF16)],
        compiler_params=_params(("arbitrary", "arbitrary")),
    )(proj, proj, proj)


def _attn_bwd(proj, do, ltot, *, name):
    T = proj.shape[0]
    tq = ATTN_BLOCK

    def body(q_ref, k_ref, v_ref, do_ref, lt_ref, dq_ref, dkt_ref, dvt_ref, kb_scr, kt_scr, vt_scr):
        qi = pl.program_id(1)

        @pl.when(qi == 0)
        def _():
            kb_scr[...] = k_ref[...].astype(BF16)
            _transposed_blocks(k_ref, kt_scr, tq)
            _transposed_blocks(v_ref, vt_scr, tq)
            dkt_ref[...] = jnp.zeros_like(dkt_ref)
            dvt_ref[...] = jnp.zeros_like(dvt_ref)

        head0 = lax.broadcasted_iota(jnp.int32, (tq, HEAD_PAIR), 1) < HEAD_DIM
        q, do_, lt = q_ref[...] * ATTN_SCALE, do_ref[...], lt_ref[...]
        qs = (jnp.where(head0, q, 0.0).astype(BF16), jnp.where(head0, 0.0, q).astype(BF16))
        q_heads = (jnp.where(head0, q, 0.0), jnp.where(head0, 0.0, q))
        do_heads = (jnp.where(head0, do_, 0.0), jnp.where(head0, 0.0, do_))
        dos = tuple(d.astype(BF16) for d in do_heads)
        qts = tuple(x.T.astype(BF16) for x in q_heads)
        dots = tuple(d.T.astype(BF16) for d in do_heads)
        lts = (jnp.max(jnp.where(head0, lt, -jnp.inf), axis=1, keepdims=True),
               jnp.max(jnp.where(head0, -jnp.inf, lt), axis=1, keepdims=True))
        r = lax.broadcasted_iota(jnp.int32, (tq, tq), 0)
        c = lax.broadcasted_iota(jnp.int32, (tq, tq), 1)
        upto = (r <= c).astype(BF16)
        before = (r < c).astype(BF16)
        upto2, before2 = _rows([upto, upto]), _rows([before, before])
        causal = lambda rows: c[rows] < r[rows]
        strips = _strips(tq)

        def log_terms(z, valid):
            lbs, his, los, sums = [], [], [], []
            for rows in strips:
                lb, lm = _log_sigmoids(z[rows])
                if valid is not None:
                    lm = jnp.where(valid(rows), lm, 0.0)
                hi, lo = _split_bf16(lm)
                lbs.append(lb)
                his.append(hi)
                los.append(lo)
                sums.append(jnp.sum(lm, axis=1, keepdims=True))
            return lbs, jnp.concatenate([_rows(his), _rows(los)], axis=1), _rows(sums)

        def weights(lbs, rest, lm_upto, da, valid):
            a_parts, es, his, los, sums = [], [], [], [], []
            for rows, lb in zip(strips, lbs):
                a = jnp.exp(lb + (rest[rows] - lm_upto[rows]))
                if valid is not None:
                    a = jnp.where(valid(rows), a, 0.0)
                e = da[rows] * a
                hi, lo = _split_bf16(e)
                a_parts.append(a.astype(BF16))
                es.append(e)
                his.append(hi)
                los.append(lo)
                sums.append(jnp.sum(e, axis=1, keepdims=True))
            return _rows(a_parts), es, jnp.concatenate([_rows(his), _rows(los)], axis=1), _rows(sums)

        def score_grads(lbs, es, run_e, e_before, valid):
            parts = []
            for rows, lb, e in zip(strips, lbs, es):
                beta = jnp.exp(lb)
                dz = e * (1.0 - beta) - (run_e[rows] + e_before[rows]) * beta
                if valid is not None:
                    dz = jnp.where(valid(rows), dz, 0.0)
                parts.append(dz.astype(BF16))
            return _rows(parts)

        def block(kj, carry, valid):
            off = pl.multiple_of(kj * tq, tq)
            kb, kt, vt = kb_scr[pl.ds(off, tq), :], kt_scr[kj], vt_scr[kj]
            run_lm0, run_e0, dq0, run_lm1, run_e1, dq1 = carry
            z0, da0 = _mm(qs[0], kt), _mm(dos[0], vt)
            z1, da1 = _mm(qs[1], kt), _mm(dos[1], vt)
            lbs0, split0, lm_sums0 = log_terms(z0, valid)
            lm_upto0 = _mm(split0, upto2)
            lbs1, split1, lm_sums1 = log_terms(z1, valid)
            lm_upto1 = _mm(split1, upto2)
            a0, es0, split0, e_sums0 = weights(lbs0, lts[0] - run_lm0, lm_upto0, da0, valid)
            e_before0 = _mm(split0, before2)
            a1, es1, split1, e_sums1 = weights(lbs1, lts[1] - run_lm1, lm_upto1, da1, valid)
            e_before1 = _mm(split1, before2)
            dz0 = score_grads(lbs0, es0, run_e0, e_before0, valid)
            dkt_blk = _mm(qts[0], dz0)
            dvt_blk = _mm(dots[0], a0)
            dq0 = dq0 + _mm(dz0, kb)
            dz1 = score_grads(lbs1, es1, run_e1, e_before1, valid)
            dkt_ref[kj] += dkt_blk + _mm(qts[1], dz1)
            dvt_ref[kj] += dvt_blk + _mm(dots[1], a1)
            dq1 = dq1 + _mm(dz1, kb)
            return run_lm0 + lm_sums0, run_e0 + e_sums0, dq0, run_lm1 + lm_sums1, run_e1 + e_sums1, dq1

        zero = (jnp.zeros((tq, 1), F32), jnp.zeros((tq, 1), F32), jnp.zeros((tq, HEAD_PAIR), F32))
        carry = lax.fori_loop(0, qi, lambda kj, cr: block(kj, cr, None), zero + zero)
        carry = block(qi, carry, causal)
        dq_ref[...] = jnp.where(head0, carry[2], carry[5]) * ATTN_SCALE

    blk = pl.BlockSpec((tq, HEAD_PAIR), lambda p, i: (i, p))
    seq = pl.BlockSpec((T // tq, HEAD_PAIR, tq), lambda p, i: (0, p, 0))
    transposed = jax.ShapeDtypeStruct((T // tq, SB_WIDTH, tq), F32)
    return pl.pallas_call(
        body, name=name, grid=(N_HEADS // 2, T // tq),
        in_specs=_attn_specs(T, tq) + [blk, blk], out_specs=[blk, seq, seq],
        out_shape=[jax.ShapeDtypeStruct((T, SB_WIDTH), F32), transposed, transposed],
        scratch_shapes=[pltpu.VMEM((T, HEAD_PAIR), BF16), pltpu.VMEM((T // tq, HEAD_PAIR, tq), BF16),
                        pltpu.VMEM((T // tq, HEAD_PAIR, tq), BF16)],
        compiler_params=_params(("arbitrary", "arbitrary")),
    )(proj, proj, proj, do, ltot)


def _branch(act_bf16, w_ref):
    return jnp.concatenate([_mm(act_bf16, w_ref[e]) for e in range(w_ref.shape[0])], axis=1)


def _mix_specs(T, D, tm, wbp, w_out):
    gate_col = (POOL_WIDTH + 3 * SB_WIDTH) // D
    row = lambda i: (i, 0)
    return [
        pl.BlockSpec((tm, D), row),
        pl.BlockSpec((tm, POOL_WIDTH), row),
        pl.BlockSpec((tm, SB_WIDTH), row),
        pl.BlockSpec((tm, D), lambda i: (i, gate_col)),
        pl.BlockSpec((tm, D), lambda i: (i, gate_col + 1)),
        pl.BlockSpec(wbp.shape, lambda i: (0, 0, 0)),
        pl.BlockSpec(wbp.shape, lambda i: (0, 0, 0)),
        pl.BlockSpec(w_out.shape, lambda i: (0, 0)),
    ]


def _mix_fwd(h, p, o, proj, wbp, wba, w_out, *, tm, name):
    T, D = h.shape
    tm = min(tm, T)

    def body(h_ref, p_ref, o_ref, glp_ref, gls_ref, wbp_ref, wba_ref, wout_ref, hout_ref, m_ref):
        yp = _branch(p_ref[...].astype(BF16), wbp_ref)
        ys = _branch(o_ref[...].astype(BF16), wba_ref)
        m = (jax.nn.sigmoid(glp_ref[...]) * yp + jax.nn.sigmoid(gls_ref[...]) * ys).astype(BF16)
        m_ref[...] = m
        hout_ref[...] = h_ref[...] + _mm(m, wout_ref[...])

    row = lambda i: (i, 0)
    return pl.pallas_call(
        body, name=name, grid=(T // tm,),
        in_specs=_mix_specs(T, D, tm, wbp, w_out),
        out_specs=[pl.BlockSpec((tm, D), row), pl.BlockSpec((tm, D), row)],
        out_shape=[jax.ShapeDtypeStruct((T, D), F32), jax.ShapeDtypeStruct((T, D), BF16)],
        compiler_params=_params(("arbitrary",)),
    )(h, p, o, proj, proj, wbp, wba, w_out)


def _mix_bwd(dh, p, o, proj, wbp, wba, w_out, *, tm, name):
    T, D = dh.shape
    tm = min(tm, T)
    bw = wbp.shape[2]

    def body(dh_ref, p_ref, o_ref, glp_ref, gls_ref, wbp_ref, wba_ref, wout_ref,
             dyp_ref, dys_ref, dp_ref, do_ref, dgl_ref):
        dm = _mm_nt(dh_ref[...].astype(BF16), wout_ref[...])
        yp = _branch(p_ref[...].astype(BF16), wbp_ref)
        ys = _branch(o_ref[...].astype(BF16), wba_ref)
        gp = jax.nn.sigmoid(glp_ref[...])
        gs = jax.nn.sigmoid(gls_ref[...])
        dyp = (dm * gp).astype(BF16)
        dys = (dm * gs).astype(BF16)
        dyp_ref[...] = dyp
        dys_ref[...] = dys
        dgl_ref[:, :D] = (dm * yp * gp * (1.0 - gp)).astype(BF16)
        dgl_ref[:, D:] = (dm * ys * gs * (1.0 - gs)).astype(BF16)
        dp = jnp.zeros(dp_ref.shape, F32)
        do_ = jnp.zeros(do_ref.shape, F32)
        for e in range(wbp_ref.shape[0]):
            dp += _mm_nt(dyp[:, e * bw:(e + 1) * bw], wbp_ref[e])
            do_ += _mm_nt(dys[:, e * bw:(e + 1) * bw], wba_ref[e])
        dp_ref[...] = dp
        do_ref[...] = do_

    row = lambda i: (i, 0)
    return pl.pallas_call(
        body, name=name, grid=(T // tm,),
        in_specs=_mix_specs(T, D, tm, wbp, w_out),
        out_specs=[pl.BlockSpec((tm, D), row), pl.BlockSpec((tm, D), row), pl.BlockSpec((tm, POOL_WIDTH), row),
                   pl.BlockSpec((tm, SB_WIDTH), row), pl.BlockSpec((tm, 2 * D), row)],
        out_shape=[jax.ShapeDtypeStruct((T, D), BF16), jax.ShapeDtypeStruct((T, D), BF16),
                   jax.ShapeDtypeStruct((T, POOL_WIDTH), F32), jax.ShapeDtypeStruct((T, SB_WIDTH), F32),
                   jax.ShapeDtypeStruct((T, 2 * D), BF16)],
        compiler_params=_params(("arbitrary",)),
    )(dh, p, o, proj, proj, wbp, wba, w_out)


def _adamw(w, g, m, v, *, name):
    R, C = w.shape
    tr = R if R * C <= 512 * 1024 else 256

    def body(w_ref, g_ref, m_ref, v_ref, d_ref, nm_ref, nv_ref):
        g_ = g_ref[...]
        m_ = ADAM_B1 * m_ref[...] + (1.0 - ADAM_B1) * g_
        v_ = ADAM_B2 * v_ref[...] + (1.0 - ADAM_B2) * (g_ * g_)
        m_hat = m_ / (1.0 - ADAM_B1 ** ADAM_STEP)
        v_hat = v_ / (1.0 - ADAM_B2 ** ADAM_STEP)
        d_ref[...] = -ADAM_LR * (m_hat / (jnp.sqrt(v_hat) + ADAM_EPS) + ADAM_WD * w_ref[...])
        nm_ref[...] = m_
        nv_ref[...] = v_

    spec = pl.BlockSpec((tr, C), lambda i: (i, 0))
    return pl.pallas_call(
        body, name=name, grid=(R // tr,), in_specs=[spec] * 4, out_specs=[spec] * 3,
        out_shape=[jax.ShapeDtypeStruct((R, C), F32)] * 3,
        compiler_params=_params(("arbitrary",)),
    )(w, g, m, v)


def _position():
    return lax.axis_index("x"), lax.axis_index("y"), lax.axis_index("c")


def _all_gather(shards, *, name, collective_id):
    n = len(shards)

    def body(*refs):
        ins, outs = refs[:n], refs[n:2 * n]
        send_sems, recv_sems, local_sems = refs[2 * n:]
        x, y, c = _position()
        me, sibling = (x, y, c), (x, y, 1 - c)
        chips = [(1 - x, y), (x, 1 - y), (1 - x, 1 - y)]

        barrier = pltpu.get_barrier_semaphore()
        for peer in [sibling] + [(*chip, c) for chip in chips]:
            pl.semaphore_signal(barrier, inc=1, device_id=peer, device_id_type=MESH)
        pl.semaphore_wait(barrier, 4)

        def block(a, pos):
            return outs[a].at[4 * pos[0] + 2 * pos[1] + pos[2]]

        def copy(a, k, pos, to, src=None):
            return pltpu.make_async_remote_copy(
                src_ref=block(a, pos) if src is None else src, dst_ref=block(a, pos),
                send_sem=send_sems.at[7 * a + k], recv_sem=recv_sems.at[7 * a + k],
                device_id=to, device_id_type=MESH)

        started = []
        for a in range(n):
            mine = pltpu.make_async_copy(ins[a], block(a, me), local_sems.at[a])
            mine.start()
            started.append(mine)
        sends = []
        for a in range(n):
            sends += [copy(a, 1 + j, me, (*chip, c), src=ins[a]) for j, chip in enumerate(chips)]
            sends.append(copy(a, 0, me, sibling, src=ins[a]))
        for cp in sends:
            cp.start()
        for j, chip in enumerate(chips):
            for a in range(n):
                copy(a, 1 + j, (*chip, c), me).wait_recv()
                passed = copy(a, 4 + j, (*chip, c), sibling)
                passed.start()
                sends.append(passed)
        for a in range(n):
            copy(a, 0, sibling, me).wait_recv()
            for j, chip in enumerate(chips):
                copy(a, 4 + j, (*chip, 1 - c), me).wait_recv()
        for cp in sends:
            cp.wait_send()
        for cp in started:
            cp.wait()

    return pl.kernel(
        body, name=name,
        out_type=[jax.ShapeDtypeStruct((N_DEV,) + s.shape, s.dtype) for s in shards],
        mesh=plsc.ScalarSubcoreMesh(axis_name="sequencer", num_cores=1),
        scratch_types=[pltpu.SemaphoreType.DMA((7 * n,)), pltpu.SemaphoreType.DMA((7 * n,)),
                       pltpu.SemaphoreType.DMA((n,))],
        compiler_params=pltpu.CompilerParams(collective_id=collective_id),
    )(*shards)


def _chip_sums(grads, *, name):
    _, R, C = grads.shape
    rc = 128 if R % 128 == 0 else R

    def body(g_ref, partial, out_ref, mine, theirs, send_sems, recv_sems, local_sems):
        x, y, c = _position()
        my_chip = 2 * x + y

        def swap(s):
            return pltpu.make_async_remote_copy(
                src_ref=g_ref.at[2 * s + (1 - c)], dst_ref=theirs.at[s],
                send_sem=send_sems.at[s], recv_sem=recv_sems.at[s],
                device_id=(x, y, 1 - c), device_id_type=MESH)

        def load(s):
            return pltpu.make_async_copy(g_ref.at[2 * s + c], mine.at[s], local_sems.at[s])

        for s in range(4):
            swap(s).start()
            load(s).start()
        for s in range(4):
            load(s).wait()
            swap(s).wait_recv()

        def chip_sum(chip, rows):
            return mine[chip, rows, :].astype(F32) + theirs[chip, rows, :].astype(F32)

        for j in (1, 2, 3):
            @pl.loop(0, R // rc)
            def _(t):
                rows = pl.ds(pl.multiple_of(t * rc, rc), rc)
                partial[j - 1, rows, :] = chip_sum(my_chip ^ j, rows).astype(BF16)

        @pl.loop(0, R // rc)
        def _(t):
            rows = pl.ds(pl.multiple_of(t * rc, rc), rc)
            out_ref[rows, :] = chip_sum(my_chip, rows)

        for s in range(4):
            swap(s).wait_send()

    vmem = pl.BlockSpec(memory_space=pltpu.VMEM)
    return pl.pallas_call(
        body, name=name,
        in_specs=[pl.BlockSpec(memory_space=pl.ANY)], out_specs=[vmem, vmem],
        out_shape=[jax.ShapeDtypeStruct((3, R, C), BF16), jax.ShapeDtypeStruct((R, C), F32)],
        scratch_shapes=[
            pltpu.VMEM((4, R, C), BF16), pltpu.VMEM((4, R, C), BF16),
            pltpu.SemaphoreType.DMA((4,)), pltpu.SemaphoreType.DMA((4,)), pltpu.SemaphoreType.DMA((4,)),
        ],
        compiler_params=_params(),
    )(grads)


def _cross_chips(partials, *, name, collective_id):
    n = len(partials)

    def body(*refs):
        ins, outs = refs[:n], refs[n:2 * n]
        send_sems, recv_sems = refs[2 * n:]
        x, y, c = _position()
        my_chip = 2 * x + y
        peers = [((my_chip ^ j) // 2, (my_chip ^ j) % 2, c) for j in (1, 2, 3)]

        barrier = pltpu.get_barrier_semaphore()
        for peer in peers:
            pl.semaphore_signal(barrier, inc=1, device_id=peer, device_id_type=MESH)
        pl.semaphore_wait(barrier, 3)

        copies = [
            pltpu.make_async_remote_copy(
                src_ref=ins[a].at[j], dst_ref=outs[a].at[j],
                send_sem=send_sems.at[3 * a + j], recv_sem=recv_sems.at[3 * a + j],
                device_id=peers[j], device_id_type=MESH)
            for a in range(n) for j in range(3)]
        for cp in copies:
            cp.start()
        for cp in copies:
            cp.wait_recv()
        for cp in copies:
            cp.wait_send()

    return pl.kernel(
        body, name=name,
        out_type=[jax.ShapeDtypeStruct(p.shape, p.dtype) for p in partials],
        mesh=plsc.ScalarSubcoreMesh(axis_name="sequencer", num_cores=1),
        scratch_types=[pltpu.SemaphoreType.DMA((3 * n,)), pltpu.SemaphoreType.DMA((3 * n,))],
        compiler_params=pltpu.CompilerParams(collective_id=collective_id),
    )(*partials)


def _owner_sum(own, landed, *, name):
    R, C = own.shape
    tr = R if R * C <= 512 * 1024 else 256

    def body(own_ref, landed_ref, out_ref):
        total = own_ref[...]
        for j in range(3):
            total = total + landed_ref[j].astype(F32)
        out_ref[...] = total

    return pl.pallas_call(
        body, name=name, grid=(R // tr,),
        in_specs=[pl.BlockSpec((tr, C), lambda i: (i, 0)), pl.BlockSpec((3, tr, C), lambda i: (0, i, 0))],
        out_specs=pl.BlockSpec((tr, C), lambda i: (i, 0)),
        out_shape=jax.ShapeDtypeStruct((R, C), F32),
        compiler_params=_params(("arbitrary",)),
    )(own, landed)


def _sum_devices(gathered, *, name):
    _, R, C = gathered.shape

    def body(in_ref, out_ref):
        total = in_ref[0]
        for d in range(1, N_DEV):
            total = total + in_ref[d]
        out_ref[...] = total

    return pl.pallas_call(
        body, name=name, grid=(1,),
        in_specs=[pl.BlockSpec((N_DEV, R, C), lambda i: (0, 0, 0))],
        out_specs=pl.BlockSpec((R, C), lambda i: (0, 0)),
        out_shape=jax.ShapeDtypeStruct((R, C), F32),
        compiler_params=_params(("arbitrary",)),
    )(gathered)


def _local_step(x, target, norms, pool_w_group, pool_scale, wgu1, wd1, w_in, wbp, wba, w_out, wgu2, wd2, exchange):
    n1g, nmg, n2g, nfg = norms
    D = x.shape[1]
    h1, gu1 = _ffn_fwd(x, n1g, wgu1, wd1, tm=512, name="ffn1_fwd")
    un, proj = _inproj_fwd(h1, nmg, w_in, tm=1024, name="inproj_fwd")
    p = _pool_fwd(proj, pool_w_group, pool_scale, name="pool_fwd")
    o, ltot = _attn_fwd(proj, name="attn_fwd")
    h2, m = _mix_fwd(h1, p, o, proj, wbp, wba, w_out, tm=256, name="mix_fwd")
    h3, gu2 = _ffn_fwd(h2, n2g, wgu2, wd2, tm=512, name="ffn2_fwd")
    dh3, loss, d_nf = _loss_bwd(h3, target, nfg, tm=256, name="loss_bwd")

    dh2, d_n2, n2, df2, dgu2, hid2 = _ffn_bwd(dh3, h2, n2g, gu2, wgu2, wd2, tm=256, name="ffn2_bwd")
    d_wgu2 = _wgrad_gate_up(n2, dgu2, tk=512, name="ffn2_wgrad_gate_up")
    d_wd2 = _wgrad_down(hid2, df2, tk=512, name="ffn2_wgrad_down")
    g_wgu2, g_wd2 = exchange("ffn2", [d_wgu2, d_wd2.reshape(N_DEV, FF_SHARD_PAD, D)])

    dyp, dys, dp, do, dgl = _mix_bwd(dh2, p, o, proj, wbp, wba, w_out, tm=256, name="mix_bwd")
    d_wout = _wgrad_full(m, dh2, tk=512, name="wgrad_out")
    d_wbp = _wgrad_full(p, dyp, tk=512, name="wgrad_branch_pool", split_lanes=wbp.shape[2])
    d_wba = _wgrad_full(o, dys, tk=512, name="wgrad_branch_attn", split_lanes=wba.shape[2])
    g_wbp, g_wba, g_wout = exchange("mix", [d_wbp, d_wba, d_wout.reshape(N_DEV, D // N_DEV, D)])
    dxp, d_wgroup, d_scale = _pool_bwd(dp, proj, pool_w_group, pool_scale, name="pool_bwd")
    dq, dkt, dvt = _attn_bwd(proj, do, ltot, name="attn_bwd")
    dk, dv = (t.transpose(0, 2, 1).reshape(dq.shape) for t in (dkt, dvt))
    dproj = jnp.concatenate([dxp.astype(BF16), dq.astype(BF16), dk.astype(BF16), dv.astype(BF16), dgl], axis=1)
    dh1, d_nm = _inproj_bwd(dproj, dh2, h1, nmg, w_in, tm=1024, name="inproj_bwd")
    d_win = _wgrad_in(un, dproj, tk=512, name="wgrad_in")
    g_win, = exchange("w_in", [d_win])

    dx, d_n1, n1, df1, dgu1, hid1 = _ffn_bwd(dh1, x, n1g, gu1, wgu1, wd1, tm=256, name="ffn1_bwd")
    d_wgu1 = _wgrad_gate_up(n1, dgu1, tk=512, name="ffn1_wgrad_gate_up")
    d_wd1 = _wgrad_down(hid1, df1, tk=512, name="ffn1_wgrad_down")
    g_wgu1, g_wd1 = exchange("ffn1", [d_wgu1, d_wd1.reshape(N_DEV, FF_SHARD_PAD, D)])

    sharded = (g_wgu1, g_wd1, g_win, g_wbp, g_wba, g_wout, g_wgu2, g_wd2)
    replicated = (d_n1, d_nm, d_n2, d_nf, d_scale, d_wgroup)
    return loss, dx, sharded, replicated


def _pad_gate_up(w):
    d = w.shape[0]
    w = w.astype(BF16).reshape(d, 2, FF_SHARD)
    return jnp.pad(w, ((0, 0), (0, 0), (0, FF_SHARD_PAD - FF_SHARD))).reshape(d, 2 * FF_SHARD_PAD)


def _unpad_gate_up(g):
    d = g.shape[0]
    return g.reshape(d, 2, FF_SHARD_PAD)[:, :, :FF_SHARD].reshape(d, 2 * FF_SHARD)


def _pad_down(w):
    return jnp.pad(w.astype(BF16), ((0, FF_SHARD_PAD - FF_SHARD), (0, 0)))


def kernel(x, ffn1_norm, ffn1_w_gate_up, ffn1_w_down, mix_norm, w_in, pool_w_group, pool_scale, w_branch_pool, w_branch_attn, w_out, ffn2_norm, ffn2_w_gate_up, ffn2_w_down, final_norm, loss_target, m_ffn1_norm, m_ffn1_w_gate_up, m_ffn1_w_down, m_mix_norm, m_w_in, m_pool_w_group, m_pool_scale, m_w_branch_pool, m_w_branch_attn, m_w_out, m_ffn2_norm, m_ffn2_w_gate_up, m_ffn2_w_down, m_final_norm, v_ffn1_norm, v_ffn1_w_gate_up, v_ffn1_w_down, v_mix_norm, v_w_in, v_pool_w_group, v_pool_scale, v_w_branch_pool, v_w_branch_attn, v_w_out, v_ffn2_norm, v_ffn2_w_gate_up, v_ffn2_w_down, v_final_norm):
    D = x.shape[-1]
    weights = dict(ffn1_norm=ffn1_norm, ffn1_w_gate_up=ffn1_w_gate_up, ffn1_w_down=ffn1_w_down, mix_norm=mix_norm,
                   w_in=w_in, pool_w_group=pool_w_group, pool_scale=pool_scale, w_branch_pool=w_branch_pool,
                   w_branch_attn=w_branch_attn, w_out=w_out, ffn2_norm=ffn2_norm, ffn2_w_gate_up=ffn2_w_gate_up,
                   ffn2_w_down=ffn2_w_down, final_norm=final_norm)
    first = dict(ffn1_norm=m_ffn1_norm, ffn1_w_gate_up=m_ffn1_w_gate_up, ffn1_w_down=m_ffn1_w_down,
                 mix_norm=m_mix_norm, w_in=m_w_in, pool_w_group=m_pool_w_group, pool_scale=m_pool_scale,
                 w_branch_pool=m_w_branch_pool, w_branch_attn=m_w_branch_attn, w_out=m_w_out,
                 ffn2_norm=m_ffn2_norm, ffn2_w_gate_up=m_ffn2_w_gate_up, ffn2_w_down=m_ffn2_w_down,
                 final_norm=m_final_norm)
    second = dict(ffn1_norm=v_ffn1_norm, ffn1_w_gate_up=v_ffn1_w_gate_up, ffn1_w_down=v_ffn1_w_down,
                  mix_norm=v_mix_norm, w_in=v_w_in, pool_w_group=v_pool_w_group, pool_scale=v_pool_scale,
                  w_branch_pool=v_w_branch_pool, w_branch_attn=v_w_branch_attn, w_out=v_w_out,
                  ffn2_norm=v_ffn2_norm, ffn2_w_gate_up=v_ffn2_w_gate_up, ffn2_w_down=v_ffn2_w_down,
                  final_norm=v_final_norm)
    order = list(weights)

    wgu1, wd1 = _all_gather([_pad_gate_up(ffn1_w_gate_up[0]), _pad_down(ffn1_w_down[0])],
                            name="all_gather_ffn1", collective_id=0)
    win_g, = _all_gather([w_in[0].astype(BF16)], name="all_gather_w_in", collective_id=1)
    wbp_g, wba_g, wout_g = _all_gather(
        [w_branch_pool[0].astype(BF16), w_branch_attn[0].astype(BF16), w_out[0].astype(BF16)],
        name="all_gather_mix", collective_id=2)
    wgu2, wd2 = _all_gather([_pad_gate_up(ffn2_w_gate_up[0]), _pad_down(ffn2_w_down[0])],
                            name="all_gather_ffn2", collective_id=3)
    wd1 = wd1.reshape(N_DEV * FF_SHARD_PAD, D)
    wd2 = wd2.reshape(N_DEV * FF_SHARD_PAD, D)
    wout_g = wout_g.reshape(D, D)

    cross_ids = {"ffn2": 4, "mix": 5, "w_in": 6, "ffn1": 7}

    def exchange(tag, group):
        sums = [_chip_sums(g, name=f"chip_sums_{tag}_{i}") for i, g in enumerate(group)]
        landed = _cross_chips([s[0] for s in sums], name="cross_chips_" + tag, collective_id=cross_ids[tag])
        return [(s[1], l) for s, l in zip(sums, landed)]

    norms = (ffn1_norm, mix_norm, ffn2_norm, final_norm.reshape(1, D))
    loss, dx, sharded, replicated = _local_step(
        x[0], loss_target[0], norms, pool_w_group[0], pool_scale, wgu1, wd1, win_g, wbp_g, wba_g, wout_g, wgu2, wd2,
        exchange)
    names = ["ffn1_w_gate_up", "ffn1_w_down", "w_in", "w_branch_pool", "w_branch_attn", "w_out",
             "ffn2_w_gate_up", "ffn2_w_down"]
    grads = {k: _owner_sum(own, landed, name="owner_sum_" + k)
             for k, (own, landed) in reversed(list(zip(names, sharded)))}
    for k in ("ffn1_w_gate_up", "ffn2_w_gate_up"):
        grads[k] = _unpad_gate_up(grads[k])
    for k in ("ffn1_w_down", "ffn2_w_down"):
        grads[k] = grads[k][:FF_SHARD]

    d_n1, d_nm, d_n2, d_nf, d_scale, d_wgroup = replicated
    small = ["ffn1_norm", "mix_norm", "ffn2_norm", "final_norm", "pool_scale", "pool_w_group"]
    def tile_rows(a):
        a = a.reshape(-1, 128)
        return jnp.pad(a, ((0, -a.shape[0] % 8), (0, 0)))

    pieces = [tile_rows(d) for d in (d_n1, d_nm, d_n2, d_nf, d_scale, d_wgroup)]
    rows = [weights[k].size // 128 for k in small]
    starts = [sum(p.shape[0] for p in pieces[:i]) for i in range(len(pieces) + 1)]
    slab = jnp.concatenate(pieces + [jnp.broadcast_to(loss, (8, 128))], axis=0)
    slabs, = _all_gather([slab], name="all_gather_replicated", collective_id=8)
    total = _sum_devices(slabs, name="sum_replicated")
    loss_out = total[starts[-1], 0]

    small_w = jnp.concatenate([tile_rows(weights[k]) for k in small], axis=0)
    small_m = jnp.concatenate([tile_rows(first[k]) for k in small], axis=0)
    small_v = jnp.concatenate([tile_rows(second[k]) for k in small], axis=0)
    small_out = _adamw(small_w, total[:starts[-1]], small_m, small_v, name="adamw_replicated")
    delta, new_m, new_v = {}, {}, {}
    for name_, start, n_rows in zip(small, starts, rows):
        shape = weights[name_].shape
        grads[name_] = total[start:start + n_rows].reshape(shape)
        delta[name_], new_m[name_], new_v[name_] = (a[start:start + n_rows].reshape(shape) for a in small_out)
    for name_ in order:
        if name_ in small:
            continue
        shape = weights[name_].shape
        two_d = shape[1:]
        out = _adamw(weights[name_].reshape(two_d), grads[name_], first[name_].reshape(two_d),
                     second[name_].reshape(two_d), name="adamw_" + name_)
        delta[name_], new_m[name_], new_v[name_] = (a.reshape(shape) for a in out)
        grads[name_] = grads[name_].reshape(shape)

    return (loss_out, dx[None], *[grads[k] for k in order], *[delta[k] for k in order],
            *[new_m[k] for k in order], *[new_v[k] for k in order])
```

```python
import functools

import jax
import jax.numpy as jnp
from jax import lax
from jax.experimental import pallas as pl
from jax.experimental.pallas import tpu as pltpu
from jax.experimental.pallas import tpu_sc as plsc

F32 = jnp.float32
BF16 = jnp.bfloat16
MESH = pl.DeviceIdType.MESH

RMS_EPS = 1e-6
N_DEV = 8
N_HEADS = 8
HEAD_DIM = 64
HEAD_PAIR = 2 * HEAD_DIM
POOL_WINDOWS = (2, 4, 8, 16)
POOL_GROUP = 128
POOL_WIDTH = 512
SB_WIDTH = 512
FF_SHARD = 352
FF_SHARD_PAD = 384
ATTN_BLOCK = 256
ATTN_SCALE = 0.125

ADAM_LR = 0.001
ADAM_B1 = 0.9
ADAM_B2 = 0.999
ADAM_EPS = 1e-08
ADAM_WD = 0.01
ADAM_STEP = 10

VMEM_LIMIT = 48 << 20
WGRAD_TOKENS = 2048


def _params(dims=None):
    return pltpu.CompilerParams(dimension_semantics=dims, vmem_limit_bytes=VMEM_LIMIT)


def _mm(a, b):
    return jnp.dot(a, b, preferred_element_type=F32)


def _mm_nt(a, b):
    return lax.dot_general(a, b, (((1,), (1,)), ((), ())), preferred_element_type=F32)


def _mm_tn(a, b):
    return lax.dot_general(a, b, (((0,), (0,)), ((), ())), preferred_element_type=F32)


def _row_tile(rows, cols):
    limit = max(8, (512 * 1024) // cols)
    return max(t for t in range(8, rows + 1, 8) if rows % t == 0 and (t <= limit or t == 8))


def _rstd(xf):
    return lax.rsqrt(jnp.mean(xf * xf, axis=-1, keepdims=True) + RMS_EPS)


def _rms_bwd(xf, gain, dn):
    r = _rstd(xf)
    xh = xf * r
    dgain = jnp.sum(dn * xh, axis=0, keepdims=True)
    dxh = dn * gain
    dx = r * (dxh - xh * jnp.mean(dxh * xh, axis=-1, keepdims=True))
    return dx, dgain


def _ffn_fwd(x, gain, wgu, wd, *, tm, name):
    T, D = x.shape
    tm = min(tm, T)
    nb, bw = wgu.shape[0] // 2, wgu.shape[1]

    def body(x_ref, gain_ref, wg_ref, wu_ref, wd_ref, h_ref, gu_ref, n_scr, acc):
        j = pl.program_id(1)

        @pl.when(j == 0)
        def _():
            xf = x_ref[...]
            n_scr[...] = (xf * _rstd(xf) * gain_ref[...]).astype(BF16)
            acc[...] = jnp.zeros_like(acc)

        n = n_scr[...]
        g = _mm_nt(n, wg_ref[...])
        u = _mm_nt(n, wu_ref[...])
        gu_ref[0] = g.astype(BF16)
        gu_ref[1] = u.astype(BF16)
        hid = (g * jax.nn.sigmoid(g) * u).astype(BF16)
        acc[...] += _mm(hid, wd_ref[...])

        @pl.when(j == nb - 1)
        def _():
            h_ref[...] = x_ref[...] + 0.5 * acc[...]

    return pl.pallas_call(
        body, name=name, grid=(T // tm, nb),
        in_specs=[
            pl.BlockSpec((tm, D), lambda i, j: (i, 0)),
            pl.BlockSpec((1, D), lambda i, j: (0, 0)),
            pl.BlockSpec((None, bw, D), lambda i, j: (j, 0, 0)),
            pl.BlockSpec((None, bw, D), lambda i, j: (j + nb, 0, 0)),
            pl.BlockSpec((bw, D), lambda i, j: (j, 0)),
        ],
        out_specs=[
            pl.BlockSpec((tm, D), lambda i, j: (i, 0)),
            pl.BlockSpec((2, tm, bw), lambda i, j: (0, i, j)),
        ],
        out_shape=[jax.ShapeDtypeStruct((T, D), F32), jax.ShapeDtypeStruct((2, T, nb * bw), BF16)],
        scratch_shapes=[pltpu.VMEM((tm, D), BF16), pltpu.VMEM((tm, D), F32)],
        compiler_params=_params(("arbitrary", "arbitrary")),
    )(x, gain, wgu, wgu, wd)


def _ffn_bwd(dh, x, gain, gu, wgu, wd, *, tm, name):
    T, D = x.shape
    tm = min(tm, T)
    nb, bw = wgu.shape[0] // 2, wgu.shape[1]

    def body(dh_ref, x_ref, gain_ref, gu_ref, wg_ref, wu_ref, wd_ref,
             dx_ref, dgain_ref, n_ref, df_ref, dgu_ref, hid_ref, dn_acc):
        i, j = pl.program_id(0), pl.program_id(1)

        @pl.when(j == 0)
        def _():
            xf = x_ref[...]
            n_ref[...] = (xf * _rstd(xf) * gain_ref[...]).astype(BF16)
            df_ref[...] = (0.5 * dh_ref[...]).astype(BF16)
            dn_acc[...] = jnp.zeros_like(dn_acc)

        @pl.when((i == 0) & (j == 0))
        def _():
            dgain_ref[...] = jnp.zeros_like(dgain_ref)

        dhid = _mm_nt(df_ref[...], wd_ref[...])
        g = gu_ref[0].astype(F32)
        u = gu_ref[1].astype(F32)
        s = jax.nn.sigmoid(g)
        silu = g * s
        hid_ref[...] = (silu * u).astype(BF16)
        dg = (dhid * u * (s * (1.0 + g * (1.0 - s)))).astype(BF16)
        du = (dhid * silu).astype(BF16)
        dgu_ref[0] = dg
        dgu_ref[1] = du
        dn_acc[...] += _mm(dg, wg_ref[...]) + _mm(du, wu_ref[...])

        @pl.when(j == nb - 1)
        def _():
            dx, dgain = _rms_bwd(x_ref[...], gain_ref[...], dn_acc[...])
            dx_ref[...] = dh_ref[...] + dx
            dgain_ref[...] += dgain

    row = lambda i, j: (i, 0)
    return pl.pallas_call(
        body, name=name, grid=(T // tm, nb),
        in_specs=[
            pl.BlockSpec((tm, D), row),
            pl.BlockSpec((tm, D), row),
            pl.BlockSpec((1, D), lambda i, j: (0, 0)),
            pl.BlockSpec((2, tm, bw), lambda i, j: (0, i, j)),
            pl.BlockSpec((None, bw, D), lambda i, j: (j, 0, 0)),
            pl.BlockSpec((None, bw, D), lambda i, j: (j + nb, 0, 0)),
            pl.BlockSpec((bw, D), lambda i, j: (j, 0)),
        ],
        out_specs=[
            pl.BlockSpec((tm, D), row),
            pl.BlockSpec((1, D), lambda i, j: (0, 0)),
            pl.BlockSpec((tm, D), row),
            pl.BlockSpec((tm, D), row),
            pl.BlockSpec((2, tm, bw), lambda i, j: (0, i, j)),
            pl.BlockSpec((tm, bw), lambda i, j: (i, j)),
        ],
        out_shape=[
            jax.ShapeDtypeStruct((T, D), F32),
            jax.ShapeDtypeStruct((1, D), F32),
            jax.ShapeDtypeStruct((T, D), BF16),
            jax.ShapeDtypeStruct((T, D), BF16),
            jax.ShapeDtypeStruct((2, T, nb * bw), BF16),
            jax.ShapeDtypeStruct((T, nb * bw), BF16),
        ],
        scratch_shapes=[pltpu.VMEM((tm, D), F32)],
        compiler_params=_params(("arbitrary", "arbitrary")),
    )(dh, x, gain, gu, wgu, wgu, wd)


def _wgrad(a, b, *, grid, a_spec, b_spec, out_spec, out_shape, acc_shape, name, split_lanes=0):
    nk = grid[2]

    def body(a_ref, b_ref, o_ref, acc):
        k = pl.program_id(2)

        @pl.when(k == 0)
        def _():
            acc[...] = jnp.zeros_like(acc)

        acc[...] += _mm_tn(a_ref[...].astype(BF16), b_ref[...].astype(BF16))

        @pl.when(k == nk - 1)
        def _():
            if split_lanes:
                for e in range(o_ref.shape[0]):
                    o_ref[e] = acc[:, e * split_lanes:(e + 1) * split_lanes].astype(o_ref.dtype)
            else:
                o_ref[...] = acc[...].astype(o_ref.dtype)

    return pl.pallas_call(
        body, name=name, grid=grid, in_specs=[a_spec, b_spec], out_specs=out_spec,
        out_shape=jax.ShapeDtypeStruct(out_shape, BF16),
        scratch_shapes=[pltpu.VMEM(acc_shape, F32)],
        compiler_params=_params(("arbitrary", "arbitrary", "arbitrary")),
    )(a, b)


def _wgrad_gate_up(n, dgu, *, tk, name):
    T, D = n.shape
    tk = min(tk, T)
    bw = FF_SHARD_PAD * 2
    nb = dgu.shape[2] // bw
    return _wgrad(
        dgu, n, grid=(2 * nb, 1, T // tk), name=name,
        a_spec=pl.BlockSpec((None, tk, bw), lambda m, c, k: (m // nb, k, m % nb)),
        b_spec=pl.BlockSpec((tk, D), lambda m, c, k: (k, 0)),
        out_spec=pl.BlockSpec((None, bw, D), lambda m, c, k: (m, 0, 0)),
        out_shape=(2 * nb, bw, D), acc_shape=(bw, D))


def _wgrad_down(hid, df, *, tk, name):
    T, D = df.shape
    tk = min(tk, T)
    bw = FF_SHARD_PAD * 2
    nb = hid.shape[1] // bw
    return _wgrad(
        hid, df, grid=(nb, 1, T // tk), name=name,
        a_spec=pl.BlockSpec((tk, bw), lambda m, c, k: (k, m)),
        b_spec=pl.BlockSpec((tk, D), lambda m, c, k: (k, 0)),
        out_spec=pl.BlockSpec((bw, D), lambda m, c, k: (m, 0)),
        out_shape=(nb * bw, D), acc_shape=(bw, D))


def _wgrad_in(un, dproj, *, tk, name):
    T, D = un.shape
    tk = min(tk, T)
    bw = dproj.shape[1] // N_DEV
    return _wgrad(
        un, dproj, grid=(1, N_DEV, T // tk), name=name,
        a_spec=pl.BlockSpec((tk, D), lambda m, c, k: (k, 0)),
        b_spec=pl.BlockSpec((tk, bw), lambda m, c, k: (k, c)),
        out_spec=pl.BlockSpec((None, D, bw), lambda m, c, k: (c, 0, 0)),
        out_shape=(N_DEV, D, bw), acc_shape=(D, bw))


def _wgrad_full(a, b, *, tk, name, split_lanes=0):
    T, M = a.shape
    tk = min(tk, T)
    N = b.shape[1]
    if split_lanes:
        out_shape = (N // split_lanes, M, split_lanes)
        out_spec = pl.BlockSpec(out_shape, lambda m, c, k: (0, 0, 0))
    else:
        out_shape = (M, N)
        out_spec = pl.BlockSpec(out_shape, lambda m, c, k: (0, 0))
    return _wgrad(
        a, b, grid=(1, 1, T // tk), name=name,
        a_spec=pl.BlockSpec((tk, M), lambda m, c, k: (k, 0)),
        b_spec=pl.BlockSpec((tk, N), lambda m, c, k: (k, 0)),
        out_spec=out_spec, out_shape=out_shape, acc_shape=(M, N), split_lanes=split_lanes)


def _loss_bwd(h, target, gain, *, tm, name):
    T, D = h.shape
    tm = min(tm, T)

    def body(h_ref, t_ref, gain_ref, dh_ref, loss_ref, dgain_ref):
        @pl.when(pl.program_id(0) == 0)
        def _():
            loss_ref[...] = jnp.zeros_like(loss_ref)
            dgain_ref[...] = jnp.zeros_like(dgain_ref)

        xf = h_ref[...]
        gain = gain_ref[...]
        err = xf * _rstd(xf) * gain - t_ref[...]
        loss_ref[...] += 0.5 * jnp.sum(jnp.mean(err * err, axis=-1, keepdims=True), axis=0, keepdims=True)
        dx, dgain = _rms_bwd(xf, gain, err * (1.0 / D))
        dh_ref[...] = dx
        dgain_ref[...] += dgain

    row = lambda i: (i, 0)
    fixed = lambda i: (0, 0)
    return pl.pallas_call(
        body, name=name, grid=(T // tm,),
        in_specs=[pl.BlockSpec((tm, D), row), pl.BlockSpec((tm, D), row), pl.BlockSpec((1, D), fixed)],
        out_specs=[pl.BlockSpec((tm, D), row), pl.BlockSpec((1, 128), fixed), pl.BlockSpec((1, D), fixed)],
        out_shape=[jax.ShapeDtypeStruct((T, D), F32), jax.ShapeDtypeStruct((1, 128), F32),
                   jax.ShapeDtypeStruct((1, D), F32)],
        compiler_params=_params(("arbitrary",)),
    )(h, target, gain)


def _inproj_fwd(h, gain, w_in, *, tm, name):
    T, D = h.shape
    tm = min(tm, T)
    nb, bw = w_in.shape[0], w_in.shape[2]

    def body(h_ref, gain_ref, w_ref, un_ref, proj_ref):
        @pl.when(pl.program_id(1) == 0)
        def _():
            xf = h_ref[...]
            un_ref[...] = (xf * _rstd(xf) * gain_ref[...]).astype(BF16)

        proj_ref[...] = _mm(un_ref[...], w_ref[...])

    return pl.pallas_call(
        body, name=name, grid=(T // tm, nb),
        in_specs=[
            pl.BlockSpec((tm, D), lambda i, j: (i, 0)),
            pl.BlockSpec((1, D), lambda i, j: (0, 0)),
            pl.BlockSpec((None, D, bw), lambda i, j: (j, 0, 0)),
        ],
        out_specs=[pl.BlockSpec((tm, D), lambda i, j: (i, 0)), pl.BlockSpec((tm, bw), lambda i, j: (i, j))],
        out_shape=[jax.ShapeDtypeStruct((T, D), BF16), jax.ShapeDtypeStruct((T, nb * bw), F32)],
        compiler_params=_params(("arbitrary", "arbitrary")),
    )(h, gain, w_in)


def _inproj_bwd(dproj, dh, h, gain, w_in, *, tm, name):
    T, D = h.shape
    tm = min(tm, T)
    nb, bw = w_in.shape[0], w_in.shape[2]

    def body(dp_ref, dh_ref, h_ref, gain_ref, w_ref, dx_ref, dgain_ref, acc):
        i, j = pl.program_id(0), pl.program_id(1)

        @pl.when(j == 0)
        def _():
            acc[...] = jnp.zeros_like(acc)

        @pl.when((i == 0) & (j == 0))
        def _():
            dgain_ref[...] = jnp.zeros_like(dgain_ref)

        acc[...] += _mm_nt(dp_ref[...], w_ref[...])

        @pl.when(j == nb - 1)
        def _():
            dx, dgain = _rms_bwd(h_ref[...], gain_ref[...], acc[...])
            dx_ref[...] = dh_ref[...] + dx
            dgain_ref[...] += dgain

    row = lambda i, j: (i, 0)
    return pl.pallas_call(
        body, name=name, grid=(T // tm, nb),
        in_specs=[
            pl.BlockSpec((tm, bw), lambda i, j: (i, j)),
            pl.BlockSpec((tm, D), row),
            pl.BlockSpec((tm, D), row),
            pl.BlockSpec((1, D), lambda i, j: (0, 0)),
            pl.BlockSpec((None, D, bw), lambda i, j: (j, 0, 0)),
        ],
        out_specs=[pl.BlockSpec((tm, D), row), pl.BlockSpec((1, D), lambda i, j: (0, 0))],
        out_shape=[jax.ShapeDtypeStruct((T, D), F32), jax.ShapeDtypeStruct((1, D), F32)],
        scratch_shapes=[pltpu.VMEM((tm, D), F32)],
        compiler_params=_params(("arbitrary", "arbitrary")),
    )(dproj, dh, h, gain, w_in)


def _window_sum(x, row, doublings, *, backward):
    T = x.shape[0]
    s = x
    for k in range(doublings):
        sh = 1 << k
        if backward:
            s = s + jnp.where(row < T - sh, pltpu.roll(s, T - sh, 0), 0.0)
        else:
            s = s + jnp.where(row >= sh, pltpu.roll(s, sh, 0), 0.0)
    return s


def _pool_fwd(proj, w_group, scale, *, name):
    T = proj.shape[0]

    def body(xp_ref, w_ref, scale_ref, p_ref):
        row = lax.broadcasted_iota(jnp.int32, (T, POOL_GROUP), 0)
        for gi, window in enumerate(POOL_WINDOWS):
            cols = slice(gi * POOL_GROUP, (gi + 1) * POOL_GROUP)
            x = xp_ref[:, cols]
            inv_count = 1.0 / jnp.minimum(row + 1, window).astype(F32)
            yc = _window_sum(x, row, gi + 1, backward=False) * inv_count - x
            pre = _mm(yc.astype(BF16), w_ref[gi].astype(BF16))
            p_ref[:, cols] = pre * scale_ref[:, cols]

    return pl.pallas_call(
        body, name=name, grid=(1,),
        in_specs=[
            pl.BlockSpec((T, POOL_WIDTH), lambda i: (0, 0)),
            pl.BlockSpec(w_group.shape, lambda i: (0, 0, 0)),
            pl.BlockSpec((1, POOL_WIDTH), lambda i: (0, 0)),
        ],
        out_specs=pl.BlockSpec((T, POOL_WIDTH), lambda i: (0, 0)),
        out_shape=jax.ShapeDtypeStruct((T, POOL_WIDTH), F32),
        compiler_params=_params(("arbitrary",)),
    )(proj, w_group, scale)


def _pool_bwd(dp, proj, w_group, scale, *, name):
    T = proj.shape[0]

    def body(dp_ref, xp_ref, w_ref, scale_ref, dxp_ref, dw_ref, dscale_ref):
        row = lax.broadcasted_iota(jnp.int32, (T, POOL_GROUP), 0)
        for gi, window in enumerate(POOL_WINDOWS):
            cols = slice(gi * POOL_GROUP, (gi + 1) * POOL_GROUP)
            x = xp_ref[:, cols]
            inv_count = 1.0 / jnp.minimum(row + 1, window).astype(F32)
            yc = (_window_sum(x, row, gi + 1, backward=False) * inv_count - x).astype(BF16)
            w = w_ref[gi].astype(BF16)
            pre = _mm(yc, w)
            dpg = dp_ref[:, cols]
            dscale_ref[:, cols] = jnp.sum(dpg * pre, axis=0, keepdims=True)
            dpre = (dpg * scale_ref[:, cols]).astype(BF16)
            dw_ref[gi] = _mm_tn(yc, dpre)
            dyc = _mm_nt(dpre, w)
            dxp_ref[:, cols] = _window_sum(dyc * inv_count, row, gi + 1, backward=True) - dyc

    return pl.pallas_call(
        body, name=name, grid=(1,),
        in_specs=[
            pl.BlockSpec((T, POOL_WIDTH), lambda i: (0, 0)),
            pl.BlockSpec((T, POOL_WIDTH), lambda i: (0, 0)),
            pl.BlockSpec(w_group.shape, lambda i: (0, 0, 0)),
            pl.BlockSpec((1, POOL_WIDTH), lambda i: (0, 0)),
        ],
        out_specs=[
            pl.BlockSpec((T, POOL_WIDTH), lambda i: (0, 0)),
            pl.BlockSpec(w_group.shape, lambda i: (0, 0, 0)),
            pl.BlockSpec((1, POOL_WIDTH), lambda i: (0, 0)),
        ],
        out_shape=[jax.ShapeDtypeStruct((T, POOL_WIDTH), F32), jax.ShapeDtypeStruct(w_group.shape, F32),
                   jax.ShapeDtypeStruct((1, POOL_WIDTH), F32)],
        compiler_params=_params(("arbitrary",)),
    )(dp, proj, w_group, scale)


ATTN_STRIP = 32


def _log_sigmoids(z):
    lb = jnp.minimum(z, 0.0) - jnp.log(1.0 + jnp.exp(-jnp.abs(z)))
    return lb, lb - z


def _transposed_blocks(x_ref, blocks_scr, tq):
    for b in range(blocks_scr.shape[0]):
        blocks_scr[b] = x_ref[b * tq:(b + 1) * tq, :].T.astype(BF16)


def _split_bf16(x):
    hi = x.astype(BF16)
    return hi, (x - hi.astype(F32)).astype(BF16)


def _strips(n):
    return [slice(i, i + ATTN_STRIP) for i in range(0, n, ATTN_STRIP)]


def _rows(parts):
    return jnp.concatenate(parts, axis=0)


def _attn_specs(T, tq):
    q_col = POOL_WIDTH // HEAD_PAIR
    k_col = q_col + SB_WIDTH // HEAD_PAIR
    v_col = k_col + SB_WIDTH // HEAD_PAIR
    return [
        pl.BlockSpec((tq, HEAD_PAIR), lambda p, i: (i, q_col + p)),
        pl.BlockSpec((T, HEAD_PAIR), lambda p, i: (0, k_col + p)),
        pl.BlockSpec((T, HEAD_PAIR), lambda p, i: (0, v_col + p)),
    ]


def _attn_fwd(proj, *, name):
    T = proj.shape[0]
    tq = ATTN_BLOCK

    def body(q_ref, k_ref, v_ref, o_ref, lt_ref, kt_scr, vb_scr):
        qi = pl.program_id(1)

        @pl.when(qi == 0)
        def _():
            _transposed_blocks(k_ref, kt_scr, tq)
            vb_scr[...] = v_ref[...].astype(BF16)

        head0 = lax.broadcasted_iota(jnp.int32, (tq, HEAD_PAIR), 1) < HEAD_DIM
        q = q_ref[...] * ATTN_SCALE
        qs = (jnp.where(head0, q, 0.0).astype(BF16), jnp.where(head0, 0.0, q).astype(BF16))
        r = lax.broadcasted_iota(jnp.int32, (tq, tq), 0)
        c = lax.broadcasted_iota(jnp.int32, (tq, tq), 1)
        later = (r > c).astype(BF16)
        later2 = _rows([later, later])
        causal = lambda rows: c[rows] < r[rows]
        strips = _strips(tq)

        def log_terms(z, valid):
            lbs, his, los, sums = [], [], [], []
            for rows in strips:
                lb, lm = _log_sigmoids(z[rows])
                if valid is not None:
                    lm = jnp.where(valid(rows), lm, 0.0)
                hi, lo = _split_bf16(lm)
                lbs.append(lb)
                his.append(hi)
                los.append(lo)
                sums.append(jnp.sum(lm, axis=1, keepdims=True))
            return lbs, jnp.concatenate([_rows(his), _rows(los)], axis=1), _rows(sums)

        def weights(lbs, run, after, valid):
            parts = []
            for rows, lb in zip(strips, lbs):
                a = jnp.exp(lb + run[rows] + after[rows])
                if valid is not None:
                    a = jnp.where(valid(rows), a, 0.0)
                parts.append(a.astype(BF16))
            return _rows(parts)

        def block(kj, carry, valid):
            kt = kt_scr[kj]
            vb = vb_scr[pl.ds(pl.multiple_of(kj * tq, tq), tq), :]
            run0, o0, run1, o1 = carry
            z0 = _mm(qs[0], kt)
            z1 = _mm(qs[1], kt)
            lbs0, split0, sums0 = log_terms(z0, valid)
            after0 = _mm(split0, later2)
            lbs1, split1, sums1 = log_terms(z1, valid)
            after1 = _mm(split1, later2)
            o0 = o0 + _mm(weights(lbs0, run0, after0, valid), vb)
            o1 = o1 + _mm(weights(lbs1, run1, after1, valid), vb)
            return run0 + sums0, o0, run1 + sums1, o1

        zero = (jnp.zeros((tq, 1), F32), jnp.zeros((tq, HEAD_PAIR), F32))
        carry = block(qi, zero + zero, causal)
        carry = lax.fori_loop(0, qi, lambda it, cr: block(qi - 1 - it, cr, None), carry)
        o_ref[...] = jnp.where(head0, carry[1], carry[3])
        lt_ref[...] = jnp.where(head0, carry[0], carry[2])

    out_spec = pl.BlockSpec((tq, HEAD_PAIR), lambda p, i: (i, p))
    return pl.pallas_call(
        body, name=name, grid=(N_HEADS // 2, T // tq),
        in_specs=_attn_specs(T, tq), out_specs=[out_spec, out_spec],
        out_shape=[jax.ShapeDtypeStruct((T, SB_WIDTH), F32), jax.ShapeDtypeStruct((T, SB_WIDTH), F32)],
        scratch_shapes=[pltpu.VMEM((T // tq, HEAD_PAIR, tq), BF16), pltpu.VMEM((T, HEAD_PAIR), BF16)],
        compiler_params=_params(("arbitrary", "arbitrary")),
    )(proj, proj, proj)


def _attn_bwd(proj, do, ltot, *, name):
    T = proj.shape[0]
    tq = ATTN_BLOCK

    def body(q_ref, k_ref, v_ref, do_ref, lt_ref, dq_ref, dkt_ref, dvt_ref, kb_scr, kt_scr, vt_scr):
        qi = pl.program_id(1)

        @pl.when(qi == 0)
        def _():
            kb_scr[...] = k_ref[...].astype(BF16)
            _transposed_blocks(k_ref, kt_scr, tq)
            _transposed_blocks(v_ref, vt_scr, tq)
            dkt_ref[...] = jnp.zeros_like(dkt_ref)
            dvt_ref[...] = jnp.zeros_like(dvt_ref)

        head0 = lax.broadcasted_iota(jnp.int32, (tq, HEAD_PAIR), 1) < HEAD_DIM
        q, do_, lt = q_ref[...] * ATTN_SCALE, do_ref[...], lt_ref[...]
        qs = (jnp.where(head0, q, 0.0).astype(BF16), jnp.where(head0, 0.0, q).astype(BF16))
        q_heads = (jnp.where(head0, q, 0.0), jnp.where(head0, 0.0, q))
        do_heads = (jnp.where(head0, do_, 0.0), jnp.where(head0, 0.0, do_))
        dos = tuple(d.astype(BF16) for d in do_heads)
        qts = tuple(x.T.astype(BF16) for x in q_heads)
        dots = tuple(d.T.astype(BF16) for d in do_heads)
        lts = (jnp.max(jnp.where(head0, lt, -jnp.inf), axis=1, keepdims=True),
               jnp.max(jnp.where(head0, -jnp.inf, lt), axis=1, keepdims=True))
        r = lax.broadcasted_iota(jnp.int32, (tq, tq), 0)
        c = lax.broadcasted_iota(jnp.int32, (tq, tq), 1)
        upto = (r <= c).astype(BF16)
        before = (r < c).astype(BF16)
        upto2, before2 = _rows([upto, upto]), _rows([before, before])
        causal = lambda rows: c[rows] < r[rows]
        strips = _strips(tq)

        def log_terms(z, valid):
            lbs, his, los, sums = [], [], [], []
            for rows in strips:
                lb, lm = _log_sigmoids(z[rows])
                if valid is not None:
                    lm = jnp.where(valid(rows), lm, 0.0)
                hi, lo = _split_bf16(lm)
                lbs.append(lb)
                his.append(hi)
                los.append(lo)
                sums.append(jnp.sum(lm, axis=1, keepdims=True))
            return lbs, jnp.concatenate([_rows(his), _rows(los)], axis=1), _rows(sums)

        def weights(lbs, rest, lm_upto, da, valid):
            a_parts, es, his, los, sums = [], [], [], [], []
            for rows, lb in zip(strips, lbs):
                a = jnp.exp(lb + (rest[rows] - lm_upto[rows]))
                if valid is not None:
                    a = jnp.where(valid(rows), a, 0.0)
                e = da[rows] * a
                hi, lo = _split_bf16(e)
                a_parts.append(a.astype(BF16))
                es.append(e)
                his.append(hi)
                los.append(lo)
                sums.append(jnp.sum(e, axis=1, keepdims=True))
            return _rows(a_parts), es, jnp.concatenate([_rows(his), _rows(los)], axis=1), _rows(sums)

        def score_grads(lbs, es, run_e, e_before, valid):
            parts = []
            for rows, lb, e in zip(strips, lbs, es):
                beta = jnp.exp(lb)
                dz = e * (1.0 - beta) - (run_e[rows] + e_before[rows]) * beta
                if valid is not None:
                    dz = jnp.where(valid(rows), dz, 0.0)
                parts.append(dz.astype(BF16))
            return _rows(parts)

        def block(kj, carry, valid):
            off = pl.multiple_of(kj * tq, tq)
            kb, kt, vt = kb_scr[pl.ds(off, tq), :], kt_scr[kj], vt_scr[kj]
            run_lm0, run_e0, dq0, run_lm1, run_e1, dq1 = carry
            z0, da0 = _mm(qs[0], kt), _mm(dos[0], vt)
            z1, da1 = _mm(qs[1], kt), _mm(dos[1], vt)
            lbs0, split0, lm_sums0 = log_terms(z0, valid)
            lm_upto0 = _mm(split0, upto2)
            lbs1, split1, lm_sums1 = log_terms(z1, valid)
            lm_upto1 = _mm(split1, upto2)
            a0, es0, split0, e_sums0 = weights(lbs0, lts[0] - run_lm0, lm_upto0, da0, valid)
            e_before0 = _mm(split0, before2)
            a1, es1, split1, e_sums1 = weights(lbs1, lts[1] - run_lm1, lm_upto1, da1, valid)
            e_before1 = _mm(split1, before2)
            dz0 = score_grads(lbs0, es0, run_e0, e_before0, valid)
            dkt_blk = _mm(qts[0], dz0)
            dvt_blk = _mm(dots[0], a0)
            dq0 = dq0 + _mm(dz0, kb)
            dz1 = score_grads(lbs1, es1, run_e1, e_before1, valid)
            dkt_ref[kj] += dkt_blk + _mm(qts[1], dz1)
            dvt_ref[kj] += dvt_blk + _mm(dots[1], a1)
            dq1 = dq1 + _mm(dz1, kb)
            return run_lm0 + lm_sums0, run_e0 + e_sums0, dq0, run_lm1 + lm_sums1, run_e1 + e_sums1, dq1

        zero = (jnp.zeros((tq, 1), F32), jnp.zeros((tq, 1), F32), jnp.zeros((tq, HEAD_PAIR), F32))
        carry = lax.fori_loop(0, qi, lambda kj, cr: block(kj, cr, None), zero + zero)
        carry = block(qi, carry, causal)
        dq_ref[...] = jnp.where(head0, carry[2], carry[5]) * ATTN_SCALE

    blk = pl.BlockSpec((tq, HEAD_PAIR), lambda p, i: (i, p))
    seq = pl.BlockSpec((T // tq, HEAD_PAIR, tq), lambda p, i: (0, p, 0))
    transposed = jax.ShapeDtypeStruct((T // tq, SB_WIDTH, tq), F32)
    return pl.pallas_call(
        body, name=name, grid=(N_HEADS // 2, T // tq),
        in_specs=_attn_specs(T, tq) + [blk, blk], out_specs=[blk, seq, seq],
        out_shape=[jax.ShapeDtypeStruct((T, SB_WIDTH), F32), transposed, transposed],
        scratch_shapes=[pltpu.VMEM((T, HEAD_PAIR), BF16), pltpu.VMEM((T // tq, HEAD_PAIR, tq), BF16),
                        pltpu.VMEM((T // tq, HEAD_PAIR, tq), BF16)],
        compiler_params=_params(("arbitrary", "arbitrary")),
    )(proj, proj, proj, do, ltot)


def _branch(act_bf16, w_ref):
    return jnp.concatenate([_mm(act_bf16, w_ref[e]) for e in range(w_ref.shape[0])], axis=1)


def _mix_specs(T, D, tm, wbp, w_out):
    gate_col = (POOL_WIDTH + 3 * SB_WIDTH) // D
    row = lambda i: (i, 0)
    return [
        pl.BlockSpec((tm, D), row),
        pl.BlockSpec((tm, POOL_WIDTH), row),
        pl.BlockSpec((tm, SB_WIDTH), row),
        pl.BlockSpec((tm, D), lambda i: (i, gate_col)),
        pl.BlockSpec((tm, D), lambda i: (i, gate_col + 1)),
        pl.BlockSpec(wbp.shape, lambda i: (0, 0, 0)),
        pl.BlockSpec(wbp.shape, lambda i: (0, 0, 0)),
        pl.BlockSpec(w_out.shape, lambda i: (0, 0)),
    ]


def _mix_fwd(h, p, o, proj, wbp, wba, w_out, *, tm, name):
    T, D = h.shape
    tm = min(tm, T)

    def body(h_ref, p_ref, o_ref, glp_ref, gls_ref, wbp_ref, wba_ref, wout_ref, hout_ref, m_ref):
        yp = _branch(p_ref[...].astype(BF16), wbp_ref)
        ys = _branch(o_ref[...].astype(BF16), wba_ref)
        m = (jax.nn.sigmoid(glp_ref[...]) * yp + jax.nn.sigmoid(gls_ref[...]) * ys).astype(BF16)
        m_ref[...] = m
        hout_ref[...] = h_ref[...] + _mm(m, wout_ref[...])

    row = lambda i: (i, 0)
    return pl.pallas_call(
        body, name=name, grid=(T // tm,),
        in_specs=_mix_specs(T, D, tm, wbp, w_out),
        out_specs=[pl.BlockSpec((tm, D), row), pl.BlockSpec((tm, D), row)],
        out_shape=[jax.ShapeDtypeStruct((T, D), F32), jax.ShapeDtypeStruct((T, D), BF16)],
        compiler_params=_params(("arbitrary",)),
    )(h, p, o, proj, proj, wbp, wba, w_out)


def _mix_bwd(dh, p, o, proj, wbp, wba, w_out, *, tm, name):
    T, D = dh.shape
    tm = min(tm, T)
    bw = wbp.shape[2]

    def body(dh_ref, p_ref, o_ref, glp_ref, gls_ref, wbp_ref, wba_ref, wout_ref,
             dyp_ref, dys_ref, dp_ref, do_ref, dgl_ref):
        dm = _mm_nt(dh_ref[...].astype(BF16), wout_ref[...])
        yp = _branch(p_ref[...].astype(BF16), wbp_ref)
        ys = _branch(o_ref[...].astype(BF16), wba_ref)
        gp = jax.nn.sigmoid(glp_ref[...])
        gs = jax.nn.sigmoid(gls_ref[...])
        dyp = (dm * gp).astype(BF16)
        dys = (dm * gs).astype(BF16)
        dyp_ref[...] = dyp
        dys_ref[...] = dys
        dgl_ref[:, :D] = (dm * yp * gp * (1.0 - gp)).astype(BF16)
        dgl_ref[:, D:] = (dm * ys * gs * (1.0 - gs)).astype(BF16)
        dp = jnp.zeros(dp_ref.shape, F32)
        do_ = jnp.zeros(do_ref.shape, F32)
        for e in range(wbp_ref.shape[0]):
            dp += _mm_nt(dyp[:, e * bw:(e + 1) * bw], wbp_ref[e])
            do_ += _mm_nt(dys[:, e * bw:(e + 1) * bw], wba_ref[e])
        dp_ref[...] = dp
        do_ref[...] = do_

    row = lambda i: (i, 0)
    return pl.pallas_call(
        body, name=name, grid=(T // tm,),
        in_specs=_mix_specs(T, D, tm, wbp, w_out),
        out_specs=[pl.BlockSpec((tm, D), row), pl.BlockSpec((tm, D), row), pl.BlockSpec((tm, POOL_WIDTH), row),
                   pl.BlockSpec((tm, SB_WIDTH), row), pl.BlockSpec((tm, 2 * D), row)],
        out_shape=[jax.ShapeDtypeStruct((T, D), BF16), jax.ShapeDtypeStruct((T, D), BF16),
                   jax.ShapeDtypeStruct((T, POOL_WIDTH), F32), jax.ShapeDtypeStruct((T, SB_WIDTH), F32),
                   jax.ShapeDtypeStruct((T, 2 * D), BF16)],
        compiler_params=_params(("arbitrary",)),
    )(dh, p, o, proj, proj, wbp, wba, w_out)


def _adamw(w, g, m, v, *, name):
    R, C = w.shape
    tr = _row_tile(R, C)

    def body(w_ref, g_ref, m_ref, v_ref, d_ref, nm_ref, nv_ref):
        g_ = g_ref[...]
        m_ = ADAM_B1 * m_ref[...] + (1.0 - ADAM_B1) * g_
        v_ = ADAM_B2 * v_ref[...] + (1.0 - ADAM_B2) * (g_ * g_)
        m_hat = m_ / (1.0 - ADAM_B1 ** ADAM_STEP)
        v_hat = v_ / (1.0 - ADAM_B2 ** ADAM_STEP)
        d_ref[...] = -ADAM_LR * (m_hat / (jnp.sqrt(v_hat) + ADAM_EPS) + ADAM_WD * w_ref[...])
        nm_ref[...] = m_
        nv_ref[...] = v_

    spec = pl.BlockSpec((tr, C), lambda i: (i, 0))
    return pl.pallas_call(
        body, name=name, grid=(R // tr,), in_specs=[spec] * 4, out_specs=[spec] * 3,
        out_shape=[jax.ShapeDtypeStruct((R, C), F32)] * 3,
        compiler_params=_params(("arbitrary",)),
    )(w, g, m, v)


def _position():
    return lax.axis_index("x"), lax.axis_index("y"), lax.axis_index("c")


def _all_gather(shards, *, name, collective_id):
    n = len(shards)

    def body(*refs):
        ins, outs = refs[:n], refs[n:2 * n]
        send_sems, recv_sems, local_sems = refs[2 * n:]
        x, y, c = _position()
        me, sibling = (x, y, c), (x, y, 1 - c)
        chips = [(1 - x, y), (x, 1 - y), (1 - x, 1 - y)]

        barrier = pltpu.get_barrier_semaphore()
        for peer in [sibling] + [(*chip, c) for chip in chips]:
            pl.semaphore_signal(barrier, inc=1, device_id=peer, device_id_type=MESH)
        pl.semaphore_wait(barrier, 4)

        def block(a, pos):
            return outs[a].at[4 * pos[0] + 2 * pos[1] + pos[2]]

        def copy(a, k, pos, to, src=None):
            return pltpu.make_async_remote_copy(
                src_ref=block(a, pos) if src is None else src, dst_ref=block(a, pos),
                send_sem=send_sems.at[7 * a + k], recv_sem=recv_sems.at[7 * a + k],
                device_id=to, device_id_type=MESH)

        started = []
        for a in range(n):
            mine = pltpu.make_async_copy(ins[a], block(a, me), local_sems.at[a])
            mine.start()
            started.append(mine)
        sends = []
        for a in range(n):
            sends += [copy(a, 1 + j, me, (*chip, c), src=ins[a]) for j, chip in enumerate(chips)]
            sends.append(copy(a, 0, me, sibling, src=ins[a]))
        for cp in sends:
            cp.start()
        for j, chip in enumerate(chips):
            for a in range(n):
                copy(a, 1 + j, (*chip, c), me).wait_recv()
                passed = copy(a, 4 + j, (*chip, c), sibling)
                passed.start()
                sends.append(passed)
        for a in range(n):
            copy(a, 0, sibling, me).wait_recv()
            for j, chip in enumerate(chips):
                copy(a, 4 + j, (*chip, 1 - c), me).wait_recv()
        for cp in sends:
            cp.wait_send()
        for cp in started:
            cp.wait()

    return pl.kernel(
        body, name=name,
        out_type=[jax.ShapeDtypeStruct((N_DEV,) + s.shape, s.dtype) for s in shards],
        mesh=plsc.ScalarSubcoreMesh(axis_name="sequencer", num_cores=1),
        scratch_types=[pltpu.SemaphoreType.DMA((7 * n,)), pltpu.SemaphoreType.DMA((7 * n,)),
                       pltpu.SemaphoreType.DMA((n,))],
        compiler_params=pltpu.CompilerParams(collective_id=collective_id),
    )(*shards)


def _chip_sums(grads, *, name):
    _, R, C = grads.shape
    rc = 128 if R % 128 == 0 else R

    def body(g_ref, partial, out_ref, mine, theirs, send_sems, recv_sems, local_sems):
        x, y, c = _position()
        my_chip = 2 * x + y

        def swap(s):
            return pltpu.make_async_remote_copy(
                src_ref=g_ref.at[2 * s + (1 - c)], dst_ref=theirs.at[s],
                send_sem=send_sems.at[s], recv_sem=recv_sems.at[s],
                device_id=(x, y, 1 - c), device_id_type=MESH)

        def load(s):
            return pltpu.make_async_copy(g_ref.at[2 * s + c], mine.at[s], local_sems.at[s])

        for s in range(4):
            swap(s).start()
            load(s).start()
        for s in range(4):
            load(s).wait()
            swap(s).wait_recv()

        def chip_sum(chip, rows):
            return mine[chip, rows, :].astype(F32) + theirs[chip, rows, :].astype(F32)

        for j in (1, 2, 3):
            @pl.loop(0, R // rc)
            def _(t):
                rows = pl.ds(pl.multiple_of(t * rc, rc), rc)
                partial[j - 1, rows, :] = chip_sum(my_chip ^ j, rows).astype(BF16)

        @pl.loop(0, R // rc)
        def _(t):
            rows = pl.ds(pl.multiple_of(t * rc, rc), rc)
            out_ref[rows, :] = chip_sum(my_chip, rows)

        for s in range(4):
            swap(s).wait_send()

    vmem = pl.BlockSpec(memory_space=pltpu.VMEM)
    return pl.pallas_call(
        body, name=name,
        in_specs=[pl.BlockSpec(memory_space=pl.ANY)], out_specs=[vmem, vmem],
        out_shape=[jax.ShapeDtypeStruct((3, R, C), BF16), jax.ShapeDtypeStruct((R, C), F32)],
        scratch_shapes=[
            pltpu.VMEM((4, R, C), BF16), pltpu.VMEM((4, R, C), BF16),
            pltpu.SemaphoreType.DMA((4,)), pltpu.SemaphoreType.DMA((4,)), pltpu.SemaphoreType.DMA((4,)),
        ],
        compiler_params=_params(),
    )(grads)


def _cross_chips(partials, *, name, collective_id):
    n = len(partials)

    def body(*refs):
        ins, outs = refs[:n], refs[n:2 * n]
        send_sems, recv_sems = refs[2 * n:]
        x, y, c = _position()
        my_chip = 2 * x + y
        peers = [((my_chip ^ j) // 2, (my_chip ^ j) % 2, c) for j in (1, 2, 3)]

        barrier = pltpu.get_barrier_semaphore()
        for peer in peers:
            pl.semaphore_signal(barrier, inc=1, device_id=peer, device_id_type=MESH)
        pl.semaphore_wait(barrier, 3)

        copies = [
            pltpu.make_async_remote_copy(
                src_ref=ins[a].at[j], dst_ref=outs[a].at[j],
                send_sem=send_sems.at[3 * a + j], recv_sem=recv_sems.at[3 * a + j],
                device_id=peers[j], device_id_type=MESH)
            for a in range(n) for j in range(3)]
        for cp in copies:
            cp.start()
        for cp in copies:
            cp.wait_recv()
        for cp in copies:
            cp.wait_send()

    return pl.kernel(
        body, name=name,
        out_type=[jax.ShapeDtypeStruct(p.shape, p.dtype) for p in partials],
        mesh=plsc.ScalarSubcoreMesh(axis_name="sequencer", num_cores=1),
        scratch_types=[pltpu.SemaphoreType.DMA((3 * n,)), pltpu.SemaphoreType.DMA((3 * n,))],
        compiler_params=pltpu.CompilerParams(collective_id=collective_id),
    )(*partials)


def _owner_sum(own, landed, *, name):
    R, C = own.shape
    tr = _row_tile(R, C)

    def body(own_ref, landed_ref, out_ref):
        total = own_ref[...]
        for j in range(3):
            total = total + landed_ref[j].astype(F32)
        out_ref[...] = total

    return pl.pallas_call(
        body, name=name, grid=(R // tr,),
        in_specs=[pl.BlockSpec((tr, C), lambda i: (i, 0)), pl.BlockSpec((3, tr, C), lambda i: (0, i, 0))],
        out_specs=pl.BlockSpec((tr, C), lambda i: (i, 0)),
        out_shape=jax.ShapeDtypeStruct((R, C), F32),
        compiler_params=_params(("arbitrary",)),
    )(own, landed)


def _sum_devices(gathered, *, name):
    _, R, C = gathered.shape

    def body(in_ref, out_ref):
        total = in_ref[0]
        for d in range(1, N_DEV):
            total = total + in_ref[d]
        out_ref[...] = total

    return pl.pallas_call(
        body, name=name, grid=(1,),
        in_specs=[pl.BlockSpec((N_DEV, R, C), lambda i: (0, 0, 0))],
        out_specs=pl.BlockSpec((R, C), lambda i: (0, 0)),
        out_shape=jax.ShapeDtypeStruct((R, C), F32),
        compiler_params=_params(("arbitrary",)),
    )(gathered)


def _local_step(x, target, norms, pool_w_group, pool_scale, wgu1, wd1, w_in, wbp, wba, w_out, wgu2, wd2, exchange):
    n1g, nmg, n2g, nfg = norms
    D = x.shape[1]
    h1, gu1 = _ffn_fwd(x, n1g, wgu1, wd1, tm=512, name="ffn1_fwd")
    un, proj = _inproj_fwd(h1, nmg, w_in, tm=1024, name="inproj_fwd")
    p = _pool_fwd(proj, pool_w_group, pool_scale, name="pool_fwd")
    o, ltot = _attn_fwd(proj, name="attn_fwd")
    h2, m = _mix_fwd(h1, p, o, proj, wbp, wba, w_out, tm=256, name="mix_fwd")
    h3, gu2 = _ffn_fwd(h2, n2g, wgu2, wd2, tm=512, name="ffn2_fwd")
    dh3, loss, d_nf = _loss_bwd(h3, target, nfg, tm=256, name="loss_bwd")

    dh2, d_n2, n2, df2, dgu2, hid2 = _ffn_bwd(dh3, h2, n2g, gu2, wgu2, wd2, tm=256, name="ffn2_bwd")
    d_wgu2 = _wgrad_gate_up(n2, dgu2, tk=WGRAD_TOKENS, name="ffn2_wgrad_gate_up")
    d_wd2 = _wgrad_down(hid2, df2, tk=WGRAD_TOKENS, name="ffn2_wgrad_down")
    g_wgu2, g_wd2 = exchange("ffn2", [d_wgu2, d_wd2.reshape(N_DEV, FF_SHARD_PAD, D)])

    dyp, dys, dp, do, dgl = _mix_bwd(dh2, p, o, proj, wbp, wba, w_out, tm=256, name="mix_bwd")
    d_wout = _wgrad_full(m, dh2, tk=WGRAD_TOKENS, name="wgrad_out")
    d_wbp = _wgrad_full(p, dyp, tk=WGRAD_TOKENS, name="wgrad_branch_pool", split_lanes=wbp.shape[2])
    d_wba = _wgrad_full(o, dys, tk=WGRAD_TOKENS, name="wgrad_branch_attn", split_lanes=wba.shape[2])
    g_wbp, g_wba, g_wout = exchange("mix", [d_wbp, d_wba, d_wout.reshape(N_DEV, D // N_DEV, D)])
    dxp, d_wgroup, d_scale = _pool_bwd(dp, proj, pool_w_group, pool_scale, name="pool_bwd")
    dq, dkt, dvt = _attn_bwd(proj, do, ltot, name="attn_bwd")
    dk, dv = (t.transpose(0, 2, 1).reshape(dq.shape) for t in (dkt, dvt))
    dproj = jnp.concatenate([dxp.astype(BF16), dq.astype(BF16), dk.astype(BF16), dv.astype(BF16), dgl], axis=1)
    dh1, d_nm = _inproj_bwd(dproj, dh2, h1, nmg, w_in, tm=1024, name="inproj_bwd")
    d_win = _wgrad_in(un, dproj, tk=WGRAD_TOKENS, name="wgrad_in")
    g_win, = exchange("w_in", [d_win])

    dx, d_n1, n1, df1, dgu1, hid1 = _ffn_bwd(dh1, x, n1g, gu1, wgu1, wd1, tm=256, name="ffn1_bwd")
    d_wgu1 = _wgrad_gate_up(n1, dgu1, tk=WGRAD_TOKENS, name="ffn1_wgrad_gate_up")
    d_wd1 = _wgrad_down(hid1, df1, tk=WGRAD_TOKENS, name="ffn1_wgrad_down")
    g_wgu1, g_wd1 = exchange("ffn1", [d_wgu1, d_wd1.reshape(N_DEV, FF_SHARD_PAD, D)])

    sharded = (g_wgu1, g_wd1, g_win, g_wbp, g_wba, g_wout, g_wgu2, g_wd2)
    replicated = (d_n1, d_nm, d_n2, d_nf, d_scale, d_wgroup)
    return loss, dx, sharded, replicated


def _hidden_major(w):
    return jnp.swapaxes(w[0], 0, 1)


def _pad_gate_up(wt):
    d = wt.shape[1]
    wt = wt.astype(BF16).reshape(2, FF_SHARD, d)
    return jnp.pad(wt, ((0, 0), (0, FF_SHARD_PAD - FF_SHARD), (0, 0))).reshape(2 * FF_SHARD_PAD, d)


def _unpad_gate_up(gt):
    d = gt.shape[1]
    return gt.reshape(2, FF_SHARD_PAD, d)[:, :FF_SHARD].reshape(2 * FF_SHARD, d)


def _pad_down(w):
    return jnp.pad(w.astype(BF16), ((0, FF_SHARD_PAD - FF_SHARD), (0, 0)))


def kernel(x, ffn1_norm, ffn1_w_gate_up, ffn1_w_down, mix_norm, w_in, pool_w_group, pool_scale, w_branch_pool, w_branch_attn, w_out, ffn2_norm, ffn2_w_gate_up, ffn2_w_down, final_norm, loss_target, m_ffn1_norm, m_ffn1_w_gate_up, m_ffn1_w_down, m_mix_norm, m_w_in, m_pool_w_group, m_pool_scale, m_w_branch_pool, m_w_branch_attn, m_w_out, m_ffn2_norm, m_ffn2_w_gate_up, m_ffn2_w_down, m_final_norm, v_ffn1_norm, v_ffn1_w_gate_up, v_ffn1_w_down, v_mix_norm, v_w_in, v_pool_w_group, v_pool_scale, v_w_branch_pool, v_w_branch_attn, v_w_out, v_ffn2_norm, v_ffn2_w_gate_up, v_ffn2_w_down, v_final_norm):
    D = x.shape[-1]
    weights = dict(ffn1_norm=ffn1_norm, ffn1_w_gate_up=ffn1_w_gate_up, ffn1_w_down=ffn1_w_down, mix_norm=mix_norm,
                   w_in=w_in, pool_w_group=pool_w_group, pool_scale=pool_scale, w_branch_pool=w_branch_pool,
                   w_branch_attn=w_branch_attn, w_out=w_out, ffn2_norm=ffn2_norm, ffn2_w_gate_up=ffn2_w_gate_up,
                   ffn2_w_down=ffn2_w_down, final_norm=final_norm)
    first = dict(ffn1_norm=m_ffn1_norm, ffn1_w_gate_up=m_ffn1_w_gate_up, ffn1_w_down=m_ffn1_w_down,
                 mix_norm=m_mix_norm, w_in=m_w_in, pool_w_group=m_pool_w_group, pool_scale=m_pool_scale,
                 w_branch_pool=m_w_branch_pool, w_branch_attn=m_w_branch_attn, w_out=m_w_out,
                 ffn2_norm=m_ffn2_norm, ffn2_w_gate_up=m_ffn2_w_gate_up, ffn2_w_down=m_ffn2_w_down,
                 final_norm=m_final_norm)
    second = dict(ffn1_norm=v_ffn1_norm, ffn1_w_gate_up=v_ffn1_w_gate_up, ffn1_w_down=v_ffn1_w_down,
                  mix_norm=v_mix_norm, w_in=v_w_in, pool_w_group=v_pool_w_group, pool_scale=v_pool_scale,
                  w_branch_pool=v_w_branch_pool, w_branch_attn=v_w_branch_attn, w_out=v_w_out,
                  ffn2_norm=v_ffn2_norm, ffn2_w_gate_up=v_ffn2_w_gate_up, ffn2_w_down=v_ffn2_w_down,
                  final_norm=v_final_norm)
    order = list(weights)

    wgu1, wd1 = _all_gather([_pad_gate_up(_hidden_major(ffn1_w_gate_up)), _pad_down(ffn1_w_down[0])],
                            name="all_gather_ffn1", collective_id=0)
    win_g, = _all_gather([w_in[0].astype(BF16)], name="all_gather_w_in", collective_id=1)
    wbp_g, wba_g, wout_g = _all_gather(
        [w_branch_pool[0].astype(BF16), w_branch_attn[0].astype(BF16), w_out[0].astype(BF16)],
        name="all_gather_mix", collective_id=2)
    wgu2, wd2 = _all_gather([_pad_gate_up(_hidden_major(ffn2_w_gate_up)), _pad_down(ffn2_w_down[0])],
                            name="all_gather_ffn2", collective_id=3)
    wd1 = wd1.reshape(N_DEV * FF_SHARD_PAD, D)
    wd2 = wd2.reshape(N_DEV * FF_SHARD_PAD, D)
    wout_g = wout_g.reshape(D, D)

    cross_ids = {"ffn2": 4, "mix": 5, "w_in": 6, "ffn1": 7}

    def exchange(tag, group):
        sums = [_chip_sums(g, name=f"chip_sums_{tag}_{i}") for i, g in enumerate(group)]
        landed = _cross_chips([s[0] for s in sums], name="cross_chips_" + tag, collective_id=cross_ids[tag])
        return [(s[1], l) for s, l in zip(sums, landed)]

    norms = (ffn1_norm, mix_norm, ffn2_norm, final_norm.reshape(1, D))
    loss, dx, sharded, replicated = _local_step(
        x[0], loss_target[0], norms, pool_w_group[0], pool_scale, wgu1, wd1, win_g, wbp_g, wba_g, wout_g, wgu2, wd2,
        exchange)
    names = ["ffn1_w_gate_up", "ffn1_w_down", "w_in", "w_branch_pool", "w_branch_attn", "w_out",
             "ffn2_w_gate_up", "ffn2_w_down"]
    grads = {k: _owner_sum(own, landed, name="owner_sum_" + k)
             for k, (own, landed) in reversed(list(zip(names, sharded)))}
    for k in ("ffn1_w_gate_up", "ffn2_w_gate_up"):
        grads[k] = _unpad_gate_up(grads[k])
    for k in ("ffn1_w_down", "ffn2_w_down"):
        grads[k] = grads[k][:FF_SHARD]

    d_n1, d_nm, d_n2, d_nf, d_scale, d_wgroup = replicated
    small = ["ffn1_norm", "mix_norm", "ffn2_norm", "final_norm", "pool_scale", "pool_w_group"]
    def tile_rows(a):
        a = a.reshape(-1, 128)
        return jnp.pad(a, ((0, -a.shape[0] % 8), (0, 0)))

    pieces = [tile_rows(d) for d in (d_n1, d_nm, d_n2, d_nf, d_scale, d_wgroup)]
    rows = [weights[k].size // 128 for k in small]
    starts = [sum(p.shape[0] for p in pieces[:i]) for i in range(len(pieces) + 1)]
    slab = jnp.concatenate(pieces + [jnp.broadcast_to(loss, (8, 128))], axis=0)
    slabs, = _all_gather([slab], name="all_gather_replicated", collective_id=8)
    total = _sum_devices(slabs, name="sum_replicated")
    loss_out = total[starts[-1], 0]

    small_w = jnp.concatenate([tile_rows(weights[k]) for k in small], axis=0)
    small_m = jnp.concatenate([tile_rows(first[k]) for k in small], axis=0)
    small_v = jnp.concatenate([tile_rows(second[k]) for k in small], axis=0)
    small_out = _adamw(small_w, total[:starts[-1]], small_m, small_v, name="adamw_replicated")
    delta, new_m, new_v = {}, {}, {}
    for name_, start, n_rows in zip(small, starts, rows):
        shape = weights[name_].shape
        grads[name_] = total[start:start + n_rows].reshape(shape)
        delta[name_], new_m[name_], new_v[name_] = (a[start:start + n_rows].reshape(shape) for a in small_out)
    for name_ in order:
        if name_ in small:
            continue
        hidden_major = name_.endswith("w_gate_up")
        view = _hidden_major if hidden_major else (lambda a: a[0])
        back = (lambda a: jnp.swapaxes(a, 0, 1)[None]) if hidden_major else (lambda a: a[None])
        out = _adamw(view(weights[name_]), grads[name_], view(first[name_]), view(second[name_]),
                     name="adamw_" + name_)
        delta[name_], new_m[name_], new_v[name_] = (back(a) for a in out)
        grads[name_] = back(grads[name_])

    return (loss_out, dx[None], *[grads[k] for k in order], *[delta[k] for k in order],
            *[new_m[k] for k in order], *[new_v[k] for k in order])
```

```python
import functools

import jax
import jax.numpy as jnp
from jax import lax
from jax.experimental import pallas as pl
from jax.experimental.pallas import tpu as pltpu
from jax.experimental.pallas import tpu_sc as plsc

F32 = jnp.float32
BF16 = jnp.bfloat16
MESH = pl.DeviceIdType.MESH

RMS_EPS = 1e-6
N_DEV = 8
N_HEADS = 8
HEAD_DIM = 64
HEAD_PAIR = 2 * HEAD_DIM
POOL_WINDOWS = (2, 4, 8, 16)
POOL_GROUP = 128
POOL_WIDTH = 512
SB_WIDTH = 512
FF_SHARD = 352
FF_SHARD_PAD = 384
ATTN_BLOCK = 256
ATTN_SCALE = 0.125

ADAM_LR = 0.001
ADAM_B1 = 0.9
ADAM_B2 = 0.999
ADAM_EPS = 1e-08
ADAM_WD = 0.01
ADAM_STEP = 10

VMEM_LIMIT = 48 << 20
WGRAD_TOKENS = 2048


def _params(dims=None):
    return pltpu.CompilerParams(dimension_semantics=dims, vmem_limit_bytes=VMEM_LIMIT)


def _mm(a, b):
    return jnp.dot(a, b, preferred_element_type=F32)


def _mm_nt(a, b):
    return lax.dot_general(a, b, (((1,), (1,)), ((), ())), preferred_element_type=F32)


def _mm_tn(a, b):
    return lax.dot_general(a, b, (((0,), (0,)), ((), ())), preferred_element_type=F32)


def _row_tile(rows, cols):
    limit = max(8, (512 * 1024) // cols)
    return max(t for t in range(8, rows + 1, 8) if rows % t == 0 and (t <= limit or t == 8))


def _rstd(xf):
    return lax.rsqrt(jnp.mean(xf * xf, axis=-1, keepdims=True) + RMS_EPS)


def _rms_bwd(xf, gain, dn):
    r = _rstd(xf)
    xh = xf * r
    dgain = jnp.sum(dn * xh, axis=0, keepdims=True)
    dxh = dn * gain
    dx = r * (dxh - xh * jnp.mean(dxh * xh, axis=-1, keepdims=True))
    return dx, dgain


def _ffn_fwd(x, gain, wgu, wd, *, tm, name):
    T, D = x.shape
    tm = min(tm, T)
    nb, bw = wgu.shape[0] // 2, wgu.shape[1]

    def body(x_ref, gain_ref, wg_ref, wu_ref, wd_ref, h_ref, gu_ref, hid_ref, n_scr, acc):
        j = pl.program_id(1)

        @pl.when(j == 0)
        def _():
            xf = x_ref[...]
            n_scr[...] = (xf * _rstd(xf) * gain_ref[...]).astype(BF16)
            acc[...] = jnp.zeros_like(acc)

        n = n_scr[...]
        g = _mm_nt(n, wg_ref[...])
        u = _mm_nt(n, wu_ref[...])
        gu_ref[0] = g.astype(BF16)
        gu_ref[1] = u.astype(BF16)
        hid = (g * jax.nn.sigmoid(g) * u).astype(BF16)
        hid_ref[...] = hid
        acc[...] += _mm(hid, wd_ref[...])

        @pl.when(j == nb - 1)
        def _():
            h_ref[...] = x_ref[...] + 0.5 * acc[...]

    return pl.pallas_call(
        body, name=name, grid=(T // tm, nb),
        in_specs=[
            pl.BlockSpec((tm, D), lambda i, j: (i, 0)),
            pl.BlockSpec((1, D), lambda i, j: (0, 0)),
            pl.BlockSpec((None, bw, D), lambda i, j: (j, 0, 0)),
            pl.BlockSpec((None, bw, D), lambda i, j: (j + nb, 0, 0)),
            pl.BlockSpec((bw, D), lambda i, j: (j, 0)),
        ],
        out_specs=[
            pl.BlockSpec((tm, D), lambda i, j: (i, 0)),
            pl.BlockSpec((2, tm, bw), lambda i, j: (0, i, j)),
            pl.BlockSpec((tm, bw), lambda i, j: (i, j)),
        ],
        out_shape=[jax.ShapeDtypeStruct((T, D), F32), jax.ShapeDtypeStruct((2, T, nb * bw), BF16),
                   jax.ShapeDtypeStruct((T, nb * bw), BF16)],
        scratch_shapes=[pltpu.VMEM((tm, D), BF16), pltpu.VMEM((tm, D), F32)],
        compiler_params=_params(("arbitrary", "arbitrary")),
    )(x, gain, wgu, wgu, wd)


def _ffn_bwd(dh, df, x, gain, gu, wgu, wd, *, tm, name):
    T, D = x.shape
    tm = min(tm, T)
    nb, bw = wgu.shape[0] // 2, wgu.shape[1]

    def body(dh_ref, df_ref, x_ref, gain_ref, gu_ref, wg_ref, wu_ref, wd_ref,
             dx_ref, dgain_ref, n_ref, dgu_ref, dn_acc):
        i, j = pl.program_id(0), pl.program_id(1)

        @pl.when(j == 0)
        def _():
            xf = x_ref[...]
            n_ref[...] = (xf * _rstd(xf) * gain_ref[...]).astype(BF16)
            dn_acc[...] = jnp.zeros_like(dn_acc)

        @pl.when((i == 0) & (j == 0))
        def _():
            dgain_ref[...] = jnp.zeros_like(dgain_ref)

        dhid = _mm_nt(df_ref[...], wd_ref[...])
        g = gu_ref[0].astype(F32)
        u = gu_ref[1].astype(F32)
        s = jax.nn.sigmoid(g)
        silu = g * s
        dg =(dhid * u * (s * (1.0 + g * (1.0 - s)))).astype(BF16)
        du = (dhid * silu).astype(BF16)
        dgu_ref[0] = dg
        dgu_ref[1] = du
        dn_acc[...] += _mm(dg, wg_ref[...]) + _mm(du, wu_ref[...])

        @pl.when(j == nb - 1)
        def _():
            dx, dgain = _rms_bwd(x_ref[...], gain_ref[...], dn_acc[...])
            dx_ref[...] = dh_ref[...] + dx
            dgain_ref[...] += dgain

    row = lambda i, j: (i, 0)
    return pl.pallas_call(
        body, name=name, grid=(T // tm, nb),
        in_specs=[
            pl.BlockSpec((tm, D), row),
            pl.BlockSpec((tm, D), row),
            pl.BlockSpec((tm, D), row),
            pl.BlockSpec((1, D), lambda i, j: (0, 0)),
            pl.BlockSpec((2, tm, bw), lambda i, j: (0, i, j)),
            pl.BlockSpec((None, bw, D), lambda i, j: (j, 0, 0)),
            pl.BlockSpec((None, bw, D), lambda i, j: (j + nb, 0, 0)),
            pl.BlockSpec((bw, D), lambda i, j: (j, 0)),
        ],
        out_specs=[
            pl.BlockSpec((tm, D), row),
            pl.BlockSpec((1, D), lambda i, j: (0, 0)),
            pl.BlockSpec((tm, D), row),
            pl.BlockSpec((2, tm, bw), lambda i, j: (0, i, j)),
        ],
        out_shape=[
            jax.ShapeDtypeStruct((T, D), F32),
            jax.ShapeDtypeStruct((1, D), F32),
            jax.ShapeDtypeStruct((T, D), BF16),
            jax.ShapeDtypeStruct((2, T, nb * bw), BF16),
        ],
        scratch_shapes=[pltpu.VMEM((tm, D), F32)],
        compiler_params=_params(("arbitrary", "arbitrary")),
    )(dh, df, x, gain, gu, wgu, wgu, wd)


def _wgrad(a, b, *, grid, a_spec, b_spec, out_spec, out_shape, acc_shape, name, split_lanes=0):
    nk = grid[2]

    def body(a_ref, b_ref, o_ref, acc):
        k = pl.program_id(2)

        @pl.when(k == 0)
        def _():
            acc[...] = jnp.zeros_like(acc)

        acc[...] += _mm_tn(a_ref[...].astype(BF16), b_ref[...].astype(BF16))

        @pl.when(k == nk - 1)
        def _():
            if split_lanes:
                for e in range(o_ref.shape[0]):
                    o_ref[e] = acc[:, e * split_lanes:(e + 1) * split_lanes].astype(o_ref.dtype)
            else:
                o_ref[...] = acc[...].astype(o_ref.dtype)

    return pl.pallas_call(
        body, name=name, grid=grid, in_specs=[a_spec, b_spec], out_specs=out_spec,
        out_shape=jax.ShapeDtypeStruct(out_shape, BF16),
        scratch_shapes=[pltpu.VMEM(acc_shape, F32)],
        compiler_params=_params(("arbitrary", "arbitrary", "arbitrary")),
    )(a, b)


def _wgrad_gate_up(n, dgu, *, tk, name):
    T, D = n.shape
    tk = min(tk, T)
    bw = FF_SHARD_PAD * 2
    nb = dgu.shape[2] // bw
    return _wgrad(
        dgu, n, grid=(2 * nb, 1, T // tk), name=name,
        a_spec=pl.BlockSpec((None, tk, bw), lambda m, c, k: (m // nb, k, m % nb)),
        b_spec=pl.BlockSpec((tk, D), lambda m, c, k: (k, 0)),
        out_spec=pl.BlockSpec((None, bw, D), lambda m, c, k: (m, 0, 0)),
        out_shape=(2 * nb, bw, D), acc_shape=(bw, D))


def _wgrad_down(hid, df, *, tk, name):
    T, D = df.shape
    tk = min(tk, T)
    bw = FF_SHARD_PAD * 2
    nb = hid.shape[1] // bw
    return _wgrad(
        hid, df, grid=(nb, 1, T // tk), name=name,
        a_spec=pl.BlockSpec((tk, bw), lambda m, c, k: (k, m)),
        b_spec=pl.BlockSpec((tk, D), lambda m, c, k: (k, 0)),
        out_spec=pl.BlockSpec((bw, D), lambda m, c, k: (m, 0)),
        out_shape=(nb * bw, D), acc_shape=(bw, D))


def _wgrad_in(un, dproj, *, tk, name):
    T, D = un.shape
    tk = min(tk, T)
    bw = dproj.shape[1] // N_DEV
    return _wgrad(
        un, dproj, grid=(1, N_DEV, T // tk), name=name,
        a_spec=pl.BlockSpec((tk, D), lambda m, c, k: (k, 0)),
        b_spec=pl.BlockSpec((tk, bw), lambda m, c, k: (k, c)),
        out_spec=pl.BlockSpec((None, D, bw), lambda m, c, k: (c, 0, 0)),
        out_shape=(N_DEV, D, bw), acc_shape=(D, bw))


def _wgrad_full(a, b, *, tk, name, split_lanes=0):
    T, M = a.shape
    tk = min(tk, T)
    N = b.shape[1]
    if split_lanes:
        out_shape = (N // split_lanes, M, split_lanes)
        out_spec = pl.BlockSpec(out_shape, lambda m, c, k: (0, 0, 0))
    else:
        out_shape = (M, N)
        out_spec = pl.BlockSpec(out_shape, lambda m, c, k: (0, 0))
    return _wgrad(
        a, b, grid=(1, 1, T // tk), name=name,
        a_spec=pl.BlockSpec((tk, M), lambda m, c, k: (k, 0)),
        b_spec=pl.BlockSpec((tk, N), lambda m, c, k: (k, 0)),
        out_spec=out_spec, out_shape=out_shape, acc_shape=(M, N), split_lanes=split_lanes)


def _loss_bwd(h, target, gain, *, tm, name):
    T, D = h.shape
    tm = min(tm, T)

    def body(h_ref, t_ref, gain_ref, dh_ref, df_ref, loss_ref, dgain_ref):
        @pl.when(pl.program_id(0) == 0)
        def _():
            loss_ref[...] = jnp.zeros_like(loss_ref)
            dgain_ref[...] = jnp.zeros_like(dgain_ref)

        xf = h_ref[...]
        gain = gain_ref[...]
        err = xf * _rstd(xf) * gain - t_ref[...]
        loss_ref[...] += 0.5 * jnp.sum(jnp.mean(err * err, axis=-1, keepdims=True), axis=0, keepdims=True)
        dx, dgain = _rms_bwd(xf, gain, err * (1.0 / D))
        dh_ref[...] = dx
        df_ref[...] = (0.5 * dx).astype(BF16)
        dgain_ref[...] += dgain

    row = lambda i: (i, 0)
    fixed = lambda i: (0, 0)
    return pl.pallas_call(
        body, name=name, grid=(T // tm,),
        in_specs=[pl.BlockSpec((tm, D), row), pl.BlockSpec((tm, D), row), pl.BlockSpec((1, D), fixed)],
        out_specs=[pl.BlockSpec((tm, D), row), pl.BlockSpec((tm, D), row), pl.BlockSpec((1, 128), fixed),
                   pl.BlockSpec((1, D), fixed)],
        out_shape=[jax.ShapeDtypeStruct((T, D), F32), jax.ShapeDtypeStruct((T, D), BF16),
                   jax.ShapeDtypeStruct((1, 128), F32), jax.ShapeDtypeStruct((1, D), F32)],
        compiler_params=_params(("arbitrary",)),
    )(h, target, gain)


def _inproj_fwd(h, gain, w_in, *, tm, name):
    T, D = h.shape
    tm = min(tm, T)
    nb, bw = w_in.shape[0], w_in.shape[2]

    def body(h_ref, gain_ref, w_ref, un_ref, proj_ref):
        @pl.when(pl.program_id(1) == 0)
        def _():
            xf = h_ref[...]
            un_ref[...] = (xf * _rstd(xf) * gain_ref[...]).astype(BF16)

        proj_ref[...] = _mm(un_ref[...], w_ref[...])

    return pl.pallas_call(
        body, name=name, grid=(T // tm, nb),
        in_specs=[
            pl.BlockSpec((tm, D), lambda i, j: (i, 0)),
            pl.BlockSpec((1, D), lambda i, j: (0, 0)),
            pl.BlockSpec((None, D, bw), lambda i, j: (j, 0, 0)),
        ],
        out_specs=[pl.BlockSpec((tm, D), lambda i, j: (i, 0)), pl.BlockSpec((tm, bw), lambda i, j: (i, j))],
        out_shape=[jax.ShapeDtypeStruct((T, D), BF16), jax.ShapeDtypeStruct((T, nb * bw), F32)],
        compiler_params=_params(("arbitrary", "arbitrary")),
    )(h, gain, w_in)


def _inproj_bwd(dproj, dh, h, gain, w_in, *, tm, name):
    T, D = h.shape
    tm = min(tm, T)
    nb, bw = w_in.shape[0], w_in.shape[2]

    def body(dp_ref, dh_ref, h_ref, gain_ref, w_ref, dx_ref, df_ref, dgain_ref, acc):
        i, j = pl.program_id(0), pl.program_id(1)

        @pl.when(j == 0)
        def _():
            acc[...] = jnp.zeros_like(acc)

        @pl.when((i == 0) & (j == 0))
        def _():
            dgain_ref[...] = jnp.zeros_like(dgain_ref)

        acc[...] += _mm_nt(dp_ref[...], w_ref[...])

        @pl.when(j == nb - 1)
        def _():
            dx, dgain = _rms_bwd(h_ref[...], gain_ref[...], acc[...])
            dh_in = dh_ref[...] + dx
            dx_ref[...] = dh_in
            df_ref[...] = (0.5 * dh_in).astype(BF16)
            dgain_ref[...] += dgain

    row = lambda i, j: (i, 0)
    return pl.pallas_call(
        body, name=name, grid=(T // tm, nb),
        in_specs=[
            pl.BlockSpec((tm, bw), lambda i, j: (i, j)),
            pl.BlockSpec((tm, D), row),
            pl.BlockSpec((tm, D), row),
            pl.BlockSpec((1, D), lambda i, j: (0, 0)),
            pl.BlockSpec((None, D, bw), lambda i, j: (j, 0, 0)),
        ],
        out_specs=[pl.BlockSpec((tm, D), row), pl.BlockSpec((tm, D), row), pl.BlockSpec((1, D), lambda i, j: (0, 0))],
        out_shape=[jax.ShapeDtypeStruct((T, D), F32), jax.ShapeDtypeStruct((T, D), BF16),
                   jax.ShapeDtypeStruct((1, D), F32)],
        scratch_shapes=[pltpu.VMEM((tm, D), F32)],
        compiler_params=_params(("arbitrary", "arbitrary")),
    )(dproj, dh, h, gain, w_in)


def _window_sum(x, row, doublings, *, backward):
    T = x.shape[0]
    s = x
    for k in range(doublings):
        sh = 1 << k
        if backward:
            s = s + jnp.where(row < T - sh, pltpu.roll(s, T - sh, 0), 0.0)
        else:
            s = s + jnp.where(row >= sh, pltpu.roll(s, sh, 0), 0.0)
    return s


def _pool_fwd(proj, w_group, scale, *, name):
    T = proj.shape[0]

    def body(xp_ref, w_ref, scale_ref, p_ref):
        row = lax.broadcasted_iota(jnp.int32, (T, POOL_GROUP), 0)
        for gi, window in enumerate(POOL_WINDOWS):
            cols = slice(gi * POOL_GROUP, (gi + 1) * POOL_GROUP)
            x = xp_ref[:, cols]
            inv_count = 1.0 / jnp.minimum(row + 1, window).astype(F32)
            yc = _window_sum(x, row, gi + 1, backward=False) * inv_count - x
            pre = _mm(yc.astype(BF16), w_ref[gi].astype(BF16))
            p_ref[:, cols] = pre * scale_ref[:, cols]

    return pl.pallas_call(
        body, name=name, grid=(1,),
        in_specs=[
            pl.BlockSpec((T, POOL_WIDTH), lambda i: (0, 0)),
            pl.BlockSpec(w_group.shape, lambda i: (0, 0, 0)),
            pl.BlockSpec((1, POOL_WIDTH), lambda i: (0, 0)),
        ],
        out_specs=pl.BlockSpec((T, POOL_WIDTH), lambda i: (0, 0)),
        out_shape=jax.ShapeDtypeStruct((T, POOL_WIDTH), F32),
        compiler_params=_params(("arbitrary",)),
    )(proj, w_group, scale)


def _pool_bwd(dp, proj, w_group, scale, *, name):
    T = proj.shape[0]

    def body(dp_ref, xp_ref, w_ref, scale_ref, dxp_ref, dw_ref, dscale_ref):
        row = lax.broadcasted_iota(jnp.int32, (T, POOL_GROUP), 0)
        for gi, window in enumerate(POOL_WINDOWS):
            cols = slice(gi * POOL_GROUP, (gi + 1) * POOL_GROUP)
            x = xp_ref[:, cols]
            inv_count = 1.0 / jnp.minimum(row + 1, window).astype(F32)
            yc = (_window_sum(x, row, gi + 1, backward=False) * inv_count - x).astype(BF16)
            w = w_ref[gi].astype(BF16)
            pre = _mm(yc, w)
            dpg = dp_ref[:, cols]
            dscale_ref[:, cols] = jnp.sum(dpg * pre, axis=0, keepdims=True)
            dpre = (dpg * scale_ref[:, cols]).astype(BF16)
            dw_ref[gi] = _mm_tn(yc, dpre)
            dyc = _mm_nt(dpre, w)
            dxp_ref[:, cols] = _window_sum(dyc * inv_count, row, gi + 1, backward=True) - dyc

    return pl.pallas_call(
        body, name=name, grid=(1,),
        in_specs=[
            pl.BlockSpec((T, POOL_WIDTH), lambda i: (0, 0)),
            pl.BlockSpec((T, POOL_WIDTH), lambda i: (0, 0)),
            pl.BlockSpec(w_group.shape, lambda i: (0, 0, 0)),
            pl.BlockSpec((1, POOL_WIDTH), lambda i: (0, 0)),
        ],
        out_specs=[
            pl.BlockSpec((T, POOL_WIDTH), lambda i: (0, 0)),
            pl.BlockSpec(w_group.shape, lambda i: (0, 0, 0)),
            pl.BlockSpec((1, POOL_WIDTH), lambda i: (0, 0)),
        ],
        out_shape=[jax.ShapeDtypeStruct((T, POOL_WIDTH), F32), jax.ShapeDtypeStruct(w_group.shape, F32),
                   jax.ShapeDtypeStruct((1, POOL_WIDTH), F32)],
        compiler_params=_params(("arbitrary",)),
    )(dp, proj, w_group, scale)


ATTN_STRIP = 32


def _log_sigmoids(z):
    lb = jnp.minimum(z, 0.0) - jnp.log(1.0 + jnp.exp(-jnp.abs(z)))
    return lb, lb - z


def _transposed_blocks(x_ref, blocks_scr, tq):
    for b in range(blocks_scr.shape[0]):
        blocks_scr[b] = x_ref[b * tq:(b + 1) * tq, :].T.astype(BF16)


def _split_bf16(x):
    hi = x.astype(BF16)
    return hi, (x - hi.astype(F32)).astype(BF16)


def _strips(n):
    return [slice(i, i + ATTN_STRIP) for i in range(0, n, ATTN_STRIP)]


def _rows(parts):
    return jnp.concatenate(parts, axis=0)


def _attn_specs(T, tq):
    q_col = POOL_WIDTH // HEAD_PAIR
    k_col = q_col + SB_WIDTH // HEAD_PAIR
    v_col = k_col + SB_WIDTH // HEAD_PAIR
    return [
        pl.BlockSpec((tq, HEAD_PAIR), lambda p, i: (i, q_col + p)),
        pl.BlockSpec((T, HEAD_PAIR), lambda p, i: (0, k_col + p)),
        pl.BlockSpec((T, HEAD_PAIR), lambda p, i: (0, v_col + p)),
    ]


def _attn_fwd(proj, *, name):
    T = proj.shape[0]
    tq = ATTN_BLOCK

    def body(q_ref, k_ref, v_ref, o_ref, lt_ref, kt_scr, vb_scr):
        qi = pl.program_id(1)

        @pl.when(qi == 0)
        def _():
            _transposed_blocks(k_ref, kt_scr, tq)
            vb_scr[...] = v_ref[...].astype(BF16)

        head0 = lax.broadcasted_iota(jnp.int32, (tq, HEAD_PAIR), 1) < HEAD_DIM
        q = q_ref[...] * ATTN_SCALE
        qs = (jnp.where(head0, q, 0.0).astype(BF16), jnp.where(head0, 0.0, q).astype(BF16))
        r = lax.broadcasted_iota(jnp.int32, (tq, tq), 0)
        c = lax.broadcasted_iota(jnp.int32, (tq, tq), 1)
        later = (r > c).astype(BF16)
        later2 = _rows([later, later])
        causal = lambda rows: c[rows] < r[rows]
        strips = _strips(tq)

        def log_terms(z, valid):
            lbs, his, los, sums = [], [], [], []
            for rows in strips:
                lb, lm = _log_sigmoids(z[rows])
                if valid is not None:
                    lm = jnp.where(valid(rows), lm, 0.0)
                hi, lo = _split_bf16(lm)
                lbs.append(lb)
                his.append(hi)
                los.append(lo)
                sums.append(jnp.sum(lm, axis=1, keepdims=True))
            return lbs, jnp.concatenate([_rows(his), _rows(los)], axis=1), _rows(sums)

        def weights(lbs, run, after, valid):
            parts = []
            for rows, lb in zip(strips, lbs):
                a = jnp.exp(lb + run[rows] + after[rows])
                if valid is not None:
                    a = jnp.where(valid(rows), a, 0.0)
                parts.append(a.astype(BF16))
            return _rows(parts)

        def block(kj, carry, valid):
            kt = kt_scr[kj]
            vb = vb_scr[pl.ds(pl.multiple_of(kj * tq, tq), tq), :]
            run0, o0, run1, o1 = carry
            z0 = _mm(qs[0], kt)
            z1 = _mm(qs[1], kt)
            lbs0, split0, sums0 = log_terms(z0, valid)
            after0 = _mm(split0, later2)
            lbs1, split1, sums1 = log_terms(z1, valid)
            after1 = _mm(split1, later2)
            o0 = o0 + _mm(weights(lbs0, run0, after0, valid), vb)
            o1 = o1 + _mm(weights(lbs1, run1, after1, valid), vb)
            return run0 + sums0, o0, run1 + sums1, o1

        zero = (jnp.zeros((tq, 1), F32), jnp.zeros((tq, HEAD_PAIR), F32))
        carry = block(qi, zero + zero, causal)
        carry = lax.fori_loop(0, qi, lambda it, cr: block(qi - 1 - it, cr, None), carry)
        o_ref[...] = jnp.where(head0, carry[1], carry[3])
        lt_ref[...] = jnp.where(head0, carry[0], carry[2])

    out_spec = pl.BlockSpec((tq, HEAD_PAIR), lambda p, i: (i, p))
    return pl.pallas_call(
        body, name=name, grid=(N_HEADS // 2, T // tq),
        in_specs=_attn_specs(T, tq), out_specs=[out_spec, out_spec],
        out_shape=[jax.ShapeDtypeStruct((T, SB_WIDTH), F32), jax.ShapeDtypeStruct((T, SB_WIDTH), F32)],
        scratch_shapes=[pltpu.VMEM((T // tq, HEAD_PAIR, tq), BF16), pltpu.VMEM((T, HEAD_PAIR), BF16)],
        compiler_params=_params(("arbitrary", "arbitrary")),
    )(proj, proj, proj)


def _attn_bwd(proj, do, ltot, *, name):
    T = proj.shape[0]
    tq = ATTN_BLOCK

    def body(q_ref, k_ref, v_ref, do_ref, lt_ref, dq_ref, dkt_ref, dvt_ref, kb_scr, kt_scr, vt_scr):
        qi = pl.program_id(1)

        @pl.when(qi == 0)
        def _():
            kb_scr[...] = k_ref[...].astype(BF16)
            _transposed_blocks(k_ref, kt_scr, tq)
            _transposed_blocks(v_ref, vt_scr, tq)
            dkt_ref[...] = jnp.zeros_like(dkt_ref)
            dvt_ref[...] = jnp.zeros_like(dvt_ref)

        head0 = lax.broadcasted_iota(jnp.int32, (tq, HEAD_PAIR), 1) < HEAD_DIM
        q, do_, lt = q_ref[...] * ATTN_SCALE, do_ref[...], lt_ref[...]
        qs = (jnp.where(head0, q, 0.0).astype(BF16), jnp.where(head0, 0.0, q).astype(BF16))
        q_heads = (jnp.where(head0, q, 0.0), jnp.where(head0, 0.0, q))
        do_heads = (jnp.where(head0, do_, 0.0), jnp.where(head0, 0.0, do_))
        dos = tuple(d.astype(BF16) for d in do_heads)
        qts = tuple(x.T.astype(BF16) for x in q_heads)
        dots = tuple(d.T.astype(BF16) for d in do_heads)
        lts = (jnp.max(jnp.where(head0, lt, -jnp.inf), axis=1, keepdims=True),
               jnp.max(jnp.where(head0, -jnp.inf, lt), axis=1, keepdims=True))
        r = lax.broadcasted_iota(jnp.int32, (tq, tq), 0)
        c = lax.broadcasted_iota(jnp.int32, (tq, tq), 1)
        upto = (r <= c).astype(BF16)
        before = (r < c).astype(BF16)
        upto2, before2 = _rows([upto, upto]), _rows([before, before])
        causal = lambda rows: c[rows] < r[rows]
        strips = _strips(tq)

        def log_terms(z, valid):
            lbs, his, los, sums = [], [], [], []
            for rows in strips:
                lb, lm = _log_sigmoids(z[rows])
                if valid is not None:
                    lm = jnp.where(valid(rows), lm, 0.0)
                hi, lo = _split_bf16(lm)
                lbs.append(lb)
                his.append(hi)
                los.append(lo)
                sums.append(jnp.sum(lm, axis=1, keepdims=True))
            return lbs, jnp.concatenate([_rows(his), _rows(los)], axis=1), _rows(sums)

        def weights(lbs, rest, lm_upto, da, valid):
            a_parts, es, his, los, sums = [], [], [], [], []
            for rows, lb in zip(strips, lbs):
                a = jnp.exp(lb + (rest[rows] - lm_upto[rows]))
                if valid is not None:
                    a = jnp.where(valid(rows), a, 0.0)
                e = da[rows] * a
                hi, lo = _split_bf16(e)
                a_parts.append(a.astype(BF16))
                es.append(e)
                his.append(hi)
                los.append(lo)
                sums.append(jnp.sum(e, axis=1, keepdims=True))
            return _rows(a_parts), es, jnp.concatenate([_rows(his), _rows(los)], axis=1), _rows(sums)

        def score_grads(lbs, es, run_e, e_before, valid):
            parts = []
            for rows, lb, e in zip(strips, lbs, es):
                beta = jnp.exp(lb)
                dz = e * (1.0 - beta) - (run_e[rows] + e_before[rows]) * beta
                if valid is not None:
                    dz = jnp.where(valid(rows), dz, 0.0)
                parts.append(dz.astype(BF16))
            return _rows(parts)

        def block(kj, carry, valid):
            off = pl.multiple_of(kj * tq, tq)
            kb, kt, vt = kb_scr[pl.ds(off, tq), :], kt_scr[kj], vt_scr[kj]
            run_lm0, run_e0, dq0, run_lm1, run_e1, dq1 = carry
            z0, da0 = _mm(qs[0], kt), _mm(dos[0], vt)
            z1, da1 = _mm(qs[1], kt), _mm(dos[1], vt)
            lbs0, split0, lm_sums0 = log_terms(z0, valid)
            lm_upto0 = _mm(split0, upto2)
            lbs1, split1, lm_sums1 = log_terms(z1, valid)
            lm_upto1 = _mm(split1, upto2)
            a0, es0, split0, e_sums0 = weights(lbs0, lts[0] - run_lm0, lm_upto0, da0, valid)
            e_before0 = _mm(split0, before2)
            a1, es1, split1, e_sums1 = weights(lbs1, lts[1] - run_lm1, lm_upto1, da1, valid)
            e_before1 = _mm(split1, before2)
            dz0 = score_grads(lbs0, es0, run_e0, e_before0, valid)
            dkt_blk = _mm(qts[0], dz0)
            dvt_blk = _mm(dots[0], a0)
            dq0 = dq0 + _mm(dz0, kb)
            dz1 = score_grads(lbs1, es1, run_e1, e_before1, valid)
            dkt_ref[kj] += dkt_blk + _mm(qts[1], dz1)
            dvt_ref[kj] += dvt_blk + _mm(dots[1], a1)
            dq1 = dq1 + _mm(dz1, kb)
            return run_lm0 + lm_sums0, run_e0 + e_sums0, dq0, run_lm1 + lm_sums1, run_e1 + e_sums1, dq1

        zero = (jnp.zeros((tq, 1), F32), jnp.zeros((tq, 1), F32), jnp.zeros((tq, HEAD_PAIR), F32))
        carry = lax.fori_loop(0, qi, lambda kj, cr: block(kj, cr, None), zero + zero)
        carry = block(qi, carry, causal)
        dq_ref[...] = jnp.where(head0, carry[2], carry[5]) * ATTN_SCALE

    blk = pl.BlockSpec((tq, HEAD_PAIR), lambda p, i: (i, p))
    seq = pl.BlockSpec((T // tq, HEAD_PAIR, tq), lambda p, i: (0, p, 0))
    transposed = jax.ShapeDtypeStruct((T // tq, SB_WIDTH, tq), F32)
    return pl.pallas_call(
        body, name=name, grid=(N_HEADS // 2, T // tq),
        in_specs=_attn_specs(T, tq) + [blk, blk], out_specs=[blk, seq, seq],
        out_shape=[jax.ShapeDtypeStruct((T, SB_WIDTH), F32), transposed, transposed],
        scratch_shapes=[pltpu.VMEM((T, HEAD_PAIR), BF16), pltpu.VMEM((T // tq, HEAD_PAIR, tq), BF16),
                        pltpu.VMEM((T // tq, HEAD_PAIR, tq), BF16)],
        compiler_params=_params(("arbitrary", "arbitrary")),
    )(proj, proj, proj, do, ltot)


def _branch(act_bf16, w_ref):
    return jnp.concatenate([_mm(act_bf16, w_ref[e]) for e in range(w_ref.shape[0])], axis=1)


def _mix_specs(T, D, tm, wbp, w_out):
    gate_col = (POOL_WIDTH + 3 * SB_WIDTH) // D
    row = lambda i: (i, 0)
    return [
        pl.BlockSpec((tm, D), row),
        pl.BlockSpec((tm, POOL_WIDTH), row),
        pl.BlockSpec((tm, SB_WIDTH), row),
        pl.BlockSpec((tm, D), lambda i: (i, gate_col)),
        pl.BlockSpec((tm, D), lambda i: (i, gate_col + 1)),
        pl.BlockSpec(wbp.shape, lambda i: (0, 0, 0)),
        pl.BlockSpec(wbp.shape, lambda i: (0, 0, 0)),
        pl.BlockSpec(w_out.shape, lambda i: (0, 0)),
    ]


def _mix_fwd(h, p, o, proj, wbp, wba, w_out, *, tm, name):
    T, D = h.shape
    tm = min(tm, T)

    def body(h_ref, p_ref, o_ref, glp_ref, gls_ref, wbp_ref, wba_ref, wout_ref, hout_ref, m_ref):
        yp = _branch(p_ref[...].astype(BF16), wbp_ref)
        ys = _branch(o_ref[...].astype(BF16), wba_ref)
        m = (jax.nn.sigmoid(glp_ref[...]) * yp + jax.nn.sigmoid(gls_ref[...]) * ys).astype(BF16)
        m_ref[...] = m
        hout_ref[...] = h_ref[...] + _mm(m, wout_ref[...])

    row = lambda i: (i, 0)
    return pl.pallas_call(
        body, name=name, grid=(T // tm,),
        in_specs=_mix_specs(T, D, tm, wbp, w_out),
        out_specs=[pl.BlockSpec((tm, D), row), pl.BlockSpec((tm, D), row)],
        out_shape=[jax.ShapeDtypeStruct((T, D), F32), jax.ShapeDtypeStruct((T, D), BF16)],
        compiler_params=_params(("arbitrary",)),
    )(h, p, o, proj, proj, wbp, wba, w_out)


def _mix_bwd(dh, p, o, proj, wbp, wba, w_out, *, tm, name):
    T, D = dh.shape
    tm = min(tm, T)
    bw = wbp.shape[2]

    def body(dh_ref, p_ref, o_ref, glp_ref, gls_ref, wbp_ref, wba_ref, wout_ref,
             dyp_ref, dys_ref, dp_ref, do_ref, dgl_ref):
        dm = _mm_nt(dh_ref[...].astype(BF16), wout_ref[...])
        yp = _branch(p_ref[...].astype(BF16), wbp_ref)
        ys = _branch(o_ref[...].astype(BF16), wba_ref)
        gp = jax.nn.sigmoid(glp_ref[...])
        gs = jax.nn.sigmoid(gls_ref[...])
        dyp = (dm * gp).astype(BF16)
        dys = (dm * gs).astype(BF16)
        dyp_ref[...] = dyp
        dys_ref[...] = dys
        dgl_ref[:, :D] = (dm * yp * gp * (1.0 - gp)).astype(BF16)
        dgl_ref[:, D:] = (dm * ys * gs * (1.0 - gs)).astype(BF16)
        dp = jnp.zeros(dp_ref.shape, F32)
        do_ = jnp.zeros(do_ref.shape, F32)
        for e in range(wbp_ref.shape[0]):
            dp += _mm_nt(dyp[:, e * bw:(e + 1) * bw], wbp_ref[e])
            do_ += _mm_nt(dys[:, e * bw:(e + 1) * bw], wba_ref[e])
        dp_ref[...] = dp
        do_ref[...] = do_

    row = lambda i: (i, 0)
    return pl.pallas_call(
        body, name=name, grid=(T // tm,),
        in_specs=_mix_specs(T, D, tm, wbp, w_out),
        out_specs=[pl.BlockSpec((tm, D), row), pl.BlockSpec((tm, D), row), pl.BlockSpec((tm, POOL_WIDTH), row),
                   pl.BlockSpec((tm, SB_WIDTH), row), pl.BlockSpec((tm, 2 * D), row)],
        out_shape=[jax.ShapeDtypeStruct((T, D), BF16), jax.ShapeDtypeStruct((T, D), BF16),
                   jax.ShapeDtypeStruct((T, POOL_WIDTH), F32), jax.ShapeDtypeStruct((T, SB_WIDTH), F32),
                   jax.ShapeDtypeStruct((T, 2 * D), BF16)],
        compiler_params=_params(("arbitrary",)),
    )(dh, p, o, proj, proj, wbp, wba, w_out)


def _adamw(w, g, m, v, *, name):
    R, C = w.shape
    tr = _row_tile(R, C)

    def body(w_ref, g_ref, m_ref, v_ref, d_ref, nm_ref, nv_ref):
        g_ = g_ref[...]
        m_ = ADAM_B1 * m_ref[...] + (1.0 - ADAM_B1) * g_
        v_ = ADAM_B2 * v_ref[...] + (1.0 - ADAM_B2) * (g_ * g_)
        m_hat = m_ / (1.0 - ADAM_B1 ** ADAM_STEP)
        v_hat = v_ / (1.0 - ADAM_B2 ** ADAM_STEP)
        d_ref[...] = -ADAM_LR * (m_hat / (jnp.sqrt(v_hat) + ADAM_EPS) + ADAM_WD * w_ref[...])
        nm_ref[...] = m_
        nv_ref[...] = v_

    spec = pl.BlockSpec((tr, C), lambda i: (i, 0))
    return pl.pallas_call(
        body, name=name, grid=(R // tr,), in_specs=[spec] * 4, out_specs=[spec] * 3,
        out_shape=[jax.ShapeDtypeStruct((R, C), F32)] * 3,
        compiler_params=_params(("arbitrary",)),
    )(w, g, m, v)


def _position():
    return lax.axis_index("x"), lax.axis_index("y"), lax.axis_index("c")


def _all_gather(shards, *, name, collective_id):
    n = len(shards)

    def body(*refs):
        ins, outs = refs[:n], refs[n:2 * n]
        send_sems, recv_sems, local_sems = refs[2 * n:]
        x, y, c = _position()
        me, sibling = (x, y, c), (x, y, 1 - c)
        chips = [(1 - x, y), (x, 1 - y), (1 - x, 1 - y)]

        barrier = pltpu.get_barrier_semaphore()
        for peer in [sibling] + [(*chip, c) for chip in chips]:
            pl.semaphore_signal(barrier, inc=1, device_id=peer, device_id_type=MESH)
        pl.semaphore_wait(barrier, 4)

        def block(a, pos):
            return outs[a].at[4 * pos[0] + 2 * pos[1] + pos[2]]

        def copy(a, k, pos, to, src=None):
            return pltpu.make_async_remote_copy(
                src_ref=block(a, pos) if src is None else src, dst_ref=block(a, pos),
                send_sem=send_sems.at[7 * a + k], recv_sem=recv_sems.at[7 * a + k],
                device_id=to, device_id_type=MESH)

        started = []
        for a in range(n):
            mine = pltpu.make_async_copy(ins[a], block(a, me), local_sems.at[a])
            mine.start()
            started.append(mine)
        sends = []
        for a in range(n):
            sends += [copy(a, 1 + j, me, (*chip, c), src=ins[a]) for j, chip in enumerate(chips)]
            sends.append(copy(a, 0, me, sibling, src=ins[a]))
        for cp in sends:
            cp.start()
        for j, chip in enumerate(chips):
            for a in range(n):
                copy(a, 1 + j, (*chip, c), me).wait_recv()
                passed = copy(a, 4 + j, (*chip, c), sibling)
                passed.start()
                sends.append(passed)
        for a in range(n):
            copy(a, 0, sibling, me).wait_recv()
            for j, chip in enumerate(chips):
                copy(a, 4 + j, (*chip, 1 - c), me).wait_recv()
        for cp in sends:
            cp.wait_send()
        for cp in started:
            cp.wait()

    return pl.kernel(
        body, name=name,
        out_type=[jax.ShapeDtypeStruct((N_DEV,) + s.shape, s.dtype) for s in shards],
        mesh=plsc.ScalarSubcoreMesh(axis_name="sequencer", num_cores=1),
        scratch_types=[pltpu.SemaphoreType.DMA((7 * n,)), pltpu.SemaphoreType.DMA((7 * n,)),
                       pltpu.SemaphoreType.DMA((n,))],
        compiler_params=pltpu.CompilerParams(collective_id=collective_id),
    )(*shards)


def _chip_sums(grads, *, name):
    _, R, C = grads.shape
    rc = 128 if R % 128 == 0 else R

    def body(g_ref, partial, out_ref, mine, theirs, send_sems, recv_sems, local_sems):
        x, y, c = _position()
        my_chip = 2 * x + y

        def swap(s):
            return pltpu.make_async_remote_copy(
                src_ref=g_ref.at[2 * s + (1 - c)], dst_ref=theirs.at[s],
                send_sem=send_sems.at[s], recv_sem=recv_sems.at[s],
                device_id=(x, y, 1 - c), device_id_type=MESH)

        def load(s):
            return pltpu.make_async_copy(g_ref.at[2 * s + c], mine.at[s], local_sems.at[s])

        for s in range(4):
            swap(s).start()
            load(s).start()
        for s in range(4):
            load(s).wait()
            swap(s).wait_recv()

        def chip_sum(chip, rows):
            return mine[chip, rows, :].astype(F32) + theirs[chip, rows, :].astype(F32)

        for j in (1, 2, 3):
            @pl.loop(0, R // rc)
            def _(t):
                rows = pl.ds(pl.multiple_of(t * rc, rc), rc)
                partial[j - 1, rows, :] = chip_sum(my_chip ^ j, rows).astype(BF16)

        @pl.loop(0, R // rc)
        def _(t):
            rows = pl.ds(pl.multiple_of(t * rc, rc), rc)
            out_ref[rows, :] = chip_sum(my_chip, rows)

        for s in range(4):
            swap(s).wait_send()

    vmem = pl.BlockSpec(memory_space=pltpu.VMEM)
    return pl.pallas_call(
        body, name=name,
        in_specs=[pl.BlockSpec(memory_space=pl.ANY)], out_specs=[vmem, vmem],
        out_shape=[jax.ShapeDtypeStruct((3, R, C), BF16), jax.ShapeDtypeStruct((R, C), F32)],
        scratch_shapes=[
            pltpu.VMEM((4, R, C), BF16), pltpu.VMEM((4, R, C), BF16),
            pltpu.SemaphoreType.DMA((4,)), pltpu.SemaphoreType.DMA((4,)), pltpu.SemaphoreType.DMA((4,)),
        ],
        compiler_params=_params(),
    )(grads)


def _cross_chips(partials, *, name, collective_id):
    n = len(partials)

    def body(*refs):
        ins, outs = refs[:n], refs[n:2 * n]
        send_sems, recv_sems = refs[2 * n:]
        x, y, c = _position()
        my_chip = 2 * x + y
        peers = [((my_chip ^ j) // 2, (my_chip ^ j) % 2, c) for j in (1, 2, 3)]

        barrier = pltpu.get_barrier_semaphore()
        for peer in peers:
            pl.semaphore_signal(barrier, inc=1, device_id=peer, device_id_type=MESH)
        pl.semaphore_wait(barrier, 3)

        copies = [
            pltpu.make_async_remote_copy(
                src_ref=ins[a].at[j], dst_ref=outs[a].at[j],
                send_sem=send_sems.at[3 * a + j], recv_sem=recv_sems.at[3 * a + j],
                device_id=peers[j], device_id_type=MESH)
            for a in range(n) for j in range(3)]
        for cp in copies:
            cp.start()
        for cp in copies:
            cp.wait_recv()
        for cp in copies:
            cp.wait_send()

    return pl.kernel(
        body, name=name,
        out_type=[jax.ShapeDtypeStruct(p.shape, p.dtype) for p in partials],
        mesh=plsc.ScalarSubcoreMesh(axis_name="sequencer", num_cores=1),
        scratch_types=[pltpu.SemaphoreType.DMA((3 * n,)), pltpu.SemaphoreType.DMA((3 * n,))],
        compiler_params=pltpu.CompilerParams(collective_id=collective_id),
    )(*partials)


def _owner_sum(own, landed, *, name):
    R, C = own.shape
    tr = _row_tile(R, C)

    def body(own_ref, landed_ref, out_ref):
        total = own_ref[...]
        for j in range(3):
            total = total + landed_ref[j].astype(F32)
        out_ref[...] = total

    return pl.pallas_call(
        body, name=name, grid=(R // tr,),
        in_specs=[pl.BlockSpec((tr, C), lambda i: (i, 0)), pl.BlockSpec((3, tr, C), lambda i: (0, i, 0))],
        out_specs=pl.BlockSpec((tr, C), lambda i: (i, 0)),
        out_shape=jax.ShapeDtypeStruct((R, C), F32),
        compiler_params=_params(("arbitrary",)),
    )(own, landed)


def _sum_devices(gathered, *, name):
    _, R, C = gathered.shape

    def body(in_ref, out_ref):
        total = in_ref[0]
        for d in range(1, N_DEV):
            total = total + in_ref[d]
        out_ref[...] = total

    return pl.pallas_call(
        body, name=name, grid=(1,),
        in_specs=[pl.BlockSpec((N_DEV, R, C), lambda i: (0, 0, 0))],
        out_specs=pl.BlockSpec((R, C), lambda i: (0, 0)),
        out_shape=jax.ShapeDtypeStruct((R, C), F32),
        compiler_params=_params(("arbitrary",)),
    )(gathered)


def _local_step(x, target, norms, pool_w_group, pool_scale, wgu1, wd1, w_in, wbp, wba, w_out, wgu2, wd2, exchange):
    n1g, nmg, n2g, nfg = norms
    D = x.shape[1]
    h1, gu1, hid1 = _ffn_fwd(x, n1g, wgu1, wd1, tm=512, name="ffn1_fwd")
    un, proj = _inproj_fwd(h1, nmg, w_in, tm=1024, name="inproj_fwd")
    p = _pool_fwd(proj, pool_w_group, pool_scale, name="pool_fwd")
    o, ltot = _attn_fwd(proj, name="attn_fwd")
    h2, m = _mix_fwd(h1, p, o, proj, wbp, wba, w_out, tm=256, name="mix_fwd")
    h3, gu2, hid2 = _ffn_fwd(h2, n2g, wgu2, wd2, tm=512, name="ffn2_fwd")
    dh3, df2, loss, d_nf = _loss_bwd(h3, target, nfg, tm=256, name="loss_bwd")

    d_wd2 = _wgrad_down(hid2, df2, tk=WGRAD_TOKENS, name="ffn2_wgrad_down")
    g_wd2, = exchange("ffn2_down", [d_wd2.reshape(N_DEV, FF_SHARD_PAD, D)])
    dh2, d_n2, n2, dgu2 = _ffn_bwd(dh3, df2, h2, n2g, gu2, wgu2, wd2, tm=256, name="ffn2_bwd")
    d_wgu2 = _wgrad_gate_up(n2, dgu2, tk=WGRAD_TOKENS, name="ffn2_wgrad_gate_up")
    g_wgu2, = exchange("ffn2_gate_up", [d_wgu2])

    dyp, dys, dp, do, dgl = _mix_bwd(dh2, p, o, proj, wbp, wba, w_out, tm=256, name="mix_bwd")
    d_wout = _wgrad_full(m, dh2, tk=WGRAD_TOKENS, name="wgrad_out")
    d_wbp = _wgrad_full(p, dyp, tk=WGRAD_TOKENS, name="wgrad_branch_pool", split_lanes=wbp.shape[2])
    d_wba = _wgrad_full(o, dys, tk=WGRAD_TOKENS, name="wgrad_branch_attn", split_lanes=wba.shape[2])
    g_wbp, g_wba, g_wout = exchange("mix", [d_wbp, d_wba, d_wout.reshape(N_DEV, D // N_DEV, D)])
    dxp, d_wgroup, d_scale = _pool_bwd(dp, proj, pool_w_group, pool_scale, name="pool_bwd")
    dq, dkt, dvt = _attn_bwd(proj, do, ltot, name="attn_bwd")
    dk, dv = (t.transpose(0, 2, 1).reshape(dq.shape) for t in (dkt, dvt))
    dproj = jnp.concatenate([dxp.astype(BF16), dq.astype(BF16), dk.astype(BF16), dv.astype(BF16), dgl], axis=1)
    dh1, df1, d_nm = _inproj_bwd(dproj, dh2, h1, nmg, w_in, tm=1024, name="inproj_bwd")
    d_wd1 = _wgrad_down(hid1, df1, tk=WGRAD_TOKENS, name="ffn1_wgrad_down")
    g_wd1, = exchange("ffn1_down", [d_wd1.reshape(N_DEV, FF_SHARD_PAD, D)])
    d_win = _wgrad_in(un, dproj, tk=WGRAD_TOKENS, name="wgrad_in")
    g_win, = exchange("w_in", [d_win])

    dx, d_n1, n1, dgu1 = _ffn_bwd(dh1, df1, x, n1g, gu1, wgu1, wd1, tm=256, name="ffn1_bwd")
    replicated = exchange("replicated", [d_n1, d_nm, d_n2, d_nf, d_scale, d_wgroup, loss])
    d_wgu1 = _wgrad_gate_up(n1, dgu1, tk=WGRAD_TOKENS, name="ffn1_wgrad_gate_up")
    g_wgu1, = exchange("ffn1_gate_up", [d_wgu1])

    sharded = (g_wgu1, g_wd1, g_win, g_wbp, g_wba, g_wout, g_wgu2, g_wd2)
    return dx, sharded, replicated


def _hidden_major(w):
    return jnp.swapaxes(w[0], 0, 1)


def _pad_gate_up(wt):
    d = wt.shape[1]
    wt = wt.astype(BF16).reshape(2, FF_SHARD, d)
    return jnp.pad(wt, ((0, 0), (0, FF_SHARD_PAD - FF_SHARD), (0, 0))).reshape(2 * FF_SHARD_PAD, d)


def _unpad_gate_up(gt):
    d = gt.shape[1]
    return gt.reshape(2, FF_SHARD_PAD, d)[:, :FF_SHARD].reshape(2 * FF_SHARD, d)


def _pad_down(w):
    return jnp.pad(w.astype(BF16), ((0, FF_SHARD_PAD - FF_SHARD), (0, 0)))


def kernel(x, ffn1_norm, ffn1_w_gate_up, ffn1_w_down, mix_norm, w_in, pool_w_group, pool_scale, w_branch_pool, w_branch_attn, w_out, ffn2_norm, ffn2_w_gate_up, ffn2_w_down, final_norm, loss_target, m_ffn1_norm, m_ffn1_w_gate_up, m_ffn1_w_down, m_mix_norm, m_w_in, m_pool_w_group, m_pool_scale, m_w_branch_pool, m_w_branch_attn, m_w_out, m_ffn2_norm, m_ffn2_w_gate_up, m_ffn2_w_down, m_final_norm, v_ffn1_norm, v_ffn1_w_gate_up, v_ffn1_w_down, v_mix_norm, v_w_in, v_pool_w_group, v_pool_scale, v_w_branch_pool, v_w_branch_attn, v_w_out, v_ffn2_norm, v_ffn2_w_gate_up, v_ffn2_w_down, v_final_norm):
    D = x.shape[-1]
    weights = dict(ffn1_norm=ffn1_norm, ffn1_w_gate_up=ffn1_w_gate_up, ffn1_w_down=ffn1_w_down, mix_norm=mix_norm,
                   w_in=w_in, pool_w_group=pool_w_group, pool_scale=pool_scale, w_branch_pool=w_branch_pool,
                   w_branch_attn=w_branch_attn, w_out=w_out, ffn2_norm=ffn2_norm, ffn2_w_gate_up=ffn2_w_gate_up,
                   ffn2_w_down=ffn2_w_down, final_norm=final_norm)
    first = dict(ffn1_norm=m_ffn1_norm, ffn1_w_gate_up=m_ffn1_w_gate_up, ffn1_w_down=m_ffn1_w_down,
                 mix_norm=m_mix_norm, w_in=m_w_in, pool_w_group=m_pool_w_group, pool_scale=m_pool_scale,
                 w_branch_pool=m_w_branch_pool, w_branch_attn=m_w_branch_attn, w_out=m_w_out,
                 ffn2_norm=m_ffn2_norm, ffn2_w_gate_up=m_ffn2_w_gate_up, ffn2_w_down=m_ffn2_w_down,
                 final_norm=m_final_norm)
    second = dict(ffn1_norm=v_ffn1_norm, ffn1_w_gate_up=v_ffn1_w_gate_up, ffn1_w_down=v_ffn1_w_down,
                  mix_norm=v_mix_norm, w_in=v_w_in, pool_w_group=v_pool_w_group, pool_scale=v_pool_scale,
                  w_branch_pool=v_w_branch_pool, w_branch_attn=v_w_branch_attn, w_out=v_w_out,
                  ffn2_norm=v_ffn2_norm, ffn2_w_gate_up=v_ffn2_w_gate_up, ffn2_w_down=v_ffn2_w_down,
                  final_norm=v_final_norm)
    order = list(weights)

    wgu1, wd1 = _all_gather([_pad_gate_up(_hidden_major(ffn1_w_gate_up)), _pad_down(ffn1_w_down[0])],
                            name="all_gather_ffn1", collective_id=0)
    win_g, = _all_gather([w_in[0].astype(BF16)], name="all_gather_w_in", collective_id=1)
    wbp_g, wba_g, wout_g = _all_gather(
        [w_branch_pool[0].astype(BF16), w_branch_attn[0].astype(BF16), w_out[0].astype(BF16)],
        name="all_gather_mix", collective_id=2)
    wgu2, wd2 = _all_gather([_pad_gate_up(_hidden_major(ffn2_w_gate_up)), _pad_down(ffn2_w_down[0])],
                            name="all_gather_ffn2", collective_id=3)
    wd1 = wd1.reshape(N_DEV * FF_SHARD_PAD, D)
    wd2 = wd2.reshape(N_DEV * FF_SHARD_PAD, D)
    wout_g = wout_g.reshape(D, D)

    cross_ids = {"ffn2_down": 4, "ffn2_gate_up": 5, "mix": 6, "ffn1_down": 7, "w_in": 9, "ffn1_gate_up": 10}
    small = ["ffn1_norm", "mix_norm", "ffn2_norm", "final_norm", "pool_scale", "pool_w_group"]

    def tile_rows(a):
        a = a.reshape(-1, 128)
        return jnp.pad(a, ((0, -a.shape[0] % 8), (0, 0)))

    def exchange(tag, group):
        if tag == "replicated":
            slab = jnp.concatenate([tile_rows(g) for g in group[:-1]] + [jnp.broadcast_to(group[-1], (8, 128))], axis=0)
            return _all_gather([slab], name="all_gather_replicated", collective_id=8)
        sums = [_chip_sums(g, name=f"chip_sums_{tag}_{i}") for i, g in enumerate(group)]
        landed = _cross_chips([s[0] for s in sums], name="cross_chips_" + tag, collective_id=cross_ids[tag])
        return [(s[1], l) for s, l in zip(sums, landed)]

    norms = (ffn1_norm, mix_norm, ffn2_norm, final_norm.reshape(1, D))
    dx, sharded, (slabs,) = _local_step(
        x[0], loss_target[0], norms, pool_w_group[0], pool_scale, wgu1, wd1, win_g, wbp_g, wba_g, wout_g, wgu2, wd2,
        exchange)
    names = ["ffn1_w_gate_up", "ffn1_w_down", "w_in", "w_branch_pool", "w_branch_attn", "w_out",
             "ffn2_w_gate_up", "ffn2_w_down"]
    grads = {k: _owner_sum(own, landed, name="owner_sum_" + k)
             for k, (own, landed) in reversed(list(zip(names, sharded)))}
    for k in ("ffn1_w_gate_up", "ffn2_w_gate_up"):
        grads[k] = _unpad_gate_up(grads[k])
    for k in ("ffn1_w_down", "ffn2_w_down"):
        grads[k] = grads[k][:FF_SHARD]

    rows = [weights[k].size // 128 for k in small]
    padded_rows = [-(-r // 8) * 8 for r in rows]
    starts = [sum(padded_rows[:i]) for i in range(len(rows) + 1)]
    total = _sum_devices(slabs, name="sum_replicated")
    loss_out = total[starts[-1], 0]

    small_w = jnp.concatenate([tile_rows(weights[k]) for k in small], axis=0)
    small_m = jnp.concatenate([tile_rows(first[k]) for k in small], axis=0)
    small_v = jnp.concatenate([tile_rows(second[k]) for k in small], axis=0)
    small_out = _adamw(small_w, total[:starts[-1]], small_m, small_v, name="adamw_replicated")
    delta, new_m, new_v = {}, {}, {}
    for name_, start, n_rows in zip(small, starts, rows):
        shape = weights[name_].shape
        grads[name_] = total[start:start + n_rows].reshape(shape)
        delta[name_], new_m[name_], new_v[name_] = (a[start:start + n_rows].reshape(shape) for a in small_out)
    for name_ in order:
        if name_ in small:
            continue
        hidden_major = name_.endswith("w_gate_up")
        view = _hidden_major if hidden_major else (lambda a: a[0])
        back = (lambda a: jnp.swapaxes(a, 0, 1)[None]) if hidden_major else (lambda a: a[None])
        out = _adamw(view(weights[name_]), grads[name_], view(first[name_]), view(second[name_]),
                     name="adamw_" + name_)
        delta[name_], new_m[name_], new_v[name_] = (back(a) for a in out)
        grads[name_] = back(grads[name_])

    return (loss_out, dx[None], *[grads[k] for k in order], *[delta[k] for k in order],
            *[new_m[k] for k in order], *[new_v[k] for k in order])
```

```python
import functools

import jax
import jax.numpy as jnp
from jax import lax
from jax.experimental import pallas as pl
from jax.experimental.pallas import tpu as pltpu
from jax.experimental.pallas import tpu_sc as plsc

F32 = jnp.float32
BF16 = jnp.bfloat16
MESH = pl.DeviceIdType.MESH

RMS_EPS = 1e-6
N_DEV = 8
N_HEADS = 8
HEAD_DIM = 64
HEAD_PAIR = 2 * HEAD_DIM
POOL_WINDOWS = (2, 4, 8, 16)
POOL_GROUP = 128
POOL_WIDTH = 512
SB_WIDTH = 512
FF_SHARD = 352
FF_SHARD_PAD = 384
ATTN_BLOCK = 256
ATTN_SCALE = 0.125

ADAM_LR = 0.001
ADAM_B1 = 0.9
ADAM_B2 = 0.999
ADAM_EPS = 1e-08
ADAM_WD = 0.01
ADAM_STEP = 10

VMEM_LIMIT = 48 << 20
WGRAD_TOKENS = 2048


def _params(dims=None):
    return pltpu.CompilerParams(dimension_semantics=dims, vmem_limit_bytes=VMEM_LIMIT)


def _mm(a, b):
    return jnp.dot(a, b, preferred_element_type=F32)


def _mm_nt(a, b):
    return lax.dot_general(a, b, (((1,), (1,)), ((), ())), preferred_element_type=F32)


def _mm_tn(a, b):
    return lax.dot_general(a, b, (((0,), (0,)), ((), ())), preferred_element_type=F32)


def _row_tile(rows, cols):
    limit = max(8, (512 * 1024) // cols)
    return max(t for t in range(8, rows + 1, 8) if rows % t == 0 and (t <= limit or t == 8))


def _rstd(xf):
    return lax.rsqrt(jnp.mean(xf * xf, axis=-1, keepdims=True) + RMS_EPS)


def _rms_bwd(xf, gain, dn):
    r = _rstd(xf)
    xh = xf * r
    dgain = jnp.sum(dn * xh, axis=0, keepdims=True)
    dxh = dn * gain
    dx = r * (dxh - xh * jnp.mean(dxh * xh, axis=-1, keepdims=True))
    return dx, dgain


def _ffn_fwd(x, gain, wgu, wd, *, tm, name):
    T, D = x.shape
    tm = min(tm, T)
    nb, bw = wgu.shape[0] // 2, wgu.shape[1]

    def body(x_ref, gain_ref, wg_ref, wu_ref, wd_ref, h_ref, gu_ref, hid_ref, n_scr, acc):
        j = pl.program_id(1)

        @pl.when(j == 0)
        def _():
            xf = x_ref[...]
            n_scr[...] = (xf * _rstd(xf) * gain_ref[...]).astype(BF16)
            acc[...] = jnp.zeros_like(acc)

        n = n_scr[...]
        g = _mm_nt(n, wg_ref[...])
        u = _mm_nt(n, wu_ref[...])
        gu_ref[0] = g.astype(BF16)
        gu_ref[1] = u.astype(BF16)
        hid = (g * jax.nn.sigmoid(g) * u).astype(BF16)
        hid_ref[...] = hid
        acc[...] += _mm(hid, wd_ref[...])

        @pl.when(j == nb - 1)
        def _():
            h_ref[...] = x_ref[...] + 0.5 * acc[...]

    return pl.pallas_call(
        body, name=name, grid=(T // tm, nb),
        in_specs=[
            pl.BlockSpec((tm, D), lambda i, j: (i, 0)),
            pl.BlockSpec((1, D), lambda i, j: (0, 0)),
            pl.BlockSpec((None, bw, D), lambda i, j: (j, 0, 0)),
            pl.BlockSpec((None, bw, D), lambda i, j: (j + nb, 0, 0)),
            pl.BlockSpec((bw, D), lambda i, j: (j, 0)),
        ],
        out_specs=[
            pl.BlockSpec((tm, D), lambda i, j: (i, 0)),
            pl.BlockSpec((2, tm, bw), lambda i, j: (0, i, j)),
            pl.BlockSpec((tm, bw), lambda i, j: (i, j)),
        ],
        out_shape=[jax.ShapeDtypeStruct((T, D), F32), jax.ShapeDtypeStruct((2, T, nb * bw), BF16),
                   jax.ShapeDtypeStruct((T, nb * bw), BF16)],
        scratch_shapes=[pltpu.VMEM((tm, D), BF16), pltpu.VMEM((tm, D), F32)],
        compiler_params=_params(("arbitrary", "arbitrary")),
    )(x, gain, wgu, wgu, wd)


AFTER = pl.BlockSpec(memory_space=pl.ANY)


def _ffn_bwd(dh, df, x, gain, gu, wgu, wd, after, *, tm, name):
    T, D = x.shape
    tm = min(tm, T)
    nb, bw = wgu.shape[0] // 2, wgu.shape[1]

    def body(dh_ref, df_ref, x_ref, gain_ref, gu_ref, wg_ref, wu_ref, wd_ref, after_ref,
             dx_ref, dgain_ref, n_ref, dgu_ref, dn_acc):
        i, j = pl.program_id(0), pl.program_id(1)

        @pl.when(j == 0)
        def _():
            xf = x_ref[...]
            n_ref[...] = (xf * _rstd(xf) * gain_ref[...]).astype(BF16)
            dn_acc[...] = jnp.zeros_like(dn_acc)

        @pl.when((i == 0) & (j == 0))
        def _():
            dgain_ref[...] = jnp.zeros_like(dgain_ref)

        dhid = _mm_nt(df_ref[...], wd_ref[...])
        g = gu_ref[0].astype(F32)
        u = gu_ref[1].astype(F32)
        s = jax.nn.sigmoid(g)
        silu = g * s
        dg =(dhid * u * (s * (1.0 + g * (1.0 - s)))).astype(BF16)
        du = (dhid * silu).astype(BF16)
        dgu_ref[0] = dg
        dgu_ref[1] = du
        dn_acc[...] += _mm(dg, wg_ref[...]) + _mm(du, wu_ref[...])

        @pl.when(j == nb - 1)
        def _():
            dx, dgain = _rms_bwd(x_ref[...], gain_ref[...], dn_acc[...])
            dx_ref[...] = dh_ref[...] + dx
            dgain_ref[...] += dgain

    row = lambda i, j: (i, 0)
    return pl.pallas_call(
        body, name=name, grid=(T // tm, nb),
        in_specs=[
            pl.BlockSpec((tm, D), row),
            pl.BlockSpec((tm, D), row),
            pl.BlockSpec((tm, D), row),
            pl.BlockSpec((1, D), lambda i, j: (0, 0)),
            pl.BlockSpec((2, tm, bw), lambda i, j: (0, i, j)),
            pl.BlockSpec((None, bw, D), lambda i, j: (j, 0, 0)),
            pl.BlockSpec((None, bw, D), lambda i, j: (j + nb, 0, 0)),
            pl.BlockSpec((bw, D), lambda i, j: (j, 0)),
            AFTER,
        ],
        out_specs=[
            pl.BlockSpec((tm, D), row),
            pl.BlockSpec((1, D), lambda i, j: (0, 0)),
            pl.BlockSpec((tm, D), row),
            pl.BlockSpec((2, tm, bw), lambda i, j: (0, i, j)),
        ],
        out_shape=[
            jax.ShapeDtypeStruct((T, D), F32),
            jax.ShapeDtypeStruct((1, D), F32),
            jax.ShapeDtypeStruct((T, D), BF16),
            jax.ShapeDtypeStruct((2, T, nb * bw), BF16),
        ],
        scratch_shapes=[pltpu.VMEM((tm, D), F32)],
        compiler_params=_params(("arbitrary", "arbitrary")),
    )(dh, df, x, gain, gu, wgu, wgu, wd, after)


def _wgrad(a, b, *, grid, a_spec, b_spec, out_spec, out_shape, acc_shape, name, split_lanes=0):
    nk = grid[2]

    def body(a_ref, b_ref, o_ref, acc):
        k = pl.program_id(2)

        @pl.when(k == 0)
        def _():
            acc[...] = jnp.zeros_like(acc)

        acc[...] += _mm_tn(a_ref[...].astype(BF16), b_ref[...].astype(BF16))

        @pl.when(k == nk - 1)
        def _():
            if split_lanes:
                for e in range(o_ref.shape[0]):
                    o_ref[e] = acc[:, e * split_lanes:(e + 1) * split_lanes].astype(o_ref.dtype)
            else:
                o_ref[...] = acc[...].astype(o_ref.dtype)

    return pl.pallas_call(
        body, name=name, grid=grid, in_specs=[a_spec, b_spec], out_specs=out_spec,
        out_shape=jax.ShapeDtypeStruct(out_shape, BF16),
        scratch_shapes=[pltpu.VMEM(acc_shape, F32)],
        compiler_params=_params(("arbitrary", "arbitrary", "arbitrary")),
    )(a, b)


def _wgrad_gate_up(n, dgu, *, tk, name):
    T, D = n.shape
    tk = min(tk, T)
    bw = FF_SHARD_PAD * 2
    nb = dgu.shape[2] // bw
    return _wgrad(
        dgu, n, grid=(2 * nb, 1, T // tk), name=name,
        a_spec=pl.BlockSpec((None, tk, bw), lambda m, c, k: (m // nb, k, m % nb)),
        b_spec=pl.BlockSpec((tk, D), lambda m, c, k: (k, 0)),
        out_spec=pl.BlockSpec((None, bw, D), lambda m, c, k: (m, 0, 0)),
        out_shape=(2 * nb, bw, D), acc_shape=(bw, D))


def _wgrad_down(hid, df, *, tk, name):
    T, D = df.shape
    tk = min(tk, T)
    bw = FF_SHARD_PAD * 2
    nb = hid.shape[1] // bw
    return _wgrad(
        hid, df, grid=(nb, 1, T // tk), name=name,
        a_spec=pl.BlockSpec((tk, bw), lambda m, c, k: (k, m)),
        b_spec=pl.BlockSpec((tk, D), lambda m, c, k: (k, 0)),
        out_spec=pl.BlockSpec((bw, D), lambda m, c, k: (m, 0)),
        out_shape=(nb * bw, D), acc_shape=(bw, D))


def _wgrad_in(un, dproj, *, tk, name):
    T, D = un.shape
    tk = min(tk, T)
    bw = dproj.shape[1] // N_DEV
    return _wgrad(
        un, dproj, grid=(1, N_DEV, T // tk), name=name,
        a_spec=pl.BlockSpec((tk, D), lambda m, c, k: (k, 0)),
        b_spec=pl.BlockSpec((tk, bw), lambda m, c, k: (k, c)),
        out_spec=pl.BlockSpec((None, D, bw), lambda m, c, k: (c, 0, 0)),
        out_shape=(N_DEV, D, bw), acc_shape=(D, bw))


def _wgrad_full(a, b, *, tk, name, split_lanes=0):
    T, M = a.shape
    tk = min(tk, T)
    N = b.shape[1]
    if split_lanes:
        out_shape = (N // split_lanes, M, split_lanes)
        out_spec = pl.BlockSpec(out_shape, lambda m, c, k: (0, 0, 0))
    else:
        out_shape = (M, N)
        out_spec = pl.BlockSpec(out_shape, lambda m, c, k: (0, 0))
    return _wgrad(
        a, b, grid=(1, 1, T // tk), name=name,
        a_spec=pl.BlockSpec((tk, M), lambda m, c, k: (k, 0)),
        b_spec=pl.BlockSpec((tk, N), lambda m, c, k: (k, 0)),
        out_spec=out_spec, out_shape=out_shape, acc_shape=(M, N), split_lanes=split_lanes)


def _loss_bwd(h, target, gain, *, tm, name):
    T, D = h.shape
    tm = min(tm, T)

    def body(h_ref, t_ref, gain_ref, dh_ref, df_ref, loss_ref, dgain_ref):
        @pl.when(pl.program_id(0) == 0)
        def _():
            loss_ref[...] = jnp.zeros_like(loss_ref)
            dgain_ref[...] = jnp.zeros_like(dgain_ref)

        xf = h_ref[...]
        gain = gain_ref[...]
        err = xf * _rstd(xf) * gain - t_ref[...]
        loss_ref[...] += 0.5 * jnp.sum(jnp.mean(err * err, axis=-1, keepdims=True), axis=0, keepdims=True)
        dx, dgain = _rms_bwd(xf, gain, err * (1.0 / D))
        dh_ref[...] = dx
        df_ref[...] = (0.5 * dx).astype(BF16)
        dgain_ref[...] += dgain

    row = lambda i: (i, 0)
    fixed = lambda i: (0, 0)
    return pl.pallas_call(
        body, name=name, grid=(T // tm,),
        in_specs=[pl.BlockSpec((tm, D), row), pl.BlockSpec((tm, D), row), pl.BlockSpec((1, D), fixed)],
        out_specs=[pl.BlockSpec((tm, D), row), pl.BlockSpec((tm, D), row), pl.BlockSpec((1, 128), fixed),
                   pl.BlockSpec((1, D), fixed)],
        out_shape=[jax.ShapeDtypeStruct((T, D), F32), jax.ShapeDtypeStruct((T, D), BF16),
                   jax.ShapeDtypeStruct((1, 128), F32), jax.ShapeDtypeStruct((1, D), F32)],
        compiler_params=_params(("arbitrary",)),
    )(h, target, gain)


def _inproj_fwd(h, gain, w_in, *, tm, name):
    T, D = h.shape
    tm = min(tm, T)
    nb, bw = w_in.shape[0], w_in.shape[2]

    def body(h_ref, gain_ref, w_ref, un_ref, proj_ref):
        @pl.when(pl.program_id(1) == 0)
        def _():
            xf = h_ref[...]
            un_ref[...] = (xf * _rstd(xf) * gain_ref[...]).astype(BF16)

        proj_ref[...] = _mm(un_ref[...], w_ref[...])

    return pl.pallas_call(
        body, name=name, grid=(T // tm, nb),
        in_specs=[
            pl.BlockSpec((tm, D), lambda i, j: (i, 0)),
            pl.BlockSpec((1, D), lambda i, j: (0, 0)),
            pl.BlockSpec((None, D, bw), lambda i, j: (j, 0, 0)),
        ],
        out_specs=[pl.BlockSpec((tm, D), lambda i, j: (i, 0)), pl.BlockSpec((tm, bw), lambda i, j: (i, j))],
        out_shape=[jax.ShapeDtypeStruct((T, D), BF16), jax.ShapeDtypeStruct((T, nb * bw), F32)],
        compiler_params=_params(("arbitrary", "arbitrary")),
    )(h, gain, w_in)


def _inproj_bwd(dproj, dh, h, gain, w_in, *, tm, name):
    T, D = h.shape
    tm = min(tm, T)
    nb, bw = w_in.shape[0], w_in.shape[2]

    def body(dp_ref, dh_ref, h_ref, gain_ref, w_ref, dx_ref, df_ref, dgain_ref, acc):
        i, j = pl.program_id(0), pl.program_id(1)

        @pl.when(j == 0)
        def _():
            acc[...] = jnp.zeros_like(acc)

        @pl.when((i == 0) & (j == 0))
        def _():
            dgain_ref[...] = jnp.zeros_like(dgain_ref)

        acc[...] += _mm_nt(dp_ref[...], w_ref[...])

        @pl.when(j == nb - 1)
        def _():
            dx, dgain = _rms_bwd(h_ref[...], gain_ref[...], acc[...])
            dh_in = dh_ref[...] + dx
            dx_ref[...] = dh_in
            df_ref[...] = (0.5 * dh_in).astype(BF16)
            dgain_ref[...] += dgain

    row = lambda i, j: (i, 0)
    return pl.pallas_call(
        body, name=name, grid=(T // tm, nb),
        in_specs=[
            pl.BlockSpec((tm, bw), lambda i, j: (i, j)),
            pl.BlockSpec((tm, D), row),
            pl.BlockSpec((tm, D), row),
            pl.BlockSpec((1, D), lambda i, j: (0, 0)),
            pl.BlockSpec((None, D, bw), lambda i, j: (j, 0, 0)),
        ],
        out_specs=[pl.BlockSpec((tm, D), row), pl.BlockSpec((tm, D), row), pl.BlockSpec((1, D), lambda i, j: (0, 0))],
        out_shape=[jax.ShapeDtypeStruct((T, D), F32), jax.ShapeDtypeStruct((T, D), BF16),
                   jax.ShapeDtypeStruct((1, D), F32)],
        scratch_shapes=[pltpu.VMEM((tm, D), F32)],
        compiler_params=_params(("arbitrary", "arbitrary")),
    )(dproj, dh, h, gain, w_in)


def _window_sum(x, row, doublings, *, backward):
    T = x.shape[0]
    s = x
    for k in range(doublings):
        sh = 1 << k
        if backward:
            s = s + jnp.where(row < T - sh, pltpu.roll(s, T - sh, 0), 0.0)
        else:
            s = s + jnp.where(row >= sh, pltpu.roll(s, sh, 0), 0.0)
    return s


def _pool_fwd(proj, w_group, scale, *, name):
    T = proj.shape[0]

    def body(xp_ref, w_ref, scale_ref, p_ref):
        row = lax.broadcasted_iota(jnp.int32, (T, POOL_GROUP), 0)
        for gi, window in enumerate(POOL_WINDOWS):
            cols = slice(gi * POOL_GROUP, (gi + 1) * POOL_GROUP)
            x = xp_ref[:, cols]
            inv_count = 1.0 / jnp.minimum(row + 1, window).astype(F32)
            yc = _window_sum(x, row, gi + 1, backward=False) * inv_count - x
            pre = _mm(yc.astype(BF16), w_ref[gi].astype(BF16))
            p_ref[:, cols] = pre * scale_ref[:, cols]

    return pl.pallas_call(
        body, name=name, grid=(1,),
        in_specs=[
            pl.BlockSpec((T, POOL_WIDTH), lambda i: (0, 0)),
            pl.BlockSpec(w_group.shape, lambda i: (0, 0, 0)),
            pl.BlockSpec((1, POOL_WIDTH), lambda i: (0, 0)),
        ],
        out_specs=pl.BlockSpec((T, POOL_WIDTH), lambda i: (0, 0)),
        out_shape=jax.ShapeDtypeStruct((T, POOL_WIDTH), F32),
        compiler_params=_params(("arbitrary",)),
    )(proj, w_group, scale)


def _pool_bwd(dp, proj, w_group, scale, *, name):
    T = proj.shape[0]

    def body(dp_ref, xp_ref, w_ref, scale_ref, dxp_ref, dw_ref, dscale_ref):
        row = lax.broadcasted_iota(jnp.int32, (T, POOL_GROUP), 0)
        for gi, window in enumerate(POOL_WINDOWS):
            cols = slice(gi * POOL_GROUP, (gi + 1) * POOL_GROUP)
            x = xp_ref[:, cols]
            inv_count = 1.0 / jnp.minimum(row + 1, window).astype(F32)
            yc = (_window_sum(x, row, gi + 1, backward=False) * inv_count - x).astype(BF16)
            w = w_ref[gi].astype(BF16)
            pre = _mm(yc, w)
            dpg = dp_ref[:, cols]
            dscale_ref[:, cols] = jnp.sum(dpg * pre, axis=0, keepdims=True)
            dpre = (dpg * scale_ref[:, cols]).astype(BF16)
            dw_ref[gi] = _mm_tn(yc, dpre)
            dyc = _mm_nt(dpre, w)
            dxp_ref[:, cols] = _window_sum(dyc * inv_count, row, gi + 1, backward=True) - dyc

    return pl.pallas_call(
        body, name=name, grid=(1,),
        in_specs=[
            pl.BlockSpec((T, POOL_WIDTH), lambda i: (0, 0)),
            pl.BlockSpec((T, POOL_WIDTH), lambda i: (0, 0)),
            pl.BlockSpec(w_group.shape, lambda i: (0, 0, 0)),
            pl.BlockSpec((1, POOL_WIDTH), lambda i: (0, 0)),
        ],
        out_specs=[
            pl.BlockSpec((T, POOL_WIDTH), lambda i: (0, 0)),
            pl.BlockSpec(w_group.shape, lambda i: (0, 0, 0)),
            pl.BlockSpec((1, POOL_WIDTH), lambda i: (0, 0)),
        ],
        out_shape=[jax.ShapeDtypeStruct((T, POOL_WIDTH), F32), jax.ShapeDtypeStruct(w_group.shape, F32),
                   jax.ShapeDtypeStruct((1, POOL_WIDTH), F32)],
        compiler_params=_params(("arbitrary",)),
    )(dp, proj, w_group, scale)


ATTN_STRIP = 32


def _log_sigmoids(z):
    lb = jnp.minimum(z, 0.0) - jnp.log(1.0 + jnp.exp(-jnp.abs(z)))
    return lb, lb - z


def _transposed_blocks(x_ref, blocks_scr, tq):
    for b in range(blocks_scr.shape[0]):
        blocks_scr[b] = x_ref[b * tq:(b + 1) * tq, :].T.astype(BF16)


def _split_bf16(x):
    hi = x.astype(BF16)
    return hi, (x - hi.astype(F32)).astype(BF16)


def _strips(n):
    return [slice(i, i + ATTN_STRIP) for i in range(0, n, ATTN_STRIP)]


def _rows(parts):
    return jnp.concatenate(parts, axis=0)


def _attn_specs(T, tq):
    q_col = POOL_WIDTH // HEAD_PAIR
    k_col = q_col + SB_WIDTH // HEAD_PAIR
    v_col = k_col + SB_WIDTH // HEAD_PAIR
    return [
        pl.BlockSpec((tq, HEAD_PAIR), lambda p, i: (i, q_col + p)),
        pl.BlockSpec((T, HEAD_PAIR), lambda p, i: (0, k_col + p)),
        pl.BlockSpec((T, HEAD_PAIR), lambda p, i: (0, v_col + p)),
    ]


def _attn_fwd(proj, *, name):
    T = proj.shape[0]
    tq = ATTN_BLOCK

    def body(q_ref, k_ref, v_ref, o_ref, lt_ref, kt_scr, vb_scr):
        qi = pl.program_id(1)

        @pl.when(qi == 0)
        def _():
            _transposed_blocks(k_ref, kt_scr, tq)
            vb_scr[...] = v_ref[...].astype(BF16)

        head0 = lax.broadcasted_iota(jnp.int32, (tq, HEAD_PAIR), 1) < HEAD_DIM
        q = q_ref[...] * ATTN_SCALE
        qs = (jnp.where(head0, q, 0.0).astype(BF16), jnp.where(head0, 0.0, q).astype(BF16))
        r = lax.broadcasted_iota(jnp.int32, (tq, tq), 0)
        c = lax.broadcasted_iota(jnp.int32, (tq, tq), 1)
        later = (r > c).astype(BF16)
        later2 = _rows([later, later])
        causal = lambda rows: c[rows] < r[rows]
        strips = _strips(tq)

        def log_terms(z, valid):
            lbs, his, los, sums = [], [], [], []
            for rows in strips:
                lb, lm = _log_sigmoids(z[rows])
                if valid is not None:
                    lm = jnp.where(valid(rows), lm, 0.0)
                hi, lo = _split_bf16(lm)
                lbs.append(lb)
                his.append(hi)
                los.append(lo)
                sums.append(jnp.sum(lm, axis=1, keepdims=True))
            return lbs, jnp.concatenate([_rows(his), _rows(los)], axis=1), _rows(sums)

        def weights(lbs, run, after, valid):
            parts = []
            for rows, lb in zip(strips, lbs):
                a = jnp.exp(lb + run[rows] + after[rows])
                if valid is not None:
                    a = jnp.where(valid(rows), a, 0.0)
                parts.append(a.astype(BF16))
            return _rows(parts)

        def block(kj, carry, valid):
            kt = kt_scr[kj]
            vb = vb_scr[pl.ds(pl.multiple_of(kj * tq, tq), tq), :]
            run0, o0, run1, o1 = carry
            z0 = _mm(qs[0], kt)
            z1 = _mm(qs[1], kt)
            lbs0, split0, sums0 = log_terms(z0, valid)
            after0 = _mm(split0, later2)
            lbs1, split1, sums1 = log_terms(z1, valid)
            after1 = _mm(split1, later2)
            o0 = o0 + _mm(weights(lbs0, run0, after0, valid), vb)
            o1 = o1 + _mm(weights(lbs1, run1, after1, valid), vb)
            return run0 + sums0, o0, run1 + sums1, o1

        zero = (jnp.zeros((tq, 1), F32), jnp.zeros((tq, HEAD_PAIR), F32))
        carry = block(qi, zero + zero, causal)
        carry = lax.fori_loop(0, qi, lambda it, cr: block(qi - 1 - it, cr, None), carry)
        o_ref[...] = jnp.where(head0, carry[1], carry[3])
        lt_ref[...] = jnp.where(head0, carry[0], carry[2])

    out_spec = pl.BlockSpec((tq, HEAD_PAIR), lambda p, i: (i, p))
    return pl.pallas_call(
        body, name=name, grid=(N_HEADS // 2, T // tq),
        in_specs=_attn_specs(T, tq), out_specs=[out_spec, out_spec],
        out_shape=[jax.ShapeDtypeStruct((T, SB_WIDTH), F32), jax.ShapeDtypeStruct((T, SB_WIDTH), F32)],
        scratch_shapes=[pltpu.VMEM((T // tq, HEAD_PAIR, tq), BF16), pltpu.VMEM((T, HEAD_PAIR), BF16)],
        compiler_params=_params(("arbitrary", "arbitrary")),
    )(proj, proj, proj)


def _attn_bwd(proj, do, ltot, after, *, name):
    T = proj.shape[0]
    tq = ATTN_BLOCK

    def body(q_ref, k_ref, v_ref, do_ref, lt_ref, after_ref, dq_ref, dkt_ref, dvt_ref, kb_scr, kt_scr, vt_scr):
        qi = pl.program_id(1)

        @pl.when(qi == 0)
        def _():
            kb_scr[...] = k_ref[...].astype(BF16)
            _transposed_blocks(k_ref, kt_scr, tq)
            _transposed_blocks(v_ref, vt_scr, tq)
            dkt_ref[...] = jnp.zeros_like(dkt_ref)
            dvt_ref[...] = jnp.zeros_like(dvt_ref)

        head0 = lax.broadcasted_iota(jnp.int32, (tq, HEAD_PAIR), 1) < HEAD_DIM
        q, do_, lt = q_ref[...] * ATTN_SCALE, do_ref[...], lt_ref[...]
        qs = (jnp.where(head0, q, 0.0).astype(BF16), jnp.where(head0, 0.0, q).astype(BF16))
        q_heads = (jnp.where(head0, q, 0.0), jnp.where(head0, 0.0, q))
        do_heads = (jnp.where(head0, do_, 0.0), jnp.where(head0, 0.0, do_))
        dos = tuple(d.astype(BF16) for d in do_heads)
        qts = tuple(x.T.astype(BF16) for x in q_heads)
        dots = tuple(d.T.astype(BF16) for d in do_heads)
        lts = (jnp.max(jnp.where(head0, lt, -jnp.inf), axis=1, keepdims=True),
               jnp.max(jnp.where(head0, -jnp.inf, lt), axis=1, keepdims=True))
        r = lax.broadcasted_iota(jnp.int32, (tq, tq), 0)
        c = lax.broadcasted_iota(jnp.int32, (tq, tq), 1)
        upto = (r <= c).astype(BF16)
        before = (r < c).astype(BF16)
        upto2, before2 = _rows([upto, upto]), _rows([before, before])
        causal = lambda rows: c[rows] < r[rows]
        strips = _strips(tq)

        def log_terms(z, valid):
            lbs, his, los, sums = [], [], [], []
            for rows in strips:
                lb, lm = _log_sigmoids(z[rows])
                if valid is not None:
                    lm = jnp.where(valid(rows), lm, 0.0)
                hi, lo = _split_bf16(lm)
                lbs.append(lb)
                his.append(hi)
                los.append(lo)
                sums.append(jnp.sum(lm, axis=1, keepdims=True))
            return lbs, jnp.concatenate([_rows(his), _rows(los)], axis=1), _rows(sums)

        def weights(lbs, rest, lm_upto, da, valid):
            a_parts, es, his, los, sums = [], [], [], [], []
            for rows, lb in zip(strips, lbs):
                a = jnp.exp(lb + (rest[rows] - lm_upto[rows]))
                if valid is not None:
                    a = jnp.where(valid(rows), a, 0.0)
                e = da[rows] * a
                hi, lo = _split_bf16(e)
                a_parts.append(a.astype(BF16))
                es.append(e)
                his.append(hi)
                los.append(lo)
                sums.append(jnp.sum(e, axis=1, keepdims=True))
            return _rows(a_parts), es, jnp.concatenate([_rows(his), _rows(los)], axis=1), _rows(sums)

        def score_grads(lbs, es, run_e, e_before, valid):
            parts = []
            for rows, lb, e in zip(strips, lbs, es):
                beta = jnp.exp(lb)
                dz = e * (1.0 - beta) - (run_e[rows] + e_before[rows]) * beta
                if valid is not None:
                    dz = jnp.where(valid(rows), dz, 0.0)
                parts.append(dz.astype(BF16))
            return _rows(parts)

        def block(kj, carry, valid):
            off = pl.multiple_of(kj * tq, tq)
            kb, kt, vt = kb_scr[pl.ds(off, tq), :], kt_scr[kj], vt_scr[kj]
            run_lm0, run_e0, dq0, run_lm1, run_e1, dq1 = carry
            z0, da0 = _mm(qs[0], kt), _mm(dos[0], vt)
            z1, da1 = _mm(qs[1], kt), _mm(dos[1], vt)
            lbs0, split0, lm_sums0 = log_terms(z0, valid)
            lm_upto0 = _mm(split0, upto2)
            lbs1, split1, lm_sums1 = log_terms(z1, valid)
            lm_upto1 = _mm(split1, upto2)
            a0, es0, split0, e_sums0 = weights(lbs0, lts[0] - run_lm0, lm_upto0, da0, valid)
            e_before0 = _mm(split0, before2)
            a1, es1, split1, e_sums1 = weights(lbs1, lts[1] - run_lm1, lm_upto1, da1, valid)
            e_before1 = _mm(split1, before2)
            dz0 = score_grads(lbs0, es0, run_e0, e_before0, valid)
            dkt_blk = _mm(qts[0], dz0)
            dvt_blk = _mm(dots[0], a0)
            dq0 = dq0 + _mm(dz0, kb)
            dz1 = score_grads(lbs1, es1, run_e1, e_before1, valid)
            dkt_ref[kj] += dkt_blk + _mm(qts[1], dz1)
            dvt_ref[kj] += dvt_blk + _mm(dots[1], a1)
            dq1 = dq1 + _mm(dz1, kb)
            return run_lm0 + lm_sums0, run_e0 + e_sums0, dq0, run_lm1 + lm_sums1, run_e1 + e_sums1, dq1

        zero = (jnp.zeros((tq, 1), F32), jnp.zeros((tq, 1), F32), jnp.zeros((tq, HEAD_PAIR), F32))
        carry = lax.fori_loop(0, qi, lambda kj, cr: block(kj, cr, None), zero + zero)
        carry = block(qi, carry, causal)
        dq_ref[...] = jnp.where(head0, carry[2], carry[5]) * ATTN_SCALE

    blk = pl.BlockSpec((tq, HEAD_PAIR), lambda p, i: (i, p))
    seq = pl.BlockSpec((T // tq, HEAD_PAIR, tq), lambda p, i: (0, p, 0))
    transposed = jax.ShapeDtypeStruct((T // tq, SB_WIDTH, tq), F32)
    return pl.pallas_call(
        body, name=name, grid=(N_HEADS // 2, T // tq),
        in_specs=_attn_specs(T, tq) + [blk, blk, AFTER], out_specs=[blk, seq, seq],
        out_shape=[jax.ShapeDtypeStruct((T, SB_WIDTH), F32), transposed, transposed],
        scratch_shapes=[pltpu.VMEM((T, HEAD_PAIR), BF16), pltpu.VMEM((T // tq, HEAD_PAIR, tq), BF16),
                        pltpu.VMEM((T // tq, HEAD_PAIR, tq), BF16)],
        compiler_params=_params(("arbitrary", "arbitrary")),
    )(proj, proj, proj, do, ltot, after)


def _branch(act_bf16, w_ref):
    return jnp.concatenate([_mm(act_bf16, w_ref[e]) for e in range(w_ref.shape[0])], axis=1)


def _mix_specs(T, D, tm, wbp, w_out):
    gate_col = (POOL_WIDTH + 3 * SB_WIDTH) // D
    row = lambda i: (i, 0)
    return [
        pl.BlockSpec((tm, D), row),
        pl.BlockSpec((tm, POOL_WIDTH), row),
        pl.BlockSpec((tm, SB_WIDTH), row),
        pl.BlockSpec((tm, D), lambda i: (i, gate_col)),
        pl.BlockSpec((tm, D), lambda i: (i, gate_col + 1)),
        pl.BlockSpec(wbp.shape, lambda i: (0, 0, 0)),
        pl.BlockSpec(wbp.shape, lambda i: (0, 0, 0)),
        pl.BlockSpec(w_out.shape, lambda i: (0, 0)),
    ]


def _mix_fwd(h, p, o, proj, wbp, wba, w_out, *, tm, name):
    T, D = h.shape
    tm = min(tm, T)

    def body(h_ref, p_ref, o_ref, glp_ref, gls_ref, wbp_ref, wba_ref, wout_ref, hout_ref, m_ref):
        yp = _branch(p_ref[...].astype(BF16), wbp_ref)
        ys = _branch(o_ref[...].astype(BF16), wba_ref)
        m = (jax.nn.sigmoid(glp_ref[...]) * yp + jax.nn.sigmoid(gls_ref[...]) * ys).astype(BF16)
        m_ref[...] = m
        hout_ref[...] = h_ref[...] + _mm(m, wout_ref[...])

    row = lambda i: (i, 0)
    return pl.pallas_call(
        body, name=name, grid=(T // tm,),
        in_specs=_mix_specs(T, D, tm, wbp, w_out),
        out_specs=[pl.BlockSpec((tm, D), row), pl.BlockSpec((tm, D), row)],
        out_shape=[jax.ShapeDtypeStruct((T, D), F32), jax.ShapeDtypeStruct((T, D), BF16)],
        compiler_params=_params(("arbitrary",)),
    )(h, p, o, proj, proj, wbp, wba, w_out)


def _mix_bwd(dh, p, o, proj, wbp, wba, w_out, after, *, tm, name):
    T, D = dh.shape
    tm = min(tm, T)
    bw = wbp.shape[2]

    def body(dh_ref, p_ref, o_ref, glp_ref, gls_ref, wbp_ref, wba_ref, wout_ref, after_ref,
             dyp_ref, dys_ref, dp_ref, do_ref, dgl_ref):
        dm = _mm_nt(dh_ref[...].astype(BF16), wout_ref[...])
        yp = _branch(p_ref[...].astype(BF16), wbp_ref)
        ys = _branch(o_ref[...].astype(BF16), wba_ref)
        gp = jax.nn.sigmoid(glp_ref[...])
        gs = jax.nn.sigmoid(gls_ref[...])
        dyp = (dm * gp).astype(BF16)
        dys = (dm * gs).astype(BF16)
        dyp_ref[...] = dyp
        dys_ref[...] = dys
        dgl_ref[:, :D] = (dm * yp * gp * (1.0 - gp)).astype(BF16)
        dgl_ref[:, D:] = (dm * ys * gs * (1.0 - gs)).astype(BF16)
        dp = jnp.zeros(dp_ref.shape, F32)
        do_ = jnp.zeros(do_ref.shape, F32)
        for e in range(wbp_ref.shape[0]):
            dp += _mm_nt(dyp[:, e * bw:(e + 1) * bw], wbp_ref[e])
            do_ += _mm_nt(dys[:, e * bw:(e + 1) * bw], wba_ref[e])
        dp_ref[...] = dp
        do_ref[...] = do_

    row = lambda i: (i, 0)
    return pl.pallas_call(
        body, name=name, grid=(T // tm,),
        in_specs=_mix_specs(T, D, tm, wbp, w_out) + [AFTER],
        out_specs=[pl.BlockSpec((tm, D), row), pl.BlockSpec((tm, D), row), pl.BlockSpec((tm, POOL_WIDTH), row),
                   pl.BlockSpec((tm, SB_WIDTH), row), pl.BlockSpec((tm, 2 * D), row)],
        out_shape=[jax.ShapeDtypeStruct((T, D), BF16), jax.ShapeDtypeStruct((T, D), BF16),
                   jax.ShapeDtypeStruct((T, POOL_WIDTH), F32), jax.ShapeDtypeStruct((T, SB_WIDTH), F32),
                   jax.ShapeDtypeStruct((T, 2 * D), BF16)],
        compiler_params=_params(("arbitrary",)),
    )(dh, p, o, proj, proj, wbp, wba, w_out, after)


def _adamw(w, g, m, v, *, name):
    R, C = w.shape
    tr = _row_tile(R, C)

    def body(w_ref, g_ref, m_ref, v_ref, d_ref, nm_ref, nv_ref):
        g_ = g_ref[...]
        m_ = ADAM_B1 * m_ref[...] + (1.0 - ADAM_B1) * g_
        v_ = ADAM_B2 * v_ref[...] + (1.0 - ADAM_B2) * (g_ * g_)
        m_hat = m_ / (1.0 - ADAM_B1 ** ADAM_STEP)
        v_hat = v_ / (1.0 - ADAM_B2 ** ADAM_STEP)
        d_ref[...] = -ADAM_LR * (m_hat / (jnp.sqrt(v_hat) + ADAM_EPS) + ADAM_WD * w_ref[...])
        nm_ref[...] = m_
        nv_ref[...] = v_

    spec = pl.BlockSpec((tr, C), lambda i: (i, 0))
    return pl.pallas_call(
        body, name=name, grid=(R // tr,), in_specs=[spec] * 4, out_specs=[spec] * 3,
        out_shape=[jax.ShapeDtypeStruct((R, C), F32)] * 3,
        compiler_params=_params(("arbitrary",)),
    )(w, g, m, v)


def _position():
    return lax.axis_index("x"), lax.axis_index("y"), lax.axis_index("c")


def _all_gather(shards, *, name, collective_id):
    n = len(shards)

    def body(*refs):
        ins, outs = refs[:n], refs[n:2 * n]
        send_sems, recv_sems, local_sems = refs[2 * n:]
        x, y, c = _position()
        me, sibling = (x, y, c), (x, y, 1 - c)
        chips = [(1 - x, y), (x, 1 - y), (1 - x, 1 - y)]

        barrier = pltpu.get_barrier_semaphore()
        for peer in [sibling] + [(*chip, c) for chip in chips]:
            pl.semaphore_signal(barrier, inc=1, device_id=peer, device_id_type=MESH)
        pl.semaphore_wait(barrier, 4)

        def block(a, pos):
            return outs[a].at[4 * pos[0] + 2 * pos[1] + pos[2]]

        def copy(a, k, pos, to, src=None):
            return pltpu.make_async_remote_copy(
                src_ref=block(a, pos) if src is None else src, dst_ref=block(a, pos),
                send_sem=send_sems.at[7 * a + k], recv_sem=recv_sems.at[7 * a + k],
                device_id=to, device_id_type=MESH)

        started = []
        for a in range(n):
            mine = pltpu.make_async_copy(ins[a], block(a, me), local_sems.at[a])
            mine.start()
            started.append(mine)
        sends = []
        for a in range(n):
            sends += [copy(a, 1 + j, me, (*chip, c), src=ins[a]) for j, chip in enumerate(chips)]
            sends.append(copy(a, 0, me, sibling, src=ins[a]))
        for cp in sends:
            cp.start()
        for j, chip in enumerate(chips):
            for a in range(n):
                copy(a, 1 + j, (*chip, c), me).wait_recv()
                passed = copy(a, 4 + j, (*chip, c), sibling)
                passed.start()
                sends.append(passed)
        for a in range(n):
            copy(a, 0, sibling, me).wait_recv()
            for j, chip in enumerate(chips):
                copy(a, 4 + j, (*chip, 1 - c), me).wait_recv()
        for cp in sends:
            cp.wait_send()
        for cp in started:
            cp.wait()

    return pl.kernel(
        body, name=name,
        out_type=[jax.ShapeDtypeStruct((N_DEV,) + s.shape, s.dtype) for s in shards],
        mesh=plsc.ScalarSubcoreMesh(axis_name="sequencer", num_cores=1),
        scratch_types=[pltpu.SemaphoreType.DMA((7 * n,)), pltpu.SemaphoreType.DMA((7 * n,)),
                       pltpu.SemaphoreType.DMA((n,))],
        compiler_params=pltpu.CompilerParams(collective_id=collective_id),
    )(*shards)


def _chip_sums(grads, *, name):
    _, R, C = grads.shape
    rc = 128 if R % 128 == 0 else R

    def body(g_ref, partial, out_ref, mine, theirs, send_sems, recv_sems, local_sems):
        x, y, c = _position()
        my_chip = 2 * x + y

        def swap(s):
            return pltpu.make_async_remote_copy(
                src_ref=g_ref.at[2 * s + (1 - c)], dst_ref=theirs.at[s],
                send_sem=send_sems.at[s], recv_sem=recv_sems.at[s],
                device_id=(x, y, 1 - c), device_id_type=MESH)

        def load(s):
            return pltpu.make_async_copy(g_ref.at[2 * s + c], mine.at[s], local_sems.at[s])

        for s in range(4):
            swap(s).start()
            load(s).start()
        for s in range(4):
            load(s).wait()
            swap(s).wait_recv()

        def chip_sum(chip, rows):
            return mine[chip, rows, :].astype(F32) + theirs[chip, rows, :].astype(F32)

        for j in (1, 2, 3):
            @pl.loop(0, R // rc)
            def _(t):
                rows = pl.ds(pl.multiple_of(t * rc, rc), rc)
                partial[j - 1, rows, :] = chip_sum(my_chip ^ j, rows).astype(BF16)

        @pl.loop(0, R // rc)
        def _(t):
            rows = pl.ds(pl.multiple_of(t * rc, rc), rc)
            out_ref[rows, :] = chip_sum(my_chip, rows)

        for s in range(4):
            swap(s).wait_send()

    vmem = pl.BlockSpec(memory_space=pltpu.VMEM)
    return pl.pallas_call(
        body, name=name,
        in_specs=[pl.BlockSpec(memory_space=pl.ANY)], out_specs=[vmem, vmem],
        out_shape=[jax.ShapeDtypeStruct((3, R, C), BF16), jax.ShapeDtypeStruct((R, C), F32)],
        scratch_shapes=[
            pltpu.VMEM((4, R, C), BF16), pltpu.VMEM((4, R, C), BF16),
            pltpu.SemaphoreType.DMA((4,)), pltpu.SemaphoreType.DMA((4,)), pltpu.SemaphoreType.DMA((4,)),
        ],
        compiler_params=_params(),
    )(grads)


def _cross_chips(partials, *, name, collective_id):
    n = len(partials)

    def body(*refs):
        ins, outs = refs[:n], refs[n:2 * n]
        send_sems, recv_sems = refs[2 * n:]
        x, y, c = _position()
        my_chip = 2 * x + y
        peers = [((my_chip ^ j) // 2, (my_chip ^ j) % 2, c) for j in (1, 2, 3)]

        barrier = pltpu.get_barrier_semaphore()
        for peer in peers:
            pl.semaphore_signal(barrier, inc=1, device_id=peer, device_id_type=MESH)
        pl.semaphore_wait(barrier, 3)

        copies = [
            pltpu.make_async_remote_copy(
                src_ref=ins[a].at[j], dst_ref=outs[a].at[j],
                send_sem=send_sems.at[3 * a + j], recv_sem=recv_sems.at[3 * a + j],
                device_id=peers[j], device_id_type=MESH)
            for a in range(n) for j in range(3)]
        for cp in copies:
            cp.start()
        for cp in copies:
            cp.wait_recv()
        for cp in copies:
            cp.wait_send()

    return pl.kernel(
        body, name=name,
        out_type=[jax.ShapeDtypeStruct(p.shape, p.dtype) for p in partials],
        mesh=plsc.ScalarSubcoreMesh(axis_name="sequencer", num_cores=1),
        scratch_types=[pltpu.SemaphoreType.DMA((3 * n,)), pltpu.SemaphoreType.DMA((3 * n,))],
        compiler_params=pltpu.CompilerParams(collective_id=collective_id),
    )(*partials)


def _owner_sum(own, landed, after, *, name):
    R, C = own.shape
    tr = _row_tile(R, C)

    def body(own_ref, landed_ref, after_ref, out_ref):
        total = own_ref[...]
        for j in range(3):
            total = total + landed_ref[j].astype(F32)
        out_ref[...] = total

    return pl.pallas_call(
        body, name=name, grid=(R // tr,),
        in_specs=[pl.BlockSpec((tr, C), lambda i: (i, 0)), pl.BlockSpec((3, tr, C), lambda i: (0, i, 0)), AFTER],
        out_specs=pl.BlockSpec((tr, C), lambda i: (i, 0)),
        out_shape=jax.ShapeDtypeStruct((R, C), F32),
        compiler_params=_params(("arbitrary",)),
    )(own, landed, after)


def _sum_devices(gathered, *, name):
    _, R, C = gathered.shape

    def body(in_ref, out_ref):
        total = in_ref[0]
        for d in range(1, N_DEV):
            total = total + in_ref[d]
        out_ref[...] = total

    return pl.pallas_call(
        body, name=name, grid=(1,),
        in_specs=[pl.BlockSpec((N_DEV, R, C), lambda i: (0, 0, 0))],
        out_specs=pl.BlockSpec((R, C), lambda i: (0, 0)),
        out_shape=jax.ShapeDtypeStruct((R, C), F32),
        compiler_params=_params(("arbitrary",)),
    )(gathered)


def _local_step(x, target, norms, pool_w_group, pool_scale, wgu1, wd1, w_in, wbp, wba, w_out, wgu2, wd2, exchange):
    n1g, nmg, n2g, nfg = norms
    D = x.shape[1]
    h1, gu1, hid1 = _ffn_fwd(x, n1g, wgu1, wd1, tm=512, name="ffn1_fwd")
    un, proj = _inproj_fwd(h1, nmg, w_in, tm=1024, name="inproj_fwd")
    p = _pool_fwd(proj, pool_w_group, pool_scale, name="pool_fwd")
    o, ltot = _attn_fwd(proj, name="attn_fwd")
    h2, m = _mix_fwd(h1, p, o, proj, wbp, wba, w_out, tm=256, name="mix_fwd")
    h3, gu2, hid2 = _ffn_fwd(h2, n2g, wgu2, wd2, tm=512, name="ffn2_fwd")
    dh3, df2, loss, d_nf = _loss_bwd(h3, target, nfg, tm=256, name="loss_bwd")

    d_wd2 = _wgrad_down(hid2, df2, tk=WGRAD_TOKENS, name="ffn2_wgrad_down")
    (g_wd2,), token = exchange("ffn2_down", [d_wd2.reshape(N_DEV, FF_SHARD_PAD, D)])
    dh2, d_n2, n2, dgu2 = _ffn_bwd(dh3, df2, h2, n2g, gu2, wgu2, wd2, token, tm=256, name="ffn2_bwd")
    d_wgu2 = _wgrad_gate_up(n2, dgu2, tk=WGRAD_TOKENS, name="ffn2_wgrad_gate_up")
    (g_wgu2,), token = exchange("ffn2_gate_up", [d_wgu2])

    dyp, dys, dp, do, dgl = _mix_bwd(dh2, p, o, proj, wbp, wba, w_out, token, tm=256, name="mix_bwd")
    d_wout = _wgrad_full(m, dh2, tk=WGRAD_TOKENS, name="wgrad_out")
    d_wbp = _wgrad_full(p, dyp, tk=WGRAD_TOKENS, name="wgrad_branch_pool", split_lanes=wbp.shape[2])
    d_wba = _wgrad_full(o, dys, tk=WGRAD_TOKENS, name="wgrad_branch_attn", split_lanes=wba.shape[2])
    (g_wbp, g_wba, g_wout), token = exchange("mix", [d_wbp, d_wba, d_wout.reshape(N_DEV, D // N_DEV, D)])
    dxp, d_wgroup, d_scale = _pool_bwd(dp, proj, pool_w_group, pool_scale, name="pool_bwd")
    dq, dkt, dvt = _attn_bwd(proj, do, ltot, token, name="attn_bwd")
    dk, dv = (t.transpose(0, 2, 1).reshape(dq.shape) for t in (dkt, dvt))
    dproj = jnp.concatenate([dxp.astype(BF16), dq.astype(BF16), dk.astype(BF16), dv.astype(BF16), dgl], axis=1)
    dh1, df1, d_nm = _inproj_bwd(dproj, dh2, h1, nmg, w_in, tm=1024, name="inproj_bwd")
    d_wd1 = _wgrad_down(hid1, df1, tk=WGRAD_TOKENS, name="ffn1_wgrad_down")
    (g_wd1,), token_down = exchange("ffn1_down", [d_wd1.reshape(N_DEV, FF_SHARD_PAD, D)])
    d_win = _wgrad_in(un, dproj, tk=WGRAD_TOKENS, name="wgrad_in")
    (g_win,), token_in = exchange("w_in", [d_win])
    token = (token_down[(0,) * token_down.ndim] + token_in[(0,) * token_in.ndim]).reshape(1, 1)

    dx, d_n1, n1, dgu1 = _ffn_bwd(dh1, df1, x, n1g, gu1, wgu1, wd1, token, tm=256, name="ffn1_bwd")
    replicated, _ = exchange("replicated", [d_n1, d_nm, d_n2, d_nf, d_scale, d_wgroup, loss])
    d_wgu1 = _wgrad_gate_up(n1, dgu1, tk=WGRAD_TOKENS, name="ffn1_wgrad_gate_up")
    (g_wgu1,), token = exchange("ffn1_gate_up", [d_wgu1])

    sharded = (g_wgu1, g_wd1, g_win, g_wbp, g_wba, g_wout, g_wgu2, g_wd2)
    return dx, sharded, replicated, token


def _hidden_major(w):
    return jnp.swapaxes(w[0], 0, 1)


def _pad_gate_up(wt):
    d = wt.shape[1]
    wt = wt.astype(BF16).reshape(2, FF_SHARD, d)
    return jnp.pad(wt, ((0, 0), (0, FF_SHARD_PAD - FF_SHARD), (0, 0))).reshape(2 * FF_SHARD_PAD, d)


def _unpad_gate_up(gt):
    d = gt.shape[1]
    return gt.reshape(2, FF_SHARD_PAD, d)[:, :FF_SHARD].reshape(2 * FF_SHARD, d)


def _pad_down(w):
    return jnp.pad(w.astype(BF16), ((0, FF_SHARD_PAD - FF_SHARD), (0, 0)))


def kernel(x, ffn1_norm, ffn1_w_gate_up, ffn1_w_down, mix_norm, w_in, pool_w_group, pool_scale, w_branch_pool, w_branch_attn, w_out, ffn2_norm, ffn2_w_gate_up, ffn2_w_down, final_norm, loss_target, m_ffn1_norm, m_ffn1_w_gate_up, m_ffn1_w_down, m_mix_norm, m_w_in, m_pool_w_group, m_pool_scale, m_w_branch_pool, m_w_branch_attn, m_w_out, m_ffn2_norm, m_ffn2_w_gate_up, m_ffn2_w_down, m_final_norm, v_ffn1_norm, v_ffn1_w_gate_up, v_ffn1_w_down, v_mix_norm, v_w_in, v_pool_w_group, v_pool_scale, v_w_branch_pool, v_w_branch_attn, v_w_out, v_ffn2_norm, v_ffn2_w_gate_up, v_ffn2_w_down, v_final_norm):
    D = x.shape[-1]
    weights = dict(ffn1_norm=ffn1_norm, ffn1_w_gate_up=ffn1_w_gate_up, ffn1_w_down=ffn1_w_down, mix_norm=mix_norm,
                   w_in=w_in, pool_w_group=pool_w_group, pool_scale=pool_scale, w_branch_pool=w_branch_pool,
                   w_branch_attn=w_branch_attn, w_out=w_out, ffn2_norm=ffn2_norm, ffn2_w_gate_up=ffn2_w_gate_up,
                   ffn2_w_down=ffn2_w_down, final_norm=final_norm)
    first = dict(ffn1_norm=m_ffn1_norm, ffn1_w_gate_up=m_ffn1_w_gate_up, ffn1_w_down=m_ffn1_w_down,
                 mix_norm=m_mix_norm, w_in=m_w_in, pool_w_group=m_pool_w_group, pool_scale=m_pool_scale,
                 w_branch_pool=m_w_branch_pool, w_branch_attn=m_w_branch_attn, w_out=m_w_out,
                 ffn2_norm=m_ffn2_norm, ffn2_w_gate_up=m_ffn2_w_gate_up, ffn2_w_down=m_ffn2_w_down,
                 final_norm=m_final_norm)
    second = dict(ffn1_norm=v_ffn1_norm, ffn1_w_gate_up=v_ffn1_w_gate_up, ffn1_w_down=v_ffn1_w_down,
                  mix_norm=v_mix_norm, w_in=v_w_in, pool_w_group=v_pool_w_group, pool_scale=v_pool_scale,
                  w_branch_pool=v_w_branch_pool, w_branch_attn=v_w_branch_attn, w_out=v_w_out,
                  ffn2_norm=v_ffn2_norm, ffn2_w_gate_up=v_ffn2_w_gate_up, ffn2_w_down=v_ffn2_w_down,
                  final_norm=v_final_norm)
    order = list(weights)

    wgu1, wd1 = _all_gather([_pad_gate_up(_hidden_major(ffn1_w_gate_up)), _pad_down(ffn1_w_down[0])],
                            name="all_gather_ffn1", collective_id=0)
    win_g, = _all_gather([w_in[0].astype(BF16)], name="all_gather_w_in", collective_id=1)
    wbp_g, wba_g, wout_g = _all_gather(
        [w_branch_pool[0].astype(BF16), w_branch_attn[0].astype(BF16), w_out[0].astype(BF16)],
        name="all_gather_mix", collective_id=2)
    wgu2, wd2 = _all_gather([_pad_gate_up(_hidden_major(ffn2_w_gate_up)), _pad_down(ffn2_w_down[0])],
                            name="all_gather_ffn2", collective_id=3)
    wd1 = wd1.reshape(N_DEV * FF_SHARD_PAD, D)
    wd2 = wd2.reshape(N_DEV * FF_SHARD_PAD, D)
    wout_g = wout_g.reshape(D, D)

    cross_ids = {"ffn2_down": 4, "ffn2_gate_up": 5, "mix": 6, "ffn1_down": 7, "w_in": 9, "ffn1_gate_up": 10}
    small = ["ffn1_norm", "mix_norm", "ffn2_norm", "final_norm", "pool_scale", "pool_w_group"]

    def tile_rows(a):
        a = a.reshape(-1, 128)
        return jnp.pad(a, ((0, -a.shape[0] % 8), (0, 0)))

    def exchange(tag, group):
        if tag == "replicated":
            slab = jnp.concatenate([tile_rows(g) for g in group[:-1]] + [jnp.broadcast_to(group[-1], (8, 128))], axis=0)
            return _all_gather([slab], name="all_gather_replicated", collective_id=8), slab
        sums = [_chip_sums(g, name=f"chip_sums_{tag}_{i}") for i, g in enumerate(group)]
        landed = _cross_chips([s[0] for s in sums], name="cross_chips_" + tag, collective_id=cross_ids[tag])
        return [(s[1], l) for s, l in zip(sums, landed)], sums[-1][1]

    norms = (ffn1_norm, mix_norm, ffn2_norm, final_norm.reshape(1, D))
    dx, sharded, (slabs,), last = _local_step(
        x[0], loss_target[0], norms, pool_w_group[0], pool_scale, wgu1, wd1, win_g, wbp_g, wba_g, wout_g, wgu2, wd2,
        exchange)
    names = ["ffn1_w_gate_up", "ffn1_w_down", "w_in", "w_branch_pool", "w_branch_attn", "w_out",
             "ffn2_w_gate_up", "ffn2_w_down"]
    grads = {k: _owner_sum(own, landed, last, name="owner_sum_" + k)
             for k, (own, landed) in reversed(list(zip(names, sharded)))}
    for k in ("ffn1_w_gate_up", "ffn2_w_gate_up"):
        grads[k] = _unpad_gate_up(grads[k])
    for k in ("ffn1_w_down", "ffn2_w_down"):
        grads[k] = grads[k][:FF_SHARD]

    rows = [weights[k].size // 128 for k in small]
    padded_rows = [-(-r // 8) * 8 for r in rows]
    starts = [sum(padded_rows[:i]) for i in range(len(rows) + 1)]
    total = _sum_devices(slabs, name="sum_replicated")
    loss_out = total[starts[-1], 0]

    small_w = jnp.concatenate([tile_rows(weights[k]) for k in small], axis=0)
    small_m = jnp.concatenate([tile_rows(first[k]) for k in small], axis=0)
    small_v = jnp.concatenate([tile_rows(second[k]) for k in small], axis=0)
    small_out = _adamw(small_w, total[:starts[-1]], small_m, small_v, name="adamw_replicated")
    delta, new_m, new_v = {}, {}, {}
    for name_, start, n_rows in zip(small, starts, rows):
        shape = weights[name_].shape
        grads[name_] = total[start:start + n_rows].reshape(shape)
        delta[name_], new_m[name_], new_v[name_] = (a[start:start + n_rows].reshape(shape) for a in small_out)
    for name_ in order:
        if name_ in small:
            continue
        hidden_major = name_.endswith("w_gate_up")
        view = _hidden_major if hidden_major else (lambda a: a[0])
        back = (lambda a: jnp.swapaxes(a, 0, 1)[None]) if hidden_major else (lambda a: a[None])
        out = _adamw(view(weights[name_]), grads[name_], view(first[name_]), view(second[name_]),
                     name="adamw_" + name_)
        delta[name_], new_m[name_], new_v[name_] = (back(a) for a in out)
        grads[name_] = back(grads[name_])

    return (loss_out, dx[None], *[grads[k] for k in order], *[delta[k] for k in order],
            *[new_m[k] for k in order], *[new_v[k] for k in order])
```

```python
import functools

import jax
import jax.numpy as jnp
from jax import lax
from jax.experimental import pallas as pl
from jax.experimental.pallas import tpu as pltpu
from jax.experimental.pallas import tpu_sc as plsc

F32 = jnp.float32
BF16 = jnp.bfloat16
MESH = pl.DeviceIdType.MESH

RMS_EPS = 1e-6
N_DEV = 8
N_HEADS = 8
HEAD_DIM = 64
HEAD_PAIR = 2 * HEAD_DIM
POOL_WINDOWS = (2, 4, 8, 16)
POOL_GROUP = 128
POOL_WIDTH = 512
SB_WIDTH = 512
FF_SHARD = 352
FF_SHARD_PAD = 384
ATTN_BLOCK = 256
ATTN_SCALE = 0.125

ADAM_LR = 0.001
ADAM_B1 = 0.9
ADAM_B2 = 0.999
ADAM_EPS = 1e-08
ADAM_WD = 0.01
ADAM_STEP = 10

VMEM_LIMIT = 48 << 20
WGRAD_TOKENS = 2048


def _params(dims=None):
    return pltpu.CompilerParams(dimension_semantics=dims, vmem_limit_bytes=VMEM_LIMIT)


def _mm(a, b):
    return jnp.dot(a, b, preferred_element_type=F32)


def _mm_nt(a, b):
    return lax.dot_general(a, b, (((1,), (1,)), ((), ())), preferred_element_type=F32)


def _mm_tn(a, b):
    return lax.dot_general(a, b, (((0,), (0,)), ((), ())), preferred_element_type=F32)


def _row_tile(rows, cols):
    limit = max(8, (512 * 1024) // cols)
    return max(t for t in range(8, rows + 1, 8) if rows % t == 0 and (t <= limit or t == 8))


def _rstd(xf):
    return lax.rsqrt(jnp.mean(xf * xf, axis=-1, keepdims=True) + RMS_EPS)


def _rms_bwd(xf, gain, dn):
    r = _rstd(xf)
    xh = xf * r
    dgain = jnp.sum(dn * xh, axis=0, keepdims=True)
    dxh = dn * gain
    dx = r * (dxh - xh * jnp.mean(dxh * xh, axis=-1, keepdims=True))
    return dx, dgain


def _ffn_fwd(x, gain, wgu, wd, *, tm, name):
    T, D = x.shape
    tm = min(tm, T)
    nb, bw = wgu.shape[0] // 2, wgu.shape[1]

    def body(x_ref, gain_ref, wg_ref, wu_ref, wd_ref, h_ref, gu_ref, hid_ref, n_scr, acc):
        j = pl.program_id(1)

        @pl.when(j == 0)
        def _():
            xf = x_ref[...]
            n_scr[...] = (xf * _rstd(xf) * gain_ref[...]).astype(BF16)
            acc[...] = jnp.zeros_like(acc)

        n = n_scr[...]
        g = _mm_nt(n, wg_ref[...])
        u = _mm_nt(n, wu_ref[...])
        gu_ref[0] = g.astype(BF16)
        gu_ref[1] = u.astype(BF16)
        hid = (g * jax.nn.sigmoid(g) * u).astype(BF16)
        hid_ref[...] = hid
        acc[...] += _mm(hid, wd_ref[...])

        @pl.when(j == nb - 1)
        def _():
            h_ref[...] = x_ref[...] + 0.5 * acc[...]

    return pl.pallas_call(
        body, name=name, grid=(T // tm, nb),
        in_specs=[
            pl.BlockSpec((tm, D), lambda i, j: (i, 0)),
            pl.BlockSpec((1, D), lambda i, j: (0, 0)),
            pl.BlockSpec((None, bw, D), lambda i, j: (j, 0, 0)),
            pl.BlockSpec((None, bw, D), lambda i, j: (j + nb, 0, 0)),
            pl.BlockSpec((bw, D), lambda i, j: (j, 0)),
        ],
        out_specs=[
            pl.BlockSpec((tm, D), lambda i, j: (i, 0)),
            pl.BlockSpec((2, tm, bw), lambda i, j: (0, i, j)),
            pl.BlockSpec((tm, bw), lambda i, j: (i, j)),
        ],
        out_shape=[jax.ShapeDtypeStruct((T, D), F32), jax.ShapeDtypeStruct((2, T, nb * bw), BF16),
                   jax.ShapeDtypeStruct((T, nb * bw), BF16)],
        scratch_shapes=[pltpu.VMEM((tm, D), BF16), pltpu.VMEM((tm, D), F32)],
        compiler_params=_params(("arbitrary", "arbitrary")),
    )(x, gain, wgu, wgu, wd)


AFTER = pl.BlockSpec(memory_space=pltpu.HBM)


def _in_hbm(token):
    return pltpu.with_memory_space_constraint(token, pltpu.HBM)


def _ffn_bwd(dh, df, x, gain, gu, wgu, wd, after, *, tm, name):
    T, D = x.shape
    tm = min(tm, T)
    nb, bw = wgu.shape[0] // 2, wgu.shape[1]

    def body(dh_ref, df_ref, x_ref, gain_ref, gu_ref, wg_ref, wu_ref, wd_ref, after_ref,
             dx_ref, dgain_ref, n_ref, dgu_ref, dn_acc):
        i, j = pl.program_id(0), pl.program_id(1)

        @pl.when(j == 0)
        def _():
            xf = x_ref[...]
            n_ref[...] = (xf * _rstd(xf) * gain_ref[...]).astype(BF16)
            dn_acc[...] = jnp.zeros_like(dn_acc)

        @pl.when((i == 0) & (j == 0))
        def _():
            dgain_ref[...] = jnp.zeros_like(dgain_ref)

        dhid = _mm_nt(df_ref[...], wd_ref[...])
        g = gu_ref[0].astype(F32)
        u = gu_ref[1].astype(F32)
        s = jax.nn.sigmoid(g)
        silu = g * s
        dg =(dhid * u * (s * (1.0 + g * (1.0 - s)))).astype(BF16)
        du = (dhid * silu).astype(BF16)
        dgu_ref[0] = dg
        dgu_ref[1] = du
        dn_acc[...] += _mm(dg, wg_ref[...]) + _mm(du, wu_ref[...])

        @pl.when(j == nb - 1)
        def _():
            dx, dgain = _rms_bwd(x_ref[...], gain_ref[...], dn_acc[...])
            dx_ref[...] = dh_ref[...] + dx
            dgain_ref[...] += dgain

    row = lambda i, j: (i, 0)
    return pl.pallas_call(
        body, name=name, grid=(T // tm, nb),
        in_specs=[
            pl.BlockSpec((tm, D), row),
            pl.BlockSpec((tm, D), row),
            pl.BlockSpec((tm, D), row),
            pl.BlockSpec((1, D), lambda i, j: (0, 0)),
            pl.BlockSpec((2, tm, bw), lambda i, j: (0, i, j)),
            pl.BlockSpec((None, bw, D), lambda i, j: (j, 0, 0)),
            pl.BlockSpec((None, bw, D), lambda i, j: (j + nb, 0, 0)),
            pl.BlockSpec((bw, D), lambda i, j: (j, 0)),
            AFTER,
        ],
        out_specs=[
            pl.BlockSpec((tm, D), row),
            pl.BlockSpec((1, D), lambda i, j: (0, 0)),
            pl.BlockSpec((tm, D), row),
            pl.BlockSpec((2, tm, bw), lambda i, j: (0, i, j)),
        ],
        out_shape=[
            jax.ShapeDtypeStruct((T, D), F32),
            jax.ShapeDtypeStruct((1, D), F32),
            jax.ShapeDtypeStruct((T, D), BF16),
            jax.ShapeDtypeStruct((2, T, nb * bw), BF16),
        ],
        scratch_shapes=[pltpu.VMEM((tm, D), F32)],
        compiler_params=_params(("arbitrary", "arbitrary")),
    )(dh, df, x, gain, gu, wgu, wgu, wd, _in_hbm(after))


def _wgrad(a, b, *, grid, a_spec, b_spec, out_spec, out_shape, acc_shape, name, split_lanes=0):
    nk = grid[2]

    def body(a_ref, b_ref, o_ref, acc):
        k = pl.program_id(2)

        @pl.when(k == 0)
        def _():
            acc[...] = jnp.zeros_like(acc)

        acc[...] += _mm_tn(a_ref[...].astype(BF16), b_ref[...].astype(BF16))

        @pl.when(k == nk - 1)
        def _():
            if split_lanes:
                for e in range(o_ref.shape[0]):
                    o_ref[e] = acc[:, e * split_lanes:(e + 1) * split_lanes].astype(o_ref.dtype)
            else:
                o_ref[...] = acc[...].astype(o_ref.dtype)

    return pl.pallas_call(
        body, name=name, grid=grid, in_specs=[a_spec, b_spec], out_specs=out_spec,
        out_shape=jax.ShapeDtypeStruct(out_shape, BF16),
        scratch_shapes=[pltpu.VMEM(acc_shape, F32)],
        compiler_params=_params(("arbitrary", "arbitrary", "arbitrary")),
    )(a, b)


def _wgrad_gate_up(n, dgu, *, tk, name):
    T, D = n.shape
    tk = min(tk, T)
    bw = FF_SHARD_PAD * 2
    nb = dgu.shape[2] // bw
    return _wgrad(
        dgu, n, grid=(2 * nb, 1, T // tk), name=name,
        a_spec=pl.BlockSpec((None, tk, bw), lambda m, c, k: (m // nb, k, m % nb)),
        b_spec=pl.BlockSpec((tk, D), lambda m, c, k: (k, 0)),
        out_spec=pl.BlockSpec((None, bw, D), lambda m, c, k: (m, 0, 0)),
        out_shape=(2 * nb, bw, D), acc_shape=(bw, D))


def _wgrad_down(hid, df, *, tk, name):
    T, D = df.shape
    tk = min(tk, T)
    bw = FF_SHARD_PAD * 2
    nb = hid.shape[1] // bw
    return _wgrad(
        hid, df, grid=(nb, 1, T // tk), name=name,
        a_spec=pl.BlockSpec((tk, bw), lambda m, c, k: (k, m)),
        b_spec=pl.BlockSpec((tk, D), lambda m, c, k: (k, 0)),
        out_spec=pl.BlockSpec((bw, D), lambda m, c, k: (m, 0)),
        out_shape=(nb * bw, D), acc_shape=(bw, D))


def _wgrad_in(un, dproj, *, tk, name):
    T, D = un.shape
    tk = min(tk, T)
    bw = dproj.shape[1] // N_DEV
    return _wgrad(
        un, dproj, grid=(1, N_DEV, T // tk), name=name,
        a_spec=pl.BlockSpec((tk, D), lambda m, c, k: (k, 0)),
        b_spec=pl.BlockSpec((tk, bw), lambda m, c, k: (k, c)),
        out_spec=pl.BlockSpec((None, D, bw), lambda m, c, k: (c, 0, 0)),
        out_shape=(N_DEV, D, bw), acc_shape=(D, bw))


def _wgrad_full(a, b, *, tk, name, split_lanes=0):
    T, M = a.shape
    tk = min(tk, T)
    N = b.shape[1]
    if split_lanes:
        out_shape = (N // split_lanes, M, split_lanes)
        out_spec = pl.BlockSpec(out_shape, lambda m, c, k: (0, 0, 0))
    else:
        out_shape = (M, N)
        out_spec = pl.BlockSpec(out_shape, lambda m, c, k: (0, 0))
    return _wgrad(
        a, b, grid=(1, 1, T // tk), name=name,
        a_spec=pl.BlockSpec((tk, M), lambda m, c, k: (k, 0)),
        b_spec=pl.BlockSpec((tk, N), lambda m, c, k: (k, 0)),
        out_spec=out_spec, out_shape=out_shape, acc_shape=(M, N), split_lanes=split_lanes)


def _loss_bwd(h, target, gain, *, tm, name):
    T, D = h.shape
    tm = min(tm, T)

    def body(h_ref, t_ref, gain_ref, dh_ref, df_ref, loss_ref, dgain_ref):
        @pl.when(pl.program_id(0) == 0)
        def _():
            loss_ref[...] = jnp.zeros_like(loss_ref)
            dgain_ref[...] = jnp.zeros_like(dgain_ref)

        xf = h_ref[...]
        gain = gain_ref[...]
        err = xf * _rstd(xf) * gain - t_ref[...]
        loss_ref[...] += 0.5 * jnp.sum(jnp.mean(err * err, axis=-1, keepdims=True), axis=0, keepdims=True)
        dx, dgain = _rms_bwd(xf, gain, err * (1.0 / D))
        dh_ref[...] = dx
        df_ref[...] = (0.5 * dx).astype(BF16)
        dgain_ref[...] += dgain

    row = lambda i: (i, 0)
    fixed = lambda i: (0, 0)
    return pl.pallas_call(
        body, name=name, grid=(T // tm,),
        in_specs=[pl.BlockSpec((tm, D), row), pl.BlockSpec((tm, D), row), pl.BlockSpec((1, D), fixed)],
        out_specs=[pl.BlockSpec((tm, D), row), pl.BlockSpec((tm, D), row), pl.BlockSpec((1, 128), fixed),
                   pl.BlockSpec((1, D), fixed)],
        out_shape=[jax.ShapeDtypeStruct((T, D), F32), jax.ShapeDtypeStruct((T, D), BF16),
                   jax.ShapeDtypeStruct((1, 128), F32), jax.ShapeDtypeStruct((1, D), F32)],
        compiler_params=_params(("arbitrary",)),
    )(h, target, gain)


def _inproj_fwd(h, gain, w_in, *, tm, name):
    T, D = h.shape
    tm = min(tm, T)
    nb, bw = w_in.shape[0], w_in.shape[2]

    def body(h_ref, gain_ref, w_ref, un_ref, proj_ref):
        @pl.when(pl.program_id(1) == 0)
        def _():
            xf = h_ref[...]
            un_ref[...] = (xf * _rstd(xf) * gain_ref[...]).astype(BF16)

        proj_ref[...] = _mm(un_ref[...], w_ref[...])

    return pl.pallas_call(
        body, name=name, grid=(T // tm, nb),
        in_specs=[
            pl.BlockSpec((tm, D), lambda i, j: (i, 0)),
            pl.BlockSpec((1, D), lambda i, j: (0, 0)),
            pl.BlockSpec((None, D, bw), lambda i, j: (j, 0, 0)),
        ],
        out_specs=[pl.BlockSpec((tm, D), lambda i, j: (i, 0)), pl.BlockSpec((tm, bw), lambda i, j: (i, j))],
        out_shape=[jax.ShapeDtypeStruct((T, D), BF16), jax.ShapeDtypeStruct((T, nb * bw), F32)],
        compiler_params=_params(("arbitrary", "arbitrary")),
    )(h, gain, w_in)


def _inproj_bwd(dproj, dh, h, gain, w_in, *, tm, name):
    T, D = h.shape
    tm = min(tm, T)
    nb, bw = w_in.shape[0], w_in.shape[2]

    def body(dp_ref, dh_ref, h_ref, gain_ref, w_ref, dx_ref, df_ref, dgain_ref, acc):
        i, j = pl.program_id(0), pl.program_id(1)

        @pl.when(j == 0)
        def _():
            acc[...] = jnp.zeros_like(acc)

        @pl.when((i == 0) & (j == 0))
        def _():
            dgain_ref[...] = jnp.zeros_like(dgain_ref)

        acc[...] += _mm_nt(dp_ref[...], w_ref[...])

        @pl.when(j == nb - 1)
        def _():
            dx, dgain = _rms_bwd(h_ref[...], gain_ref[...], acc[...])
            dh_in = dh_ref[...] + dx
            dx_ref[...] = dh_in
            df_ref[...] = (0.5 * dh_in).astype(BF16)
            dgain_ref[...] += dgain

    row = lambda i, j: (i, 0)
    return pl.pallas_call(
        body, name=name, grid=(T // tm, nb),
        in_specs=[
            pl.BlockSpec((tm, bw), lambda i, j: (i, j)),
            pl.BlockSpec((tm, D), row),
            pl.BlockSpec((tm, D), row),
            pl.BlockSpec((1, D), lambda i, j: (0, 0)),
            pl.BlockSpec((None, D, bw), lambda i, j: (j, 0, 0)),
        ],
        out_specs=[pl.BlockSpec((tm, D), row), pl.BlockSpec((tm, D), row), pl.BlockSpec((1, D), lambda i, j: (0, 0))],
        out_shape=[jax.ShapeDtypeStruct((T, D), F32), jax.ShapeDtypeStruct((T, D), BF16),
                   jax.ShapeDtypeStruct((1, D), F32)],
        scratch_shapes=[pltpu.VMEM((tm, D), F32)],
        compiler_params=_params(("arbitrary", "arbitrary")),
    )(dproj, dh, h, gain, w_in)


def _window_sum(x, row, doublings, *, backward):
    T = x.shape[0]
    s = x
    for k in range(doublings):
        sh = 1 << k
        if backward:
            s = s + jnp.where(row < T - sh, pltpu.roll(s, T - sh, 0), 0.0)
        else:
            s = s + jnp.where(row >= sh, pltpu.roll(s, sh, 0), 0.0)
    return s


def _pool_fwd(proj, w_group, scale, *, name):
    T = proj.shape[0]

    def body(xp_ref, w_ref, scale_ref, p_ref):
        row = lax.broadcasted_iota(jnp.int32, (T, POOL_GROUP), 0)
        for gi, window in enumerate(POOL_WINDOWS):
            cols = slice(gi * POOL_GROUP, (gi + 1) * POOL_GROUP)
            x = xp_ref[:, cols]
            inv_count = 1.0 / jnp.minimum(row + 1, window).astype(F32)
            yc = _window_sum(x, row, gi + 1, backward=False) * inv_count - x
            pre = _mm(yc.astype(BF16), w_ref[gi].astype(BF16))
            p_ref[:, cols] = pre * scale_ref[:, cols]

    return pl.pallas_call(
        body, name=name, grid=(1,),
        in_specs=[
            pl.BlockSpec((T, POOL_WIDTH), lambda i: (0, 0)),
            pl.BlockSpec(w_group.shape, lambda i: (0, 0, 0)),
            pl.BlockSpec((1, POOL_WIDTH), lambda i: (0, 0)),
        ],
        out_specs=pl.BlockSpec((T, POOL_WIDTH), lambda i: (0, 0)),
        out_shape=jax.ShapeDtypeStruct((T, POOL_WIDTH), F32),
        compiler_params=_params(("arbitrary",)),
    )(proj, w_group, scale)


def _pool_bwd(dp, proj, w_group, scale, *, name):
    T = proj.shape[0]

    def body(dp_ref, xp_ref, w_ref, scale_ref, dxp_ref, dw_ref, dscale_ref):
        row = lax.broadcasted_iota(jnp.int32, (T, POOL_GROUP), 0)
        for gi, window in enumerate(POOL_WINDOWS):
            cols = slice(gi * POOL_GROUP, (gi + 1) * POOL_GROUP)
            x = xp_ref[:, cols]
            inv_count = 1.0 / jnp.minimum(row + 1, window).astype(F32)
            yc = (_window_sum(x, row, gi + 1, backward=False) * inv_count - x).astype(BF16)
            w = w_ref[gi].astype(BF16)
            pre = _mm(yc, w)
            dpg = dp_ref[:, cols]
            dscale_ref[:, cols] = jnp.sum(dpg * pre, axis=0, keepdims=True)
            dpre = (dpg * scale_ref[:, cols]).astype(BF16)
            dw_ref[gi] = _mm_tn(yc, dpre)
            dyc = _mm_nt(dpre, w)
            dxp_ref[:, cols] = _window_sum(dyc * inv_count, row, gi + 1, backward=True) - dyc

    return pl.pallas_call(
        body, name=name, grid=(1,),
        in_specs=[
            pl.BlockSpec((T, POOL_WIDTH), lambda i: (0, 0)),
            pl.BlockSpec((T, POOL_WIDTH), lambda i: (0, 0)),
            pl.BlockSpec(w_group.shape, lambda i: (0, 0, 0)),
            pl.BlockSpec((1, POOL_WIDTH), lambda i: (0, 0)),
        ],
        out_specs=[
            pl.BlockSpec((T, POOL_WIDTH), lambda i: (0, 0)),
            pl.BlockSpec(w_group.shape, lambda i: (0, 0, 0)),
            pl.BlockSpec((1, POOL_WIDTH), lambda i: (0, 0)),
        ],
        out_shape=[jax.ShapeDtypeStruct((T, POOL_WIDTH), F32), jax.ShapeDtypeStruct(w_group.shape, F32),
                   jax.ShapeDtypeStruct((1, POOL_WIDTH), F32)],
        compiler_params=_params(("arbitrary",)),
    )(dp, proj, w_group, scale)


ATTN_STRIP = 32


def _log_sigmoids(z):
    lb = jnp.minimum(z, 0.0) - jnp.log(1.0 + jnp.exp(-jnp.abs(z)))
    return lb, lb - z


def _transposed_blocks(x_ref, blocks_scr, tq):
    for b in range(blocks_scr.shape[0]):
        blocks_scr[b] = x_ref[b * tq:(b + 1) * tq, :].T.astype(BF16)


def _split_bf16(x):
    hi = x.astype(BF16)
    return hi, (x - hi.astype(F32)).astype(BF16)


def _strips(n):
    return [slice(i, i + ATTN_STRIP) for i in range(0, n, ATTN_STRIP)]


def _rows(parts):
    return jnp.concatenate(parts, axis=0)


def _attn_specs(T, tq):
    q_col = POOL_WIDTH // HEAD_PAIR
    k_col = q_col + SB_WIDTH // HEAD_PAIR
    v_col = k_col + SB_WIDTH // HEAD_PAIR
    return [
        pl.BlockSpec((tq, HEAD_PAIR), lambda p, i: (i, q_col + p)),
        pl.BlockSpec((T, HEAD_PAIR), lambda p, i: (0, k_col + p)),
        pl.BlockSpec((T, HEAD_PAIR), lambda p, i: (0, v_col + p)),
    ]


def _attn_fwd(proj, *, name):
    T = proj.shape[0]
    tq = ATTN_BLOCK

    def body(q_ref, k_ref, v_ref, o_ref, lt_ref, kt_scr, vb_scr):
        qi = pl.program_id(1)

        @pl.when(qi == 0)
        def _():
            _transposed_blocks(k_ref, kt_scr, tq)
            vb_scr[...] = v_ref[...].astype(BF16)

        head0 = lax.broadcasted_iota(jnp.int32, (tq, HEAD_PAIR), 1) < HEAD_DIM
        q = q_ref[...] * ATTN_SCALE
        qs = (jnp.where(head0, q, 0.0).astype(BF16), jnp.where(head0, 0.0, q).astype(BF16))
        r = lax.broadcasted_iota(jnp.int32, (tq, tq), 0)
        c = lax.broadcasted_iota(jnp.int32, (tq, tq), 1)
        later = (r > c).astype(BF16)
        later2 = _rows([later, later])
        causal = lambda rows: c[rows] < r[rows]
        strips = _strips(tq)

        def log_terms(z, valid):
            lbs, his, los, sums = [], [], [], []
            for rows in strips:
                lb, lm = _log_sigmoids(z[rows])
                if valid is not None:
                    lm = jnp.where(valid(rows), lm, 0.0)
                hi, lo = _split_bf16(lm)
                lbs.append(lb)
                his.append(hi)
                los.append(lo)
                sums.append(jnp.sum(lm, axis=1, keepdims=True))
            return lbs, jnp.concatenate([_rows(his), _rows(los)], axis=1), _rows(sums)

        def weights(lbs, run, after, valid):
            parts = []
            for rows, lb in zip(strips, lbs):
                a = jnp.exp(lb + run[rows] + after[rows])
                if valid is not None:
                    a = jnp.where(valid(rows), a, 0.0)
                parts.append(a.astype(BF16))
            return _rows(parts)

        def block(kj, carry, valid):
            kt = kt_scr[kj]
            vb = vb_scr[pl.ds(pl.multiple_of(kj * tq, tq), tq), :]
            run0, o0, run1, o1 = carry
            z0 = _mm(qs[0], kt)
            z1 = _mm(qs[1], kt)
            lbs0, split0, sums0 = log_terms(z0, valid)
            after0 = _mm(split0, later2)
            lbs1, split1, sums1 = log_terms(z1, valid)
            after1 = _mm(split1, later2)
            o0 = o0 + _mm(weights(lbs0, run0, after0, valid), vb)
            o1 = o1 + _mm(weights(lbs1, run1, after1, valid), vb)
            return run0 + sums0, o0, run1 + sums1, o1

        zero = (jnp.zeros((tq, 1), F32), jnp.zeros((tq, HEAD_PAIR), F32))
        carry = block(qi, zero + zero, causal)
        carry = lax.fori_loop(0, qi, lambda it, cr: block(qi - 1 - it, cr, None), carry)
        o_ref[...] = jnp.where(head0, carry[1], carry[3])
        lt_ref[...] = jnp.where(head0, carry[0], carry[2])

    out_spec = pl.BlockSpec((tq, HEAD_PAIR), lambda p, i: (i, p))
    return pl.pallas_call(
        body, name=name, grid=(N_HEADS // 2, T // tq),
        in_specs=_attn_specs(T, tq), out_specs=[out_spec, out_spec],
        out_shape=[jax.ShapeDtypeStruct((T, SB_WIDTH), F32), jax.ShapeDtypeStruct((T, SB_WIDTH), F32)],
        scratch_shapes=[pltpu.VMEM((T // tq, HEAD_PAIR, tq), BF16), pltpu.VMEM((T, HEAD_PAIR), BF16)],
        compiler_params=_params(("arbitrary", "arbitrary")),
    )(proj, proj, proj)


def _attn_bwd(proj, do, ltot, after, *, name):
    T = proj.shape[0]
    tq = ATTN_BLOCK

    def body(q_ref, k_ref, v_ref, do_ref, lt_ref, after_ref, dq_ref, dkt_ref, dvt_ref, kb_scr, kt_scr, vt_scr):
        qi = pl.program_id(1)

        @pl.when(qi == 0)
        def _():
            kb_scr[...] = k_ref[...].astype(BF16)
            _transposed_blocks(k_ref, kt_scr, tq)
            _transposed_blocks(v_ref, vt_scr, tq)
            dkt_ref[...] = jnp.zeros_like(dkt_ref)
            dvt_ref[...] = jnp.zeros_like(dvt_ref)

        head0 = lax.broadcasted_iota(jnp.int32, (tq, HEAD_PAIR), 1) < HEAD_DIM
        q, do_, lt = q_ref[...] * ATTN_SCALE, do_ref[...], lt_ref[...]
        qs = (jnp.where(head0, q, 0.0).astype(BF16), jnp.where(head0, 0.0, q).astype(BF16))
        q_heads = (jnp.where(head0, q, 0.0), jnp.where(head0, 0.0, q))
        do_heads = (jnp.where(head0, do_, 0.0), jnp.where(head0, 0.0, do_))
        dos = tuple(d.astype(BF16) for d in do_heads)
        qts = tuple(x.T.astype(BF16) for x in q_heads)
        dots = tuple(d.T.astype(BF16) for d in do_heads)
        lts = (jnp.max(jnp.where(head0, lt, -jnp.inf), axis=1, keepdims=True),
               jnp.max(jnp.where(head0, -jnp.inf, lt), axis=1, keepdims=True))
        r = lax.broadcasted_iota(jnp.int32, (tq, tq), 0)
        c = lax.broadcasted_iota(jnp.int32, (tq, tq), 1)
        upto = (r <= c).astype(BF16)
        before = (r < c).astype(BF16)
        upto2, before2 = _rows([upto, upto]), _rows([before, before])
        causal = lambda rows: c[rows] < r[rows]
        strips = _strips(tq)

        def log_terms(z, valid):
            lbs, his, los, sums = [], [], [], []
            for rows in strips:
                lb, lm = _log_sigmoids(z[rows])
                if valid is not None:
                    lm = jnp.where(valid(rows), lm, 0.0)
                hi, lo = _split_bf16(lm)
                lbs.append(lb)
                his.append(hi)
                los.append(lo)
                sums.append(jnp.sum(lm, axis=1, keepdims=True))
            return lbs, jnp.concatenate([_rows(his), _rows(los)], axis=1), _rows(sums)

        def weights(lbs, rest, lm_upto, da, valid):
            a_parts, es, his, los, sums = [], [], [], [], []
            for rows, lb in zip(strips, lbs):
                a = jnp.exp(lb + (rest[rows] - lm_upto[rows]))
                if valid is not None:
                    a = jnp.where(valid(rows), a, 0.0)
                e = da[rows] * a
                hi, lo = _split_bf16(e)
                a_parts.append(a.astype(BF16))
                es.append(e)
                his.append(hi)
                los.append(lo)
                sums.append(jnp.sum(e, axis=1, keepdims=True))
            return _rows(a_parts), es, jnp.concatenate([_rows(his), _rows(los)], axis=1), _rows(sums)

        def score_grads(lbs, es, run_e, e_before, valid):
            parts = []
            for rows, lb, e in zip(strips, lbs, es):
                beta = jnp.exp(lb)
                dz = e * (1.0 - beta) - (run_e[rows] + e_before[rows]) * beta
                if valid is not None:
                    dz = jnp.where(valid(rows), dz, 0.0)
                parts.append(dz.astype(BF16))
            return _rows(parts)

        def block(kj, carry, valid):
            off = pl.multiple_of(kj * tq, tq)
            kb, kt, vt = kb_scr[pl.ds(off, tq), :], kt_scr[kj], vt_scr[kj]
            run_lm0, run_e0, dq0, run_lm1, run_e1, dq1 = carry
            z0, da0 = _mm(qs[0], kt), _mm(dos[0], vt)
            z1, da1 = _mm(qs[1], kt), _mm(dos[1], vt)
            lbs0, split0, lm_sums0 = log_terms(z0, valid)
            lm_upto0 = _mm(split0, upto2)
            lbs1, split1, lm_sums1 = log_terms(z1, valid)
            lm_upto1 = _mm(split1, upto2)
            a0, es0, split0, e_sums0 = weights(lbs0, lts[0] - run_lm0, lm_upto0, da0, valid)
            e_before0 = _mm(split0, before2)
            a1, es1, split1, e_sums1 = weights(lbs1, lts[1] - run_lm1, lm_upto1, da1, valid)
            e_before1 = _mm(split1, before2)
            dz0 = score_grads(lbs0, es0, run_e0, e_before0, valid)
            dkt_blk = _mm(qts[0], dz0)
            dvt_blk = _mm(dots[0], a0)
            dq0 = dq0 + _mm(dz0, kb)
            dz1 = score_grads(lbs1, es1, run_e1, e_before1, valid)
            dkt_ref[kj] += dkt_blk + _mm(qts[1], dz1)
            dvt_ref[kj] += dvt_blk + _mm(dots[1], a1)
            dq1 = dq1 + _mm(dz1, kb)
            return run_lm0 + lm_sums0, run_e0 + e_sums0, dq0, run_lm1 + lm_sums1, run_e1 + e_sums1, dq1

        zero = (jnp.zeros((tq, 1), F32), jnp.zeros((tq, 1), F32), jnp.zeros((tq, HEAD_PAIR), F32))
        carry = lax.fori_loop(0, qi, lambda kj, cr: block(kj, cr, None), zero + zero)
        carry = block(qi, carry, causal)
        dq_ref[...] = jnp.where(head0, carry[2], carry[5]) * ATTN_SCALE

    blk = pl.BlockSpec((tq, HEAD_PAIR), lambda p, i: (i, p))
    seq = pl.BlockSpec((T // tq, HEAD_PAIR, tq), lambda p, i: (0, p, 0))
    transposed = jax.ShapeDtypeStruct((T // tq, SB_WIDTH, tq), F32)
    return pl.pallas_call(
        body, name=name, grid=(N_HEADS // 2, T // tq),
        in_specs=_attn_specs(T, tq) + [blk, blk, AFTER], out_specs=[blk, seq, seq],
        out_shape=[jax.ShapeDtypeStruct((T, SB_WIDTH), F32), transposed, transposed],
        scratch_shapes=[pltpu.VMEM((T, HEAD_PAIR), BF16), pltpu.VMEM((T // tq, HEAD_PAIR, tq), BF16),
                        pltpu.VMEM((T // tq, HEAD_PAIR, tq), BF16)],
        compiler_params=_params(("arbitrary", "arbitrary")),
    )(proj, proj, proj, do, ltot, _in_hbm(after))


def _branch(act_bf16, w_ref):
    return jnp.concatenate([_mm(act_bf16, w_ref[e]) for e in range(w_ref.shape[0])], axis=1)


def _mix_specs(T, D, tm, wbp, w_out):
    gate_col = (POOL_WIDTH + 3 * SB_WIDTH) // D
    row = lambda i: (i, 0)
    return [
        pl.BlockSpec((tm, D), row),
        pl.BlockSpec((tm, POOL_WIDTH), row),
        pl.BlockSpec((tm, SB_WIDTH), row),
        pl.BlockSpec((tm, D), lambda i: (i, gate_col)),
        pl.BlockSpec((tm, D), lambda i: (i, gate_col + 1)),
        pl.BlockSpec(wbp.shape, lambda i: (0, 0, 0)),
        pl.BlockSpec(wbp.shape, lambda i: (0, 0, 0)),
        pl.BlockSpec(w_out.shape, lambda i: (0, 0)),
    ]


def _mix_fwd(h, p, o, proj, wbp, wba, w_out, *, tm, name):
    T, D = h.shape
    tm = min(tm, T)

    def body(h_ref, p_ref, o_ref, glp_ref, gls_ref, wbp_ref, wba_ref, wout_ref, hout_ref, m_ref):
        yp = _branch(p_ref[...].astype(BF16), wbp_ref)
        ys = _branch(o_ref[...].astype(BF16), wba_ref)
        m = (jax.nn.sigmoid(glp_ref[...]) * yp + jax.nn.sigmoid(gls_ref[...]) * ys).astype(BF16)
        m_ref[...] = m
        hout_ref[...] = h_ref[...] + _mm(m, wout_ref[...])

    row = lambda i: (i, 0)
    return pl.pallas_call(
        body, name=name, grid=(T // tm,),
        in_specs=_mix_specs(T, D, tm, wbp, w_out),
        out_specs=[pl.BlockSpec((tm, D), row), pl.BlockSpec((tm, D), row)],
        out_shape=[jax.ShapeDtypeStruct((T, D), F32), jax.ShapeDtypeStruct((T, D), BF16)],
        compiler_params=_params(("arbitrary",)),
    )(h, p, o, proj, proj, wbp, wba, w_out)


def _mix_bwd(dh, p, o, proj, wbp, wba, w_out, after, *, tm, name):
    T, D = dh.shape
    tm = min(tm, T)
    bw = wbp.shape[2]

    def body(dh_ref, p_ref, o_ref, glp_ref, gls_ref, wbp_ref, wba_ref, wout_ref, after_ref,
             dyp_ref, dys_ref, dp_ref, do_ref, dgl_ref):
        dm = _mm_nt(dh_ref[...].astype(BF16), wout_ref[...])
        yp = _branch(p_ref[...].astype(BF16), wbp_ref)
        ys = _branch(o_ref[...].astype(BF16), wba_ref)
        gp = jax.nn.sigmoid(glp_ref[...])
        gs = jax.nn.sigmoid(gls_ref[...])
        dyp = (dm * gp).astype(BF16)
        dys = (dm * gs).astype(BF16)
        dyp_ref[...] = dyp
        dys_ref[...] = dys
        dgl_ref[:, :D] = (dm * yp * gp * (1.0 - gp)).astype(BF16)
        dgl_ref[:, D:] = (dm * ys * gs * (1.0 - gs)).astype(BF16)
        dp = jnp.zeros(dp_ref.shape, F32)
        do_ = jnp.zeros(do_ref.shape, F32)
        for e in range(wbp_ref.shape[0]):
            dp += _mm_nt(dyp[:, e * bw:(e + 1) * bw], wbp_ref[e])
            do_ += _mm_nt(dys[:, e * bw:(e + 1) * bw], wba_ref[e])
        dp_ref[...] = dp
        do_ref[...] = do_

    row = lambda i: (i, 0)
    return pl.pallas_call(
        body, name=name, grid=(T // tm,),
        in_specs=_mix_specs(T, D, tm, wbp, w_out) + [AFTER],
        out_specs=[pl.BlockSpec((tm, D), row), pl.BlockSpec((tm, D), row), pl.BlockSpec((tm, POOL_WIDTH), row),
                   pl.BlockSpec((tm, SB_WIDTH), row), pl.BlockSpec((tm, 2 * D), row)],
        out_shape=[jax.ShapeDtypeStruct((T, D), BF16), jax.ShapeDtypeStruct((T, D), BF16),
                   jax.ShapeDtypeStruct((T, POOL_WIDTH), F32), jax.ShapeDtypeStruct((T, SB_WIDTH), F32),
                   jax.ShapeDtypeStruct((T, 2 * D), BF16)],
        compiler_params=_params(("arbitrary",)),
    )(dh, p, o, proj, proj, wbp, wba, w_out, _in_hbm(after))


def _adamw(w, g, m, v, *, name):
    R, C = w.shape
    tr = _row_tile(R, C)

    def body(w_ref, g_ref, m_ref, v_ref, d_ref, nm_ref, nv_ref):
        g_ = g_ref[...]
        m_ = ADAM_B1 * m_ref[...] + (1.0 - ADAM_B1) * g_
        v_ = ADAM_B2 * v_ref[...] + (1.0 - ADAM_B2) * (g_ * g_)
        m_hat = m_ / (1.0 - ADAM_B1 ** ADAM_STEP)
        v_hat = v_ / (1.0 - ADAM_B2 ** ADAM_STEP)
        d_ref[...] = -ADAM_LR * (m_hat / (jnp.sqrt(v_hat) + ADAM_EPS) + ADAM_WD * w_ref[...])
        nm_ref[...] = m_
        nv_ref[...] = v_

    spec = pl.BlockSpec((tr, C), lambda i: (i, 0))
    return pl.pallas_call(
        body, name=name, grid=(R // tr,), in_specs=[spec] * 4, out_specs=[spec] * 3,
        out_shape=[jax.ShapeDtypeStruct((R, C), F32)] * 3,
        compiler_params=_params(("arbitrary",)),
    )(w, g, m, v)


def _position():
    return lax.axis_index("x"), lax.axis_index("y"), lax.axis_index("c")


def _all_gather(shards, *, name, collective_id):
    n = len(shards)

    def body(*refs):
        ins, outs = refs[:n], refs[n:2 * n]
        send_sems, recv_sems, local_sems = refs[2 * n:]
        x, y, c = _position()
        me, sibling = (x, y, c), (x, y, 1 - c)
        chips = [(1 - x, y), (x, 1 - y), (1 - x, 1 - y)]

        barrier = pltpu.get_barrier_semaphore()
        for peer in [sibling] + [(*chip, c) for chip in chips]:
            pl.semaphore_signal(barrier, inc=1, device_id=peer, device_id_type=MESH)
        pl.semaphore_wait(barrier, 4)

        def block(a, pos):
            return outs[a].at[4 * pos[0] + 2 * pos[1] + pos[2]]

        def copy(a, k, pos, to, src=None):
            return pltpu.make_async_remote_copy(
                src_ref=block(a, pos) if src is None else src, dst_ref=block(a, pos),
                send_sem=send_sems.at[7 * a + k], recv_sem=recv_sems.at[7 * a + k],
                device_id=to, device_id_type=MESH)

        started = []
        for a in range(n):
            mine = pltpu.make_async_copy(ins[a], block(a, me), local_sems.at[a])
            mine.start()
            started.append(mine)
        sends = []
        for a in range(n):
            sends += [copy(a, 1 + j, me, (*chip, c), src=ins[a]) for j, chip in enumerate(chips)]
            sends.append(copy(a, 0, me, sibling, src=ins[a]))
        for cp in sends:
            cp.start()
        for j, chip in enumerate(chips):
            for a in range(n):
                copy(a, 1 + j, (*chip, c), me).wait_recv()
                passed = copy(a, 4 + j, (*chip, c), sibling)
                passed.start()
                sends.append(passed)
        for a in range(n):
            copy(a, 0, sibling, me).wait_recv()
            for j, chip in enumerate(chips):
                copy(a, 4 + j, (*chip, 1 - c), me).wait_recv()
        for cp in sends:
            cp.wait_send()
        for cp in started:
            cp.wait()

    return pl.kernel(
        body, name=name,
        out_type=[jax.ShapeDtypeStruct((N_DEV,) + s.shape, s.dtype) for s in shards],
        mesh=plsc.ScalarSubcoreMesh(axis_name="sequencer", num_cores=1),
        scratch_types=[pltpu.SemaphoreType.DMA((7 * n,)), pltpu.SemaphoreType.DMA((7 * n,)),
                       pltpu.SemaphoreType.DMA((n,))],
        compiler_params=pltpu.CompilerParams(collective_id=collective_id),
    )(*shards)


def _chip_sums(grads, *, name):
    _, R, C = grads.shape
    rc = 128 if R % 128 == 0 else R

    def body(g_ref, partial, out_ref, mine, theirs, send_sems, recv_sems, local_sems):
        x, y, c = _position()
        my_chip = 2 * x + y

        def swap(s):
            return pltpu.make_async_remote_copy(
                src_ref=g_ref.at[2 * s + (1 - c)], dst_ref=theirs.at[s],
                send_sem=send_sems.at[s], recv_sem=recv_sems.at[s],
                device_id=(x, y, 1 - c), device_id_type=MESH)

        def load(s):
            return pltpu.make_async_copy(g_ref.at[2 * s + c], mine.at[s], local_sems.at[s])

        for s in range(4):
            swap(s).start()
            load(s).start()
        for s in range(4):
            load(s).wait()
            swap(s).wait_recv()

        def chip_sum(chip, rows):
            return mine[chip, rows, :].astype(F32) + theirs[chip, rows, :].astype(F32)

        for j in (1, 2, 3):
            @pl.loop(0, R // rc)
            def _(t):
                rows = pl.ds(pl.multiple_of(t * rc, rc), rc)
                partial[j - 1, rows, :] = chip_sum(my_chip ^ j, rows).astype(BF16)

        @pl.loop(0, R // rc)
        def _(t):
            rows = pl.ds(pl.multiple_of(t * rc, rc), rc)
            out_ref[rows, :] = chip_sum(my_chip, rows)

        for s in range(4):
            swap(s).wait_send()

    vmem = pl.BlockSpec(memory_space=pltpu.VMEM)
    return pl.pallas_call(
        body, name=name,
        in_specs=[pl.BlockSpec(memory_space=pl.ANY)], out_specs=[vmem, vmem],
        out_shape=[jax.ShapeDtypeStruct((3, R, C), BF16), jax.ShapeDtypeStruct((R, C), F32)],
        scratch_shapes=[
            pltpu.VMEM((4, R, C), BF16), pltpu.VMEM((4, R, C), BF16),
            pltpu.SemaphoreType.DMA((4,)), pltpu.SemaphoreType.DMA((4,)), pltpu.SemaphoreType.DMA((4,)),
        ],
        compiler_params=_params(),
    )(grads)


def _cross_chips(partials, *, name, collective_id):
    n = len(partials)

    def body(*refs):
        ins, outs = refs[:n], refs[n:2 * n]
        send_sems, recv_sems = refs[2 * n:]
        x, y, c = _position()
        my_chip = 2 * x + y
        peers = [((my_chip ^ j) // 2, (my_chip ^ j) % 2, c) for j in (1, 2, 3)]

        barrier = pltpu.get_barrier_semaphore()
        for peer in peers:
            pl.semaphore_signal(barrier, inc=1, device_id=peer, device_id_type=MESH)
        pl.semaphore_wait(barrier, 3)

        copies = [
            pltpu.make_async_remote_copy(
                src_ref=ins[a].at[j], dst_ref=outs[a].at[j],
                send_sem=send_sems.at[3 * a + j], recv_sem=recv_sems.at[3 * a + j],
                device_id=peers[j], device_id_type=MESH)
            for a in range(n) for j in range(3)]
        for cp in copies:
            cp.start()
        for cp in copies:
            cp.wait_recv()
        for cp in copies:
            cp.wait_send()

    return pl.kernel(
        body, name=name,
        out_type=[jax.ShapeDtypeStruct(p.shape, p.dtype) for p in partials],
        mesh=plsc.ScalarSubcoreMesh(axis_name="sequencer", num_cores=1),
        scratch_types=[pltpu.SemaphoreType.DMA((3 * n,)), pltpu.SemaphoreType.DMA((3 * n,))],
        compiler_params=pltpu.CompilerParams(collective_id=collective_id),
    )(*partials)


def _owner_sum(own, landed, after, *, name):
    R, C = own.shape
    tr = _row_tile(R, C)

    def body(own_ref, landed_ref, after_ref, out_ref):
        total = own_ref[...]
        for j in range(3):
            total = total + landed_ref[j].astype(F32)
        out_ref[...] = total

    return pl.pallas_call(
        body, name=name, grid=(R // tr,),
        in_specs=[pl.BlockSpec((tr, C), lambda i: (i, 0)), pl.BlockSpec((3, tr, C), lambda i: (0, i, 0)), AFTER],
        out_specs=pl.BlockSpec((tr, C), lambda i: (i, 0)),
        out_shape=jax.ShapeDtypeStruct((R, C), F32),
        compiler_params=_params(("arbitrary",)),
    )(own, landed, _in_hbm(after))


def _sum_devices(gathered, *, name):
    _, R, C = gathered.shape

    def body(in_ref, out_ref):
        total = in_ref[0]
        for d in range(1, N_DEV):
            total = total + in_ref[d]
        out_ref[...] = total

    return pl.pallas_call(
        body, name=name, grid=(1,),
        in_specs=[pl.BlockSpec((N_DEV, R, C), lambda i: (0, 0, 0))],
        out_specs=pl.BlockSpec((R, C), lambda i: (0, 0)),
        out_shape=jax.ShapeDtypeStruct((R, C), F32),
        compiler_params=_params(("arbitrary",)),
    )(gathered)


def _local_step(x, target, norms, pool_w_group, pool_scale, wgu1, wd1, w_in, wbp, wba, w_out, wgu2, wd2, exchange):
    n1g, nmg, n2g, nfg = norms
    D = x.shape[1]
    h1, gu1, hid1 = _ffn_fwd(x, n1g, wgu1, wd1, tm=512, name="ffn1_fwd")
    un, proj = _inproj_fwd(h1, nmg, w_in, tm=1024, name="inproj_fwd")
    p = _pool_fwd(proj, pool_w_group, pool_scale, name="pool_fwd")
    o, ltot = _attn_fwd(proj, name="attn_fwd")
    h2, m = _mix_fwd(h1, p, o, proj, wbp, wba, w_out, tm=256, name="mix_fwd")
    h3, gu2, hid2 = _ffn_fwd(h2, n2g, wgu2, wd2, tm=512, name="ffn2_fwd")
    dh3, df2, loss, d_nf = _loss_bwd(h3, target, nfg, tm=256, name="loss_bwd")

    d_wd2 = _wgrad_down(hid2, df2, tk=WGRAD_TOKENS, name="ffn2_wgrad_down")
    (g_wd2,), token = exchange("ffn2_down", [d_wd2.reshape(N_DEV, FF_SHARD_PAD, D)])
    dh2, d_n2, n2, dgu2 = _ffn_bwd(dh3, df2, h2, n2g, gu2, wgu2, wd2, token, tm=256, name="ffn2_bwd")
    d_wgu2 = _wgrad_gate_up(n2, dgu2, tk=WGRAD_TOKENS, name="ffn2_wgrad_gate_up")
    (g_wgu2,), token = exchange("ffn2_gate_up", [d_wgu2])

    dyp, dys, dp, do, dgl = _mix_bwd(dh2, p, o, proj, wbp, wba, w_out, token, tm=256, name="mix_bwd")
    d_wout = _wgrad_full(m, dh2, tk=WGRAD_TOKENS, name="wgrad_out")
    d_wbp = _wgrad_full(p, dyp, tk=WGRAD_TOKENS, name="wgrad_branch_pool", split_lanes=wbp.shape[2])
    d_wba = _wgrad_full(o, dys, tk=WGRAD_TOKENS, name="wgrad_branch_attn", split_lanes=wba.shape[2])
    (g_wbp, g_wba, g_wout), token = exchange("mix", [d_wbp, d_wba, d_wout.reshape(N_DEV, D // N_DEV, D)])
    dxp, d_wgroup, d_scale = _pool_bwd(dp, proj, pool_w_group, pool_scale, name="pool_bwd")
    dq, dkt, dvt = _attn_bwd(proj, do, ltot, token, name="attn_bwd")
    dk, dv = (t.transpose(0, 2, 1).reshape(dq.shape) for t in (dkt, dvt))
    dproj = jnp.concatenate([dxp.astype(BF16), dq.astype(BF16), dk.astype(BF16), dv.astype(BF16), dgl], axis=1)
    dh1, df1, d_nm = _inproj_bwd(dproj, dh2, h1, nmg, w_in, tm=1024, name="inproj_bwd")
    d_wd1 = _wgrad_down(hid1, df1, tk=WGRAD_TOKENS, name="ffn1_wgrad_down")
    (g_wd1,), token_down = exchange("ffn1_down", [d_wd1.reshape(N_DEV, FF_SHARD_PAD, D)])
    d_win = _wgrad_in(un, dproj, tk=WGRAD_TOKENS, name="wgrad_in")
    (g_win,), token_in = exchange("w_in", [d_win])
    token = (token_down[(0,) * token_down.ndim] + token_in[(0,) * token_in.ndim]).reshape(1, 1)

    dx, d_n1, n1, dgu1 = _ffn_bwd(dh1, df1, x, n1g, gu1, wgu1, wd1, token, tm=256, name="ffn1_bwd")
    replicated, _ = exchange("replicated", [d_n1, d_nm, d_n2, d_nf, d_scale, d_wgroup, loss])
    d_wgu1 = _wgrad_gate_up(n1, dgu1, tk=WGRAD_TOKENS, name="ffn1_wgrad_gate_up")
    (g_wgu1,), token = exchange("ffn1_gate_up", [d_wgu1])

    sharded = (g_wgu1, g_wd1, g_win, g_wbp, g_wba, g_wout, g_wgu2, g_wd2)
    return dx, sharded, replicated, token


def _hidden_major(w):
    return jnp.swapaxes(w[0], 0, 1)


def _pad_gate_up(wt):
    d = wt.shape[1]
    wt = wt.astype(BF16).reshape(2, FF_SHARD, d)
    return jnp.pad(wt, ((0, 0), (0, FF_SHARD_PAD - FF_SHARD), (0, 0))).reshape(2 * FF_SHARD_PAD, d)


def _unpad_gate_up(gt):
    d = gt.shape[1]
    return gt.reshape(2, FF_SHARD_PAD, d)[:, :FF_SHARD].reshape(2 * FF_SHARD, d)


def _pad_down(w):
    return jnp.pad(w.astype(BF16), ((0, FF_SHARD_PAD - FF_SHARD), (0, 0)))


def kernel(x, ffn1_norm, ffn1_w_gate_up, ffn1_w_down, mix_norm, w_in, pool_w_group, pool_scale, w_branch_pool, w_branch_attn, w_out, ffn2_norm, ffn2_w_gate_up, ffn2_w_down, final_norm, loss_target, m_ffn1_norm, m_ffn1_w_gate_up, m_ffn1_w_down, m_mix_norm, m_w_in, m_pool_w_group, m_pool_scale, m_w_branch_pool, m_w_branch_attn, m_w_out, m_ffn2_norm, m_ffn2_w_gate_up, m_ffn2_w_down, m_final_norm, v_ffn1_norm, v_ffn1_w_gate_up, v_ffn1_w_down, v_mix_norm, v_w_in, v_pool_w_group, v_pool_scale, v_w_branch_pool, v_w_branch_attn, v_w_out, v_ffn2_norm, v_ffn2_w_gate_up, v_ffn2_w_down, v_final_norm):
    D = x.shape[-1]
    weights = dict(ffn1_norm=ffn1_norm, ffn1_w_gate_up=ffn1_w_gate_up, ffn1_w_down=ffn1_w_down, mix_norm=mix_norm,
                   w_in=w_in, pool_w_group=pool_w_group, pool_scale=pool_scale, w_branch_pool=w_branch_pool,
                   w_branch_attn=w_branch_attn, w_out=w_out, ffn2_norm=ffn2_norm, ffn2_w_gate_up=ffn2_w_gate_up,
                   ffn2_w_down=ffn2_w_down, final_norm=final_norm)
    first = dict(ffn1_norm=m_ffn1_norm, ffn1_w_gate_up=m_ffn1_w_gate_up, ffn1_w_down=m_ffn1_w_down,
                 mix_norm=m_mix_norm, w_in=m_w_in, pool_w_group=m_pool_w_group, pool_scale=m_pool_scale,
                 w_branch_pool=m_w_branch_pool, w_branch_attn=m_w_branch_attn, w_out=m_w_out,
                 ffn2_norm=m_ffn2_norm, ffn2_w_gate_up=m_ffn2_w_gate_up, ffn2_w_down=m_ffn2_w_down,
                 final_norm=m_final_norm)
    second = dict(ffn1_norm=v_ffn1_norm, ffn1_w_gate_up=v_ffn1_w_gate_up, ffn1_w_down=v_ffn1_w_down,
                  mix_norm=v_mix_norm, w_in=v_w_in, pool_w_group=v_pool_w_group, pool_scale=v_pool_scale,
                  w_branch_pool=v_w_branch_pool, w_branch_attn=v_w_branch_attn, w_out=v_w_out,
                  ffn2_norm=v_ffn2_norm, ffn2_w_gate_up=v_ffn2_w_gate_up, ffn2_w_down=v_ffn2_w_down,
                  final_norm=v_final_norm)
    order = list(weights)

    wgu1, wd1 = _all_gather([_pad_gate_up(_hidden_major(ffn1_w_gate_up)), _pad_down(ffn1_w_down[0])],
                            name="all_gather_ffn1", collective_id=0)
    win_g, = _all_gather([w_in[0].astype(BF16)], name="all_gather_w_in", collective_id=1)
    wbp_g, wba_g, wout_g = _all_gather(
        [w_branch_pool[0].astype(BF16), w_branch_attn[0].astype(BF16), w_out[0].astype(BF16)],
        name="all_gather_mix", collective_id=2)
    wgu2, wd2 = _all_gather([_pad_gate_up(_hidden_major(ffn2_w_gate_up)), _pad_down(ffn2_w_down[0])],
                            name="all_gather_ffn2", collective_id=3)
    wd1 = wd1.reshape(N_DEV * FF_SHARD_PAD, D)
    wd2 = wd2.reshape(N_DEV * FF_SHARD_PAD, D)
    wout_g = wout_g.reshape(D, D)

    cross_ids = {"ffn2_down": 4, "ffn2_gate_up": 5, "mix": 6, "ffn1_down": 7, "w_in": 9, "ffn1_gate_up": 10}
    small = ["ffn1_norm", "mix_norm", "ffn2_norm", "final_norm", "pool_scale", "pool_w_group"]

    def tile_rows(a):
        a = a.reshape(-1, 128)
        return jnp.pad(a, ((0, -a.shape[0] % 8), (0, 0)))

    def exchange(tag, group):
        if tag == "replicated":
            slab = jnp.concatenate([tile_rows(g) for g in group[:-1]] + [jnp.broadcast_to(group[-1], (8, 128))], axis=0)
            return _all_gather([slab], name="all_gather_replicated", collective_id=8), slab
        sums = [_chip_sums(g, name=f"chip_sums_{tag}_{i}") for i, g in enumerate(group)]
        landed = _cross_chips([s[0] for s in sums], name="cross_chips_" + tag, collective_id=cross_ids[tag])
        token = sums[0][1] if len(sums) == 1 else sum(s[1][0, 0] for s in sums).reshape(1, 1)
        return [(s[1], l) for s, l in zip(sums, landed)], token

    norms = (ffn1_norm, mix_norm, ffn2_norm, final_norm.reshape(1, D))
    dx, sharded, (slabs,), last = _local_step(
        x[0], loss_target[0], norms, pool_w_group[0], pool_scale, wgu1, wd1, win_g, wbp_g, wba_g, wout_g, wgu2, wd2,
        exchange)
    names = ["ffn1_w_gate_up", "ffn1_w_down", "w_in", "w_branch_pool", "w_branch_attn", "w_out",
             "ffn2_w_gate_up", "ffn2_w_down"]
    grads = {k: _owner_sum(own, landed, last, name="owner_sum_" + k)
             for k, (own, landed) in reversed(list(zip(names, sharded)))}
    for k in ("ffn1_w_gate_up", "ffn2_w_gate_up"):
        grads[k] = _unpad_gate_up(grads[k])
    for k in ("ffn1_w_down", "ffn2_w_down"):
        grads[k] = grads[k][:FF_SHARD]

    rows = [weights[k].size // 128 for k in small]
    padded_rows = [-(-r // 8) * 8 for r in rows]
    starts = [sum(padded_rows[:i]) for i in range(len(rows) + 1)]
    total = _sum_devices(slabs, name="sum_replicated")
    loss_out = total[starts[-1], 0]

    small_w = jnp.concatenate([tile_rows(weights[k]) for k in small], axis=0)
    small_m = jnp.concatenate([tile_rows(first[k]) for k in small], axis=0)
    small_v = jnp.concatenate([tile_rows(second[k]) for k in small], axis=0)
    small_out = _adamw(small_w, total[:starts[-1]], small_m, small_v, name="adamw_replicated")
    delta, new_m, new_v = {}, {}, {}
    for name_, start, n_rows in zip(small, starts, rows):
        shape = weights[name_].shape
        grads[name_] = total[start:start + n_rows].reshape(shape)
        delta[name_], new_m[name_], new_v[name_] = (a[start:start + n_rows].reshape(shape) for a in small_out)
    for name_ in order:
        if name_ in small:
            continue
        hidden_major = name_.endswith("w_gate_up")
        view = _hidden_major if hidden_major else (lambda a: a[0])
        back = (lambda a: jnp.swapaxes(a, 0, 1)[None]) if hidden_major else (lambda a: a[None])
        out = _adamw(view(weights[name_]), grads[name_], view(first[name_]), view(second[name_]),
                     name="adamw_" + name_)
        delta[name_], new_m[name_], new_v[name_] = (back(a) for a in out)
        grads[name_] = back(grads[name_])

    return (loss_out, dx[None], *[grads[k] for k in order], *[delta[k] for k in order],
            *[new_m[k] for k in order], *[new_v[k] for k in order])
```

```python
import functools

import jax
import jax.numpy as jnp
from jax import lax
from jax.experimental import pallas as pl
from jax.experimental.pallas import tpu as pltpu
from jax.experimental.pallas import tpu_sc as plsc

F32 = jnp.float32
BF16 = jnp.bfloat16
MESH = pl.DeviceIdType.MESH

RMS_EPS = 1e-6
N_DEV = 8
N_HEADS = 8
HEAD_DIM = 64
HEAD_PAIR = 2 * HEAD_DIM
POOL_WINDOWS = (2, 4, 8, 16)
POOL_GROUP = 128
POOL_WIDTH = 512
SB_WIDTH = 512
FF_SHARD = 352
FF_SHARD_PAD = 384
ATTN_BLOCK = 256
ATTN_SCALE = 0.125

ADAM_LR = 0.001
ADAM_B1 = 0.9
ADAM_B2 = 0.999
ADAM_EPS = 1e-08
ADAM_WD = 0.01
ADAM_STEP = 10

VMEM_LIMIT = 48 << 20
WGRAD_TOKENS = 2048


def _params(dims=None):
    return pltpu.CompilerParams(dimension_semantics=dims, vmem_limit_bytes=VMEM_LIMIT)


def _mm(a, b):
    return jnp.dot(a, b, preferred_element_type=F32)


def _mm_nt(a, b):
    return lax.dot_general(a, b, (((1,), (1,)), ((), ())), preferred_element_type=F32)


def _mm_tn(a, b):
    return lax.dot_general(a, b, (((0,), (0,)), ((), ())), preferred_element_type=F32)


def _row_tile(rows, cols):
    limit = max(8, (512 * 1024) // cols)
    return max(t for t in range(8, rows + 1, 8) if rows % t == 0 and (t <= limit or t == 8))


def _rstd(xf):
    return lax.rsqrt(jnp.mean(xf * xf, axis=-1, keepdims=True) + RMS_EPS)


def _rms_bwd(xf, gain, dn):
    r = _rstd(xf)
    xh = xf * r
    dgain = jnp.sum(dn * xh, axis=0, keepdims=True)
    dxh = dn * gain
    dx = r * (dxh - xh * jnp.mean(dxh * xh, axis=-1, keepdims=True))
    return dx, dgain


def _ffn_fwd(x, gain, wgu, wd, *, tm, name):
    T, D = x.shape
    tm = min(tm, T)
    nb, bw = wgu.shape[0] // 2, wgu.shape[1]

    def body(x_ref, gain_ref, wg_ref, wu_ref, wd_ref, h_ref, gu_ref, hid_ref, n_scr, acc):
        j = pl.program_id(1)

        @pl.when(j == 0)
        def _():
            xf = x_ref[...]
            n_scr[...] = (xf * _rstd(xf) * gain_ref[...]).astype(BF16)
            acc[...] = jnp.zeros_like(acc)

        n = n_scr[...]
        g = _mm_nt(n, wg_ref[...])
        u = _mm_nt(n, wu_ref[...])
        gu_ref[0] = g.astype(BF16)
        gu_ref[1] = u.astype(BF16)
        hid = (g * jax.nn.sigmoid(g) * u).astype(BF16)
        hid_ref[...] = hid
        acc[...] += _mm(hid, wd_ref[...])

        @pl.when(j == nb - 1)
        def _():
            h_ref[...] = x_ref[...] + 0.5 * acc[...]

    return pl.pallas_call(
        body, name=name, grid=(T // tm, nb),
        in_specs=[
            pl.BlockSpec((tm, D), lambda i, j: (i, 0)),
            pl.BlockSpec((1, D), lambda i, j: (0, 0)),
            pl.BlockSpec((None, bw, D), lambda i, j: (j, 0, 0)),
            pl.BlockSpec((None, bw, D), lambda i, j: (j + nb, 0, 0)),
            pl.BlockSpec((bw, D), lambda i, j: (j, 0)),
        ],
        out_specs=[
            pl.BlockSpec((tm, D), lambda i, j: (i, 0)),
            pl.BlockSpec((2, tm, bw), lambda i, j: (0, i, j)),
            pl.BlockSpec((tm, bw), lambda i, j: (i, j)),
        ],
        out_shape=[jax.ShapeDtypeStruct((T, D), F32), jax.ShapeDtypeStruct((2, T, nb * bw), BF16),
                   jax.ShapeDtypeStruct((T, nb * bw), BF16)],
        scratch_shapes=[pltpu.VMEM((tm, D), BF16), pltpu.VMEM((tm, D), F32)],
        compiler_params=_params(("arbitrary", "arbitrary")),
    )(x, gain, wgu, wgu, wd)


AFTER = pl.BlockSpec(memory_space=pltpu.HBM)


def _in_hbm(token):
    return pltpu.with_memory_space_constraint(token, pltpu.HBM)


def _ffn_bwd(dh, df, x, gain, gu, wgu, wd, after, *, tm, name):
    T, D = x.shape
    tm = min(tm, T)
    nb, bw = wgu.shape[0] // 2, wgu.shape[1]

    def body(dh_ref, df_ref, x_ref, gain_ref, gu_ref, wg_ref, wu_ref, wd_ref, after_ref,
             dx_ref, dgain_ref, n_ref, dgu_ref, dn_acc):
        i, j = pl.program_id(0), pl.program_id(1)

        @pl.when(j == 0)
        def _():
            xf = x_ref[...]
            n_ref[...] = (xf * _rstd(xf) * gain_ref[...]).astype(BF16)
            dn_acc[...] = jnp.zeros_like(dn_acc)

        @pl.when((i == 0) & (j == 0))
        def _():
            dgain_ref[...] = jnp.zeros_like(dgain_ref)

        dhid = _mm_nt(df_ref[...], wd_ref[...])
        g = gu_ref[0].astype(F32)
        u = gu_ref[1].astype(F32)
        s = jax.nn.sigmoid(g)
        silu = g * s
        dg =(dhid * u * (s * (1.0 + g * (1.0 - s)))).astype(BF16)
        du = (dhid * silu).astype(BF16)
        dgu_ref[0] = dg
        dgu_ref[1] = du
        dn_acc[...] += _mm(dg, wg_ref[...]) + _mm(du, wu_ref[...])

        @pl.when(j == nb - 1)
        def _():
            dx, dgain = _rms_bwd(x_ref[...], gain_ref[...], dn_acc[...])
            dx_ref[...] = dh_ref[...] + dx
            dgain_ref[...] += dgain

    row = lambda i, j: (i, 0)
    return pl.pallas_call(
        body, name=name, grid=(T // tm, nb),
        in_specs=[
            pl.BlockSpec((tm, D), row),
            pl.BlockSpec((tm, D), row),
            pl.BlockSpec((tm, D), row),
            pl.BlockSpec((1, D), lambda i, j: (0, 0)),
            pl.BlockSpec((2, tm, bw), lambda i, j: (0, i, j)),
            pl.BlockSpec((None, bw, D), lambda i, j: (j, 0, 0)),
            pl.BlockSpec((None, bw, D), lambda i, j: (j + nb, 0, 0)),
            pl.BlockSpec((bw, D), lambda i, j: (j, 0)),
            AFTER,
        ],
        out_specs=[
            pl.BlockSpec((tm, D), row),
            pl.BlockSpec((1, D), lambda i, j: (0, 0)),
            pl.BlockSpec((tm, D), row),
            pl.BlockSpec((2, tm, bw), lambda i, j: (0, i, j)),
        ],
        out_shape=[
            jax.ShapeDtypeStruct((T, D), F32),
            jax.ShapeDtypeStruct((1, D), F32),
            jax.ShapeDtypeStruct((T, D), BF16),
            jax.ShapeDtypeStruct((2, T, nb * bw), BF16),
        ],
        scratch_shapes=[pltpu.VMEM((tm, D), F32)],
        compiler_params=_params(("arbitrary", "arbitrary")),
    )(dh, df, x, gain, gu, wgu, wgu, wd, _in_hbm(after))


def _wgrad(a, b, *, grid, a_spec, b_spec, out_spec, out_shape, acc_shape, name, split_lanes=0):
    nk = grid[2]

    def body(a_ref, b_ref, o_ref, acc):
        k = pl.program_id(2)

        @pl.when(k == 0)
        def _():
            acc[...] = jnp.zeros_like(acc)

        acc[...] += _mm_tn(a_ref[...].astype(BF16), b_ref[...].astype(BF16))

        @pl.when(k == nk - 1)
        def _():
            if split_lanes:
                for e in range(o_ref.shape[0]):
                    o_ref[e] = acc[:, e * split_lanes:(e + 1) * split_lanes].astype(o_ref.dtype)
            else:
                o_ref[...] = acc[...].astype(o_ref.dtype)

    return pl.pallas_call(
        body, name=name, grid=grid, in_specs=[a_spec, b_spec], out_specs=out_spec,
        out_shape=jax.ShapeDtypeStruct(out_shape, BF16),
        scratch_shapes=[pltpu.VMEM(acc_shape, F32)],
        compiler_params=_params(("arbitrary", "arbitrary", "arbitrary")),
    )(a, b)


def _wgrad_gate_up(n, dgu, *, tk, name):
    T, D = n.shape
    tk = min(tk, T)
    bw = FF_SHARD_PAD * 2
    nb = dgu.shape[2] // bw
    return _wgrad(
        dgu, n, grid=(2 * nb, 1, T // tk), name=name,
        a_spec=pl.BlockSpec((None, tk, bw), lambda m, c, k: (m // nb, k, m % nb)),
        b_spec=pl.BlockSpec((tk, D), lambda m, c, k: (k, 0)),
        out_spec=pl.BlockSpec((None, bw, D), lambda m, c, k: (m, 0, 0)),
        out_shape=(2 * nb, bw, D), acc_shape=(bw, D))


def _wgrad_down(hid, df, *, tk, name):
    T, D = df.shape
    tk = min(tk, T)
    bw = FF_SHARD_PAD * 2
    nb = hid.shape[1] // bw
    return _wgrad(
        hid, df, grid=(nb, 1, T // tk), name=name,
        a_spec=pl.BlockSpec((tk, bw), lambda m, c, k: (k, m)),
        b_spec=pl.BlockSpec((tk, D), lambda m, c, k: (k, 0)),
        out_spec=pl.BlockSpec((bw, D), lambda m, c, k: (m, 0)),
        out_shape=(nb * bw, D), acc_shape=(bw, D))


def _wgrad_in(un, dproj, *, tk, name):
    T, D = un.shape
    tk = min(tk, T)
    bw = dproj.shape[1] // N_DEV
    return _wgrad(
        un, dproj, grid=(1, N_DEV, T // tk), name=name,
        a_spec=pl.BlockSpec((tk, D), lambda m, c, k: (k, 0)),
        b_spec=pl.BlockSpec((tk, bw), lambda m, c, k: (k, c)),
        out_spec=pl.BlockSpec((None, D, bw), lambda m, c, k: (c, 0, 0)),
        out_shape=(N_DEV, D, bw), acc_shape=(D, bw))


def _wgrad_full(a, b, *, tk, name, split_lanes=0):
    T, M = a.shape
    tk = min(tk, T)
    N = b.shape[1]
    if split_lanes:
        out_shape = (N // split_lanes, M, split_lanes)
        out_spec = pl.BlockSpec(out_shape, lambda m, c, k: (0, 0, 0))
    else:
        out_shape = (M, N)
        out_spec = pl.BlockSpec(out_shape, lambda m, c, k: (0, 0))
    return _wgrad(
        a, b, grid=(1, 1, T // tk), name=name,
        a_spec=pl.BlockSpec((tk, M), lambda m, c, k: (k, 0)),
        b_spec=pl.BlockSpec((tk, N), lambda m, c, k: (k, 0)),
        out_spec=out_spec, out_shape=out_shape, acc_shape=(M, N), split_lanes=split_lanes)


def _loss_bwd(h, target, gain, *, tm, name):
    T, D = h.shape
    tm = min(tm, T)

    def body(h_ref, t_ref, gain_ref, dh_ref, df_ref, loss_ref, dgain_ref):
        @pl.when(pl.program_id(0) == 0)
        def _():
            loss_ref[...] = jnp.zeros_like(loss_ref)
            dgain_ref[...] = jnp.zeros_like(dgain_ref)

        xf = h_ref[...]
        gain = gain_ref[...]
        err = xf * _rstd(xf) * gain - t_ref[...]
        loss_ref[...] += 0.5 * jnp.sum(jnp.mean(err * err, axis=-1, keepdims=True), axis=0, keepdims=True)
        dx, dgain = _rms_bwd(xf, gain, err * (1.0 / D))
        dh_ref[...] = dx
        df_ref[...] = (0.5 * dx).astype(BF16)
        dgain_ref[...] += dgain

    row = lambda i: (i, 0)
    fixed = lambda i: (0, 0)
    return pl.pallas_call(
        body, name=name, grid=(T // tm,),
        in_specs=[pl.BlockSpec((tm, D), row), pl.BlockSpec((tm, D), row), pl.BlockSpec((1, D), fixed)],
        out_specs=[pl.BlockSpec((tm, D), row), pl.BlockSpec((tm, D), row), pl.BlockSpec((1, 128), fixed),
                   pl.BlockSpec((1, D), fixed)],
        out_shape=[jax.ShapeDtypeStruct((T, D), F32), jax.ShapeDtypeStruct((T, D), BF16),
                   jax.ShapeDtypeStruct((1, 128), F32), jax.ShapeDtypeStruct((1, D), F32)],
        compiler_params=_params(("arbitrary",)),
    )(h, target, gain)


def _inproj_fwd(h, gain, w_in, *, tm, name):
    T, D = h.shape
    tm = min(tm, T)
    nb, bw = w_in.shape[0], w_in.shape[2]

    def body(h_ref, gain_ref, w_ref, un_ref, proj_ref):
        @pl.when(pl.program_id(1) == 0)
        def _():
            xf = h_ref[...]
            un_ref[...] = (xf * _rstd(xf) * gain_ref[...]).astype(BF16)

        proj_ref[...] = _mm(un_ref[...], w_ref[...])

    return pl.pallas_call(
        body, name=name, grid=(T // tm, nb),
        in_specs=[
            pl.BlockSpec((tm, D), lambda i, j: (i, 0)),
            pl.BlockSpec((1, D), lambda i, j: (0, 0)),
            pl.BlockSpec((None, D, bw), lambda i, j: (j, 0, 0)),
        ],
        out_specs=[pl.BlockSpec((tm, D), lambda i, j: (i, 0)), pl.BlockSpec((tm, bw), lambda i, j: (i, j))],
        out_shape=[jax.ShapeDtypeStruct((T, D), BF16), jax.ShapeDtypeStruct((T, nb * bw), F32)],
        compiler_params=_params(("arbitrary", "arbitrary")),
    )(h, gain, w_in)


def _inproj_bwd(dproj, dh, h, gain, w_in, *, tm, name):
    T, D = h.shape
    tm = min(tm, T)
    nb, bw = w_in.shape[0], w_in.shape[2]

    def body(dp_ref, dh_ref, h_ref, gain_ref, w_ref, dx_ref, df_ref, dgain_ref, acc):
        i, j = pl.program_id(0), pl.program_id(1)

        @pl.when(j == 0)
        def _():
            acc[...] = jnp.zeros_like(acc)

        @pl.when((i == 0) & (j == 0))
        def _():
            dgain_ref[...] = jnp.zeros_like(dgain_ref)

        acc[...] += _mm_nt(dp_ref[...], w_ref[...])

        @pl.when(j == nb - 1)
        def _():
            dx, dgain = _rms_bwd(h_ref[...], gain_ref[...], acc[...])
            dh_in = dh_ref[...] + dx
            dx_ref[...] = dh_in
            df_ref[...] = (0.5 * dh_in).astype(BF16)
            dgain_ref[...] += dgain

    row = lambda i, j: (i, 0)
    return pl.pallas_call(
        body, name=name, grid=(T // tm, nb),
        in_specs=[
            pl.BlockSpec((tm, bw), lambda i, j: (i, j)),
            pl.BlockSpec((tm, D), row),
            pl.BlockSpec((tm, D), row),
            pl.BlockSpec((1, D), lambda i, j: (0, 0)),
            pl.BlockSpec((None, D, bw), lambda i, j: (j, 0, 0)),
        ],
        out_specs=[pl.BlockSpec((tm, D), row), pl.BlockSpec((tm, D), row), pl.BlockSpec((1, D), lambda i, j: (0, 0))],
        out_shape=[jax.ShapeDtypeStruct((T, D), F32), jax.ShapeDtypeStruct((T, D), BF16),
                   jax.ShapeDtypeStruct((1, D), F32)],
        scratch_shapes=[pltpu.VMEM((tm, D), F32)],
        compiler_params=_params(("arbitrary", "arbitrary")),
    )(dproj, dh, h, gain, w_in)


def _window_sum(x, row, doublings, *, backward):
    T = x.shape[0]
    s = x
    for k in range(doublings):
        sh = 1 << k
        if backward:
            s = s + jnp.where(row < T - sh, pltpu.roll(s, T - sh, 0), 0.0)
        else:
            s = s + jnp.where(row >= sh, pltpu.roll(s, sh, 0), 0.0)
    return s


def _pool_fwd(proj, w_group, scale, *, name):
    T = proj.shape[0]

    def body(xp_ref, w_ref, scale_ref, p_ref):
        row = lax.broadcasted_iota(jnp.int32, (T, POOL_GROUP), 0)
        for gi, window in enumerate(POOL_WINDOWS):
            cols = slice(gi * POOL_GROUP, (gi + 1) * POOL_GROUP)
            x = xp_ref[:, cols]
            inv_count = 1.0 / jnp.minimum(row + 1, window).astype(F32)
            yc = _window_sum(x, row, gi + 1, backward=False) * inv_count - x
            pre = _mm(yc.astype(BF16), w_ref[gi].astype(BF16))
            p_ref[:, cols] = pre * scale_ref[:, cols]

    return pl.pallas_call(
        body, name=name, grid=(1,),
        in_specs=[
            pl.BlockSpec((T, POOL_WIDTH), lambda i: (0, 0)),
            pl.BlockSpec(w_group.shape, lambda i: (0, 0, 0)),
            pl.BlockSpec((1, POOL_WIDTH), lambda i: (0, 0)),
        ],
        out_specs=pl.BlockSpec((T, POOL_WIDTH), lambda i: (0, 0)),
        out_shape=jax.ShapeDtypeStruct((T, POOL_WIDTH), F32),
        compiler_params=_params(("arbitrary",)),
    )(proj, w_group, scale)


def _pool_bwd(dp, proj, w_group, scale, *, name):
    T = proj.shape[0]

    def body(dp_ref, xp_ref, w_ref, scale_ref, dxp_ref, dw_ref, dscale_ref):
        row = lax.broadcasted_iota(jnp.int32, (T, POOL_GROUP), 0)
        for gi, window in enumerate(POOL_WINDOWS):
            cols = slice(gi * POOL_GROUP, (gi + 1) * POOL_GROUP)
            x = xp_ref[:, cols]
            inv_count = 1.0 / jnp.minimum(row + 1, window).astype(F32)
            yc = (_window_sum(x, row, gi + 1, backward=False) * inv_count - x).astype(BF16)
            w = w_ref[gi].astype(BF16)
            pre = _mm(yc, w)
            dpg = dp_ref[:, cols]
            dscale_ref[:, cols] = jnp.sum(dpg * pre, axis=0, keepdims=True)
            dpre = (dpg * scale_ref[:, cols]).astype(BF16)
            dw_ref[gi] = _mm_tn(yc, dpre)
            dyc = _mm_nt(dpre, w)
            dxp_ref[:, cols] = _window_sum(dyc * inv_count, row, gi + 1, backward=True) - dyc

    return pl.pallas_call(
        body, name=name, grid=(1,),
        in_specs=[
            pl.BlockSpec((T, POOL_WIDTH), lambda i: (0, 0)),
            pl.BlockSpec((T, POOL_WIDTH), lambda i: (0, 0)),
            pl.BlockSpec(w_group.shape, lambda i: (0, 0, 0)),
            pl.BlockSpec((1, POOL_WIDTH), lambda i: (0, 0)),
        ],
        out_specs=[
            pl.BlockSpec((T, POOL_WIDTH), lambda i: (0, 0)),
            pl.BlockSpec(w_group.shape, lambda i: (0, 0, 0)),
            pl.BlockSpec((1, POOL_WIDTH), lambda i: (0, 0)),
        ],
        out_shape=[jax.ShapeDtypeStruct((T, POOL_WIDTH), F32), jax.ShapeDtypeStruct(w_group.shape, F32),
                   jax.ShapeDtypeStruct((1, POOL_WIDTH), F32)],
        compiler_params=_params(("arbitrary",)),
    )(dp, proj, w_group, scale)


ATTN_STRIP = 32


def _log_sigmoids(z):
    lb = jnp.minimum(z, 0.0) - jnp.log(1.0 + jnp.exp(-jnp.abs(z)))
    return lb, lb - z


def _transposed_blocks(x_ref, blocks_scr, tq):
    for b in range(blocks_scr.shape[0]):
        blocks_scr[b] = x_ref[b * tq:(b + 1) * tq, :].T.astype(BF16)


def _split_bf16(x):
    hi = x.astype(BF16)
    return hi, (x - hi.astype(F32)).astype(BF16)


def _strips(n):
    return [slice(i, i + ATTN_STRIP) for i in range(0, n, ATTN_STRIP)]


def _rows(parts):
    return jnp.concatenate(parts, axis=0)


def _attn_specs(T, tq):
    q_col = POOL_WIDTH // HEAD_PAIR
    k_col = q_col + SB_WIDTH // HEAD_PAIR
    v_col = k_col + SB_WIDTH // HEAD_PAIR
    return [
        pl.BlockSpec((tq, HEAD_PAIR), lambda p, i: (i, q_col + p)),
        pl.BlockSpec((T, HEAD_PAIR), lambda p, i: (0, k_col + p)),
        pl.BlockSpec((T, HEAD_PAIR), lambda p, i: (0, v_col + p)),
    ]


def _attn_fwd(proj, *, name):
    T = proj.shape[0]
    tq = ATTN_BLOCK

    def body(q_ref, k_ref, v_ref, o_ref, lt_ref, kt_scr, vb_scr):
        qi = pl.program_id(1)

        @pl.when(qi == 0)
        def _():
            _transposed_blocks(k_ref, kt_scr, tq)
            vb_scr[...] = v_ref[...].astype(BF16)

        head0 = lax.broadcasted_iota(jnp.int32, (tq, HEAD_PAIR), 1) < HEAD_DIM
        q = q_ref[...] * ATTN_SCALE
        qs = (jnp.where(head0, q, 0.0).astype(BF16), jnp.where(head0, 0.0, q).astype(BF16))
        r = lax.broadcasted_iota(jnp.int32, (tq, tq), 0)
        c = lax.broadcasted_iota(jnp.int32, (tq, tq), 1)
        later = (r > c).astype(BF16)
        later2 = _rows([later, later])
        causal = lambda rows: c[rows] < r[rows]
        strips = _strips(tq)

        def log_terms(z, valid):
            lbs, his, los, sums = [], [], [], []
            for rows in strips:
                lb, lm = _log_sigmoids(z[rows])
                if valid is not None:
                    lm = jnp.where(valid(rows), lm, 0.0)
                hi, lo = _split_bf16(lm)
                lbs.append(lb)
                his.append(hi)
                los.append(lo)
                sums.append(jnp.sum(lm, axis=1, keepdims=True))
            return lbs, jnp.concatenate([_rows(his), _rows(los)], axis=1), _rows(sums)

        def weights(lbs, run, after, valid):
            parts = []
            for rows, lb in zip(strips, lbs):
                a = jnp.exp(lb + run[rows] + after[rows])
                if valid is not None:
                    a = jnp.where(valid(rows), a, 0.0)
                parts.append(a.astype(BF16))
            return _rows(parts)

        def block(kj, carry, valid):
            kt = kt_scr[kj]
            vb = vb_scr[pl.ds(pl.multiple_of(kj * tq, tq), tq), :]
            run0, o0, run1, o1 = carry
            z0 = _mm(qs[0], kt)
            z1 = _mm(qs[1], kt)
            lbs0, split0, sums0 = log_terms(z0, valid)
            after0 = _mm(split0, later2)
            lbs1, split1, sums1 = log_terms(z1, valid)
            after1 = _mm(split1, later2)
            o0 = o0 + _mm(weights(lbs0, run0, after0, valid), vb)
            o1 = o1 + _mm(weights(lbs1, run1, after1, valid), vb)
            return run0 + sums0, o0, run1 + sums1, o1

        zero = (jnp.zeros((tq, 1), F32), jnp.zeros((tq, HEAD_PAIR), F32))
        carry = block(qi, zero + zero, causal)
        carry = lax.fori_loop(0, qi, lambda it, cr: block(qi - 1 - it, cr, None), carry)
        o_ref[...] = jnp.where(head0, carry[1], carry[3])
        lt_ref[...] = jnp.where(head0, carry[0], carry[2])

    out_spec = pl.BlockSpec((tq, HEAD_PAIR), lambda p, i: (i, p))
    return pl.pallas_call(
        body, name=name, grid=(N_HEADS // 2, T // tq),
        in_specs=_attn_specs(T, tq), out_specs=[out_spec, out_spec],
        out_shape=[jax.ShapeDtypeStruct((T, SB_WIDTH), F32), jax.ShapeDtypeStruct((T, SB_WIDTH), F32)],
        scratch_shapes=[pltpu.VMEM((T // tq, HEAD_PAIR, tq), BF16), pltpu.VMEM((T, HEAD_PAIR), BF16)],
        compiler_params=_params(("arbitrary", "arbitrary")),
    )(proj, proj, proj)


def _attn_bwd(proj, do, ltot, after, *, name):
    T = proj.shape[0]
    tq = ATTN_BLOCK

    def body(q_ref, k_ref, v_ref, do_ref, lt_ref, after_ref, dq_ref, dkt_ref, dvt_ref, kb_scr, kt_scr, vt_scr):
        qi = pl.program_id(1)

        @pl.when(qi == 0)
        def _():
            kb_scr[...] = k_ref[...].astype(BF16)
            _transposed_blocks(k_ref, kt_scr, tq)
            _transposed_blocks(v_ref, vt_scr, tq)
            dkt_ref[...] = jnp.zeros_like(dkt_ref)
            dvt_ref[...] = jnp.zeros_like(dvt_ref)

        head0 = lax.broadcasted_iota(jnp.int32, (tq, HEAD_PAIR), 1) < HEAD_DIM
        q, do_, lt = q_ref[...] * ATTN_SCALE, do_ref[...], lt_ref[...]
        qs = (jnp.where(head0, q, 0.0).astype(BF16), jnp.where(head0, 0.0, q).astype(BF16))
        q_heads = (jnp.where(head0, q, 0.0), jnp.where(head0, 0.0, q))
        do_heads = (jnp.where(head0, do_, 0.0), jnp.where(head0, 0.0, do_))
        dos = tuple(d.astype(BF16) for d in do_heads)
        qts = tuple(x.T.astype(BF16) for x in q_heads)
        dots = tuple(d.T.astype(BF16) for d in do_heads)
        lts = (jnp.max(jnp.where(head0, lt, -jnp.inf), axis=1, keepdims=True),
               jnp.max(jnp.where(head0, -jnp.inf, lt), axis=1, keepdims=True))
        r = lax.broadcasted_iota(jnp.int32, (tq, tq), 0)
        c = lax.broadcasted_iota(jnp.int32, (tq, tq), 1)
        upto = (r <= c).astype(BF16)
        before = (r < c).astype(BF16)
        upto2, before2 = _rows([upto, upto]), _rows([before, before])
        causal = lambda rows: c[rows] < r[rows]
        strips = _strips(tq)

        def log_terms(z, valid):
            lbs, his, los, sums = [], [], [], []
            for rows in strips:
                lb, lm = _log_sigmoids(z[rows])
                if valid is not None:
                    lm = jnp.where(valid(rows), lm, 0.0)
                hi, lo = _split_bf16(lm)
                lbs.append(lb)
                his.append(hi)
                los.append(lo)
                sums.append(jnp.sum(lm, axis=1, keepdims=True))
            return lbs, jnp.concatenate([_rows(his), _rows(los)], axis=1), _rows(sums)

        def weights(lbs, rest, lm_upto, da, valid):
            a_parts, es, his, los, sums = [], [], [], [], []
            for rows, lb in zip(strips, lbs):
                a = jnp.exp(lb + (rest[rows] - lm_upto[rows]))
                if valid is not None:
                    a = jnp.where(valid(rows), a, 0.0)
                e = da[rows] * a
                hi, lo = _split_bf16(e)
                a_parts.append(a.astype(BF16))
                es.append(e)
                his.append(hi)
                los.append(lo)
                sums.append(jnp.sum(e, axis=1, keepdims=True))
            return _rows(a_parts), es, jnp.concatenate([_rows(his), _rows(los)], axis=1), _rows(sums)

        def score_grads(lbs, es, run_e, e_before, valid):
            parts = []
            for rows, lb, e in zip(strips, lbs, es):
                beta = jnp.exp(lb)
                dz = e * (1.0 - beta) - (run_e[rows] + e_before[rows]) * beta
                if valid is not None:
                    dz = jnp.where(valid(rows), dz, 0.0)
                parts.append(dz.astype(BF16))
            return _rows(parts)

        def block(kj, carry, valid):
            off = pl.multiple_of(kj * tq, tq)
            kb, kt, vt = kb_scr[pl.ds(off, tq), :], kt_scr[kj], vt_scr[kj]
            run_lm0, run_e0, dq0, run_lm1, run_e1, dq1 = carry
            z0, da0 = _mm(qs[0], kt), _mm(dos[0], vt)
            z1, da1 = _mm(qs[1], kt), _mm(dos[1], vt)
            lbs0, split0, lm_sums0 = log_terms(z0, valid)
            lm_upto0 = _mm(split0, upto2)
            lbs1, split1, lm_sums1 = log_terms(z1, valid)
            lm_upto1 = _mm(split1, upto2)
            a0, es0, split0, e_sums0 = weights(lbs0, lts[0] - run_lm0, lm_upto0, da0, valid)
            e_before0 = _mm(split0, before2)
            a1, es1, split1, e_sums1 = weights(lbs1, lts[1] - run_lm1, lm_upto1, da1, valid)
            e_before1 = _mm(split1, before2)
            dz0 = score_grads(lbs0, es0, run_e0, e_before0, valid)
            dkt_blk = _mm(qts[0], dz0)
            dvt_blk = _mm(dots[0], a0)
            dq0 = dq0 + _mm(dz0, kb)
            dz1 = score_grads(lbs1, es1, run_e1, e_before1, valid)
            dkt_ref[kj] += dkt_blk + _mm(qts[1], dz1)
            dvt_ref[kj] += dvt_blk + _mm(dots[1], a1)
            dq1 = dq1 + _mm(dz1, kb)
            return run_lm0 + lm_sums0, run_e0 + e_sums0, dq0, run_lm1 + lm_sums1, run_e1 + e_sums1, dq1

        zero = (jnp.zeros((tq, 1), F32), jnp.zeros((tq, 1), F32), jnp.zeros((tq, HEAD_PAIR), F32))
        carry = lax.fori_loop(0, qi, lambda kj, cr: block(kj, cr, None), zero + zero)
        carry = block(qi, carry, causal)
        dq_ref[...] = jnp.where(head0, carry[2], carry[5]) * ATTN_SCALE

    blk = pl.BlockSpec((tq, HEAD_PAIR), lambda p, i: (i, p))
    seq = pl.BlockSpec((T // tq, HEAD_PAIR, tq), lambda p, i: (0, p, 0))
    transposed = jax.ShapeDtypeStruct((T // tq, SB_WIDTH, tq), F32)
    return pl.pallas_call(
        body, name=name, grid=(N_HEADS // 2, T // tq),
        in_specs=_attn_specs(T, tq) + [blk, blk, AFTER], out_specs=[blk, seq, seq],
        out_shape=[jax.ShapeDtypeStruct((T, SB_WIDTH), F32), transposed, transposed],
        scratch_shapes=[pltpu.VMEM((T, HEAD_PAIR), BF16), pltpu.VMEM((T // tq, HEAD_PAIR, tq), BF16),
                        pltpu.VMEM((T // tq, HEAD_PAIR, tq), BF16)],
        compiler_params=_params(("arbitrary", "arbitrary")),
    )(proj, proj, proj, do, ltot, _in_hbm(after))


def _branch(act_bf16, w_ref):
    return jnp.concatenate([_mm(act_bf16, w_ref[e]) for e in range(w_ref.shape[0])], axis=1)


def _mix_specs(T, D, tm, wbp, w_out):
    gate_col = (POOL_WIDTH + 3 * SB_WIDTH) // D
    row = lambda i: (i, 0)
    return [
        pl.BlockSpec((tm, D), row),
        pl.BlockSpec((tm, POOL_WIDTH), row),
        pl.BlockSpec((tm, SB_WIDTH), row),
        pl.BlockSpec((tm, D), lambda i: (i, gate_col)),
        pl.BlockSpec((tm, D), lambda i: (i, gate_col + 1)),
        pl.BlockSpec(wbp.shape, lambda i: (0, 0, 0)),
        pl.BlockSpec(wbp.shape, lambda i: (0, 0, 0)),
        pl.BlockSpec(w_out.shape, lambda i: (0, 0)),
    ]


def _mix_fwd(h, p, o, proj, wbp, wba, w_out, *, tm, name):
    T, D = h.shape
    tm = min(tm, T)

    def body(h_ref, p_ref, o_ref, glp_ref, gls_ref, wbp_ref, wba_ref, wout_ref, hout_ref, m_ref):
        yp = _branch(p_ref[...].astype(BF16), wbp_ref)
        ys = _branch(o_ref[...].astype(BF16), wba_ref)
        m = (jax.nn.sigmoid(glp_ref[...]) * yp + jax.nn.sigmoid(gls_ref[...]) * ys).astype(BF16)
        m_ref[...] = m
        hout_ref[...] = h_ref[...] + _mm(m, wout_ref[...])

    row = lambda i: (i, 0)
    return pl.pallas_call(
        body, name=name, grid=(T // tm,),
        in_specs=_mix_specs(T, D, tm, wbp, w_out),
        out_specs=[pl.BlockSpec((tm, D), row), pl.BlockSpec((tm, D), row)],
        out_shape=[jax.ShapeDtypeStruct((T, D), F32), jax.ShapeDtypeStruct((T, D), BF16)],
        compiler_params=_params(("arbitrary",)),
    )(h, p, o, proj, proj, wbp, wba, w_out)


def _mix_bwd(dh, p, o, proj, wbp, wba, w_out, after, *, tm, name):
    T, D = dh.shape
    tm = min(tm, T)
    bw = wbp.shape[2]

    def body(dh_ref, p_ref, o_ref, glp_ref, gls_ref, wbp_ref, wba_ref, wout_ref, after_ref,
             dyp_ref, dys_ref, dp_ref, do_ref, dgl_ref):
        dm = _mm_nt(dh_ref[...].astype(BF16), wout_ref[...])
        yp = _branch(p_ref[...].astype(BF16), wbp_ref)
        ys = _branch(o_ref[...].astype(BF16), wba_ref)
        gp = jax.nn.sigmoid(glp_ref[...])
        gs = jax.nn.sigmoid(gls_ref[...])
        dyp = (dm * gp).astype(BF16)
        dys = (dm * gs).astype(BF16)
        dyp_ref[...] = dyp
        dys_ref[...] = dys
        dgl_ref[:, :D] = (dm * yp * gp * (1.0 - gp)).astype(BF16)
        dgl_ref[:, D:] = (dm * ys * gs * (1.0 - gs)).astype(BF16)
        dp = jnp.zeros(dp_ref.shape, F32)
        do_ = jnp.zeros(do_ref.shape, F32)
        for e in range(wbp_ref.shape[0]):
            dp += _mm_nt(dyp[:, e * bw:(e + 1) * bw], wbp_ref[e])
            do_ += _mm_nt(dys[:, e * bw:(e + 1) * bw], wba_ref[e])
        dp_ref[...] = dp
        do_ref[...] = do_

    row = lambda i: (i, 0)
    return pl.pallas_call(
        body, name=name, grid=(T // tm,),
        in_specs=_mix_specs(T, D, tm, wbp, w_out) + [AFTER],
        out_specs=[pl.BlockSpec((tm, D), row), pl.BlockSpec((tm, D), row), pl.BlockSpec((tm, POOL_WIDTH), row),
                   pl.BlockSpec((tm, SB_WIDTH), row), pl.BlockSpec((tm, 2 * D), row)],
        out_shape=[jax.ShapeDtypeStruct((T, D), BF16), jax.ShapeDtypeStruct((T, D), BF16),
                   jax.ShapeDtypeStruct((T, POOL_WIDTH), F32), jax.ShapeDtypeStruct((T, SB_WIDTH), F32),
                   jax.ShapeDtypeStruct((T, 2 * D), BF16)],
        compiler_params=_params(("arbitrary",)),
    )(dh, p, o, proj, proj, wbp, wba, w_out, _in_hbm(after))


def _adamw(w, g, m, v, *, name):
    R, C = w.shape
    tr = _row_tile(R, C)

    def body(w_ref, g_ref, m_ref, v_ref, d_ref, nm_ref, nv_ref):
        g_ = g_ref[...]
        m_ = ADAM_B1 * m_ref[...] + (1.0 - ADAM_B1) * g_
        v_ = ADAM_B2 * v_ref[...] + (1.0 - ADAM_B2) * (g_ * g_)
        m_hat = m_ / (1.0 - ADAM_B1 ** ADAM_STEP)
        v_hat = v_ / (1.0 - ADAM_B2 ** ADAM_STEP)
        d_ref[...] = -ADAM_LR * (m_hat / (jnp.sqrt(v_hat) + ADAM_EPS) + ADAM_WD * w_ref[...])
        nm_ref[...] = m_
        nv_ref[...] = v_

    spec = pl.BlockSpec((tr, C), lambda i: (i, 0))
    return pl.pallas_call(
        body, name=name, grid=(R // tr,), in_specs=[spec] * 4, out_specs=[spec] * 3,
        out_shape=[jax.ShapeDtypeStruct((R, C), F32)] * 3,
        compiler_params=_params(("arbitrary",)),
    )(w, g, m, v)


def _position():
    return lax.axis_index("x"), lax.axis_index("y"), lax.axis_index("c")


def _all_gather(shards, *, name, collective_id):
    n = len(shards)

    def body(*refs):
        ins, outs = refs[:n], refs[n:2 * n]
        send_sems, recv_sems, local_sems = refs[2 * n:]
        x, y, c = _position()
        me, sibling = (x, y, c), (x, y, 1 - c)
        chips = [(1 - x, y), (x, 1 - y), (1 - x, 1 - y)]

        barrier = pltpu.get_barrier_semaphore()
        for peer in [sibling] + [(*chip, c) for chip in chips]:
            pl.semaphore_signal(barrier, inc=1, device_id=peer, device_id_type=MESH)
        pl.semaphore_wait(barrier, 4)

        def block(a, pos):
            return outs[a].at[4 * pos[0] + 2 * pos[1] + pos[2]]

        def copy(a, k, pos, to, src=None):
            return pltpu.make_async_remote_copy(
                src_ref=block(a, pos) if src is None else src, dst_ref=block(a, pos),
                send_sem=send_sems.at[7 * a + k], recv_sem=recv_sems.at[7 * a + k],
                device_id=to, device_id_type=MESH)

        started = []
        for a in range(n):
            mine = pltpu.make_async_copy(ins[a], block(a, me), local_sems.at[a])
            mine.start()
            started.append(mine)
        sends = []
        for a in range(n):
            sends += [copy(a, 1 + j, me, (*chip, c), src=ins[a]) for j, chip in enumerate(chips)]
            sends.append(copy(a, 0, me, sibling, src=ins[a]))
        for cp in sends:
            cp.start()
        for j, chip in enumerate(chips):
            for a in range(n):
                copy(a, 1 + j, (*chip, c), me).wait_recv()
                passed = copy(a, 4 + j, (*chip, c), sibling)
                passed.start()
                sends.append(passed)
        for a in range(n):
            copy(a, 0, sibling, me).wait_recv()
            for j, chip in enumerate(chips):
                copy(a, 4 + j, (*chip, 1 - c), me).wait_recv()
        for cp in sends:
            cp.wait_send()
        for cp in started:
            cp.wait()

    return pl.kernel(
        body, name=name,
        out_type=[jax.ShapeDtypeStruct((N_DEV,) + s.shape, s.dtype) for s in shards],
        mesh=plsc.ScalarSubcoreMesh(axis_name="sequencer", num_cores=1),
        scratch_types=[pltpu.SemaphoreType.DMA((7 * n,)), pltpu.SemaphoreType.DMA((7 * n,)),
                       pltpu.SemaphoreType.DMA((n,))],
        compiler_params=pltpu.CompilerParams(collective_id=collective_id),
    )(*shards)


def _chip_sums(grads, *, name):
    _, R, C = grads.shape
    rc = 128 if R % 128 == 0 else R

    def body(g_ref, partial, out_ref, mine, theirs, send_sems, recv_sems, local_sems):
        x, y, c = _position()
        my_chip = 2 * x + y

        def swap(s):
            return pltpu.make_async_remote_copy(
                src_ref=g_ref.at[2 * s + (1 - c)], dst_ref=theirs.at[s],
                send_sem=send_sems.at[s], recv_sem=recv_sems.at[s],
                device_id=(x, y, 1 - c), device_id_type=MESH)

        def load(s):
            return pltpu.make_async_copy(g_ref.at[2 * s + c], mine.at[s], local_sems.at[s])

        for s in range(4):
            swap(s).start()
            load(s).start()
        for s in range(4):
            load(s).wait()
            swap(s).wait_recv()

        def chip_sum(chip, rows):
            return mine[chip, rows, :].astype(F32) + theirs[chip, rows, :].astype(F32)

        for j in (1, 2, 3):
            @pl.loop(0, R // rc)
            def _(t):
                rows = pl.ds(pl.multiple_of(t * rc, rc), rc)
                partial[j - 1, rows, :] = chip_sum(my_chip ^ j, rows).astype(BF16)

        @pl.loop(0, R // rc)
        def _(t):
            rows = pl.ds(pl.multiple_of(t * rc, rc), rc)
            out_ref[rows, :] = chip_sum(my_chip, rows)

        for s in range(4):
            swap(s).wait_send()

    vmem = pl.BlockSpec(memory_space=pltpu.VMEM)
    return pl.pallas_call(
        body, name=name,
        in_specs=[pl.BlockSpec(memory_space=pl.ANY)], out_specs=[vmem, vmem],
        out_shape=[jax.ShapeDtypeStruct((3, R, C), BF16), jax.ShapeDtypeStruct((R, C), F32)],
        scratch_shapes=[
            pltpu.VMEM((4, R, C), BF16), pltpu.VMEM((4, R, C), BF16),
            pltpu.SemaphoreType.DMA((4,)), pltpu.SemaphoreType.DMA((4,)), pltpu.SemaphoreType.DMA((4,)),
        ],
        compiler_params=_params(),
    )(grads)


def _cross_chips(partials, *, name, collective_id):
    n = len(partials)

    def body(*refs):
        ins, outs = refs[:n], refs[n:2 * n]
        send_sems, recv_sems = refs[2 * n:]
        x, y, c = _position()
        my_chip = 2 * x + y
        peers = [((my_chip ^ j) // 2, (my_chip ^ j) % 2, c) for j in (1, 2, 3)]

        barrier = pltpu.get_barrier_semaphore()
        for peer in peers:
            pl.semaphore_signal(barrier, inc=1, device_id=peer, device_id_type=MESH)
        pl.semaphore_wait(barrier, 3)

        copies = [
            pltpu.make_async_remote_copy(
                src_ref=ins[a].at[j], dst_ref=outs[a].at[j],
                send_sem=send_sems.at[3 * a + j], recv_sem=recv_sems.at[3 * a + j],
                device_id=peers[j], device_id_type=MESH)
            for a in range(n) for j in range(3)]
        for cp in copies:
            cp.start()
        for cp in copies:
            cp.wait_recv()
        for cp in copies:
            cp.wait_send()

    return pl.kernel(
        body, name=name,
        out_type=[jax.ShapeDtypeStruct(p.shape, p.dtype) for p in partials],
        mesh=plsc.ScalarSubcoreMesh(axis_name="sequencer", num_cores=1),
        scratch_types=[pltpu.SemaphoreType.DMA((3 * n,)), pltpu.SemaphoreType.DMA((3 * n,))],
        compiler_params=pltpu.CompilerParams(collective_id=collective_id),
    )(*partials)


def _owner_sum(own, landed, after, *, name):
    R, C = own.shape
    tr = _row_tile(R, C)

    def body(own_ref, landed_ref, after_ref, out_ref):
        total = own_ref[...]
        for j in range(3):
            total = total + landed_ref[j].astype(F32)
        out_ref[...] = total

    return pl.pallas_call(
        body, name=name, grid=(R // tr,),
        in_specs=[pl.BlockSpec((tr, C), lambda i: (i, 0)), pl.BlockSpec((3, tr, C), lambda i: (0, i, 0)), AFTER],
        out_specs=pl.BlockSpec((tr, C), lambda i: (i, 0)),
        out_shape=jax.ShapeDtypeStruct((R, C), F32),
        compiler_params=_params(("arbitrary",)),
    )(own, landed, _in_hbm(after))


def _all_reduce_small(slab, *, name):
    R, C = slab.shape

    def body(in_ref, out_ref, gathered, send_sems, recv_sems):
        x, y, c = _position()
        me = 4 * x + 2 * y + c

        def copy(k):
            peer = me ^ k
            return pltpu.make_async_remote_copy(
                src_ref=in_ref, dst_ref=gathered.at[me],
                send_sem=send_sems.at[k - 1], recv_sem=recv_sems.at[k - 1],
                device_id=(peer // 4, (peer // 2) % 2, peer % 2), device_id_type=MESH)

        def arrival(k):
            return pltpu.make_async_remote_copy(
                src_ref=in_ref, dst_ref=gathered.at[me ^ k],
                send_sem=send_sems.at[k - 1], recv_sem=recv_sems.at[k - 1],
                device_id=(x, y, c), device_id_type=MESH)

        for k in range(1, N_DEV):
            copy(k).start()
        gathered[me] = in_ref[...]
        for k in range(1, N_DEV):
            arrival(k).wait_recv()
        total = gathered[0]
        for d in range(1, N_DEV):
            total = total + gathered[d]
        out_ref[...] = total
        for k in range(1, N_DEV):
            copy(k).wait_send()

    return pl.pallas_call(
        body, name=name,
        in_specs=[pl.BlockSpec(memory_space=pltpu.VMEM)],
        out_specs=pl.BlockSpec(memory_space=pltpu.VMEM),
        out_shape=jax.ShapeDtypeStruct((R, C), F32),
        scratch_shapes=[pltpu.VMEM((N_DEV, R, C), F32),
                        pltpu.SemaphoreType.DMA((N_DEV - 1,)), pltpu.SemaphoreType.DMA((N_DEV - 1,))],
        compiler_params=_params(),
    )(slab)


def _local_step(x, target, norms, pool_w_group, pool_scale, wgu1, wd1, w_in, wbp, wba, w_out, wgu2, wd2, exchange):
    n1g, nmg, n2g, nfg = norms
    D = x.shape[1]
    h1, gu1, hid1 = _ffn_fwd(x, n1g, wgu1, wd1, tm=512, name="ffn1_fwd")
    un, proj = _inproj_fwd(h1, nmg, w_in, tm=1024, name="inproj_fwd")
    p = _pool_fwd(proj, pool_w_group, pool_scale, name="pool_fwd")
    o, ltot = _attn_fwd(proj, name="attn_fwd")
    h2, m = _mix_fwd(h1, p, o, proj, wbp, wba, w_out, tm=256, name="mix_fwd")
    h3, gu2, hid2 = _ffn_fwd(h2, n2g, wgu2, wd2, tm=512, name="ffn2_fwd")
    dh3, df2, loss, d_nf = _loss_bwd(h3, target, nfg, tm=256, name="loss_bwd")

    d_wd2 = _wgrad_down(hid2, df2, tk=WGRAD_TOKENS, name="ffn2_wgrad_down")
    (g_wd2,), token = exchange("ffn2_down", [d_wd2.reshape(N_DEV, FF_SHARD_PAD, D)])
    dh2, d_n2, n2, dgu2 = _ffn_bwd(dh3, df2, h2, n2g, gu2, wgu2, wd2, token, tm=256, name="ffn2_bwd")
    d_wgu2 = _wgrad_gate_up(n2, dgu2, tk=WGRAD_TOKENS, name="ffn2_wgrad_gate_up")
    (g_wgu2,), token = exchange("ffn2_gate_up", [d_wgu2])

    dyp, dys, dp, do, dgl = _mix_bwd(dh2, p, o, proj, wbp, wba, w_out, token, tm=256, name="mix_bwd")
    d_wout = _wgrad_full(m, dh2, tk=WGRAD_TOKENS, name="wgrad_out")
    d_wbp = _wgrad_full(p, dyp, tk=WGRAD_TOKENS, name="wgrad_branch_pool", split_lanes=wbp.shape[2])
    d_wba = _wgrad_full(o, dys, tk=WGRAD_TOKENS, name="wgrad_branch_attn", split_lanes=wba.shape[2])
    (g_wbp, g_wba, g_wout), token = exchange("mix", [d_wbp, d_wba, d_wout.reshape(N_DEV, D // N_DEV, D)])
    dxp, d_wgroup, d_scale = _pool_bwd(dp, proj, pool_w_group, pool_scale, name="pool_bwd")
    dq, dkt, dvt = _attn_bwd(proj, do, ltot, token, name="attn_bwd")
    dk, dv = (t.transpose(0, 2, 1).reshape(dq.shape) for t in (dkt, dvt))
    dproj = jnp.concatenate([dxp.astype(BF16), dq.astype(BF16), dk.astype(BF16), dv.astype(BF16), dgl], axis=1)
    d_win = _wgrad_in(un, dproj, tk=WGRAD_TOKENS, name="wgrad_in")
    (g_win,), token_in = exchange("w_in", [d_win])
    dh1, df1, d_nm = _inproj_bwd(dproj, dh2, h1, nmg, w_in, tm=1024, name="inproj_bwd")
    d_wd1 = _wgrad_down(hid1, df1, tk=WGRAD_TOKENS, name="ffn1_wgrad_down")
    (g_wd1,), token_down = exchange("ffn1_down", [d_wd1.reshape(N_DEV, FF_SHARD_PAD, D)])
    token = (token_down[(0,) * token_down.ndim] + token_in[(0,) * token_in.ndim]).reshape(1, 1)

    dx, d_n1, n1, dgu1 = _ffn_bwd(dh1, df1, x, n1g, gu1, wgu1, wd1, token, tm=256, name="ffn1_bwd")
    replicated, _ = exchange("replicated", [d_n1, d_nm, d_n2, d_nf, d_scale, d_wgroup, loss])
    d_wgu1 = _wgrad_gate_up(n1, dgu1, tk=WGRAD_TOKENS, name="ffn1_wgrad_gate_up")
    (g_wgu1,), token = exchange("ffn1_gate_up", [d_wgu1])

    sharded = (g_wgu1, g_wd1, g_win, g_wbp, g_wba, g_wout, g_wgu2, g_wd2)
    return dx, sharded, replicated, token


def _hidden_major(w):
    return jnp.swapaxes(w[0], 0, 1)


def _pad_gate_up(wt):
    d = wt.shape[1]
    wt = wt.astype(BF16).reshape(2, FF_SHARD, d)
    return jnp.pad(wt, ((0, 0), (0, FF_SHARD_PAD - FF_SHARD), (0, 0))).reshape(2 * FF_SHARD_PAD, d)


def _unpad_gate_up(gt):
    d = gt.shape[1]
    return gt.reshape(2, FF_SHARD_PAD, d)[:, :FF_SHARD].reshape(2 * FF_SHARD, d)


def _pad_down(w):
    return jnp.pad(w.astype(BF16), ((0, FF_SHARD_PAD - FF_SHARD), (0, 0)))


def kernel(x, ffn1_norm, ffn1_w_gate_up, ffn1_w_down, mix_norm, w_in, pool_w_group, pool_scale, w_branch_pool, w_branch_attn, w_out, ffn2_norm, ffn2_w_gate_up, ffn2_w_down, final_norm, loss_target, m_ffn1_norm, m_ffn1_w_gate_up, m_ffn1_w_down, m_mix_norm, m_w_in, m_pool_w_group, m_pool_scale, m_w_branch_pool, m_w_branch_attn, m_w_out, m_ffn2_norm, m_ffn2_w_gate_up, m_ffn2_w_down, m_final_norm, v_ffn1_norm, v_ffn1_w_gate_up, v_ffn1_w_down, v_mix_norm, v_w_in, v_pool_w_group, v_pool_scale, v_w_branch_pool, v_w_branch_attn, v_w_out, v_ffn2_norm, v_ffn2_w_gate_up, v_ffn2_w_down, v_final_norm):
    D = x.shape[-1]
    weights = dict(ffn1_norm=ffn1_norm, ffn1_w_gate_up=ffn1_w_gate_up, ffn1_w_down=ffn1_w_down, mix_norm=mix_norm,
                   w_in=w_in, pool_w_group=pool_w_group, pool_scale=pool_scale, w_branch_pool=w_branch_pool,
                   w_branch_attn=w_branch_attn, w_out=w_out, ffn2_norm=ffn2_norm, ffn2_w_gate_up=ffn2_w_gate_up,
                   ffn2_w_down=ffn2_w_down, final_norm=final_norm)
    first = dict(ffn1_norm=m_ffn1_norm, ffn1_w_gate_up=m_ffn1_w_gate_up, ffn1_w_down=m_ffn1_w_down,
                 mix_norm=m_mix_norm, w_in=m_w_in, pool_w_group=m_pool_w_group, pool_scale=m_pool_scale,
                 w_branch_pool=m_w_branch_pool, w_branch_attn=m_w_branch_attn, w_out=m_w_out,
                 ffn2_norm=m_ffn2_norm, ffn2_w_gate_up=m_ffn2_w_gate_up, ffn2_w_down=m_ffn2_w_down,
                 final_norm=m_final_norm)
    second = dict(ffn1_norm=v_ffn1_norm, ffn1_w_gate_up=v_ffn1_w_gate_up, ffn1_w_down=v_ffn1_w_down,
                  mix_norm=v_mix_norm, w_in=v_w_in, pool_w_group=v_pool_w_group, pool_scale=v_pool_scale,
                  w_branch_pool=v_w_branch_pool, w_branch_attn=v_w_branch_attn, w_out=v_w_out,
                  ffn2_norm=v_ffn2_norm, ffn2_w_gate_up=v_ffn2_w_gate_up, ffn2_w_down=v_ffn2_w_down,
                  final_norm=v_final_norm)
    order = list(weights)

    wgu1, wd1 = _all_gather([_pad_gate_up(_hidden_major(ffn1_w_gate_up)), _pad_down(ffn1_w_down[0])],
                            name="all_gather_ffn1", collective_id=0)
    win_g, = _all_gather([w_in[0].astype(BF16)], name="all_gather_w_in", collective_id=1)
    wbp_g, wba_g, wout_g = _all_gather(
        [w_branch_pool[0].astype(BF16), w_branch_attn[0].astype(BF16), w_out[0].astype(BF16)],
        name="all_gather_mix", collective_id=2)
    wgu2, wd2 = _all_gather([_pad_gate_up(_hidden_major(ffn2_w_gate_up)), _pad_down(ffn2_w_down[0])],
                            name="all_gather_ffn2", collective_id=3)
    wd1 = wd1.reshape(N_DEV * FF_SHARD_PAD, D)
    wd2 = wd2.reshape(N_DEV * FF_SHARD_PAD, D)
    wout_g = wout_g.reshape(D, D)

    cross_ids = {"ffn2_down": 4, "ffn2_gate_up": 5, "mix": 6, "ffn1_down": 7, "w_in": 9, "ffn1_gate_up": 10}
    small = ["ffn1_norm", "mix_norm", "ffn2_norm", "final_norm", "pool_scale", "pool_w_group"]

    def tile_rows(a):
        a = a.reshape(-1, 128)
        return jnp.pad(a, ((0, -a.shape[0] % 8), (0, 0)))

    def exchange(tag, group):
        if tag == "replicated":
            slab = jnp.concatenate([tile_rows(g) for g in group[:-1]] + [jnp.broadcast_to(group[-1], (8, 128))], axis=0)
            return [slab], slab
        sums = [_chip_sums(g, name=f"chip_sums_{tag}_{i}") for i, g in enumerate(group)]
        landed = _cross_chips([s[0] for s in sums], name="cross_chips_" + tag, collective_id=cross_ids[tag])
        token = sums[0][1] if len(sums) == 1 else sum(s[1][0, 0] for s in sums).reshape(1, 1)
        return [(s[1], l) for s, l in zip(sums, landed)], token

    norms = (ffn1_norm, mix_norm, ffn2_norm, final_norm.reshape(1, D))
    dx, sharded, (slab,), last = _local_step(
        x[0], loss_target[0], norms, pool_w_group[0], pool_scale, wgu1, wd1, win_g, wbp_g, wba_g, wout_g, wgu2, wd2,
        exchange)
    names = ["ffn1_w_gate_up", "ffn1_w_down", "w_in", "w_branch_pool", "w_branch_attn", "w_out",
             "ffn2_w_gate_up", "ffn2_w_down"]
    handles = dict(zip(names, sharded))
    grads, after = {}, last
    for k in ("ffn2_w_down", "ffn2_w_gate_up", "w_branch_pool", "w_branch_attn", "w_out", "w_in", "ffn1_w_down",
              "ffn1_w_gate_up"):
        grads[k] = after = _owner_sum(*handles[k], after, name="owner_sum_" + k)
    for k in ("ffn1_w_gate_up", "ffn2_w_gate_up"):
        grads[k] = _unpad_gate_up(grads[k])
    for k in ("ffn1_w_down", "ffn2_w_down"):
        grads[k] = grads[k][:FF_SHARD]

    rows = [weights[k].size // 128 for k in small]
    padded_rows = [-(-r // 8) * 8 for r in rows]
    starts = [sum(padded_rows[:i]) for i in range(len(rows) + 1)]
    total = _all_reduce_small(slab, name="all_reduce_replicated")
    loss_out = total[starts[-1], 0]

    small_w = jnp.concatenate([tile_rows(weights[k]) for k in small], axis=0)
    small_m = jnp.concatenate([tile_rows(first[k]) for k in small], axis=0)
    small_v = jnp.concatenate([tile_rows(second[k]) for k in small], axis=0)
    small_out = _adamw(small_w, total[:starts[-1]], small_m, small_v, name="adamw_replicated")
    delta, new_m, new_v = {}, {}, {}
    for name_, start, n_rows in zip(small, starts, rows):
        shape = weights[name_].shape
        grads[name_] = total[start:start + n_rows].reshape(shape)
        delta[name_], new_m[name_], new_v[name_] = (a[start:start + n_rows].reshape(shape) for a in small_out)
    for name_ in order:
        if name_ in small:
            continue
        hidden_major = name_.endswith("w_gate_up")
        view = _hidden_major if hidden_major else (lambda a: a[0])
        back = (lambda a: jnp.swapaxes(a, 0, 1)[None]) if hidden_major else (lambda a: a[None])
        out = _adamw(view(weights[name_]), grads[name_], view(first[name_]), view(second[name_]),
                     name="adamw_" + name_)
        delta[name_], new_m[name_], new_v[name_] = (back(a) for a in out)
        grads[name_] = back(grads[name_])

    return (loss_out, dx[None], *[grads[k] for k in order], *[delta[k] for k in order],
            *[new_m[k] for k in order], *[new_v[k] for k in order])
```

```python
import functools

import jax
import jax.numpy as jnp
from jax import lax
from jax.experimental import pallas as pl
from jax.experimental.pallas import tpu as pltpu
from jax.experimental.pallas import tpu_sc as plsc

F32 = jnp.float32
BF16 = jnp.bfloat16
MESH = pl.DeviceIdType.MESH

RMS_EPS = 1e-6
N_DEV = 8
N_HEADS = 8
HEAD_DIM = 64
HEAD_PAIR = 2 * HEAD_DIM
POOL_WINDOWS = (2, 4, 8, 16)
POOL_GROUP = 128
POOL_WIDTH = 512
SB_WIDTH = 512
FF_SHARD = 352
FF_SHARD_PAD = 384
ATTN_BLOCK = 256
ATTN_SCALE = 0.125

ADAM_LR = 0.001
ADAM_B1 = 0.9
ADAM_B2 = 0.999
ADAM_EPS = 1e-08
ADAM_WD = 0.01
ADAM_STEP = 10

VMEM_LIMIT = 48 << 20
WGRAD_TOKENS = 2048


def _params(dims=None):
    return pltpu.CompilerParams(dimension_semantics=dims, vmem_limit_bytes=VMEM_LIMIT)


def _mm(a, b):
    return jnp.dot(a, b, preferred_element_type=F32)


def _mm_nt(a, b):
    return lax.dot_general(a, b, (((1,), (1,)), ((), ())), preferred_element_type=F32)


def _mm_tn(a, b):
    return lax.dot_general(a, b, (((0,), (0,)), ((), ())), preferred_element_type=F32)


def _row_tile(rows, cols):
    limit = max(8, (512 * 1024) // cols)
    return max(t for t in range(8, rows + 1, 8) if rows % t == 0 and (t <= limit or t == 8))


def _rstd(xf):
    return lax.rsqrt(jnp.mean(xf * xf, axis=-1, keepdims=True) + RMS_EPS)


def _rms_bwd(xf, gain, dn):
    r = _rstd(xf)
    xh = xf * r
    dgain = jnp.sum(dn * xh, axis=0, keepdims=True)
    dxh = dn * gain
    dx = r * (dxh - xh * jnp.mean(dxh * xh, axis=-1, keepdims=True))
    return dx, dgain


def _ffn_fwd(x, gain, wgu, wd, *, tm, name):
    T, D = x.shape
    tm = min(tm, T)
    nb, bw = wgu.shape[0] // 2, wgu.shape[1]

    def body(x_ref, gain_ref, wg_ref, wu_ref, wd_ref, h_ref, gu_ref, hid_ref, n_scr, acc):
        j = pl.program_id(1)

        @pl.when(j == 0)
        def _():
            xf = x_ref[...]
            n_scr[...] = (xf * _rstd(xf) * gain_ref[...]).astype(BF16)
            acc[...] = jnp.zeros_like(acc)

        n = n_scr[...]
        g = _mm_nt(n, wg_ref[...])
        u = _mm_nt(n, wu_ref[...])
        gu_ref[0] = g.astype(BF16)
        gu_ref[1] = u.astype(BF16)
        hid = (g * jax.nn.sigmoid(g) * u).astype(BF16)
        hid_ref[...] = hid
        acc[...] += _mm(hid, wd_ref[...])

        @pl.when(j == nb - 1)
        def _():
            h_ref[...] = x_ref[...] + 0.5 * acc[...]

    return pl.pallas_call(
        body, name=name, grid=(T // tm, nb),
        in_specs=[
            pl.BlockSpec((tm, D), lambda i, j: (i, 0)),
            pl.BlockSpec((1, D), lambda i, j: (0, 0)),
            pl.BlockSpec((None, bw, D), lambda i, j: (j, 0, 0)),
            pl.BlockSpec((None, bw, D), lambda i, j: (j + nb, 0, 0)),
            pl.BlockSpec((bw, D), lambda i, j: (j, 0)),
        ],
        out_specs=[
            pl.BlockSpec((tm, D), lambda i, j: (i, 0)),
            pl.BlockSpec((2, tm, bw), lambda i, j: (0, i, j)),
            pl.BlockSpec((tm, bw), lambda i, j: (i, j)),
        ],
        out_shape=[jax.ShapeDtypeStruct((T, D), F32), jax.ShapeDtypeStruct((2, T, nb * bw), BF16),
                   jax.ShapeDtypeStruct((T, nb * bw), BF16)],
        scratch_shapes=[pltpu.VMEM((tm, D), BF16), pltpu.VMEM((tm, D), F32)],
        compiler_params=_params(("arbitrary", "arbitrary")),
    )(x, gain, wgu, wgu, wd)


AFTER = pl.BlockSpec(memory_space=pltpu.HBM)


def _in_hbm(token):
    return pltpu.with_memory_space_constraint(token, pltpu.HBM)


def _ffn_bwd(dh, df, x, gain, gu, wgu, wd, after, *, tm, name):
    T, D = x.shape
    tm = min(tm, T)
    nb, bw = wgu.shape[0] // 2, wgu.shape[1]

    def body(dh_ref, df_ref, x_ref, gain_ref, gu_ref, wg_ref, wu_ref, wd_ref, after_ref,
             dx_ref, dgain_ref, n_ref, dgu_ref, dn_acc):
        i, j = pl.program_id(0), pl.program_id(1)

        @pl.when(j == 0)
        def _():
            xf = x_ref[...]
            n_ref[...] = (xf * _rstd(xf) * gain_ref[...]).astype(BF16)
            dn_acc[...] = jnp.zeros_like(dn_acc)

        @pl.when((i == 0) & (j == 0))
        def _():
            dgain_ref[...] = jnp.zeros_like(dgain_ref)

        dhid = _mm_nt(df_ref[...], wd_ref[...])
        g = gu_ref[0].astype(F32)
        u = gu_ref[1].astype(F32)
        s = jax.nn.sigmoid(g)
        silu = g * s
        dg =(dhid * u * (s * (1.0 + g * (1.0 - s)))).astype(BF16)
        du = (dhid * silu).astype(BF16)
        dgu_ref[0] = dg
        dgu_ref[1] = du
        dn_acc[...] += _mm(dg, wg_ref[...]) + _mm(du, wu_ref[...])

        @pl.when(j == nb - 1)
        def _():
            dx, dgain = _rms_bwd(x_ref[...], gain_ref[...], dn_acc[...])
            dx_ref[...] = dh_ref[...] + dx
            dgain_ref[...] += dgain

    row = lambda i, j: (i, 0)
    return pl.pallas_call(
        body, name=name, grid=(T // tm, nb),
        in_specs=[
            pl.BlockSpec((tm, D), row),
            pl.BlockSpec((tm, D), row),
            pl.BlockSpec((tm, D), row),
            pl.BlockSpec((1, D), lambda i, j: (0, 0)),
            pl.BlockSpec((2, tm, bw), lambda i, j: (0, i, j)),
            pl.BlockSpec((None, bw, D), lambda i, j: (j, 0, 0)),
            pl.BlockSpec((None, bw, D), lambda i, j: (j + nb, 0, 0)),
            pl.BlockSpec((bw, D), lambda i, j: (j, 0)),
            AFTER,
        ],
        out_specs=[
            pl.BlockSpec((tm, D), row),
            pl.BlockSpec((1, D), lambda i, j: (0, 0)),
            pl.BlockSpec((tm, D), row),
            pl.BlockSpec((2, tm, bw), lambda i, j: (0, i, j)),
        ],
        out_shape=[
            jax.ShapeDtypeStruct((T, D), F32),
            jax.ShapeDtypeStruct((1, D), F32),
            jax.ShapeDtypeStruct((T, D), BF16),
            jax.ShapeDtypeStruct((2, T, nb * bw), BF16),
        ],
        scratch_shapes=[pltpu.VMEM((tm, D), F32)],
        compiler_params=_params(("arbitrary", "arbitrary")),
    )(dh, df, x, gain, gu, wgu, wgu, wd, _in_hbm(after))


def _wgrad(a, b, *, grid, a_spec, b_spec, out_spec, out_shape, acc_shape, name, split_lanes=0):
    nk = grid[2]

    def body(a_ref, b_ref, o_ref, acc):
        k = pl.program_id(2)

        @pl.when(k == 0)
        def _():
            acc[...] = jnp.zeros_like(acc)

        acc[...] += _mm_tn(a_ref[...].astype(BF16), b_ref[...].astype(BF16))

        @pl.when(k == nk - 1)
        def _():
            if split_lanes:
                for e in range(o_ref.shape[0]):
                    o_ref[e] = acc[:, e * split_lanes:(e + 1) * split_lanes].astype(o_ref.dtype)
            else:
                o_ref[...] = acc[...].astype(o_ref.dtype)

    return pl.pallas_call(
        body, name=name, grid=grid, in_specs=[a_spec, b_spec], out_specs=out_spec,
        out_shape=jax.ShapeDtypeStruct(out_shape, BF16),
        scratch_shapes=[pltpu.VMEM(acc_shape, F32)],
        compiler_params=_params(("arbitrary", "arbitrary", "arbitrary")),
    )(a, b)


def _wgrad_gate_up(n, dgu, *, tk, name):
    T, D = n.shape
    tk = min(tk, T)
    bw = FF_SHARD_PAD * 2
    nb = dgu.shape[2] // bw
    return _wgrad(
        dgu, n, grid=(2 * nb, 1, T // tk), name=name,
        a_spec=pl.BlockSpec((None, tk, bw), lambda m, c, k: (m // nb, k, m % nb)),
        b_spec=pl.BlockSpec((tk, D), lambda m, c, k: (k, 0)),
        out_spec=pl.BlockSpec((None, bw, D), lambda m, c, k: (m, 0, 0)),
        out_shape=(2 * nb, bw, D), acc_shape=(bw, D))


def _wgrad_down(hid, df, *, tk, name):
    T, D = df.shape
    tk = min(tk, T)
    bw = FF_SHARD_PAD * 2
    nb = hid.shape[1] // bw
    return _wgrad(
        hid, df, grid=(nb, 1, T // tk), name=name,
        a_spec=pl.BlockSpec((tk, bw), lambda m, c, k: (k, m)),
        b_spec=pl.BlockSpec((tk, D), lambda m, c, k: (k, 0)),
        out_spec=pl.BlockSpec((bw, D), lambda m, c, k: (m, 0)),
        out_shape=(nb * bw, D), acc_shape=(bw, D))


def _wgrad_in(un, dproj, *, tk, name):
    T, D = un.shape
    tk = min(tk, T)
    bw = dproj.shape[1] // N_DEV
    return _wgrad(
        un, dproj, grid=(1, N_DEV, T // tk), name=name,
        a_spec=pl.BlockSpec((tk, D), lambda m, c, k: (k, 0)),
        b_spec=pl.BlockSpec((tk, bw), lambda m, c, k: (k, c)),
        out_spec=pl.BlockSpec((None, D, bw), lambda m, c, k: (c, 0, 0)),
        out_shape=(N_DEV, D, bw), acc_shape=(D, bw))


def _wgrad_full(a, b, *, tk, name, split_lanes=0):
    T, M = a.shape
    tk = min(tk, T)
    N = b.shape[1]
    if split_lanes:
        out_shape = (N // split_lanes, M, split_lanes)
        out_spec = pl.BlockSpec(out_shape, lambda m, c, k: (0, 0, 0))
    else:
        out_shape = (M, N)
        out_spec = pl.BlockSpec(out_shape, lambda m, c, k: (0, 0))
    return _wgrad(
        a, b, grid=(1, 1, T // tk), name=name,
        a_spec=pl.BlockSpec((tk, M), lambda m, c, k: (k, 0)),
        b_spec=pl.BlockSpec((tk, N), lambda m, c, k: (k, 0)),
        out_spec=out_spec, out_shape=out_shape, acc_shape=(M, N), split_lanes=split_lanes)


def _loss_bwd(h, target, gain, *, tm, name):
    T, D = h.shape
    tm = min(tm, T)

    def body(h_ref, t_ref, gain_ref, dh_ref, df_ref, loss_ref, dgain_ref):
        @pl.when(pl.program_id(0) == 0)
        def _():
            loss_ref[...] = jnp.zeros_like(loss_ref)
            dgain_ref[...] = jnp.zeros_like(dgain_ref)

        xf = h_ref[...]
        gain = gain_ref[...]
        err = xf * _rstd(xf) * gain - t_ref[...]
        loss_ref[...] += 0.5 * jnp.sum(jnp.mean(err * err, axis=-1, keepdims=True), axis=0, keepdims=True)
        dx, dgain = _rms_bwd(xf, gain, err * (1.0 / D))
        dh_ref[...] = dx
        df_ref[...] = (0.5 * dx).astype(BF16)
        dgain_ref[...] += dgain

    row = lambda i: (i, 0)
    fixed = lambda i: (0, 0)
    return pl.pallas_call(
        body, name=name, grid=(T // tm,),
        in_specs=[pl.BlockSpec((tm, D), row), pl.BlockSpec((tm, D), row), pl.BlockSpec((1, D), fixed)],
        out_specs=[pl.BlockSpec((tm, D), row), pl.BlockSpec((tm, D), row), pl.BlockSpec((1, 128), fixed),
                   pl.BlockSpec((1, D), fixed)],
        out_shape=[jax.ShapeDtypeStruct((T, D), F32), jax.ShapeDtypeStruct((T, D), BF16),
                   jax.ShapeDtypeStruct((1, 128), F32), jax.ShapeDtypeStruct((1, D), F32)],
        compiler_params=_params(("arbitrary",)),
    )(h, target, gain)


def _inproj_fwd(h, gain, w_in, *, tm, name):
    T, D = h.shape
    tm = min(tm, T)
    nb, bw = w_in.shape[0], w_in.shape[2]

    def body(h_ref, gain_ref, w_ref, un_ref, proj_ref):
        @pl.when(pl.program_id(1) == 0)
        def _():
            xf = h_ref[...]
            un_ref[...] = (xf * _rstd(xf) * gain_ref[...]).astype(BF16)

        proj_ref[...] = _mm(un_ref[...], w_ref[...])

    return pl.pallas_call(
        body, name=name, grid=(T // tm, nb),
        in_specs=[
            pl.BlockSpec((tm, D), lambda i, j: (i, 0)),
            pl.BlockSpec((1, D), lambda i, j: (0, 0)),
            pl.BlockSpec((None, D, bw), lambda i, j: (j, 0, 0)),
        ],
        out_specs=[pl.BlockSpec((tm, D), lambda i, j: (i, 0)), pl.BlockSpec((tm, bw), lambda i, j: (i, j))],
        out_shape=[jax.ShapeDtypeStruct((T, D), BF16), jax.ShapeDtypeStruct((T, nb * bw), F32)],
        compiler_params=_params(("arbitrary", "arbitrary")),
    )(h, gain, w_in)


def _inproj_bwd(dproj, dh, h, gain, w_in, *, tm, name):
    T, D = h.shape
    tm = min(tm, T)
    nb, bw = w_in.shape[0], w_in.shape[2]

    def body(dp_ref, dh_ref, h_ref, gain_ref, w_ref, dx_ref, df_ref, dgain_ref, acc):
        i, j = pl.program_id(0), pl.program_id(1)

        @pl.when(j == 0)
        def _():
            acc[...] = jnp.zeros_like(acc)

        @pl.when((i == 0) & (j == 0))
        def _():
            dgain_ref[...] = jnp.zeros_like(dgain_ref)

        acc[...] += _mm_nt(dp_ref[...], w_ref[...])

        @pl.when(j == nb - 1)
        def _():
            dx, dgain = _rms_bwd(h_ref[...], gain_ref[...], acc[...])
            dh_in = dh_ref[...] + dx
            dx_ref[...] = dh_in
            df_ref[...] = (0.5 * dh_in).astype(BF16)
            dgain_ref[...] += dgain

    row = lambda i, j: (i, 0)
    return pl.pallas_call(
        body, name=name, grid=(T // tm, nb),
        in_specs=[
            pl.BlockSpec((tm, bw), lambda i, j: (i, j)),
            pl.BlockSpec((tm, D), row),
            pl.BlockSpec((tm, D), row),
            pl.BlockSpec((1, D), lambda i, j: (0, 0)),
            pl.BlockSpec((None, D, bw), lambda i, j: (j, 0, 0)),
        ],
        out_specs=[pl.BlockSpec((tm, D), row), pl.BlockSpec((tm, D), row), pl.BlockSpec((1, D), lambda i, j: (0, 0))],
        out_shape=[jax.ShapeDtypeStruct((T, D), F32), jax.ShapeDtypeStruct((T, D), BF16),
                   jax.ShapeDtypeStruct((1, D), F32)],
        scratch_shapes=[pltpu.VMEM((tm, D), F32)],
        compiler_params=_params(("arbitrary", "arbitrary")),
    )(dproj, dh, h, gain, w_in)


def _window_sum(x, row, doublings, *, backward):
    T = x.shape[0]
    s = x
    for k in range(doublings):
        sh = 1 << k
        if backward:
            s = s + jnp.where(row < T - sh, pltpu.roll(s, T - sh, 0), 0.0)
        else:
            s = s + jnp.where(row >= sh, pltpu.roll(s, sh, 0), 0.0)
    return s


def _pool_fwd(proj, w_group, scale, *, name):
    T = proj.shape[0]

    def body(xp_ref, w_ref, scale_ref, p_ref):
        row = lax.broadcasted_iota(jnp.int32, (T, POOL_GROUP), 0)
        for gi, window in enumerate(POOL_WINDOWS):
            cols = slice(gi * POOL_GROUP, (gi + 1) * POOL_GROUP)
            x = xp_ref[:, cols]
            inv_count = 1.0 / jnp.minimum(row + 1, window).astype(F32)
            yc = _window_sum(x, row, gi + 1, backward=False) * inv_count - x
            pre = _mm(yc.astype(BF16), w_ref[gi].astype(BF16))
            p_ref[:, cols] = pre * scale_ref[:, cols]

    return pl.pallas_call(
        body, name=name, grid=(1,),
        in_specs=[
            pl.BlockSpec((T, POOL_WIDTH), lambda i: (0, 0)),
            pl.BlockSpec(w_group.shape, lambda i: (0, 0, 0)),
            pl.BlockSpec((1, POOL_WIDTH), lambda i: (0, 0)),
        ],
        out_specs=pl.BlockSpec((T, POOL_WIDTH), lambda i: (0, 0)),
        out_shape=jax.ShapeDtypeStruct((T, POOL_WIDTH), F32),
        compiler_params=_params(("arbitrary",)),
    )(proj, w_group, scale)


def _pool_bwd(dp, proj, w_group, scale, *, name):
    T = proj.shape[0]

    def body(dp_ref, xp_ref, w_ref, scale_ref, dxp_ref, dw_ref, dscale_ref):
        row = lax.broadcasted_iota(jnp.int32, (T, POOL_GROUP), 0)
        for gi, window in enumerate(POOL_WINDOWS):
            cols = slice(gi * POOL_GROUP, (gi + 1) * POOL_GROUP)
            x = xp_ref[:, cols]
            inv_count = 1.0 / jnp.minimum(row + 1, window).astype(F32)
            yc = (_window_sum(x, row, gi + 1, backward=False) * inv_count - x).astype(BF16)
            w = w_ref[gi].astype(BF16)
            pre = _mm(yc, w)
            dpg = dp_ref[:, cols]
            dscale_ref[:, cols] = jnp.sum(dpg * pre, axis=0, keepdims=True)
            dpre = (dpg * scale_ref[:, cols]).astype(BF16)
            dw_ref[gi] = _mm_tn(yc, dpre)
            dyc = _mm_nt(dpre, w)
            dxp_ref[:, cols] = _window_sum(dyc * inv_count, row, gi + 1, backward=True) - dyc

    return pl.pallas_call(
        body, name=name, grid=(1,),
        in_specs=[
            pl.BlockSpec((T, POOL_WIDTH), lambda i: (0, 0)),
            pl.BlockSpec((T, POOL_WIDTH), lambda i: (0, 0)),
            pl.BlockSpec(w_group.shape, lambda i: (0, 0, 0)),
            pl.BlockSpec((1, POOL_WIDTH), lambda i: (0, 0)),
        ],
        out_specs=[
            pl.BlockSpec((T, POOL_WIDTH), lambda i: (0, 0)),
            pl.BlockSpec(w_group.shape, lambda i: (0, 0, 0)),
            pl.BlockSpec((1, POOL_WIDTH), lambda i: (0, 0)),
        ],
        out_shape=[jax.ShapeDtypeStruct((T, POOL_WIDTH), F32), jax.ShapeDtypeStruct(w_group.shape, F32),
                   jax.ShapeDtypeStruct((1, POOL_WIDTH), F32)],
        compiler_params=_params(("arbitrary",)),
    )(dp, proj, w_group, scale)


ATTN_STRIP = 32


def _log_sigmoids(z):
    lb = jnp.minimum(z, 0.0) - jnp.log(1.0 + jnp.exp(-jnp.abs(z)))
    return lb, lb - z


def _transposed_blocks(x_ref, blocks_scr, tq):
    for b in range(blocks_scr.shape[0]):
        blocks_scr[b] = x_ref[b * tq:(b + 1) * tq, :].T.astype(BF16)


def _split_bf16(x):
    hi = x.astype(BF16)
    return hi, (x - hi.astype(F32)).astype(BF16)


def _strips(n):
    return [slice(i, i + ATTN_STRIP) for i in range(0, n, ATTN_STRIP)]


def _rows(parts):
    return jnp.concatenate(parts, axis=0)


def _attn_specs(T, tq):
    q_col = POOL_WIDTH // HEAD_PAIR
    k_col = q_col + SB_WIDTH // HEAD_PAIR
    v_col = k_col + SB_WIDTH // HEAD_PAIR
    return [
        pl.BlockSpec((tq, HEAD_PAIR), lambda p, i: (i, q_col + p)),
        pl.BlockSpec((T, HEAD_PAIR), lambda p, i: (0, k_col + p)),
        pl.BlockSpec((T, HEAD_PAIR), lambda p, i: (0, v_col + p)),
    ]


def _attn_fwd(proj, *, name):
    T = proj.shape[0]
    tq = ATTN_BLOCK

    def body(q_ref, k_ref, v_ref, o_ref, lt_ref, kt_scr, vb_scr):
        qi = pl.program_id(1)

        @pl.when(qi == 0)
        def _():
            _transposed_blocks(k_ref, kt_scr, tq)
            vb_scr[...] = v_ref[...].astype(BF16)

        head0 = lax.broadcasted_iota(jnp.int32, (tq, HEAD_PAIR), 1) < HEAD_DIM
        q = q_ref[...] * ATTN_SCALE
        qs = (jnp.where(head0, q, 0.0).astype(BF16), jnp.where(head0, 0.0, q).astype(BF16))
        r = lax.broadcasted_iota(jnp.int32, (tq, tq), 0)
        c = lax.broadcasted_iota(jnp.int32, (tq, tq), 1)
        later = (r > c).astype(BF16)
        later2 = _rows([later, later])
        causal = lambda rows: c[rows] < r[rows]
        strips = _strips(tq)

        def log_terms(z, valid):
            lbs, his, los, sums = [], [], [], []
            for rows in strips:
                lb, lm = _log_sigmoids(z[rows])
                if valid is not None:
                    lm = jnp.where(valid(rows), lm, 0.0)
                hi, lo = _split_bf16(lm)
                lbs.append(lb)
                his.append(hi)
                los.append(lo)
                sums.append(jnp.sum(lm, axis=1, keepdims=True))
            return lbs, jnp.concatenate([_rows(his), _rows(los)], axis=1), _rows(sums)

        def weights(lbs, run, after, valid):
            parts = []
            for rows, lb in zip(strips, lbs):
                a = jnp.exp(lb + run[rows] + after[rows])
                if valid is not None:
                    a = jnp.where(valid(rows), a, 0.0)
                parts.append(a.astype(BF16))
            return _rows(parts)

        def block(kj, carry, valid):
            kt = kt_scr[kj]
            vb = vb_scr[pl.ds(pl.multiple_of(kj * tq, tq), tq), :]
            run0, o0, run1, o1 = carry
            z0 = _mm(qs[0], kt)
            z1 = _mm(qs[1], kt)
            lbs0, split0, sums0 = log_terms(z0, valid)
            after0 = _mm(split0, later2)
            lbs1, split1, sums1 = log_terms(z1, valid)
            after1 = _mm(split1, later2)
            o0 = o0 + _mm(weights(lbs0, run0, after0, valid), vb)
            o1 = o1 + _mm(weights(lbs1, run1, after1, valid), vb)
            return run0 + sums0, o0, run1 + sums1, o1

        zero = (jnp.zeros((tq, 1), F32), jnp.zeros((tq, HEAD_PAIR), F32))
        carry = block(qi, zero + zero, causal)
        carry = lax.fori_loop(0, qi, lambda it, cr: block(qi - 1 - it, cr, None), carry)
        o_ref[...] = jnp.where(head0, carry[1], carry[3])
        lt_ref[...] = jnp.where(head0, carry[0], carry[2])

    out_spec = pl.BlockSpec((tq, HEAD_PAIR), lambda p, i: (i, p))
    return pl.pallas_call(
        body, name=name, grid=(N_HEADS // 2, T // tq),
        in_specs=_attn_specs(T, tq), out_specs=[out_spec, out_spec],
        out_shape=[jax.ShapeDtypeStruct((T, SB_WIDTH), F32), jax.ShapeDtypeStruct((T, SB_WIDTH), F32)],
        scratch_shapes=[pltpu.VMEM((T // tq, HEAD_PAIR, tq), BF16), pltpu.VMEM((T, HEAD_PAIR), BF16)],
        compiler_params=_params(("arbitrary", "arbitrary")),
    )(proj, proj, proj)


def _attn_bwd(proj, do, ltot, after, *, name):
    T = proj.shape[0]
    tq = ATTN_BLOCK

    def body(q_ref, k_ref, v_ref, do_ref, lt_ref, after_ref, dq_ref, dkt_ref, dvt_ref, kb_scr, kt_scr, vt_scr):
        qi = pl.program_id(1)

        @pl.when(qi == 0)
        def _():
            kb_scr[...] = k_ref[...].astype(BF16)
            _transposed_blocks(k_ref, kt_scr, tq)
            _transposed_blocks(v_ref, vt_scr, tq)
            dkt_ref[...] = jnp.zeros_like(dkt_ref)
            dvt_ref[...] = jnp.zeros_like(dvt_ref)

        head0 = lax.broadcasted_iota(jnp.int32, (tq, HEAD_PAIR), 1) < HEAD_DIM
        q, do_, lt = q_ref[...] * ATTN_SCALE, do_ref[...], lt_ref[...]
        qs = (jnp.where(head0, q, 0.0).astype(BF16), jnp.where(head0, 0.0, q).astype(BF16))
        q_heads = (jnp.where(head0, q, 0.0), jnp.where(head0, 0.0, q))
        do_heads = (jnp.where(head0, do_, 0.0), jnp.where(head0, 0.0, do_))
        dos = tuple(d.astype(BF16) for d in do_heads)
        qts = tuple(x.T.astype(BF16) for x in q_heads)
        dots = tuple(d.T.astype(BF16) for d in do_heads)
        lts = (jnp.max(jnp.where(head0, lt, -jnp.inf), axis=1, keepdims=True),
               jnp.max(jnp.where(head0, -jnp.inf, lt), axis=1, keepdims=True))
        r = lax.broadcasted_iota(jnp.int32, (tq, tq), 0)
        c = lax.broadcasted_iota(jnp.int32, (tq, tq), 1)
        upto = (r <= c).astype(BF16)
        before = (r < c).astype(BF16)
        upto2, before2 = _rows([upto, upto]), _rows([before, before])
        causal = lambda rows: c[rows] < r[rows]
        strips = _strips(tq)

        def log_terms(z, valid):
            lbs, his, los, sums = [], [], [], []
            for rows in strips:
                lb, lm = _log_sigmoids(z[rows])
                if valid is not None:
                    lm = jnp.where(valid(rows), lm, 0.0)
                hi, lo = _split_bf16(lm)
                lbs.append(lb)
                his.append(hi)
                los.append(lo)
                sums.append(jnp.sum(lm, axis=1, keepdims=True))
            return lbs, jnp.concatenate([_rows(his), _rows(los)], axis=1), _rows(sums)

        def weights(lbs, rest, lm_upto, da, valid):
            a_parts, es, his, los, sums = [], [], [], [], []
            for rows, lb in zip(strips, lbs):
                a = jnp.exp(lb + (rest[rows] - lm_upto[rows]))
                if valid is not None:
                    a = jnp.where(valid(rows), a, 0.0)
                e = da[rows] * a
                hi, lo = _split_bf16(e)
                a_parts.append(a.astype(BF16))
                es.append(e)
                his.append(hi)
                los.append(lo)
                sums.append(jnp.sum(e, axis=1, keepdims=True))
            return _rows(a_parts), es, jnp.concatenate([_rows(his), _rows(los)], axis=1), _rows(sums)

        def score_grads(lbs, es, run_e, e_before, valid):
            parts = []
            for rows, lb, e in zip(strips, lbs, es):
                beta = jnp.exp(lb)
                dz = e * (1.0 - beta) - (run_e[rows] + e_before[rows]) * beta
                if valid is not None:
                    dz = jnp.where(valid(rows), dz, 0.0)
                parts.append(dz.astype(BF16))
            return _rows(parts)

        def block(kj, carry, valid):
            off = pl.multiple_of(kj * tq, tq)
            kb, kt, vt = kb_scr[pl.ds(off, tq), :], kt_scr[kj], vt_scr[kj]
            run_lm0, run_e0, dq0, run_lm1, run_e1, dq1 = carry
            z0, da0 = _mm(qs[0], kt), _mm(dos[0], vt)
            z1, da1 = _mm(qs[1], kt), _mm(dos[1], vt)
            lbs0, split0, lm_sums0 = log_terms(z0, valid)
            lm_upto0 = _mm(split0, upto2)
            lbs1, split1, lm_sums1 = log_terms(z1, valid)
            lm_upto1 = _mm(split1, upto2)
            a0, es0, split0, e_sums0 = weights(lbs0, lts[0] - run_lm0, lm_upto0, da0, valid)
            e_before0 = _mm(split0, before2)
            a1, es1, split1, e_sums1 = weights(lbs1, lts[1] - run_lm1, lm_upto1, da1, valid)
            e_before1 = _mm(split1, before2)
            dz0 = score_grads(lbs0, es0, run_e0, e_before0, valid)
            dkt_blk = _mm(qts[0], dz0)
            dvt_blk = _mm(dots[0], a0)
            dq0 = dq0 + _mm(dz0, kb)
            dz1 = score_grads(lbs1, es1, run_e1, e_before1, valid)
            dkt_ref[kj] += dkt_blk + _mm(qts[1], dz1)
            dvt_ref[kj] += dvt_blk + _mm(dots[1], a1)
            dq1 = dq1 + _mm(dz1, kb)
            return run_lm0 + lm_sums0, run_e0 + e_sums0, dq0, run_lm1 + lm_sums1, run_e1 + e_sums1, dq1

        zero = (jnp.zeros((tq, 1), F32), jnp.zeros((tq, 1), F32), jnp.zeros((tq, HEAD_PAIR), F32))
        carry = lax.fori_loop(0, qi, lambda kj, cr: block(kj, cr, None), zero + zero)
        carry = block(qi, carry, causal)
        dq_ref[...] = jnp.where(head0, carry[2], carry[5]) * ATTN_SCALE

    blk = pl.BlockSpec((tq, HEAD_PAIR), lambda p, i: (i, p))
    seq = pl.BlockSpec((T // tq, HEAD_PAIR, tq), lambda p, i: (0, p, 0))
    transposed = jax.ShapeDtypeStruct((T // tq, SB_WIDTH, tq), F32)
    return pl.pallas_call(
        body, name=name, grid=(N_HEADS // 2, T // tq),
        in_specs=_attn_specs(T, tq) + [blk, blk, AFTER], out_specs=[blk, seq, seq],
        out_shape=[jax.ShapeDtypeStruct((T, SB_WIDTH), F32), transposed, transposed],
        scratch_shapes=[pltpu.VMEM((T, HEAD_PAIR), BF16), pltpu.VMEM((T // tq, HEAD_PAIR, tq), BF16),
                        pltpu.VMEM((T // tq, HEAD_PAIR, tq), BF16)],
        compiler_params=_params(("arbitrary", "arbitrary")),
    )(proj, proj, proj, do, ltot, _in_hbm(after))


def _branch(act_bf16, w_ref):
    return jnp.concatenate([_mm(act_bf16, w_ref[e]) for e in range(w_ref.shape[0])], axis=1)


def _mix_specs(T, D, tm, wbp, w_out):
    gate_col = (POOL_WIDTH + 3 * SB_WIDTH) // D
    row = lambda i: (i, 0)
    return [
        pl.BlockSpec((tm, D), row),
        pl.BlockSpec((tm, POOL_WIDTH), row),
        pl.BlockSpec((tm, SB_WIDTH), row),
        pl.BlockSpec((tm, D), lambda i: (i, gate_col)),
        pl.BlockSpec((tm, D), lambda i: (i, gate_col + 1)),
        pl.BlockSpec(wbp.shape, lambda i: (0, 0, 0)),
        pl.BlockSpec(wbp.shape, lambda i: (0, 0, 0)),
        pl.BlockSpec(w_out.shape, lambda i: (0, 0)),
    ]


def _mix_fwd(h, p, o, proj, wbp, wba, w_out, *, tm, name):
    T, D = h.shape
    tm = min(tm, T)

    def body(h_ref, p_ref, o_ref, glp_ref, gls_ref, wbp_ref, wba_ref, wout_ref, hout_ref, m_ref):
        yp = _branch(p_ref[...].astype(BF16), wbp_ref)
        ys = _branch(o_ref[...].astype(BF16), wba_ref)
        m = (jax.nn.sigmoid(glp_ref[...]) * yp + jax.nn.sigmoid(gls_ref[...]) * ys).astype(BF16)
        m_ref[...] = m
        hout_ref[...] = h_ref[...] + _mm(m, wout_ref[...])

    row = lambda i: (i, 0)
    return pl.pallas_call(
        body, name=name, grid=(T // tm,),
        in_specs=_mix_specs(T, D, tm, wbp, w_out),
        out_specs=[pl.BlockSpec((tm, D), row), pl.BlockSpec((tm, D), row)],
        out_shape=[jax.ShapeDtypeStruct((T, D), F32), jax.ShapeDtypeStruct((T, D), BF16)],
        compiler_params=_params(("arbitrary",)),
    )(h, p, o, proj, proj, wbp, wba, w_out)


def _mix_bwd(dh, p, o, proj, wbp, wba, w_out, after, *, tm, name):
    T, D = dh.shape
    tm = min(tm, T)
    bw = wbp.shape[2]

    def body(dh_ref, p_ref, o_ref, glp_ref, gls_ref, wbp_ref, wba_ref, wout_ref, after_ref,
             dyp_ref, dys_ref, dp_ref, do_ref, dgl_ref):
        dm = _mm_nt(dh_ref[...].astype(BF16), wout_ref[...])
        yp = _branch(p_ref[...].astype(BF16), wbp_ref)
        ys = _branch(o_ref[...].astype(BF16), wba_ref)
        gp = jax.nn.sigmoid(glp_ref[...])
        gs = jax.nn.sigmoid(gls_ref[...])
        dyp = (dm * gp).astype(BF16)
        dys = (dm * gs).astype(BF16)
        dyp_ref[...] = dyp
        dys_ref[...] = dys
        dgl_ref[:, :D] = (dm * yp * gp * (1.0 - gp)).astype(BF16)
        dgl_ref[:, D:] = (dm * ys * gs * (1.0 - gs)).astype(BF16)
        dp = jnp.zeros(dp_ref.shape, F32)
        do_ = jnp.zeros(do_ref.shape, F32)
        for e in range(wbp_ref.shape[0]):
            dp += _mm_nt(dyp[:, e * bw:(e + 1) * bw], wbp_ref[e])
            do_ += _mm_nt(dys[:, e * bw:(e + 1) * bw], wba_ref[e])
        dp_ref[...] = dp
        do_ref[...] = do_

    row = lambda i: (i, 0)
    return pl.pallas_call(
        body, name=name, grid=(T // tm,),
        in_specs=_mix_specs(T, D, tm, wbp, w_out) + [AFTER],
        out_specs=[pl.BlockSpec((tm, D), row), pl.BlockSpec((tm, D), row), pl.BlockSpec((tm, POOL_WIDTH), row),
                   pl.BlockSpec((tm, SB_WIDTH), row), pl.BlockSpec((tm, 2 * D), row)],
        out_shape=[jax.ShapeDtypeStruct((T, D), BF16), jax.ShapeDtypeStruct((T, D), BF16),
                   jax.ShapeDtypeStruct((T, POOL_WIDTH), F32), jax.ShapeDtypeStruct((T, SB_WIDTH), F32),
                   jax.ShapeDtypeStruct((T, 2 * D), BF16)],
        compiler_params=_params(("arbitrary",)),
    )(dh, p, o, proj, proj, wbp, wba, w_out, _in_hbm(after))


def _adamw(w, g, m, v, *, name):
    R, C = w.shape
    tr = _row_tile(R, C)

    def body(w_ref, g_ref, m_ref, v_ref, d_ref, nm_ref, nv_ref):
        g_ = g_ref[...]
        m_ = ADAM_B1 * m_ref[...] + (1.0 - ADAM_B1) * g_
        v_ = ADAM_B2 * v_ref[...] + (1.0 - ADAM_B2) * (g_ * g_)
        m_hat = m_ / (1.0 - ADAM_B1 ** ADAM_STEP)
        v_hat = v_ / (1.0 - ADAM_B2 ** ADAM_STEP)
        d_ref[...] = -ADAM_LR * (m_hat / (jnp.sqrt(v_hat) + ADAM_EPS) + ADAM_WD * w_ref[...])
        nm_ref[...] = m_
        nv_ref[...] = v_

    spec = pl.BlockSpec((tr, C), lambda i: (i, 0))
    return pl.pallas_call(
        body, name=name, grid=(R // tr,), in_specs=[spec] * 4, out_specs=[spec] * 3,
        out_shape=[jax.ShapeDtypeStruct((R, C), F32)] * 3,
        compiler_params=_params(("arbitrary",)),
    )(w, g, m, v)


def _position():
    return lax.axis_index("x"), lax.axis_index("y"), lax.axis_index("c")


def _all_gather(shards, *, name, collective_id):
    n = len(shards)

    def body(*refs):
        ins, outs = refs[:n], refs[n:2 * n]
        send_sems, recv_sems, local_sems = refs[2 * n:]
        x, y, c = _position()
        me, sibling = (x, y, c), (x, y, 1 - c)
        chips = [(1 - x, y), (x, 1 - y), (1 - x, 1 - y)]

        barrier = pltpu.get_barrier_semaphore()
        for peer in [sibling] + [(*chip, c) for chip in chips]:
            pl.semaphore_signal(barrier, inc=1, device_id=peer, device_id_type=MESH)
        pl.semaphore_wait(barrier, 4)

        def block(a, pos):
            return outs[a].at[4 * pos[0] + 2 * pos[1] + pos[2]]

        def copy(a, k, pos, to, src=None):
            return pltpu.make_async_remote_copy(
                src_ref=block(a, pos) if src is None else src, dst_ref=block(a, pos),
                send_sem=send_sems.at[7 * a + k], recv_sem=recv_sems.at[7 * a + k],
                device_id=to, device_id_type=MESH)

        started = []
        for a in range(n):
            mine = pltpu.make_async_copy(ins[a], block(a, me), local_sems.at[a])
            mine.start()
            started.append(mine)
        sends = []
        for a in range(n):
            sends += [copy(a, 1 + j, me, (*chip, c), src=ins[a]) for j, chip in enumerate(chips)]
            sends.append(copy(a, 0, me, sibling, src=ins[a]))
        for cp in sends:
            cp.start()
        for j, chip in enumerate(chips):
            for a in range(n):
                copy(a, 1 + j, (*chip, c), me).wait_recv()
                passed = copy(a, 4 + j, (*chip, c), sibling)
                passed.start()
                sends.append(passed)
        for a in range(n):
            copy(a, 0, sibling, me).wait_recv()
            for j, chip in enumerate(chips):
                copy(a, 4 + j, (*chip, 1 - c), me).wait_recv()
        for cp in sends:
            cp.wait_send()
        for cp in started:
            cp.wait()

    return pl.kernel(
        body, name=name,
        out_type=[jax.ShapeDtypeStruct((N_DEV,) + s.shape, s.dtype) for s in shards],
        mesh=plsc.ScalarSubcoreMesh(axis_name="sequencer", num_cores=1),
        scratch_types=[pltpu.SemaphoreType.DMA((7 * n,)), pltpu.SemaphoreType.DMA((7 * n,)),
                       pltpu.SemaphoreType.DMA((n,))],
        compiler_params=pltpu.CompilerParams(collective_id=collective_id),
    )(*shards)


def _chip_sums(grads, *, name):
    _, R, C = grads.shape
    rc = 128 if R % 128 == 0 else R

    def body(g_ref, partial, out_ref, mine, theirs, send_sems, recv_sems, local_sems):
        x, y, c = _position()
        my_chip = 2 * x + y

        def swap(s):
            return pltpu.make_async_remote_copy(
                src_ref=g_ref.at[2 * s + (1 - c)], dst_ref=theirs.at[s],
                send_sem=send_sems.at[s], recv_sem=recv_sems.at[s],
                device_id=(x, y, 1 - c), device_id_type=MESH)

        def load(s):
            return pltpu.make_async_copy(g_ref.at[2 * s + c], mine.at[s], local_sems.at[s])

        for s in range(4):
            swap(s).start()
            load(s).start()
        for s in range(4):
            load(s).wait()
            swap(s).wait_recv()

        def chip_sum(chip, rows):
            return mine[chip, rows, :].astype(F32) + theirs[chip, rows, :].astype(F32)

        for j in (1, 2, 3):
            @pl.loop(0, R // rc)
            def _(t):
                rows = pl.ds(pl.multiple_of(t * rc, rc), rc)
                partial[j - 1, rows, :] = chip_sum(my_chip ^ j, rows).astype(BF16)

        @pl.loop(0, R // rc)
        def _(t):
            rows = pl.ds(pl.multiple_of(t * rc, rc), rc)
            out_ref[rows, :] = chip_sum(my_chip, rows)

        for s in range(4):
            swap(s).wait_send()

    vmem = pl.BlockSpec(memory_space=pltpu.VMEM)
    return pl.pallas_call(
        body, name=name,
        in_specs=[pl.BlockSpec(memory_space=pl.ANY)], out_specs=[vmem, vmem],
        out_shape=[jax.ShapeDtypeStruct((3, R, C), BF16), jax.ShapeDtypeStruct((R, C), F32)],
        scratch_shapes=[
            pltpu.VMEM((4, R, C), BF16), pltpu.VMEM((4, R, C), BF16),
            pltpu.SemaphoreType.DMA((4,)), pltpu.SemaphoreType.DMA((4,)), pltpu.SemaphoreType.DMA((4,)),
        ],
        compiler_params=_params(),
    )(grads)


def _cross_chips(partials, *, name, collective_id):
    n = len(partials)

    def body(*refs):
        ins, outs = refs[:n], refs[n:2 * n]
        send_sems, recv_sems = refs[2 * n:]
        x, y, c = _position()
        my_chip = 2 * x + y
        peers = [((my_chip ^ j) // 2, (my_chip ^ j) % 2, c) for j in (1, 2, 3)]

        barrier = pltpu.get_barrier_semaphore()
        for peer in peers:
            pl.semaphore_signal(barrier, inc=1, device_id=peer, device_id_type=MESH)
        pl.semaphore_wait(barrier, 3)

        copies = [
            pltpu.make_async_remote_copy(
                src_ref=ins[a].at[j], dst_ref=outs[a].at[j],
                send_sem=send_sems.at[3 * a + j], recv_sem=recv_sems.at[3 * a + j],
                device_id=peers[j], device_id_type=MESH)
            for a in range(n) for j in range(3)]
        for cp in copies:
            cp.start()
        for cp in copies:
            cp.wait_recv()
        for cp in copies:
            cp.wait_send()

    return pl.kernel(
        body, name=name,
        out_type=[jax.ShapeDtypeStruct(p.shape, p.dtype) for p in partials],
        mesh=plsc.ScalarSubcoreMesh(axis_name="sequencer", num_cores=1),
        scratch_types=[pltpu.SemaphoreType.DMA((3 * n,)), pltpu.SemaphoreType.DMA((3 * n,))],
        compiler_params=pltpu.CompilerParams(collective_id=collective_id),
    )(*partials)


def _cross_chips_and_gather(partial, slab, *, name, collective_id):
    def body(part_ref, slab_ref, landed_ref, slabs_ref, send_sems, recv_sems, local_sem):
        x, y, c = _position()
        me, my_chip = 4 * x + 2 * y + c, 2 * x + y
        others = [me ^ k for k in range(1, N_DEV)]
        ids = [(o // 4, (o // 2) % 2, o % 2) for o in others]

        barrier = pltpu.get_barrier_semaphore()
        for peer in ids:
            pl.semaphore_signal(barrier, inc=1, device_id=peer, device_id_type=MESH)
        pl.semaphore_wait(barrier, N_DEV - 1)

        mine = pltpu.make_async_copy(slab_ref, slabs_ref.at[me], local_sem)
        mine.start()
        sends = [
            pltpu.make_async_remote_copy(
                src_ref=part_ref.at[j], dst_ref=landed_ref.at[j], send_sem=send_sems.at[j], recv_sem=recv_sems.at[j],
                device_id=((my_chip ^ (j + 1)) // 2, (my_chip ^ (j + 1)) % 2, c), device_id_type=MESH)
            for j in range(3)]
        sends += [
            pltpu.make_async_remote_copy(
                src_ref=slab_ref, dst_ref=slabs_ref.at[me], send_sem=send_sems.at[3 + k], recv_sem=recv_sems.at[3 + k],
                device_id=ids[k], device_id_type=MESH)
            for k in range(N_DEV - 1)]
        arrivals = sends[:3] + [
            pltpu.make_async_remote_copy(
                src_ref=slab_ref, dst_ref=slabs_ref.at[others[k]], send_sem=send_sems.at[3 + k],
                recv_sem=recv_sems.at[3 + k], device_id=ids[k], device_id_type=MESH)
            for k in range(N_DEV - 1)]
        for cp in sends:
            cp.start()
        for cp in arrivals:
            cp.wait_recv()
        for cp in sends:
            cp.wait_send()
        mine.wait()

    n_sems = 3 + N_DEV - 1
    return pl.kernel(
        body, name=name,
        out_type=[jax.ShapeDtypeStruct(partial.shape, partial.dtype),
                  jax.ShapeDtypeStruct((N_DEV,) + slab.shape, slab.dtype)],
        mesh=plsc.ScalarSubcoreMesh(axis_name="sequencer", num_cores=1),
        scratch_types=[pltpu.SemaphoreType.DMA((n_sems,)), pltpu.SemaphoreType.DMA((n_sems,)), pltpu.SemaphoreType.DMA],
        compiler_params=pltpu.CompilerParams(collective_id=collective_id),
    )(partial, slab)


def _sum_devices(gathered, after, *, name):
    _, R, C = gathered.shape

    def body(in_ref, after_ref, out_ref):
        total = in_ref[0]
        for d in range(1, N_DEV):
            total = total + in_ref[d]
        out_ref[...] = total

    return pl.pallas_call(
        body, name=name, grid=(1,),
        in_specs=[pl.BlockSpec((N_DEV, R, C), lambda i: (0, 0, 0)), AFTER],
        out_specs=pl.BlockSpec((R, C), lambda i: (0, 0)),
        out_shape=jax.ShapeDtypeStruct((R, C), F32),
        compiler_params=_params(("arbitrary",)),
    )(gathered, _in_hbm(after))


def _owner_sum(own, landed, after, *, name):
    R, C = own.shape
    tr = _row_tile(R, C)

    def body(own_ref, landed_ref, after_ref, out_ref):
        total = own_ref[...]
        for j in range(3):
            total = total + landed_ref[j].astype(F32)
        out_ref[...] = total

    return pl.pallas_call(
        body, name=name, grid=(R // tr,),
        in_specs=[pl.BlockSpec((tr, C), lambda i: (i, 0)), pl.BlockSpec((3, tr, C), lambda i: (0, i, 0)), AFTER],
        out_specs=pl.BlockSpec((tr, C), lambda i: (i, 0)),
        out_shape=jax.ShapeDtypeStruct((R, C), F32),
        compiler_params=_params(("arbitrary",)),
    )(own, landed, _in_hbm(after))


def _local_step(x, target, norms, pool_w_group, pool_scale, wgu1, wd1, w_in, wbp, wba, w_out, wgu2, wd2, exchange):
    n1g, nmg, n2g, nfg = norms
    D = x.shape[1]
    h1, gu1, hid1 = _ffn_fwd(x, n1g, wgu1, wd1, tm=512, name="ffn1_fwd")
    un, proj = _inproj_fwd(h1, nmg, w_in, tm=1024, name="inproj_fwd")
    p = _pool_fwd(proj, pool_w_group, pool_scale, name="pool_fwd")
    o, ltot = _attn_fwd(proj, name="attn_fwd")
    h2, m = _mix_fwd(h1, p, o, proj, wbp, wba, w_out, tm=256, name="mix_fwd")
    h3, gu2, hid2 = _ffn_fwd(h2, n2g, wgu2, wd2, tm=512, name="ffn2_fwd")
    dh3, df2, loss, d_nf = _loss_bwd(h3, target, nfg, tm=256, name="loss_bwd")

    d_wd2 = _wgrad_down(hid2, df2, tk=WGRAD_TOKENS, name="ffn2_wgrad_down")
    (g_wd2,), token = exchange("ffn2_down", [d_wd2.reshape(N_DEV, FF_SHARD_PAD, D)])
    dh2, d_n2, n2, dgu2 = _ffn_bwd(dh3, df2, h2, n2g, gu2, wgu2, wd2, token, tm=256, name="ffn2_bwd")
    d_wgu2 = _wgrad_gate_up(n2, dgu2, tk=WGRAD_TOKENS, name="ffn2_wgrad_gate_up")
    (g_wgu2,), token = exchange("ffn2_gate_up", [d_wgu2])

    dyp, dys, dp, do, dgl = _mix_bwd(dh2, p, o, proj, wbp, wba, w_out, token, tm=256, name="mix_bwd")
    d_wout = _wgrad_full(m, dh2, tk=WGRAD_TOKENS, name="wgrad_out")
    d_wbp = _wgrad_full(p, dyp, tk=WGRAD_TOKENS, name="wgrad_branch_pool", split_lanes=wbp.shape[2])
    d_wba = _wgrad_full(o, dys, tk=WGRAD_TOKENS, name="wgrad_branch_attn", split_lanes=wba.shape[2])
    (g_wbp, g_wba, g_wout), token = exchange("mix", [d_wbp, d_wba, d_wout.reshape(N_DEV, D // N_DEV, D)])
    dxp, d_wgroup, d_scale = _pool_bwd(dp, proj, pool_w_group, pool_scale, name="pool_bwd")
    dq, dkt, dvt = _attn_bwd(proj, do, ltot, token, name="attn_bwd")
    dk, dv = (t.transpose(0, 2, 1).reshape(dq.shape) for t in (dkt, dvt))
    dproj = jnp.concatenate([dxp.astype(BF16), dq.astype(BF16), dk.astype(BF16), dv.astype(BF16), dgl], axis=1)
    d_win = _wgrad_in(un, dproj, tk=WGRAD_TOKENS, name="wgrad_in")
    (g_win,), token_in = exchange("w_in", [d_win])
    dh1, df1, d_nm = _inproj_bwd(dproj, dh2, h1, nmg, w_in, tm=1024, name="inproj_bwd")
    d_wd1 = _wgrad_down(hid1, df1, tk=WGRAD_TOKENS, name="ffn1_wgrad_down")
    (g_wd1,), token_down = exchange("ffn1_down", [d_wd1.reshape(N_DEV, FF_SHARD_PAD, D)])
    token = (token_down[(0,) * token_down.ndim] + token_in[(0,) * token_in.ndim]).reshape(1, 1)

    dx, d_n1, n1, dgu1 = _ffn_bwd(dh1, df1, x, n1g, gu1, wgu1, wd1, token, tm=256, name="ffn1_bwd")
    d_wgu1 = _wgrad_gate_up(n1, dgu1, tk=WGRAD_TOKENS, name="ffn1_wgrad_gate_up")
    (g_wgu1, replicated), token = exchange("last", [d_wgu1, d_n1, d_nm, d_n2, d_nf, d_scale, d_wgroup, loss])

    sharded = (g_wgu1, g_wd1, g_win, g_wbp, g_wba, g_wout, g_wgu2, g_wd2)
    return dx, sharded, replicated, token


def _hidden_major(w):
    return jnp.swapaxes(w[0], 0, 1)


def _pad_gate_up(wt):
    d = wt.shape[1]
    wt = wt.astype(BF16).reshape(2, FF_SHARD, d)
    return jnp.pad(wt, ((0, 0), (0, FF_SHARD_PAD - FF_SHARD), (0, 0))).reshape(2 * FF_SHARD_PAD, d)


def _unpad_gate_up(gt):
    d = gt.shape[1]
    return gt.reshape(2, FF_SHARD_PAD, d)[:, :FF_SHARD].reshape(2 * FF_SHARD, d)


def _pad_down(w):
    return jnp.pad(w.astype(BF16), ((0, FF_SHARD_PAD - FF_SHARD), (0, 0)))


def kernel(x, ffn1_norm, ffn1_w_gate_up, ffn1_w_down, mix_norm, w_in, pool_w_group, pool_scale, w_branch_pool, w_branch_attn, w_out, ffn2_norm, ffn2_w_gate_up, ffn2_w_down, final_norm, loss_target, m_ffn1_norm, m_ffn1_w_gate_up, m_ffn1_w_down, m_mix_norm, m_w_in, m_pool_w_group, m_pool_scale, m_w_branch_pool, m_w_branch_attn, m_w_out, m_ffn2_norm, m_ffn2_w_gate_up, m_ffn2_w_down, m_final_norm, v_ffn1_norm, v_ffn1_w_gate_up, v_ffn1_w_down, v_mix_norm, v_w_in, v_pool_w_group, v_pool_scale, v_w_branch_pool, v_w_branch_attn, v_w_out, v_ffn2_norm, v_ffn2_w_gate_up, v_ffn2_w_down, v_final_norm):
    D = x.shape[-1]
    weights = dict(ffn1_norm=ffn1_norm, ffn1_w_gate_up=ffn1_w_gate_up, ffn1_w_down=ffn1_w_down, mix_norm=mix_norm,
                   w_in=w_in, pool_w_group=pool_w_group, pool_scale=pool_scale, w_branch_pool=w_branch_pool,
                   w_branch_attn=w_branch_attn, w_out=w_out, ffn2_norm=ffn2_norm, ffn2_w_gate_up=ffn2_w_gate_up,
                   ffn2_w_down=ffn2_w_down, final_norm=final_norm)
    first = dict(ffn1_norm=m_ffn1_norm, ffn1_w_gate_up=m_ffn1_w_gate_up, ffn1_w_down=m_ffn1_w_down,
                 mix_norm=m_mix_norm, w_in=m_w_in, pool_w_group=m_pool_w_group, pool_scale=m_pool_scale,
                 w_branch_pool=m_w_branch_pool, w_branch_attn=m_w_branch_attn, w_out=m_w_out,
                 ffn2_norm=m_ffn2_norm, ffn2_w_gate_up=m_ffn2_w_gate_up, ffn2_w_down=m_ffn2_w_down,
                 final_norm=m_final_norm)
    second = dict(ffn1_norm=v_ffn1_norm, ffn1_w_gate_up=v_ffn1_w_gate_up, ffn1_w_down=v_ffn1_w_down,
                  mix_norm=v_mix_norm, w_in=v_w_in, pool_w_group=v_pool_w_group, pool_scale=v_pool_scale,
                  w_branch_pool=v_w_branch_pool, w_branch_attn=v_w_branch_attn, w_out=v_w_out,
                  ffn2_norm=v_ffn2_norm, ffn2_w_gate_up=v_ffn2_w_gate_up, ffn2_w_down=v_ffn2_w_down,
                  final_norm=v_final_norm)
    order = list(weights)

    wgu1, wd1 = _all_gather([_pad_gate_up(_hidden_major(ffn1_w_gate_up)), _pad_down(ffn1_w_down[0])],
                            name="all_gather_ffn1", collective_id=0)
    win_g, = _all_gather([w_in[0].astype(BF16)], name="all_gather_w_in", collective_id=1)
    wbp_g, wba_g, wout_g = _all_gather(
        [w_branch_pool[0].astype(BF16), w_branch_attn[0].astype(BF16), w_out[0].astype(BF16)],
        name="all_gather_mix", collective_id=2)
    wgu2, wd2 = _all_gather([_pad_gate_up(_hidden_major(ffn2_w_gate_up)), _pad_down(ffn2_w_down[0])],
                            name="all_gather_ffn2", collective_id=3)
    wd1 = wd1.reshape(N_DEV * FF_SHARD_PAD, D)
    wd2 = wd2.reshape(N_DEV * FF_SHARD_PAD, D)
    wout_g = wout_g.reshape(D, D)

    cross_ids = {"ffn2_down": 4, "ffn2_gate_up": 5, "mix": 6, "ffn1_down": 7, "w_in": 8, "last": 9}
    small = ["ffn1_norm", "mix_norm", "ffn2_norm", "final_norm", "pool_scale", "pool_w_group"]

    def tile_rows(a):
        a = a.reshape(-1, 128)
        return jnp.pad(a, ((0, -a.shape[0] % 8), (0, 0)))

    def exchange(tag, group):
        if tag == "last":
            slab = jnp.concatenate([tile_rows(g) for g in group[1:-1]] + [jnp.broadcast_to(group[-1], (8, 128))], axis=0)
            partial, own = _chip_sums(group[0], name="chip_sums_last")
            landed, slabs = _cross_chips_and_gather(partial, slab, name="cross_chips_last", collective_id=cross_ids[tag])
            return [(own, landed), slabs], own
        sums = [_chip_sums(g, name=f"chip_sums_{tag}_{i}") for i, g in enumerate(group)]
        landed = _cross_chips([s[0] for s in sums], name="cross_chips_" + tag, collective_id=cross_ids[tag])
        token = sums[0][1] if len(sums) == 1 else sum(s[1][0, 0] for s in sums).reshape(1, 1)
        return [(s[1], l) for s, l in zip(sums, landed)], token

    norms = (ffn1_norm, mix_norm, ffn2_norm, final_norm.reshape(1, D))
    dx, sharded, slabs, last = _local_step(
        x[0], loss_target[0], norms, pool_w_group[0], pool_scale, wgu1, wd1, win_g, wbp_g, wba_g, wout_g, wgu2, wd2,
        exchange)
    names = ["ffn1_w_gate_up", "ffn1_w_down", "w_in", "w_branch_pool", "w_branch_attn", "w_out",
             "ffn2_w_gate_up", "ffn2_w_down"]
    handles = dict(zip(names, sharded))
    grads, after = {}, last
    for k in ("ffn2_w_down", "ffn2_w_gate_up", "w_branch_pool", "w_branch_attn", "w_out", "w_in", "ffn1_w_down",
              "ffn1_w_gate_up"):
        grads[k] = after = _owner_sum(*handles[k], after, name="owner_sum_" + k)
    last_sum = after
    for k in ("ffn1_w_gate_up", "ffn2_w_gate_up"):
        grads[k] = _unpad_gate_up(grads[k])
    for k in ("ffn1_w_down", "ffn2_w_down"):
        grads[k] = grads[k][:FF_SHARD]

    rows = [weights[k].size // 128 for k in small]
    padded_rows = [-(-r // 8) * 8 for r in rows]
    starts = [sum(padded_rows[:i]) for i in range(len(rows) + 1)]
    total = _sum_devices(slabs, last_sum, name="sum_replicated")
    loss_out = total[starts[-1], 0]

    small_w = jnp.concatenate([tile_rows(weights[k]) for k in small], axis=0)
    small_m = jnp.concatenate([tile_rows(first[k]) for k in small], axis=0)
    small_v = jnp.concatenate([tile_rows(second[k]) for k in small], axis=0)
    small_out = _adamw(small_w, total[:starts[-1]], small_m, small_v, name="adamw_replicated")
    delta, new_m, new_v = {}, {}, {}
    for name_, start, n_rows in zip(small, starts, rows):
        shape = weights[name_].shape
        grads[name_] = total[start:start + n_rows].reshape(shape)
        delta[name_], new_m[name_], new_v[name_] = (a[start:start + n_rows].reshape(shape) for a in small_out)
    for name_ in order:
        if name_ in small:
            continue
        hidden_major = name_.endswith("w_gate_up")
        view = _hidden_major if hidden_major else (lambda a: a[0])
        back = (lambda a: jnp.swapaxes(a, 0, 1)[None]) if hidden_major else (lambda a: a[None])
        out = _adamw(view(weights[name_]), grads[name_], view(first[name_]), view(second[name_]),
                     name="adamw_" + name_)
        delta[name_], new_m[name_], new_v[name_] = (back(a) for a in out)
        grads[name_] = back(grads[name_])

    return (loss_out, dx[None], *[grads[k] for k in order], *[delta[k] for k in order],
            *[new_m[k] for k in order], *[new_v[k] for k in order])
```

```python
import functools

import jax
import jax.numpy as jnp
from jax import lax
from jax.experimental import pallas as pl
from jax.experimental.pallas import tpu as pltpu
from jax.experimental.pallas import tpu_sc as plsc

F32 = jnp.float32
BF16 = jnp.bfloat16
MESH = pl.DeviceIdType.MESH

RMS_EPS = 1e-6
N_DEV = 8
N_HEADS = 8
HEAD_DIM = 64
HEAD_PAIR = 2 * HEAD_DIM
POOL_WINDOWS = (2, 4, 8, 16)
POOL_GROUP = 128
POOL_WIDTH = 512
SB_WIDTH = 512
FF_SHARD = 352
FF_SHARD_PAD = 384
ATTN_BLOCK = 256
ATTN_SCALE = 0.125

ADAM_LR = 0.001
ADAM_B1 = 0.9
ADAM_B2 = 0.999
ADAM_EPS = 1e-08
ADAM_WD = 0.01
ADAM_STEP = 10

VMEM_LIMIT = 48 << 20
WGRAD_TOKENS = 2048


def _params(dims=None):
    return pltpu.CompilerParams(dimension_semantics=dims, vmem_limit_bytes=VMEM_LIMIT)


def _mm(a, b):
    return jnp.dot(a, b, preferred_element_type=F32)


def _mm_nt(a, b):
    return lax.dot_general(a, b, (((1,), (1,)), ((), ())), preferred_element_type=F32)


def _mm_tn(a, b):
    return lax.dot_general(a, b, (((0,), (0,)), ((), ())), preferred_element_type=F32)


def _row_tile(rows, cols):
    limit = max(8, (512 * 1024) // cols)
    return max(t for t in range(8, rows + 1, 8) if rows % t == 0 and (t <= limit or t == 8))


def _rstd(xf):
    return lax.rsqrt(jnp.mean(xf * xf, axis=-1, keepdims=True) + RMS_EPS)


def _rms_bwd(xf, gain, dn):
    r = _rstd(xf)
    xh = xf * r
    dgain = jnp.sum(dn * xh, axis=0, keepdims=True)
    dxh = dn * gain
    dx = r * (dxh - xh * jnp.mean(dxh * xh, axis=-1, keepdims=True))
    return dx, dgain


def _ffn_fwd(x, gain, wgu, wd, *, tm, name):
    T, D = x.shape
    tm = min(tm, T)
    nb, bw = wgu.shape[0] // 2, wgu.shape[1]

    def body(x_ref, gain_ref, wg_ref, wu_ref, wd_ref, h_ref, gu_ref, hid_ref, n_scr, acc):
        j = pl.program_id(1)

        @pl.when(j == 0)
        def _():
            xf = x_ref[...]
            n_scr[...] = (xf * _rstd(xf) * gain_ref[...]).astype(BF16)
            acc[...] = jnp.zeros_like(acc)

        n = n_scr[...]
        g = _mm_nt(n, wg_ref[...])
        u = _mm_nt(n, wu_ref[...])
        gu_ref[0] = g.astype(BF16)
        gu_ref[1] = u.astype(BF16)
        hid = (g * jax.nn.sigmoid(g) * u).astype(BF16)
        hid_ref[...] = hid
        acc[...] += _mm(hid, wd_ref[...])

        @pl.when(j == nb - 1)
        def _():
            h_ref[...] = x_ref[...] + 0.5 * acc[...]

    return pl.pallas_call(
        body, name=name, grid=(T // tm, nb),
        in_specs=[
            pl.BlockSpec((tm, D), lambda i, j: (i, 0)),
            pl.BlockSpec((1, D), lambda i, j: (0, 0)),
            pl.BlockSpec((None, bw, D), lambda i, j: (j, 0, 0)),
            pl.BlockSpec((None, bw, D), lambda i, j: (j + nb, 0, 0)),
            pl.BlockSpec((bw, D), lambda i, j: (j, 0)),
        ],
        out_specs=[
            pl.BlockSpec((tm, D), lambda i, j: (i, 0)),
            pl.BlockSpec((2, tm, bw), lambda i, j: (0, i, j)),
            pl.BlockSpec((tm, bw), lambda i, j: (i, j)),
        ],
        out_shape=[jax.ShapeDtypeStruct((T, D), F32), jax.ShapeDtypeStruct((2, T, nb * bw), BF16),
                   jax.ShapeDtypeStruct((T, nb * bw), BF16)],
        scratch_shapes=[pltpu.VMEM((tm, D), BF16), pltpu.VMEM((tm, D), F32)],
        compiler_params=_params(("arbitrary", "arbitrary")),
    )(x, gain, wgu, wgu, wd)


AFTER = pl.BlockSpec(memory_space=pltpu.HBM)


def _in_hbm(token):
    return pltpu.with_memory_space_constraint(token, pltpu.HBM)


def _ffn_bwd(dh, df, x, gain, gu, wgu, wd, after, *, tm, name):
    T, D = x.shape
    tm = min(tm, T)
    nb, bw = wgu.shape[0] // 2, wgu.shape[1]

    def body(dh_ref, df_ref, x_ref, gain_ref, gu_ref, wg_ref, wu_ref, wd_ref, after_ref,
             dx_ref, dgain_ref, n_ref, dgu_ref, dn_acc):
        i, j = pl.program_id(0), pl.program_id(1)

        @pl.when(j == 0)
        def _():
            xf = x_ref[...]
            n_ref[...] = (xf * _rstd(xf) * gain_ref[...]).astype(BF16)
            dn_acc[...] = jnp.zeros_like(dn_acc)

        @pl.when((i == 0) & (j == 0))
        def _():
            dgain_ref[...] = jnp.zeros_like(dgain_ref)

        dhid = _mm_nt(df_ref[...], wd_ref[...])
        g = gu_ref[0].astype(F32)
        u = gu_ref[1].astype(F32)
        s = jax.nn.sigmoid(g)
        silu = g * s
        dg =(dhid * u * (s * (1.0 + g * (1.0 - s)))).astype(BF16)
        du = (dhid * silu).astype(BF16)
        dgu_ref[0] = dg
        dgu_ref[1] = du
        dn_acc[...] += _mm(dg, wg_ref[...]) + _mm(du, wu_ref[...])

        @pl.when(j == nb - 1)
        def _():
            dx, dgain = _rms_bwd(x_ref[...], gain_ref[...], dn_acc[...])
            dx_ref[...] = dh_ref[...] + dx
            dgain_ref[...] += dgain

    row = lambda i, j: (i, 0)
    return pl.pallas_call(
        body, name=name, grid=(T // tm, nb),
        in_specs=[
            pl.BlockSpec((tm, D), row),
            pl.BlockSpec((tm, D), row),
            pl.BlockSpec((tm, D), row),
            pl.BlockSpec((1, D), lambda i, j: (0, 0)),
            pl.BlockSpec((2, tm, bw), lambda i, j: (0, i, j)),
            pl.BlockSpec((None, bw, D), lambda i, j: (j, 0, 0)),
            pl.BlockSpec((None, bw, D), lambda i, j: (j + nb, 0, 0)),
            pl.BlockSpec((bw, D), lambda i, j: (j, 0)),
            AFTER,
        ],
        out_specs=[
            pl.BlockSpec((tm, D), row),
            pl.BlockSpec((1, D), lambda i, j: (0, 0)),
            pl.BlockSpec((tm, D), row),
            pl.BlockSpec((2, tm, bw), lambda i, j: (0, i, j)),
        ],
        out_shape=[
            jax.ShapeDtypeStruct((T, D), F32),
            jax.ShapeDtypeStruct((1, D), F32),
            jax.ShapeDtypeStruct((T, D), BF16),
            jax.ShapeDtypeStruct((2, T, nb * bw), BF16),
        ],
        scratch_shapes=[pltpu.VMEM((tm, D), F32)],
        compiler_params=_params(("arbitrary", "arbitrary")),
    )(dh, df, x, gain, gu, wgu, wgu, wd, _in_hbm(after))


def _wgrad(a, b, *, grid, a_spec, b_spec, out_spec, out_shape, acc_shape, name, split_lanes=0):
    nk = grid[2]

    def body(a_ref, b_ref, o_ref, acc):
        k = pl.program_id(2)

        @pl.when(k == 0)
        def _():
            acc[...] = jnp.zeros_like(acc)

        acc[...] += _mm_tn(a_ref[...].astype(BF16), b_ref[...].astype(BF16))

        @pl.when(k == nk - 1)
        def _():
            if split_lanes:
                for e in range(o_ref.shape[0]):
                    o_ref[e] = acc[:, e * split_lanes:(e + 1) * split_lanes].astype(o_ref.dtype)
            else:
                o_ref[...] = acc[...].astype(o_ref.dtype)

    return pl.pallas_call(
        body, name=name, grid=grid, in_specs=[a_spec, b_spec], out_specs=out_spec,
        out_shape=jax.ShapeDtypeStruct(out_shape, BF16),
        scratch_shapes=[pltpu.VMEM(acc_shape, F32)],
        compiler_params=_params(("arbitrary", "arbitrary", "arbitrary")),
    )(a, b)


def _wgrad_gate_up(n, dgu, *, tk, name):
    T, D = n.shape
    tk = min(tk, T)
    bw = FF_SHARD_PAD * 2
    nb = dgu.shape[2] // bw
    return _wgrad(
        dgu, n, grid=(2 * nb, 1, T // tk), name=name,
        a_spec=pl.BlockSpec((None, tk, bw), lambda m, c, k: (m // nb, k, m % nb)),
        b_spec=pl.BlockSpec((tk, D), lambda m, c, k: (k, 0)),
        out_spec=pl.BlockSpec((None, bw, D), lambda m, c, k: (m, 0, 0)),
        out_shape=(2 * nb, bw, D), acc_shape=(bw, D))


def _wgrad_down(hid, df, *, tk, name):
    T, D = df.shape
    tk = min(tk, T)
    bw = FF_SHARD_PAD * 2
    nb = hid.shape[1] // bw
    return _wgrad(
        hid, df, grid=(nb, 1, T // tk), name=name,
        a_spec=pl.BlockSpec((tk, bw), lambda m, c, k: (k, m)),
        b_spec=pl.BlockSpec((tk, D), lambda m, c, k: (k, 0)),
        out_spec=pl.BlockSpec((bw, D), lambda m, c, k: (m, 0)),
        out_shape=(nb * bw, D), acc_shape=(bw, D))


def _wgrad_in(un, dproj, *, tk, name):
    T, D = un.shape
    tk = min(tk, T)
    bw = dproj.shape[1] // N_DEV
    return _wgrad(
        un, dproj, grid=(1, N_DEV, T // tk), name=name,
        a_spec=pl.BlockSpec((tk, D), lambda m, c, k: (k, 0)),
        b_spec=pl.BlockSpec((tk, bw), lambda m, c, k: (k, c)),
        out_spec=pl.BlockSpec((None, D, bw), lambda m, c, k: (c, 0, 0)),
        out_shape=(N_DEV, D, bw), acc_shape=(D, bw))


def _wgrad_full(a, b, *, tk, name, split_lanes=0):
    T, M = a.shape
    tk = min(tk, T)
    N = b.shape[1]
    if split_lanes:
        out_shape = (N // split_lanes, M, split_lanes)
        out_spec = pl.BlockSpec(out_shape, lambda m, c, k: (0, 0, 0))
    else:
        out_shape = (M, N)
        out_spec = pl.BlockSpec(out_shape, lambda m, c, k: (0, 0))
    return _wgrad(
        a, b, grid=(1, 1, T // tk), name=name,
        a_spec=pl.BlockSpec((tk, M), lambda m, c, k: (k, 0)),
        b_spec=pl.BlockSpec((tk, N), lambda m, c, k: (k, 0)),
        out_spec=out_spec, out_shape=out_shape, acc_shape=(M, N), split_lanes=split_lanes)


def _loss_bwd(h, target, gain, *, tm, name):
    T, D = h.shape
    tm = min(tm, T)

    def body(h_ref, t_ref, gain_ref, dh_ref, df_ref, loss_ref, dgain_ref):
        @pl.when(pl.program_id(0) == 0)
        def _():
            loss_ref[...] = jnp.zeros_like(loss_ref)
            dgain_ref[...] = jnp.zeros_like(dgain_ref)

        xf = h_ref[...]
        gain = gain_ref[...]
        err = xf * _rstd(xf) * gain - t_ref[...]
        loss_ref[...] += 0.5 * jnp.sum(jnp.mean(err * err, axis=-1, keepdims=True), axis=0, keepdims=True)
        dx, dgain = _rms_bwd(xf, gain, err * (1.0 / D))
        dh_ref[...] = dx
        df_ref[...] = (0.5 * dx).astype(BF16)
        dgain_ref[...] += dgain

    row = lambda i: (i, 0)
    fixed = lambda i: (0, 0)
    return pl.pallas_call(
        body, name=name, grid=(T // tm,),
        in_specs=[pl.BlockSpec((tm, D), row), pl.BlockSpec((tm, D), row), pl.BlockSpec((1, D), fixed)],
        out_specs=[pl.BlockSpec((tm, D), row), pl.BlockSpec((tm, D), row), pl.BlockSpec((1, 128), fixed),
                   pl.BlockSpec((1, D), fixed)],
        out_shape=[jax.ShapeDtypeStruct((T, D), F32), jax.ShapeDtypeStruct((T, D), BF16),
                   jax.ShapeDtypeStruct((1, 128), F32), jax.ShapeDtypeStruct((1, D), F32)],
        compiler_params=_params(("arbitrary",)),
    )(h, target, gain)


def _inproj_fwd(h, gain, w_in, *, tm, name):
    T, D = h.shape
    tm = min(tm, T)
    nb, bw = w_in.shape[0], w_in.shape[2]

    def body(h_ref, gain_ref, w_ref, un_ref, proj_ref):
        @pl.when(pl.program_id(1) == 0)
        def _():
            xf = h_ref[...]
            un_ref[...] = (xf * _rstd(xf) * gain_ref[...]).astype(BF16)

        proj_ref[...] = _mm(un_ref[...], w_ref[...])

    return pl.pallas_call(
        body, name=name, grid=(T // tm, nb),
        in_specs=[
            pl.BlockSpec((tm, D), lambda i, j: (i, 0)),
            pl.BlockSpec((1, D), lambda i, j: (0, 0)),
            pl.BlockSpec((None, D, bw), lambda i, j: (j, 0, 0)),
        ],
        out_specs=[pl.BlockSpec((tm, D), lambda i, j: (i, 0)), pl.BlockSpec((tm, bw), lambda i, j: (i, j))],
        out_shape=[jax.ShapeDtypeStruct((T, D), BF16), jax.ShapeDtypeStruct((T, nb * bw), F32)],
        compiler_params=_params(("arbitrary", "arbitrary")),
    )(h, gain, w_in)


def _inproj_bwd(dproj, dh, h, gain, w_in, *, tm, name):
    T, D = h.shape
    tm = min(tm, T)
    nb, bw = w_in.shape[0], w_in.shape[2]

    def body(dp_ref, dh_ref, h_ref, gain_ref, w_ref, dx_ref, df_ref, dgain_ref, acc):
        i, j = pl.program_id(0), pl.program_id(1)

        @pl.when(j == 0)
        def _():
            acc[...] = jnp.zeros_like(acc)

        @pl.when((i == 0) & (j == 0))
        def _():
            dgain_ref[...] = jnp.zeros_like(dgain_ref)

        acc[...] += _mm_nt(dp_ref[...], w_ref[...])

        @pl.when(j == nb - 1)
        def _():
            dx, dgain = _rms_bwd(h_ref[...], gain_ref[...], acc[...])
            dh_in = dh_ref[...] + dx
            dx_ref[...] = dh_in
            df_ref[...] = (0.5 * dh_in).astype(BF16)
            dgain_ref[...] += dgain

    row = lambda i, j: (i, 0)
    return pl.pallas_call(
        body, name=name, grid=(T // tm, nb),
        in_specs=[
            pl.BlockSpec((tm, bw), lambda i, j: (i, j)),
            pl.BlockSpec((tm, D), row),
            pl.BlockSpec((tm, D), row),
            pl.BlockSpec((1, D), lambda i, j: (0, 0)),
            pl.BlockSpec((None, D, bw), lambda i, j: (j, 0, 0)),
        ],
        out_specs=[pl.BlockSpec((tm, D), row), pl.BlockSpec((tm, D), row), pl.BlockSpec((1, D), lambda i, j: (0, 0))],
        out_shape=[jax.ShapeDtypeStruct((T, D), F32), jax.ShapeDtypeStruct((T, D), BF16),
                   jax.ShapeDtypeStruct((1, D), F32)],
        scratch_shapes=[pltpu.VMEM((tm, D), F32)],
        compiler_params=_params(("arbitrary", "arbitrary")),
    )(dproj, dh, h, gain, w_in)


def _window_sum(x, row, doublings, *, backward):
    T = x.shape[0]
    s = x
    for k in range(doublings):
        sh = 1 << k
        if backward:
            s = s + jnp.where(row < T - sh, pltpu.roll(s, T - sh, 0), 0.0)
        else:
            s = s + jnp.where(row >= sh, pltpu.roll(s, sh, 0), 0.0)
    return s


def _pool_fwd(proj, w_group, scale, *, name):
    T = proj.shape[0]

    def body(xp_ref, w_ref, scale_ref, p_ref):
        row = lax.broadcasted_iota(jnp.int32, (T, POOL_GROUP), 0)
        for gi, window in enumerate(POOL_WINDOWS):
            cols = slice(gi * POOL_GROUP, (gi + 1) * POOL_GROUP)
            x = xp_ref[:, cols]
            inv_count = 1.0 / jnp.minimum(row + 1, window).astype(F32)
            yc = _window_sum(x, row, gi + 1, backward=False) * inv_count - x
            pre = _mm(yc.astype(BF16), w_ref[gi].astype(BF16))
            p_ref[:, cols] = pre * scale_ref[:, cols]

    return pl.pallas_call(
        body, name=name, grid=(1,),
        in_specs=[
            pl.BlockSpec((T, POOL_WIDTH), lambda i: (0, 0)),
            pl.BlockSpec(w_group.shape, lambda i: (0, 0, 0)),
            pl.BlockSpec((1, POOL_WIDTH), lambda i: (0, 0)),
        ],
        out_specs=pl.BlockSpec((T, POOL_WIDTH), lambda i: (0, 0)),
        out_shape=jax.ShapeDtypeStruct((T, POOL_WIDTH), F32),
        compiler_params=_params(("arbitrary",)),
    )(proj, w_group, scale)


def _pool_bwd(dp, proj, w_group, scale, *, name):
    T = proj.shape[0]

    def body(dp_ref, xp_ref, w_ref, scale_ref, dxp_ref, dw_ref, dscale_ref):
        row = lax.broadcasted_iota(jnp.int32, (T, POOL_GROUP), 0)
        for gi, window in enumerate(POOL_WINDOWS):
            cols = slice(gi * POOL_GROUP, (gi + 1) * POOL_GROUP)
            x = xp_ref[:, cols]
            inv_count = 1.0 / jnp.minimum(row + 1, window).astype(F32)
            yc = (_window_sum(x, row, gi + 1, backward=False) * inv_count - x).astype(BF16)
            w = w_ref[gi].astype(BF16)
            pre = _mm(yc, w)
            dpg = dp_ref[:, cols]
            dscale_ref[:, cols] = jnp.sum(dpg * pre, axis=0, keepdims=True)
            dpre = (dpg * scale_ref[:, cols]).astype(BF16)
            dw_ref[gi] = _mm_tn(yc, dpre)
            dyc = _mm_nt(dpre, w)
            dxp_ref[:, cols] = _window_sum(dyc * inv_count, row, gi + 1, backward=True) - dyc

    return pl.pallas_call(
        body, name=name, grid=(1,),
        in_specs=[
            pl.BlockSpec((T, POOL_WIDTH), lambda i: (0, 0)),
            pl.BlockSpec((T, POOL_WIDTH), lambda i: (0, 0)),
            pl.BlockSpec(w_group.shape, lambda i: (0, 0, 0)),
            pl.BlockSpec((1, POOL_WIDTH), lambda i: (0, 0)),
        ],
        out_specs=[
            pl.BlockSpec((T, POOL_WIDTH), lambda i: (0, 0)),
            pl.BlockSpec(w_group.shape, lambda i: (0, 0, 0)),
            pl.BlockSpec((1, POOL_WIDTH), lambda i: (0, 0)),
        ],
        out_shape=[jax.ShapeDtypeStruct((T, POOL_WIDTH), F32), jax.ShapeDtypeStruct(w_group.shape, F32),
                   jax.ShapeDtypeStruct((1, POOL_WIDTH), F32)],
        compiler_params=_params(("arbitrary",)),
    )(dp, proj, w_group, scale)


ATTN_STRIP = 32


def _log_sigmoids(z):
    lb = jnp.minimum(z, 0.0) - jnp.log(1.0 + jnp.exp(-jnp.abs(z)))
    return lb, lb - z


def _transposed_blocks(x_ref, blocks_scr, tq):
    for b in range(blocks_scr.shape[0]):
        blocks_scr[b] = x_ref[b * tq:(b + 1) * tq, :].T.astype(BF16)


def _split_bf16(x):
    hi = x.astype(BF16)
    return hi, (x - hi.astype(F32)).astype(BF16)


def _strips(n):
    return [slice(i, i + ATTN_STRIP) for i in range(0, n, ATTN_STRIP)]


def _rows(parts):
    return jnp.concatenate(parts, axis=0)


def _attn_specs(T, tq):
    q_col = POOL_WIDTH // HEAD_PAIR
    k_col = q_col + SB_WIDTH // HEAD_PAIR
    v_col = k_col + SB_WIDTH // HEAD_PAIR
    return [
        pl.BlockSpec((tq, HEAD_PAIR), lambda p, i: (i, q_col + p)),
        pl.BlockSpec((T, HEAD_PAIR), lambda p, i: (0, k_col + p)),
        pl.BlockSpec((T, HEAD_PAIR), lambda p, i: (0, v_col + p)),
    ]


def _attn_fwd(proj, *, name):
    T = proj.shape[0]
    tq = ATTN_BLOCK

    def body(q_ref, k_ref, v_ref, o_ref, lt_ref, kt_scr, vb_scr):
        qi = pl.program_id(1)

        @pl.when(qi == 0)
        def _():
            _transposed_blocks(k_ref, kt_scr, tq)
            vb_scr[...] = v_ref[...].astype(BF16)

        head0 = lax.broadcasted_iota(jnp.int32, (tq, HEAD_PAIR), 1) < HEAD_DIM
        q = q_ref[...] * ATTN_SCALE
        qs = (jnp.where(head0, q, 0.0).astype(BF16), jnp.where(head0, 0.0, q).astype(BF16))
        r = lax.broadcasted_iota(jnp.int32, (tq, tq), 0)
        c = lax.broadcasted_iota(jnp.int32, (tq, tq), 1)
        later = (r > c).astype(BF16)
        later2 = _rows([later, later])
        causal = lambda rows: c[rows] < r[rows]
        strips = _strips(tq)

        def log_terms(z, valid):
            lbs, his, los, sums = [], [], [], []
            for rows in strips:
                lb, lm = _log_sigmoids(z[rows])
                if valid is not None:
                    lm = jnp.where(valid(rows), lm, 0.0)
                hi, lo = _split_bf16(lm)
                lbs.append(lb)
                his.append(hi)
                los.append(lo)
                sums.append(jnp.sum(lm, axis=1, keepdims=True))
            return lbs, jnp.concatenate([_rows(his), _rows(los)], axis=1), _rows(sums)

        def weights(lbs, run, after, valid):
            parts = []
            for rows, lb in zip(strips, lbs):
                a = jnp.exp(lb + run[rows] + after[rows])
                if valid is not None:
                    a = jnp.where(valid(rows), a, 0.0)
                parts.append(a.astype(BF16))
            return _rows(parts)

        def block(kj, carry, valid):
            kt = kt_scr[kj]
            vb = vb_scr[pl.ds(pl.multiple_of(kj * tq, tq), tq), :]
            run0, o0, run1, o1 = carry
            z0 = _mm(qs[0], kt)
            z1 = _mm(qs[1], kt)
            lbs0, split0, sums0 = log_terms(z0, valid)
            after0 = _mm(split0, later2)
            lbs1, split1, sums1 = log_terms(z1, valid)
            after1 = _mm(split1, later2)
            o0 = o0 + _mm(weights(lbs0, run0, after0, valid), vb)
            o1 = o1 + _mm(weights(lbs1, run1, after1, valid), vb)
            return run0 + sums0, o0, run1 + sums1, o1

        zero = (jnp.zeros((tq, 1), F32), jnp.zeros((tq, HEAD_PAIR), F32))
        carry = block(qi, zero + zero, causal)
        carry = lax.fori_loop(0, qi, lambda it, cr: block(qi - 1 - it, cr, None), carry)
        o_ref[...] = jnp.where(head0, carry[1], carry[3])
        lt_ref[...] = jnp.where(head0, carry[0], carry[2])

    out_spec = pl.BlockSpec((tq, HEAD_PAIR), lambda p, i: (i, p))
    return pl.pallas_call(
        body, name=name, grid=(N_HEADS // 2, T // tq),
        in_specs=_attn_specs(T, tq), out_specs=[out_spec, out_spec],
        out_shape=[jax.ShapeDtypeStruct((T, SB_WIDTH), F32), jax.ShapeDtypeStruct((T, SB_WIDTH), F32)],
        scratch_shapes=[pltpu.VMEM((T // tq, HEAD_PAIR, tq), BF16), pltpu.VMEM((T, HEAD_PAIR), BF16)],
        compiler_params=_params(("arbitrary", "arbitrary")),
    )(proj, proj, proj)


def _attn_bwd(proj, do, ltot, after, *, name):
    T = proj.shape[0]
    tq = ATTN_BLOCK

    def body(q_ref, k_ref, v_ref, do_ref, lt_ref, after_ref, dq_ref, dkt_ref, dvt_ref, kb_scr, kt_scr, vt_scr):
        qi = pl.program_id(1)

        @pl.when(qi == 0)
        def _():
            kb_scr[...] = k_ref[...].astype(BF16)
            _transposed_blocks(k_ref, kt_scr, tq)
            _transposed_blocks(v_ref, vt_scr, tq)
            dkt_ref[...] = jnp.zeros_like(dkt_ref)
            dvt_ref[...] = jnp.zeros_like(dvt_ref)

        head0 = lax.broadcasted_iota(jnp.int32, (tq, HEAD_PAIR), 1) < HEAD_DIM
        q, do_, lt = q_ref[...] * ATTN_SCALE, do_ref[...], lt_ref[...]
        qs = (jnp.where(head0, q, 0.0).astype(BF16), jnp.where(head0, 0.0, q).astype(BF16))
        q_heads = (jnp.where(head0, q, 0.0), jnp.where(head0, 0.0, q))
        do_heads = (jnp.where(head0, do_, 0.0), jnp.where(head0, 0.0, do_))
        dos = tuple(d.astype(BF16) for d in do_heads)
        qts = tuple(x.T.astype(BF16) for x in q_heads)
        dots = tuple(d.T.astype(BF16) for d in do_heads)
        lts = (jnp.max(jnp.where(head0, lt, -jnp.inf), axis=1, keepdims=True),
               jnp.max(jnp.where(head0, -jnp.inf, lt), axis=1, keepdims=True))
        r = lax.broadcasted_iota(jnp.int32, (tq, tq), 0)
        c = lax.broadcasted_iota(jnp.int32, (tq, tq), 1)
        upto = (r <= c).astype(BF16)
        before = (r < c).astype(BF16)
        upto2, before2 = _rows([upto, upto]), _rows([before, before])
        causal = lambda rows: c[rows] < r[rows]
        strips = _strips(tq)

        def log_terms(z, valid):
            lbs, his, los, sums = [], [], [], []
            for rows in strips:
                lb, lm = _log_sigmoids(z[rows])
                if valid is not None:
                    lm = jnp.where(valid(rows), lm, 0.0)
                hi, lo = _split_bf16(lm)
                lbs.append(lb)
                his.append(hi)
                los.append(lo)
                sums.append(jnp.sum(lm, axis=1, keepdims=True))
            return lbs, jnp.concatenate([_rows(his), _rows(los)], axis=1), _rows(sums)

        def weights(lbs, rest, lm_upto, da, valid):
            a_parts, es, his, los, sums = [], [], [], [], []
            for rows, lb in zip(strips, lbs):
                a = jnp.exp(lb + (rest[rows] - lm_upto[rows]))
                if valid is not None:
                    a = jnp.where(valid(rows), a, 0.0)
                e = da[rows] * a
                hi, lo = _split_bf16(e)
                a_parts.append(a.astype(BF16))
                es.append(e)
                his.append(hi)
                los.append(lo)
                sums.append(jnp.sum(e, axis=1, keepdims=True))
            return _rows(a_parts), es, jnp.concatenate([_rows(his), _rows(los)], axis=1), _rows(sums)

        def score_grads(lbs, es, run_e, e_before, valid):
            parts = []
            for rows, lb, e in zip(strips, lbs, es):
                beta = jnp.exp(lb)
                dz = e * (1.0 - beta) - (run_e[rows] + e_before[rows]) * beta
                if valid is not None:
                    dz = jnp.where(valid(rows), dz, 0.0)
                parts.append(dz.astype(BF16))
            return _rows(parts)

        def block(kj, carry, valid):
            off = pl.multiple_of(kj * tq, tq)
            kb, kt, vt = kb_scr[pl.ds(off, tq), :], kt_scr[kj], vt_scr[kj]
            run_lm0, run_e0, dq0, run_lm1, run_e1, dq1 = carry
            z0, da0 = _mm(qs[0], kt), _mm(dos[0], vt)
            z1, da1 = _mm(qs[1], kt), _mm(dos[1], vt)
            lbs0, split0, lm_sums0 = log_terms(z0, valid)
            lm_upto0 = _mm(split0, upto2)
            lbs1, split1, lm_sums1 = log_terms(z1, valid)
            lm_upto1 = _mm(split1, upto2)
            a0, es0, split0, e_sums0 = weights(lbs0, lts[0] - run_lm0, lm_upto0, da0, valid)
            e_before0 = _mm(split0, before2)
            a1, es1, split1, e_sums1 = weights(lbs1, lts[1] - run_lm1, lm_upto1, da1, valid)
            e_before1 = _mm(split1, before2)
            dz0 = score_grads(lbs0, es0, run_e0, e_before0, valid)
            dkt_blk = _mm(qts[0], dz0)
            dvt_blk = _mm(dots[0], a0)
            dq0 = dq0 + _mm(dz0, kb)
            dz1 = score_grads(lbs1, es1, run_e1, e_before1, valid)
            dkt_ref[kj] += dkt_blk + _mm(qts[1], dz1)
            dvt_ref[kj] += dvt_blk + _mm(dots[1], a1)
            dq1 = dq1 + _mm(dz1, kb)
            return run_lm0 + lm_sums0, run_e0 + e_sums0, dq0, run_lm1 + lm_sums1, run_e1 + e_sums1, dq1

        zero = (jnp.zeros((tq, 1), F32), jnp.zeros((tq, 1), F32), jnp.zeros((tq, HEAD_PAIR), F32))
        carry = lax.fori_loop(0, qi, lambda kj, cr: block(kj, cr, None), zero + zero)
        carry = block(qi, carry, causal)
        dq_ref[...] = jnp.where(head0, carry[2], carry[5]) * ATTN_SCALE

    blk = pl.BlockSpec((tq, HEAD_PAIR), lambda p, i: (i, p))
    seq = pl.BlockSpec((T // tq, HEAD_PAIR, tq), lambda p, i: (0, p, 0))
    transposed = jax.ShapeDtypeStruct((T // tq, SB_WIDTH, tq), F32)
    return pl.pallas_call(
        body, name=name, grid=(N_HEADS // 2, T // tq),
        in_specs=_attn_specs(T, tq) + [blk, blk, AFTER], out_specs=[blk, seq, seq],
        out_shape=[jax.ShapeDtypeStruct((T, SB_WIDTH), F32), transposed, transposed],
        scratch_shapes=[pltpu.VMEM((T, HEAD_PAIR), BF16), pltpu.VMEM((T // tq, HEAD_PAIR, tq), BF16),
                        pltpu.VMEM((T // tq, HEAD_PAIR, tq), BF16)],
        compiler_params=_params(("arbitrary", "arbitrary")),
    )(proj, proj, proj, do, ltot, _in_hbm(after))


def _branch(act_bf16, w_ref):
    return jnp.concatenate([_mm(act_bf16, w_ref[e]) for e in range(w_ref.shape[0])], axis=1)


def _mix_specs(T, D, tm, wbp, w_out):
    gate_col = (POOL_WIDTH + 3 * SB_WIDTH) // D
    row = lambda i: (i, 0)
    return [
        pl.BlockSpec((tm, D), row),
        pl.BlockSpec((tm, POOL_WIDTH), row),
        pl.BlockSpec((tm, SB_WIDTH), row),
        pl.BlockSpec((tm, D), lambda i: (i, gate_col)),
        pl.BlockSpec((tm, D), lambda i: (i, gate_col + 1)),
        pl.BlockSpec(wbp.shape, lambda i: (0, 0, 0)),
        pl.BlockSpec(wbp.shape, lambda i: (0, 0, 0)),
        pl.BlockSpec(w_out.shape, lambda i: (0, 0)),
    ]


def _mix_fwd(h, p, o, proj, wbp, wba, w_out, *, tm, name):
    T, D = h.shape
    tm = min(tm, T)

    def body(h_ref, p_ref, o_ref, glp_ref, gls_ref, wbp_ref, wba_ref, wout_ref, hout_ref, m_ref):
        yp = _branch(p_ref[...].astype(BF16), wbp_ref)
        ys = _branch(o_ref[...].astype(BF16), wba_ref)
        m = (jax.nn.sigmoid(glp_ref[...]) * yp + jax.nn.sigmoid(gls_ref[...]) * ys).astype(BF16)
        m_ref[...] = m
        hout_ref[...] = h_ref[...] + _mm(m, wout_ref[...])

    row = lambda i: (i, 0)
    return pl.pallas_call(
        body, name=name, grid=(T // tm,),
        in_specs=_mix_specs(T, D, tm, wbp, w_out),
        out_specs=[pl.BlockSpec((tm, D), row), pl.BlockSpec((tm, D), row)],
        out_shape=[jax.ShapeDtypeStruct((T, D), F32), jax.ShapeDtypeStruct((T, D), BF16)],
        compiler_params=_params(("arbitrary",)),
    )(h, p, o, proj, proj, wbp, wba, w_out)


def _mix_bwd(dh, p, o, proj, wbp, wba, w_out, after, *, tm, name):
    T, D = dh.shape
    tm = min(tm, T)
    bw = wbp.shape[2]

    def body(dh_ref, p_ref, o_ref, glp_ref, gls_ref, wbp_ref, wba_ref, wout_ref, after_ref,
             dyp_ref, dys_ref, dp_ref, do_ref, dgl_ref):
        dm = _mm_nt(dh_ref[...].astype(BF16), wout_ref[...])
        yp = _branch(p_ref[...].astype(BF16), wbp_ref)
        ys = _branch(o_ref[...].astype(BF16), wba_ref)
        gp = jax.nn.sigmoid(glp_ref[...])
        gs = jax.nn.sigmoid(gls_ref[...])
        dyp = (dm * gp).astype(BF16)
        dys = (dm * gs).astype(BF16)
        dyp_ref[...] = dyp
        dys_ref[...] = dys
        dgl_ref[:, :D] = (dm * yp * gp * (1.0 - gp)).astype(BF16)
        dgl_ref[:, D:] = (dm * ys * gs * (1.0 - gs)).astype(BF16)
        dp = jnp.zeros(dp_ref.shape, F32)
        do_ = jnp.zeros(do_ref.shape, F32)
        for e in range(wbp_ref.shape[0]):
            dp += _mm_nt(dyp[:, e * bw:(e + 1) * bw], wbp_ref[e])
            do_ += _mm_nt(dys[:, e * bw:(e + 1) * bw], wba_ref[e])
        dp_ref[...] = dp
        do_ref[...] = do_

    row = lambda i: (i, 0)
    return pl.pallas_call(
        body, name=name, grid=(T // tm,),
        in_specs=_mix_specs(T, D, tm, wbp, w_out) + [AFTER],
        out_specs=[pl.BlockSpec((tm, D), row), pl.BlockSpec((tm, D), row), pl.BlockSpec((tm, POOL_WIDTH), row),
                   pl.BlockSpec((tm, SB_WIDTH), row), pl.BlockSpec((tm, 2 * D), row)],
        out_shape=[jax.ShapeDtypeStruct((T, D), BF16), jax.ShapeDtypeStruct((T, D), BF16),
                   jax.ShapeDtypeStruct((T, POOL_WIDTH), F32), jax.ShapeDtypeStruct((T, SB_WIDTH), F32),
                   jax.ShapeDtypeStruct((T, 2 * D), BF16)],
        compiler_params=_params(("arbitrary",)),
    )(dh, p, o, proj, proj, wbp, wba, w_out, _in_hbm(after))


def _adamw(w, g, m, v, *, name):
    R, C = w.shape
    tr = _row_tile(R, C)

    def body(w_ref, g_ref, m_ref, v_ref, d_ref, nm_ref, nv_ref):
        g_ = g_ref[...]
        m_ = ADAM_B1 * m_ref[...] + (1.0 - ADAM_B1) * g_
        v_ = ADAM_B2 * v_ref[...] + (1.0 - ADAM_B2) * (g_ * g_)
        m_hat = m_ / (1.0 - ADAM_B1 ** ADAM_STEP)
        v_hat = v_ / (1.0 - ADAM_B2 ** ADAM_STEP)
        d_ref[...] = -ADAM_LR * (m_hat / (jnp.sqrt(v_hat) + ADAM_EPS) + ADAM_WD * w_ref[...])
        nm_ref[...] = m_
        nv_ref[...] = v_

    spec = pl.BlockSpec((tr, C), lambda i: (i, 0))
    return pl.pallas_call(
        body, name=name, grid=(R // tr,), in_specs=[spec] * 4, out_specs=[spec] * 3,
        out_shape=[jax.ShapeDtypeStruct((R, C), F32)] * 3,
        compiler_params=_params(("arbitrary",)),
    )(w, g, m, v)


def _position():
    return lax.axis_index("x"), lax.axis_index("y"), lax.axis_index("c")


def _all_gather(shards, *, name, collective_id):
    n = len(shards)

    def body(*refs):
        ins, outs = refs[:n], refs[n:2 * n]
        send_sems, recv_sems, local_sems = refs[2 * n:]
        x, y, c = _position()
        me, sibling = (x, y, c), (x, y, 1 - c)
        chips = [(1 - x, y), (x, 1 - y), (1 - x, 1 - y)]

        barrier = pltpu.get_barrier_semaphore()
        for peer in [sibling] + [(*chip, c) for chip in chips]:
            pl.semaphore_signal(barrier, inc=1, device_id=peer, device_id_type=MESH)
        pl.semaphore_wait(barrier, 4)

        def block(a, pos):
            return outs[a].at[4 * pos[0] + 2 * pos[1] + pos[2]]

        def copy(a, k, pos, to, src=None):
            return pltpu.make_async_remote_copy(
                src_ref=block(a, pos) if src is None else src, dst_ref=block(a, pos),
                send_sem=send_sems.at[7 * a + k], recv_sem=recv_sems.at[7 * a + k],
                device_id=to, device_id_type=MESH)

        started = []
        for a in range(n):
            mine = pltpu.make_async_copy(ins[a], block(a, me), local_sems.at[a])
            mine.start()
            started.append(mine)
        sends = []
        for a in range(n):
            sends += [copy(a, 1 + j, me, (*chip, c), src=ins[a]) for j, chip in enumerate(chips)]
            sends.append(copy(a, 0, me, sibling, src=ins[a]))
        for cp in sends:
            cp.start()
        for j, chip in enumerate(chips):
            for a in range(n):
                copy(a, 1 + j, (*chip, c), me).wait_recv()
                passed = copy(a, 4 + j, (*chip, c), sibling)
                passed.start()
                sends.append(passed)
        for a in range(n):
            copy(a, 0, sibling, me).wait_recv()
            for j, chip in enumerate(chips):
                copy(a, 4 + j, (*chip, 1 - c), me).wait_recv()
        for cp in sends:
            cp.wait_send()
        for cp in started:
            cp.wait()

    return pl.kernel(
        body, name=name,
        out_type=[jax.ShapeDtypeStruct((N_DEV,) + s.shape, s.dtype) for s in shards],
        mesh=plsc.ScalarSubcoreMesh(axis_name="sequencer", num_cores=1),
        scratch_types=[pltpu.SemaphoreType.DMA((7 * n,)), pltpu.SemaphoreType.DMA((7 * n,)),
                       pltpu.SemaphoreType.DMA((n,))],
        compiler_params=pltpu.CompilerParams(collective_id=collective_id),
    )(*shards)


def _chip_sums(grads, *, name):
    _, R, C = grads.shape
    rc = 128 if R % 128 == 0 else R

    def body(g_ref, partial, out_ref, mine, theirs, send_sems, recv_sems, local_sems):
        x, y, c = _position()
        my_chip = 2 * x + y

        def swap(s):
            return pltpu.make_async_remote_copy(
                src_ref=g_ref.at[2 * s + (1 - c)], dst_ref=theirs.at[s],
                send_sem=send_sems.at[s], recv_sem=recv_sems.at[s],
                device_id=(x, y, 1 - c), device_id_type=MESH)

        def load(s):
            return pltpu.make_async_copy(g_ref.at[2 * s + c], mine.at[s], local_sems.at[s])

        for s in range(4):
            swap(s).start()
            load(s).start()
        for s in range(4):
            load(s).wait()
            swap(s).wait_recv()

        def chip_sum(chip, rows):
            return mine[chip, rows, :].astype(F32) + theirs[chip, rows, :].astype(F32)

        for j in (1, 2, 3):
            @pl.loop(0, R // rc)
            def _(t):
                rows = pl.ds(pl.multiple_of(t * rc, rc), rc)
                partial[j - 1, rows, :] = chip_sum(my_chip ^ j, rows).astype(BF16)

        @pl.loop(0, R // rc)
        def _(t):
            rows = pl.ds(pl.multiple_of(t * rc, rc), rc)
            out_ref[rows, :] = chip_sum(my_chip, rows)

        for s in range(4):
            swap(s).wait_send()

    vmem = pl.BlockSpec(memory_space=pltpu.VMEM)
    return pl.pallas_call(
        body, name=name,
        in_specs=[pl.BlockSpec(memory_space=pl.ANY)], out_specs=[vmem, vmem],
        out_shape=[jax.ShapeDtypeStruct((3, R, C), BF16), jax.ShapeDtypeStruct((R, C), F32)],
        scratch_shapes=[
            pltpu.VMEM((4, R, C), BF16), pltpu.VMEM((4, R, C), BF16),
            pltpu.SemaphoreType.DMA((4,)), pltpu.SemaphoreType.DMA((4,)), pltpu.SemaphoreType.DMA((4,)),
        ],
        compiler_params=_params(),
    )(grads)


def _cross_chips(partials, *, name, collective_id):
    n = len(partials)

    def body(*refs):
        ins, outs = refs[:n], refs[n:2 * n]
        send_sems, recv_sems = refs[2 * n:]
        x, y, c = _position()
        my_chip = 2 * x + y
        peers = [((my_chip ^ j) // 2, (my_chip ^ j) % 2, c) for j in (1, 2, 3)]

        barrier = pltpu.get_barrier_semaphore()
        for peer in peers:
            pl.semaphore_signal(barrier, inc=1, device_id=peer, device_id_type=MESH)
        pl.semaphore_wait(barrier, 3)

        copies = [
            pltpu.make_async_remote_copy(
                src_ref=ins[a].at[j], dst_ref=outs[a].at[j],
                send_sem=send_sems.at[3 * a + j], recv_sem=recv_sems.at[3 * a + j],
                device_id=peers[j], device_id_type=MESH)
            for a in range(n) for j in range(3)]
        for cp in copies:
            cp.start()
        for cp in copies:
            cp.wait_recv()
        for cp in copies:
            cp.wait_send()

    return pl.kernel(
        body, name=name,
        out_type=[jax.ShapeDtypeStruct(p.shape, p.dtype) for p in partials],
        mesh=plsc.ScalarSubcoreMesh(axis_name="sequencer", num_cores=1),
        scratch_types=[pltpu.SemaphoreType.DMA((3 * n,)), pltpu.SemaphoreType.DMA((3 * n,))],
        compiler_params=pltpu.CompilerParams(collective_id=collective_id),
    )(*partials)


def _cross_chips_and_gather(partial, slab, *, name, collective_id):
    def body(part_ref, slab_ref, landed_ref, slabs_ref, send_sems, recv_sems, local_sem):
        x, y, c = _position()
        me, my_chip = 4 * x + 2 * y + c, 2 * x + y
        others = [me ^ k for k in range(1, N_DEV)]
        ids = [(o // 4, (o // 2) % 2, o % 2) for o in others]

        barrier = pltpu.get_barrier_semaphore()
        for peer in ids:
            pl.semaphore_signal(barrier, inc=1, device_id=peer, device_id_type=MESH)
        pl.semaphore_wait(barrier, N_DEV - 1)

        mine = pltpu.make_async_copy(slab_ref, slabs_ref.at[me], local_sem)
        mine.start()
        sends = [
            pltpu.make_async_remote_copy(
                src_ref=part_ref.at[j], dst_ref=landed_ref.at[j], send_sem=send_sems.at[j], recv_sem=recv_sems.at[j],
                device_id=((my_chip ^ (j + 1)) // 2, (my_chip ^ (j + 1)) % 2, c), device_id_type=MESH)
            for j in range(3)]
        sends += [
            pltpu.make_async_remote_copy(
                src_ref=slab_ref, dst_ref=slabs_ref.at[me], send_sem=send_sems.at[3 + k], recv_sem=recv_sems.at[3 + k],
                device_id=ids[k], device_id_type=MESH)
            for k in range(N_DEV - 1)]
        arrivals = sends[:3] + [
            pltpu.make_async_remote_copy(
                src_ref=slab_ref, dst_ref=slabs_ref.at[others[k]], send_sem=send_sems.at[3 + k],
                recv_sem=recv_sems.at[3 + k], device_id=ids[k], device_id_type=MESH)
            for k in range(N_DEV - 1)]
        for cp in sends:
            cp.start()
        for cp in arrivals:
            cp.wait_recv()
        for cp in sends:
            cp.wait_send()
        mine.wait()

    n_sems = 3 + N_DEV - 1
    return pl.kernel(
        body, name=name,
        out_type=[jax.ShapeDtypeStruct(partial.shape, partial.dtype),
                  jax.ShapeDtypeStruct((N_DEV,) + slab.shape, slab.dtype)],
        mesh=plsc.ScalarSubcoreMesh(axis_name="sequencer", num_cores=1),
        scratch_types=[pltpu.SemaphoreType.DMA((n_sems,)), pltpu.SemaphoreType.DMA((n_sems,)), pltpu.SemaphoreType.DMA],
        compiler_params=pltpu.CompilerParams(collective_id=collective_id),
    )(partial, slab)


def _sum_devices(gathered, after, *, name):
    _, R, C = gathered.shape

    def body(in_ref, after_ref, out_ref):
        total = in_ref[0]
        for d in range(1, N_DEV):
            total = total + in_ref[d]
        out_ref[...] = total

    return pl.pallas_call(
        body, name=name, grid=(1,),
        in_specs=[pl.BlockSpec((N_DEV, R, C), lambda i: (0, 0, 0)), AFTER],
        out_specs=pl.BlockSpec((R, C), lambda i: (0, 0)),
        out_shape=jax.ShapeDtypeStruct((R, C), F32),
        compiler_params=_params(("arbitrary",)),
    )(gathered, _in_hbm(after))


def _owner_sum(own, landed, after, *, name):
    R, C = own.shape
    tr = _row_tile(R, C)

    def body(own_ref, landed_ref, after_ref, out_ref):
        total = own_ref[...]
        for j in range(3):
            total = total + landed_ref[j].astype(F32)
        out_ref[...] = total

    return pl.pallas_call(
        body, name=name, grid=(R // tr,),
        in_specs=[pl.BlockSpec((tr, C), lambda i: (i, 0)), pl.BlockSpec((3, tr, C), lambda i: (0, i, 0)), AFTER],
        out_specs=pl.BlockSpec((tr, C), lambda i: (i, 0)),
        out_shape=jax.ShapeDtypeStruct((R, C), F32),
        compiler_params=_params(("arbitrary",)),
    )(own, landed, _in_hbm(after))


def _local_step(x, target, norms, pool_w_group, pool_scale, wgu1, wd1, w_in, wbp, wba, w_out, wgu2, wd2, exchange):
    n1g, nmg, n2g, nfg = norms
    D = x.shape[1]
    h1, gu1, hid1 = _ffn_fwd(x, n1g, wgu1, wd1, tm=512, name="ffn1_fwd")
    un, proj = _inproj_fwd(h1, nmg, w_in, tm=1024, name="inproj_fwd")
    p = _pool_fwd(proj, pool_w_group, pool_scale, name="pool_fwd")
    o, ltot = _attn_fwd(proj, name="attn_fwd")
    h2, m = _mix_fwd(h1, p, o, proj, wbp, wba, w_out, tm=256, name="mix_fwd")
    h3, gu2, hid2 = _ffn_fwd(h2, n2g, wgu2, wd2, tm=512, name="ffn2_fwd")
    dh3, df2, loss, d_nf = _loss_bwd(h3, target, nfg, tm=256, name="loss_bwd")

    d_wd2 = _wgrad_down(hid2, df2, tk=WGRAD_TOKENS, name="ffn2_wgrad_down")
    (g_wd2,), token = exchange("ffn2_down", [d_wd2.reshape(N_DEV, FF_SHARD_PAD, D)])
    dh2, d_n2, n2, dgu2 = _ffn_bwd(dh3, df2, h2, n2g, gu2, wgu2, wd2, token, tm=256, name="ffn2_bwd")
    d_wgu2 = _wgrad_gate_up(n2, dgu2, tk=WGRAD_TOKENS, name="ffn2_wgrad_gate_up")
    (g_wgu2,), token = exchange("ffn2_gate_up", [d_wgu2])

    dyp, dys, dp, do, dgl = _mix_bwd(dh2, p, o, proj, wbp, wba, w_out, token, tm=256, name="mix_bwd")
    d_wout = _wgrad_full(m, dh2, tk=WGRAD_TOKENS, name="wgrad_out")
    d_wbp = _wgrad_full(p, dyp, tk=WGRAD_TOKENS, name="wgrad_branch_pool", split_lanes=wbp.shape[2])
    d_wba = _wgrad_full(o, dys, tk=WGRAD_TOKENS, name="wgrad_branch_attn", split_lanes=wba.shape[2])
    (g_wbp, g_wba, g_wout), token = exchange("mix", [d_wbp, d_wba, d_wout.reshape(N_DEV, D // N_DEV, D)])
    dxp, d_wgroup, d_scale = _pool_bwd(dp, proj, pool_w_group, pool_scale, name="pool_bwd")
    dq, dkt, dvt = _attn_bwd(proj, do, ltot, token, name="attn_bwd")
    dk, dv = (t.transpose(0, 2, 1).reshape(dq.shape) for t in (dkt, dvt))
    dproj = jnp.concatenate([dxp.astype(BF16), dq.astype(BF16), dk.astype(BF16), dv.astype(BF16), dgl], axis=1)
    d_win = _wgrad_in(un, dproj, tk=WGRAD_TOKENS, name="wgrad_in")
    (g_win,), token_in = exchange("w_in", [d_win])
    dh1, df1, d_nm = _inproj_bwd(dproj, dh2, h1, nmg, w_in, tm=1024, name="inproj_bwd")
    d_wd1 = _wgrad_down(hid1, df1, tk=WGRAD_TOKENS, name="ffn1_wgrad_down")
    (g_wd1,), token_down = exchange("ffn1_down", [d_wd1.reshape(N_DEV, FF_SHARD_PAD, D)])
    token = (token_down[(0,) * token_down.ndim] + token_in[(0,) * token_in.ndim]).reshape(1, 1)

    dx, d_n1, n1, dgu1 = _ffn_bwd(dh1, df1, x, n1g, gu1, wgu1, wd1, token, tm=256, name="ffn1_bwd")
    d_wgu1 = _wgrad_gate_up(n1, dgu1, tk=WGRAD_TOKENS, name="ffn1_wgrad_gate_up")
    (g_wgu1, replicated), token = exchange("last", [d_wgu1, d_n1, d_nm, d_n2, d_nf, d_scale, d_wgroup, loss])

    sharded = (g_wgu1, g_wd1, g_win, g_wbp, g_wba, g_wout, g_wgu2, g_wd2)
    return dx, sharded, replicated, token


def _hidden_major(w):
    return jnp.swapaxes(w[0], 0, 1)


def _pad_gate_up(wt):
    d = wt.shape[1]
    wt = wt.astype(BF16).reshape(2, FF_SHARD, d)
    return jnp.pad(wt, ((0, 0), (0, FF_SHARD_PAD - FF_SHARD), (0, 0))).reshape(2 * FF_SHARD_PAD, d)


def _unpad_gate_up(gt):
    d = gt.shape[1]
    return gt.reshape(2, FF_SHARD_PAD, d)[:, :FF_SHARD].reshape(2 * FF_SHARD, d)


def _pad_down(w):
    return jnp.pad(w.astype(BF16), ((0, FF_SHARD_PAD - FF_SHARD), (0, 0)))


def kernel(x, ffn1_norm, ffn1_w_gate_up, ffn1_w_down, mix_norm, w_in, pool_w_group, pool_scale, w_branch_pool, w_branch_attn, w_out, ffn2_norm, ffn2_w_gate_up, ffn2_w_down, final_norm, loss_target, m_ffn1_norm, m_ffn1_w_gate_up, m_ffn1_w_down, m_mix_norm, m_w_in, m_pool_w_group, m_pool_scale, m_w_branch_pool, m_w_branch_attn, m_w_out, m_ffn2_norm, m_ffn2_w_gate_up, m_ffn2_w_down, m_final_norm, v_ffn1_norm, v_ffn1_w_gate_up, v_ffn1_w_down, v_mix_norm, v_w_in, v_pool_w_group, v_pool_scale, v_w_branch_pool, v_w_branch_attn, v_w_out, v_ffn2_norm, v_ffn2_w_gate_up, v_ffn2_w_down, v_final_norm):
    D = x.shape[-1]
    weights = dict(ffn1_norm=ffn1_norm, ffn1_w_gate_up=ffn1_w_gate_up, ffn1_w_down=ffn1_w_down, mix_norm=mix_norm,
                   w_in=w_in, pool_w_group=pool_w_group, pool_scale=pool_scale, w_branch_pool=w_branch_pool,
                   w_branch_attn=w_branch_attn, w_out=w_out, ffn2_norm=ffn2_norm, ffn2_w_gate_up=ffn2_w_gate_up,
                   ffn2_w_down=ffn2_w_down, final_norm=final_norm)
    first = dict(ffn1_norm=m_ffn1_norm, ffn1_w_gate_up=m_ffn1_w_gate_up, ffn1_w_down=m_ffn1_w_down,
                 mix_norm=m_mix_norm, w_in=m_w_in, pool_w_group=m_pool_w_group, pool_scale=m_pool_scale,
                 w_branch_pool=m_w_branch_pool, w_branch_attn=m_w_branch_attn, w_out=m_w_out,
                 ffn2_norm=m_ffn2_norm, ffn2_w_gate_up=m_ffn2_w_gate_up, ffn2_w_down=m_ffn2_w_down,
                 final_norm=m_final_norm)
    second = dict(ffn1_norm=v_ffn1_norm, ffn1_w_gate_up=v_ffn1_w_gate_up, ffn1_w_down=v_ffn1_w_down,
                  mix_norm=v_mix_norm, w_in=v_w_in, pool_w_group=v_pool_w_group, pool_scale=v_pool_scale,
                  w_branch_pool=v_w_branch_pool, w_branch_attn=v_w_branch_attn, w_out=v_w_out,
                  ffn2_norm=v_ffn2_norm, ffn2_w_gate_up=v_ffn2_w_gate_up, ffn2_w_down=v_ffn2_w_down,
                  final_norm=v_final_norm)
    order = list(weights)

    wgu1, wd1 = _all_gather([_pad_gate_up(_hidden_major(ffn1_w_gate_up)), _pad_down(ffn1_w_down[0])],
                            name="all_gather_ffn1", collective_id=0)
    win_g, = _all_gather([w_in[0].astype(BF16)], name="all_gather_w_in", collective_id=1)
    wbp_g, wba_g, wout_g = _all_gather(
        [w_branch_pool[0].astype(BF16), w_branch_attn[0].astype(BF16), w_out[0].astype(BF16)],
        name="all_gather_mix", collective_id=2)
    wgu2, wd2 = _all_gather([_pad_gate_up(_hidden_major(ffn2_w_gate_up)), _pad_down(ffn2_w_down[0])],
                            name="all_gather_ffn2", collective_id=3)
    wd1 = wd1.reshape(N_DEV * FF_SHARD_PAD, D)
    wd2 = wd2.reshape(N_DEV * FF_SHARD_PAD, D)
    wout_g = wout_g.reshape(D, D)

    cross_ids = {"ffn2_down": 4, "ffn2_gate_up": 5, "mix": 6, "ffn1_down": 7, "w_in": 8, "last": 9}
    small = ["ffn1_norm", "mix_norm", "ffn2_norm", "final_norm", "pool_scale", "pool_w_group"]

    def tile_rows(a):
        a = a.reshape(-1, 128)
        return jnp.pad(a, ((0, -a.shape[0] % 8), (0, 0)))

    def exchange(tag, group):
        if tag == "last":
            slab = jnp.concatenate([tile_rows(g) for g in group[1:-1]] + [jnp.broadcast_to(group[-1], (8, 128))], axis=0)
            partial, own = _chip_sums(group[0], name="chip_sums_last")
            landed, slabs = _cross_chips_and_gather(partial, slab, name="cross_chips_last", collective_id=cross_ids[tag])
            return [(own, landed), slabs], own
        sums = [_chip_sums(g, name=f"chip_sums_{tag}_{i}") for i, g in enumerate(group)]
        landed = _cross_chips([s[0] for s in sums], name="cross_chips_" + tag, collective_id=cross_ids[tag])
        token = sums[0][1] if len(sums) == 1 else sum(s[1][0, 0] for s in sums).reshape(1, 1)
        return [(s[1], l) for s, l in zip(sums, landed)], token

    norms = (ffn1_norm, mix_norm, ffn2_norm, final_norm.reshape(1, D))
    dx, sharded, slabs, last = _local_step(
        x[0], loss_target[0], norms, pool_w_group[0], pool_scale, wgu1, wd1, win_g, wbp_g, wba_g, wout_g, wgu2, wd2,
        exchange)
    names = ["ffn1_w_gate_up", "ffn1_w_down", "w_in", "w_branch_pool", "w_branch_attn", "w_out",
             "ffn2_w_gate_up", "ffn2_w_down"]
    handles = dict(zip(names, sharded))
    grads, delta, new_m, new_v = {}, {}, {}, {}
    after = last
    for k in ("ffn2_w_down", "ffn2_w_gate_up", "w_branch_pool", "w_branch_attn", "w_out", "w_in", "ffn1_w_down",
              "ffn1_w_gate_up"):
        g = _owner_sum(*handles[k], after, name="owner_sum_" + k)
        hidden_major = k.endswith("w_gate_up")
        g = _unpad_gate_up(g) if hidden_major else g[:weights[k].shape[1]]
        view = _hidden_major if hidden_major else (lambda a: a[0])
        back = (lambda a: jnp.swapaxes(a, 0, 1)[None]) if hidden_major else (lambda a: a[None])
        out = _adamw(view(weights[k]), g, view(first[k]), view(second[k]), name="adamw_" + k)
        after = out[0]
        grads[k] = back(g)
        delta[k], new_m[k], new_v[k] = (back(a) for a in out)

    rows = [weights[k].size // 128 for k in small]
    padded_rows = [-(-r // 8) * 8 for r in rows]
    starts = [sum(padded_rows[:i]) for i in range(len(rows) + 1)]
    total = _sum_devices(slabs, after, name="sum_replicated")
    loss_out = total[starts[-1], 0]
    small_w = jnp.concatenate([tile_rows(weights[k]) for k in small], axis=0)
    small_m = jnp.concatenate([tile_rows(first[k]) for k in small], axis=0)
    small_v = jnp.concatenate([tile_rows(second[k]) for k in small], axis=0)
    small_out = _adamw(small_w, total[:starts[-1]], small_m, small_v, name="adamw_replicated")
    for name_, start, n_rows in zip(small, starts, rows):
        shape = weights[name_].shape
        grads[name_] = total[start:start + n_rows].reshape(shape)
        delta[name_], new_m[name_], new_v[name_] = (a[start:start + n_rows].reshape(shape) for a in small_out)

    return (loss_out, dx[None], *[grads[k] for k in order], *[delta[k] for k in order],
            *[new_m[k] for k in order], *[new_v[k] for k in order])
```

```python
import functools

import jax
import jax.numpy as jnp
from jax import lax
from jax.experimental import pallas as pl
from jax.experimental.pallas import tpu as pltpu
from jax.experimental.pallas import tpu_sc as plsc

F32 = jnp.float32
BF16 = jnp.bfloat16
MESH = pl.DeviceIdType.MESH

RMS_EPS = 1e-6
N_DEV = 8
N_HEADS = 8
HEAD_DIM = 64
HEAD_PAIR = 2 * HEAD_DIM
POOL_WINDOWS = (2, 4, 8, 16)
POOL_GROUP = 128
POOL_WIDTH = 512
SB_WIDTH = 512
FF_SHARD = 352
FF_SHARD_PAD = 384
ATTN_BLOCK = 256
ATTN_SCALE = 0.125

ADAM_LR = 0.001
ADAM_B1 = 0.9
ADAM_B2 = 0.999
ADAM_EPS = 1e-08
ADAM_WD = 0.01
ADAM_STEP = 10

VMEM_LIMIT = 48 << 20
WGRAD_TOKENS = 2048


def _params(dims=None):
    return pltpu.CompilerParams(dimension_semantics=dims, vmem_limit_bytes=VMEM_LIMIT)


def _mm(a, b):
    return jnp.dot(a, b, preferred_element_type=F32)


def _mm_nt(a, b):
    return lax.dot_general(a, b, (((1,), (1,)), ((), ())), preferred_element_type=F32)


def _mm_tn(a, b):
    return lax.dot_general(a, b, (((0,), (0,)), ((), ())), preferred_element_type=F32)


def _row_tile(rows, cols):
    limit = max(8, (512 * 1024) // cols)
    return max(t for t in range(8, rows + 1, 8) if rows % t == 0 and (t <= limit or t == 8))


def _rstd(xf):
    return lax.rsqrt(jnp.mean(xf * xf, axis=-1, keepdims=True) + RMS_EPS)


def _rms_bwd(xf, gain, dn):
    r = _rstd(xf)
    xh = xf * r
    dgain = jnp.sum(dn * xh, axis=0, keepdims=True)
    dxh = dn * gain
    dx = r * (dxh - xh * jnp.mean(dxh * xh, axis=-1, keepdims=True))
    return dx, dgain


def _ffn_up(x, gain, wgu, *, tm, name):
    T, D = x.shape
    tm = min(tm, T)
    nb, bw = wgu.shape[0] // 2, wgu.shape[1]

    def body(x_ref, gain_ref, wg_ref, wu_ref, gu_ref, hid_ref, n_scr):
        @pl.when(pl.program_id(1) == 0)
        def _():
            xf = x_ref[...]
            n_scr[...] = (xf * _rstd(xf) * gain_ref[...]).astype(BF16)

        n = n_scr[...]
        g = _mm_nt(n, wg_ref[...])
        u = _mm_nt(n, wu_ref[...])
        gu_ref[0] = g.astype(BF16)
        gu_ref[1] = u.astype(BF16)
        hid_ref[...] = (g * jax.nn.sigmoid(g) * u).astype(BF16)

    return pl.pallas_call(
        body, name=name, grid=(T // tm, nb),
        in_specs=[
            pl.BlockSpec((tm, D), lambda i, j: (i, 0)),
            pl.BlockSpec((1, D), lambda i, j: (0, 0)),
            pl.BlockSpec((None, bw, D), lambda i, j: (j, 0, 0)),
            pl.BlockSpec((None, bw, D), lambda i, j: (j + nb, 0, 0)),
        ],
        out_specs=[
            pl.BlockSpec((2, tm, bw), lambda i, j: (0, i, j)),
            pl.BlockSpec((tm, bw), lambda i, j: (i, j)),
        ],
        out_shape=[jax.ShapeDtypeStruct((2, T, nb * bw), BF16), jax.ShapeDtypeStruct((T, nb * bw), BF16)],
        scratch_shapes=[pltpu.VMEM((tm, D), BF16)],
        compiler_params=_params(("arbitrary", "arbitrary")),
    )(x, gain, wgu, wgu)


def _ffn_down(x, hid, wd, *, tm, name):
    T, D = x.shape
    tm = min(tm, T)
    F = hid.shape[1]

    def body(x_ref, hid_ref, wd_ref, h_ref):
        h_ref[...] = x_ref[...] + 0.5 * _mm(hid_ref[...], wd_ref[...])

    return pl.pallas_call(
        body, name=name, grid=(T // tm,),
        in_specs=[
            pl.BlockSpec((tm, D), lambda i: (i, 0)),
            pl.BlockSpec((tm, F), lambda i: (i, 0)),
            pl.BlockSpec((F, D), lambda i: (0, 0)),
        ],
        out_specs=pl.BlockSpec((tm, D), lambda i: (i, 0)),
        out_shape=jax.ShapeDtypeStruct((T, D), F32),
        compiler_params=_params(("arbitrary",)),
    )(x, hid, wd)


AFTER = pl.BlockSpec(memory_space=pltpu.HBM)


def _in_hbm(token):
    return pltpu.with_memory_space_constraint(token, pltpu.HBM)


def _ffn_bwd(dh, df, x, gain, gu, wgu, wd, after, *, tm, name):
    T, D = x.shape
    tm = min(tm, T)
    nb, bw = wgu.shape[0] // 2, wgu.shape[1]

    def body(dh_ref, df_ref, x_ref, gain_ref, gu_ref, wg_ref, wu_ref, wd_ref, after_ref,
             dx_ref, dgain_ref, n_ref, dgu_ref, dn_acc):
        i, j = pl.program_id(0), pl.program_id(1)

        @pl.when(j == 0)
        def _():
            xf = x_ref[...]
            n_ref[...] = (xf * _rstd(xf) * gain_ref[...]).astype(BF16)
            dn_acc[...] = jnp.zeros_like(dn_acc)

        @pl.when((i == 0) & (j == 0))
        def _():
            dgain_ref[...] = jnp.zeros_like(dgain_ref)

        dhid = _mm_nt(df_ref[...], wd_ref[...])
        g = gu_ref[0].astype(F32)
        u = gu_ref[1].astype(F32)
        s = jax.nn.sigmoid(g)
        silu = g * s
        dg =(dhid * u * (s * (1.0 + g * (1.0 - s)))).astype(BF16)
        du = (dhid * silu).astype(BF16)
        dgu_ref[0] = dg
        dgu_ref[1] = du
        dn_acc[...] += _mm(dg, wg_ref[...]) + _mm(du, wu_ref[...])

        @pl.when(j == nb - 1)
        def _():
            dx, dgain = _rms_bwd(x_ref[...], gain_ref[...], dn_acc[...])
            dx_ref[...] = dh_ref[...] + dx
            dgain_ref[...] += dgain

    row = lambda i, j: (i, 0)
    return pl.pallas_call(
        body, name=name, grid=(T // tm, nb),
        in_specs=[
            pl.BlockSpec((tm, D), row),
            pl.BlockSpec((tm, D), row),
            pl.BlockSpec((tm, D), row),
            pl.BlockSpec((1, D), lambda i, j: (0, 0)),
            pl.BlockSpec((2, tm, bw), lambda i, j: (0, i, j)),
            pl.BlockSpec((None, bw, D), lambda i, j: (j, 0, 0)),
            pl.BlockSpec((None, bw, D), lambda i, j: (j + nb, 0, 0)),
            pl.BlockSpec((bw, D), lambda i, j: (j, 0)),
            AFTER,
        ],
        out_specs=[
            pl.BlockSpec((tm, D), row),
            pl.BlockSpec((1, D), lambda i, j: (0, 0)),
            pl.BlockSpec((tm, D), row),
            pl.BlockSpec((2, tm, bw), lambda i, j: (0, i, j)),
        ],
        out_shape=[
            jax.ShapeDtypeStruct((T, D), F32),
            jax.ShapeDtypeStruct((1, D), F32),
            jax.ShapeDtypeStruct((T, D), BF16),
            jax.ShapeDtypeStruct((2, T, nb * bw), BF16),
        ],
        scratch_shapes=[pltpu.VMEM((tm, D), F32)],
        compiler_params=_params(("arbitrary", "arbitrary")),
    )(dh, df, x, gain, gu, wgu, wgu, wd, _in_hbm(after))


def _wgrad(a, b, *, grid, a_spec, b_spec, out_spec, out_shape, acc_shape, name, split_lanes=0):
    nk = grid[2]

    def body(a_ref, b_ref, o_ref, acc):
        k = pl.program_id(2)

        @pl.when(k == 0)
        def _():
            acc[...] = jnp.zeros_like(acc)

        acc[...] += _mm_tn(a_ref[...].astype(BF16), b_ref[...].astype(BF16))

        @pl.when(k == nk - 1)
        def _():
            if split_lanes:
                for e in range(o_ref.shape[0]):
                    o_ref[e] = acc[:, e * split_lanes:(e + 1) * split_lanes].astype(o_ref.dtype)
            else:
                o_ref[...] = acc[...].astype(o_ref.dtype)

    return pl.pallas_call(
        body, name=name, grid=grid, in_specs=[a_spec, b_spec], out_specs=out_spec,
        out_shape=jax.ShapeDtypeStruct(out_shape, BF16),
        scratch_shapes=[pltpu.VMEM(acc_shape, F32)],
        compiler_params=_params(("arbitrary", "arbitrary", "arbitrary")),
    )(a, b)


def _wgrad_gate_up(n, dgu, *, tk, name):
    T, D = n.shape
    tk = min(tk, T)
    bw = FF_SHARD_PAD * 2
    nb = dgu.shape[2] // bw
    return _wgrad(
        dgu, n, grid=(2 * nb, 1, T // tk), name=name,
        a_spec=pl.BlockSpec((None, tk, bw), lambda m, c, k: (m // nb, k, m % nb)),
        b_spec=pl.BlockSpec((tk, D), lambda m, c, k: (k, 0)),
        out_spec=pl.BlockSpec((None, bw, D), lambda m, c, k: (m, 0, 0)),
        out_shape=(2 * nb, bw, D), acc_shape=(bw, D))


def _wgrad_down(hid, df, *, tk, name):
    T, D = df.shape
    tk = min(tk, T)
    bw = FF_SHARD_PAD * 2
    nb = hid.shape[1] // bw
    return _wgrad(
        hid, df, grid=(nb, 1, T // tk), name=name,
        a_spec=pl.BlockSpec((tk, bw), lambda m, c, k: (k, m)),
        b_spec=pl.BlockSpec((tk, D), lambda m, c, k: (k, 0)),
        out_spec=pl.BlockSpec((bw, D), lambda m, c, k: (m, 0)),
        out_shape=(nb * bw, D), acc_shape=(bw, D))


def _wgrad_in(un, dproj, *, tk, name):
    T, D = un.shape
    tk = min(tk, T)
    bw = dproj.shape[1] // N_DEV
    return _wgrad(
        un, dproj, grid=(1, N_DEV, T // tk), name=name,
        a_spec=pl.BlockSpec((tk, D), lambda m, c, k: (k, 0)),
        b_spec=pl.BlockSpec((tk, bw), lambda m, c, k: (k, c)),
        out_spec=pl.BlockSpec((None, D, bw), lambda m, c, k: (c, 0, 0)),
        out_shape=(N_DEV, D, bw), acc_shape=(D, bw))


def _wgrad_full(a, b, *, tk, name, split_lanes=0):
    T, M = a.shape
    tk = min(tk, T)
    N = b.shape[1]
    if split_lanes:
        out_shape = (N // split_lanes, M, split_lanes)
        out_spec = pl.BlockSpec(out_shape, lambda m, c, k: (0, 0, 0))
    else:
        out_shape = (M, N)
        out_spec = pl.BlockSpec(out_shape, lambda m, c, k: (0, 0))
    return _wgrad(
        a, b, grid=(1, 1, T // tk), name=name,
        a_spec=pl.BlockSpec((tk, M), lambda m, c, k: (k, 0)),
        b_spec=pl.BlockSpec((tk, N), lambda m, c, k: (k, 0)),
        out_spec=out_spec, out_shape=out_shape, acc_shape=(M, N), split_lanes=split_lanes)


def _loss_bwd(h, target, gain, *, tm, name):
    T, D = h.shape
    tm = min(tm, T)

    def body(h_ref, t_ref, gain_ref, dh_ref, df_ref, loss_ref, dgain_ref):
        @pl.when(pl.program_id(0) == 0)
        def _():
            loss_ref[...] = jnp.zeros_like(loss_ref)
            dgain_ref[...] = jnp.zeros_like(dgain_ref)

        xf = h_ref[...]
        gain = gain_ref[...]
        err = xf * _rstd(xf) * gain - t_ref[...]
        loss_ref[...] += 0.5 * jnp.sum(jnp.mean(err * err, axis=-1, keepdims=True), axis=0, keepdims=True)
        dx, dgain = _rms_bwd(xf, gain, err * (1.0 / D))
        dh_ref[...] = dx
        df_ref[...] = (0.5 * dx).astype(BF16)
        dgain_ref[...] += dgain

    row = lambda i: (i, 0)
    fixed = lambda i: (0, 0)
    return pl.pallas_call(
        body, name=name, grid=(T // tm,),
        in_specs=[pl.BlockSpec((tm, D), row), pl.BlockSpec((tm, D), row), pl.BlockSpec((1, D), fixed)],
        out_specs=[pl.BlockSpec((tm, D), row), pl.BlockSpec((tm, D), row), pl.BlockSpec((1, 128), fixed),
                   pl.BlockSpec((1, D), fixed)],
        out_shape=[jax.ShapeDtypeStruct((T, D), F32), jax.ShapeDtypeStruct((T, D), BF16),
                   jax.ShapeDtypeStruct((1, 128), F32), jax.ShapeDtypeStruct((1, D), F32)],
        compiler_params=_params(("arbitrary",)),
    )(h, target, gain)


def _inproj_fwd(h, gain, w_in, *, tm, name):
    T, D = h.shape
    tm = min(tm, T)
    nb, bw = w_in.shape[0], w_in.shape[2]

    def body(h_ref, gain_ref, w_ref, un_ref, proj_ref):
        @pl.when(pl.program_id(1) == 0)
        def _():
            xf = h_ref[...]
            un_ref[...] = (xf * _rstd(xf) * gain_ref[...]).astype(BF16)

        proj_ref[...] = _mm(un_ref[...], w_ref[...])

    return pl.pallas_call(
        body, name=name, grid=(T // tm, nb),
        in_specs=[
            pl.BlockSpec((tm, D), lambda i, j: (i, 0)),
            pl.BlockSpec((1, D), lambda i, j: (0, 0)),
            pl.BlockSpec((None, D, bw), lambda i, j: (j, 0, 0)),
        ],
        out_specs=[pl.BlockSpec((tm, D), lambda i, j: (i, 0)), pl.BlockSpec((tm, bw), lambda i, j: (i, j))],
        out_shape=[jax.ShapeDtypeStruct((T, D), BF16), jax.ShapeDtypeStruct((T, nb * bw), F32)],
        compiler_params=_params(("arbitrary", "arbitrary")),
    )(h, gain, w_in)


def _inproj_bwd(dproj, dh, h, gain, w_in, *, tm, name):
    T, D = h.shape
    tm = min(tm, T)
    nb, bw = w_in.shape[0], w_in.shape[2]

    def body(dp_ref, dh_ref, h_ref, gain_ref, w_ref, dx_ref, df_ref, dgain_ref, acc):
        i, j = pl.program_id(0), pl.program_id(1)

        @pl.when(j == 0)
        def _():
            acc[...] = jnp.zeros_like(acc)

        @pl.when((i == 0) & (j == 0))
        def _():
            dgain_ref[...] = jnp.zeros_like(dgain_ref)

        acc[...] += _mm_nt(dp_ref[...], w_ref[...])

        @pl.when(j == nb - 1)
        def _():
            dx, dgain = _rms_bwd(h_ref[...], gain_ref[...], acc[...])
            dh_in = dh_ref[...] + dx
            dx_ref[...] = dh_in
            df_ref[...] = (0.5 * dh_in).astype(BF16)
            dgain_ref[...] += dgain

    row = lambda i, j: (i, 0)
    return pl.pallas_call(
        body, name=name, grid=(T // tm, nb),
        in_specs=[
            pl.BlockSpec((tm, bw), lambda i, j: (i, j)),
            pl.BlockSpec((tm, D), row),
            pl.BlockSpec((tm, D), row),
            pl.BlockSpec((1, D), lambda i, j: (0, 0)),
            pl.BlockSpec((None, D, bw), lambda i, j: (j, 0, 0)),
        ],
        out_specs=[pl.BlockSpec((tm, D), row), pl.BlockSpec((tm, D), row), pl.BlockSpec((1, D), lambda i, j: (0, 0))],
        out_shape=[jax.ShapeDtypeStruct((T, D), F32), jax.ShapeDtypeStruct((T, D), BF16),
                   jax.ShapeDtypeStruct((1, D), F32)],
        scratch_shapes=[pltpu.VMEM((tm, D), F32)],
        compiler_params=_params(("arbitrary", "arbitrary")),
    )(dproj, dh, h, gain, w_in)


def _window_sum(x, row, doublings, *, backward):
    T = x.shape[0]
    s = x
    for k in range(doublings):
        sh = 1 << k
        if backward:
            s = s + jnp.where(row < T - sh, pltpu.roll(s, T - sh, 0), 0.0)
        else:
            s = s + jnp.where(row >= sh, pltpu.roll(s, sh, 0), 0.0)
    return s


def _pool_fwd(proj, w_group, scale, *, name):
    T = proj.shape[0]

    def body(xp_ref, w_ref, scale_ref, p_ref):
        row = lax.broadcasted_iota(jnp.int32, (T, POOL_GROUP), 0)
        for gi, window in enumerate(POOL_WINDOWS):
            cols = slice(gi * POOL_GROUP, (gi + 1) * POOL_GROUP)
            x = xp_ref[:, cols]
            inv_count = 1.0 / jnp.minimum(row + 1, window).astype(F32)
            yc = _window_sum(x, row, gi + 1, backward=False) * inv_count - x
            pre = _mm(yc.astype(BF16), w_ref[gi].astype(BF16))
            p_ref[:, cols] = pre * scale_ref[:, cols]

    return pl.pallas_call(
        body, name=name, grid=(1,),
        in_specs=[
            pl.BlockSpec((T, POOL_WIDTH), lambda i: (0, 0)),
            pl.BlockSpec(w_group.shape, lambda i: (0, 0, 0)),
            pl.BlockSpec((1, POOL_WIDTH), lambda i: (0, 0)),
        ],
        out_specs=pl.BlockSpec((T, POOL_WIDTH), lambda i: (0, 0)),
        out_shape=jax.ShapeDtypeStruct((T, POOL_WIDTH), F32),
        compiler_params=_params(("arbitrary",)),
    )(proj, w_group, scale)


def _pool_bwd(dp, proj, w_group, scale, *, name):
    T = proj.shape[0]

    def body(dp_ref, xp_ref, w_ref, scale_ref, dxp_ref, dw_ref, dscale_ref):
        row = lax.broadcasted_iota(jnp.int32, (T, POOL_GROUP), 0)
        for gi, window in enumerate(POOL_WINDOWS):
            cols = slice(gi * POOL_GROUP, (gi + 1) * POOL_GROUP)
            x = xp_ref[:, cols]
            inv_count = 1.0 / jnp.minimum(row + 1, window).astype(F32)
            yc = (_window_sum(x, row, gi + 1, backward=False) * inv_count - x).astype(BF16)
            w = w_ref[gi].astype(BF16)
            pre = _mm(yc, w)
            dpg = dp_ref[:, cols]
            dscale_ref[:, cols] = jnp.sum(dpg * pre, axis=0, keepdims=True)
            dpre = (dpg * scale_ref[:, cols]).astype(BF16)
            dw_ref[gi] = _mm_tn(yc, dpre)
            dyc = _mm_nt(dpre, w)
            dxp_ref[:, cols] = _window_sum(dyc * inv_count, row, gi + 1, backward=True) - dyc

    return pl.pallas_call(
        body, name=name, grid=(1,),
        in_specs=[
            pl.BlockSpec((T, POOL_WIDTH), lambda i: (0, 0)),
            pl.BlockSpec((T, POOL_WIDTH), lambda i: (0, 0)),
            pl.BlockSpec(w_group.shape, lambda i: (0, 0, 0)),
            pl.BlockSpec((1, POOL_WIDTH), lambda i: (0, 0)),
        ],
        out_specs=[
            pl.BlockSpec((T, POOL_WIDTH), lambda i: (0, 0)),
            pl.BlockSpec(w_group.shape, lambda i: (0, 0, 0)),
            pl.BlockSpec((1, POOL_WIDTH), lambda i: (0, 0)),
        ],
        out_shape=[jax.ShapeDtypeStruct((T, POOL_WIDTH), F32), jax.ShapeDtypeStruct(w_group.shape, F32),
                   jax.ShapeDtypeStruct((1, POOL_WIDTH), F32)],
        compiler_params=_params(("arbitrary",)),
    )(dp, proj, w_group, scale)


ATTN_STRIP = 32


def _log_sigmoids(z):
    lb = jnp.minimum(z, 0.0) - jnp.log(1.0 + jnp.exp(-jnp.abs(z)))
    return lb, lb - z


def _transposed_blocks(x_ref, blocks_scr, tq):
    for b in range(blocks_scr.shape[0]):
        blocks_scr[b] = x_ref[b * tq:(b + 1) * tq, :].T.astype(BF16)


def _split_bf16(x):
    hi = x.astype(BF16)
    return hi, (x - hi.astype(F32)).astype(BF16)


def _strips(n):
    return [slice(i, i + ATTN_STRIP) for i in range(0, n, ATTN_STRIP)]


def _rows(parts):
    return jnp.concatenate(parts, axis=0)


def _attn_specs(T, tq):
    q_col = POOL_WIDTH // HEAD_PAIR
    k_col = q_col + SB_WIDTH // HEAD_PAIR
    v_col = k_col + SB_WIDTH // HEAD_PAIR
    return [
        pl.BlockSpec((tq, HEAD_PAIR), lambda p, i: (i, q_col + p)),
        pl.BlockSpec((T, HEAD_PAIR), lambda p, i: (0, k_col + p)),
        pl.BlockSpec((T, HEAD_PAIR), lambda p, i: (0, v_col + p)),
    ]


def _attn_fwd(proj, *, name):
    T = proj.shape[0]
    tq = ATTN_BLOCK

    def body(q_ref, k_ref, v_ref, o_ref, lt_ref, kt_scr, vb_scr):
        qi = pl.program_id(1)

        @pl.when(qi == 0)
        def _():
            _transposed_blocks(k_ref, kt_scr, tq)
            vb_scr[...] = v_ref[...].astype(BF16)

        head0 = lax.broadcasted_iota(jnp.int32, (tq, HEAD_PAIR), 1) < HEAD_DIM
        q = q_ref[...] * ATTN_SCALE
        qs = (jnp.where(head0, q, 0.0).astype(BF16), jnp.where(head0, 0.0, q).astype(BF16))
        r = lax.broadcasted_iota(jnp.int32, (tq, tq), 0)
        c = lax.broadcasted_iota(jnp.int32, (tq, tq), 1)
        later = (r > c).astype(BF16)
        later2 = _rows([later, later])
        causal = lambda rows: c[rows] < r[rows]
        strips = _strips(tq)

        def log_terms(z, valid):
            lbs, his, los, sums = [], [], [], []
            for rows in strips:
                lb, lm = _log_sigmoids(z[rows])
                if valid is not None:
                    lm = jnp.where(valid(rows), lm, 0.0)
                hi, lo = _split_bf16(lm)
                lbs.append(lb)
                his.append(hi)
                los.append(lo)
                sums.append(jnp.sum(lm, axis=1, keepdims=True))
            return lbs, jnp.concatenate([_rows(his), _rows(los)], axis=1), _rows(sums)

        def weights(lbs, run, after, valid):
            parts = []
            for rows, lb in zip(strips, lbs):
                a = jnp.exp(lb + run[rows] + after[rows])
                if valid is not None:
                    a = jnp.where(valid(rows), a, 0.0)
                parts.append(a.astype(BF16))
            return _rows(parts)

        def block(kj, carry, valid):
            kt = kt_scr[kj]
            vb = vb_scr[pl.ds(pl.multiple_of(kj * tq, tq), tq), :]
            run0, o0, run1, o1 = carry
            z0 = _mm(qs[0], kt)
            z1 = _mm(qs[1], kt)
            lbs0, split0, sums0 = log_terms(z0, valid)
            after0 = _mm(split0, later2)
            lbs1, split1, sums1 = log_terms(z1, valid)
            after1 = _mm(split1, later2)
            o0 = o0 + _mm(weights(lbs0, run0, after0, valid), vb)
            o1 = o1 + _mm(weights(lbs1, run1, after1, valid), vb)
            return run0 + sums0, o0, run1 + sums1, o1

        zero = (jnp.zeros((tq, 1), F32), jnp.zeros((tq, HEAD_PAIR), F32))
        carry = block(qi, zero + zero, causal)
        carry = lax.fori_loop(0, qi, lambda it, cr: block(qi - 1 - it, cr, None), carry)
        o_ref[...] = jnp.where(head0, carry[1], carry[3])
        lt_ref[...] = jnp.where(head0, carry[0], carry[2])

    out_spec = pl.BlockSpec((tq, HEAD_PAIR), lambda p, i: (i, p))
    return pl.pallas_call(
        body, name=name, grid=(N_HEADS // 2, T // tq),
        in_specs=_attn_specs(T, tq), out_specs=[out_spec, out_spec],
        out_shape=[jax.ShapeDtypeStruct((T, SB_WIDTH), F32), jax.ShapeDtypeStruct((T, SB_WIDTH), F32)],
        scratch_shapes=[pltpu.VMEM((T // tq, HEAD_PAIR, tq), BF16), pltpu.VMEM((T, HEAD_PAIR), BF16)],
        compiler_params=_params(("arbitrary", "arbitrary")),
    )(proj, proj, proj)


def _attn_bwd(proj, do, ltot, after, *, name):
    T = proj.shape[0]
    tq = ATTN_BLOCK

    def body(q_ref, k_ref, v_ref, do_ref, lt_ref, after_ref, dq_ref, dkt_ref, dvt_ref, kb_scr, kt_scr, vt_scr):
        qi = pl.program_id(1)

        @pl.when(qi == 0)
        def _():
            kb_scr[...] = k_ref[...].astype(BF16)
            _transposed_blocks(k_ref, kt_scr, tq)
            _transposed_blocks(v_ref, vt_scr, tq)
            dkt_ref[...] = jnp.zeros_like(dkt_ref)
            dvt_ref[...] = jnp.zeros_like(dvt_ref)

        head0 = lax.broadcasted_iota(jnp.int32, (tq, HEAD_PAIR), 1) < HEAD_DIM
        q, do_, lt = q_ref[...] * ATTN_SCALE, do_ref[...], lt_ref[...]
        qs = (jnp.where(head0, q, 0.0).astype(BF16), jnp.where(head0, 0.0, q).astype(BF16))
        q_heads = (jnp.where(head0, q, 0.0), jnp.where(head0, 0.0, q))
        do_heads = (jnp.where(head0, do_, 0.0), jnp.where(head0, 0.0, do_))
        dos = tuple(d.astype(BF16) for d in do_heads)
        qts = tuple(x.T.astype(BF16) for x in q_heads)
        dots = tuple(d.T.astype(BF16) for d in do_heads)
        lts = (jnp.max(jnp.where(head0, lt, -jnp.inf), axis=1, keepdims=True),
               jnp.max(jnp.where(head0, -jnp.inf, lt), axis=1, keepdims=True))
        r = lax.broadcasted_iota(jnp.int32, (tq, tq), 0)
        c = lax.broadcasted_iota(jnp.int32, (tq, tq), 1)
        upto = (r <= c).astype(BF16)
        before = (r < c).astype(BF16)
        upto2, before2 = _rows([upto, upto]), _rows([before, before])
        causal = lambda rows: c[rows] < r[rows]
        strips = _strips(tq)

        def log_terms(z, valid):
            lbs, his, los, sums = [], [], [], []
            for rows in strips:
                lb, lm = _log_sigmoids(z[rows])
                if valid is not None:
                    lm = jnp.where(valid(rows), lm, 0.0)
                hi, lo = _split_bf16(lm)
                lbs.append(lb)
                his.append(hi)
                los.append(lo)
                sums.append(jnp.sum(lm, axis=1, keepdims=True))
            return lbs, jnp.concatenate([_rows(his), _rows(los)], axis=1), _rows(sums)

        def weights(lbs, rest, lm_upto, da, valid):
            a_parts, es, his, los, sums = [], [], [], [], []
            for rows, lb in zip(strips, lbs):
                a = jnp.exp(lb + (rest[rows] - lm_upto[rows]))
                if valid is not None:
                    a = jnp.where(valid(rows), a, 0.0)
                e = da[rows] * a
                hi, lo = _split_bf16(e)
                a_parts.append(a.astype(BF16))
                es.append(e)
                his.append(hi)
                los.append(lo)
                sums.append(jnp.sum(e, axis=1, keepdims=True))
            return _rows(a_parts), es, jnp.concatenate([_rows(his), _rows(los)], axis=1), _rows(sums)

        def score_grads(lbs, es, run_e, e_before, valid):
            parts = []
            for rows, lb, e in zip(strips, lbs, es):
                beta = jnp.exp(lb)
                dz = e * (1.0 - beta) - (run_e[rows] + e_before[rows]) * beta
                if valid is not None:
                    dz = jnp.where(valid(rows), dz, 0.0)
                parts.append(dz.astype(BF16))
            return _rows(parts)

        def block(kj, carry, valid):
            off = pl.multiple_of(kj * tq, tq)
            kb, kt, vt = kb_scr[pl.ds(off, tq), :], kt_scr[kj], vt_scr[kj]
            run_lm0, run_e0, dq0, run_lm1, run_e1, dq1 = carry
            z0, da0 = _mm(qs[0], kt), _mm(dos[0], vt)
            z1, da1 = _mm(qs[1], kt), _mm(dos[1], vt)
            lbs0, split0, lm_sums0 = log_terms(z0, valid)
            lm_upto0 = _mm(split0, upto2)
            lbs1, split1, lm_sums1 = log_terms(z1, valid)
            lm_upto1 = _mm(split1, upto2)
            a0, es0, split0, e_sums0 = weights(lbs0, lts[0] - run_lm0, lm_upto0, da0, valid)
            e_before0 = _mm(split0, before2)
            a1, es1, split1, e_sums1 = weights(lbs1, lts[1] - run_lm1, lm_upto1, da1, valid)
            e_before1 = _mm(split1, before2)
            dz0 = score_grads(lbs0, es0, run_e0, e_before0, valid)
            dkt_blk = _mm(qts[0], dz0)
            dvt_blk = _mm(dots[0], a0)
            dq0 = dq0 + _mm(dz0, kb)
            dz1 = score_grads(lbs1, es1, run_e1, e_before1, valid)
            dkt_ref[kj] += dkt_blk + _mm(qts[1], dz1)
            dvt_ref[kj] += dvt_blk + _mm(dots[1], a1)
            dq1 = dq1 + _mm(dz1, kb)
            return run_lm0 + lm_sums0, run_e0 + e_sums0, dq0, run_lm1 + lm_sums1, run_e1 + e_sums1, dq1

        zero = (jnp.zeros((tq, 1), F32), jnp.zeros((tq, 1), F32), jnp.zeros((tq, HEAD_PAIR), F32))
        carry = lax.fori_loop(0, qi, lambda kj, cr: block(kj, cr, None), zero + zero)
        carry = block(qi, carry, causal)
        dq_ref[...] = jnp.where(head0, carry[2], carry[5]) * ATTN_SCALE

    blk = pl.BlockSpec((tq, HEAD_PAIR), lambda p, i: (i, p))
    seq = pl.BlockSpec((T // tq, HEAD_PAIR, tq), lambda p, i: (0, p, 0))
    transposed = jax.ShapeDtypeStruct((T // tq, SB_WIDTH, tq), F32)
    return pl.pallas_call(
        body, name=name, grid=(N_HEADS // 2, T // tq),
        in_specs=_attn_specs(T, tq) + [blk, blk, AFTER], out_specs=[blk, seq, seq],
        out_shape=[jax.ShapeDtypeStruct((T, SB_WIDTH), F32), transposed, transposed],
        scratch_shapes=[pltpu.VMEM((T, HEAD_PAIR), BF16), pltpu.VMEM((T // tq, HEAD_PAIR, tq), BF16),
                        pltpu.VMEM((T // tq, HEAD_PAIR, tq), BF16)],
        compiler_params=_params(("arbitrary", "arbitrary")),
    )(proj, proj, proj, do, ltot, _in_hbm(after))


def _branch(act_bf16, w_ref):
    return jnp.concatenate([_mm(act_bf16, w_ref[e]) for e in range(w_ref.shape[0])], axis=1)


def _mix_specs(T, D, tm, wbp, w_out):
    gate_col = (POOL_WIDTH + 3 * SB_WIDTH) // D
    row = lambda i: (i, 0)
    return [
        pl.BlockSpec((tm, D), row),
        pl.BlockSpec((tm, POOL_WIDTH), row),
        pl.BlockSpec((tm, SB_WIDTH), row),
        pl.BlockSpec((tm, D), lambda i: (i, gate_col)),
        pl.BlockSpec((tm, D), lambda i: (i, gate_col + 1)),
        pl.BlockSpec(wbp.shape, lambda i: (0, 0, 0)),
        pl.BlockSpec(wbp.shape, lambda i: (0, 0, 0)),
        pl.BlockSpec(w_out.shape, lambda i: (0, 0)),
    ]


def _mix_fwd(h, p, o, proj, wbp, wba, w_out, *, tm, name):
    T, D = h.shape
    tm = min(tm, T)

    def body(h_ref, p_ref, o_ref, glp_ref, gls_ref, wbp_ref, wba_ref, wout_ref, hout_ref, m_ref):
        yp = _branch(p_ref[...].astype(BF16), wbp_ref)
        ys = _branch(o_ref[...].astype(BF16), wba_ref)
        m = (jax.nn.sigmoid(glp_ref[...]) * yp + jax.nn.sigmoid(gls_ref[...]) * ys).astype(BF16)
        m_ref[...] = m
        hout_ref[...] = h_ref[...] + _mm(m, wout_ref[...])

    row = lambda i: (i, 0)
    return pl.pallas_call(
        body, name=name, grid=(T // tm,),
        in_specs=_mix_specs(T, D, tm, wbp, w_out),
        out_specs=[pl.BlockSpec((tm, D), row), pl.BlockSpec((tm, D), row)],
        out_shape=[jax.ShapeDtypeStruct((T, D), F32), jax.ShapeDtypeStruct((T, D), BF16)],
        compiler_params=_params(("arbitrary",)),
    )(h, p, o, proj, proj, wbp, wba, w_out)


def _mix_bwd(dh, p, o, proj, wbp, wba, w_out, after, *, tm, name):
    T, D = dh.shape
    tm = min(tm, T)
    bw = wbp.shape[2]

    def body(dh_ref, p_ref, o_ref, glp_ref, gls_ref, wbp_ref, wba_ref, wout_ref, after_ref,
             dyp_ref, dys_ref, dp_ref, do_ref, dgl_ref):
        dm = _mm_nt(dh_ref[...].astype(BF16), wout_ref[...])
        yp = _branch(p_ref[...].astype(BF16), wbp_ref)
        ys = _branch(o_ref[...].astype(BF16), wba_ref)
        gp = jax.nn.sigmoid(glp_ref[...])
        gs = jax.nn.sigmoid(gls_ref[...])
        dyp = (dm * gp).astype(BF16)
        dys = (dm * gs).astype(BF16)
        dyp_ref[...] = dyp
        dys_ref[...] = dys
        dgl_ref[:, :D] = (dm * yp * gp * (1.0 - gp)).astype(BF16)
        dgl_ref[:, D:] = (dm * ys * gs * (1.0 - gs)).astype(BF16)
        dp = jnp.zeros(dp_ref.shape, F32)
        do_ = jnp.zeros(do_ref.shape, F32)
        for e in range(wbp_ref.shape[0]):
            dp += _mm_nt(dyp[:, e * bw:(e + 1) * bw], wbp_ref[e])
            do_ += _mm_nt(dys[:, e * bw:(e + 1) * bw], wba_ref[e])
        dp_ref[...] = dp
        do_ref[...] = do_

    row = lambda i: (i, 0)
    return pl.pallas_call(
        body, name=name, grid=(T // tm,),
        in_specs=_mix_specs(T, D, tm, wbp, w_out) + [AFTER],
        out_specs=[pl.BlockSpec((tm, D), row), pl.BlockSpec((tm, D), row), pl.BlockSpec((tm, POOL_WIDTH), row),
                   pl.BlockSpec((tm, SB_WIDTH), row), pl.BlockSpec((tm, 2 * D), row)],
        out_shape=[jax.ShapeDtypeStruct((T, D), BF16), jax.ShapeDtypeStruct((T, D), BF16),
                   jax.ShapeDtypeStruct((T, POOL_WIDTH), F32), jax.ShapeDtypeStruct((T, SB_WIDTH), F32),
                   jax.ShapeDtypeStruct((T, 2 * D), BF16)],
        compiler_params=_params(("arbitrary",)),
    )(dh, p, o, proj, proj, wbp, wba, w_out, _in_hbm(after))


def _adamw(w, g, m, v, *, name):
    R, C = w.shape
    tr = _row_tile(R, C)

    def body(w_ref, g_ref, m_ref, v_ref, d_ref, nm_ref, nv_ref):
        g_ = g_ref[...]
        m_ = ADAM_B1 * m_ref[...] + (1.0 - ADAM_B1) * g_
        v_ = ADAM_B2 * v_ref[...] + (1.0 - ADAM_B2) * (g_ * g_)
        m_hat = m_ / (1.0 - ADAM_B1 ** ADAM_STEP)
        v_hat = v_ / (1.0 - ADAM_B2 ** ADAM_STEP)
        d_ref[...] = -ADAM_LR * (m_hat / (jnp.sqrt(v_hat) + ADAM_EPS) + ADAM_WD * w_ref[...])
        nm_ref[...] = m_
        nv_ref[...] = v_

    spec = pl.BlockSpec((tr, C), lambda i: (i, 0))
    return pl.pallas_call(
        body, name=name, grid=(R // tr,), in_specs=[spec] * 4, out_specs=[spec] * 3,
        out_shape=[jax.ShapeDtypeStruct((R, C), F32)] * 3,
        compiler_params=_params(("arbitrary",)),
    )(w, g, m, v)


def _position():
    return lax.axis_index("x"), lax.axis_index("y"), lax.axis_index("c")


def _all_gather(shards, *, name, collective_id):
    n = len(shards)

    def body(*refs):
        ins, outs = refs[:n], refs[n:2 * n]
        send_sems, recv_sems, local_sems = refs[2 * n:]
        x, y, c = _position()
        me, sibling = (x, y, c), (x, y, 1 - c)
        chips = [(1 - x, y), (x, 1 - y), (1 - x, 1 - y)]

        barrier = pltpu.get_barrier_semaphore()
        for peer in [sibling] + [(*chip, c) for chip in chips]:
            pl.semaphore_signal(barrier, inc=1, device_id=peer, device_id_type=MESH)
        pl.semaphore_wait(barrier, 4)

        def block(a, pos):
            return outs[a].at[4 * pos[0] + 2 * pos[1] + pos[2]]

        def copy(a, k, pos, to, src=None):
            return pltpu.make_async_remote_copy(
                src_ref=block(a, pos) if src is None else src, dst_ref=block(a, pos),
                send_sem=send_sems.at[7 * a + k], recv_sem=recv_sems.at[7 * a + k],
                device_id=to, device_id_type=MESH)

        started = []
        for a in range(n):
            mine = pltpu.make_async_copy(ins[a], block(a, me), local_sems.at[a])
            mine.start()
            started.append(mine)
        sends = []
        for a in range(n):
            sends += [copy(a, 1 + j, me, (*chip, c), src=ins[a]) for j, chip in enumerate(chips)]
            sends.append(copy(a, 0, me, sibling, src=ins[a]))
        for cp in sends:
            cp.start()
        for j, chip in enumerate(chips):
            for a in range(n):
                copy(a, 1 + j, (*chip, c), me).wait_recv()
                passed = copy(a, 4 + j, (*chip, c), sibling)
                passed.start()
                sends.append(passed)
        for a in range(n):
            copy(a, 0, sibling, me).wait_recv()
            for j, chip in enumerate(chips):
                copy(a, 4 + j, (*chip, 1 - c), me).wait_recv()
        for cp in sends:
            cp.wait_send()
        for cp in started:
            cp.wait()

    return pl.kernel(
        body, name=name,
        out_type=[jax.ShapeDtypeStruct((N_DEV,) + s.shape, s.dtype) for s in shards],
        mesh=plsc.ScalarSubcoreMesh(axis_name="sequencer", num_cores=1),
        scratch_types=[pltpu.SemaphoreType.DMA((7 * n,)), pltpu.SemaphoreType.DMA((7 * n,)),
                       pltpu.SemaphoreType.DMA((n,))],
        compiler_params=pltpu.CompilerParams(collective_id=collective_id),
    )(*shards)


def _chip_sums(grads, *, name):
    _, R, C = grads.shape
    rc = 128 if R % 128 == 0 else R

    def body(g_ref, partial, out_ref, mine, theirs, send_sems, recv_sems, local_sems):
        x, y, c = _position()
        my_chip = 2 * x + y

        def swap(s):
            return pltpu.make_async_remote_copy(
                src_ref=g_ref.at[2 * s + (1 - c)], dst_ref=theirs.at[s],
                send_sem=send_sems.at[s], recv_sem=recv_sems.at[s],
                device_id=(x, y, 1 - c), device_id_type=MESH)

        def load(s):
            return pltpu.make_async_copy(g_ref.at[2 * s + c], mine.at[s], local_sems.at[s])

        for s in range(4):
            swap(s).start()
            load(s).start()
        for s in range(4):
            load(s).wait()
            swap(s).wait_recv()

        def chip_sum(chip, rows):
            return mine[chip, rows, :].astype(F32) + theirs[chip, rows, :].astype(F32)

        for j in (1, 2, 3):
            @pl.loop(0, R // rc)
            def _(t):
                rows = pl.ds(pl.multiple_of(t * rc, rc), rc)
                partial[j - 1, rows, :] = chip_sum(my_chip ^ j, rows).astype(BF16)

        @pl.loop(0, R // rc)
        def _(t):
            rows = pl.ds(pl.multiple_of(t * rc, rc), rc)
            out_ref[rows, :] = chip_sum(my_chip, rows)

        for s in range(4):
            swap(s).wait_send()

    vmem = pl.BlockSpec(memory_space=pltpu.VMEM)
    return pl.pallas_call(
        body, name=name,
        in_specs=[pl.BlockSpec(memory_space=pl.ANY)], out_specs=[vmem, vmem],
        out_shape=[jax.ShapeDtypeStruct((3, R, C), BF16), jax.ShapeDtypeStruct((R, C), F32)],
        scratch_shapes=[
            pltpu.VMEM((4, R, C), BF16), pltpu.VMEM((4, R, C), BF16),
            pltpu.SemaphoreType.DMA((4,)), pltpu.SemaphoreType.DMA((4,)), pltpu.SemaphoreType.DMA((4,)),
        ],
        compiler_params=_params(),
    )(grads)


def _cross_chips(partials, *, name, collective_id):
    n = len(partials)

    def body(*refs):
        ins, outs = refs[:n], refs[n:2 * n]
        send_sems, recv_sems = refs[2 * n:]
        x, y, c = _position()
        my_chip = 2 * x + y
        peers = [((my_chip ^ j) // 2, (my_chip ^ j) % 2, c) for j in (1, 2, 3)]

        barrier = pltpu.get_barrier_semaphore()
        for peer in peers:
            pl.semaphore_signal(barrier, inc=1, device_id=peer, device_id_type=MESH)
        pl.semaphore_wait(barrier, 3)

        copies = [
            pltpu.make_async_remote_copy(
                src_ref=ins[a].at[j], dst_ref=outs[a].at[j],
                send_sem=send_sems.at[3 * a + j], recv_sem=recv_sems.at[3 * a + j],
                device_id=peers[j], device_id_type=MESH)
            for a in range(n) for j in range(3)]
        for cp in copies:
            cp.start()
        for cp in copies:
            cp.wait_recv()
        for cp in copies:
            cp.wait_send()

    return pl.kernel(
        body, name=name,
        out_type=[jax.ShapeDtypeStruct(p.shape, p.dtype) for p in partials],
        mesh=plsc.ScalarSubcoreMesh(axis_name="sequencer", num_cores=1),
        scratch_types=[pltpu.SemaphoreType.DMA((3 * n,)), pltpu.SemaphoreType.DMA((3 * n,))],
        compiler_params=pltpu.CompilerParams(collective_id=collective_id),
    )(*partials)


def _cross_chips_and_gather(partial, slab, *, name, collective_id):
    def body(part_ref, slab_ref, landed_ref, slabs_ref, send_sems, recv_sems, local_sem):
        x, y, c = _position()
        me, my_chip = 4 * x + 2 * y + c, 2 * x + y
        others = [me ^ k for k in range(1, N_DEV)]
        ids = [(o // 4, (o // 2) % 2, o % 2) for o in others]

        barrier = pltpu.get_barrier_semaphore()
        for peer in ids:
            pl.semaphore_signal(barrier, inc=1, device_id=peer, device_id_type=MESH)
        pl.semaphore_wait(barrier, N_DEV - 1)

        mine = pltpu.make_async_copy(slab_ref, slabs_ref.at[me], local_sem)
        mine.start()
        sends = [
            pltpu.make_async_remote_copy(
                src_ref=part_ref.at[j], dst_ref=landed_ref.at[j], send_sem=send_sems.at[j], recv_sem=recv_sems.at[j],
                device_id=((my_chip ^ (j + 1)) // 2, (my_chip ^ (j + 1)) % 2, c), device_id_type=MESH)
            for j in range(3)]
        sends += [
            pltpu.make_async_remote_copy(
                src_ref=slab_ref, dst_ref=slabs_ref.at[me], send_sem=send_sems.at[3 + k], recv_sem=recv_sems.at[3 + k],
                device_id=ids[k], device_id_type=MESH)
            for k in range(N_DEV - 1)]
        arrivals = sends[:3] + [
            pltpu.make_async_remote_copy(
                src_ref=slab_ref, dst_ref=slabs_ref.at[others[k]], send_sem=send_sems.at[3 + k],
                recv_sem=recv_sems.at[3 + k], device_id=ids[k], device_id_type=MESH)
            for k in range(N_DEV - 1)]
        for cp in sends:
            cp.start()
        for cp in arrivals:
            cp.wait_recv()
        for cp in sends:
            cp.wait_send()
        mine.wait()

    n_sems = 3 + N_DEV - 1
    return pl.kernel(
        body, name=name,
        out_type=[jax.ShapeDtypeStruct(partial.shape, partial.dtype),
                  jax.ShapeDtypeStruct((N_DEV,) + slab.shape, slab.dtype)],
        mesh=plsc.ScalarSubcoreMesh(axis_name="sequencer", num_cores=1),
        scratch_types=[pltpu.SemaphoreType.DMA((n_sems,)), pltpu.SemaphoreType.DMA((n_sems,)), pltpu.SemaphoreType.DMA],
        compiler_params=pltpu.CompilerParams(collective_id=collective_id),
    )(partial, slab)


def _sum_devices(gathered, after, *, name):
    _, R, C = gathered.shape

    def body(in_ref, after_ref, out_ref):
        total = in_ref[0]
        for d in range(1, N_DEV):
            total = total + in_ref[d]
        out_ref[...] = total

    return pl.pallas_call(
        body, name=name, grid=(1,),
        in_specs=[pl.BlockSpec((N_DEV, R, C), lambda i: (0, 0, 0)), AFTER],
        out_specs=pl.BlockSpec((R, C), lambda i: (0, 0)),
        out_shape=jax.ShapeDtypeStruct((R, C), F32),
        compiler_params=_params(("arbitrary",)),
    )(gathered, _in_hbm(after))


def _owner_sum(own, landed, after, *, name):
    R, C = own.shape
    tr = _row_tile(R, C)

    def body(own_ref, landed_ref, after_ref, out_ref):
        total = own_ref[...]
        for j in range(3):
            total = total + landed_ref[j].astype(F32)
        out_ref[...] = total

    return pl.pallas_call(
        body, name=name, grid=(R // tr,),
        in_specs=[pl.BlockSpec((tr, C), lambda i: (i, 0)), pl.BlockSpec((3, tr, C), lambda i: (0, i, 0)), AFTER],
        out_specs=pl.BlockSpec((tr, C), lambda i: (i, 0)),
        out_shape=jax.ShapeDtypeStruct((R, C), F32),
        compiler_params=_params(("arbitrary",)),
    )(own, landed, _in_hbm(after))


def _local_step(x, target, norms, pool_w_group, pool_scale, wgu1, wd1, w_in, wbp, wba, w_out, wgu2, wd2, exchange):
    n1g, nmg, n2g, nfg = norms
    D = x.shape[1]
    gu1, hid1 = _ffn_up(x, n1g, wgu1, tm=512, name="ffn1_up")
    h1 = _ffn_down(x, hid1, wd1, tm=512, name="ffn1_down")
    un, proj = _inproj_fwd(h1, nmg, w_in, tm=1024, name="inproj_fwd")
    p = _pool_fwd(proj, pool_w_group, pool_scale, name="pool_fwd")
    o, ltot = _attn_fwd(proj, name="attn_fwd")
    h2, m = _mix_fwd(h1, p, o, proj, wbp, wba, w_out, tm=256, name="mix_fwd")
    gu2, hid2 = _ffn_up(h2, n2g, wgu2, tm=512, name="ffn2_up")
    h3 = _ffn_down(h2, hid2, wd2, tm=512, name="ffn2_down")
    dh3, df2, loss, d_nf = _loss_bwd(h3, target, nfg, tm=256, name="loss_bwd")

    d_wd2 = _wgrad_down(hid2, df2, tk=WGRAD_TOKENS, name="ffn2_wgrad_down")
    (g_wd2,), token = exchange("ffn2_down", [d_wd2.reshape(N_DEV, FF_SHARD_PAD, D)])
    dh2, d_n2, n2, dgu2 = _ffn_bwd(dh3, df2, h2, n2g, gu2, wgu2, wd2, token, tm=256, name="ffn2_bwd")
    d_wgu2 = _wgrad_gate_up(n2, dgu2, tk=WGRAD_TOKENS, name="ffn2_wgrad_gate_up")
    (g_wgu2,), token = exchange("ffn2_gate_up", [d_wgu2])

    dyp, dys, dp, do, dgl = _mix_bwd(dh2, p, o, proj, wbp, wba, w_out, token, tm=256, name="mix_bwd")
    d_wout = _wgrad_full(m, dh2, tk=WGRAD_TOKENS, name="wgrad_out")
    d_wbp = _wgrad_full(p, dyp, tk=WGRAD_TOKENS, name="wgrad_branch_pool", split_lanes=wbp.shape[2])
    d_wba = _wgrad_full(o, dys, tk=WGRAD_TOKENS, name="wgrad_branch_attn", split_lanes=wba.shape[2])
    (g_wbp, g_wba, g_wout), token = exchange("mix", [d_wbp, d_wba, d_wout.reshape(N_DEV, D // N_DEV, D)])
    dxp, d_wgroup, d_scale = _pool_bwd(dp, proj, pool_w_group, pool_scale, name="pool_bwd")
    dq, dkt, dvt = _attn_bwd(proj, do, ltot, token, name="attn_bwd")
    dk, dv = (t.transpose(0, 2, 1).reshape(dq.shape) for t in (dkt, dvt))
    dproj = jnp.concatenate([dxp.astype(BF16), dq.astype(BF16), dk.astype(BF16), dv.astype(BF16), dgl], axis=1)
    d_win = _wgrad_in(un, dproj, tk=WGRAD_TOKENS, name="wgrad_in")
    (g_win,), token_in = exchange("w_in", [d_win])
    dh1, df1, d_nm = _inproj_bwd(dproj, dh2, h1, nmg, w_in, tm=1024, name="inproj_bwd")
    d_wd1 = _wgrad_down(hid1, df1, tk=WGRAD_TOKENS, name="ffn1_wgrad_down")
    (g_wd1,), token_down = exchange("ffn1_down", [d_wd1.reshape(N_DEV, FF_SHARD_PAD, D)])
    token = (token_down[(0,) * token_down.ndim] + token_in[(0,) * token_in.ndim]).reshape(1, 1)

    dx, d_n1, n1, dgu1 = _ffn_bwd(dh1, df1, x, n1g, gu1, wgu1, wd1, token, tm=256, name="ffn1_bwd")
    d_wgu1 = _wgrad_gate_up(n1, dgu1, tk=WGRAD_TOKENS, name="ffn1_wgrad_gate_up")
    (g_wgu1, replicated), token = exchange("last", [d_wgu1, d_n1, d_nm, d_n2, d_nf, d_scale, d_wgroup, loss])

    sharded = (g_wgu1, g_wd1, g_win, g_wbp, g_wba, g_wout, g_wgu2, g_wd2)
    return dx, sharded, replicated, token


def _hidden_major(w):
    return jnp.swapaxes(w[0], 0, 1)


def _pad_gate_up(wt):
    d = wt.shape[1]
    wt = wt.astype(BF16).reshape(2, FF_SHARD, d)
    return jnp.pad(wt, ((0, 0), (0, FF_SHARD_PAD - FF_SHARD), (0, 0))).reshape(2 * FF_SHARD_PAD, d)


def _unpad_gate_up(gt):
    d = gt.shape[1]
    return gt.reshape(2, FF_SHARD_PAD, d)[:, :FF_SHARD].reshape(2 * FF_SHARD, d)


def _pad_down(w):
    return jnp.pad(w.astype(BF16), ((0, FF_SHARD_PAD - FF_SHARD), (0, 0)))


def kernel(x, ffn1_norm, ffn1_w_gate_up, ffn1_w_down, mix_norm, w_in, pool_w_group, pool_scale, w_branch_pool, w_branch_attn, w_out, ffn2_norm, ffn2_w_gate_up, ffn2_w_down, final_norm, loss_target, m_ffn1_norm, m_ffn1_w_gate_up, m_ffn1_w_down, m_mix_norm, m_w_in, m_pool_w_group, m_pool_scale, m_w_branch_pool, m_w_branch_attn, m_w_out, m_ffn2_norm, m_ffn2_w_gate_up, m_ffn2_w_down, m_final_norm, v_ffn1_norm, v_ffn1_w_gate_up, v_ffn1_w_down, v_mix_norm, v_w_in, v_pool_w_group, v_pool_scale, v_w_branch_pool, v_w_branch_attn, v_w_out, v_ffn2_norm, v_ffn2_w_gate_up, v_ffn2_w_down, v_final_norm):
    D = x.shape[-1]
    weights = dict(ffn1_norm=ffn1_norm, ffn1_w_gate_up=ffn1_w_gate_up, ffn1_w_down=ffn1_w_down, mix_norm=mix_norm,
                   w_in=w_in, pool_w_group=pool_w_group, pool_scale=pool_scale, w_branch_pool=w_branch_pool,
                   w_branch_attn=w_branch_attn, w_out=w_out, ffn2_norm=ffn2_norm, ffn2_w_gate_up=ffn2_w_gate_up,
                   ffn2_w_down=ffn2_w_down, final_norm=final_norm)
    first = dict(ffn1_norm=m_ffn1_norm, ffn1_w_gate_up=m_ffn1_w_gate_up, ffn1_w_down=m_ffn1_w_down,
                 mix_norm=m_mix_norm, w_in=m_w_in, pool_w_group=m_pool_w_group, pool_scale=m_pool_scale,
                 w_branch_pool=m_w_branch_pool, w_branch_attn=m_w_branch_attn, w_out=m_w_out,
                 ffn2_norm=m_ffn2_norm, ffn2_w_gate_up=m_ffn2_w_gate_up, ffn2_w_down=m_ffn2_w_down,
                 final_norm=m_final_norm)
    second = dict(ffn1_norm=v_ffn1_norm, ffn1_w_gate_up=v_ffn1_w_gate_up, ffn1_w_down=v_ffn1_w_down,
                  mix_norm=v_mix_norm, w_in=v_w_in, pool_w_group=v_pool_w_group, pool_scale=v_pool_scale,
                  w_branch_pool=v_w_branch_pool, w_branch_attn=v_w_branch_attn, w_out=v_w_out,
                  ffn2_norm=v_ffn2_norm, ffn2_w_gate_up=v_ffn2_w_gate_up, ffn2_w_down=v_ffn2_w_down,
                  final_norm=v_final_norm)
    order = list(weights)

    wgu1, = _all_gather([_pad_gate_up(_hidden_major(ffn1_w_gate_up))], name="all_gather_ffn1_gate_up", collective_id=0)
    wd1, = _all_gather([_pad_down(ffn1_w_down[0])], name="all_gather_ffn1_down", collective_id=10)
    win_g, = _all_gather([w_in[0].astype(BF16)], name="all_gather_w_in", collective_id=1)
    wbp_g, wba_g, wout_g = _all_gather(
        [w_branch_pool[0].astype(BF16), w_branch_attn[0].astype(BF16), w_out[0].astype(BF16)],
        name="all_gather_mix", collective_id=2)
    wgu2, wd2 = _all_gather([_pad_gate_up(_hidden_major(ffn2_w_gate_up)), _pad_down(ffn2_w_down[0])],
                            name="all_gather_ffn2", collective_id=3)
    wd1 = wd1.reshape(N_DEV * FF_SHARD_PAD, D)
    wd2 = wd2.reshape(N_DEV * FF_SHARD_PAD, D)
    wout_g = wout_g.reshape(D, D)

    cross_ids = {"ffn2_down": 4, "ffn2_gate_up": 5, "mix": 6, "ffn1_down": 7, "w_in": 8, "last": 9}
    small = ["ffn1_norm", "mix_norm", "ffn2_norm", "final_norm", "pool_scale", "pool_w_group"]

    def tile_rows(a):
        a = a.reshape(-1, 128)
        return jnp.pad(a, ((0, -a.shape[0] % 8), (0, 0)))

    def exchange(tag, group):
        if tag == "last":
            slab = jnp.concatenate([tile_rows(g) for g in group[1:-1]] + [jnp.broadcast_to(group[-1], (8, 128))], axis=0)
            partial, own = _chip_sums(group[0], name="chip_sums_last")
            landed, slabs = _cross_chips_and_gather(partial, slab, name="cross_chips_last", collective_id=cross_ids[tag])
            return [(own, landed), slabs], own
        sums = [_chip_sums(g, name=f"chip_sums_{tag}_{i}") for i, g in enumerate(group)]
        landed = _cross_chips([s[0] for s in sums], name="cross_chips_" + tag, collective_id=cross_ids[tag])
        token = sums[0][1] if len(sums) == 1 else sum(s[1][0, 0] for s in sums).reshape(1, 1)
        return [(s[1], l) for s, l in zip(sums, landed)], token

    norms = (ffn1_norm, mix_norm, ffn2_norm, final_norm.reshape(1, D))
    dx, sharded, slabs, last = _local_step(
        x[0], loss_target[0], norms, pool_w_group[0], pool_scale, wgu1, wd1, win_g, wbp_g, wba_g, wout_g, wgu2, wd2,
        exchange)
    names = ["ffn1_w_gate_up", "ffn1_w_down", "w_in", "w_branch_pool", "w_branch_attn", "w_out",
             "ffn2_w_gate_up", "ffn2_w_down"]
    handles = dict(zip(names, sharded))
    grads, delta, new_m, new_v = {}, {}, {}, {}
    after = last
    for k in ("ffn2_w_down", "ffn2_w_gate_up", "w_branch_pool", "w_branch_attn", "w_out", "w_in", "ffn1_w_down",
              "ffn1_w_gate_up"):
        g = _owner_sum(*handles[k], after, name="owner_sum_" + k)
        hidden_major = k.endswith("w_gate_up")
        g = _unpad_gate_up(g) if hidden_major else g[:weights[k].shape[1]]
        view = _hidden_major if hidden_major else (lambda a: a[0])
        back = (lambda a: jnp.swapaxes(a, 0, 1)[None]) if hidden_major else (lambda a: a[None])
        out = _adamw(view(weights[k]), g, view(first[k]), view(second[k]), name="adamw_" + k)
        after = out[0]
        grads[k] = back(g)
        delta[k], new_m[k], new_v[k] = (back(a) for a in out)

    rows = [weights[k].size // 128 for k in small]
    padded_rows = [-(-r // 8) * 8 for r in rows]
    starts = [sum(padded_rows[:i]) for i in range(len(rows) + 1)]
    total = _sum_devices(slabs, after, name="sum_replicated")
    loss_out = total[starts[-1], 0]
    small_w = jnp.concatenate([tile_rows(weights[k]) for k in small], axis=0)
    small_m = jnp.concatenate([tile_rows(first[k]) for k in small], axis=0)
    small_v = jnp.concatenate([tile_rows(second[k]) for k in small], axis=0)
    small_out = _adamw(small_w, total[:starts[-1]], small_m, small_v, name="adamw_replicated")
    for name_, start, n_rows in zip(small, starts, rows):
        shape = weights[name_].shape
        grads[name_] = total[start:start + n_rows].reshape(shape)
        delta[name_], new_m[name_], new_v[name_] = (a[start:start + n_rows].reshape(shape) for a in small_out)

    return (loss_out, dx[None], *[grads[k] for k in order], *[delta[k] for k in order],
            *[new_m[k] for k in order], *[new_v[k] for k in order])
```

```python
import functools

import jax
import jax.numpy as jnp
from jax import lax
from jax.experimental import pallas as pl
from jax.experimental.pallas import tpu as pltpu
from jax.experimental.pallas import tpu_sc as plsc

F32 = jnp.float32
BF16 = jnp.bfloat16
MESH = pl.DeviceIdType.MESH

RMS_EPS = 1e-6
N_DEV = 8
N_HEADS = 8
HEAD_DIM = 64
HEAD_PAIR = 2 * HEAD_DIM
POOL_WINDOWS = (2, 4, 8, 16)
POOL_GROUP = 128
POOL_WIDTH = 512
SB_WIDTH = 512
FF_SHARD = 352
FF_SHARD_PAD = 384
ATTN_BLOCK = 256
ATTN_SCALE = 0.125

ADAM_LR = 0.001
ADAM_B1 = 0.9
ADAM_B2 = 0.999
ADAM_EPS = 1e-08
ADAM_WD = 0.01
ADAM_STEP = 10

VMEM_LIMIT = 48 << 20
WGRAD_TOKENS = 2048


def _params(dims=None):
    return pltpu.CompilerParams(dimension_semantics=dims, vmem_limit_bytes=VMEM_LIMIT)


def _mm(a, b):
    return jnp.dot(a, b, preferred_element_type=F32)


def _mm_nt(a, b):
    return lax.dot_general(a, b, (((1,), (1,)), ((), ())), preferred_element_type=F32)


def _mm_tn(a, b):
    return lax.dot_general(a, b, (((0,), (0,)), ((), ())), preferred_element_type=F32)


def _row_tile(rows, cols):
    limit = max(8, (512 * 1024) // cols)
    return max(t for t in range(8, rows + 1, 8) if rows % t == 0 and (t <= limit or t == 8))


def _rstd(xf):
    return lax.rsqrt(jnp.mean(xf * xf, axis=-1, keepdims=True) + RMS_EPS)


def _rms_bwd(xf, gain, dn):
    r = _rstd(xf)
    xh = xf * r
    dgain = jnp.sum(dn * xh, axis=0, keepdims=True)
    dxh = dn * gain
    dx = r * (dxh - xh * jnp.mean(dxh * xh, axis=-1, keepdims=True))
    return dx, dgain


def _ffn_up(x, gain, wgu, *, tm, name):
    T, D = x.shape
    tm = min(tm, T)
    nb, bw = wgu.shape[0] // 2, wgu.shape[1]

    def body(x_ref, gain_ref, wg_ref, wu_ref, gu_ref, hid_ref, n_scr):
        @pl.when(pl.program_id(1) == 0)
        def _():
            xf = x_ref[...]
            n_scr[...] = (xf * _rstd(xf) * gain_ref[...]).astype(BF16)

        n = n_scr[...]
        g = _mm_nt(n, wg_ref[...])
        u = _mm_nt(n, wu_ref[...])
        gu_ref[0] = g.astype(BF16)
        gu_ref[1] = u.astype(BF16)
        hid_ref[...] = (g * jax.nn.sigmoid(g) * u).astype(BF16)

    return pl.pallas_call(
        body, name=name, grid=(T // tm, nb),
        in_specs=[
            pl.BlockSpec((tm, D), lambda i, j: (i, 0)),
            pl.BlockSpec((1, D), lambda i, j: (0, 0)),
            pl.BlockSpec((None, bw, D), lambda i, j: (j, 0, 0)),
            pl.BlockSpec((None, bw, D), lambda i, j: (j + nb, 0, 0)),
        ],
        out_specs=[
            pl.BlockSpec((2, tm, bw), lambda i, j: (0, i, j)),
            pl.BlockSpec((tm, bw), lambda i, j: (i, j)),
        ],
        out_shape=[jax.ShapeDtypeStruct((2, T, nb * bw), BF16), jax.ShapeDtypeStruct((T, nb * bw), BF16)],
        scratch_shapes=[pltpu.VMEM((tm, D), BF16)],
        compiler_params=_params(("arbitrary", "arbitrary")),
    )(x, gain, wgu, wgu)


def _ffn_down(x, hid, wd, *, tm, name):
    T, D = x.shape
    tm = min(tm, T)
    F = hid.shape[1]

    def body(x_ref, hid_ref, wd_ref, h_ref):
        h_ref[...] = x_ref[...] + 0.5 * _mm(hid_ref[...], wd_ref[...])

    return pl.pallas_call(
        body, name=name, grid=(T // tm,),
        in_specs=[
            pl.BlockSpec((tm, D), lambda i: (i, 0)),
            pl.BlockSpec((tm, F), lambda i: (i, 0)),
            pl.BlockSpec((F, D), lambda i: (0, 0)),
        ],
        out_specs=pl.BlockSpec((tm, D), lambda i: (i, 0)),
        out_shape=jax.ShapeDtypeStruct((T, D), F32),
        compiler_params=_params(("arbitrary",)),
    )(x, hid, wd)


AFTER = pl.BlockSpec(memory_space=pltpu.HBM)


def _in_hbm(token):
    return pltpu.with_memory_space_constraint(token, pltpu.HBM)


def _ffn_bwd(dh, df, x, gain, gu, wgu, wd, after, *, tm, name):
    T, D = x.shape
    tm = min(tm, T)
    nb, bw = wgu.shape[0] // 2, wgu.shape[1]

    def body(dh_ref, df_ref, x_ref, gain_ref, gu_ref, wg_ref, wu_ref, wd_ref, after_ref,
             dx_ref, dgain_ref, n_ref, dgu_ref, dn_acc):
        i, j = pl.program_id(0), pl.program_id(1)

        @pl.when(j == 0)
        def _():
            xf = x_ref[...]
            n_ref[...] = (xf * _rstd(xf) * gain_ref[...]).astype(BF16)
            dn_acc[...] = jnp.zeros_like(dn_acc)

        @pl.when((i == 0) & (j == 0))
        def _():
            dgain_ref[...] = jnp.zeros_like(dgain_ref)

        dhid = _mm_nt(df_ref[...], wd_ref[...])
        g = gu_ref[0].astype(F32)
        u = gu_ref[1].astype(F32)
        s = jax.nn.sigmoid(g)
        silu = g * s
        dg =(dhid * u * (s * (1.0 + g * (1.0 - s)))).astype(BF16)
        du = (dhid * silu).astype(BF16)
        dgu_ref[0] = dg
        dgu_ref[1] = du
        dn_acc[...] += _mm(dg, wg_ref[...]) + _mm(du, wu_ref[...])

        @pl.when(j == nb - 1)
        def _():
            dx, dgain = _rms_bwd(x_ref[...], gain_ref[...], dn_acc[...])
            dx_ref[...] = dh_ref[...] + dx
            dgain_ref[...] += dgain

    row = lambda i, j: (i, 0)
    return pl.pallas_call(
        body, name=name, grid=(T // tm, nb),
        in_specs=[
            pl.BlockSpec((tm, D), row),
            pl.BlockSpec((tm, D), row),
            pl.BlockSpec((tm, D), row),
            pl.BlockSpec((1, D), lambda i, j: (0, 0)),
            pl.BlockSpec((2, tm, bw), lambda i, j: (0, i, j)),
            pl.BlockSpec((None, bw, D), lambda i, j: (j, 0, 0)),
            pl.BlockSpec((None, bw, D), lambda i, j: (j + nb, 0, 0)),
            pl.BlockSpec((bw, D), lambda i, j: (j, 0)),
            AFTER,
        ],
        out_specs=[
            pl.BlockSpec((tm, D), row),
            pl.BlockSpec((1, D), lambda i, j: (0, 0)),
            pl.BlockSpec((tm, D), row),
            pl.BlockSpec((2, tm, bw), lambda i, j: (0, i, j)),
        ],
        out_shape=[
            jax.ShapeDtypeStruct((T, D), F32),
            jax.ShapeDtypeStruct((1, D), F32),
            jax.ShapeDtypeStruct((T, D), BF16),
            jax.ShapeDtypeStruct((2, T, nb * bw), BF16),
        ],
        scratch_shapes=[pltpu.VMEM((tm, D), F32)],
        compiler_params=_params(("arbitrary", "arbitrary")),
    )(dh, df, x, gain, gu, wgu, wgu, wd, _in_hbm(after))


def _wgrad(a, b, *, grid, a_spec, b_spec, out_spec, out_shape, acc_shape, name, split_lanes=0):
    nk = grid[2]

    def body(a_ref, b_ref, o_ref, acc):
        k = pl.program_id(2)

        @pl.when(k == 0)
        def _():
            acc[...] = jnp.zeros_like(acc)

        acc[...] += _mm_tn(a_ref[...].astype(BF16), b_ref[...].astype(BF16))

        @pl.when(k == nk - 1)
        def _():
            if split_lanes:
                for e in range(o_ref.shape[0]):
                    o_ref[e] = acc[:, e * split_lanes:(e + 1) * split_lanes].astype(o_ref.dtype)
            else:
                o_ref[...] = acc[...].astype(o_ref.dtype)

    return pl.pallas_call(
        body, name=name, grid=grid, in_specs=[a_spec, b_spec], out_specs=out_spec,
        out_shape=jax.ShapeDtypeStruct(out_shape, BF16),
        scratch_shapes=[pltpu.VMEM(acc_shape, F32)],
        compiler_params=_params(("arbitrary", "arbitrary", "arbitrary")),
    )(a, b)


def _wgrad_gate_up(n, dgu, *, tk, name):
    T, D = n.shape
    tk = min(tk, T)
    bw = FF_SHARD_PAD * 2
    nb = dgu.shape[2] // bw
    return _wgrad(
        dgu, n, grid=(2 * nb, 1, T // tk), name=name,
        a_spec=pl.BlockSpec((None, tk, bw), lambda m, c, k: (m // nb, k, m % nb)),
        b_spec=pl.BlockSpec((tk, D), lambda m, c, k: (k, 0)),
        out_spec=pl.BlockSpec((None, bw, D), lambda m, c, k: (m, 0, 0)),
        out_shape=(2 * nb, bw, D), acc_shape=(bw, D))


def _wgrad_down(hid, df, *, tk, name):
    T, D = df.shape
    tk = min(tk, T)
    bw = FF_SHARD_PAD * 2
    nb = hid.shape[1] // bw
    return _wgrad(
        hid, df, grid=(nb, 1, T // tk), name=name,
        a_spec=pl.BlockSpec((tk, bw), lambda m, c, k: (k, m)),
        b_spec=pl.BlockSpec((tk, D), lambda m, c, k: (k, 0)),
        out_spec=pl.BlockSpec((bw, D), lambda m, c, k: (m, 0)),
        out_shape=(nb * bw, D), acc_shape=(bw, D))


def _wgrad_in(un, dproj, *, tk, name):
    T, D = un.shape
    tk = min(tk, T)
    bw = dproj.shape[1] // N_DEV
    return _wgrad(
        un, dproj, grid=(1, N_DEV, T // tk), name=name,
        a_spec=pl.BlockSpec((tk, D), lambda m, c, k: (k, 0)),
        b_spec=pl.BlockSpec((tk, bw), lambda m, c, k: (k, c)),
        out_spec=pl.BlockSpec((None, D, bw), lambda m, c, k: (c, 0, 0)),
        out_shape=(N_DEV, D, bw), acc_shape=(D, bw))


def _wgrad_full(a, b, *, tk, name, split_lanes=0):
    T, M = a.shape
    tk = min(tk, T)
    N = b.shape[1]
    if split_lanes:
        out_shape = (N // split_lanes, M, split_lanes)
        out_spec = pl.BlockSpec(out_shape, lambda m, c, k: (0, 0, 0))
    else:
        out_shape = (M, N)
        out_spec = pl.BlockSpec(out_shape, lambda m, c, k: (0, 0))
    return _wgrad(
        a, b, grid=(1, 1, T // tk), name=name,
        a_spec=pl.BlockSpec((tk, M), lambda m, c, k: (k, 0)),
        b_spec=pl.BlockSpec((tk, N), lambda m, c, k: (k, 0)),
        out_spec=out_spec, out_shape=out_shape, acc_shape=(M, N), split_lanes=split_lanes)


def _loss_bwd(h, target, gain, *, tm, name):
    T, D = h.shape
    tm = min(tm, T)

    def body(h_ref, t_ref, gain_ref, dh_ref, df_ref, loss_ref, dgain_ref):
        @pl.when(pl.program_id(0) == 0)
        def _():
            loss_ref[...] = jnp.zeros_like(loss_ref)
            dgain_ref[...] = jnp.zeros_like(dgain_ref)

        xf = h_ref[...]
        gain = gain_ref[...]
        err = xf * _rstd(xf) * gain - t_ref[...]
        loss_ref[...] += 0.5 * jnp.sum(jnp.mean(err * err, axis=-1, keepdims=True), axis=0, keepdims=True)
        dx, dgain = _rms_bwd(xf, gain, err * (1.0 / D))
        dh_ref[...] = dx
        df_ref[...] = (0.5 * dx).astype(BF16)
        dgain_ref[...] += dgain

    row = lambda i: (i, 0)
    fixed = lambda i: (0, 0)
    return pl.pallas_call(
        body, name=name, grid=(T // tm,),
        in_specs=[pl.BlockSpec((tm, D), row), pl.BlockSpec((tm, D), row), pl.BlockSpec((1, D), fixed)],
        out_specs=[pl.BlockSpec((tm, D), row), pl.BlockSpec((tm, D), row), pl.BlockSpec((1, 128), fixed),
                   pl.BlockSpec((1, D), fixed)],
        out_shape=[jax.ShapeDtypeStruct((T, D), F32), jax.ShapeDtypeStruct((T, D), BF16),
                   jax.ShapeDtypeStruct((1, 128), F32), jax.ShapeDtypeStruct((1, D), F32)],
        compiler_params=_params(("arbitrary",)),
    )(h, target, gain)


def _inproj_fwd(h, gain, w_in, *, tm, name):
    T, D = h.shape
    tm = min(tm, T)
    nb, bw = w_in.shape[0], w_in.shape[2]

    def body(h_ref, gain_ref, w_ref, un_ref, proj_ref):
        @pl.when(pl.program_id(1) == 0)
        def _():
            xf = h_ref[...]
            un_ref[...] = (xf * _rstd(xf) * gain_ref[...]).astype(BF16)

        proj_ref[...] = _mm(un_ref[...], w_ref[...])

    return pl.pallas_call(
        body, name=name, grid=(T // tm, nb),
        in_specs=[
            pl.BlockSpec((tm, D), lambda i, j: (i, 0)),
            pl.BlockSpec((1, D), lambda i, j: (0, 0)),
            pl.BlockSpec((None, D, bw), lambda i, j: (j, 0, 0)),
        ],
        out_specs=[pl.BlockSpec((tm, D), lambda i, j: (i, 0)), pl.BlockSpec((tm, bw), lambda i, j: (i, j))],
        out_shape=[jax.ShapeDtypeStruct((T, D), BF16), jax.ShapeDtypeStruct((T, nb * bw), F32)],
        compiler_params=_params(("arbitrary", "arbitrary")),
    )(h, gain, w_in)


def _inproj_bwd(dproj, dh, h, gain, w_in, *, tm, name):
    T, D = h.shape
    tm = min(tm, T)
    nb, bw = w_in.shape[0], w_in.shape[2]

    def body(dp_ref, dh_ref, h_ref, gain_ref, w_ref, dx_ref, df_ref, dgain_ref, acc):
        i, j = pl.program_id(0), pl.program_id(1)

        @pl.when(j == 0)
        def _():
            acc[...] = jnp.zeros_like(acc)

        @pl.when((i == 0) & (j == 0))
        def _():
            dgain_ref[...] = jnp.zeros_like(dgain_ref)

        acc[...] += _mm_nt(dp_ref[...], w_ref[...])

        @pl.when(j == nb - 1)
        def _():
            dx, dgain = _rms_bwd(h_ref[...], gain_ref[...], acc[...])
            dh_in = dh_ref[...] + dx
            dx_ref[...] = dh_in
            df_ref[...] = (0.5 * dh_in).astype(BF16)
            dgain_ref[...] += dgain

    row = lambda i, j: (i, 0)
    return pl.pallas_call(
        body, name=name, grid=(T // tm, nb),
        in_specs=[
            pl.BlockSpec((tm, bw), lambda i, j: (i, j)),
            pl.BlockSpec((tm, D), row),
            pl.BlockSpec((tm, D), row),
            pl.BlockSpec((1, D), lambda i, j: (0, 0)),
            pl.BlockSpec((None, D, bw), lambda i, j: (j, 0, 0)),
        ],
        out_specs=[pl.BlockSpec((tm, D), row), pl.BlockSpec((tm, D), row), pl.BlockSpec((1, D), lambda i, j: (0, 0))],
        out_shape=[jax.ShapeDtypeStruct((T, D), F32), jax.ShapeDtypeStruct((T, D), BF16),
                   jax.ShapeDtypeStruct((1, D), F32)],
        scratch_shapes=[pltpu.VMEM((tm, D), F32)],
        compiler_params=_params(("arbitrary", "arbitrary")),
    )(dproj, dh, h, gain, w_in)


def _window_sum(x, row, doublings, *, backward):
    T = x.shape[0]
    s = x
    for k in range(doublings):
        sh = 1 << k
        if backward:
            s = s + jnp.where(row < T - sh, pltpu.roll(s, T - sh, 0), 0.0)
        else:
            s = s + jnp.where(row >= sh, pltpu.roll(s, sh, 0), 0.0)
    return s


def _pool_fwd(proj, w_group, scale, *, name):
    T = proj.shape[0]

    def body(xp_ref, w_ref, scale_ref, p_ref):
        row = lax.broadcasted_iota(jnp.int32, (T, POOL_GROUP), 0)
        for gi, window in enumerate(POOL_WINDOWS):
            cols = slice(gi * POOL_GROUP, (gi + 1) * POOL_GROUP)
            x = xp_ref[:, cols]
            inv_count = 1.0 / jnp.minimum(row + 1, window).astype(F32)
            yc = _window_sum(x, row, gi + 1, backward=False) * inv_count - x
            pre = _mm(yc.astype(BF16), w_ref[gi].astype(BF16))
            p_ref[:, cols] = pre * scale_ref[:, cols]

    return pl.pallas_call(
        body, name=name, grid=(1,),
        in_specs=[
            pl.BlockSpec((T, POOL_WIDTH), lambda i: (0, 0)),
            pl.BlockSpec(w_group.shape, lambda i: (0, 0, 0)),
            pl.BlockSpec((1, POOL_WIDTH), lambda i: (0, 0)),
        ],
        out_specs=pl.BlockSpec((T, POOL_WIDTH), lambda i: (0, 0)),
        out_shape=jax.ShapeDtypeStruct((T, POOL_WIDTH), F32),
        compiler_params=_params(("arbitrary",)),
    )(proj, w_group, scale)


def _pool_bwd(dp, proj, w_group, scale, *, name):
    T = proj.shape[0]

    def body(dp_ref, xp_ref, w_ref, scale_ref, dxp_ref, dw_ref, dscale_ref):
        row = lax.broadcasted_iota(jnp.int32, (T, POOL_GROUP), 0)
        for gi, window in enumerate(POOL_WINDOWS):
            cols = slice(gi * POOL_GROUP, (gi + 1) * POOL_GROUP)
            x = xp_ref[:, cols]
            inv_count = 1.0 / jnp.minimum(row + 1, window).astype(F32)
            yc = (_window_sum(x, row, gi + 1, backward=False) * inv_count - x).astype(BF16)
            w = w_ref[gi].astype(BF16)
            pre = _mm(yc, w)
            dpg = dp_ref[:, cols]
            dscale_ref[:, cols] = jnp.sum(dpg * pre, axis=0, keepdims=True)
            dpre = (dpg * scale_ref[:, cols]).astype(BF16)
            dw_ref[gi] = _mm_tn(yc, dpre)
            dyc = _mm_nt(dpre, w)
            dxp_ref[:, cols] = _window_sum(dyc * inv_count, row, gi + 1, backward=True) - dyc

    return pl.pallas_call(
        body, name=name, grid=(1,),
        in_specs=[
            pl.BlockSpec((T, POOL_WIDTH), lambda i: (0, 0)),
            pl.BlockSpec((T, POOL_WIDTH), lambda i: (0, 0)),
            pl.BlockSpec(w_group.shape, lambda i: (0, 0, 0)),
            pl.BlockSpec((1, POOL_WIDTH), lambda i: (0, 0)),
        ],
        out_specs=[
            pl.BlockSpec((T, POOL_WIDTH), lambda i: (0, 0)),
            pl.BlockSpec(w_group.shape, lambda i: (0, 0, 0)),
            pl.BlockSpec((1, POOL_WIDTH), lambda i: (0, 0)),
        ],
        out_shape=[jax.ShapeDtypeStruct((T, POOL_WIDTH), F32), jax.ShapeDtypeStruct(w_group.shape, F32),
                   jax.ShapeDtypeStruct((1, POOL_WIDTH), F32)],
        compiler_params=_params(("arbitrary",)),
    )(dp, proj, w_group, scale)


ATTN_STRIP = 32


def _log_sigmoids(z):
    lb = jnp.minimum(z, 0.0) - jnp.log(1.0 + jnp.exp(-jnp.abs(z)))
    return lb, lb - z


def _transposed_blocks(x_ref, blocks_scr, tq):
    for b in range(blocks_scr.shape[0]):
        blocks_scr[b] = x_ref[b * tq:(b + 1) * tq, :].T.astype(BF16)


def _split_bf16(x):
    hi = x.astype(BF16)
    return hi, (x - hi.astype(F32)).astype(BF16)


def _strips(n):
    return [slice(i, i + ATTN_STRIP) for i in range(0, n, ATTN_STRIP)]


def _rows(parts):
    return jnp.concatenate(parts, axis=0)


def _attn_specs(T, tq):
    q_col = POOL_WIDTH // HEAD_PAIR
    k_col = q_col + SB_WIDTH // HEAD_PAIR
    v_col = k_col + SB_WIDTH // HEAD_PAIR
    return [
        pl.BlockSpec((tq, HEAD_PAIR), lambda p, i: (i, q_col + p)),
        pl.BlockSpec((T, HEAD_PAIR), lambda p, i: (0, k_col + p)),
        pl.BlockSpec((T, HEAD_PAIR), lambda p, i: (0, v_col + p)),
    ]


def _attn_fwd(proj, *, name):
    T = proj.shape[0]
    tq = ATTN_BLOCK

    def body(q_ref, k_ref, v_ref, o_ref, lt_ref, kt_scr, vb_scr):
        qi = pl.program_id(1)

        @pl.when(qi == 0)
        def _():
            _transposed_blocks(k_ref, kt_scr, tq)
            vb_scr[...] = v_ref[...].astype(BF16)

        head0 = lax.broadcasted_iota(jnp.int32, (tq, HEAD_PAIR), 1) < HEAD_DIM
        q = q_ref[...] * ATTN_SCALE
        qs = (jnp.where(head0, q, 0.0).astype(BF16), jnp.where(head0, 0.0, q).astype(BF16))
        r = lax.broadcasted_iota(jnp.int32, (tq, tq), 0)
        c = lax.broadcasted_iota(jnp.int32, (tq, tq), 1)
        later = (r > c).astype(BF16)
        later2 = _rows([later, later])
        causal = lambda rows: c[rows] < r[rows]
        strips = _strips(tq)

        def log_terms(z, valid):
            lbs, his, los, sums = [], [], [], []
            for rows in strips:
                lb, lm = _log_sigmoids(z[rows])
                if valid is not None:
                    lm = jnp.where(valid(rows), lm, 0.0)
                hi, lo = _split_bf16(lm)
                lbs.append(lb)
                his.append(hi)
                los.append(lo)
                sums.append(jnp.sum(lm, axis=1, keepdims=True))
            return lbs, jnp.concatenate([_rows(his), _rows(los)], axis=1), _rows(sums)

        def weights(lbs, run, after, valid):
            parts = []
            for rows, lb in zip(strips, lbs):
                a = jnp.exp(lb + run[rows] + after[rows])
                if valid is not None:
                    a = jnp.where(valid(rows), a, 0.0)
                parts.append(a.astype(BF16))
            return _rows(parts)

        def block(kj, carry, valid):
            kt = kt_scr[kj]
            vb = vb_scr[pl.ds(pl.multiple_of(kj * tq, tq), tq), :]
            run0, o0, run1, o1 = carry
            z0 = _mm(qs[0], kt)
            z1 = _mm(qs[1], kt)
            lbs0, split0, sums0 = log_terms(z0, valid)
            after0 = _mm(split0, later2)
            lbs1, split1, sums1 = log_terms(z1, valid)
            after1 = _mm(split1, later2)
            o0 = o0 + _mm(weights(lbs0, run0, after0, valid), vb)
            o1 = o1 + _mm(weights(lbs1, run1, after1, valid), vb)
            return run0 + sums0, o0, run1 + sums1, o1

        zero = (jnp.zeros((tq, 1), F32), jnp.zeros((tq, HEAD_PAIR), F32))
        carry = block(qi, zero + zero, causal)
        carry = lax.fori_loop(0, qi, lambda it, cr: block(qi - 1 - it, cr, None), carry)
        o_ref[...] = jnp.where(head0, carry[1], carry[3])
        lt_ref[...] = jnp.where(head0, carry[0], carry[2])

    out_spec = pl.BlockSpec((tq, HEAD_PAIR), lambda p, i: (i, p))
    return pl.pallas_call(
        body, name=name, grid=(N_HEADS // 2, T // tq),
        in_specs=_attn_specs(T, tq), out_specs=[out_spec, out_spec],
        out_shape=[jax.ShapeDtypeStruct((T, SB_WIDTH), F32), jax.ShapeDtypeStruct((T, SB_WIDTH), F32)],
        scratch_shapes=[pltpu.VMEM((T // tq, HEAD_PAIR, tq), BF16), pltpu.VMEM((T, HEAD_PAIR), BF16)],
        compiler_params=_params(("arbitrary", "arbitrary")),
    )(proj, proj, proj)


def _attn_bwd(proj, do, ltot, after, *, name):
    T = proj.shape[0]
    tq = ATTN_BLOCK

    def body(q_ref, k_ref, v_ref, do_ref, lt_ref, after_ref, dq_ref, dkt_ref, dvt_ref, kb_scr, kt_scr, vt_scr):
        qi = pl.program_id(1)

        @pl.when(qi == 0)
        def _():
            kb_scr[...] = k_ref[...].astype(BF16)
            _transposed_blocks(k_ref, kt_scr, tq)
            _transposed_blocks(v_ref, vt_scr, tq)
            dkt_ref[...] = jnp.zeros_like(dkt_ref)
            dvt_ref[...] = jnp.zeros_like(dvt_ref)

        head0 = lax.broadcasted_iota(jnp.int32, (tq, HEAD_PAIR), 1) < HEAD_DIM
        q, do_, lt = q_ref[...] * ATTN_SCALE, do_ref[...], lt_ref[...]
        qs = (jnp.where(head0, q, 0.0).astype(BF16), jnp.where(head0, 0.0, q).astype(BF16))
        q_heads = (jnp.where(head0, q, 0.0), jnp.where(head0, 0.0, q))
        do_heads = (jnp.where(head0, do_, 0.0), jnp.where(head0, 0.0, do_))
        dos = tuple(d.astype(BF16) for d in do_heads)
        qts = tuple(x.T.astype(BF16) for x in q_heads)
        dots = tuple(d.T.astype(BF16) for d in do_heads)
        lts = (jnp.max(jnp.where(head0, lt, -jnp.inf), axis=1, keepdims=True),
               jnp.max(jnp.where(head0, -jnp.inf, lt), axis=1, keepdims=True))
        r = lax.broadcasted_iota(jnp.int32, (tq, tq), 0)
        c = lax.broadcasted_iota(jnp.int32, (tq, tq), 1)
        upto = (r <= c).astype(BF16)
        before = (r < c).astype(BF16)
        upto2, before2 = _rows([upto, upto]), _rows([before, before])
        causal = lambda rows: c[rows] < r[rows]
        strips = _strips(tq)

        def log_terms(z, valid):
            lbs, his, los, sums = [], [], [], []
            for rows in strips:
                lb, lm = _log_sigmoids(z[rows])
                if valid is not None:
                    lm = jnp.where(valid(rows), lm, 0.0)
                hi, lo = _split_bf16(lm)
                lbs.append(lb)
                his.append(hi)
                los.append(lo)
                sums.append(jnp.sum(lm, axis=1, keepdims=True))
            return lbs, jnp.concatenate([_rows(his), _rows(los)], axis=1), _rows(sums)

        def weights(lbs, rest, lm_upto, da, valid):
            a_parts, es, his, los, sums = [], [], [], [], []
            for rows, lb in zip(strips, lbs):
                a = jnp.exp(lb + (rest[rows] - lm_upto[rows]))
                if valid is not None:
                    a = jnp.where(valid(rows), a, 0.0)
                e = da[rows] * a
                hi, lo = _split_bf16(e)
                a_parts.append(a.astype(BF16))
                es.append(e)
                his.append(hi)
                los.append(lo)
                sums.append(jnp.sum(e, axis=1, keepdims=True))
            return _rows(a_parts), es, jnp.concatenate([_rows(his), _rows(los)], axis=1), _rows(sums)

        def score_grads(lbs, es, run_e, e_before, valid):
            parts = []
            for rows, lb, e in zip(strips, lbs, es):
                beta = jnp.exp(lb)
                dz = e * (1.0 - beta) - (run_e[rows] + e_before[rows]) * beta
                if valid is not None:
                    dz = jnp.where(valid(rows), dz, 0.0)
                parts.append(dz.astype(BF16))
            return _rows(parts)

        def block(kj, carry, valid):
            off = pl.multiple_of(kj * tq, tq)
            kb, kt, vt = kb_scr[pl.ds(off, tq), :], kt_scr[kj], vt_scr[kj]
            run_lm0, run_e0, dq0, run_lm1, run_e1, dq1 = carry
            z0, da0 = _mm(qs[0], kt), _mm(dos[0], vt)
            z1, da1 = _mm(qs[1], kt), _mm(dos[1], vt)
            lbs0, split0, lm_sums0 = log_terms(z0, valid)
            lm_upto0 = _mm(split0, upto2)
            lbs1, split1, lm_sums1 = log_terms(z1, valid)
            lm_upto1 = _mm(split1, upto2)
            a0, es0, split0, e_sums0 = weights(lbs0, lts[0] - run_lm0, lm_upto0, da0, valid)
            e_before0 = _mm(split0, before2)
            a1, es1, split1, e_sums1 = weights(lbs1, lts[1] - run_lm1, lm_upto1, da1, valid)
            e_before1 = _mm(split1, before2)
            dz0 = score_grads(lbs0, es0, run_e0, e_before0, valid)
            dkt_blk = _mm(qts[0], dz0)
            dvt_blk = _mm(dots[0], a0)
            dq0 = dq0 + _mm(dz0, kb)
            dz1 = score_grads(lbs1, es1, run_e1, e_before1, valid)
            dkt_ref[kj] += dkt_blk + _mm(qts[1], dz1)
            dvt_ref[kj] += dvt_blk + _mm(dots[1], a1)
            dq1 = dq1 + _mm(dz1, kb)
            return run_lm0 + lm_sums0, run_e0 + e_sums0, dq0, run_lm1 + lm_sums1, run_e1 + e_sums1, dq1

        zero = (jnp.zeros((tq, 1), F32), jnp.zeros((tq, 1), F32), jnp.zeros((tq, HEAD_PAIR), F32))
        carry = lax.fori_loop(0, qi, lambda kj, cr: block(kj, cr, None), zero + zero)
        carry = block(qi, carry, causal)
        dq_ref[...] = jnp.where(head0, carry[2], carry[5]) * ATTN_SCALE

    blk = pl.BlockSpec((tq, HEAD_PAIR), lambda p, i: (i, p))
    seq = pl.BlockSpec((T // tq, HEAD_PAIR, tq), lambda p, i: (0, p, 0))
    transposed = jax.ShapeDtypeStruct((T // tq, SB_WIDTH, tq), F32)
    return pl.pallas_call(
        body, name=name, grid=(N_HEADS // 2, T // tq),
        in_specs=_attn_specs(T, tq) + [blk, blk, AFTER], out_specs=[blk, seq, seq],
        out_shape=[jax.ShapeDtypeStruct((T, SB_WIDTH), F32), transposed, transposed],
        scratch_shapes=[pltpu.VMEM((T, HEAD_PAIR), BF16), pltpu.VMEM((T // tq, HEAD_PAIR, tq), BF16),
                        pltpu.VMEM((T // tq, HEAD_PAIR, tq), BF16)],
        compiler_params=_params(("arbitrary", "arbitrary")),
    )(proj, proj, proj, do, ltot, _in_hbm(after))


def _branch(act_bf16, w_ref):
    return jnp.concatenate([_mm(act_bf16, w_ref[e]) for e in range(w_ref.shape[0])], axis=1)


def _mix_specs(T, D, tm, wbp, w_out):
    gate_col = (POOL_WIDTH + 3 * SB_WIDTH) // D
    row = lambda i: (i, 0)
    return [
        pl.BlockSpec((tm, D), row),
        pl.BlockSpec((tm, POOL_WIDTH), row),
        pl.BlockSpec((tm, SB_WIDTH), row),
        pl.BlockSpec((tm, D), lambda i: (i, gate_col)),
        pl.BlockSpec((tm, D), lambda i: (i, gate_col + 1)),
        pl.BlockSpec(wbp.shape, lambda i: (0, 0, 0)),
        pl.BlockSpec(wbp.shape, lambda i: (0, 0, 0)),
        pl.BlockSpec(w_out.shape, lambda i: (0, 0)),
    ]


def _mix_fwd(h, p, o, proj, wbp, wba, w_out, *, tm, name):
    T, D = h.shape
    tm = min(tm, T)

    def body(h_ref, p_ref, o_ref, glp_ref, gls_ref, wbp_ref, wba_ref, wout_ref, hout_ref, m_ref):
        yp = _branch(p_ref[...].astype(BF16), wbp_ref)
        ys = _branch(o_ref[...].astype(BF16), wba_ref)
        m = (jax.nn.sigmoid(glp_ref[...]) * yp + jax.nn.sigmoid(gls_ref[...]) * ys).astype(BF16)
        m_ref[...] = m
        hout_ref[...] = h_ref[...] + _mm(m, wout_ref[...])

    row = lambda i: (i, 0)
    return pl.pallas_call(
        body, name=name, grid=(T // tm,),
        in_specs=_mix_specs(T, D, tm, wbp, w_out),
        out_specs=[pl.BlockSpec((tm, D), row), pl.BlockSpec((tm, D), row)],
        out_shape=[jax.ShapeDtypeStruct((T, D), F32), jax.ShapeDtypeStruct((T, D), BF16)],
        compiler_params=_params(("arbitrary",)),
    )(h, p, o, proj, proj, wbp, wba, w_out)


def _mix_bwd(dh, p, o, proj, wbp, wba, w_out, after, *, tm, name):
    T, D = dh.shape
    tm = min(tm, T)
    bw = wbp.shape[2]

    def body(dh_ref, p_ref, o_ref, glp_ref, gls_ref, wbp_ref, wba_ref, wout_ref, after_ref,
             dyp_ref, dys_ref, dp_ref, do_ref, dgl_ref):
        dm = _mm_nt(dh_ref[...].astype(BF16), wout_ref[...])
        yp = _branch(p_ref[...].astype(BF16), wbp_ref)
        ys = _branch(o_ref[...].astype(BF16), wba_ref)
        gp = jax.nn.sigmoid(glp_ref[...])
        gs = jax.nn.sigmoid(gls_ref[...])
        dyp = (dm * gp).astype(BF16)
        dys = (dm * gs).astype(BF16)
        dyp_ref[...] = dyp
        dys_ref[...] = dys
        dgl_ref[:, :D] = (dm * yp * gp * (1.0 - gp)).astype(BF16)
        dgl_ref[:, D:] = (dm * ys * gs * (1.0 - gs)).astype(BF16)
        dp = jnp.zeros(dp_ref.shape, F32)
        do_ = jnp.zeros(do_ref.shape, F32)
        for e in range(wbp_ref.shape[0]):
            dp += _mm_nt(dyp[:, e * bw:(e + 1) * bw], wbp_ref[e])
            do_ += _mm_nt(dys[:, e * bw:(e + 1) * bw], wba_ref[e])
        dp_ref[...] = dp
        do_ref[...] = do_

    row = lambda i: (i, 0)
    return pl.pallas_call(
        body, name=name, grid=(T // tm,),
        in_specs=_mix_specs(T, D, tm, wbp, w_out) + [AFTER],
        out_specs=[pl.BlockSpec((tm, D), row), pl.BlockSpec((tm, D), row), pl.BlockSpec((tm, POOL_WIDTH), row),
                   pl.BlockSpec((tm, SB_WIDTH), row), pl.BlockSpec((tm, 2 * D), row)],
        out_shape=[jax.ShapeDtypeStruct((T, D), BF16), jax.ShapeDtypeStruct((T, D), BF16),
                   jax.ShapeDtypeStruct((T, POOL_WIDTH), F32), jax.ShapeDtypeStruct((T, SB_WIDTH), F32),
                   jax.ShapeDtypeStruct((T, 2 * D), BF16)],
        compiler_params=_params(("arbitrary",)),
    )(dh, p, o, proj, proj, wbp, wba, w_out, _in_hbm(after))


def _adamw(w, g, m, v, *, name):
    R, C = w.shape
    tr = _row_tile(R, C)

    def body(w_ref, g_ref, m_ref, v_ref, d_ref, nm_ref, nv_ref):
        g_ = g_ref[...]
        m_ = ADAM_B1 * m_ref[...] + (1.0 - ADAM_B1) * g_
        v_ = ADAM_B2 * v_ref[...] + (1.0 - ADAM_B2) * (g_ * g_)
        m_hat = m_ / (1.0 - ADAM_B1 ** ADAM_STEP)
        v_hat = v_ / (1.0 - ADAM_B2 ** ADAM_STEP)
        d_ref[...] = -ADAM_LR * (m_hat / (jnp.sqrt(v_hat) + ADAM_EPS) + ADAM_WD * w_ref[...])
        nm_ref[...] = m_
        nv_ref[...] = v_

    spec = pl.BlockSpec((tr, C), lambda i: (i, 0))
    return pl.pallas_call(
        body, name=name, grid=(R // tr,), in_specs=[spec] * 4, out_specs=[spec] * 3,
        out_shape=[jax.ShapeDtypeStruct((R, C), F32)] * 3,
        compiler_params=_params(("arbitrary",)),
    )(w, g, m, v)


def _position():
    return lax.axis_index("x"), lax.axis_index("y"), lax.axis_index("c")


def _all_gather(shards, *, name, collective_id):
    n = len(shards)
    n_copies = 9

    def body(*refs):
        ins, outs = refs[:n], refs[n:2 * n]
        send_sems, recv_sems, local_sems = refs[2 * n:]
        x, y, c = _position()
        me, sibling = (x, y, c), (x, y, 1 - c)
        x_nbr, y_nbr, diagonal = (1 - x, y, c), (x, 1 - y, c), (1 - x, 1 - y, c)
        other = lambda pos: (pos[0], pos[1], 1 - c)

        barrier = pltpu.get_barrier_semaphore()
        for peer in (sibling, x_nbr, y_nbr):
            pl.semaphore_signal(barrier, inc=1, device_id=peer, device_id_type=MESH)
        pl.semaphore_wait(barrier, 3)

        def block(a, pos, half=None):
            ref = outs[a].at[4 * pos[0] + 2 * pos[1] + pos[2]]
            rows = ref.shape[0] // 2
            return ref if half is None else ref.at[pl.ds(half * rows, rows)]

        def copy(a, k, pos, to, half=None, src=None):
            return pltpu.make_async_remote_copy(
                src_ref=block(a, pos, half) if src is None else src, dst_ref=block(a, pos, half),
                send_sem=send_sems.at[n_copies * a + k], recv_sem=recv_sems.at[n_copies * a + k],
                device_id=to, device_id_type=MESH)

        started = []
        for a in range(n):
            mine = pltpu.make_async_copy(ins[a], block(a, me), local_sems.at[a])
            mine.start()
            started.append(mine)
        sends = []
        for a in range(n):
            sends += [copy(a, 1, me, x_nbr, src=ins[a]), copy(a, 2, me, y_nbr, src=ins[a]),
                      copy(a, 0, me, sibling, src=ins[a])]
        for cp in sends:
            cp.start()

        def pass_on(copies):
            for cp in copies:
                cp.start()
                sends.append(cp)

        for a in range(n):
            copy(a, 1, x_nbr, me).wait_recv()
            pass_on([copy(a, 5, x_nbr, y_nbr, half=0), copy(a, 3, x_nbr, sibling)])
            copy(a, 2, y_nbr, me).wait_recv()
            pass_on([copy(a, 6, y_nbr, x_nbr, half=1), copy(a, 4, y_nbr, sibling)])
        for a in range(n):
            copy(a, 5, diagonal, me, half=0).wait_recv()
            pass_on([copy(a, 7, diagonal, sibling, half=0)])
            copy(a, 6, diagonal, me, half=1).wait_recv()
            pass_on([copy(a, 8, diagonal, sibling, half=1)])
        for a in range(n):
            copy(a, 0, sibling, me).wait_recv()
            copy(a, 3, other(x_nbr), me).wait_recv()
            copy(a, 4, other(y_nbr), me).wait_recv()
            copy(a, 7, other(diagonal), me, half=0).wait_recv()
            copy(a, 8, other(diagonal), me, half=1).wait_recv()
        for cp in sends:
            cp.wait_send()
        for cp in started:
            cp.wait()

    return pl.kernel(
        body, name=name,
        out_type=[jax.ShapeDtypeStruct((N_DEV,) + s.shape, s.dtype) for s in shards],
        mesh=plsc.ScalarSubcoreMesh(axis_name="sequencer", num_cores=1),
        scratch_types=[pltpu.SemaphoreType.DMA((n_copies * n,)), pltpu.SemaphoreType.DMA((n_copies * n,)),
                       pltpu.SemaphoreType.DMA((n,))],
        compiler_params=pltpu.CompilerParams(collective_id=collective_id),
    )(*shards)


def _chip_sums(grads, *, name):
    _, R, C = grads.shape
    rc = 128 if R % 128 == 0 else R

    def body(g_ref, partial, out_ref, mine, theirs, send_sems, recv_sems, local_sems):
        x, y, c = _position()
        my_chip = 2 * x + y

        def swap(s):
            return pltpu.make_async_remote_copy(
                src_ref=g_ref.at[2 * s + (1 - c)], dst_ref=theirs.at[s],
                send_sem=send_sems.at[s], recv_sem=recv_sems.at[s],
                device_id=(x, y, 1 - c), device_id_type=MESH)

        def load(s):
            return pltpu.make_async_copy(g_ref.at[2 * s + c], mine.at[s], local_sems.at[s])

        for s in range(4):
            swap(s).start()
            load(s).start()
        for s in range(4):
            load(s).wait()
            swap(s).wait_recv()

        def chip_sum(chip, rows):
            return mine[chip, rows, :].astype(F32) + theirs[chip, rows, :].astype(F32)

        for j in (1, 2, 3):
            @pl.loop(0, R // rc)
            def _(t):
                rows = pl.ds(pl.multiple_of(t * rc, rc), rc)
                partial[j - 1, rows, :] = chip_sum(my_chip ^ j, rows).astype(BF16)

        @pl.loop(0, R // rc)
        def _(t):
            rows = pl.ds(pl.multiple_of(t * rc, rc), rc)
            out_ref[rows, :] = chip_sum(my_chip, rows)

        for s in range(4):
            swap(s).wait_send()

    vmem = pl.BlockSpec(memory_space=pltpu.VMEM)
    return pl.pallas_call(
        body, name=name,
        in_specs=[pl.BlockSpec(memory_space=pl.ANY)], out_specs=[vmem, vmem],
        out_shape=[jax.ShapeDtypeStruct((3, R, C), BF16), jax.ShapeDtypeStruct((R, C), F32)],
        scratch_shapes=[
            pltpu.VMEM((4, R, C), BF16), pltpu.VMEM((4, R, C), BF16),
            pltpu.SemaphoreType.DMA((4,)), pltpu.SemaphoreType.DMA((4,)), pltpu.SemaphoreType.DMA((4,)),
        ],
        compiler_params=_params(),
    )(grads)


def _cross_chips(partials, *, name, collective_id):
    n = len(partials)

    def body(*refs):
        ins, outs = refs[:n], refs[n:2 * n]
        send_sems, recv_sems = refs[2 * n:]
        x, y, c = _position()
        my_chip = 2 * x + y
        peers = [((my_chip ^ j) // 2, (my_chip ^ j) % 2, c) for j in (1, 2, 3)]

        barrier = pltpu.get_barrier_semaphore()
        for peer in peers:
            pl.semaphore_signal(barrier, inc=1, device_id=peer, device_id_type=MESH)
        pl.semaphore_wait(barrier, 3)

        copies = [
            pltpu.make_async_remote_copy(
                src_ref=ins[a].at[j], dst_ref=outs[a].at[j],
                send_sem=send_sems.at[3 * a + j], recv_sem=recv_sems.at[3 * a + j],
                device_id=peers[j], device_id_type=MESH)
            for a in range(n) for j in range(3)]
        for cp in copies:
            cp.start()
        for cp in copies:
            cp.wait_recv()
        for cp in copies:
            cp.wait_send()

    return pl.kernel(
        body, name=name,
        out_type=[jax.ShapeDtypeStruct(p.shape, p.dtype) for p in partials],
        mesh=plsc.ScalarSubcoreMesh(axis_name="sequencer", num_cores=1),
        scratch_types=[pltpu.SemaphoreType.DMA((3 * n,)), pltpu.SemaphoreType.DMA((3 * n,))],
        compiler_params=pltpu.CompilerParams(collective_id=collective_id),
    )(*partials)


def _cross_chips_and_gather(partial, slab, *, name, collective_id):
    def body(part_ref, slab_ref, landed_ref, slabs_ref, send_sems, recv_sems, local_sem):
        x, y, c = _position()
        me, my_chip = 4 * x + 2 * y + c, 2 * x + y
        others = [me ^ k for k in range(1, N_DEV)]
        ids = [(o // 4, (o // 2) % 2, o % 2) for o in others]

        barrier = pltpu.get_barrier_semaphore()
        for peer in ids:
            pl.semaphore_signal(barrier, inc=1, device_id=peer, device_id_type=MESH)
        pl.semaphore_wait(barrier, N_DEV - 1)

        mine = pltpu.make_async_copy(slab_ref, slabs_ref.at[me], local_sem)
        mine.start()
        sends = [
            pltpu.make_async_remote_copy(
                src_ref=part_ref.at[j], dst_ref=landed_ref.at[j], send_sem=send_sems.at[j], recv_sem=recv_sems.at[j],
                device_id=((my_chip ^ (j + 1)) // 2, (my_chip ^ (j + 1)) % 2, c), device_id_type=MESH)
            for j in range(3)]
        sends += [
            pltpu.make_async_remote_copy(
                src_ref=slab_ref, dst_ref=slabs_ref.at[me], send_sem=send_sems.at[3 + k], recv_sem=recv_sems.at[3 + k],
                device_id=ids[k], device_id_type=MESH)
            for k in range(N_DEV - 1)]
        arrivals = sends[:3] + [
            pltpu.make_async_remote_copy(
                src_ref=slab_ref, dst_ref=slabs_ref.at[others[k]], send_sem=send_sems.at[3 + k],
                recv_sem=recv_sems.at[3 + k], device_id=ids[k], device_id_type=MESH)
            for k in range(N_DEV - 1)]
        for cp in sends:
            cp.start()
        for cp in arrivals:
            cp.wait_recv()
        for cp in sends:
            cp.wait_send()
        mine.wait()

    n_sems = 3 + N_DEV - 1
    return pl.kernel(
        body, name=name,
        out_type=[jax.ShapeDtypeStruct(partial.shape, partial.dtype),
                  jax.ShapeDtypeStruct((N_DEV,) + slab.shape, slab.dtype)],
        mesh=plsc.ScalarSubcoreMesh(axis_name="sequencer", num_cores=1),
        scratch_types=[pltpu.SemaphoreType.DMA((n_sems,)), pltpu.SemaphoreType.DMA((n_sems,)), pltpu.SemaphoreType.DMA],
        compiler_params=pltpu.CompilerParams(collective_id=collective_id),
    )(partial, slab)


def _sum_devices(gathered, after, *, name):
    _, R, C = gathered.shape

    def body(in_ref, after_ref, out_ref):
        total = in_ref[0]
        for d in range(1, N_DEV):
            total = total + in_ref[d]
        out_ref[...] = total

    return pl.pallas_call(
        body, name=name, grid=(1,),
        in_specs=[pl.BlockSpec((N_DEV, R, C), lambda i: (0, 0, 0)), AFTER],
        out_specs=pl.BlockSpec((R, C), lambda i: (0, 0)),
        out_shape=jax.ShapeDtypeStruct((R, C), F32),
        compiler_params=_params(("arbitrary",)),
    )(gathered, _in_hbm(after))


def _owner_sum(own, landed, after, *, name):
    R, C = own.shape
    tr = _row_tile(R, C)

    def body(own_ref, landed_ref, after_ref, out_ref):
        total = own_ref[...]
        for j in range(3):
            total = total + landed_ref[j].astype(F32)
        out_ref[...] = total

    return pl.pallas_call(
        body, name=name, grid=(R // tr,),
        in_specs=[pl.BlockSpec((tr, C), lambda i: (i, 0)), pl.BlockSpec((3, tr, C), lambda i: (0, i, 0)), AFTER],
        out_specs=pl.BlockSpec((tr, C), lambda i: (i, 0)),
        out_shape=jax.ShapeDtypeStruct((R, C), F32),
        compiler_params=_params(("arbitrary",)),
    )(own, landed, _in_hbm(after))


def _local_step(x, target, norms, pool_w_group, pool_scale, wgu1, wd1, w_in, wbp, wba, w_out, wgu2, wd2, exchange):
    n1g, nmg, n2g, nfg = norms
    D = x.shape[1]
    gu1, hid1 = _ffn_up(x, n1g, wgu1, tm=512, name="ffn1_up")
    h1 = _ffn_down(x, hid1, wd1, tm=512, name="ffn1_down")
    un, proj = _inproj_fwd(h1, nmg, w_in, tm=1024, name="inproj_fwd")
    p = _pool_fwd(proj, pool_w_group, pool_scale, name="pool_fwd")
    o, ltot = _attn_fwd(proj, name="attn_fwd")
    h2, m = _mix_fwd(h1, p, o, proj, wbp, wba, w_out, tm=256, name="mix_fwd")
    gu2, hid2 = _ffn_up(h2, n2g, wgu2, tm=512, name="ffn2_up")
    h3 = _ffn_down(h2, hid2, wd2, tm=512, name="ffn2_down")
    dh3, df2, loss, d_nf = _loss_bwd(h3, target, nfg, tm=256, name="loss_bwd")

    d_wd2 = _wgrad_down(hid2, df2, tk=WGRAD_TOKENS, name="ffn2_wgrad_down")
    (g_wd2,), token = exchange("ffn2_down", [d_wd2.reshape(N_DEV, FF_SHARD_PAD, D)])
    dh2, d_n2, n2, dgu2 = _ffn_bwd(dh3, df2, h2, n2g, gu2, wgu2, wd2, token, tm=256, name="ffn2_bwd")
    d_wgu2 = _wgrad_gate_up(n2, dgu2, tk=WGRAD_TOKENS, name="ffn2_wgrad_gate_up")
    (g_wgu2,), token = exchange("ffn2_gate_up", [d_wgu2])

    dyp, dys, dp, do, dgl = _mix_bwd(dh2, p, o, proj, wbp, wba, w_out, token, tm=256, name="mix_bwd")
    d_wout = _wgrad_full(m, dh2, tk=WGRAD_TOKENS, name="wgrad_out")
    d_wbp = _wgrad_full(p, dyp, tk=WGRAD_TOKENS, name="wgrad_branch_pool", split_lanes=wbp.shape[2])
    d_wba = _wgrad_full(o, dys, tk=WGRAD_TOKENS, name="wgrad_branch_attn", split_lanes=wba.shape[2])
    (g_wbp, g_wba, g_wout), token = exchange("mix", [d_wbp, d_wba, d_wout.reshape(N_DEV, D // N_DEV, D)])
    dxp, d_wgroup, d_scale = _pool_bwd(dp, proj, pool_w_group, pool_scale, name="pool_bwd")
    dq, dkt, dvt = _attn_bwd(proj, do, ltot, token, name="attn_bwd")
    dk, dv = (t.transpose(0, 2, 1).reshape(dq.shape) for t in (dkt, dvt))
    dproj = jnp.concatenate([dxp.astype(BF16), dq.astype(BF16), dk.astype(BF16), dv.astype(BF16), dgl], axis=1)
    d_win = _wgrad_in(un, dproj, tk=WGRAD_TOKENS, name="wgrad_in")
    (g_win,), token_in = exchange("w_in", [d_win])
    dh1, df1, d_nm = _inproj_bwd(dproj, dh2, h1, nmg, w_in, tm=1024, name="inproj_bwd")
    d_wd1 = _wgrad_down(hid1, df1, tk=WGRAD_TOKENS, name="ffn1_wgrad_down")
    (g_wd1,), token_down = exchange("ffn1_down", [d_wd1.reshape(N_DEV, FF_SHARD_PAD, D)])
    token = (token_down[(0,) * token_down.ndim] + token_in[(0,) * token_in.ndim]).reshape(1, 1)

    dx, d_n1, n1, dgu1 = _ffn_bwd(dh1, df1, x, n1g, gu1, wgu1, wd1, token, tm=256, name="ffn1_bwd")
    d_wgu1 = _wgrad_gate_up(n1, dgu1, tk=WGRAD_TOKENS, name="ffn1_wgrad_gate_up")
    (g_wgu1, replicated), token = exchange("last", [d_wgu1, d_n1, d_nm, d_n2, d_nf, d_scale, d_wgroup, loss])

    sharded = (g_wgu1, g_wd1, g_win, g_wbp, g_wba, g_wout, g_wgu2, g_wd2)
    return dx, sharded, replicated, token


def _hidden_major(w):
    return jnp.swapaxes(w[0], 0, 1)


def _pad_gate_up(wt):
    d = wt.shape[1]
    wt = wt.astype(BF16).reshape(2, FF_SHARD, d)
    return jnp.pad(wt, ((0, 0), (0, FF_SHARD_PAD - FF_SHARD), (0, 0))).reshape(2 * FF_SHARD_PAD, d)


def _unpad_gate_up(gt):
    d = gt.shape[1]
    return gt.reshape(2, FF_SHARD_PAD, d)[:, :FF_SHARD].reshape(2 * FF_SHARD, d)


def _pad_down(w):
    return jnp.pad(w.astype(BF16), ((0, FF_SHARD_PAD - FF_SHARD), (0, 0)))


def kernel(x, ffn1_norm, ffn1_w_gate_up, ffn1_w_down, mix_norm, w_in, pool_w_group, pool_scale, w_branch_pool, w_branch_attn, w_out, ffn2_norm, ffn2_w_gate_up, ffn2_w_down, final_norm, loss_target, m_ffn1_norm, m_ffn1_w_gate_up, m_ffn1_w_down, m_mix_norm, m_w_in, m_pool_w_group, m_pool_scale, m_w_branch_pool, m_w_branch_attn, m_w_out, m_ffn2_norm, m_ffn2_w_gate_up, m_ffn2_w_down, m_final_norm, v_ffn1_norm, v_ffn1_w_gate_up, v_ffn1_w_down, v_mix_norm, v_w_in, v_pool_w_group, v_pool_scale, v_w_branch_pool, v_w_branch_attn, v_w_out, v_ffn2_norm, v_ffn2_w_gate_up, v_ffn2_w_down, v_final_norm):
    D = x.shape[-1]
    weights = dict(ffn1_norm=ffn1_norm, ffn1_w_gate_up=ffn1_w_gate_up, ffn1_w_down=ffn1_w_down, mix_norm=mix_norm,
                   w_in=w_in, pool_w_group=pool_w_group, pool_scale=pool_scale, w_branch_pool=w_branch_pool,
                   w_branch_attn=w_branch_attn, w_out=w_out, ffn2_norm=ffn2_norm, ffn2_w_gate_up=ffn2_w_gate_up,
                   ffn2_w_down=ffn2_w_down, final_norm=final_norm)
    first = dict(ffn1_norm=m_ffn1_norm, ffn1_w_gate_up=m_ffn1_w_gate_up, ffn1_w_down=m_ffn1_w_down,
                 mix_norm=m_mix_norm, w_in=m_w_in, pool_w_group=m_pool_w_group, pool_scale=m_pool_scale,
                 w_branch_pool=m_w_branch_pool, w_branch_attn=m_w_branch_attn, w_out=m_w_out,
                 ffn2_norm=m_ffn2_norm, ffn2_w_gate_up=m_ffn2_w_gate_up, ffn2_w_down=m_ffn2_w_down,
                 final_norm=m_final_norm)
    second = dict(ffn1_norm=v_ffn1_norm, ffn1_w_gate_up=v_ffn1_w_gate_up, ffn1_w_down=v_ffn1_w_down,
                  mix_norm=v_mix_norm, w_in=v_w_in, pool_w_group=v_pool_w_group, pool_scale=v_pool_scale,
                  w_branch_pool=v_w_branch_pool, w_branch_attn=v_w_branch_attn, w_out=v_w_out,
                  ffn2_norm=v_ffn2_norm, ffn2_w_gate_up=v_ffn2_w_gate_up, ffn2_w_down=v_ffn2_w_down,
                  final_norm=v_final_norm)
    order = list(weights)

    wgu1, = _all_gather([_pad_gate_up(_hidden_major(ffn1_w_gate_up))], name="all_gather_ffn1_gate_up", collective_id=0)
    wd1, = _all_gather([_pad_down(ffn1_w_down[0])], name="all_gather_ffn1_down", collective_id=10)
    win_g, = _all_gather([w_in[0].astype(BF16)], name="all_gather_w_in", collective_id=1)
    wbp_g, wba_g = _all_gather([w_branch_pool[0].astype(BF16), w_branch_attn[0].astype(BF16)],
                               name="all_gather_branches", collective_id=2)
    wout_g, = _all_gather([w_out[0].astype(BF16)], name="all_gather_w_out", collective_id=11)
    wgu2, wd2 = _all_gather([_pad_gate_up(_hidden_major(ffn2_w_gate_up)), _pad_down(ffn2_w_down[0])],
                            name="all_gather_ffn2", collective_id=3)
    wd1 = wd1.reshape(N_DEV * FF_SHARD_PAD, D)
    wd2 = wd2.reshape(N_DEV * FF_SHARD_PAD, D)
    wout_g = wout_g.reshape(D, D)

    cross_ids = {"ffn2_down": 4, "ffn2_gate_up": 5, "mix": 6, "ffn1_down": 7, "w_in": 8, "last": 9}
    small = ["ffn1_norm", "mix_norm", "ffn2_norm", "final_norm", "pool_scale", "pool_w_group"]

    def tile_rows(a):
        a = a.reshape(-1, 128)
        return jnp.pad(a, ((0, -a.shape[0] % 8), (0, 0)))

    def exchange(tag, group):
        if tag == "last":
            slab = jnp.concatenate([tile_rows(g) for g in group[1:-1]] + [jnp.broadcast_to(group[-1], (8, 128))], axis=0)
            partial, own = _chip_sums(group[0], name="chip_sums_last")
            landed, slabs = _cross_chips_and_gather(partial, slab, name="cross_chips_last", collective_id=cross_ids[tag])
            return [(own, landed), slabs], own
        sums = [_chip_sums(g, name=f"chip_sums_{tag}_{i}") for i, g in enumerate(group)]
        landed = _cross_chips([s[0] for s in sums], name="cross_chips_" + tag, collective_id=cross_ids[tag])
        token = sums[0][1] if len(sums) == 1 else sum(s[1][0, 0] for s in sums).reshape(1, 1)
        return [(s[1], l) for s, l in zip(sums, landed)], token

    norms = (ffn1_norm, mix_norm, ffn2_norm, final_norm.reshape(1, D))
    dx, sharded, slabs, last = _local_step(
        x[0], loss_target[0], norms, pool_w_group[0], pool_scale, wgu1, wd1, win_g, wbp_g, wba_g, wout_g, wgu2, wd2,
        exchange)
    names = ["ffn1_w_gate_up", "ffn1_w_down", "w_in", "w_branch_pool", "w_branch_attn", "w_out",
             "ffn2_w_gate_up", "ffn2_w_down"]
    handles = dict(zip(names, sharded))
    grads, delta, new_m, new_v = {}, {}, {}, {}
    after = last
    for k in ("ffn2_w_down", "ffn2_w_gate_up", "w_branch_pool", "w_branch_attn", "w_out", "w_in", "ffn1_w_down",
              "ffn1_w_gate_up"):
        g = _owner_sum(*handles[k], after, name="owner_sum_" + k)
        hidden_major = k.endswith("w_gate_up")
        g = _unpad_gate_up(g) if hidden_major else g[:weights[k].shape[1]]
        view = _hidden_major if hidden_major else (lambda a: a[0])
        back = (lambda a: jnp.swapaxes(a, 0, 1)[None]) if hidden_major else (lambda a: a[None])
        out = _adamw(view(weights[k]), g, view(first[k]), view(second[k]), name="adamw_" + k)
        after = out[0]
        grads[k] = back(g)
        delta[k], new_m[k], new_v[k] = (back(a) for a in out)

    rows = [weights[k].size // 128 for k in small]
    padded_rows = [-(-r // 8) * 8 for r in rows]
    starts = [sum(padded_rows[:i]) for i in range(len(rows) + 1)]
    total = _sum_devices(slabs, after, name="sum_replicated")
    loss_out = total[starts[-1], 0]
    small_w = jnp.concatenate([tile_rows(weights[k]) for k in small], axis=0)
    small_m = jnp.concatenate([tile_rows(first[k]) for k in small], axis=0)
    small_v = jnp.concatenate([tile_rows(second[k]) for k in small], axis=0)
    small_out = _adamw(small_w, total[:starts[-1]], small_m, small_v, name="adamw_replicated")
    for name_, start, n_rows in zip(small, starts, rows):
        shape = weights[name_].shape
        grads[name_] = total[start:start + n_rows].reshape(shape)
        delta[name_], new_m[name_], new_v[name_] = (a[start:start + n_rows].reshape(shape) for a in small_out)

    return (loss_out, dx[None], *[grads[k] for k in order], *[delta[k] for k in order],
            *[new_m[k] for k in order], *[new_v[k] for k in order])
```

```python
import functools

import jax
import jax.numpy as jnp
from jax import lax
from jax.experimental import pallas as pl
from jax.experimental.pallas import tpu as pltpu
from jax.experimental.pallas import tpu_sc as plsc

F32 = jnp.float32
BF16 = jnp.bfloat16
MESH = pl.DeviceIdType.MESH

RMS_EPS = 1e-6
N_DEV = 8
N_HEADS = 8
HEAD_DIM = 64
HEAD_PAIR = 2 * HEAD_DIM
POOL_WINDOWS = (2, 4, 8, 16)
POOL_GROUP = 128
POOL_WIDTH = 512
SB_WIDTH = 512
FF_SHARD = 352
FF_SHARD_PAD = 384
ATTN_BLOCK = 256
ATTN_SCALE = 0.125

ADAM_LR = 0.001
ADAM_B1 = 0.9
ADAM_B2 = 0.999
ADAM_EPS = 1e-08
ADAM_WD = 0.01
ADAM_STEP = 10

VMEM_LIMIT = 48 << 20
WGRAD_TOKENS = 2048


def _params(dims=None):
    return pltpu.CompilerParams(dimension_semantics=dims, vmem_limit_bytes=VMEM_LIMIT)


def _mm(a, b):
    return jnp.dot(a, b, preferred_element_type=F32)


def _mm_nt(a, b):
    return lax.dot_general(a, b, (((1,), (1,)), ((), ())), preferred_element_type=F32)


def _mm_tn(a, b):
    return lax.dot_general(a, b, (((0,), (0,)), ((), ())), preferred_element_type=F32)


def _row_tile(rows, cols):
    limit = max(8, (512 * 1024) // cols)
    return max(t for t in range(8, rows + 1, 8) if rows % t == 0 and (t <= limit or t == 8))


def _rstd(xf):
    return lax.rsqrt(jnp.mean(xf * xf, axis=-1, keepdims=True) + RMS_EPS)


def _rms_bwd(xf, gain, dn):
    r = _rstd(xf)
    xh = xf * r
    dgain = jnp.sum(dn * xh, axis=0, keepdims=True)
    dxh = dn * gain
    dx = r * (dxh - xh * jnp.mean(dxh * xh, axis=-1, keepdims=True))
    return dx, dgain


def _ffn_up(x, gain, wgu, *, tm, name):
    T, D = x.shape
    tm = min(tm, T)
    nb, bw = wgu.shape[0] // 2, wgu.shape[1]

    def body(x_ref, gain_ref, wg_ref, wu_ref, gu_ref, hid_ref, n_scr):
        @pl.when(pl.program_id(1) == 0)
        def _():
            xf = x_ref[...]
            n_scr[...] = (xf * _rstd(xf) * gain_ref[...]).astype(BF16)

        halves = (pl.ds(0, tm // 2), pl.ds(tm // 2, tm // 2))
        wg, wu = wg_ref[...], wu_ref[...]
        gus = [(_mm_nt(n_scr[rows, :], wg), _mm_nt(n_scr[rows, :], wu)) for rows in halves]
        for rows, (g, u) in zip(halves, gus):
            gu_ref[0, rows, :] = g.astype(BF16)
            gu_ref[1, rows, :] = u.astype(BF16)
            hid_ref[rows, :] = (g * jax.nn.sigmoid(g) * u).astype(BF16)

    return pl.pallas_call(
        body, name=name, grid=(T // tm, nb),
        in_specs=[
            pl.BlockSpec((tm, D), lambda i, j: (i, 0)),
            pl.BlockSpec((1, D), lambda i, j: (0, 0)),
            pl.BlockSpec((None, bw, D), lambda i, j: (j, 0, 0)),
            pl.BlockSpec((None, bw, D), lambda i, j: (j + nb, 0, 0)),
        ],
        out_specs=[
            pl.BlockSpec((2, tm, bw), lambda i, j: (0, i, j)),
            pl.BlockSpec((tm, bw), lambda i, j: (i, j)),
        ],
        out_shape=[jax.ShapeDtypeStruct((2, T, nb * bw), BF16), jax.ShapeDtypeStruct((T, nb * bw), BF16)],
        scratch_shapes=[pltpu.VMEM((tm, D), BF16)],
        compiler_params=_params(("arbitrary", "arbitrary")),
    )(x, gain, wgu, wgu)


def _ffn_down(x, hid, wd, *, tm, name):
    T, D = x.shape
    tm = min(tm, T)
    F = hid.shape[1]

    def body(x_ref, hid_ref, wd_ref, h_ref):
        h_ref[...] = x_ref[...] + 0.5 * _mm(hid_ref[...], wd_ref[...])

    return pl.pallas_call(
        body, name=name, grid=(T // tm,),
        in_specs=[
            pl.BlockSpec((tm, D), lambda i: (i, 0)),
            pl.BlockSpec((tm, F), lambda i: (i, 0)),
            pl.BlockSpec((F, D), lambda i: (0, 0)),
        ],
        out_specs=pl.BlockSpec((tm, D), lambda i: (i, 0)),
        out_shape=jax.ShapeDtypeStruct((T, D), F32),
        compiler_params=_params(("arbitrary",)),
    )(x, hid, wd)


AFTER = pl.BlockSpec(memory_space=pltpu.HBM)


def _in_hbm(token):
    return pltpu.with_memory_space_constraint(token, pltpu.HBM)


def _ffn_bwd(dh, df, x, gain, gu, wgu, wd, after, *, tm, name):
    T, D = x.shape
    tm = min(tm, T)
    nb, bw = wgu.shape[0] // 2, wgu.shape[1]

    def body(dh_ref, df_ref, x_ref, gain_ref, gu_ref, wg_ref, wu_ref, wd_ref, after_ref,
             dx_ref, dgain_ref, n_ref, dgu_ref, dn_acc):
        i, j = pl.program_id(0), pl.program_id(1)

        @pl.when(j == 0)
        def _():
            xf = x_ref[...]
            n_ref[...] = (xf * _rstd(xf) * gain_ref[...]).astype(BF16)
            dn_acc[...] = jnp.zeros_like(dn_acc)

        @pl.when((i == 0) & (j == 0))
        def _():
            dgain_ref[...] = jnp.zeros_like(dgain_ref)

        halves = (pl.ds(0, tm // 2), pl.ds(tm // 2, tm // 2))
        wd, wg, wu = wd_ref[...], wg_ref[...], wu_ref[...]
        dhids = [_mm_nt(df_ref[rows, :], wd) for rows in halves]
        for rows, dhid in zip(halves, dhids):
            g = gu_ref[0, rows, :].astype(F32)
            u = gu_ref[1, rows, :].astype(F32)
            s = jax.nn.sigmoid(g)
            silu = g * s
            dg = (dhid * u * (s * (1.0 + g * (1.0 - s)))).astype(BF16)
            du = (dhid * silu).astype(BF16)
            dgu_ref[0, rows, :] = dg
            dgu_ref[1, rows, :] = du
            dn_acc[rows, :] += _mm(dg, wg) + _mm(du, wu)

        @pl.when(j == nb - 1)
        def _():
            dx, dgain = _rms_bwd(x_ref[...], gain_ref[...], dn_acc[...])
            dx_ref[...] = dh_ref[...] + dx
            dgain_ref[...] += dgain

    row = lambda i, j: (i, 0)
    return pl.pallas_call(
        body, name=name, grid=(T // tm, nb),
        in_specs=[
            pl.BlockSpec((tm, D), row),
            pl.BlockSpec((tm, D), row),
            pl.BlockSpec((tm, D), row),
            pl.BlockSpec((1, D), lambda i, j: (0, 0)),
            pl.BlockSpec((2, tm, bw), lambda i, j: (0, i, j)),
            pl.BlockSpec((None, bw, D), lambda i, j: (j, 0, 0)),
            pl.BlockSpec((None, bw, D), lambda i, j: (j + nb, 0, 0)),
            pl.BlockSpec((bw, D), lambda i, j: (j, 0)),
            AFTER,
        ],
        out_specs=[
            pl.BlockSpec((tm, D), row),
            pl.BlockSpec((1, D), lambda i, j: (0, 0)),
            pl.BlockSpec((tm, D), row),
            pl.BlockSpec((2, tm, bw), lambda i, j: (0, i, j)),
        ],
        out_shape=[
            jax.ShapeDtypeStruct((T, D), F32),
            jax.ShapeDtypeStruct((1, D), F32),
            jax.ShapeDtypeStruct((T, D), BF16),
            jax.ShapeDtypeStruct((2, T, nb * bw), BF16),
        ],
        scratch_shapes=[pltpu.VMEM((tm, D), F32)],
        compiler_params=_params(("arbitrary", "arbitrary")),
    )(dh, df, x, gain, gu, wgu, wgu, wd, _in_hbm(after))


def _wgrad(a, b, *, grid, a_spec, b_spec, out_spec, out_shape, acc_shape, name, split_lanes=0):
    nk = grid[2]

    def body(a_ref, b_ref, o_ref, acc):
        k = pl.program_id(2)

        @pl.when(k == 0)
        def _():
            acc[...] = jnp.zeros_like(acc)

        acc[...] += _mm_tn(a_ref[...].astype(BF16), b_ref[...].astype(BF16))

        @pl.when(k == nk - 1)
        def _():
            if split_lanes:
                for e in range(o_ref.shape[0]):
                    o_ref[e] = acc[:, e * split_lanes:(e + 1) * split_lanes].astype(o_ref.dtype)
            else:
                o_ref[...] = acc[...].astype(o_ref.dtype)

    return pl.pallas_call(
        body, name=name, grid=grid, in_specs=[a_spec, b_spec], out_specs=out_spec,
        out_shape=jax.ShapeDtypeStruct(out_shape, BF16),
        scratch_shapes=[pltpu.VMEM(acc_shape, F32)],
        compiler_params=_params(("arbitrary", "arbitrary", "arbitrary")),
    )(a, b)


def _wgrad_gate_up(n, dgu, *, tk, name):
    T, D = n.shape
    tk = min(tk, T)
    bw = FF_SHARD_PAD * 2
    nb = dgu.shape[2] // bw
    return _wgrad(
        dgu, n, grid=(2 * nb, 1, T // tk), name=name,
        a_spec=pl.BlockSpec((None, tk, bw), lambda m, c, k: (m // nb, k, m % nb)),
        b_spec=pl.BlockSpec((tk, D), lambda m, c, k: (k, 0)),
        out_spec=pl.BlockSpec((None, bw, D), lambda m, c, k: (m, 0, 0)),
        out_shape=(2 * nb, bw, D), acc_shape=(bw, D))


def _wgrad_down(hid, df, *, tk, name):
    T, D = df.shape
    tk = min(tk, T)
    bw = FF_SHARD_PAD * 2
    nb = hid.shape[1] // bw
    return _wgrad(
        hid, df, grid=(nb, 1, T // tk), name=name,
        a_spec=pl.BlockSpec((tk, bw), lambda m, c, k: (k, m)),
        b_spec=pl.BlockSpec((tk, D), lambda m, c, k: (k, 0)),
        out_spec=pl.BlockSpec((bw, D), lambda m, c, k: (m, 0)),
        out_shape=(nb * bw, D), acc_shape=(bw, D))


def _wgrad_in(un, dproj, *, tk, name):
    T, D = un.shape
    tk = min(tk, T)
    bw = dproj.shape[1] // N_DEV
    return _wgrad(
        un, dproj, grid=(1, N_DEV, T // tk), name=name,
        a_spec=pl.BlockSpec((tk, D), lambda m, c, k: (k, 0)),
        b_spec=pl.BlockSpec((tk, bw), lambda m, c, k: (k, c)),
        out_spec=pl.BlockSpec((None, D, bw), lambda m, c, k: (c, 0, 0)),
        out_shape=(N_DEV, D, bw), acc_shape=(D, bw))


def _wgrad_full(a, b, *, tk, name, split_lanes=0):
    T, M = a.shape
    tk = min(tk, T)
    N = b.shape[1]
    if split_lanes:
        out_shape = (N // split_lanes, M, split_lanes)
        out_spec = pl.BlockSpec(out_shape, lambda m, c, k: (0, 0, 0))
    else:
        out_shape = (M, N)
        out_spec = pl.BlockSpec(out_shape, lambda m, c, k: (0, 0))
    return _wgrad(
        a, b, grid=(1, 1, T // tk), name=name,
        a_spec=pl.BlockSpec((tk, M), lambda m, c, k: (k, 0)),
        b_spec=pl.BlockSpec((tk, N), lambda m, c, k: (k, 0)),
        out_spec=out_spec, out_shape=out_shape, acc_shape=(M, N), split_lanes=split_lanes)


def _loss_bwd(h, target, gain, *, tm, name):
    T, D = h.shape
    tm = min(tm, T)

    def body(h_ref, t_ref, gain_ref, dh_ref, df_ref, loss_ref, dgain_ref):
        @pl.when(pl.program_id(0) == 0)
        def _():
            loss_ref[...] = jnp.zeros_like(loss_ref)
            dgain_ref[...] = jnp.zeros_like(dgain_ref)

        xf = h_ref[...]
        gain = gain_ref[...]
        err = xf * _rstd(xf) * gain - t_ref[...]
        loss_ref[...] += 0.5 * jnp.sum(jnp.mean(err * err, axis=-1, keepdims=True), axis=0, keepdims=True)
        dx, dgain = _rms_bwd(xf, gain, err * (1.0 / D))
        dh_ref[...] = dx
        df_ref[...] = (0.5 * dx).astype(BF16)
        dgain_ref[...] += dgain

    row = lambda i: (i, 0)
    fixed = lambda i: (0, 0)
    return pl.pallas_call(
        body, name=name, grid=(T // tm,),
        in_specs=[pl.BlockSpec((tm, D), row), pl.BlockSpec((tm, D), row), pl.BlockSpec((1, D), fixed)],
        out_specs=[pl.BlockSpec((tm, D), row), pl.BlockSpec((tm, D), row), pl.BlockSpec((1, 128), fixed),
                   pl.BlockSpec((1, D), fixed)],
        out_shape=[jax.ShapeDtypeStruct((T, D), F32), jax.ShapeDtypeStruct((T, D), BF16),
                   jax.ShapeDtypeStruct((1, 128), F32), jax.ShapeDtypeStruct((1, D), F32)],
        compiler_params=_params(("arbitrary",)),
    )(h, target, gain)


def _inproj_fwd(h, gain, w_in, *, tm, name):
    T, D = h.shape
    tm = min(tm, T)
    nb, bw = w_in.shape[0], w_in.shape[2]

    def body(h_ref, gain_ref, w_ref, un_ref, proj_ref):
        @pl.when(pl.program_id(1) == 0)
        def _():
            xf = h_ref[...]
            un_ref[...] = (xf * _rstd(xf) * gain_ref[...]).astype(BF16)

        proj_ref[...] = _mm(un_ref[...], w_ref[...])

    return pl.pallas_call(
        body, name=name, grid=(T // tm, nb),
        in_specs=[
            pl.BlockSpec((tm, D), lambda i, j: (i, 0)),
            pl.BlockSpec((1, D), lambda i, j: (0, 0)),
            pl.BlockSpec((None, D, bw), lambda i, j: (j, 0, 0)),
        ],
        out_specs=[pl.BlockSpec((tm, D), lambda i, j: (i, 0)), pl.BlockSpec((tm, bw), lambda i, j: (i, j))],
        out_shape=[jax.ShapeDtypeStruct((T, D), BF16), jax.ShapeDtypeStruct((T, nb * bw), F32)],
        compiler_params=_params(("arbitrary", "arbitrary")),
    )(h, gain, w_in)


def _inproj_bwd(dproj, dh, h, gain, w_in, *, tm, name):
    T, D = h.shape
    tm = min(tm, T)
    nb, bw = w_in.shape[0], w_in.shape[2]

    def body(dp_ref, dh_ref, h_ref, gain_ref, w_ref, dx_ref, df_ref, dgain_ref, acc):
        i, j = pl.program_id(0), pl.program_id(1)

        @pl.when(j == 0)
        def _():
            acc[...] = jnp.zeros_like(acc)

        @pl.when((i == 0) & (j == 0))
        def _():
            dgain_ref[...] = jnp.zeros_like(dgain_ref)

        acc[...] += _mm_nt(dp_ref[...], w_ref[...])

        @pl.when(j == nb - 1)
        def _():
            dx, dgain = _rms_bwd(h_ref[...], gain_ref[...], acc[...])
            dh_in = dh_ref[...] + dx
            dx_ref[...] = dh_in
            df_ref[...] = (0.5 * dh_in).astype(BF16)
            dgain_ref[...] += dgain

    row = lambda i, j: (i, 0)
    return pl.pallas_call(
        body, name=name, grid=(T // tm, nb),
        in_specs=[
            pl.BlockSpec((tm, bw), lambda i, j: (i, j)),
            pl.BlockSpec((tm, D), row),
            pl.BlockSpec((tm, D), row),
            pl.BlockSpec((1, D), lambda i, j: (0, 0)),
            pl.BlockSpec((None, D, bw), lambda i, j: (j, 0, 0)),
        ],
        out_specs=[pl.BlockSpec((tm, D), row), pl.BlockSpec((tm, D), row), pl.BlockSpec((1, D), lambda i, j: (0, 0))],
        out_shape=[jax.ShapeDtypeStruct((T, D), F32), jax.ShapeDtypeStruct((T, D), BF16),
                   jax.ShapeDtypeStruct((1, D), F32)],
        scratch_shapes=[pltpu.VMEM((tm, D), F32)],
        compiler_params=_params(("arbitrary", "arbitrary")),
    )(dproj, dh, h, gain, w_in)


def _window_sum(x, row, doublings, *, backward):
    T = x.shape[0]
    s = x
    for k in range(doublings):
        sh = 1 << k
        if backward:
            s = s + jnp.where(row < T - sh, pltpu.roll(s, T - sh, 0), 0.0)
        else:
            s = s + jnp.where(row >= sh, pltpu.roll(s, sh, 0), 0.0)
    return s


def _pool_fwd(proj, w_group, scale, *, name):
    T = proj.shape[0]

    def body(xp_ref, w_ref, scale_ref, p_ref):
        row = lax.broadcasted_iota(jnp.int32, (T, POOL_GROUP), 0)
        for gi, window in enumerate(POOL_WINDOWS):
            cols = slice(gi * POOL_GROUP, (gi + 1) * POOL_GROUP)
            x = xp_ref[:, cols]
            inv_count = 1.0 / jnp.minimum(row + 1, window).astype(F32)
            yc = _window_sum(x, row, gi + 1, backward=False) * inv_count - x
            pre = _mm(yc.astype(BF16), w_ref[gi].astype(BF16))
            p_ref[:, cols] = pre * scale_ref[:, cols]

    return pl.pallas_call(
        body, name=name, grid=(1,),
        in_specs=[
            pl.BlockSpec((T, POOL_WIDTH), lambda i: (0, 0)),
            pl.BlockSpec(w_group.shape, lambda i: (0, 0, 0)),
            pl.BlockSpec((1, POOL_WIDTH), lambda i: (0, 0)),
        ],
        out_specs=pl.BlockSpec((T, POOL_WIDTH), lambda i: (0, 0)),
        out_shape=jax.ShapeDtypeStruct((T, POOL_WIDTH), F32),
        compiler_params=_params(("arbitrary",)),
    )(proj, w_group, scale)


def _pool_bwd(dp, proj, w_group, scale, *, name):
    T = proj.shape[0]

    def body(dp_ref, xp_ref, w_ref, scale_ref, dxp_ref, dw_ref, dscale_ref):
        row = lax.broadcasted_iota(jnp.int32, (T, POOL_GROUP), 0)
        for gi, window in enumerate(POOL_WINDOWS):
            cols = slice(gi * POOL_GROUP, (gi + 1) * POOL_GROUP)
            x = xp_ref[:, cols]
            inv_count = 1.0 / jnp.minimum(row + 1, window).astype(F32)
            yc = (_window_sum(x, row, gi + 1, backward=False) * inv_count - x).astype(BF16)
            w = w_ref[gi].astype(BF16)
            pre = _mm(yc, w)
            dpg = dp_ref[:, cols]
            dscale_ref[:, cols] = jnp.sum(dpg * pre, axis=0, keepdims=True)
            dpre = (dpg * scale_ref[:, cols]).astype(BF16)
            dw_ref[gi] = _mm_tn(yc, dpre)
            dyc = _mm_nt(dpre, w)
            dxp_ref[:, cols] = _window_sum(dyc * inv_count, row, gi + 1, backward=True) - dyc

    return pl.pallas_call(
        body, name=name, grid=(1,),
        in_specs=[
            pl.BlockSpec((T, POOL_WIDTH), lambda i: (0, 0)),
            pl.BlockSpec((T, POOL_WIDTH), lambda i: (0, 0)),
            pl.BlockSpec(w_group.shape, lambda i: (0, 0, 0)),
            pl.BlockSpec((1, POOL_WIDTH), lambda i: (0, 0)),
        ],
        out_specs=[
            pl.BlockSpec((T, POOL_WIDTH), lambda i: (0, 0)),
            pl.BlockSpec(w_group.shape, lambda i: (0, 0, 0)),
            pl.BlockSpec((1, POOL_WIDTH), lambda i: (0, 0)),
        ],
        out_shape=[jax.ShapeDtypeStruct((T, POOL_WIDTH), F32), jax.ShapeDtypeStruct(w_group.shape, F32),
                   jax.ShapeDtypeStruct((1, POOL_WIDTH), F32)],
        compiler_params=_params(("arbitrary",)),
    )(dp, proj, w_group, scale)


ATTN_STRIP = 32


def _log_sigmoids(z):
    lb = jnp.minimum(z, 0.0) - jnp.log(1.0 + jnp.exp(-jnp.abs(z)))
    return lb, lb - z


def _transposed_blocks(x_ref, blocks_scr, tq):
    for b in range(blocks_scr.shape[0]):
        blocks_scr[b] = x_ref[b * tq:(b + 1) * tq, :].T.astype(BF16)


def _split_bf16(x):
    hi = x.astype(BF16)
    return hi, (x - hi.astype(F32)).astype(BF16)


def _strips(n):
    return [slice(i, i + ATTN_STRIP) for i in range(0, n, ATTN_STRIP)]


def _rows(parts):
    return jnp.concatenate(parts, axis=0)


def _attn_specs(T, tq):
    q_col = POOL_WIDTH // HEAD_PAIR
    k_col = q_col + SB_WIDTH // HEAD_PAIR
    v_col = k_col + SB_WIDTH // HEAD_PAIR
    return [
        pl.BlockSpec((tq, HEAD_PAIR), lambda p, i: (i, q_col + p)),
        pl.BlockSpec((T, HEAD_PAIR), lambda p, i: (0, k_col + p)),
        pl.BlockSpec((T, HEAD_PAIR), lambda p, i: (0, v_col + p)),
    ]


def _attn_fwd(proj, *, name):
    T = proj.shape[0]
    tq = ATTN_BLOCK

    def body(q_ref, k_ref, v_ref, o_ref, lt_ref, kt_scr, vb_scr):
        qi = pl.program_id(1)

        @pl.when(qi == 0)
        def _():
            _transposed_blocks(k_ref, kt_scr, tq)
            vb_scr[...] = v_ref[...].astype(BF16)

        head0 = lax.broadcasted_iota(jnp.int32, (tq, HEAD_PAIR), 1) < HEAD_DIM
        q = q_ref[...] * ATTN_SCALE
        qs = (jnp.where(head0, q, 0.0).astype(BF16), jnp.where(head0, 0.0, q).astype(BF16))
        r = lax.broadcasted_iota(jnp.int32, (tq, tq), 0)
        c = lax.broadcasted_iota(jnp.int32, (tq, tq), 1)
        later = (r > c).astype(BF16)
        later2 = _rows([later, later])
        causal = lambda rows: c[rows] < r[rows]
        strips = _strips(tq)

        def log_terms(z, valid):
            lbs, his, los, sums = [], [], [], []
            for rows in strips:
                lb, lm = _log_sigmoids(z[rows])
                if valid is not None:
                    lm = jnp.where(valid(rows), lm, 0.0)
                hi, lo = _split_bf16(lm)
                lbs.append(lb)
                his.append(hi)
                los.append(lo)
                sums.append(jnp.sum(lm, axis=1, keepdims=True))
            return lbs, jnp.concatenate([_rows(his), _rows(los)], axis=1), _rows(sums)

        def weights(lbs, run, after, valid):
            parts = []
            for rows, lb in zip(strips, lbs):
                a = jnp.exp(lb + run[rows] + after[rows])
                if valid is not None:
                    a = jnp.where(valid(rows), a, 0.0)
                parts.append(a.astype(BF16))
            return _rows(parts)

        def block(kj, carry, valid):
            kt = kt_scr[kj]
            vb = vb_scr[pl.ds(pl.multiple_of(kj * tq, tq), tq), :]
            run0, o0, run1, o1 = carry
            z0 = _mm(qs[0], kt)
            z1 = _mm(qs[1], kt)
            lbs0, split0, sums0 = log_terms(z0, valid)
            after0 = _mm(split0, later2)
            lbs1, split1, sums1 = log_terms(z1, valid)
            after1 = _mm(split1, later2)
            o0 = o0 + _mm(weights(lbs0, run0, after0, valid), vb)
            o1 = o1 + _mm(weights(lbs1, run1, after1, valid), vb)
            return run0 + sums0, o0, run1 + sums1, o1

        zero = (jnp.zeros((tq, 1), F32), jnp.zeros((tq, HEAD_PAIR), F32))
        carry = block(qi, zero + zero, causal)
        carry = lax.fori_loop(0, qi, lambda it, cr: block(qi - 1 - it, cr, None), carry)
        o_ref[...] = jnp.where(head0, carry[1], carry[3])
        lt_ref[...] = jnp.where(head0, carry[0], carry[2])

    out_spec = pl.BlockSpec((tq, HEAD_PAIR), lambda p, i: (i, p))
    return pl.pallas_call(
        body, name=name, grid=(N_HEADS // 2, T // tq),
        in_specs=_attn_specs(T, tq), out_specs=[out_spec, out_spec],
        out_shape=[jax.ShapeDtypeStruct((T, SB_WIDTH), F32), jax.ShapeDtypeStruct((T, SB_WIDTH), F32)],
        scratch_shapes=[pltpu.VMEM((T // tq, HEAD_PAIR, tq), BF16), pltpu.VMEM((T, HEAD_PAIR), BF16)],
        compiler_params=_params(("arbitrary", "arbitrary")),
    )(proj, proj, proj)


def _attn_bwd(proj, do, ltot, after, *, name):
    T = proj.shape[0]
    tq = ATTN_BLOCK

    def body(q_ref, k_ref, v_ref, do_ref, lt_ref, after_ref, dq_ref, dkt_ref, dvt_ref, kb_scr, kt_scr, vt_scr):
        qi = pl.program_id(1)

        @pl.when(qi == 0)
        def _():
            kb_scr[...] = k_ref[...].astype(BF16)
            _transposed_blocks(k_ref, kt_scr, tq)
            _transposed_blocks(v_ref, vt_scr, tq)
            dkt_ref[...] = jnp.zeros_like(dkt_ref)
            dvt_ref[...] = jnp.zeros_like(dvt_ref)

        head0 = lax.broadcasted_iota(jnp.int32, (tq, HEAD_PAIR), 1) < HEAD_DIM
        q, do_, lt = q_ref[...] * ATTN_SCALE, do_ref[...], lt_ref[...]
        qs = (jnp.where(head0, q, 0.0).astype(BF16), jnp.where(head0, 0.0, q).astype(BF16))
        q_heads = (jnp.where(head0, q, 0.0), jnp.where(head0, 0.0, q))
        do_heads = (jnp.where(head0, do_, 0.0), jnp.where(head0, 0.0, do_))
        dos = tuple(d.astype(BF16) for d in do_heads)
        qts = tuple(x.T.astype(BF16) for x in q_heads)
        dots = tuple(d.T.astype(BF16) for d in do_heads)
        lts = (jnp.max(jnp.where(head0, lt, -jnp.inf), axis=1, keepdims=True),
               jnp.max(jnp.where(head0, -jnp.inf, lt), axis=1, keepdims=True))
        r = lax.broadcasted_iota(jnp.int32, (tq, tq), 0)
        c = lax.broadcasted_iota(jnp.int32, (tq, tq), 1)
        upto = (r <= c).astype(BF16)
        before = (r < c).astype(BF16)
        upto2, before2 = _rows([upto, upto]), _rows([before, before])
        causal = lambda rows: c[rows] < r[rows]
        strips = _strips(tq)

        def log_terms(z, valid):
            lbs, his, los, sums = [], [], [], []
            for rows in strips:
                lb, lm = _log_sigmoids(z[rows])
                if valid is not None:
                    lm = jnp.where(valid(rows), lm, 0.0)
                hi, lo = _split_bf16(lm)
                lbs.append(lb)
                his.append(hi)
                los.append(lo)
                sums.append(jnp.sum(lm, axis=1, keepdims=True))
            return lbs, jnp.concatenate([_rows(his), _rows(los)], axis=1), _rows(sums)

        def weights(lbs, rest, lm_upto, da, valid):
            a_parts, es, his, los, sums = [], [], [], [], []
            for rows, lb in zip(strips, lbs):
                a = jnp.exp(lb + (rest[rows] - lm_upto[rows]))
                if valid is not None:
                    a = jnp.where(valid(rows), a, 0.0)
                e = da[rows] * a
                hi, lo = _split_bf16(e)
                a_parts.append(a.astype(BF16))
                es.append(e)
                his.append(hi)
                los.append(lo)
                sums.append(jnp.sum(e, axis=1, keepdims=True))
            return _rows(a_parts), es, jnp.concatenate([_rows(his), _rows(los)], axis=1), _rows(sums)

        def score_grads(lbs, es, run_e, e_before, valid):
            parts = []
            for rows, lb, e in zip(strips, lbs, es):
                beta = jnp.exp(lb)
                dz = e * (1.0 - beta) - (run_e[rows] + e_before[rows]) * beta
                if valid is not None:
                    dz = jnp.where(valid(rows), dz, 0.0)
                parts.append(dz.astype(BF16))
            return _rows(parts)

        def block(kj, carry, valid):
            off = pl.multiple_of(kj * tq, tq)
            kb, kt, vt = kb_scr[pl.ds(off, tq), :], kt_scr[kj], vt_scr[kj]
            run_lm0, run_e0, dq0, run_lm1, run_e1, dq1 = carry
            z0, da0 = _mm(qs[0], kt), _mm(dos[0], vt)
            z1, da1 = _mm(qs[1], kt), _mm(dos[1], vt)
            lbs0, split0, lm_sums0 = log_terms(z0, valid)
            lm_upto0 = _mm(split0, upto2)
            lbs1, split1, lm_sums1 = log_terms(z1, valid)
            lm_upto1 = _mm(split1, upto2)
            a0, es0, split0, e_sums0 = weights(lbs0, lts[0] - run_lm0, lm_upto0, da0, valid)
            e_before0 = _mm(split0, before2)
            a1, es1, split1, e_sums1 = weights(lbs1, lts[1] - run_lm1, lm_upto1, da1, valid)
            e_before1 = _mm(split1, before2)
            dz0 = score_grads(lbs0, es0, run_e0, e_before0, valid)
            dkt_blk = _mm(qts[0], dz0)
            dvt_blk = _mm(dots[0], a0)
            dq0 = dq0 + _mm(dz0, kb)
            dz1 = score_grads(lbs1, es1, run_e1, e_before1, valid)
            dkt_ref[kj] += dkt_blk + _mm(qts[1], dz1)
            dvt_ref[kj] += dvt_blk + _mm(dots[1], a1)
            dq1 = dq1 + _mm(dz1, kb)
            return run_lm0 + lm_sums0, run_e0 + e_sums0, dq0, run_lm1 + lm_sums1, run_e1 + e_sums1, dq1

        zero = (jnp.zeros((tq, 1), F32), jnp.zeros((tq, 1), F32), jnp.zeros((tq, HEAD_PAIR), F32))
        carry = lax.fori_loop(0, qi, lambda kj, cr: block(kj, cr, None), zero + zero)
        carry = block(qi, carry, causal)
        dq_ref[...] = jnp.where(head0, carry[2], carry[5]) * ATTN_SCALE

    blk = pl.BlockSpec((tq, HEAD_PAIR), lambda p, i: (i, p))
    seq = pl.BlockSpec((T // tq, HEAD_PAIR, tq), lambda p, i: (0, p, 0))
    transposed = jax.ShapeDtypeStruct((T // tq, SB_WIDTH, tq), F32)
    return pl.pallas_call(
        body, name=name, grid=(N_HEADS // 2, T // tq),
        in_specs=_attn_specs(T, tq) + [blk, blk, AFTER], out_specs=[blk, seq, seq],
        out_shape=[jax.ShapeDtypeStruct((T, SB_WIDTH), F32), transposed, transposed],
        scratch_shapes=[pltpu.VMEM((T, HEAD_PAIR), BF16), pltpu.VMEM((T // tq, HEAD_PAIR, tq), BF16),
                        pltpu.VMEM((T // tq, HEAD_PAIR, tq), BF16)],
        compiler_params=_params(("arbitrary", "arbitrary")),
    )(proj, proj, proj, do, ltot, _in_hbm(after))


def _branch(act_bf16, w_ref):
    return jnp.concatenate([_mm(act_bf16, w_ref[e]) for e in range(w_ref.shape[0])], axis=1)


def _mix_specs(T, D, tm, wbp, w_out):
    gate_col = (POOL_WIDTH + 3 * SB_WIDTH) // D
    row = lambda i: (i, 0)
    return [
        pl.BlockSpec((tm, D), row),
        pl.BlockSpec((tm, POOL_WIDTH), row),
        pl.BlockSpec((tm, SB_WIDTH), row),
        pl.BlockSpec((tm, D), lambda i: (i, gate_col)),
        pl.BlockSpec((tm, D), lambda i: (i, gate_col + 1)),
        pl.BlockSpec(wbp.shape, lambda i: (0, 0, 0)),
        pl.BlockSpec(wbp.shape, lambda i: (0, 0, 0)),
        pl.BlockSpec(w_out.shape, lambda i: (0, 0)),
    ]


def _mix_fwd(h, p, o, proj, wbp, wba, w_out, *, tm, name):
    T, D = h.shape
    tm = min(tm, T)

    def body(h_ref, p_ref, o_ref, glp_ref, gls_ref, wbp_ref, wba_ref, wout_ref, hout_ref, m_ref):
        yp = _branch(p_ref[...].astype(BF16), wbp_ref)
        ys = _branch(o_ref[...].astype(BF16), wba_ref)
        m = (jax.nn.sigmoid(glp_ref[...]) * yp + jax.nn.sigmoid(gls_ref[...]) * ys).astype(BF16)
        m_ref[...] = m
        hout_ref[...] = h_ref[...] + _mm(m, wout_ref[...])

    row = lambda i: (i, 0)
    return pl.pallas_call(
        body, name=name, grid=(T // tm,),
        in_specs=_mix_specs(T, D, tm, wbp, w_out),
        out_specs=[pl.BlockSpec((tm, D), row), pl.BlockSpec((tm, D), row)],
        out_shape=[jax.ShapeDtypeStruct((T, D), F32), jax.ShapeDtypeStruct((T, D), BF16)],
        compiler_params=_params(("arbitrary",)),
    )(h, p, o, proj, proj, wbp, wba, w_out)


def _mix_bwd(dh, p, o, proj, wbp, wba, w_out, after, *, tm, name):
    T, D = dh.shape
    tm = min(tm, T)
    bw = wbp.shape[2]

    def body(dh_ref, p_ref, o_ref, glp_ref, gls_ref, wbp_ref, wba_ref, wout_ref, after_ref,
             dyp_ref, dys_ref, dp_ref, do_ref, dgl_ref):
        dm = _mm_nt(dh_ref[...].astype(BF16), wout_ref[...])
        yp = _branch(p_ref[...].astype(BF16), wbp_ref)
        ys = _branch(o_ref[...].astype(BF16), wba_ref)
        gp = jax.nn.sigmoid(glp_ref[...])
        gs = jax.nn.sigmoid(gls_ref[...])
        dyp = (dm * gp).astype(BF16)
        dys = (dm * gs).astype(BF16)
        dyp_ref[...] = dyp
        dys_ref[...] = dys
        dgl_ref[:, :D] = (dm * yp * gp * (1.0 - gp)).astype(BF16)
        dgl_ref[:, D:] = (dm * ys * gs * (1.0 - gs)).astype(BF16)
        dp = jnp.zeros(dp_ref.shape, F32)
        do_ = jnp.zeros(do_ref.shape, F32)
        for e in range(wbp_ref.shape[0]):
            dp += _mm_nt(dyp[:, e * bw:(e + 1) * bw], wbp_ref[e])
            do_ += _mm_nt(dys[:, e * bw:(e + 1) * bw], wba_ref[e])
        dp_ref[...] = dp
        do_ref[...] = do_

    row = lambda i: (i, 0)
    return pl.pallas_call(
        body, name=name, grid=(T // tm,),
        in_specs=_mix_specs(T, D, tm, wbp, w_out) + [AFTER],
        out_specs=[pl.BlockSpec((tm, D), row), pl.BlockSpec((tm, D), row), pl.BlockSpec((tm, POOL_WIDTH), row),
                   pl.BlockSpec((tm, SB_WIDTH), row), pl.BlockSpec((tm, 2 * D), row)],
        out_shape=[jax.ShapeDtypeStruct((T, D), BF16), jax.ShapeDtypeStruct((T, D), BF16),
                   jax.ShapeDtypeStruct((T, POOL_WIDTH), F32), jax.ShapeDtypeStruct((T, SB_WIDTH), F32),
                   jax.ShapeDtypeStruct((T, 2 * D), BF16)],
        compiler_params=_params(("arbitrary",)),
    )(dh, p, o, proj, proj, wbp, wba, w_out, _in_hbm(after))


def _adamw(w, g, m, v, *, name):
    R, C = w.shape
    tr = _row_tile(R, C)

    def body(w_ref, g_ref, m_ref, v_ref, d_ref, nm_ref, nv_ref):
        g_ = g_ref[...]
        m_ = ADAM_B1 * m_ref[...] + (1.0 - ADAM_B1) * g_
        v_ = ADAM_B2 * v_ref[...] + (1.0 - ADAM_B2) * (g_ * g_)
        m_hat = m_ / (1.0 - ADAM_B1 ** ADAM_STEP)
        v_hat = v_ / (1.0 - ADAM_B2 ** ADAM_STEP)
        d_ref[...] = -ADAM_LR * (m_hat / (jnp.sqrt(v_hat) + ADAM_EPS) + ADAM_WD * w_ref[...])
        nm_ref[...] = m_
        nv_ref[...] = v_

    spec = pl.BlockSpec((tr, C), lambda i: (i, 0))
    return pl.pallas_call(
        body, name=name, grid=(R // tr,), in_specs=[spec] * 4, out_specs=[spec] * 3,
        out_shape=[jax.ShapeDtypeStruct((R, C), F32)] * 3,
        compiler_params=_params(("arbitrary",)),
    )(w, g, m, v)


def _position():
    return lax.axis_index("x"), lax.axis_index("y"), lax.axis_index("c")


def _all_gather(shards, *, name, collective_id):
    n = len(shards)
    n_copies = 9

    def body(*refs):
        ins, outs = refs[:n], refs[n:2 * n]
        send_sems, recv_sems, local_sems = refs[2 * n:]
        x, y, c = _position()
        me, sibling = (x, y, c), (x, y, 1 - c)
        x_nbr, y_nbr, diagonal = (1 - x, y, c), (x, 1 - y, c), (1 - x, 1 - y, c)
        other = lambda pos: (pos[0], pos[1], 1 - c)

        barrier = pltpu.get_barrier_semaphore()
        for peer in (sibling, x_nbr, y_nbr):
            pl.semaphore_signal(barrier, inc=1, device_id=peer, device_id_type=MESH)
        pl.semaphore_wait(barrier, 3)

        def block(a, pos, half=None):
            ref = outs[a].at[4 * pos[0] + 2 * pos[1] + pos[2]]
            rows = ref.shape[0] // 2
            return ref if half is None else ref.at[pl.ds(half * rows, rows)]

        def copy(a, k, pos, to, half=None, src=None):
            return pltpu.make_async_remote_copy(
                src_ref=block(a, pos, half) if src is None else src, dst_ref=block(a, pos, half),
                send_sem=send_sems.at[n_copies * a + k], recv_sem=recv_sems.at[n_copies * a + k],
                device_id=to, device_id_type=MESH)

        started = []
        for a in range(n):
            mine = pltpu.make_async_copy(ins[a], block(a, me), local_sems.at[a])
            mine.start()
            started.append(mine)
        sends = []
        for a in range(n):
            sends += [copy(a, 1, me, x_nbr, src=ins[a]), copy(a, 2, me, y_nbr, src=ins[a]),
                      copy(a, 0, me, sibling, src=ins[a])]
        for cp in sends:
            cp.start()

        def pass_on(copies):
            for cp in copies:
                cp.start()
                sends.append(cp)

        for a in range(n):
            copy(a, 1, x_nbr, me).wait_recv()
            pass_on([copy(a, 5, x_nbr, y_nbr, half=0), copy(a, 3, x_nbr, sibling)])
            copy(a, 2, y_nbr, me).wait_recv()
            pass_on([copy(a, 6, y_nbr, x_nbr, half=1), copy(a, 4, y_nbr, sibling)])
        for a in range(n):
            copy(a, 5, diagonal, me, half=0).wait_recv()
            pass_on([copy(a, 7, diagonal, sibling, half=0)])
            copy(a, 6, diagonal, me, half=1).wait_recv()
            pass_on([copy(a, 8, diagonal, sibling, half=1)])
        for a in range(n):
            copy(a, 0, sibling, me).wait_recv()
            copy(a, 3, other(x_nbr), me).wait_recv()
            copy(a, 4, other(y_nbr), me).wait_recv()
            copy(a, 7, other(diagonal), me, half=0).wait_recv()
            copy(a, 8, other(diagonal), me, half=1).wait_recv()
        for cp in sends:
            cp.wait_send()
        for cp in started:
            cp.wait()

    return pl.kernel(
        body, name=name,
        out_type=[jax.ShapeDtypeStruct((N_DEV,) + s.shape, s.dtype) for s in shards],
        mesh=plsc.ScalarSubcoreMesh(axis_name="sequencer", num_cores=1),
        scratch_types=[pltpu.SemaphoreType.DMA((n_copies * n,)), pltpu.SemaphoreType.DMA((n_copies * n,)),
                       pltpu.SemaphoreType.DMA((n,))],
        compiler_params=pltpu.CompilerParams(collective_id=collective_id),
    )(*shards)


def _chip_sums(grads, *, name):
    _, R, C = grads.shape
    rc = 128 if R % 128 == 0 else R

    def body(g_ref, partial, out_ref, mine, theirs, send_sems, recv_sems, local_sems):
        x, y, c = _position()
        my_chip = 2 * x + y

        def swap(s):
            return pltpu.make_async_remote_copy(
                src_ref=g_ref.at[2 * s + (1 - c)], dst_ref=theirs.at[s],
                send_sem=send_sems.at[s], recv_sem=recv_sems.at[s],
                device_id=(x, y, 1 - c), device_id_type=MESH)

        def load(s):
            return pltpu.make_async_copy(g_ref.at[2 * s + c], mine.at[s], local_sems.at[s])

        for s in range(4):
            swap(s).start()
            load(s).start()
        for s in range(4):
            load(s).wait()
            swap(s).wait_recv()

        def chip_sum(chip, rows):
            return mine[chip, rows, :].astype(F32) + theirs[chip, rows, :].astype(F32)

        for j in (1, 2, 3):
            @pl.loop(0, R // rc)
            def _(t):
                rows = pl.ds(pl.multiple_of(t * rc, rc), rc)
                partial[j - 1, rows, :] = chip_sum(my_chip ^ j, rows).astype(BF16)

        @pl.loop(0, R // rc)
        def _(t):
            rows = pl.ds(pl.multiple_of(t * rc, rc), rc)
            out_ref[rows, :] = chip_sum(my_chip, rows)

        for s in range(4):
            swap(s).wait_send()

    vmem = pl.BlockSpec(memory_space=pltpu.VMEM)
    return pl.pallas_call(
        body, name=name,
        in_specs=[pl.BlockSpec(memory_space=pl.ANY)], out_specs=[vmem, vmem],
        out_shape=[jax.ShapeDtypeStruct((3, R, C), BF16), jax.ShapeDtypeStruct((R, C), F32)],
        scratch_shapes=[
            pltpu.VMEM((4, R, C), BF16), pltpu.VMEM((4, R, C), BF16),
            pltpu.SemaphoreType.DMA((4,)), pltpu.SemaphoreType.DMA((4,)), pltpu.SemaphoreType.DMA((4,)),
        ],
        compiler_params=_params(),
    )(grads)


def _cross_chips(partials, *, name, collective_id):
    n = len(partials)

    def body(*refs):
        ins, outs = refs[:n], refs[n:2 * n]
        send_sems, recv_sems = refs[2 * n:]
        x, y, c = _position()
        my_chip = 2 * x + y
        peers = [((my_chip ^ j) // 2, (my_chip ^ j) % 2, c) for j in (1, 2, 3)]

        barrier = pltpu.get_barrier_semaphore()
        for peer in peers:
            pl.semaphore_signal(barrier, inc=1, device_id=peer, device_id_type=MESH)
        pl.semaphore_wait(barrier, 3)

        copies = [
            pltpu.make_async_remote_copy(
                src_ref=ins[a].at[j], dst_ref=outs[a].at[j],
                send_sem=send_sems.at[3 * a + j], recv_sem=recv_sems.at[3 * a + j],
                device_id=peers[j], device_id_type=MESH)
            for a in range(n) for j in range(3)]
        for cp in copies:
            cp.start()
        for cp in copies:
            cp.wait_recv()
        for cp in copies:
            cp.wait_send()

    return pl.kernel(
        body, name=name,
        out_type=[jax.ShapeDtypeStruct(p.shape, p.dtype) for p in partials],
        mesh=plsc.ScalarSubcoreMesh(axis_name="sequencer", num_cores=1),
        scratch_types=[pltpu.SemaphoreType.DMA((3 * n,)), pltpu.SemaphoreType.DMA((3 * n,))],
        compiler_params=pltpu.CompilerParams(collective_id=collective_id),
    )(*partials)


def _cross_chips_and_gather(partial, slab, *, name, collective_id):
    def body(part_ref, slab_ref, landed_ref, slabs_ref, send_sems, recv_sems, local_sem):
        x, y, c = _position()
        me, my_chip = 4 * x + 2 * y + c, 2 * x + y
        others = [me ^ k for k in range(1, N_DEV)]
        ids = [(o // 4, (o // 2) % 2, o % 2) for o in others]

        barrier = pltpu.get_barrier_semaphore()
        for peer in ids:
            pl.semaphore_signal(barrier, inc=1, device_id=peer, device_id_type=MESH)
        pl.semaphore_wait(barrier, N_DEV - 1)

        mine = pltpu.make_async_copy(slab_ref, slabs_ref.at[me], local_sem)
        mine.start()
        sends = [
            pltpu.make_async_remote_copy(
                src_ref=part_ref.at[j], dst_ref=landed_ref.at[j], send_sem=send_sems.at[j], recv_sem=recv_sems.at[j],
                device_id=((my_chip ^ (j + 1)) // 2, (my_chip ^ (j + 1)) % 2, c), device_id_type=MESH)
            for j in range(3)]
        sends += [
            pltpu.make_async_remote_copy(
                src_ref=slab_ref, dst_ref=slabs_ref.at[me], send_sem=send_sems.at[3 + k], recv_sem=recv_sems.at[3 + k],
                device_id=ids[k], device_id_type=MESH)
            for k in range(N_DEV - 1)]
        arrivals = sends[:3] + [
            pltpu.make_async_remote_copy(
                src_ref=slab_ref, dst_ref=slabs_ref.at[others[k]], send_sem=send_sems.at[3 + k],
                recv_sem=recv_sems.at[3 + k], device_id=ids[k], device_id_type=MESH)
            for k in range(N_DEV - 1)]
        for cp in sends:
            cp.start()
        for cp in arrivals:
            cp.wait_recv()
        for cp in sends:
            cp.wait_send()
        mine.wait()

    n_sems = 3 + N_DEV - 1
    return pl.kernel(
        body, name=name,
        out_type=[jax.ShapeDtypeStruct(partial.shape, partial.dtype),
                  jax.ShapeDtypeStruct((N_DEV,) + slab.shape, slab.dtype)],
        mesh=plsc.ScalarSubcoreMesh(axis_name="sequencer", num_cores=1),
        scratch_types=[pltpu.SemaphoreType.DMA((n_sems,)), pltpu.SemaphoreType.DMA((n_sems,)), pltpu.SemaphoreType.DMA],
        compiler_params=pltpu.CompilerParams(collective_id=collective_id),
    )(partial, slab)


def _sum_devices(gathered, after, *, name):
    _, R, C = gathered.shape

    def body(in_ref, after_ref, out_ref):
        total = in_ref[0]
        for d in range(1, N_DEV):
            total = total + in_ref[d]
        out_ref[...] = total

    return pl.pallas_call(
        body, name=name, grid=(1,),
        in_specs=[pl.BlockSpec((N_DEV, R, C), lambda i: (0, 0, 0)), AFTER],
        out_specs=pl.BlockSpec((R, C), lambda i: (0, 0)),
        out_shape=jax.ShapeDtypeStruct((R, C), F32),
        compiler_params=_params(("arbitrary",)),
    )(gathered, _in_hbm(after))


def _owner_sum(own, landed, after, *, name):
    R, C = own.shape
    tr = _row_tile(R, C)

    def body(own_ref, landed_ref, after_ref, out_ref):
        total = own_ref[...]
        for j in range(3):
            total = total + landed_ref[j].astype(F32)
        out_ref[...] = total

    return pl.pallas_call(
        body, name=name, grid=(R // tr,),
        in_specs=[pl.BlockSpec((tr, C), lambda i: (i, 0)), pl.BlockSpec((3, tr, C), lambda i: (0, i, 0)), AFTER],
        out_specs=pl.BlockSpec((tr, C), lambda i: (i, 0)),
        out_shape=jax.ShapeDtypeStruct((R, C), F32),
        compiler_params=_params(("arbitrary",)),
    )(own, landed, _in_hbm(after))


def _local_step(x, target, norms, pool_w_group, pool_scale, wgu1, wd1, w_in, wbp, wba, w_out, wgu2, wd2, exchange):
    n1g, nmg, n2g, nfg = norms
    D = x.shape[1]
    gu1, hid1 = _ffn_up(x, n1g, wgu1, tm=512, name="ffn1_up")
    h1 = _ffn_down(x, hid1, wd1, tm=512, name="ffn1_down")
    un, proj = _inproj_fwd(h1, nmg, w_in, tm=1024, name="inproj_fwd")
    p = _pool_fwd(proj, pool_w_group, pool_scale, name="pool_fwd")
    o, ltot = _attn_fwd(proj, name="attn_fwd")
    h2, m = _mix_fwd(h1, p, o, proj, wbp, wba, w_out, tm=256, name="mix_fwd")
    gu2, hid2 = _ffn_up(h2, n2g, wgu2, tm=512, name="ffn2_up")
    h3 = _ffn_down(h2, hid2, wd2, tm=512, name="ffn2_down")
    dh3, df2, loss, d_nf = _loss_bwd(h3, target, nfg, tm=256, name="loss_bwd")

    d_wd2 = _wgrad_down(hid2, df2, tk=WGRAD_TOKENS, name="ffn2_wgrad_down")
    (g_wd2,), token = exchange("ffn2_down", [d_wd2.reshape(N_DEV, FF_SHARD_PAD, D)])
    dh2, d_n2, n2, dgu2 = _ffn_bwd(dh3, df2, h2, n2g, gu2, wgu2, wd2, token, tm=512, name="ffn2_bwd")
    d_wgu2 = _wgrad_gate_up(n2, dgu2, tk=WGRAD_TOKENS, name="ffn2_wgrad_gate_up")
    (g_wgu2,), token = exchange("ffn2_gate_up", [d_wgu2])

    dyp, dys, dp, do, dgl = _mix_bwd(dh2, p, o, proj, wbp, wba, w_out, token, tm=256, name="mix_bwd")
    d_wout = _wgrad_full(m, dh2, tk=WGRAD_TOKENS, name="wgrad_out")
    d_wbp = _wgrad_full(p, dyp, tk=WGRAD_TOKENS, name="wgrad_branch_pool", split_lanes=wbp.shape[2])
    d_wba = _wgrad_full(o, dys, tk=WGRAD_TOKENS, name="wgrad_branch_attn", split_lanes=wba.shape[2])
    (g_wbp, g_wba, g_wout), token = exchange("mix", [d_wbp, d_wba, d_wout.reshape(N_DEV, D // N_DEV, D)])
    dxp, d_wgroup, d_scale = _pool_bwd(dp, proj, pool_w_group, pool_scale, name="pool_bwd")
    dq, dkt, dvt = _attn_bwd(proj, do, ltot, token, name="attn_bwd")
    dk, dv = (t.transpose(0, 2, 1).reshape(dq.shape) for t in (dkt, dvt))
    dproj = jnp.concatenate([dxp.astype(BF16), dq.astype(BF16), dk.astype(BF16), dv.astype(BF16), dgl], axis=1)
    d_win = _wgrad_in(un, dproj, tk=WGRAD_TOKENS, name="wgrad_in")
    (g_win,), token_in = exchange("w_in", [d_win])
    dh1, df1, d_nm = _inproj_bwd(dproj, dh2, h1, nmg, w_in, tm=1024, name="inproj_bwd")
    d_wd1 = _wgrad_down(hid1, df1, tk=WGRAD_TOKENS, name="ffn1_wgrad_down")
    (g_wd1,), token_down = exchange("ffn1_down", [d_wd1.reshape(N_DEV, FF_SHARD_PAD, D)])
    token = (token_down[(0,) * token_down.ndim] + token_in[(0,) * token_in.ndim]).reshape(1, 1)

    dx, d_n1, n1, dgu1 = _ffn_bwd(dh1, df1, x, n1g, gu1, wgu1, wd1, token, tm=512, name="ffn1_bwd")
    d_wgu1 = _wgrad_gate_up(n1, dgu1, tk=WGRAD_TOKENS, name="ffn1_wgrad_gate_up")
    (g_wgu1, replicated), token = exchange("last", [d_wgu1, d_n1, d_nm, d_n2, d_nf, d_scale, d_wgroup, loss])

    sharded = (g_wgu1, g_wd1, g_win, g_wbp, g_wba, g_wout, g_wgu2, g_wd2)
    return dx, sharded, replicated, token


def _hidden_major(w):
    return jnp.swapaxes(w[0], 0, 1)


def _pad_gate_up(wt):
    d = wt.shape[1]
    wt = wt.astype(BF16).reshape(2, FF_SHARD, d)
    return jnp.pad(wt, ((0, 0), (0, FF_SHARD_PAD - FF_SHARD), (0, 0))).reshape(2 * FF_SHARD_PAD, d)


def _unpad_gate_up(gt):
    d = gt.shape[1]
    return gt.reshape(2, FF_SHARD_PAD, d)[:, :FF_SHARD].reshape(2 * FF_SHARD, d)


def _pad_down(w):
    return jnp.pad(w.astype(BF16), ((0, FF_SHARD_PAD - FF_SHARD), (0, 0)))


def kernel(x, ffn1_norm, ffn1_w_gate_up, ffn1_w_down, mix_norm, w_in, pool_w_group, pool_scale, w_branch_pool, w_branch_attn, w_out, ffn2_norm, ffn2_w_gate_up, ffn2_w_down, final_norm, loss_target, m_ffn1_norm, m_ffn1_w_gate_up, m_ffn1_w_down, m_mix_norm, m_w_in, m_pool_w_group, m_pool_scale, m_w_branch_pool, m_w_branch_attn, m_w_out, m_ffn2_norm, m_ffn2_w_gate_up, m_ffn2_w_down, m_final_norm, v_ffn1_norm, v_ffn1_w_gate_up, v_ffn1_w_down, v_mix_norm, v_w_in, v_pool_w_group, v_pool_scale, v_w_branch_pool, v_w_branch_attn, v_w_out, v_ffn2_norm, v_ffn2_w_gate_up, v_ffn2_w_down, v_final_norm):
    D = x.shape[-1]
    weights = dict(ffn1_norm=ffn1_norm, ffn1_w_gate_up=ffn1_w_gate_up, ffn1_w_down=ffn1_w_down, mix_norm=mix_norm,
                   w_in=w_in, pool_w_group=pool_w_group, pool_scale=pool_scale, w_branch_pool=w_branch_pool,
                   w_branch_attn=w_branch_attn, w_out=w_out, ffn2_norm=ffn2_norm, ffn2_w_gate_up=ffn2_w_gate_up,
                   ffn2_w_down=ffn2_w_down, final_norm=final_norm)
    first = dict(ffn1_norm=m_ffn1_norm, ffn1_w_gate_up=m_ffn1_w_gate_up, ffn1_w_down=m_ffn1_w_down,
                 mix_norm=m_mix_norm, w_in=m_w_in, pool_w_group=m_pool_w_group, pool_scale=m_pool_scale,
                 w_branch_pool=m_w_branch_pool, w_branch_attn=m_w_branch_attn, w_out=m_w_out,
                 ffn2_norm=m_ffn2_norm, ffn2_w_gate_up=m_ffn2_w_gate_up, ffn2_w_down=m_ffn2_w_down,
                 final_norm=m_final_norm)
    second = dict(ffn1_norm=v_ffn1_norm, ffn1_w_gate_up=v_ffn1_w_gate_up, ffn1_w_down=v_ffn1_w_down,
                  mix_norm=v_mix_norm, w_in=v_w_in, pool_w_group=v_pool_w_group, pool_scale=v_pool_scale,
                  w_branch_pool=v_w_branch_pool, w_branch_attn=v_w_branch_attn, w_out=v_w_out,
                  ffn2_norm=v_ffn2_norm, ffn2_w_gate_up=v_ffn2_w_gate_up, ffn2_w_down=v_ffn2_w_down,
                  final_norm=v_final_norm)
    order = list(weights)

    wgu1, = _all_gather([_pad_gate_up(_hidden_major(ffn1_w_gate_up))], name="all_gather_ffn1_gate_up", collective_id=0)
    wd1, = _all_gather([_pad_down(ffn1_w_down[0])], name="all_gather_ffn1_down", collective_id=10)
    win_g, = _all_gather([w_in[0].astype(BF16)], name="all_gather_w_in", collective_id=1)
    wbp_g, wba_g = _all_gather([w_branch_pool[0].astype(BF16), w_branch_attn[0].astype(BF16)],
                               name="all_gather_branches", collective_id=2)
    wout_g, = _all_gather([w_out[0].astype(BF16)], name="all_gather_w_out", collective_id=11)
    wgu2, wd2 = _all_gather([_pad_gate_up(_hidden_major(ffn2_w_gate_up)), _pad_down(ffn2_w_down[0])],
                            name="all_gather_ffn2", collective_id=3)
    wd1 = wd1.reshape(N_DEV * FF_SHARD_PAD, D)
    wd2 = wd2.reshape(N_DEV * FF_SHARD_PAD, D)
    wout_g = wout_g.reshape(D, D)

    cross_ids = {"ffn2_down": 4, "ffn2_gate_up": 5, "mix": 6, "ffn1_down": 7, "w_in": 8, "last": 9}
    small = ["ffn1_norm", "mix_norm", "ffn2_norm", "final_norm", "pool_scale", "pool_w_group"]

    def tile_rows(a):
        a = a.reshape(-1, 128)
        return jnp.pad(a, ((0, -a.shape[0] % 8), (0, 0)))

    def exchange(tag, group):
        if tag == "last":
            slab = jnp.concatenate([tile_rows(g) for g in group[1:-1]] + [jnp.broadcast_to(group[-1], (8, 128))], axis=0)
            partial, own = _chip_sums(group[0], name="chip_sums_last")
            landed, slabs = _cross_chips_and_gather(partial, slab, name="cross_chips_last", collective_id=cross_ids[tag])
            return [(own, landed), slabs], own
        sums = [_chip_sums(g, name=f"chip_sums_{tag}_{i}") for i, g in enumerate(group)]
        landed = _cross_chips([s[0] for s in sums], name="cross_chips_" + tag, collective_id=cross_ids[tag])
        token = sums[0][1] if len(sums) == 1 else sum(s[1][0, 0] for s in sums).reshape(1, 1)
        return [(s[1], l) for s, l in zip(sums, landed)], token

    norms = (ffn1_norm, mix_norm, ffn2_norm, final_norm.reshape(1, D))
    dx, sharded, slabs, last = _local_step(
        x[0], loss_target[0], norms, pool_w_group[0], pool_scale, wgu1, wd1, win_g, wbp_g, wba_g, wout_g, wgu2, wd2,
        exchange)
    names = ["ffn1_w_gate_up", "ffn1_w_down", "w_in", "w_branch_pool", "w_branch_attn", "w_out",
             "ffn2_w_gate_up", "ffn2_w_down"]
    handles = dict(zip(names, sharded))
    grads, delta, new_m, new_v = {}, {}, {}, {}
    after = last
    for k in ("ffn2_w_down", "ffn2_w_gate_up", "w_branch_pool", "w_branch_attn", "w_out", "w_in", "ffn1_w_down",
              "ffn1_w_gate_up"):
        g = _owner_sum(*handles[k], after, name="owner_sum_" + k)
        hidden_major = k.endswith("w_gate_up")
        g = _unpad_gate_up(g) if hidden_major else g[:weights[k].shape[1]]
        view = _hidden_major if hidden_major else (lambda a: a[0])
        back = (lambda a: jnp.swapaxes(a, 0, 1)[None]) if hidden_major else (lambda a: a[None])
        out = _adamw(view(weights[k]), g, view(first[k]), view(second[k]), name="adamw_" + k)
        after = out[0]
        grads[k] = back(g)
        delta[k], new_m[k], new_v[k] = (back(a) for a in out)

    rows = [weights[k].size // 128 for k in small]
    padded_rows = [-(-r // 8) * 8 for r in rows]
    starts = [sum(padded_rows[:i]) for i in range(len(rows) + 1)]
    total = _sum_devices(slabs, after, name="sum_replicated")
    loss_out = total[starts[-1], 0]
    small_w = jnp.concatenate([tile_rows(weights[k]) for k in small], axis=0)
    small_m = jnp.concatenate([tile_rows(first[k]) for k in small], axis=0)
    small_v = jnp.concatenate([tile_rows(second[k]) for k in small], axis=0)
    small_out = _adamw(small_w, total[:starts[-1]], small_m, small_v, name="adamw_replicated")
    for name_, start, n_rows in zip(small, starts, rows):
        shape = weights[name_].shape
        grads[name_] = total[start:start + n_rows].reshape(shape)
        delta[name_], new_m[name_], new_v[name_] = (a[start:start + n_rows].reshape(shape) for a in small_out)

    return (loss_out, dx[None], *[grads[k] for k in order], *[delta[k] for k in order],
            *[new_m[k] for k in order], *[new_v[k] for k in order])
```

```python
import functools

import jax
import jax.numpy as jnp
from jax import lax
from jax.experimental import pallas as pl
from jax.experimental.pallas import tpu as pltpu
from jax.experimental.pallas import tpu_sc as plsc

F32 = jnp.float32
BF16 = jnp.bfloat16
MESH = pl.DeviceIdType.MESH

RMS_EPS = 1e-6
N_DEV = 8
N_HEADS = 8
HEAD_DIM = 64
HEAD_PAIR = 2 * HEAD_DIM
POOL_WINDOWS = (2, 4, 8, 16)
POOL_GROUP = 128
POOL_WIDTH = 512
SB_WIDTH = 512
FF_SHARD = 352
FF_SHARD_PAD = 384
ATTN_K_BLOCK = 256
ATTN_Q_BLOCK_FWD = 512
ATTN_Q_BLOCK_BWD = 256
ATTN_SCALE = 0.125

ADAM_LR = 0.001
ADAM_B1 = 0.9
ADAM_B2 = 0.999
ADAM_EPS = 1e-08
ADAM_WD = 0.01
ADAM_STEP = 10

VMEM_LIMIT = 48 << 20
WGRAD_TOKENS = 2048


def _params(dims=None):
    return pltpu.CompilerParams(dimension_semantics=dims, vmem_limit_bytes=VMEM_LIMIT)


def _mm(a, b):
    return jnp.dot(a, b, preferred_element_type=F32)


def _mm_nt(a, b):
    return lax.dot_general(a, b, (((1,), (1,)), ((), ())), preferred_element_type=F32)


def _mm_tn(a, b):
    return lax.dot_general(a, b, (((0,), (0,)), ((), ())), preferred_element_type=F32)


def _row_tile(rows, cols):
    limit = max(8, (512 * 1024) // cols)
    return max(t for t in range(8, rows + 1, 8) if rows % t == 0 and (t <= limit or t == 8))


def _rstd(xf):
    return lax.rsqrt(jnp.mean(xf * xf, axis=-1, keepdims=True) + RMS_EPS)


def _rms_bwd(xf, gain, dn):
    r = _rstd(xf)
    xh = xf * r
    dgain = jnp.sum(dn * xh, axis=0, keepdims=True)
    dxh = dn * gain
    dx = r * (dxh - xh * jnp.mean(dxh * xh, axis=-1, keepdims=True))
    return dx, dgain


def _ffn_up(x, gain, wgu, *, tm, name):
    T, D = x.shape
    tm = min(tm, T)
    nb, bw = wgu.shape[0] // 2, wgu.shape[1]

    def body(x_ref, gain_ref, wg_ref, wu_ref, gu_ref, hid_ref, n_scr):
        @pl.when(pl.program_id(1) == 0)
        def _():
            xf = x_ref[...]
            n_scr[...] = (xf * _rstd(xf) * gain_ref[...]).astype(BF16)

        halves = (pl.ds(0, tm // 2), pl.ds(tm // 2, tm // 2))
        wg, wu = wg_ref[...], wu_ref[...]
        gus = [(_mm_nt(n_scr[rows, :], wg), _mm_nt(n_scr[rows, :], wu)) for rows in halves]
        for rows, (g, u) in zip(halves, gus):
            gu_ref[0, rows, :] = g.astype(BF16)
            gu_ref[1, rows, :] = u.astype(BF16)
            hid_ref[rows, :] = (g * jax.nn.sigmoid(g) * u).astype(BF16)

    return pl.pallas_call(
        body, name=name, grid=(T // tm, nb),
        in_specs=[
            pl.BlockSpec((tm, D), lambda i, j: (i, 0)),
            pl.BlockSpec((1, D), lambda i, j: (0, 0)),
            pl.BlockSpec((None, bw, D), lambda i, j: (j, 0, 0)),
            pl.BlockSpec((None, bw, D), lambda i, j: (j + nb, 0, 0)),
        ],
        out_specs=[
            pl.BlockSpec((2, tm, bw), lambda i, j: (0, i, j)),
            pl.BlockSpec((tm, bw), lambda i, j: (i, j)),
        ],
        out_shape=[jax.ShapeDtypeStruct((2, T, nb * bw), BF16), jax.ShapeDtypeStruct((T, nb * bw), BF16)],
        scratch_shapes=[pltpu.VMEM((tm, D), BF16)],
        compiler_params=_params(("arbitrary", "arbitrary")),
    )(x, gain, wgu, wgu)


def _ffn_down(x, hid, wd, *, tm, name):
    T, D = x.shape
    tm = min(tm, T)
    F = hid.shape[1]

    def body(x_ref, hid_ref, wd_ref, h_ref):
        h_ref[...] = x_ref[...] + 0.5 * _mm(hid_ref[...], wd_ref[...])

    return pl.pallas_call(
        body, name=name, grid=(T // tm,),
        in_specs=[
            pl.BlockSpec((tm, D), lambda i: (i, 0)),
            pl.BlockSpec((tm, F), lambda i: (i, 0)),
            pl.BlockSpec((F, D), lambda i: (0, 0)),
        ],
        out_specs=pl.BlockSpec((tm, D), lambda i: (i, 0)),
        out_shape=jax.ShapeDtypeStruct((T, D), F32),
        compiler_params=_params(("arbitrary",)),
    )(x, hid, wd)


AFTER = pl.BlockSpec(memory_space=pltpu.HBM)


def _in_hbm(token):
    return pltpu.with_memory_space_constraint(token, pltpu.HBM)


def _ffn_bwd(dh, df, x, gain, gu, wgu, wd, after, *, tm, name):
    T, D = x.shape
    tm = min(tm, T)
    nb, bw = wgu.shape[0] // 2, wgu.shape[1]

    def body(dh_ref, df_ref, x_ref, gain_ref, gu_ref, wg_ref, wu_ref, wd_ref, after_ref,
             dx_ref, dgain_ref, n_ref, dgu_ref, dn_acc):
        i, j = pl.program_id(0), pl.program_id(1)

        @pl.when(j == 0)
        def _():
            xf = x_ref[...]
            n_ref[...] = (xf * _rstd(xf) * gain_ref[...]).astype(BF16)
            dn_acc[...] = jnp.zeros_like(dn_acc)

        @pl.when((i == 0) & (j == 0))
        def _():
            dgain_ref[...] = jnp.zeros_like(dgain_ref)

        halves = (pl.ds(0, tm // 2), pl.ds(tm // 2, tm // 2))
        wd, wg, wu = wd_ref[...], wg_ref[...], wu_ref[...]
        dhids = [_mm_nt(df_ref[rows, :], wd) for rows in halves]
        for rows, dhid in zip(halves, dhids):
            g = gu_ref[0, rows, :].astype(F32)
            u = gu_ref[1, rows, :].astype(F32)
            s = jax.nn.sigmoid(g)
            silu = g * s
            dg = (dhid * u * (s * (1.0 + g * (1.0 - s)))).astype(BF16)
            du = (dhid * silu).astype(BF16)
            dgu_ref[0, rows, :] = dg
            dgu_ref[1, rows, :] = du
            dn_acc[rows, :] += _mm(dg, wg) + _mm(du, wu)

        @pl.when(j == nb - 1)
        def _():
            dx, dgain = _rms_bwd(x_ref[...], gain_ref[...], dn_acc[...])
            dx_ref[...] = dh_ref[...] + dx
            dgain_ref[...] += dgain

    row = lambda i, j: (i, 0)
    return pl.pallas_call(
        body, name=name, grid=(T // tm, nb),
        in_specs=[
            pl.BlockSpec((tm, D), row),
            pl.BlockSpec((tm, D), row),
            pl.BlockSpec((tm, D), row),
            pl.BlockSpec((1, D), lambda i, j: (0, 0)),
            pl.BlockSpec((2, tm, bw), lambda i, j: (0, i, j)),
            pl.BlockSpec((None, bw, D), lambda i, j: (j, 0, 0)),
            pl.BlockSpec((None, bw, D), lambda i, j: (j + nb, 0, 0)),
            pl.BlockSpec((bw, D), lambda i, j: (j, 0)),
            AFTER,
        ],
        out_specs=[
            pl.BlockSpec((tm, D), row),
            pl.BlockSpec((1, D), lambda i, j: (0, 0)),
            pl.BlockSpec((tm, D), row),
            pl.BlockSpec((2, tm, bw), lambda i, j: (0, i, j)),
        ],
        out_shape=[
            jax.ShapeDtypeStruct((T, D), F32),
            jax.ShapeDtypeStruct((1, D), F32),
            jax.ShapeDtypeStruct((T, D), BF16),
            jax.ShapeDtypeStruct((2, T, nb * bw), BF16),
        ],
        scratch_shapes=[pltpu.VMEM((tm, D), F32)],
        compiler_params=_params(("arbitrary", "arbitrary")),
    )(dh, df, x, gain, gu, wgu, wgu, wd, _in_hbm(after))


def _wgrad(a, b, *, grid, a_spec, b_spec, out_spec, out_shape, acc_shape, name, split_lanes=0):
    nk = grid[2]

    def body(a_ref, b_ref, o_ref, acc):
        k = pl.program_id(2)

        @pl.when(k == 0)
        def _():
            acc[...] = jnp.zeros_like(acc)

        acc[...] += _mm_tn(a_ref[...].astype(BF16), b_ref[...].astype(BF16))

        @pl.when(k == nk - 1)
        def _():
            if split_lanes:
                for e in range(o_ref.shape[0]):
                    o_ref[e] = acc[:, e * split_lanes:(e + 1) * split_lanes].astype(o_ref.dtype)
            else:
                o_ref[...] = acc[...].astype(o_ref.dtype)

    return pl.pallas_call(
        body, name=name, grid=grid, in_specs=[a_spec, b_spec], out_specs=out_spec,
        out_shape=jax.ShapeDtypeStruct(out_shape, BF16),
        scratch_shapes=[pltpu.VMEM(acc_shape, F32)],
        compiler_params=_params(("arbitrary", "arbitrary", "arbitrary")),
    )(a, b)


def _wgrad_gate_up(n, dgu, *, tk, name):
    T, D = n.shape
    tk = min(tk, T)
    bw = FF_SHARD_PAD * 2
    nb = dgu.shape[2] // bw
    return _wgrad(
        dgu, n, grid=(2 * nb, 1, T // tk), name=name,
        a_spec=pl.BlockSpec((None, tk, bw), lambda m, c, k: (m // nb, k, m % nb)),
        b_spec=pl.BlockSpec((tk, D), lambda m, c, k: (k, 0)),
        out_spec=pl.BlockSpec((None, bw, D), lambda m, c, k: (m, 0, 0)),
        out_shape=(2 * nb, bw, D), acc_shape=(bw, D))


def _wgrad_down(hid, df, *, tk, name):
    T, D = df.shape
    tk = min(tk, T)
    bw = FF_SHARD_PAD * 2
    nb = hid.shape[1] // bw
    return _wgrad(
        hid, df, grid=(nb, 1, T // tk), name=name,
        a_spec=pl.BlockSpec((tk, bw), lambda m, c, k: (k, m)),
        b_spec=pl.BlockSpec((tk, D), lambda m, c, k: (k, 0)),
        out_spec=pl.BlockSpec((bw, D), lambda m, c, k: (m, 0)),
        out_shape=(nb * bw, D), acc_shape=(bw, D))


def _wgrad_in(un, dproj, *, tk, name):
    T, D = un.shape
    tk = min(tk, T)
    bw = dproj.shape[1] // N_DEV
    return _wgrad(
        un, dproj, grid=(1, N_DEV, T // tk), name=name,
        a_spec=pl.BlockSpec((tk, D), lambda m, c, k: (k, 0)),
        b_spec=pl.BlockSpec((tk, bw), lambda m, c, k: (k, c)),
        out_spec=pl.BlockSpec((None, D, bw), lambda m, c, k: (c, 0, 0)),
        out_shape=(N_DEV, D, bw), acc_shape=(D, bw))


def _wgrad_full(a, b, *, tk, name, split_lanes=0):
    T, M = a.shape
    tk = min(tk, T)
    N = b.shape[1]
    if split_lanes:
        out_shape = (N // split_lanes, M, split_lanes)
        out_spec = pl.BlockSpec(out_shape, lambda m, c, k: (0, 0, 0))
    else:
        out_shape = (M, N)
        out_spec = pl.BlockSpec(out_shape, lambda m, c, k: (0, 0))
    return _wgrad(
        a, b, grid=(1, 1, T // tk), name=name,
        a_spec=pl.BlockSpec((tk, M), lambda m, c, k: (k, 0)),
        b_spec=pl.BlockSpec((tk, N), lambda m, c, k: (k, 0)),
        out_spec=out_spec, out_shape=out_shape, acc_shape=(M, N), split_lanes=split_lanes)


def _loss_bwd(h, target, gain, *, tm, name):
    T, D = h.shape
    tm = min(tm, T)

    def body(h_ref, t_ref, gain_ref, dh_ref, df_ref, loss_ref, dgain_ref):
        @pl.when(pl.program_id(0) == 0)
        def _():
            loss_ref[...] = jnp.zeros_like(loss_ref)
            dgain_ref[...] = jnp.zeros_like(dgain_ref)

        xf = h_ref[...]
        gain = gain_ref[...]
        err = xf * _rstd(xf) * gain - t_ref[...]
        loss_ref[...] += 0.5 * jnp.sum(jnp.mean(err * err, axis=-1, keepdims=True), axis=0, keepdims=True)
        dx, dgain = _rms_bwd(xf, gain, err * (1.0 / D))
        dh_ref[...] = dx
        df_ref[...] = (0.5 * dx).astype(BF16)
        dgain_ref[...] += dgain

    row = lambda i: (i, 0)
    fixed = lambda i: (0, 0)
    return pl.pallas_call(
        body, name=name, grid=(T // tm,),
        in_specs=[pl.BlockSpec((tm, D), row), pl.BlockSpec((tm, D), row), pl.BlockSpec((1, D), fixed)],
        out_specs=[pl.BlockSpec((tm, D), row), pl.BlockSpec((tm, D), row), pl.BlockSpec((1, 128), fixed),
                   pl.BlockSpec((1, D), fixed)],
        out_shape=[jax.ShapeDtypeStruct((T, D), F32), jax.ShapeDtypeStruct((T, D), BF16),
                   jax.ShapeDtypeStruct((1, 128), F32), jax.ShapeDtypeStruct((1, D), F32)],
        compiler_params=_params(("arbitrary",)),
    )(h, target, gain)


def _inproj_fwd(h, gain, w_in, *, tm, name):
    T, D = h.shape
    tm = min(tm, T)
    nb, bw = w_in.shape[0], w_in.shape[2]

    def body(h_ref, gain_ref, w_ref, un_ref, proj_ref):
        @pl.when(pl.program_id(1) == 0)
        def _():
            xf = h_ref[...]
            un_ref[...] = (xf * _rstd(xf) * gain_ref[...]).astype(BF16)

        proj_ref[...] = _mm(un_ref[...], w_ref[...])

    return pl.pallas_call(
        body, name=name, grid=(T // tm, nb),
        in_specs=[
            pl.BlockSpec((tm, D), lambda i, j: (i, 0)),
            pl.BlockSpec((1, D), lambda i, j: (0, 0)),
            pl.BlockSpec((None, D, bw), lambda i, j: (j, 0, 0)),
        ],
        out_specs=[pl.BlockSpec((tm, D), lambda i, j: (i, 0)), pl.BlockSpec((tm, bw), lambda i, j: (i, j))],
        out_shape=[jax.ShapeDtypeStruct((T, D), BF16), jax.ShapeDtypeStruct((T, nb * bw), F32)],
        compiler_params=_params(("arbitrary", "arbitrary")),
    )(h, gain, w_in)


def _inproj_bwd(dproj, dh, h, gain, w_in, *, tm, name):
    T, D = h.shape
    tm = min(tm, T)
    nb, bw = w_in.shape[0], w_in.shape[2]

    def body(dp_ref, dh_ref, h_ref, gain_ref, w_ref, dx_ref, df_ref, dgain_ref, acc):
        i, j = pl.program_id(0), pl.program_id(1)

        @pl.when(j == 0)
        def _():
            acc[...] = jnp.zeros_like(acc)

        @pl.when((i == 0) & (j == 0))
        def _():
            dgain_ref[...] = jnp.zeros_like(dgain_ref)

        acc[...] += _mm_nt(dp_ref[...], w_ref[...])

        @pl.when(j == nb - 1)
        def _():
            dx, dgain = _rms_bwd(h_ref[...], gain_ref[...], acc[...])
            dh_in = dh_ref[...] + dx
            dx_ref[...] = dh_in
            df_ref[...] = (0.5 * dh_in).astype(BF16)
            dgain_ref[...] += dgain

    row = lambda i, j: (i, 0)
    return pl.pallas_call(
        body, name=name, grid=(T // tm, nb),
        in_specs=[
            pl.BlockSpec((tm, bw), lambda i, j: (i, j)),
            pl.BlockSpec((tm, D), row),
            pl.BlockSpec((tm, D), row),
            pl.BlockSpec((1, D), lambda i, j: (0, 0)),
            pl.BlockSpec((None, D, bw), lambda i, j: (j, 0, 0)),
        ],
        out_specs=[pl.BlockSpec((tm, D), row), pl.BlockSpec((tm, D), row), pl.BlockSpec((1, D), lambda i, j: (0, 0))],
        out_shape=[jax.ShapeDtypeStruct((T, D), F32), jax.ShapeDtypeStruct((T, D), BF16),
                   jax.ShapeDtypeStruct((1, D), F32)],
        scratch_shapes=[pltpu.VMEM((tm, D), F32)],
        compiler_params=_params(("arbitrary", "arbitrary")),
    )(dproj, dh, h, gain, w_in)


def _window_sum(x, row, doublings, *, backward):
    T = x.shape[0]
    s = x
    for k in range(doublings):
        sh = 1 << k
        if backward:
            s = s + jnp.where(row < T - sh, pltpu.roll(s, T - sh, 0), 0.0)
        else:
            s = s + jnp.where(row >= sh, pltpu.roll(s, sh, 0), 0.0)
    return s


def _pool_fwd(proj, w_group, scale, *, name):
    T = proj.shape[0]

    def body(xp_ref, w_ref, scale_ref, p_ref):
        row = lax.broadcasted_iota(jnp.int32, (T, POOL_GROUP), 0)
        for gi, window in enumerate(POOL_WINDOWS):
            cols = slice(gi * POOL_GROUP, (gi + 1) * POOL_GROUP)
            x = xp_ref[:, cols]
            inv_count = 1.0 / jnp.minimum(row + 1, window).astype(F32)
            yc = _window_sum(x, row, gi + 1, backward=False) * inv_count - x
            pre = _mm(yc.astype(BF16), w_ref[gi].astype(BF16))
            p_ref[:, cols] = pre * scale_ref[:, cols]

    return pl.pallas_call(
        body, name=name, grid=(1,),
        in_specs=[
            pl.BlockSpec((T, POOL_WIDTH), lambda i: (0, 0)),
            pl.BlockSpec(w_group.shape, lambda i: (0, 0, 0)),
            pl.BlockSpec((1, POOL_WIDTH), lambda i: (0, 0)),
        ],
        out_specs=pl.BlockSpec((T, POOL_WIDTH), lambda i: (0, 0)),
        out_shape=jax.ShapeDtypeStruct((T, POOL_WIDTH), F32),
        compiler_params=_params(("arbitrary",)),
    )(proj, w_group, scale)


def _pool_bwd(dp, proj, w_group, scale, *, name):
    T = proj.shape[0]

    def body(dp_ref, xp_ref, w_ref, scale_ref, dxp_ref, dw_ref, dscale_ref):
        row = lax.broadcasted_iota(jnp.int32, (T, POOL_GROUP), 0)
        for gi, window in enumerate(POOL_WINDOWS):
            cols = slice(gi * POOL_GROUP, (gi + 1) * POOL_GROUP)
            x = xp_ref[:, cols]
            inv_count = 1.0 / jnp.minimum(row + 1, window).astype(F32)
            yc = (_window_sum(x, row, gi + 1, backward=False) * inv_count - x).astype(BF16)
            w = w_ref[gi].astype(BF16)
            pre = _mm(yc, w)
            dpg = dp_ref[:, cols]
            dscale_ref[:, cols] = jnp.sum(dpg * pre, axis=0, keepdims=True)
            dpre = (dpg * scale_ref[:, cols]).astype(BF16)
            dw_ref[gi] = _mm_tn(yc, dpre)
            dyc = _mm_nt(dpre, w)
            dxp_ref[:, cols] = _window_sum(dyc * inv_count, row, gi + 1, backward=True) - dyc

    return pl.pallas_call(
        body, name=name, grid=(1,),
        in_specs=[
            pl.BlockSpec((T, POOL_WIDTH), lambda i: (0, 0)),
            pl.BlockSpec((T, POOL_WIDTH), lambda i: (0, 0)),
            pl.BlockSpec(w_group.shape, lambda i: (0, 0, 0)),
            pl.BlockSpec((1, POOL_WIDTH), lambda i: (0, 0)),
        ],
        out_specs=[
            pl.BlockSpec((T, POOL_WIDTH), lambda i: (0, 0)),
            pl.BlockSpec(w_group.shape, lambda i: (0, 0, 0)),
            pl.BlockSpec((1, POOL_WIDTH), lambda i: (0, 0)),
        ],
        out_shape=[jax.ShapeDtypeStruct((T, POOL_WIDTH), F32), jax.ShapeDtypeStruct(w_group.shape, F32),
                   jax.ShapeDtypeStruct((1, POOL_WIDTH), F32)],
        compiler_params=_params(("arbitrary",)),
    )(dp, proj, w_group, scale)


ATTN_STRIP = 32


def _log_sigmoids(z):
    lb = jnp.minimum(z, 0.0) - jnp.log(1.0 + jnp.exp(-jnp.abs(z)))
    return lb, lb - z


def _transposed_blocks(x_ref, blocks_scr, tq):
    for b in range(blocks_scr.shape[0]):
        blocks_scr[b] = x_ref[b * tq:(b + 1) * tq, :].T.astype(BF16)


def _split_bf16(x):
    hi = x.astype(BF16)
    return hi, (x - hi.astype(F32)).astype(BF16)


def _strips(n):
    return [slice(i, i + ATTN_STRIP) for i in range(0, n, ATTN_STRIP)]


def _rows(parts):
    return jnp.concatenate(parts, axis=0)


def _attn_specs(T, tq):
    q_col = POOL_WIDTH // HEAD_PAIR
    k_col = q_col + SB_WIDTH // HEAD_PAIR
    v_col = k_col + SB_WIDTH // HEAD_PAIR
    return [
        pl.BlockSpec((tq, HEAD_PAIR), lambda p, i: (i, q_col + p)),
        pl.BlockSpec((T, HEAD_PAIR), lambda p, i: (0, k_col + p)),
        pl.BlockSpec((T, HEAD_PAIR), lambda p, i: (0, v_col + p)),
    ]


def _attn_fwd(proj, *, name):
    T = proj.shape[0]
    tk = min(ATTN_K_BLOCK, T)
    tq = min(ATTN_Q_BLOCK_FWD, T)
    diagonal_blocks = tq // tk

    def body(q_ref, k_ref, v_ref, o_ref, lt_ref, kt_scr, vb_scr):
        qi = pl.program_id(1)

        @pl.when(qi == 0)
        def _():
            _transposed_blocks(k_ref, kt_scr, tk)
            vb_scr[...] = v_ref[...].astype(BF16)

        head0 = lax.broadcasted_iota(jnp.int32, (tq, HEAD_PAIR), 1) < HEAD_DIM
        q = q_ref[...] * ATTN_SCALE
        qs = (jnp.where(head0, q, 0.0).astype(BF16), jnp.where(head0, 0.0, q).astype(BF16))
        r = lax.broadcasted_iota(jnp.int32, (tq, tk), 0)
        c = lax.broadcasted_iota(jnp.int32, (tq, tk), 1)
        later = (r[:tk] > c[:tk]).astype(BF16)
        later2 = _rows([later, later])
        causal = lambda d: (lambda rows: c[rows] + d * tk < r[rows])
        strips = _strips(tq)

        def log_terms(z, valid):
            lbs, his, los, sums = [], [], [], []
            for rows in strips:
                lb, lm = _log_sigmoids(z[rows])
                if valid is not None:
                    lm = jnp.where(valid(rows), lm, 0.0)
                hi, lo = _split_bf16(lm)
                lbs.append(lb)
                his.append(hi)
                los.append(lo)
                sums.append(jnp.sum(lm, axis=1, keepdims=True))
            return lbs, jnp.concatenate([_rows(his), _rows(los)], axis=1), _rows(sums)

        def weights(lbs, run, after, valid):
            parts = []
            for rows, lb in zip(strips, lbs):
                a = jnp.exp(lb + run[rows] + after[rows])
                if valid is not None:
                    a = jnp.where(valid(rows), a, 0.0)
                parts.append(a.astype(BF16))
            return _rows(parts)

        def block(kj, carry, valid):
            kt = kt_scr[kj]
            vb = vb_scr[pl.ds(pl.multiple_of(kj * tk, tk), tk), :]
            run0, o0, run1, o1 = carry
            z0 = _mm(qs[0], kt)
            z1 = _mm(qs[1], kt)
            lbs0, split0, sums0 = log_terms(z0, valid)
            after0 = _mm(split0, later2)
            lbs1, split1, sums1 = log_terms(z1, valid)
            after1 = _mm(split1, later2)
            o0 = o0 + _mm(weights(lbs0, run0, after0, valid), vb)
            o1 = o1 + _mm(weights(lbs1, run1, after1, valid), vb)
            return run0 + sums0, o0, run1 + sums1, o1

        zero = (jnp.zeros((tq, 1), F32), jnp.zeros((tq, HEAD_PAIR), F32))
        first = diagonal_blocks * qi
        carry = zero + zero
        for d in reversed(range(diagonal_blocks)):
            carry = block(first + d, carry, causal(d))
        carry = lax.fori_loop(0, first, lambda it, cr: block(first - 1 - it, cr, None), carry)
        o_ref[...] = jnp.where(head0, carry[1], carry[3])
        lt_ref[...] = jnp.where(head0, carry[0], carry[2])

    out_spec = pl.BlockSpec((tq, HEAD_PAIR), lambda p, i: (i, p))
    return pl.pallas_call(
        body, name=name, grid=(N_HEADS // 2, T // tq),
        in_specs=_attn_specs(T, tq), out_specs=[out_spec, out_spec],
        out_shape=[jax.ShapeDtypeStruct((T, SB_WIDTH), F32), jax.ShapeDtypeStruct((T, SB_WIDTH), F32)],
        scratch_shapes=[pltpu.VMEM((T // tk, HEAD_PAIR, tk), BF16), pltpu.VMEM((T, HEAD_PAIR), BF16)],
        compiler_params=_params(("arbitrary", "arbitrary")),
    )(proj, proj, proj)


def _attn_bwd(proj, do, ltot, after, *, name):
    T = proj.shape[0]
    tk = min(ATTN_K_BLOCK, T)
    tq = min(ATTN_Q_BLOCK_BWD, T)
    diagonal_blocks = tq // tk

    def body(q_ref, k_ref, v_ref, do_ref, lt_ref, after_ref, dq_ref, dkt_ref, dvt_ref, kb_scr, kt_scr, vt_scr):
        qi = pl.program_id(1)

        @pl.when(qi == 0)
        def _():
            kb_scr[...] = k_ref[...].astype(BF16)
            _transposed_blocks(k_ref, kt_scr, tk)
            _transposed_blocks(v_ref, vt_scr, tk)
            dkt_ref[...] = jnp.zeros_like(dkt_ref)
            dvt_ref[...] = jnp.zeros_like(dvt_ref)

        head0 = lax.broadcasted_iota(jnp.int32, (tq, HEAD_PAIR), 1) < HEAD_DIM
        q, do_, lt = q_ref[...] * ATTN_SCALE, do_ref[...], lt_ref[...]
        qs = (jnp.where(head0, q, 0.0).astype(BF16), jnp.where(head0, 0.0, q).astype(BF16))
        q_heads = (jnp.where(head0, q, 0.0), jnp.where(head0, 0.0, q))
        do_heads = (jnp.where(head0, do_, 0.0), jnp.where(head0, 0.0, do_))
        dos = tuple(d.astype(BF16) for d in do_heads)
        qts = tuple(x.T.astype(BF16) for x in q_heads)
        dots = tuple(d.T.astype(BF16) for d in do_heads)
        lts = (jnp.max(jnp.where(head0, lt, -jnp.inf), axis=1, keepdims=True),
               jnp.max(jnp.where(head0, -jnp.inf, lt), axis=1, keepdims=True))
        r = lax.broadcasted_iota(jnp.int32, (tq, tk), 0)
        c = lax.broadcasted_iota(jnp.int32, (tq, tk), 1)
        upto = (r[:tk] <= c[:tk]).astype(BF16)
        before = (r[:tk] < c[:tk]).astype(BF16)
        upto2, before2 = _rows([upto, upto]), _rows([before, before])
        causal = lambda d: (lambda rows: c[rows] + d * tk < r[rows])
        strips = _strips(tq)

        def log_terms(z, valid):
            lbs, his, los, sums = [], [], [], []
            for rows in strips:
                lb, lm = _log_sigmoids(z[rows])
                if valid is not None:
                    lm = jnp.where(valid(rows), lm, 0.0)
                hi, lo = _split_bf16(lm)
                lbs.append(lb)
                his.append(hi)
                los.append(lo)
                sums.append(jnp.sum(lm, axis=1, keepdims=True))
            return lbs, jnp.concatenate([_rows(his), _rows(los)], axis=1), _rows(sums)

        def weights(lbs, rest, lm_upto, da, valid):
            a_parts, es, his, los, sums = [], [], [], [], []
            for rows, lb in zip(strips, lbs):
                a = jnp.exp(lb + (rest[rows] - lm_upto[rows]))
                if valid is not None:
                    a = jnp.where(valid(rows), a, 0.0)
                e = da[rows] * a
                hi, lo = _split_bf16(e)
                a_parts.append(a.astype(BF16))
                es.append(e)
                his.append(hi)
                los.append(lo)
                sums.append(jnp.sum(e, axis=1, keepdims=True))
            return _rows(a_parts), es, jnp.concatenate([_rows(his), _rows(los)], axis=1), _rows(sums)

        def score_grads(lbs, es, run_e, e_before, valid):
            parts = []
            for rows, lb, e in zip(strips, lbs, es):
                beta = jnp.exp(lb)
                dz = e * (1.0 - beta) - (run_e[rows] + e_before[rows]) * beta
                if valid is not None:
                    dz = jnp.where(valid(rows), dz, 0.0)
                parts.append(dz.astype(BF16))
            return _rows(parts)

        def block(kj, carry, valid):
            off = pl.multiple_of(kj * tk, tk)
            kb, kt, vt = kb_scr[pl.ds(off, tk), :], kt_scr[kj], vt_scr[kj]
            run_lm0, run_e0, dq0, run_lm1, run_e1, dq1 = carry
            z0, da0 = _mm(qs[0], kt), _mm(dos[0], vt)
            z1, da1 = _mm(qs[1], kt), _mm(dos[1], vt)
            lbs0, split0, lm_sums0 = log_terms(z0, valid)
            lm_upto0 = _mm(split0, upto2)
            lbs1, split1, lm_sums1 = log_terms(z1, valid)
            lm_upto1 = _mm(split1, upto2)
            a0, es0, split0, e_sums0 = weights(lbs0, lts[0] - run_lm0, lm_upto0, da0, valid)
            e_before0 = _mm(split0, before2)
            a1, es1, split1, e_sums1 = weights(lbs1, lts[1] - run_lm1, lm_upto1, da1, valid)
            e_before1 = _mm(split1, before2)
            dz0 = score_grads(lbs0, es0, run_e0, e_before0, valid)
            dkt_blk = _mm(qts[0], dz0)
            dvt_blk = _mm(dots[0], a0)
            dq0 = dq0 + _mm(dz0, kb)
            dz1 = score_grads(lbs1, es1, run_e1, e_before1, valid)
            dkt_ref[kj] += dkt_blk + _mm(qts[1], dz1)
            dvt_ref[kj] += dvt_blk + _mm(dots[1], a1)
            dq1 = dq1 + _mm(dz1, kb)
            return run_lm0 + lm_sums0, run_e0 + e_sums0, dq0, run_lm1 + lm_sums1, run_e1 + e_sums1, dq1

        zero = (jnp.zeros((tq, 1), F32), jnp.zeros((tq, 1), F32), jnp.zeros((tq, HEAD_PAIR), F32))
        first = diagonal_blocks * qi
        carry = lax.fori_loop(0, first, lambda kj, cr: block(kj, cr, None), zero + zero)
        for d in range(diagonal_blocks):
            carry = block(first + d, carry, causal(d))
        dq_ref[...] = jnp.where(head0, carry[2], carry[5]) * ATTN_SCALE

    blk = pl.BlockSpec((tq, HEAD_PAIR), lambda p, i: (i, p))
    seq = pl.BlockSpec((T // tk, HEAD_PAIR, tk), lambda p, i: (0, p, 0))
    transposed = jax.ShapeDtypeStruct((T // tk, SB_WIDTH, tk), F32)
    return pl.pallas_call(
        body, name=name, grid=(N_HEADS // 2, T // tq),
        in_specs=_attn_specs(T, tq) + [blk, blk, AFTER], out_specs=[blk, seq, seq],
        out_shape=[jax.ShapeDtypeStruct((T, SB_WIDTH), F32), transposed, transposed],
        scratch_shapes=[pltpu.VMEM((T, HEAD_PAIR), BF16), pltpu.VMEM((T // tk, HEAD_PAIR, tk), BF16),
                        pltpu.VMEM((T // tk, HEAD_PAIR, tk), BF16)],
        compiler_params=_params(("arbitrary", "arbitrary")),
    )(proj, proj, proj, do, ltot, _in_hbm(after))


def _branch(act_bf16, w_ref):
    return jnp.concatenate([_mm(act_bf16, w_ref[e]) for e in range(w_ref.shape[0])], axis=1)


def _mix_specs(T, D, tm, wbp, w_out):
    gate_col = (POOL_WIDTH + 3 * SB_WIDTH) // D
    row = lambda i: (i, 0)
    return [
        pl.BlockSpec((tm, D), row),
        pl.BlockSpec((tm, POOL_WIDTH), row),
        pl.BlockSpec((tm, SB_WIDTH), row),
        pl.BlockSpec((tm, D), lambda i: (i, gate_col)),
        pl.BlockSpec((tm, D), lambda i: (i, gate_col + 1)),
        pl.BlockSpec(wbp.shape, lambda i: (0, 0, 0)),
        pl.BlockSpec(wbp.shape, lambda i: (0, 0, 0)),
        pl.BlockSpec(w_out.shape, lambda i: (0, 0)),
    ]


def _mix_fwd(h, p, o, proj, wbp, wba, w_out, *, tm, name):
    T, D = h.shape
    tm = min(tm, T)

    def body(h_ref, p_ref, o_ref, glp_ref, gls_ref, wbp_ref, wba_ref, wout_ref, hout_ref, m_ref):
        yp = _branch(p_ref[...].astype(BF16), wbp_ref)
        ys = _branch(o_ref[...].astype(BF16), wba_ref)
        m = (jax.nn.sigmoid(glp_ref[...]) * yp + jax.nn.sigmoid(gls_ref[...]) * ys).astype(BF16)
        m_ref[...] = m
        hout_ref[...] = h_ref[...] + _mm(m, wout_ref[...])

    row = lambda i: (i, 0)
    return pl.pallas_call(
        body, name=name, grid=(T // tm,),
        in_specs=_mix_specs(T, D, tm, wbp, w_out),
        out_specs=[pl.BlockSpec((tm, D), row), pl.BlockSpec((tm, D), row)],
        out_shape=[jax.ShapeDtypeStruct((T, D), F32), jax.ShapeDtypeStruct((T, D), BF16)],
        compiler_params=_params(("arbitrary",)),
    )(h, p, o, proj, proj, wbp, wba, w_out)


def _mix_bwd(dh, p, o, proj, wbp, wba, w_out, after, *, tm, name):
    T, D = dh.shape
    tm = min(tm, T)
    bw = wbp.shape[2]

    def body(dh_ref, p_ref, o_ref, glp_ref, gls_ref, wbp_ref, wba_ref, wout_ref, after_ref,
             dyp_ref, dys_ref, dp_ref, do_ref, dgl_ref):
        dm = _mm_nt(dh_ref[...].astype(BF16), wout_ref[...])
        yp = _branch(p_ref[...].astype(BF16), wbp_ref)
        ys = _branch(o_ref[...].astype(BF16), wba_ref)
        gp = jax.nn.sigmoid(glp_ref[...])
        gs = jax.nn.sigmoid(gls_ref[...])
        dyp = (dm * gp).astype(BF16)
        dys = (dm * gs).astype(BF16)
        dyp_ref[...] = dyp
        dys_ref[...] = dys
        dgl_ref[:, :D] = (dm * yp * gp * (1.0 - gp)).astype(BF16)
        dgl_ref[:, D:] = (dm * ys * gs * (1.0 - gs)).astype(BF16)
        dp = jnp.zeros(dp_ref.shape, F32)
        do_ = jnp.zeros(do_ref.shape, F32)
        for e in range(wbp_ref.shape[0]):
            dp += _mm_nt(dyp[:, e * bw:(e + 1) * bw], wbp_ref[e])
            do_ += _mm_nt(dys[:, e * bw:(e + 1) * bw], wba_ref[e])
        dp_ref[...] = dp
        do_ref[...] = do_

    row = lambda i: (i, 0)
    return pl.pallas_call(
        body, name=name, grid=(T // tm,),
        in_specs=_mix_specs(T, D, tm, wbp, w_out) + [AFTER],
        out_specs=[pl.BlockSpec((tm, D), row), pl.BlockSpec((tm, D), row), pl.BlockSpec((tm, POOL_WIDTH), row),
                   pl.BlockSpec((tm, SB_WIDTH), row), pl.BlockSpec((tm, 2 * D), row)],
        out_shape=[jax.ShapeDtypeStruct((T, D), BF16), jax.ShapeDtypeStruct((T, D), BF16),
                   jax.ShapeDtypeStruct((T, POOL_WIDTH), F32), jax.ShapeDtypeStruct((T, SB_WIDTH), F32),
                   jax.ShapeDtypeStruct((T, 2 * D), BF16)],
        compiler_params=_params(("arbitrary",)),
    )(dh, p, o, proj, proj, wbp, wba, w_out, _in_hbm(after))


def _adamw(w, g, m, v, *, name):
    R, C = w.shape
    tr = _row_tile(R, C)

    def body(w_ref, g_ref, m_ref, v_ref, d_ref, nm_ref, nv_ref):
        g_ = g_ref[...]
        m_ = ADAM_B1 * m_ref[...] + (1.0 - ADAM_B1) * g_
        v_ = ADAM_B2 * v_ref[...] + (1.0 - ADAM_B2) * (g_ * g_)
        m_hat = m_ / (1.0 - ADAM_B1 ** ADAM_STEP)
        v_hat = v_ / (1.0 - ADAM_B2 ** ADAM_STEP)
        d_ref[...] = -ADAM_LR * (m_hat / (jnp.sqrt(v_hat) + ADAM_EPS) + ADAM_WD * w_ref[...])
        nm_ref[...] = m_
        nv_ref[...] = v_

    spec = pl.BlockSpec((tr, C), lambda i: (i, 0))
    return pl.pallas_call(
        body, name=name, grid=(R // tr,), in_specs=[spec] * 4, out_specs=[spec] * 3,
        out_shape=[jax.ShapeDtypeStruct((R, C), F32)] * 3,
        compiler_params=_params(("arbitrary",)),
    )(w, g, m, v)


def _position():
    return lax.axis_index("x"), lax.axis_index("y"), lax.axis_index("c")


def _all_gather(shards, *, name, collective_id):
    n = len(shards)
    n_copies = 9

    def body(*refs):
        ins, outs = refs[:n], refs[n:2 * n]
        send_sems, recv_sems, local_sems = refs[2 * n:]
        x, y, c = _position()
        me, sibling = (x, y, c), (x, y, 1 - c)
        x_nbr, y_nbr, diagonal = (1 - x, y, c), (x, 1 - y, c), (1 - x, 1 - y, c)
        other = lambda pos: (pos[0], pos[1], 1 - c)

        barrier = pltpu.get_barrier_semaphore()
        for peer in (sibling, x_nbr, y_nbr):
            pl.semaphore_signal(barrier, inc=1, device_id=peer, device_id_type=MESH)
        pl.semaphore_wait(barrier, 3)

        def block(a, pos, half=None):
            ref = outs[a].at[4 * pos[0] + 2 * pos[1] + pos[2]]
            rows = ref.shape[0] // 2
            return ref if half is None else ref.at[pl.ds(half * rows, rows)]

        def copy(a, k, pos, to, half=None, src=None):
            return pltpu.make_async_remote_copy(
                src_ref=block(a, pos, half) if src is None else src, dst_ref=block(a, pos, half),
                send_sem=send_sems.at[n_copies * a + k], recv_sem=recv_sems.at[n_copies * a + k],
                device_id=to, device_id_type=MESH)

        started = []
        for a in range(n):
            mine = pltpu.make_async_copy(ins[a], block(a, me), local_sems.at[a])
            mine.start()
            started.append(mine)
        sends = []
        for a in range(n):
            sends += [copy(a, 1, me, x_nbr, src=ins[a]), copy(a, 2, me, y_nbr, src=ins[a]),
                      copy(a, 0, me, sibling, src=ins[a])]
        for cp in sends:
            cp.start()

        def pass_on(copies):
            for cp in copies:
                cp.start()
                sends.append(cp)

        for a in range(n):
            copy(a, 1, x_nbr, me).wait_recv()
            pass_on([copy(a, 5, x_nbr, y_nbr, half=0), copy(a, 3, x_nbr, sibling)])
            copy(a, 2, y_nbr, me).wait_recv()
            pass_on([copy(a, 6, y_nbr, x_nbr, half=1), copy(a, 4, y_nbr, sibling)])
        for a in range(n):
            copy(a, 5, diagonal, me, half=0).wait_recv()
            pass_on([copy(a, 7, diagonal, sibling, half=0)])
            copy(a, 6, diagonal, me, half=1).wait_recv()
            pass_on([copy(a, 8, diagonal, sibling, half=1)])
        for a in range(n):
            copy(a, 0, sibling, me).wait_recv()
            copy(a, 3, other(x_nbr), me).wait_recv()
            copy(a, 4, other(y_nbr), me).wait_recv()
            copy(a, 7, other(diagonal), me, half=0).wait_recv()
            copy(a, 8, other(diagonal), me, half=1).wait_recv()
        for cp in sends:
            cp.wait_send()
        for cp in started:
            cp.wait()

    return pl.kernel(
        body, name=name,
        out_type=[jax.ShapeDtypeStruct((N_DEV,) + s.shape, s.dtype) for s in shards],
        mesh=plsc.ScalarSubcoreMesh(axis_name="sequencer", num_cores=1),
        scratch_types=[pltpu.SemaphoreType.DMA((n_copies * n,)), pltpu.SemaphoreType.DMA((n_copies * n,)),
                       pltpu.SemaphoreType.DMA((n,))],
        compiler_params=pltpu.CompilerParams(collective_id=collective_id),
    )(*shards)


def _chip_sums(grads, *, name):
    _, R, C = grads.shape
    rc = 128 if R % 128 == 0 else R

    def body(g_ref, partial, out_ref, mine, theirs, send_sems, recv_sems, local_sems):
        x, y, c = _position()
        my_chip = 2 * x + y

        def swap(s):
            return pltpu.make_async_remote_copy(
                src_ref=g_ref.at[2 * s + (1 - c)], dst_ref=theirs.at[s],
                send_sem=send_sems.at[s], recv_sem=recv_sems.at[s],
                device_id=(x, y, 1 - c), device_id_type=MESH)

        def load(s):
            return pltpu.make_async_copy(g_ref.at[2 * s + c], mine.at[s], local_sems.at[s])

        for s in range(4):
            swap(s).start()
            load(s).start()
        for s in range(4):
            load(s).wait()
            swap(s).wait_recv()

        def chip_sum(chip, rows):
            return mine[chip, rows, :].astype(F32) + theirs[chip, rows, :].astype(F32)

        for j in (1, 2, 3):
            @pl.loop(0, R // rc)
            def _(t):
                rows = pl.ds(pl.multiple_of(t * rc, rc), rc)
                partial[j - 1, rows, :] = chip_sum(my_chip ^ j, rows).astype(BF16)

        @pl.loop(0, R // rc)
        def _(t):
            rows = pl.ds(pl.multiple_of(t * rc, rc), rc)
            out_ref[rows, :] = chip_sum(my_chip, rows)

        for s in range(4):
            swap(s).wait_send()

    vmem = pl.BlockSpec(memory_space=pltpu.VMEM)
    return pl.pallas_call(
        body, name=name,
        in_specs=[pl.BlockSpec(memory_space=pl.ANY)], out_specs=[vmem, vmem],
        out_shape=[jax.ShapeDtypeStruct((3, R, C), BF16), jax.ShapeDtypeStruct((R, C), F32)],
        scratch_shapes=[
            pltpu.VMEM((4, R, C), BF16), pltpu.VMEM((4, R, C), BF16),
            pltpu.SemaphoreType.DMA((4,)), pltpu.SemaphoreType.DMA((4,)), pltpu.SemaphoreType.DMA((4,)),
        ],
        compiler_params=_params(),
    )(grads)


def _cross_chips(partials, *, name, collective_id):
    n = len(partials)

    def body(*refs):
        ins, outs = refs[:n], refs[n:2 * n]
        send_sems, recv_sems = refs[2 * n:]
        x, y, c = _position()
        my_chip = 2 * x + y
        peers = [((my_chip ^ j) // 2, (my_chip ^ j) % 2, c) for j in (1, 2, 3)]

        barrier = pltpu.get_barrier_semaphore()
        for peer in peers:
            pl.semaphore_signal(barrier, inc=1, device_id=peer, device_id_type=MESH)
        pl.semaphore_wait(barrier, 3)

        copies = [
            pltpu.make_async_remote_copy(
                src_ref=ins[a].at[j], dst_ref=outs[a].at[j],
                send_sem=send_sems.at[3 * a + j], recv_sem=recv_sems.at[3 * a + j],
                device_id=peers[j], device_id_type=MESH)
            for a in range(n) for j in range(3)]
        for cp in copies:
            cp.start()
        for cp in copies:
            cp.wait_recv()
        for cp in copies:
            cp.wait_send()

    return pl.kernel(
        body, name=name,
        out_type=[jax.ShapeDtypeStruct(p.shape, p.dtype) for p in partials],
        mesh=plsc.ScalarSubcoreMesh(axis_name="sequencer", num_cores=1),
        scratch_types=[pltpu.SemaphoreType.DMA((3 * n,)), pltpu.SemaphoreType.DMA((3 * n,))],
        compiler_params=pltpu.CompilerParams(collective_id=collective_id),
    )(*partials)


def _cross_chips_and_gather(partial, slab, *, name, collective_id):
    def body(part_ref, slab_ref, landed_ref, slabs_ref, send_sems, recv_sems, local_sem):
        x, y, c = _position()
        me, my_chip = 4 * x + 2 * y + c, 2 * x + y
        others = [me ^ k for k in range(1, N_DEV)]
        ids = [(o // 4, (o // 2) % 2, o % 2) for o in others]

        barrier = pltpu.get_barrier_semaphore()
        for peer in ids:
            pl.semaphore_signal(barrier, inc=1, device_id=peer, device_id_type=MESH)
        pl.semaphore_wait(barrier, N_DEV - 1)

        mine = pltpu.make_async_copy(slab_ref, slabs_ref.at[me], local_sem)
        mine.start()
        sends = [
            pltpu.make_async_remote_copy(
                src_ref=part_ref.at[j], dst_ref=landed_ref.at[j], send_sem=send_sems.at[j], recv_sem=recv_sems.at[j],
                device_id=((my_chip ^ (j + 1)) // 2, (my_chip ^ (j + 1)) % 2, c), device_id_type=MESH)
            for j in range(3)]
        sends += [
            pltpu.make_async_remote_copy(
                src_ref=slab_ref, dst_ref=slabs_ref.at[me], send_sem=send_sems.at[3 + k], recv_sem=recv_sems.at[3 + k],
                device_id=ids[k], device_id_type=MESH)
            for k in range(N_DEV - 1)]
        arrivals = sends[:3] + [
            pltpu.make_async_remote_copy(
                src_ref=slab_ref, dst_ref=slabs_ref.at[others[k]], send_sem=send_sems.at[3 + k],
                recv_sem=recv_sems.at[3 + k], device_id=ids[k], device_id_type=MESH)
            for k in range(N_DEV - 1)]
        for cp in sends:
            cp.start()
        for cp in arrivals:
            cp.wait_recv()
        for cp in sends:
            cp.wait_send()
        mine.wait()

    n_sems = 3 + N_DEV - 1
    return pl.kernel(
        body, name=name,
        out_type=[jax.ShapeDtypeStruct(partial.shape, partial.dtype),
                  jax.ShapeDtypeStruct((N_DEV,) + slab.shape, slab.dtype)],
        mesh=plsc.ScalarSubcoreMesh(axis_name="sequencer", num_cores=1),
        scratch_types=[pltpu.SemaphoreType.DMA((n_sems,)), pltpu.SemaphoreType.DMA((n_sems,)), pltpu.SemaphoreType.DMA],
        compiler_params=pltpu.CompilerParams(collective_id=collective_id),
    )(partial, slab)


def _sum_devices(gathered, after, *, name):
    _, R, C = gathered.shape

    def body(in_ref, after_ref, out_ref):
        total = in_ref[0]
        for d in range(1, N_DEV):
            total = total + in_ref[d]
        out_ref[...] = total

    return pl.pallas_call(
        body, name=name, grid=(1,),
        in_specs=[pl.BlockSpec((N_DEV, R, C), lambda i: (0, 0, 0)), AFTER],
        out_specs=pl.BlockSpec((R, C), lambda i: (0, 0)),
        out_shape=jax.ShapeDtypeStruct((R, C), F32),
        compiler_params=_params(("arbitrary",)),
    )(gathered, _in_hbm(after))


def _owner_sum(own, landed, after, *, name):
    R, C = own.shape
    tr = _row_tile(R, C)

    def body(own_ref, landed_ref, after_ref, out_ref):
        total = own_ref[...]
        for j in range(3):
            total = total + landed_ref[j].astype(F32)
        out_ref[...] = total

    return pl.pallas_call(
        body, name=name, grid=(R // tr,),
        in_specs=[pl.BlockSpec((tr, C), lambda i: (i, 0)), pl.BlockSpec((3, tr, C), lambda i: (0, i, 0)), AFTER],
        out_specs=pl.BlockSpec((tr, C), lambda i: (i, 0)),
        out_shape=jax.ShapeDtypeStruct((R, C), F32),
        compiler_params=_params(("arbitrary",)),
    )(own, landed, _in_hbm(after))


def _local_step(x, target, norms, pool_w_group, pool_scale, wgu1, wd1, w_in, wbp, wba, w_out, wgu2, wd2, exchange):
    n1g, nmg, n2g, nfg = norms
    D = x.shape[1]
    gu1, hid1 = _ffn_up(x, n1g, wgu1, tm=1024, name="ffn1_up")
    h1 = _ffn_down(x, hid1, wd1, tm=512, name="ffn1_down")
    un, proj = _inproj_fwd(h1, nmg, w_in, tm=1024, name="inproj_fwd")
    p = _pool_fwd(proj, pool_w_group, pool_scale, name="pool_fwd")
    o, ltot = _attn_fwd(proj, name="attn_fwd")
    h2, m = _mix_fwd(h1, p, o, proj, wbp, wba, w_out, tm=256, name="mix_fwd")
    gu2, hid2 = _ffn_up(h2, n2g, wgu2, tm=1024, name="ffn2_up")
    h3 = _ffn_down(h2, hid2, wd2, tm=512, name="ffn2_down")
    dh3, df2, loss, d_nf = _loss_bwd(h3, target, nfg, tm=256, name="loss_bwd")

    d_wd2 = _wgrad_down(hid2, df2, tk=WGRAD_TOKENS, name="ffn2_wgrad_down")
    (g_wd2,), token = exchange("ffn2_down", [d_wd2.reshape(N_DEV, FF_SHARD_PAD, D)])
    dh2, d_n2, n2, dgu2 = _ffn_bwd(dh3, df2, h2, n2g, gu2, wgu2, wd2, token, tm=512, name="ffn2_bwd")
    d_wgu2 = _wgrad_gate_up(n2, dgu2, tk=WGRAD_TOKENS, name="ffn2_wgrad_gate_up")
    (g_wgu2,), token = exchange("ffn2_gate_up", [d_wgu2])

    dyp, dys, dp, do, dgl = _mix_bwd(dh2, p, o, proj, wbp, wba, w_out, token, tm=256, name="mix_bwd")
    d_wout = _wgrad_full(m, dh2, tk=WGRAD_TOKENS, name="wgrad_out")
    d_wbp = _wgrad_full(p, dyp, tk=WGRAD_TOKENS, name="wgrad_branch_pool", split_lanes=wbp.shape[2])
    d_wba = _wgrad_full(o, dys, tk=WGRAD_TOKENS, name="wgrad_branch_attn", split_lanes=wba.shape[2])
    (g_wbp, g_wba, g_wout), token = exchange("mix", [d_wbp, d_wba, d_wout.reshape(N_DEV, D // N_DEV, D)])
    dxp, d_wgroup, d_scale = _pool_bwd(dp, proj, pool_w_group, pool_scale, name="pool_bwd")
    dq, dkt, dvt = _attn_bwd(proj, do, ltot, token, name="attn_bwd")
    dk, dv = (t.transpose(0, 2, 1).reshape(dq.shape) for t in (dkt, dvt))
    dproj = jnp.concatenate([dxp.astype(BF16), dq.astype(BF16), dk.astype(BF16), dv.astype(BF16), dgl], axis=1)
    d_win = _wgrad_in(un, dproj, tk=WGRAD_TOKENS, name="wgrad_in")
    (g_win,), token_in = exchange("w_in", [d_win])
    dh1, df1, d_nm = _inproj_bwd(dproj, dh2, h1, nmg, w_in, tm=1024, name="inproj_bwd")
    d_wd1 = _wgrad_down(hid1, df1, tk=WGRAD_TOKENS, name="ffn1_wgrad_down")
    (g_wd1,), token_down = exchange("ffn1_down", [d_wd1.reshape(N_DEV, FF_SHARD_PAD, D)])
    token = (token_down[(0,) * token_down.ndim] + token_in[(0,) * token_in.ndim]).reshape(1, 1)

    dx, d_n1, n1, dgu1 = _ffn_bwd(dh1, df1, x, n1g, gu1, wgu1, wd1, token, tm=512, name="ffn1_bwd")
    d_wgu1 = _wgrad_gate_up(n1, dgu1, tk=WGRAD_TOKENS, name="ffn1_wgrad_gate_up")
    (g_wgu1, replicated), token = exchange("last", [d_wgu1, d_n1, d_nm, d_n2, d_nf, d_scale, d_wgroup, loss])

    sharded = (g_wgu1, g_wd1, g_win, g_wbp, g_wba, g_wout, g_wgu2, g_wd2)
    return dx, sharded, replicated, token


def _hidden_major(w):
    return jnp.swapaxes(w[0], 0, 1)


def _pad_gate_up(wt):
    d = wt.shape[1]
    wt = wt.astype(BF16).reshape(2, FF_SHARD, d)
    return jnp.pad(wt, ((0, 0), (0, FF_SHARD_PAD - FF_SHARD), (0, 0))).reshape(2 * FF_SHARD_PAD, d)


def _unpad_gate_up(gt):
    d = gt.shape[1]
    return gt.reshape(2, FF_SHARD_PAD, d)[:, :FF_SHARD].reshape(2 * FF_SHARD, d)


def _pad_down(w):
    return jnp.pad(w.astype(BF16), ((0, FF_SHARD_PAD - FF_SHARD), (0, 0)))


def kernel(x, ffn1_norm, ffn1_w_gate_up, ffn1_w_down, mix_norm, w_in, pool_w_group, pool_scale, w_branch_pool, w_branch_attn, w_out, ffn2_norm, ffn2_w_gate_up, ffn2_w_down, final_norm, loss_target, m_ffn1_norm, m_ffn1_w_gate_up, m_ffn1_w_down, m_mix_norm, m_w_in, m_pool_w_group, m_pool_scale, m_w_branch_pool, m_w_branch_attn, m_w_out, m_ffn2_norm, m_ffn2_w_gate_up, m_ffn2_w_down, m_final_norm, v_ffn1_norm, v_ffn1_w_gate_up, v_ffn1_w_down, v_mix_norm, v_w_in, v_pool_w_group, v_pool_scale, v_w_branch_pool, v_w_branch_attn, v_w_out, v_ffn2_norm, v_ffn2_w_gate_up, v_ffn2_w_down, v_final_norm):
    D = x.shape[-1]
    weights = dict(ffn1_norm=ffn1_norm, ffn1_w_gate_up=ffn1_w_gate_up, ffn1_w_down=ffn1_w_down, mix_norm=mix_norm,
                   w_in=w_in, pool_w_group=pool_w_group, pool_scale=pool_scale, w_branch_pool=w_branch_pool,
                   w_branch_attn=w_branch_attn, w_out=w_out, ffn2_norm=ffn2_norm, ffn2_w_gate_up=ffn2_w_gate_up,
                   ffn2_w_down=ffn2_w_down, final_norm=final_norm)
    first = dict(ffn1_norm=m_ffn1_norm, ffn1_w_gate_up=m_ffn1_w_gate_up, ffn1_w_down=m_ffn1_w_down,
                 mix_norm=m_mix_norm, w_in=m_w_in, pool_w_group=m_pool_w_group, pool_scale=m_pool_scale,
                 w_branch_pool=m_w_branch_pool, w_branch_attn=m_w_branch_attn, w_out=m_w_out,
                 ffn2_norm=m_ffn2_norm, ffn2_w_gate_up=m_ffn2_w_gate_up, ffn2_w_down=m_ffn2_w_down,
                 final_norm=m_final_norm)
    second = dict(ffn1_norm=v_ffn1_norm, ffn1_w_gate_up=v_ffn1_w_gate_up, ffn1_w_down=v_ffn1_w_down,
                  mix_norm=v_mix_norm, w_in=v_w_in, pool_w_group=v_pool_w_group, pool_scale=v_pool_scale,
                  w_branch_pool=v_w_branch_pool, w_branch_attn=v_w_branch_attn, w_out=v_w_out,
                  ffn2_norm=v_ffn2_norm, ffn2_w_gate_up=v_ffn2_w_gate_up, ffn2_w_down=v_ffn2_w_down,
                  final_norm=v_final_norm)
    order = list(weights)

    wgu1, = _all_gather([_pad_gate_up(_hidden_major(ffn1_w_gate_up))], name="all_gather_ffn1_gate_up", collective_id=0)
    wd1, = _all_gather([_pad_down(ffn1_w_down[0])], name="all_gather_ffn1_down", collective_id=10)
    win_g, = _all_gather([w_in[0].astype(BF16)], name="all_gather_w_in", collective_id=1)
    wbp_g, wba_g = _all_gather([w_branch_pool[0].astype(BF16), w_branch_attn[0].astype(BF16)],
                               name="all_gather_branches", collective_id=2)
    wout_g, = _all_gather([w_out[0].astype(BF16)], name="all_gather_w_out", collective_id=11)
    wgu2, wd2 = _all_gather([_pad_gate_up(_hidden_major(ffn2_w_gate_up)), _pad_down(ffn2_w_down[0])],
                            name="all_gather_ffn2", collective_id=3)
    wd1 = wd1.reshape(N_DEV * FF_SHARD_PAD, D)
    wd2 = wd2.reshape(N_DEV * FF_SHARD_PAD, D)
    wout_g = wout_g.reshape(D, D)

    cross_ids = {"ffn2_down": 4, "ffn2_gate_up": 5, "mix": 6, "ffn1_down": 7, "w_in": 8, "last": 9}
    small = ["ffn1_norm", "mix_norm", "ffn2_norm", "final_norm", "pool_scale", "pool_w_group"]

    def tile_rows(a):
        a = a.reshape(-1, 128)
        return jnp.pad(a, ((0, -a.shape[0] % 8), (0, 0)))

    def exchange(tag, group):
        if tag == "last":
            slab = jnp.concatenate([tile_rows(g) for g in group[1:-1]] + [jnp.broadcast_to(group[-1], (8, 128))], axis=0)
            partial, own = _chip_sums(group[0], name="chip_sums_last")
            landed, slabs = _cross_chips_and_gather(partial, slab, name="cross_chips_last", collective_id=cross_ids[tag])
            return [(own, landed), slabs], own
        sums = [_chip_sums(g, name=f"chip_sums_{tag}_{i}") for i, g in enumerate(group)]
        landed = _cross_chips([s[0] for s in sums], name="cross_chips_" + tag, collective_id=cross_ids[tag])
        token = sums[0][1] if len(sums) == 1 else sum(s[1][0, 0] for s in sums).reshape(1, 1)
        return [(s[1], l) for s, l in zip(sums, landed)], token

    norms = (ffn1_norm, mix_norm, ffn2_norm, final_norm.reshape(1, D))
    dx, sharded, slabs, last = _local_step(
        x[0], loss_target[0], norms, pool_w_group[0], pool_scale, wgu1, wd1, win_g, wbp_g, wba_g, wout_g, wgu2, wd2,
        exchange)
    names = ["ffn1_w_gate_up", "ffn1_w_down", "w_in", "w_branch_pool", "w_branch_attn", "w_out",
             "ffn2_w_gate_up", "ffn2_w_down"]
    handles = dict(zip(names, sharded))
    grads, delta, new_m, new_v = {}, {}, {}, {}
    after = last
    for k in ("ffn2_w_down", "ffn2_w_gate_up", "w_branch_pool", "w_branch_attn", "w_out", "w_in", "ffn1_w_down",
              "ffn1_w_gate_up"):
        g = _owner_sum(*handles[k], after, name="owner_sum_" + k)
        hidden_major = k.endswith("w_gate_up")
        g = _unpad_gate_up(g) if hidden_major else g[:weights[k].shape[1]]
        view = _hidden_major if hidden_major else (lambda a: a[0])
        back = (lambda a: jnp.swapaxes(a, 0, 1)[None]) if hidden_major else (lambda a: a[None])
        out = _adamw(view(weights[k]), g, view(first[k]), view(second[k]), name="adamw_" + k)
        after = out[0]
        grads[k] = back(g)
        delta[k], new_m[k], new_v[k] = (back(a) for a in out)

    rows = [weights[k].size // 128 for k in small]
    padded_rows = [-(-r // 8) * 8 for r in rows]
    starts = [sum(padded_rows[:i]) for i in range(len(rows) + 1)]
    total = _sum_devices(slabs, after, name="sum_replicated")
    loss_out = total[starts[-1], 0]
    small_w = jnp.concatenate([tile_rows(weights[k]) for k in small], axis=0)
    small_m = jnp.concatenate([tile_rows(first[k]) for k in small], axis=0)
    small_v = jnp.concatenate([tile_rows(second[k]) for k in small], axis=0)
    small_out = _adamw(small_w, total[:starts[-1]], small_m, small_v, name="adamw_replicated")
    for name_, start, n_rows in zip(small, starts, rows):
        shape = weights[name_].shape
        grads[name_] = total[start:start + n_rows].reshape(shape)
        delta[name_], new_m[name_], new_v[name_] = (a[start:start + n_rows].reshape(shape) for a in small_out)

    return (loss_out, dx[None], *[grads[k] for k in order], *[delta[k] for k in order],
            *[new_m[k] for k in order], *[new_v[k] for k in order])
```

```python
import functools

import jax
import jax.numpy as jnp
from jax import lax
from jax.experimental import pallas as pl
from jax.experimental.pallas import tpu as pltpu
from jax.experimental.pallas import tpu_sc as plsc

F32 = jnp.float32
BF16 = jnp.bfloat16
MESH = pl.DeviceIdType.MESH

RMS_EPS = 1e-6
N_DEV = 8
N_HEADS = 8
HEAD_DIM = 64
HEAD_PAIR = 2 * HEAD_DIM
POOL_WINDOWS = (2, 4, 8, 16)
POOL_GROUP = 128
POOL_WIDTH = 512
SB_WIDTH = 512
FF_SHARD = 352
FF_SHARD_PAD = 384
ATTN_K_BLOCK = 256
ATTN_Q_BLOCK_FWD = 512
ATTN_Q_BLOCK_BWD = 256
ATTN_SCALE = 0.125

ADAM_LR = 0.001
ADAM_B1 = 0.9
ADAM_B2 = 0.999
ADAM_EPS = 1e-08
ADAM_WD = 0.01
ADAM_STEP = 10

VMEM_LIMIT = 48 << 20
WGRAD_TOKENS = 2048


def _tc_call(body, **kwargs):
    call = pl.pallas_call(body, **kwargs)
    return lambda *operands: call(*[pltpu.with_memory_space_constraint(o, pltpu.HBM) for o in operands])


def _params(dims=None):
    return pltpu.CompilerParams(dimension_semantics=dims, vmem_limit_bytes=VMEM_LIMIT)


def _mm(a, b):
    return jnp.dot(a, b, preferred_element_type=F32)


def _mm_nt(a, b):
    return lax.dot_general(a, b, (((1,), (1,)), ((), ())), preferred_element_type=F32)


def _mm_tn(a, b):
    return lax.dot_general(a, b, (((0,), (0,)), ((), ())), preferred_element_type=F32)


def _row_tile(rows, cols):
    limit = max(8, (512 * 1024) // cols)
    return max(t for t in range(8, rows + 1, 8) if rows % t == 0 and (t <= limit or t == 8))


def _rstd(xf):
    return lax.rsqrt(jnp.mean(xf * xf, axis=-1, keepdims=True) + RMS_EPS)


def _rms_bwd(xf, gain, dn):
    r = _rstd(xf)
    xh = xf * r
    dgain = jnp.sum(dn * xh, axis=0, keepdims=True)
    dxh = dn * gain
    dx = r * (dxh - xh * jnp.mean(dxh * xh, axis=-1, keepdims=True))
    return dx, dgain


def _ffn_up(x, gain, wgu, *, tm, name):
    T, D = x.shape
    tm = min(tm, T)
    nb, bw = wgu.shape[0] // 2, wgu.shape[1]

    def body(x_ref, gain_ref, wg_ref, wu_ref, gu_ref, hid_ref, n_scr):
        @pl.when(pl.program_id(1) == 0)
        def _():
            xf = x_ref[...]
            n_scr[...] = (xf * _rstd(xf) * gain_ref[...]).astype(BF16)

        halves = (pl.ds(0, tm // 2), pl.ds(tm // 2, tm // 2))
        wg, wu = wg_ref[...], wu_ref[...]
        gus = [(_mm_nt(n_scr[rows, :], wg), _mm_nt(n_scr[rows, :], wu)) for rows in halves]
        for rows, (g, u) in zip(halves, gus):
            gu_ref[0, rows, :] = g.astype(BF16)
            gu_ref[1, rows, :] = u.astype(BF16)
            hid_ref[rows, :] = (g * jax.nn.sigmoid(g) * u).astype(BF16)

    return _tc_call(
        body, name=name, grid=(T // tm, nb),
        in_specs=[
            pl.BlockSpec((tm, D), lambda i, j: (i, 0)),
            pl.BlockSpec((1, D), lambda i, j: (0, 0)),
            pl.BlockSpec((None, bw, D), lambda i, j: (j, 0, 0)),
            pl.BlockSpec((None, bw, D), lambda i, j: (j + nb, 0, 0)),
        ],
        out_specs=[
            pl.BlockSpec((2, tm, bw), lambda i, j: (0, i, j)),
            pl.BlockSpec((tm, bw), lambda i, j: (i, j)),
        ],
        out_shape=[jax.ShapeDtypeStruct((2, T, nb * bw), BF16), jax.ShapeDtypeStruct((T, nb * bw), BF16)],
        scratch_shapes=[pltpu.VMEM((tm, D), BF16)],
        compiler_params=_params(("arbitrary", "arbitrary")),
    )(x, gain, wgu, wgu)


def _ffn_down(x, hid, wd, *, tm, name):
    T, D = x.shape
    tm = min(tm, T)
    F = hid.shape[1]

    def body(x_ref, hid_ref, wd_ref, h_ref):
        h_ref[...] = x_ref[...] + 0.5 * _mm(hid_ref[...], wd_ref[...])

    return _tc_call(
        body, name=name, grid=(T // tm,),
        in_specs=[
            pl.BlockSpec((tm, D), lambda i: (i, 0)),
            pl.BlockSpec((tm, F), lambda i: (i, 0)),
            pl.BlockSpec((F, D), lambda i: (0, 0)),
        ],
        out_specs=pl.BlockSpec((tm, D), lambda i: (i, 0)),
        out_shape=jax.ShapeDtypeStruct((T, D), F32),
        compiler_params=_params(("arbitrary",)),
    )(x, hid, wd)


AFTER = pl.BlockSpec(memory_space=pltpu.HBM)


def _in_hbm(token):
    return pltpu.with_memory_space_constraint(token, pltpu.HBM)


def _ffn_bwd(dh, df, x, gain, gu, wgu, wd, after, *, tm, name):
    T, D = x.shape
    tm = min(tm, T)
    nb, bw = wgu.shape[0] // 2, wgu.shape[1]

    def body(dh_ref, df_ref, x_ref, gain_ref, gu_ref, wg_ref, wu_ref, wd_ref, after_ref,
             dx_ref, dgain_ref, n_ref, dgu_ref, dn_acc):
        i, j = pl.program_id(0), pl.program_id(1)

        @pl.when(j == 0)
        def _():
            xf = x_ref[...]
            n_ref[...] = (xf * _rstd(xf) * gain_ref[...]).astype(BF16)
            dn_acc[...] = jnp.zeros_like(dn_acc)

        @pl.when((i == 0) & (j == 0))
        def _():
            dgain_ref[...] = jnp.zeros_like(dgain_ref)

        halves = (pl.ds(0, tm // 2), pl.ds(tm // 2, tm // 2))
        wd, wg, wu = wd_ref[...], wg_ref[...], wu_ref[...]
        dhids = [_mm_nt(df_ref[rows, :], wd) for rows in halves]
        for rows, dhid in zip(halves, dhids):
            g = gu_ref[0, rows, :].astype(F32)
            u = gu_ref[1, rows, :].astype(F32)
            s = jax.nn.sigmoid(g)
            silu = g * s
            dg = (dhid * u * (s * (1.0 + g * (1.0 - s)))).astype(BF16)
            du = (dhid * silu).astype(BF16)
            dgu_ref[0, rows, :] = dg
            dgu_ref[1, rows, :] = du
            dn_acc[rows, :] += _mm(dg, wg) + _mm(du, wu)

        @pl.when(j == nb - 1)
        def _():
            dx, dgain = _rms_bwd(x_ref[...], gain_ref[...], dn_acc[...])
            dx_ref[...] = dh_ref[...] + dx
            dgain_ref[...] += dgain

    row = lambda i, j: (i, 0)
    return _tc_call(
        body, name=name, grid=(T // tm, nb),
        in_specs=[
            pl.BlockSpec((tm, D), row),
            pl.BlockSpec((tm, D), row),
            pl.BlockSpec((tm, D), row),
            pl.BlockSpec((1, D), lambda i, j: (0, 0)),
            pl.BlockSpec((2, tm, bw), lambda i, j: (0, i, j)),
            pl.BlockSpec((None, bw, D), lambda i, j: (j, 0, 0)),
            pl.BlockSpec((None, bw, D), lambda i, j: (j + nb, 0, 0)),
            pl.BlockSpec((bw, D), lambda i, j: (j, 0)),
            AFTER,
        ],
        out_specs=[
            pl.BlockSpec((tm, D), row),
            pl.BlockSpec((1, D), lambda i, j: (0, 0)),
            pl.BlockSpec((tm, D), row),
            pl.BlockSpec((2, tm, bw), lambda i, j: (0, i, j)),
        ],
        out_shape=[
            jax.ShapeDtypeStruct((T, D), F32),
            jax.ShapeDtypeStruct((1, D), F32),
            jax.ShapeDtypeStruct((T, D), BF16),
            jax.ShapeDtypeStruct((2, T, nb * bw), BF16),
        ],
        scratch_shapes=[pltpu.VMEM((tm, D), F32)],
        compiler_params=_params(("arbitrary", "arbitrary")),
    )(dh, df, x, gain, gu, wgu, wgu, wd, _in_hbm(after))


def _wgrad(a, b, *, grid, a_spec, b_spec, out_spec, out_shape, acc_shape, name, split_lanes=0):
    nk = grid[2]

    def body(a_ref, b_ref, o_ref, acc):
        k = pl.program_id(2)

        @pl.when(k == 0)
        def _():
            acc[...] = jnp.zeros_like(acc)

        acc[...] += _mm_tn(a_ref[...].astype(BF16), b_ref[...].astype(BF16))

        @pl.when(k == nk - 1)
        def _():
            if split_lanes:
                for e in range(o_ref.shape[0]):
                    o_ref[e] = acc[:, e * split_lanes:(e + 1) * split_lanes].astype(o_ref.dtype)
            else:
                o_ref[...] = acc[...].astype(o_ref.dtype)

    return _tc_call(
        body, name=name, grid=grid, in_specs=[a_spec, b_spec], out_specs=out_spec,
        out_shape=jax.ShapeDtypeStruct(out_shape, BF16),
        scratch_shapes=[pltpu.VMEM(acc_shape, F32)],
        compiler_params=_params(("arbitrary", "arbitrary", "arbitrary")),
    )(a, b)


def _wgrad_gate_up(n, dgu, *, tk, name):
    T, D = n.shape
    tk = min(tk, T)
    bw = FF_SHARD_PAD * 2
    nb = dgu.shape[2] // bw
    return _wgrad(
        dgu, n, grid=(2 * nb, 1, T // tk), name=name,
        a_spec=pl.BlockSpec((None, tk, bw), lambda m, c, k: (m // nb, k, m % nb)),
        b_spec=pl.BlockSpec((tk, D), lambda m, c, k: (k, 0)),
        out_spec=pl.BlockSpec((None, bw, D), lambda m, c, k: (m, 0, 0)),
        out_shape=(2 * nb, bw, D), acc_shape=(bw, D))


def _wgrad_down(hid, df, *, tk, name):
    T, D = df.shape
    tk = min(tk, T)
    bw = FF_SHARD_PAD * 2
    nb = hid.shape[1] // bw
    return _wgrad(
        hid, df, grid=(nb, 1, T // tk), name=name,
        a_spec=pl.BlockSpec((tk, bw), lambda m, c, k: (k, m)),
        b_spec=pl.BlockSpec((tk, D), lambda m, c, k: (k, 0)),
        out_spec=pl.BlockSpec((bw, D), lambda m, c, k: (m, 0)),
        out_shape=(nb * bw, D), acc_shape=(bw, D))


def _wgrad_in(un, dproj, *, tk, name):
    T, D = un.shape
    tk = min(tk, T)
    bw = dproj.shape[1] // N_DEV
    return _wgrad(
        un, dproj, grid=(1, N_DEV, T // tk), name=name,
        a_spec=pl.BlockSpec((tk, D), lambda m, c, k: (k, 0)),
        b_spec=pl.BlockSpec((tk, bw), lambda m, c, k: (k, c)),
        out_spec=pl.BlockSpec((None, D, bw), lambda m, c, k: (c, 0, 0)),
        out_shape=(N_DEV, D, bw), acc_shape=(D, bw))


def _wgrad_full(a, b, *, tk, name, split_lanes=0):
    T, M = a.shape
    tk = min(tk, T)
    N = b.shape[1]
    if split_lanes:
        out_shape = (N // split_lanes, M, split_lanes)
        out_spec = pl.BlockSpec(out_shape, lambda m, c, k: (0, 0, 0))
    else:
        out_shape = (M, N)
        out_spec = pl.BlockSpec(out_shape, lambda m, c, k: (0, 0))
    return _wgrad(
        a, b, grid=(1, 1, T // tk), name=name,
        a_spec=pl.BlockSpec((tk, M), lambda m, c, k: (k, 0)),
        b_spec=pl.BlockSpec((tk, N), lambda m, c, k: (k, 0)),
        out_spec=out_spec, out_shape=out_shape, acc_shape=(M, N), split_lanes=split_lanes)


def _loss_bwd(h, target, gain, *, tm, name):
    T, D = h.shape
    tm = min(tm, T)

    def body(h_ref, t_ref, gain_ref, dh_ref, df_ref, loss_ref, dgain_ref):
        @pl.when(pl.program_id(0) == 0)
        def _():
            loss_ref[...] = jnp.zeros_like(loss_ref)
            dgain_ref[...] = jnp.zeros_like(dgain_ref)

        xf = h_ref[...]
        gain = gain_ref[...]
        err = xf * _rstd(xf) * gain - t_ref[...]
        loss_ref[...] += 0.5 * jnp.sum(jnp.mean(err * err, axis=-1, keepdims=True), axis=0, keepdims=True)
        dx, dgain = _rms_bwd(xf, gain, err * (1.0 / D))
        dh_ref[...] = dx
        df_ref[...] = (0.5 * dx).astype(BF16)
        dgain_ref[...] += dgain

    row = lambda i: (i, 0)
    fixed = lambda i: (0, 0)
    return _tc_call(
        body, name=name, grid=(T // tm,),
        in_specs=[pl.BlockSpec((tm, D), row), pl.BlockSpec((tm, D), row), pl.BlockSpec((1, D), fixed)],
        out_specs=[pl.BlockSpec((tm, D), row), pl.BlockSpec((tm, D), row), pl.BlockSpec((1, 128), fixed),
                   pl.BlockSpec((1, D), fixed)],
        out_shape=[jax.ShapeDtypeStruct((T, D), F32), jax.ShapeDtypeStruct((T, D), BF16),
                   jax.ShapeDtypeStruct((1, 128), F32), jax.ShapeDtypeStruct((1, D), F32)],
        compiler_params=_params(("arbitrary",)),
    )(h, target, gain)


def _inproj_fwd(h, gain, w_in, *, tm, name):
    T, D = h.shape
    tm = min(tm, T)
    nb, bw = w_in.shape[0], w_in.shape[2]

    def body(h_ref, gain_ref, w_ref, un_ref, proj_ref):
        @pl.when(pl.program_id(1) == 0)
        def _():
            xf = h_ref[...]
            un_ref[...] = (xf * _rstd(xf) * gain_ref[...]).astype(BF16)

        proj_ref[...] = _mm(un_ref[...], w_ref[...])

    return _tc_call(
        body, name=name, grid=(T // tm, nb),
        in_specs=[
            pl.BlockSpec((tm, D), lambda i, j: (i, 0)),
            pl.BlockSpec((1, D), lambda i, j: (0, 0)),
            pl.BlockSpec((None, D, bw), lambda i, j: (j, 0, 0)),
        ],
        out_specs=[pl.BlockSpec((tm, D), lambda i, j: (i, 0)), pl.BlockSpec((tm, bw), lambda i, j: (i, j))],
        out_shape=[jax.ShapeDtypeStruct((T, D), BF16), jax.ShapeDtypeStruct((T, nb * bw), F32)],
        compiler_params=_params(("arbitrary", "arbitrary")),
    )(h, gain, w_in)


def _inproj_bwd(dproj, dh, h, gain, w_in, *, tm, name):
    T, D = h.shape
    tm = min(tm, T)
    nb, bw = w_in.shape[0], w_in.shape[2]

    def body(dp_ref, dh_ref, h_ref, gain_ref, w_ref, dx_ref, df_ref, dgain_ref, acc):
        i, j = pl.program_id(0), pl.program_id(1)

        @pl.when(j == 0)
        def _():
            acc[...] = jnp.zeros_like(acc)

        @pl.when((i == 0) & (j == 0))
        def _():
            dgain_ref[...] = jnp.zeros_like(dgain_ref)

        acc[...] += _mm_nt(dp_ref[...], w_ref[...])

        @pl.when(j == nb - 1)
        def _():
            dx, dgain = _rms_bwd(h_ref[...], gain_ref[...], acc[...])
            dh_in = dh_ref[...] + dx
            dx_ref[...] = dh_in
            df_ref[...] = (0.5 * dh_in).astype(BF16)
            dgain_ref[...] += dgain

    row = lambda i, j: (i, 0)
    return _tc_call(
        body, name=name, grid=(T // tm, nb),
        in_specs=[
            pl.BlockSpec((tm, bw), lambda i, j: (i, j)),
            pl.BlockSpec((tm, D), row),
            pl.BlockSpec((tm, D), row),
            pl.BlockSpec((1, D), lambda i, j: (0, 0)),
            pl.BlockSpec((None, D, bw), lambda i, j: (j, 0, 0)),
        ],
        out_specs=[pl.BlockSpec((tm, D), row), pl.BlockSpec((tm, D), row), pl.BlockSpec((1, D), lambda i, j: (0, 0))],
        out_shape=[jax.ShapeDtypeStruct((T, D), F32), jax.ShapeDtypeStruct((T, D), BF16),
                   jax.ShapeDtypeStruct((1, D), F32)],
        scratch_shapes=[pltpu.VMEM((tm, D), F32)],
        compiler_params=_params(("arbitrary", "arbitrary")),
    )(dproj, dh, h, gain, w_in)


def _window_sum(x, row, doublings, *, backward):
    T = x.shape[0]
    s = x
    for k in range(doublings):
        sh = 1 << k
        if backward:
            s = s + jnp.where(row < T - sh, pltpu.roll(s, T - sh, 0), 0.0)
        else:
            s = s + jnp.where(row >= sh, pltpu.roll(s, sh, 0), 0.0)
    return s


def _pool_fwd(proj, w_group, scale, *, name):
    T = proj.shape[0]

    def body(xp_ref, w_ref, scale_ref, p_ref):
        row = lax.broadcasted_iota(jnp.int32, (T, POOL_GROUP), 0)
        for gi, window in enumerate(POOL_WINDOWS):
            cols = slice(gi * POOL_GROUP, (gi + 1) * POOL_GROUP)
            x = xp_ref[:, cols]
            inv_count = 1.0 / jnp.minimum(row + 1, window).astype(F32)
            yc = _window_sum(x, row, gi + 1, backward=False) * inv_count - x
            pre = _mm(yc.astype(BF16), w_ref[gi].astype(BF16))
            p_ref[:, cols] = pre * scale_ref[:, cols]

    return _tc_call(
        body, name=name, grid=(1,),
        in_specs=[
            pl.BlockSpec((T, POOL_WIDTH), lambda i: (0, 0)),
            pl.BlockSpec(w_group.shape, lambda i: (0, 0, 0)),
            pl.BlockSpec((1, POOL_WIDTH), lambda i: (0, 0)),
        ],
        out_specs=pl.BlockSpec((T, POOL_WIDTH), lambda i: (0, 0)),
        out_shape=jax.ShapeDtypeStruct((T, POOL_WIDTH), F32),
        compiler_params=_params(("arbitrary",)),
    )(proj, w_group, scale)


def _pool_bwd(dp, proj, w_group, scale, *, name):
    T = proj.shape[0]

    def body(dp_ref, xp_ref, w_ref, scale_ref, dxp_ref, dw_ref, dscale_ref):
        row = lax.broadcasted_iota(jnp.int32, (T, POOL_GROUP), 0)
        for gi, window in enumerate(POOL_WINDOWS):
            cols = slice(gi * POOL_GROUP, (gi + 1) * POOL_GROUP)
            x = xp_ref[:, cols]
            inv_count = 1.0 / jnp.minimum(row + 1, window).astype(F32)
            yc = (_window_sum(x, row, gi + 1, backward=False) * inv_count - x).astype(BF16)
            w = w_ref[gi].astype(BF16)
            pre = _mm(yc, w)
            dpg = dp_ref[:, cols]
            dscale_ref[:, cols] = jnp.sum(dpg * pre, axis=0, keepdims=True)
            dpre = (dpg * scale_ref[:, cols]).astype(BF16)
            dw_ref[gi] = _mm_tn(yc, dpre)
            dyc = _mm_nt(dpre, w)
            dxp_ref[:, cols] = _window_sum(dyc * inv_count, row, gi + 1, backward=True) - dyc

    return _tc_call(
        body, name=name, grid=(1,),
        in_specs=[
            pl.BlockSpec((T, POOL_WIDTH), lambda i: (0, 0)),
            pl.BlockSpec((T, POOL_WIDTH), lambda i: (0, 0)),
            pl.BlockSpec(w_group.shape, lambda i: (0, 0, 0)),
            pl.BlockSpec((1, POOL_WIDTH), lambda i: (0, 0)),
        ],
        out_specs=[
            pl.BlockSpec((T, POOL_WIDTH), lambda i: (0, 0)),
            pl.BlockSpec(w_group.shape, lambda i: (0, 0, 0)),
            pl.BlockSpec((1, POOL_WIDTH), lambda i: (0, 0)),
        ],
        out_shape=[jax.ShapeDtypeStruct((T, POOL_WIDTH), F32), jax.ShapeDtypeStruct(w_group.shape, F32),
                   jax.ShapeDtypeStruct((1, POOL_WIDTH), F32)],
        compiler_params=_params(("arbitrary",)),
    )(dp, proj, w_group, scale)


ATTN_STRIP = 32


def _log_sigmoids(z):
    lb = jnp.minimum(z, 0.0) - jnp.log(1.0 + jnp.exp(-jnp.abs(z)))
    return lb, lb - z


def _transposed_blocks(x_ref, blocks_scr, tq):
    for b in range(blocks_scr.shape[0]):
        blocks_scr[b] = x_ref[b * tq:(b + 1) * tq, :].T.astype(BF16)


def _split_bf16(x):
    hi = x.astype(BF16)
    return hi, (x - hi.astype(F32)).astype(BF16)


def _strips(n):
    return [slice(i, i + ATTN_STRIP) for i in range(0, n, ATTN_STRIP)]


def _rows(parts):
    return jnp.concatenate(parts, axis=0)


def _attn_specs(T, tq):
    q_col = POOL_WIDTH // HEAD_PAIR
    k_col = q_col + SB_WIDTH // HEAD_PAIR
    v_col = k_col + SB_WIDTH // HEAD_PAIR
    return [
        pl.BlockSpec((tq, HEAD_PAIR), lambda p, i: (i, q_col + p)),
        pl.BlockSpec((T, HEAD_PAIR), lambda p, i: (0, k_col + p)),
        pl.BlockSpec((T, HEAD_PAIR), lambda p, i: (0, v_col + p)),
    ]


def _attn_fwd(proj, *, name):
    T = proj.shape[0]
    tk = min(ATTN_K_BLOCK, T)
    tq = min(ATTN_Q_BLOCK_FWD, T)
    diagonal_blocks = tq // tk

    def body(q_ref, k_ref, v_ref, o_ref, lt_ref, kt_scr, vb_scr):
        qi = pl.program_id(1)

        @pl.when(qi == 0)
        def _():
            _transposed_blocks(k_ref, kt_scr, tk)
            vb_scr[...] = v_ref[...].astype(BF16)

        head0 = lax.broadcasted_iota(jnp.int32, (tq, HEAD_PAIR), 1) < HEAD_DIM
        q = q_ref[...] * ATTN_SCALE
        qs = (jnp.where(head0, q, 0.0).astype(BF16), jnp.where(head0, 0.0, q).astype(BF16))
        r = lax.broadcasted_iota(jnp.int32, (tq, tk), 0)
        c = lax.broadcasted_iota(jnp.int32, (tq, tk), 1)
        later = (r[:tk] > c[:tk]).astype(BF16)
        later2 = _rows([later, later])
        causal = lambda d: (lambda rows: c[rows] + d * tk < r[rows])
        strips = _strips(tq)

        def log_terms(z, valid):
            lbs, his, los, sums = [], [], [], []
            for rows in strips:
                lb, lm = _log_sigmoids(z[rows])
                if valid is not None:
                    lm = jnp.where(valid(rows), lm, 0.0)
                hi, lo = _split_bf16(lm)
                lbs.append(lb)
                his.append(hi)
                los.append(lo)
                sums.append(jnp.sum(lm, axis=1, keepdims=True))
            return lbs, jnp.concatenate([_rows(his), _rows(los)], axis=1), _rows(sums)

        def weights(lbs, run, after, valid):
            parts = []
            for rows, lb in zip(strips, lbs):
                a = jnp.exp(lb + run[rows] + after[rows])
                if valid is not None:
                    a = jnp.where(valid(rows), a, 0.0)
                parts.append(a.astype(BF16))
            return _rows(parts)

        def block(kj, carry, valid):
            kt = kt_scr[kj]
            vb = vb_scr[pl.ds(pl.multiple_of(kj * tk, tk), tk), :]
            run0, o0, run1, o1 = carry
            z0 = _mm(qs[0], kt)
            z1 = _mm(qs[1], kt)
            lbs0, split0, sums0 = log_terms(z0, valid)
            after0 = _mm(split0, later2)
            lbs1, split1, sums1 = log_terms(z1, valid)
            after1 = _mm(split1, later2)
            o0 = o0 + _mm(weights(lbs0, run0, after0, valid), vb)
            o1 = o1 + _mm(weights(lbs1, run1, after1, valid), vb)
            return run0 + sums0, o0, run1 + sums1, o1

        zero = (jnp.zeros((tq, 1), F32), jnp.zeros((tq, HEAD_PAIR), F32))
        first = diagonal_blocks * qi
        carry = zero + zero
        for d in reversed(range(diagonal_blocks)):
            carry = block(first + d, carry, causal(d))
        carry = lax.fori_loop(0, first, lambda it, cr: block(first - 1 - it, cr, None), carry)
        o_ref[...] = jnp.where(head0, carry[1], carry[3])
        lt_ref[...] = jnp.where(head0, carry[0], carry[2])

    out_spec = pl.BlockSpec((tq, HEAD_PAIR), lambda p, i: (i, p))
    return _tc_call(
        body, name=name, grid=(N_HEADS // 2, T // tq),
        in_specs=_attn_specs(T, tq), out_specs=[out_spec, out_spec],
        out_shape=[jax.ShapeDtypeStruct((T, SB_WIDTH), F32), jax.ShapeDtypeStruct((T, SB_WIDTH), F32)],
        scratch_shapes=[pltpu.VMEM((T // tk, HEAD_PAIR, tk), BF16), pltpu.VMEM((T, HEAD_PAIR), BF16)],
        compiler_params=_params(("arbitrary", "arbitrary")),
    )(proj, proj, proj)


def _attn_bwd(proj, do, ltot, after, *, name):
    T = proj.shape[0]
    tk = min(ATTN_K_BLOCK, T)
    tq = min(ATTN_Q_BLOCK_BWD, T)
    diagonal_blocks = tq // tk

    def body(q_ref, k_ref, v_ref, do_ref, lt_ref, after_ref, dq_ref, dkt_ref, dvt_ref, kb_scr, kt_scr, vt_scr):
        qi = pl.program_id(1)

        @pl.when(qi == 0)
        def _():
            kb_scr[...] = k_ref[...].astype(BF16)
            _transposed_blocks(k_ref, kt_scr, tk)
            _transposed_blocks(v_ref, vt_scr, tk)
            dkt_ref[...] = jnp.zeros_like(dkt_ref)
            dvt_ref[...] = jnp.zeros_like(dvt_ref)

        head0 = lax.broadcasted_iota(jnp.int32, (tq, HEAD_PAIR), 1) < HEAD_DIM
        q, do_, lt = q_ref[...] * ATTN_SCALE, do_ref[...], lt_ref[...]
        qs = (jnp.where(head0, q, 0.0).astype(BF16), jnp.where(head0, 0.0, q).astype(BF16))
        q_heads = (jnp.where(head0, q, 0.0), jnp.where(head0, 0.0, q))
        do_heads = (jnp.where(head0, do_, 0.0), jnp.where(head0, 0.0, do_))
        dos = tuple(d.astype(BF16) for d in do_heads)
        qts = tuple(x.T.astype(BF16) for x in q_heads)
        dots = tuple(d.T.astype(BF16) for d in do_heads)
        lts = (jnp.max(jnp.where(head0, lt, -jnp.inf), axis=1, keepdims=True),
               jnp.max(jnp.where(head0, -jnp.inf, lt), axis=1, keepdims=True))
        r = lax.broadcasted_iota(jnp.int32, (tq, tk), 0)
        c = lax.broadcasted_iota(jnp.int32, (tq, tk), 1)
        upto = (r[:tk] <= c[:tk]).astype(BF16)
        before = (r[:tk] < c[:tk]).astype(BF16)
        upto2, before2 = _rows([upto, upto]), _rows([before, before])
        causal = lambda d: (lambda rows: c[rows] + d * tk < r[rows])
        strips = _strips(tq)

        def log_terms(z, valid):
            lbs, his, los, sums = [], [], [], []
            for rows in strips:
                lb, lm = _log_sigmoids(z[rows])
                if valid is not None:
                    lm = jnp.where(valid(rows), lm, 0.0)
                hi, lo = _split_bf16(lm)
                lbs.append(lb)
                his.append(hi)
                los.append(lo)
                sums.append(jnp.sum(lm, axis=1, keepdims=True))
            return lbs, jnp.concatenate([_rows(his), _rows(los)], axis=1), _rows(sums)

        def weights(lbs, rest, lm_upto, da, valid):
            a_parts, es, his, los, sums = [], [], [], [], []
            for rows, lb in zip(strips, lbs):
                a = jnp.exp(lb + (rest[rows] - lm_upto[rows]))
                if valid is not None:
                    a = jnp.where(valid(rows), a, 0.0)
                e = da[rows] * a
                hi, lo = _split_bf16(e)
                a_parts.append(a.astype(BF16))
                es.append(e)
                his.append(hi)
                los.append(lo)
                sums.append(jnp.sum(e, axis=1, keepdims=True))
            return _rows(a_parts), es, jnp.concatenate([_rows(his), _rows(los)], axis=1), _rows(sums)

        def score_grads(lbs, es, run_e, e_before, valid):
            parts = []
            for rows, lb, e in zip(strips, lbs, es):
                beta = jnp.exp(lb)
                dz = e * (1.0 - beta) - (run_e[rows] + e_before[rows]) * beta
                if valid is not None:
                    dz = jnp.where(valid(rows), dz, 0.0)
                parts.append(dz.astype(BF16))
            return _rows(parts)

        def block(kj, carry, valid):
            off = pl.multiple_of(kj * tk, tk)
            kb, kt, vt = kb_scr[pl.ds(off, tk), :], kt_scr[kj], vt_scr[kj]
            run_lm0, run_e0, dq0, run_lm1, run_e1, dq1 = carry
            z0, da0 = _mm(qs[0], kt), _mm(dos[0], vt)
            z1, da1 = _mm(qs[1], kt), _mm(dos[1], vt)
            lbs0, split0, lm_sums0 = log_terms(z0, valid)
            lm_upto0 = _mm(split0, upto2)
            lbs1, split1, lm_sums1 = log_terms(z1, valid)
            lm_upto1 = _mm(split1, upto2)
            a0, es0, split0, e_sums0 = weights(lbs0, lts[0] - run_lm0, lm_upto0, da0, valid)
            e_before0 = _mm(split0, before2)
            a1, es1, split1, e_sums1 = weights(lbs1, lts[1] - run_lm1, lm_upto1, da1, valid)
            e_before1 = _mm(split1, before2)
            dz0 = score_grads(lbs0, es0, run_e0, e_before0, valid)
            dkt_blk = _mm(qts[0], dz0)
            dvt_blk = _mm(dots[0], a0)
            dq0 = dq0 + _mm(dz0, kb)
            dz1 = score_grads(lbs1, es1, run_e1, e_before1, valid)
            dkt_ref[kj] += dkt_blk + _mm(qts[1], dz1)
            dvt_ref[kj] += dvt_blk + _mm(dots[1], a1)
            dq1 = dq1 + _mm(dz1, kb)
            return run_lm0 + lm_sums0, run_e0 + e_sums0, dq0, run_lm1 + lm_sums1, run_e1 + e_sums1, dq1

        zero = (jnp.zeros((tq, 1), F32), jnp.zeros((tq, 1), F32), jnp.zeros((tq, HEAD_PAIR), F32))
        first = diagonal_blocks * qi
        carry = lax.fori_loop(0, first, lambda kj, cr: block(kj, cr, None), zero + zero)
        for d in range(diagonal_blocks):
            carry = block(first + d, carry, causal(d))
        dq_ref[...] = jnp.where(head0, carry[2], carry[5]) * ATTN_SCALE

    blk = pl.BlockSpec((tq, HEAD_PAIR), lambda p, i: (i, p))
    seq = pl.BlockSpec((T // tk, HEAD_PAIR, tk), lambda p, i: (0, p, 0))
    transposed = jax.ShapeDtypeStruct((T // tk, SB_WIDTH, tk), F32)
    return _tc_call(
        body, name=name, grid=(N_HEADS // 2, T // tq),
        in_specs=_attn_specs(T, tq) + [blk, blk, AFTER], out_specs=[blk, seq, seq],
        out_shape=[jax.ShapeDtypeStruct((T, SB_WIDTH), F32), transposed, transposed],
        scratch_shapes=[pltpu.VMEM((T, HEAD_PAIR), BF16), pltpu.VMEM((T // tk, HEAD_PAIR, tk), BF16),
                        pltpu.VMEM((T // tk, HEAD_PAIR, tk), BF16)],
        compiler_params=_params(("arbitrary", "arbitrary")),
    )(proj, proj, proj, do, ltot, _in_hbm(after))


def _branch(act_bf16, w_ref):
    return jnp.concatenate([_mm(act_bf16, w_ref[e]) for e in range(w_ref.shape[0])], axis=1)


def _mix_specs(T, D, tm, wbp, w_out):
    gate_col = (POOL_WIDTH + 3 * SB_WIDTH) // D
    row = lambda i: (i, 0)
    return [
        pl.BlockSpec((tm, D), row),
        pl.BlockSpec((tm, POOL_WIDTH), row),
        pl.BlockSpec((tm, SB_WIDTH), row),
        pl.BlockSpec((tm, D), lambda i: (i, gate_col)),
        pl.BlockSpec((tm, D), lambda i: (i, gate_col + 1)),
        pl.BlockSpec(wbp.shape, lambda i: (0, 0, 0)),
        pl.BlockSpec(wbp.shape, lambda i: (0, 0, 0)),
        pl.BlockSpec(w_out.shape, lambda i: (0, 0)),
    ]


def _mix_fwd(h, p, o, proj, wbp, wba, w_out, *, tm, name):
    T, D = h.shape
    tm = min(tm, T)

    def body(h_ref, p_ref, o_ref, glp_ref, gls_ref, wbp_ref, wba_ref, wout_ref, hout_ref, m_ref):
        yp = _branch(p_ref[...].astype(BF16), wbp_ref)
        ys = _branch(o_ref[...].astype(BF16), wba_ref)
        m = (jax.nn.sigmoid(glp_ref[...]) * yp + jax.nn.sigmoid(gls_ref[...]) * ys).astype(BF16)
        m_ref[...] = m
        hout_ref[...] = h_ref[...] + _mm(m, wout_ref[...])

    row = lambda i: (i, 0)
    return _tc_call(
        body, name=name, grid=(T // tm,),
        in_specs=_mix_specs(T, D, tm, wbp, w_out),
        out_specs=[pl.BlockSpec((tm, D), row), pl.BlockSpec((tm, D), row)],
        out_shape=[jax.ShapeDtypeStruct((T, D), F32), jax.ShapeDtypeStruct((T, D), BF16)],
        compiler_params=_params(("arbitrary",)),
    )(h, p, o, proj, proj, wbp, wba, w_out)


def _mix_bwd(dh, p, o, proj, wbp, wba, w_out, after, *, tm, name):
    T, D = dh.shape
    tm = min(tm, T)
    bw = wbp.shape[2]

    def body(dh_ref, p_ref, o_ref, glp_ref, gls_ref, wbp_ref, wba_ref, wout_ref, after_ref,
             dyp_ref, dys_ref, dp_ref, do_ref, dgl_ref):
        dm = _mm_nt(dh_ref[...].astype(BF16), wout_ref[...])
        yp = _branch(p_ref[...].astype(BF16), wbp_ref)
        ys = _branch(o_ref[...].astype(BF16), wba_ref)
        gp = jax.nn.sigmoid(glp_ref[...])
        gs = jax.nn.sigmoid(gls_ref[...])
        dyp = (dm * gp).astype(BF16)
        dys = (dm * gs).astype(BF16)
        dyp_ref[...] = dyp
        dys_ref[...] = dys
        dgl_ref[:, :D] = (dm * yp * gp * (1.0 - gp)).astype(BF16)
        dgl_ref[:, D:] = (dm * ys * gs * (1.0 - gs)).astype(BF16)
        dp = jnp.zeros(dp_ref.shape, F32)
        do_ = jnp.zeros(do_ref.shape, F32)
        for e in range(wbp_ref.shape[0]):
            dp += _mm_nt(dyp[:, e * bw:(e + 1) * bw], wbp_ref[e])
            do_ += _mm_nt(dys[:, e * bw:(e + 1) * bw], wba_ref[e])
        dp_ref[...] = dp
        do_ref[...] = do_

    row = lambda i: (i, 0)
    return _tc_call(
        body, name=name, grid=(T // tm,),
        in_specs=_mix_specs(T, D, tm, wbp, w_out) + [AFTER],
        out_specs=[pl.BlockSpec((tm, D), row), pl.BlockSpec((tm, D), row), pl.BlockSpec((tm, POOL_WIDTH), row),
                   pl.BlockSpec((tm, SB_WIDTH), row), pl.BlockSpec((tm, 2 * D), row)],
        out_shape=[jax.ShapeDtypeStruct((T, D), BF16), jax.ShapeDtypeStruct((T, D), BF16),
                   jax.ShapeDtypeStruct((T, POOL_WIDTH), F32), jax.ShapeDtypeStruct((T, SB_WIDTH), F32),
                   jax.ShapeDtypeStruct((T, 2 * D), BF16)],
        compiler_params=_params(("arbitrary",)),
    )(dh, p, o, proj, proj, wbp, wba, w_out, _in_hbm(after))


def _adamw(w, g, m, v, *, name):
    R, C = w.shape
    tr = _row_tile(R, C)

    def body(w_ref, g_ref, m_ref, v_ref, d_ref, nm_ref, nv_ref):
        g_ = g_ref[...]
        m_ = ADAM_B1 * m_ref[...] + (1.0 - ADAM_B1) * g_
        v_ = ADAM_B2 * v_ref[...] + (1.0 - ADAM_B2) * (g_ * g_)
        m_hat = m_ / (1.0 - ADAM_B1 ** ADAM_STEP)
        v_hat = v_ / (1.0 - ADAM_B2 ** ADAM_STEP)
        d_ref[...] = -ADAM_LR * (m_hat / (jnp.sqrt(v_hat) + ADAM_EPS) + ADAM_WD * w_ref[...])
        nm_ref[...] = m_
        nv_ref[...] = v_

    spec = pl.BlockSpec((tr, C), lambda i: (i, 0))
    return _tc_call(
        body, name=name, grid=(R // tr,), in_specs=[spec] * 4, out_specs=[spec] * 3,
        out_shape=[jax.ShapeDtypeStruct((R, C), F32)] * 3,
        compiler_params=_params(("arbitrary",)),
    )(w, g, m, v)


def _position():
    return lax.axis_index("x"), lax.axis_index("y"), lax.axis_index("c")


def _all_gather(shards, *, name, collective_id):
    n = len(shards)
    n_copies = 9

    def body(*refs):
        ins, outs = refs[:n], refs[n:2 * n]
        send_sems, recv_sems, local_sems = refs[2 * n:]
        x, y, c = _position()
        me, sibling = (x, y, c), (x, y, 1 - c)
        x_nbr, y_nbr, diagonal = (1 - x, y, c), (x, 1 - y, c), (1 - x, 1 - y, c)
        other = lambda pos: (pos[0], pos[1], 1 - c)

        barrier = pltpu.get_barrier_semaphore()
        for peer in (sibling, x_nbr, y_nbr):
            pl.semaphore_signal(barrier, inc=1, device_id=peer, device_id_type=MESH)
        pl.semaphore_wait(barrier, 3)

        def block(a, pos, half=None):
            ref = outs[a].at[4 * pos[0] + 2 * pos[1] + pos[2]]
            rows = ref.shape[0] // 2
            return ref if half is None else ref.at[pl.ds(half * rows, rows)]

        def copy(a, k, pos, to, half=None, src=None):
            return pltpu.make_async_remote_copy(
                src_ref=block(a, pos, half) if src is None else src, dst_ref=block(a, pos, half),
                send_sem=send_sems.at[n_copies * a + k], recv_sem=recv_sems.at[n_copies * a + k],
                device_id=to, device_id_type=MESH)

        started = []
        for a in range(n):
            mine = pltpu.make_async_copy(ins[a], block(a, me), local_sems.at[a])
            mine.start()
            started.append(mine)
        sends = []
        for a in range(n):
            sends += [copy(a, 1, me, x_nbr, src=ins[a]), copy(a, 2, me, y_nbr, src=ins[a]),
                      copy(a, 0, me, sibling, src=ins[a])]
        for cp in sends:
            cp.start()

        def pass_on(copies):
            for cp in copies:
                cp.start()
                sends.append(cp)

        for a in range(n):
            copy(a, 1, x_nbr, me).wait_recv()
            pass_on([copy(a, 5, x_nbr, y_nbr, half=0), copy(a, 3, x_nbr, sibling)])
            copy(a, 2, y_nbr, me).wait_recv()
            pass_on([copy(a, 6, y_nbr, x_nbr, half=1), copy(a, 4, y_nbr, sibling)])
        for a in range(n):
            copy(a, 5, diagonal, me, half=0).wait_recv()
            pass_on([copy(a, 7, diagonal, sibling, half=0)])
            copy(a, 6, diagonal, me, half=1).wait_recv()
            pass_on([copy(a, 8, diagonal, sibling, half=1)])
        for a in range(n):
            copy(a, 0, sibling, me).wait_recv()
            copy(a, 3, other(x_nbr), me).wait_recv()
            copy(a, 4, other(y_nbr), me).wait_recv()
            copy(a, 7, other(diagonal), me, half=0).wait_recv()
            copy(a, 8, other(diagonal), me, half=1).wait_recv()
        for cp in sends:
            cp.wait_send()
        for cp in started:
            cp.wait()

    return pl.kernel(
        body, name=name,
        out_type=[jax.ShapeDtypeStruct((N_DEV,) + s.shape, s.dtype) for s in shards],
        mesh=plsc.ScalarSubcoreMesh(axis_name="sequencer", num_cores=1),
        scratch_types=[pltpu.SemaphoreType.DMA((n_copies * n,)), pltpu.SemaphoreType.DMA((n_copies * n,)),
                       pltpu.SemaphoreType.DMA((n,))],
        compiler_params=pltpu.CompilerParams(collective_id=collective_id),
    )(*shards)


def _chip_sums(grads, *, name):
    _, R, C = grads.shape
    rc = 128 if R % 128 == 0 else R

    def body(g_ref, partial, out_ref, mine, theirs, send_sems, recv_sems, local_sems):
        x, y, c = _position()
        my_chip = 2 * x + y

        def swap(s):
            return pltpu.make_async_remote_copy(
                src_ref=g_ref.at[2 * s + (1 - c)], dst_ref=theirs.at[s],
                send_sem=send_sems.at[s], recv_sem=recv_sems.at[s],
                device_id=(x, y, 1 - c), device_id_type=MESH)

        def load(s):
            return pltpu.make_async_copy(g_ref.at[2 * s + c], mine.at[s], local_sems.at[s])

        for s in range(4):
            swap(s).start()
            load(s).start()
        for s in range(4):
            load(s).wait()
            swap(s).wait_recv()

        def chip_sum(chip, rows):
            return mine[chip, rows, :].astype(F32) + theirs[chip, rows, :].astype(F32)

        for j in (1, 2, 3):
            @pl.loop(0, R // rc)
            def _(t):
                rows = pl.ds(pl.multiple_of(t * rc, rc), rc)
                partial[j - 1, rows, :] = chip_sum(my_chip ^ j, rows).astype(BF16)

        @pl.loop(0, R // rc)
        def _(t):
            rows = pl.ds(pl.multiple_of(t * rc, rc), rc)
            out_ref[rows, :] = chip_sum(my_chip, rows)

        for s in range(4):
            swap(s).wait_send()

    vmem = pl.BlockSpec(memory_space=pltpu.VMEM)
    return _tc_call(
        body, name=name,
        in_specs=[pl.BlockSpec(memory_space=pl.ANY)], out_specs=[vmem, vmem],
        out_shape=[jax.ShapeDtypeStruct((3, R, C), BF16), jax.ShapeDtypeStruct((R, C), F32)],
        scratch_shapes=[
            pltpu.VMEM((4, R, C), BF16), pltpu.VMEM((4, R, C), BF16),
            pltpu.SemaphoreType.DMA((4,)), pltpu.SemaphoreType.DMA((4,)), pltpu.SemaphoreType.DMA((4,)),
        ],
        compiler_params=_params(),
    )(grads)


def _cross_chips(partials, *, name, collective_id):
    n = len(partials)

    def body(*refs):
        ins, outs = refs[:n], refs[n:2 * n]
        send_sems, recv_sems = refs[2 * n:]
        x, y, c = _position()
        my_chip = 2 * x + y
        peers = [((my_chip ^ j) // 2, (my_chip ^ j) % 2, c) for j in (1, 2, 3)]

        barrier = pltpu.get_barrier_semaphore()
        for peer in peers:
            pl.semaphore_signal(barrier, inc=1, device_id=peer, device_id_type=MESH)
        pl.semaphore_wait(barrier, 3)

        copies = [
            pltpu.make_async_remote_copy(
                src_ref=ins[a].at[j], dst_ref=outs[a].at[j],
                send_sem=send_sems.at[3 * a + j], recv_sem=recv_sems.at[3 * a + j],
                device_id=peers[j], device_id_type=MESH)
            for a in range(n) for j in range(3)]
        for cp in copies:
            cp.start()
        for cp in copies:
            cp.wait_recv()
        for cp in copies:
            cp.wait_send()

    return pl.kernel(
        body, name=name,
        out_type=[jax.ShapeDtypeStruct(p.shape, p.dtype) for p in partials],
        mesh=plsc.ScalarSubcoreMesh(axis_name="sequencer", num_cores=1),
        scratch_types=[pltpu.SemaphoreType.DMA((3 * n,)), pltpu.SemaphoreType.DMA((3 * n,))],
        compiler_params=pltpu.CompilerParams(collective_id=collective_id),
    )(*partials)


def _cross_chips_and_gather(partial, slab, *, name, collective_id):
    def body(part_ref, slab_ref, landed_ref, slabs_ref, send_sems, recv_sems, local_sem):
        x, y, c = _position()
        me, my_chip = 4 * x + 2 * y + c, 2 * x + y
        others = [me ^ k for k in range(1, N_DEV)]
        ids = [(o // 4, (o // 2) % 2, o % 2) for o in others]

        barrier = pltpu.get_barrier_semaphore()
        for peer in ids:
            pl.semaphore_signal(barrier, inc=1, device_id=peer, device_id_type=MESH)
        pl.semaphore_wait(barrier, N_DEV - 1)

        mine = pltpu.make_async_copy(slab_ref, slabs_ref.at[me], local_sem)
        mine.start()
        sends = [
            pltpu.make_async_remote_copy(
                src_ref=part_ref.at[j], dst_ref=landed_ref.at[j], send_sem=send_sems.at[j], recv_sem=recv_sems.at[j],
                device_id=((my_chip ^ (j + 1)) // 2, (my_chip ^ (j + 1)) % 2, c), device_id_type=MESH)
            for j in range(3)]
        sends += [
            pltpu.make_async_remote_copy(
                src_ref=slab_ref, dst_ref=slabs_ref.at[me], send_sem=send_sems.at[3 + k], recv_sem=recv_sems.at[3 + k],
                device_id=ids[k], device_id_type=MESH)
            for k in range(N_DEV - 1)]
        arrivals = sends[:3] + [
            pltpu.make_async_remote_copy(
                src_ref=slab_ref, dst_ref=slabs_ref.at[others[k]], send_sem=send_sems.at[3 + k],
                recv_sem=recv_sems.at[3 + k], device_id=ids[k], device_id_type=MESH)
            for k in range(N_DEV - 1)]
        for cp in sends:
            cp.start()
        for cp in arrivals:
            cp.wait_recv()
        for cp in sends:
            cp.wait_send()
        mine.wait()

    n_sems = 3 + N_DEV - 1
    return pl.kernel(
        body, name=name,
        out_type=[jax.ShapeDtypeStruct(partial.shape, partial.dtype),
                  jax.ShapeDtypeStruct((N_DEV,) + slab.shape, slab.dtype)],
        mesh=plsc.ScalarSubcoreMesh(axis_name="sequencer", num_cores=1),
        scratch_types=[pltpu.SemaphoreType.DMA((n_sems,)), pltpu.SemaphoreType.DMA((n_sems,)), pltpu.SemaphoreType.DMA],
        compiler_params=pltpu.CompilerParams(collective_id=collective_id),
    )(partial, slab)


def _sum_devices(gathered, after, *, name):
    _, R, C = gathered.shape

    def body(in_ref, after_ref, out_ref):
        total = in_ref[0]
        for d in range(1, N_DEV):
            total = total + in_ref[d]
        out_ref[...] = total

    return _tc_call(
        body, name=name, grid=(1,),
        in_specs=[pl.BlockSpec((N_DEV, R, C), lambda i: (0, 0, 0)), AFTER],
        out_specs=pl.BlockSpec((R, C), lambda i: (0, 0)),
        out_shape=jax.ShapeDtypeStruct((R, C), F32),
        compiler_params=_params(("arbitrary",)),
    )(gathered, _in_hbm(after))


def _owner_sum(own, landed, after, *, name):
    R, C = own.shape
    tr = _row_tile(R, C)

    def body(own_ref, landed_ref, after_ref, out_ref):
        total = own_ref[...]
        for j in range(3):
            total = total + landed_ref[j].astype(F32)
        out_ref[...] = total

    return _tc_call(
        body, name=name, grid=(R // tr,),
        in_specs=[pl.BlockSpec((tr, C), lambda i: (i, 0)), pl.BlockSpec((3, tr, C), lambda i: (0, i, 0)), AFTER],
        out_specs=pl.BlockSpec((tr, C), lambda i: (i, 0)),
        out_shape=jax.ShapeDtypeStruct((R, C), F32),
        compiler_params=_params(("arbitrary",)),
    )(own, landed, _in_hbm(after))


def _local_step(x, target, norms, pool_w_group, pool_scale, wgu1, wd1, w_in, wbp, wba, w_out, wgu2, wd2, exchange):
    n1g, nmg, n2g, nfg = norms
    D = x.shape[1]
    gu1, hid1 = _ffn_up(x, n1g, wgu1, tm=1024, name="ffn1_up")
    h1 = _ffn_down(x, hid1, wd1, tm=512, name="ffn1_down")
    un, proj = _inproj_fwd(h1, nmg, w_in, tm=1024, name="inproj_fwd")
    p = _pool_fwd(proj, pool_w_group, pool_scale, name="pool_fwd")
    o, ltot = _attn_fwd(proj, name="attn_fwd")
    h2, m = _mix_fwd(h1, p, o, proj, wbp, wba, w_out, tm=256, name="mix_fwd")
    gu2, hid2 = _ffn_up(h2, n2g, wgu2, tm=1024, name="ffn2_up")
    h3 = _ffn_down(h2, hid2, wd2, tm=512, name="ffn2_down")
    dh3, df2, loss, d_nf = _loss_bwd(h3, target, nfg, tm=256, name="loss_bwd")

    d_wd2 = _wgrad_down(hid2, df2, tk=WGRAD_TOKENS, name="ffn2_wgrad_down")
    (g_wd2,), token = exchange("ffn2_down", [d_wd2.reshape(N_DEV, FF_SHARD_PAD, D)])
    dh2, d_n2, n2, dgu2 = _ffn_bwd(dh3, df2, h2, n2g, gu2, wgu2, wd2, token, tm=512, name="ffn2_bwd")
    d_wgu2 = _wgrad_gate_up(n2, dgu2, tk=WGRAD_TOKENS, name="ffn2_wgrad_gate_up")
    (g_wgu2,), token = exchange("ffn2_gate_up", [d_wgu2])

    dyp, dys, dp, do, dgl = _mix_bwd(dh2, p, o, proj, wbp, wba, w_out, token, tm=256, name="mix_bwd")
    d_wout = _wgrad_full(m, dh2, tk=WGRAD_TOKENS, name="wgrad_out")
    d_wbp = _wgrad_full(p, dyp, tk=WGRAD_TOKENS, name="wgrad_branch_pool", split_lanes=wbp.shape[2])
    d_wba = _wgrad_full(o, dys, tk=WGRAD_TOKENS, name="wgrad_branch_attn", split_lanes=wba.shape[2])
    (g_wbp, g_wba, g_wout), token = exchange("mix", [d_wbp, d_wba, d_wout.reshape(N_DEV, D // N_DEV, D)])
    dxp, d_wgroup, d_scale = _pool_bwd(dp, proj, pool_w_group, pool_scale, name="pool_bwd")
    dq, dkt, dvt = _attn_bwd(proj, do, ltot, token, name="attn_bwd")
    dk, dv = (t.transpose(0, 2, 1).reshape(dq.shape) for t in (dkt, dvt))
    dproj = jnp.concatenate([dxp.astype(BF16), dq.astype(BF16), dk.astype(BF16), dv.astype(BF16), dgl], axis=1)
    d_win = _wgrad_in(un, dproj, tk=WGRAD_TOKENS, name="wgrad_in")
    (g_win,), token_in = exchange("w_in", [d_win])
    dh1, df1, d_nm = _inproj_bwd(dproj, dh2, h1, nmg, w_in, tm=1024, name="inproj_bwd")
    d_wd1 = _wgrad_down(hid1, df1, tk=WGRAD_TOKENS, name="ffn1_wgrad_down")
    (g_wd1,), token_down = exchange("ffn1_down", [d_wd1.reshape(N_DEV, FF_SHARD_PAD, D)])
    token = (token_down[(0,) * token_down.ndim] + token_in[(0,) * token_in.ndim]).reshape(1, 1)

    dx, d_n1, n1, dgu1 = _ffn_bwd(dh1, df1, x, n1g, gu1, wgu1, wd1, token, tm=512, name="ffn1_bwd")
    d_wgu1 = _wgrad_gate_up(n1, dgu1, tk=WGRAD_TOKENS, name="ffn1_wgrad_gate_up")
    (g_wgu1, replicated), token = exchange("last", [d_wgu1, d_n1, d_nm, d_n2, d_nf, d_scale, d_wgroup, loss])

    sharded = (g_wgu1, g_wd1, g_win, g_wbp, g_wba, g_wout, g_wgu2, g_wd2)
    return dx, sharded, replicated, token


def _hidden_major(w):
    return jnp.swapaxes(w[0], 0, 1)


def _pad_gate_up(wt):
    d = wt.shape[1]
    wt = wt.astype(BF16).reshape(2, FF_SHARD, d)
    return jnp.pad(wt, ((0, 0), (0, FF_SHARD_PAD - FF_SHARD), (0, 0))).reshape(2 * FF_SHARD_PAD, d)


def _unpad_gate_up(gt):
    d = gt.shape[1]
    return gt.reshape(2, FF_SHARD_PAD, d)[:, :FF_SHARD].reshape(2 * FF_SHARD, d)


def _pad_down(w):
    return jnp.pad(w.astype(BF16), ((0, FF_SHARD_PAD - FF_SHARD), (0, 0)))


def kernel(x, ffn1_norm, ffn1_w_gate_up, ffn1_w_down, mix_norm, w_in, pool_w_group, pool_scale, w_branch_pool, w_branch_attn, w_out, ffn2_norm, ffn2_w_gate_up, ffn2_w_down, final_norm, loss_target, m_ffn1_norm, m_ffn1_w_gate_up, m_ffn1_w_down, m_mix_norm, m_w_in, m_pool_w_group, m_pool_scale, m_w_branch_pool, m_w_branch_attn, m_w_out, m_ffn2_norm, m_ffn2_w_gate_up, m_ffn2_w_down, m_final_norm, v_ffn1_norm, v_ffn1_w_gate_up, v_ffn1_w_down, v_mix_norm, v_w_in, v_pool_w_group, v_pool_scale, v_w_branch_pool, v_w_branch_attn, v_w_out, v_ffn2_norm, v_ffn2_w_gate_up, v_ffn2_w_down, v_final_norm):
    D = x.shape[-1]
    weights = dict(ffn1_norm=ffn1_norm, ffn1_w_gate_up=ffn1_w_gate_up, ffn1_w_down=ffn1_w_down, mix_norm=mix_norm,
                   w_in=w_in, pool_w_group=pool_w_group, pool_scale=pool_scale, w_branch_pool=w_branch_pool,
                   w_branch_attn=w_branch_attn, w_out=w_out, ffn2_norm=ffn2_norm, ffn2_w_gate_up=ffn2_w_gate_up,
                   ffn2_w_down=ffn2_w_down, final_norm=final_norm)
    first = dict(ffn1_norm=m_ffn1_norm, ffn1_w_gate_up=m_ffn1_w_gate_up, ffn1_w_down=m_ffn1_w_down,
                 mix_norm=m_mix_norm, w_in=m_w_in, pool_w_group=m_pool_w_group, pool_scale=m_pool_scale,
                 w_branch_pool=m_w_branch_pool, w_branch_attn=m_w_branch_attn, w_out=m_w_out,
                 ffn2_norm=m_ffn2_norm, ffn2_w_gate_up=m_ffn2_w_gate_up, ffn2_w_down=m_ffn2_w_down,
                 final_norm=m_final_norm)
    second = dict(ffn1_norm=v_ffn1_norm, ffn1_w_gate_up=v_ffn1_w_gate_up, ffn1_w_down=v_ffn1_w_down,
                  mix_norm=v_mix_norm, w_in=v_w_in, pool_w_group=v_pool_w_group, pool_scale=v_pool_scale,
                  w_branch_pool=v_w_branch_pool, w_branch_attn=v_w_branch_attn, w_out=v_w_out,
                  ffn2_norm=v_ffn2_norm, ffn2_w_gate_up=v_ffn2_w_gate_up, ffn2_w_down=v_ffn2_w_down,
                  final_norm=v_final_norm)
    order = list(weights)

    wgu1, = _all_gather([_pad_gate_up(_hidden_major(ffn1_w_gate_up))], name="all_gather_ffn1_gate_up", collective_id=0)
    wd1, = _all_gather([_pad_down(ffn1_w_down[0])], name="all_gather_ffn1_down", collective_id=10)
    win_g, = _all_gather([w_in[0].astype(BF16)], name="all_gather_w_in", collective_id=1)
    wbp_g, wba_g = _all_gather([w_branch_pool[0].astype(BF16), w_branch_attn[0].astype(BF16)],
                               name="all_gather_branches", collective_id=2)
    wout_g, = _all_gather([w_out[0].astype(BF16)], name="all_gather_w_out", collective_id=11)
    wgu2, wd2 = _all_gather([_pad_gate_up(_hidden_major(ffn2_w_gate_up)), _pad_down(ffn2_w_down[0])],
                            name="all_gather_ffn2", collective_id=3)
    wd1 = wd1.reshape(N_DEV * FF_SHARD_PAD, D)
    wd2 = wd2.reshape(N_DEV * FF_SHARD_PAD, D)
    wout_g = wout_g.reshape(D, D)

    cross_ids = {"ffn2_down": 4, "ffn2_gate_up": 5, "mix": 6, "ffn1_down": 7, "w_in": 8, "last": 9}
    small = ["ffn1_norm", "mix_norm", "ffn2_norm", "final_norm", "pool_scale", "pool_w_group"]

    def tile_rows(a):
        a = a.reshape(-1, 128)
        return jnp.pad(a, ((0, -a.shape[0] % 8), (0, 0)))

    def exchange(tag, group):
        if tag == "last":
            slab = jnp.concatenate([tile_rows(g) for g in group[1:-1]] + [jnp.broadcast_to(group[-1], (8, 128))], axis=0)
            partial, own = _chip_sums(group[0], name="chip_sums_last")
            landed, slabs = _cross_chips_and_gather(partial, slab, name="cross_chips_last", collective_id=cross_ids[tag])
            return [(own, landed), slabs], own
        sums = [_chip_sums(g, name=f"chip_sums_{tag}_{i}") for i, g in enumerate(group)]
        landed = _cross_chips([s[0] for s in sums], name="cross_chips_" + tag, collective_id=cross_ids[tag])
        token = sums[0][1] if len(sums) == 1 else sum(s[1][0, 0] for s in sums).reshape(1, 1)
        return [(s[1], l) for s, l in zip(sums, landed)], token

    norms = (ffn1_norm, mix_norm, ffn2_norm, final_norm.reshape(1, D))
    dx, sharded, slabs, last = _local_step(
        x[0], loss_target[0], norms, pool_w_group[0], pool_scale, wgu1, wd1, win_g, wbp_g, wba_g, wout_g, wgu2, wd2,
        exchange)
    names = ["ffn1_w_gate_up", "ffn1_w_down", "w_in", "w_branch_pool", "w_branch_attn", "w_out",
             "ffn2_w_gate_up", "ffn2_w_down"]
    handles = dict(zip(names, sharded))
    grads, delta, new_m, new_v = {}, {}, {}, {}
    after = last
    for k in ("ffn2_w_down", "ffn2_w_gate_up", "w_branch_pool", "w_branch_attn", "w_out", "w_in", "ffn1_w_down",
              "ffn1_w_gate_up"):
        g = _owner_sum(*handles[k], after, name="owner_sum_" + k)
        hidden_major = k.endswith("w_gate_up")
        g = _unpad_gate_up(g) if hidden_major else g[:weights[k].shape[1]]
        view = _hidden_major if hidden_major else (lambda a: a[0])
        back = (lambda a: jnp.swapaxes(a, 0, 1)[None]) if hidden_major else (lambda a: a[None])
        out = _adamw(view(weights[k]), g, view(first[k]), view(second[k]), name="adamw_" + k)
        after = out[0]
        grads[k] = back(g)
        delta[k], new_m[k], new_v[k] = (back(a) for a in out)

    rows = [weights[k].size // 128 for k in small]
    padded_rows = [-(-r // 8) * 8 for r in rows]
    starts = [sum(padded_rows[:i]) for i in range(len(rows) + 1)]
    total = _sum_devices(slabs, after, name="sum_replicated")
    loss_out = total[starts[-1], 0]
    small_w = jnp.concatenate([tile_rows(weights[k]) for k in small], axis=0)
    small_m = jnp.concatenate([tile_rows(first[k]) for k in small], axis=0)
    small_v = jnp.concatenate([tile_rows(second[k]) for k in small], axis=0)
    small_out = _adamw(small_w, total[:starts[-1]], small_m, small_v, name="adamw_replicated")
    for name_, start, n_rows in zip(small, starts, rows):
        shape = weights[name_].shape
        grads[name_] = total[start:start + n_rows].reshape(shape)
        delta[name_], new_m[name_], new_v[name_] = (a[start:start + n_rows].reshape(shape) for a in small_out)

    return (loss_out, dx[None], *[grads[k] for k in order], *[delta[k] for k in order],
            *[new_m[k] for k in order], *[new_v[k] for k in order])
```

```python
import functools

import jax
import jax.numpy as jnp
from jax import lax
from jax.experimental import pallas as pl
from jax.experimental.pallas import tpu as pltpu
from jax.experimental.pallas import tpu_sc as plsc

F32 = jnp.float32
BF16 = jnp.bfloat16
MESH = pl.DeviceIdType.MESH

RMS_EPS = 1e-6
N_DEV = 8
N_HEADS = 8
HEAD_DIM = 64
HEAD_PAIR = 2 * HEAD_DIM
POOL_WINDOWS = (2, 4, 8, 16)
POOL_GROUP = 128
POOL_WIDTH = 512
SB_WIDTH = 512
FF_SHARD = 352
FF_SHARD_PAD = 384
ATTN_K_BLOCK = 256
ATTN_Q_BLOCK_FWD = 512
ATTN_Q_BLOCK_BWD = 256
ATTN_SCALE = 0.125

ADAM_LR = 0.001
ADAM_B1 = 0.9
ADAM_B2 = 0.999
ADAM_EPS = 1e-08
ADAM_WD = 0.01
ADAM_STEP = 10

VMEM_LIMIT = 48 << 20
WGRAD_TOKENS = 2048


def _params(dims=None):
    return pltpu.CompilerParams(dimension_semantics=dims, vmem_limit_bytes=VMEM_LIMIT)


def _mm(a, b):
    return jnp.dot(a, b, preferred_element_type=F32)


def _mm_nt(a, b):
    return lax.dot_general(a, b, (((1,), (1,)), ((), ())), preferred_element_type=F32)


def _mm_tn(a, b):
    return lax.dot_general(a, b, (((0,), (0,)), ((), ())), preferred_element_type=F32)


def _row_tile(rows, cols):
    limit = max(8, (512 * 1024) // cols)
    return max(t for t in range(8, rows + 1, 8) if rows % t == 0 and (t <= limit or t == 8))


def _rstd(xf):
    return lax.rsqrt(jnp.mean(xf * xf, axis=-1, keepdims=True) + RMS_EPS)


def _rms_bwd(xf, gain, dn):
    r = _rstd(xf)
    xh = xf * r
    dgain = jnp.sum(dn * xh, axis=0, keepdims=True)
    dxh = dn * gain
    dx = r * (dxh - xh * jnp.mean(dxh * xh, axis=-1, keepdims=True))
    return dx, dgain


def _ffn_up(x, gain, wgu, *, tm, name):
    T, D = x.shape
    tm = min(tm, T)
    nb, bw = wgu.shape[0] // 2, wgu.shape[1]

    def body(x_ref, gain_ref, wg_ref, wu_ref, gu_ref, hid_ref, n_scr):
        @pl.when(pl.program_id(1) == 0)
        def _():
            xf = x_ref[...]
            n_scr[...] = (xf * _rstd(xf) * gain_ref[...]).astype(BF16)

        halves = (pl.ds(0, tm // 2), pl.ds(tm // 2, tm // 2))
        wg, wu = wg_ref[...], wu_ref[...]
        gus = [(_mm_nt(n_scr[rows, :], wg), _mm_nt(n_scr[rows, :], wu)) for rows in halves]
        for rows, (g, u) in zip(halves, gus):
            gu_ref[0, rows, :] = g.astype(BF16)
            gu_ref[1, rows, :] = u.astype(BF16)
            hid_ref[rows, :] = (g * jax.nn.sigmoid(g) * u).astype(BF16)

    return pl.pallas_call(
        body, name=name, grid=(T // tm, nb),
        in_specs=[
            pl.BlockSpec((tm, D), lambda i, j: (i, 0)),
            pl.BlockSpec((1, D), lambda i, j: (0, 0)),
            pl.BlockSpec((None, bw, D), lambda i, j: (j, 0, 0)),
            pl.BlockSpec((None, bw, D), lambda i, j: (j + nb, 0, 0)),
        ],
        out_specs=[
            pl.BlockSpec((2, tm, bw), lambda i, j: (0, i, j)),
            pl.BlockSpec((tm, bw), lambda i, j: (i, j)),
        ],
        out_shape=[jax.ShapeDtypeStruct((2, T, nb * bw), BF16), jax.ShapeDtypeStruct((T, nb * bw), BF16)],
        scratch_shapes=[pltpu.VMEM((tm, D), BF16)],
        compiler_params=_params(("arbitrary", "arbitrary")),
    )(x, gain, wgu, wgu)


def _ffn_down(x, hid, wd, *, tm, name):
    T, D = x.shape
    tm = min(tm, T)
    F = hid.shape[1]

    def body(x_ref, hid_ref, wd_ref, h_ref):
        h_ref[...] = x_ref[...] + 0.5 * _mm(hid_ref[...], wd_ref[...])

    return pl.pallas_call(
        body, name=name, grid=(T // tm,),
        in_specs=[
            pl.BlockSpec((tm, D), lambda i: (i, 0)),
            pl.BlockSpec((tm, F), lambda i: (i, 0)),
            pl.BlockSpec((F, D), lambda i: (0, 0)),
        ],
        out_specs=pl.BlockSpec((tm, D), lambda i: (i, 0)),
        out_shape=jax.ShapeDtypeStruct((T, D), F32),
        compiler_params=_params(("arbitrary",)),
    )(x, hid, wd)


AFTER = pl.BlockSpec(memory_space=pltpu.HBM)


def _in_hbm(token):
    return pltpu.with_memory_space_constraint(token, pltpu.HBM)


def _ffn_bwd(dh, df, x, gain, gu, wgu, wd, after, *, tm, name):
    T, D = x.shape
    tm = min(tm, T)
    nb, bw = wgu.shape[0] // 2, wgu.shape[1]

    def body(dh_ref, df_ref, x_ref, gain_ref, gu_ref, wg_ref, wu_ref, wd_ref, after_ref,
             dx_ref, dgain_ref, n_ref, dgu_ref, dn_acc):
        i, j = pl.program_id(0), pl.program_id(1)

        @pl.when(j == 0)
        def _():
            xf = x_ref[...]
            n_ref[...] = (xf * _rstd(xf) * gain_ref[...]).astype(BF16)
            dn_acc[...] = jnp.zeros_like(dn_acc)

        @pl.when((i == 0) & (j == 0))
        def _():
            dgain_ref[...] = jnp.zeros_like(dgain_ref)

        halves = (pl.ds(0, tm // 2), pl.ds(tm // 2, tm // 2))
        wd, wg, wu = wd_ref[...], wg_ref[...], wu_ref[...]
        dhids = [_mm_nt(df_ref[rows, :], wd) for rows in halves]
        for rows, dhid in zip(halves, dhids):
            g = gu_ref[0, rows, :].astype(F32)
            u = gu_ref[1, rows, :].astype(F32)
            s = jax.nn.sigmoid(g)
            silu = g * s
            dg = (dhid * u * (s * (1.0 + g * (1.0 - s)))).astype(BF16)
            du = (dhid * silu).astype(BF16)
            dgu_ref[0, rows, :] = dg
            dgu_ref[1, rows, :] = du
            dn_acc[rows, :] += _mm(dg, wg) + _mm(du, wu)

        @pl.when(j == nb - 1)
        def _():
            dx, dgain = _rms_bwd(x_ref[...], gain_ref[...], dn_acc[...])
            dx_ref[...] = dh_ref[...] + dx
            dgain_ref[...] += dgain

    row = lambda i, j: (i, 0)
    return pl.pallas_call(
        body, name=name, grid=(T // tm, nb),
        in_specs=[
            pl.BlockSpec((tm, D), row),
            pl.BlockSpec((tm, D), row),
            pl.BlockSpec((tm, D), row),
            pl.BlockSpec((1, D), lambda i, j: (0, 0)),
            pl.BlockSpec((2, tm, bw), lambda i, j: (0, i, j)),
            pl.BlockSpec((None, bw, D), lambda i, j: (j, 0, 0)),
            pl.BlockSpec((None, bw, D), lambda i, j: (j + nb, 0, 0)),
            pl.BlockSpec((bw, D), lambda i, j: (j, 0)),
            AFTER,
        ],
        out_specs=[
            pl.BlockSpec((tm, D), row),
            pl.BlockSpec((1, D), lambda i, j: (0, 0)),
            pl.BlockSpec((tm, D), row),
            pl.BlockSpec((2, tm, bw), lambda i, j: (0, i, j)),
        ],
        out_shape=[
            jax.ShapeDtypeStruct((T, D), F32),
            jax.ShapeDtypeStruct((1, D), F32),
            jax.ShapeDtypeStruct((T, D), BF16),
            jax.ShapeDtypeStruct((2, T, nb * bw), BF16),
        ],
        scratch_shapes=[pltpu.VMEM((tm, D), F32)],
        compiler_params=_params(("arbitrary", "arbitrary")),
    )(dh, df, x, gain, gu, wgu, wgu, wd, _in_hbm(after))


def _wgrad(a, b, *, grid, a_spec, b_spec, out_spec, out_shape, acc_shape, name):
    nk = grid[2]

    def body(a_ref, b_ref, o_ref, acc):
        k = pl.program_id(2)

        @pl.when(k == 0)
        def _():
            acc[...] = jnp.zeros_like(acc)

        acc[...] += _mm_tn(a_ref[...].astype(BF16), b_ref[...].astype(BF16))

        @pl.when(k == nk - 1)
        def _():
            o_ref[...] = acc[...].astype(o_ref.dtype)

    return pl.pallas_call(
        body, name=name, grid=grid, in_specs=[a_spec, b_spec], out_specs=out_spec,
        out_shape=jax.ShapeDtypeStruct(out_shape, BF16),
        scratch_shapes=[pltpu.VMEM(acc_shape, F32)],
        compiler_params=_params(("arbitrary", "arbitrary", "arbitrary")),
    )(a, b)


def _wgrad_gate_up(n, dgu, *, tk, name):
    T, D = n.shape
    tk = min(tk, T)
    bw = FF_SHARD_PAD * 2
    nb = dgu.shape[2] // bw
    return _wgrad(
        dgu, n, grid=(2 * nb, 1, T // tk), name=name,
        a_spec=pl.BlockSpec((None, tk, bw), lambda m, c, k: (m // nb, k, m % nb)),
        b_spec=pl.BlockSpec((tk, D), lambda m, c, k: (k, 0)),
        out_spec=pl.BlockSpec((None, bw, D), lambda m, c, k: (m, 0, 0)),
        out_shape=(2 * nb, bw, D), acc_shape=(bw, D))


def _wgrad_down(hid, df, *, tk, name):
    T, D = df.shape
    tk = min(tk, T)
    bw = FF_SHARD_PAD * 2
    nb = hid.shape[1] // bw
    return _wgrad(
        hid, df, grid=(nb, 1, T // tk), name=name,
        a_spec=pl.BlockSpec((tk, bw), lambda m, c, k: (k, m)),
        b_spec=pl.BlockSpec((tk, D), lambda m, c, k: (k, 0)),
        out_spec=pl.BlockSpec((bw, D), lambda m, c, k: (m, 0)),
        out_shape=(nb * bw, D), acc_shape=(bw, D))


def _wgrad_in(dproj, un, *, tk, name):
    T, D = un.shape
    tk = min(tk, T)
    bw = dproj.shape[1] // N_DEV
    return _wgrad(
        dproj, un, grid=(N_DEV, 1, T // tk), name=name,
        a_spec=pl.BlockSpec((tk, bw), lambda m, c, k: (k, m)),
        b_spec=pl.BlockSpec((tk, D), lambda m, c, k: (k, 0)),
        out_spec=pl.BlockSpec((None, bw, D), lambda m, c, k: (m, 0, 0)),
        out_shape=(N_DEV, bw, D), acc_shape=(bw, D))


def _wgrad_full(a, b, *, tk, name):
    T, M = a.shape
    tk = min(tk, T)
    N = b.shape[1]
    return _wgrad(
        a, b, grid=(1, 1, T // tk), name=name,
        a_spec=pl.BlockSpec((tk, M), lambda m, c, k: (k, 0)),
        b_spec=pl.BlockSpec((tk, N), lambda m, c, k: (k, 0)),
        out_spec=pl.BlockSpec((M, N), lambda m, c, k: (0, 0)), out_shape=(M, N), acc_shape=(M, N))


def _loss_bwd(h, target, gain, *, tm, name):
    T, D = h.shape
    tm = min(tm, T)

    def body(h_ref, t_ref, gain_ref, dh_ref, df_ref, loss_ref, dgain_ref):
        @pl.when(pl.program_id(0) == 0)
        def _():
            loss_ref[...] = jnp.zeros_like(loss_ref)
            dgain_ref[...] = jnp.zeros_like(dgain_ref)

        xf = h_ref[...]
        gain = gain_ref[...]
        err = xf * _rstd(xf) * gain - t_ref[...]
        loss_ref[...] += 0.5 * jnp.sum(jnp.mean(err * err, axis=-1, keepdims=True), axis=0, keepdims=True)
        dx, dgain = _rms_bwd(xf, gain, err * (1.0 / D))
        dh_ref[...] = dx
        df_ref[...] = (0.5 * dx).astype(BF16)
        dgain_ref[...] += dgain

    row = lambda i: (i, 0)
    fixed = lambda i: (0, 0)
    return pl.pallas_call(
        body, name=name, grid=(T // tm,),
        in_specs=[pl.BlockSpec((tm, D), row), pl.BlockSpec((tm, D), row), pl.BlockSpec((1, D), fixed)],
        out_specs=[pl.BlockSpec((tm, D), row), pl.BlockSpec((tm, D), row), pl.BlockSpec((1, 128), fixed),
                   pl.BlockSpec((1, D), fixed)],
        out_shape=[jax.ShapeDtypeStruct((T, D), F32), jax.ShapeDtypeStruct((T, D), BF16),
                   jax.ShapeDtypeStruct((1, 128), F32), jax.ShapeDtypeStruct((1, D), F32)],
        compiler_params=_params(("arbitrary",)),
    )(h, target, gain)


def _inproj_fwd(h, gain, w_in_t, *, tm, name):
    T, D = h.shape
    tm = min(tm, T)
    bn = D
    nb = w_in_t.shape[0] // bn

    def body(h_ref, gain_ref, wt_ref, un_ref, proj_ref):
        @pl.when(pl.program_id(1) == 0)
        def _():
            xf = h_ref[...]
            un_ref[...] = (xf * _rstd(xf) * gain_ref[...]).astype(BF16)

        proj_ref[...] = _mm_nt(un_ref[...], wt_ref[...])

    return pl.pallas_call(
        body, name=name, grid=(T // tm, nb),
        in_specs=[
            pl.BlockSpec((tm, D), lambda i, j: (i, 0)),
            pl.BlockSpec((1, D), lambda i, j: (0, 0)),
            pl.BlockSpec((bn, D), lambda i, j: (j, 0)),
        ],
        out_specs=[pl.BlockSpec((tm, D), lambda i, j: (i, 0)), pl.BlockSpec((tm, bn), lambda i, j: (i, j))],
        out_shape=[jax.ShapeDtypeStruct((T, D), BF16), jax.ShapeDtypeStruct((T, nb * bn), F32)],
        compiler_params=_params(("arbitrary", "arbitrary")),
    )(h, gain, w_in_t)


def _inproj_bwd(dproj, dh, h, gain, w_in_t, *, tm, name):
    T, D = h.shape
    tm = min(tm, T)
    width = w_in_t.shape[0]

    def body(dp_ref, dh_ref, h_ref, gain_ref, wt_ref, dx_ref, df_ref, dgain_ref):
        @pl.when(pl.program_id(0) == 0)
        def _():
            dgain_ref[...] = jnp.zeros_like(dgain_ref)

        dx, dgain = _rms_bwd(h_ref[...], gain_ref[...], _mm(dp_ref[...], wt_ref[...]))
        dh_in = dh_ref[...] + dx
        dx_ref[...] = dh_in
        df_ref[...] = (0.5 * dh_in).astype(BF16)
        dgain_ref[...] += dgain

    row = lambda i: (i, 0)
    fixed = lambda i: (0, 0)
    return pl.pallas_call(
        body, name=name, grid=(T // tm,),
        in_specs=[
            pl.BlockSpec((tm, width), row),
            pl.BlockSpec((tm, D), row),
            pl.BlockSpec((tm, D), row),
            pl.BlockSpec((1, D), fixed),
            pl.BlockSpec((width, D), fixed),
        ],
        out_specs=[pl.BlockSpec((tm, D), row), pl.BlockSpec((tm, D), row), pl.BlockSpec((1, D), fixed)],
        out_shape=[jax.ShapeDtypeStruct((T, D), F32), jax.ShapeDtypeStruct((T, D), BF16),
                   jax.ShapeDtypeStruct((1, D), F32)],
        compiler_params=_params(("arbitrary",)),
    )(dproj, dh, h, gain, w_in_t)


def _window_sum(x, row, doublings, *, backward):
    T = x.shape[0]
    s = x
    for k in range(doublings):
        sh = 1 << k
        if backward:
            s = s + jnp.where(row < T - sh, pltpu.roll(s, T - sh, 0), 0.0)
        else:
            s = s + jnp.where(row >= sh, pltpu.roll(s, sh, 0), 0.0)
    return s


def _pool_fwd(proj, w_group, scale, *, name):
    T = proj.shape[0]

    def body(xp_ref, w_ref, scale_ref, p_ref):
        row = lax.broadcasted_iota(jnp.int32, (T, POOL_GROUP), 0)
        for gi, window in enumerate(POOL_WINDOWS):
            cols = slice(gi * POOL_GROUP, (gi + 1) * POOL_GROUP)
            x = xp_ref[:, cols]
            inv_count = 1.0 / jnp.minimum(row + 1, window).astype(F32)
            yc = _window_sum(x, row, gi + 1, backward=False) * inv_count - x
            pre = _mm(yc.astype(BF16), w_ref[gi].astype(BF16))
            p_ref[:, cols] = pre * scale_ref[:, cols]

    return pl.pallas_call(
        body, name=name, grid=(1,),
        in_specs=[
            pl.BlockSpec((T, POOL_WIDTH), lambda i: (0, 0)),
            pl.BlockSpec(w_group.shape, lambda i: (0, 0, 0)),
            pl.BlockSpec((1, POOL_WIDTH), lambda i: (0, 0)),
        ],
        out_specs=pl.BlockSpec((T, POOL_WIDTH), lambda i: (0, 0)),
        out_shape=jax.ShapeDtypeStruct((T, POOL_WIDTH), F32),
        compiler_params=_params(("arbitrary",)),
    )(proj, w_group, scale)


def _pool_bwd(dp, proj, w_group, scale, *, name):
    T = proj.shape[0]

    def body(dp_ref, xp_ref, w_ref, scale_ref, dxp_ref, dw_ref, dscale_ref):
        row = lax.broadcasted_iota(jnp.int32, (T, POOL_GROUP), 0)
        for gi, window in enumerate(POOL_WINDOWS):
            cols = slice(gi * POOL_GROUP, (gi + 1) * POOL_GROUP)
            x = xp_ref[:, cols]
            inv_count = 1.0 / jnp.minimum(row + 1, window).astype(F32)
            yc = (_window_sum(x, row, gi + 1, backward=False) * inv_count - x).astype(BF16)
            w = w_ref[gi].astype(BF16)
            pre = _mm(yc, w)
            dpg = dp_ref[:, cols]
            dscale_ref[:, cols] = jnp.sum(dpg * pre, axis=0, keepdims=True)
            dpre = (dpg * scale_ref[:, cols]).astype(BF16)
            dw_ref[gi] = _mm_tn(yc, dpre)
            dyc = _mm_nt(dpre, w)
            dxp_ref[:, cols] = _window_sum(dyc * inv_count, row, gi + 1, backward=True) - dyc

    return pl.pallas_call(
        body, name=name, grid=(1,),
        in_specs=[
            pl.BlockSpec((T, POOL_WIDTH), lambda i: (0, 0)),
            pl.BlockSpec((T, POOL_WIDTH), lambda i: (0, 0)),
            pl.BlockSpec(w_group.shape, lambda i: (0, 0, 0)),
            pl.BlockSpec((1, POOL_WIDTH), lambda i: (0, 0)),
        ],
        out_specs=[
            pl.BlockSpec((T, POOL_WIDTH), lambda i: (0, 0)),
            pl.BlockSpec(w_group.shape, lambda i: (0, 0, 0)),
            pl.BlockSpec((1, POOL_WIDTH), lambda i: (0, 0)),
        ],
        out_shape=[jax.ShapeDtypeStruct((T, POOL_WIDTH), F32), jax.ShapeDtypeStruct(w_group.shape, F32),
                   jax.ShapeDtypeStruct((1, POOL_WIDTH), F32)],
        compiler_params=_params(("arbitrary",)),
    )(dp, proj, w_group, scale)


ATTN_STRIP = 32


def _log_sigmoids(z):
    lb = jnp.minimum(z, 0.0) - jnp.log(1.0 + jnp.exp(-jnp.abs(z)))
    return lb, lb - z


def _transposed_blocks(x_ref, blocks_scr, tq):
    for b in range(blocks_scr.shape[0]):
        blocks_scr[b] = x_ref[b * tq:(b + 1) * tq, :].T.astype(BF16)


def _split_bf16(x):
    hi = x.astype(BF16)
    return hi, (x - hi.astype(F32)).astype(BF16)


def _strips(n):
    return [slice(i, i + ATTN_STRIP) for i in range(0, n, ATTN_STRIP)]


def _rows(parts):
    return jnp.concatenate(parts, axis=0)


def _attn_specs(T, tq):
    q_col = POOL_WIDTH // HEAD_PAIR
    k_col = q_col + SB_WIDTH // HEAD_PAIR
    v_col = k_col + SB_WIDTH // HEAD_PAIR
    return [
        pl.BlockSpec((tq, HEAD_PAIR), lambda p, i: (i, q_col + p)),
        pl.BlockSpec((T, HEAD_PAIR), lambda p, i: (0, k_col + p)),
        pl.BlockSpec((T, HEAD_PAIR), lambda p, i: (0, v_col + p)),
    ]


def _attn_fwd(proj, *, name):
    T = proj.shape[0]
    tk = min(ATTN_K_BLOCK, T)
    tq = min(ATTN_Q_BLOCK_FWD, T)
    diagonal_blocks = tq // tk

    def body(q_ref, k_ref, v_ref, o_ref, lt_ref, kt_scr, vb_scr):
        qi = pl.program_id(1)

        @pl.when(qi == 0)
        def _():
            _transposed_blocks(k_ref, kt_scr, tk)
            vb_scr[...] = v_ref[...].astype(BF16)

        head0 = lax.broadcasted_iota(jnp.int32, (tq, HEAD_PAIR), 1) < HEAD_DIM
        q = q_ref[...] * ATTN_SCALE
        qs = (jnp.where(head0, q, 0.0).astype(BF16), jnp.where(head0, 0.0, q).astype(BF16))
        r = lax.broadcasted_iota(jnp.int32, (tq, tk), 0)
        c = lax.broadcasted_iota(jnp.int32, (tq, tk), 1)
        later = (r[:tk] > c[:tk]).astype(BF16)
        later2 = _rows([later, later])
        causal = lambda d: (lambda rows: c[rows] + d * tk < r[rows])
        strips = _strips(tq)

        def log_terms(z, valid):
            lbs, his, los, sums = [], [], [], []
            for rows in strips:
                lb, lm = _log_sigmoids(z[rows])
                if valid is not None:
                    lm = jnp.where(valid(rows), lm, 0.0)
                hi, lo = _split_bf16(lm)
                lbs.append(lb)
                his.append(hi)
                los.append(lo)
                sums.append(jnp.sum(lm, axis=1, keepdims=True))
            return lbs, jnp.concatenate([_rows(his), _rows(los)], axis=1), _rows(sums)

        def weights(lbs, run, after, valid):
            parts = []
            for rows, lb in zip(strips, lbs):
                a = jnp.exp(lb + run[rows] + after[rows])
                if valid is not None:
                    a = jnp.where(valid(rows), a, 0.0)
                parts.append(a.astype(BF16))
            return _rows(parts)

        def block(kj, carry, valid):
            kt = kt_scr[kj]
            vb = vb_scr[pl.ds(pl.multiple_of(kj * tk, tk), tk), :]
            run0, o0, run1, o1 = carry
            z0 = _mm(qs[0], kt)
            z1 = _mm(qs[1], kt)
            lbs0, split0, sums0 = log_terms(z0, valid)
            after0 = _mm(split0, later2)
            lbs1, split1, sums1 = log_terms(z1, valid)
            after1 = _mm(split1, later2)
            o0 = o0 + _mm(weights(lbs0, run0, after0, valid), vb)
            o1 = o1 + _mm(weights(lbs1, run1, after1, valid), vb)
            return run0 + sums0, o0, run1 + sums1, o1

        zero = (jnp.zeros((tq, 1), F32), jnp.zeros((tq, HEAD_PAIR), F32))
        first = diagonal_blocks * qi
        carry = zero + zero
        for d in reversed(range(diagonal_blocks)):
            carry = block(first + d, carry, causal(d))
        carry = lax.fori_loop(0, first, lambda it, cr: block(first - 1 - it, cr, None), carry)
        o_ref[...] = jnp.where(head0, carry[1], carry[3])
        lt_ref[...] = jnp.where(head0, carry[0], carry[2])

    out_spec = pl.BlockSpec((tq, HEAD_PAIR), lambda p, i: (i, p))
    return pl.pallas_call(
        body, name=name, grid=(N_HEADS // 2, T // tq),
        in_specs=_attn_specs(T, tq), out_specs=[out_spec, out_spec],
        out_shape=[jax.ShapeDtypeStruct((T, SB_WIDTH), F32), jax.ShapeDtypeStruct((T, SB_WIDTH), F32)],
        scratch_shapes=[pltpu.VMEM((T // tk, HEAD_PAIR, tk), BF16), pltpu.VMEM((T, HEAD_PAIR), BF16)],
        compiler_params=_params(("arbitrary", "arbitrary")),
    )(proj, proj, proj)


def _attn_bwd(proj, do, ltot, after, *, name):
    T = proj.shape[0]
    tk = min(ATTN_K_BLOCK, T)
    tq = min(ATTN_Q_BLOCK_BWD, T)
    diagonal_blocks = tq // tk

    def body(q_ref, k_ref, v_ref, do_ref, lt_ref, after_ref, dq_ref, dkt_ref, dvt_ref, kb_scr, kt_scr, vt_scr):
        qi = pl.program_id(1)

        @pl.when(qi == 0)
        def _():
            kb_scr[...] = k_ref[...].astype(BF16)
            _transposed_blocks(k_ref, kt_scr, tk)
            _transposed_blocks(v_ref, vt_scr, tk)
            dkt_ref[...] = jnp.zeros_like(dkt_ref)
            dvt_ref[...] = jnp.zeros_like(dvt_ref)

        head0 = lax.broadcasted_iota(jnp.int32, (tq, HEAD_PAIR), 1) < HEAD_DIM
        q, do_, lt = q_ref[...] * ATTN_SCALE, do_ref[...], lt_ref[...]
        qs = (jnp.where(head0, q, 0.0).astype(BF16), jnp.where(head0, 0.0, q).astype(BF16))
        q_heads = (jnp.where(head0, q, 0.0), jnp.where(head0, 0.0, q))
        do_heads = (jnp.where(head0, do_, 0.0), jnp.where(head0, 0.0, do_))
        dos = tuple(d.astype(BF16) for d in do_heads)
        qts = tuple(x.T.astype(BF16) for x in q_heads)
        dots = tuple(d.T.astype(BF16) for d in do_heads)
        lts = (jnp.max(jnp.where(head0, lt, -jnp.inf), axis=1, keepdims=True),
               jnp.max(jnp.where(head0, -jnp.inf, lt), axis=1, keepdims=True))
        r = lax.broadcasted_iota(jnp.int32, (tq, tk), 0)
        c = lax.broadcasted_iota(jnp.int32, (tq, tk), 1)
        upto = (r[:tk] <= c[:tk]).astype(BF16)
        before = (r[:tk] < c[:tk]).astype(BF16)
        upto2, before2 = _rows([upto, upto]), _rows([before, before])
        causal = lambda d: (lambda rows: c[rows] + d * tk < r[rows])
        strips = _strips(tq)

        def log_terms(z, valid):
            lbs, his, los, sums = [], [], [], []
            for rows in strips:
                lb, lm = _log_sigmoids(z[rows])
                if valid is not None:
                    lm = jnp.where(valid(rows), lm, 0.0)
                hi, lo = _split_bf16(lm)
                lbs.append(lb)
                his.append(hi)
                los.append(lo)
                sums.append(jnp.sum(lm, axis=1, keepdims=True))
            return lbs, jnp.concatenate([_rows(his), _rows(los)], axis=1), _rows(sums)

        def weights(lbs, rest, lm_upto, da, valid):
            a_parts, es, his, los, sums = [], [], [], [], []
            for rows, lb in zip(strips, lbs):
                a = jnp.exp(lb + (rest[rows] - lm_upto[rows]))
                if valid is not None:
                    a = jnp.where(valid(rows), a, 0.0)
                e = da[rows] * a
                hi, lo = _split_bf16(e)
                a_parts.append(a.astype(BF16))
                es.append(e)
                his.append(hi)
                los.append(lo)
                sums.append(jnp.sum(e, axis=1, keepdims=True))
            return _rows(a_parts), es, jnp.concatenate([_rows(his), _rows(los)], axis=1), _rows(sums)

        def score_grads(lbs, es, run_e, e_before, valid):
            parts = []
            for rows, lb, e in zip(strips, lbs, es):
                beta = jnp.exp(lb)
                dz = e * (1.0 - beta) - (run_e[rows] + e_before[rows]) * beta
                if valid is not None:
                    dz = jnp.where(valid(rows), dz, 0.0)
                parts.append(dz.astype(BF16))
            return _rows(parts)

        def block(kj, carry, valid):
            off = pl.multiple_of(kj * tk, tk)
            kb, kt, vt = kb_scr[pl.ds(off, tk), :], kt_scr[kj], vt_scr[kj]
            run_lm0, run_e0, dq0, run_lm1, run_e1, dq1 = carry
            z0, da0 = _mm(qs[0], kt), _mm(dos[0], vt)
            z1, da1 = _mm(qs[1], kt), _mm(dos[1], vt)
            lbs0, split0, lm_sums0 = log_terms(z0, valid)
            lm_upto0 = _mm(split0, upto2)
            lbs1, split1, lm_sums1 = log_terms(z1, valid)
            lm_upto1 = _mm(split1, upto2)
            a0, es0, split0, e_sums0 = weights(lbs0, lts[0] - run_lm0, lm_upto0, da0, valid)
            e_before0 = _mm(split0, before2)
            a1, es1, split1, e_sums1 = weights(lbs1, lts[1] - run_lm1, lm_upto1, da1, valid)
            e_before1 = _mm(split1, before2)
            dz0 = score_grads(lbs0, es0, run_e0, e_before0, valid)
            dkt_blk = _mm(qts[0], dz0)
            dvt_blk = _mm(dots[0], a0)
            dq0 = dq0 + _mm(dz0, kb)
            dz1 = score_grads(lbs1, es1, run_e1, e_before1, valid)
            dkt_ref[kj] += dkt_blk + _mm(qts[1], dz1)
            dvt_ref[kj] += dvt_blk + _mm(dots[1], a1)
            dq1 = dq1 + _mm(dz1, kb)
            return run_lm0 + lm_sums0, run_e0 + e_sums0, dq0, run_lm1 + lm_sums1, run_e1 + e_sums1, dq1

        zero = (jnp.zeros((tq, 1), F32), jnp.zeros((tq, 1), F32), jnp.zeros((tq, HEAD_PAIR), F32))
        first = diagonal_blocks * qi
        carry = lax.fori_loop(0, first, lambda kj, cr: block(kj, cr, None), zero + zero)
        for d in range(diagonal_blocks):
            carry = block(first + d, carry, causal(d))
        dq_ref[...] = jnp.where(head0, carry[2], carry[5]) * ATTN_SCALE

    blk = pl.BlockSpec((tq, HEAD_PAIR), lambda p, i: (i, p))
    seq = pl.BlockSpec((T // tk, HEAD_PAIR, tk), lambda p, i: (0, p, 0))
    transposed = jax.ShapeDtypeStruct((T // tk, SB_WIDTH, tk), F32)
    return pl.pallas_call(
        body, name=name, grid=(N_HEADS // 2, T // tq),
        in_specs=_attn_specs(T, tq) + [blk, blk, AFTER], out_specs=[blk, seq, seq],
        out_shape=[jax.ShapeDtypeStruct((T, SB_WIDTH), F32), transposed, transposed],
        scratch_shapes=[pltpu.VMEM((T, HEAD_PAIR), BF16), pltpu.VMEM((T // tk, HEAD_PAIR, tk), BF16),
                        pltpu.VMEM((T // tk, HEAD_PAIR, tk), BF16)],
        compiler_params=_params(("arbitrary", "arbitrary")),
    )(proj, proj, proj, do, ltot, _in_hbm(after))


def _mix_specs(T, D, tm, wbp, w_out):
    gate_col = (POOL_WIDTH + 3 * SB_WIDTH) // D
    row = lambda i: (i, 0)
    return [
        pl.BlockSpec((tm, D), row),
        pl.BlockSpec((tm, POOL_WIDTH), row),
        pl.BlockSpec((tm, SB_WIDTH), row),
        pl.BlockSpec((tm, D), lambda i: (i, gate_col)),
        pl.BlockSpec((tm, D), lambda i: (i, gate_col + 1)),
        pl.BlockSpec(wbp.shape, lambda i: (0, 0)),
        pl.BlockSpec(wbp.shape, lambda i: (0, 0)),
        pl.BlockSpec(w_out.shape, lambda i: (0, 0)),
    ]


def _mix_fwd(h, p, o, proj, wbp, wba, w_out, *, tm, name):
    T, D = h.shape
    tm = min(tm, T)

    def body(h_ref, p_ref, o_ref, glp_ref, gls_ref, wbp_ref, wba_ref, wout_ref, hout_ref, m_ref):
        yp = _mm_nt(p_ref[...].astype(BF16), wbp_ref[...])
        ys = _mm_nt(o_ref[...].astype(BF16), wba_ref[...])
        m = (jax.nn.sigmoid(glp_ref[...]) * yp + jax.nn.sigmoid(gls_ref[...]) * ys).astype(BF16)
        m_ref[...] = m
        hout_ref[...] = h_ref[...] + _mm(m, wout_ref[...])

    row = lambda i: (i, 0)
    return pl.pallas_call(
        body, name=name, grid=(T // tm,),
        in_specs=_mix_specs(T, D, tm, wbp, w_out),
        out_specs=[pl.BlockSpec((tm, D), row), pl.BlockSpec((tm, D), row)],
        out_shape=[jax.ShapeDtypeStruct((T, D), F32), jax.ShapeDtypeStruct((T, D), BF16)],
        compiler_params=_params(("arbitrary",)),
    )(h, p, o, proj, proj, wbp, wba, w_out)


def _mix_bwd(dh, p, o, proj, wbp, wba, w_out, after, *, tm, name):
    T, D = dh.shape
    tm = min(tm, T)

    def body(dh_ref, p_ref, o_ref, glp_ref, gls_ref, wbp_ref, wba_ref, wout_ref, after_ref,
             dyp_ref, dys_ref, dp_ref, do_ref, dgl_ref):
        dm = _mm_nt(dh_ref[...].astype(BF16), wout_ref[...])
        yp = _mm_nt(p_ref[...].astype(BF16), wbp_ref[...])
        ys = _mm_nt(o_ref[...].astype(BF16), wba_ref[...])
        gp = jax.nn.sigmoid(glp_ref[...])
        gs = jax.nn.sigmoid(gls_ref[...])
        dyp = (dm * gp).astype(BF16)
        dys = (dm * gs).astype(BF16)
        dyp_ref[...] = dyp
        dys_ref[...] = dys
        dgl_ref[:, :D] = (dm * yp * gp * (1.0 - gp)).astype(BF16)
        dgl_ref[:, D:] = (dm * ys * gs * (1.0 - gs)).astype(BF16)
        dp_ref[...] = _mm(dyp, wbp_ref[...])
        do_ref[...] = _mm(dys, wba_ref[...])

    row = lambda i: (i, 0)
    return pl.pallas_call(
        body, name=name, grid=(T // tm,),
        in_specs=_mix_specs(T, D, tm, wbp, w_out) + [AFTER],
        out_specs=[pl.BlockSpec((tm, D), row), pl.BlockSpec((tm, D), row), pl.BlockSpec((tm, POOL_WIDTH), row),
                   pl.BlockSpec((tm, SB_WIDTH), row), pl.BlockSpec((tm, 2 * D), row)],
        out_shape=[jax.ShapeDtypeStruct((T, D), BF16), jax.ShapeDtypeStruct((T, D), BF16),
                   jax.ShapeDtypeStruct((T, POOL_WIDTH), F32), jax.ShapeDtypeStruct((T, SB_WIDTH), F32),
                   jax.ShapeDtypeStruct((T, 2 * D), BF16)],
        compiler_params=_params(("arbitrary",)),
    )(dh, p, o, proj, proj, wbp, wba, w_out, _in_hbm(after))


def _adamw(w, g, m, v, *, name):
    R, C = w.shape
    tr = _row_tile(R, C)

    def body(w_ref, g_ref, m_ref, v_ref, d_ref, nm_ref, nv_ref):
        g_ = g_ref[...]
        m_ = ADAM_B1 * m_ref[...] + (1.0 - ADAM_B1) * g_
        v_ = ADAM_B2 * v_ref[...] + (1.0 - ADAM_B2) * (g_ * g_)
        m_hat = m_ / (1.0 - ADAM_B1 ** ADAM_STEP)
        v_hat = v_ / (1.0 - ADAM_B2 ** ADAM_STEP)
        d_ref[...] = -ADAM_LR * (m_hat / (jnp.sqrt(v_hat) + ADAM_EPS) + ADAM_WD * w_ref[...])
        nm_ref[...] = m_
        nv_ref[...] = v_

    spec = pl.BlockSpec((tr, C), lambda i: (i, 0))
    return pl.pallas_call(
        body, name=name, grid=(R // tr,), in_specs=[spec] * 4, out_specs=[spec] * 3,
        out_shape=[jax.ShapeDtypeStruct((R, C), F32)] * 3,
        compiler_params=_params(("arbitrary",)),
    )(w, g, m, v)


def _position():
    return lax.axis_index("x"), lax.axis_index("y"), lax.axis_index("c")


def _all_gather(shards, *, name, collective_id):
    n = len(shards)
    n_copies = 9

    def body(*refs):
        ins, outs = refs[:n], refs[n:2 * n]
        send_sems, recv_sems, local_sems = refs[2 * n:]
        x, y, c = _position()
        me, sibling = (x, y, c), (x, y, 1 - c)
        x_nbr, y_nbr, diagonal = (1 - x, y, c), (x, 1 - y, c), (1 - x, 1 - y, c)
        other = lambda pos: (pos[0], pos[1], 1 - c)

        barrier = pltpu.get_barrier_semaphore()
        for peer in (sibling, x_nbr, y_nbr):
            pl.semaphore_signal(barrier, inc=1, device_id=peer, device_id_type=MESH)
        pl.semaphore_wait(barrier, 3)

        def block(a, pos, half=None):
            ref = outs[a].at[4 * pos[0] + 2 * pos[1] + pos[2]]
            rows = ref.shape[0] // 2
            return ref if half is None else ref.at[pl.ds(half * rows, rows)]

        def copy(a, k, pos, to, half=None, src=None):
            return pltpu.make_async_remote_copy(
                src_ref=block(a, pos, half) if src is None else src, dst_ref=block(a, pos, half),
                send_sem=send_sems.at[n_copies * a + k], recv_sem=recv_sems.at[n_copies * a + k],
                device_id=to, device_id_type=MESH)

        started = []
        for a in range(n):
            mine = pltpu.make_async_copy(ins[a], block(a, me), local_sems.at[a])
            mine.start()
            started.append(mine)
        sends = []
        for a in range(n):
            sends += [copy(a, 1, me, x_nbr, src=ins[a]), copy(a, 2, me, y_nbr, src=ins[a]),
                      copy(a, 0, me, sibling, src=ins[a])]
        for cp in sends:
            cp.start()

        def pass_on(copies):
            for cp in copies:
                cp.start()
                sends.append(cp)

        for a in range(n):
            copy(a, 1, x_nbr, me).wait_recv()
            pass_on([copy(a, 5, x_nbr, y_nbr, half=0), copy(a, 3, x_nbr, sibling)])
            copy(a, 2, y_nbr, me).wait_recv()
            pass_on([copy(a, 6, y_nbr, x_nbr, half=1), copy(a, 4, y_nbr, sibling)])
        for a in range(n):
            copy(a, 5, diagonal, me, half=0).wait_recv()
            pass_on([copy(a, 7, diagonal, sibling, half=0)])
            copy(a, 6, diagonal, me, half=1).wait_recv()
            pass_on([copy(a, 8, diagonal, sibling, half=1)])
        for a in range(n):
            copy(a, 0, sibling, me).wait_recv()
            copy(a, 3, other(x_nbr), me).wait_recv()
            copy(a, 4, other(y_nbr), me).wait_recv()
            copy(a, 7, other(diagonal), me, half=0).wait_recv()
            copy(a, 8, other(diagonal), me, half=1).wait_recv()
        for cp in sends:
            cp.wait_send()
        for cp in started:
            cp.wait()

    return pl.kernel(
        body, name=name,
        out_type=[jax.ShapeDtypeStruct((N_DEV,) + s.shape, s.dtype) for s in shards],
        mesh=plsc.ScalarSubcoreMesh(axis_name="sequencer", num_cores=1),
        scratch_types=[pltpu.SemaphoreType.DMA((n_copies * n,)), pltpu.SemaphoreType.DMA((n_copies * n,)),
                       pltpu.SemaphoreType.DMA((n,))],
        compiler_params=pltpu.CompilerParams(collective_id=collective_id),
    )(*shards)


def _chip_sums(grads, *, name):
    _, R, C = grads.shape
    rc = 128 if R % 128 == 0 else R

    def body(g_ref, partial, out_ref, mine, theirs, send_sems, recv_sems, local_sems):
        x, y, c = _position()
        my_chip = 2 * x + y

        def swap(s):
            return pltpu.make_async_remote_copy(
                src_ref=g_ref.at[2 * s + (1 - c)], dst_ref=theirs.at[s],
                send_sem=send_sems.at[s], recv_sem=recv_sems.at[s],
                device_id=(x, y, 1 - c), device_id_type=MESH)

        def load(s):
            return pltpu.make_async_copy(g_ref.at[2 * s + c], mine.at[s], local_sems.at[s])

        for s in range(4):
            swap(s).start()
            load(s).start()
        for s in range(4):
            load(s).wait()
            swap(s).wait_recv()

        def chip_sum(chip, rows):
            return mine[chip, rows, :].astype(F32) + theirs[chip, rows, :].astype(F32)

        for j in (1, 2, 3):
            @pl.loop(0, R // rc)
            def _(t):
                rows = pl.ds(pl.multiple_of(t * rc, rc), rc)
                partial[j - 1, rows, :] = chip_sum(my_chip ^ j, rows).astype(BF16)

        @pl.loop(0, R // rc)
        def _(t):
            rows = pl.ds(pl.multiple_of(t * rc, rc), rc)
            out_ref[rows, :] = chip_sum(my_chip, rows)

        for s in range(4):
            swap(s).wait_send()

    vmem = pl.BlockSpec(memory_space=pltpu.VMEM)
    return pl.pallas_call(
        body, name=name,
        in_specs=[pl.BlockSpec(memory_space=pl.ANY)], out_specs=[vmem, vmem],
        out_shape=[jax.ShapeDtypeStruct((3, R, C), BF16), jax.ShapeDtypeStruct((R, C), F32)],
        scratch_shapes=[
            pltpu.VMEM((4, R, C), BF16), pltpu.VMEM((4, R, C), BF16),
            pltpu.SemaphoreType.DMA((4,)), pltpu.SemaphoreType.DMA((4,)), pltpu.SemaphoreType.DMA((4,)),
        ],
        compiler_params=_params(),
    )(grads)


def _cross_chips(partials, *, name, collective_id):
    n = len(partials)

    def body(*refs):
        ins, outs = refs[:n], refs[n:2 * n]
        send_sems, recv_sems = refs[2 * n:]
        x, y, c = _position()
        my_chip = 2 * x + y
        peers = [((my_chip ^ j) // 2, (my_chip ^ j) % 2, c) for j in (1, 2, 3)]

        barrier = pltpu.get_barrier_semaphore()
        for peer in peers:
            pl.semaphore_signal(barrier, inc=1, device_id=peer, device_id_type=MESH)
        pl.semaphore_wait(barrier, 3)

        copies = [
            pltpu.make_async_remote_copy(
                src_ref=ins[a].at[j], dst_ref=outs[a].at[j],
                send_sem=send_sems.at[3 * a + j], recv_sem=recv_sems.at[3 * a + j],
                device_id=peers[j], device_id_type=MESH)
            for a in range(n) for j in range(3)]
        for cp in copies:
            cp.start()
        for cp in copies:
            cp.wait_recv()
        for cp in copies:
            cp.wait_send()

    return pl.kernel(
        body, name=name,
        out_type=[jax.ShapeDtypeStruct(p.shape, p.dtype) for p in partials],
        mesh=plsc.ScalarSubcoreMesh(axis_name="sequencer", num_cores=1),
        scratch_types=[pltpu.SemaphoreType.DMA((3 * n,)), pltpu.SemaphoreType.DMA((3 * n,))],
        compiler_params=pltpu.CompilerParams(collective_id=collective_id),
    )(*partials)


def _cross_chips_and_gather(partial, slab, *, name, collective_id):
    def body(part_ref, slab_ref, landed_ref, slabs_ref, send_sems, recv_sems, local_sem):
        x, y, c = _position()
        me, my_chip = 4 * x + 2 * y + c, 2 * x + y
        others = [me ^ k for k in range(1, N_DEV)]
        ids = [(o // 4, (o // 2) % 2, o % 2) for o in others]

        barrier = pltpu.get_barrier_semaphore()
        for peer in ids:
            pl.semaphore_signal(barrier, inc=1, device_id=peer, device_id_type=MESH)
        pl.semaphore_wait(barrier, N_DEV - 1)

        mine = pltpu.make_async_copy(slab_ref, slabs_ref.at[me], local_sem)
        mine.start()
        sends = [
            pltpu.make_async_remote_copy(
                src_ref=part_ref.at[j], dst_ref=landed_ref.at[j], send_sem=send_sems.at[j], recv_sem=recv_sems.at[j],
                device_id=((my_chip ^ (j + 1)) // 2, (my_chip ^ (j + 1)) % 2, c), device_id_type=MESH)
            for j in range(3)]
        sends += [
            pltpu.make_async_remote_copy(
                src_ref=slab_ref, dst_ref=slabs_ref.at[me], send_sem=send_sems.at[3 + k], recv_sem=recv_sems.at[3 + k],
                device_id=ids[k], device_id_type=MESH)
            for k in range(N_DEV - 1)]
        arrivals = sends[:3] + [
            pltpu.make_async_remote_copy(
                src_ref=slab_ref, dst_ref=slabs_ref.at[others[k]], send_sem=send_sems.at[3 + k],
                recv_sem=recv_sems.at[3 + k], device_id=ids[k], device_id_type=MESH)
            for k in range(N_DEV - 1)]
        for cp in sends:
            cp.start()
        for cp in arrivals:
            cp.wait_recv()
        for cp in sends:
            cp.wait_send()
        mine.wait()

    n_sems = 3 + N_DEV - 1
    return pl.kernel(
        body, name=name,
        out_type=[jax.ShapeDtypeStruct(partial.shape, partial.dtype),
                  jax.ShapeDtypeStruct((N_DEV,) + slab.shape, slab.dtype)],
        mesh=plsc.ScalarSubcoreMesh(axis_name="sequencer", num_cores=1),
        scratch_types=[pltpu.SemaphoreType.DMA((n_sems,)), pltpu.SemaphoreType.DMA((n_sems,)), pltpu.SemaphoreType.DMA],
        compiler_params=pltpu.CompilerParams(collective_id=collective_id),
    )(partial, slab)


def _sum_devices(gathered, after, *, name):
    _, R, C = gathered.shape

    def body(in_ref, after_ref, out_ref):
        total = in_ref[0]
        for d in range(1, N_DEV):
            total = total + in_ref[d]
        out_ref[...] = total

    return pl.pallas_call(
        body, name=name, grid=(1,),
        in_specs=[pl.BlockSpec((N_DEV, R, C), lambda i: (0, 0, 0)), AFTER],
        out_specs=pl.BlockSpec((R, C), lambda i: (0, 0)),
        out_shape=jax.ShapeDtypeStruct((R, C), F32),
        compiler_params=_params(("arbitrary",)),
    )(gathered, _in_hbm(after))


def _owner_sum(own, landed, after, *, name):
    R, C = own.shape
    tr = _row_tile(R, C)

    def body(own_ref, landed_ref, after_ref, out_ref):
        total = own_ref[...]
        for j in range(3):
            total = total + landed_ref[j].astype(F32)
        out_ref[...] = total

    return pl.pallas_call(
        body, name=name, grid=(R // tr,),
        in_specs=[pl.BlockSpec((tr, C), lambda i: (i, 0)), pl.BlockSpec((3, tr, C), lambda i: (0, i, 0)), AFTER],
        out_specs=pl.BlockSpec((tr, C), lambda i: (i, 0)),
        out_shape=jax.ShapeDtypeStruct((R, C), F32),
        compiler_params=_params(("arbitrary",)),
    )(own, landed, _in_hbm(after))


def _local_step(x, target, norms, pool_w_group, pool_scale, wgu1, wd1, w_in, wbp, wba, w_out, wgu2, wd2, exchange):
    n1g, nmg, n2g, nfg = norms
    D = x.shape[1]
    gu1, hid1 = _ffn_up(x, n1g, wgu1, tm=1024, name="ffn1_up")
    h1 = _ffn_down(x, hid1, wd1, tm=512, name="ffn1_down")
    un, proj = _inproj_fwd(h1, nmg, w_in, tm=1024, name="inproj_fwd")
    p = _pool_fwd(proj, pool_w_group, pool_scale, name="pool_fwd")
    o, ltot = _attn_fwd(proj, name="attn_fwd")
    h2, m = _mix_fwd(h1, p, o, proj, wbp, wba, w_out, tm=256, name="mix_fwd")
    gu2, hid2 = _ffn_up(h2, n2g, wgu2, tm=1024, name="ffn2_up")
    h3 = _ffn_down(h2, hid2, wd2, tm=512, name="ffn2_down")
    dh3, df2, loss, d_nf = _loss_bwd(h3, target, nfg, tm=256, name="loss_bwd")

    d_wd2 = _wgrad_down(hid2, df2, tk=WGRAD_TOKENS, name="ffn2_wgrad_down")
    (g_wd2,), token = exchange("ffn2_down", [d_wd2.reshape(N_DEV, FF_SHARD_PAD, D)])
    dh2, d_n2, n2, dgu2 = _ffn_bwd(dh3, df2, h2, n2g, gu2, wgu2, wd2, token, tm=512, name="ffn2_bwd")
    d_wgu2 = _wgrad_gate_up(n2, dgu2, tk=WGRAD_TOKENS, name="ffn2_wgrad_gate_up")
    (g_wgu2,), token = exchange("ffn2_gate_up", [d_wgu2])

    dyp, dys, dp, do, dgl = _mix_bwd(dh2, p, o, proj, wbp, wba, w_out, token, tm=256, name="mix_bwd")
    d_wout = _wgrad_full(m, dh2, tk=WGRAD_TOKENS, name="wgrad_out")
    d_wbp = _wgrad_full(dyp, p, tk=WGRAD_TOKENS, name="wgrad_branch_pool")
    d_wba = _wgrad_full(dys, o, tk=WGRAD_TOKENS, name="wgrad_branch_attn")
    by_owner = lambda g: g.reshape(N_DEV, g.shape[0] // N_DEV, g.shape[1])
    (g_wbp, g_wba, g_wout), token = exchange("mix", [by_owner(d_wbp), by_owner(d_wba), by_owner(d_wout)])
    dxp, d_wgroup, d_scale = _pool_bwd(dp, proj, pool_w_group, pool_scale, name="pool_bwd")
    dq, dkt, dvt = _attn_bwd(proj, do, ltot, token, name="attn_bwd")
    dk, dv = (t.transpose(0, 2, 1).reshape(dq.shape) for t in (dkt, dvt))
    dproj = jnp.concatenate([dxp.astype(BF16), dq.astype(BF16), dk.astype(BF16), dv.astype(BF16), dgl], axis=1)
    d_win = _wgrad_in(dproj, un, tk=WGRAD_TOKENS, name="wgrad_in")
    (g_win,), token_in = exchange("w_in", [d_win])
    dh1, df1, d_nm = _inproj_bwd(dproj, dh2, h1, nmg, w_in, tm=512, name="inproj_bwd")
    d_wd1 = _wgrad_down(hid1, df1, tk=WGRAD_TOKENS, name="ffn1_wgrad_down")
    (g_wd1,), token_down = exchange("ffn1_down", [d_wd1.reshape(N_DEV, FF_SHARD_PAD, D)])
    token = (token_down[(0,) * token_down.ndim] + token_in[(0,) * token_in.ndim]).reshape(1, 1)

    dx, d_n1, n1, dgu1 = _ffn_bwd(dh1, df1, x, n1g, gu1, wgu1, wd1, token, tm=512, name="ffn1_bwd")
    d_wgu1 = _wgrad_gate_up(n1, dgu1, tk=WGRAD_TOKENS, name="ffn1_wgrad_gate_up")
    (g_wgu1, replicated), token = exchange("last", [d_wgu1, d_n1, d_nm, d_n2, d_nf, d_scale, d_wgroup, loss])

    sharded = (g_wgu1, g_wd1, g_win, g_wbp, g_wba, g_wout, g_wgu2, g_wd2)
    return dx, sharded, replicated, token


def _hidden_major(w):
    return jnp.swapaxes(w[0], 0, 1)


def _pad_gate_up(wt):
    d = wt.shape[1]
    wt = wt.astype(BF16).reshape(2, FF_SHARD, d)
    return jnp.pad(wt, ((0, 0), (0, FF_SHARD_PAD - FF_SHARD), (0, 0))).reshape(2 * FF_SHARD_PAD, d)


def _unpad_gate_up(gt):
    d = gt.shape[1]
    return gt.reshape(2, FF_SHARD_PAD, d)[:, :FF_SHARD].reshape(2 * FF_SHARD, d)


def _pad_down(w):
    return jnp.pad(w.astype(BF16), ((0, FF_SHARD_PAD - FF_SHARD), (0, 0)))


def kernel(x, ffn1_norm, ffn1_w_gate_up, ffn1_w_down, mix_norm, w_in, pool_w_group, pool_scale, w_branch_pool, w_branch_attn, w_out, ffn2_norm, ffn2_w_gate_up, ffn2_w_down, final_norm, loss_target, m_ffn1_norm, m_ffn1_w_gate_up, m_ffn1_w_down, m_mix_norm, m_w_in, m_pool_w_group, m_pool_scale, m_w_branch_pool, m_w_branch_attn, m_w_out, m_ffn2_norm, m_ffn2_w_gate_up, m_ffn2_w_down, m_final_norm, v_ffn1_norm, v_ffn1_w_gate_up, v_ffn1_w_down, v_mix_norm, v_w_in, v_pool_w_group, v_pool_scale, v_w_branch_pool, v_w_branch_attn, v_w_out, v_ffn2_norm, v_ffn2_w_gate_up, v_ffn2_w_down, v_final_norm):
    D = x.shape[-1]
    weights = dict(ffn1_norm=ffn1_norm, ffn1_w_gate_up=ffn1_w_gate_up, ffn1_w_down=ffn1_w_down, mix_norm=mix_norm,
                   w_in=w_in, pool_w_group=pool_w_group, pool_scale=pool_scale, w_branch_pool=w_branch_pool,
                   w_branch_attn=w_branch_attn, w_out=w_out, ffn2_norm=ffn2_norm, ffn2_w_gate_up=ffn2_w_gate_up,
                   ffn2_w_down=ffn2_w_down, final_norm=final_norm)
    first = dict(ffn1_norm=m_ffn1_norm, ffn1_w_gate_up=m_ffn1_w_gate_up, ffn1_w_down=m_ffn1_w_down,
                 mix_norm=m_mix_norm, w_in=m_w_in, pool_w_group=m_pool_w_group, pool_scale=m_pool_scale,
                 w_branch_pool=m_w_branch_pool, w_branch_attn=m_w_branch_attn, w_out=m_w_out,
                 ffn2_norm=m_ffn2_norm, ffn2_w_gate_up=m_ffn2_w_gate_up, ffn2_w_down=m_ffn2_w_down,
                 final_norm=m_final_norm)
    second = dict(ffn1_norm=v_ffn1_norm, ffn1_w_gate_up=v_ffn1_w_gate_up, ffn1_w_down=v_ffn1_w_down,
                  mix_norm=v_mix_norm, w_in=v_w_in, pool_w_group=v_pool_w_group, pool_scale=v_pool_scale,
                  w_branch_pool=v_w_branch_pool, w_branch_attn=v_w_branch_attn, w_out=v_w_out,
                  ffn2_norm=v_ffn2_norm, ffn2_w_gate_up=v_ffn2_w_gate_up, ffn2_w_down=v_ffn2_w_down,
                  final_norm=v_final_norm)
    order = list(weights)

    wgu1, = _all_gather([_pad_gate_up(_hidden_major(ffn1_w_gate_up))], name="all_gather_ffn1_gate_up", collective_id=0)
    wd1, = _all_gather([_pad_down(ffn1_w_down[0])], name="all_gather_ffn1_down", collective_id=10)
    transposed = lambda w: jnp.swapaxes(w[0], 0, 1).astype(BF16)
    win_g, = _all_gather([transposed(w_in)], name="all_gather_w_in", collective_id=1)
    wbp_g, wba_g = _all_gather([transposed(w_branch_pool), transposed(w_branch_attn)],
                               name="all_gather_branches", collective_id=2)
    wout_g, = _all_gather([w_out[0].astype(BF16)], name="all_gather_w_out", collective_id=11)
    wgu2, wd2 = _all_gather([_pad_gate_up(_hidden_major(ffn2_w_gate_up)), _pad_down(ffn2_w_down[0])],
                            name="all_gather_ffn2", collective_id=3)
    whole = lambda g: g.reshape(g.shape[0] * g.shape[1], g.shape[2])
    wd1, wd2, win_g, wbp_g, wba_g, wout_g = (whole(g) for g in (wd1, wd2, win_g, wbp_g, wba_g, wout_g))

    cross_ids = {"ffn2_down": 4, "ffn2_gate_up": 5, "mix": 6, "ffn1_down": 7, "w_in": 8, "last": 9}
    small = ["ffn1_norm", "mix_norm", "ffn2_norm", "final_norm", "pool_scale", "pool_w_group"]

    def tile_rows(a):
        a = a.reshape(-1, 128)
        return jnp.pad(a, ((0, -a.shape[0] % 8), (0, 0)))

    def exchange(tag, group):
        if tag == "last":
            slab = jnp.concatenate([tile_rows(g) for g in group[1:-1]] + [jnp.broadcast_to(group[-1], (8, 128))], axis=0)
            partial, own = _chip_sums(group[0], name="chip_sums_last")
            landed, slabs = _cross_chips_and_gather(partial, slab, name="cross_chips_last", collective_id=cross_ids[tag])
            return [(own, landed), slabs], own
        sums = [_chip_sums(g, name=f"chip_sums_{tag}_{i}") for i, g in enumerate(group)]
        landed = _cross_chips([s[0] for s in sums], name="cross_chips_" + tag, collective_id=cross_ids[tag])
        token = sums[0][1] if len(sums) == 1 else sum(s[1][0, 0] for s in sums).reshape(1, 1)
        return [(s[1], l) for s, l in zip(sums, landed)], token

    norms = (ffn1_norm, mix_norm, ffn2_norm, final_norm.reshape(1, D))
    dx, sharded, slabs, last = _local_step(
        x[0], loss_target[0], norms, pool_w_group[0], pool_scale, wgu1, wd1, win_g, wbp_g, wba_g, wout_g, wgu2, wd2,
        exchange)
    names = ["ffn1_w_gate_up", "ffn1_w_down", "w_in", "w_branch_pool", "w_branch_attn", "w_out",
             "ffn2_w_gate_up", "ffn2_w_down"]
    handles = dict(zip(names, sharded))
    grads, delta, new_m, new_v = {}, {}, {}, {}
    after = last
    for k in ("ffn2_w_down", "ffn2_w_gate_up", "w_branch_pool", "w_branch_attn", "w_out", "w_in", "ffn1_w_down",
              "ffn1_w_gate_up"):
        g = _owner_sum(*handles[k], after, name="owner_sum_" + k)
        hidden_major = k.endswith("w_gate_up")
        if hidden_major:
            g = _unpad_gate_up(g)
        elif k in ("w_in", "w_branch_pool", "w_branch_attn"):
            g = jnp.swapaxes(g, 0, 1)
        else:
            g = g[:weights[k].shape[1]]
        view = _hidden_major if hidden_major else (lambda a: a[0])
        back = (lambda a: jnp.swapaxes(a, 0, 1)[None]) if hidden_major else (lambda a: a[None])
        out = _adamw(view(weights[k]), g, view(first[k]), view(second[k]), name="adamw_" + k)
        after = out[0]
        grads[k] = back(g)
        delta[k], new_m[k], new_v[k] = (back(a) for a in out)

    rows = [weights[k].size // 128 for k in small]
    padded_rows = [-(-r // 8) * 8 for r in rows]
    starts = [sum(padded_rows[:i]) for i in range(len(rows) + 1)]
    total = _sum_devices(slabs, after, name="sum_replicated")
    loss_out = total[starts[-1], 0]
    small_w = jnp.concatenate([tile_rows(weights[k]) for k in small], axis=0)
    small_m = jnp.concatenate([tile_rows(first[k]) for k in small], axis=0)
    small_v = jnp.concatenate([tile_rows(second[k]) for k in small], axis=0)
    small_out = _adamw(small_w, total[:starts[-1]], small_m, small_v, name="adamw_replicated")
    for name_, start, n_rows in zip(small, starts, rows):
        shape = weights[name_].shape
        grads[name_] = total[start:start + n_rows].reshape(shape)
        delta[name_], new_m[name_], new_v[name_] = (a[start:start + n_rows].reshape(shape) for a in small_out)

    return (loss_out, dx[None], *[grads[k] for k in order], *[delta[k] for k in order],
            *[new_m[k] for k in order], *[new_v[k] for k in order])
```

```python
import functools

import jax
import jax.numpy as jnp
from jax import lax
from jax.experimental import pallas as pl
from jax.experimental.pallas import tpu as pltpu
from jax.experimental.pallas import tpu_sc as plsc

F32 = jnp.float32
BF16 = jnp.bfloat16
MESH = pl.DeviceIdType.MESH

RMS_EPS = 1e-6
N_DEV = 8
N_HEADS = 8
HEAD_DIM = 64
HEAD_PAIR = 2 * HEAD_DIM
POOL_WINDOWS = (2, 4, 8, 16)
POOL_GROUP = 128
POOL_WIDTH = 512
SB_WIDTH = 512
FF_SHARD = 352
FF_SHARD_PAD = 384
ATTN_K_BLOCK = 256
ATTN_Q_BLOCK_FWD = 512
ATTN_Q_BLOCK_BWD = 256
ATTN_SCALE = 0.125

ADAM_LR = 0.001
ADAM_B1 = 0.9
ADAM_B2 = 0.999
ADAM_EPS = 1e-08
ADAM_WD = 0.01
ADAM_STEP = 10

VMEM_LIMIT = 48 << 20
WGRAD_TOKENS = 2048


def _params(dims=None):
    return pltpu.CompilerParams(dimension_semantics=dims, vmem_limit_bytes=VMEM_LIMIT)


def _mm(a, b):
    return jnp.dot(a, b, preferred_element_type=F32)


def _mm_nt(a, b):
    return lax.dot_general(a, b, (((1,), (1,)), ((), ())), preferred_element_type=F32)


def _mm_tn(a, b):
    return lax.dot_general(a, b, (((0,), (0,)), ((), ())), preferred_element_type=F32)


def _row_tile(rows, cols):
    limit = max(8, (512 * 1024) // cols)
    return max(t for t in range(8, rows + 1, 8) if rows % t == 0 and (t <= limit or t == 8))


def _rstd(xf):
    return lax.rsqrt(jnp.mean(xf * xf, axis=-1, keepdims=True) + RMS_EPS)


def _rms_bwd(xf, gain, dn):
    r = _rstd(xf)
    xh = xf * r
    dgain = jnp.sum(dn * xh, axis=0, keepdims=True)
    dxh = dn * gain
    dx = r * (dxh - xh * jnp.mean(dxh * xh, axis=-1, keepdims=True))
    return dx, dgain


def _ffn_up(x, gain, wgu, *, tm, name):
    T, D = x.shape
    tm = min(tm, T)
    nb, bw = wgu.shape[0] // 2, wgu.shape[1]

    def body(x_ref, gain_ref, wg_ref, wu_ref, gu_ref, hid_ref, n_scr):
        @pl.when(pl.program_id(1) == 0)
        def _():
            xf = x_ref[...]
            n_scr[...] = (xf * _rstd(xf) * gain_ref[...]).astype(BF16)

        halves = (pl.ds(0, tm // 2), pl.ds(tm // 2, tm // 2))
        wg, wu = wg_ref[...], wu_ref[...]
        gus = [(_mm_nt(n_scr[rows, :], wg), _mm_nt(n_scr[rows, :], wu)) for rows in halves]
        for rows, (g, u) in zip(halves, gus):
            gu_ref[0, rows, :] = g.astype(BF16)
            gu_ref[1, rows, :] = u.astype(BF16)
            hid_ref[rows, :] = (g * jax.nn.sigmoid(g) * u).astype(BF16)

    return pl.pallas_call(
        body, name=name, grid=(T // tm, nb),
        in_specs=[
            pl.BlockSpec((tm, D), lambda i, j: (i, 0)),
            pl.BlockSpec((1, D), lambda i, j: (0, 0)),
            pl.BlockSpec((None, bw, D), lambda i, j: (j, 0, 0)),
            pl.BlockSpec((None, bw, D), lambda i, j: (j + nb, 0, 0)),
        ],
        out_specs=[
            pl.BlockSpec((2, tm, bw), lambda i, j: (0, i, j)),
            pl.BlockSpec((tm, bw), lambda i, j: (i, j)),
        ],
        out_shape=[jax.ShapeDtypeStruct((2, T, nb * bw), BF16), jax.ShapeDtypeStruct((T, nb * bw), BF16)],
        scratch_shapes=[pltpu.VMEM((tm, D), BF16)],
        compiler_params=_params(("arbitrary", "arbitrary")),
    )(x, gain, wgu, wgu)


def _ffn_down(x, hid, wd, *, tm, name):
    T, D = x.shape
    tm = min(tm, T)
    F = hid.shape[1]

    def body(x_ref, hid_ref, wd_ref, h_ref):
        h_ref[...] = x_ref[...] + 0.5 * _mm(hid_ref[...], wd_ref[...])

    return pl.pallas_call(
        body, name=name, grid=(T // tm,),
        in_specs=[
            pl.BlockSpec((tm, D), lambda i: (i, 0)),
            pl.BlockSpec((tm, F), lambda i: (i, 0)),
            pl.BlockSpec((F, D), lambda i: (0, 0)),
        ],
        out_specs=pl.BlockSpec((tm, D), lambda i: (i, 0)),
        out_shape=jax.ShapeDtypeStruct((T, D), F32),
        compiler_params=_params(("arbitrary",)),
    )(x, hid, wd)


AFTER = pl.BlockSpec(memory_space=pltpu.HBM)


def _in_hbm(token):
    return pltpu.with_memory_space_constraint(token, pltpu.HBM)


def _ffn_bwd(dh, df, x, gain, gu, wgu, wd, after, *, tm, name):
    T, D = x.shape
    tm = min(tm, T)
    nb, bw = wgu.shape[0] // 2, wgu.shape[1]

    def body(dh_ref, df_ref, x_ref, gain_ref, gu_ref, wg_ref, wu_ref, wd_ref, after_ref,
             dx_ref, dgain_ref, n_ref, dgu_ref, dn_acc):
        i, j = pl.program_id(0), pl.program_id(1)

        @pl.when(j == 0)
        def _():
            xf = x_ref[...]
            n_ref[...] = (xf * _rstd(xf) * gain_ref[...]).astype(BF16)
            dn_acc[...] = jnp.zeros_like(dn_acc)

        @pl.when((i == 0) & (j == 0))
        def _():
            dgain_ref[...] = jnp.zeros_like(dgain_ref)

        halves = (pl.ds(0, tm // 2), pl.ds(tm // 2, tm // 2))
        wd, wg, wu = wd_ref[...], wg_ref[...], wu_ref[...]
        dhids = [_mm_nt(df_ref[rows, :], wd) for rows in halves]
        for rows, dhid in zip(halves, dhids):
            g = gu_ref[0, rows, :].astype(F32)
            u = gu_ref[1, rows, :].astype(F32)
            s = jax.nn.sigmoid(g)
            silu = g * s
            dg = (dhid * u * (s * (1.0 + g * (1.0 - s)))).astype(BF16)
            du = (dhid * silu).astype(BF16)
            dgu_ref[0, rows, :] = dg
            dgu_ref[1, rows, :] = du
            dn_acc[rows, :] += _mm(dg, wg) + _mm(du, wu)

        @pl.when(j == nb - 1)
        def _():
            dx, dgain = _rms_bwd(x_ref[...], gain_ref[...], dn_acc[...])
            dx_ref[...] = dh_ref[...] + dx
            dgain_ref[...] += dgain

    row = lambda i, j: (i, 0)
    return pl.pallas_call(
        body, name=name, grid=(T // tm, nb),
        in_specs=[
            pl.BlockSpec((tm, D), row),
            pl.BlockSpec((tm, D), row),
            pl.BlockSpec((tm, D), row),
            pl.BlockSpec((1, D), lambda i, j: (0, 0)),
            pl.BlockSpec((2, tm, bw), lambda i, j: (0, i, j)),
            pl.BlockSpec((None, bw, D), lambda i, j: (j, 0, 0)),
            pl.BlockSpec((None, bw, D), lambda i, j: (j + nb, 0, 0)),
            pl.BlockSpec((bw, D), lambda i, j: (j, 0)),
            AFTER,
        ],
        out_specs=[
            pl.BlockSpec((tm, D), row),
            pl.BlockSpec((1, D), lambda i, j: (0, 0)),
            pl.BlockSpec((tm, D), row),
            pl.BlockSpec((2, tm, bw), lambda i, j: (0, i, j)),
        ],
        out_shape=[
            jax.ShapeDtypeStruct((T, D), F32),
            jax.ShapeDtypeStruct((1, D), F32),
            jax.ShapeDtypeStruct((T, D), BF16),
            jax.ShapeDtypeStruct((2, T, nb * bw), BF16),
        ],
        scratch_shapes=[pltpu.VMEM((tm, D), F32)],
        compiler_params=_params(("arbitrary", "arbitrary")),
    )(dh, df, x, gain, gu, wgu, wgu, wd, _in_hbm(after))


def _wgrad(a, b, *, grid, a_spec, b_spec, out_spec, out_shape, acc_shape, name):
    nk = grid[2]

    def body(a_ref, b_ref, o_ref, acc):
        k = pl.program_id(2)

        @pl.when(k == 0)
        def _():
            acc[...] = jnp.zeros_like(acc)

        acc[...] += _mm_tn(a_ref[...].astype(BF16), b_ref[...].astype(BF16))

        @pl.when(k == nk - 1)
        def _():
            o_ref[...] = acc[...].astype(o_ref.dtype)

    return pl.pallas_call(
        body, name=name, grid=grid, in_specs=[a_spec, b_spec], out_specs=out_spec,
        out_shape=jax.ShapeDtypeStruct(out_shape, BF16),
        scratch_shapes=[pltpu.VMEM(acc_shape, F32)],
        compiler_params=_params(("arbitrary", "arbitrary", "arbitrary")),
    )(a, b)


def _wgrad_gate_up(n, dgu, *, tk, name, part=0, parts=1):
    T, D = n.shape
    tk = min(tk, T)
    owner_rows = FF_SHARD_PAD * 2
    nb = dgu.shape[2] // owner_rows
    bw = owner_rows // parts
    return _wgrad(
        dgu, n, grid=(2 * nb, 1, T // tk), name=name,
        a_spec=pl.BlockSpec((None, tk, bw), lambda m, c, k: (m // nb, k, parts * (m % nb) + part)),
        b_spec=pl.BlockSpec((tk, D), lambda m, c, k: (k, 0)),
        out_spec=pl.BlockSpec((None, bw, D), lambda m, c, k: (m, 0, 0)),
        out_shape=(2 * nb, bw, D), acc_shape=(bw, D))


def _wgrad_down(hid, df, *, tk, name):
    T, D = df.shape
    tk = min(tk, T)
    bw = FF_SHARD_PAD * 2
    nb = hid.shape[1] // bw
    return _wgrad(
        hid, df, grid=(nb, 1, T // tk), name=name,
        a_spec=pl.BlockSpec((tk, bw), lambda m, c, k: (k, m)),
        b_spec=pl.BlockSpec((tk, D), lambda m, c, k: (k, 0)),
        out_spec=pl.BlockSpec((bw, D), lambda m, c, k: (m, 0)),
        out_shape=(nb * bw, D), acc_shape=(bw, D))


def _wgrad_in(dproj, un, *, tk, name):
    T, D = un.shape
    tk = min(tk, T)
    bw = dproj.shape[1] // N_DEV
    return _wgrad(
        dproj, un, grid=(N_DEV, 1, T // tk), name=name,
        a_spec=pl.BlockSpec((tk, bw), lambda m, c, k: (k, m)),
        b_spec=pl.BlockSpec((tk, D), lambda m, c, k: (k, 0)),
        out_spec=pl.BlockSpec((None, bw, D), lambda m, c, k: (m, 0, 0)),
        out_shape=(N_DEV, bw, D), acc_shape=(bw, D))


def _wgrad_full(a, b, *, tk, name):
    T, M = a.shape
    tk = min(tk, T)
    N = b.shape[1]
    return _wgrad(
        a, b, grid=(1, 1, T // tk), name=name,
        a_spec=pl.BlockSpec((tk, M), lambda m, c, k: (k, 0)),
        b_spec=pl.BlockSpec((tk, N), lambda m, c, k: (k, 0)),
        out_spec=pl.BlockSpec((M, N), lambda m, c, k: (0, 0)), out_shape=(M, N), acc_shape=(M, N))


def _loss_bwd(h, target, gain, *, tm, name):
    T, D = h.shape
    tm = min(tm, T)

    def body(h_ref, t_ref, gain_ref, dh_ref, df_ref, loss_ref, dgain_ref):
        @pl.when(pl.program_id(0) == 0)
        def _():
            loss_ref[...] = jnp.zeros_like(loss_ref)
            dgain_ref[...] = jnp.zeros_like(dgain_ref)

        xf = h_ref[...]
        gain = gain_ref[...]
        err = xf * _rstd(xf) * gain - t_ref[...]
        loss_ref[...] += 0.5 * jnp.sum(jnp.mean(err * err, axis=-1, keepdims=True), axis=0, keepdims=True)
        dx, dgain = _rms_bwd(xf, gain, err * (1.0 / D))
        dh_ref[...] = dx
        df_ref[...] = (0.5 * dx).astype(BF16)
        dgain_ref[...] += dgain

    row = lambda i: (i, 0)
    fixed = lambda i: (0, 0)
    return pl.pallas_call(
        body, name=name, grid=(T // tm,),
        in_specs=[pl.BlockSpec((tm, D), row), pl.BlockSpec((tm, D), row), pl.BlockSpec((1, D), fixed)],
        out_specs=[pl.BlockSpec((tm, D), row), pl.BlockSpec((tm, D), row), pl.BlockSpec((1, 128), fixed),
                   pl.BlockSpec((1, D), fixed)],
        out_shape=[jax.ShapeDtypeStruct((T, D), F32), jax.ShapeDtypeStruct((T, D), BF16),
                   jax.ShapeDtypeStruct((1, 128), F32), jax.ShapeDtypeStruct((1, D), F32)],
        compiler_params=_params(("arbitrary",)),
    )(h, target, gain)


def _inproj_fwd(h, gain, w_in_t, *, tm, name):
    T, D = h.shape
    tm = min(tm, T)
    bn = D
    nb = w_in_t.shape[0] // bn

    def body(h_ref, gain_ref, wt_ref, un_ref, proj_ref):
        @pl.when(pl.program_id(1) == 0)
        def _():
            xf = h_ref[...]
            un_ref[...] = (xf * _rstd(xf) * gain_ref[...]).astype(BF16)

        proj_ref[...] = _mm_nt(un_ref[...], wt_ref[...])

    return pl.pallas_call(
        body, name=name, grid=(T // tm, nb),
        in_specs=[
            pl.BlockSpec((tm, D), lambda i, j: (i, 0)),
            pl.BlockSpec((1, D), lambda i, j: (0, 0)),
            pl.BlockSpec((bn, D), lambda i, j: (j, 0)),
        ],
        out_specs=[pl.BlockSpec((tm, D), lambda i, j: (i, 0)), pl.BlockSpec((tm, bn), lambda i, j: (i, j))],
        out_shape=[jax.ShapeDtypeStruct((T, D), BF16), jax.ShapeDtypeStruct((T, nb * bn), F32)],
        compiler_params=_params(("arbitrary", "arbitrary")),
    )(h, gain, w_in_t)


def _inproj_bwd(dproj, dh, h, gain, w_in_t, *, tm, name):
    T, D = h.shape
    tm = min(tm, T)
    width = w_in_t.shape[0]

    def body(dp_ref, dh_ref, h_ref, gain_ref, wt_ref, dx_ref, df_ref, dgain_ref):
        @pl.when(pl.program_id(0) == 0)
        def _():
            dgain_ref[...] = jnp.zeros_like(dgain_ref)

        dx, dgain = _rms_bwd(h_ref[...], gain_ref[...], _mm(dp_ref[...], wt_ref[...]))
        dh_in = dh_ref[...] + dx
        dx_ref[...] = dh_in
        df_ref[...] = (0.5 * dh_in).astype(BF16)
        dgain_ref[...] += dgain

    row = lambda i: (i, 0)
    fixed = lambda i: (0, 0)
    return pl.pallas_call(
        body, name=name, grid=(T // tm,),
        in_specs=[
            pl.BlockSpec((tm, width), row),
            pl.BlockSpec((tm, D), row),
            pl.BlockSpec((tm, D), row),
            pl.BlockSpec((1, D), fixed),
            pl.BlockSpec((width, D), fixed),
        ],
        out_specs=[pl.BlockSpec((tm, D), row), pl.BlockSpec((tm, D), row), pl.BlockSpec((1, D), fixed)],
        out_shape=[jax.ShapeDtypeStruct((T, D), F32), jax.ShapeDtypeStruct((T, D), BF16),
                   jax.ShapeDtypeStruct((1, D), F32)],
        compiler_params=_params(("arbitrary",)),
    )(dproj, dh, h, gain, w_in_t)


def _window_sum(x, row, doublings, *, backward):
    T = x.shape[0]
    s = x
    for k in range(doublings):
        sh = 1 << k
        if backward:
            s = s + jnp.where(row < T - sh, pltpu.roll(s, T - sh, 0), 0.0)
        else:
            s = s + jnp.where(row >= sh, pltpu.roll(s, sh, 0), 0.0)
    return s


def _pool_fwd(proj, w_group, scale, *, name):
    T = proj.shape[0]

    def body(xp_ref, w_ref, scale_ref, p_ref):
        row = lax.broadcasted_iota(jnp.int32, (T, POOL_GROUP), 0)
        for gi, window in enumerate(POOL_WINDOWS):
            cols = slice(gi * POOL_GROUP, (gi + 1) * POOL_GROUP)
            x = xp_ref[:, cols]
            inv_count = 1.0 / jnp.minimum(row + 1, window).astype(F32)
            yc = _window_sum(x, row, gi + 1, backward=False) * inv_count - x
            pre = _mm(yc.astype(BF16), w_ref[gi].astype(BF16))
            p_ref[:, cols] = pre * scale_ref[:, cols]

    return pl.pallas_call(
        body, name=name, grid=(1,),
        in_specs=[
            pl.BlockSpec((T, POOL_WIDTH), lambda i: (0, 0)),
            pl.BlockSpec(w_group.shape, lambda i: (0, 0, 0)),
            pl.BlockSpec((1, POOL_WIDTH), lambda i: (0, 0)),
        ],
        out_specs=pl.BlockSpec((T, POOL_WIDTH), lambda i: (0, 0)),
        out_shape=jax.ShapeDtypeStruct((T, POOL_WIDTH), F32),
        compiler_params=_params(("arbitrary",)),
    )(proj, w_group, scale)


def _pool_bwd(dp, proj, w_group, scale, *, name):
    T = proj.shape[0]

    def body(dp_ref, xp_ref, w_ref, scale_ref, dxp_ref, dw_ref, dscale_ref):
        row = lax.broadcasted_iota(jnp.int32, (T, POOL_GROUP), 0)
        for gi, window in enumerate(POOL_WINDOWS):
            cols = slice(gi * POOL_GROUP, (gi + 1) * POOL_GROUP)
            x = xp_ref[:, cols]
            inv_count = 1.0 / jnp.minimum(row + 1, window).astype(F32)
            yc = (_window_sum(x, row, gi + 1, backward=False) * inv_count - x).astype(BF16)
            w = w_ref[gi].astype(BF16)
            pre = _mm(yc, w)
            dpg = dp_ref[:, cols]
            dscale_ref[:, cols] = jnp.sum(dpg * pre, axis=0, keepdims=True)
            dpre = (dpg * scale_ref[:, cols]).astype(BF16)
            dw_ref[gi] = _mm_tn(yc, dpre)
            dyc = _mm_nt(dpre, w)
            dxp_ref[:, cols] = _window_sum(dyc * inv_count, row, gi + 1, backward=True) - dyc

    return pl.pallas_call(
        body, name=name, grid=(1,),
        in_specs=[
            pl.BlockSpec((T, POOL_WIDTH), lambda i: (0, 0)),
            pl.BlockSpec((T, POOL_WIDTH), lambda i: (0, 0)),
            pl.BlockSpec(w_group.shape, lambda i: (0, 0, 0)),
            pl.BlockSpec((1, POOL_WIDTH), lambda i: (0, 0)),
        ],
        out_specs=[
            pl.BlockSpec((T, POOL_WIDTH), lambda i: (0, 0)),
            pl.BlockSpec(w_group.shape, lambda i: (0, 0, 0)),
            pl.BlockSpec((1, POOL_WIDTH), lambda i: (0, 0)),
        ],
        out_shape=[jax.ShapeDtypeStruct((T, POOL_WIDTH), F32), jax.ShapeDtypeStruct(w_group.shape, F32),
                   jax.ShapeDtypeStruct((1, POOL_WIDTH), F32)],
        compiler_params=_params(("arbitrary",)),
    )(dp, proj, w_group, scale)


ATTN_STRIP = 32


def _log_sigmoids(z):
    lb = jnp.minimum(z, 0.0) - jnp.log(1.0 + jnp.exp(-jnp.abs(z)))
    return lb, lb - z


def _transposed_blocks(x_ref, blocks_scr, tq):
    for b in range(blocks_scr.shape[0]):
        blocks_scr[b] = x_ref[b * tq:(b + 1) * tq, :].T.astype(BF16)


def _split_bf16(x):
    hi = x.astype(BF16)
    return hi, (x - hi.astype(F32)).astype(BF16)


def _strips(n):
    return [slice(i, i + ATTN_STRIP) for i in range(0, n, ATTN_STRIP)]


def _rows(parts):
    return jnp.concatenate(parts, axis=0)


def _attn_specs(T, tq):
    q_col = POOL_WIDTH // HEAD_PAIR
    k_col = q_col + SB_WIDTH // HEAD_PAIR
    v_col = k_col + SB_WIDTH // HEAD_PAIR
    return [
        pl.BlockSpec((tq, HEAD_PAIR), lambda p, i: (i, q_col + p)),
        pl.BlockSpec((T, HEAD_PAIR), lambda p, i: (0, k_col + p)),
        pl.BlockSpec((T, HEAD_PAIR), lambda p, i: (0, v_col + p)),
    ]


def _attn_fwd(proj, *, name):
    T = proj.shape[0]
    tk = min(ATTN_K_BLOCK, T)
    tq = min(ATTN_Q_BLOCK_FWD, T)
    diagonal_blocks = tq // tk

    def body(q_ref, k_ref, v_ref, o_ref, lt_ref, kt_scr, vb_scr):
        qi = pl.program_id(1)

        @pl.when(qi == 0)
        def _():
            _transposed_blocks(k_ref, kt_scr, tk)
            vb_scr[...] = v_ref[...].astype(BF16)

        head0 = lax.broadcasted_iota(jnp.int32, (tq, HEAD_PAIR), 1) < HEAD_DIM
        q = q_ref[...] * ATTN_SCALE
        qs = (jnp.where(head0, q, 0.0).astype(BF16), jnp.where(head0, 0.0, q).astype(BF16))
        r = lax.broadcasted_iota(jnp.int32, (tq, tk), 0)
        c = lax.broadcasted_iota(jnp.int32, (tq, tk), 1)
        later = (r[:tk] > c[:tk]).astype(BF16)
        later2 = _rows([later, later])
        causal = lambda d: (lambda rows: c[rows] + d * tk < r[rows])
        strips = _strips(tq)

        def log_terms(z, valid):
            lbs, his, los, sums = [], [], [], []
            for rows in strips:
                lb, lm = _log_sigmoids(z[rows])
                if valid is not None:
                    lm = jnp.where(valid(rows), lm, 0.0)
                hi, lo = _split_bf16(lm)
                lbs.append(lb)
                his.append(hi)
                los.append(lo)
                sums.append(jnp.sum(lm, axis=1, keepdims=True))
            return lbs, jnp.concatenate([_rows(his), _rows(los)], axis=1), _rows(sums)

        def weights(lbs, run, after, valid):
            parts = []
            for rows, lb in zip(strips, lbs):
                a = jnp.exp(lb + run[rows] + after[rows])
                if valid is not None:
                    a = jnp.where(valid(rows), a, 0.0)
                parts.append(a.astype(BF16))
            return _rows(parts)

        def block(kj, carry, valid):
            kt = kt_scr[kj]
            vb = vb_scr[pl.ds(pl.multiple_of(kj * tk, tk), tk), :]
            run0, o0, run1, o1 = carry
            z0 = _mm(qs[0], kt)
            z1 = _mm(qs[1], kt)
            lbs0, split0, sums0 = log_terms(z0, valid)
            after0 = _mm(split0, later2)
            lbs1, split1, sums1 = log_terms(z1, valid)
            after1 = _mm(split1, later2)
            o0 = o0 + _mm(weights(lbs0, run0, after0, valid), vb)
            o1 = o1 + _mm(weights(lbs1, run1, after1, valid), vb)
            return run0 + sums0, o0, run1 + sums1, o1

        zero = (jnp.zeros((tq, 1), F32), jnp.zeros((tq, HEAD_PAIR), F32))
        first = diagonal_blocks * qi
        carry = zero + zero
        for d in reversed(range(diagonal_blocks)):
            carry = block(first + d, carry, causal(d))
        carry = lax.fori_loop(0, first, lambda it, cr: block(first - 1 - it, cr, None), carry)
        o_ref[...] = jnp.where(head0, carry[1], carry[3])
        lt_ref[...] = jnp.where(head0, carry[0], carry[2])

    out_spec = pl.BlockSpec((tq, HEAD_PAIR), lambda p, i: (i, p))
    return pl.pallas_call(
        body, name=name, grid=(N_HEADS // 2, T // tq),
        in_specs=_attn_specs(T, tq), out_specs=[out_spec, out_spec],
        out_shape=[jax.ShapeDtypeStruct((T, SB_WIDTH), F32), jax.ShapeDtypeStruct((T, SB_WIDTH), F32)],
        scratch_shapes=[pltpu.VMEM((T // tk, HEAD_PAIR, tk), BF16), pltpu.VMEM((T, HEAD_PAIR), BF16)],
        compiler_params=_params(("arbitrary", "arbitrary")),
    )(proj, proj, proj)


def _attn_bwd(proj, do, ltot, after, *, name):
    T = proj.shape[0]
    tk = min(ATTN_K_BLOCK, T)
    tq = min(ATTN_Q_BLOCK_BWD, T)
    diagonal_blocks = tq // tk

    def body(q_ref, k_ref, v_ref, do_ref, lt_ref, after_ref, dq_ref, dkt_ref, dvt_ref, kb_scr, kt_scr, vt_scr):
        qi = pl.program_id(1)

        @pl.when(qi == 0)
        def _():
            kb_scr[...] = k_ref[...].astype(BF16)
            _transposed_blocks(k_ref, kt_scr, tk)
            _transposed_blocks(v_ref, vt_scr, tk)
            dkt_ref[...] = jnp.zeros_like(dkt_ref)
            dvt_ref[...] = jnp.zeros_like(dvt_ref)

        head0 = lax.broadcasted_iota(jnp.int32, (tq, HEAD_PAIR), 1) < HEAD_DIM
        q, do_, lt = q_ref[...] * ATTN_SCALE, do_ref[...], lt_ref[...]
        qs = (jnp.where(head0, q, 0.0).astype(BF16), jnp.where(head0, 0.0, q).astype(BF16))
        q_heads = (jnp.where(head0, q, 0.0), jnp.where(head0, 0.0, q))
        do_heads = (jnp.where(head0, do_, 0.0), jnp.where(head0, 0.0, do_))
        dos = tuple(d.astype(BF16) for d in do_heads)
        qts = tuple(x.T.astype(BF16) for x in q_heads)
        dots = tuple(d.T.astype(BF16) for d in do_heads)
        lts = (jnp.max(jnp.where(head0, lt, -jnp.inf), axis=1, keepdims=True),
               jnp.max(jnp.where(head0, -jnp.inf, lt), axis=1, keepdims=True))
        r = lax.broadcasted_iota(jnp.int32, (tq, tk), 0)
        c = lax.broadcasted_iota(jnp.int32, (tq, tk), 1)
        upto = (r[:tk] <= c[:tk]).astype(BF16)
        before = (r[:tk] < c[:tk]).astype(BF16)
        upto2, before2 = _rows([upto, upto]), _rows([before, before])
        causal = lambda d: (lambda rows: c[rows] + d * tk < r[rows])
        strips = _strips(tq)

        def log_terms(z, valid):
            lbs, his, los, sums = [], [], [], []
            for rows in strips:
                lb, lm = _log_sigmoids(z[rows])
                if valid is not None:
                    lm = jnp.where(valid(rows), lm, 0.0)
                hi, lo = _split_bf16(lm)
                lbs.append(lb)
                his.append(hi)
                los.append(lo)
                sums.append(jnp.sum(lm, axis=1, keepdims=True))
            return lbs, jnp.concatenate([_rows(his), _rows(los)], axis=1), _rows(sums)

        def weights(lbs, rest, lm_upto, da, valid):
            a_parts, es, his, los, sums = [], [], [], [], []
            for rows, lb in zip(strips, lbs):
                a = jnp.exp(lb + (rest[rows] - lm_upto[rows]))
                if valid is not None:
                    a = jnp.where(valid(rows), a, 0.0)
                e = da[rows] * a
                hi, lo = _split_bf16(e)
                a_parts.append(a.astype(BF16))
                es.append(e)
                his.append(hi)
                los.append(lo)
                sums.append(jnp.sum(e, axis=1, keepdims=True))
            return _rows(a_parts), es, jnp.concatenate([_rows(his), _rows(los)], axis=1), _rows(sums)

        def score_grads(lbs, es, run_e, e_before, valid):
            parts = []
            for rows, lb, e in zip(strips, lbs, es):
                beta = jnp.exp(lb)
                dz = e * (1.0 - beta) - (run_e[rows] + e_before[rows]) * beta
                if valid is not None:
                    dz = jnp.where(valid(rows), dz, 0.0)
                parts.append(dz.astype(BF16))
            return _rows(parts)

        def block(kj, carry, valid):
            off = pl.multiple_of(kj * tk, tk)
            kb, kt, vt = kb_scr[pl.ds(off, tk), :], kt_scr[kj], vt_scr[kj]
            run_lm0, run_e0, dq0, run_lm1, run_e1, dq1 = carry
            z0, da0 = _mm(qs[0], kt), _mm(dos[0], vt)
            z1, da1 = _mm(qs[1], kt), _mm(dos[1], vt)
            lbs0, split0, lm_sums0 = log_terms(z0, valid)
            lm_upto0 = _mm(split0, upto2)
            lbs1, split1, lm_sums1 = log_terms(z1, valid)
            lm_upto1 = _mm(split1, upto2)
            a0, es0, split0, e_sums0 = weights(lbs0, lts[0] - run_lm0, lm_upto0, da0, valid)
            e_before0 = _mm(split0, before2)
            a1, es1, split1, e_sums1 = weights(lbs1, lts[1] - run_lm1, lm_upto1, da1, valid)
            e_before1 = _mm(split1, before2)
            dz0 = score_grads(lbs0, es0, run_e0, e_before0, valid)
            dkt_blk = _mm(qts[0], dz0)
            dvt_blk = _mm(dots[0], a0)
            dq0 = dq0 + _mm(dz0, kb)
            dz1 = score_grads(lbs1, es1, run_e1, e_before1, valid)
            dkt_ref[kj] += dkt_blk + _mm(qts[1], dz1)
            dvt_ref[kj] += dvt_blk + _mm(dots[1], a1)
            dq1 = dq1 + _mm(dz1, kb)
            return run_lm0 + lm_sums0, run_e0 + e_sums0, dq0, run_lm1 + lm_sums1, run_e1 + e_sums1, dq1

        zero = (jnp.zeros((tq, 1), F32), jnp.zeros((tq, 1), F32), jnp.zeros((tq, HEAD_PAIR), F32))
        first = diagonal_blocks * qi
        carry = lax.fori_loop(0, first, lambda kj, cr: block(kj, cr, None), zero + zero)
        for d in range(diagonal_blocks):
            carry = block(first + d, carry, causal(d))
        dq_ref[...] = jnp.where(head0, carry[2], carry[5]) * ATTN_SCALE

    blk = pl.BlockSpec((tq, HEAD_PAIR), lambda p, i: (i, p))
    seq = pl.BlockSpec((T // tk, HEAD_PAIR, tk), lambda p, i: (0, p, 0))
    transposed = jax.ShapeDtypeStruct((T // tk, SB_WIDTH, tk), F32)
    return pl.pallas_call(
        body, name=name, grid=(N_HEADS // 2, T // tq),
        in_specs=_attn_specs(T, tq) + [blk, blk, AFTER], out_specs=[blk, seq, seq],
        out_shape=[jax.ShapeDtypeStruct((T, SB_WIDTH), F32), transposed, transposed],
        scratch_shapes=[pltpu.VMEM((T, HEAD_PAIR), BF16), pltpu.VMEM((T // tk, HEAD_PAIR, tk), BF16),
                        pltpu.VMEM((T // tk, HEAD_PAIR, tk), BF16)],
        compiler_params=_params(("arbitrary", "arbitrary")),
    )(proj, proj, proj, do, ltot, _in_hbm(after))


def _mix_specs(T, D, tm, wbp, w_out):
    gate_col = (POOL_WIDTH + 3 * SB_WIDTH) // D
    row = lambda i: (i, 0)
    return [
        pl.BlockSpec((tm, D), row),
        pl.BlockSpec((tm, POOL_WIDTH), row),
        pl.BlockSpec((tm, SB_WIDTH), row),
        pl.BlockSpec((tm, D), lambda i: (i, gate_col)),
        pl.BlockSpec((tm, D), lambda i: (i, gate_col + 1)),
        pl.BlockSpec(wbp.shape, lambda i: (0, 0)),
        pl.BlockSpec(wbp.shape, lambda i: (0, 0)),
        pl.BlockSpec(w_out.shape, lambda i: (0, 0)),
    ]


def _mix_fwd(h, p, o, proj, wbp, wba, w_out, *, tm, name):
    T, D = h.shape
    tm = min(tm, T)

    def body(h_ref, p_ref, o_ref, glp_ref, gls_ref, wbp_ref, wba_ref, wout_ref, hout_ref, m_ref):
        yp = _mm_nt(p_ref[...].astype(BF16), wbp_ref[...])
        ys = _mm_nt(o_ref[...].astype(BF16), wba_ref[...])
        m = (jax.nn.sigmoid(glp_ref[...]) * yp + jax.nn.sigmoid(gls_ref[...]) * ys).astype(BF16)
        m_ref[...] = m
        hout_ref[...] = h_ref[...] + _mm(m, wout_ref[...])

    row = lambda i: (i, 0)
    return pl.pallas_call(
        body, name=name, grid=(T // tm,),
        in_specs=_mix_specs(T, D, tm, wbp, w_out),
        out_specs=[pl.BlockSpec((tm, D), row), pl.BlockSpec((tm, D), row)],
        out_shape=[jax.ShapeDtypeStruct((T, D), F32), jax.ShapeDtypeStruct((T, D), BF16)],
        compiler_params=_params(("arbitrary",)),
    )(h, p, o, proj, proj, wbp, wba, w_out)


def _mix_bwd(dh, p, o, proj, wbp, wba, w_out, after, *, tm, name):
    T, D = dh.shape
    tm = min(tm, T)

    def body(dh_ref, p_ref, o_ref, glp_ref, gls_ref, wbp_ref, wba_ref, wout_ref, after_ref,
             dyp_ref, dys_ref, dp_ref, do_ref, dgl_ref):
        dm = _mm_nt(dh_ref[...].astype(BF16), wout_ref[...])
        yp = _mm_nt(p_ref[...].astype(BF16), wbp_ref[...])
        ys = _mm_nt(o_ref[...].astype(BF16), wba_ref[...])
        gp = jax.nn.sigmoid(glp_ref[...])
        gs = jax.nn.sigmoid(gls_ref[...])
        dyp = (dm * gp).astype(BF16)
        dys = (dm * gs).astype(BF16)
        dyp_ref[...] = dyp
        dys_ref[...] = dys
        dgl_ref[:, :D] = (dm * yp * gp * (1.0 - gp)).astype(BF16)
        dgl_ref[:, D:] = (dm * ys * gs * (1.0 - gs)).astype(BF16)
        dp_ref[...] = _mm(dyp, wbp_ref[...])
        do_ref[...] = _mm(dys, wba_ref[...])

    row = lambda i: (i, 0)
    return pl.pallas_call(
        body, name=name, grid=(T // tm,),
        in_specs=_mix_specs(T, D, tm, wbp, w_out) + [AFTER],
        out_specs=[pl.BlockSpec((tm, D), row), pl.BlockSpec((tm, D), row), pl.BlockSpec((tm, POOL_WIDTH), row),
                   pl.BlockSpec((tm, SB_WIDTH), row), pl.BlockSpec((tm, 2 * D), row)],
        out_shape=[jax.ShapeDtypeStruct((T, D), BF16), jax.ShapeDtypeStruct((T, D), BF16),
                   jax.ShapeDtypeStruct((T, POOL_WIDTH), F32), jax.ShapeDtypeStruct((T, SB_WIDTH), F32),
                   jax.ShapeDtypeStruct((T, 2 * D), BF16)],
        compiler_params=_params(("arbitrary",)),
    )(dh, p, o, proj, proj, wbp, wba, w_out, _in_hbm(after))


def _adamw(w, g, m, v, *, name):
    R, C = w.shape
    tr = _row_tile(R, C)

    def body(w_ref, g_ref, m_ref, v_ref, d_ref, nm_ref, nv_ref):
        g_ = g_ref[...]
        m_ = ADAM_B1 * m_ref[...] + (1.0 - ADAM_B1) * g_
        v_ = ADAM_B2 * v_ref[...] + (1.0 - ADAM_B2) * (g_ * g_)
        m_hat = m_ / (1.0 - ADAM_B1 ** ADAM_STEP)
        v_hat = v_ / (1.0 - ADAM_B2 ** ADAM_STEP)
        d_ref[...] = -ADAM_LR * (m_hat / (jnp.sqrt(v_hat) + ADAM_EPS) + ADAM_WD * w_ref[...])
        nm_ref[...] = m_
        nv_ref[...] = v_

    spec = pl.BlockSpec((tr, C), lambda i: (i, 0))
    return pl.pallas_call(
        body, name=name, grid=(R // tr,), in_specs=[spec] * 4, out_specs=[spec] * 3,
        out_shape=[jax.ShapeDtypeStruct((R, C), F32)] * 3,
        compiler_params=_params(("arbitrary",)),
    )(w, g, m, v)


def _position():
    return lax.axis_index("x"), lax.axis_index("y"), lax.axis_index("c")


def _all_gather(shards, *, name, collective_id):
    n = len(shards)
    n_copies = 9

    def body(*refs):
        ins, outs = refs[:n], refs[n:2 * n]
        send_sems, recv_sems, local_sems = refs[2 * n:]
        x, y, c = _position()
        me, sibling = (x, y, c), (x, y, 1 - c)
        x_nbr, y_nbr, diagonal = (1 - x, y, c), (x, 1 - y, c), (1 - x, 1 - y, c)
        other = lambda pos: (pos[0], pos[1], 1 - c)

        barrier = pltpu.get_barrier_semaphore()
        for peer in (sibling, x_nbr, y_nbr):
            pl.semaphore_signal(barrier, inc=1, device_id=peer, device_id_type=MESH)
        pl.semaphore_wait(barrier, 3)

        def block(a, pos, half=None):
            ref = outs[a].at[4 * pos[0] + 2 * pos[1] + pos[2]]
            rows = ref.shape[0] // 2
            return ref if half is None else ref.at[pl.ds(half * rows, rows)]

        def copy(a, k, pos, to, half=None, src=None):
            return pltpu.make_async_remote_copy(
                src_ref=block(a, pos, half) if src is None else src, dst_ref=block(a, pos, half),
                send_sem=send_sems.at[n_copies * a + k], recv_sem=recv_sems.at[n_copies * a + k],
                device_id=to, device_id_type=MESH)

        started = []
        for a in range(n):
            mine = pltpu.make_async_copy(ins[a], block(a, me), local_sems.at[a])
            mine.start()
            started.append(mine)
        sends = []
        for a in range(n):
            sends += [copy(a, 1, me, x_nbr, src=ins[a]), copy(a, 2, me, y_nbr, src=ins[a]),
                      copy(a, 0, me, sibling, src=ins[a])]
        for cp in sends:
            cp.start()

        def pass_on(copies):
            for cp in copies:
                cp.start()
                sends.append(cp)

        for a in range(n):
            copy(a, 1, x_nbr, me).wait_recv()
            pass_on([copy(a, 5, x_nbr, y_nbr, half=0), copy(a, 3, x_nbr, sibling)])
            copy(a, 2, y_nbr, me).wait_recv()
            pass_on([copy(a, 6, y_nbr, x_nbr, half=1), copy(a, 4, y_nbr, sibling)])
        for a in range(n):
            copy(a, 5, diagonal, me, half=0).wait_recv()
            pass_on([copy(a, 7, diagonal, sibling, half=0)])
            copy(a, 6, diagonal, me, half=1).wait_recv()
            pass_on([copy(a, 8, diagonal, sibling, half=1)])
        for a in range(n):
            copy(a, 0, sibling, me).wait_recv()
            copy(a, 3, other(x_nbr), me).wait_recv()
            copy(a, 4, other(y_nbr), me).wait_recv()
            copy(a, 7, other(diagonal), me, half=0).wait_recv()
            copy(a, 8, other(diagonal), me, half=1).wait_recv()
        for cp in sends:
            cp.wait_send()
        for cp in started:
            cp.wait()

    return pl.kernel(
        body, name=name,
        out_type=[jax.ShapeDtypeStruct((N_DEV,) + s.shape, s.dtype) for s in shards],
        mesh=plsc.ScalarSubcoreMesh(axis_name="sequencer", num_cores=1),
        scratch_types=[pltpu.SemaphoreType.DMA((n_copies * n,)), pltpu.SemaphoreType.DMA((n_copies * n,)),
                       pltpu.SemaphoreType.DMA((n,))],
        compiler_params=pltpu.CompilerParams(collective_id=collective_id),
    )(*shards)


def _chip_sums(grads, *, name):
    _, R, C = grads.shape
    rc = 128 if R % 128 == 0 else R

    def body(g_ref, partial, out_ref, mine, theirs, send_sems, recv_sems, local_sems):
        x, y, c = _position()
        my_chip = 2 * x + y

        def swap(s):
            return pltpu.make_async_remote_copy(
                src_ref=g_ref.at[2 * s + (1 - c)], dst_ref=theirs.at[s],
                send_sem=send_sems.at[s], recv_sem=recv_sems.at[s],
                device_id=(x, y, 1 - c), device_id_type=MESH)

        def load(s):
            return pltpu.make_async_copy(g_ref.at[2 * s + c], mine.at[s], local_sems.at[s])

        for s in range(4):
            swap(s).start()
            load(s).start()
        for s in range(4):
            load(s).wait()
            swap(s).wait_recv()

        def chip_sum(chip, rows):
            return mine[chip, rows, :].astype(F32) + theirs[chip, rows, :].astype(F32)

        for j in (1, 2, 3):
            @pl.loop(0, R // rc)
            def _(t):
                rows = pl.ds(pl.multiple_of(t * rc, rc), rc)
                partial[j - 1, rows, :] = chip_sum(my_chip ^ j, rows).astype(BF16)

        @pl.loop(0, R // rc)
        def _(t):
            rows = pl.ds(pl.multiple_of(t * rc, rc), rc)
            out_ref[rows, :] = chip_sum(my_chip, rows)

        for s in range(4):
            swap(s).wait_send()

    vmem = pl.BlockSpec(memory_space=pltpu.VMEM)
    return pl.pallas_call(
        body, name=name,
        in_specs=[pl.BlockSpec(memory_space=pl.ANY)], out_specs=[vmem, vmem],
        out_shape=[jax.ShapeDtypeStruct((3, R, C), BF16), jax.ShapeDtypeStruct((R, C), F32)],
        scratch_shapes=[
            pltpu.VMEM((4, R, C), BF16), pltpu.VMEM((4, R, C), BF16),
            pltpu.SemaphoreType.DMA((4,)), pltpu.SemaphoreType.DMA((4,)), pltpu.SemaphoreType.DMA((4,)),
        ],
        compiler_params=_params(),
    )(grads)


def _cross_chips(partials, *, name, collective_id):
    n = len(partials)

    def body(*refs):
        ins, outs = refs[:n], refs[n:2 * n]
        send_sems, recv_sems = refs[2 * n:]
        x, y, c = _position()
        my_chip = 2 * x + y
        peers = [((my_chip ^ j) // 2, (my_chip ^ j) % 2, c) for j in (1, 2, 3)]

        barrier = pltpu.get_barrier_semaphore()
        for peer in peers:
            pl.semaphore_signal(barrier, inc=1, device_id=peer, device_id_type=MESH)
        pl.semaphore_wait(barrier, 3)

        copies = [
            pltpu.make_async_remote_copy(
                src_ref=ins[a].at[j], dst_ref=outs[a].at[j],
                send_sem=send_sems.at[3 * a + j], recv_sem=recv_sems.at[3 * a + j],
                device_id=peers[j], device_id_type=MESH)
            for a in range(n) for j in range(3)]
        for cp in copies:
            cp.start()
        for cp in copies:
            cp.wait_recv()
        for cp in copies:
            cp.wait_send()

    return pl.kernel(
        body, name=name,
        out_type=[jax.ShapeDtypeStruct(p.shape, p.dtype) for p in partials],
        mesh=plsc.ScalarSubcoreMesh(axis_name="sequencer", num_cores=1),
        scratch_types=[pltpu.SemaphoreType.DMA((3 * n,)), pltpu.SemaphoreType.DMA((3 * n,))],
        compiler_params=pltpu.CompilerParams(collective_id=collective_id),
    )(*partials)


def _cross_chips_and_gather(partial, slab, *, name, collective_id):
    def body(part_ref, slab_ref, landed_ref, slabs_ref, send_sems, recv_sems, local_sem):
        x, y, c = _position()
        me, my_chip = 4 * x + 2 * y + c, 2 * x + y
        others = [me ^ k for k in range(1, N_DEV)]
        ids = [(o // 4, (o // 2) % 2, o % 2) for o in others]

        barrier = pltpu.get_barrier_semaphore()
        for peer in ids:
            pl.semaphore_signal(barrier, inc=1, device_id=peer, device_id_type=MESH)
        pl.semaphore_wait(barrier, N_DEV - 1)

        mine = pltpu.make_async_copy(slab_ref, slabs_ref.at[me], local_sem)
        mine.start()
        sends = [
            pltpu.make_async_remote_copy(
                src_ref=part_ref.at[j], dst_ref=landed_ref.at[j], send_sem=send_sems.at[j], recv_sem=recv_sems.at[j],
                device_id=((my_chip ^ (j + 1)) // 2, (my_chip ^ (j + 1)) % 2, c), device_id_type=MESH)
            for j in range(3)]
        sends += [
            pltpu.make_async_remote_copy(
                src_ref=slab_ref, dst_ref=slabs_ref.at[me], send_sem=send_sems.at[3 + k], recv_sem=recv_sems.at[3 + k],
                device_id=ids[k], device_id_type=MESH)
            for k in range(N_DEV - 1)]
        arrivals = sends[:3] + [
            pltpu.make_async_remote_copy(
                src_ref=slab_ref, dst_ref=slabs_ref.at[others[k]], send_sem=send_sems.at[3 + k],
                recv_sem=recv_sems.at[3 + k], device_id=ids[k], device_id_type=MESH)
            for k in range(N_DEV - 1)]
        for cp in sends:
            cp.start()
        for cp in arrivals:
            cp.wait_recv()
        for cp in sends:
            cp.wait_send()
        mine.wait()

    n_sems = 3 + N_DEV - 1
    return pl.kernel(
        body, name=name,
        out_type=[jax.ShapeDtypeStruct(partial.shape, partial.dtype),
                  jax.ShapeDtypeStruct((N_DEV,) + slab.shape, slab.dtype)],
        mesh=plsc.ScalarSubcoreMesh(axis_name="sequencer", num_cores=1),
        scratch_types=[pltpu.SemaphoreType.DMA((n_sems,)), pltpu.SemaphoreType.DMA((n_sems,)), pltpu.SemaphoreType.DMA],
        compiler_params=pltpu.CompilerParams(collective_id=collective_id),
    )(partial, slab)


def _sum_devices(gathered, after, *, name):
    _, R, C = gathered.shape

    def body(in_ref, after_ref, out_ref):
        total = in_ref[0]
        for d in range(1, N_DEV):
            total = total + in_ref[d]
        out_ref[...] = total

    return pl.pallas_call(
        body, name=name, grid=(1,),
        in_specs=[pl.BlockSpec((N_DEV, R, C), lambda i: (0, 0, 0)), AFTER],
        out_specs=pl.BlockSpec((R, C), lambda i: (0, 0)),
        out_shape=jax.ShapeDtypeStruct((R, C), F32),
        compiler_params=_params(("arbitrary",)),
    )(gathered, _in_hbm(after))


def _owner_sum(own, landed, after, *, name):
    R, C = own.shape
    tr = _row_tile(R, C)

    def body(own_ref, landed_ref, after_ref, out_ref):
        total = own_ref[...]
        for j in range(3):
            total = total + landed_ref[j].astype(F32)
        out_ref[...] = total

    return pl.pallas_call(
        body, name=name, grid=(R // tr,),
        in_specs=[pl.BlockSpec((tr, C), lambda i: (i, 0)), pl.BlockSpec((3, tr, C), lambda i: (0, i, 0)), AFTER],
        out_specs=pl.BlockSpec((tr, C), lambda i: (i, 0)),
        out_shape=jax.ShapeDtypeStruct((R, C), F32),
        compiler_params=_params(("arbitrary",)),
    )(own, landed, _in_hbm(after))


def _local_step(x, target, norms, pool_w_group, pool_scale, wgu1, wd1, w_in, wbp, wba, w_out, wgu2, wd2, exchange):
    n1g, nmg, n2g, nfg = norms
    D = x.shape[1]
    gu1, hid1 = _ffn_up(x, n1g, wgu1, tm=1024, name="ffn1_up")
    h1 = _ffn_down(x, hid1, wd1, tm=512, name="ffn1_down")
    un, proj = _inproj_fwd(h1, nmg, w_in, tm=1024, name="inproj_fwd")
    p = _pool_fwd(proj, pool_w_group, pool_scale, name="pool_fwd")
    o, ltot = _attn_fwd(proj, name="attn_fwd")
    h2, m = _mix_fwd(h1, p, o, proj, wbp, wba, w_out, tm=256, name="mix_fwd")
    gu2, hid2 = _ffn_up(h2, n2g, wgu2, tm=1024, name="ffn2_up")
    h3 = _ffn_down(h2, hid2, wd2, tm=512, name="ffn2_down")
    dh3, df2, loss, d_nf = _loss_bwd(h3, target, nfg, tm=256, name="loss_bwd")

    d_wd2 = _wgrad_down(hid2, df2, tk=WGRAD_TOKENS, name="ffn2_wgrad_down")
    (g_wd2,), token = exchange("ffn2_down", [d_wd2.reshape(N_DEV, FF_SHARD_PAD, D)])
    dh2, d_n2, n2, dgu2 = _ffn_bwd(dh3, df2, h2, n2g, gu2, wgu2, wd2, token, tm=512, name="ffn2_bwd")
    d_wgu2 = _wgrad_gate_up(n2, dgu2, tk=WGRAD_TOKENS, name="ffn2_wgrad_gate_up")
    (g_wgu2,), token = exchange("ffn2_gate_up", [d_wgu2])

    dyp, dys, dp, do, dgl = _mix_bwd(dh2, p, o, proj, wbp, wba, w_out, token, tm=256, name="mix_bwd")
    d_wout = _wgrad_full(m, dh2, tk=WGRAD_TOKENS, name="wgrad_out")
    d_wbp = _wgrad_full(dyp, p, tk=WGRAD_TOKENS, name="wgrad_branch_pool")
    d_wba = _wgrad_full(dys, o, tk=WGRAD_TOKENS, name="wgrad_branch_attn")
    by_owner = lambda g: g.reshape(N_DEV, g.shape[0] // N_DEV, g.shape[1])
    (g_wbp, g_wba, g_wout), token = exchange("mix", [by_owner(d_wbp), by_owner(d_wba), by_owner(d_wout)])
    dxp, d_wgroup, d_scale = _pool_bwd(dp, proj, pool_w_group, pool_scale, name="pool_bwd")
    dq, dkt, dvt = _attn_bwd(proj, do, ltot, token, name="attn_bwd")
    dk, dv = (t.transpose(0, 2, 1).reshape(dq.shape) for t in (dkt, dvt))
    dproj = jnp.concatenate([dxp.astype(BF16), dq.astype(BF16), dk.astype(BF16), dv.astype(BF16), dgl], axis=1)
    d_win = _wgrad_in(dproj, un, tk=WGRAD_TOKENS, name="wgrad_in")
    (g_win,), token_in = exchange("w_in", [d_win])
    dh1, df1, d_nm = _inproj_bwd(dproj, dh2, h1, nmg, w_in, tm=512, name="inproj_bwd")
    d_wd1 = _wgrad_down(hid1, df1, tk=WGRAD_TOKENS, name="ffn1_wgrad_down")
    (g_wd1,), token_down = exchange("ffn1_down", [d_wd1.reshape(N_DEV, FF_SHARD_PAD, D)])
    token = (token_down[(0,) * token_down.ndim] + token_in[(0,) * token_in.ndim]).reshape(1, 1)

    dx, d_n1, n1, dgu1 = _ffn_bwd(dh1, df1, x, n1g, gu1, wgu1, wd1, token, tm=512, name="ffn1_bwd")
    d_wgu1_a = _wgrad_gate_up(n1, dgu1, tk=WGRAD_TOKENS, name="ffn1_wgrad_gate_up_a", part=0, parts=2)
    (g_wgu1_a,), token = exchange("ffn1_gate_up_a", [d_wgu1_a])
    d_wgu1_b = _wgrad_gate_up(n1, dgu1, tk=WGRAD_TOKENS, name="ffn1_wgrad_gate_up_b", part=1, parts=2)
    (g_wgu1_b, replicated), token = exchange("last", [d_wgu1_b, d_n1, d_nm, d_n2, d_nf, d_scale, d_wgroup, loss])
    g_wgu1 = (g_wgu1_a, g_wgu1_b)

    sharded = (g_wgu1, g_wd1, g_win, g_wbp, g_wba, g_wout, g_wgu2, g_wd2)
    return dx, sharded, replicated, token


def _hidden_major(w):
    return jnp.swapaxes(w[0], 0, 1)


def _pad_gate_up(wt):
    d = wt.shape[1]
    wt = wt.astype(BF16).reshape(2, FF_SHARD, d)
    return jnp.pad(wt, ((0, 0), (0, FF_SHARD_PAD - FF_SHARD), (0, 0))).reshape(2 * FF_SHARD_PAD, d)


def _unpad_gate_up(gt):
    d = gt.shape[1]
    return gt.reshape(2, FF_SHARD_PAD, d)[:, :FF_SHARD].reshape(2 * FF_SHARD, d)


def _pad_down(w):
    return jnp.pad(w.astype(BF16), ((0, FF_SHARD_PAD - FF_SHARD), (0, 0)))


def kernel(x, ffn1_norm, ffn1_w_gate_up, ffn1_w_down, mix_norm, w_in, pool_w_group, pool_scale, w_branch_pool, w_branch_attn, w_out, ffn2_norm, ffn2_w_gate_up, ffn2_w_down, final_norm, loss_target, m_ffn1_norm, m_ffn1_w_gate_up, m_ffn1_w_down, m_mix_norm, m_w_in, m_pool_w_group, m_pool_scale, m_w_branch_pool, m_w_branch_attn, m_w_out, m_ffn2_norm, m_ffn2_w_gate_up, m_ffn2_w_down, m_final_norm, v_ffn1_norm, v_ffn1_w_gate_up, v_ffn1_w_down, v_mix_norm, v_w_in, v_pool_w_group, v_pool_scale, v_w_branch_pool, v_w_branch_attn, v_w_out, v_ffn2_norm, v_ffn2_w_gate_up, v_ffn2_w_down, v_final_norm):
    D = x.shape[-1]
    weights = dict(ffn1_norm=ffn1_norm, ffn1_w_gate_up=ffn1_w_gate_up, ffn1_w_down=ffn1_w_down, mix_norm=mix_norm,
                   w_in=w_in, pool_w_group=pool_w_group, pool_scale=pool_scale, w_branch_pool=w_branch_pool,
                   w_branch_attn=w_branch_attn, w_out=w_out, ffn2_norm=ffn2_norm, ffn2_w_gate_up=ffn2_w_gate_up,
                   ffn2_w_down=ffn2_w_down, final_norm=final_norm)
    first = dict(ffn1_norm=m_ffn1_norm, ffn1_w_gate_up=m_ffn1_w_gate_up, ffn1_w_down=m_ffn1_w_down,
                 mix_norm=m_mix_norm, w_in=m_w_in, pool_w_group=m_pool_w_group, pool_scale=m_pool_scale,
                 w_branch_pool=m_w_branch_pool, w_branch_attn=m_w_branch_attn, w_out=m_w_out,
                 ffn2_norm=m_ffn2_norm, ffn2_w_gate_up=m_ffn2_w_gate_up, ffn2_w_down=m_ffn2_w_down,
                 final_norm=m_final_norm)
    second = dict(ffn1_norm=v_ffn1_norm, ffn1_w_gate_up=v_ffn1_w_gate_up, ffn1_w_down=v_ffn1_w_down,
                  mix_norm=v_mix_norm, w_in=v_w_in, pool_w_group=v_pool_w_group, pool_scale=v_pool_scale,
                  w_branch_pool=v_w_branch_pool, w_branch_attn=v_w_branch_attn, w_out=v_w_out,
                  ffn2_norm=v_ffn2_norm, ffn2_w_gate_up=v_ffn2_w_gate_up, ffn2_w_down=v_ffn2_w_down,
                  final_norm=v_final_norm)
    order = list(weights)

    wgu1, = _all_gather([_pad_gate_up(_hidden_major(ffn1_w_gate_up))], name="all_gather_ffn1_gate_up", collective_id=0)
    wd1, = _all_gather([_pad_down(ffn1_w_down[0])], name="all_gather_ffn1_down", collective_id=10)
    transposed = lambda w: jnp.swapaxes(w[0], 0, 1).astype(BF16)
    win_g, = _all_gather([transposed(w_in)], name="all_gather_w_in", collective_id=1)
    wbp_g, wba_g = _all_gather([transposed(w_branch_pool), transposed(w_branch_attn)],
                               name="all_gather_branches", collective_id=2)
    wout_g, = _all_gather([w_out[0].astype(BF16)], name="all_gather_w_out", collective_id=11)
    wgu2, wd2 = _all_gather([_pad_gate_up(_hidden_major(ffn2_w_gate_up)), _pad_down(ffn2_w_down[0])],
                            name="all_gather_ffn2", collective_id=3)
    whole = lambda g: g.reshape(g.shape[0] * g.shape[1], g.shape[2])
    wd1, wd2, win_g, wbp_g, wba_g, wout_g = (whole(g) for g in (wd1, wd2, win_g, wbp_g, wba_g, wout_g))

    cross_ids = {"ffn2_down": 4, "ffn2_gate_up": 5, "mix": 6, "ffn1_down": 7, "w_in": 8, "last": 9,
                 "ffn1_gate_up_a": 12}
    small = ["ffn1_norm", "mix_norm", "ffn2_norm", "final_norm", "pool_scale", "pool_w_group"]

    def tile_rows(a):
        a = a.reshape(-1, 128)
        return jnp.pad(a, ((0, -a.shape[0] % 8), (0, 0)))

    def exchange(tag, group):
        if tag == "last":
            slab = jnp.concatenate([tile_rows(g) for g in group[1:-1]] + [jnp.broadcast_to(group[-1], (8, 128))], axis=0)
            partial, own = _chip_sums(group[0], name="chip_sums_last")
            landed, slabs = _cross_chips_and_gather(partial, slab, name="cross_chips_last", collective_id=cross_ids[tag])
            return [(own, landed), slabs], own
        sums = [_chip_sums(g, name=f"chip_sums_{tag}_{i}") for i, g in enumerate(group)]
        landed = _cross_chips([s[0] for s in sums], name="cross_chips_" + tag, collective_id=cross_ids[tag])
        token = sums[0][1] if len(sums) == 1 else sum(s[1][0, 0] for s in sums).reshape(1, 1)
        return [(s[1], l) for s, l in zip(sums, landed)], token

    norms = (ffn1_norm, mix_norm, ffn2_norm, final_norm.reshape(1, D))
    dx, sharded, slabs, last = _local_step(
        x[0], loss_target[0], norms, pool_w_group[0], pool_scale, wgu1, wd1, win_g, wbp_g, wba_g, wout_g, wgu2, wd2,
        exchange)
    names = ["ffn1_w_gate_up", "ffn1_w_down", "w_in", "w_branch_pool", "w_branch_attn", "w_out",
             "ffn2_w_gate_up", "ffn2_w_down"]
    handles = dict(zip(names, sharded))
    grads, delta, new_m, new_v = {}, {}, {}, {}
    after = last
    for k in ("ffn2_w_down", "ffn2_w_gate_up", "w_branch_pool", "w_branch_attn", "w_out", "w_in", "ffn1_w_down",
              "ffn1_w_gate_up"):
        hidden_major = k.endswith("w_gate_up")
        if isinstance(handles[k][0], tuple):
            first_half = _owner_sum(*handles[k][0], after, name="owner_sum_" + k + "_a")
            second_half = _owner_sum(*handles[k][1], first_half, name="owner_sum_" + k + "_b")
            g = jnp.concatenate([first_half[:FF_SHARD], second_half[:FF_SHARD]], axis=0)
        else:
            g = _owner_sum(*handles[k], after, name="owner_sum_" + k)
            if hidden_major:
                g = _unpad_gate_up(g)
            elif k in ("w_in", "w_branch_pool", "w_branch_attn"):
                g = jnp.swapaxes(g, 0, 1)
            else:
                g = g[:weights[k].shape[1]]
        view = _hidden_major if hidden_major else (lambda a: a[0])
        back = (lambda a: jnp.swapaxes(a, 0, 1)[None]) if hidden_major else (lambda a: a[None])
        out = _adamw(view(weights[k]), g, view(first[k]), view(second[k]), name="adamw_" + k)
        after = out[0]
        grads[k] = back(g)
        delta[k], new_m[k], new_v[k] = (back(a) for a in out)

    rows = [weights[k].size // 128 for k in small]
    padded_rows = [-(-r // 8) * 8 for r in rows]
    starts = [sum(padded_rows[:i]) for i in range(len(rows) + 1)]
    total = _sum_devices(slabs, after, name="sum_replicated")
    loss_out = total[starts[-1], 0]
    small_w = jnp.concatenate([tile_rows(weights[k]) for k in small], axis=0)
    small_m = jnp.concatenate([tile_rows(first[k]) for k in small], axis=0)
    small_v = jnp.concatenate([tile_rows(second[k]) for k in small], axis=0)
    small_out = _adamw(small_w, total[:starts[-1]], small_m, small_v, name="adamw_replicated")
    for name_, start, n_rows in zip(small, starts, rows):
        shape = weights[name_].shape
        grads[name_] = total[start:start + n_rows].reshape(shape)
        delta[name_], new_m[name_], new_v[name_] = (a[start:start + n_rows].reshape(shape) for a in small_out)

    return (loss_out, dx[None], *[grads[k] for k in order], *[delta[k] for k in order],
            *[new_m[k] for k in order], *[new_v[k] for k in order])
```

```python
import functools

import jax
import jax.numpy as jnp
from jax import lax
from jax.experimental import pallas as pl
from jax.experimental.pallas import tpu as pltpu
from jax.experimental.pallas import tpu_sc as plsc

F32 = jnp.float32
BF16 = jnp.bfloat16
MESH = pl.DeviceIdType.MESH

RMS_EPS = 1e-6
N_DEV = 8
N_HEADS = 8
HEAD_DIM = 64
HEAD_PAIR = 2 * HEAD_DIM
POOL_WINDOWS = (2, 4, 8, 16)
POOL_GROUP = 128
POOL_WIDTH = 512
SB_WIDTH = 512
FF_SHARD = 352
FF_SHARD_PAD = 384
ATTN_K_BLOCK = 256
ATTN_Q_BLOCK_FWD = 512
ATTN_Q_BLOCK_BWD = 256
ATTN_SCALE = 0.125

ADAM_LR = 0.001
ADAM_B1 = 0.9
ADAM_B2 = 0.999
ADAM_EPS = 1e-08
ADAM_WD = 0.01
ADAM_STEP = 10

VMEM_LIMIT = 48 << 20
WGRAD_TOKENS = 2048


def _params(dims=None):
    return pltpu.CompilerParams(dimension_semantics=dims, vmem_limit_bytes=VMEM_LIMIT)


def _mm(a, b):
    return jnp.dot(a, b, preferred_element_type=F32)


def _mm_nt(a, b):
    return lax.dot_general(a, b, (((1,), (1,)), ((), ())), preferred_element_type=F32)


def _mm_tn(a, b):
    return lax.dot_general(a, b, (((0,), (0,)), ((), ())), preferred_element_type=F32)


def _row_tile(rows, cols):
    limit = max(8, (512 * 1024) // cols)
    return max(t for t in range(8, rows + 1, 8) if rows % t == 0 and (t <= limit or t == 8))


def _rstd(xf):
    return lax.rsqrt(jnp.mean(xf * xf, axis=-1, keepdims=True) + RMS_EPS)


def _rms_bwd(xf, gain, dn):
    r = _rstd(xf)
    xh = xf * r
    dgain = jnp.sum(dn * xh, axis=0, keepdims=True)
    dxh = dn * gain
    dx = r * (dxh - xh * jnp.mean(dxh * xh, axis=-1, keepdims=True))
    return dx, dgain


def _ffn_up(x, gain, wgu, *, tm, name):
    T, D = x.shape
    tm = min(tm, T)
    nb, bw = wgu.shape[0] // 2, wgu.shape[1]

    def body(x_ref, gain_ref, wg_ref, wu_ref, gu_ref, hid_ref, n_scr):
        @pl.when(pl.program_id(1) == 0)
        def _():
            xf = x_ref[...]
            n_scr[...] = (xf * _rstd(xf) * gain_ref[...]).astype(BF16)

        halves = (pl.ds(0, tm // 2), pl.ds(tm // 2, tm // 2))
        wg, wu = wg_ref[...], wu_ref[...]
        gus = [(_mm_nt(n_scr[rows, :], wg), _mm_nt(n_scr[rows, :], wu)) for rows in halves]
        for rows, (g, u) in zip(halves, gus):
            gu_ref[0, rows, :] = g.astype(BF16)
            gu_ref[1, rows, :] = u.astype(BF16)
            hid_ref[rows, :] = (g * jax.nn.sigmoid(g) * u).astype(BF16)

    return pl.pallas_call(
        body, name=name, grid=(T // tm, nb),
        in_specs=[
            pl.BlockSpec((tm, D), lambda i, j: (i, 0)),
            pl.BlockSpec((1, D), lambda i, j: (0, 0)),
            pl.BlockSpec((None, bw, D), lambda i, j: (j, 0, 0)),
            pl.BlockSpec((None, bw, D), lambda i, j: (j + nb, 0, 0)),
        ],
        out_specs=[
            pl.BlockSpec((2, tm, bw), lambda i, j: (0, i, j)),
            pl.BlockSpec((tm, bw), lambda i, j: (i, j)),
        ],
        out_shape=[jax.ShapeDtypeStruct((2, T, nb * bw), BF16), jax.ShapeDtypeStruct((T, nb * bw), BF16)],
        scratch_shapes=[pltpu.VMEM((tm, D), BF16)],
        compiler_params=_params(("arbitrary", "arbitrary")),
    )(x, gain, wgu, wgu)


def _ffn_down(x, hid, wd, *, tm, name):
    T, D = x.shape
    tm = min(tm, T)
    F = hid.shape[1]

    def body(x_ref, hid_ref, wd_ref, h_ref):
        h_ref[...] = x_ref[...] + 0.5 * _mm(hid_ref[...], wd_ref[...])

    return pl.pallas_call(
        body, name=name, grid=(T // tm,),
        in_specs=[
            pl.BlockSpec((tm, D), lambda i: (i, 0)),
            pl.BlockSpec((tm, F), lambda i: (i, 0)),
            pl.BlockSpec((F, D), lambda i: (0, 0)),
        ],
        out_specs=pl.BlockSpec((tm, D), lambda i: (i, 0)),
        out_shape=jax.ShapeDtypeStruct((T, D), F32),
        compiler_params=_params(("arbitrary",)),
    )(x, hid, wd)


AFTER = pl.BlockSpec(memory_space=pltpu.HBM)


def _in_hbm(token):
    return pltpu.with_memory_space_constraint(token, pltpu.HBM)


def _ffn_bwd(dh, df, x, gain, gu, wgu, wd, after, *, tm, name):
    T, D = x.shape
    tm = min(tm, T)
    nb, bw = wgu.shape[0] // 2, wgu.shape[1]

    def body(dh_ref, df_ref, x_ref, gain_ref, gu_ref, wg_ref, wu_ref, wd_ref, after_ref,
             dx_ref, dgain_ref, n_ref, dgu_ref, dn_acc):
        i, j = pl.program_id(0), pl.program_id(1)

        @pl.when(j == 0)
        def _():
            xf = x_ref[...]
            n_ref[...] = (xf * _rstd(xf) * gain_ref[...]).astype(BF16)
            dn_acc[...] = jnp.zeros_like(dn_acc)

        @pl.when((i == 0) & (j == 0))
        def _():
            dgain_ref[...] = jnp.zeros_like(dgain_ref)

        halves = (pl.ds(0, tm // 2), pl.ds(tm // 2, tm // 2))
        wd, wg, wu = wd_ref[...], wg_ref[...], wu_ref[...]
        dhids = [_mm_nt(df_ref[rows, :], wd) for rows in halves]
        for rows, dhid in zip(halves, dhids):
            g = gu_ref[0, rows, :].astype(F32)
            u = gu_ref[1, rows, :].astype(F32)
            s = jax.nn.sigmoid(g)
            silu = g * s
            dg = (dhid * u * (s * (1.0 + g * (1.0 - s)))).astype(BF16)
            du = (dhid * silu).astype(BF16)
            dgu_ref[0, rows, :] = dg
            dgu_ref[1, rows, :] = du
            dn_acc[rows, :] += _mm(dg, wg) + _mm(du, wu)

        @pl.when(j == nb - 1)
        def _():
            dx, dgain = _rms_bwd(x_ref[...], gain_ref[...], dn_acc[...])
            dx_ref[...] = dh_ref[...] + dx
            dgain_ref[...] += dgain

    row = lambda i, j: (i, 0)
    return pl.pallas_call(
        body, name=name, grid=(T // tm, nb),
        in_specs=[
            pl.BlockSpec((tm, D), row),
            pl.BlockSpec((tm, D), row),
            pl.BlockSpec((tm, D), row),
            pl.BlockSpec((1, D), lambda i, j: (0, 0)),
            pl.BlockSpec((2, tm, bw), lambda i, j: (0, i, j)),
            pl.BlockSpec((None, bw, D), lambda i, j: (j, 0, 0)),
            pl.BlockSpec((None, bw, D), lambda i, j: (j + nb, 0, 0)),
            pl.BlockSpec((bw, D), lambda i, j: (j, 0)),
            AFTER,
        ],
        out_specs=[
            pl.BlockSpec((tm, D), row),
            pl.BlockSpec((1, D), lambda i, j: (0, 0)),
            pl.BlockSpec((tm, D), row),
            pl.BlockSpec((2, tm, bw), lambda i, j: (0, i, j)),
        ],
        out_shape=[
            jax.ShapeDtypeStruct((T, D), F32),
            jax.ShapeDtypeStruct((1, D), F32),
            jax.ShapeDtypeStruct((T, D), BF16),
            jax.ShapeDtypeStruct((2, T, nb * bw), BF16),
        ],
        scratch_shapes=[pltpu.VMEM((tm, D), F32)],
        compiler_params=_params(("arbitrary", "arbitrary")),
    )(dh, df, x, gain, gu, wgu, wgu, wd, _in_hbm(after))


def _wgrad(a, b, *, grid, a_spec, b_spec, out_spec, out_shape, acc_shape, name):
    nk = grid[2]

    def body(a_ref, b_ref, o_ref, acc):
        k = pl.program_id(2)

        @pl.when(k == 0)
        def _():
            acc[...] = jnp.zeros_like(acc)

        acc[...] += _mm_tn(a_ref[...].astype(BF16), b_ref[...].astype(BF16))

        @pl.when(k == nk - 1)
        def _():
            o_ref[...] = acc[...].astype(o_ref.dtype)

    return pl.pallas_call(
        body, name=name, grid=grid, in_specs=[a_spec, b_spec], out_specs=out_spec,
        out_shape=jax.ShapeDtypeStruct(out_shape, BF16),
        scratch_shapes=[pltpu.VMEM(acc_shape, F32)],
        compiler_params=_params(("arbitrary", "arbitrary", "arbitrary")),
    )(a, b)


def _wgrad_gate_up(n, dgu, *, tk, name, part=0, parts=1):
    T, D = n.shape
    tk = min(tk, T)
    owner_rows = FF_SHARD_PAD * 2
    nb = dgu.shape[2] // owner_rows
    bw = owner_rows // parts
    return _wgrad(
        dgu, n, grid=(2 * nb, 1, T // tk), name=name,
        a_spec=pl.BlockSpec((None, tk, bw), lambda m, c, k: (m // nb, k, parts * (m % nb) + part)),
        b_spec=pl.BlockSpec((tk, D), lambda m, c, k: (k, 0)),
        out_spec=pl.BlockSpec((None, bw, D), lambda m, c, k: (m, 0, 0)),
        out_shape=(2 * nb, bw, D), acc_shape=(bw, D))


def _wgrad_down(hid, df, *, tk, name):
    T, D = df.shape
    tk = min(tk, T)
    bw = FF_SHARD_PAD * 2
    nb = hid.shape[1] // bw
    return _wgrad(
        hid, df, grid=(nb, 1, T // tk), name=name,
        a_spec=pl.BlockSpec((tk, bw), lambda m, c, k: (k, m)),
        b_spec=pl.BlockSpec((tk, D), lambda m, c, k: (k, 0)),
        out_spec=pl.BlockSpec((bw, D), lambda m, c, k: (m, 0)),
        out_shape=(nb * bw, D), acc_shape=(bw, D))


def _wgrad_in(dproj, un, *, tk, name):
    T, D = un.shape
    tk = min(tk, T)
    bw = dproj.shape[1] // N_DEV
    return _wgrad(
        dproj, un, grid=(N_DEV, 1, T // tk), name=name,
        a_spec=pl.BlockSpec((tk, bw), lambda m, c, k: (k, m)),
        b_spec=pl.BlockSpec((tk, D), lambda m, c, k: (k, 0)),
        out_spec=pl.BlockSpec((None, bw, D), lambda m, c, k: (m, 0, 0)),
        out_shape=(N_DEV, bw, D), acc_shape=(bw, D))


def _wgrad_full(a, b, *, tk, name):
    T, M = a.shape
    tk = min(tk, T)
    N = b.shape[1]
    return _wgrad(
        a, b, grid=(1, 1, T // tk), name=name,
        a_spec=pl.BlockSpec((tk, M), lambda m, c, k: (k, 0)),
        b_spec=pl.BlockSpec((tk, N), lambda m, c, k: (k, 0)),
        out_spec=pl.BlockSpec((M, N), lambda m, c, k: (0, 0)), out_shape=(M, N), acc_shape=(M, N))


def _loss_bwd(h, target, gain, *, tm, name):
    T, D = h.shape
    tm = min(tm, T)

    def body(h_ref, t_ref, gain_ref, dh_ref, df_ref, loss_ref, dgain_ref):
        @pl.when(pl.program_id(0) == 0)
        def _():
            loss_ref[...] = jnp.zeros_like(loss_ref)
            dgain_ref[...] = jnp.zeros_like(dgain_ref)

        xf = h_ref[...]
        gain = gain_ref[...]
        err = xf * _rstd(xf) * gain - t_ref[...]
        loss_ref[...] += 0.5 * jnp.sum(jnp.mean(err * err, axis=-1, keepdims=True), axis=0, keepdims=True)
        dx, dgain = _rms_bwd(xf, gain, err * (1.0 / D))
        dh_ref[...] = dx
        df_ref[...] = (0.5 * dx).astype(BF16)
        dgain_ref[...] += dgain

    row = lambda i: (i, 0)
    fixed = lambda i: (0, 0)
    return pl.pallas_call(
        body, name=name, grid=(T // tm,),
        in_specs=[pl.BlockSpec((tm, D), row), pl.BlockSpec((tm, D), row), pl.BlockSpec((1, D), fixed)],
        out_specs=[pl.BlockSpec((tm, D), row), pl.BlockSpec((tm, D), row), pl.BlockSpec((1, 128), fixed),
                   pl.BlockSpec((1, D), fixed)],
        out_shape=[jax.ShapeDtypeStruct((T, D), F32), jax.ShapeDtypeStruct((T, D), BF16),
                   jax.ShapeDtypeStruct((1, 128), F32), jax.ShapeDtypeStruct((1, D), F32)],
        compiler_params=_params(("arbitrary",)),
    )(h, target, gain)


def _inproj_fwd(h, gain, w_in_t, *, tm, name):
    T, D = h.shape
    tm = min(tm, T)
    bn = D
    nb = w_in_t.shape[0] // bn

    def body(h_ref, gain_ref, wt_ref, un_ref, proj_ref):
        @pl.when(pl.program_id(1) == 0)
        def _():
            xf = h_ref[...]
            un_ref[...] = (xf * _rstd(xf) * gain_ref[...]).astype(BF16)

        proj_ref[...] = _mm_nt(un_ref[...], wt_ref[...])

    return pl.pallas_call(
        body, name=name, grid=(T // tm, nb),
        in_specs=[
            pl.BlockSpec((tm, D), lambda i, j: (i, 0)),
            pl.BlockSpec((1, D), lambda i, j: (0, 0)),
            pl.BlockSpec((bn, D), lambda i, j: (j, 0)),
        ],
        out_specs=[pl.BlockSpec((tm, D), lambda i, j: (i, 0)), pl.BlockSpec((tm, bn), lambda i, j: (i, j))],
        out_shape=[jax.ShapeDtypeStruct((T, D), BF16), jax.ShapeDtypeStruct((T, nb * bn), F32)],
        compiler_params=_params(("arbitrary", "arbitrary")),
    )(h, gain, w_in_t)


def _inproj_bwd(dproj, dh, h, gain, w_in_t, *, tm, name):
    T, D = h.shape
    tm = min(tm, T)
    width = w_in_t.shape[0]

    def body(dp_ref, dh_ref, h_ref, gain_ref, wt_ref, dx_ref, df_ref, dgain_ref):
        @pl.when(pl.program_id(0) == 0)
        def _():
            dgain_ref[...] = jnp.zeros_like(dgain_ref)

        dx, dgain = _rms_bwd(h_ref[...], gain_ref[...], _mm(dp_ref[...], wt_ref[...]))
        dh_in = dh_ref[...] + dx
        dx_ref[...] = dh_in
        df_ref[...] = (0.5 * dh_in).astype(BF16)
        dgain_ref[...] += dgain

    row = lambda i: (i, 0)
    fixed = lambda i: (0, 0)
    return pl.pallas_call(
        body, name=name, grid=(T // tm,),
        in_specs=[
            pl.BlockSpec((tm, width), row),
            pl.BlockSpec((tm, D), row),
            pl.BlockSpec((tm, D), row),
            pl.BlockSpec((1, D), fixed),
            pl.BlockSpec((width, D), fixed),
        ],
        out_specs=[pl.BlockSpec((tm, D), row), pl.BlockSpec((tm, D), row), pl.BlockSpec((1, D), fixed)],
        out_shape=[jax.ShapeDtypeStruct((T, D), F32), jax.ShapeDtypeStruct((T, D), BF16),
                   jax.ShapeDtypeStruct((1, D), F32)],
        compiler_params=_params(("arbitrary",)),
    )(dproj, dh, h, gain, w_in_t)


def _window_sum(x, row, doublings, *, backward):
    T = x.shape[0]
    s = x
    for k in range(doublings):
        sh = 1 << k
        if backward:
            s = s + jnp.where(row < T - sh, pltpu.roll(s, T - sh, 0), 0.0)
        else:
            s = s + jnp.where(row >= sh, pltpu.roll(s, sh, 0), 0.0)
    return s


def _pool_fwd(proj, w_group, scale, *, name):
    T = proj.shape[0]

    def body(xp_ref, w_ref, scale_ref, p_ref):
        row = lax.broadcasted_iota(jnp.int32, (T, POOL_GROUP), 0)
        for gi, window in enumerate(POOL_WINDOWS):
            cols = slice(gi * POOL_GROUP, (gi + 1) * POOL_GROUP)
            x = xp_ref[:, cols]
            inv_count = 1.0 / jnp.minimum(row + 1, window).astype(F32)
            yc = _window_sum(x, row, gi + 1, backward=False) * inv_count - x
            pre = _mm(yc.astype(BF16), w_ref[gi].astype(BF16))
            p_ref[:, cols] = pre * scale_ref[:, cols]

    return pl.pallas_call(
        body, name=name, grid=(1,),
        in_specs=[
            pl.BlockSpec((T, POOL_WIDTH), lambda i: (0, 0)),
            pl.BlockSpec(w_group.shape, lambda i: (0, 0, 0)),
            pl.BlockSpec((1, POOL_WIDTH), lambda i: (0, 0)),
        ],
        out_specs=pl.BlockSpec((T, POOL_WIDTH), lambda i: (0, 0)),
        out_shape=jax.ShapeDtypeStruct((T, POOL_WIDTH), F32),
        compiler_params=_params(("arbitrary",)),
    )(proj, w_group, scale)


def _pool_bwd(dp, proj, w_group, scale, *, name):
    T = proj.shape[0]

    def body(dp_ref, xp_ref, w_ref, scale_ref, dxp_ref, dw_ref, dscale_ref):
        row = lax.broadcasted_iota(jnp.int32, (T, POOL_GROUP), 0)
        for gi, window in enumerate(POOL_WINDOWS):
            cols = slice(gi * POOL_GROUP, (gi + 1) * POOL_GROUP)
            x = xp_ref[:, cols]
            inv_count = 1.0 / jnp.minimum(row + 1, window).astype(F32)
            yc = (_window_sum(x, row, gi + 1, backward=False) * inv_count - x).astype(BF16)
            w = w_ref[gi].astype(BF16)
            pre = _mm(yc, w)
            dpg = dp_ref[:, cols]
            dscale_ref[:, cols] = jnp.sum(dpg * pre, axis=0, keepdims=True)
            dpre = (dpg * scale_ref[:, cols]).astype(BF16)
            dw_ref[gi] = _mm_tn(yc, dpre)
            dyc = _mm_nt(dpre, w)
            dxp_ref[:, cols] = (_window_sum(dyc * inv_count, row, gi + 1, backward=True) - dyc).astype(BF16)

    return pl.pallas_call(
        body, name=name, grid=(1,),
        in_specs=[
            pl.BlockSpec((T, POOL_WIDTH), lambda i: (0, 0)),
            pl.BlockSpec((T, POOL_WIDTH), lambda i: (0, 0)),
            pl.BlockSpec(w_group.shape, lambda i: (0, 0, 0)),
            pl.BlockSpec((1, POOL_WIDTH), lambda i: (0, 0)),
        ],
        out_specs=[
            pl.BlockSpec((T, POOL_WIDTH), lambda i: (0, 0)),
            pl.BlockSpec(w_group.shape, lambda i: (0, 0, 0)),
            pl.BlockSpec((1, POOL_WIDTH), lambda i: (0, 0)),
        ],
        out_shape=[jax.ShapeDtypeStruct((T, POOL_WIDTH), BF16), jax.ShapeDtypeStruct(w_group.shape, F32),
                   jax.ShapeDtypeStruct((1, POOL_WIDTH), F32)],
        compiler_params=_params(("arbitrary",)),
    )(dp, proj, w_group, scale)


ATTN_STRIP = 32


def _log_sigmoids(z):
    lb = jnp.minimum(z, 0.0) - jnp.log(1.0 + jnp.exp(-jnp.abs(z)))
    return lb, lb - z


def _transposed_blocks(x_ref, blocks_scr, tq):
    for b in range(blocks_scr.shape[0]):
        blocks_scr[b] = x_ref[b * tq:(b + 1) * tq, :].T.astype(BF16)


def _split_bf16(x):
    hi = x.astype(BF16)
    return hi, (x - hi.astype(F32)).astype(BF16)


def _strips(n):
    return [slice(i, i + ATTN_STRIP) for i in range(0, n, ATTN_STRIP)]


def _rows(parts):
    return jnp.concatenate(parts, axis=0)


def _attn_specs(T, tq):
    q_col = POOL_WIDTH // HEAD_PAIR
    k_col = q_col + SB_WIDTH // HEAD_PAIR
    v_col = k_col + SB_WIDTH // HEAD_PAIR
    return [
        pl.BlockSpec((tq, HEAD_PAIR), lambda p, i: (i, q_col + p)),
        pl.BlockSpec((T, HEAD_PAIR), lambda p, i: (0, k_col + p)),
        pl.BlockSpec((T, HEAD_PAIR), lambda p, i: (0, v_col + p)),
    ]


def _attn_fwd(proj, *, name):
    T = proj.shape[0]
    tk = min(ATTN_K_BLOCK, T)
    tq = min(ATTN_Q_BLOCK_FWD, T)
    diagonal_blocks = tq // tk

    def body(q_ref, k_ref, v_ref, o_ref, lt_ref, kt_scr, vb_scr):
        qi = pl.program_id(1)

        @pl.when(qi == 0)
        def _():
            _transposed_blocks(k_ref, kt_scr, tk)
            vb_scr[...] = v_ref[...].astype(BF16)

        head0 = lax.broadcasted_iota(jnp.int32, (tq, HEAD_PAIR), 1) < HEAD_DIM
        q = q_ref[...] * ATTN_SCALE
        qs = (jnp.where(head0, q, 0.0).astype(BF16), jnp.where(head0, 0.0, q).astype(BF16))
        r = lax.broadcasted_iota(jnp.int32, (tq, tk), 0)
        c = lax.broadcasted_iota(jnp.int32, (tq, tk), 1)
        later = (r[:tk] > c[:tk]).astype(BF16)
        later2 = _rows([later, later])
        causal = lambda d: (lambda rows: c[rows] + d * tk < r[rows])
        strips = _strips(tq)

        def log_terms(z, valid):
            lbs, his, los, sums = [], [], [], []
            for rows in strips:
                lb, lm = _log_sigmoids(z[rows])
                if valid is not None:
                    lm = jnp.where(valid(rows), lm, 0.0)
                hi, lo = _split_bf16(lm)
                lbs.append(lb)
                his.append(hi)
                los.append(lo)
                sums.append(jnp.sum(lm, axis=1, keepdims=True))
            return lbs, jnp.concatenate([_rows(his), _rows(los)], axis=1), _rows(sums)

        def weights(lbs, run, after, valid):
            parts = []
            for rows, lb in zip(strips, lbs):
                a = jnp.exp(lb + run[rows] + after[rows])
                if valid is not None:
                    a = jnp.where(valid(rows), a, 0.0)
                parts.append(a.astype(BF16))
            return _rows(parts)

        def block(kj, carry, valid):
            kt = kt_scr[kj]
            vb = vb_scr[pl.ds(pl.multiple_of(kj * tk, tk), tk), :]
            run0, o0, run1, o1 = carry
            z0 = _mm(qs[0], kt)
            z1 = _mm(qs[1], kt)
            lbs0, split0, sums0 = log_terms(z0, valid)
            after0 = _mm(split0, later2)
            lbs1, split1, sums1 = log_terms(z1, valid)
            after1 = _mm(split1, later2)
            o0 = o0 + _mm(weights(lbs0, run0, after0, valid), vb)
            o1 = o1 + _mm(weights(lbs1, run1, after1, valid), vb)
            return run0 + sums0, o0, run1 + sums1, o1

        zero = (jnp.zeros((tq, 1), F32), jnp.zeros((tq, HEAD_PAIR), F32))
        first = diagonal_blocks * qi
        carry = zero + zero
        for d in reversed(range(diagonal_blocks)):
            carry = block(first + d, carry, causal(d))
        carry = lax.fori_loop(0, first, lambda it, cr: block(first - 1 - it, cr, None), carry)
        o_ref[...] = jnp.where(head0, carry[1], carry[3])
        lt_ref[...] = jnp.where(head0, carry[0], carry[2])

    out_spec = pl.BlockSpec((tq, HEAD_PAIR), lambda p, i: (i, p))
    return pl.pallas_call(
        body, name=name, grid=(N_HEADS // 2, T // tq),
        in_specs=_attn_specs(T, tq), out_specs=[out_spec, out_spec],
        out_shape=[jax.ShapeDtypeStruct((T, SB_WIDTH), F32), jax.ShapeDtypeStruct((T, SB_WIDTH), F32)],
        scratch_shapes=[pltpu.VMEM((T // tk, HEAD_PAIR, tk), BF16), pltpu.VMEM((T, HEAD_PAIR), BF16)],
        compiler_params=_params(("arbitrary", "arbitrary")),
    )(proj, proj, proj)


def _attn_bwd(proj, do, ltot, after, *, name):
    T = proj.shape[0]
    tk = min(ATTN_K_BLOCK, T)
    tq = min(ATTN_Q_BLOCK_BWD, T)
    diagonal_blocks = tq // tk

    def body(q_ref, k_ref, v_ref, do_ref, lt_ref, after_ref, dq_ref, dk_ref, dv_ref,
             kb_scr, kt_scr, vt_scr, dkt_ref, dvt_ref):
        qi = pl.program_id(1)

        @pl.when(qi == 0)
        def _():
            kb_scr[...] = k_ref[...].astype(BF16)
            _transposed_blocks(k_ref, kt_scr, tk)
            _transposed_blocks(v_ref, vt_scr, tk)
            dkt_ref[...] = jnp.zeros_like(dkt_ref)
            dvt_ref[...] = jnp.zeros_like(dvt_ref)

        head0 = lax.broadcasted_iota(jnp.int32, (tq, HEAD_PAIR), 1) < HEAD_DIM
        q, do_, lt = q_ref[...] * ATTN_SCALE, do_ref[...], lt_ref[...]
        qs = (jnp.where(head0, q, 0.0).astype(BF16), jnp.where(head0, 0.0, q).astype(BF16))
        q_heads = (jnp.where(head0, q, 0.0), jnp.where(head0, 0.0, q))
        do_heads = (jnp.where(head0, do_, 0.0), jnp.where(head0, 0.0, do_))
        dos = tuple(d.astype(BF16) for d in do_heads)
        qts = tuple(x.T.astype(BF16) for x in q_heads)
        dots = tuple(d.T.astype(BF16) for d in do_heads)
        lts = (jnp.max(jnp.where(head0, lt, -jnp.inf), axis=1, keepdims=True),
               jnp.max(jnp.where(head0, -jnp.inf, lt), axis=1, keepdims=True))
        r = lax.broadcasted_iota(jnp.int32, (tq, tk), 0)
        c = lax.broadcasted_iota(jnp.int32, (tq, tk), 1)
        upto = (r[:tk] <= c[:tk]).astype(BF16)
        before = (r[:tk] < c[:tk]).astype(BF16)
        upto2, before2 = _rows([upto, upto]), _rows([before, before])
        causal = lambda d: (lambda rows: c[rows] + d * tk < r[rows])
        strips = _strips(tq)

        def log_terms(z, valid):
            lbs, his, los, sums = [], [], [], []
            for rows in strips:
                lb, lm = _log_sigmoids(z[rows])
                if valid is not None:
                    lm = jnp.where(valid(rows), lm, 0.0)
                hi, lo = _split_bf16(lm)
                lbs.append(lb)
                his.append(hi)
                los.append(lo)
                sums.append(jnp.sum(lm, axis=1, keepdims=True))
            return lbs, jnp.concatenate([_rows(his), _rows(los)], axis=1), _rows(sums)

        def weights(lbs, rest, lm_upto, da, valid):
            a_parts, es, his, los, sums = [], [], [], [], []
            for rows, lb in zip(strips, lbs):
                a = jnp.exp(lb + (rest[rows] - lm_upto[rows]))
                if valid is not None:
                    a = jnp.where(valid(rows), a, 0.0)
                e = da[rows] * a
                hi, lo = _split_bf16(e)
                a_parts.append(a.astype(BF16))
                es.append(e)
                his.append(hi)
                los.append(lo)
                sums.append(jnp.sum(e, axis=1, keepdims=True))
            return _rows(a_parts), es, jnp.concatenate([_rows(his), _rows(los)], axis=1), _rows(sums)

        def score_grads(lbs, es, run_e, e_before, valid):
            parts = []
            for rows, lb, e in zip(strips, lbs, es):
                beta = jnp.exp(lb)
                dz = e * (1.0 - beta) - (run_e[rows] + e_before[rows]) * beta
                if valid is not None:
                    dz = jnp.where(valid(rows), dz, 0.0)
                parts.append(dz.astype(BF16))
            return _rows(parts)

        def block(kj, carry, valid):
            off = pl.multiple_of(kj * tk, tk)
            kb, kt, vt = kb_scr[pl.ds(off, tk), :], kt_scr[kj], vt_scr[kj]
            run_lm0, run_e0, dq0, run_lm1, run_e1, dq1 = carry
            z0, da0 = _mm(qs[0], kt), _mm(dos[0], vt)
            z1, da1 = _mm(qs[1], kt), _mm(dos[1], vt)
            lbs0, split0, lm_sums0 = log_terms(z0, valid)
            lm_upto0 = _mm(split0, upto2)
            lbs1, split1, lm_sums1 = log_terms(z1, valid)
            lm_upto1 = _mm(split1, upto2)
            a0, es0, split0, e_sums0 = weights(lbs0, lts[0] - run_lm0, lm_upto0, da0, valid)
            e_before0 = _mm(split0, before2)
            a1, es1, split1, e_sums1 = weights(lbs1, lts[1] - run_lm1, lm_upto1, da1, valid)
            e_before1 = _mm(split1, before2)
            dz0 = score_grads(lbs0, es0, run_e0, e_before0, valid)
            dkt_blk = _mm(qts[0], dz0)
            dvt_blk = _mm(dots[0], a0)
            dq0 = dq0 + _mm(dz0, kb)
            dz1 = score_grads(lbs1, es1, run_e1, e_before1, valid)
            dkt_ref[kj] += dkt_blk + _mm(qts[1], dz1)
            dvt_ref[kj] += dvt_blk + _mm(dots[1], a1)
            dq1 = dq1 + _mm(dz1, kb)
            return run_lm0 + lm_sums0, run_e0 + e_sums0, dq0, run_lm1 + lm_sums1, run_e1 + e_sums1, dq1

        zero = (jnp.zeros((tq, 1), F32), jnp.zeros((tq, 1), F32), jnp.zeros((tq, HEAD_PAIR), F32))
        first = diagonal_blocks * qi
        carry = lax.fori_loop(0, first, lambda kj, cr: block(kj, cr, None), zero + zero)
        for d in range(diagonal_blocks):
            carry = block(first + d, carry, causal(d))
        dq_ref[...] = (jnp.where(head0, carry[2], carry[5]) * ATTN_SCALE).astype(BF16)

        @pl.when(qi == T // tq - 1)
        def _():
            for b in range(T // tk):
                dk_ref[b * tk:(b + 1) * tk, :] = dkt_ref[b].T.astype(BF16)
                dv_ref[b * tk:(b + 1) * tk, :] = dvt_ref[b].T.astype(BF16)

    blk = pl.BlockSpec((tq, HEAD_PAIR), lambda p, i: (i, p))
    seq = pl.BlockSpec((T, HEAD_PAIR), lambda p, i: (0, p))
    transposed = pltpu.VMEM((T // tk, HEAD_PAIR, tk), F32)
    return pl.pallas_call(
        body, name=name, grid=(N_HEADS // 2, T // tq),
        in_specs=_attn_specs(T, tq) + [blk, blk, AFTER], out_specs=[blk, seq, seq],
        out_shape=[jax.ShapeDtypeStruct((T, SB_WIDTH), BF16)] * 3,
        scratch_shapes=[pltpu.VMEM((T, HEAD_PAIR), BF16), pltpu.VMEM((T // tk, HEAD_PAIR, tk), BF16),
                        pltpu.VMEM((T // tk, HEAD_PAIR, tk), BF16), transposed, transposed],
        compiler_params=_params(("arbitrary", "arbitrary")),
    )(proj, proj, proj, do, ltot, _in_hbm(after))


def _mix_specs(T, D, tm, wbp, w_out):
    gate_col = (POOL_WIDTH + 3 * SB_WIDTH) // D
    row = lambda i: (i, 0)
    return [
        pl.BlockSpec((tm, D), row),
        pl.BlockSpec((tm, POOL_WIDTH), row),
        pl.BlockSpec((tm, SB_WIDTH), row),
        pl.BlockSpec((tm, D), lambda i: (i, gate_col)),
        pl.BlockSpec((tm, D), lambda i: (i, gate_col + 1)),
        pl.BlockSpec(wbp.shape, lambda i: (0, 0)),
        pl.BlockSpec(wbp.shape, lambda i: (0, 0)),
        pl.BlockSpec(w_out.shape, lambda i: (0, 0)),
    ]


def _mix_fwd(h, p, o, proj, wbp, wba, w_out, *, tm, name):
    T, D = h.shape
    tm = min(tm, T)

    def body(h_ref, p_ref, o_ref, glp_ref, gls_ref, wbp_ref, wba_ref, wout_ref, hout_ref, m_ref):
        yp = _mm_nt(p_ref[...].astype(BF16), wbp_ref[...])
        ys = _mm_nt(o_ref[...].astype(BF16), wba_ref[...])
        m = (jax.nn.sigmoid(glp_ref[...]) * yp + jax.nn.sigmoid(gls_ref[...]) * ys).astype(BF16)
        m_ref[...] = m
        hout_ref[...] = h_ref[...] + _mm(m, wout_ref[...])

    row = lambda i: (i, 0)
    return pl.pallas_call(
        body, name=name, grid=(T // tm,),
        in_specs=_mix_specs(T, D, tm, wbp, w_out),
        out_specs=[pl.BlockSpec((tm, D), row), pl.BlockSpec((tm, D), row)],
        out_shape=[jax.ShapeDtypeStruct((T, D), F32), jax.ShapeDtypeStruct((T, D), BF16)],
        compiler_params=_params(("arbitrary",)),
    )(h, p, o, proj, proj, wbp, wba, w_out)


def _mix_bwd(dh, p, o, proj, wbp, wba, w_out, after, *, tm, name):
    T, D = dh.shape
    tm = min(tm, T)

    def body(dh_ref, p_ref, o_ref, glp_ref, gls_ref, wbp_ref, wba_ref, wout_ref, after_ref,
             dyp_ref, dys_ref, dp_ref, do_ref, dgl_ref):
        dm = _mm_nt(dh_ref[...].astype(BF16), wout_ref[...])
        yp = _mm_nt(p_ref[...].astype(BF16), wbp_ref[...])
        ys = _mm_nt(o_ref[...].astype(BF16), wba_ref[...])
        gp = jax.nn.sigmoid(glp_ref[...])
        gs = jax.nn.sigmoid(gls_ref[...])
        dyp = (dm * gp).astype(BF16)
        dys = (dm * gs).astype(BF16)
        dyp_ref[...] = dyp
        dys_ref[...] = dys
        dgl_ref[:, :D] = (dm * yp * gp * (1.0 - gp)).astype(BF16)
        dgl_ref[:, D:] = (dm * ys * gs * (1.0 - gs)).astype(BF16)
        dp_ref[...] = _mm(dyp, wbp_ref[...])
        do_ref[...] = _mm(dys, wba_ref[...])

    row = lambda i: (i, 0)
    return pl.pallas_call(
        body, name=name, grid=(T // tm,),
        in_specs=_mix_specs(T, D, tm, wbp, w_out) + [AFTER],
        out_specs=[pl.BlockSpec((tm, D), row), pl.BlockSpec((tm, D), row), pl.BlockSpec((tm, POOL_WIDTH), row),
                   pl.BlockSpec((tm, SB_WIDTH), row), pl.BlockSpec((tm, 2 * D), row)],
        out_shape=[jax.ShapeDtypeStruct((T, D), BF16), jax.ShapeDtypeStruct((T, D), BF16),
                   jax.ShapeDtypeStruct((T, POOL_WIDTH), F32), jax.ShapeDtypeStruct((T, SB_WIDTH), F32),
                   jax.ShapeDtypeStruct((T, 2 * D), BF16)],
        compiler_params=_params(("arbitrary",)),
    )(dh, p, o, proj, proj, wbp, wba, w_out, _in_hbm(after))


def _adamw(w, g, m, v, *, name):
    R, C = w.shape
    tr = _row_tile(R, C)

    def body(w_ref, g_ref, m_ref, v_ref, d_ref, nm_ref, nv_ref):
        g_ = g_ref[...]
        m_ = ADAM_B1 * m_ref[...] + (1.0 - ADAM_B1) * g_
        v_ = ADAM_B2 * v_ref[...] + (1.0 - ADAM_B2) * (g_ * g_)
        m_hat = m_ / (1.0 - ADAM_B1 ** ADAM_STEP)
        v_hat = v_ / (1.0 - ADAM_B2 ** ADAM_STEP)
        d_ref[...] = -ADAM_LR * (m_hat / (jnp.sqrt(v_hat) + ADAM_EPS) + ADAM_WD * w_ref[...])
        nm_ref[...] = m_
        nv_ref[...] = v_

    spec = pl.BlockSpec((tr, C), lambda i: (i, 0))
    return pl.pallas_call(
        body, name=name, grid=(R // tr,), in_specs=[spec] * 4, out_specs=[spec] * 3,
        out_shape=[jax.ShapeDtypeStruct((R, C), F32)] * 3,
        compiler_params=_params(("arbitrary",)),
    )(w, g, m, v)


def _position():
    return lax.axis_index("x"), lax.axis_index("y"), lax.axis_index("c")


def _all_gather(shards, *, name, collective_id):
    n = len(shards)
    n_copies = 9

    def body(*refs):
        ins, outs = refs[:n], refs[n:2 * n]
        send_sems, recv_sems, local_sems = refs[2 * n:]
        x, y, c = _position()
        me, sibling = (x, y, c), (x, y, 1 - c)
        x_nbr, y_nbr, diagonal = (1 - x, y, c), (x, 1 - y, c), (1 - x, 1 - y, c)
        other = lambda pos: (pos[0], pos[1], 1 - c)

        barrier = pltpu.get_barrier_semaphore()
        for peer in (sibling, x_nbr, y_nbr):
            pl.semaphore_signal(barrier, inc=1, device_id=peer, device_id_type=MESH)
        pl.semaphore_wait(barrier, 3)

        def block(a, pos, half=None):
            ref = outs[a].at[4 * pos[0] + 2 * pos[1] + pos[2]]
            rows = ref.shape[0] // 2
            return ref if half is None else ref.at[pl.ds(half * rows, rows)]

        def copy(a, k, pos, to, half=None, src=None):
            return pltpu.make_async_remote_copy(
                src_ref=block(a, pos, half) if src is None else src, dst_ref=block(a, pos, half),
                send_sem=send_sems.at[n_copies * a + k], recv_sem=recv_sems.at[n_copies * a + k],
                device_id=to, device_id_type=MESH)

        started = []
        for a in range(n):
            mine = pltpu.make_async_copy(ins[a], block(a, me), local_sems.at[a])
            mine.start()
            started.append(mine)
        sends = []
        for a in range(n):
            sends += [copy(a, 1, me, x_nbr, src=ins[a]), copy(a, 2, me, y_nbr, src=ins[a]),
                      copy(a, 0, me, sibling, src=ins[a])]
        for cp in sends:
            cp.start()

        def pass_on(copies):
            for cp in copies:
                cp.start()
                sends.append(cp)

        for a in range(n):
            copy(a, 1, x_nbr, me).wait_recv()
            pass_on([copy(a, 5, x_nbr, y_nbr, half=0), copy(a, 3, x_nbr, sibling)])
            copy(a, 2, y_nbr, me).wait_recv()
            pass_on([copy(a, 6, y_nbr, x_nbr, half=1), copy(a, 4, y_nbr, sibling)])
        for a in range(n):
            copy(a, 5, diagonal, me, half=0).wait_recv()
            pass_on([copy(a, 7, diagonal, sibling, half=0)])
            copy(a, 6, diagonal, me, half=1).wait_recv()
            pass_on([copy(a, 8, diagonal, sibling, half=1)])
        for a in range(n):
            copy(a, 0, sibling, me).wait_recv()
            copy(a, 3, other(x_nbr), me).wait_recv()
            copy(a, 4, other(y_nbr), me).wait_recv()
            copy(a, 7, other(diagonal), me, half=0).wait_recv()
            copy(a, 8, other(diagonal), me, half=1).wait_recv()
        for cp in sends:
            cp.wait_send()
        for cp in started:
            cp.wait()

    return pl.kernel(
        body, name=name,
        out_type=[jax.ShapeDtypeStruct((N_DEV,) + s.shape, s.dtype) for s in shards],
        mesh=plsc.ScalarSubcoreMesh(axis_name="sequencer", num_cores=1),
        scratch_types=[pltpu.SemaphoreType.DMA((n_copies * n,)), pltpu.SemaphoreType.DMA((n_copies * n,)),
                       pltpu.SemaphoreType.DMA((n,))],
        compiler_params=pltpu.CompilerParams(collective_id=collective_id),
    )(*shards)


def _chip_sums(grads, *, name):
    _, R, C = grads.shape
    rc = 128 if R % 128 == 0 else R

    def body(g_ref, partial, out_ref, mine, theirs, send_sems, recv_sems, local_sems):
        x, y, c = _position()
        my_chip = 2 * x + y

        def swap(s):
            return pltpu.make_async_remote_copy(
                src_ref=g_ref.at[2 * s + (1 - c)], dst_ref=theirs.at[s],
                send_sem=send_sems.at[s], recv_sem=recv_sems.at[s],
                device_id=(x, y, 1 - c), device_id_type=MESH)

        def load(s):
            return pltpu.make_async_copy(g_ref.at[2 * s + c], mine.at[s], local_sems.at[s])

        for s in range(4):
            swap(s).start()
            load(s).start()

        def chip_sum(chip, rows):
            return mine[chip, rows, :].astype(F32) + theirs[chip, rows, :].astype(F32)

        for s in range(4):
            load(s).wait()
            swap(s).wait_recv()

            @pl.when(s == my_chip)
            def _():
                @pl.loop(0, R // rc)
                def _(t):
                    rows = pl.ds(pl.multiple_of(t * rc, rc), rc)
                    out_ref[rows, :] = chip_sum(s, rows)

            @pl.when(s != my_chip)
            def _():
                @pl.loop(0, R // rc)
                def _(t):
                    rows = pl.ds(pl.multiple_of(t * rc, rc), rc)
                    partial[(s ^ my_chip) - 1, rows, :] = chip_sum(s, rows).astype(BF16)

        for s in range(4):
            swap(s).wait_send()

    vmem = pl.BlockSpec(memory_space=pltpu.VMEM)
    return pl.pallas_call(
        body, name=name,
        in_specs=[pl.BlockSpec(memory_space=pl.ANY)], out_specs=[vmem, vmem],
        out_shape=[jax.ShapeDtypeStruct((3, R, C), BF16), jax.ShapeDtypeStruct((R, C), F32)],
        scratch_shapes=[
            pltpu.VMEM((4, R, C), BF16), pltpu.VMEM((4, R, C), BF16),
            pltpu.SemaphoreType.DMA((4,)), pltpu.SemaphoreType.DMA((4,)), pltpu.SemaphoreType.DMA((4,)),
        ],
        compiler_params=_params(),
    )(grads)


def _cross_chips(partials, *, name, collective_id):
    n = len(partials)

    def body(*refs):
        ins, outs = refs[:n], refs[n:2 * n]
        send_sems, recv_sems = refs[2 * n:]
        x, y, c = _position()
        my_chip = 2 * x + y
        peers = [((my_chip ^ j) // 2, (my_chip ^ j) % 2, c) for j in (1, 2, 3)]

        barrier = pltpu.get_barrier_semaphore()
        for peer in peers:
            pl.semaphore_signal(barrier, inc=1, device_id=peer, device_id_type=MESH)
        pl.semaphore_wait(barrier, 3)

        copies = [
            pltpu.make_async_remote_copy(
                src_ref=ins[a].at[j], dst_ref=outs[a].at[j],
                send_sem=send_sems.at[3 * a + j], recv_sem=recv_sems.at[3 * a + j],
                device_id=peers[j], device_id_type=MESH)
            for a in range(n) for j in range(3)]
        for cp in copies:
            cp.start()
        for cp in copies:
            cp.wait_recv()
        for cp in copies:
            cp.wait_send()

    return pl.kernel(
        body, name=name,
        out_type=[jax.ShapeDtypeStruct(p.shape, p.dtype) for p in partials],
        mesh=plsc.ScalarSubcoreMesh(axis_name="sequencer", num_cores=1),
        scratch_types=[pltpu.SemaphoreType.DMA((3 * n,)), pltpu.SemaphoreType.DMA((3 * n,))],
        compiler_params=pltpu.CompilerParams(collective_id=collective_id),
    )(*partials)


def _cross_chips_and_gather(partial, slab, *, name, collective_id):
    def body(part_ref, slab_ref, landed_ref, slabs_ref, send_sems, recv_sems, local_sem):
        x, y, c = _position()
        me, my_chip = 4 * x + 2 * y + c, 2 * x + y
        others = [me ^ k for k in range(1, N_DEV)]
        ids = [(o // 4, (o // 2) % 2, o % 2) for o in others]

        barrier = pltpu.get_barrier_semaphore()
        for peer in ids:
            pl.semaphore_signal(barrier, inc=1, device_id=peer, device_id_type=MESH)
        pl.semaphore_wait(barrier, N_DEV - 1)

        mine = pltpu.make_async_copy(slab_ref, slabs_ref.at[me], local_sem)
        mine.start()
        sends = [
            pltpu.make_async_remote_copy(
                src_ref=part_ref.at[j], dst_ref=landed_ref.at[j], send_sem=send_sems.at[j], recv_sem=recv_sems.at[j],
                device_id=((my_chip ^ (j + 1)) // 2, (my_chip ^ (j + 1)) % 2, c), device_id_type=MESH)
            for j in range(3)]
        sends += [
            pltpu.make_async_remote_copy(
                src_ref=slab_ref, dst_ref=slabs_ref.at[me], send_sem=send_sems.at[3 + k], recv_sem=recv_sems.at[3 + k],
                device_id=ids[k], device_id_type=MESH)
            for k in range(N_DEV - 1)]
        arrivals = sends[:3] + [
            pltpu.make_async_remote_copy(
                src_ref=slab_ref, dst_ref=slabs_ref.at[others[k]], send_sem=send_sems.at[3 + k],
                recv_sem=recv_sems.at[3 + k], device_id=ids[k], device_id_type=MESH)
            for k in range(N_DEV - 1)]
        for cp in sends:
            cp.start()
        for cp in arrivals:
            cp.wait_recv()
        for cp in sends:
            cp.wait_send()
        mine.wait()

    n_sems = 3 + N_DEV - 1
    return pl.kernel(
        body, name=name,
        out_type=[jax.ShapeDtypeStruct(partial.shape, partial.dtype),
                  jax.ShapeDtypeStruct((N_DEV,) + slab.shape, slab.dtype)],
        mesh=plsc.ScalarSubcoreMesh(axis_name="sequencer", num_cores=1),
        scratch_types=[pltpu.SemaphoreType.DMA((n_sems,)), pltpu.SemaphoreType.DMA((n_sems,)), pltpu.SemaphoreType.DMA],
        compiler_params=pltpu.CompilerParams(collective_id=collective_id),
    )(partial, slab)


def _sum_devices(gathered, after, *, name):
    _, R, C = gathered.shape

    def body(in_ref, after_ref, out_ref):
        total = in_ref[0]
        for d in range(1, N_DEV):
            total = total + in_ref[d]
        out_ref[...] = total

    return pl.pallas_call(
        body, name=name, grid=(1,),
        in_specs=[pl.BlockSpec((N_DEV, R, C), lambda i: (0, 0, 0)), AFTER],
        out_specs=pl.BlockSpec((R, C), lambda i: (0, 0)),
        out_shape=jax.ShapeDtypeStruct((R, C), F32),
        compiler_params=_params(("arbitrary",)),
    )(gathered, _in_hbm(after))


def _owner_sum(own, landed, after, *, name):
    R, C = own.shape
    tr = _row_tile(R, C)

    def body(own_ref, landed_ref, after_ref, out_ref):
        total = own_ref[...]
        for j in range(3):
            total = total + landed_ref[j].astype(F32)
        out_ref[...] = total

    return pl.pallas_call(
        body, name=name, grid=(R // tr,),
        in_specs=[pl.BlockSpec((tr, C), lambda i: (i, 0)), pl.BlockSpec((3, tr, C), lambda i: (0, i, 0)), AFTER],
        out_specs=pl.BlockSpec((tr, C), lambda i: (i, 0)),
        out_shape=jax.ShapeDtypeStruct((R, C), F32),
        compiler_params=_params(("arbitrary",)),
    )(own, landed, _in_hbm(after))


def _local_step(x, target, norms, pool_w_group, pool_scale, wgu1, wd1, w_in, wbp, wba, w_out, wgu2, wd2, exchange):
    n1g, nmg, n2g, nfg = norms
    D = x.shape[1]
    gu1, hid1 = _ffn_up(x, n1g, wgu1, tm=1024, name="ffn1_up")
    h1 = _ffn_down(x, hid1, wd1, tm=512, name="ffn1_down")
    un, proj = _inproj_fwd(h1, nmg, w_in, tm=1024, name="inproj_fwd")
    p = _pool_fwd(proj, pool_w_group, pool_scale, name="pool_fwd")
    o, ltot = _attn_fwd(proj, name="attn_fwd")
    h2, m = _mix_fwd(h1, p, o, proj, wbp, wba, w_out, tm=256, name="mix_fwd")
    gu2, hid2 = _ffn_up(h2, n2g, wgu2, tm=1024, name="ffn2_up")
    h3 = _ffn_down(h2, hid2, wd2, tm=512, name="ffn2_down")
    dh3, df2, loss, d_nf = _loss_bwd(h3, target, nfg, tm=256, name="loss_bwd")

    d_wd2 = _wgrad_down(hid2, df2, tk=WGRAD_TOKENS, name="ffn2_wgrad_down")
    (g_wd2,), token = exchange("ffn2_down", [d_wd2.reshape(N_DEV, FF_SHARD_PAD, D)])
    dh2, d_n2, n2, dgu2 = _ffn_bwd(dh3, df2, h2, n2g, gu2, wgu2, wd2, token, tm=512, name="ffn2_bwd")
    d_wgu2 = _wgrad_gate_up(n2, dgu2, tk=WGRAD_TOKENS, name="ffn2_wgrad_gate_up")
    (g_wgu2,), token = exchange("ffn2_gate_up", [d_wgu2])

    dyp, dys, dp, do, dgl = _mix_bwd(dh2, p, o, proj, wbp, wba, w_out, token, tm=256, name="mix_bwd")
    d_wout = _wgrad_full(m, dh2, tk=WGRAD_TOKENS, name="wgrad_out")
    d_wbp = _wgrad_full(dyp, p, tk=WGRAD_TOKENS, name="wgrad_branch_pool")
    d_wba = _wgrad_full(dys, o, tk=WGRAD_TOKENS, name="wgrad_branch_attn")
    by_owner = lambda g: g.reshape(N_DEV, g.shape[0] // N_DEV, g.shape[1])
    (g_wbp, g_wba, g_wout), token = exchange("mix", [by_owner(d_wbp), by_owner(d_wba), by_owner(d_wout)])
    dxp, d_wgroup, d_scale = _pool_bwd(dp, proj, pool_w_group, pool_scale, name="pool_bwd")
    dq, dk, dv = _attn_bwd(proj, do, ltot, token, name="attn_bwd")
    dproj = jnp.concatenate([dxp, dq, dk, dv, dgl], axis=1)
    d_win = _wgrad_in(dproj, un, tk=WGRAD_TOKENS, name="wgrad_in")
    (g_win,), token_in = exchange("w_in", [d_win])
    dh1, df1, d_nm = _inproj_bwd(dproj, dh2, h1, nmg, w_in, tm=512, name="inproj_bwd")
    d_wd1 = _wgrad_down(hid1, df1, tk=WGRAD_TOKENS, name="ffn1_wgrad_down")
    (g_wd1,), token_down = exchange("ffn1_down", [d_wd1.reshape(N_DEV, FF_SHARD_PAD, D)])
    token = (token_down[(0,) * token_down.ndim] + token_in[(0,) * token_in.ndim]).reshape(1, 1)

    dx, d_n1, n1, dgu1 = _ffn_bwd(dh1, df1, x, n1g, gu1, wgu1, wd1, token, tm=512, name="ffn1_bwd")
    d_wgu1_a = _wgrad_gate_up(n1, dgu1, tk=WGRAD_TOKENS, name="ffn1_wgrad_gate_up_a", part=0, parts=2)
    (g_wgu1_a,), token = exchange("ffn1_gate_up_a", [d_wgu1_a])
    d_wgu1_b = _wgrad_gate_up(n1, dgu1, tk=WGRAD_TOKENS, name="ffn1_wgrad_gate_up_b", part=1, parts=2)
    (g_wgu1_b, replicated), token = exchange("last", [d_wgu1_b, d_n1, d_nm, d_n2, d_nf, d_scale, d_wgroup, loss])
    g_wgu1 = (g_wgu1_a, g_wgu1_b)

    sharded = (g_wgu1, g_wd1, g_win, g_wbp, g_wba, g_wout, g_wgu2, g_wd2)
    return dx, sharded, replicated, token


def _hidden_major(w):
    return jnp.swapaxes(w[0], 0, 1)


def _pad_gate_up(wt):
    d = wt.shape[1]
    wt = wt.astype(BF16).reshape(2, FF_SHARD, d)
    return jnp.pad(wt, ((0, 0), (0, FF_SHARD_PAD - FF_SHARD), (0, 0))).reshape(2 * FF_SHARD_PAD, d)


def _unpad_gate_up(gt):
    d = gt.shape[1]
    return gt.reshape(2, FF_SHARD_PAD, d)[:, :FF_SHARD].reshape(2 * FF_SHARD, d)


def _pad_down(w):
    return jnp.pad(w.astype(BF16), ((0, FF_SHARD_PAD - FF_SHARD), (0, 0)))


def kernel(x, ffn1_norm, ffn1_w_gate_up, ffn1_w_down, mix_norm, w_in, pool_w_group, pool_scale, w_branch_pool, w_branch_attn, w_out, ffn2_norm, ffn2_w_gate_up, ffn2_w_down, final_norm, loss_target, m_ffn1_norm, m_ffn1_w_gate_up, m_ffn1_w_down, m_mix_norm, m_w_in, m_pool_w_group, m_pool_scale, m_w_branch_pool, m_w_branch_attn, m_w_out, m_ffn2_norm, m_ffn2_w_gate_up, m_ffn2_w_down, m_final_norm, v_ffn1_norm, v_ffn1_w_gate_up, v_ffn1_w_down, v_mix_norm, v_w_in, v_pool_w_group, v_pool_scale, v_w_branch_pool, v_w_branch_attn, v_w_out, v_ffn2_norm, v_ffn2_w_gate_up, v_ffn2_w_down, v_final_norm):
    D = x.shape[-1]
    weights = dict(ffn1_norm=ffn1_norm, ffn1_w_gate_up=ffn1_w_gate_up, ffn1_w_down=ffn1_w_down, mix_norm=mix_norm,
                   w_in=w_in, pool_w_group=pool_w_group, pool_scale=pool_scale, w_branch_pool=w_branch_pool,
                   w_branch_attn=w_branch_attn, w_out=w_out, ffn2_norm=ffn2_norm, ffn2_w_gate_up=ffn2_w_gate_up,
                   ffn2_w_down=ffn2_w_down, final_norm=final_norm)
    first = dict(ffn1_norm=m_ffn1_norm, ffn1_w_gate_up=m_ffn1_w_gate_up, ffn1_w_down=m_ffn1_w_down,
                 mix_norm=m_mix_norm, w_in=m_w_in, pool_w_group=m_pool_w_group, pool_scale=m_pool_scale,
                 w_branch_pool=m_w_branch_pool, w_branch_attn=m_w_branch_attn, w_out=m_w_out,
                 ffn2_norm=m_ffn2_norm, ffn2_w_gate_up=m_ffn2_w_gate_up, ffn2_w_down=m_ffn2_w_down,
                 final_norm=m_final_norm)
    second = dict(ffn1_norm=v_ffn1_norm, ffn1_w_gate_up=v_ffn1_w_gate_up, ffn1_w_down=v_ffn1_w_down,
                  mix_norm=v_mix_norm, w_in=v_w_in, pool_w_group=v_pool_w_group, pool_scale=v_pool_scale,
                  w_branch_pool=v_w_branch_pool, w_branch_attn=v_w_branch_attn, w_out=v_w_out,
                  ffn2_norm=v_ffn2_norm, ffn2_w_gate_up=v_ffn2_w_gate_up, ffn2_w_down=v_ffn2_w_down,
                  final_norm=v_final_norm)
    order = list(weights)

    wgu1, = _all_gather([_pad_gate_up(_hidden_major(ffn1_w_gate_up))], name="all_gather_ffn1_gate_up", collective_id=0)
    wd1, = _all_gather([_pad_down(ffn1_w_down[0])], name="all_gather_ffn1_down", collective_id=10)
    transposed = lambda w: jnp.swapaxes(w[0], 0, 1).astype(BF16)
    win_g, = _all_gather([transposed(w_in)], name="all_gather_w_in", collective_id=1)
    wbp_g, wba_g = _all_gather([transposed(w_branch_pool), transposed(w_branch_attn)],
                               name="all_gather_branches", collective_id=2)
    wout_g, = _all_gather([w_out[0].astype(BF16)], name="all_gather_w_out", collective_id=11)
    wgu2, wd2 = _all_gather([_pad_gate_up(_hidden_major(ffn2_w_gate_up)), _pad_down(ffn2_w_down[0])],
                            name="all_gather_ffn2", collective_id=3)
    whole = lambda g: g.reshape(g.shape[0] * g.shape[1], g.shape[2])
    wd1, wd2, win_g, wbp_g, wba_g, wout_g = (whole(g) for g in (wd1, wd2, win_g, wbp_g, wba_g, wout_g))

    cross_ids = {"ffn2_down": 4, "ffn2_gate_up": 5, "mix": 6, "ffn1_down": 7, "w_in": 8, "last": 9,
                 "ffn1_gate_up_a": 12}
    small = ["ffn1_norm", "mix_norm", "ffn2_norm", "final_norm", "pool_scale", "pool_w_group"]

    def tile_rows(a):
        a = a.reshape(-1, 128)
        return jnp.pad(a, ((0, -a.shape[0] % 8), (0, 0)))

    def exchange(tag, group):
        if tag == "last":
            slab = jnp.concatenate([tile_rows(g) for g in group[1:-1]] + [jnp.broadcast_to(group[-1], (8, 128))], axis=0)
            partial, own = _chip_sums(group[0], name="chip_sums_last")
            landed, slabs = _cross_chips_and_gather(partial, slab, name="cross_chips_last", collective_id=cross_ids[tag])
            return [(own, landed), slabs], own
        sums = [_chip_sums(g, name=f"chip_sums_{tag}_{i}") for i, g in enumerate(group)]
        landed = _cross_chips([s[0] for s in sums], name="cross_chips_" + tag, collective_id=cross_ids[tag])
        token = sums[0][1] if len(sums) == 1 else sum(s[1][0, 0] for s in sums).reshape(1, 1)
        return [(s[1], l) for s, l in zip(sums, landed)], token

    norms = (ffn1_norm, mix_norm, ffn2_norm, final_norm.reshape(1, D))
    dx, sharded, slabs, last = _local_step(
        x[0], loss_target[0], norms, pool_w_group[0], pool_scale, wgu1, wd1, win_g, wbp_g, wba_g, wout_g, wgu2, wd2,
        exchange)
    names = ["ffn1_w_gate_up", "ffn1_w_down", "w_in", "w_branch_pool", "w_branch_attn", "w_out",
             "ffn2_w_gate_up", "ffn2_w_down"]
    handles = dict(zip(names, sharded))
    grads, delta, new_m, new_v = {}, {}, {}, {}
    after = last
    for k in ("ffn2_w_down", "ffn2_w_gate_up", "w_branch_pool", "w_branch_attn", "w_out", "w_in", "ffn1_w_down",
              "ffn1_w_gate_up"):
        hidden_major = k.endswith("w_gate_up")
        if isinstance(handles[k][0], tuple):
            first_half = _owner_sum(*handles[k][0], after, name="owner_sum_" + k + "_a")
            second_half = _owner_sum(*handles[k][1], first_half, name="owner_sum_" + k + "_b")
            g = jnp.concatenate([first_half[:FF_SHARD], second_half[:FF_SHARD]], axis=0)
        else:
            g = _owner_sum(*handles[k], after, name="owner_sum_" + k)
            if hidden_major:
                g = _unpad_gate_up(g)
            elif k in ("w_in", "w_branch_pool", "w_branch_attn"):
                g = jnp.swapaxes(g, 0, 1)
            else:
                g = g[:weights[k].shape[1]]
        view = _hidden_major if hidden_major else (lambda a: a[0])
        back = (lambda a: jnp.swapaxes(a, 0, 1)[None]) if hidden_major else (lambda a: a[None])
        out = _adamw(view(weights[k]), g, view(first[k]), view(second[k]), name="adamw_" + k)
        after = out[0]
        grads[k] = back(g)
        delta[k], new_m[k], new_v[k] = (back(a) for a in out)

    rows = [weights[k].size // 128 for k in small]
    padded_rows = [-(-r // 8) * 8 for r in rows]
    starts = [sum(padded_rows[:i]) for i in range(len(rows) + 1)]
    total = _sum_devices(slabs, after, name="sum_replicated")
    loss_out = total[starts[-1], 0]
    small_w = jnp.concatenate([tile_rows(weights[k]) for k in small], axis=0)
    small_m = jnp.concatenate([tile_rows(first[k]) for k in small], axis=0)
    small_v = jnp.concatenate([tile_rows(second[k]) for k in small], axis=0)
    small_out = _adamw(small_w, total[:starts[-1]], small_m, small_v, name="adamw_replicated")
    for name_, start, n_rows in zip(small, starts, rows):
        shape = weights[name_].shape
        grads[name_] = total[start:start + n_rows].reshape(shape)
        delta[name_], new_m[name_], new_v[name_] = (a[start:start + n_rows].reshape(shape) for a in small_out)

    return (loss_out, dx[None], *[grads[k] for k in order], *[delta[k] for k in order],
            *[new_m[k] for k in order], *[new_v[k] for k in order])
```

```python
import functools

import jax
import jax.numpy as jnp
from jax import lax
from jax.experimental import pallas as pl
from jax.experimental.pallas import tpu as pltpu
from jax.experimental.pallas import tpu_sc as plsc

F32 = jnp.float32
BF16 = jnp.bfloat16
MESH = pl.DeviceIdType.MESH

RMS_EPS = 1e-6
N_DEV = 8
N_HEADS = 8
HEAD_DIM = 64
HEAD_PAIR = 2 * HEAD_DIM
POOL_WINDOWS = (2, 4, 8, 16)
POOL_GROUP = 128
POOL_WIDTH = 512
SB_WIDTH = 512
FF_SHARD = 352
FF_SHARD_PAD = 384
ATTN_K_BLOCK = 256
ATTN_Q_BLOCK_FWD = 512
ATTN_Q_BLOCK_BWD = 256
ATTN_SCALE = 0.125

ADAM_LR = 0.001
ADAM_B1 = 0.9
ADAM_B2 = 0.999
ADAM_EPS = 1e-08
ADAM_WD = 0.01
ADAM_STEP = 10

VMEM_LIMIT = 48 << 20
WGRAD_TOKENS = 2048


def _params(dims=None):
    return pltpu.CompilerParams(dimension_semantics=dims, vmem_limit_bytes=VMEM_LIMIT)


def _mm(a, b):
    return jnp.dot(a, b, preferred_element_type=F32)


def _mm_nt(a, b):
    return lax.dot_general(a, b, (((1,), (1,)), ((), ())), preferred_element_type=F32)


def _mm_tn(a, b):
    return lax.dot_general(a, b, (((0,), (0,)), ((), ())), preferred_element_type=F32)


def _row_tile(rows, cols):
    limit = max(8, (512 * 1024) // cols)
    return max(t for t in range(8, rows + 1, 8) if rows % t == 0 and (t <= limit or t == 8))


def _rstd(xf):
    return lax.rsqrt(jnp.mean(xf * xf, axis=-1, keepdims=True) + RMS_EPS)


def _rms_bwd(xf, gain, dn):
    r = _rstd(xf)
    xh = xf * r
    dgain = jnp.sum(dn * xh, axis=0, keepdims=True)
    dxh = dn * gain
    dx = r * (dxh - xh * jnp.mean(dxh * xh, axis=-1, keepdims=True))
    return dx, dgain


def _ffn_up(x, gain, wgu, *, tm, name):
    T, D = x.shape
    tm = min(tm, T)
    nb, bw = wgu.shape[0] // 2, wgu.shape[1]

    def body(x_ref, gain_ref, wg_ref, wu_ref, gu_ref, hid_ref, n_scr):
        @pl.when(pl.program_id(1) == 0)
        def _():
            xf = x_ref[...]
            n_scr[...] = (xf * _rstd(xf) * gain_ref[...]).astype(BF16)

        halves = (pl.ds(0, tm // 2), pl.ds(tm // 2, tm // 2))
        wg, wu = wg_ref[...], wu_ref[...]
        gus = [(_mm_nt(n_scr[rows, :], wg), _mm_nt(n_scr[rows, :], wu)) for rows in halves]
        for rows, (g, u) in zip(halves, gus):
            gu_ref[0, rows, :] = g.astype(BF16)
            gu_ref[1, rows, :] = u.astype(BF16)
            hid_ref[rows, :] = (g * jax.nn.sigmoid(g) * u).astype(BF16)

    return pl.pallas_call(
        body, name=name, grid=(T // tm, nb),
        in_specs=[
            pl.BlockSpec((tm, D), lambda i, j: (i, 0)),
            pl.BlockSpec((1, D), lambda i, j: (0, 0)),
            pl.BlockSpec((None, bw, D), lambda i, j: (j, 0, 0)),
            pl.BlockSpec((None, bw, D), lambda i, j: (j + nb, 0, 0)),
        ],
        out_specs=[
            pl.BlockSpec((2, tm, bw), lambda i, j: (0, i, j)),
            pl.BlockSpec((tm, bw), lambda i, j: (i, j)),
        ],
        out_shape=[jax.ShapeDtypeStruct((2, T, nb * bw), BF16), jax.ShapeDtypeStruct((T, nb * bw), BF16)],
        scratch_shapes=[pltpu.VMEM((tm, D), BF16)],
        compiler_params=_params(("arbitrary", "arbitrary")),
    )(x, gain, wgu, wgu)


def _ffn_down(x, hid, wd, *, tm, name):
    T, D = x.shape
    tm = min(tm, T)
    F = hid.shape[1]

    def body(x_ref, hid_ref, wd_ref, h_ref):
        h_ref[...] = x_ref[...] + 0.5 * _mm(hid_ref[...], wd_ref[...])

    return pl.pallas_call(
        body, name=name, grid=(T // tm,),
        in_specs=[
            pl.BlockSpec((tm, D), lambda i: (i, 0)),
            pl.BlockSpec((tm, F), lambda i: (i, 0)),
            pl.BlockSpec((F, D), lambda i: (0, 0)),
        ],
        out_specs=pl.BlockSpec((tm, D), lambda i: (i, 0)),
        out_shape=jax.ShapeDtypeStruct((T, D), F32),
        compiler_params=_params(("arbitrary",)),
    )(x, hid, wd)


AFTER = pl.BlockSpec(memory_space=pltpu.HBM)


def _in_hbm(token):
    return pltpu.with_memory_space_constraint(token, pltpu.HBM)


def _ffn_bwd(dh, df, x, gain, gu, wgu, wd, after, *, tm, name):
    T, D = x.shape
    tm = min(tm, T)
    nb, bw = wgu.shape[0] // 2, wgu.shape[1]

    def body(dh_ref, df_ref, x_ref, gain_ref, gu_ref, wg_ref, wu_ref, wd_ref, after_ref,
             dx_ref, dgain_ref, n_ref, dgu_ref, dn_acc):
        i, j = pl.program_id(0), pl.program_id(1)

        @pl.when(j == 0)
        def _():
            xf = x_ref[...]
            n_ref[...] = (xf * _rstd(xf) * gain_ref[...]).astype(BF16)
            dn_acc[...] = jnp.zeros_like(dn_acc)

        @pl.when((i == 0) & (j == 0))
        def _():
            dgain_ref[...] = jnp.zeros_like(dgain_ref)

        halves = (pl.ds(0, tm // 2), pl.ds(tm // 2, tm // 2))
        wd, wg, wu = wd_ref[...], wg_ref[...], wu_ref[...]
        dhids = [_mm_nt(df_ref[rows, :], wd) for rows in halves]
        for rows, dhid in zip(halves, dhids):
            g = gu_ref[0, rows, :].astype(F32)
            u = gu_ref[1, rows, :].astype(F32)
            s = jax.nn.sigmoid(g)
            silu = g * s
            dg = (dhid * u * (s * (1.0 + g * (1.0 - s)))).astype(BF16)
            du = (dhid * silu).astype(BF16)
            dgu_ref[0, rows, :] = dg
            dgu_ref[1, rows, :] = du
            dn_acc[rows, :] += _mm(dg, wg) + _mm(du, wu)

        @pl.when(j == nb - 1)
        def _():
            dx, dgain = _rms_bwd(x_ref[...], gain_ref[...], dn_acc[...])
            dx_ref[...] = dh_ref[...] + dx
            dgain_ref[...] += dgain

    row = lambda i, j: (i, 0)
    return pl.pallas_call(
        body, name=name, grid=(T // tm, nb),
        in_specs=[
            pl.BlockSpec((tm, D), row),
            pl.BlockSpec((tm, D), row),
            pl.BlockSpec((tm, D), row),
            pl.BlockSpec((1, D), lambda i, j: (0, 0)),
            pl.BlockSpec((2, tm, bw), lambda i, j: (0, i, j)),
            pl.BlockSpec((None, bw, D), lambda i, j: (j, 0, 0)),
            pl.BlockSpec((None, bw, D), lambda i, j: (j + nb, 0, 0)),
            pl.BlockSpec((bw, D), lambda i, j: (j, 0)),
            AFTER,
        ],
        out_specs=[
            pl.BlockSpec((tm, D), row),
            pl.BlockSpec((1, D), lambda i, j: (0, 0)),
            pl.BlockSpec((tm, D), row),
            pl.BlockSpec((2, tm, bw), lambda i, j: (0, i, j)),
        ],
        out_shape=[
            jax.ShapeDtypeStruct((T, D), F32),
            jax.ShapeDtypeStruct((1, D), F32),
            jax.ShapeDtypeStruct((T, D), BF16),
            jax.ShapeDtypeStruct((2, T, nb * bw), BF16),
        ],
        scratch_shapes=[pltpu.VMEM((tm, D), F32)],
        compiler_params=_params(("arbitrary", "arbitrary")),
    )(dh, df, x, gain, gu, wgu, wgu, wd, _in_hbm(after))


def _wgrad(a, b, *, grid, a_spec, b_spec, out_spec, out_shape, acc_shape, name):
    nk = grid[2]

    def body(a_ref, b_ref, o_ref, acc):
        k = pl.program_id(2)

        @pl.when(k == 0)
        def _():
            acc[...] = jnp.zeros_like(acc)

        acc[...] += _mm_tn(a_ref[...].astype(BF16), b_ref[...].astype(BF16))

        @pl.when(k == nk - 1)
        def _():
            o_ref[...] = acc[...].astype(o_ref.dtype)

    return pl.pallas_call(
        body, name=name, grid=grid, in_specs=[a_spec, b_spec], out_specs=out_spec,
        out_shape=jax.ShapeDtypeStruct(out_shape, BF16),
        scratch_shapes=[pltpu.VMEM(acc_shape, F32)],
        compiler_params=_params(("arbitrary", "arbitrary", "arbitrary")),
    )(a, b)


def _wgrad_gate_up(n, dgu, *, tk, name, part=0, parts=1):
    T, D = n.shape
    tk = min(tk, T)
    owner_rows = FF_SHARD_PAD * 2
    nb = dgu.shape[2] // owner_rows
    bw = owner_rows // parts
    return _wgrad(
        dgu, n, grid=(2 * nb, 1, T // tk), name=name,
        a_spec=pl.BlockSpec((None, tk, bw), lambda m, c, k: (m // nb, k, parts * (m % nb) + part)),
        b_spec=pl.BlockSpec((tk, D), lambda m, c, k: (k, 0)),
        out_spec=pl.BlockSpec((None, bw, D), lambda m, c, k: (m, 0, 0)),
        out_shape=(2 * nb, bw, D), acc_shape=(bw, D))


def _wgrad_down(hid, df, *, tk, name):
    T, D = df.shape
    tk = min(tk, T)
    bw = FF_SHARD_PAD * 2
    nb = hid.shape[1] // bw
    return _wgrad(
        hid, df, grid=(nb, 1, T // tk), name=name,
        a_spec=pl.BlockSpec((tk, bw), lambda m, c, k: (k, m)),
        b_spec=pl.BlockSpec((tk, D), lambda m, c, k: (k, 0)),
        out_spec=pl.BlockSpec((bw, D), lambda m, c, k: (m, 0)),
        out_shape=(nb * bw, D), acc_shape=(bw, D))


def _wgrad_in(dproj, un, *, tk, name):
    T, D = un.shape
    tk = min(tk, T)
    bw = dproj.shape[1] // N_DEV
    return _wgrad(
        dproj, un, grid=(N_DEV, 1, T // tk), name=name,
        a_spec=pl.BlockSpec((tk, bw), lambda m, c, k: (k, m)),
        b_spec=pl.BlockSpec((tk, D), lambda m, c, k: (k, 0)),
        out_spec=pl.BlockSpec((None, bw, D), lambda m, c, k: (m, 0, 0)),
        out_shape=(N_DEV, bw, D), acc_shape=(bw, D))


def _wgrad_full(a, b, *, tk, name):
    T, M = a.shape
    tk = min(tk, T)
    N = b.shape[1]
    return _wgrad(
        a, b, grid=(1, 1, T // tk), name=name,
        a_spec=pl.BlockSpec((tk, M), lambda m, c, k: (k, 0)),
        b_spec=pl.BlockSpec((tk, N), lambda m, c, k: (k, 0)),
        out_spec=pl.BlockSpec((M, N), lambda m, c, k: (0, 0)), out_shape=(M, N), acc_shape=(M, N))


def _loss_bwd(h, target, gain, *, tm, name):
    T, D = h.shape
    tm = min(tm, T)

    def body(h_ref, t_ref, gain_ref, dh_ref, df_ref, loss_ref, dgain_ref):
        @pl.when(pl.program_id(0) == 0)
        def _():
            loss_ref[...] = jnp.zeros_like(loss_ref)
            dgain_ref[...] = jnp.zeros_like(dgain_ref)

        xf = h_ref[...]
        gain = gain_ref[...]
        err = xf * _rstd(xf) * gain - t_ref[...]
        loss_ref[...] += 0.5 * jnp.sum(jnp.mean(err * err, axis=-1, keepdims=True), axis=0, keepdims=True)
        dx, dgain = _rms_bwd(xf, gain, err * (1.0 / D))
        dh_ref[...] = dx
        df_ref[...] = (0.5 * dx).astype(BF16)
        dgain_ref[...] += dgain

    row = lambda i: (i, 0)
    fixed = lambda i: (0, 0)
    return pl.pallas_call(
        body, name=name, grid=(T // tm,),
        in_specs=[pl.BlockSpec((tm, D), row), pl.BlockSpec((tm, D), row), pl.BlockSpec((1, D), fixed)],
        out_specs=[pl.BlockSpec((tm, D), row), pl.BlockSpec((tm, D), row), pl.BlockSpec((1, 128), fixed),
                   pl.BlockSpec((1, D), fixed)],
        out_shape=[jax.ShapeDtypeStruct((T, D), F32), jax.ShapeDtypeStruct((T, D), BF16),
                   jax.ShapeDtypeStruct((1, 128), F32), jax.ShapeDtypeStruct((1, D), F32)],
        compiler_params=_params(("arbitrary",)),
    )(h, target, gain)


def _inproj_fwd(h, gain, w_in_t, *, tm, name):
    T, D = h.shape
    tm = min(tm, T)
    bn = D
    nb = w_in_t.shape[0] // bn

    def body(h_ref, gain_ref, wt_ref, un_ref, proj_ref):
        @pl.when(pl.program_id(1) == 0)
        def _():
            xf = h_ref[...]
            un_ref[...] = (xf * _rstd(xf) * gain_ref[...]).astype(BF16)

        proj_ref[...] = _mm_nt(un_ref[...], wt_ref[...])

    return pl.pallas_call(
        body, name=name, grid=(T // tm, nb),
        in_specs=[
            pl.BlockSpec((tm, D), lambda i, j: (i, 0)),
            pl.BlockSpec((1, D), lambda i, j: (0, 0)),
            pl.BlockSpec((bn, D), lambda i, j: (j, 0)),
        ],
        out_specs=[pl.BlockSpec((tm, D), lambda i, j: (i, 0)), pl.BlockSpec((tm, bn), lambda i, j: (i, j))],
        out_shape=[jax.ShapeDtypeStruct((T, D), BF16), jax.ShapeDtypeStruct((T, nb * bn), F32)],
        compiler_params=_params(("arbitrary", "arbitrary")),
    )(h, gain, w_in_t)


def _inproj_bwd(dproj, dh, h, gain, w_in_t, *, tm, name):
    T, D = h.shape
    tm = min(tm, T)
    width = w_in_t.shape[0]

    def body(dp_ref, dh_ref, h_ref, gain_ref, wt_ref, dx_ref, df_ref, dgain_ref):
        @pl.when(pl.program_id(0) == 0)
        def _():
            dgain_ref[...] = jnp.zeros_like(dgain_ref)

        dx, dgain = _rms_bwd(h_ref[...], gain_ref[...], _mm(dp_ref[...], wt_ref[...]))
        dh_in = dh_ref[...] + dx
        dx_ref[...] = dh_in
        df_ref[...] = (0.5 * dh_in).astype(BF16)
        dgain_ref[...] += dgain

    row = lambda i: (i, 0)
    fixed = lambda i: (0, 0)
    return pl.pallas_call(
        body, name=name, grid=(T // tm,),
        in_specs=[
            pl.BlockSpec((tm, width), row),
            pl.BlockSpec((tm, D), row),
            pl.BlockSpec((tm, D), row),
            pl.BlockSpec((1, D), fixed),
            pl.BlockSpec((width, D), fixed),
        ],
        out_specs=[pl.BlockSpec((tm, D), row), pl.BlockSpec((tm, D), row), pl.BlockSpec((1, D), fixed)],
        out_shape=[jax.ShapeDtypeStruct((T, D), F32), jax.ShapeDtypeStruct((T, D), BF16),
                   jax.ShapeDtypeStruct((1, D), F32)],
        compiler_params=_params(("arbitrary",)),
    )(dproj, dh, h, gain, w_in_t)


def _window_sum(x, row, doublings, *, backward):
    T = x.shape[0]
    s = x
    for k in range(doublings):
        sh = 1 << k
        if backward:
            s = s + jnp.where(row < T - sh, pltpu.roll(s, T - sh, 0), 0.0)
        else:
            s = s + jnp.where(row >= sh, pltpu.roll(s, sh, 0), 0.0)
    return s


def _pool_fwd(proj, w_group, scale, *, name):
    T = proj.shape[0]

    def body(xp_ref, w_ref, scale_ref, p_ref):
        row = lax.broadcasted_iota(jnp.int32, (T, POOL_GROUP), 0)
        for gi, window in enumerate(POOL_WINDOWS):
            cols = slice(gi * POOL_GROUP, (gi + 1) * POOL_GROUP)
            x = xp_ref[:, cols]
            inv_count = 1.0 / jnp.minimum(row + 1, window).astype(F32)
            yc = _window_sum(x, row, gi + 1, backward=False) * inv_count - x
            pre = _mm(yc.astype(BF16), w_ref[gi].astype(BF16))
            p_ref[:, cols] = pre * scale_ref[:, cols]

    return pl.pallas_call(
        body, name=name, grid=(1,),
        in_specs=[
            pl.BlockSpec((T, POOL_WIDTH), lambda i: (0, 0)),
            pl.BlockSpec(w_group.shape, lambda i: (0, 0, 0)),
            pl.BlockSpec((1, POOL_WIDTH), lambda i: (0, 0)),
        ],
        out_specs=pl.BlockSpec((T, POOL_WIDTH), lambda i: (0, 0)),
        out_shape=jax.ShapeDtypeStruct((T, POOL_WIDTH), F32),
        compiler_params=_params(("arbitrary",)),
    )(proj, w_group, scale)


def _pool_bwd(dp, proj, w_group, scale, *, name):
    T = proj.shape[0]

    def body(dp_ref, xp_ref, w_ref, scale_ref, dxp_ref, dw_ref, dscale_ref):
        row = lax.broadcasted_iota(jnp.int32, (T, POOL_GROUP), 0)
        for gi, window in enumerate(POOL_WINDOWS):
            cols = slice(gi * POOL_GROUP, (gi + 1) * POOL_GROUP)
            x = xp_ref[:, cols]
            inv_count = 1.0 / jnp.minimum(row + 1, window).astype(F32)
            yc = (_window_sum(x, row, gi + 1, backward=False) * inv_count - x).astype(BF16)
            w = w_ref[gi].astype(BF16)
            pre = _mm(yc, w)
            dpg = dp_ref[:, cols]
            dscale_ref[:, cols] = jnp.sum(dpg * pre, axis=0, keepdims=True)
            dpre = (dpg * scale_ref[:, cols]).astype(BF16)
            dw_ref[gi] = _mm_tn(yc, dpre)
            dyc = _mm_nt(dpre, w)
            dxp_ref[:, cols] = (_window_sum(dyc * inv_count, row, gi + 1, backward=True) - dyc).astype(BF16)

    return pl.pallas_call(
        body, name=name, grid=(1,),
        in_specs=[
            pl.BlockSpec((T, POOL_WIDTH), lambda i: (0, 0)),
            pl.BlockSpec((T, POOL_WIDTH), lambda i: (0, 0)),
            pl.BlockSpec(w_group.shape, lambda i: (0, 0, 0)),
            pl.BlockSpec((1, POOL_WIDTH), lambda i: (0, 0)),
        ],
        out_specs=[
            pl.BlockSpec((T, POOL_WIDTH), lambda i: (0, 0)),
            pl.BlockSpec(w_group.shape, lambda i: (0, 0, 0)),
            pl.BlockSpec((1, POOL_WIDTH), lambda i: (0, 0)),
        ],
        out_shape=[jax.ShapeDtypeStruct((T, POOL_WIDTH), BF16), jax.ShapeDtypeStruct(w_group.shape, F32),
                   jax.ShapeDtypeStruct((1, POOL_WIDTH), F32)],
        compiler_params=_params(("arbitrary",)),
    )(dp, proj, w_group, scale)


ATTN_STRIP = 32


def _log_sigmoids(z):
    lb = jnp.minimum(z, 0.0) - jnp.log(1.0 + jnp.exp(-jnp.abs(z)))
    return lb, lb - z


def _transposed_blocks(x_ref, blocks_scr, tq):
    for b in range(blocks_scr.shape[0]):
        blocks_scr[b] = x_ref[b * tq:(b + 1) * tq, :].T.astype(BF16)


def _split_bf16(x):
    hi = x.astype(BF16)
    return hi, (x - hi.astype(F32)).astype(BF16)


def _strips(n):
    return [slice(i, i + ATTN_STRIP) for i in range(0, n, ATTN_STRIP)]


def _rows(parts):
    return jnp.concatenate(parts, axis=0)


def _attn_specs(T, tq):
    q_col = POOL_WIDTH // HEAD_PAIR
    k_col = q_col + SB_WIDTH // HEAD_PAIR
    v_col = k_col + SB_WIDTH // HEAD_PAIR
    return [
        pl.BlockSpec((tq, HEAD_PAIR), lambda p, i: (i, q_col + p)),
        pl.BlockSpec((T, HEAD_PAIR), lambda p, i: (0, k_col + p)),
        pl.BlockSpec((T, HEAD_PAIR), lambda p, i: (0, v_col + p)),
    ]


def _attn_fwd(proj, *, name):
    T = proj.shape[0]
    tk = min(ATTN_K_BLOCK, T)
    tq = min(ATTN_Q_BLOCK_FWD, T)
    diagonal_blocks = tq // tk

    def body(q_ref, k_ref, v_ref, o_ref, lt_ref, kt_scr, vb_scr):
        qi = pl.program_id(1)

        @pl.when(qi == 0)
        def _():
            _transposed_blocks(k_ref, kt_scr, tk)
            vb_scr[...] = v_ref[...].astype(BF16)

        head0 = lax.broadcasted_iota(jnp.int32, (tq, HEAD_PAIR), 1) < HEAD_DIM
        q = q_ref[...] * ATTN_SCALE
        qs = (jnp.where(head0, q, 0.0).astype(BF16), jnp.where(head0, 0.0, q).astype(BF16))
        r = lax.broadcasted_iota(jnp.int32, (tq, tk), 0)
        c = lax.broadcasted_iota(jnp.int32, (tq, tk), 1)
        later = (r[:tk] > c[:tk]).astype(BF16)
        later2 = _rows([later, later])
        causal = lambda d: (lambda rows: c[rows] + d * tk < r[rows])
        strips = _strips(tq)

        def log_terms(z, valid):
            lbs, his, los, sums = [], [], [], []
            for rows in strips:
                lb, lm = _log_sigmoids(z[rows])
                if valid is not None:
                    lm = jnp.where(valid(rows), lm, 0.0)
                hi, lo = _split_bf16(lm)
                lbs.append(lb)
                his.append(hi)
                los.append(lo)
                sums.append(jnp.sum(lm, axis=1, keepdims=True))
            return lbs, jnp.concatenate([_rows(his), _rows(los)], axis=1), _rows(sums)

        def weights(lbs, run, after, valid):
            parts = []
            for rows, lb in zip(strips, lbs):
                a = jnp.exp(lb + run[rows] + after[rows])
                if valid is not None:
                    a = jnp.where(valid(rows), a, 0.0)
                parts.append(a.astype(BF16))
            return _rows(parts)

        def block(kj, carry, valid):
            kt = kt_scr[kj]
            vb = vb_scr[pl.ds(pl.multiple_of(kj * tk, tk), tk), :]
            run0, o0, run1, o1 = carry
            z0 = _mm(qs[0], kt)
            z1 = _mm(qs[1], kt)
            lbs0, split0, sums0 = log_terms(z0, valid)
            after0 = _mm(split0, later2)
            lbs1, split1, sums1 = log_terms(z1, valid)
            after1 = _mm(split1, later2)
            o0 = o0 + _mm(weights(lbs0, run0, after0, valid), vb)
            o1 = o1 + _mm(weights(lbs1, run1, after1, valid), vb)
            return run0 + sums0, o0, run1 + sums1, o1

        zero = (jnp.zeros((tq, 1), F32), jnp.zeros((tq, HEAD_PAIR), F32))
        first = diagonal_blocks * qi
        carry = zero + zero
        for d in reversed(range(diagonal_blocks)):
            carry = block(first + d, carry, causal(d))
        carry = lax.fori_loop(0, first, lambda it, cr: block(first - 1 - it, cr, None), carry)
        o_ref[...] = jnp.where(head0, carry[1], carry[3])
        lt_ref[...] = jnp.where(head0, carry[0], carry[2])

    out_spec = pl.BlockSpec((tq, HEAD_PAIR), lambda p, i: (i, p))
    return pl.pallas_call(
        body, name=name, grid=(N_HEADS // 2, T // tq),
        in_specs=_attn_specs(T, tq), out_specs=[out_spec, out_spec],
        out_shape=[jax.ShapeDtypeStruct((T, SB_WIDTH), F32), jax.ShapeDtypeStruct((T, SB_WIDTH), F32)],
        scratch_shapes=[pltpu.VMEM((T // tk, HEAD_PAIR, tk), BF16), pltpu.VMEM((T, HEAD_PAIR), BF16)],
        compiler_params=_params(("arbitrary", "arbitrary")),
    )(proj, proj, proj)


def _attn_bwd(proj, do, ltot, after, *, name):
    T = proj.shape[0]
    tk = min(ATTN_K_BLOCK, T)
    tq = min(ATTN_Q_BLOCK_BWD, T)
    diagonal_blocks = tq // tk

    def body(q_ref, k_ref, v_ref, do_ref, lt_ref, after_ref, dq_ref, dk_ref, dv_ref,
             kb_scr, kt_scr, vt_scr, dkt_ref, dvt_ref):
        qi = pl.program_id(1)

        @pl.when(qi == 0)
        def _():
            kb_scr[...] = k_ref[...].astype(BF16)
            _transposed_blocks(k_ref, kt_scr, tk)
            _transposed_blocks(v_ref, vt_scr, tk)
            dkt_ref[...] = jnp.zeros_like(dkt_ref)
            dvt_ref[...] = jnp.zeros_like(dvt_ref)

        head0 = lax.broadcasted_iota(jnp.int32, (tq, HEAD_PAIR), 1) < HEAD_DIM
        q, do_, lt = q_ref[...] * ATTN_SCALE, do_ref[...], lt_ref[...]
        qs = (jnp.where(head0, q, 0.0).astype(BF16), jnp.where(head0, 0.0, q).astype(BF16))
        q_heads = (jnp.where(head0, q, 0.0), jnp.where(head0, 0.0, q))
        do_heads = (jnp.where(head0, do_, 0.0), jnp.where(head0, 0.0, do_))
        dos = tuple(d.astype(BF16) for d in do_heads)
        qts = tuple(x.T.astype(BF16) for x in q_heads)
        dots = tuple(d.T.astype(BF16) for d in do_heads)
        lts = (jnp.max(jnp.where(head0, lt, -jnp.inf), axis=1, keepdims=True),
               jnp.max(jnp.where(head0, -jnp.inf, lt), axis=1, keepdims=True))
        r = lax.broadcasted_iota(jnp.int32, (tq, tk), 0)
        c = lax.broadcasted_iota(jnp.int32, (tq, tk), 1)
        upto = (r[:tk] <= c[:tk]).astype(BF16)
        before = (r[:tk] < c[:tk]).astype(BF16)
        upto2, before2 = _rows([upto, upto]), _rows([before, before])
        causal = lambda d: (lambda rows: c[rows] + d * tk < r[rows])
        strips = _strips(tq)

        def log_terms(z, valid):
            lbs, his, los, sums = [], [], [], []
            for rows in strips:
                lb, lm = _log_sigmoids(z[rows])
                if valid is not None:
                    lm = jnp.where(valid(rows), lm, 0.0)
                hi, lo = _split_bf16(lm)
                lbs.append(lb)
                his.append(hi)
                los.append(lo)
                sums.append(jnp.sum(lm, axis=1, keepdims=True))
            return lbs, jnp.concatenate([_rows(his), _rows(los)], axis=1), _rows(sums)

        def weights(lbs, rest, lm_upto, da, valid):
            a_parts, es, his, los, sums = [], [], [], [], []
            for rows, lb in zip(strips, lbs):
                a = jnp.exp(lb + (rest[rows] - lm_upto[rows]))
                if valid is not None:
                    a = jnp.where(valid(rows), a, 0.0)
                e = da[rows] * a
                hi, lo = _split_bf16(e)
                a_parts.append(a.astype(BF16))
                es.append(e)
                his.append(hi)
                los.append(lo)
                sums.append(jnp.sum(e, axis=1, keepdims=True))
            return _rows(a_parts), es, jnp.concatenate([_rows(his), _rows(los)], axis=1), _rows(sums)

        def score_grads(lbs, es, run_e, e_before, valid):
            parts = []
            for rows, lb, e in zip(strips, lbs, es):
                beta = jnp.exp(lb)
                dz = e * (1.0 - beta) - (run_e[rows] + e_before[rows]) * beta
                if valid is not None:
                    dz = jnp.where(valid(rows), dz, 0.0)
                parts.append(dz.astype(BF16))
            return _rows(parts)

        def block(kj, carry, valid):
            off = pl.multiple_of(kj * tk, tk)
            kb, kt, vt = kb_scr[pl.ds(off, tk), :], kt_scr[kj], vt_scr[kj]
            run_lm0, run_e0, dq0, run_lm1, run_e1, dq1 = carry
            z0, da0 = _mm(qs[0], kt), _mm(dos[0], vt)
            z1, da1 = _mm(qs[1], kt), _mm(dos[1], vt)
            lbs0, split0, lm_sums0 = log_terms(z0, valid)
            lm_upto0 = _mm(split0, upto2)
            lbs1, split1, lm_sums1 = log_terms(z1, valid)
            lm_upto1 = _mm(split1, upto2)
            a0, es0, split0, e_sums0 = weights(lbs0, lts[0] - run_lm0, lm_upto0, da0, valid)
            e_before0 = _mm(split0, before2)
            a1, es1, split1, e_sums1 = weights(lbs1, lts[1] - run_lm1, lm_upto1, da1, valid)
            e_before1 = _mm(split1, before2)
            dz0 = score_grads(lbs0, es0, run_e0, e_before0, valid)
            dkt_blk = _mm(qts[0], dz0)
            dvt_blk = _mm(dots[0], a0)
            dq0 = dq0 + _mm(dz0, kb)
            dz1 = score_grads(lbs1, es1, run_e1, e_before1, valid)
            dkt_ref[kj] += dkt_blk + _mm(qts[1], dz1)
            dvt_ref[kj] += dvt_blk + _mm(dots[1], a1)
            dq1 = dq1 + _mm(dz1, kb)
            return run_lm0 + lm_sums0, run_e0 + e_sums0, dq0, run_lm1 + lm_sums1, run_e1 + e_sums1, dq1

        zero = (jnp.zeros((tq, 1), F32), jnp.zeros((tq, 1), F32), jnp.zeros((tq, HEAD_PAIR), F32))
        first = diagonal_blocks * qi
        carry = lax.fori_loop(0, first, lambda kj, cr: block(kj, cr, None), zero + zero)
        for d in range(diagonal_blocks):
            carry = block(first + d, carry, causal(d))
        dq_ref[...] = (jnp.where(head0, carry[2], carry[5]) * ATTN_SCALE).astype(BF16)

        @pl.when(qi == T // tq - 1)
        def _():
            for b in range(T // tk):
                dk_ref[b * tk:(b + 1) * tk, :] = dkt_ref[b].T.astype(BF16)
                dv_ref[b * tk:(b + 1) * tk, :] = dvt_ref[b].T.astype(BF16)

    blk = pl.BlockSpec((tq, HEAD_PAIR), lambda p, i: (i, p))
    seq = pl.BlockSpec((T, HEAD_PAIR), lambda p, i: (0, p))
    transposed = pltpu.VMEM((T // tk, HEAD_PAIR, tk), F32)
    return pl.pallas_call(
        body, name=name, grid=(N_HEADS // 2, T // tq),
        in_specs=_attn_specs(T, tq) + [blk, blk, AFTER], out_specs=[blk, seq, seq],
        out_shape=[jax.ShapeDtypeStruct((T, SB_WIDTH), BF16)] * 3,
        scratch_shapes=[pltpu.VMEM((T, HEAD_PAIR), BF16), pltpu.VMEM((T // tk, HEAD_PAIR, tk), BF16),
                        pltpu.VMEM((T // tk, HEAD_PAIR, tk), BF16), transposed, transposed],
        compiler_params=_params(("arbitrary", "arbitrary")),
    )(proj, proj, proj, do, ltot, _in_hbm(after))


def _mix_specs(T, D, tm, wbp, w_out):
    gate_col = (POOL_WIDTH + 3 * SB_WIDTH) // D
    row = lambda i: (i, 0)
    return [
        pl.BlockSpec((tm, D), row),
        pl.BlockSpec((tm, POOL_WIDTH), row),
        pl.BlockSpec((tm, SB_WIDTH), row),
        pl.BlockSpec((tm, D), lambda i: (i, gate_col)),
        pl.BlockSpec((tm, D), lambda i: (i, gate_col + 1)),
        pl.BlockSpec(wbp.shape, lambda i: (0, 0)),
        pl.BlockSpec(wbp.shape, lambda i: (0, 0)),
        pl.BlockSpec(w_out.shape, lambda i: (0, 0)),
    ]


def _mix_fwd(h, p, o, proj, wbp, wba, w_out, *, tm, name):
    T, D = h.shape
    tm = min(tm, T)

    def body(h_ref, p_ref, o_ref, glp_ref, gls_ref, wbp_ref, wba_ref, wout_ref, hout_ref, m_ref):
        yp = _mm_nt(p_ref[...].astype(BF16), wbp_ref[...])
        ys = _mm_nt(o_ref[...].astype(BF16), wba_ref[...])
        m = (jax.nn.sigmoid(glp_ref[...]) * yp + jax.nn.sigmoid(gls_ref[...]) * ys).astype(BF16)
        m_ref[...] = m
        hout_ref[...] = h_ref[...] + _mm(m, wout_ref[...])

    row = lambda i: (i, 0)
    return pl.pallas_call(
        body, name=name, grid=(T // tm,),
        in_specs=_mix_specs(T, D, tm, wbp, w_out),
        out_specs=[pl.BlockSpec((tm, D), row), pl.BlockSpec((tm, D), row)],
        out_shape=[jax.ShapeDtypeStruct((T, D), F32), jax.ShapeDtypeStruct((T, D), BF16)],
        compiler_params=_params(("arbitrary",)),
    )(h, p, o, proj, proj, wbp, wba, w_out)


def _mix_bwd(dh, p, o, proj, wbp, wba, w_out, after, *, tm, name):
    T, D = dh.shape
    tm = min(tm, T)

    def body(dh_ref, p_ref, o_ref, glp_ref, gls_ref, wbp_ref, wba_ref, wout_ref, after_ref,
             dyp_ref, dys_ref, dp_ref, do_ref, dgl_ref):
        dm = _mm_nt(dh_ref[...].astype(BF16), wout_ref[...])
        yp = _mm_nt(p_ref[...].astype(BF16), wbp_ref[...])
        ys = _mm_nt(o_ref[...].astype(BF16), wba_ref[...])
        gp = jax.nn.sigmoid(glp_ref[...])
        gs = jax.nn.sigmoid(gls_ref[...])
        dyp = (dm * gp).astype(BF16)
        dys = (dm * gs).astype(BF16)
        dyp_ref[...] = dyp
        dys_ref[...] = dys
        dgl_ref[:, :D] = (dm * yp * gp * (1.0 - gp)).astype(BF16)
        dgl_ref[:, D:] = (dm * ys * gs * (1.0 - gs)).astype(BF16)
        dp_ref[...] = _mm(dyp, wbp_ref[...])
        do_ref[...] = _mm(dys, wba_ref[...])

    row = lambda i: (i, 0)
    return pl.pallas_call(
        body, name=name, grid=(T // tm,),
        in_specs=_mix_specs(T, D, tm, wbp, w_out) + [AFTER],
        out_specs=[pl.BlockSpec((tm, D), row), pl.BlockSpec((tm, D), row), pl.BlockSpec((tm, POOL_WIDTH), row),
                   pl.BlockSpec((tm, SB_WIDTH), row), pl.BlockSpec((tm, 2 * D), row)],
        out_shape=[jax.ShapeDtypeStruct((T, D), BF16), jax.ShapeDtypeStruct((T, D), BF16),
                   jax.ShapeDtypeStruct((T, POOL_WIDTH), F32), jax.ShapeDtypeStruct((T, SB_WIDTH), F32),
                   jax.ShapeDtypeStruct((T, 2 * D), BF16)],
        compiler_params=_params(("arbitrary",)),
    )(dh, p, o, proj, proj, wbp, wba, w_out, _in_hbm(after))


def _adamw(w, g, m, v, *, name):
    R, C = w.shape
    tr = _row_tile(R, C)

    def body(w_ref, g_ref, m_ref, v_ref, d_ref, nm_ref, nv_ref):
        g_ = g_ref[...]
        m_ = ADAM_B1 * m_ref[...] + (1.0 - ADAM_B1) * g_
        v_ = ADAM_B2 * v_ref[...] + (1.0 - ADAM_B2) * (g_ * g_)
        m_hat = m_ / (1.0 - ADAM_B1 ** ADAM_STEP)
        v_hat = v_ / (1.0 - ADAM_B2 ** ADAM_STEP)
        d_ref[...] = -ADAM_LR * (m_hat / (jnp.sqrt(v_hat) + ADAM_EPS) + ADAM_WD * w_ref[...])
        nm_ref[...] = m_
        nv_ref[...] = v_

    spec = pl.BlockSpec((tr, C), lambda i: (i, 0))
    return pl.pallas_call(
        body, name=name, grid=(R // tr,), in_specs=[spec] * 4, out_specs=[spec] * 3,
        out_shape=[jax.ShapeDtypeStruct((R, C), F32)] * 3,
        compiler_params=_params(("arbitrary",)),
    )(*[_in_hbm(a) for a in (w, g, m, v)])


def _position():
    return lax.axis_index("x"), lax.axis_index("y"), lax.axis_index("c")


def _all_gather(shards, *, name, collective_id):
    n = len(shards)
    n_copies = 9

    def body(*refs):
        ins, outs = refs[:n], refs[n:2 * n]
        send_sems, recv_sems, local_sems = refs[2 * n:]
        x, y, c = _position()
        me, sibling = (x, y, c), (x, y, 1 - c)
        x_nbr, y_nbr, diagonal = (1 - x, y, c), (x, 1 - y, c), (1 - x, 1 - y, c)
        other = lambda pos: (pos[0], pos[1], 1 - c)

        barrier = pltpu.get_barrier_semaphore()
        for peer in (sibling, x_nbr, y_nbr):
            pl.semaphore_signal(barrier, inc=1, device_id=peer, device_id_type=MESH)
        pl.semaphore_wait(barrier, 3)

        def block(a, pos, half=None):
            ref = outs[a].at[4 * pos[0] + 2 * pos[1] + pos[2]]
            rows = ref.shape[0] // 2
            return ref if half is None else ref.at[pl.ds(half * rows, rows)]

        def copy(a, k, pos, to, half=None, src=None):
            return pltpu.make_async_remote_copy(
                src_ref=block(a, pos, half) if src is None else src, dst_ref=block(a, pos, half),
                send_sem=send_sems.at[n_copies * a + k], recv_sem=recv_sems.at[n_copies * a + k],
                device_id=to, device_id_type=MESH)

        started = []
        for a in range(n):
            mine = pltpu.make_async_copy(ins[a], block(a, me), local_sems.at[a])
            mine.start()
            started.append(mine)
        sends = []
        for a in range(n):
            sends += [copy(a, 1, me, x_nbr, src=ins[a]), copy(a, 2, me, y_nbr, src=ins[a]),
                      copy(a, 0, me, sibling, src=ins[a])]
        for cp in sends:
            cp.start()

        def pass_on(copies):
            for cp in copies:
                cp.start()
                sends.append(cp)

        for a in range(n):
            copy(a, 1, x_nbr, me).wait_recv()
            pass_on([copy(a, 5, x_nbr, y_nbr, half=0), copy(a, 3, x_nbr, sibling)])
            copy(a, 2, y_nbr, me).wait_recv()
            pass_on([copy(a, 6, y_nbr, x_nbr, half=1), copy(a, 4, y_nbr, sibling)])
        for a in range(n):
            copy(a, 5, diagonal, me, half=0).wait_recv()
            pass_on([copy(a, 7, diagonal, sibling, half=0)])
            copy(a, 6, diagonal, me, half=1).wait_recv()
            pass_on([copy(a, 8, diagonal, sibling, half=1)])
        for a in range(n):
            copy(a, 0, sibling, me).wait_recv()
            copy(a, 3, other(x_nbr), me).wait_recv()
            copy(a, 4, other(y_nbr), me).wait_recv()
            copy(a, 7, other(diagonal), me, half=0).wait_recv()
            copy(a, 8, other(diagonal), me, half=1).wait_recv()
        for cp in sends:
            cp.wait_send()
        for cp in started:
            cp.wait()

    return pl.kernel(
        body, name=name,
        out_type=[jax.ShapeDtypeStruct((N_DEV,) + s.shape, s.dtype) for s in shards],
        mesh=plsc.ScalarSubcoreMesh(axis_name="sequencer", num_cores=1),
        scratch_types=[pltpu.SemaphoreType.DMA((n_copies * n,)), pltpu.SemaphoreType.DMA((n_copies * n,)),
                       pltpu.SemaphoreType.DMA((n,))],
        compiler_params=pltpu.CompilerParams(collective_id=collective_id),
    )(*shards)


def _chip_sums(grads, *, name):
    _, R, C = grads.shape
    rc = 128 if R % 128 == 0 else R

    def body(g_ref, partial, out_ref, mine, theirs, send_sems, recv_sems, local_sems):
        x, y, c = _position()
        my_chip = 2 * x + y

        def swap(s):
            return pltpu.make_async_remote_copy(
                src_ref=g_ref.at[2 * s + (1 - c)], dst_ref=theirs.at[s],
                send_sem=send_sems.at[s], recv_sem=recv_sems.at[s],
                device_id=(x, y, 1 - c), device_id_type=MESH)

        def load(s):
            return pltpu.make_async_copy(g_ref.at[2 * s + c], mine.at[s], local_sems.at[s])

        for s in range(4):
            swap(s).start()
            load(s).start()

        def chip_sum(chip, rows):
            return mine[chip, rows, :].astype(F32) + theirs[chip, rows, :].astype(F32)

        for s in range(4):
            load(s).wait()
            swap(s).wait_recv()

            @pl.when(s == my_chip)
            def _():
                @pl.loop(0, R // rc)
                def _(t):
                    rows = pl.ds(pl.multiple_of(t * rc, rc), rc)
                    out_ref[rows, :] = chip_sum(s, rows)

            @pl.when(s != my_chip)
            def _():
                @pl.loop(0, R // rc)
                def _(t):
                    rows = pl.ds(pl.multiple_of(t * rc, rc), rc)
                    partial[(s ^ my_chip) - 1, rows, :] = chip_sum(s, rows).astype(BF16)

        for s in range(4):
            swap(s).wait_send()

    vmem = pl.BlockSpec(memory_space=pltpu.VMEM)
    return pl.pallas_call(
        body, name=name,
        in_specs=[pl.BlockSpec(memory_space=pl.ANY)], out_specs=[vmem, vmem],
        out_shape=[jax.ShapeDtypeStruct((3, R, C), BF16), jax.ShapeDtypeStruct((R, C), F32)],
        scratch_shapes=[
            pltpu.VMEM((4, R, C), BF16), pltpu.VMEM((4, R, C), BF16),
            pltpu.SemaphoreType.DMA((4,)), pltpu.SemaphoreType.DMA((4,)), pltpu.SemaphoreType.DMA((4,)),
        ],
        compiler_params=_params(),
    )(grads)


def _cross_chips(partials, *, name, collective_id):
    n = len(partials)

    def body(*refs):
        ins, outs = refs[:n], refs[n:2 * n]
        send_sems, recv_sems = refs[2 * n:]
        x, y, c = _position()
        my_chip = 2 * x + y
        peers = [((my_chip ^ j) // 2, (my_chip ^ j) % 2, c) for j in (1, 2, 3)]

        barrier = pltpu.get_barrier_semaphore()
        for peer in peers:
            pl.semaphore_signal(barrier, inc=1, device_id=peer, device_id_type=MESH)
        pl.semaphore_wait(barrier, 3)

        copies = [
            pltpu.make_async_remote_copy(
                src_ref=ins[a].at[j], dst_ref=outs[a].at[j],
                send_sem=send_sems.at[3 * a + j], recv_sem=recv_sems.at[3 * a + j],
                device_id=peers[j], device_id_type=MESH)
            for a in range(n) for j in range(3)]
        for cp in copies:
            cp.start()
        for cp in copies:
            cp.wait_recv()
        for cp in copies:
            cp.wait_send()

    return pl.kernel(
        body, name=name,
        out_type=[jax.ShapeDtypeStruct(p.shape, p.dtype) for p in partials],
        mesh=plsc.ScalarSubcoreMesh(axis_name="sequencer", num_cores=1),
        scratch_types=[pltpu.SemaphoreType.DMA((3 * n,)), pltpu.SemaphoreType.DMA((3 * n,))],
        compiler_params=pltpu.CompilerParams(collective_id=collective_id),
    )(*partials)


def _cross_chips_and_gather(partial, slab, *, name, collective_id):
    def body(part_ref, slab_ref, landed_ref, slabs_ref, send_sems, recv_sems, local_sem):
        x, y, c = _position()
        me, my_chip = 4 * x + 2 * y + c, 2 * x + y
        others = [me ^ k for k in range(1, N_DEV)]
        ids = [(o // 4, (o // 2) % 2, o % 2) for o in others]

        barrier = pltpu.get_barrier_semaphore()
        for peer in ids:
            pl.semaphore_signal(barrier, inc=1, device_id=peer, device_id_type=MESH)
        pl.semaphore_wait(barrier, N_DEV - 1)

        mine = pltpu.make_async_copy(slab_ref, slabs_ref.at[me], local_sem)
        mine.start()
        sends = [
            pltpu.make_async_remote_copy(
                src_ref=part_ref.at[j], dst_ref=landed_ref.at[j], send_sem=send_sems.at[j], recv_sem=recv_sems.at[j],
                device_id=((my_chip ^ (j + 1)) // 2, (my_chip ^ (j + 1)) % 2, c), device_id_type=MESH)
            for j in range(3)]
        sends += [
            pltpu.make_async_remote_copy(
                src_ref=slab_ref, dst_ref=slabs_ref.at[me], send_sem=send_sems.at[3 + k], recv_sem=recv_sems.at[3 + k],
                device_id=ids[k], device_id_type=MESH)
            for k in range(N_DEV - 1)]
        arrivals = sends[:3] + [
            pltpu.make_async_remote_copy(
                src_ref=slab_ref, dst_ref=slabs_ref.at[others[k]], send_sem=send_sems.at[3 + k],
                recv_sem=recv_sems.at[3 + k], device_id=ids[k], device_id_type=MESH)
            for k in range(N_DEV - 1)]
        for cp in sends:
            cp.start()
        for cp in arrivals:
            cp.wait_recv()
        for cp in sends:
            cp.wait_send()
        mine.wait()

    n_sems = 3 + N_DEV - 1
    return pl.kernel(
        body, name=name,
        out_type=[jax.ShapeDtypeStruct(partial.shape, partial.dtype),
                  jax.ShapeDtypeStruct((N_DEV,) + slab.shape, slab.dtype)],
        mesh=plsc.ScalarSubcoreMesh(axis_name="sequencer", num_cores=1),
        scratch_types=[pltpu.SemaphoreType.DMA((n_sems,)), pltpu.SemaphoreType.DMA((n_sems,)), pltpu.SemaphoreType.DMA],
        compiler_params=pltpu.CompilerParams(collective_id=collective_id),
    )(partial, slab)


def _sum_devices(gathered, after, *, name):
    _, R, C = gathered.shape

    def body(in_ref, after_ref, out_ref):
        total = in_ref[0]
        for d in range(1, N_DEV):
            total = total + in_ref[d]
        out_ref[...] = total

    return pl.pallas_call(
        body, name=name, grid=(1,),
        in_specs=[pl.BlockSpec((N_DEV, R, C), lambda i: (0, 0, 0)), AFTER],
        out_specs=pl.BlockSpec((R, C), lambda i: (0, 0)),
        out_shape=jax.ShapeDtypeStruct((R, C), F32),
        compiler_params=_params(("arbitrary",)),
    )(gathered, _in_hbm(after))


def _owner_sum(own, landed, after, *, name):
    R, C = own.shape
    tr = _row_tile(R, C)

    def body(own_ref, landed_ref, after_ref, out_ref):
        total = own_ref[...]
        for j in range(3):
            total = total + landed_ref[j].astype(F32)
        out_ref[...] = total

    return pl.pallas_call(
        body, name=name, grid=(R // tr,),
        in_specs=[pl.BlockSpec((tr, C), lambda i: (i, 0)), pl.BlockSpec((3, tr, C), lambda i: (0, i, 0)), AFTER],
        out_specs=pl.BlockSpec((tr, C), lambda i: (i, 0)),
        out_shape=jax.ShapeDtypeStruct((R, C), F32),
        compiler_params=_params(("arbitrary",)),
    )(_in_hbm(own), _in_hbm(landed), _in_hbm(after))


def _local_step(x, target, norms, pool_w_group, pool_scale, wgu1, wd1, w_in, wbp, wba, w_out, wgu2, wd2, exchange):
    n1g, nmg, n2g, nfg = norms
    D = x.shape[1]
    gu1, hid1 = _ffn_up(x, n1g, wgu1, tm=1024, name="ffn1_up")
    h1 = _ffn_down(x, hid1, wd1, tm=512, name="ffn1_down")
    un, proj = _inproj_fwd(h1, nmg, w_in, tm=1024, name="inproj_fwd")
    p = _pool_fwd(proj, pool_w_group, pool_scale, name="pool_fwd")
    o, ltot = _attn_fwd(proj, name="attn_fwd")
    h2, m = _mix_fwd(h1, p, o, proj, wbp, wba, w_out, tm=256, name="mix_fwd")
    gu2, hid2 = _ffn_up(h2, n2g, wgu2, tm=1024, name="ffn2_up")
    h3 = _ffn_down(h2, hid2, wd2, tm=512, name="ffn2_down")
    dh3, df2, loss, d_nf = _loss_bwd(h3, target, nfg, tm=256, name="loss_bwd")

    d_wd2 = _wgrad_down(hid2, df2, tk=WGRAD_TOKENS, name="ffn2_wgrad_down")
    (g_wd2,), token = exchange("ffn2_down", [d_wd2.reshape(N_DEV, FF_SHARD_PAD, D)])
    dh2, d_n2, n2, dgu2 = _ffn_bwd(dh3, df2, h2, n2g, gu2, wgu2, wd2, token, tm=512, name="ffn2_bwd")
    d_wgu2 = _wgrad_gate_up(n2, dgu2, tk=WGRAD_TOKENS, name="ffn2_wgrad_gate_up")
    (g_wgu2,), token = exchange("ffn2_gate_up", [d_wgu2])

    dyp, dys, dp, do, dgl = _mix_bwd(dh2, p, o, proj, wbp, wba, w_out, token, tm=256, name="mix_bwd")
    d_wout = _wgrad_full(m, dh2, tk=WGRAD_TOKENS, name="wgrad_out")
    d_wbp = _wgrad_full(dyp, p, tk=WGRAD_TOKENS, name="wgrad_branch_pool")
    d_wba = _wgrad_full(dys, o, tk=WGRAD_TOKENS, name="wgrad_branch_attn")
    by_owner = lambda g: g.reshape(N_DEV, g.shape[0] // N_DEV, g.shape[1])
    (g_wbp, g_wba, g_wout), token = exchange("mix", [by_owner(d_wbp), by_owner(d_wba), by_owner(d_wout)])
    dxp, d_wgroup, d_scale = _pool_bwd(dp, proj, pool_w_group, pool_scale, name="pool_bwd")
    dq, dk, dv = _attn_bwd(proj, do, ltot, token, name="attn_bwd")
    dproj = jnp.concatenate([dxp, dq, dk, dv, dgl], axis=1)
    d_win = _wgrad_in(dproj, un, tk=WGRAD_TOKENS, name="wgrad_in")
    (g_win,), token_in = exchange("w_in", [d_win])
    dh1, df1, d_nm = _inproj_bwd(dproj, dh2, h1, nmg, w_in, tm=512, name="inproj_bwd")
    d_wd1 = _wgrad_down(hid1, df1, tk=WGRAD_TOKENS, name="ffn1_wgrad_down")
    (g_wd1,), token_down = exchange("ffn1_down", [d_wd1.reshape(N_DEV, FF_SHARD_PAD, D)])
    token = (token_down[(0,) * token_down.ndim] + token_in[(0,) * token_in.ndim]).reshape(1, 1)

    dx, d_n1, n1, dgu1 = _ffn_bwd(dh1, df1, x, n1g, gu1, wgu1, wd1, token, tm=512, name="ffn1_bwd")
    d_wgu1_a = _wgrad_gate_up(n1, dgu1, tk=WGRAD_TOKENS, name="ffn1_wgrad_gate_up_a", part=0, parts=2)
    (g_wgu1_a,), token = exchange("ffn1_gate_up_a", [d_wgu1_a])
    d_wgu1_b = _wgrad_gate_up(n1, dgu1, tk=WGRAD_TOKENS, name="ffn1_wgrad_gate_up_b", part=1, parts=2)
    (g_wgu1_b, replicated), token = exchange("last", [d_wgu1_b, d_n1, d_nm, d_n2, d_nf, d_scale, d_wgroup, loss])
    g_wgu1 = (g_wgu1_a, g_wgu1_b)

    sharded = (g_wgu1, g_wd1, g_win, g_wbp, g_wba, g_wout, g_wgu2, g_wd2)
    return dx, sharded, replicated, token


def _hidden_major(w):
    return jnp.swapaxes(w[0], 0, 1)


def _pad_gate_up(wt):
    d = wt.shape[1]
    wt = wt.astype(BF16).reshape(2, FF_SHARD, d)
    return jnp.pad(wt, ((0, 0), (0, FF_SHARD_PAD - FF_SHARD), (0, 0))).reshape(2 * FF_SHARD_PAD, d)


def _unpad_gate_up(gt):
    d = gt.shape[1]
    return gt.reshape(2, FF_SHARD_PAD, d)[:, :FF_SHARD].reshape(2 * FF_SHARD, d)


def _pad_down(w):
    return jnp.pad(w.astype(BF16), ((0, FF_SHARD_PAD - FF_SHARD), (0, 0)))


def kernel(x, ffn1_norm, ffn1_w_gate_up, ffn1_w_down, mix_norm, w_in, pool_w_group, pool_scale, w_branch_pool, w_branch_attn, w_out, ffn2_norm, ffn2_w_gate_up, ffn2_w_down, final_norm, loss_target, m_ffn1_norm, m_ffn1_w_gate_up, m_ffn1_w_down, m_mix_norm, m_w_in, m_pool_w_group, m_pool_scale, m_w_branch_pool, m_w_branch_attn, m_w_out, m_ffn2_norm, m_ffn2_w_gate_up, m_ffn2_w_down, m_final_norm, v_ffn1_norm, v_ffn1_w_gate_up, v_ffn1_w_down, v_mix_norm, v_w_in, v_pool_w_group, v_pool_scale, v_w_branch_pool, v_w_branch_attn, v_w_out, v_ffn2_norm, v_ffn2_w_gate_up, v_ffn2_w_down, v_final_norm):
    D = x.shape[-1]
    weights = dict(ffn1_norm=ffn1_norm, ffn1_w_gate_up=ffn1_w_gate_up, ffn1_w_down=ffn1_w_down, mix_norm=mix_norm,
                   w_in=w_in, pool_w_group=pool_w_group, pool_scale=pool_scale, w_branch_pool=w_branch_pool,
                   w_branch_attn=w_branch_attn, w_out=w_out, ffn2_norm=ffn2_norm, ffn2_w_gate_up=ffn2_w_gate_up,
                   ffn2_w_down=ffn2_w_down, final_norm=final_norm)
    first = dict(ffn1_norm=m_ffn1_norm, ffn1_w_gate_up=m_ffn1_w_gate_up, ffn1_w_down=m_ffn1_w_down,
                 mix_norm=m_mix_norm, w_in=m_w_in, pool_w_group=m_pool_w_group, pool_scale=m_pool_scale,
                 w_branch_pool=m_w_branch_pool, w_branch_attn=m_w_branch_attn, w_out=m_w_out,
                 ffn2_norm=m_ffn2_norm, ffn2_w_gate_up=m_ffn2_w_gate_up, ffn2_w_down=m_ffn2_w_down,
                 final_norm=m_final_norm)
    second = dict(ffn1_norm=v_ffn1_norm, ffn1_w_gate_up=v_ffn1_w_gate_up, ffn1_w_down=v_ffn1_w_down,
                  mix_norm=v_mix_norm, w_in=v_w_in, pool_w_group=v_pool_w_group, pool_scale=v_pool_scale,
                  w_branch_pool=v_w_branch_pool, w_branch_attn=v_w_branch_attn, w_out=v_w_out,
                  ffn2_norm=v_ffn2_norm, ffn2_w_gate_up=v_ffn2_w_gate_up, ffn2_w_down=v_ffn2_w_down,
                  final_norm=v_final_norm)
    order = list(weights)

    wgu1, = _all_gather([_pad_gate_up(_hidden_major(ffn1_w_gate_up))], name="all_gather_ffn1_gate_up", collective_id=0)
    wd1, = _all_gather([_pad_down(ffn1_w_down[0])], name="all_gather_ffn1_down", collective_id=10)
    transposed = lambda w: jnp.swapaxes(w[0], 0, 1).astype(BF16)
    win_g, = _all_gather([transposed(w_in)], name="all_gather_w_in", collective_id=1)
    wbp_g, wba_g = _all_gather([transposed(w_branch_pool), transposed(w_branch_attn)],
                               name="all_gather_branches", collective_id=2)
    wout_g, = _all_gather([w_out[0].astype(BF16)], name="all_gather_w_out", collective_id=11)
    wgu2, wd2 = _all_gather([_pad_gate_up(_hidden_major(ffn2_w_gate_up)), _pad_down(ffn2_w_down[0])],
                            name="all_gather_ffn2", collective_id=3)
    whole = lambda g: g.reshape(g.shape[0] * g.shape[1], g.shape[2])
    wd1, wd2, win_g, wbp_g, wba_g, wout_g = (whole(g) for g in (wd1, wd2, win_g, wbp_g, wba_g, wout_g))

    cross_ids = {"ffn2_down": 4, "ffn2_gate_up": 5, "mix": 6, "ffn1_down": 7, "w_in": 8, "last": 9,
                 "ffn1_gate_up_a": 12}
    small = ["ffn1_norm", "mix_norm", "ffn2_norm", "final_norm", "pool_scale", "pool_w_group"]

    def tile_rows(a):
        a = a.reshape(-1, 128)
        return jnp.pad(a, ((0, -a.shape[0] % 8), (0, 0)))

    def exchange(tag, group):
        if tag == "last":
            slab = jnp.concatenate([tile_rows(g) for g in group[1:-1]] + [jnp.broadcast_to(group[-1], (8, 128))], axis=0)
            partial, own = _chip_sums(group[0], name="chip_sums_last")
            landed, slabs = _cross_chips_and_gather(partial, slab, name="cross_chips_last", collective_id=cross_ids[tag])
            return [(own, landed), slabs], own
        sums = [_chip_sums(g, name=f"chip_sums_{tag}_{i}") for i, g in enumerate(group)]
        landed = _cross_chips([s[0] for s in sums], name="cross_chips_" + tag, collective_id=cross_ids[tag])
        token = sums[0][1] if len(sums) == 1 else sum(s[1][0, 0] for s in sums).reshape(1, 1)
        return [(s[1], l) for s, l in zip(sums, landed)], token

    norms = (ffn1_norm, mix_norm, ffn2_norm, final_norm.reshape(1, D))
    dx, sharded, slabs, last = _local_step(
        x[0], loss_target[0], norms, pool_w_group[0], pool_scale, wgu1, wd1, win_g, wbp_g, wba_g, wout_g, wgu2, wd2,
        exchange)
    names = ["ffn1_w_gate_up", "ffn1_w_down", "w_in", "w_branch_pool", "w_branch_attn", "w_out",
             "ffn2_w_gate_up", "ffn2_w_down"]
    handles = dict(zip(names, sharded))
    grads, delta, new_m, new_v = {}, {}, {}, {}
    after = last
    for k in ("ffn2_w_down", "ffn2_w_gate_up", "w_branch_pool", "w_branch_attn", "w_out", "w_in", "ffn1_w_down",
              "ffn1_w_gate_up"):
        hidden_major = k.endswith("w_gate_up")
        if isinstance(handles[k][0], tuple):
            first_half = _owner_sum(*handles[k][0], after, name="owner_sum_" + k + "_a")
            second_half = _owner_sum(*handles[k][1], first_half, name="owner_sum_" + k + "_b")
            g = jnp.concatenate([first_half[:FF_SHARD], second_half[:FF_SHARD]], axis=0)
        else:
            g = _owner_sum(*handles[k], after, name="owner_sum_" + k)
            if hidden_major:
                g = _unpad_gate_up(g)
            elif k in ("w_in", "w_branch_pool", "w_branch_attn"):
                g = jnp.swapaxes(g, 0, 1)
            else:
                g = g[:weights[k].shape[1]]
        view = _hidden_major if hidden_major else (lambda a: a[0])
        back = (lambda a: jnp.swapaxes(a, 0, 1)[None]) if hidden_major else (lambda a: a[None])
        out = _adamw(view(weights[k]), g, view(first[k]), view(second[k]), name="adamw_" + k)
        after = out[0]
        grads[k] = back(g)
        delta[k], new_m[k], new_v[k] = (back(a) for a in out)

    rows = [weights[k].size // 128 for k in small]
    padded_rows = [-(-r // 8) * 8 for r in rows]
    starts = [sum(padded_rows[:i]) for i in range(len(rows) + 1)]
    total = _sum_devices(slabs, after, name="sum_replicated")
    loss_out = total[starts[-1], 0]
    small_w = jnp.concatenate([tile_rows(weights[k]) for k in small], axis=0)
    small_m = jnp.concatenate([tile_rows(first[k]) for k in small], axis=0)
    small_v = jnp.concatenate([tile_rows(second[k]) for k in small], axis=0)
    small_out = _adamw(small_w, total[:starts[-1]], small_m, small_v, name="adamw_replicated")
    for name_, start, n_rows in zip(small, starts, rows):
        shape = weights[name_].shape
        grads[name_] = total[start:start + n_rows].reshape(shape)
        delta[name_], new_m[name_], new_v[name_] = (a[start:start + n_rows].reshape(shape) for a in small_out)

    return (loss_out, dx[None], *[grads[k] for k in order], *[delta[k] for k in order],
            *[new_m[k] for k in order], *[new_v[k] for k in order])
```

```python
import functools

import jax
import jax.numpy as jnp
from jax import lax
from jax.experimental import pallas as pl
from jax.experimental.pallas import tpu as pltpu
from jax.experimental.pallas import tpu_sc as plsc

F32 = jnp.float32
BF16 = jnp.bfloat16
MESH = pl.DeviceIdType.MESH

RMS_EPS = 1e-6
N_DEV = 8
N_HEADS = 8
HEAD_DIM = 64
HEAD_PAIR = 2 * HEAD_DIM
POOL_WINDOWS = (2, 4, 8, 16)
POOL_GROUP = 128
POOL_WIDTH = 512
SB_WIDTH = 512
FF_SHARD = 352
FF_SHARD_PAD = 384
ATTN_K_BLOCK = 256
ATTN_Q_BLOCK_FWD = 512
ATTN_Q_BLOCK_BWD = 256
ATTN_SCALE = 0.125

ADAM_LR = 0.001
ADAM_B1 = 0.9
ADAM_B2 = 0.999
ADAM_EPS = 1e-08
ADAM_WD = 0.01
ADAM_STEP = 10

VMEM_LIMIT = 48 << 20
WGRAD_TOKENS = 2048


def _params(dims=None):
    return pltpu.CompilerParams(dimension_semantics=dims, vmem_limit_bytes=VMEM_LIMIT)


def _mm(a, b):
    return jnp.dot(a, b, preferred_element_type=F32)


def _mm_nt(a, b):
    return lax.dot_general(a, b, (((1,), (1,)), ((), ())), preferred_element_type=F32)


def _mm_tn(a, b):
    return lax.dot_general(a, b, (((0,), (0,)), ((), ())), preferred_element_type=F32)


def _row_tile(rows, cols):
    limit = max(8, (512 * 1024) // cols)
    return max(t for t in range(8, rows + 1, 8) if rows % t == 0 and (t <= limit or t == 8))


def _rstd(xf):
    return lax.rsqrt(jnp.mean(xf * xf, axis=-1, keepdims=True) + RMS_EPS)


def _rms_bwd(xf, gain, dn):
    r = _rstd(xf)
    xh = xf * r
    dgain = jnp.sum(dn * xh, axis=0, keepdims=True)
    dxh = dn * gain
    dx = r * (dxh - xh * jnp.mean(dxh * xh, axis=-1, keepdims=True))
    return dx, dgain


def _ffn_up(x, gain, wgu, *, tm, name):
    T, D = x.shape
    tm = min(tm, T)
    nb, bw = wgu.shape[0] // 2, wgu.shape[1]

    def body(x_ref, gain_ref, wg_ref, wu_ref, gu_ref, hid_ref, n_scr):
        @pl.when(pl.program_id(1) == 0)
        def _():
            xf = x_ref[...]
            n_scr[...] = (xf * _rstd(xf) * gain_ref[...]).astype(BF16)

        halves = (pl.ds(0, tm // 2), pl.ds(tm // 2, tm // 2))
        wg, wu = wg_ref[...], wu_ref[...]
        gus = [(_mm_nt(n_scr[rows, :], wg), _mm_nt(n_scr[rows, :], wu)) for rows in halves]
        for rows, (g, u) in zip(halves, gus):
            gu_ref[0, rows, :] = g.astype(BF16)
            gu_ref[1, rows, :] = u.astype(BF16)
            hid_ref[rows, :] = (g * jax.nn.sigmoid(g) * u).astype(BF16)

    return pl.pallas_call(
        body, name=name, grid=(T // tm, nb),
        in_specs=[
            pl.BlockSpec((tm, D), lambda i, j: (i, 0)),
            pl.BlockSpec((1, D), lambda i, j: (0, 0)),
            pl.BlockSpec((None, bw, D), lambda i, j: (j, 0, 0)),
            pl.BlockSpec((None, bw, D), lambda i, j: (j + nb, 0, 0)),
        ],
        out_specs=[
            pl.BlockSpec((2, tm, bw), lambda i, j: (0, i, j)),
            pl.BlockSpec((tm, bw), lambda i, j: (i, j)),
        ],
        out_shape=[jax.ShapeDtypeStruct((2, T, nb * bw), BF16), jax.ShapeDtypeStruct((T, nb * bw), BF16)],
        scratch_shapes=[pltpu.VMEM((tm, D), BF16)],
        compiler_params=_params(("arbitrary", "arbitrary")),
    )(x, gain, wgu, wgu)


def _ffn_down(x, hid, wd, *, tm, name):
    T, D = x.shape
    tm = min(tm, T)
    F = hid.shape[1]

    def body(x_ref, hid_ref, wd_ref, h_ref):
        h_ref[...] = x_ref[...] + 0.5 * _mm(hid_ref[...], wd_ref[...])

    return pl.pallas_call(
        body, name=name, grid=(T // tm,),
        in_specs=[
            pl.BlockSpec((tm, D), lambda i: (i, 0)),
            pl.BlockSpec((tm, F), lambda i: (i, 0)),
            pl.BlockSpec((F, D), lambda i: (0, 0)),
        ],
        out_specs=pl.BlockSpec((tm, D), lambda i: (i, 0)),
        out_shape=jax.ShapeDtypeStruct((T, D), F32),
        compiler_params=_params(("arbitrary",)),
    )(x, hid, wd)


AFTER = pl.BlockSpec(memory_space=pltpu.HBM)


def _in_hbm(token):
    return pltpu.with_memory_space_constraint(token, pltpu.HBM)


def _ffn_bwd(dh, df, x, gain, gu, wgu, wd, after, *, tm, name):
    T, D = x.shape
    tm = min(tm, T)
    nb, bw = wgu.shape[0] // 2, wgu.shape[1]

    def body(dh_ref, df_ref, x_ref, gain_ref, gu_ref, wg_ref, wu_ref, wd_ref, after_ref,
             dx_ref, dgain_ref, n_ref, dgu_ref, dn_acc):
        i, j = pl.program_id(0), pl.program_id(1)

        @pl.when(j == 0)
        def _():
            xf = x_ref[...]
            n_ref[...] = (xf * _rstd(xf) * gain_ref[...]).astype(BF16)
            dn_acc[...] = jnp.zeros_like(dn_acc)

        @pl.when((i == 0) & (j == 0))
        def _():
            dgain_ref[...] = jnp.zeros_like(dgain_ref)

        halves = (pl.ds(0, tm // 2), pl.ds(tm // 2, tm // 2))
        wd, wg, wu = wd_ref[...], wg_ref[...], wu_ref[...]
        dhids = [_mm_nt(df_ref[rows, :], wd) for rows in halves]
        for rows, dhid in zip(halves, dhids):
            g = gu_ref[0, rows, :].astype(F32)
            u = gu_ref[1, rows, :].astype(F32)
            s = jax.nn.sigmoid(g)
            silu = g * s
            dg = (dhid * u * (s * (1.0 + g * (1.0 - s)))).astype(BF16)
            du = (dhid * silu).astype(BF16)
            dgu_ref[0, rows, :] = dg
            dgu_ref[1, rows, :] = du
            dn_acc[rows, :] += _mm(dg, wg) + _mm(du, wu)

        @pl.when(j == nb - 1)
        def _():
            dx, dgain = _rms_bwd(x_ref[...], gain_ref[...], dn_acc[...])
            dx_ref[...] = dh_ref[...] + dx
            dgain_ref[...] += dgain

    row = lambda i, j: (i, 0)
    return pl.pallas_call(
        body, name=name, grid=(T // tm, nb),
        in_specs=[
            pl.BlockSpec((tm, D), row),
            pl.BlockSpec((tm, D), row),
            pl.BlockSpec((tm, D), row),
            pl.BlockSpec((1, D), lambda i, j: (0, 0)),
            pl.BlockSpec((2, tm, bw), lambda i, j: (0, i, j)),
            pl.BlockSpec((None, bw, D), lambda i, j: (j, 0, 0)),
            pl.BlockSpec((None, bw, D), lambda i, j: (j + nb, 0, 0)),
            pl.BlockSpec((bw, D), lambda i, j: (j, 0)),
            AFTER,
        ],
        out_specs=[
            pl.BlockSpec((tm, D), row),
            pl.BlockSpec((1, D), lambda i, j: (0, 0)),
            pl.BlockSpec((tm, D), row),
            pl.BlockSpec((2, tm, bw), lambda i, j: (0, i, j)),
        ],
        out_shape=[
            jax.ShapeDtypeStruct((T, D), F32),
            jax.ShapeDtypeStruct((1, D), F32),
            jax.ShapeDtypeStruct((T, D), BF16),
            jax.ShapeDtypeStruct((2, T, nb * bw), BF16),
        ],
        scratch_shapes=[pltpu.VMEM((tm, D), F32)],
        compiler_params=_params(("arbitrary", "arbitrary")),
    )(dh, df, x, gain, gu, wgu, wgu, wd, _in_hbm(after))


def _wgrad(a, b, *, grid, a_spec, b_spec, out_spec, out_shape, acc_shape, name):
    nk = grid[2]

    def body(a_ref, b_ref, o_ref, acc):
        k = pl.program_id(2)

        @pl.when(k == 0)
        def _():
            acc[...] = jnp.zeros_like(acc)

        acc[...] += _mm_tn(a_ref[...].astype(BF16), b_ref[...].astype(BF16))

        @pl.when(k == nk - 1)
        def _():
            o_ref[...] = acc[...].astype(o_ref.dtype)

    return pl.pallas_call(
        body, name=name, grid=grid, in_specs=[a_spec, b_spec], out_specs=out_spec,
        out_shape=jax.ShapeDtypeStruct(out_shape, BF16),
        scratch_shapes=[pltpu.VMEM(acc_shape, F32)],
        compiler_params=_params(("arbitrary", "arbitrary", "arbitrary")),
    )(a, b)


def _wgrad_gate_up(n, dgu, *, tk, name, part=0, parts=1):
    T, D = n.shape
    tk = min(tk, T)
    owner_rows = FF_SHARD_PAD * 2
    nb = dgu.shape[2] // owner_rows
    bw = owner_rows // parts
    return _wgrad(
        dgu, n, grid=(2 * nb, 1, T // tk), name=name,
        a_spec=pl.BlockSpec((None, tk, bw), lambda m, c, k: (m // nb, k, parts * (m % nb) + part)),
        b_spec=pl.BlockSpec((tk, D), lambda m, c, k: (k, 0)),
        out_spec=pl.BlockSpec((None, bw, D), lambda m, c, k: (m, 0, 0)),
        out_shape=(2 * nb, bw, D), acc_shape=(bw, D))


def _wgrad_down(hid, df, *, tk, name):
    T, D = df.shape
    tk = min(tk, T)
    bw = FF_SHARD_PAD * 2
    nb = hid.shape[1] // bw
    return _wgrad(
        hid, df, grid=(nb, 1, T // tk), name=name,
        a_spec=pl.BlockSpec((tk, bw), lambda m, c, k: (k, m)),
        b_spec=pl.BlockSpec((tk, D), lambda m, c, k: (k, 0)),
        out_spec=pl.BlockSpec((bw, D), lambda m, c, k: (m, 0)),
        out_shape=(nb * bw, D), acc_shape=(bw, D))


def _wgrad_in(dproj, un, *, tk, name):
    T, D = un.shape
    tk = min(tk, T)
    bw = dproj.shape[1] // N_DEV
    return _wgrad(
        dproj, un, grid=(N_DEV, 1, T // tk), name=name,
        a_spec=pl.BlockSpec((tk, bw), lambda m, c, k: (k, m)),
        b_spec=pl.BlockSpec((tk, D), lambda m, c, k: (k, 0)),
        out_spec=pl.BlockSpec((None, bw, D), lambda m, c, k: (m, 0, 0)),
        out_shape=(N_DEV, bw, D), acc_shape=(bw, D))


def _wgrad_full(a, b, *, tk, name):
    T, M = a.shape
    tk = min(tk, T)
    N = b.shape[1]
    return _wgrad(
        a, b, grid=(1, 1, T // tk), name=name,
        a_spec=pl.BlockSpec((tk, M), lambda m, c, k: (k, 0)),
        b_spec=pl.BlockSpec((tk, N), lambda m, c, k: (k, 0)),
        out_spec=pl.BlockSpec((M, N), lambda m, c, k: (0, 0)), out_shape=(M, N), acc_shape=(M, N))


def _loss_bwd(h, target, gain, *, tm, name):
    T, D = h.shape
    tm = min(tm, T)

    def body(h_ref, t_ref, gain_ref, dh_ref, df_ref, loss_ref, dgain_ref):
        @pl.when(pl.program_id(0) == 0)
        def _():
            loss_ref[...] = jnp.zeros_like(loss_ref)
            dgain_ref[...] = jnp.zeros_like(dgain_ref)

        xf = h_ref[...]
        gain = gain_ref[...]
        err = xf * _rstd(xf) * gain - t_ref[...]
        loss_ref[...] += 0.5 * jnp.sum(jnp.mean(err * err, axis=-1, keepdims=True), axis=0, keepdims=True)
        dx, dgain = _rms_bwd(xf, gain, err * (1.0 / D))
        dh_ref[...] = dx
        df_ref[...] = (0.5 * dx).astype(BF16)
        dgain_ref[...] += dgain

    row = lambda i: (i, 0)
    fixed = lambda i: (0, 0)
    return pl.pallas_call(
        body, name=name, grid=(T // tm,),
        in_specs=[pl.BlockSpec((tm, D), row), pl.BlockSpec((tm, D), row), pl.BlockSpec((1, D), fixed)],
        out_specs=[pl.BlockSpec((tm, D), row), pl.BlockSpec((tm, D), row), pl.BlockSpec((1, 128), fixed),
                   pl.BlockSpec((1, D), fixed)],
        out_shape=[jax.ShapeDtypeStruct((T, D), F32), jax.ShapeDtypeStruct((T, D), BF16),
                   jax.ShapeDtypeStruct((1, 128), F32), jax.ShapeDtypeStruct((1, D), F32)],
        compiler_params=_params(("arbitrary",)),
    )(h, target, gain)


def _inproj_fwd(h, gain, w_in_t, *, tm, name):
    T, D = h.shape
    tm = min(tm, T)
    bn = D
    nb = w_in_t.shape[0] // bn

    def body(h_ref, gain_ref, wt_ref, un_ref, proj_ref):
        @pl.when(pl.program_id(1) == 0)
        def _():
            xf = h_ref[...]
            un_ref[...] = (xf * _rstd(xf) * gain_ref[...]).astype(BF16)

        proj_ref[...] = _mm_nt(un_ref[...], wt_ref[...])

    return pl.pallas_call(
        body, name=name, grid=(T // tm, nb),
        in_specs=[
            pl.BlockSpec((tm, D), lambda i, j: (i, 0)),
            pl.BlockSpec((1, D), lambda i, j: (0, 0)),
            pl.BlockSpec((bn, D), lambda i, j: (j, 0)),
        ],
        out_specs=[pl.BlockSpec((tm, D), lambda i, j: (i, 0)), pl.BlockSpec((tm, bn), lambda i, j: (i, j))],
        out_shape=[jax.ShapeDtypeStruct((T, D), BF16), jax.ShapeDtypeStruct((T, nb * bn), F32)],
        compiler_params=_params(("arbitrary", "arbitrary")),
    )(h, gain, w_in_t)


def _inproj_bwd(dproj, dh, h, gain, w_in_t, *, tm, name):
    T, D = h.shape
    tm = min(tm, T)
    width = w_in_t.shape[0]

    def body(dp_ref, dh_ref, h_ref, gain_ref, wt_ref, dx_ref, df_ref, dgain_ref):
        @pl.when(pl.program_id(0) == 0)
        def _():
            dgain_ref[...] = jnp.zeros_like(dgain_ref)

        dx, dgain = _rms_bwd(h_ref[...], gain_ref[...], _mm(dp_ref[...], wt_ref[...]))
        dh_in = dh_ref[...] + dx
        dx_ref[...] = dh_in
        df_ref[...] = (0.5 * dh_in).astype(BF16)
        dgain_ref[...] += dgain

    row = lambda i: (i, 0)
    fixed = lambda i: (0, 0)
    return pl.pallas_call(
        body, name=name, grid=(T // tm,),
        in_specs=[
            pl.BlockSpec((tm, width), row),
            pl.BlockSpec((tm, D), row),
            pl.BlockSpec((tm, D), row),
            pl.BlockSpec((1, D), fixed),
            pl.BlockSpec((width, D), fixed),
        ],
        out_specs=[pl.BlockSpec((tm, D), row), pl.BlockSpec((tm, D), row), pl.BlockSpec((1, D), fixed)],
        out_shape=[jax.ShapeDtypeStruct((T, D), F32), jax.ShapeDtypeStruct((T, D), BF16),
                   jax.ShapeDtypeStruct((1, D), F32)],
        compiler_params=_params(("arbitrary",)),
    )(dproj, dh, h, gain, w_in_t)


def _window_sum(x, row, doublings, *, backward):
    T = x.shape[0]
    s = x
    for k in range(doublings):
        sh = 1 << k
        if backward:
            s = s + jnp.where(row < T - sh, pltpu.roll(s, T - sh, 0), 0.0)
        else:
            s = s + jnp.where(row >= sh, pltpu.roll(s, sh, 0), 0.0)
    return s


def _pool_fwd(proj, w_group, scale, *, name):
    T = proj.shape[0]

    def body(xp_ref, w_ref, scale_ref, p_ref):
        row = lax.broadcasted_iota(jnp.int32, (T, POOL_GROUP), 0)
        for gi, window in enumerate(POOL_WINDOWS):
            cols = slice(gi * POOL_GROUP, (gi + 1) * POOL_GROUP)
            x = xp_ref[:, cols]
            inv_count = 1.0 / jnp.minimum(row + 1, window).astype(F32)
            yc = _window_sum(x, row, gi + 1, backward=False) * inv_count - x
            pre = _mm(yc.astype(BF16), w_ref[gi].astype(BF16))
            p_ref[:, cols] = pre * scale_ref[:, cols]

    return pl.pallas_call(
        body, name=name, grid=(1,),
        in_specs=[
            pl.BlockSpec((T, POOL_WIDTH), lambda i: (0, 0)),
            pl.BlockSpec(w_group.shape, lambda i: (0, 0, 0)),
            pl.BlockSpec((1, POOL_WIDTH), lambda i: (0, 0)),
        ],
        out_specs=pl.BlockSpec((T, POOL_WIDTH), lambda i: (0, 0)),
        out_shape=jax.ShapeDtypeStruct((T, POOL_WIDTH), F32),
        compiler_params=_params(("arbitrary",)),
    )(proj, w_group, scale)


def _pool_bwd(dp, proj, w_group, scale, *, name):
    T = proj.shape[0]

    def body(dp_ref, xp_ref, w_ref, scale_ref, dxp_ref, dw_ref, dscale_ref):
        row = lax.broadcasted_iota(jnp.int32, (T, POOL_GROUP), 0)
        for gi, window in enumerate(POOL_WINDOWS):
            cols = slice(gi * POOL_GROUP, (gi + 1) * POOL_GROUP)
            x = xp_ref[:, cols]
            inv_count = 1.0 / jnp.minimum(row + 1, window).astype(F32)
            yc = (_window_sum(x, row, gi + 1, backward=False) * inv_count - x).astype(BF16)
            w = w_ref[gi].astype(BF16)
            pre = _mm(yc, w)
            dpg = dp_ref[:, cols]
            dscale_ref[:, cols] = jnp.sum(dpg * pre, axis=0, keepdims=True)
            dpre = (dpg * scale_ref[:, cols]).astype(BF16)
            dw_ref[gi] = _mm_tn(yc, dpre)
            dyc = _mm_nt(dpre, w)
            dxp_ref[:, cols] = (_window_sum(dyc * inv_count, row, gi + 1, backward=True) - dyc).astype(BF16)

    return pl.pallas_call(
        body, name=name, grid=(1,),
        in_specs=[
            pl.BlockSpec((T, POOL_WIDTH), lambda i: (0, 0)),
            pl.BlockSpec((T, POOL_WIDTH), lambda i: (0, 0)),
            pl.BlockSpec(w_group.shape, lambda i: (0, 0, 0)),
            pl.BlockSpec((1, POOL_WIDTH), lambda i: (0, 0)),
        ],
        out_specs=[
            pl.BlockSpec((T, POOL_WIDTH), lambda i: (0, 0)),
            pl.BlockSpec(w_group.shape, lambda i: (0, 0, 0)),
            pl.BlockSpec((1, POOL_WIDTH), lambda i: (0, 0)),
        ],
        out_shape=[jax.ShapeDtypeStruct((T, POOL_WIDTH), BF16), jax.ShapeDtypeStruct(w_group.shape, F32),
                   jax.ShapeDtypeStruct((1, POOL_WIDTH), F32)],
        compiler_params=_params(("arbitrary",)),
    )(dp, proj, w_group, scale)


ATTN_STRIP = 32


def _log_sigmoids(z):
    lb = jnp.minimum(z, 0.0) - jnp.log(1.0 + jnp.exp(-jnp.abs(z)))
    return lb, lb - z


def _transposed_blocks(x_ref, blocks_scr, tq):
    for b in range(blocks_scr.shape[0]):
        blocks_scr[b] = x_ref[b * tq:(b + 1) * tq, :].T.astype(BF16)


def _split_bf16(x):
    hi = x.astype(BF16)
    return hi, (x - hi.astype(F32)).astype(BF16)


def _strips(n):
    return [slice(i, i + ATTN_STRIP) for i in range(0, n, ATTN_STRIP)]


def _rows(parts):
    return jnp.concatenate(parts, axis=0)


def _attn_specs(T, tq):
    q_col = POOL_WIDTH // HEAD_PAIR
    k_col = q_col + SB_WIDTH // HEAD_PAIR
    v_col = k_col + SB_WIDTH // HEAD_PAIR
    return [
        pl.BlockSpec((tq, HEAD_PAIR), lambda p, i: (i, q_col + p)),
        pl.BlockSpec((T, HEAD_PAIR), lambda p, i: (0, k_col + p)),
        pl.BlockSpec((T, HEAD_PAIR), lambda p, i: (0, v_col + p)),
    ]


def _attn_fwd(proj, *, name):
    T = proj.shape[0]
    tk = min(ATTN_K_BLOCK, T)
    tq = min(ATTN_Q_BLOCK_FWD, T)
    diagonal_blocks = tq // tk

    def body(q_ref, k_ref, v_ref, o_ref, lt_ref, kt_scr, vb_scr):
        qi = pl.program_id(1)

        @pl.when(qi == 0)
        def _():
            _transposed_blocks(k_ref, kt_scr, tk)
            vb_scr[...] = v_ref[...].astype(BF16)

        head0 = lax.broadcasted_iota(jnp.int32, (tq, HEAD_PAIR), 1) < HEAD_DIM
        q = q_ref[...] * ATTN_SCALE
        qs = (jnp.where(head0, q, 0.0).astype(BF16), jnp.where(head0, 0.0, q).astype(BF16))
        r = lax.broadcasted_iota(jnp.int32, (tq, tk), 0)
        c = lax.broadcasted_iota(jnp.int32, (tq, tk), 1)
        later = (r[:tk] > c[:tk]).astype(BF16)
        later2 = _rows([later, later])
        causal = lambda d: (lambda rows: c[rows] + d * tk < r[rows])
        strips = _strips(tq)

        def log_terms(z, valid):
            lbs, his, los, sums = [], [], [], []
            for rows in strips:
                lb, lm = _log_sigmoids(z[rows])
                if valid is not None:
                    lm = jnp.where(valid(rows), lm, 0.0)
                hi, lo = _split_bf16(lm)
                lbs.append(lb)
                his.append(hi)
                los.append(lo)
                sums.append(jnp.sum(lm, axis=1, keepdims=True))
            return lbs, jnp.concatenate([_rows(his), _rows(los)], axis=1), _rows(sums)

        def weights(lbs, run, after, valid):
            parts = []
            for rows, lb in zip(strips, lbs):
                a = jnp.exp(lb + run[rows] + after[rows])
                if valid is not None:
                    a = jnp.where(valid(rows), a, 0.0)
                parts.append(a.astype(BF16))
            return _rows(parts)

        def block(kj, carry, valid):
            kt = kt_scr[kj]
            vb = vb_scr[pl.ds(pl.multiple_of(kj * tk, tk), tk), :]
            run0, o0, run1, o1 = carry
            z0 = _mm(qs[0], kt)
            z1 = _mm(qs[1], kt)
            lbs0, split0, sums0 = log_terms(z0, valid)
            after0 = _mm(split0, later2)
            lbs1, split1, sums1 = log_terms(z1, valid)
            after1 = _mm(split1, later2)
            o0 = o0 + _mm(weights(lbs0, run0, after0, valid), vb)
            o1 = o1 + _mm(weights(lbs1, run1, after1, valid), vb)
            return run0 + sums0, o0, run1 + sums1, o1

        zero = (jnp.zeros((tq, 1), F32), jnp.zeros((tq, HEAD_PAIR), F32))
        first = diagonal_blocks * qi
        carry = zero + zero
        for d in reversed(range(diagonal_blocks)):
            carry = block(first + d, carry, causal(d))
        carry = lax.fori_loop(0, first, lambda it, cr: block(first - 1 - it, cr, None), carry)
        o_ref[...] = jnp.where(head0, carry[1], carry[3])
        lt_ref[...] = jnp.where(head0, carry[0], carry[2])

    out_spec = pl.BlockSpec((tq, HEAD_PAIR), lambda p, i: (i, p))
    return pl.pallas_call(
        body, name=name, grid=(N_HEADS // 2, T // tq),
        in_specs=_attn_specs(T, tq), out_specs=[out_spec, out_spec],
        out_shape=[jax.ShapeDtypeStruct((T, SB_WIDTH), F32), jax.ShapeDtypeStruct((T, SB_WIDTH), F32)],
        scratch_shapes=[pltpu.VMEM((T // tk, HEAD_PAIR, tk), BF16), pltpu.VMEM((T, HEAD_PAIR), BF16)],
        compiler_params=_params(("arbitrary", "arbitrary")),
    )(proj, proj, proj)


def _attn_bwd(proj, do, ltot, after, *, name):
    T = proj.shape[0]
    tk = min(ATTN_K_BLOCK, T)
    tq = min(ATTN_Q_BLOCK_BWD, T)
    diagonal_blocks = tq // tk

    def body(q_ref, k_ref, v_ref, do_ref, lt_ref, after_ref, dq_ref, dk_ref, dv_ref,
             kb_scr, kt_scr, vt_scr, dkt_ref, dvt_ref):
        qi = pl.program_id(1)

        @pl.when(qi == 0)
        def _():
            kb_scr[...] = k_ref[...].astype(BF16)
            _transposed_blocks(k_ref, kt_scr, tk)
            _transposed_blocks(v_ref, vt_scr, tk)
            dkt_ref[...] = jnp.zeros_like(dkt_ref)
            dvt_ref[...] = jnp.zeros_like(dvt_ref)

        head0 = lax.broadcasted_iota(jnp.int32, (tq, HEAD_PAIR), 1) < HEAD_DIM
        q, do_, lt = q_ref[...] * ATTN_SCALE, do_ref[...], lt_ref[...]
        qs = (jnp.where(head0, q, 0.0).astype(BF16), jnp.where(head0, 0.0, q).astype(BF16))
        q_heads = (jnp.where(head0, q, 0.0), jnp.where(head0, 0.0, q))
        do_heads = (jnp.where(head0, do_, 0.0), jnp.where(head0, 0.0, do_))
        dos = tuple(d.astype(BF16) for d in do_heads)
        qts = tuple(x.T.astype(BF16) for x in q_heads)
        dots = tuple(d.T.astype(BF16) for d in do_heads)
        lts = (jnp.max(jnp.where(head0, lt, -jnp.inf), axis=1, keepdims=True),
               jnp.max(jnp.where(head0, -jnp.inf, lt), axis=1, keepdims=True))
        r = lax.broadcasted_iota(jnp.int32, (tq, tk), 0)
        c = lax.broadcasted_iota(jnp.int32, (tq, tk), 1)
        upto = (r[:tk] <= c[:tk]).astype(BF16)
        before = (r[:tk] < c[:tk]).astype(BF16)
        upto2, before2 = _rows([upto, upto]), _rows([before, before])
        causal = lambda d: (lambda rows: c[rows] + d * tk < r[rows])
        strips = _strips(tq)

        def log_terms(z, valid):
            lbs, his, los, sums = [], [], [], []
            for rows in strips:
                lb, lm = _log_sigmoids(z[rows])
                if valid is not None:
                    lm = jnp.where(valid(rows), lm, 0.0)
                hi, lo = _split_bf16(lm)
                lbs.append(lb)
                his.append(hi)
                los.append(lo)
                sums.append(jnp.sum(lm, axis=1, keepdims=True))
            return lbs, jnp.concatenate([_rows(his), _rows(los)], axis=1), _rows(sums)

        def weights(lbs, rest, lm_upto, da, valid):
            a_parts, es, his, los, sums = [], [], [], [], []
            for rows, lb in zip(strips, lbs):
                a = jnp.exp(lb + (rest[rows] - lm_upto[rows]))
                if valid is not None:
                    a = jnp.where(valid(rows), a, 0.0)
                e = da[rows] * a
                hi, lo = _split_bf16(e)
                a_parts.append(a.astype(BF16))
                es.append(e)
                his.append(hi)
                los.append(lo)
                sums.append(jnp.sum(e, axis=1, keepdims=True))
            return _rows(a_parts), es, jnp.concatenate([_rows(his), _rows(los)], axis=1), _rows(sums)

        def score_grads(lbs, es, run_e, e_before, valid):
            parts = []
            for rows, lb, e in zip(strips, lbs, es):
                beta = jnp.exp(lb)
                dz = e * (1.0 - beta) - (run_e[rows] + e_before[rows]) * beta
                if valid is not None:
                    dz = jnp.where(valid(rows), dz, 0.0)
                parts.append(dz.astype(BF16))
            return _rows(parts)

        def block(kj, carry, valid):
            off = pl.multiple_of(kj * tk, tk)
            kb, kt, vt = kb_scr[pl.ds(off, tk), :], kt_scr[kj], vt_scr[kj]
            run_lm0, run_e0, dq0, run_lm1, run_e1, dq1 = carry
            z0, da0 = _mm(qs[0], kt), _mm(dos[0], vt)
            z1, da1 = _mm(qs[1], kt), _mm(dos[1], vt)
            lbs0, split0, lm_sums0 = log_terms(z0, valid)
            lm_upto0 = _mm(split0, upto2)
            lbs1, split1, lm_sums1 = log_terms(z1, valid)
            lm_upto1 = _mm(split1, upto2)
            a0, es0, split0, e_sums0 = weights(lbs0, lts[0] - run_lm0, lm_upto0, da0, valid)
            e_before0 = _mm(split0, before2)
            a1, es1, split1, e_sums1 = weights(lbs1, lts[1] - run_lm1, lm_upto1, da1, valid)
            e_before1 = _mm(split1, before2)
            dz0 = score_grads(lbs0, es0, run_e0, e_before0, valid)
            dkt_blk = _mm(qts[0], dz0)
            dvt_blk = _mm(dots[0], a0)
            dq0 = dq0 + _mm(dz0, kb)
            dz1 = score_grads(lbs1, es1, run_e1, e_before1, valid)
            dkt_ref[kj] += dkt_blk + _mm(qts[1], dz1)
            dvt_ref[kj] += dvt_blk + _mm(dots[1], a1)
            dq1 = dq1 + _mm(dz1, kb)
            return run_lm0 + lm_sums0, run_e0 + e_sums0, dq0, run_lm1 + lm_sums1, run_e1 + e_sums1, dq1

        zero = (jnp.zeros((tq, 1), F32), jnp.zeros((tq, 1), F32), jnp.zeros((tq, HEAD_PAIR), F32))
        first = diagonal_blocks * qi
        carry = lax.fori_loop(0, first, lambda kj, cr: block(kj, cr, None), zero + zero)
        for d in range(diagonal_blocks):
            carry = block(first + d, carry, causal(d))
        dq_ref[...] = (jnp.where(head0, carry[2], carry[5]) * ATTN_SCALE).astype(BF16)

        @pl.when(qi == T // tq - 1)
        def _():
            for b in range(T // tk):
                dk_ref[b * tk:(b + 1) * tk, :] = dkt_ref[b].T.astype(BF16)
                dv_ref[b * tk:(b + 1) * tk, :] = dvt_ref[b].T.astype(BF16)

    blk = pl.BlockSpec((tq, HEAD_PAIR), lambda p, i: (i, p))
    seq = pl.BlockSpec((T, HEAD_PAIR), lambda p, i: (0, p))
    transposed = pltpu.VMEM((T // tk, HEAD_PAIR, tk), F32)
    return pl.pallas_call(
        body, name=name, grid=(N_HEADS // 2, T // tq),
        in_specs=_attn_specs(T, tq) + [blk, blk, AFTER], out_specs=[blk, seq, seq],
        out_shape=[jax.ShapeDtypeStruct((T, SB_WIDTH), BF16)] * 3,
        scratch_shapes=[pltpu.VMEM((T, HEAD_PAIR), BF16), pltpu.VMEM((T // tk, HEAD_PAIR, tk), BF16),
                        pltpu.VMEM((T // tk, HEAD_PAIR, tk), BF16), transposed, transposed],
        compiler_params=_params(("arbitrary", "arbitrary")),
    )(proj, proj, proj, do, ltot, _in_hbm(after))


def _mix_specs(T, D, tm, wbp, w_out):
    gate_col = (POOL_WIDTH + 3 * SB_WIDTH) // D
    row = lambda i: (i, 0)
    return [
        pl.BlockSpec((tm, D), row),
        pl.BlockSpec((tm, POOL_WIDTH), row),
        pl.BlockSpec((tm, SB_WIDTH), row),
        pl.BlockSpec((tm, D), lambda i: (i, gate_col)),
        pl.BlockSpec((tm, D), lambda i: (i, gate_col + 1)),
        pl.BlockSpec(wbp.shape, lambda i: (0, 0)),
        pl.BlockSpec(wbp.shape, lambda i: (0, 0)),
        pl.BlockSpec(w_out.shape, lambda i: (0, 0)),
    ]


def _mix_fwd(h, p, o, proj, wbp, wba, w_out, *, tm, name):
    T, D = h.shape
    tm = min(tm, T)

    def body(h_ref, p_ref, o_ref, glp_ref, gls_ref, wbp_ref, wba_ref, wout_ref, hout_ref, m_ref):
        yp = _mm_nt(p_ref[...].astype(BF16), wbp_ref[...])
        ys = _mm_nt(o_ref[...].astype(BF16), wba_ref[...])
        m = (jax.nn.sigmoid(glp_ref[...]) * yp + jax.nn.sigmoid(gls_ref[...]) * ys).astype(BF16)
        m_ref[...] = m
        hout_ref[...] = h_ref[...] + _mm(m, wout_ref[...])

    row = lambda i: (i, 0)
    return pl.pallas_call(
        body, name=name, grid=(T // tm,),
        in_specs=_mix_specs(T, D, tm, wbp, w_out),
        out_specs=[pl.BlockSpec((tm, D), row), pl.BlockSpec((tm, D), row)],
        out_shape=[jax.ShapeDtypeStruct((T, D), F32), jax.ShapeDtypeStruct((T, D), BF16)],
        compiler_params=_params(("arbitrary",)),
    )(h, p, o, proj, proj, wbp, wba, w_out)


def _mix_bwd(dh, p, o, proj, wbp, wba, w_out, after, *, tm, name):
    T, D = dh.shape
    tm = min(tm, T)

    def body(dh_ref, p_ref, o_ref, glp_ref, gls_ref, wbp_ref, wba_ref, wout_ref, after_ref,
             dyp_ref, dys_ref, dp_ref, do_ref, dgl_ref):
        dm = _mm_nt(dh_ref[...].astype(BF16), wout_ref[...])
        yp = _mm_nt(p_ref[...].astype(BF16), wbp_ref[...])
        ys = _mm_nt(o_ref[...].astype(BF16), wba_ref[...])
        gp = jax.nn.sigmoid(glp_ref[...])
        gs = jax.nn.sigmoid(gls_ref[...])
        dyp = (dm * gp).astype(BF16)
        dys = (dm * gs).astype(BF16)
        dyp_ref[...] = dyp
        dys_ref[...] = dys
        dgl_ref[:, :D] = (dm * yp * gp * (1.0 - gp)).astype(BF16)
        dgl_ref[:, D:] = (dm * ys * gs * (1.0 - gs)).astype(BF16)
        dp_ref[...] = _mm(dyp, wbp_ref[...])
        do_ref[...] = _mm(dys, wba_ref[...])

    row = lambda i: (i, 0)
    return pl.pallas_call(
        body, name=name, grid=(T // tm,),
        in_specs=_mix_specs(T, D, tm, wbp, w_out) + [AFTER],
        out_specs=[pl.BlockSpec((tm, D), row), pl.BlockSpec((tm, D), row), pl.BlockSpec((tm, POOL_WIDTH), row),
                   pl.BlockSpec((tm, SB_WIDTH), row), pl.BlockSpec((tm, 2 * D), row)],
        out_shape=[jax.ShapeDtypeStruct((T, D), BF16), jax.ShapeDtypeStruct((T, D), BF16),
                   jax.ShapeDtypeStruct((T, POOL_WIDTH), F32), jax.ShapeDtypeStruct((T, SB_WIDTH), F32),
                   jax.ShapeDtypeStruct((T, 2 * D), BF16)],
        compiler_params=_params(("arbitrary",)),
    )(dh, p, o, proj, proj, wbp, wba, w_out, _in_hbm(after))


def _adamw(w, g, m, v, *, name):
    R, C = w.shape
    tr = _row_tile(R, C)

    def body(w_ref, g_ref, m_ref, v_ref, d_ref, nm_ref, nv_ref):
        g_ = g_ref[...]
        m_ = ADAM_B1 * m_ref[...] + (1.0 - ADAM_B1) * g_
        v_ = ADAM_B2 * v_ref[...] + (1.0 - ADAM_B2) * (g_ * g_)
        m_hat = m_ / (1.0 - ADAM_B1 ** ADAM_STEP)
        v_hat = v_ / (1.0 - ADAM_B2 ** ADAM_STEP)
        d_ref[...] = -ADAM_LR * (m_hat / (jnp.sqrt(v_hat) + ADAM_EPS) + ADAM_WD * w_ref[...])
        nm_ref[...] = m_
        nv_ref[...] = v_

    spec = pl.BlockSpec((tr, C), lambda i: (i, 0))
    return pl.pallas_call(
        body, name=name, grid=(R // tr,), in_specs=[spec] * 4, out_specs=[spec] * 3,
        out_shape=[jax.ShapeDtypeStruct((R, C), F32)] * 3,
        compiler_params=_params(("arbitrary",)),
    )(w, g, m, v)


def _position():
    return lax.axis_index("x"), lax.axis_index("y"), lax.axis_index("c")


def _all_gather(shards, *, name, collective_id):
    n = len(shards)
    n_copies = 9

    def body(*refs):
        ins, outs = refs[:n], refs[n:2 * n]
        send_sems, recv_sems, local_sems = refs[2 * n:]
        x, y, c = _position()
        me, sibling = (x, y, c), (x, y, 1 - c)
        x_nbr, y_nbr, diagonal = (1 - x, y, c), (x, 1 - y, c), (1 - x, 1 - y, c)
        other = lambda pos: (pos[0], pos[1], 1 - c)

        barrier = pltpu.get_barrier_semaphore()
        for peer in (sibling, x_nbr, y_nbr):
            pl.semaphore_signal(barrier, inc=1, device_id=peer, device_id_type=MESH)
        pl.semaphore_wait(barrier, 3)

        def block(a, pos, half=None):
            ref = outs[a].at[4 * pos[0] + 2 * pos[1] + pos[2]]
            rows = ref.shape[0] // 2
            return ref if half is None else ref.at[pl.ds(half * rows, rows)]

        def copy(a, k, pos, to, half=None, src=None):
            return pltpu.make_async_remote_copy(
                src_ref=block(a, pos, half) if src is None else src, dst_ref=block(a, pos, half),
                send_sem=send_sems.at[n_copies * a + k], recv_sem=recv_sems.at[n_copies * a + k],
                device_id=to, device_id_type=MESH)

        started = []
        for a in range(n):
            mine = pltpu.make_async_copy(ins[a], block(a, me), local_sems.at[a])
            mine.start()
            started.append(mine)
        sends = []
        for a in range(n):
            sends += [copy(a, 1, me, x_nbr, src=ins[a]), copy(a, 2, me, y_nbr, src=ins[a]),
                      copy(a, 0, me, sibling, src=ins[a])]
        for cp in sends:
            cp.start()

        def pass_on(copies):
            for cp in copies:
                cp.start()
                sends.append(cp)

        for a in range(n):
            copy(a, 1, x_nbr, me).wait_recv()
            pass_on([copy(a, 5, x_nbr, y_nbr, half=0), copy(a, 3, x_nbr, sibling)])
            copy(a, 2, y_nbr, me).wait_recv()
            pass_on([copy(a, 6, y_nbr, x_nbr, half=1), copy(a, 4, y_nbr, sibling)])
        for a in range(n):
            copy(a, 5, diagonal, me, half=0).wait_recv()
            pass_on([copy(a, 7, diagonal, sibling, half=0)])
            copy(a, 6, diagonal, me, half=1).wait_recv()
            pass_on([copy(a, 8, diagonal, sibling, half=1)])
        for a in range(n):
            copy(a, 0, sibling, me).wait_recv()
            copy(a, 3, other(x_nbr), me).wait_recv()
            copy(a, 4, other(y_nbr), me).wait_recv()
            copy(a, 7, other(diagonal), me, half=0).wait_recv()
            copy(a, 8, other(diagonal), me, half=1).wait_recv()
        for cp in sends:
            cp.wait_send()
        for cp in started:
            cp.wait()

    return pl.kernel(
        body, name=name,
        out_type=[jax.ShapeDtypeStruct((N_DEV,) + s.shape, s.dtype) for s in shards],
        mesh=plsc.ScalarSubcoreMesh(axis_name="sequencer", num_cores=1),
        scratch_types=[pltpu.SemaphoreType.DMA((n_copies * n,)), pltpu.SemaphoreType.DMA((n_copies * n,)),
                       pltpu.SemaphoreType.DMA((n,))],
        compiler_params=pltpu.CompilerParams(collective_id=collective_id),
    )(*shards)


def _chip_sums(group, *, name):
    n = len(group)
    shapes = [g.shape[1:] for g in group]

    def body(*refs):
        g_refs, partials, out_refs = refs[:n], refs[n:3 * n:2], refs[n + 1:3 * n:2]
        mines, theirs = refs[3 * n:5 * n:2], refs[3 * n + 1:5 * n:2]
        send_sems, recv_sems, local_sems = refs[5 * n:]
        x, y, c = _position()
        my_chip = 2 * x + y

        def swap(a, s):
            return pltpu.make_async_remote_copy(
                src_ref=g_refs[a].at[2 * s + (1 - c)], dst_ref=theirs[a].at[s],
                send_sem=send_sems.at[4 * a + s], recv_sem=recv_sems.at[4 * a + s],
                device_id=(x, y, 1 - c), device_id_type=MESH)

        def load(a, s):
            return pltpu.make_async_copy(g_refs[a].at[2 * s + c], mines[a].at[s], local_sems.at[4 * a + s])

        for a in range(n):
            for s in range(4):
                swap(a, s).start()
                load(a, s).start()

        for a, (R, C) in enumerate(shapes):
            rc = 128 if R % 128 == 0 else R

            def chip_sum(chip, rows):
                return mines[a][chip, rows, :].astype(F32) + theirs[a][chip, rows, :].astype(F32)

            for s in range(4):
                load(a, s).wait()
                swap(a, s).wait_recv()

                @pl.when(s == my_chip)
                def _():
                    @pl.loop(0, R // rc)
                    def _(t):
                        rows = pl.ds(pl.multiple_of(t * rc, rc), rc)
                        out_refs[a][rows, :] = chip_sum(s, rows)

                @pl.when(s != my_chip)
                def _():
                    @pl.loop(0, R // rc)
                    def _(t):
                        rows = pl.ds(pl.multiple_of(t * rc, rc), rc)
                        partials[a][(s ^ my_chip) - 1, rows, :] = chip_sum(s, rows).astype(BF16)

        for a in range(n):
            for s in range(4):
                swap(a, s).wait_send()

    vmem = pl.BlockSpec(memory_space=pltpu.VMEM)
    outs = pl.pallas_call(
        body, name=name,
        in_specs=[pl.BlockSpec(memory_space=pl.ANY)] * n, out_specs=[vmem] * (2 * n),
        out_shape=[shape for R, C in shapes
                   for shape in (jax.ShapeDtypeStruct((3, R, C), BF16), jax.ShapeDtypeStruct((R, C), F32))],
        scratch_shapes=[pltpu.VMEM((4, R, C), BF16) for R, C in shapes for _ in range(2)] + [
            pltpu.SemaphoreType.DMA((4 * n,)), pltpu.SemaphoreType.DMA((4 * n,)), pltpu.SemaphoreType.DMA((4 * n,))],
        compiler_params=_params(),
    )(*group)
    return [(outs[2 * a], outs[2 * a + 1]) for a in range(n)]


def _cross_chips(partials, *, name, collective_id):
    n = len(partials)

    def body(*refs):
        ins, outs = refs[:n], refs[n:2 * n]
        send_sems, recv_sems = refs[2 * n:]
        x, y, c = _position()
        my_chip = 2 * x + y
        peers = [((my_chip ^ j) // 2, (my_chip ^ j) % 2, c) for j in (1, 2, 3)]

        barrier = pltpu.get_barrier_semaphore()
        for peer in peers:
            pl.semaphore_signal(barrier, inc=1, device_id=peer, device_id_type=MESH)
        pl.semaphore_wait(barrier, 3)

        copies = [
            pltpu.make_async_remote_copy(
                src_ref=ins[a].at[j], dst_ref=outs[a].at[j],
                send_sem=send_sems.at[3 * a + j], recv_sem=recv_sems.at[3 * a + j],
                device_id=peers[j], device_id_type=MESH)
            for a in range(n) for j in range(3)]
        for cp in copies:
            cp.start()
        for cp in copies:
            cp.wait_recv()
        for cp in copies:
            cp.wait_send()

    return pl.kernel(
        body, name=name,
        out_type=[jax.ShapeDtypeStruct(p.shape, p.dtype) for p in partials],
        mesh=plsc.ScalarSubcoreMesh(axis_name="sequencer", num_cores=1),
        scratch_types=[pltpu.SemaphoreType.DMA((3 * n,)), pltpu.SemaphoreType.DMA((3 * n,))],
        compiler_params=pltpu.CompilerParams(collective_id=collective_id),
    )(*partials)


def _cross_chips_and_gather(partial, slab, *, name, collective_id):
    def body(part_ref, slab_ref, landed_ref, slabs_ref, send_sems, recv_sems, local_sem):
        x, y, c = _position()
        me, my_chip = 4 * x + 2 * y + c, 2 * x + y
        others = [me ^ k for k in range(1, N_DEV)]
        ids = [(o // 4, (o // 2) % 2, o % 2) for o in others]

        barrier = pltpu.get_barrier_semaphore()
        for peer in ids:
            pl.semaphore_signal(barrier, inc=1, device_id=peer, device_id_type=MESH)
        pl.semaphore_wait(barrier, N_DEV - 1)

        mine = pltpu.make_async_copy(slab_ref, slabs_ref.at[me], local_sem)
        mine.start()
        sends = [
            pltpu.make_async_remote_copy(
                src_ref=part_ref.at[j], dst_ref=landed_ref.at[j], send_sem=send_sems.at[j], recv_sem=recv_sems.at[j],
                device_id=((my_chip ^ (j + 1)) // 2, (my_chip ^ (j + 1)) % 2, c), device_id_type=MESH)
            for j in range(3)]
        sends += [
            pltpu.make_async_remote_copy(
                src_ref=slab_ref, dst_ref=slabs_ref.at[me], send_sem=send_sems.at[3 + k], recv_sem=recv_sems.at[3 + k],
                device_id=ids[k], device_id_type=MESH)
            for k in range(N_DEV - 1)]
        arrivals = sends[:3] + [
            pltpu.make_async_remote_copy(
                src_ref=slab_ref, dst_ref=slabs_ref.at[others[k]], send_sem=send_sems.at[3 + k],
                recv_sem=recv_sems.at[3 + k], device_id=ids[k], device_id_type=MESH)
            for k in range(N_DEV - 1)]
        for cp in sends:
            cp.start()
        for cp in arrivals:
            cp.wait_recv()
        for cp in sends:
            cp.wait_send()
        mine.wait()

    n_sems = 3 + N_DEV - 1
    return pl.kernel(
        body, name=name,
        out_type=[jax.ShapeDtypeStruct(partial.shape, partial.dtype),
                  jax.ShapeDtypeStruct((N_DEV,) + slab.shape, slab.dtype)],
        mesh=plsc.ScalarSubcoreMesh(axis_name="sequencer", num_cores=1),
        scratch_types=[pltpu.SemaphoreType.DMA((n_sems,)), pltpu.SemaphoreType.DMA((n_sems,)), pltpu.SemaphoreType.DMA],
        compiler_params=pltpu.CompilerParams(collective_id=collective_id),
    )(partial, slab)


def _sum_devices(gathered, after, *, name):
    _, R, C = gathered.shape

    def body(in_ref, after_ref, out_ref):
        total = in_ref[0]
        for d in range(1, N_DEV):
            total = total + in_ref[d]
        out_ref[...] = total

    return pl.pallas_call(
        body, name=name, grid=(1,),
        in_specs=[pl.BlockSpec((N_DEV, R, C), lambda i: (0, 0, 0)), AFTER],
        out_specs=pl.BlockSpec((R, C), lambda i: (0, 0)),
        out_shape=jax.ShapeDtypeStruct((R, C), F32),
        compiler_params=_params(("arbitrary",)),
    )(gathered, _in_hbm(after))


def _owner_sum(own, landed, after, *, name):
    R, C = own.shape
    tr = _row_tile(R, C)

    def body(own_ref, landed_ref, after_ref, out_ref):
        total = own_ref[...]
        for j in range(3):
            total = total + landed_ref[j].astype(F32)
        out_ref[...] = total

    return pl.pallas_call(
        body, name=name, grid=(R // tr,),
        in_specs=[pl.BlockSpec((tr, C), lambda i: (i, 0)), pl.BlockSpec((3, tr, C), lambda i: (0, i, 0)), AFTER],
        out_specs=pl.BlockSpec((tr, C), lambda i: (i, 0)),
        out_shape=jax.ShapeDtypeStruct((R, C), F32),
        compiler_params=_params(("arbitrary",)),
    )(own, landed, _in_hbm(after))


def _local_step(x, target, norms, pool_w_group, pool_scale, wgu1, wd1, w_in, wbp, wba, w_out, wgu2, wd2, exchange):
    n1g, nmg, n2g, nfg = norms
    D = x.shape[1]
    gu1, hid1 = _ffn_up(x, n1g, wgu1, tm=1024, name="ffn1_up")
    h1 = _ffn_down(x, hid1, wd1, tm=512, name="ffn1_down")
    un, proj = _inproj_fwd(h1, nmg, w_in, tm=1024, name="inproj_fwd")
    p = _pool_fwd(proj, pool_w_group, pool_scale, name="pool_fwd")
    o, ltot = _attn_fwd(proj, name="attn_fwd")
    h2, m = _mix_fwd(h1, p, o, proj, wbp, wba, w_out, tm=256, name="mix_fwd")
    gu2, hid2 = _ffn_up(h2, n2g, wgu2, tm=1024, name="ffn2_up")
    h3 = _ffn_down(h2, hid2, wd2, tm=512, name="ffn2_down")
    dh3, df2, loss, d_nf = _loss_bwd(h3, target, nfg, tm=256, name="loss_bwd")

    dh2, d_n2, n2, dgu2 = _ffn_bwd(dh3, df2, h2, n2g, gu2, wgu2, wd2, df2, tm=512, name="ffn2_bwd")
    d_wd2 = _wgrad_down(hid2, df2, tk=WGRAD_TOKENS, name="ffn2_wgrad_down")
    d_wgu2 = _wgrad_gate_up(n2, dgu2, tk=WGRAD_TOKENS, name="ffn2_wgrad_gate_up")
    (g_wd2, g_wgu2), token = exchange("ffn2", [d_wd2.reshape(N_DEV, FF_SHARD_PAD, D), d_wgu2])

    dyp, dys, dp, do, dgl = _mix_bwd(dh2, p, o, proj, wbp, wba, w_out, token, tm=256, name="mix_bwd")
    d_wout = _wgrad_full(m, dh2, tk=WGRAD_TOKENS, name="wgrad_out")
    d_wbp = _wgrad_full(dyp, p, tk=WGRAD_TOKENS, name="wgrad_branch_pool")
    d_wba = _wgrad_full(dys, o, tk=WGRAD_TOKENS, name="wgrad_branch_attn")
    by_owner = lambda g: g.reshape(N_DEV, g.shape[0] // N_DEV, g.shape[1])
    (g_wbp, g_wba, g_wout), token = exchange("mix", [by_owner(d_wbp), by_owner(d_wba), by_owner(d_wout)])
    dxp, d_wgroup, d_scale = _pool_bwd(dp, proj, pool_w_group, pool_scale, name="pool_bwd")
    dq, dk, dv = _attn_bwd(proj, do, ltot, token, name="attn_bwd")
    dproj = jnp.concatenate([dxp, dq, dk, dv, dgl], axis=1)
    dh1, df1, d_nm = _inproj_bwd(dproj, dh2, h1, nmg, w_in, tm=512, name="inproj_bwd")
    d_win = _wgrad_in(dproj, un, tk=WGRAD_TOKENS, name="wgrad_in")
    d_wd1 = _wgrad_down(hid1, df1, tk=WGRAD_TOKENS, name="ffn1_wgrad_down")
    (g_win, g_wd1), token = exchange("w_in_ffn1_down", [d_win, d_wd1.reshape(N_DEV, FF_SHARD_PAD, D)])

    dx, d_n1, n1, dgu1 = _ffn_bwd(dh1, df1, x, n1g, gu1, wgu1, wd1, token, tm=512, name="ffn1_bwd")
    d_wgu1_a = _wgrad_gate_up(n1, dgu1, tk=WGRAD_TOKENS, name="ffn1_wgrad_gate_up_a", part=0, parts=2)
    (g_wgu1_a,), token = exchange("ffn1_gate_up_a", [d_wgu1_a])
    d_wgu1_b = _wgrad_gate_up(n1, dgu1, tk=WGRAD_TOKENS, name="ffn1_wgrad_gate_up_b", part=1, parts=2)
    (g_wgu1_b, replicated), token = exchange("last", [d_wgu1_b, d_n1, d_nm, d_n2, d_nf, d_scale, d_wgroup, loss])
    g_wgu1 = (g_wgu1_a, g_wgu1_b)

    sharded = (g_wgu1, g_wd1, g_win, g_wbp, g_wba, g_wout, g_wgu2, g_wd2)
    return dx, sharded, replicated, token


def _hidden_major(w):
    return jnp.swapaxes(w[0], 0, 1)


def _pad_gate_up(wt):
    d = wt.shape[1]
    wt = wt.astype(BF16).reshape(2, FF_SHARD, d)
    return jnp.pad(wt, ((0, 0), (0, FF_SHARD_PAD - FF_SHARD), (0, 0))).reshape(2 * FF_SHARD_PAD, d)


def _unpad_gate_up(gt):
    d = gt.shape[1]
    return gt.reshape(2, FF_SHARD_PAD, d)[:, :FF_SHARD].reshape(2 * FF_SHARD, d)


def _pad_down(w):
    return jnp.pad(w.astype(BF16), ((0, FF_SHARD_PAD - FF_SHARD), (0, 0)))


def kernel(x, ffn1_norm, ffn1_w_gate_up, ffn1_w_down, mix_norm, w_in, pool_w_group, pool_scale, w_branch_pool, w_branch_attn, w_out, ffn2_norm, ffn2_w_gate_up, ffn2_w_down, final_norm, loss_target, m_ffn1_norm, m_ffn1_w_gate_up, m_ffn1_w_down, m_mix_norm, m_w_in, m_pool_w_group, m_pool_scale, m_w_branch_pool, m_w_branch_attn, m_w_out, m_ffn2_norm, m_ffn2_w_gate_up, m_ffn2_w_down, m_final_norm, v_ffn1_norm, v_ffn1_w_gate_up, v_ffn1_w_down, v_mix_norm, v_w_in, v_pool_w_group, v_pool_scale, v_w_branch_pool, v_w_branch_attn, v_w_out, v_ffn2_norm, v_ffn2_w_gate_up, v_ffn2_w_down, v_final_norm):
    D = x.shape[-1]
    weights = dict(ffn1_norm=ffn1_norm, ffn1_w_gate_up=ffn1_w_gate_up, ffn1_w_down=ffn1_w_down, mix_norm=mix_norm,
                   w_in=w_in, pool_w_group=pool_w_group, pool_scale=pool_scale, w_branch_pool=w_branch_pool,
                   w_branch_attn=w_branch_attn, w_out=w_out, ffn2_norm=ffn2_norm, ffn2_w_gate_up=ffn2_w_gate_up,
                   ffn2_w_down=ffn2_w_down, final_norm=final_norm)
    first = dict(ffn1_norm=m_ffn1_norm, ffn1_w_gate_up=m_ffn1_w_gate_up, ffn1_w_down=m_ffn1_w_down,
                 mix_norm=m_mix_norm, w_in=m_w_in, pool_w_group=m_pool_w_group, pool_scale=m_pool_scale,
                 w_branch_pool=m_w_branch_pool, w_branch_attn=m_w_branch_attn, w_out=m_w_out,
                 ffn2_norm=m_ffn2_norm, ffn2_w_gate_up=m_ffn2_w_gate_up, ffn2_w_down=m_ffn2_w_down,
                 final_norm=m_final_norm)
    second = dict(ffn1_norm=v_ffn1_norm, ffn1_w_gate_up=v_ffn1_w_gate_up, ffn1_w_down=v_ffn1_w_down,
                  mix_norm=v_mix_norm, w_in=v_w_in, pool_w_group=v_pool_w_group, pool_scale=v_pool_scale,
                  w_branch_pool=v_w_branch_pool, w_branch_attn=v_w_branch_attn, w_out=v_w_out,
                  ffn2_norm=v_ffn2_norm, ffn2_w_gate_up=v_ffn2_w_gate_up, ffn2_w_down=v_ffn2_w_down,
                  final_norm=v_final_norm)
    order = list(weights)

    wgu1, = _all_gather([_pad_gate_up(_hidden_major(ffn1_w_gate_up))], name="all_gather_ffn1_gate_up", collective_id=0)
    wd1, = _all_gather([_pad_down(ffn1_w_down[0])], name="all_gather_ffn1_down", collective_id=10)
    transposed = lambda w: jnp.swapaxes(w[0], 0, 1).astype(BF16)
    win_g, = _all_gather([transposed(w_in)], name="all_gather_w_in", collective_id=1)
    wbp_g, wba_g = _all_gather([transposed(w_branch_pool), transposed(w_branch_attn)],
                               name="all_gather_branches", collective_id=2)
    wout_g, = _all_gather([w_out[0].astype(BF16)], name="all_gather_w_out", collective_id=11)
    wgu2, wd2 = _all_gather([_pad_gate_up(_hidden_major(ffn2_w_gate_up)), _pad_down(ffn2_w_down[0])],
                            name="all_gather_ffn2", collective_id=3)
    whole = lambda g: g.reshape(g.shape[0] * g.shape[1], g.shape[2])
    wd1, wd2, win_g, wbp_g, wba_g, wout_g = (whole(g) for g in (wd1, wd2, win_g, wbp_g, wba_g, wout_g))

    cross_ids = {"ffn2": 4, "mix": 5, "w_in_ffn1_down": 6, "ffn1_gate_up_a": 7, "last": 8}
    small = ["ffn1_norm", "mix_norm", "ffn2_norm", "final_norm", "pool_scale", "pool_w_group"]

    def tile_rows(a):
        a = a.reshape(-1, 128)
        return jnp.pad(a, ((0, -a.shape[0] % 8), (0, 0)))

    def exchange(tag, group):
        if tag == "last":
            slab = jnp.concatenate([tile_rows(g) for g in group[1:-1]] + [jnp.broadcast_to(group[-1], (8, 128))], axis=0)
            (partial, own), = _chip_sums([group[0]], name="chip_sums_last")
            landed, slabs = _cross_chips_and_gather(partial, slab, name="cross_chips_last", collective_id=cross_ids[tag])
            return [(own, landed), slabs], own
        sums = _chip_sums(group, name="chip_sums_" + tag)
        landed = _cross_chips([s[0] for s in sums], name="cross_chips_" + tag, collective_id=cross_ids[tag])
        return [(s[1], l) for s, l in zip(sums, landed)], sums[-1][1]

    norms = (ffn1_norm, mix_norm, ffn2_norm, final_norm.reshape(1, D))
    dx, sharded, slabs, last = _local_step(
        x[0], loss_target[0], norms, pool_w_group[0], pool_scale, wgu1, wd1, win_g, wbp_g, wba_g, wout_g, wgu2, wd2,
        exchange)
    names = ["ffn1_w_gate_up", "ffn1_w_down", "w_in", "w_branch_pool", "w_branch_attn", "w_out",
             "ffn2_w_gate_up", "ffn2_w_down"]
    handles = dict(zip(names, sharded))
    grads, delta, new_m, new_v = {}, {}, {}, {}
    after = last
    for k in ("ffn2_w_down", "ffn2_w_gate_up", "w_branch_pool", "w_branch_attn", "w_out", "w_in", "ffn1_w_down",
              "ffn1_w_gate_up"):
        hidden_major = k.endswith("w_gate_up")
        if isinstance(handles[k][0], tuple):
            first_half = _owner_sum(*handles[k][0], after, name="owner_sum_" + k + "_a")
            second_half = _owner_sum(*handles[k][1], first_half, name="owner_sum_" + k + "_b")
            g = jnp.concatenate([first_half[:FF_SHARD], second_half[:FF_SHARD]], axis=0)
        else:
            g = _owner_sum(*handles[k], after, name="owner_sum_" + k)
            if hidden_major:
                g = _unpad_gate_up(g)
            elif k in ("w_in", "w_branch_pool", "w_branch_attn"):
                g = jnp.swapaxes(g, 0, 1)
            else:
                g = g[:weights[k].shape[1]]
        view = _hidden_major if hidden_major else (lambda a: a[0])
        back = (lambda a: jnp.swapaxes(a, 0, 1)[None]) if hidden_major else (lambda a: a[None])
        out = _adamw(view(weights[k]), g, view(first[k]), view(second[k]), name="adamw_" + k)
        after = out[0]
        grads[k] = back(g)
        delta[k], new_m[k], new_v[k] = (back(a) for a in out)

    rows = [weights[k].size // 128 for k in small]
    padded_rows = [-(-r // 8) * 8 for r in rows]
    starts = [sum(padded_rows[:i]) for i in range(len(rows) + 1)]
    total = _sum_devices(slabs, after, name="sum_replicated")
    loss_out = total[starts[-1], 0]
    small_w = jnp.concatenate([tile_rows(weights[k]) for k in small], axis=0)
    small_m = jnp.concatenate([tile_rows(first[k]) for k in small], axis=0)
    small_v = jnp.concatenate([tile_rows(second[k]) for k in small], axis=0)
    small_out = _adamw(small_w, total[:starts[-1]], small_m, small_v, name="adamw_replicated")
    for name_, start, n_rows in zip(small, starts, rows):
        shape = weights[name_].shape
        grads[name_] = total[start:start + n_rows].reshape(shape)
        delta[name_], new_m[name_], new_v[name_] = (a[start:start + n_rows].reshape(shape) for a in small_out)

    return (loss_out, dx[None], *[grads[k] for k in order], *[delta[k] for k in order],
            *[new_m[k] for k in order], *[new_v[k] for k in order])
```

```python
import functools

import jax
import jax.numpy as jnp
from jax import lax
from jax.experimental import pallas as pl
from jax.experimental.pallas import tpu as pltpu
from jax.experimental.pallas import tpu_sc as plsc

F32 = jnp.float32
BF16 = jnp.bfloat16
MESH = pl.DeviceIdType.MESH

RMS_EPS = 1e-6
N_DEV = 8
N_HEADS = 8
HEAD_DIM = 64
HEAD_PAIR = 2 * HEAD_DIM
POOL_WINDOWS = (2, 4, 8, 16)
POOL_GROUP = 128
POOL_WIDTH = 512
SB_WIDTH = 512
FF_SHARD = 352
FF_SHARD_PAD = 384
ATTN_K_BLOCK = 256
ATTN_Q_BLOCK_FWD = 512
ATTN_Q_BLOCK_BWD = 256
ATTN_SCALE = 0.125

ADAM_LR = 0.001
ADAM_B1 = 0.9
ADAM_B2 = 0.999
ADAM_EPS = 1e-08
ADAM_WD = 0.01
ADAM_STEP = 10

VMEM_LIMIT = 48 << 20
WGRAD_TOKENS = 2048


def _params(dims=None):
    return pltpu.CompilerParams(dimension_semantics=dims, vmem_limit_bytes=VMEM_LIMIT)


def _mm(a, b):
    return jnp.dot(a, b, preferred_element_type=F32)


def _mm_nt(a, b):
    return lax.dot_general(a, b, (((1,), (1,)), ((), ())), preferred_element_type=F32)


def _mm_tn(a, b):
    return lax.dot_general(a, b, (((0,), (0,)), ((), ())), preferred_element_type=F32)


def _row_tile(rows, cols):
    limit = max(8, (512 * 1024) // cols)
    return max(t for t in range(8, rows + 1, 8) if rows % t == 0 and (t <= limit or t == 8))


def _rstd(xf):
    return lax.rsqrt(jnp.mean(xf * xf, axis=-1, keepdims=True) + RMS_EPS)


def _rms_bwd(xf, gain, dn):
    r = _rstd(xf)
    xh = xf * r
    dgain = jnp.sum(dn * xh, axis=0, keepdims=True)
    dxh = dn * gain
    dx = r * (dxh - xh * jnp.mean(dxh * xh, axis=-1, keepdims=True))
    return dx, dgain


def _ffn_up(x, gain, wgu, *, tm, name):
    T, D = x.shape
    tm = min(tm, T)
    nb, bw = wgu.shape[0] // 2, wgu.shape[1]

    def body(x_ref, gain_ref, wg_ref, wu_ref, gu_ref, hid_ref, n_scr):
        @pl.when(pl.program_id(1) == 0)
        def _():
            xf = x_ref[...]
            n_scr[...] = (xf * _rstd(xf) * gain_ref[...]).astype(BF16)

        halves = (pl.ds(0, tm // 2), pl.ds(tm // 2, tm // 2))
        wg, wu = wg_ref[...], wu_ref[...]
        gus = [(_mm_nt(n_scr[rows, :], wg), _mm_nt(n_scr[rows, :], wu)) for rows in halves]
        for rows, (g, u) in zip(halves, gus):
            gu_ref[0, rows, :] = g.astype(BF16)
            gu_ref[1, rows, :] = u.astype(BF16)
            hid_ref[rows, :] = (g * jax.nn.sigmoid(g) * u).astype(BF16)

    return pl.pallas_call(
        body, name=name, grid=(T // tm, nb),
        in_specs=[
            pl.BlockSpec((tm, D), lambda i, j: (i, 0)),
            pl.BlockSpec((1, D), lambda i, j: (0, 0)),
            pl.BlockSpec((None, bw, D), lambda i, j: (j, 0, 0)),
            pl.BlockSpec((None, bw, D), lambda i, j: (j + nb, 0, 0)),
        ],
        out_specs=[
            pl.BlockSpec((2, tm, bw), lambda i, j: (0, i, j)),
            pl.BlockSpec((tm, bw), lambda i, j: (i, j)),
        ],
        out_shape=[jax.ShapeDtypeStruct((2, T, nb * bw), BF16), jax.ShapeDtypeStruct((T, nb * bw), BF16)],
        scratch_shapes=[pltpu.VMEM((tm, D), BF16)],
        compiler_params=_params(("arbitrary", "arbitrary")),
    )(x, gain, wgu, wgu)


def _ffn_down(x, hid, wd, *, tm, name):
    T, D = x.shape
    tm = min(tm, T)
    F = hid.shape[1]

    def body(x_ref, hid_ref, wd_ref, h_ref):
        h_ref[...] = x_ref[...] + 0.5 * _mm(hid_ref[...], wd_ref[...])

    return pl.pallas_call(
        body, name=name, grid=(T // tm,),
        in_specs=[
            pl.BlockSpec((tm, D), lambda i: (i, 0)),
            pl.BlockSpec((tm, F), lambda i: (i, 0)),
            pl.BlockSpec((F, D), lambda i: (0, 0)),
        ],
        out_specs=pl.BlockSpec((tm, D), lambda i: (i, 0)),
        out_shape=jax.ShapeDtypeStruct((T, D), F32),
        compiler_params=_params(("arbitrary",)),
    )(x, hid, wd)


AFTER = pl.BlockSpec(memory_space=pltpu.HBM)


def _in_hbm(token):
    return pltpu.with_memory_space_constraint(token, pltpu.HBM)


def _ffn_bwd(dh, df, x, gain, gu, wgu, wd, after, *, tm, name):
    T, D = x.shape
    tm = min(tm, T)
    nb, bw = wgu.shape[0] // 2, wgu.shape[1]

    def body(dh_ref, df_ref, x_ref, gain_ref, gu_ref, wg_ref, wu_ref, wd_ref, after_ref,
             dx_ref, dgain_ref, n_ref, dgu_ref, dn_acc):
        i, j = pl.program_id(0), pl.program_id(1)

        @pl.when(j == 0)
        def _():
            xf = x_ref[...]
            n_ref[...] = (xf * _rstd(xf) * gain_ref[...]).astype(BF16)
            dn_acc[...] = jnp.zeros_like(dn_acc)

        @pl.when((i == 0) & (j == 0))
        def _():
            dgain_ref[...] = jnp.zeros_like(dgain_ref)

        halves = (pl.ds(0, tm // 2), pl.ds(tm // 2, tm // 2))
        wd, wg, wu = wd_ref[...], wg_ref[...], wu_ref[...]
        dhids = [_mm_nt(df_ref[rows, :], wd) for rows in halves]
        for rows, dhid in zip(halves, dhids):
            g = gu_ref[0, rows, :].astype(F32)
            u = gu_ref[1, rows, :].astype(F32)
            s = jax.nn.sigmoid(g)
            silu = g * s
            dg = (dhid * u * (s * (1.0 + g * (1.0 - s)))).astype(BF16)
            du = (dhid * silu).astype(BF16)
            dgu_ref[0, rows, :] = dg
            dgu_ref[1, rows, :] = du
            dn_acc[rows, :] += _mm(dg, wg) + _mm(du, wu)

        @pl.when(j == nb - 1)
        def _():
            dx, dgain = _rms_bwd(x_ref[...], gain_ref[...], dn_acc[...])
            dx_ref[...] = dh_ref[...] + dx
            dgain_ref[...] += dgain

    row = lambda i, j: (i, 0)
    return pl.pallas_call(
        body, name=name, grid=(T // tm, nb),
        in_specs=[
            pl.BlockSpec((tm, D), row),
            pl.BlockSpec((tm, D), row),
            pl.BlockSpec((tm, D), row),
            pl.BlockSpec((1, D), lambda i, j: (0, 0)),
            pl.BlockSpec((2, tm, bw), lambda i, j: (0, i, j)),
            pl.BlockSpec((None, bw, D), lambda i, j: (j, 0, 0)),
            pl.BlockSpec((None, bw, D), lambda i, j: (j + nb, 0, 0)),
            pl.BlockSpec((bw, D), lambda i, j: (j, 0)),
            AFTER,
        ],
        out_specs=[
            pl.BlockSpec((tm, D), row),
            pl.BlockSpec((1, D), lambda i, j: (0, 0)),
            pl.BlockSpec((tm, D), row),
            pl.BlockSpec((2, tm, bw), lambda i, j: (0, i, j)),
        ],
        out_shape=[
            jax.ShapeDtypeStruct((T, D), F32),
            jax.ShapeDtypeStruct((1, D), F32),
            jax.ShapeDtypeStruct((T, D), BF16),
            jax.ShapeDtypeStruct((2, T, nb * bw), BF16),
        ],
        scratch_shapes=[pltpu.VMEM((tm, D), F32)],
        compiler_params=_params(("arbitrary", "arbitrary")),
    )(dh, df, x, gain, gu, wgu, wgu, wd, _in_hbm(after))


def _wgrad(a, b, *, grid, a_spec, b_spec, out_spec, out_shape, acc_shape, name):
    nk = grid[2]

    def body(a_ref, b_ref, o_ref, acc):
        k = pl.program_id(2)

        @pl.when(k == 0)
        def _():
            acc[...] = jnp.zeros_like(acc)

        acc[...] += _mm_tn(a_ref[...].astype(BF16), b_ref[...].astype(BF16))

        @pl.when(k == nk - 1)
        def _():
            o_ref[...] = acc[...].astype(o_ref.dtype)

    return pl.pallas_call(
        body, name=name, grid=grid, in_specs=[a_spec, b_spec], out_specs=out_spec,
        out_shape=jax.ShapeDtypeStruct(out_shape, BF16),
        scratch_shapes=[pltpu.VMEM(acc_shape, F32)],
        compiler_params=_params(("arbitrary", "arbitrary", "arbitrary")),
    )(a, b)


def _wgrad_gate_up(n, dgu, *, tk, name, part=0, parts=1):
    T, D = n.shape
    tk = min(tk, T)
    owner_rows = FF_SHARD_PAD * 2
    nb = dgu.shape[2] // owner_rows
    bw = owner_rows // parts
    return _wgrad(
        dgu, n, grid=(2 * nb, 1, T // tk), name=name,
        a_spec=pl.BlockSpec((None, tk, bw), lambda m, c, k: (m // nb, k, parts * (m % nb) + part)),
        b_spec=pl.BlockSpec((tk, D), lambda m, c, k: (k, 0)),
        out_spec=pl.BlockSpec((None, bw, D), lambda m, c, k: (m, 0, 0)),
        out_shape=(2 * nb, bw, D), acc_shape=(bw, D))


def _wgrad_down(hid, df, *, tk, name):
    T, D = df.shape
    tk = min(tk, T)
    bw = FF_SHARD_PAD * 2
    nb = hid.shape[1] // bw
    return _wgrad(
        hid, df, grid=(nb, 1, T // tk), name=name,
        a_spec=pl.BlockSpec((tk, bw), lambda m, c, k: (k, m)),
        b_spec=pl.BlockSpec((tk, D), lambda m, c, k: (k, 0)),
        out_spec=pl.BlockSpec((bw, D), lambda m, c, k: (m, 0)),
        out_shape=(nb * bw, D), acc_shape=(bw, D))


def _wgrad_in(dproj, un, *, tk, name):
    T, D = un.shape
    tk = min(tk, T)
    bw = dproj.shape[1] // N_DEV
    return _wgrad(
        dproj, un, grid=(N_DEV, 1, T // tk), name=name,
        a_spec=pl.BlockSpec((tk, bw), lambda m, c, k: (k, m)),
        b_spec=pl.BlockSpec((tk, D), lambda m, c, k: (k, 0)),
        out_spec=pl.BlockSpec((None, bw, D), lambda m, c, k: (m, 0, 0)),
        out_shape=(N_DEV, bw, D), acc_shape=(bw, D))


def _wgrad_full(a, b, *, tk, name):
    T, M = a.shape
    tk = min(tk, T)
    N = b.shape[1]
    return _wgrad(
        a, b, grid=(1, 1, T // tk), name=name,
        a_spec=pl.BlockSpec((tk, M), lambda m, c, k: (k, 0)),
        b_spec=pl.BlockSpec((tk, N), lambda m, c, k: (k, 0)),
        out_spec=pl.BlockSpec((M, N), lambda m, c, k: (0, 0)), out_shape=(M, N), acc_shape=(M, N))


def _loss_bwd(h, target, gain, *, tm, name):
    T, D = h.shape
    tm = min(tm, T)

    def body(h_ref, t_ref, gain_ref, dh_ref, df_ref, loss_ref, dgain_ref):
        @pl.when(pl.program_id(0) == 0)
        def _():
            loss_ref[...] = jnp.zeros_like(loss_ref)
            dgain_ref[...] = jnp.zeros_like(dgain_ref)

        xf = h_ref[...]
        gain = gain_ref[...]
        err = xf * _rstd(xf) * gain - t_ref[...]
        loss_ref[...] += 0.5 * jnp.sum(jnp.mean(err * err, axis=-1, keepdims=True), axis=0, keepdims=True)
        dx, dgain = _rms_bwd(xf, gain, err * (1.0 / D))
        dh_ref[...] = dx
        df_ref[...] = (0.5 * dx).astype(BF16)
        dgain_ref[...] += dgain

    row = lambda i: (i, 0)
    fixed = lambda i: (0, 0)
    return pl.pallas_call(
        body, name=name, grid=(T // tm,),
        in_specs=[pl.BlockSpec((tm, D), row), pl.BlockSpec((tm, D), row), pl.BlockSpec((1, D), fixed)],
        out_specs=[pl.BlockSpec((tm, D), row), pl.BlockSpec((tm, D), row), pl.BlockSpec((1, 128), fixed),
                   pl.BlockSpec((1, D), fixed)],
        out_shape=[jax.ShapeDtypeStruct((T, D), F32), jax.ShapeDtypeStruct((T, D), BF16),
                   jax.ShapeDtypeStruct((1, 128), F32), jax.ShapeDtypeStruct((1, D), F32)],
        compiler_params=_params(("arbitrary",)),
    )(h, target, gain)


def _inproj_fwd(h, gain, w_in_t, *, tm, name):
    T, D = h.shape
    tm = min(tm, T)
    bn = D
    nb = w_in_t.shape[0] // bn

    def body(h_ref, gain_ref, wt_ref, un_ref, proj_ref):
        @pl.when(pl.program_id(1) == 0)
        def _():
            xf = h_ref[...]
            un_ref[...] = (xf * _rstd(xf) * gain_ref[...]).astype(BF16)

        proj_ref[...] = _mm_nt(un_ref[...], wt_ref[...])

    return pl.pallas_call(
        body, name=name, grid=(T // tm, nb),
        in_specs=[
            pl.BlockSpec((tm, D), lambda i, j: (i, 0)),
            pl.BlockSpec((1, D), lambda i, j: (0, 0)),
            pl.BlockSpec((bn, D), lambda i, j: (j, 0)),
        ],
        out_specs=[pl.BlockSpec((tm, D), lambda i, j: (i, 0)), pl.BlockSpec((tm, bn), lambda i, j: (i, j))],
        out_shape=[jax.ShapeDtypeStruct((T, D), BF16), jax.ShapeDtypeStruct((T, nb * bn), F32)],
        compiler_params=_params(("arbitrary", "arbitrary")),
    )(h, gain, w_in_t)


def _inproj_bwd(dproj, dh, h, gain, w_in_t, *, tm, name):
    T, D = h.shape
    tm = min(tm, T)
    width = w_in_t.shape[0]

    def body(dp_ref, dh_ref, h_ref, gain_ref, wt_ref, dx_ref, df_ref, dgain_ref):
        @pl.when(pl.program_id(0) == 0)
        def _():
            dgain_ref[...] = jnp.zeros_like(dgain_ref)

        dx, dgain = _rms_bwd(h_ref[...], gain_ref[...], _mm(dp_ref[...], wt_ref[...]))
        dh_in = dh_ref[...] + dx
        dx_ref[...] = dh_in
        df_ref[...] = (0.5 * dh_in).astype(BF16)
        dgain_ref[...] += dgain

    row = lambda i: (i, 0)
    fixed = lambda i: (0, 0)
    return pl.pallas_call(
        body, name=name, grid=(T // tm,),
        in_specs=[
            pl.BlockSpec((tm, width), row),
            pl.BlockSpec((tm, D), row),
            pl.BlockSpec((tm, D), row),
            pl.BlockSpec((1, D), fixed),
            pl.BlockSpec((width, D), fixed),
        ],
        out_specs=[pl.BlockSpec((tm, D), row), pl.BlockSpec((tm, D), row), pl.BlockSpec((1, D), fixed)],
        out_shape=[jax.ShapeDtypeStruct((T, D), F32), jax.ShapeDtypeStruct((T, D), BF16),
                   jax.ShapeDtypeStruct((1, D), F32)],
        compiler_params=_params(("arbitrary",)),
    )(dproj, dh, h, gain, w_in_t)


def _window_sum(x, row, doublings, *, backward):
    T = x.shape[0]
    s = x
    for k in range(doublings):
        sh = 1 << k
        if backward:
            s = s + jnp.where(row < T - sh, pltpu.roll(s, T - sh, 0), 0.0)
        else:
            s = s + jnp.where(row >= sh, pltpu.roll(s, sh, 0), 0.0)
    return s


def _pool_fwd(proj, w_group, scale, *, name):
    T = proj.shape[0]

    def body(xp_ref, w_ref, scale_ref, p_ref):
        row = lax.broadcasted_iota(jnp.int32, (T, POOL_GROUP), 0)
        for gi, window in enumerate(POOL_WINDOWS):
            cols = slice(gi * POOL_GROUP, (gi + 1) * POOL_GROUP)
            x = xp_ref[:, cols]
            inv_count = 1.0 / jnp.minimum(row + 1, window).astype(F32)
            yc = _window_sum(x, row, gi + 1, backward=False) * inv_count - x
            pre = _mm(yc.astype(BF16), w_ref[gi].astype(BF16))
            p_ref[:, cols] = pre * scale_ref[:, cols]

    return pl.pallas_call(
        body, name=name, grid=(1,),
        in_specs=[
            pl.BlockSpec((T, POOL_WIDTH), lambda i: (0, 0)),
            pl.BlockSpec(w_group.shape, lambda i: (0, 0, 0)),
            pl.BlockSpec((1, POOL_WIDTH), lambda i: (0, 0)),
        ],
        out_specs=pl.BlockSpec((T, POOL_WIDTH), lambda i: (0, 0)),
        out_shape=jax.ShapeDtypeStruct((T, POOL_WIDTH), F32),
        compiler_params=_params(("arbitrary",)),
    )(proj, w_group, scale)


def _pool_bwd(dp, proj, w_group, scale, *, name):
    T = proj.shape[0]

    def body(dp_ref, xp_ref, w_ref, scale_ref, dxp_ref, dw_ref, dscale_ref):
        row = lax.broadcasted_iota(jnp.int32, (T, POOL_GROUP), 0)
        for gi, window in enumerate(POOL_WINDOWS):
            cols = slice(gi * POOL_GROUP, (gi + 1) * POOL_GROUP)
            x = xp_ref[:, cols]
            inv_count = 1.0 / jnp.minimum(row + 1, window).astype(F32)
            yc = (_window_sum(x, row, gi + 1, backward=False) * inv_count - x).astype(BF16)
            w = w_ref[gi].astype(BF16)
            pre = _mm(yc, w)
            dpg = dp_ref[:, cols]
            dscale_ref[:, cols] = jnp.sum(dpg * pre, axis=0, keepdims=True)
            dpre = (dpg * scale_ref[:, cols]).astype(BF16)
            dw_ref[gi] = _mm_tn(yc, dpre)
            dyc = _mm_nt(dpre, w)
            dxp_ref[:, cols] = (_window_sum(dyc * inv_count, row, gi + 1, backward=True) - dyc).astype(BF16)

    return pl.pallas_call(
        body, name=name, grid=(1,),
        in_specs=[
            pl.BlockSpec((T, POOL_WIDTH), lambda i: (0, 0)),
            pl.BlockSpec((T, POOL_WIDTH), lambda i: (0, 0)),
            pl.BlockSpec(w_group.shape, lambda i: (0, 0, 0)),
            pl.BlockSpec((1, POOL_WIDTH), lambda i: (0, 0)),
        ],
        out_specs=[
            pl.BlockSpec((T, POOL_WIDTH), lambda i: (0, 0)),
            pl.BlockSpec(w_group.shape, lambda i: (0, 0, 0)),
            pl.BlockSpec((1, POOL_WIDTH), lambda i: (0, 0)),
        ],
        out_shape=[jax.ShapeDtypeStruct((T, POOL_WIDTH), BF16), jax.ShapeDtypeStruct(w_group.shape, F32),
                   jax.ShapeDtypeStruct((1, POOL_WIDTH), F32)],
        compiler_params=_params(("arbitrary",)),
    )(dp, proj, w_group, scale)


ATTN_STRIP = 32


def _log_sigmoids(z):
    lb = jnp.minimum(z, 0.0) - jnp.log(1.0 + jnp.exp(-jnp.abs(z)))
    return lb, lb - z


def _transposed_blocks(x_ref, blocks_scr, tq):
    for b in range(blocks_scr.shape[0]):
        blocks_scr[b] = x_ref[b * tq:(b + 1) * tq, :].T.astype(BF16)


def _split_bf16(x):
    hi = x.astype(BF16)
    return hi, (x - hi.astype(F32)).astype(BF16)


def _strips(n):
    return [slice(i, i + ATTN_STRIP) for i in range(0, n, ATTN_STRIP)]


def _rows(parts):
    return jnp.concatenate(parts, axis=0)


def _attn_specs(T, tq):
    q_col = POOL_WIDTH // HEAD_PAIR
    k_col = q_col + SB_WIDTH // HEAD_PAIR
    v_col = k_col + SB_WIDTH // HEAD_PAIR
    return [
        pl.BlockSpec((tq, HEAD_PAIR), lambda p, i: (i, q_col + p)),
        pl.BlockSpec((T, HEAD_PAIR), lambda p, i: (0, k_col + p)),
        pl.BlockSpec((T, HEAD_PAIR), lambda p, i: (0, v_col + p)),
    ]


def _attn_fwd(proj, *, name):
    T = proj.shape[0]
    tk = min(ATTN_K_BLOCK, T)
    tq = min(ATTN_Q_BLOCK_FWD, T)
    diagonal_blocks = tq // tk

    def body(q_ref, k_ref, v_ref, o_ref, lt_ref, kt_scr, vb_scr):
        qi = pl.program_id(1)

        @pl.when(qi == 0)
        def _():
            _transposed_blocks(k_ref, kt_scr, tk)
            vb_scr[...] = v_ref[...].astype(BF16)

        head0 = lax.broadcasted_iota(jnp.int32, (tq, HEAD_PAIR), 1) < HEAD_DIM
        q = q_ref[...] * ATTN_SCALE
        qs = (jnp.where(head0, q, 0.0).astype(BF16), jnp.where(head0, 0.0, q).astype(BF16))
        r = lax.broadcasted_iota(jnp.int32, (tq, tk), 0)
        c = lax.broadcasted_iota(jnp.int32, (tq, tk), 1)
        later = (r[:tk] > c[:tk]).astype(BF16)
        later2 = _rows([later, later])
        causal = lambda d: (lambda rows: c[rows] + d * tk < r[rows])
        strips = _strips(tq)

        def log_terms(z, valid):
            lbs, his, los, sums = [], [], [], []
            for rows in strips:
                lb, lm = _log_sigmoids(z[rows])
                if valid is not None:
                    lm = jnp.where(valid(rows), lm, 0.0)
                hi, lo = _split_bf16(lm)
                lbs.append(lb)
                his.append(hi)
                los.append(lo)
                sums.append(jnp.sum(lm, axis=1, keepdims=True))
            return lbs, jnp.concatenate([_rows(his), _rows(los)], axis=1), _rows(sums)

        def weights(lbs, run, after, valid):
            parts = []
            for rows, lb in zip(strips, lbs):
                a = jnp.exp(lb + run[rows] + after[rows])
                if valid is not None:
                    a = jnp.where(valid(rows), a, 0.0)
                parts.append(a.astype(BF16))
            return _rows(parts)

        def block(kj, carry, valid):
            kt = kt_scr[kj]
            vb = vb_scr[pl.ds(pl.multiple_of(kj * tk, tk), tk), :]
            run0, o0, run1, o1 = carry
            z0 = _mm(qs[0], kt)
            z1 = _mm(qs[1], kt)
            lbs0, split0, sums0 = log_terms(z0, valid)
            after0 = _mm(split0, later2)
            lbs1, split1, sums1 = log_terms(z1, valid)
            after1 = _mm(split1, later2)
            o0 = o0 + _mm(weights(lbs0, run0, after0, valid), vb)
            o1 = o1 + _mm(weights(lbs1, run1, after1, valid), vb)
            return run0 + sums0, o0, run1 + sums1, o1

        zero = (jnp.zeros((tq, 1), F32), jnp.zeros((tq, HEAD_PAIR), F32))
        first = diagonal_blocks * qi
        carry = zero + zero
        for d in reversed(range(diagonal_blocks)):
            carry = block(first + d, carry, causal(d))
        carry = lax.fori_loop(0, first, lambda it, cr: block(first - 1 - it, cr, None), carry)
        o_ref[...] = jnp.where(head0, carry[1], carry[3])
        lt_ref[...] = jnp.where(head0, carry[0], carry[2])

    out_spec = pl.BlockSpec((tq, HEAD_PAIR), lambda p, i: (i, p))
    return pl.pallas_call(
        body, name=name, grid=(N_HEADS // 2, T // tq),
        in_specs=_attn_specs(T, tq), out_specs=[out_spec, out_spec],
        out_shape=[jax.ShapeDtypeStruct((T, SB_WIDTH), F32), jax.ShapeDtypeStruct((T, SB_WIDTH), F32)],
        scratch_shapes=[pltpu.VMEM((T // tk, HEAD_PAIR, tk), BF16), pltpu.VMEM((T, HEAD_PAIR), BF16)],
        compiler_params=_params(("arbitrary", "arbitrary")),
    )(proj, proj, proj)


def _attn_bwd(proj, do, ltot, after, *, name):
    T = proj.shape[0]
    tk = min(ATTN_K_BLOCK, T)
    tq = min(ATTN_Q_BLOCK_BWD, T)
    diagonal_blocks = tq // tk

    def body(q_ref, k_ref, v_ref, do_ref, lt_ref, after_ref, dq_ref, dk_ref, dv_ref,
             kb_scr, kt_scr, vt_scr, dkt_ref, dvt_ref):
        qi = pl.program_id(1)

        @pl.when(qi == 0)
        def _():
            kb_scr[...] = k_ref[...].astype(BF16)
            _transposed_blocks(k_ref, kt_scr, tk)
            _transposed_blocks(v_ref, vt_scr, tk)
            dkt_ref[...] = jnp.zeros_like(dkt_ref)
            dvt_ref[...] = jnp.zeros_like(dvt_ref)

        head0 = lax.broadcasted_iota(jnp.int32, (tq, HEAD_PAIR), 1) < HEAD_DIM
        q, do_, lt = q_ref[...] * ATTN_SCALE, do_ref[...], lt_ref[...]
        qs = (jnp.where(head0, q, 0.0).astype(BF16), jnp.where(head0, 0.0, q).astype(BF16))
        q_heads = (jnp.where(head0, q, 0.0), jnp.where(head0, 0.0, q))
        do_heads = (jnp.where(head0, do_, 0.0), jnp.where(head0, 0.0, do_))
        dos = tuple(d.astype(BF16) for d in do_heads)
        qts = tuple(x.T.astype(BF16) for x in q_heads)
        dots = tuple(d.T.astype(BF16) for d in do_heads)
        lts = (jnp.max(jnp.where(head0, lt, -jnp.inf), axis=1, keepdims=True),
               jnp.max(jnp.where(head0, -jnp.inf, lt), axis=1, keepdims=True))
        r = lax.broadcasted_iota(jnp.int32, (tq, tk), 0)
        c = lax.broadcasted_iota(jnp.int32, (tq, tk), 1)
        upto = (r[:tk] <= c[:tk]).astype(BF16)
        before = (r[:tk] < c[:tk]).astype(BF16)
        upto2, before2 = _rows([upto, upto]), _rows([before, before])
        causal = lambda d: (lambda rows: c[rows] + d * tk < r[rows])
        strips = _strips(tq)

        def log_terms(z, valid):
            lbs, his, los, sums = [], [], [], []
            for rows in strips:
                lb, lm = _log_sigmoids(z[rows])
                if valid is not None:
                    lm = jnp.where(valid(rows), lm, 0.0)
                hi, lo = _split_bf16(lm)
                lbs.append(lb)
                his.append(hi)
                los.append(lo)
                sums.append(jnp.sum(lm, axis=1, keepdims=True))
            return lbs, jnp.concatenate([_rows(his), _rows(los)], axis=1), _rows(sums)

        def weights(lbs, rest, lm_upto, da, valid):
            a_parts, es, his, los, sums = [], [], [], [], []
            for rows, lb in zip(strips, lbs):
                a = jnp.exp(lb + (rest[rows] - lm_upto[rows]))
                if valid is not None:
                    a = jnp.where(valid(rows), a, 0.0)
                e = da[rows] * a
                hi, lo = _split_bf16(e)
                a_parts.append(a.astype(BF16))
                es.append(e)
                his.append(hi)
                los.append(lo)
                sums.append(jnp.sum(e, axis=1, keepdims=True))
            return _rows(a_parts), es, jnp.concatenate([_rows(his), _rows(los)], axis=1), _rows(sums)

        def score_grads(lbs, es, run_e, e_before, valid):
            parts = []
            for rows, lb, e in zip(strips, lbs, es):
                beta = jnp.exp(lb)
                dz = e * (1.0 - beta) - (run_e[rows] + e_before[rows]) * beta
                if valid is not None:
                    dz = jnp.where(valid(rows), dz, 0.0)
                parts.append(dz.astype(BF16))
            return _rows(parts)

        def block(kj, carry, valid):
            off = pl.multiple_of(kj * tk, tk)
            kb, kt, vt = kb_scr[pl.ds(off, tk), :], kt_scr[kj], vt_scr[kj]
            run_lm0, run_e0, dq0, run_lm1, run_e1, dq1 = carry
            z0, da0 = _mm(qs[0], kt), _mm(dos[0], vt)
            z1, da1 = _mm(qs[1], kt), _mm(dos[1], vt)
            lbs0, split0, lm_sums0 = log_terms(z0, valid)
            lm_upto0 = _mm(split0, upto2)
            lbs1, split1, lm_sums1 = log_terms(z1, valid)
            lm_upto1 = _mm(split1, upto2)
            a0, es0, split0, e_sums0 = weights(lbs0, lts[0] - run_lm0, lm_upto0, da0, valid)
            e_before0 = _mm(split0, before2)
            a1, es1, split1, e_sums1 = weights(lbs1, lts[1] - run_lm1, lm_upto1, da1, valid)
            e_before1 = _mm(split1, before2)
            dz0 = score_grads(lbs0, es0, run_e0, e_before0, valid)
            dkt_blk = _mm(qts[0], dz0)
            dvt_blk = _mm(dots[0], a0)
            dq0 = dq0 + _mm(dz0, kb)
            dz1 = score_grads(lbs1, es1, run_e1, e_before1, valid)
            dkt_ref[kj] += dkt_blk + _mm(qts[1], dz1)
            dvt_ref[kj] += dvt_blk + _mm(dots[1], a1)
            dq1 = dq1 + _mm(dz1, kb)
            return run_lm0 + lm_sums0, run_e0 + e_sums0, dq0, run_lm1 + lm_sums1, run_e1 + e_sums1, dq1

        zero = (jnp.zeros((tq, 1), F32), jnp.zeros((tq, 1), F32), jnp.zeros((tq, HEAD_PAIR), F32))
        first = diagonal_blocks * qi
        carry = lax.fori_loop(0, first, lambda kj, cr: block(kj, cr, None), zero + zero)
        for d in range(diagonal_blocks):
            carry = block(first + d, carry, causal(d))
        dq_ref[...] = (jnp.where(head0, carry[2], carry[5]) * ATTN_SCALE).astype(BF16)

        @pl.when(qi == T // tq - 1)
        def _():
            for b in range(T // tk):
                dk_ref[b * tk:(b + 1) * tk, :] = dkt_ref[b].T.astype(BF16)
                dv_ref[b * tk:(b + 1) * tk, :] = dvt_ref[b].T.astype(BF16)

    blk = pl.BlockSpec((tq, HEAD_PAIR), lambda p, i: (i, p))
    seq = pl.BlockSpec((T, HEAD_PAIR), lambda p, i: (0, p))
    transposed = pltpu.VMEM((T // tk, HEAD_PAIR, tk), F32)
    return pl.pallas_call(
        body, name=name, grid=(N_HEADS // 2, T // tq),
        in_specs=_attn_specs(T, tq) + [blk, blk, AFTER], out_specs=[blk, seq, seq],
        out_shape=[jax.ShapeDtypeStruct((T, SB_WIDTH), BF16)] * 3,
        scratch_shapes=[pltpu.VMEM((T, HEAD_PAIR), BF16), pltpu.VMEM((T // tk, HEAD_PAIR, tk), BF16),
                        pltpu.VMEM((T // tk, HEAD_PAIR, tk), BF16), transposed, transposed],
        compiler_params=_params(("arbitrary", "arbitrary")),
    )(proj, proj, proj, do, ltot, _in_hbm(after))


def _mix_specs(T, D, tm, wbp, w_out):
    gate_col = (POOL_WIDTH + 3 * SB_WIDTH) // D
    row = lambda i: (i, 0)
    return [
        pl.BlockSpec((tm, D), row),
        pl.BlockSpec((tm, POOL_WIDTH), row),
        pl.BlockSpec((tm, SB_WIDTH), row),
        pl.BlockSpec((tm, D), lambda i: (i, gate_col)),
        pl.BlockSpec((tm, D), lambda i: (i, gate_col + 1)),
        pl.BlockSpec(wbp.shape, lambda i: (0, 0)),
        pl.BlockSpec(wbp.shape, lambda i: (0, 0)),
        pl.BlockSpec(w_out.shape, lambda i: (0, 0)),
    ]


def _mix_fwd(h, p, o, proj, wbp, wba, w_out, *, tm, name):
    T, D = h.shape
    tm = min(tm, T)

    def body(h_ref, p_ref, o_ref, glp_ref, gls_ref, wbp_ref, wba_ref, wout_ref, hout_ref, m_ref):
        halves = (pl.ds(0, tm // 2), pl.ds(tm // 2, tm // 2))
        wbp, wba, wout = wbp_ref[...], wba_ref[...], wout_ref[...]
        branches = [(_mm_nt(p_ref[rows, :].astype(BF16), wbp), _mm_nt(o_ref[rows, :].astype(BF16), wba))
                    for rows in halves]
        for rows, (yp, ys) in zip(halves, branches):
            m = (jax.nn.sigmoid(glp_ref[rows, :]) * yp + jax.nn.sigmoid(gls_ref[rows, :]) * ys).astype(BF16)
            m_ref[rows, :] = m
            hout_ref[rows, :] = h_ref[rows, :] + _mm(m, wout)

    row = lambda i: (i, 0)
    return pl.pallas_call(
        body, name=name, grid=(T // tm,),
        in_specs=_mix_specs(T, D, tm, wbp, w_out),
        out_specs=[pl.BlockSpec((tm, D), row), pl.BlockSpec((tm, D), row)],
        out_shape=[jax.ShapeDtypeStruct((T, D), F32), jax.ShapeDtypeStruct((T, D), BF16)],
        compiler_params=_params(("arbitrary",)),
    )(h, p, o, proj, proj, wbp, wba, w_out)


def _mix_bwd(dh, p, o, proj, wbp, wba, w_out, after, *, tm, name):
    T, D = dh.shape
    tm = min(tm, T)

    def body(dh_ref, p_ref, o_ref, glp_ref, gls_ref, wbp_ref, wba_ref, wout_ref, after_ref,
             dyp_ref, dys_ref, dp_ref, do_ref, dgl_ref):
        halves = (pl.ds(0, tm // 2), pl.ds(tm // 2, tm // 2))
        wbp, wba, wout = wbp_ref[...], wba_ref[...], wout_ref[...]
        products = [(_mm_nt(dh_ref[rows, :].astype(BF16), wout), _mm_nt(p_ref[rows, :].astype(BF16), wbp),
                     _mm_nt(o_ref[rows, :].astype(BF16), wba)) for rows in halves]
        for rows, (dm, yp, ys) in zip(halves, products):
            gp = jax.nn.sigmoid(glp_ref[rows, :])
            gs = jax.nn.sigmoid(gls_ref[rows, :])
            dyp = (dm * gp).astype(BF16)
            dys = (dm * gs).astype(BF16)
            dyp_ref[rows, :] = dyp
            dys_ref[rows, :] = dys
            dgl_ref[rows, :D] = (dm * yp * gp * (1.0 - gp)).astype(BF16)
            dgl_ref[rows, D:] = (dm * ys * gs * (1.0 - gs)).astype(BF16)
            dp_ref[rows, :] = _mm(dyp, wbp)
            do_ref[rows, :] = _mm(dys, wba)

    row = lambda i: (i, 0)
    return pl.pallas_call(
        body, name=name, grid=(T // tm,),
        in_specs=_mix_specs(T, D, tm, wbp, w_out) + [AFTER],
        out_specs=[pl.BlockSpec((tm, D), row), pl.BlockSpec((tm, D), row), pl.BlockSpec((tm, POOL_WIDTH), row),
                   pl.BlockSpec((tm, SB_WIDTH), row), pl.BlockSpec((tm, 2 * D), row)],
        out_shape=[jax.ShapeDtypeStruct((T, D), BF16), jax.ShapeDtypeStruct((T, D), BF16),
                   jax.ShapeDtypeStruct((T, POOL_WIDTH), F32), jax.ShapeDtypeStruct((T, SB_WIDTH), F32),
                   jax.ShapeDtypeStruct((T, 2 * D), BF16)],
        compiler_params=_params(("arbitrary",)),
    )(dh, p, o, proj, proj, wbp, wba, w_out, _in_hbm(after))


def _adamw(w, g, m, v, *, name):
    R, C = w.shape
    tr = _row_tile(R, C)

    def body(w_ref, g_ref, m_ref, v_ref, d_ref, nm_ref, nv_ref):
        g_ = g_ref[...]
        m_ = ADAM_B1 * m_ref[...] + (1.0 - ADAM_B1) * g_
        v_ = ADAM_B2 * v_ref[...] + (1.0 - ADAM_B2) * (g_ * g_)
        m_hat = m_ / (1.0 - ADAM_B1 ** ADAM_STEP)
        v_hat = v_ / (1.0 - ADAM_B2 ** ADAM_STEP)
        d_ref[...] = -ADAM_LR * (m_hat / (jnp.sqrt(v_hat) + ADAM_EPS) + ADAM_WD * w_ref[...])
        nm_ref[...] = m_
        nv_ref[...] = v_

    spec = pl.BlockSpec((tr, C), lambda i: (i, 0))
    return pl.pallas_call(
        body, name=name, grid=(R // tr,), in_specs=[spec] * 4, out_specs=[spec] * 3,
        out_shape=[jax.ShapeDtypeStruct((R, C), F32)] * 3,
        compiler_params=_params(("arbitrary",)),
    )(w, g, m, v)


def _position():
    return lax.axis_index("x"), lax.axis_index("y"), lax.axis_index("c")


def _all_gather(shards, *, name, collective_id):
    n = len(shards)
    n_copies = 9

    def body(*refs):
        ins, outs = refs[:n], refs[n:2 * n]
        send_sems, recv_sems, local_sems = refs[2 * n:]
        x, y, c = _position()
        me, sibling = (x, y, c), (x, y, 1 - c)
        x_nbr, y_nbr, diagonal = (1 - x, y, c), (x, 1 - y, c), (1 - x, 1 - y, c)
        other = lambda pos: (pos[0], pos[1], 1 - c)

        barrier = pltpu.get_barrier_semaphore()
        for peer in (sibling, x_nbr, y_nbr):
            pl.semaphore_signal(barrier, inc=1, device_id=peer, device_id_type=MESH)
        pl.semaphore_wait(barrier, 3)

        def block(a, pos, half=None):
            ref = outs[a].at[4 * pos[0] + 2 * pos[1] + pos[2]]
            rows = ref.shape[0] // 2
            return ref if half is None else ref.at[pl.ds(half * rows, rows)]

        def copy(a, k, pos, to, half=None, src=None):
            return pltpu.make_async_remote_copy(
                src_ref=block(a, pos, half) if src is None else src, dst_ref=block(a, pos, half),
                send_sem=send_sems.at[n_copies * a + k], recv_sem=recv_sems.at[n_copies * a + k],
                device_id=to, device_id_type=MESH)

        started = []
        for a in range(n):
            mine = pltpu.make_async_copy(ins[a], block(a, me), local_sems.at[a])
            mine.start()
            started.append(mine)
        sends = []
        for a in range(n):
            sends += [copy(a, 1, me, x_nbr, src=ins[a]), copy(a, 2, me, y_nbr, src=ins[a]),
                      copy(a, 0, me, sibling, src=ins[a])]
        for cp in sends:
            cp.start()

        def pass_on(copies):
            for cp in copies:
                cp.start()
                sends.append(cp)

        for a in range(n):
            copy(a, 1, x_nbr, me).wait_recv()
            pass_on([copy(a, 5, x_nbr, y_nbr, half=0), copy(a, 3, x_nbr, sibling)])
            copy(a, 2, y_nbr, me).wait_recv()
            pass_on([copy(a, 6, y_nbr, x_nbr, half=1), copy(a, 4, y_nbr, sibling)])
        for a in range(n):
            copy(a, 5, diagonal, me, half=0).wait_recv()
            pass_on([copy(a, 7, diagonal, sibling, half=0)])
            copy(a, 6, diagonal, me, half=1).wait_recv()
            pass_on([copy(a, 8, diagonal, sibling, half=1)])
        for a in range(n):
            copy(a, 0, sibling, me).wait_recv()
            copy(a, 3, other(x_nbr), me).wait_recv()
            copy(a, 4, other(y_nbr), me).wait_recv()
            copy(a, 7, other(diagonal), me, half=0).wait_recv()
            copy(a, 8, other(diagonal), me, half=1).wait_recv()
        for cp in sends:
            cp.wait_send()
        for cp in started:
            cp.wait()

    return pl.kernel(
        body, name=name,
        out_type=[jax.ShapeDtypeStruct((N_DEV,) + s.shape, s.dtype) for s in shards],
        mesh=plsc.ScalarSubcoreMesh(axis_name="sequencer", num_cores=1),
        scratch_types=[pltpu.SemaphoreType.DMA((n_copies * n,)), pltpu.SemaphoreType.DMA((n_copies * n,)),
                       pltpu.SemaphoreType.DMA((n,))],
        compiler_params=pltpu.CompilerParams(collective_id=collective_id),
    )(*shards)


def _chip_sums(group, *, name):
    n = len(group)
    shapes = [g.shape[1:] for g in group]

    def body(*refs):
        g_refs, partials, out_refs = refs[:n], refs[n:3 * n:2], refs[n + 1:3 * n:2]
        mines, theirs = refs[3 * n:5 * n:2], refs[3 * n + 1:5 * n:2]
        send_sems, recv_sems, local_sems = refs[5 * n:]
        x, y, c = _position()
        my_chip = 2 * x + y

        def swap(a, s):
            return pltpu.make_async_remote_copy(
                src_ref=g_refs[a].at[2 * s + (1 - c)], dst_ref=theirs[a].at[s],
                send_sem=send_sems.at[4 * a + s], recv_sem=recv_sems.at[4 * a + s],
                device_id=(x, y, 1 - c), device_id_type=MESH)

        def load(a, s):
            return pltpu.make_async_copy(g_refs[a].at[2 * s + c], mines[a].at[s], local_sems.at[4 * a + s])

        for a in range(n):
            for s in range(4):
                swap(a, s).start()
                load(a, s).start()

        for a, (R, C) in enumerate(shapes):
            rc = 128 if R % 128 == 0 else R

            def chip_sum(chip, rows):
                return mines[a][chip, rows, :].astype(F32) + theirs[a][chip, rows, :].astype(F32)

            for s in range(4):
                load(a, s).wait()
                swap(a, s).wait_recv()

                @pl.when(s == my_chip)
                def _():
                    @pl.loop(0, R // rc)
                    def _(t):
                        rows = pl.ds(pl.multiple_of(t * rc, rc), rc)
                        out_refs[a][rows, :] = chip_sum(s, rows)

                @pl.when(s != my_chip)
                def _():
                    @pl.loop(0, R // rc)
                    def _(t):
                        rows = pl.ds(pl.multiple_of(t * rc, rc), rc)
                        partials[a][(s ^ my_chip) - 1, rows, :] = chip_sum(s, rows).astype(BF16)

        for a in range(n):
            for s in range(4):
                swap(a, s).wait_send()

    vmem = pl.BlockSpec(memory_space=pltpu.VMEM)
    outs = pl.pallas_call(
        body, name=name,
        in_specs=[pl.BlockSpec(memory_space=pl.ANY)] * n, out_specs=[vmem] * (2 * n),
        out_shape=[shape for R, C in shapes
                   for shape in (jax.ShapeDtypeStruct((3, R, C), BF16), jax.ShapeDtypeStruct((R, C), F32))],
        scratch_shapes=[pltpu.VMEM((4, R, C), BF16) for R, C in shapes for _ in range(2)] + [
            pltpu.SemaphoreType.DMA((4 * n,)), pltpu.SemaphoreType.DMA((4 * n,)), pltpu.SemaphoreType.DMA((4 * n,))],
        compiler_params=_params(),
    )(*group)
    return [(outs[2 * a], outs[2 * a + 1]) for a in range(n)]


def _cross_chips(partials, *, name, collective_id):
    n = len(partials)

    def body(*refs):
        ins, outs = refs[:n], refs[n:2 * n]
        send_sems, recv_sems = refs[2 * n:]
        x, y, c = _position()
        my_chip = 2 * x + y
        peers = [((my_chip ^ j) // 2, (my_chip ^ j) % 2, c) for j in (1, 2, 3)]

        barrier = pltpu.get_barrier_semaphore()
        for peer in peers:
            pl.semaphore_signal(barrier, inc=1, device_id=peer, device_id_type=MESH)
        pl.semaphore_wait(barrier, 3)

        copies = [
            pltpu.make_async_remote_copy(
                src_ref=ins[a].at[j], dst_ref=outs[a].at[j],
                send_sem=send_sems.at[3 * a + j], recv_sem=recv_sems.at[3 * a + j],
                device_id=peers[j], device_id_type=MESH)
            for a in range(n) for j in range(3)]
        for cp in copies:
            cp.start()
        for cp in copies:
            cp.wait_recv()
        for cp in copies:
            cp.wait_send()

    return pl.kernel(
        body, name=name,
        out_type=[jax.ShapeDtypeStruct(p.shape, p.dtype) for p in partials],
        mesh=plsc.ScalarSubcoreMesh(axis_name="sequencer", num_cores=1),
        scratch_types=[pltpu.SemaphoreType.DMA((3 * n,)), pltpu.SemaphoreType.DMA((3 * n,))],
        compiler_params=pltpu.CompilerParams(collective_id=collective_id),
    )(*partials)


def _cross_chips_and_gather(partial, slab, *, name, collective_id):
    def body(part_ref, slab_ref, landed_ref, slabs_ref, send_sems, recv_sems, local_sem):
        x, y, c = _position()
        me, my_chip = 4 * x + 2 * y + c, 2 * x + y
        others = [me ^ k for k in range(1, N_DEV)]
        ids = [(o // 4, (o // 2) % 2, o % 2) for o in others]

        barrier = pltpu.get_barrier_semaphore()
        for peer in ids:
            pl.semaphore_signal(barrier, inc=1, device_id=peer, device_id_type=MESH)
        pl.semaphore_wait(barrier, N_DEV - 1)

        mine = pltpu.make_async_copy(slab_ref, slabs_ref.at[me], local_sem)
        mine.start()
        sends = [
            pltpu.make_async_remote_copy(
                src_ref=part_ref.at[j], dst_ref=landed_ref.at[j], send_sem=send_sems.at[j], recv_sem=recv_sems.at[j],
                device_id=((my_chip ^ (j + 1)) // 2, (my_chip ^ (j + 1)) % 2, c), device_id_type=MESH)
            for j in range(3)]
        sends += [
            pltpu.make_async_remote_copy(
                src_ref=slab_ref, dst_ref=slabs_ref.at[me], send_sem=send_sems.at[3 + k], recv_sem=recv_sems.at[3 + k],
                device_id=ids[k], device_id_type=MESH)
            for k in range(N_DEV - 1)]
        arrivals = sends[:3] + [
            pltpu.make_async_remote_copy(
                src_ref=slab_ref, dst_ref=slabs_ref.at[others[k]], send_sem=send_sems.at[3 + k],
                recv_sem=recv_sems.at[3 + k], device_id=ids[k], device_id_type=MESH)
            for k in range(N_DEV - 1)]
        for cp in sends:
            cp.start()
        for cp in arrivals:
            cp.wait_recv()
        for cp in sends:
            cp.wait_send()
        mine.wait()

    n_sems = 3 + N_DEV - 1
    return pl.kernel(
        body, name=name,
        out_type=[jax.ShapeDtypeStruct(partial.shape, partial.dtype),
                  jax.ShapeDtypeStruct((N_DEV,) + slab.shape, slab.dtype)],
        mesh=plsc.ScalarSubcoreMesh(axis_name="sequencer", num_cores=1),
        scratch_types=[pltpu.SemaphoreType.DMA((n_sems,)), pltpu.SemaphoreType.DMA((n_sems,)), pltpu.SemaphoreType.DMA],
        compiler_params=pltpu.CompilerParams(collective_id=collective_id),
    )(partial, slab)


def _sum_devices(gathered, after, *, name):
    _, R, C = gathered.shape

    def body(in_ref, after_ref, out_ref):
        total = in_ref[0]
        for d in range(1, N_DEV):
            total = total + in_ref[d]
        out_ref[...] = total

    return pl.pallas_call(
        body, name=name, grid=(1,),
        in_specs=[pl.BlockSpec((N_DEV, R, C), lambda i: (0, 0, 0)), AFTER],
        out_specs=pl.BlockSpec((R, C), lambda i: (0, 0)),
        out_shape=jax.ShapeDtypeStruct((R, C), F32),
        compiler_params=_params(("arbitrary",)),
    )(gathered, _in_hbm(after))


def _owner_sum(own, landed, after, *, name):
    R, C = own.shape
    tr = _row_tile(R, C)

    def body(own_ref, landed_ref, after_ref, out_ref):
        total = own_ref[...]
        for j in range(3):
            total = total + landed_ref[j].astype(F32)
        out_ref[...] = total

    return pl.pallas_call(
        body, name=name, grid=(R // tr,),
        in_specs=[pl.BlockSpec((tr, C), lambda i: (i, 0)), pl.BlockSpec((3, tr, C), lambda i: (0, i, 0)), AFTER],
        out_specs=pl.BlockSpec((tr, C), lambda i: (i, 0)),
        out_shape=jax.ShapeDtypeStruct((R, C), F32),
        compiler_params=_params(("arbitrary",)),
    )(own, landed, _in_hbm(after))


def _local_step(x, target, norms, pool_w_group, pool_scale, wgu1, wd1, w_in, wbp, wba, w_out, wgu2, wd2, exchange):
    n1g, nmg, n2g, nfg = norms
    D = x.shape[1]
    gu1, hid1 = _ffn_up(x, n1g, wgu1, tm=1024, name="ffn1_up")
    h1 = _ffn_down(x, hid1, wd1, tm=512, name="ffn1_down")
    un, proj = _inproj_fwd(h1, nmg, w_in, tm=1024, name="inproj_fwd")
    p = _pool_fwd(proj, pool_w_group, pool_scale, name="pool_fwd")
    o, ltot = _attn_fwd(proj, name="attn_fwd")
    h2, m = _mix_fwd(h1, p, o, proj, wbp, wba, w_out, tm=512, name="mix_fwd")
    gu2, hid2 = _ffn_up(h2, n2g, wgu2, tm=1024, name="ffn2_up")
    h3 = _ffn_down(h2, hid2, wd2, tm=512, name="ffn2_down")
    dh3, df2, loss, d_nf = _loss_bwd(h3, target, nfg, tm=256, name="loss_bwd")

    dh2, d_n2, n2, dgu2 = _ffn_bwd(dh3, df2, h2, n2g, gu2, wgu2, wd2, df2, tm=512, name="ffn2_bwd")
    d_wd2 = _wgrad_down(hid2, df2, tk=WGRAD_TOKENS, name="ffn2_wgrad_down")
    d_wgu2 = _wgrad_gate_up(n2, dgu2, tk=WGRAD_TOKENS, name="ffn2_wgrad_gate_up")
    (g_wd2, g_wgu2), token = exchange("ffn2", [d_wd2.reshape(N_DEV, FF_SHARD_PAD, D), d_wgu2])

    dyp, dys, dp, do, dgl = _mix_bwd(dh2, p, o, proj, wbp, wba, w_out, token, tm=512, name="mix_bwd")
    d_wout = _wgrad_full(m, dh2, tk=WGRAD_TOKENS, name="wgrad_out")
    d_wbp = _wgrad_full(dyp, p, tk=WGRAD_TOKENS, name="wgrad_branch_pool")
    d_wba = _wgrad_full(dys, o, tk=WGRAD_TOKENS, name="wgrad_branch_attn")
    by_owner = lambda g: g.reshape(N_DEV, g.shape[0] // N_DEV, g.shape[1])
    (g_wbp, g_wba, g_wout), token = exchange("mix", [by_owner(d_wbp), by_owner(d_wba), by_owner(d_wout)])
    dxp, d_wgroup, d_scale = _pool_bwd(dp, proj, pool_w_group, pool_scale, name="pool_bwd")
    dq, dk, dv = _attn_bwd(proj, do, ltot, token, name="attn_bwd")
    dproj = jnp.concatenate([dxp, dq, dk, dv, dgl], axis=1)
    dh1, df1, d_nm = _inproj_bwd(dproj, dh2, h1, nmg, w_in, tm=512, name="inproj_bwd")
    d_win = _wgrad_in(dproj, un, tk=WGRAD_TOKENS, name="wgrad_in")
    d_wd1 = _wgrad_down(hid1, df1, tk=WGRAD_TOKENS, name="ffn1_wgrad_down")
    (g_win, g_wd1), token = exchange("w_in_ffn1_down", [d_win, d_wd1.reshape(N_DEV, FF_SHARD_PAD, D)])

    dx, d_n1, n1, dgu1 = _ffn_bwd(dh1, df1, x, n1g, gu1, wgu1, wd1, token, tm=512, name="ffn1_bwd")
    d_wgu1_a = _wgrad_gate_up(n1, dgu1, tk=WGRAD_TOKENS, name="ffn1_wgrad_gate_up_a", part=0, parts=2)
    (g_wgu1_a,), token = exchange("ffn1_gate_up_a", [d_wgu1_a])
    d_wgu1_b = _wgrad_gate_up(n1, dgu1, tk=WGRAD_TOKENS, name="ffn1_wgrad_gate_up_b", part=1, parts=2)
    (g_wgu1_b, replicated), token = exchange("last", [d_wgu1_b, d_n1, d_nm, d_n2, d_nf, d_scale, d_wgroup, loss])
    g_wgu1 = (g_wgu1_a, g_wgu1_b)

    sharded = (g_wgu1, g_wd1, g_win, g_wbp, g_wba, g_wout, g_wgu2, g_wd2)
    return dx, sharded, replicated, token


def _hidden_major(w):
    return jnp.swapaxes(w[0], 0, 1)


def _pad_gate_up(wt):
    d = wt.shape[1]
    wt = wt.astype(BF16).reshape(2, FF_SHARD, d)
    return jnp.pad(wt, ((0, 0), (0, FF_SHARD_PAD - FF_SHARD), (0, 0))).reshape(2 * FF_SHARD_PAD, d)


def _unpad_gate_up(gt):
    d = gt.shape[1]
    return gt.reshape(2, FF_SHARD_PAD, d)[:, :FF_SHARD].reshape(2 * FF_SHARD, d)


def _pad_down(w):
    return jnp.pad(w.astype(BF16), ((0, FF_SHARD_PAD - FF_SHARD), (0, 0)))


def kernel(x, ffn1_norm, ffn1_w_gate_up, ffn1_w_down, mix_norm, w_in, pool_w_group, pool_scale, w_branch_pool, w_branch_attn, w_out, ffn2_norm, ffn2_w_gate_up, ffn2_w_down, final_norm, loss_target, m_ffn1_norm, m_ffn1_w_gate_up, m_ffn1_w_down, m_mix_norm, m_w_in, m_pool_w_group, m_pool_scale, m_w_branch_pool, m_w_branch_attn, m_w_out, m_ffn2_norm, m_ffn2_w_gate_up, m_ffn2_w_down, m_final_norm, v_ffn1_norm, v_ffn1_w_gate_up, v_ffn1_w_down, v_mix_norm, v_w_in, v_pool_w_group, v_pool_scale, v_w_branch_pool, v_w_branch_attn, v_w_out, v_ffn2_norm, v_ffn2_w_gate_up, v_ffn2_w_down, v_final_norm):
    D = x.shape[-1]
    weights = dict(ffn1_norm=ffn1_norm, ffn1_w_gate_up=ffn1_w_gate_up, ffn1_w_down=ffn1_w_down, mix_norm=mix_norm,
                   w_in=w_in, pool_w_group=pool_w_group, pool_scale=pool_scale, w_branch_pool=w_branch_pool,
                   w_branch_attn=w_branch_attn, w_out=w_out, ffn2_norm=ffn2_norm, ffn2_w_gate_up=ffn2_w_gate_up,
                   ffn2_w_down=ffn2_w_down, final_norm=final_norm)
    first = dict(ffn1_norm=m_ffn1_norm, ffn1_w_gate_up=m_ffn1_w_gate_up, ffn1_w_down=m_ffn1_w_down,
                 mix_norm=m_mix_norm, w_in=m_w_in, pool_w_group=m_pool_w_group, pool_scale=m_pool_scale,
                 w_branch_pool=m_w_branch_pool, w_branch_attn=m_w_branch_attn, w_out=m_w_out,
                 ffn2_norm=m_ffn2_norm, ffn2_w_gate_up=m_ffn2_w_gate_up, ffn2_w_down=m_ffn2_w_down,
                 final_norm=m_final_norm)
    second = dict(ffn1_norm=v_ffn1_norm, ffn1_w_gate_up=v_ffn1_w_gate_up, ffn1_w_down=v_ffn1_w_down,
                  mix_norm=v_mix_norm, w_in=v_w_in, pool_w_group=v_pool_w_group, pool_scale=v_pool_scale,
                  w_branch_pool=v_w_branch_pool, w_branch_attn=v_w_branch_attn, w_out=v_w_out,
                  ffn2_norm=v_ffn2_norm, ffn2_w_gate_up=v_ffn2_w_gate_up, ffn2_w_down=v_ffn2_w_down,
                  final_norm=v_final_norm)
    order = list(weights)

    wgu1, = _all_gather([_pad_gate_up(_hidden_major(ffn1_w_gate_up))], name="all_gather_ffn1_gate_up", collective_id=0)
    wd1, = _all_gather([_pad_down(ffn1_w_down[0])], name="all_gather_ffn1_down", collective_id=10)
    transposed = lambda w: jnp.swapaxes(w[0], 0, 1).astype(BF16)
    win_g, = _all_gather([transposed(w_in)], name="all_gather_w_in", collective_id=1)
    wbp_g, wba_g = _all_gather([transposed(w_branch_pool), transposed(w_branch_attn)],
                               name="all_gather_branches", collective_id=2)
    wout_g, = _all_gather([w_out[0].astype(BF16)], name="all_gather_w_out", collective_id=11)
    wgu2, wd2 = _all_gather([_pad_gate_up(_hidden_major(ffn2_w_gate_up)), _pad_down(ffn2_w_down[0])],
                            name="all_gather_ffn2", collective_id=3)
    whole = lambda g: g.reshape(g.shape[0] * g.shape[1], g.shape[2])
    wd1, wd2, win_g, wbp_g, wba_g, wout_g = (whole(g) for g in (wd1, wd2, win_g, wbp_g, wba_g, wout_g))

    cross_ids = {"ffn2": 4, "mix": 5, "w_in_ffn1_down": 6, "ffn1_gate_up_a": 7, "last": 8}
    small = ["ffn1_norm", "mix_norm", "ffn2_norm", "final_norm", "pool_scale", "pool_w_group"]

    def tile_rows(a):
        a = a.reshape(-1, 128)
        return jnp.pad(a, ((0, -a.shape[0] % 8), (0, 0)))

    def exchange(tag, group):
        if tag == "last":
            slab = jnp.concatenate([tile_rows(g) for g in group[1:-1]] + [jnp.broadcast_to(group[-1], (8, 128))], axis=0)
            (partial, own), = _chip_sums([group[0]], name="chip_sums_last")
            landed, slabs = _cross_chips_and_gather(partial, slab, name="cross_chips_last", collective_id=cross_ids[tag])
            return [(own, landed), slabs], own
        sums = _chip_sums(group, name="chip_sums_" + tag)
        landed = _cross_chips([s[0] for s in sums], name="cross_chips_" + tag, collective_id=cross_ids[tag])
        return [(s[1], l) for s, l in zip(sums, landed)], sums[-1][1]

    norms = (ffn1_norm, mix_norm, ffn2_norm, final_norm.reshape(1, D))
    dx, sharded, slabs, last = _local_step(
        x[0], loss_target[0], norms, pool_w_group[0], pool_scale, wgu1, wd1, win_g, wbp_g, wba_g, wout_g, wgu2, wd2,
        exchange)
    names = ["ffn1_w_gate_up", "ffn1_w_down", "w_in", "w_branch_pool", "w_branch_attn", "w_out",
             "ffn2_w_gate_up", "ffn2_w_down"]
    handles = dict(zip(names, sharded))
    grads, delta, new_m, new_v = {}, {}, {}, {}
    after = last
    for k in ("ffn2_w_down", "ffn2_w_gate_up", "w_branch_pool", "w_branch_attn", "w_out", "w_in", "ffn1_w_down",
              "ffn1_w_gate_up"):
        hidden_major = k.endswith("w_gate_up")
        if isinstance(handles[k][0], tuple):
            first_half = _owner_sum(*handles[k][0], after, name="owner_sum_" + k + "_a")
            second_half = _owner_sum(*handles[k][1], first_half, name="owner_sum_" + k + "_b")
            g = jnp.concatenate([first_half[:FF_SHARD], second_half[:FF_SHARD]], axis=0)
        else:
            g = _owner_sum(*handles[k], after, name="owner_sum_" + k)
            if hidden_major:
                g = _unpad_gate_up(g)
            elif k in ("w_in", "w_branch_pool", "w_branch_attn"):
                g = jnp.swapaxes(g, 0, 1)
            else:
                g = g[:weights[k].shape[1]]
        view = _hidden_major if hidden_major else (lambda a: a[0])
        back = (lambda a: jnp.swapaxes(a, 0, 1)[None]) if hidden_major else (lambda a: a[None])
        out = _adamw(view(weights[k]), g, view(first[k]), view(second[k]), name="adamw_" + k)
        after = out[0]
        grads[k] = back(g)
        delta[k], new_m[k], new_v[k] = (back(a) for a in out)

    rows = [weights[k].size // 128 for k in small]
    padded_rows = [-(-r // 8) * 8 for r in rows]
    starts = [sum(padded_rows[:i]) for i in range(len(rows) + 1)]
    total = _sum_devices(slabs, after, name="sum_replicated")
    loss_out = total[starts[-1], 0]
    small_w = jnp.concatenate([tile_rows(weights[k]) for k in small], axis=0)
    small_m = jnp.concatenate([tile_rows(first[k]) for k in small], axis=0)
    small_v = jnp.concatenate([tile_rows(second[k]) for k in small], axis=0)
    small_out = _adamw(small_w, total[:starts[-1]], small_m, small_v, name="adamw_replicated")
    for name_, start, n_rows in zip(small, starts, rows):
        shape = weights[name_].shape
        grads[name_] = total[start:start + n_rows].reshape(shape)
        delta[name_], new_m[name_], new_v[name_] = (a[start:start + n_rows].reshape(shape) for a in small_out)

    return (loss_out, dx[None], *[grads[k] for k in order], *[delta[k] for k in order],
            *[new_m[k] for k in order], *[new_v[k] for k in order])
```

```python
import jax
import jax.numpy as jnp
from jax import lax
from jax.experimental import pallas as pl
from jax.experimental.pallas import tpu as pltpu
from jax.experimental.pallas import tpu_sc as plsc

F32 = jnp.float32
BF16 = jnp.bfloat16
MESH = pl.DeviceIdType.MESH

RMS_EPS = 1e-6
N_DEV = 8
N_HEADS = 8
HEAD_DIM = 64
HEAD_PAIR = 2 * HEAD_DIM
POOL_WINDOWS = (2, 4, 8, 16)
POOL_GROUP = 128
POOL_WIDTH = 512
SB_WIDTH = 512
FF_SHARD = 352
FF_SHARD_PAD = 384
ATTN_K_BLOCK = 256
ATTN_Q_BLOCK_FWD = 512
ATTN_Q_BLOCK_BWD = 256
ATTN_SCALE = 0.125

ADAM_LR = 0.001
ADAM_B1 = 0.9
ADAM_B2 = 0.999
ADAM_EPS = 1e-08
ADAM_WD = 0.01
ADAM_STEP = 10

VMEM_LIMIT = 48 << 20
WGRAD_TOKENS = 2048


def _params(dims=None):
    return pltpu.CompilerParams(dimension_semantics=dims, vmem_limit_bytes=VMEM_LIMIT)


def _mm(a, b):
    return jnp.dot(a, b, preferred_element_type=F32)


def _mm_nt(a, b):
    return lax.dot_general(a, b, (((1,), (1,)), ((), ())), preferred_element_type=F32)


def _mm_tn(a, b):
    return lax.dot_general(a, b, (((0,), (0,)), ((), ())), preferred_element_type=F32)


def _row_tile(rows, cols):
    limit = max(8, (512 * 1024) // cols)
    return max(t for t in range(8, rows + 1, 8) if rows % t == 0 and (t <= limit or t == 8))


def _rstd(xf):
    return lax.rsqrt(jnp.mean(xf * xf, axis=-1, keepdims=True) + RMS_EPS)


def _rms_bwd(xf, gain, dn):
    r = _rstd(xf)
    xh = xf * r
    dgain = jnp.sum(dn * xh, axis=0, keepdims=True)
    dxh = dn * gain
    dx = r * (dxh - xh * jnp.mean(dxh * xh, axis=-1, keepdims=True))
    return dx, dgain


def _ffn_up(x, gain, wgu, *, tm, name):
    T, D = x.shape
    tm = min(tm, T)
    nb, bw = wgu.shape[0] // 2, wgu.shape[1]

    def body(x_ref, gain_ref, wg_ref, wu_ref, gu_ref, hid_ref, n_scr):
        @pl.when(pl.program_id(1) == 0)
        def _():
            xf = x_ref[...]
            n_scr[...] = (xf * _rstd(xf) * gain_ref[...]).astype(BF16)

        halves = (pl.ds(0, tm // 2), pl.ds(tm // 2, tm // 2))
        wg, wu = wg_ref[...], wu_ref[...]
        gus = [(_mm_nt(n_scr[rows, :], wg), _mm_nt(n_scr[rows, :], wu)) for rows in halves]
        for rows, (g, u) in zip(halves, gus):
            gu_ref[0, rows, :] = g.astype(BF16)
            gu_ref[1, rows, :] = u.astype(BF16)
            hid_ref[rows, :] = (g * jax.nn.sigmoid(g) * u).astype(BF16)

    return pl.pallas_call(
        body, name=name, grid=(T // tm, nb),
        in_specs=[
            pl.BlockSpec((tm, D), lambda i, j: (i, 0)),
            pl.BlockSpec((1, D), lambda i, j: (0, 0)),
            pl.BlockSpec((None, bw, D), lambda i, j: (j, 0, 0)),
            pl.BlockSpec((None, bw, D), lambda i, j: (j + nb, 0, 0)),
        ],
        out_specs=[
            pl.BlockSpec((2, tm, bw), lambda i, j: (0, i, j)),
            pl.BlockSpec((tm, bw), lambda i, j: (i, j)),
        ],
        out_shape=[jax.ShapeDtypeStruct((2, T, nb * bw), BF16), jax.ShapeDtypeStruct((T, nb * bw), BF16)],
        scratch_shapes=[pltpu.VMEM((tm, D), BF16)],
        compiler_params=_params(("arbitrary", "arbitrary")),
    )(x, gain, wgu, wgu)


def _ffn_down(x, hid, wd, *, tm, name):
    T, D = x.shape
    tm = min(tm, T)
    F = hid.shape[1]

    def body(x_ref, hid_ref, wd_ref, h_ref):
        h_ref[...] = x_ref[...] + 0.5 * _mm(hid_ref[...], wd_ref[...])

    return pl.pallas_call(
        body, name=name, grid=(T // tm,),
        in_specs=[
            pl.BlockSpec((tm, D), lambda i: (i, 0)),
            pl.BlockSpec((tm, F), lambda i: (i, 0)),
            pl.BlockSpec((F, D), lambda i: (0, 0)),
        ],
        out_specs=pl.BlockSpec((tm, D), lambda i: (i, 0)),
        out_shape=jax.ShapeDtypeStruct((T, D), F32),
        compiler_params=_params(("arbitrary",)),
    )(x, hid, wd)


AFTER = pl.BlockSpec(memory_space=pltpu.HBM)


def _in_hbm(token):
    return pltpu.with_memory_space_constraint(token, pltpu.HBM)


def _ffn_bwd(dh, df, x, gain, gu, wgu, wd, after, *, tm, name):
    T, D = x.shape
    tm = min(tm, T)
    nb, bw = wgu.shape[0] // 2, wgu.shape[1]

    def body(dh_ref, df_ref, x_ref, gain_ref, gu_ref, wg_ref, wu_ref, wd_ref, after_ref,
             dx_ref, dgain_ref, n_ref, dgu_ref, dn_acc):
        i, j = pl.program_id(0), pl.program_id(1)

        @pl.when(j == 0)
        def _():
            xf = x_ref[...]
            n_ref[...] = (xf * _rstd(xf) * gain_ref[...]).astype(BF16)
            dn_acc[...] = jnp.zeros_like(dn_acc)

        @pl.when((i == 0) & (j == 0))
        def _():
            dgain_ref[...] = jnp.zeros_like(dgain_ref)

        halves = (pl.ds(0, tm // 2), pl.ds(tm // 2, tm // 2))
        wd, wg, wu = wd_ref[...], wg_ref[...], wu_ref[...]
        dhids = [_mm_nt(df_ref[rows, :], wd) for rows in halves]
        for rows, dhid in zip(halves, dhids):
            g = gu_ref[0, rows, :].astype(F32)
            u = gu_ref[1, rows, :].astype(F32)
            s = jax.nn.sigmoid(g)
            silu = g * s
            dg = (dhid * u * (s * (1.0 + g * (1.0 - s)))).astype(BF16)
            du = (dhid * silu).astype(BF16)
            dgu_ref[0, rows, :] = dg
            dgu_ref[1, rows, :] = du
            dn_acc[rows, :] += _mm(dg, wg) + _mm(du, wu)

        @pl.when(j == nb - 1)
        def _():
            dx, dgain = _rms_bwd(x_ref[...], gain_ref[...], dn_acc[...])
            dx_ref[...] = dh_ref[...] + dx
            dgain_ref[...] += dgain

    row = lambda i, j: (i, 0)
    return pl.pallas_call(
        body, name=name, grid=(T // tm, nb),
        in_specs=[
            pl.BlockSpec((tm, D), row),
            pl.BlockSpec((tm, D), row),
            pl.BlockSpec((tm, D), row),
            pl.BlockSpec((1, D), lambda i, j: (0, 0)),
            pl.BlockSpec((2, tm, bw), lambda i, j: (0, i, j)),
            pl.BlockSpec((None, bw, D), lambda i, j: (j, 0, 0)),
            pl.BlockSpec((None, bw, D), lambda i, j: (j + nb, 0, 0)),
            pl.BlockSpec((bw, D), lambda i, j: (j, 0)),
            AFTER,
        ],
        out_specs=[
            pl.BlockSpec((tm, D), row),
            pl.BlockSpec((1, D), lambda i, j: (0, 0)),
            pl.BlockSpec((tm, D), row),
            pl.BlockSpec((2, tm, bw), lambda i, j: (0, i, j)),
        ],
        out_shape=[
            jax.ShapeDtypeStruct((T, D), F32),
            jax.ShapeDtypeStruct((1, D), F32),
            jax.ShapeDtypeStruct((T, D), BF16),
            jax.ShapeDtypeStruct((2, T, nb * bw), BF16),
        ],
        scratch_shapes=[pltpu.VMEM((tm, D), F32)],
        compiler_params=_params(("arbitrary", "arbitrary")),
    )(dh, df, x, gain, gu, wgu, wgu, wd, _in_hbm(after))


def _wgrad(a, b, *, grid, a_spec, b_spec, out_spec, out_shape, acc_shape, name):
    nk = grid[2]

    def body(a_ref, b_ref, o_ref, acc):
        k = pl.program_id(2)

        @pl.when(k == 0)
        def _():
            acc[...] = jnp.zeros_like(acc)

        acc[...] += _mm_tn(a_ref[...].astype(BF16), b_ref[...].astype(BF16))

        @pl.when(k == nk - 1)
        def _():
            o_ref[...] = acc[...].astype(o_ref.dtype)

    return pl.pallas_call(
        body, name=name, grid=grid, in_specs=[a_spec, b_spec], out_specs=out_spec,
        out_shape=jax.ShapeDtypeStruct(out_shape, BF16),
        scratch_shapes=[pltpu.VMEM(acc_shape, F32)],
        compiler_params=_params(("arbitrary", "arbitrary", "arbitrary")),
    )(a, b)


def _wgrad_gate_up(n, dgu, *, tk, name, part=0, parts=1):
    T, D = n.shape
    tk = min(tk, T)
    owner_rows = FF_SHARD_PAD * 2
    nb = dgu.shape[2] // owner_rows
    bw = owner_rows // parts
    return _wgrad(
        dgu, n, grid=(2 * nb, 1, T // tk), name=name,
        a_spec=pl.BlockSpec((None, tk, bw), lambda m, c, k: (m // nb, k, parts * (m % nb) + part)),
        b_spec=pl.BlockSpec((tk, D), lambda m, c, k: (k, 0)),
        out_spec=pl.BlockSpec((None, bw, D), lambda m, c, k: (m, 0, 0)),
        out_shape=(2 * nb, bw, D), acc_shape=(bw, D))


def _wgrad_down(hid, df, *, tk, name):
    T, D = df.shape
    tk = min(tk, T)
    bw = FF_SHARD_PAD * 2
    nb = hid.shape[1] // bw
    return _wgrad(
        hid, df, grid=(nb, 1, T // tk), name=name,
        a_spec=pl.BlockSpec((tk, bw), lambda m, c, k: (k, m)),
        b_spec=pl.BlockSpec((tk, D), lambda m, c, k: (k, 0)),
        out_spec=pl.BlockSpec((bw, D), lambda m, c, k: (m, 0)),
        out_shape=(nb * bw, D), acc_shape=(bw, D))


def _wgrad_in(dproj, un, *, tk, name):
    T, D = un.shape
    tk = min(tk, T)
    bw = dproj.shape[1] // N_DEV
    return _wgrad(
        dproj, un, grid=(N_DEV, 1, T // tk), name=name,
        a_spec=pl.BlockSpec((tk, bw), lambda m, c, k: (k, m)),
        b_spec=pl.BlockSpec((tk, D), lambda m, c, k: (k, 0)),
        out_spec=pl.BlockSpec((None, bw, D), lambda m, c, k: (m, 0, 0)),
        out_shape=(N_DEV, bw, D), acc_shape=(bw, D))


def _wgrad_full(a, b, *, tk, name):
    T, M = a.shape
    tk = min(tk, T)
    N = b.shape[1]
    return _wgrad(
        a, b, grid=(1, 1, T // tk), name=name,
        a_spec=pl.BlockSpec((tk, M), lambda m, c, k: (k, 0)),
        b_spec=pl.BlockSpec((tk, N), lambda m, c, k: (k, 0)),
        out_spec=pl.BlockSpec((M, N), lambda m, c, k: (0, 0)), out_shape=(M, N), acc_shape=(M, N))


def _loss_bwd(h, target, gain, *, tm, name):
    T, D = h.shape
    tm = min(tm, T)

    def body(h_ref, t_ref, gain_ref, dh_ref, df_ref, loss_ref, dgain_ref):
        @pl.when(pl.program_id(0) == 0)
        def _():
            loss_ref[...] = jnp.zeros_like(loss_ref)
            dgain_ref[...] = jnp.zeros_like(dgain_ref)

        xf = h_ref[...]
        gain = gain_ref[...]
        err = xf * _rstd(xf) * gain - t_ref[...]
        loss_ref[...] += 0.5 * jnp.sum(jnp.mean(err * err, axis=-1, keepdims=True), axis=0, keepdims=True)
        dx, dgain = _rms_bwd(xf, gain, err * (1.0 / D))
        dh_ref[...] = dx
        df_ref[...] = (0.5 * dx).astype(BF16)
        dgain_ref[...] += dgain

    row = lambda i: (i, 0)
    fixed = lambda i: (0, 0)
    return pl.pallas_call(
        body, name=name, grid=(T // tm,),
        in_specs=[pl.BlockSpec((tm, D), row), pl.BlockSpec((tm, D), row), pl.BlockSpec((1, D), fixed)],
        out_specs=[pl.BlockSpec((tm, D), row), pl.BlockSpec((tm, D), row), pl.BlockSpec((1, 128), fixed),
                   pl.BlockSpec((1, D), fixed)],
        out_shape=[jax.ShapeDtypeStruct((T, D), F32), jax.ShapeDtypeStruct((T, D), BF16),
                   jax.ShapeDtypeStruct((1, 128), F32), jax.ShapeDtypeStruct((1, D), F32)],
        compiler_params=_params(("arbitrary",)),
    )(h, target, gain)


def _inproj_fwd(h, gain, w_in_t, *, tm, name):
    T, D = h.shape
    tm = min(tm, T)
    bn = D
    nb = w_in_t.shape[0] // bn

    def body(h_ref, gain_ref, wt_ref, un_ref, proj_ref):
        @pl.when(pl.program_id(1) == 0)
        def _():
            xf = h_ref[...]
            un_ref[...] = (xf * _rstd(xf) * gain_ref[...]).astype(BF16)

        proj_ref[...] = _mm_nt(un_ref[...], wt_ref[...])

    return pl.pallas_call(
        body, name=name, grid=(T // tm, nb),
        in_specs=[
            pl.BlockSpec((tm, D), lambda i, j: (i, 0)),
            pl.BlockSpec((1, D), lambda i, j: (0, 0)),
            pl.BlockSpec((bn, D), lambda i, j: (j, 0)),
        ],
        out_specs=[pl.BlockSpec((tm, D), lambda i, j: (i, 0)), pl.BlockSpec((tm, bn), lambda i, j: (i, j))],
        out_shape=[jax.ShapeDtypeStruct((T, D), BF16), jax.ShapeDtypeStruct((T, nb * bn), F32)],
        compiler_params=_params(("arbitrary", "arbitrary")),
    )(h, gain, w_in_t)


def _inproj_bwd(dproj, dh, h, gain, w_in_t, *, tm, name):
    T, D = h.shape
    tm = min(tm, T)
    width = w_in_t.shape[0]

    def body(dp_ref, dh_ref, h_ref, gain_ref, wt_ref, dx_ref, df_ref, dgain_ref):
        @pl.when(pl.program_id(0) == 0)
        def _():
            dgain_ref[...] = jnp.zeros_like(dgain_ref)

        dx, dgain = _rms_bwd(h_ref[...], gain_ref[...], _mm(dp_ref[...], wt_ref[...]))
        dh_in = dh_ref[...] + dx
        dx_ref[...] = dh_in
        df_ref[...] = (0.5 * dh_in).astype(BF16)
        dgain_ref[...] += dgain

    row = lambda i: (i, 0)
    fixed = lambda i: (0, 0)
    return pl.pallas_call(
        body, name=name, grid=(T // tm,),
        in_specs=[
            pl.BlockSpec((tm, width), row),
            pl.BlockSpec((tm, D), row),
            pl.BlockSpec((tm, D), row),
            pl.BlockSpec((1, D), fixed),
            pl.BlockSpec((width, D), fixed),
        ],
        out_specs=[pl.BlockSpec((tm, D), row), pl.BlockSpec((tm, D), row), pl.BlockSpec((1, D), fixed)],
        out_shape=[jax.ShapeDtypeStruct((T, D), F32), jax.ShapeDtypeStruct((T, D), BF16),
                   jax.ShapeDtypeStruct((1, D), F32)],
        compiler_params=_params(("arbitrary",)),
    )(dproj, dh, h, gain, w_in_t)


def _window_sum(x, row, doublings, *, backward):
    T = x.shape[0]
    s = x
    for k in range(doublings):
        sh = 1 << k
        if backward:
            s = s + jnp.where(row < T - sh, pltpu.roll(s, T - sh, 0), 0.0)
        else:
            s = s + jnp.where(row >= sh, pltpu.roll(s, sh, 0), 0.0)
    return s


def _pool_fwd(proj, w_group, scale, *, name):
    T = proj.shape[0]

    def body(xp_ref, w_ref, scale_ref, p_ref):
        row = lax.broadcasted_iota(jnp.int32, (T, POOL_GROUP), 0)
        for gi, window in enumerate(POOL_WINDOWS):
            cols = slice(gi * POOL_GROUP, (gi + 1) * POOL_GROUP)
            x = xp_ref[:, cols]
            inv_count = 1.0 / jnp.minimum(row + 1, window).astype(F32)
            yc = _window_sum(x, row, gi + 1, backward=False) * inv_count - x
            pre = _mm(yc.astype(BF16), w_ref[gi].astype(BF16))
            p_ref[:, cols] = pre * scale_ref[:, cols]

    return pl.pallas_call(
        body, name=name, grid=(1,),
        in_specs=[
            pl.BlockSpec((T, POOL_WIDTH), lambda i: (0, 0)),
            pl.BlockSpec(w_group.shape, lambda i: (0, 0, 0)),
            pl.BlockSpec((1, POOL_WIDTH), lambda i: (0, 0)),
        ],
        out_specs=pl.BlockSpec((T, POOL_WIDTH), lambda i: (0, 0)),
        out_shape=jax.ShapeDtypeStruct((T, POOL_WIDTH), F32),
        compiler_params=_params(("arbitrary",)),
    )(proj, w_group, scale)


def _pool_bwd(dp, proj, w_group, scale, *, name):
    T = proj.shape[0]

    def body(dp_ref, xp_ref, w_ref, scale_ref, dxp_ref, dw_ref, dscale_ref):
        row = lax.broadcasted_iota(jnp.int32, (T, POOL_GROUP), 0)
        for gi, window in enumerate(POOL_WINDOWS):
            cols = slice(gi * POOL_GROUP, (gi + 1) * POOL_GROUP)
            x = xp_ref[:, cols]
            inv_count = 1.0 / jnp.minimum(row + 1, window).astype(F32)
            yc = (_window_sum(x, row, gi + 1, backward=False) * inv_count - x).astype(BF16)
            w = w_ref[gi].astype(BF16)
            pre = _mm(yc, w)
            dpg = dp_ref[:, cols]
            dscale_ref[:, cols] = jnp.sum(dpg * pre, axis=0, keepdims=True)
            dpre = (dpg * scale_ref[:, cols]).astype(BF16)
            dw_ref[gi] = _mm_tn(yc, dpre)
            dyc = _mm_nt(dpre, w)
            dxp_ref[:, cols] = (_window_sum(dyc * inv_count, row, gi + 1, backward=True) - dyc).astype(BF16)

    return pl.pallas_call(
        body, name=name, grid=(1,),
        in_specs=[
            pl.BlockSpec((T, POOL_WIDTH), lambda i: (0, 0)),
            pl.BlockSpec((T, POOL_WIDTH), lambda i: (0, 0)),
            pl.BlockSpec(w_group.shape, lambda i: (0, 0, 0)),
            pl.BlockSpec((1, POOL_WIDTH), lambda i: (0, 0)),
        ],
        out_specs=[
            pl.BlockSpec((T, POOL_WIDTH), lambda i: (0, 0)),
            pl.BlockSpec(w_group.shape, lambda i: (0, 0, 0)),
            pl.BlockSpec((1, POOL_WIDTH), lambda i: (0, 0)),
        ],
        out_shape=[jax.ShapeDtypeStruct((T, POOL_WIDTH), BF16), jax.ShapeDtypeStruct(w_group.shape, F32),
                   jax.ShapeDtypeStruct((1, POOL_WIDTH), F32)],
        compiler_params=_params(("arbitrary",)),
    )(dp, proj, w_group, scale)


ATTN_STRIP = 32


def _log_sigmoids(z):
    lb = jnp.minimum(z, 0.0) - jnp.log(1.0 + jnp.exp(-jnp.abs(z)))
    return lb, lb - z


def _transposed_blocks(x_ref, blocks_scr, tq):
    for b in range(blocks_scr.shape[0]):
        blocks_scr[b] = x_ref[b * tq:(b + 1) * tq, :].T.astype(BF16)


def _split_bf16(x):
    hi = x.astype(BF16)
    return hi, (x - hi.astype(F32)).astype(BF16)


def _strips(n):
    return [slice(i, i + ATTN_STRIP) for i in range(0, n, ATTN_STRIP)]


def _rows(parts):
    return jnp.concatenate(parts, axis=0)


def _attn_specs(T, tq):
    q_col = POOL_WIDTH // HEAD_PAIR
    k_col = q_col + SB_WIDTH // HEAD_PAIR
    v_col = k_col + SB_WIDTH // HEAD_PAIR
    return [
        pl.BlockSpec((tq, HEAD_PAIR), lambda p, i: (i, q_col + p)),
        pl.BlockSpec((T, HEAD_PAIR), lambda p, i: (0, k_col + p)),
        pl.BlockSpec((T, HEAD_PAIR), lambda p, i: (0, v_col + p)),
    ]


def _attn_fwd(proj, *, name):
    T = proj.shape[0]
    tk = min(ATTN_K_BLOCK, T)
    tq = min(ATTN_Q_BLOCK_FWD, T)
    diagonal_blocks = tq // tk

    def body(q_ref, k_ref, v_ref, o_ref, lt_ref, kt_scr, vb_scr):
        qi = pl.program_id(1)

        @pl.when(qi == 0)
        def _():
            _transposed_blocks(k_ref, kt_scr, tk)
            vb_scr[...] = v_ref[...].astype(BF16)

        head0 = lax.broadcasted_iota(jnp.int32, (tq, HEAD_PAIR), 1) < HEAD_DIM
        q = q_ref[...] * ATTN_SCALE
        qs = (jnp.where(head0, q, 0.0).astype(BF16), jnp.where(head0, 0.0, q).astype(BF16))
        r = lax.broadcasted_iota(jnp.int32, (tq, tk), 0)
        c = lax.broadcasted_iota(jnp.int32, (tq, tk), 1)
        later = (r[:tk] > c[:tk]).astype(BF16)
        later2 = _rows([later, later])
        causal = lambda d: (lambda rows: c[rows] + d * tk < r[rows])
        strips = _strips(tq)

        def log_terms(z, valid):
            lbs, his, los, sums = [], [], [], []
            for rows in strips:
                lb, lm = _log_sigmoids(z[rows])
                if valid is not None:
                    lm = jnp.where(valid(rows), lm, 0.0)
                hi, lo = _split_bf16(lm)
                lbs.append(lb)
                his.append(hi)
                los.append(lo)
                sums.append(jnp.sum(lm, axis=1, keepdims=True))
            return lbs, jnp.concatenate([_rows(his), _rows(los)], axis=1), _rows(sums)

        def weights(lbs, run, after, valid):
            parts = []
            for rows, lb in zip(strips, lbs):
                a = jnp.exp(lb + run[rows] + after[rows])
                if valid is not None:
                    a = jnp.where(valid(rows), a, 0.0)
                parts.append(a.astype(BF16))
            return _rows(parts)

        def block(kj, carry, valid):
            kt = kt_scr[kj]
            vb = vb_scr[pl.ds(pl.multiple_of(kj * tk, tk), tk), :]
            run0, o0, run1, o1 = carry
            z0 = _mm(qs[0], kt)
            z1 = _mm(qs[1], kt)
            lbs0, split0, sums0 = log_terms(z0, valid)
            after0 = _mm(split0, later2)
            lbs1, split1, sums1 = log_terms(z1, valid)
            after1 = _mm(split1, later2)
            o0 = o0 + _mm(weights(lbs0, run0, after0, valid), vb)
            o1 = o1 + _mm(weights(lbs1, run1, after1, valid), vb)
            return run0 + sums0, o0, run1 + sums1, o1

        zero = (jnp.zeros((tq, 1), F32), jnp.zeros((tq, HEAD_PAIR), F32))
        first = diagonal_blocks * qi
        carry = zero + zero
        for d in reversed(range(diagonal_blocks)):
            carry = block(first + d, carry, causal(d))
        carry = lax.fori_loop(0, first, lambda it, cr: block(first - 1 - it, cr, None), carry)
        o_ref[...] = jnp.where(head0, carry[1], carry[3])
        lt_ref[...] = jnp.where(head0, carry[0], carry[2])

    out_spec = pl.BlockSpec((tq, HEAD_PAIR), lambda p, i: (i, p))
    return pl.pallas_call(
        body, name=name, grid=(N_HEADS // 2, T // tq),
        in_specs=_attn_specs(T, tq), out_specs=[out_spec, out_spec],
        out_shape=[jax.ShapeDtypeStruct((T, SB_WIDTH), F32), jax.ShapeDtypeStruct((T, SB_WIDTH), F32)],
        scratch_shapes=[pltpu.VMEM((T // tk, HEAD_PAIR, tk), BF16), pltpu.VMEM((T, HEAD_PAIR), BF16)],
        compiler_params=_params(("arbitrary", "arbitrary")),
    )(proj, proj, proj)


def _attn_bwd(proj, do, ltot, after, *, name):
    T = proj.shape[0]
    tk = min(ATTN_K_BLOCK, T)
    tq = min(ATTN_Q_BLOCK_BWD, T)
    diagonal_blocks = tq // tk

    def body(q_ref, k_ref, v_ref, do_ref, lt_ref, after_ref, dq_ref, dk_ref, dv_ref,
             kb_scr, kt_scr, vt_scr, dkt_ref, dvt_ref):
        qi = pl.program_id(1)

        @pl.when(qi == 0)
        def _():
            kb_scr[...] = k_ref[...].astype(BF16)
            _transposed_blocks(k_ref, kt_scr, tk)
            _transposed_blocks(v_ref, vt_scr, tk)
            dkt_ref[...] = jnp.zeros_like(dkt_ref)
            dvt_ref[...] = jnp.zeros_like(dvt_ref)

        head0 = lax.broadcasted_iota(jnp.int32, (tq, HEAD_PAIR), 1) < HEAD_DIM
        q, do_, lt = q_ref[...] * ATTN_SCALE, do_ref[...], lt_ref[...]
        qs = (jnp.where(head0, q, 0.0).astype(BF16), jnp.where(head0, 0.0, q).astype(BF16))
        q_heads = (jnp.where(head0, q, 0.0), jnp.where(head0, 0.0, q))
        do_heads = (jnp.where(head0, do_, 0.0), jnp.where(head0, 0.0, do_))
        dos = tuple(d.astype(BF16) for d in do_heads)
        qts = tuple(x.T.astype(BF16) for x in q_heads)
        dots = tuple(d.T.astype(BF16) for d in do_heads)
        lts = (jnp.max(jnp.where(head0, lt, -jnp.inf), axis=1, keepdims=True),
               jnp.max(jnp.where(head0, -jnp.inf, lt), axis=1, keepdims=True))
        r = lax.broadcasted_iota(jnp.int32, (tq, tk), 0)
        c = lax.broadcasted_iota(jnp.int32, (tq, tk), 1)
        upto = (r[:tk] <= c[:tk]).astype(BF16)
        before = (r[:tk] < c[:tk]).astype(BF16)
        upto2, before2 = _rows([upto, upto]), _rows([before, before])
        causal = lambda d: (lambda rows: c[rows] + d * tk < r[rows])
        strips = _strips(tq)

        def log_terms(z, valid):
            lbs, his, los, sums = [], [], [], []
            for rows in strips:
                lb, lm = _log_sigmoids(z[rows])
                if valid is not None:
                    lm = jnp.where(valid(rows), lm, 0.0)
                hi, lo = _split_bf16(lm)
                lbs.append(lb)
                his.append(hi)
                los.append(lo)
                sums.append(jnp.sum(lm, axis=1, keepdims=True))
            return lbs, jnp.concatenate([_rows(his), _rows(los)], axis=1), _rows(sums)

        def weights(lbs, rest, lm_upto, da, valid):
            a_parts, es, his, los, sums = [], [], [], [], []
            for rows, lb in zip(strips, lbs):
                a = jnp.exp(lb + (rest[rows] - lm_upto[rows]))
                if valid is not None:
                    a = jnp.where(valid(rows), a, 0.0)
                e = da[rows] * a
                hi, lo = _split_bf16(e)
                a_parts.append(a.astype(BF16))
                es.append(e)
                his.append(hi)
                los.append(lo)
                sums.append(jnp.sum(e, axis=1, keepdims=True))
            return _rows(a_parts), es, jnp.concatenate([_rows(his), _rows(los)], axis=1), _rows(sums)

        def score_grads(lbs, es, run_e, e_before, valid):
            parts = []
            for rows, lb, e in zip(strips, lbs, es):
                beta = jnp.exp(lb)
                dz = e * (1.0 - beta) - (run_e[rows] + e_before[rows]) * beta
                if valid is not None:
                    dz = jnp.where(valid(rows), dz, 0.0)
                parts.append(dz.astype(BF16))
            return _rows(parts)

        def block(kj, carry, valid):
            off = pl.multiple_of(kj * tk, tk)
            kb, kt, vt = kb_scr[pl.ds(off, tk), :], kt_scr[kj], vt_scr[kj]
            run_lm0, run_e0, dq0, run_lm1, run_e1, dq1 = carry
            z0, da0 = _mm(qs[0], kt), _mm(dos[0], vt)
            z1, da1 = _mm(qs[1], kt), _mm(dos[1], vt)
            lbs0, split0, lm_sums0 = log_terms(z0, valid)
            lm_upto0 = _mm(split0, upto2)
            lbs1, split1, lm_sums1 = log_terms(z1, valid)
            lm_upto1 = _mm(split1, upto2)
            a0, es0, split0, e_sums0 = weights(lbs0, lts[0] - run_lm0, lm_upto0, da0, valid)
            e_before0 = _mm(split0, before2)
            a1, es1, split1, e_sums1 = weights(lbs1, lts[1] - run_lm1, lm_upto1, da1, valid)
            e_before1 = _mm(split1, before2)
            dz0 = score_grads(lbs0, es0, run_e0, e_before0, valid)
            dkt_blk = _mm(qts[0], dz0)
            dvt_blk = _mm(dots[0], a0)
            dq0 = dq0 + _mm(dz0, kb)
            dz1 = score_grads(lbs1, es1, run_e1, e_before1, valid)
            dkt_ref[kj] += dkt_blk + _mm(qts[1], dz1)
            dvt_ref[kj] += dvt_blk + _mm(dots[1], a1)
            dq1 = dq1 + _mm(dz1, kb)
            return run_lm0 + lm_sums0, run_e0 + e_sums0, dq0, run_lm1 + lm_sums1, run_e1 + e_sums1, dq1

        zero = (jnp.zeros((tq, 1), F32), jnp.zeros((tq, 1), F32), jnp.zeros((tq, HEAD_PAIR), F32))
        first = diagonal_blocks * qi
        carry = lax.fori_loop(0, first, lambda kj, cr: block(kj, cr, None), zero + zero)
        for d in range(diagonal_blocks):
            carry = block(first + d, carry, causal(d))
        dq_ref[...] = (jnp.where(head0, carry[2], carry[5]) * ATTN_SCALE).astype(BF16)

        @pl.when(qi == T // tq - 1)
        def _():
            for b in range(T // tk):
                dk_ref[b * tk:(b + 1) * tk, :] = dkt_ref[b].T.astype(BF16)
                dv_ref[b * tk:(b + 1) * tk, :] = dvt_ref[b].T.astype(BF16)

    blk = pl.BlockSpec((tq, HEAD_PAIR), lambda p, i: (i, p))
    seq = pl.BlockSpec((T, HEAD_PAIR), lambda p, i: (0, p))
    transposed = pltpu.VMEM((T // tk, HEAD_PAIR, tk), F32)
    return pl.pallas_call(
        body, name=name, grid=(N_HEADS // 2, T // tq),
        in_specs=_attn_specs(T, tq) + [blk, blk, AFTER], out_specs=[blk, seq, seq],
        out_shape=[jax.ShapeDtypeStruct((T, SB_WIDTH), BF16)] * 3,
        scratch_shapes=[pltpu.VMEM((T, HEAD_PAIR), BF16), pltpu.VMEM((T // tk, HEAD_PAIR, tk), BF16),
                        pltpu.VMEM((T // tk, HEAD_PAIR, tk), BF16), transposed, transposed],
        compiler_params=_params(("arbitrary", "arbitrary")),
    )(proj, proj, proj, do, ltot, _in_hbm(after))


def _mix_specs(T, D, tm, wbp, w_out):
    gate_col = (POOL_WIDTH + 3 * SB_WIDTH) // D
    row = lambda i: (i, 0)
    return [
        pl.BlockSpec((tm, D), row),
        pl.BlockSpec((tm, POOL_WIDTH), row),
        pl.BlockSpec((tm, SB_WIDTH), row),
        pl.BlockSpec((tm, D), lambda i: (i, gate_col)),
        pl.BlockSpec((tm, D), lambda i: (i, gate_col + 1)),
        pl.BlockSpec(wbp.shape, lambda i: (0, 0)),
        pl.BlockSpec(wbp.shape, lambda i: (0, 0)),
        pl.BlockSpec(w_out.shape, lambda i: (0, 0)),
    ]


def _mix_fwd(h, p, o, proj, wbp, wba, w_out, *, tm, name):
    T, D = h.shape
    tm = min(tm, T)

    def body(h_ref, p_ref, o_ref, glp_ref, gls_ref, wbp_ref, wba_ref, wout_ref, hout_ref, m_ref):
        halves = (pl.ds(0, tm // 2), pl.ds(tm // 2, tm // 2))
        wbp, wba, wout = wbp_ref[...], wba_ref[...], wout_ref[...]
        branches = [(_mm_nt(p_ref[rows, :].astype(BF16), wbp), _mm_nt(o_ref[rows, :].astype(BF16), wba))
                    for rows in halves]
        for rows, (yp, ys) in zip(halves, branches):
            m = (jax.nn.sigmoid(glp_ref[rows, :]) * yp + jax.nn.sigmoid(gls_ref[rows, :]) * ys).astype(BF16)
            m_ref[rows, :] = m
            hout_ref[rows, :] = h_ref[rows, :] + _mm(m, wout)

    row = lambda i: (i, 0)
    return pl.pallas_call(
        body, name=name, grid=(T // tm,),
        in_specs=_mix_specs(T, D, tm, wbp, w_out),
        out_specs=[pl.BlockSpec((tm, D), row), pl.BlockSpec((tm, D), row)],
        out_shape=[jax.ShapeDtypeStruct((T, D), F32), jax.ShapeDtypeStruct((T, D), BF16)],
        compiler_params=_params(("arbitrary",)),
    )(h, p, o, proj, proj, wbp, wba, w_out)


def _mix_bwd(dh, p, o, proj, wbp, wba, w_out, after, *, tm, name):
    T, D = dh.shape
    tm = min(tm, T)

    def body(dh_ref, p_ref, o_ref, glp_ref, gls_ref, wbp_ref, wba_ref, wout_ref, after_ref,
             dyp_ref, dys_ref, dp_ref, do_ref, dgl_ref):
        halves = (pl.ds(0, tm // 2), pl.ds(tm // 2, tm // 2))
        wbp, wba, wout = wbp_ref[...], wba_ref[...], wout_ref[...]
        products = [(_mm_nt(dh_ref[rows, :].astype(BF16), wout), _mm_nt(p_ref[rows, :].astype(BF16), wbp),
                     _mm_nt(o_ref[rows, :].astype(BF16), wba)) for rows in halves]
        for rows, (dm, yp, ys) in zip(halves, products):
            gp = jax.nn.sigmoid(glp_ref[rows, :])
            gs = jax.nn.sigmoid(gls_ref[rows, :])
            dyp = (dm * gp).astype(BF16)
            dys = (dm * gs).astype(BF16)
            dyp_ref[rows, :] = dyp
            dys_ref[rows, :] = dys
            dgl_ref[rows, :D] = (dm * yp * gp * (1.0 - gp)).astype(BF16)
            dgl_ref[rows, D:] = (dm * ys * gs * (1.0 - gs)).astype(BF16)
            dp_ref[rows, :] = _mm(dyp, wbp)
            do_ref[rows, :] = _mm(dys, wba)

    row = lambda i: (i, 0)
    return pl.pallas_call(
        body, name=name, grid=(T // tm,),
        in_specs=_mix_specs(T, D, tm, wbp, w_out) + [AFTER],
        out_specs=[pl.BlockSpec((tm, D), row), pl.BlockSpec((tm, D), row), pl.BlockSpec((tm, POOL_WIDTH), row),
                   pl.BlockSpec((tm, SB_WIDTH), row), pl.BlockSpec((tm, 2 * D), row)],
        out_shape=[jax.ShapeDtypeStruct((T, D), BF16), jax.ShapeDtypeStruct((T, D), BF16),
                   jax.ShapeDtypeStruct((T, POOL_WIDTH), F32), jax.ShapeDtypeStruct((T, SB_WIDTH), F32),
                   jax.ShapeDtypeStruct((T, 2 * D), BF16)],
        compiler_params=_params(("arbitrary",)),
    )(dh, p, o, proj, proj, wbp, wba, w_out, _in_hbm(after))


def _adamw(w, g, m, v, *, name):
    R, C = w.shape
    tr = _row_tile(R, C)

    def body(w_ref, g_ref, m_ref, v_ref, d_ref, nm_ref, nv_ref):
        g_ = g_ref[...]
        m_ = ADAM_B1 * m_ref[...] + (1.0 - ADAM_B1) * g_
        v_ = ADAM_B2 * v_ref[...] + (1.0 - ADAM_B2) * (g_ * g_)
        m_hat = m_ / (1.0 - ADAM_B1 ** ADAM_STEP)
        v_hat = v_ / (1.0 - ADAM_B2 ** ADAM_STEP)
        d_ref[...] = -ADAM_LR * (m_hat / (jnp.sqrt(v_hat) + ADAM_EPS) + ADAM_WD * w_ref[...])
        nm_ref[...] = m_
        nv_ref[...] = v_

    spec = pl.BlockSpec((tr, C), lambda i: (i, 0))
    return pl.pallas_call(
        body, name=name, grid=(R // tr,), in_specs=[spec] * 4, out_specs=[spec] * 3,
        out_shape=[jax.ShapeDtypeStruct((R, C), F32)] * 3,
        compiler_params=_params(("arbitrary",)),
    )(w, g, m, v)


def _position():
    return lax.axis_index("x"), lax.axis_index("y"), lax.axis_index("c")


def _all_gather(shards, *, name, collective_id):
    n = len(shards)
    n_copies = 9

    def body(*refs):
        ins, outs = refs[:n], refs[n:2 * n]
        send_sems, recv_sems, local_sems = refs[2 * n:]
        x, y, c = _position()
        me, sibling = (x, y, c), (x, y, 1 - c)
        x_nbr, y_nbr, diagonal = (1 - x, y, c), (x, 1 - y, c), (1 - x, 1 - y, c)
        other = lambda pos: (pos[0], pos[1], 1 - c)

        barrier = pltpu.get_barrier_semaphore()
        for peer in (sibling, x_nbr, y_nbr):
            pl.semaphore_signal(barrier, inc=1, device_id=peer, device_id_type=MESH)
        pl.semaphore_wait(barrier, 3)

        def block(a, pos, half=None):
            ref = outs[a].at[4 * pos[0] + 2 * pos[1] + pos[2]]
            rows = ref.shape[0] // 2
            return ref if half is None else ref.at[pl.ds(half * rows, rows)]

        def copy(a, k, pos, to, half=None, src=None):
            return pltpu.make_async_remote_copy(
                src_ref=block(a, pos, half) if src is None else src, dst_ref=block(a, pos, half),
                send_sem=send_sems.at[n_copies * a + k], recv_sem=recv_sems.at[n_copies * a + k],
                device_id=to, device_id_type=MESH)

        started = []
        for a in range(n):
            mine = pltpu.make_async_copy(ins[a], block(a, me), local_sems.at[a])
            mine.start()
            started.append(mine)
        sends = []
        for a in range(n):
            sends += [copy(a, 1, me, x_nbr, src=ins[a]), copy(a, 2, me, y_nbr, src=ins[a]),
                      copy(a, 0, me, sibling, src=ins[a])]
        for cp in sends:
            cp.start()

        def pass_on(copies):
            for cp in copies:
                cp.start()
                sends.append(cp)

        for a in range(n):
            copy(a, 1, x_nbr, me).wait_recv()
            pass_on([copy(a, 5, x_nbr, y_nbr, half=0), copy(a, 3, x_nbr, sibling)])
            copy(a, 2, y_nbr, me).wait_recv()
            pass_on([copy(a, 6, y_nbr, x_nbr, half=1), copy(a, 4, y_nbr, sibling)])
        for a in range(n):
            copy(a, 5, diagonal, me, half=0).wait_recv()
            pass_on([copy(a, 7, diagonal, sibling, half=0)])
            copy(a, 6, diagonal, me, half=1).wait_recv()
            pass_on([copy(a, 8, diagonal, sibling, half=1)])
        for a in range(n):
            copy(a, 0, sibling, me).wait_recv()
            copy(a, 3, other(x_nbr), me).wait_recv()
            copy(a, 4, other(y_nbr), me).wait_recv()
            copy(a, 7, other(diagonal), me, half=0).wait_recv()
            copy(a, 8, other(diagonal), me, half=1).wait_recv()
        for cp in sends:
            cp.wait_send()
        for cp in started:
            cp.wait()

    return pl.kernel(
        body, name=name,
        out_type=[jax.ShapeDtypeStruct((N_DEV,) + s.shape, s.dtype) for s in shards],
        mesh=plsc.ScalarSubcoreMesh(axis_name="sequencer", num_cores=1),
        scratch_types=[pltpu.SemaphoreType.DMA((n_copies * n,)), pltpu.SemaphoreType.DMA((n_copies * n,)),
                       pltpu.SemaphoreType.DMA((n,))],
        compiler_params=pltpu.CompilerParams(collective_id=collective_id),
    )(*shards)


def _chip_sums(group, *, name):
    n = len(group)
    shapes = [g.shape[1:] for g in group]

    def body(*refs):
        g_refs, partials, out_refs = refs[:n], refs[n:3 * n:2], refs[n + 1:3 * n:2]
        mines, theirs = refs[3 * n:5 * n:2], refs[3 * n + 1:5 * n:2]
        send_sems, recv_sems, local_sems = refs[5 * n:]
        x, y, c = _position()
        my_chip = 2 * x + y

        def swap(a, s):
            return pltpu.make_async_remote_copy(
                src_ref=g_refs[a].at[2 * s + (1 - c)], dst_ref=theirs[a].at[s],
                send_sem=send_sems.at[4 * a + s], recv_sem=recv_sems.at[4 * a + s],
                device_id=(x, y, 1 - c), device_id_type=MESH)

        def load(a, s):
            return pltpu.make_async_copy(g_refs[a].at[2 * s + c], mines[a].at[s], local_sems.at[4 * a + s])

        for a in range(n):
            for s in range(4):
                swap(a, s).start()
                load(a, s).start()

        for a, (R, C) in enumerate(shapes):
            rc = 128 if R % 128 == 0 else R

            def chip_sum(chip, rows):
                return mines[a][chip, rows, :].astype(F32) + theirs[a][chip, rows, :].astype(F32)

            for s in range(4):
                load(a, s).wait()
                swap(a, s).wait_recv()

                @pl.when(s == my_chip)
                def _():
                    @pl.loop(0, R // rc)
                    def _(t):
                        rows = pl.ds(pl.multiple_of(t * rc, rc), rc)
                        out_refs[a][rows, :] = chip_sum(s, rows)

                @pl.when(s != my_chip)
                def _():
                    @pl.loop(0, R // rc)
                    def _(t):
                        rows = pl.ds(pl.multiple_of(t * rc, rc), rc)
                        partials[a][(s ^ my_chip) - 1, rows, :] = chip_sum(s, rows).astype(BF16)

        for a in range(n):
            for s in range(4):
                swap(a, s).wait_send()

    vmem = pl.BlockSpec(memory_space=pltpu.VMEM)
    outs = pl.pallas_call(
        body, name=name,
        in_specs=[pl.BlockSpec(memory_space=pl.ANY)] * n, out_specs=[vmem] * (2 * n),
        out_shape=[shape for R, C in shapes
                   for shape in (jax.ShapeDtypeStruct((3, R, C), BF16), jax.ShapeDtypeStruct((R, C), F32))],
        scratch_shapes=[pltpu.VMEM((4, R, C), BF16) for R, C in shapes for _ in range(2)] + [
            pltpu.SemaphoreType.DMA((4 * n,)), pltpu.SemaphoreType.DMA((4 * n,)), pltpu.SemaphoreType.DMA((4 * n,))],
        compiler_params=_params(),
    )(*group)
    return [(outs[2 * a], outs[2 * a + 1]) for a in range(n)]


def _cross_chips(partials, *, name, collective_id):
    n = len(partials)

    def body(*refs):
        ins, outs = refs[:n], refs[n:2 * n]
        send_sems, recv_sems = refs[2 * n:]
        x, y, c = _position()
        my_chip = 2 * x + y
        peers = [((my_chip ^ j) // 2, (my_chip ^ j) % 2, c) for j in (1, 2, 3)]

        barrier = pltpu.get_barrier_semaphore()
        for peer in peers:
            pl.semaphore_signal(barrier, inc=1, device_id=peer, device_id_type=MESH)
        pl.semaphore_wait(barrier, 3)

        copies = [
            pltpu.make_async_remote_copy(
                src_ref=ins[a].at[j], dst_ref=outs[a].at[j],
                send_sem=send_sems.at[3 * a + j], recv_sem=recv_sems.at[3 * a + j],
                device_id=peers[j], device_id_type=MESH)
            for a in range(n) for j in range(3)]
        for cp in copies:
            cp.start()
        for cp in copies:
            cp.wait_recv()
        for cp in copies:
            cp.wait_send()

    return pl.kernel(
        body, name=name,
        out_type=[jax.ShapeDtypeStruct(p.shape, p.dtype) for p in partials],
        mesh=plsc.ScalarSubcoreMesh(axis_name="sequencer", num_cores=1),
        scratch_types=[pltpu.SemaphoreType.DMA((3 * n,)), pltpu.SemaphoreType.DMA((3 * n,))],
        compiler_params=pltpu.CompilerParams(collective_id=collective_id),
    )(*partials)


def _cross_chips_and_gather(partial, slab, *, name, collective_id):
    def body(part_ref, slab_ref, landed_ref, slabs_ref, send_sems, recv_sems, local_sem):
        x, y, c = _position()
        me, my_chip = 4 * x + 2 * y + c, 2 * x + y
        others = [me ^ k for k in range(1, N_DEV)]
        ids = [(o // 4, (o // 2) % 2, o % 2) for o in others]

        barrier = pltpu.get_barrier_semaphore()
        for peer in ids:
            pl.semaphore_signal(barrier, inc=1, device_id=peer, device_id_type=MESH)
        pl.semaphore_wait(barrier, N_DEV - 1)

        mine = pltpu.make_async_copy(slab_ref, slabs_ref.at[me], local_sem)
        mine.start()
        sends = [
            pltpu.make_async_remote_copy(
                src_ref=part_ref.at[j], dst_ref=landed_ref.at[j], send_sem=send_sems.at[j], recv_sem=recv_sems.at[j],
                device_id=((my_chip ^ (j + 1)) // 2, (my_chip ^ (j + 1)) % 2, c), device_id_type=MESH)
            for j in range(3)]
        sends += [
            pltpu.make_async_remote_copy(
                src_ref=slab_ref, dst_ref=slabs_ref.at[me], send_sem=send_sems.at[3 + k], recv_sem=recv_sems.at[3 + k],
                device_id=ids[k], device_id_type=MESH)
            for k in range(N_DEV - 1)]
        arrivals = sends[:3] + [
            pltpu.make_async_remote_copy(
                src_ref=slab_ref, dst_ref=slabs_ref.at[others[k]], send_sem=send_sems.at[3 + k],
                recv_sem=recv_sems.at[3 + k], device_id=ids[k], device_id_type=MESH)
            for k in range(N_DEV - 1)]
        for cp in sends:
            cp.start()
        for cp in arrivals:
            cp.wait_recv()
        for cp in sends:
            cp.wait_send()
        mine.wait()

    n_sems = 3 + N_DEV - 1
    return pl.kernel(
        body, name=name,
        out_type=[jax.ShapeDtypeStruct(partial.shape, partial.dtype),
                  jax.ShapeDtypeStruct((N_DEV,) + slab.shape, slab.dtype)],
        mesh=plsc.ScalarSubcoreMesh(axis_name="sequencer", num_cores=1),
        scratch_types=[pltpu.SemaphoreType.DMA((n_sems,)), pltpu.SemaphoreType.DMA((n_sems,)), pltpu.SemaphoreType.DMA],
        compiler_params=pltpu.CompilerParams(collective_id=collective_id),
    )(partial, slab)


def _sum_devices(gathered, after, *, name):
    _, R, C = gathered.shape

    def body(in_ref, after_ref, out_ref):
        total = in_ref[0]
        for d in range(1, N_DEV):
            total = total + in_ref[d]
        out_ref[...] = total

    return pl.pallas_call(
        body, name=name, grid=(1,),
        in_specs=[pl.BlockSpec((N_DEV, R, C), lambda i: (0, 0, 0)), AFTER],
        out_specs=pl.BlockSpec((R, C), lambda i: (0, 0)),
        out_shape=jax.ShapeDtypeStruct((R, C), F32),
        compiler_params=_params(("arbitrary",)),
    )(gathered, _in_hbm(after))


def _owner_sum(own, landed, after, *, name):
    R, C = own.shape
    tr = _row_tile(R, C)

    def body(own_ref, landed_ref, after_ref, out_ref):
        total = own_ref[...]
        for j in range(3):
            total = total + landed_ref[j].astype(F32)
        out_ref[...] = total

    return pl.pallas_call(
        body, name=name, grid=(R // tr,),
        in_specs=[pl.BlockSpec((tr, C), lambda i: (i, 0)), pl.BlockSpec((3, tr, C), lambda i: (0, i, 0)), AFTER],
        out_specs=pl.BlockSpec((tr, C), lambda i: (i, 0)),
        out_shape=jax.ShapeDtypeStruct((R, C), F32),
        compiler_params=_params(("arbitrary",)),
    )(own, landed, _in_hbm(after))


def _local_step(x, target, norms, pool_w_group, pool_scale, wgu1, wd1, w_in, wbp, wba, w_out, wgu2, wd2, exchange):
    n1g, nmg, n2g, nfg = norms
    D = x.shape[1]
    gu1, hid1 = _ffn_up(x, n1g, wgu1, tm=1024, name="ffn1_up")
    h1 = _ffn_down(x, hid1, wd1, tm=512, name="ffn1_down")
    un, proj = _inproj_fwd(h1, nmg, w_in, tm=1024, name="inproj_fwd")
    p = _pool_fwd(proj, pool_w_group, pool_scale, name="pool_fwd")
    o, ltot = _attn_fwd(proj, name="attn_fwd")
    h2, m = _mix_fwd(h1, p, o, proj, wbp, wba, w_out, tm=512, name="mix_fwd")
    gu2, hid2 = _ffn_up(h2, n2g, wgu2, tm=1024, name="ffn2_up")
    h3 = _ffn_down(h2, hid2, wd2, tm=512, name="ffn2_down")
    dh3, df2, loss, d_nf = _loss_bwd(h3, target, nfg, tm=256, name="loss_bwd")

    dh2, d_n2, n2, dgu2 = _ffn_bwd(dh3, df2, h2, n2g, gu2, wgu2, wd2, df2, tm=512, name="ffn2_bwd")
    d_wd2 = _wgrad_down(hid2, df2, tk=WGRAD_TOKENS, name="ffn2_wgrad_down")
    d_wgu2 = _wgrad_gate_up(n2, dgu2, tk=WGRAD_TOKENS, name="ffn2_wgrad_gate_up")
    (g_wd2, g_wgu2), token = exchange("ffn2", [d_wd2.reshape(N_DEV, FF_SHARD_PAD, D), d_wgu2])

    dyp, dys, dp, do, dgl = _mix_bwd(dh2, p, o, proj, wbp, wba, w_out, token, tm=512, name="mix_bwd")
    d_wout = _wgrad_full(m, dh2, tk=WGRAD_TOKENS, name="wgrad_out")
    d_wbp = _wgrad_full(dyp, p, tk=WGRAD_TOKENS, name="wgrad_branch_pool")
    d_wba = _wgrad_full(dys, o, tk=WGRAD_TOKENS, name="wgrad_branch_attn")
    by_owner = lambda g: g.reshape(N_DEV, g.shape[0] // N_DEV, g.shape[1])
    (g_wbp, g_wba, g_wout), token = exchange("mix", [by_owner(d_wbp), by_owner(d_wba), by_owner(d_wout)])
    dxp, d_wgroup, d_scale = _pool_bwd(dp, proj, pool_w_group, pool_scale, name="pool_bwd")
    dq, dk, dv = _attn_bwd(proj, do, ltot, token, name="attn_bwd")
    dproj = jnp.concatenate([dxp, dq, dk, dv, dgl], axis=1)
    dh1, df1, d_nm = _inproj_bwd(dproj, dh2, h1, nmg, w_in, tm=512, name="inproj_bwd")
    d_win = _wgrad_in(dproj, un, tk=WGRAD_TOKENS, name="wgrad_in")
    d_wd1 = _wgrad_down(hid1, df1, tk=WGRAD_TOKENS, name="ffn1_wgrad_down")
    (g_win, g_wd1), token = exchange("w_in_ffn1_down", [d_win, d_wd1.reshape(N_DEV, FF_SHARD_PAD, D)])

    dx, d_n1, n1, dgu1 = _ffn_bwd(dh1, df1, x, n1g, gu1, wgu1, wd1, token, tm=512, name="ffn1_bwd")
    d_wgu1_a = _wgrad_gate_up(n1, dgu1, tk=WGRAD_TOKENS, name="ffn1_wgrad_gate_up_a", part=0, parts=2)
    (g_wgu1_a, replicated), token = exchange(
        "with_replicated", [d_wgu1_a, d_n1, d_nm, d_n2, d_nf, d_scale, d_wgroup, loss])
    d_wgu1_b = _wgrad_gate_up(n1, dgu1, tk=WGRAD_TOKENS, name="ffn1_wgrad_gate_up_b", part=1, parts=2)
    (g_wgu1_b,), token = exchange("last", [d_wgu1_b])
    g_wgu1 = (g_wgu1_a, g_wgu1_b)

    sharded = (g_wgu1, g_wd1, g_win, g_wbp, g_wba, g_wout, g_wgu2, g_wd2)
    return dx, sharded, replicated, token


def _hidden_major(w):
    return jnp.swapaxes(w[0], 0, 1)


def _pad_gate_up(wt):
    d = wt.shape[1]
    wt = wt.astype(BF16).reshape(2, FF_SHARD, d)
    return jnp.pad(wt, ((0, 0), (0, FF_SHARD_PAD - FF_SHARD), (0, 0))).reshape(2 * FF_SHARD_PAD, d)


def _unpad_gate_up(gt):
    d = gt.shape[1]
    return gt.reshape(2, FF_SHARD_PAD, d)[:, :FF_SHARD].reshape(2 * FF_SHARD, d)


def _pad_down(w):
    return jnp.pad(w.astype(BF16), ((0, FF_SHARD_PAD - FF_SHARD), (0, 0)))


def kernel(x, ffn1_norm, ffn1_w_gate_up, ffn1_w_down, mix_norm, w_in, pool_w_group, pool_scale, w_branch_pool, w_branch_attn, w_out, ffn2_norm, ffn2_w_gate_up, ffn2_w_down, final_norm, loss_target, m_ffn1_norm, m_ffn1_w_gate_up, m_ffn1_w_down, m_mix_norm, m_w_in, m_pool_w_group, m_pool_scale, m_w_branch_pool, m_w_branch_attn, m_w_out, m_ffn2_norm, m_ffn2_w_gate_up, m_ffn2_w_down, m_final_norm, v_ffn1_norm, v_ffn1_w_gate_up, v_ffn1_w_down, v_mix_norm, v_w_in, v_pool_w_group, v_pool_scale, v_w_branch_pool, v_w_branch_attn, v_w_out, v_ffn2_norm, v_ffn2_w_gate_up, v_ffn2_w_down, v_final_norm):
    D = x.shape[-1]
    weights = dict(ffn1_norm=ffn1_norm, ffn1_w_gate_up=ffn1_w_gate_up, ffn1_w_down=ffn1_w_down, mix_norm=mix_norm,
                   w_in=w_in, pool_w_group=pool_w_group, pool_scale=pool_scale, w_branch_pool=w_branch_pool,
                   w_branch_attn=w_branch_attn, w_out=w_out, ffn2_norm=ffn2_norm, ffn2_w_gate_up=ffn2_w_gate_up,
                   ffn2_w_down=ffn2_w_down, final_norm=final_norm)
    first = dict(ffn1_norm=m_ffn1_norm, ffn1_w_gate_up=m_ffn1_w_gate_up, ffn1_w_down=m_ffn1_w_down,
                 mix_norm=m_mix_norm, w_in=m_w_in, pool_w_group=m_pool_w_group, pool_scale=m_pool_scale,
                 w_branch_pool=m_w_branch_pool, w_branch_attn=m_w_branch_attn, w_out=m_w_out,
                 ffn2_norm=m_ffn2_norm, ffn2_w_gate_up=m_ffn2_w_gate_up, ffn2_w_down=m_ffn2_w_down,
                 final_norm=m_final_norm)
    second = dict(ffn1_norm=v_ffn1_norm, ffn1_w_gate_up=v_ffn1_w_gate_up, ffn1_w_down=v_ffn1_w_down,
                  mix_norm=v_mix_norm, w_in=v_w_in, pool_w_group=v_pool_w_group, pool_scale=v_pool_scale,
                  w_branch_pool=v_w_branch_pool, w_branch_attn=v_w_branch_attn, w_out=v_w_out,
                  ffn2_norm=v_ffn2_norm, ffn2_w_gate_up=v_ffn2_w_gate_up, ffn2_w_down=v_ffn2_w_down,
                  final_norm=v_final_norm)
    order = list(weights)

    wgu1, = _all_gather([_pad_gate_up(_hidden_major(ffn1_w_gate_up))], name="all_gather_ffn1_gate_up", collective_id=0)
    wd1, = _all_gather([_pad_down(ffn1_w_down[0])], name="all_gather_ffn1_down", collective_id=10)
    transposed = lambda w: jnp.swapaxes(w[0], 0, 1).astype(BF16)
    win_g, = _all_gather([transposed(w_in)], name="all_gather_w_in", collective_id=1)
    wbp_g, wba_g = _all_gather([transposed(w_branch_pool), transposed(w_branch_attn)],
                               name="all_gather_branches", collective_id=2)
    wout_g, = _all_gather([w_out[0].astype(BF16)], name="all_gather_w_out", collective_id=11)
    wgu2, wd2 = _all_gather([_pad_gate_up(_hidden_major(ffn2_w_gate_up)), _pad_down(ffn2_w_down[0])],
                            name="all_gather_ffn2", collective_id=3)
    whole = lambda g: g.reshape(g.shape[0] * g.shape[1], g.shape[2])
    wd1, wd2, win_g, wbp_g, wba_g, wout_g = (whole(g) for g in (wd1, wd2, win_g, wbp_g, wba_g, wout_g))

    cross_ids = {"ffn2": 4, "mix": 5, "w_in_ffn1_down": 6, "with_replicated": 8, "last": 7}
    small = ["ffn1_norm", "mix_norm", "ffn2_norm", "final_norm", "pool_scale", "pool_w_group"]

    def tile_rows(a):
        a = a.reshape(-1, 128)
        return jnp.pad(a, ((0, -a.shape[0] % 8), (0, 0)))

    def exchange(tag, group):
        if tag == "with_replicated":
            slab = jnp.concatenate([tile_rows(g) for g in group[1:-1]] + [jnp.broadcast_to(group[-1], (8, 128))], axis=0)
            (partial, own), = _chip_sums([group[0]], name="chip_sums_" + tag)
            landed, slabs = _cross_chips_and_gather(partial, slab, name="cross_chips_" + tag, collective_id=cross_ids[tag])
            return [(own, landed), slabs], own
        sums = _chip_sums(group, name="chip_sums_" + tag)
        landed = _cross_chips([s[0] for s in sums], name="cross_chips_" + tag, collective_id=cross_ids[tag])
        return [(s[1], l) for s, l in zip(sums, landed)], sums[-1][1]

    norms = (ffn1_norm, mix_norm, ffn2_norm, final_norm.reshape(1, D))
    dx, sharded, slabs, last = _local_step(
        x[0], loss_target[0], norms, pool_w_group[0], pool_scale, wgu1, wd1, win_g, wbp_g, wba_g, wout_g, wgu2, wd2,
        exchange)
    names = ["ffn1_w_gate_up", "ffn1_w_down", "w_in", "w_branch_pool", "w_branch_attn", "w_out",
             "ffn2_w_gate_up", "ffn2_w_down"]
    handles = dict(zip(names, sharded))
    grads, delta, new_m, new_v = {}, {}, {}, {}
    after = last
    for k in ("ffn2_w_down", "ffn2_w_gate_up", "w_branch_pool", "w_branch_attn", "w_out", "w_in", "ffn1_w_down",
              "ffn1_w_gate_up"):
        hidden_major = k.endswith("w_gate_up")
        if isinstance(handles[k][0], tuple):
            first_half = _owner_sum(*handles[k][0], after, name="owner_sum_" + k + "_a")
            second_half = _owner_sum(*handles[k][1], first_half, name="owner_sum_" + k + "_b")
            g = jnp.concatenate([first_half[:FF_SHARD], second_half[:FF_SHARD]], axis=0)
        else:
            g = _owner_sum(*handles[k], after, name="owner_sum_" + k)
            if hidden_major:
                g = _unpad_gate_up(g)
            elif k in ("w_in", "w_branch_pool", "w_branch_attn"):
                g = jnp.swapaxes(g, 0, 1)
            else:
                g = g[:weights[k].shape[1]]
        view = _hidden_major if hidden_major else (lambda a: a[0])
        back = (lambda a: jnp.swapaxes(a, 0, 1)[None]) if hidden_major else (lambda a: a[None])
        out = _adamw(view(weights[k]), g, view(first[k]), view(second[k]), name="adamw_" + k)
        after = out[0]
        grads[k] = back(g)
        delta[k], new_m[k], new_v[k] = (back(a) for a in out)

    rows = [weights[k].size // 128 for k in small]
    padded_rows = [-(-r // 8) * 8 for r in rows]
    starts = [sum(padded_rows[:i]) for i in range(len(rows) + 1)]
    total = _sum_devices(slabs, after, name="sum_replicated")
    loss_out = total[starts[-1], 0]
    small_w = jnp.concatenate([tile_rows(weights[k]) for k in small], axis=0)
    small_m = jnp.concatenate([tile_rows(first[k]) for k in small], axis=0)
    small_v = jnp.concatenate([tile_rows(second[k]) for k in small], axis=0)
    small_out = _adamw(small_w, total[:starts[-1]], small_m, small_v, name="adamw_replicated")
    for name_, start, n_rows in zip(small, starts, rows):
        shape = weights[name_].shape
        grads[name_] = total[start:start + n_rows].reshape(shape)
        delta[name_], new_m[name_], new_v[name_] = (a[start:start + n_rows].reshape(shape) for a in small_out)

    return (loss_out, dx[None], *[grads[k] for k in order], *[delta[k] for k in order],
            *[new_m[k] for k in order], *[new_v[k] for k in order])
```

```python
import jax
import jax.numpy as jnp
from jax import lax
from jax.experimental import pallas as pl
from jax.experimental.pallas import tpu as pltpu
from jax.experimental.pallas import tpu_sc as plsc

F32 = jnp.float32
BF16 = jnp.bfloat16
MESH = pl.DeviceIdType.MESH

RMS_EPS = 1e-6
N_DEV = 8
N_HEADS = 8
HEAD_DIM = 64
HEAD_PAIR = 2 * HEAD_DIM
POOL_WINDOWS = (2, 4, 8, 16)
POOL_GROUP = 128
POOL_WIDTH = 512
SB_WIDTH = 512
FF_SHARD = 352
FF_SHARD_PAD = 384
ATTN_K_BLOCK = 256
ATTN_Q_BLOCK_FWD = 512
ATTN_Q_BLOCK_BWD = 256
ATTN_SCALE = 0.125

ADAM_LR = 0.001
ADAM_B1 = 0.9
ADAM_B2 = 0.999
ADAM_EPS = 1e-08
ADAM_WD = 0.01
ADAM_STEP = 10

VMEM_LIMIT = 48 << 20
WGRAD_TOKENS = 2048


def _params(dims=None):
    return pltpu.CompilerParams(dimension_semantics=dims, vmem_limit_bytes=VMEM_LIMIT)


def _mm(a, b):
    return jnp.dot(a, b, preferred_element_type=F32)


def _mm_nt(a, b):
    return lax.dot_general(a, b, (((1,), (1,)), ((), ())), preferred_element_type=F32)


def _mm_tn(a, b):
    return lax.dot_general(a, b, (((0,), (0,)), ((), ())), preferred_element_type=F32)


def _row_tile(rows, cols):
    limit = max(8, (512 * 1024) // cols)
    return max(t for t in range(8, rows + 1, 8) if rows % t == 0 and (t <= limit or t == 8))


def _rstd(xf):
    return lax.rsqrt(jnp.mean(xf * xf, axis=-1, keepdims=True) + RMS_EPS)


def _rms_bwd(xf, gain, dn):
    r = _rstd(xf)
    xh = xf * r
    dgain = jnp.sum(dn * xh, axis=0, keepdims=True)
    dxh = dn * gain
    dx = r * (dxh - xh * jnp.mean(dxh * xh, axis=-1, keepdims=True))
    return dx, dgain


def _ffn_up(x, gain, wgu, *, tm, name):
    T, D = x.shape
    tm = min(tm, T)
    nb, bw = wgu.shape[0] // 2, wgu.shape[1]

    def body(x_ref, gain_ref, wg_ref, wu_ref, gu_ref, hid_ref, n_scr):
        @pl.when(pl.program_id(1) == 0)
        def _():
            xf = x_ref[...]
            n_scr[...] = (xf * _rstd(xf) * gain_ref[...]).astype(BF16)

        halves = (pl.ds(0, tm // 2), pl.ds(tm // 2, tm // 2))
        wg, wu = wg_ref[...], wu_ref[...]
        gus = [(_mm_nt(n_scr[rows, :], wg), _mm_nt(n_scr[rows, :], wu)) for rows in halves]
        for rows, (g, u) in zip(halves, gus):
            gu_ref[0, rows, :] = g.astype(BF16)
            gu_ref[1, rows, :] = u.astype(BF16)
            hid_ref[rows, :] = (g * jax.nn.sigmoid(g) * u).astype(BF16)

    return pl.pallas_call(
        body, name=name, grid=(T // tm, nb),
        in_specs=[
            pl.BlockSpec((tm, D), lambda i, j: (i, 0)),
            pl.BlockSpec((1, D), lambda i, j: (0, 0)),
            pl.BlockSpec((None, bw, D), lambda i, j: (j, 0, 0)),
            pl.BlockSpec((None, bw, D), lambda i, j: (j + nb, 0, 0)),
        ],
        out_specs=[
            pl.BlockSpec((2, tm, bw), lambda i, j: (0, i, j)),
            pl.BlockSpec((tm, bw), lambda i, j: (i, j)),
        ],
        out_shape=[jax.ShapeDtypeStruct((2, T, nb * bw), BF16), jax.ShapeDtypeStruct((T, nb * bw), BF16)],
        scratch_shapes=[pltpu.VMEM((tm, D), BF16)],
        compiler_params=_params(("arbitrary", "arbitrary")),
    )(x, gain, wgu, wgu)


def _ffn_down(x, hid, wd, *, tm, name):
    T, D = x.shape
    tm = min(tm, T)
    F = hid.shape[1]

    def body(x_ref, hid_ref, wd_ref, h_ref):
        h_ref[...] = x_ref[...] + 0.5 * _mm(hid_ref[...], wd_ref[...])

    return pl.pallas_call(
        body, name=name, grid=(T // tm,),
        in_specs=[
            pl.BlockSpec((tm, D), lambda i: (i, 0)),
            pl.BlockSpec((tm, F), lambda i: (i, 0)),
            pl.BlockSpec((F, D), lambda i: (0, 0)),
        ],
        out_specs=pl.BlockSpec((tm, D), lambda i: (i, 0)),
        out_shape=jax.ShapeDtypeStruct((T, D), F32),
        compiler_params=_params(("arbitrary",)),
    )(x, hid, wd)


AFTER = pl.BlockSpec(memory_space=pltpu.HBM)


def _in_hbm(token):
    return pltpu.with_memory_space_constraint(token, pltpu.HBM)


def _ffn_bwd(dh, df, x, gain, gu, wgu, wd, after, *, tm, name):
    T, D = x.shape
    tm = min(tm, T)
    nb, bw = wgu.shape[0] // 2, wgu.shape[1]

    def body(dh_ref, df_ref, x_ref, gain_ref, gu_ref, wg_ref, wu_ref, wd_ref, after_ref,
             dx_ref, dgain_ref, n_ref, dgu_ref, dn_acc):
        i, j = pl.program_id(0), pl.program_id(1)

        @pl.when(j == 0)
        def _():
            xf = x_ref[...]
            n_ref[...] = (xf * _rstd(xf) * gain_ref[...]).astype(BF16)
            dn_acc[...] = jnp.zeros_like(dn_acc)

        @pl.when((i == 0) & (j == 0))
        def _():
            dgain_ref[...] = jnp.zeros_like(dgain_ref)

        halves = (pl.ds(0, tm // 2), pl.ds(tm // 2, tm // 2))
        wd, wg, wu = wd_ref[...], wg_ref[...], wu_ref[...]
        dhids = [_mm_nt(df_ref[rows, :], wd) for rows in halves]
        for rows, dhid in zip(halves, dhids):
            g = gu_ref[0, rows, :].astype(F32)
            u = gu_ref[1, rows, :].astype(F32)
            s = jax.nn.sigmoid(g)
            silu = g * s
            dg = (dhid * u * (s * (1.0 + g * (1.0 - s)))).astype(BF16)
            du = (dhid * silu).astype(BF16)
            dgu_ref[0, rows, :] = dg
            dgu_ref[1, rows, :] = du
            dn_acc[rows, :] += _mm(dg, wg) + _mm(du, wu)

        @pl.when(j == nb - 1)
        def _():
            dx, dgain = _rms_bwd(x_ref[...], gain_ref[...], dn_acc[...])
            dx_ref[...] = dh_ref[...] + dx
            dgain_ref[...] += dgain

    row = lambda i, j: (i, 0)
    return pl.pallas_call(
        body, name=name, grid=(T // tm, nb),
        in_specs=[
            pl.BlockSpec((tm, D), row),
            pl.BlockSpec((tm, D), row),
            pl.BlockSpec((tm, D), row),
            pl.BlockSpec((1, D), lambda i, j: (0, 0)),
            pl.BlockSpec((2, tm, bw), lambda i, j: (0, i, j)),
            pl.BlockSpec((None, bw, D), lambda i, j: (j, 0, 0)),
            pl.BlockSpec((None, bw, D), lambda i, j: (j + nb, 0, 0)),
            pl.BlockSpec((bw, D), lambda i, j: (j, 0)),
            AFTER,
        ],
        out_specs=[
            pl.BlockSpec((tm, D), row),
            pl.BlockSpec((1, D), lambda i, j: (0, 0)),
            pl.BlockSpec((tm, D), row),
            pl.BlockSpec((2, tm, bw), lambda i, j: (0, i, j)),
        ],
        out_shape=[
            jax.ShapeDtypeStruct((T, D), F32),
            jax.ShapeDtypeStruct((1, D), F32),
            jax.ShapeDtypeStruct((T, D), BF16),
            jax.ShapeDtypeStruct((2, T, nb * bw), BF16),
        ],
        scratch_shapes=[pltpu.VMEM((tm, D), F32)],
        compiler_params=_params(("arbitrary", "arbitrary")),
    )(dh, df, x, gain, gu, wgu, wgu, wd, _in_hbm(after))


def _wgrad(a, b, *, grid, a_spec, b_spec, out_spec, out_shape, acc_shape, name):
    nk = grid[2]

    def body(a_ref, b_ref, o_ref, acc):
        k = pl.program_id(2)

        @pl.when(k == 0)
        def _():
            acc[...] = jnp.zeros_like(acc)

        acc[...] += _mm_tn(a_ref[...].astype(BF16), b_ref[...].astype(BF16))

        @pl.when(k == nk - 1)
        def _():
            o_ref[...] = acc[...].astype(o_ref.dtype)

    return pl.pallas_call(
        body, name=name, grid=grid, in_specs=[a_spec, b_spec], out_specs=out_spec,
        out_shape=jax.ShapeDtypeStruct(out_shape, BF16),
        scratch_shapes=[pltpu.VMEM(acc_shape, F32)],
        compiler_params=_params(("arbitrary", "arbitrary", "arbitrary")),
    )(a, b)


def _wgrad_gate_up(n, dgu, *, tk, name, part=0, parts=1):
    T, D = n.shape
    tk = min(tk, T)
    owner_rows = FF_SHARD_PAD * 2
    nb = dgu.shape[2] // owner_rows
    bw = owner_rows // parts
    return _wgrad(
        dgu, n, grid=(2 * nb, 1, T // tk), name=name,
        a_spec=pl.BlockSpec((None, tk, bw), lambda m, c, k: (m // nb, k, parts * (m % nb) + part)),
        b_spec=pl.BlockSpec((tk, D), lambda m, c, k: (k, 0)),
        out_spec=pl.BlockSpec((None, bw, D), lambda m, c, k: (m, 0, 0)),
        out_shape=(2 * nb, bw, D), acc_shape=(bw, D))


def _wgrad_down(hid, df, *, tk, name):
    T, D = df.shape
    tk = min(tk, T)
    bw = FF_SHARD_PAD * 2
    nb = hid.shape[1] // bw
    return _wgrad(
        hid, df, grid=(nb, 1, T // tk), name=name,
        a_spec=pl.BlockSpec((tk, bw), lambda m, c, k: (k, m)),
        b_spec=pl.BlockSpec((tk, D), lambda m, c, k: (k, 0)),
        out_spec=pl.BlockSpec((bw, D), lambda m, c, k: (m, 0)),
        out_shape=(nb * bw, D), acc_shape=(bw, D))


def _wgrad_in(dproj, un, *, tk, name):
    T, D = un.shape
    tk = min(tk, T)
    bw = dproj.shape[1] // N_DEV
    return _wgrad(
        dproj, un, grid=(N_DEV, 1, T // tk), name=name,
        a_spec=pl.BlockSpec((tk, bw), lambda m, c, k: (k, m)),
        b_spec=pl.BlockSpec((tk, D), lambda m, c, k: (k, 0)),
        out_spec=pl.BlockSpec((None, bw, D), lambda m, c, k: (m, 0, 0)),
        out_shape=(N_DEV, bw, D), acc_shape=(bw, D))


def _wgrad_full(a, b, *, tk, name):
    T, M = a.shape
    tk = min(tk, T)
    N = b.shape[1]
    return _wgrad(
        a, b, grid=(1, 1, T // tk), name=name,
        a_spec=pl.BlockSpec((tk, M), lambda m, c, k: (k, 0)),
        b_spec=pl.BlockSpec((tk, N), lambda m, c, k: (k, 0)),
        out_spec=pl.BlockSpec((M, N), lambda m, c, k: (0, 0)), out_shape=(M, N), acc_shape=(M, N))


def _loss_bwd(h, target, gain, *, tm, name):
    T, D = h.shape
    tm = min(tm, T)

    def body(h_ref, t_ref, gain_ref, dh_ref, df_ref, loss_ref, dgain_ref):
        @pl.when(pl.program_id(0) == 0)
        def _():
            loss_ref[...] = jnp.zeros_like(loss_ref)
            dgain_ref[...] = jnp.zeros_like(dgain_ref)

        xf = h_ref[...]
        gain = gain_ref[...]
        err = xf * _rstd(xf) * gain - t_ref[...]
        loss_ref[...] += 0.5 * jnp.sum(jnp.mean(err * err, axis=-1, keepdims=True), axis=0, keepdims=True)
        dx, dgain = _rms_bwd(xf, gain, err * (1.0 / D))
        dh_ref[...] = dx
        df_ref[...] = (0.5 * dx).astype(BF16)
        dgain_ref[...] += dgain

    row = lambda i: (i, 0)
    fixed = lambda i: (0, 0)
    return pl.pallas_call(
        body, name=name, grid=(T // tm,),
        in_specs=[pl.BlockSpec((tm, D), row), pl.BlockSpec((tm, D), row), pl.BlockSpec((1, D), fixed)],
        out_specs=[pl.BlockSpec((tm, D), row), pl.BlockSpec((tm, D), row), pl.BlockSpec((1, 128), fixed),
                   pl.BlockSpec((1, D), fixed)],
        out_shape=[jax.ShapeDtypeStruct((T, D), F32), jax.ShapeDtypeStruct((T, D), BF16),
                   jax.ShapeDtypeStruct((1, 128), F32), jax.ShapeDtypeStruct((1, D), F32)],
        compiler_params=_params(("arbitrary",)),
    )(h, target, gain)


def _inproj_fwd(h, gain, w_in_t, *, tm, name):
    T, D = h.shape
    tm = min(tm, T)
    bn = D
    nb = w_in_t.shape[0] // bn

    def body(h_ref, gain_ref, wt_ref, un_ref, proj_ref):
        @pl.when(pl.program_id(1) == 0)
        def _():
            xf = h_ref[...]
            un_ref[...] = (xf * _rstd(xf) * gain_ref[...]).astype(BF16)

        proj_ref[...] = _mm_nt(un_ref[...], wt_ref[...])

    return pl.pallas_call(
        body, name=name, grid=(T // tm, nb),
        in_specs=[
            pl.BlockSpec((tm, D), lambda i, j: (i, 0)),
            pl.BlockSpec((1, D), lambda i, j: (0, 0)),
            pl.BlockSpec((bn, D), lambda i, j: (j, 0)),
        ],
        out_specs=[pl.BlockSpec((tm, D), lambda i, j: (i, 0)), pl.BlockSpec((tm, bn), lambda i, j: (i, j))],
        out_shape=[jax.ShapeDtypeStruct((T, D), BF16), jax.ShapeDtypeStruct((T, nb * bn), F32)],
        compiler_params=_params(("arbitrary", "arbitrary")),
    )(h, gain, w_in_t)


def _inproj_bwd(dproj, dh, h, gain, w_in_t, *, tm, name):
    T, D = h.shape
    tm = min(tm, T)
    width = w_in_t.shape[0]

    def body(dp_ref, dh_ref, h_ref, gain_ref, wt_ref, dx_ref, df_ref, dgain_ref):
        @pl.when(pl.program_id(0) == 0)
        def _():
            dgain_ref[...] = jnp.zeros_like(dgain_ref)

        dx, dgain = _rms_bwd(h_ref[...], gain_ref[...], _mm(dp_ref[...], wt_ref[...]))
        dh_in = dh_ref[...] + dx
        dx_ref[...] = dh_in
        df_ref[...] = (0.5 * dh_in).astype(BF16)
        dgain_ref[...] += dgain

    row = lambda i: (i, 0)
    fixed = lambda i: (0, 0)
    return pl.pallas_call(
        body, name=name, grid=(T // tm,),
        in_specs=[
            pl.BlockSpec((tm, width), row),
            pl.BlockSpec((tm, D), row),
            pl.BlockSpec((tm, D), row),
            pl.BlockSpec((1, D), fixed),
            pl.BlockSpec((width, D), fixed),
        ],
        out_specs=[pl.BlockSpec((tm, D), row), pl.BlockSpec((tm, D), row), pl.BlockSpec((1, D), fixed)],
        out_shape=[jax.ShapeDtypeStruct((T, D), F32), jax.ShapeDtypeStruct((T, D), BF16),
                   jax.ShapeDtypeStruct((1, D), F32)],
        compiler_params=_params(("arbitrary",)),
    )(dproj, dh, h, gain, w_in_t)


def _window_sum(x, row, doublings, *, backward):
    T = x.shape[0]
    s = x
    for k in range(doublings):
        sh = 1 << k
        if backward:
            s = s + jnp.where(row < T - sh, pltpu.roll(s, T - sh, 0), 0.0)
        else:
            s = s + jnp.where(row >= sh, pltpu.roll(s, sh, 0), 0.0)
    return s


def _pool_fwd(proj, w_group, scale, *, name):
    T = proj.shape[0]

    def body(xp_ref, w_ref, scale_ref, p_ref):
        row = lax.broadcasted_iota(jnp.int32, (T, POOL_GROUP), 0)
        for gi, window in enumerate(POOL_WINDOWS):
            cols = slice(gi * POOL_GROUP, (gi + 1) * POOL_GROUP)
            x = xp_ref[:, cols]
            inv_count = 1.0 / jnp.minimum(row + 1, window).astype(F32)
            yc = _window_sum(x, row, gi + 1, backward=False) * inv_count - x
            pre = _mm(yc.astype(BF16), w_ref[gi].astype(BF16))
            p_ref[:, cols] = pre * scale_ref[:, cols]

    return pl.pallas_call(
        body, name=name, grid=(1,),
        in_specs=[
            pl.BlockSpec((T, POOL_WIDTH), lambda i: (0, 0)),
            pl.BlockSpec(w_group.shape, lambda i: (0, 0, 0)),
            pl.BlockSpec((1, POOL_WIDTH), lambda i: (0, 0)),
        ],
        out_specs=pl.BlockSpec((T, POOL_WIDTH), lambda i: (0, 0)),
        out_shape=jax.ShapeDtypeStruct((T, POOL_WIDTH), F32),
        compiler_params=_params(("arbitrary",)),
    )(proj, w_group, scale)


def _pool_bwd(dp, proj, w_group, scale, *, name):
    T = proj.shape[0]

    def body(dp_ref, xp_ref, w_ref, scale_ref, dxp_ref, dw_ref, dscale_ref):
        row = lax.broadcasted_iota(jnp.int32, (T, POOL_GROUP), 0)
        for gi, window in enumerate(POOL_WINDOWS):
            cols = slice(gi * POOL_GROUP, (gi + 1) * POOL_GROUP)
            x = xp_ref[:, cols]
            inv_count = 1.0 / jnp.minimum(row + 1, window).astype(F32)
            yc = (_window_sum(x, row, gi + 1, backward=False) * inv_count - x).astype(BF16)
            w = w_ref[gi].astype(BF16)
            pre = _mm(yc, w)
            dpg = dp_ref[:, cols]
            dscale_ref[:, cols] = jnp.sum(dpg * pre, axis=0, keepdims=True)
            dpre = (dpg * scale_ref[:, cols]).astype(BF16)
            dw_ref[gi] = _mm_tn(yc, dpre)
            dyc = _mm_nt(dpre, w)
            dxp_ref[:, cols] = (_window_sum(dyc * inv_count, row, gi + 1, backward=True) - dyc).astype(BF16)

    return pl.pallas_call(
        body, name=name, grid=(1,),
        in_specs=[
            pl.BlockSpec((T, POOL_WIDTH), lambda i: (0, 0)),
            pl.BlockSpec((T, POOL_WIDTH), lambda i: (0, 0)),
            pl.BlockSpec(w_group.shape, lambda i: (0, 0, 0)),
            pl.BlockSpec((1, POOL_WIDTH), lambda i: (0, 0)),
        ],
        out_specs=[
            pl.BlockSpec((T, POOL_WIDTH), lambda i: (0, 0)),
            pl.BlockSpec(w_group.shape, lambda i: (0, 0, 0)),
            pl.BlockSpec((1, POOL_WIDTH), lambda i: (0, 0)),
        ],
        out_shape=[jax.ShapeDtypeStruct((T, POOL_WIDTH), BF16), jax.ShapeDtypeStruct(w_group.shape, F32),
                   jax.ShapeDtypeStruct((1, POOL_WIDTH), F32)],
        compiler_params=_params(("arbitrary",)),
    )(dp, proj, w_group, scale)


ATTN_STRIP = 32


def _log_sigmoids(z):
    lb = jnp.minimum(z, 0.0) - jnp.log(1.0 + jnp.exp(-jnp.abs(z)))
    return lb, lb - z


def _transposed_blocks(x_ref, blocks_scr, tq):
    for b in range(blocks_scr.shape[0]):
        blocks_scr[b] = x_ref[b * tq:(b + 1) * tq, :].T.astype(BF16)


def _split_bf16(x):
    hi = x.astype(BF16)
    return hi, (x - hi.astype(F32)).astype(BF16)


def _strips(n):
    return [slice(i, i + ATTN_STRIP) for i in range(0, n, ATTN_STRIP)]


def _rows(parts):
    return jnp.concatenate(parts, axis=0)


def _attn_specs(T, tq):
    q_col = POOL_WIDTH // HEAD_PAIR
    k_col = q_col + SB_WIDTH // HEAD_PAIR
    v_col = k_col + SB_WIDTH // HEAD_PAIR
    return [
        pl.BlockSpec((tq, HEAD_PAIR), lambda p, i: (i, q_col + p)),
        pl.BlockSpec((T, HEAD_PAIR), lambda p, i: (0, k_col + p)),
        pl.BlockSpec((T, HEAD_PAIR), lambda p, i: (0, v_col + p)),
    ]


def _attn_fwd(proj, *, name):
    T = proj.shape[0]
    tk = min(ATTN_K_BLOCK, T)
    tq = min(ATTN_Q_BLOCK_FWD, T)
    diagonal_blocks = tq // tk

    def body(q_ref, k_ref, v_ref, o_ref, lt_ref, kt_scr, vb_scr):
        qi = pl.program_id(1)

        @pl.when(qi == 0)
        def _():
            _transposed_blocks(k_ref, kt_scr, tk)
            vb_scr[...] = v_ref[...].astype(BF16)

        head0 = lax.broadcasted_iota(jnp.int32, (tq, HEAD_PAIR), 1) < HEAD_DIM
        q = q_ref[...] * ATTN_SCALE
        qs = (jnp.where(head0, q, 0.0).astype(BF16), jnp.where(head0, 0.0, q).astype(BF16))
        r = lax.broadcasted_iota(jnp.int32, (tq, tk), 0)
        c = lax.broadcasted_iota(jnp.int32, (tq, tk), 1)
        later = (r[:tk] > c[:tk]).astype(BF16)
        later2 = _rows([later, later])
        causal = lambda d: (lambda rows: c[rows] + d * tk < r[rows])
        strips = _strips(tq)

        def log_terms(z, valid):
            lbs, his, los, sums = [], [], [], []
            for rows in strips:
                lb, lm = _log_sigmoids(z[rows])
                if valid is not None:
                    lm = jnp.where(valid(rows), lm, 0.0)
                hi, lo = _split_bf16(lm)
                lbs.append(lb)
                his.append(hi)
                los.append(lo)
                sums.append(jnp.sum(lm, axis=1, keepdims=True))
            return lbs, jnp.concatenate([_rows(his), _rows(los)], axis=1), _rows(sums)

        def weights(lbs, run, after, valid):
            parts = []
            for rows, lb in zip(strips, lbs):
                a = jnp.exp(lb + run[rows] + after[rows])
                if valid is not None:
                    a = jnp.where(valid(rows), a, 0.0)
                parts.append(a.astype(BF16))
            return _rows(parts)

        def block(kj, carry, valid):
            kt = kt_scr[kj]
            vb = vb_scr[pl.ds(pl.multiple_of(kj * tk, tk), tk), :]
            run0, o0, run1, o1 = carry
            z0 = _mm(qs[0], kt)
            z1 = _mm(qs[1], kt)
            lbs0, split0, sums0 = log_terms(z0, valid)
            after0 = _mm(split0, later2)
            lbs1, split1, sums1 = log_terms(z1, valid)
            after1 = _mm(split1, later2)
            o0 = o0 + _mm(weights(lbs0, run0, after0, valid), vb)
            o1 = o1 + _mm(weights(lbs1, run1, after1, valid), vb)
            return run0 + sums0, o0, run1 + sums1, o1

        zero = (jnp.zeros((tq, 1), F32), jnp.zeros((tq, HEAD_PAIR), F32))
        first = diagonal_blocks * qi
        carry = zero + zero
        for d in reversed(range(diagonal_blocks)):
            carry = block(first + d, carry, causal(d))
        carry = lax.fori_loop(0, first, lambda it, cr: block(first - 1 - it, cr, None), carry)
        o_ref[...] = jnp.where(head0, carry[1], carry[3])
        lt_ref[...] = jnp.where(head0, carry[0], carry[2])

    out_spec = pl.BlockSpec((tq, HEAD_PAIR), lambda p, i: (i, p))
    return pl.pallas_call(
        body, name=name, grid=(N_HEADS // 2, T // tq),
        in_specs=_attn_specs(T, tq), out_specs=[out_spec, out_spec],
        out_shape=[jax.ShapeDtypeStruct((T, SB_WIDTH), F32), jax.ShapeDtypeStruct((T, SB_WIDTH), F32)],
        scratch_shapes=[pltpu.VMEM((T // tk, HEAD_PAIR, tk), BF16), pltpu.VMEM((T, HEAD_PAIR), BF16)],
        compiler_params=_params(("arbitrary", "arbitrary")),
    )(proj, proj, proj)


def _attn_bwd(proj, do, ltot, after, *, name):
    T = proj.shape[0]
    tk = min(ATTN_K_BLOCK, T)
    tq = min(ATTN_Q_BLOCK_BWD, T)
    diagonal_blocks = tq // tk

    def body(q_ref, k_ref, v_ref, do_ref, lt_ref, after_ref, dq_ref, dk_ref, dv_ref,
             kb_scr, kt_scr, vt_scr, dkt_ref, dvt_ref):
        qi = pl.program_id(1)

        @pl.when(qi == 0)
        def _():
            kb_scr[...] = k_ref[...].astype(BF16)
            _transposed_blocks(k_ref, kt_scr, tk)
            _transposed_blocks(v_ref, vt_scr, tk)
            dkt_ref[...] = jnp.zeros_like(dkt_ref)
            dvt_ref[...] = jnp.zeros_like(dvt_ref)

        head0 = lax.broadcasted_iota(jnp.int32, (tq, HEAD_PAIR), 1) < HEAD_DIM
        q, do_, lt = q_ref[...] * ATTN_SCALE, do_ref[...], lt_ref[...]
        qs = (jnp.where(head0, q, 0.0).astype(BF16), jnp.where(head0, 0.0, q).astype(BF16))
        q_heads = (jnp.where(head0, q, 0.0), jnp.where(head0, 0.0, q))
        do_heads = (jnp.where(head0, do_, 0.0), jnp.where(head0, 0.0, do_))
        dos = tuple(d.astype(BF16) for d in do_heads)
        qts = tuple(x.T.astype(BF16) for x in q_heads)
        dots = tuple(d.T.astype(BF16) for d in do_heads)
        lts = (jnp.max(jnp.where(head0, lt, -jnp.inf), axis=1, keepdims=True),
               jnp.max(jnp.where(head0, -jnp.inf, lt), axis=1, keepdims=True))
        r = lax.broadcasted_iota(jnp.int32, (tq, tk), 0)
        c = lax.broadcasted_iota(jnp.int32, (tq, tk), 1)
        upto = (r[:tk] <= c[:tk]).astype(BF16)
        before = (r[:tk] < c[:tk]).astype(BF16)
        upto2, before2 = _rows([upto, upto]), _rows([before, before])
        causal = lambda d: (lambda rows: c[rows] + d * tk < r[rows])
        strips = _strips(tq)

        def log_terms(z, valid):
            lbs, his, los, sums = [], [], [], []
            for rows in strips:
                lb, lm = _log_sigmoids(z[rows])
                if valid is not None:
                    lm = jnp.where(valid(rows), lm, 0.0)
                hi, lo = _split_bf16(lm)
                lbs.append(lb)
                his.append(hi)
                los.append(lo)
                sums.append(jnp.sum(lm, axis=1, keepdims=True))
            return lbs, jnp.concatenate([_rows(his), _rows(los)], axis=1), _rows(sums)

        def weights(lbs, rest, lm_upto, da, valid):
            a_parts, es, his, los, sums = [], [], [], [], []
            for rows, lb in zip(strips, lbs):
                a = jnp.exp(lb + (rest[rows] - lm_upto[rows]))
                if valid is not None:
                    a = jnp.where(valid(rows), a, 0.0)
                e = da[rows] * a
                hi, lo = _split_bf16(e)
                a_parts.append(a.astype(BF16))
                es.append(e)
                his.append(hi)
                los.append(lo)
                sums.append(jnp.sum(e, axis=1, keepdims=True))
            return _rows(a_parts), es, jnp.concatenate([_rows(his), _rows(los)], axis=1), _rows(sums)

        def score_grads(lbs, es, run_e, e_before, valid):
            parts = []
            for rows, lb, e in zip(strips, lbs, es):
                beta = jnp.exp(lb)
                dz = e * (1.0 - beta) - (run_e[rows] + e_before[rows]) * beta
                if valid is not None:
                    dz = jnp.where(valid(rows), dz, 0.0)
                parts.append(dz.astype(BF16))
            return _rows(parts)

        def block(kj, carry, valid):
            off = pl.multiple_of(kj * tk, tk)
            kb, kt, vt = kb_scr[pl.ds(off, tk), :], kt_scr[kj], vt_scr[kj]
            run_lm0, run_e0, dq0, run_lm1, run_e1, dq1 = carry
            z0, da0 = _mm(qs[0], kt), _mm(dos[0], vt)
            z1, da1 = _mm(qs[1], kt), _mm(dos[1], vt)
            lbs0, split0, lm_sums0 = log_terms(z0, valid)
            lm_upto0 = _mm(split0, upto2)
            lbs1, split1, lm_sums1 = log_terms(z1, valid)
            lm_upto1 = _mm(split1, upto2)
            a0, es0, split0, e_sums0 = weights(lbs0, lts[0] - run_lm0, lm_upto0, da0, valid)
            e_before0 = _mm(split0, before2)
            a1, es1, split1, e_sums1 = weights(lbs1, lts[1] - run_lm1, lm_upto1, da1, valid)
            e_before1 = _mm(split1, before2)
            dz0 = score_grads(lbs0, es0, run_e0, e_before0, valid)
            dkt_blk = _mm(qts[0], dz0)
            dvt_blk = _mm(dots[0], a0)
            dq0 = dq0 + _mm(dz0, kb)
            dz1 = score_grads(lbs1, es1, run_e1, e_before1, valid)
            dkt_ref[kj] += dkt_blk + _mm(qts[1], dz1)
            dvt_ref[kj] += dvt_blk + _mm(dots[1], a1)
            dq1 = dq1 + _mm(dz1, kb)
            return run_lm0 + lm_sums0, run_e0 + e_sums0, dq0, run_lm1 + lm_sums1, run_e1 + e_sums1, dq1

        zero = (jnp.zeros((tq, 1), F32), jnp.zeros((tq, 1), F32), jnp.zeros((tq, HEAD_PAIR), F32))
        first = diagonal_blocks * qi
        carry = lax.fori_loop(0, first, lambda kj, cr: block(kj, cr, None), zero + zero)
        for d in range(diagonal_blocks):
            carry = block(first + d, carry, causal(d))
        dq_ref[...] = (jnp.where(head0, carry[2], carry[5]) * ATTN_SCALE).astype(BF16)

        @pl.when(qi == T // tq - 1)
        def _():
            for b in range(T // tk):
                dk_ref[b * tk:(b + 1) * tk, :] = dkt_ref[b].T.astype(BF16)
                dv_ref[b * tk:(b + 1) * tk, :] = dvt_ref[b].T.astype(BF16)

    blk = pl.BlockSpec((tq, HEAD_PAIR), lambda p, i: (i, p))
    seq = pl.BlockSpec((T, HEAD_PAIR), lambda p, i: (0, p))
    transposed = pltpu.VMEM((T // tk, HEAD_PAIR, tk), F32)
    return pl.pallas_call(
        body, name=name, grid=(N_HEADS // 2, T // tq),
        in_specs=_attn_specs(T, tq) + [blk, blk, AFTER], out_specs=[blk, seq, seq],
        out_shape=[jax.ShapeDtypeStruct((T, SB_WIDTH), BF16)] * 3,
        scratch_shapes=[pltpu.VMEM((T, HEAD_PAIR), BF16), pltpu.VMEM((T // tk, HEAD_PAIR, tk), BF16),
                        pltpu.VMEM((T // tk, HEAD_PAIR, tk), BF16), transposed, transposed],
        compiler_params=_params(("arbitrary", "arbitrary")),
    )(proj, proj, proj, do, ltot, _in_hbm(after))


def _mix_specs(T, D, tm, wbp, w_out):
    gate_col = (POOL_WIDTH + 3 * SB_WIDTH) // D
    row = lambda i: (i, 0)
    return [
        pl.BlockSpec((tm, D), row),
        pl.BlockSpec((tm, POOL_WIDTH), row),
        pl.BlockSpec((tm, SB_WIDTH), row),
        pl.BlockSpec((tm, D), lambda i: (i, gate_col)),
        pl.BlockSpec((tm, D), lambda i: (i, gate_col + 1)),
        pl.BlockSpec(wbp.shape, lambda i: (0, 0)),
        pl.BlockSpec(wbp.shape, lambda i: (0, 0)),
        pl.BlockSpec(w_out.shape, lambda i: (0, 0)),
    ]


def _mix_fwd(h, p, o, proj, wbp, wba, w_out, *, tm, name):
    T, D = h.shape
    tm = min(tm, T)

    def body(h_ref, p_ref, o_ref, glp_ref, gls_ref, wbp_ref, wba_ref, wout_ref, hout_ref, m_ref):
        halves = (pl.ds(0, tm // 2), pl.ds(tm // 2, tm // 2))
        wbp, wba, wout = wbp_ref[...], wba_ref[...], wout_ref[...]
        branches = [(_mm_nt(p_ref[rows, :].astype(BF16), wbp), _mm_nt(o_ref[rows, :].astype(BF16), wba))
                    for rows in halves]
        for rows, (yp, ys) in zip(halves, branches):
            m = (jax.nn.sigmoid(glp_ref[rows, :]) * yp + jax.nn.sigmoid(gls_ref[rows, :]) * ys).astype(BF16)
            m_ref[rows, :] = m
            hout_ref[rows, :] = h_ref[rows, :] + _mm(m, wout)

    row = lambda i: (i, 0)
    return pl.pallas_call(
        body, name=name, grid=(T // tm,),
        in_specs=_mix_specs(T, D, tm, wbp, w_out),
        out_specs=[pl.BlockSpec((tm, D), row), pl.BlockSpec((tm, D), row)],
        out_shape=[jax.ShapeDtypeStruct((T, D), F32), jax.ShapeDtypeStruct((T, D), BF16)],
        compiler_params=_params(("arbitrary",)),
    )(h, p, o, proj, proj, wbp, wba, w_out)


def _mix_bwd(dh, p, o, proj, wbp, wba, w_out, after, *, tm, name):
    T, D = dh.shape
    tm = min(tm, T)

    def body(dh_ref, p_ref, o_ref, glp_ref, gls_ref, wbp_ref, wba_ref, wout_ref, after_ref,
             dyp_ref, dys_ref, dp_ref, do_ref, dgl_ref):
        halves = (pl.ds(0, tm // 2), pl.ds(tm // 2, tm // 2))
        wbp, wba, wout = wbp_ref[...], wba_ref[...], wout_ref[...]
        products = [(_mm_nt(dh_ref[rows, :].astype(BF16), wout), _mm_nt(p_ref[rows, :].astype(BF16), wbp),
                     _mm_nt(o_ref[rows, :].astype(BF16), wba)) for rows in halves]
        for rows, (dm, yp, ys) in zip(halves, products):
            gp = jax.nn.sigmoid(glp_ref[rows, :])
            gs = jax.nn.sigmoid(gls_ref[rows, :])
            dyp = (dm * gp).astype(BF16)
            dys = (dm * gs).astype(BF16)
            dyp_ref[rows, :] = dyp
            dys_ref[rows, :] = dys
            dgl_ref[rows, :D] = (dm * yp * gp * (1.0 - gp)).astype(BF16)
            dgl_ref[rows, D:] = (dm * ys * gs * (1.0 - gs)).astype(BF16)
            dp_ref[rows, :] = _mm(dyp, wbp)
            do_ref[rows, :] = _mm(dys, wba)

    row = lambda i: (i, 0)
    return pl.pallas_call(
        body, name=name, grid=(T // tm,),
        in_specs=_mix_specs(T, D, tm, wbp, w_out) + [AFTER],
        out_specs=[pl.BlockSpec((tm, D), row), pl.BlockSpec((tm, D), row), pl.BlockSpec((tm, POOL_WIDTH), row),
                   pl.BlockSpec((tm, SB_WIDTH), row), pl.BlockSpec((tm, 2 * D), row)],
        out_shape=[jax.ShapeDtypeStruct((T, D), BF16), jax.ShapeDtypeStruct((T, D), BF16),
                   jax.ShapeDtypeStruct((T, POOL_WIDTH), F32), jax.ShapeDtypeStruct((T, SB_WIDTH), F32),
                   jax.ShapeDtypeStruct((T, 2 * D), BF16)],
        compiler_params=_params(("arbitrary",)),
    )(dh, p, o, proj, proj, wbp, wba, w_out, _in_hbm(after))


def _adamw(w, g, m, v, *, name):
    R, C = w.shape
    tr = _row_tile(R, C)

    def body(w_ref, g_ref, m_ref, v_ref, d_ref, nm_ref, nv_ref):
        g_ = g_ref[...]
        m_ = ADAM_B1 * m_ref[...] + (1.0 - ADAM_B1) * g_
        v_ = ADAM_B2 * v_ref[...] + (1.0 - ADAM_B2) * (g_ * g_)
        m_hat = m_ / (1.0 - ADAM_B1 ** ADAM_STEP)
        v_hat = v_ / (1.0 - ADAM_B2 ** ADAM_STEP)
        d_ref[...] = -ADAM_LR * (m_hat / (jnp.sqrt(v_hat) + ADAM_EPS) + ADAM_WD * w_ref[...])
        nm_ref[...] = m_
        nv_ref[...] = v_

    spec = pl.BlockSpec((tr, C), lambda i: (i, 0))
    return pl.pallas_call(
        body, name=name, grid=(R // tr,), in_specs=[spec] * 4, out_specs=[spec] * 3,
        out_shape=[jax.ShapeDtypeStruct((R, C), F32)] * 3,
        compiler_params=_params(("arbitrary",)),
    )(w, g, m, v)


def _position():
    return lax.axis_index("x"), lax.axis_index("y"), lax.axis_index("c")


def _all_gather(shards, *, name, collective_id):
    n = len(shards)
    n_copies = 9

    def body(*refs):
        ins, outs = refs[:n], refs[n:2 * n]
        send_sems, recv_sems, local_sems = refs[2 * n:]
        x, y, c = _position()
        me, sibling = (x, y, c), (x, y, 1 - c)
        x_nbr, y_nbr, diagonal = (1 - x, y, c), (x, 1 - y, c), (1 - x, 1 - y, c)
        other = lambda pos: (pos[0], pos[1], 1 - c)

        barrier = pltpu.get_barrier_semaphore()
        for peer in (sibling, x_nbr, y_nbr):
            pl.semaphore_signal(barrier, inc=1, device_id=peer, device_id_type=MESH)
        pl.semaphore_wait(barrier, 3)

        def block(a, pos, half=None):
            ref = outs[a].at[4 * pos[0] + 2 * pos[1] + pos[2]]
            rows = ref.shape[0] // 2
            return ref if half is None else ref.at[pl.ds(half * rows, rows)]

        def copy(a, k, pos, to, half=None, src=None):
            return pltpu.make_async_remote_copy(
                src_ref=block(a, pos, half) if src is None else src, dst_ref=block(a, pos, half),
                send_sem=send_sems.at[n_copies * a + k], recv_sem=recv_sems.at[n_copies * a + k],
                device_id=to, device_id_type=MESH)

        started = []
        for a in range(n):
            mine = pltpu.make_async_copy(ins[a], block(a, me), local_sems.at[a])
            mine.start()
            started.append(mine)
        sends = []
        for a in range(n):
            sends += [copy(a, 1, me, x_nbr, src=ins[a]), copy(a, 2, me, y_nbr, src=ins[a]),
                      copy(a, 0, me, sibling, src=ins[a])]
        for cp in sends:
            cp.start()

        def pass_on(copies):
            for cp in copies:
                cp.start()
                sends.append(cp)

        for a in range(n):
            copy(a, 1, x_nbr, me).wait_recv()
            pass_on([copy(a, 5, x_nbr, y_nbr, half=0), copy(a, 3, x_nbr, sibling)])
            copy(a, 2, y_nbr, me).wait_recv()
            pass_on([copy(a, 6, y_nbr, x_nbr, half=1), copy(a, 4, y_nbr, sibling)])
        for a in range(n):
            copy(a, 5, diagonal, me, half=0).wait_recv()
            pass_on([copy(a, 7, diagonal, sibling, half=0)])
            copy(a, 6, diagonal, me, half=1).wait_recv()
            pass_on([copy(a, 8, diagonal, sibling, half=1)])
        for a in range(n):
            copy(a, 0, sibling, me).wait_recv()
            copy(a, 3, other(x_nbr), me).wait_recv()
            copy(a, 4, other(y_nbr), me).wait_recv()
            copy(a, 7, other(diagonal), me, half=0).wait_recv()
            copy(a, 8, other(diagonal), me, half=1).wait_recv()
        for cp in sends:
            cp.wait_send()
        for cp in started:
            cp.wait()

    return pl.kernel(
        body, name=name,
        out_type=[jax.ShapeDtypeStruct((N_DEV,) + s.shape, s.dtype) for s in shards],
        mesh=plsc.ScalarSubcoreMesh(axis_name="sequencer", num_cores=1),
        scratch_types=[pltpu.SemaphoreType.DMA((n_copies * n,)), pltpu.SemaphoreType.DMA((n_copies * n,)),
                       pltpu.SemaphoreType.DMA((n,))],
        compiler_params=pltpu.CompilerParams(collective_id=collective_id),
    )(*shards)


def _chip_sums(group, *, name):
    n = len(group)
    shapes = [g.shape[1:] for g in group]

    def body(*refs):
        g_refs, partials, out_refs = refs[:n], refs[n:3 * n:2], refs[n + 1:3 * n:2]
        mines, theirs = refs[3 * n:5 * n:2], refs[3 * n + 1:5 * n:2]
        send_sems, recv_sems, local_sems = refs[5 * n:]
        x, y, c = _position()
        my_chip = 2 * x + y

        def swap(a, s):
            return pltpu.make_async_remote_copy(
                src_ref=g_refs[a].at[2 * s + (1 - c)], dst_ref=theirs[a].at[s],
                send_sem=send_sems.at[4 * a + s], recv_sem=recv_sems.at[4 * a + s],
                device_id=(x, y, 1 - c), device_id_type=MESH)

        def load(a, s):
            return pltpu.make_async_copy(g_refs[a].at[2 * s + c], mines[a].at[s], local_sems.at[4 * a + s])

        for a in range(n):
            for s in range(4):
                swap(a, s).start()
                load(a, s).start()

        for a, (R, C) in enumerate(shapes):
            rc = 128 if R % 128 == 0 else R

            def chip_sum(chip, rows):
                return mines[a][chip, rows, :].astype(F32) + theirs[a][chip, rows, :].astype(F32)

            for s in range(4):
                load(a, s).wait()
                swap(a, s).wait_recv()

                @pl.when(s == my_chip)
                def _():
                    @pl.loop(0, R // rc)
                    def _(t):
                        rows = pl.ds(pl.multiple_of(t * rc, rc), rc)
                        out_refs[a][rows, :] = chip_sum(s, rows)

                @pl.when(s != my_chip)
                def _():
                    @pl.loop(0, R // rc)
                    def _(t):
                        rows = pl.ds(pl.multiple_of(t * rc, rc), rc)
                        partials[a][(s ^ my_chip) - 1, rows, :] = chip_sum(s, rows).astype(BF16)

        for a in range(n):
            for s in range(4):
                swap(a, s).wait_send()

    vmem = pl.BlockSpec(memory_space=pltpu.VMEM)
    outs = pl.pallas_call(
        body, name=name,
        in_specs=[pl.BlockSpec(memory_space=pl.ANY)] * n, out_specs=[vmem] * (2 * n),
        out_shape=[shape for R, C in shapes
                   for shape in (jax.ShapeDtypeStruct((3, R, C), BF16), jax.ShapeDtypeStruct((R, C), F32))],
        scratch_shapes=[pltpu.VMEM((4, R, C), BF16) for R, C in shapes for _ in range(2)] + [
            pltpu.SemaphoreType.DMA((4 * n,)), pltpu.SemaphoreType.DMA((4 * n,)), pltpu.SemaphoreType.DMA((4 * n,))],
        compiler_params=_params(),
    )(*group)
    return [(outs[2 * a], outs[2 * a + 1]) for a in range(n)]


def _cross_chips(partials, *, name, collective_id):
    n = len(partials)

    def body(*refs):
        ins, outs = refs[:n], refs[n:2 * n]
        send_sems, recv_sems = refs[2 * n:]
        x, y, c = _position()
        my_chip = 2 * x + y
        peers = [((my_chip ^ j) // 2, (my_chip ^ j) % 2, c) for j in (1, 2, 3)]

        barrier = pltpu.get_barrier_semaphore()
        for peer in peers:
            pl.semaphore_signal(barrier, inc=1, device_id=peer, device_id_type=MESH)
        pl.semaphore_wait(barrier, 3)

        copies = [
            pltpu.make_async_remote_copy(
                src_ref=ins[a].at[j], dst_ref=outs[a].at[j],
                send_sem=send_sems.at[3 * a + j], recv_sem=recv_sems.at[3 * a + j],
                device_id=peers[j], device_id_type=MESH)
            for a in range(n) for j in range(3)]
        for cp in copies:
            cp.start()
        for cp in copies:
            cp.wait_recv()
        for cp in copies:
            cp.wait_send()

    return pl.kernel(
        body, name=name,
        out_type=[jax.ShapeDtypeStruct(p.shape, p.dtype) for p in partials],
        mesh=plsc.ScalarSubcoreMesh(axis_name="sequencer", num_cores=1),
        scratch_types=[pltpu.SemaphoreType.DMA((3 * n,)), pltpu.SemaphoreType.DMA((3 * n,))],
        compiler_params=pltpu.CompilerParams(collective_id=collective_id),
    )(*partials)


def _cross_chips_and_gather(partials, slab, *, name, collective_id):
    n = len(partials)

    def body(*refs):
        part_refs, slab_ref = refs[:n], refs[n]
        landed_refs, slabs_ref = refs[n + 1:2 * n + 1], refs[2 * n + 1]
        send_sems, recv_sems, local_sem = refs[2 * n + 2:]
        x, y, c = _position()
        me, my_chip = 4 * x + 2 * y + c, 2 * x + y
        others = [me ^ k for k in range(1, N_DEV)]
        ids = [(o // 4, (o // 2) % 2, o % 2) for o in others]

        barrier = pltpu.get_barrier_semaphore()
        for peer in ids:
            pl.semaphore_signal(barrier, inc=1, device_id=peer, device_id_type=MESH)
        pl.semaphore_wait(barrier, N_DEV - 1)

        mine = pltpu.make_async_copy(slab_ref, slabs_ref.at[me], local_sem)
        mine.start()
        sends = [
            pltpu.make_async_remote_copy(
                src_ref=part_refs[a].at[j], dst_ref=landed_refs[a].at[j],
                send_sem=send_sems.at[3 * a + j], recv_sem=recv_sems.at[3 * a + j],
                device_id=((my_chip ^ (j + 1)) // 2, (my_chip ^ (j + 1)) % 2, c), device_id_type=MESH)
            for a in range(n) for j in range(3)]
        sends += [
            pltpu.make_async_remote_copy(
                src_ref=slab_ref, dst_ref=slabs_ref.at[me],
                send_sem=send_sems.at[3 * n + k], recv_sem=recv_sems.at[3 * n + k],
                device_id=ids[k], device_id_type=MESH)
            for k in range(N_DEV - 1)]
        arrivals = sends[:3 * n] + [
            pltpu.make_async_remote_copy(
                src_ref=slab_ref, dst_ref=slabs_ref.at[others[k]],
                send_sem=send_sems.at[3 * n + k], recv_sem=recv_sems.at[3 * n + k],
                device_id=ids[k], device_id_type=MESH)
            for k in range(N_DEV - 1)]
        for cp in sends:
            cp.start()
        for cp in arrivals:
            cp.wait_recv()
        for cp in sends:
            cp.wait_send()
        mine.wait()

    n_sems = 3 * n + N_DEV - 1
    outs = pl.kernel(
        body, name=name,
        out_type=[jax.ShapeDtypeStruct(p.shape, p.dtype) for p in partials]
                 + [jax.ShapeDtypeStruct((N_DEV,) + slab.shape, slab.dtype)],
        mesh=plsc.ScalarSubcoreMesh(axis_name="sequencer", num_cores=1),
        scratch_types=[pltpu.SemaphoreType.DMA((n_sems,)), pltpu.SemaphoreType.DMA((n_sems,)), pltpu.SemaphoreType.DMA],
        compiler_params=pltpu.CompilerParams(collective_id=collective_id),
    )(*partials, slab)
    return outs[:n], outs[n]


def _sum_devices(gathered, after, *, name):
    _, R, C = gathered.shape

    def body(in_ref, after_ref, out_ref):
        total = in_ref[0]
        for d in range(1, N_DEV):
            total = total + in_ref[d]
        out_ref[...] = total

    return pl.pallas_call(
        body, name=name, grid=(1,),
        in_specs=[pl.BlockSpec((N_DEV, R, C), lambda i: (0, 0, 0)), AFTER],
        out_specs=pl.BlockSpec((R, C), lambda i: (0, 0)),
        out_shape=jax.ShapeDtypeStruct((R, C), F32),
        compiler_params=_params(("arbitrary",)),
    )(gathered, _in_hbm(after))


def _owner_sum(own, landed, after, *, name):
    R, C = own.shape
    tr = _row_tile(R, C)

    def body(own_ref, landed_ref, after_ref, out_ref):
        total = own_ref[...]
        for j in range(3):
            total = total + landed_ref[j].astype(F32)
        out_ref[...] = total

    return pl.pallas_call(
        body, name=name, grid=(R // tr,),
        in_specs=[pl.BlockSpec((tr, C), lambda i: (i, 0)), pl.BlockSpec((3, tr, C), lambda i: (0, i, 0)), AFTER],
        out_specs=pl.BlockSpec((tr, C), lambda i: (i, 0)),
        out_shape=jax.ShapeDtypeStruct((R, C), F32),
        compiler_params=_params(("arbitrary",)),
    )(own, landed, _in_hbm(after))


def _local_step(x, target, norms, pool_w_group, pool_scale, wgu1, wd1, w_in, wbp, wba, w_out, wgu2, wd2, exchange):
    n1g, nmg, n2g, nfg = norms
    D = x.shape[1]
    gu1, hid1 = _ffn_up(x, n1g, wgu1, tm=1024, name="ffn1_up")
    h1 = _ffn_down(x, hid1, wd1, tm=512, name="ffn1_down")
    un, proj = _inproj_fwd(h1, nmg, w_in, tm=1024, name="inproj_fwd")
    p = _pool_fwd(proj, pool_w_group, pool_scale, name="pool_fwd")
    o, ltot = _attn_fwd(proj, name="attn_fwd")
    h2, m = _mix_fwd(h1, p, o, proj, wbp, wba, w_out, tm=512, name="mix_fwd")
    gu2, hid2 = _ffn_up(h2, n2g, wgu2, tm=1024, name="ffn2_up")
    h3 = _ffn_down(h2, hid2, wd2, tm=512, name="ffn2_down")
    dh3, df2, loss, d_nf = _loss_bwd(h3, target, nfg, tm=256, name="loss_bwd")

    dh2, d_n2, n2, dgu2 = _ffn_bwd(dh3, df2, h2, n2g, gu2, wgu2, wd2, df2, tm=512, name="ffn2_bwd")
    d_wd2 = _wgrad_down(hid2, df2, tk=WGRAD_TOKENS, name="ffn2_wgrad_down")
    d_wgu2 = _wgrad_gate_up(n2, dgu2, tk=WGRAD_TOKENS, name="ffn2_wgrad_gate_up")
    (g_wd2, g_wgu2), token = exchange("ffn2", [d_wd2.reshape(N_DEV, FF_SHARD_PAD, D), d_wgu2])

    dyp, dys, dp, do, dgl = _mix_bwd(dh2, p, o, proj, wbp, wba, w_out, token, tm=512, name="mix_bwd")
    d_wout = _wgrad_full(m, dh2, tk=WGRAD_TOKENS, name="wgrad_out")
    d_wbp = _wgrad_full(dyp, p, tk=WGRAD_TOKENS, name="wgrad_branch_pool")
    d_wba = _wgrad_full(dys, o, tk=WGRAD_TOKENS, name="wgrad_branch_attn")
    by_owner = lambda g: g.reshape(N_DEV, g.shape[0] // N_DEV, g.shape[1])
    (g_wbp, g_wba, g_wout), token = exchange("mix", [by_owner(d_wbp), by_owner(d_wba), by_owner(d_wout)])
    dxp, d_wgroup, d_scale = _pool_bwd(dp, proj, pool_w_group, pool_scale, name="pool_bwd")
    dq, dk, dv = _attn_bwd(proj, do, ltot, token, name="attn_bwd")
    dproj = jnp.concatenate([dxp, dq, dk, dv, dgl], axis=1)
    dh1, df1, d_nm = _inproj_bwd(dproj, dh2, h1, nmg, w_in, tm=512, name="inproj_bwd")
    d_win = _wgrad_in(dproj, un, tk=WGRAD_TOKENS, name="wgrad_in")
    d_wd1 = _wgrad_down(hid1, df1, tk=WGRAD_TOKENS, name="ffn1_wgrad_down")
    (g_win, g_wd1, replicated_early), token = exchange(
        "w_in_ffn1_down", [d_win, d_wd1.reshape(N_DEV, FF_SHARD_PAD, D), d_nm, d_n2, d_nf, d_scale, d_wgroup, loss])

    dx, d_n1, n1, dgu1 = _ffn_bwd(dh1, df1, x, n1g, gu1, wgu1, wd1, token, tm=512, name="ffn1_bwd")
    d_wgu1_a = _wgrad_gate_up(n1, dgu1, tk=WGRAD_TOKENS, name="ffn1_wgrad_gate_up_a", part=0, parts=2)
    (g_wgu1_a, replicated_late), token = exchange("ffn1_gate_up_a", [d_wgu1_a, d_n1])
    d_wgu1_b = _wgrad_gate_up(n1, dgu1, tk=WGRAD_TOKENS, name="ffn1_wgrad_gate_up_b", part=1, parts=2)
    (g_wgu1_b,), token = exchange("last", [d_wgu1_b])
    g_wgu1 = (g_wgu1_a, g_wgu1_b)

    sharded = (g_wgu1, g_wd1, g_win, g_wbp, g_wba, g_wout, g_wgu2, g_wd2)
    return dx, sharded, (replicated_late, replicated_early), token


def _hidden_major(w):
    return jnp.swapaxes(w[0], 0, 1)


def _pad_gate_up(wt):
    d = wt.shape[1]
    wt = wt.astype(BF16).reshape(2, FF_SHARD, d)
    return jnp.pad(wt, ((0, 0), (0, FF_SHARD_PAD - FF_SHARD), (0, 0))).reshape(2 * FF_SHARD_PAD, d)


def _unpad_gate_up(gt):
    d = gt.shape[1]
    return gt.reshape(2, FF_SHARD_PAD, d)[:, :FF_SHARD].reshape(2 * FF_SHARD, d)


def _pad_down(w):
    return jnp.pad(w.astype(BF16), ((0, FF_SHARD_PAD - FF_SHARD), (0, 0)))


def kernel(x, ffn1_norm, ffn1_w_gate_up, ffn1_w_down, mix_norm, w_in, pool_w_group, pool_scale, w_branch_pool, w_branch_attn, w_out, ffn2_norm, ffn2_w_gate_up, ffn2_w_down, final_norm, loss_target, m_ffn1_norm, m_ffn1_w_gate_up, m_ffn1_w_down, m_mix_norm, m_w_in, m_pool_w_group, m_pool_scale, m_w_branch_pool, m_w_branch_attn, m_w_out, m_ffn2_norm, m_ffn2_w_gate_up, m_ffn2_w_down, m_final_norm, v_ffn1_norm, v_ffn1_w_gate_up, v_ffn1_w_down, v_mix_norm, v_w_in, v_pool_w_group, v_pool_scale, v_w_branch_pool, v_w_branch_attn, v_w_out, v_ffn2_norm, v_ffn2_w_gate_up, v_ffn2_w_down, v_final_norm):
    D = x.shape[-1]
    weights = dict(ffn1_norm=ffn1_norm, ffn1_w_gate_up=ffn1_w_gate_up, ffn1_w_down=ffn1_w_down, mix_norm=mix_norm,
                   w_in=w_in, pool_w_group=pool_w_group, pool_scale=pool_scale, w_branch_pool=w_branch_pool,
                   w_branch_attn=w_branch_attn, w_out=w_out, ffn2_norm=ffn2_norm, ffn2_w_gate_up=ffn2_w_gate_up,
                   ffn2_w_down=ffn2_w_down, final_norm=final_norm)
    first = dict(ffn1_norm=m_ffn1_norm, ffn1_w_gate_up=m_ffn1_w_gate_up, ffn1_w_down=m_ffn1_w_down,
                 mix_norm=m_mix_norm, w_in=m_w_in, pool_w_group=m_pool_w_group, pool_scale=m_pool_scale,
                 w_branch_pool=m_w_branch_pool, w_branch_attn=m_w_branch_attn, w_out=m_w_out,
                 ffn2_norm=m_ffn2_norm, ffn2_w_gate_up=m_ffn2_w_gate_up, ffn2_w_down=m_ffn2_w_down,
                 final_norm=m_final_norm)
    second = dict(ffn1_norm=v_ffn1_norm, ffn1_w_gate_up=v_ffn1_w_gate_up, ffn1_w_down=v_ffn1_w_down,
                  mix_norm=v_mix_norm, w_in=v_w_in, pool_w_group=v_pool_w_group, pool_scale=v_pool_scale,
                  w_branch_pool=v_w_branch_pool, w_branch_attn=v_w_branch_attn, w_out=v_w_out,
                  ffn2_norm=v_ffn2_norm, ffn2_w_gate_up=v_ffn2_w_gate_up, ffn2_w_down=v_ffn2_w_down,
                  final_norm=v_final_norm)
    order = list(weights)

    wgu1, = _all_gather([_pad_gate_up(_hidden_major(ffn1_w_gate_up))], name="all_gather_ffn1_gate_up", collective_id=0)
    wd1, = _all_gather([_pad_down(ffn1_w_down[0])], name="all_gather_ffn1_down", collective_id=10)
    transposed = lambda w: jnp.swapaxes(w[0], 0, 1).astype(BF16)
    win_g, = _all_gather([transposed(w_in)], name="all_gather_w_in", collective_id=1)
    wbp_g, wba_g = _all_gather([transposed(w_branch_pool), transposed(w_branch_attn)],
                               name="all_gather_branches", collective_id=2)
    wout_g, = _all_gather([w_out[0].astype(BF16)], name="all_gather_w_out", collective_id=11)
    wgu2, wd2 = _all_gather([_pad_gate_up(_hidden_major(ffn2_w_gate_up)), _pad_down(ffn2_w_down[0])],
                            name="all_gather_ffn2", collective_id=3)
    whole = lambda g: g.reshape(g.shape[0] * g.shape[1], g.shape[2])
    wd1, wd2, win_g, wbp_g, wba_g, wout_g = (whole(g) for g in (wd1, wd2, win_g, wbp_g, wba_g, wout_g))

    cross_ids = {"ffn2": 4, "mix": 5, "w_in_ffn1_down": 8, "ffn1_gate_up_a": 9, "last": 7}
    small = ["ffn1_norm", "mix_norm", "ffn2_norm", "final_norm", "pool_scale", "pool_w_group"]

    def tile_rows(a):
        a = a.reshape(-1, 128)
        return jnp.pad(a, ((0, -a.shape[0] % 8), (0, 0)))

    def exchange(tag, group):
        grads = [g for g in group if g.dtype == BF16]
        extras = [tile_rows(g) for g in group if g.dtype != BF16]
        sums = _chip_sums(grads, name="chip_sums_" + tag)
        partials = [s[0] for s in sums]
        handles = []
        if extras:
            landed, slabs = _cross_chips_and_gather(partials, jnp.concatenate(extras, axis=0),
                                                    name="cross_chips_" + tag, collective_id=cross_ids[tag])
            handles = [slabs]
        else:
            landed = _cross_chips(partials, name="cross_chips_" + tag, collective_id=cross_ids[tag])
        return [(s[1], l) for s, l in zip(sums, landed)] + handles, sums[-1][1]

    norms = (ffn1_norm, mix_norm, ffn2_norm, final_norm.reshape(1, D))
    dx, sharded, (slabs_late, slabs_early), last = _local_step(
        x[0], loss_target[0], norms, pool_w_group[0], pool_scale, wgu1, wd1, win_g, wbp_g, wba_g, wout_g, wgu2, wd2,
        exchange)
    names = ["ffn1_w_gate_up", "ffn1_w_down", "w_in", "w_branch_pool", "w_branch_attn", "w_out",
             "ffn2_w_gate_up", "ffn2_w_down"]
    handles = dict(zip(names, sharded))
    grads, delta, new_m, new_v = {}, {}, {}, {}
    after = last
    for k in ("ffn2_w_down", "ffn2_w_gate_up", "w_branch_pool", "w_branch_attn", "w_out", "w_in", "ffn1_w_down",
              "ffn1_w_gate_up"):
        hidden_major = k.endswith("w_gate_up")
        if isinstance(handles[k][0], tuple):
            first_half = _owner_sum(*handles[k][0], after, name="owner_sum_" + k + "_a")
            second_half = _owner_sum(*handles[k][1], first_half, name="owner_sum_" + k + "_b")
            g = jnp.concatenate([first_half[:FF_SHARD], second_half[:FF_SHARD]], axis=0)
        else:
            g = _owner_sum(*handles[k], after, name="owner_sum_" + k)
            if hidden_major:
                g = _unpad_gate_up(g)
            elif k in ("w_in", "w_branch_pool", "w_branch_attn"):
                g = jnp.swapaxes(g, 0, 1)
            else:
                g = g[:weights[k].shape[1]]
        view = _hidden_major if hidden_major else (lambda a: a[0])
        back = (lambda a: jnp.swapaxes(a, 0, 1)[None]) if hidden_major else (lambda a: a[None])
        out = _adamw(view(weights[k]), g, view(first[k]), view(second[k]), name="adamw_" + k)
        after = out[0]
        grads[k] = back(g)
        delta[k], new_m[k], new_v[k] = (back(a) for a in out)

    rows = [weights[k].size // 128 for k in small]
    padded_rows = [-(-r // 8) * 8 for r in rows]
    starts = [sum(padded_rows[:i]) for i in range(len(rows) + 1)]
    total = jnp.concatenate([_sum_devices(slabs_late, after, name="sum_replicated_late"),
                             _sum_devices(slabs_early, after, name="sum_replicated_early")], axis=0)
    loss_out = total[starts[-1], 0]
    small_w = jnp.concatenate([tile_rows(weights[k]) for k in small], axis=0)
    small_m = jnp.concatenate([tile_rows(first[k]) for k in small], axis=0)
    small_v = jnp.concatenate([tile_rows(second[k]) for k in small], axis=0)
    small_out = _adamw(small_w, total[:starts[-1]], small_m, small_v, name="adamw_replicated")
    for name_, start, n_rows in zip(small, starts, rows):
        shape = weights[name_].shape
        grads[name_] = total[start:start + n_rows].reshape(shape)
        delta[name_], new_m[name_], new_v[name_] = (a[start:start + n_rows].reshape(shape) for a in small_out)

    return (loss_out, dx[None], *[grads[k] for k in order], *[delta[k] for k in order],
            *[new_m[k] for k in order], *[new_v[k] for k in order])
```

```python
import jax
import jax.numpy as jnp
from jax import lax
from jax.experimental import pallas as pl
from jax.experimental.pallas import tpu as pltpu
from jax.experimental.pallas import tpu_sc as plsc

F32 = jnp.float32
BF16 = jnp.bfloat16
MESH = pl.DeviceIdType.MESH

RMS_EPS = 1e-6
N_DEV = 8
N_HEADS = 8
HEAD_DIM = 64
HEAD_PAIR = 2 * HEAD_DIM
POOL_WINDOWS = (2, 4, 8, 16)
POOL_GROUP = 128
POOL_WIDTH = 512
SB_WIDTH = 512
FF_SHARD = 352
FF_SHARD_PAD = 384
ATTN_K_BLOCK = 256
ATTN_Q_BLOCK_FWD = 512
ATTN_Q_BLOCK_BWD = 256
ATTN_SCALE = 0.125

ADAM_LR = 0.001
ADAM_B1 = 0.9
ADAM_B2 = 0.999
ADAM_EPS = 1e-08
ADAM_WD = 0.01
ADAM_STEP = 10

VMEM_LIMIT = 48 << 20
WGRAD_TOKENS = 2048


def _params(dims=None):
    return pltpu.CompilerParams(dimension_semantics=dims, vmem_limit_bytes=VMEM_LIMIT)


def _mm(a, b):
    return jnp.dot(a, b, preferred_element_type=F32)


def _mm_nt(a, b):
    return lax.dot_general(a, b, (((1,), (1,)), ((), ())), preferred_element_type=F32)


def _mm_tn(a, b):
    return lax.dot_general(a, b, (((0,), (0,)), ((), ())), preferred_element_type=F32)


def _row_tile(rows, cols):
    limit = max(8, (512 * 1024) // cols)
    return max(t for t in range(8, rows + 1, 8) if rows % t == 0 and (t <= limit or t == 8))


def _rstd(xf):
    return lax.rsqrt(jnp.mean(xf * xf, axis=-1, keepdims=True) + RMS_EPS)


def _rms_bwd(xf, gain, dn):
    r = _rstd(xf)
    xh = xf * r
    dgain = jnp.sum(dn * xh, axis=0, keepdims=True)
    dxh = dn * gain
    dx = r * (dxh - xh * jnp.mean(dxh * xh, axis=-1, keepdims=True))
    return dx, dgain


def _ffn_up(x, gain, wgu, *, tm, name):
    T, D = x.shape
    tm = min(tm, T)
    nb, bw = wgu.shape[0] // 2, wgu.shape[1]

    def body(x_ref, gain_ref, wg_ref, wu_ref, gu_ref, hid_ref, n_scr):
        @pl.when(pl.program_id(1) == 0)
        def _():
            xf = x_ref[...]
            n_scr[...] = (xf * _rstd(xf) * gain_ref[...]).astype(BF16)

        halves = (pl.ds(0, tm // 2), pl.ds(tm // 2, tm // 2))
        wg, wu = wg_ref[...], wu_ref[...]
        gus = [(_mm_nt(n_scr[rows, :], wg), _mm_nt(n_scr[rows, :], wu)) for rows in halves]
        for rows, (g, u) in zip(halves, gus):
            gu_ref[0, rows, :] = g.astype(BF16)
            gu_ref[1, rows, :] = u.astype(BF16)
            hid_ref[rows, :] = (g * jax.nn.sigmoid(g) * u).astype(BF16)

    return pl.pallas_call(
        body, name=name, grid=(T // tm, nb),
        in_specs=[
            pl.BlockSpec((tm, D), lambda i, j: (i, 0)),
            pl.BlockSpec((1, D), lambda i, j: (0, 0)),
            pl.BlockSpec((None, bw, D), lambda i, j: (j, 0, 0)),
            pl.BlockSpec((None, bw, D), lambda i, j: (j + nb, 0, 0)),
        ],
        out_specs=[
            pl.BlockSpec((2, tm, bw), lambda i, j: (0, i, j)),
            pl.BlockSpec((tm, bw), lambda i, j: (i, j)),
        ],
        out_shape=[jax.ShapeDtypeStruct((2, T, nb * bw), BF16), jax.ShapeDtypeStruct((T, nb * bw), BF16)],
        scratch_shapes=[pltpu.VMEM((tm, D), BF16)],
        compiler_params=_params(("arbitrary", "arbitrary")),
    )(x, gain, wgu, wgu)


def _ffn_down(x, hid, wd, *, tm, name):
    T, D = x.shape
    tm = min(tm, T)
    F = hid.shape[1]

    def body(x_ref, hid_ref, wd_ref, h_ref):
        h_ref[...] = x_ref[...] + 0.5 * _mm(hid_ref[...], wd_ref[...])

    return pl.pallas_call(
        body, name=name, grid=(T // tm,),
        in_specs=[
            pl.BlockSpec((tm, D), lambda i: (i, 0)),
            pl.BlockSpec((tm, F), lambda i: (i, 0)),
            pl.BlockSpec((F, D), lambda i: (0, 0)),
        ],
        out_specs=pl.BlockSpec((tm, D), lambda i: (i, 0)),
        out_shape=jax.ShapeDtypeStruct((T, D), F32),
        compiler_params=_params(("arbitrary",)),
    )(x, hid, wd)


AFTER = pl.BlockSpec(memory_space=pltpu.HBM)


def _in_hbm(token):
    return pltpu.with_memory_space_constraint(token, pltpu.HBM)


def _ffn_bwd(dh, df, x, gain, gu, wgu, wd, after, *, tm, name):
    T, D = x.shape
    tm = min(tm, T)
    nb, bw = wgu.shape[0] // 2, wgu.shape[1]

    def body(dh_ref, df_ref, x_ref, gain_ref, gu_ref, wg_ref, wu_ref, wd_ref, after_ref,
             dx_ref, dgain_ref, n_ref, dgu_ref, dn_acc):
        i, j = pl.program_id(0), pl.program_id(1)

        @pl.when(j == 0)
        def _():
            xf = x_ref[...]
            n_ref[...] = (xf * _rstd(xf) * gain_ref[...]).astype(BF16)
            dn_acc[...] = jnp.zeros_like(dn_acc)

        @pl.when((i == 0) & (j == 0))
        def _():
            dgain_ref[...] = jnp.zeros_like(dgain_ref)

        halves = (pl.ds(0, tm // 2), pl.ds(tm // 2, tm // 2))
        wd, wg, wu = wd_ref[...], wg_ref[...], wu_ref[...]
        dhids = [_mm_nt(df_ref[rows, :], wd) for rows in halves]
        for rows, dhid in zip(halves, dhids):
            g = gu_ref[0, rows, :].astype(F32)
            u = gu_ref[1, rows, :].astype(F32)
            s = jax.nn.sigmoid(g)
            silu = g * s
            dg = (dhid * u * (s * (1.0 + g * (1.0 - s)))).astype(BF16)
            du = (dhid * silu).astype(BF16)
            dgu_ref[0, rows, :] = dg
            dgu_ref[1, rows, :] = du
            dn_acc[rows, :] += _mm(dg, wg) + _mm(du, wu)

        @pl.when(j == nb - 1)
        def _():
            dx, dgain = _rms_bwd(x_ref[...], gain_ref[...], dn_acc[...])
            dx_ref[...] = dh_ref[...] + dx
            dgain_ref[...] += dgain

    row = lambda i, j: (i, 0)
    return pl.pallas_call(
        body, name=name, grid=(T // tm, nb),
        in_specs=[
            pl.BlockSpec((tm, D), row),
            pl.BlockSpec((tm, D), row),
            pl.BlockSpec((tm, D), row),
            pl.BlockSpec((1, D), lambda i, j: (0, 0)),
            pl.BlockSpec((2, tm, bw), lambda i, j: (0, i, j)),
            pl.BlockSpec((None, bw, D), lambda i, j: (j, 0, 0)),
            pl.BlockSpec((None, bw, D), lambda i, j: (j + nb, 0, 0)),
            pl.BlockSpec((bw, D), lambda i, j: (j, 0)),
            AFTER,
        ],
        out_specs=[
            pl.BlockSpec((tm, D), row),
            pl.BlockSpec((1, D), lambda i, j: (0, 0)),
            pl.BlockSpec((tm, D), row),
            pl.BlockSpec((2, tm, bw), lambda i, j: (0, i, j)),
        ],
        out_shape=[
            jax.ShapeDtypeStruct((T, D), F32),
            jax.ShapeDtypeStruct((1, D), F32),
            jax.ShapeDtypeStruct((T, D), BF16),
            jax.ShapeDtypeStruct((2, T, nb * bw), BF16),
        ],
        scratch_shapes=[pltpu.VMEM((tm, D), F32)],
        compiler_params=_params(("arbitrary", "arbitrary")),
    )(dh, df, x, gain, gu, wgu, wgu, wd, _in_hbm(after))


def _wgrad(a, b, *, grid, a_spec, b_spec, out_spec, out_shape, acc_shape, name):
    nk = grid[2]

    def body(a_ref, b_ref, o_ref, acc):
        k = pl.program_id(2)

        @pl.when(k == 0)
        def _():
            acc[...] = jnp.zeros_like(acc)

        acc[...] += _mm_tn(a_ref[...].astype(BF16), b_ref[...].astype(BF16))

        @pl.when(k == nk - 1)
        def _():
            o_ref[...] = acc[...].astype(o_ref.dtype)

    return pl.pallas_call(
        body, name=name, grid=grid, in_specs=[a_spec, b_spec], out_specs=out_spec,
        out_shape=jax.ShapeDtypeStruct(out_shape, BF16),
        scratch_shapes=[pltpu.VMEM(acc_shape, F32)],
        compiler_params=_params(("arbitrary", "arbitrary", "arbitrary")),
    )(a, b)


def _wgrad_gate_up(n, dgu, *, tk, name, part=0, parts=1):
    T, D = n.shape
    tk = min(tk, T)
    owner_rows = FF_SHARD_PAD * 2
    nb = dgu.shape[2] // owner_rows
    bw = owner_rows // parts
    return _wgrad(
        dgu, n, grid=(2 * nb, 1, T // tk), name=name,
        a_spec=pl.BlockSpec((None, tk, bw), lambda m, c, k: (m // nb, k, parts * (m % nb) + part)),
        b_spec=pl.BlockSpec((tk, D), lambda m, c, k: (k, 0)),
        out_spec=pl.BlockSpec((None, bw, D), lambda m, c, k: (m, 0, 0)),
        out_shape=(2 * nb, bw, D), acc_shape=(bw, D))


def _wgrad_down(hid, df, *, tk, name):
    T, D = df.shape
    tk = min(tk, T)
    bw = FF_SHARD_PAD * 2
    nb = hid.shape[1] // bw
    return _wgrad(
        hid, df, grid=(nb, 1, T // tk), name=name,
        a_spec=pl.BlockSpec((tk, bw), lambda m, c, k: (k, m)),
        b_spec=pl.BlockSpec((tk, D), lambda m, c, k: (k, 0)),
        out_spec=pl.BlockSpec((bw, D), lambda m, c, k: (m, 0)),
        out_shape=(nb * bw, D), acc_shape=(bw, D))


def _wgrad_in(dparts, un, *, name):
    T, D = un.shape
    bw = sum(p.shape[1] for p in dparts) // N_DEV
    first = [sum(p.shape[1] for p in dparts[:i]) // bw for i in range(len(dparts) + 1)]

    def body(*refs):
        dp_refs, un_ref, o_ref = refs[:-2], refs[-2], refs[-1]
        m = pl.program_id(0)
        for dp_ref, lo, hi in zip(dp_refs, first[:-1], first[1:]):
            @pl.when((m >= lo) & (m < hi))
            def _():
                o_ref[...] = _mm_tn(dp_ref[...], un_ref[...]).astype(o_ref.dtype)

    def piece_spec(lo, hi):
        return pl.BlockSpec((T, bw), lambda m: (0, jnp.clip(m - lo, 0, hi - lo - 1)))

    return pl.pallas_call(
        body, name=name, grid=(N_DEV,),
        in_specs=[piece_spec(lo, hi) for lo, hi in zip(first[:-1], first[1:])] + [pl.BlockSpec((T, D), lambda m: (0, 0))],
        out_specs=pl.BlockSpec((None, bw, D), lambda m: (m, 0, 0)),
        out_shape=jax.ShapeDtypeStruct((N_DEV, bw, D), BF16),
        compiler_params=_params(("arbitrary",)),
    )(*dparts, un)


def _wgrad_full(a, b, *, tk, name):
    T, M = a.shape
    tk = min(tk, T)
    N = b.shape[1]
    return _wgrad(
        a, b, grid=(1, 1, T // tk), name=name,
        a_spec=pl.BlockSpec((tk, M), lambda m, c, k: (k, 0)),
        b_spec=pl.BlockSpec((tk, N), lambda m, c, k: (k, 0)),
        out_spec=pl.BlockSpec((M, N), lambda m, c, k: (0, 0)), out_shape=(M, N), acc_shape=(M, N))


def _loss_bwd(h, target, gain, *, tm, name):
    T, D = h.shape
    tm = min(tm, T)

    def body(h_ref, t_ref, gain_ref, dh_ref, df_ref, loss_ref, dgain_ref):
        @pl.when(pl.program_id(0) == 0)
        def _():
            loss_ref[...] = jnp.zeros_like(loss_ref)
            dgain_ref[...] = jnp.zeros_like(dgain_ref)

        xf = h_ref[...]
        gain = gain_ref[...]
        err = xf * _rstd(xf) * gain - t_ref[...]
        loss_ref[...] += 0.5 * jnp.sum(jnp.mean(err * err, axis=-1, keepdims=True), axis=0, keepdims=True)
        dx, dgain = _rms_bwd(xf, gain, err * (1.0 / D))
        dh_ref[...] = dx
        df_ref[...] = (0.5 * dx).astype(BF16)
        dgain_ref[...] += dgain

    row = lambda i: (i, 0)
    fixed = lambda i: (0, 0)
    return pl.pallas_call(
        body, name=name, grid=(T // tm,),
        in_specs=[pl.BlockSpec((tm, D), row), pl.BlockSpec((tm, D), row), pl.BlockSpec((1, D), fixed)],
        out_specs=[pl.BlockSpec((tm, D), row), pl.BlockSpec((tm, D), row), pl.BlockSpec((1, 128), fixed),
                   pl.BlockSpec((1, D), fixed)],
        out_shape=[jax.ShapeDtypeStruct((T, D), F32), jax.ShapeDtypeStruct((T, D), BF16),
                   jax.ShapeDtypeStruct((1, 128), F32), jax.ShapeDtypeStruct((1, D), F32)],
        compiler_params=_params(("arbitrary",)),
    )(h, target, gain)


def _inproj_fwd(h, gain, w_in_t, *, tm, name):
    T, D = h.shape
    tm = min(tm, T)
    bn = D
    nb = w_in_t.shape[0] // bn

    def body(h_ref, gain_ref, wt_ref, un_ref, proj_ref):
        @pl.when(pl.program_id(1) == 0)
        def _():
            xf = h_ref[...]
            un_ref[...] = (xf * _rstd(xf) * gain_ref[...]).astype(BF16)

        proj_ref[...] = _mm_nt(un_ref[...], wt_ref[...])

    return pl.pallas_call(
        body, name=name, grid=(T // tm, nb),
        in_specs=[
            pl.BlockSpec((tm, D), lambda i, j: (i, 0)),
            pl.BlockSpec((1, D), lambda i, j: (0, 0)),
            pl.BlockSpec((bn, D), lambda i, j: (j, 0)),
        ],
        out_specs=[pl.BlockSpec((tm, D), lambda i, j: (i, 0)), pl.BlockSpec((tm, bn), lambda i, j: (i, j))],
        out_shape=[jax.ShapeDtypeStruct((T, D), BF16), jax.ShapeDtypeStruct((T, nb * bn), F32)],
        compiler_params=_params(("arbitrary", "arbitrary")),
    )(h, gain, w_in_t)


def _inproj_bwd(dparts, dh, h, gain, w_in_t, *, tm, name):
    T, D = h.shape
    tm = min(tm, T)
    n = len(dparts)
    widths = [p.shape[1] for p in dparts]
    starts = [sum(widths[:i]) for i in range(n)]

    def body(*refs):
        dp_refs = refs[:n]
        dh_ref, h_ref, gain_ref, wt_ref, dx_ref, df_ref, dgain_ref = refs[n:]

        @pl.when(pl.program_id(0) == 0)
        def _():
            dgain_ref[...] = jnp.zeros_like(dgain_ref)

        dn = sum(_mm(dp_ref[...], wt_ref[start:start + width, :])
                 for dp_ref, start, width in zip(dp_refs, starts, widths))
        dx, dgain = _rms_bwd(h_ref[...], gain_ref[...], dn)
        dh_in = dh_ref[...] + dx
        dx_ref[...] = dh_in
        df_ref[...] = (0.5 * dh_in).astype(BF16)
        dgain_ref[...] += dgain

    row = lambda i: (i, 0)
    fixed = lambda i: (0, 0)
    return pl.pallas_call(
        body, name=name, grid=(T // tm,),
        in_specs=[pl.BlockSpec((tm, width), row) for width in widths] + [
            pl.BlockSpec((tm, D), row),
            pl.BlockSpec((tm, D), row),
            pl.BlockSpec((1, D), fixed),
            pl.BlockSpec(w_in_t.shape, fixed),
        ],
        out_specs=[pl.BlockSpec((tm, D), row), pl.BlockSpec((tm, D), row), pl.BlockSpec((1, D), fixed)],
        out_shape=[jax.ShapeDtypeStruct((T, D), F32), jax.ShapeDtypeStruct((T, D), BF16),
                   jax.ShapeDtypeStruct((1, D), F32)],
        compiler_params=_params(("arbitrary",)),
    )(*dparts, dh, h, gain, w_in_t)


def _window_sum(x, row, doublings, *, backward):
    T = x.shape[0]
    s = x
    for k in range(doublings):
        sh = 1 << k
        if backward:
            s = s + jnp.where(row < T - sh, pltpu.roll(s, T - sh, 0), 0.0)
        else:
            s = s + jnp.where(row >= sh, pltpu.roll(s, sh, 0), 0.0)
    return s


def _pool_fwd(proj, w_group, scale, *, name):
    T = proj.shape[0]

    def body(xp_ref, w_ref, scale_ref, p_ref):
        row = lax.broadcasted_iota(jnp.int32, (T, POOL_GROUP), 0)
        for gi, window in enumerate(POOL_WINDOWS):
            cols = slice(gi * POOL_GROUP, (gi + 1) * POOL_GROUP)
            x = xp_ref[:, cols]
            inv_count = 1.0 / jnp.minimum(row + 1, window).astype(F32)
            yc = _window_sum(x, row, gi + 1, backward=False) * inv_count - x
            pre = _mm(yc.astype(BF16), w_ref[gi].astype(BF16))
            p_ref[:, cols] = pre * scale_ref[:, cols]

    return pl.pallas_call(
        body, name=name, grid=(1,),
        in_specs=[
            pl.BlockSpec((T, POOL_WIDTH), lambda i: (0, 0)),
            pl.BlockSpec(w_group.shape, lambda i: (0, 0, 0)),
            pl.BlockSpec((1, POOL_WIDTH), lambda i: (0, 0)),
        ],
        out_specs=pl.BlockSpec((T, POOL_WIDTH), lambda i: (0, 0)),
        out_shape=jax.ShapeDtypeStruct((T, POOL_WIDTH), F32),
        compiler_params=_params(("arbitrary",)),
    )(proj, w_group, scale)


def _pool_bwd(dp, proj, w_group, scale, *, name):
    T = proj.shape[0]

    def body(dp_ref, xp_ref, w_ref, scale_ref, dxp_ref, dw_ref, dscale_ref):
        row = lax.broadcasted_iota(jnp.int32, (T, POOL_GROUP), 0)
        for gi, window in enumerate(POOL_WINDOWS):
            cols = slice(gi * POOL_GROUP, (gi + 1) * POOL_GROUP)
            x = xp_ref[:, cols]
            inv_count = 1.0 / jnp.minimum(row + 1, window).astype(F32)
            yc = (_window_sum(x, row, gi + 1, backward=False) * inv_count - x).astype(BF16)
            w = w_ref[gi].astype(BF16)
            pre = _mm(yc, w)
            dpg = dp_ref[:, cols]
            dscale_ref[:, cols] = jnp.sum(dpg * pre, axis=0, keepdims=True)
            dpre = (dpg * scale_ref[:, cols]).astype(BF16)
            dw_ref[gi] = _mm_tn(yc, dpre)
            dyc = _mm_nt(dpre, w)
            dxp_ref[:, cols] = (_window_sum(dyc * inv_count, row, gi + 1, backward=True) - dyc).astype(BF16)

    return pl.pallas_call(
        body, name=name, grid=(1,),
        in_specs=[
            pl.BlockSpec((T, POOL_WIDTH), lambda i: (0, 0)),
            pl.BlockSpec((T, POOL_WIDTH), lambda i: (0, 0)),
            pl.BlockSpec(w_group.shape, lambda i: (0, 0, 0)),
            pl.BlockSpec((1, POOL_WIDTH), lambda i: (0, 0)),
        ],
        out_specs=[
            pl.BlockSpec((T, POOL_WIDTH), lambda i: (0, 0)),
            pl.BlockSpec(w_group.shape, lambda i: (0, 0, 0)),
            pl.BlockSpec((1, POOL_WIDTH), lambda i: (0, 0)),
        ],
        out_shape=[jax.ShapeDtypeStruct((T, POOL_WIDTH), BF16), jax.ShapeDtypeStruct(w_group.shape, F32),
                   jax.ShapeDtypeStruct((1, POOL_WIDTH), F32)],
        compiler_params=_params(("arbitrary",)),
    )(dp, proj, w_group, scale)


ATTN_STRIP = 32


def _log_sigmoids(z):
    lb = jnp.minimum(z, 0.0) - jnp.log(1.0 + jnp.exp(-jnp.abs(z)))
    return lb, lb - z


def _transposed_blocks(x_ref, blocks_scr, tq):
    for b in range(blocks_scr.shape[0]):
        blocks_scr[b] = x_ref[b * tq:(b + 1) * tq, :].T.astype(BF16)


def _split_bf16(x):
    hi = x.astype(BF16)
    return hi, (x - hi.astype(F32)).astype(BF16)


def _strips(n):
    return [slice(i, i + ATTN_STRIP) for i in range(0, n, ATTN_STRIP)]


def _rows(parts):
    return jnp.concatenate(parts, axis=0)


def _attn_specs(T, tq):
    q_col = POOL_WIDTH // HEAD_PAIR
    k_col = q_col + SB_WIDTH // HEAD_PAIR
    v_col = k_col + SB_WIDTH // HEAD_PAIR
    return [
        pl.BlockSpec((tq, HEAD_PAIR), lambda p, i: (i, q_col + p)),
        pl.BlockSpec((T, HEAD_PAIR), lambda p, i: (0, k_col + p)),
        pl.BlockSpec((T, HEAD_PAIR), lambda p, i: (0, v_col + p)),
    ]


def _attn_fwd(proj, *, name):
    T = proj.shape[0]
    tk = min(ATTN_K_BLOCK, T)
    tq = min(ATTN_Q_BLOCK_FWD, T)
    diagonal_blocks = tq // tk

    def body(q_ref, k_ref, v_ref, o_ref, lt_ref, kt_scr, vb_scr):
        qi = pl.program_id(1)

        @pl.when(qi == 0)
        def _():
            _transposed_blocks(k_ref, kt_scr, tk)
            vb_scr[...] = v_ref[...].astype(BF16)

        head0 = lax.broadcasted_iota(jnp.int32, (tq, HEAD_PAIR), 1) < HEAD_DIM
        q = q_ref[...] * ATTN_SCALE
        qs = (jnp.where(head0, q, 0.0).astype(BF16), jnp.where(head0, 0.0, q).astype(BF16))
        r = lax.broadcasted_iota(jnp.int32, (tq, tk), 0)
        c = lax.broadcasted_iota(jnp.int32, (tq, tk), 1)
        later = (r[:tk] > c[:tk]).astype(BF16)
        later2 = _rows([later, later])
        causal = lambda d: (lambda rows: c[rows] + d * tk < r[rows])
        strips = _strips(tq)

        def log_terms(z, valid):
            lbs, his, los, sums = [], [], [], []
            for rows in strips:
                lb, lm = _log_sigmoids(z[rows])
                if valid is not None:
                    lm = jnp.where(valid(rows), lm, 0.0)
                hi, lo = _split_bf16(lm)
                lbs.append(lb)
                his.append(hi)
                los.append(lo)
                sums.append(jnp.sum(lm, axis=1, keepdims=True))
            return lbs, jnp.concatenate([_rows(his), _rows(los)], axis=1), _rows(sums)

        def weights(lbs, run, after, valid):
            parts = []
            for rows, lb in zip(strips, lbs):
                a = jnp.exp(lb + run[rows] + after[rows])
                if valid is not None:
                    a = jnp.where(valid(rows), a, 0.0)
                parts.append(a.astype(BF16))
            return _rows(parts)

        def block(kj, carry, valid):
            kt = kt_scr[kj]
            vb = vb_scr[pl.ds(pl.multiple_of(kj * tk, tk), tk), :]
            run0, o0, run1, o1 = carry
            z0 = _mm(qs[0], kt)
            z1 = _mm(qs[1], kt)
            lbs0, split0, sums0 = log_terms(z0, valid)
            after0 = _mm(split0, later2)
            lbs1, split1, sums1 = log_terms(z1, valid)
            after1 = _mm(split1, later2)
            o0 = o0 + _mm(weights(lbs0, run0, after0, valid), vb)
            o1 = o1 + _mm(weights(lbs1, run1, after1, valid), vb)
            return run0 + sums0, o0, run1 + sums1, o1

        zero = (jnp.zeros((tq, 1), F32), jnp.zeros((tq, HEAD_PAIR), F32))
        first = diagonal_blocks * qi
        carry = zero + zero
        for d in reversed(range(diagonal_blocks)):
            carry = block(first + d, carry, causal(d))
        carry = lax.fori_loop(0, first, lambda it, cr: block(first - 1 - it, cr, None), carry)
        o_ref[...] = jnp.where(head0, carry[1], carry[3])
        lt_ref[...] = jnp.where(head0, carry[0], carry[2])

    out_spec = pl.BlockSpec((tq, HEAD_PAIR), lambda p, i: (i, p))
    return pl.pallas_call(
        body, name=name, grid=(N_HEADS // 2, T // tq),
        in_specs=_attn_specs(T, tq), out_specs=[out_spec, out_spec],
        out_shape=[jax.ShapeDtypeStruct((T, SB_WIDTH), F32), jax.ShapeDtypeStruct((T, SB_WIDTH), F32)],
        scratch_shapes=[pltpu.VMEM((T // tk, HEAD_PAIR, tk), BF16), pltpu.VMEM((T, HEAD_PAIR), BF16)],
        compiler_params=_params(("arbitrary", "arbitrary")),
    )(proj, proj, proj)


def _attn_bwd(proj, do, ltot, after, *, name):
    T = proj.shape[0]
    tk = min(ATTN_K_BLOCK, T)
    tq = min(ATTN_Q_BLOCK_BWD, T)
    diagonal_blocks = tq // tk

    def body(q_ref, k_ref, v_ref, do_ref, lt_ref, after_ref, dq_ref, dk_ref, dv_ref,
             kb_scr, kt_scr, vt_scr, dkt_ref, dvt_ref):
        qi = pl.program_id(1)

        @pl.when(qi == 0)
        def _():
            kb_scr[...] = k_ref[...].astype(BF16)
            _transposed_blocks(k_ref, kt_scr, tk)
            _transposed_blocks(v_ref, vt_scr, tk)
            dkt_ref[...] = jnp.zeros_like(dkt_ref)
            dvt_ref[...] = jnp.zeros_like(dvt_ref)

        head0 = lax.broadcasted_iota(jnp.int32, (tq, HEAD_PAIR), 1) < HEAD_DIM
        q, do_, lt = q_ref[...] * ATTN_SCALE, do_ref[...], lt_ref[...]
        qs = (jnp.where(head0, q, 0.0).astype(BF16), jnp.where(head0, 0.0, q).astype(BF16))
        q_heads = (jnp.where(head0, q, 0.0), jnp.where(head0, 0.0, q))
        do_heads = (jnp.where(head0, do_, 0.0), jnp.where(head0, 0.0, do_))
        dos = tuple(d.astype(BF16) for d in do_heads)
        qts = tuple(x.T.astype(BF16) for x in q_heads)
        dots = tuple(d.T.astype(BF16) for d in do_heads)
        lts = (jnp.max(jnp.where(head0, lt, -jnp.inf), axis=1, keepdims=True),
               jnp.max(jnp.where(head0, -jnp.inf, lt), axis=1, keepdims=True))
        r = lax.broadcasted_iota(jnp.int32, (tq, tk), 0)
        c = lax.broadcasted_iota(jnp.int32, (tq, tk), 1)
        upto = (r[:tk] <= c[:tk]).astype(BF16)
        before = (r[:tk] < c[:tk]).astype(BF16)
        upto2, before2 = _rows([upto, upto]), _rows([before, before])
        causal = lambda d: (lambda rows: c[rows] + d * tk < r[rows])
        strips = _strips(tq)

        def log_terms(z, valid):
            lbs, his, los, sums = [], [], [], []
            for rows in strips:
                lb, lm = _log_sigmoids(z[rows])
                if valid is not None:
                    lm = jnp.where(valid(rows), lm, 0.0)
                hi, lo = _split_bf16(lm)
                lbs.append(lb)
                his.append(hi)
                los.append(lo)
                sums.append(jnp.sum(lm, axis=1, keepdims=True))
            return lbs, jnp.concatenate([_rows(his), _rows(los)], axis=1), _rows(sums)

        def weights(lbs, rest, lm_upto, da, valid):
            a_parts, es, his, los, sums = [], [], [], [], []
            for rows, lb in zip(strips, lbs):
                a = jnp.exp(lb + (rest[rows] - lm_upto[rows]))
                if valid is not None:
                    a = jnp.where(valid(rows), a, 0.0)
                e = da[rows] * a
                hi, lo = _split_bf16(e)
                a_parts.append(a.astype(BF16))
                es.append(e)
                his.append(hi)
                los.append(lo)
                sums.append(jnp.sum(e, axis=1, keepdims=True))
            return _rows(a_parts), es, jnp.concatenate([_rows(his), _rows(los)], axis=1), _rows(sums)

        def score_grads(lbs, es, run_e, e_before, valid):
            parts = []
            for rows, lb, e in zip(strips, lbs, es):
                beta = jnp.exp(lb)
                dz = e * (1.0 - beta) - (run_e[rows] + e_before[rows]) * beta
                if valid is not None:
                    dz = jnp.where(valid(rows), dz, 0.0)
                parts.append(dz.astype(BF16))
            return _rows(parts)

        def block(kj, carry, valid):
            off = pl.multiple_of(kj * tk, tk)
            kb, kt, vt = kb_scr[pl.ds(off, tk), :], kt_scr[kj], vt_scr[kj]
            run_lm0, run_e0, dq0, run_lm1, run_e1, dq1 = carry
            z0, da0 = _mm(qs[0], kt), _mm(dos[0], vt)
            z1, da1 = _mm(qs[1], kt), _mm(dos[1], vt)
            lbs0, split0, lm_sums0 = log_terms(z0, valid)
            lm_upto0 = _mm(split0, upto2)
            lbs1, split1, lm_sums1 = log_terms(z1, valid)
            lm_upto1 = _mm(split1, upto2)
            a0, es0, split0, e_sums0 = weights(lbs0, lts[0] - run_lm0, lm_upto0, da0, valid)
            e_before0 = _mm(split0, before2)
            a1, es1, split1, e_sums1 = weights(lbs1, lts[1] - run_lm1, lm_upto1, da1, valid)
            e_before1 = _mm(split1, before2)
            dz0 = score_grads(lbs0, es0, run_e0, e_before0, valid)
            dkt_blk = _mm(qts[0], dz0)
            dvt_blk = _mm(dots[0], a0)
            dq0 = dq0 + _mm(dz0, kb)
            dz1 = score_grads(lbs1, es1, run_e1, e_before1, valid)
            dkt_ref[kj] += dkt_blk + _mm(qts[1], dz1)
            dvt_ref[kj] += dvt_blk + _mm(dots[1], a1)
            dq1 = dq1 + _mm(dz1, kb)
            return run_lm0 + lm_sums0, run_e0 + e_sums0, dq0, run_lm1 + lm_sums1, run_e1 + e_sums1, dq1

        zero = (jnp.zeros((tq, 1), F32), jnp.zeros((tq, 1), F32), jnp.zeros((tq, HEAD_PAIR), F32))
        first = diagonal_blocks * qi
        carry = lax.fori_loop(0, first, lambda kj, cr: block(kj, cr, None), zero + zero)
        for d in range(diagonal_blocks):
            carry = block(first + d, carry, causal(d))
        dq_ref[...] = (jnp.where(head0, carry[2], carry[5]) * ATTN_SCALE).astype(BF16)

        @pl.when(qi == T // tq - 1)
        def _():
            for b in range(T // tk):
                dk_ref[b * tk:(b + 1) * tk, :] = dkt_ref[b].T.astype(BF16)
                dv_ref[b * tk:(b + 1) * tk, :] = dvt_ref[b].T.astype(BF16)

    blk = pl.BlockSpec((tq, HEAD_PAIR), lambda p, i: (i, p))
    seq = pl.BlockSpec((T, HEAD_PAIR), lambda p, i: (0, p))
    transposed = pltpu.VMEM((T // tk, HEAD_PAIR, tk), F32)
    return pl.pallas_call(
        body, name=name, grid=(N_HEADS // 2, T // tq),
        in_specs=_attn_specs(T, tq) + [blk, blk, AFTER], out_specs=[blk, seq, seq],
        out_shape=[jax.ShapeDtypeStruct((T, SB_WIDTH), BF16)] * 3,
        scratch_shapes=[pltpu.VMEM((T, HEAD_PAIR), BF16), pltpu.VMEM((T // tk, HEAD_PAIR, tk), BF16),
                        pltpu.VMEM((T // tk, HEAD_PAIR, tk), BF16), transposed, transposed],
        compiler_params=_params(("arbitrary", "arbitrary")),
    )(proj, proj, proj, do, ltot, _in_hbm(after))


def _mix_specs(T, D, tm, wbp, w_out):
    gate_col = (POOL_WIDTH + 3 * SB_WIDTH) // D
    row = lambda i: (i, 0)
    return [
        pl.BlockSpec((tm, D), row),
        pl.BlockSpec((tm, POOL_WIDTH), row),
        pl.BlockSpec((tm, SB_WIDTH), row),
        pl.BlockSpec((tm, D), lambda i: (i, gate_col)),
        pl.BlockSpec((tm, D), lambda i: (i, gate_col + 1)),
        pl.BlockSpec(wbp.shape, lambda i: (0, 0)),
        pl.BlockSpec(wbp.shape, lambda i: (0, 0)),
        pl.BlockSpec(w_out.shape, lambda i: (0, 0)),
    ]


def _mix_fwd(h, p, o, proj, wbp, wba, w_out, *, tm, name):
    T, D = h.shape
    tm = min(tm, T)

    def body(h_ref, p_ref, o_ref, glp_ref, gls_ref, wbp_ref, wba_ref, wout_ref, hout_ref, m_ref):
        halves = (pl.ds(0, tm // 2), pl.ds(tm // 2, tm // 2))
        wbp, wba, wout = wbp_ref[...], wba_ref[...], wout_ref[...]
        branches = [(_mm_nt(p_ref[rows, :].astype(BF16), wbp), _mm_nt(o_ref[rows, :].astype(BF16), wba))
                    for rows in halves]
        for rows, (yp, ys) in zip(halves, branches):
            m = (jax.nn.sigmoid(glp_ref[rows, :]) * yp + jax.nn.sigmoid(gls_ref[rows, :]) * ys).astype(BF16)
            m_ref[rows, :] = m
            hout_ref[rows, :] = h_ref[rows, :] + _mm(m, wout)

    row = lambda i: (i, 0)
    return pl.pallas_call(
        body, name=name, grid=(T // tm,),
        in_specs=_mix_specs(T, D, tm, wbp, w_out),
        out_specs=[pl.BlockSpec((tm, D), row), pl.BlockSpec((tm, D), row)],
        out_shape=[jax.ShapeDtypeStruct((T, D), F32), jax.ShapeDtypeStruct((T, D), BF16)],
        compiler_params=_params(("arbitrary",)),
    )(h, p, o, proj, proj, wbp, wba, w_out)


def _mix_bwd(dh, p, o, proj, wbp, wba, w_out, after, *, tm, name):
    T, D = dh.shape
    tm = min(tm, T)

    def body(dh_ref, p_ref, o_ref, glp_ref, gls_ref, wbp_ref, wba_ref, wout_ref, after_ref,
             dyp_ref, dys_ref, dp_ref, do_ref, dgl_ref):
        halves = (pl.ds(0, tm // 2), pl.ds(tm // 2, tm // 2))
        wbp, wba, wout = wbp_ref[...], wba_ref[...], wout_ref[...]
        products = [(_mm_nt(dh_ref[rows, :].astype(BF16), wout), _mm_nt(p_ref[rows, :].astype(BF16), wbp),
                     _mm_nt(o_ref[rows, :].astype(BF16), wba)) for rows in halves]
        for rows, (dm, yp, ys) in zip(halves, products):
            gp = jax.nn.sigmoid(glp_ref[rows, :])
            gs = jax.nn.sigmoid(gls_ref[rows, :])
            dyp = (dm * gp).astype(BF16)
            dys = (dm * gs).astype(BF16)
            dyp_ref[rows, :] = dyp
            dys_ref[rows, :] = dys
            dgl_ref[rows, :D] = (dm * yp * gp * (1.0 - gp)).astype(BF16)
            dgl_ref[rows, D:] = (dm * ys * gs * (1.0 - gs)).astype(BF16)
            dp_ref[rows, :] = _mm(dyp, wbp)
            do_ref[rows, :] = _mm(dys, wba)

    row = lambda i: (i, 0)
    return pl.pallas_call(
        body, name=name, grid=(T // tm,),
        in_specs=_mix_specs(T, D, tm, wbp, w_out) + [AFTER],
        out_specs=[pl.BlockSpec((tm, D), row), pl.BlockSpec((tm, D), row), pl.BlockSpec((tm, POOL_WIDTH), row),
                   pl.BlockSpec((tm, SB_WIDTH), row), pl.BlockSpec((tm, 2 * D), row)],
        out_shape=[jax.ShapeDtypeStruct((T, D), BF16), jax.ShapeDtypeStruct((T, D), BF16),
                   jax.ShapeDtypeStruct((T, POOL_WIDTH), F32), jax.ShapeDtypeStruct((T, SB_WIDTH), F32),
                   jax.ShapeDtypeStruct((T, 2 * D), BF16)],
        compiler_params=_params(("arbitrary",)),
    )(dh, p, o, proj, proj, wbp, wba, w_out, _in_hbm(after))


def _adamw(w, g, m, v, *, name):
    R, C = w.shape
    tr = _row_tile(R, C)

    def body(w_ref, g_ref, m_ref, v_ref, d_ref, nm_ref, nv_ref):
        g_ = g_ref[...]
        m_ = ADAM_B1 * m_ref[...] + (1.0 - ADAM_B1) * g_
        v_ = ADAM_B2 * v_ref[...] + (1.0 - ADAM_B2) * (g_ * g_)
        m_hat = m_ / (1.0 - ADAM_B1 ** ADAM_STEP)
        v_hat = v_ / (1.0 - ADAM_B2 ** ADAM_STEP)
        d_ref[...] = -ADAM_LR * (m_hat / (jnp.sqrt(v_hat) + ADAM_EPS) + ADAM_WD * w_ref[...])
        nm_ref[...] = m_
        nv_ref[...] = v_

    spec = pl.BlockSpec((tr, C), lambda i: (i, 0))
    return pl.pallas_call(
        body, name=name, grid=(R // tr,), in_specs=[spec] * 4, out_specs=[spec] * 3,
        out_shape=[jax.ShapeDtypeStruct((R, C), F32)] * 3,
        compiler_params=_params(("arbitrary",)),
    )(w, g, m, v)


def _position():
    return lax.axis_index("x"), lax.axis_index("y"), lax.axis_index("c")


def _all_gather(shards, *, name, collective_id):
    n = len(shards)
    n_copies = 9

    def body(*refs):
        ins, outs = refs[:n], refs[n:2 * n]
        send_sems, recv_sems, local_sems = refs[2 * n:]
        x, y, c = _position()
        me, sibling = (x, y, c), (x, y, 1 - c)
        x_nbr, y_nbr, diagonal = (1 - x, y, c), (x, 1 - y, c), (1 - x, 1 - y, c)
        other = lambda pos: (pos[0], pos[1], 1 - c)

        barrier = pltpu.get_barrier_semaphore()
        for peer in (sibling, x_nbr, y_nbr):
            pl.semaphore_signal(barrier, inc=1, device_id=peer, device_id_type=MESH)
        pl.semaphore_wait(barrier, 3)

        def block(a, pos, half=None):
            ref = outs[a].at[4 * pos[0] + 2 * pos[1] + pos[2]]
            rows = ref.shape[0] // 2
            return ref if half is None else ref.at[pl.ds(half * rows, rows)]

        def copy(a, k, pos, to, half=None, src=None):
            return pltpu.make_async_remote_copy(
                src_ref=block(a, pos, half) if src is None else src, dst_ref=block(a, pos, half),
                send_sem=send_sems.at[n_copies * a + k], recv_sem=recv_sems.at[n_copies * a + k],
                device_id=to, device_id_type=MESH)

        started = []
        for a in range(n):
            mine = pltpu.make_async_copy(ins[a], block(a, me), local_sems.at[a])
            mine.start()
            started.append(mine)
        sends = []
        for a in range(n):
            sends += [copy(a, 1, me, x_nbr, src=ins[a]), copy(a, 2, me, y_nbr, src=ins[a]),
                      copy(a, 0, me, sibling, src=ins[a])]
        for cp in sends:
            cp.start()

        def pass_on(copies):
            for cp in copies:
                cp.start()
                sends.append(cp)

        for a in range(n):
            copy(a, 1, x_nbr, me).wait_recv()
            pass_on([copy(a, 5, x_nbr, y_nbr, half=0), copy(a, 3, x_nbr, sibling)])
            copy(a, 2, y_nbr, me).wait_recv()
            pass_on([copy(a, 6, y_nbr, x_nbr, half=1), copy(a, 4, y_nbr, sibling)])
        for a in range(n):
            copy(a, 5, diagonal, me, half=0).wait_recv()
            pass_on([copy(a, 7, diagonal, sibling, half=0)])
            copy(a, 6, diagonal, me, half=1).wait_recv()
            pass_on([copy(a, 8, diagonal, sibling, half=1)])
        for a in range(n):
            copy(a, 0, sibling, me).wait_recv()
            copy(a, 3, other(x_nbr), me).wait_recv()
            copy(a, 4, other(y_nbr), me).wait_recv()
            copy(a, 7, other(diagonal), me, half=0).wait_recv()
            copy(a, 8, other(diagonal), me, half=1).wait_recv()
        for cp in sends:
            cp.wait_send()
        for cp in started:
            cp.wait()

    return pl.kernel(
        body, name=name,
        out_type=[jax.ShapeDtypeStruct((N_DEV,) + s.shape, s.dtype) for s in shards],
        mesh=plsc.ScalarSubcoreMesh(axis_name="sequencer", num_cores=1),
        scratch_types=[pltpu.SemaphoreType.DMA((n_copies * n,)), pltpu.SemaphoreType.DMA((n_copies * n,)),
                       pltpu.SemaphoreType.DMA((n,))],
        compiler_params=pltpu.CompilerParams(collective_id=collective_id),
    )(*shards)


def _chip_sums(group, *, name):
    n = len(group)
    shapes = [g.shape[1:] for g in group]

    def body(*refs):
        g_refs, partials, out_refs = refs[:n], refs[n:3 * n:2], refs[n + 1:3 * n:2]
        mines, theirs = refs[3 * n:5 * n:2], refs[3 * n + 1:5 * n:2]
        send_sems, recv_sems, local_sems = refs[5 * n:]
        x, y, c = _position()
        my_chip = 2 * x + y

        def swap(a, s):
            return pltpu.make_async_remote_copy(
                src_ref=g_refs[a].at[2 * s + (1 - c)], dst_ref=theirs[a].at[s],
                send_sem=send_sems.at[4 * a + s], recv_sem=recv_sems.at[4 * a + s],
                device_id=(x, y, 1 - c), device_id_type=MESH)

        def load(a, s):
            return pltpu.make_async_copy(g_refs[a].at[2 * s + c], mines[a].at[s], local_sems.at[4 * a + s])

        for a in range(n):
            for s in range(4):
                swap(a, s).start()
                load(a, s).start()

        for a, (R, C) in enumerate(shapes):
            rc = 128 if R % 128 == 0 else R

            def chip_sum(chip, rows):
                return mines[a][chip, rows, :].astype(F32) + theirs[a][chip, rows, :].astype(F32)

            for s in range(4):
                load(a, s).wait()
                swap(a, s).wait_recv()

                @pl.when(s == my_chip)
                def _():
                    @pl.loop(0, R // rc)
                    def _(t):
                        rows = pl.ds(pl.multiple_of(t * rc, rc), rc)
                        out_refs[a][rows, :] = chip_sum(s, rows)

                @pl.when(s != my_chip)
                def _():
                    @pl.loop(0, R // rc)
                    def _(t):
                        rows = pl.ds(pl.multiple_of(t * rc, rc), rc)
                        partials[a][(s ^ my_chip) - 1, rows, :] = chip_sum(s, rows).astype(BF16)

        for a in range(n):
            for s in range(4):
                swap(a, s).wait_send()

    vmem = pl.BlockSpec(memory_space=pltpu.VMEM)
    outs = pl.pallas_call(
        body, name=name,
        in_specs=[pl.BlockSpec(memory_space=pl.ANY)] * n, out_specs=[vmem] * (2 * n),
        out_shape=[shape for R, C in shapes
                   for shape in (jax.ShapeDtypeStruct((3, R, C), BF16), jax.ShapeDtypeStruct((R, C), F32))],
        scratch_shapes=[pltpu.VMEM((4, R, C), BF16) for R, C in shapes for _ in range(2)] + [
            pltpu.SemaphoreType.DMA((4 * n,)), pltpu.SemaphoreType.DMA((4 * n,)), pltpu.SemaphoreType.DMA((4 * n,))],
        compiler_params=_params(),
    )(*group)
    return [(outs[2 * a], outs[2 * a + 1]) for a in range(n)]


def _cross_chips(partials, *, name, collective_id):
    n = len(partials)

    def body(*refs):
        ins, outs = refs[:n], refs[n:2 * n]
        send_sems, recv_sems = refs[2 * n:]
        x, y, c = _position()
        my_chip = 2 * x + y
        peers = [((my_chip ^ j) // 2, (my_chip ^ j) % 2, c) for j in (1, 2, 3)]

        barrier = pltpu.get_barrier_semaphore()
        for peer in peers:
            pl.semaphore_signal(barrier, inc=1, device_id=peer, device_id_type=MESH)
        pl.semaphore_wait(barrier, 3)

        copies = [
            pltpu.make_async_remote_copy(
                src_ref=ins[a].at[j], dst_ref=outs[a].at[j],
                send_sem=send_sems.at[3 * a + j], recv_sem=recv_sems.at[3 * a + j],
                device_id=peers[j], device_id_type=MESH)
            for a in range(n) for j in range(3)]
        for cp in copies:
            cp.start()
        for cp in copies:
            cp.wait_recv()
        for cp in copies:
            cp.wait_send()

    return pl.kernel(
        body, name=name,
        out_type=[jax.ShapeDtypeStruct(p.shape, p.dtype) for p in partials],
        mesh=plsc.ScalarSubcoreMesh(axis_name="sequencer", num_cores=1),
        scratch_types=[pltpu.SemaphoreType.DMA((3 * n,)), pltpu.SemaphoreType.DMA((3 * n,))],
        compiler_params=pltpu.CompilerParams(collective_id=collective_id),
    )(*partials)


def _cross_chips_and_gather(partials, slab, *, name, collective_id):
    n = len(partials)

    def body(*refs):
        part_refs, slab_ref = refs[:n], refs[n]
        landed_refs, slabs_ref = refs[n + 1:2 * n + 1], refs[2 * n + 1]
        send_sems, recv_sems, local_sem = refs[2 * n + 2:]
        x, y, c = _position()
        me, my_chip = 4 * x + 2 * y + c, 2 * x + y
        others = [me ^ k for k in range(1, N_DEV)]
        ids = [(o // 4, (o // 2) % 2, o % 2) for o in others]

        barrier = pltpu.get_barrier_semaphore()
        for peer in ids:
            pl.semaphore_signal(barrier, inc=1, device_id=peer, device_id_type=MESH)
        pl.semaphore_wait(barrier, N_DEV - 1)

        mine = pltpu.make_async_copy(slab_ref, slabs_ref.at[me], local_sem)
        mine.start()
        sends = [
            pltpu.make_async_remote_copy(
                src_ref=part_refs[a].at[j], dst_ref=landed_refs[a].at[j],
                send_sem=send_sems.at[3 * a + j], recv_sem=recv_sems.at[3 * a + j],
                device_id=((my_chip ^ (j + 1)) // 2, (my_chip ^ (j + 1)) % 2, c), device_id_type=MESH)
            for a in range(n) for j in range(3)]
        sends += [
            pltpu.make_async_remote_copy(
                src_ref=slab_ref, dst_ref=slabs_ref.at[me],
                send_sem=send_sems.at[3 * n + k], recv_sem=recv_sems.at[3 * n + k],
                device_id=ids[k], device_id_type=MESH)
            for k in range(N_DEV - 1)]
        arrivals = sends[:3 * n] + [
            pltpu.make_async_remote_copy(
                src_ref=slab_ref, dst_ref=slabs_ref.at[others[k]],
                send_sem=send_sems.at[3 * n + k], recv_sem=recv_sems.at[3 * n + k],
                device_id=ids[k], device_id_type=MESH)
            for k in range(N_DEV - 1)]
        for cp in sends:
            cp.start()
        for cp in arrivals:
            cp.wait_recv()
        for cp in sends:
            cp.wait_send()
        mine.wait()

    n_sems = 3 * n + N_DEV - 1
    outs = pl.kernel(
        body, name=name,
        out_type=[jax.ShapeDtypeStruct(p.shape, p.dtype) for p in partials]
                 + [jax.ShapeDtypeStruct((N_DEV,) + slab.shape, slab.dtype)],
        mesh=plsc.ScalarSubcoreMesh(axis_name="sequencer", num_cores=1),
        scratch_types=[pltpu.SemaphoreType.DMA((n_sems,)), pltpu.SemaphoreType.DMA((n_sems,)), pltpu.SemaphoreType.DMA],
        compiler_params=pltpu.CompilerParams(collective_id=collective_id),
    )(*partials, slab)
    return outs[:n], outs[n]


def _sum_devices(gathered, after, *, name):
    _, R, C = gathered.shape

    def body(in_ref, after_ref, out_ref):
        total = in_ref[0]
        for d in range(1, N_DEV):
            total = total + in_ref[d]
        out_ref[...] = total

    return pl.pallas_call(
        body, name=name, grid=(1,),
        in_specs=[pl.BlockSpec((N_DEV, R, C), lambda i: (0, 0, 0)), AFTER],
        out_specs=pl.BlockSpec((R, C), lambda i: (0, 0)),
        out_shape=jax.ShapeDtypeStruct((R, C), F32),
        compiler_params=_params(("arbitrary",)),
    )(gathered, _in_hbm(after))


def _owner_sum(own, landed, after, *, name):
    R, C = own.shape
    tr = _row_tile(R, C)

    def body(own_ref, landed_ref, after_ref, out_ref):
        total = own_ref[...]
        for j in range(3):
            total = total + landed_ref[j].astype(F32)
        out_ref[...] = total

    return pl.pallas_call(
        body, name=name, grid=(R // tr,),
        in_specs=[pl.BlockSpec((tr, C), lambda i: (i, 0)), pl.BlockSpec((3, tr, C), lambda i: (0, i, 0)), AFTER],
        out_specs=pl.BlockSpec((tr, C), lambda i: (i, 0)),
        out_shape=jax.ShapeDtypeStruct((R, C), F32),
        compiler_params=_params(("arbitrary",)),
    )(own, landed, _in_hbm(after))


def _local_step(x, target, norms, pool_w_group, pool_scale, wgu1, wd1, w_in, wbp, wba, w_out, wgu2, wd2, exchange):
    n1g, nmg, n2g, nfg = norms
    D = x.shape[1]
    gu1, hid1 = _ffn_up(x, n1g, wgu1, tm=1024, name="ffn1_up")
    h1 = _ffn_down(x, hid1, wd1, tm=512, name="ffn1_down")
    un, proj = _inproj_fwd(h1, nmg, w_in, tm=1024, name="inproj_fwd")
    p = _pool_fwd(proj, pool_w_group, pool_scale, name="pool_fwd")
    o, ltot = _attn_fwd(proj, name="attn_fwd")
    h2, m = _mix_fwd(h1, p, o, proj, wbp, wba, w_out, tm=512, name="mix_fwd")
    gu2, hid2 = _ffn_up(h2, n2g, wgu2, tm=1024, name="ffn2_up")
    h3 = _ffn_down(h2, hid2, wd2, tm=512, name="ffn2_down")
    dh3, df2, loss, d_nf = _loss_bwd(h3, target, nfg, tm=256, name="loss_bwd")

    dh2, d_n2, n2, dgu2 = _ffn_bwd(dh3, df2, h2, n2g, gu2, wgu2, wd2, df2, tm=512, name="ffn2_bwd")
    d_wd2 = _wgrad_down(hid2, df2, tk=WGRAD_TOKENS, name="ffn2_wgrad_down")
    d_wgu2 = _wgrad_gate_up(n2, dgu2, tk=WGRAD_TOKENS, name="ffn2_wgrad_gate_up")
    (g_wd2, g_wgu2), token = exchange("ffn2", [d_wd2.reshape(N_DEV, FF_SHARD_PAD, D), d_wgu2])

    dyp, dys, dp, do, dgl = _mix_bwd(dh2, p, o, proj, wbp, wba, w_out, token, tm=512, name="mix_bwd")
    d_wout = _wgrad_full(m, dh2, tk=WGRAD_TOKENS, name="wgrad_out")
    d_wbp = _wgrad_full(dyp, p, tk=WGRAD_TOKENS, name="wgrad_branch_pool")
    d_wba = _wgrad_full(dys, o, tk=WGRAD_TOKENS, name="wgrad_branch_attn")
    by_owner = lambda g: g.reshape(N_DEV, g.shape[0] // N_DEV, g.shape[1])
    (g_wbp, g_wba, g_wout), token = exchange("mix", [by_owner(d_wbp), by_owner(d_wba), by_owner(d_wout)])
    dxp, d_wgroup, d_scale = _pool_bwd(dp, proj, pool_w_group, pool_scale, name="pool_bwd")
    dq, dk, dv = _attn_bwd(proj, do, ltot, token, name="attn_bwd")
    dproj_parts = [dxp, dq, dk, dv, dgl]
    dh1, df1, d_nm = _inproj_bwd(dproj_parts, dh2, h1, nmg, w_in, tm=512, name="inproj_bwd")
    d_win = _wgrad_in(dproj_parts, un, name="wgrad_in")
    d_wd1 = _wgrad_down(hid1, df1, tk=WGRAD_TOKENS, name="ffn1_wgrad_down")
    (g_win, g_wd1, replicated_early), token = exchange(
        "w_in_ffn1_down", [d_win, d_wd1.reshape(N_DEV, FF_SHARD_PAD, D), d_nm, d_n2, d_nf, d_scale, d_wgroup, loss])

    dx, d_n1, n1, dgu1 = _ffn_bwd(dh1, df1, x, n1g, gu1, wgu1, wd1, token, tm=512, name="ffn1_bwd")
    d_wgu1_a = _wgrad_gate_up(n1, dgu1, tk=WGRAD_TOKENS, name="ffn1_wgrad_gate_up_a", part=0, parts=2)
    (g_wgu1_a, replicated_late), token = exchange("ffn1_gate_up_a", [d_wgu1_a, d_n1])
    d_wgu1_b = _wgrad_gate_up(n1, dgu1, tk=WGRAD_TOKENS, name="ffn1_wgrad_gate_up_b", part=1, parts=2)
    (g_wgu1_b,), token = exchange("last", [d_wgu1_b])
    g_wgu1 = (g_wgu1_a, g_wgu1_b)

    sharded = (g_wgu1, g_wd1, g_win, g_wbp, g_wba, g_wout, g_wgu2, g_wd2)
    return dx, sharded, (replicated_late, replicated_early), token


def _hidden_major(w):
    return jnp.swapaxes(w[0], 0, 1)


def _pad_gate_up(wt):
    d = wt.shape[1]
    wt = wt.astype(BF16).reshape(2, FF_SHARD, d)
    return jnp.pad(wt, ((0, 0), (0, FF_SHARD_PAD - FF_SHARD), (0, 0))).reshape(2 * FF_SHARD_PAD, d)


def _unpad_gate_up(gt):
    d = gt.shape[1]
    return gt.reshape(2, FF_SHARD_PAD, d)[:, :FF_SHARD].reshape(2 * FF_SHARD, d)


def _pad_down(w):
    return jnp.pad(w.astype(BF16), ((0, FF_SHARD_PAD - FF_SHARD), (0, 0)))


def kernel(x, ffn1_norm, ffn1_w_gate_up, ffn1_w_down, mix_norm, w_in, pool_w_group, pool_scale, w_branch_pool, w_branch_attn, w_out, ffn2_norm, ffn2_w_gate_up, ffn2_w_down, final_norm, loss_target, m_ffn1_norm, m_ffn1_w_gate_up, m_ffn1_w_down, m_mix_norm, m_w_in, m_pool_w_group, m_pool_scale, m_w_branch_pool, m_w_branch_attn, m_w_out, m_ffn2_norm, m_ffn2_w_gate_up, m_ffn2_w_down, m_final_norm, v_ffn1_norm, v_ffn1_w_gate_up, v_ffn1_w_down, v_mix_norm, v_w_in, v_pool_w_group, v_pool_scale, v_w_branch_pool, v_w_branch_attn, v_w_out, v_ffn2_norm, v_ffn2_w_gate_up, v_ffn2_w_down, v_final_norm):
    D = x.shape[-1]
    weights = dict(ffn1_norm=ffn1_norm, ffn1_w_gate_up=ffn1_w_gate_up, ffn1_w_down=ffn1_w_down, mix_norm=mix_norm,
                   w_in=w_in, pool_w_group=pool_w_group, pool_scale=pool_scale, w_branch_pool=w_branch_pool,
                   w_branch_attn=w_branch_attn, w_out=w_out, ffn2_norm=ffn2_norm, ffn2_w_gate_up=ffn2_w_gate_up,
                   ffn2_w_down=ffn2_w_down, final_norm=final_norm)
    first = dict(ffn1_norm=m_ffn1_norm, ffn1_w_gate_up=m_ffn1_w_gate_up, ffn1_w_down=m_ffn1_w_down,
                 mix_norm=m_mix_norm, w_in=m_w_in, pool_w_group=m_pool_w_group, pool_scale=m_pool_scale,
                 w_branch_pool=m_w_branch_pool, w_branch_attn=m_w_branch_attn, w_out=m_w_out,
                 ffn2_norm=m_ffn2_norm, ffn2_w_gate_up=m_ffn2_w_gate_up, ffn2_w_down=m_ffn2_w_down,
                 final_norm=m_final_norm)
    second = dict(ffn1_norm=v_ffn1_norm, ffn1_w_gate_up=v_ffn1_w_gate_up, ffn1_w_down=v_ffn1_w_down,
                  mix_norm=v_mix_norm, w_in=v_w_in, pool_w_group=v_pool_w_group, pool_scale=v_pool_scale,
                  w_branch_pool=v_w_branch_pool, w_branch_attn=v_w_branch_attn, w_out=v_w_out,
                  ffn2_norm=v_ffn2_norm, ffn2_w_gate_up=v_ffn2_w_gate_up, ffn2_w_down=v_ffn2_w_down,
                  final_norm=v_final_norm)
    order = list(weights)

    wgu1, = _all_gather([_pad_gate_up(_hidden_major(ffn1_w_gate_up))], name="all_gather_ffn1_gate_up", collective_id=0)
    wd1, = _all_gather([_pad_down(ffn1_w_down[0])], name="all_gather_ffn1_down", collective_id=10)
    transposed = lambda w: jnp.swapaxes(w[0], 0, 1).astype(BF16)
    win_g, = _all_gather([transposed(w_in)], name="all_gather_w_in", collective_id=1)
    wbp_g, wba_g = _all_gather([transposed(w_branch_pool), transposed(w_branch_attn)],
                               name="all_gather_branches", collective_id=2)
    wout_g, = _all_gather([w_out[0].astype(BF16)], name="all_gather_w_out", collective_id=11)
    wgu2, wd2 = _all_gather([_pad_gate_up(_hidden_major(ffn2_w_gate_up)), _pad_down(ffn2_w_down[0])],
                            name="all_gather_ffn2", collective_id=3)
    whole = lambda g: g.reshape(g.shape[0] * g.shape[1], g.shape[2])
    wd1, wd2, win_g, wbp_g, wba_g, wout_g = (whole(g) for g in (wd1, wd2, win_g, wbp_g, wba_g, wout_g))

    cross_ids = {"ffn2": 4, "mix": 5, "w_in_ffn1_down": 8, "ffn1_gate_up_a": 9, "last": 7}
    small = ["ffn1_norm", "mix_norm", "ffn2_norm", "final_norm", "pool_scale", "pool_w_group"]

    def tile_rows(a):
        a = a.reshape(-1, 128)
        return jnp.pad(a, ((0, -a.shape[0] % 8), (0, 0)))

    def exchange(tag, group):
        grads = [g for g in group if g.dtype == BF16]
        extras = [tile_rows(g) for g in group if g.dtype != BF16]
        sums = _chip_sums(grads, name="chip_sums_" + tag)
        partials = [s[0] for s in sums]
        handles = []
        if extras:
            landed, slabs = _cross_chips_and_gather(partials, jnp.concatenate(extras, axis=0),
                                                    name="cross_chips_" + tag, collective_id=cross_ids[tag])
            handles = [slabs]
        else:
            landed = _cross_chips(partials, name="cross_chips_" + tag, collective_id=cross_ids[tag])
        return [(s[1], l) for s, l in zip(sums, landed)] + handles, sums[-1][1]

    norms = (ffn1_norm, mix_norm, ffn2_norm, final_norm.reshape(1, D))
    dx, sharded, (slabs_late, slabs_early), last = _local_step(
        x[0], loss_target[0], norms, pool_w_group[0], pool_scale, wgu1, wd1, win_g, wbp_g, wba_g, wout_g, wgu2, wd2,
        exchange)
    names = ["ffn1_w_gate_up", "ffn1_w_down", "w_in", "w_branch_pool", "w_branch_attn", "w_out",
             "ffn2_w_gate_up", "ffn2_w_down"]
    handles = dict(zip(names, sharded))
    grads, delta, new_m, new_v = {}, {}, {}, {}
    after = last
    for k in ("ffn2_w_down", "ffn2_w_gate_up", "w_branch_pool", "w_branch_attn", "w_out", "w_in", "ffn1_w_down",
              "ffn1_w_gate_up"):
        hidden_major = k.endswith("w_gate_up")
        if isinstance(handles[k][0], tuple):
            first_half = _owner_sum(*handles[k][0], after, name="owner_sum_" + k + "_a")
            second_half = _owner_sum(*handles[k][1], first_half, name="owner_sum_" + k + "_b")
            g = jnp.concatenate([first_half[:FF_SHARD], second_half[:FF_SHARD]], axis=0)
        else:
            g = _owner_sum(*handles[k], after, name="owner_sum_" + k)
            if hidden_major:
                g = _unpad_gate_up(g)
            elif k in ("w_in", "w_branch_pool", "w_branch_attn"):
                g = jnp.swapaxes(g, 0, 1)
            else:
                g = g[:weights[k].shape[1]]
        view = _hidden_major if hidden_major else (lambda a: a[0])
        back = (lambda a: jnp.swapaxes(a, 0, 1)[None]) if hidden_major else (lambda a: a[None])
        out = _adamw(view(weights[k]), g, view(first[k]), view(second[k]), name="adamw_" + k)
        after = out[0]
        grads[k] = back(g)
        delta[k], new_m[k], new_v[k] = (back(a) for a in out)

    rows = [weights[k].size // 128 for k in small]
    padded_rows = [-(-r // 8) * 8 for r in rows]
    starts = [sum(padded_rows[:i]) for i in range(len(rows) + 1)]
    total = jnp.concatenate([_sum_devices(slabs_late, after, name="sum_replicated_late"),
                             _sum_devices(slabs_early, after, name="sum_replicated_early")], axis=0)
    loss_out = total[starts[-1], 0]
    small_w = jnp.concatenate([tile_rows(weights[k]) for k in small], axis=0)
    small_m = jnp.concatenate([tile_rows(first[k]) for k in small], axis=0)
    small_v = jnp.concatenate([tile_rows(second[k]) for k in small], axis=0)
    small_out = _adamw(small_w, total[:starts[-1]], small_m, small_v, name="adamw_replicated")
    for name_, start, n_rows in zip(small, starts, rows):
        shape = weights[name_].shape
        grads[name_] = total[start:start + n_rows].reshape(shape)
        delta[name_], new_m[name_], new_v[name_] = (a[start:start + n_rows].reshape(shape) for a in small_out)

    return (loss_out, dx[None], *[grads[k] for k in order], *[delta[k] for k in order],
            *[new_m[k] for k in order], *[new_v[k] for k in order])
```

```python
import jax
import jax.numpy as jnp
from jax import lax
from jax.experimental import pallas as pl
from jax.experimental.pallas import tpu as pltpu
from jax.experimental.pallas import tpu_sc as plsc

F32 = jnp.float32
BF16 = jnp.bfloat16
MESH = pl.DeviceIdType.MESH

RMS_EPS = 1e-6
N_DEV = 8
N_HEADS = 8
HEAD_DIM = 64
HEAD_PAIR = 2 * HEAD_DIM
POOL_WINDOWS = (2, 4, 8, 16)
POOL_GROUP = 128
POOL_WIDTH = 512
SB_WIDTH = 512
FF_SHARD = 352
FF_SHARD_PAD = 384
ATTN_K_BLOCK = 256
ATTN_Q_BLOCK_FWD = 512
ATTN_Q_BLOCK_BWD = 256
ATTN_SCALE = 0.125

ADAM_LR = 0.001
ADAM_B1 = 0.9
ADAM_B2 = 0.999
ADAM_EPS = 1e-08
ADAM_WD = 0.01
ADAM_STEP = 10

VMEM_LIMIT = 48 << 20
WGRAD_TOKENS = 2048


def _params(dims=None):
    return pltpu.CompilerParams(dimension_semantics=dims, vmem_limit_bytes=VMEM_LIMIT)


def _mm(a, b):
    return jnp.dot(a, b, preferred_element_type=F32)


def _mm_nt(a, b):
    return lax.dot_general(a, b, (((1,), (1,)), ((), ())), preferred_element_type=F32)


def _mm_tn(a, b):
    return lax.dot_general(a, b, (((0,), (0,)), ((), ())), preferred_element_type=F32)


def _row_tile(rows, cols):
    limit = max(8, (512 * 1024) // cols)
    return max(t for t in range(8, rows + 1, 8) if rows % t == 0 and (t <= limit or t == 8))


def _rstd(xf):
    return lax.rsqrt(jnp.mean(xf * xf, axis=-1, keepdims=True) + RMS_EPS)


def _rms_bwd(xf, gain, dn):
    r = _rstd(xf)
    xh = xf * r
    dgain = jnp.sum(dn * xh, axis=0, keepdims=True)
    dxh = dn * gain
    dx = r * (dxh - xh * jnp.mean(dxh * xh, axis=-1, keepdims=True))
    return dx, dgain


def _ffn_up(x, gain, wgu, *, tm, name):
    T, D = x.shape
    tm = min(tm, T)
    nb, bw = wgu.shape[0] // 2, wgu.shape[1]

    def body(x_ref, gain_ref, wg_ref, wu_ref, gu_ref, hid_ref, n_scr):
        @pl.when(pl.program_id(1) == 0)
        def _():
            xf = x_ref[...]
            n_scr[...] = (xf * _rstd(xf) * gain_ref[...]).astype(BF16)

        halves = (pl.ds(0, tm // 2), pl.ds(tm // 2, tm // 2))
        wg, wu = wg_ref[...], wu_ref[...]
        gus = [(_mm_nt(n_scr[rows, :], wg), _mm_nt(n_scr[rows, :], wu)) for rows in halves]
        for rows, (g, u) in zip(halves, gus):
            gu_ref[0, rows, :] = g.astype(BF16)
            gu_ref[1, rows, :] = u.astype(BF16)
            hid_ref[rows, :] = (g * jax.nn.sigmoid(g) * u).astype(BF16)

    return pl.pallas_call(
        body, name=name, grid=(T // tm, nb),
        in_specs=[
            pl.BlockSpec((tm, D), lambda i, j: (i, 0)),
            pl.BlockSpec((1, D), lambda i, j: (0, 0)),
            pl.BlockSpec((None, bw, D), lambda i, j: (j, 0, 0)),
            pl.BlockSpec((None, bw, D), lambda i, j: (j + nb, 0, 0)),
        ],
        out_specs=[
            pl.BlockSpec((2, tm, bw), lambda i, j: (0, i, j)),
            pl.BlockSpec((tm, bw), lambda i, j: (i, j)),
        ],
        out_shape=[jax.ShapeDtypeStruct((2, T, nb * bw), BF16), jax.ShapeDtypeStruct((T, nb * bw), BF16)],
        scratch_shapes=[pltpu.VMEM((tm, D), BF16)],
        compiler_params=_params(("arbitrary", "arbitrary")),
    )(x, gain, wgu, wgu)


def _ffn_down(x, hid, wd, *, tm, name):
    T, D = x.shape
    tm = min(tm, T)
    F = hid.shape[1]

    def body(x_ref, hid_ref, wd_ref, h_ref):
        h_ref[...] = x_ref[...] + 0.5 * _mm(hid_ref[...], wd_ref[...])

    return pl.pallas_call(
        body, name=name, grid=(T // tm,),
        in_specs=[
            pl.BlockSpec((tm, D), lambda i: (i, 0)),
            pl.BlockSpec((tm, F), lambda i: (i, 0)),
            pl.BlockSpec((F, D), lambda i: (0, 0)),
        ],
        out_specs=pl.BlockSpec((tm, D), lambda i: (i, 0)),
        out_shape=jax.ShapeDtypeStruct((T, D), F32),
        compiler_params=_params(("arbitrary",)),
    )(x, hid, wd)


AFTER = pl.BlockSpec(memory_space=pltpu.HBM)


def _in_hbm(token):
    return pltpu.with_memory_space_constraint(token, pltpu.HBM)


def _ffn_bwd(dh, df, x, gain, gu, wgu, wd, after, *, tm, name):
    T, D = x.shape
    tm = min(tm, T)
    nb, bw = wgu.shape[0] // 2, wgu.shape[1]

    def body(dh_ref, df_ref, x_ref, gain_ref, gu_ref, wg_ref, wu_ref, wd_ref, after_ref,
             dx_ref, dgain_ref, n_ref, dgu_ref, dn_acc):
        i, j = pl.program_id(0), pl.program_id(1)

        @pl.when(j == 0)
        def _():
            xf = x_ref[...]
            n_ref[...] = (xf * _rstd(xf) * gain_ref[...]).astype(BF16)
            dn_acc[...] = jnp.zeros_like(dn_acc)

        @pl.when((i == 0) & (j == 0))
        def _():
            dgain_ref[...] = jnp.zeros_like(dgain_ref)

        halves = (pl.ds(0, tm // 2), pl.ds(tm // 2, tm // 2))
        wd, wg, wu = wd_ref[...], wg_ref[...], wu_ref[...]
        dhids = [_mm_nt(df_ref[rows, :], wd) for rows in halves]
        for rows, dhid in zip(halves, dhids):
            g = gu_ref[0, rows, :].astype(F32)
            u = gu_ref[1, rows, :].astype(F32)
            s = jax.nn.sigmoid(g)
            silu = g * s
            dg = (dhid * u * (s * (1.0 + g * (1.0 - s)))).astype(BF16)
            du = (dhid * silu).astype(BF16)
            dgu_ref[0, rows, :] = dg
            dgu_ref[1, rows, :] = du
            dn_acc[rows, :] += _mm(dg, wg) + _mm(du, wu)

        @pl.when(j == nb - 1)
        def _():
            dx, dgain = _rms_bwd(x_ref[...], gain_ref[...], dn_acc[...])
            dx_ref[...] = dh_ref[...] + dx
            dgain_ref[...] += dgain

    row = lambda i, j: (i, 0)
    return pl.pallas_call(
        body, name=name, grid=(T // tm, nb),
        in_specs=[
            pl.BlockSpec((tm, D), row),
            pl.BlockSpec((tm, D), row),
            pl.BlockSpec((tm, D), row),
            pl.BlockSpec((1, D), lambda i, j: (0, 0)),
            pl.BlockSpec((2, tm, bw), lambda i, j: (0, i, j)),
            pl.BlockSpec((None, bw, D), lambda i, j: (j, 0, 0)),
            pl.BlockSpec((None, bw, D), lambda i, j: (j + nb, 0, 0)),
            pl.BlockSpec((bw, D), lambda i, j: (j, 0)),
            AFTER,
        ],
        out_specs=[
            pl.BlockSpec((tm, D), row),
            pl.BlockSpec((1, D), lambda i, j: (0, 0)),
            pl.BlockSpec((tm, D), row),
            pl.BlockSpec((2, tm, bw), lambda i, j: (0, i, j)),
        ],
        out_shape=[
            jax.ShapeDtypeStruct((T, D), F32),
            jax.ShapeDtypeStruct((1, D), F32),
            jax.ShapeDtypeStruct((T, D), BF16),
            jax.ShapeDtypeStruct((2, T, nb * bw), BF16),
        ],
        scratch_shapes=[pltpu.VMEM((tm, D), F32)],
        compiler_params=_params(("arbitrary", "arbitrary")),
    )(dh, df, x, gain, gu, wgu, wgu, wd, _in_hbm(after))


def _wgrad(a, b, *, grid, a_spec, b_spec, out_spec, out_shape, acc_shape, name):
    nk = grid[2]

    def body(a_ref, b_ref, o_ref, acc):
        k = pl.program_id(2)

        @pl.when(k == 0)
        def _():
            acc[...] = jnp.zeros_like(acc)

        acc[...] += _mm_tn(a_ref[...].astype(BF16), b_ref[...].astype(BF16))

        @pl.when(k == nk - 1)
        def _():
            o_ref[...] = acc[...].astype(o_ref.dtype)

    return pl.pallas_call(
        body, name=name, grid=grid, in_specs=[a_spec, b_spec], out_specs=out_spec,
        out_shape=jax.ShapeDtypeStruct(out_shape, BF16),
        scratch_shapes=[pltpu.VMEM(acc_shape, F32)],
        compiler_params=_params(("arbitrary", "arbitrary", "arbitrary")),
    )(a, b)


def _wgrad_gate_up(n, dgu, *, tk, name, part=0, parts=1):
    T, D = n.shape
    tk = min(tk, T)
    owner_rows = FF_SHARD_PAD * 2
    nb = dgu.shape[2] // owner_rows
    bw = owner_rows // parts
    return _wgrad(
        dgu, n, grid=(2 * nb, 1, T // tk), name=name,
        a_spec=pl.BlockSpec((None, tk, bw), lambda m, c, k: (m // nb, k, parts * (m % nb) + part)),
        b_spec=pl.BlockSpec((tk, D), lambda m, c, k: (k, 0)),
        out_spec=pl.BlockSpec((None, bw, D), lambda m, c, k: (m, 0, 0)),
        out_shape=(2 * nb, bw, D), acc_shape=(bw, D))


def _wgrad_down(hid, df, *, tk, name):
    T, D = df.shape
    tk = min(tk, T)
    bw = FF_SHARD_PAD * 2
    nb = hid.shape[1] // bw
    return _wgrad(
        hid, df, grid=(nb, 1, T // tk), name=name,
        a_spec=pl.BlockSpec((tk, bw), lambda m, c, k: (k, m)),
        b_spec=pl.BlockSpec((tk, D), lambda m, c, k: (k, 0)),
        out_spec=pl.BlockSpec((bw, D), lambda m, c, k: (m, 0)),
        out_shape=(nb * bw, D), acc_shape=(bw, D))


def _wgrad_in(dparts, un, *, name):
    T, D = un.shape
    bw = sum(p.shape[1] for p in dparts) // N_DEV
    first = [sum(p.shape[1] for p in dparts[:i]) // bw for i in range(len(dparts) + 1)]

    def body(*refs):
        dp_refs, un_ref, o_ref = refs[:-2], refs[-2], refs[-1]
        m = pl.program_id(0)
        for dp_ref, lo, hi in zip(dp_refs, first[:-1], first[1:]):
            @pl.when((m >= lo) & (m < hi))
            def _():
                o_ref[...] = _mm_tn(dp_ref[...], un_ref[...]).astype(o_ref.dtype)

    def piece_spec(lo, hi):
        return pl.BlockSpec((T, bw), lambda m: (0, jnp.clip(m - lo, 0, hi - lo - 1)))

    return pl.pallas_call(
        body, name=name, grid=(N_DEV,),
        in_specs=[piece_spec(lo, hi) for lo, hi in zip(first[:-1], first[1:])] + [pl.BlockSpec((T, D), lambda m: (0, 0))],
        out_specs=pl.BlockSpec((None, bw, D), lambda m: (m, 0, 0)),
        out_shape=jax.ShapeDtypeStruct((N_DEV, bw, D), BF16),
        compiler_params=_params(("arbitrary",)),
    )(*dparts, un)


def _wgrad_full(a, b, *, tk, name):
    T, M = a.shape
    tk = min(tk, T)
    N = b.shape[1]
    return _wgrad(
        a, b, grid=(1, 1, T // tk), name=name,
        a_spec=pl.BlockSpec((tk, M), lambda m, c, k: (k, 0)),
        b_spec=pl.BlockSpec((tk, N), lambda m, c, k: (k, 0)),
        out_spec=pl.BlockSpec((M, N), lambda m, c, k: (0, 0)), out_shape=(M, N), acc_shape=(M, N))


def _loss_bwd(h, target, gain, *, tm, name):
    T, D = h.shape
    tm = min(tm, T)

    def body(h_ref, t_ref, gain_ref, dh_ref, df_ref, loss_ref, dgain_ref):
        @pl.when(pl.program_id(0) == 0)
        def _():
            loss_ref[...] = jnp.zeros_like(loss_ref)
            dgain_ref[...] = jnp.zeros_like(dgain_ref)

        xf = h_ref[...]
        gain = gain_ref[...]
        err = xf * _rstd(xf) * gain - t_ref[...]
        loss_ref[...] += 0.5 * jnp.sum(jnp.mean(err * err, axis=-1, keepdims=True), axis=0, keepdims=True)
        dx, dgain = _rms_bwd(xf, gain, err * (1.0 / D))
        dh_ref[...] = dx
        df_ref[...] = (0.5 * dx).astype(BF16)
        dgain_ref[...] += dgain

    row = lambda i: (i, 0)
    fixed = lambda i: (0, 0)
    return pl.pallas_call(
        body, name=name, grid=(T // tm,),
        in_specs=[pl.BlockSpec((tm, D), row), pl.BlockSpec((tm, D), row), pl.BlockSpec((1, D), fixed)],
        out_specs=[pl.BlockSpec((tm, D), row), pl.BlockSpec((tm, D), row), pl.BlockSpec((1, 128), fixed),
                   pl.BlockSpec((1, D), fixed)],
        out_shape=[jax.ShapeDtypeStruct((T, D), F32), jax.ShapeDtypeStruct((T, D), BF16),
                   jax.ShapeDtypeStruct((1, 128), F32), jax.ShapeDtypeStruct((1, D), F32)],
        compiler_params=_params(("arbitrary",)),
    )(h, target, gain)


def _inproj_fwd(h, gain, w_in_t, *, tm, name):
    T, D = h.shape
    tm = min(tm, T)
    bn = D
    nb = w_in_t.shape[0] // bn

    def body(h_ref, gain_ref, wt_ref, un_ref, proj_ref):
        @pl.when(pl.program_id(1) == 0)
        def _():
            xf = h_ref[...]
            un_ref[...] = (xf * _rstd(xf) * gain_ref[...]).astype(BF16)

        proj_ref[...] = _mm_nt(un_ref[...], wt_ref[...])

    return pl.pallas_call(
        body, name=name, grid=(T // tm, nb),
        in_specs=[
            pl.BlockSpec((tm, D), lambda i, j: (i, 0)),
            pl.BlockSpec((1, D), lambda i, j: (0, 0)),
            pl.BlockSpec((bn, D), lambda i, j: (j, 0)),
        ],
        out_specs=[pl.BlockSpec((tm, D), lambda i, j: (i, 0)), pl.BlockSpec((tm, bn), lambda i, j: (i, j))],
        out_shape=[jax.ShapeDtypeStruct((T, D), BF16), jax.ShapeDtypeStruct((T, nb * bn), F32)],
        compiler_params=_params(("arbitrary", "arbitrary")),
    )(h, gain, w_in_t)


def _inproj_bwd(dparts, dh, h, gain, w_in_t, after, *, tm, name):
    T, D = h.shape
    tm = min(tm, T)
    n = len(dparts)
    widths = [p.shape[1] for p in dparts]
    starts = [sum(widths[:i]) for i in range(n)]

    def body(*refs):
        dp_refs = refs[:n]
        dh_ref, h_ref, gain_ref, wt_ref, after_ref, dx_ref, df_ref, dgain_ref = refs[n:]

        @pl.when(pl.program_id(0) == 0)
        def _():
            dgain_ref[...] = jnp.zeros_like(dgain_ref)

        dn = sum(_mm(dp_ref[...], wt_ref[start:start + width, :])
                 for dp_ref, start, width in zip(dp_refs, starts, widths))
        dx, dgain = _rms_bwd(h_ref[...], gain_ref[...], dn)
        dh_in = dh_ref[...] + dx
        dx_ref[...] = dh_in
        df_ref[...] = (0.5 * dh_in).astype(BF16)
        dgain_ref[...] += dgain

    row = lambda i: (i, 0)
    fixed = lambda i: (0, 0)
    return pl.pallas_call(
        body, name=name, grid=(T // tm,),
        in_specs=[pl.BlockSpec((tm, width), row) for width in widths] + [
            pl.BlockSpec((tm, D), row),
            pl.BlockSpec((tm, D), row),
            pl.BlockSpec((1, D), fixed),
            pl.BlockSpec(w_in_t.shape, fixed),
            AFTER,
        ],
        out_specs=[pl.BlockSpec((tm, D), row), pl.BlockSpec((tm, D), row), pl.BlockSpec((1, D), fixed)],
        out_shape=[jax.ShapeDtypeStruct((T, D), F32), jax.ShapeDtypeStruct((T, D), BF16),
                   jax.ShapeDtypeStruct((1, D), F32)],
        compiler_params=_params(("arbitrary",)),
    )(*dparts, dh, h, gain, w_in_t, _in_hbm(after))


def _window_sum(x, row, doublings, *, backward):
    T = x.shape[0]
    s = x
    for k in range(doublings):
        sh = 1 << k
        if backward:
            s = s + jnp.where(row < T - sh, pltpu.roll(s, T - sh, 0), 0.0)
        else:
            s = s + jnp.where(row >= sh, pltpu.roll(s, sh, 0), 0.0)
    return s


def _pool_fwd(proj, w_group, scale, *, name):
    T = proj.shape[0]

    def body(xp_ref, w_ref, scale_ref, p_ref):
        row = lax.broadcasted_iota(jnp.int32, (T, POOL_GROUP), 0)
        for gi, window in enumerate(POOL_WINDOWS):
            cols = slice(gi * POOL_GROUP, (gi + 1) * POOL_GROUP)
            x = xp_ref[:, cols]
            inv_count = 1.0 / jnp.minimum(row + 1, window).astype(F32)
            yc = _window_sum(x, row, gi + 1, backward=False) * inv_count - x
            pre = _mm(yc.astype(BF16), w_ref[gi].astype(BF16))
            p_ref[:, cols] = pre * scale_ref[:, cols]

    return pl.pallas_call(
        body, name=name, grid=(1,),
        in_specs=[
            pl.BlockSpec((T, POOL_WIDTH), lambda i: (0, 0)),
            pl.BlockSpec(w_group.shape, lambda i: (0, 0, 0)),
            pl.BlockSpec((1, POOL_WIDTH), lambda i: (0, 0)),
        ],
        out_specs=pl.BlockSpec((T, POOL_WIDTH), lambda i: (0, 0)),
        out_shape=jax.ShapeDtypeStruct((T, POOL_WIDTH), F32),
        compiler_params=_params(("arbitrary",)),
    )(proj, w_group, scale)


def _pool_bwd(dp, proj, w_group, scale, *, name):
    T = proj.shape[0]

    def body(dp_ref, xp_ref, w_ref, scale_ref, dxp_ref, dw_ref, dscale_ref):
        row = lax.broadcasted_iota(jnp.int32, (T, POOL_GROUP), 0)
        for gi, window in enumerate(POOL_WINDOWS):
            cols = slice(gi * POOL_GROUP, (gi + 1) * POOL_GROUP)
            x = xp_ref[:, cols]
            inv_count = 1.0 / jnp.minimum(row + 1, window).astype(F32)
            yc = (_window_sum(x, row, gi + 1, backward=False) * inv_count - x).astype(BF16)
            w = w_ref[gi].astype(BF16)
            pre = _mm(yc, w)
            dpg = dp_ref[:, cols]
            dscale_ref[:, cols] = jnp.sum(dpg * pre, axis=0, keepdims=True)
            dpre = (dpg * scale_ref[:, cols]).astype(BF16)
            dw_ref[gi] = _mm_tn(yc, dpre)
            dyc = _mm_nt(dpre, w)
            dxp_ref[:, cols] = (_window_sum(dyc * inv_count, row, gi + 1, backward=True) - dyc).astype(BF16)

    return pl.pallas_call(
        body, name=name, grid=(1,),
        in_specs=[
            pl.BlockSpec((T, POOL_WIDTH), lambda i: (0, 0)),
            pl.BlockSpec((T, POOL_WIDTH), lambda i: (0, 0)),
            pl.BlockSpec(w_group.shape, lambda i: (0, 0, 0)),
            pl.BlockSpec((1, POOL_WIDTH), lambda i: (0, 0)),
        ],
        out_specs=[
            pl.BlockSpec((T, POOL_WIDTH), lambda i: (0, 0)),
            pl.BlockSpec(w_group.shape, lambda i: (0, 0, 0)),
            pl.BlockSpec((1, POOL_WIDTH), lambda i: (0, 0)),
        ],
        out_shape=[jax.ShapeDtypeStruct((T, POOL_WIDTH), BF16), jax.ShapeDtypeStruct(w_group.shape, F32),
                   jax.ShapeDtypeStruct((1, POOL_WIDTH), F32)],
        compiler_params=_params(("arbitrary",)),
    )(dp, proj, w_group, scale)


ATTN_STRIP = 32


def _log_sigmoids(z):
    lb = jnp.minimum(z, 0.0) - jnp.log(1.0 + jnp.exp(-jnp.abs(z)))
    return lb, lb - z


def _transposed_blocks(x_ref, blocks_scr, tq):
    for b in range(blocks_scr.shape[0]):
        blocks_scr[b] = x_ref[b * tq:(b + 1) * tq, :].T.astype(BF16)


def _split_bf16(x):
    hi = x.astype(BF16)
    return hi, (x - hi.astype(F32)).astype(BF16)


def _strips(n):
    return [slice(i, i + ATTN_STRIP) for i in range(0, n, ATTN_STRIP)]


def _rows(parts):
    return jnp.concatenate(parts, axis=0)


def _attn_specs(T, tq):
    q_col = POOL_WIDTH // HEAD_PAIR
    k_col = q_col + SB_WIDTH // HEAD_PAIR
    v_col = k_col + SB_WIDTH // HEAD_PAIR
    return [
        pl.BlockSpec((tq, HEAD_PAIR), lambda p, i: (i, q_col + p)),
        pl.BlockSpec((T, HEAD_PAIR), lambda p, i: (0, k_col + p)),
        pl.BlockSpec((T, HEAD_PAIR), lambda p, i: (0, v_col + p)),
    ]


def _attn_fwd(proj, *, name):
    T = proj.shape[0]
    tk = min(ATTN_K_BLOCK, T)
    tq = min(ATTN_Q_BLOCK_FWD, T)
    diagonal_blocks = tq // tk

    def body(q_ref, k_ref, v_ref, o_ref, lt_ref, kt_scr, vb_scr):
        qi = pl.program_id(1)

        @pl.when(qi == 0)
        def _():
            _transposed_blocks(k_ref, kt_scr, tk)
            vb_scr[...] = v_ref[...].astype(BF16)

        head0 = lax.broadcasted_iota(jnp.int32, (tq, HEAD_PAIR), 1) < HEAD_DIM
        q = q_ref[...] * ATTN_SCALE
        qs = (jnp.where(head0, q, 0.0).astype(BF16), jnp.where(head0, 0.0, q).astype(BF16))
        r = lax.broadcasted_iota(jnp.int32, (tq, tk), 0)
        c = lax.broadcasted_iota(jnp.int32, (tq, tk), 1)
        later = (r[:tk] > c[:tk]).astype(BF16)
        later2 = _rows([later, later])
        causal = lambda d: (lambda rows: c[rows] + d * tk < r[rows])
        strips = _strips(tq)

        def log_terms(z, valid):
            lbs, his, los, sums = [], [], [], []
            for rows in strips:
                lb, lm = _log_sigmoids(z[rows])
                if valid is not None:
                    lm = jnp.where(valid(rows), lm, 0.0)
                hi, lo = _split_bf16(lm)
                lbs.append(lb)
                his.append(hi)
                los.append(lo)
                sums.append(jnp.sum(lm, axis=1, keepdims=True))
            return lbs, jnp.concatenate([_rows(his), _rows(los)], axis=1), _rows(sums)

        def weights(lbs, run, after, valid):
            parts = []
            for rows, lb in zip(strips, lbs):
                a = jnp.exp(lb + run[rows] + after[rows])
                if valid is not None:
                    a = jnp.where(valid(rows), a, 0.0)
                parts.append(a.astype(BF16))
            return _rows(parts)

        def block(kj, carry, valid):
            kt = kt_scr[kj]
            vb = vb_scr[pl.ds(pl.multiple_of(kj * tk, tk), tk), :]
            run0, o0, run1, o1 = carry
            z0 = _mm(qs[0], kt)
            z1 = _mm(qs[1], kt)
            lbs0, split0, sums0 = log_terms(z0, valid)
            after0 = _mm(split0, later2)
            lbs1, split1, sums1 = log_terms(z1, valid)
            after1 = _mm(split1, later2)
            o0 = o0 + _mm(weights(lbs0, run0, after0, valid), vb)
            o1 = o1 + _mm(weights(lbs1, run1, after1, valid), vb)
            return run0 + sums0, o0, run1 + sums1, o1

        zero = (jnp.zeros((tq, 1), F32), jnp.zeros((tq, HEAD_PAIR), F32))
        first = diagonal_blocks * qi
        carry = zero + zero
        for d in reversed(range(diagonal_blocks)):
            carry = block(first + d, carry, causal(d))
        carry = lax.fori_loop(0, first, lambda it, cr: block(first - 1 - it, cr, None), carry)
        o_ref[...] = jnp.where(head0, carry[1], carry[3])
        lt_ref[...] = jnp.where(head0, carry[0], carry[2])

    out_spec = pl.BlockSpec((tq, HEAD_PAIR), lambda p, i: (i, p))
    return pl.pallas_call(
        body, name=name, grid=(N_HEADS // 2, T // tq),
        in_specs=_attn_specs(T, tq), out_specs=[out_spec, out_spec],
        out_shape=[jax.ShapeDtypeStruct((T, SB_WIDTH), F32), jax.ShapeDtypeStruct((T, SB_WIDTH), F32)],
        scratch_shapes=[pltpu.VMEM((T // tk, HEAD_PAIR, tk), BF16), pltpu.VMEM((T, HEAD_PAIR), BF16)],
        compiler_params=_params(("arbitrary", "arbitrary")),
    )(proj, proj, proj)


def _attn_bwd(proj, do, ltot, after, *, name):
    T = proj.shape[0]
    tk = min(ATTN_K_BLOCK, T)
    tq = min(ATTN_Q_BLOCK_BWD, T)
    diagonal_blocks = tq // tk

    def body(q_ref, k_ref, v_ref, do_ref, lt_ref, after_ref, dq_ref, dk_ref, dv_ref,
             kb_scr, kt_scr, vt_scr, dkt_ref, dvt_ref):
        qi = pl.program_id(1)

        @pl.when(qi == 0)
        def _():
            kb_scr[...] = k_ref[...].astype(BF16)
            _transposed_blocks(k_ref, kt_scr, tk)
            _transposed_blocks(v_ref, vt_scr, tk)
            dkt_ref[...] = jnp.zeros_like(dkt_ref)
            dvt_ref[...] = jnp.zeros_like(dvt_ref)

        head0 = lax.broadcasted_iota(jnp.int32, (tq, HEAD_PAIR), 1) < HEAD_DIM
        q, do_, lt = q_ref[...] * ATTN_SCALE, do_ref[...], lt_ref[...]
        qs = (jnp.where(head0, q, 0.0).astype(BF16), jnp.where(head0, 0.0, q).astype(BF16))
        q_heads = (jnp.where(head0, q, 0.0), jnp.where(head0, 0.0, q))
        do_heads = (jnp.where(head0, do_, 0.0), jnp.where(head0, 0.0, do_))
        dos = tuple(d.astype(BF16) for d in do_heads)
        qts = tuple(x.T.astype(BF16) for x in q_heads)
        dots = tuple(d.T.astype(BF16) for d in do_heads)
        lts = (jnp.max(jnp.where(head0, lt, -jnp.inf), axis=1, keepdims=True),
               jnp.max(jnp.where(head0, -jnp.inf, lt), axis=1, keepdims=True))
        r = lax.broadcasted_iota(jnp.int32, (tq, tk), 0)
        c = lax.broadcasted_iota(jnp.int32, (tq, tk), 1)
        upto = (r[:tk] <= c[:tk]).astype(BF16)
        before = (r[:tk] < c[:tk]).astype(BF16)
        upto2, before2 = _rows([upto, upto]), _rows([before, before])
        causal = lambda d: (lambda rows: c[rows] + d * tk < r[rows])
        strips = _strips(tq)

        def log_terms(z, valid):
            lbs, his, los, sums = [], [], [], []
            for rows in strips:
                lb, lm = _log_sigmoids(z[rows])
                if valid is not None:
                    lm = jnp.where(valid(rows), lm, 0.0)
                hi, lo = _split_bf16(lm)
                lbs.append(lb)
                his.append(hi)
                los.append(lo)
                sums.append(jnp.sum(lm, axis=1, keepdims=True))
            return lbs, jnp.concatenate([_rows(his), _rows(los)], axis=1), _rows(sums)

        def weights(lbs, rest, lm_upto, da, valid):
            a_parts, es, his, los, sums = [], [], [], [], []
            for rows, lb in zip(strips, lbs):
                a = jnp.exp(lb + (rest[rows] - lm_upto[rows]))
                if valid is not None:
                    a = jnp.where(valid(rows), a, 0.0)
                e = da[rows] * a
                hi, lo = _split_bf16(e)
                a_parts.append(a.astype(BF16))
                es.append(e)
                his.append(hi)
                los.append(lo)
                sums.append(jnp.sum(e, axis=1, keepdims=True))
            return _rows(a_parts), es, jnp.concatenate([_rows(his), _rows(los)], axis=1), _rows(sums)

        def score_grads(lbs, es, run_e, e_before, valid):
            parts = []
            for rows, lb, e in zip(strips, lbs, es):
                beta = jnp.exp(lb)
                dz = e * (1.0 - beta) - (run_e[rows] + e_before[rows]) * beta
                if valid is not None:
                    dz = jnp.where(valid(rows), dz, 0.0)
                parts.append(dz.astype(BF16))
            return _rows(parts)

        def block(kj, carry, valid):
            off = pl.multiple_of(kj * tk, tk)
            kb, kt, vt = kb_scr[pl.ds(off, tk), :], kt_scr[kj], vt_scr[kj]
            run_lm0, run_e0, dq0, run_lm1, run_e1, dq1 = carry
            z0, da0 = _mm(qs[0], kt), _mm(dos[0], vt)
            z1, da1 = _mm(qs[1], kt), _mm(dos[1], vt)
            lbs0, split0, lm_sums0 = log_terms(z0, valid)
            lm_upto0 = _mm(split0, upto2)
            lbs1, split1, lm_sums1 = log_terms(z1, valid)
            lm_upto1 = _mm(split1, upto2)
            a0, es0, split0, e_sums0 = weights(lbs0, lts[0] - run_lm0, lm_upto0, da0, valid)
            e_before0 = _mm(split0, before2)
            a1, es1, split1, e_sums1 = weights(lbs1, lts[1] - run_lm1, lm_upto1, da1, valid)
            e_before1 = _mm(split1, before2)
            dz0 = score_grads(lbs0, es0, run_e0, e_before0, valid)
            dkt_blk = _mm(qts[0], dz0)
            dvt_blk = _mm(dots[0], a0)
            dq0 = dq0 + _mm(dz0, kb)
            dz1 = score_grads(lbs1, es1, run_e1, e_before1, valid)
            dkt_ref[kj] += dkt_blk + _mm(qts[1], dz1)
            dvt_ref[kj] += dvt_blk + _mm(dots[1], a1)
            dq1 = dq1 + _mm(dz1, kb)
            return run_lm0 + lm_sums0, run_e0 + e_sums0, dq0, run_lm1 + lm_sums1, run_e1 + e_sums1, dq1

        zero = (jnp.zeros((tq, 1), F32), jnp.zeros((tq, 1), F32), jnp.zeros((tq, HEAD_PAIR), F32))
        first = diagonal_blocks * qi
        carry = lax.fori_loop(0, first, lambda kj, cr: block(kj, cr, None), zero + zero)
        for d in range(diagonal_blocks):
            carry = block(first + d, carry, causal(d))
        dq_ref[...] = (jnp.where(head0, carry[2], carry[5]) * ATTN_SCALE).astype(BF16)

        @pl.when(qi == T // tq - 1)
        def _():
            for b in range(T // tk):
                dk_ref[b * tk:(b + 1) * tk, :] = dkt_ref[b].T.astype(BF16)
                dv_ref[b * tk:(b + 1) * tk, :] = dvt_ref[b].T.astype(BF16)

    blk = pl.BlockSpec((tq, HEAD_PAIR), lambda p, i: (i, p))
    seq = pl.BlockSpec((T, HEAD_PAIR), lambda p, i: (0, p))
    transposed = pltpu.VMEM((T // tk, HEAD_PAIR, tk), F32)
    return pl.pallas_call(
        body, name=name, grid=(N_HEADS // 2, T // tq),
        in_specs=_attn_specs(T, tq) + [blk, blk, AFTER], out_specs=[blk, seq, seq],
        out_shape=[jax.ShapeDtypeStruct((T, SB_WIDTH), BF16)] * 3,
        scratch_shapes=[pltpu.VMEM((T, HEAD_PAIR), BF16), pltpu.VMEM((T // tk, HEAD_PAIR, tk), BF16),
                        pltpu.VMEM((T // tk, HEAD_PAIR, tk), BF16), transposed, transposed],
        compiler_params=_params(("arbitrary", "arbitrary")),
    )(proj, proj, proj, do, ltot, _in_hbm(after))


def _mix_specs(T, D, tm, wbp, w_out):
    gate_col = (POOL_WIDTH + 3 * SB_WIDTH) // D
    row = lambda i: (i, 0)
    return [
        pl.BlockSpec((tm, D), row),
        pl.BlockSpec((tm, POOL_WIDTH), row),
        pl.BlockSpec((tm, SB_WIDTH), row),
        pl.BlockSpec((tm, D), lambda i: (i, gate_col)),
        pl.BlockSpec((tm, D), lambda i: (i, gate_col + 1)),
        pl.BlockSpec(wbp.shape, lambda i: (0, 0)),
        pl.BlockSpec(wbp.shape, lambda i: (0, 0)),
        pl.BlockSpec(w_out.shape, lambda i: (0, 0)),
    ]


def _mix_fwd(h, p, o, proj, wbp, wba, w_out, *, tm, name):
    T, D = h.shape
    tm = min(tm, T)

    def body(h_ref, p_ref, o_ref, glp_ref, gls_ref, wbp_ref, wba_ref, wout_ref, hout_ref, m_ref):
        halves = (pl.ds(0, tm // 2), pl.ds(tm // 2, tm // 2))
        wbp, wba, wout = wbp_ref[...], wba_ref[...], wout_ref[...]
        branches = [(_mm_nt(p_ref[rows, :].astype(BF16), wbp), _mm_nt(o_ref[rows, :].astype(BF16), wba))
                    for rows in halves]
        for rows, (yp, ys) in zip(halves, branches):
            m = (jax.nn.sigmoid(glp_ref[rows, :]) * yp + jax.nn.sigmoid(gls_ref[rows, :]) * ys).astype(BF16)
            m_ref[rows, :] = m
            hout_ref[rows, :] = h_ref[rows, :] + _mm(m, wout)

    row = lambda i: (i, 0)
    return pl.pallas_call(
        body, name=name, grid=(T // tm,),
        in_specs=_mix_specs(T, D, tm, wbp, w_out),
        out_specs=[pl.BlockSpec((tm, D), row), pl.BlockSpec((tm, D), row)],
        out_shape=[jax.ShapeDtypeStruct((T, D), F32), jax.ShapeDtypeStruct((T, D), BF16)],
        compiler_params=_params(("arbitrary",)),
    )(h, p, o, proj, proj, wbp, wba, w_out)


def _mix_bwd(dh, p, o, proj, wbp, wba, w_out, after, *, tm, name):
    T, D = dh.shape
    tm = min(tm, T)

    def body(dh_ref, p_ref, o_ref, glp_ref, gls_ref, wbp_ref, wba_ref, wout_ref, after_ref,
             dyp_ref, dys_ref, dp_ref, do_ref, dgl_ref):
        halves = (pl.ds(0, tm // 2), pl.ds(tm // 2, tm // 2))
        wbp, wba, wout = wbp_ref[...], wba_ref[...], wout_ref[...]
        products = [(_mm_nt(dh_ref[rows, :].astype(BF16), wout), _mm_nt(p_ref[rows, :].astype(BF16), wbp),
                     _mm_nt(o_ref[rows, :].astype(BF16), wba)) for rows in halves]
        for rows, (dm, yp, ys) in zip(halves, products):
            gp = jax.nn.sigmoid(glp_ref[rows, :])
            gs = jax.nn.sigmoid(gls_ref[rows, :])
            dyp = (dm * gp).astype(BF16)
            dys = (dm * gs).astype(BF16)
            dyp_ref[rows, :] = dyp
            dys_ref[rows, :] = dys
            dgl_ref[rows, :D] = (dm * yp * gp * (1.0 - gp)).astype(BF16)
            dgl_ref[rows, D:] = (dm * ys * gs * (1.0 - gs)).astype(BF16)
            dp_ref[rows, :] = _mm(dyp, wbp)
            do_ref[rows, :] = _mm(dys, wba)

    row = lambda i: (i, 0)
    return pl.pallas_call(
        body, name=name, grid=(T // tm,),
        in_specs=_mix_specs(T, D, tm, wbp, w_out) + [AFTER],
        out_specs=[pl.BlockSpec((tm, D), row), pl.BlockSpec((tm, D), row), pl.BlockSpec((tm, POOL_WIDTH), row),
                   pl.BlockSpec((tm, SB_WIDTH), row), pl.BlockSpec((tm, 2 * D), row)],
        out_shape=[jax.ShapeDtypeStruct((T, D), BF16), jax.ShapeDtypeStruct((T, D), BF16),
                   jax.ShapeDtypeStruct((T, POOL_WIDTH), F32), jax.ShapeDtypeStruct((T, SB_WIDTH), F32),
                   jax.ShapeDtypeStruct((T, 2 * D), BF16)],
        compiler_params=_params(("arbitrary",)),
    )(dh, p, o, proj, proj, wbp, wba, w_out, _in_hbm(after))


def _adamw(w, g, m, v, *, name):
    R, C = w.shape
    tr = _row_tile(R, C)

    def body(w_ref, g_ref, m_ref, v_ref, d_ref, nm_ref, nv_ref):
        g_ = g_ref[...]
        m_ = ADAM_B1 * m_ref[...] + (1.0 - ADAM_B1) * g_
        v_ = ADAM_B2 * v_ref[...] + (1.0 - ADAM_B2) * (g_ * g_)
        m_hat = m_ / (1.0 - ADAM_B1 ** ADAM_STEP)
        v_hat = v_ / (1.0 - ADAM_B2 ** ADAM_STEP)
        d_ref[...] = -ADAM_LR * (m_hat / (jnp.sqrt(v_hat) + ADAM_EPS) + ADAM_WD * w_ref[...])
        nm_ref[...] = m_
        nv_ref[...] = v_

    spec = pl.BlockSpec((tr, C), lambda i: (i, 0))
    return pl.pallas_call(
        body, name=name, grid=(R // tr,), in_specs=[spec] * 4, out_specs=[spec] * 3,
        out_shape=[jax.ShapeDtypeStruct((R, C), F32)] * 3,
        compiler_params=_params(("arbitrary",)),
    )(w, g, m, v)


def _position():
    return lax.axis_index("x"), lax.axis_index("y"), lax.axis_index("c")


def _all_gather(shards, *, name, collective_id):
    n = len(shards)
    n_copies = 9

    def body(*refs):
        ins, outs = refs[:n], refs[n:2 * n]
        send_sems, recv_sems, local_sems = refs[2 * n:]
        x, y, c = _position()
        me, sibling = (x, y, c), (x, y, 1 - c)
        x_nbr, y_nbr, diagonal = (1 - x, y, c), (x, 1 - y, c), (1 - x, 1 - y, c)
        other = lambda pos: (pos[0], pos[1], 1 - c)

        barrier = pltpu.get_barrier_semaphore()
        for peer in (sibling, x_nbr, y_nbr):
            pl.semaphore_signal(barrier, inc=1, device_id=peer, device_id_type=MESH)
        pl.semaphore_wait(barrier, 3)

        def block(a, pos, half=None):
            ref = outs[a].at[4 * pos[0] + 2 * pos[1] + pos[2]]
            rows = ref.shape[0] // 2
            return ref if half is None else ref.at[pl.ds(half * rows, rows)]

        def copy(a, k, pos, to, half=None, src=None):
            return pltpu.make_async_remote_copy(
                src_ref=block(a, pos, half) if src is None else src, dst_ref=block(a, pos, half),
                send_sem=send_sems.at[n_copies * a + k], recv_sem=recv_sems.at[n_copies * a + k],
                device_id=to, device_id_type=MESH)

        started = []
        for a in range(n):
            mine = pltpu.make_async_copy(ins[a], block(a, me), local_sems.at[a])
            mine.start()
            started.append(mine)
        sends = []
        for a in range(n):
            sends += [copy(a, 1, me, x_nbr, src=ins[a]), copy(a, 2, me, y_nbr, src=ins[a]),
                      copy(a, 0, me, sibling, src=ins[a])]
        for cp in sends:
            cp.start()

        def pass_on(copies):
            for cp in copies:
                cp.start()
                sends.append(cp)

        for a in range(n):
            copy(a, 1, x_nbr, me).wait_recv()
            pass_on([copy(a, 5, x_nbr, y_nbr, half=0), copy(a, 3, x_nbr, sibling)])
            copy(a, 2, y_nbr, me).wait_recv()
            pass_on([copy(a, 6, y_nbr, x_nbr, half=1), copy(a, 4, y_nbr, sibling)])
        for a in range(n):
            copy(a, 5, diagonal, me, half=0).wait_recv()
            pass_on([copy(a, 7, diagonal, sibling, half=0)])
            copy(a, 6, diagonal, me, half=1).wait_recv()
            pass_on([copy(a, 8, diagonal, sibling, half=1)])
        for a in range(n):
            copy(a, 0, sibling, me).wait_recv()
            copy(a, 3, other(x_nbr), me).wait_recv()
            copy(a, 4, other(y_nbr), me).wait_recv()
            copy(a, 7, other(diagonal), me, half=0).wait_recv()
            copy(a, 8, other(diagonal), me, half=1).wait_recv()
        for cp in sends:
            cp.wait_send()
        for cp in started:
            cp.wait()

    return pl.kernel(
        body, name=name,
        out_type=[jax.ShapeDtypeStruct((N_DEV,) + s.shape, s.dtype) for s in shards],
        mesh=plsc.ScalarSubcoreMesh(axis_name="sequencer", num_cores=1),
        scratch_types=[pltpu.SemaphoreType.DMA((n_copies * n,)), pltpu.SemaphoreType.DMA((n_copies * n,)),
                       pltpu.SemaphoreType.DMA((n,))],
        compiler_params=pltpu.CompilerParams(collective_id=collective_id),
    )(*shards)


def _chip_sums(group, *, name):
    n = len(group)
    shapes = [g.shape[1:] for g in group]

    def body(*refs):
        g_refs, partials, out_refs = refs[:n], refs[n:3 * n:2], refs[n + 1:3 * n:2]
        mines, theirs = refs[3 * n:5 * n:2], refs[3 * n + 1:5 * n:2]
        send_sems, recv_sems, local_sems = refs[5 * n:]
        x, y, c = _position()
        my_chip = 2 * x + y

        def swap(a, s):
            return pltpu.make_async_remote_copy(
                src_ref=g_refs[a].at[2 * s + (1 - c)], dst_ref=theirs[a].at[s],
                send_sem=send_sems.at[4 * a + s], recv_sem=recv_sems.at[4 * a + s],
                device_id=(x, y, 1 - c), device_id_type=MESH)

        def load(a, s):
            return pltpu.make_async_copy(g_refs[a].at[2 * s + c], mines[a].at[s], local_sems.at[4 * a + s])

        for a in range(n):
            for s in range(4):
                swap(a, s).start()
                load(a, s).start()

        for a, (R, C) in enumerate(shapes):
            rc = 128 if R % 128 == 0 else R

            def chip_sum(chip, rows):
                return mines[a][chip, rows, :].astype(F32) + theirs[a][chip, rows, :].astype(F32)

            for s in range(4):
                load(a, s).wait()
                swap(a, s).wait_recv()

                @pl.when(s == my_chip)
                def _():
                    @pl.loop(0, R // rc)
                    def _(t):
                        rows = pl.ds(pl.multiple_of(t * rc, rc), rc)
                        out_refs[a][rows, :] = chip_sum(s, rows)

                @pl.when(s != my_chip)
                def _():
                    @pl.loop(0, R // rc)
                    def _(t):
                        rows = pl.ds(pl.multiple_of(t * rc, rc), rc)
                        partials[a][(s ^ my_chip) - 1, rows, :] = chip_sum(s, rows).astype(BF16)

        for a in range(n):
            for s in range(4):
                swap(a, s).wait_send()

    vmem = pl.BlockSpec(memory_space=pltpu.VMEM)
    outs = pl.pallas_call(
        body, name=name,
        in_specs=[pl.BlockSpec(memory_space=pl.ANY)] * n, out_specs=[vmem] * (2 * n),
        out_shape=[shape for R, C in shapes
                   for shape in (jax.ShapeDtypeStruct((3, R, C), BF16), jax.ShapeDtypeStruct((R, C), F32))],
        scratch_shapes=[pltpu.VMEM((4, R, C), BF16) for R, C in shapes for _ in range(2)] + [
            pltpu.SemaphoreType.DMA((4 * n,)), pltpu.SemaphoreType.DMA((4 * n,)), pltpu.SemaphoreType.DMA((4 * n,))],
        compiler_params=_params(),
    )(*group)
    return [(outs[2 * a], outs[2 * a + 1]) for a in range(n)]


def _cross_chips(partials, *, name, collective_id):
    n = len(partials)

    def body(*refs):
        ins, outs = refs[:n], refs[n:2 * n]
        send_sems, recv_sems = refs[2 * n:]
        x, y, c = _position()
        my_chip = 2 * x + y
        peers = [((my_chip ^ j) // 2, (my_chip ^ j) % 2, c) for j in (1, 2, 3)]

        barrier = pltpu.get_barrier_semaphore()
        for peer in peers:
            pl.semaphore_signal(barrier, inc=1, device_id=peer, device_id_type=MESH)
        pl.semaphore_wait(barrier, 3)

        copies = [
            pltpu.make_async_remote_copy(
                src_ref=ins[a].at[j], dst_ref=outs[a].at[j],
                send_sem=send_sems.at[3 * a + j], recv_sem=recv_sems.at[3 * a + j],
                device_id=peers[j], device_id_type=MESH)
            for a in range(n) for j in range(3)]
        for cp in copies:
            cp.start()
        for cp in copies:
            cp.wait_recv()
        for cp in copies:
            cp.wait_send()

    return pl.kernel(
        body, name=name,
        out_type=[jax.ShapeDtypeStruct(p.shape, p.dtype) for p in partials],
        mesh=plsc.ScalarSubcoreMesh(axis_name="sequencer", num_cores=1),
        scratch_types=[pltpu.SemaphoreType.DMA((3 * n,)), pltpu.SemaphoreType.DMA((3 * n,))],
        compiler_params=pltpu.CompilerParams(collective_id=collective_id),
    )(*partials)


def _cross_chips_and_gather(partials, slab, *, name, collective_id):
    n = len(partials)

    def body(*refs):
        part_refs, slab_ref = refs[:n], refs[n]
        landed_refs, slabs_ref = refs[n + 1:2 * n + 1], refs[2 * n + 1]
        send_sems, recv_sems, local_sem = refs[2 * n + 2:]
        x, y, c = _position()
        me, my_chip = 4 * x + 2 * y + c, 2 * x + y
        others = [me ^ k for k in range(1, N_DEV)]
        ids = [(o // 4, (o // 2) % 2, o % 2) for o in others]

        barrier = pltpu.get_barrier_semaphore()
        for peer in ids:
            pl.semaphore_signal(barrier, inc=1, device_id=peer, device_id_type=MESH)
        pl.semaphore_wait(barrier, N_DEV - 1)

        mine = pltpu.make_async_copy(slab_ref, slabs_ref.at[me], local_sem)
        mine.start()
        sends = [
            pltpu.make_async_remote_copy(
                src_ref=part_refs[a].at[j], dst_ref=landed_refs[a].at[j],
                send_sem=send_sems.at[3 * a + j], recv_sem=recv_sems.at[3 * a + j],
                device_id=((my_chip ^ (j + 1)) // 2, (my_chip ^ (j + 1)) % 2, c), device_id_type=MESH)
            for a in range(n) for j in range(3)]
        sends += [
            pltpu.make_async_remote_copy(
                src_ref=slab_ref, dst_ref=slabs_ref.at[me],
                send_sem=send_sems.at[3 * n + k], recv_sem=recv_sems.at[3 * n + k],
                device_id=ids[k], device_id_type=MESH)
            for k in range(N_DEV - 1)]
        arrivals = sends[:3 * n] + [
            pltpu.make_async_remote_copy(
                src_ref=slab_ref, dst_ref=slabs_ref.at[others[k]],
                send_sem=send_sems.at[3 * n + k], recv_sem=recv_sems.at[3 * n + k],
                device_id=ids[k], device_id_type=MESH)
            for k in range(N_DEV - 1)]
        for cp in sends:
            cp.start()
        for cp in arrivals:
            cp.wait_recv()
        for cp in sends:
            cp.wait_send()
        mine.wait()

    n_sems = 3 * n + N_DEV - 1
    outs = pl.kernel(
        body, name=name,
        out_type=[jax.ShapeDtypeStruct(p.shape, p.dtype) for p in partials]
                 + [jax.ShapeDtypeStruct((N_DEV,) + slab.shape, slab.dtype)],
        mesh=plsc.ScalarSubcoreMesh(axis_name="sequencer", num_cores=1),
        scratch_types=[pltpu.SemaphoreType.DMA((n_sems,)), pltpu.SemaphoreType.DMA((n_sems,)), pltpu.SemaphoreType.DMA],
        compiler_params=pltpu.CompilerParams(collective_id=collective_id),
    )(*partials, slab)
    return outs[:n], outs[n]


def _sum_devices(gathered, after, *, name):
    _, R, C = gathered.shape

    def body(in_ref, after_ref, out_ref):
        total = in_ref[0]
        for d in range(1, N_DEV):
            total = total + in_ref[d]
        out_ref[...] = total

    return pl.pallas_call(
        body, name=name, grid=(1,),
        in_specs=[pl.BlockSpec((N_DEV, R, C), lambda i: (0, 0, 0)), AFTER],
        out_specs=pl.BlockSpec((R, C), lambda i: (0, 0)),
        out_shape=jax.ShapeDtypeStruct((R, C), F32),
        compiler_params=_params(("arbitrary",)),
    )(gathered, _in_hbm(after))


def _owner_sum(own, landed, after, *, name):
    R, C = own.shape
    tr = _row_tile(R, C)

    def body(own_ref, landed_ref, after_ref, out_ref):
        total = own_ref[...]
        for j in range(3):
            total = total + landed_ref[j].astype(F32)
        out_ref[...] = total

    return pl.pallas_call(
        body, name=name, grid=(R // tr,),
        in_specs=[pl.BlockSpec((tr, C), lambda i: (i, 0)), pl.BlockSpec((3, tr, C), lambda i: (0, i, 0)), AFTER],
        out_specs=pl.BlockSpec((tr, C), lambda i: (i, 0)),
        out_shape=jax.ShapeDtypeStruct((R, C), F32),
        compiler_params=_params(("arbitrary",)),
    )(own, landed, _in_hbm(after))


def _local_step(x, target, norms, pool_w_group, pool_scale, wgu1, wd1, w_in, wbp, wba, w_out, wgu2, wd2, exchange):
    n1g, nmg, n2g, nfg = norms
    D = x.shape[1]
    gu1, hid1 = _ffn_up(x, n1g, wgu1, tm=1024, name="ffn1_up")
    h1 = _ffn_down(x, hid1, wd1, tm=512, name="ffn1_down")
    un, proj = _inproj_fwd(h1, nmg, w_in, tm=1024, name="inproj_fwd")
    p = _pool_fwd(proj, pool_w_group, pool_scale, name="pool_fwd")
    o, ltot = _attn_fwd(proj, name="attn_fwd")
    h2, m = _mix_fwd(h1, p, o, proj, wbp, wba, w_out, tm=512, name="mix_fwd")
    gu2, hid2 = _ffn_up(h2, n2g, wgu2, tm=1024, name="ffn2_up")
    h3 = _ffn_down(h2, hid2, wd2, tm=512, name="ffn2_down")
    dh3, df2, loss, d_nf = _loss_bwd(h3, target, nfg, tm=256, name="loss_bwd")

    dh2, d_n2, n2, dgu2 = _ffn_bwd(dh3, df2, h2, n2g, gu2, wgu2, wd2, df2, tm=512, name="ffn2_bwd")
    d_wd2 = _wgrad_down(hid2, df2, tk=WGRAD_TOKENS, name="ffn2_wgrad_down")
    d_wgu2 = _wgrad_gate_up(n2, dgu2, tk=WGRAD_TOKENS, name="ffn2_wgrad_gate_up")
    (g_wd2, g_wgu2), token = exchange("ffn2", [d_wd2.reshape(N_DEV, FF_SHARD_PAD, D), d_wgu2])

    dyp, dys, dp, do, dgl = _mix_bwd(dh2, p, o, proj, wbp, wba, w_out, token, tm=512, name="mix_bwd")
    d_wout = _wgrad_full(m, dh2, tk=WGRAD_TOKENS, name="wgrad_out")
    d_wbp = _wgrad_full(dyp, p, tk=WGRAD_TOKENS, name="wgrad_branch_pool")
    d_wba = _wgrad_full(dys, o, tk=WGRAD_TOKENS, name="wgrad_branch_attn")
    by_owner = lambda g: g.reshape(N_DEV, g.shape[0] // N_DEV, g.shape[1])
    (g_wbp, g_wba, g_wout), token = exchange("mix", [by_owner(d_wbp), by_owner(d_wba), by_owner(d_wout)])
    dxp, d_wgroup, d_scale = _pool_bwd(dp, proj, pool_w_group, pool_scale, name="pool_bwd")
    dq, dk, dv = _attn_bwd(proj, do, ltot, token, name="attn_bwd")
    dproj_parts = [dxp, dq, dk, dv, dgl]
    d_win = _wgrad_in(dproj_parts, un, name="wgrad_in")
    (g_win,), token = exchange("w_in", [d_win])
    dh1, df1, d_nm = _inproj_bwd(dproj_parts, dh2, h1, nmg, w_in, token, tm=512, name="inproj_bwd")
    d_wd1 = _wgrad_down(hid1, df1, tk=WGRAD_TOKENS, name="ffn1_wgrad_down")
    (g_wd1, replicated_early), token = exchange(
        "ffn1_down", [d_wd1.reshape(N_DEV, FF_SHARD_PAD, D), d_nm, d_n2, d_nf, d_scale, d_wgroup, loss])

    dx, d_n1, n1, dgu1 = _ffn_bwd(dh1, df1, x, n1g, gu1, wgu1, wd1, token, tm=512, name="ffn1_bwd")
    d_wgu1_a = _wgrad_gate_up(n1, dgu1, tk=WGRAD_TOKENS, name="ffn1_wgrad_gate_up_a", part=0, parts=2)
    (g_wgu1_a, replicated_late), token = exchange("ffn1_gate_up_a", [d_wgu1_a, d_n1])
    d_wgu1_b = _wgrad_gate_up(n1, dgu1, tk=WGRAD_TOKENS, name="ffn1_wgrad_gate_up_b", part=1, parts=2)
    (g_wgu1_b,), token = exchange("last", [d_wgu1_b])
    g_wgu1 = (g_wgu1_a, g_wgu1_b)

    sharded = (g_wgu1, g_wd1, g_win, g_wbp, g_wba, g_wout, g_wgu2, g_wd2)
    return dx, sharded, (replicated_late, replicated_early), token


def _hidden_major(w):
    return jnp.swapaxes(w[0], 0, 1)


def _pad_gate_up(wt):
    d = wt.shape[1]
    wt = wt.astype(BF16).reshape(2, FF_SHARD, d)
    return jnp.pad(wt, ((0, 0), (0, FF_SHARD_PAD - FF_SHARD), (0, 0))).reshape(2 * FF_SHARD_PAD, d)


def _unpad_gate_up(gt):
    d = gt.shape[1]
    return gt.reshape(2, FF_SHARD_PAD, d)[:, :FF_SHARD].reshape(2 * FF_SHARD, d)


def _pad_down(w):
    return jnp.pad(w.astype(BF16), ((0, FF_SHARD_PAD - FF_SHARD), (0, 0)))


def kernel(x, ffn1_norm, ffn1_w_gate_up, ffn1_w_down, mix_norm, w_in, pool_w_group, pool_scale, w_branch_pool, w_branch_attn, w_out, ffn2_norm, ffn2_w_gate_up, ffn2_w_down, final_norm, loss_target, m_ffn1_norm, m_ffn1_w_gate_up, m_ffn1_w_down, m_mix_norm, m_w_in, m_pool_w_group, m_pool_scale, m_w_branch_pool, m_w_branch_attn, m_w_out, m_ffn2_norm, m_ffn2_w_gate_up, m_ffn2_w_down, m_final_norm, v_ffn1_norm, v_ffn1_w_gate_up, v_ffn1_w_down, v_mix_norm, v_w_in, v_pool_w_group, v_pool_scale, v_w_branch_pool, v_w_branch_attn, v_w_out, v_ffn2_norm, v_ffn2_w_gate_up, v_ffn2_w_down, v_final_norm):
    D = x.shape[-1]
    weights = dict(ffn1_norm=ffn1_norm, ffn1_w_gate_up=ffn1_w_gate_up, ffn1_w_down=ffn1_w_down, mix_norm=mix_norm,
                   w_in=w_in, pool_w_group=pool_w_group, pool_scale=pool_scale, w_branch_pool=w_branch_pool,
                   w_branch_attn=w_branch_attn, w_out=w_out, ffn2_norm=ffn2_norm, ffn2_w_gate_up=ffn2_w_gate_up,
                   ffn2_w_down=ffn2_w_down, final_norm=final_norm)
    first = dict(ffn1_norm=m_ffn1_norm, ffn1_w_gate_up=m_ffn1_w_gate_up, ffn1_w_down=m_ffn1_w_down,
                 mix_norm=m_mix_norm, w_in=m_w_in, pool_w_group=m_pool_w_group, pool_scale=m_pool_scale,
                 w_branch_pool=m_w_branch_pool, w_branch_attn=m_w_branch_attn, w_out=m_w_out,
                 ffn2_norm=m_ffn2_norm, ffn2_w_gate_up=m_ffn2_w_gate_up, ffn2_w_down=m_ffn2_w_down,
                 final_norm=m_final_norm)
    second = dict(ffn1_norm=v_ffn1_norm, ffn1_w_gate_up=v_ffn1_w_gate_up, ffn1_w_down=v_ffn1_w_down,
                  mix_norm=v_mix_norm, w_in=v_w_in, pool_w_group=v_pool_w_group, pool_scale=v_pool_scale,
                  w_branch_pool=v_w_branch_pool, w_branch_attn=v_w_branch_attn, w_out=v_w_out,
                  ffn2_norm=v_ffn2_norm, ffn2_w_gate_up=v_ffn2_w_gate_up, ffn2_w_down=v_ffn2_w_down,
                  final_norm=v_final_norm)
    order = list(weights)

    wgu1, = _all_gather([_pad_gate_up(_hidden_major(ffn1_w_gate_up))], name="all_gather_ffn1_gate_up", collective_id=0)
    wd1, = _all_gather([_pad_down(ffn1_w_down[0])], name="all_gather_ffn1_down", collective_id=10)
    transposed = lambda w: jnp.swapaxes(w[0], 0, 1).astype(BF16)
    win_g, = _all_gather([transposed(w_in)], name="all_gather_w_in", collective_id=1)
    wbp_g, wba_g = _all_gather([transposed(w_branch_pool), transposed(w_branch_attn)],
                               name="all_gather_branches", collective_id=2)
    wout_g, = _all_gather([w_out[0].astype(BF16)], name="all_gather_w_out", collective_id=11)
    wgu2, wd2 = _all_gather([_pad_gate_up(_hidden_major(ffn2_w_gate_up)), _pad_down(ffn2_w_down[0])],
                            name="all_gather_ffn2", collective_id=3)
    whole = lambda g: g.reshape(g.shape[0] * g.shape[1], g.shape[2])
    wd1, wd2, win_g, wbp_g, wba_g, wout_g = (whole(g) for g in (wd1, wd2, win_g, wbp_g, wba_g, wout_g))

    cross_ids = {"ffn2": 4, "mix": 5, "w_in": 6, "ffn1_down": 8, "ffn1_gate_up_a": 9, "last": 7}
    small = ["ffn1_norm", "mix_norm", "ffn2_norm", "final_norm", "pool_scale", "pool_w_group"]

    def tile_rows(a):
        a = a.reshape(-1, 128)
        return jnp.pad(a, ((0, -a.shape[0] % 8), (0, 0)))

    def exchange(tag, group):
        grads = [g for g in group if g.dtype == BF16]
        extras = [tile_rows(g) for g in group if g.dtype != BF16]
        sums = _chip_sums(grads, name="chip_sums_" + tag)
        partials = [s[0] for s in sums]
        handles = []
        if extras:
            landed, slabs = _cross_chips_and_gather(partials, jnp.concatenate(extras, axis=0),
                                                    name="cross_chips_" + tag, collective_id=cross_ids[tag])
            handles = [slabs]
        else:
            landed = _cross_chips(partials, name="cross_chips_" + tag, collective_id=cross_ids[tag])
        return [(s[1], l) for s, l in zip(sums, landed)] + handles, sums[-1][1]

    norms = (ffn1_norm, mix_norm, ffn2_norm, final_norm.reshape(1, D))
    dx, sharded, (slabs_late, slabs_early), last = _local_step(
        x[0], loss_target[0], norms, pool_w_group[0], pool_scale, wgu1, wd1, win_g, wbp_g, wba_g, wout_g, wgu2, wd2,
        exchange)
    names = ["ffn1_w_gate_up", "ffn1_w_down", "w_in", "w_branch_pool", "w_branch_attn", "w_out",
             "ffn2_w_gate_up", "ffn2_w_down"]
    handles = dict(zip(names, sharded))
    grads, delta, new_m, new_v = {}, {}, {}, {}
    after = last
    for k in ("ffn2_w_down", "ffn2_w_gate_up", "w_branch_pool", "w_branch_attn", "w_out", "w_in", "ffn1_w_down",
              "ffn1_w_gate_up"):
        hidden_major = k.endswith("w_gate_up")
        if isinstance(handles[k][0], tuple):
            first_half = _owner_sum(*handles[k][0], after, name="owner_sum_" + k + "_a")
            second_half = _owner_sum(*handles[k][1], first_half, name="owner_sum_" + k + "_b")
            g = jnp.concatenate([first_half[:FF_SHARD], second_half[:FF_SHARD]], axis=0)
        else:
            g = _owner_sum(*handles[k], after, name="owner_sum_" + k)
            if hidden_major:
                g = _unpad_gate_up(g)
            elif k in ("w_in", "w_branch_pool", "w_branch_attn"):
                g = jnp.swapaxes(g, 0, 1)
            else:
                g = g[:weights[k].shape[1]]
        view = _hidden_major if hidden_major else (lambda a: a[0])
        back = (lambda a: jnp.swapaxes(a, 0, 1)[None]) if hidden_major else (lambda a: a[None])
        out = _adamw(view(weights[k]), g, view(first[k]), view(second[k]), name="adamw_" + k)
        after = out[0]
        grads[k] = back(g)
        delta[k], new_m[k], new_v[k] = (back(a) for a in out)

    rows = [weights[k].size // 128 for k in small]
    padded_rows = [-(-r // 8) * 8 for r in rows]
    starts = [sum(padded_rows[:i]) for i in range(len(rows) + 1)]
    total = jnp.concatenate([_sum_devices(slabs_late, after, name="sum_replicated_late"),
                             _sum_devices(slabs_early, after, name="sum_replicated_early")], axis=0)
    loss_out = total[starts[-1], 0]
    small_w = jnp.concatenate([tile_rows(weights[k]) for k in small], axis=0)
    small_m = jnp.concatenate([tile_rows(first[k]) for k in small], axis=0)
    small_v = jnp.concatenate([tile_rows(second[k]) for k in small], axis=0)
    small_out = _adamw(small_w, total[:starts[-1]], small_m, small_v, name="adamw_replicated")
    for name_, start, n_rows in zip(small, starts, rows):
        shape = weights[name_].shape
        grads[name_] = total[start:start + n_rows].reshape(shape)
        delta[name_], new_m[name_], new_v[name_] = (a[start:start + n_rows].reshape(shape) for a in small_out)

    return (loss_out, dx[None], *[grads[k] for k in order], *[delta[k] for k in order],
            *[new_m[k] for k in order], *[new_v[k] for k in order])
```

```python
import jax
import jax.numpy as jnp
from jax import lax
from jax.experimental import pallas as pl
from jax.experimental.pallas import tpu as pltpu
from jax.experimental.pallas import tpu_sc as plsc

F32 = jnp.float32
BF16 = jnp.bfloat16
MESH = pl.DeviceIdType.MESH

RMS_EPS = 1e-6
N_DEV = 8
N_HEADS = 8
HEAD_DIM = 64
HEAD_PAIR = 2 * HEAD_DIM
POOL_WINDOWS = (2, 4, 8, 16)
POOL_GROUP = 128
POOL_WIDTH = 512
SB_WIDTH = 512
FF_SHARD = 352
FF_SHARD_PAD = 384
ATTN_K_BLOCK = 256
ATTN_Q_BLOCK_FWD = 512
ATTN_Q_BLOCK_BWD = 256
ATTN_SCALE = 0.125

ADAM_LR = 0.001
ADAM_B1 = 0.9
ADAM_B2 = 0.999
ADAM_EPS = 1e-08
ADAM_WD = 0.01
ADAM_STEP = 10

VMEM_LIMIT = 48 << 20
WGRAD_TOKENS = 2048


def _params(dims=None):
    return pltpu.CompilerParams(dimension_semantics=dims, vmem_limit_bytes=VMEM_LIMIT)


def _mm(a, b):
    return jnp.dot(a, b, preferred_element_type=F32)


def _mm_nt(a, b):
    return lax.dot_general(a, b, (((1,), (1,)), ((), ())), preferred_element_type=F32)


def _mm_tn(a, b):
    return lax.dot_general(a, b, (((0,), (0,)), ((), ())), preferred_element_type=F32)


def _row_tile(rows, cols):
    limit = max(8, (512 * 1024) // cols)
    return max(t for t in range(8, rows + 1, 8) if rows % t == 0 and (t <= limit or t == 8))


def _rstd(xf):
    return lax.rsqrt(jnp.mean(xf * xf, axis=-1, keepdims=True) + RMS_EPS)


def _rms_bwd(xf, gain, dn):
    r = _rstd(xf)
    xh = xf * r
    dgain = jnp.sum(dn * xh, axis=0, keepdims=True)
    dxh = dn * gain
    dx = r * (dxh - xh * jnp.mean(dxh * xh, axis=-1, keepdims=True))
    return dx, dgain


def _ffn_up(x, gain, wgu, *, tm, name):
    T, D = x.shape
    tm = min(tm, T)
    nb, bw = wgu.shape[0] // 2, wgu.shape[1]

    def body(x_ref, gain_ref, wg_ref, wu_ref, gu_ref, hid_ref, n_scr):
        @pl.when(pl.program_id(1) == 0)
        def _():
            xf = x_ref[...]
            n_scr[...] = (xf * _rstd(xf) * gain_ref[...]).astype(BF16)

        halves = (pl.ds(0, tm // 2), pl.ds(tm // 2, tm // 2))
        wg, wu = wg_ref[...], wu_ref[...]
        gus = [(_mm_nt(n_scr[rows, :], wg), _mm_nt(n_scr[rows, :], wu)) for rows in halves]
        for rows, (g, u) in zip(halves, gus):
            gu_ref[0, rows, :] = g.astype(BF16)
            gu_ref[1, rows, :] = u.astype(BF16)
            hid_ref[rows, :] = (g * jax.nn.sigmoid(g) * u).astype(BF16)

    return pl.pallas_call(
        body, name=name, grid=(T // tm, nb),
        in_specs=[
            pl.BlockSpec((tm, D), lambda i, j: (i, 0)),
            pl.BlockSpec((1, D), lambda i, j: (0, 0)),
            pl.BlockSpec((None, bw, D), lambda i, j: (j, 0, 0)),
            pl.BlockSpec((None, bw, D), lambda i, j: (j + nb, 0, 0)),
        ],
        out_specs=[
            pl.BlockSpec((2, tm, bw), lambda i, j: (0, i, j)),
            pl.BlockSpec((tm, bw), lambda i, j: (i, j)),
        ],
        out_shape=[jax.ShapeDtypeStruct((2, T, nb * bw), BF16), jax.ShapeDtypeStruct((T, nb * bw), BF16)],
        scratch_shapes=[pltpu.VMEM((tm, D), BF16)],
        compiler_params=_params(("arbitrary", "arbitrary")),
    )(x, gain, wgu, wgu)


def _ffn_down(x, hid, wd, *, tm, name):
    T, D = x.shape
    tm = min(tm, T)
    F = hid.shape[1]

    def body(x_ref, hid_ref, wd_ref, h_ref):
        h_ref[...] = x_ref[...] + 0.5 * _mm(hid_ref[...], wd_ref[...])

    return pl.pallas_call(
        body, name=name, grid=(T // tm,),
        in_specs=[
            pl.BlockSpec((tm, D), lambda i: (i, 0)),
            pl.BlockSpec((tm, F), lambda i: (i, 0)),
            pl.BlockSpec((F, D), lambda i: (0, 0)),
        ],
        out_specs=pl.BlockSpec((tm, D), lambda i: (i, 0)),
        out_shape=jax.ShapeDtypeStruct((T, D), F32),
        compiler_params=_params(("arbitrary",)),
    )(x, hid, wd)


AFTER = pl.BlockSpec(memory_space=pltpu.HBM)


def _in_hbm(token):
    return pltpu.with_memory_space_constraint(token, pltpu.HBM)


def _ffn_bwd(dh, df, x, gain, gu, wgu, wd, after, *, tm, name):
    T, D = x.shape
    tm = min(tm, T)
    nb, bw = wgu.shape[0] // 2, wgu.shape[1]

    def body(dh_ref, df_ref, x_ref, gain_ref, gu_ref, wg_ref, wu_ref, wd_ref, after_ref,
             dx_ref, dgain_ref, n_ref, dgu_ref, dn_acc):
        i, j = pl.program_id(0), pl.program_id(1)

        @pl.when(j == 0)
        def _():
            xf = x_ref[...]
            n_ref[...] = (xf * _rstd(xf) * gain_ref[...]).astype(BF16)
            dn_acc[...] = jnp.zeros_like(dn_acc)

        @pl.when((i == 0) & (j == 0))
        def _():
            dgain_ref[...] = jnp.zeros_like(dgain_ref)

        halves = (pl.ds(0, tm // 2), pl.ds(tm // 2, tm // 2))
        wd, wg, wu = wd_ref[...], wg_ref[...], wu_ref[...]
        dhids = [_mm_nt(df_ref[rows, :], wd) for rows in halves]
        for rows, dhid in zip(halves, dhids):
            g = gu_ref[0, rows, :].astype(F32)
            u = gu_ref[1, rows, :].astype(F32)
            s = jax.nn.sigmoid(g)
            silu = g * s
            dg = (dhid * u * (s * (1.0 + g * (1.0 - s)))).astype(BF16)
            du = (dhid * silu).astype(BF16)
            dgu_ref[0, rows, :] = dg
            dgu_ref[1, rows, :] = du
            dn_acc[rows, :] += _mm(dg, wg) + _mm(du, wu)

        @pl.when(j == nb - 1)
        def _():
            dx, dgain = _rms_bwd(x_ref[...], gain_ref[...], dn_acc[...])
            dx_ref[...] = dh_ref[...] + dx
            dgain_ref[...] += dgain

    row = lambda i, j: (i, 0)
    return pl.pallas_call(
        body, name=name, grid=(T // tm, nb),
        in_specs=[
            pl.BlockSpec((tm, D), row),
            pl.BlockSpec((tm, D), row),
            pl.BlockSpec((tm, D), row),
            pl.BlockSpec((1, D), lambda i, j: (0, 0)),
            pl.BlockSpec((2, tm, bw), lambda i, j: (0, i, j)),
            pl.BlockSpec((None, bw, D), lambda i, j: (j, 0, 0)),
            pl.BlockSpec((None, bw, D), lambda i, j: (j + nb, 0, 0)),
            pl.BlockSpec((bw, D), lambda i, j: (j, 0)),
            AFTER,
        ],
        out_specs=[
            pl.BlockSpec((tm, D), row),
            pl.BlockSpec((1, D), lambda i, j: (0, 0)),
            pl.BlockSpec((tm, D), row),
            pl.BlockSpec((2, tm, bw), lambda i, j: (0, i, j)),
        ],
        out_shape=[
            jax.ShapeDtypeStruct((T, D), F32),
            jax.ShapeDtypeStruct((1, D), F32),
            jax.ShapeDtypeStruct((T, D), BF16),
            jax.ShapeDtypeStruct((2, T, nb * bw), BF16),
        ],
        scratch_shapes=[pltpu.VMEM((tm, D), F32)],
        compiler_params=_params(("arbitrary", "arbitrary")),
    )(dh, df, x, gain, gu, wgu, wgu, wd, _in_hbm(after))


def _wgrad(a, b, *, grid, a_spec, b_spec, out_spec, out_shape, acc_shape, name):
    nk = grid[2]

    def body(a_ref, b_ref, o_ref, acc):
        k = pl.program_id(2)

        @pl.when(k == 0)
        def _():
            acc[...] = jnp.zeros_like(acc)

        acc[...] += _mm_tn(a_ref[...].astype(BF16), b_ref[...].astype(BF16))

        @pl.when(k == nk - 1)
        def _():
            o_ref[...] = acc[...].astype(o_ref.dtype)

    return pl.pallas_call(
        body, name=name, grid=grid, in_specs=[a_spec, b_spec], out_specs=out_spec,
        out_shape=jax.ShapeDtypeStruct(out_shape, BF16),
        scratch_shapes=[pltpu.VMEM(acc_shape, F32)],
        compiler_params=_params(("arbitrary", "arbitrary", "arbitrary")),
    )(a, b)


def _wgrad_gate_up(n, dgu, *, tk, name, part=0, parts=1):
    T, D = n.shape
    tk = min(tk, T)
    owner_rows = FF_SHARD_PAD * 2
    nb = dgu.shape[2] // owner_rows
    bw = owner_rows // parts
    return _wgrad(
        dgu, n, grid=(2 * nb, 1, T // tk), name=name,
        a_spec=pl.BlockSpec((None, tk, bw), lambda m, c, k: (m // nb, k, parts * (m % nb) + part)),
        b_spec=pl.BlockSpec((tk, D), lambda m, c, k: (k, 0)),
        out_spec=pl.BlockSpec((None, bw, D), lambda m, c, k: (m, 0, 0)),
        out_shape=(2 * nb, bw, D), acc_shape=(bw, D))


def _wgrad_down(hid, df, *, tk, name):
    T, D = df.shape
    tk = min(tk, T)
    bw = FF_SHARD_PAD * 2
    nb = hid.shape[1] // bw
    return _wgrad(
        hid, df, grid=(nb, 1, T // tk), name=name,
        a_spec=pl.BlockSpec((tk, bw), lambda m, c, k: (k, m)),
        b_spec=pl.BlockSpec((tk, D), lambda m, c, k: (k, 0)),
        out_spec=pl.BlockSpec((bw, D), lambda m, c, k: (m, 0)),
        out_shape=(nb * bw, D), acc_shape=(bw, D))


def _wgrad_in(dparts, un, *, name):
    T, D = un.shape
    bw = sum(p.shape[1] for p in dparts) // N_DEV
    first = [sum(p.shape[1] for p in dparts[:i]) // bw for i in range(len(dparts) + 1)]

    def body(*refs):
        dp_refs, un_ref, o_ref = refs[:-2], refs[-2], refs[-1]
        m = pl.program_id(0)
        for dp_ref, lo, hi in zip(dp_refs, first[:-1], first[1:]):
            @pl.when((m >= lo) & (m < hi))
            def _():
                o_ref[...] = _mm_tn(dp_ref[...], un_ref[...]).astype(o_ref.dtype)

    def piece_spec(lo, hi):
        return pl.BlockSpec((T, bw), lambda m: (0, jnp.clip(m - lo, 0, hi - lo - 1)))

    return pl.pallas_call(
        body, name=name, grid=(N_DEV,),
        in_specs=[piece_spec(lo, hi) for lo, hi in zip(first[:-1], first[1:])] + [pl.BlockSpec((T, D), lambda m: (0, 0))],
        out_specs=pl.BlockSpec((None, bw, D), lambda m: (m, 0, 0)),
        out_shape=jax.ShapeDtypeStruct((N_DEV, bw, D), BF16),
        compiler_params=_params(("arbitrary",)),
    )(*dparts, un)


def _wgrad_full(a, b, *, tk, name):
    T, M = a.shape
    tk = min(tk, T)
    N = b.shape[1]
    return _wgrad(
        a, b, grid=(1, 1, T // tk), name=name,
        a_spec=pl.BlockSpec((tk, M), lambda m, c, k: (k, 0)),
        b_spec=pl.BlockSpec((tk, N), lambda m, c, k: (k, 0)),
        out_spec=pl.BlockSpec((M, N), lambda m, c, k: (0, 0)), out_shape=(M, N), acc_shape=(M, N))


def _loss_bwd(h, target, gain, *, tm, name):
    T, D = h.shape
    tm = min(tm, T)

    def body(h_ref, t_ref, gain_ref, dh_ref, df_ref, loss_ref, dgain_ref):
        @pl.when(pl.program_id(0) == 0)
        def _():
            loss_ref[...] = jnp.zeros_like(loss_ref)
            dgain_ref[...] = jnp.zeros_like(dgain_ref)

        xf = h_ref[...]
        gain = gain_ref[...]
        err = xf * _rstd(xf) * gain - t_ref[...]
        loss_ref[...] += 0.5 * jnp.sum(jnp.mean(err * err, axis=-1, keepdims=True), axis=0, keepdims=True)
        dx, dgain = _rms_bwd(xf, gain, err * (1.0 / D))
        dh_ref[...] = dx
        df_ref[...] = (0.5 * dx).astype(BF16)
        dgain_ref[...] += dgain

    row = lambda i: (i, 0)
    fixed = lambda i: (0, 0)
    return pl.pallas_call(
        body, name=name, grid=(T // tm,),
        in_specs=[pl.BlockSpec((tm, D), row), pl.BlockSpec((tm, D), row), pl.BlockSpec((1, D), fixed)],
        out_specs=[pl.BlockSpec((tm, D), row), pl.BlockSpec((tm, D), row), pl.BlockSpec((1, 128), fixed),
                   pl.BlockSpec((1, D), fixed)],
        out_shape=[jax.ShapeDtypeStruct((T, D), F32), jax.ShapeDtypeStruct((T, D), BF16),
                   jax.ShapeDtypeStruct((1, 128), F32), jax.ShapeDtypeStruct((1, D), F32)],
        compiler_params=_params(("arbitrary",)),
    )(h, target, gain)


def _inproj_fwd(h, gain, w_in_t, *, tm, name):
    T, D = h.shape
    tm = min(tm, T)
    bn = D
    nb = w_in_t.shape[0] // bn

    def body(h_ref, gain_ref, wt_ref, un_ref, proj_ref):
        @pl.when(pl.program_id(1) == 0)
        def _():
            xf = h_ref[...]
            un_ref[...] = (xf * _rstd(xf) * gain_ref[...]).astype(BF16)

        proj_ref[...] = _mm_nt(un_ref[...], wt_ref[...])

    return pl.pallas_call(
        body, name=name, grid=(T // tm, nb),
        in_specs=[
            pl.BlockSpec((tm, D), lambda i, j: (i, 0)),
            pl.BlockSpec((1, D), lambda i, j: (0, 0)),
            pl.BlockSpec((bn, D), lambda i, j: (j, 0)),
        ],
        out_specs=[pl.BlockSpec((tm, D), lambda i, j: (i, 0)), pl.BlockSpec((tm, bn), lambda i, j: (i, j))],
        out_shape=[jax.ShapeDtypeStruct((T, D), BF16), jax.ShapeDtypeStruct((T, nb * bn), F32)],
        compiler_params=_params(("arbitrary", "arbitrary")),
    )(h, gain, w_in_t)


def _inproj_bwd(dparts, dh, h, gain, w_in_t, *, tm, name):
    T, D = h.shape
    tm = min(tm, T)
    n = len(dparts)
    widths = [p.shape[1] for p in dparts]
    starts = [sum(widths[:i]) for i in range(n)]

    def body(*refs):
        dp_refs = refs[:n]
        dh_ref, h_ref, gain_ref, wt_ref, dx_ref, df_ref, dgain_ref = refs[n:]

        @pl.when(pl.program_id(0) == 0)
        def _():
            dgain_ref[...] = jnp.zeros_like(dgain_ref)

        dn = sum(_mm(dp_ref[...], wt_ref[start:start + width, :])
                 for dp_ref, start, width in zip(dp_refs, starts, widths))
        dx, dgain = _rms_bwd(h_ref[...], gain_ref[...], dn)
        dh_in = dh_ref[...] + dx
        dx_ref[...] = dh_in
        df_ref[...] = (0.5 * dh_in).astype(BF16)
        dgain_ref[...] += dgain

    row = lambda i: (i, 0)
    fixed = lambda i: (0, 0)
    return pl.pallas_call(
        body, name=name, grid=(T // tm,),
        in_specs=[pl.BlockSpec((tm, width), row) for width in widths] + [
            pl.BlockSpec((tm, D), row),
            pl.BlockSpec((tm, D), row),
            pl.BlockSpec((1, D), fixed),
            pl.BlockSpec(w_in_t.shape, fixed),
        ],
        out_specs=[pl.BlockSpec((tm, D), row), pl.BlockSpec((tm, D), row), pl.BlockSpec((1, D), fixed)],
        out_shape=[jax.ShapeDtypeStruct((T, D), F32), jax.ShapeDtypeStruct((T, D), BF16),
                   jax.ShapeDtypeStruct((1, D), F32)],
        compiler_params=_params(("arbitrary",)),
    )(*dparts, dh, h, gain, w_in_t)


def _window_sum(x, row, doublings, *, backward):
    T = x.shape[0]
    s = x
    for k in range(doublings):
        sh = 1 << k
        if backward:
            s = s + jnp.where(row < T - sh, pltpu.roll(s, T - sh, 0), 0.0)
        else:
            s = s + jnp.where(row >= sh, pltpu.roll(s, sh, 0), 0.0)
    return s


def _pool_fwd(proj, w_group, scale, *, name):
    T = proj.shape[0]

    def body(xp_ref, w_ref, scale_ref, p_ref):
        row = lax.broadcasted_iota(jnp.int32, (T, POOL_GROUP), 0)
        for gi, window in enumerate(POOL_WINDOWS):
            cols = slice(gi * POOL_GROUP, (gi + 1) * POOL_GROUP)
            x = xp_ref[:, cols]
            inv_count = 1.0 / jnp.minimum(row + 1, window).astype(F32)
            yc = _window_sum(x, row, gi + 1, backward=False) * inv_count - x
            pre = _mm(yc.astype(BF16), w_ref[gi].astype(BF16))
            p_ref[:, cols] = pre * scale_ref[:, cols]

    return pl.pallas_call(
        body, name=name, grid=(1,),
        in_specs=[
            pl.BlockSpec((T, POOL_WIDTH), lambda i: (0, 0)),
            pl.BlockSpec(w_group.shape, lambda i: (0, 0, 0)),
            pl.BlockSpec((1, POOL_WIDTH), lambda i: (0, 0)),
        ],
        out_specs=pl.BlockSpec((T, POOL_WIDTH), lambda i: (0, 0)),
        out_shape=jax.ShapeDtypeStruct((T, POOL_WIDTH), F32),
        compiler_params=_params(("arbitrary",)),
    )(proj, w_group, scale)


def _pool_bwd(dp, proj, w_group, scale, *, name):
    T = proj.shape[0]

    def body(dp_ref, xp_ref, w_ref, scale_ref, dxp_ref, dw_ref, dscale_ref):
        row = lax.broadcasted_iota(jnp.int32, (T, POOL_GROUP), 0)
        for gi, window in enumerate(POOL_WINDOWS):
            cols = slice(gi * POOL_GROUP, (gi + 1) * POOL_GROUP)
            x = xp_ref[:, cols]
            inv_count = 1.0 / jnp.minimum(row + 1, window).astype(F32)
            yc = (_window_sum(x, row, gi + 1, backward=False) * inv_count - x).astype(BF16)
            w = w_ref[gi].astype(BF16)
            pre = _mm(yc, w)
            dpg = dp_ref[:, cols]
            dscale_ref[:, cols] = jnp.sum(dpg * pre, axis=0, keepdims=True)
            dpre = (dpg * scale_ref[:, cols]).astype(BF16)
            dw_ref[gi] = _mm_tn(yc, dpre)
            dyc = _mm_nt(dpre, w)
            dxp_ref[:, cols] = (_window_sum(dyc * inv_count, row, gi + 1, backward=True) - dyc).astype(BF16)

    return pl.pallas_call(
        body, name=name, grid=(1,),
        in_specs=[
            pl.BlockSpec((T, POOL_WIDTH), lambda i: (0, 0)),
            pl.BlockSpec((T, POOL_WIDTH), lambda i: (0, 0)),
            pl.BlockSpec(w_group.shape, lambda i: (0, 0, 0)),
            pl.BlockSpec((1, POOL_WIDTH), lambda i: (0, 0)),
        ],
        out_specs=[
            pl.BlockSpec((T, POOL_WIDTH), lambda i: (0, 0)),
            pl.BlockSpec(w_group.shape, lambda i: (0, 0, 0)),
            pl.BlockSpec((1, POOL_WIDTH), lambda i: (0, 0)),
        ],
        out_shape=[jax.ShapeDtypeStruct((T, POOL_WIDTH), BF16), jax.ShapeDtypeStruct(w_group.shape, F32),
                   jax.ShapeDtypeStruct((1, POOL_WIDTH), F32)],
        compiler_params=_params(("arbitrary",)),
    )(dp, proj, w_group, scale)


ATTN_STRIP = 32


def _log_sigmoids(z):
    lb = jnp.minimum(z, 0.0) - jnp.log(1.0 + jnp.exp(-jnp.abs(z)))
    return lb, lb - z


def _transposed_blocks(x_ref, blocks_scr, tq):
    for b in range(blocks_scr.shape[0]):
        blocks_scr[b] = x_ref[b * tq:(b + 1) * tq, :].T.astype(BF16)


def _split_bf16(x):
    hi = x.astype(BF16)
    return hi, (x - hi.astype(F32)).astype(BF16)


def _strips(n):
    return [slice(i, i + ATTN_STRIP) for i in range(0, n, ATTN_STRIP)]


def _rows(parts):
    return jnp.concatenate(parts, axis=0)


def _attn_specs(T, tq):
    q_col = POOL_WIDTH // HEAD_PAIR
    k_col = q_col + SB_WIDTH // HEAD_PAIR
    v_col = k_col + SB_WIDTH // HEAD_PAIR
    return [
        pl.BlockSpec((tq, HEAD_PAIR), lambda p, i: (i, q_col + p)),
        pl.BlockSpec((T, HEAD_PAIR), lambda p, i: (0, k_col + p)),
        pl.BlockSpec((T, HEAD_PAIR), lambda p, i: (0, v_col + p)),
    ]


def _attn_fwd(proj, *, name):
    T = proj.shape[0]
    tk = min(ATTN_K_BLOCK, T)
    tq = min(ATTN_Q_BLOCK_FWD, T)
    diagonal_blocks = tq // tk

    def body(q_ref, k_ref, v_ref, o_ref, lt_ref, kt_scr, vb_scr):
        qi = pl.program_id(1)

        @pl.when(qi == 0)
        def _():
            _transposed_blocks(k_ref, kt_scr, tk)
            vb_scr[...] = v_ref[...].astype(BF16)

        head0 = lax.broadcasted_iota(jnp.int32, (tq, HEAD_PAIR), 1) < HEAD_DIM
        q = q_ref[...] * ATTN_SCALE
        qs = (jnp.where(head0, q, 0.0).astype(BF16), jnp.where(head0, 0.0, q).astype(BF16))
        r = lax.broadcasted_iota(jnp.int32, (tq, tk), 0)
        c = lax.broadcasted_iota(jnp.int32, (tq, tk), 1)
        later = (r[:tk] > c[:tk]).astype(BF16)
        later2 = _rows([later, later])
        causal = lambda d: (lambda rows: c[rows] + d * tk < r[rows])
        strips = _strips(tq)

        def log_terms(z, valid):
            lbs, his, los, sums = [], [], [], []
            for rows in strips:
                lb, lm = _log_sigmoids(z[rows])
                if valid is not None:
                    lm = jnp.where(valid(rows), lm, 0.0)
                hi, lo = _split_bf16(lm)
                lbs.append(lb)
                his.append(hi)
                los.append(lo)
                sums.append(jnp.sum(lm, axis=1, keepdims=True))
            return lbs, jnp.concatenate([_rows(his), _rows(los)], axis=1), _rows(sums)

        def weights(lbs, run, after, valid):
            parts = []
            for rows, lb in zip(strips, lbs):
                a = jnp.exp(lb + run[rows] + after[rows])
                if valid is not None:
                    a = jnp.where(valid(rows), a, 0.0)
                parts.append(a.astype(BF16))
            return _rows(parts)

        def block(kj, carry, valid):
            kt = kt_scr[kj]
            vb = vb_scr[pl.ds(pl.multiple_of(kj * tk, tk), tk), :]
            run0, o0, run1, o1 = carry
            z0 = _mm(qs[0], kt)
            z1 = _mm(qs[1], kt)
            lbs0, split0, sums0 = log_terms(z0, valid)
            after0 = _mm(split0, later2)
            lbs1, split1, sums1 = log_terms(z1, valid)
            after1 = _mm(split1, later2)
            o0 = o0 + _mm(weights(lbs0, run0, after0, valid), vb)
            o1 = o1 + _mm(weights(lbs1, run1, after1, valid), vb)
            return run0 + sums0, o0, run1 + sums1, o1

        zero = (jnp.zeros((tq, 1), F32), jnp.zeros((tq, HEAD_PAIR), F32))
        first = diagonal_blocks * qi
        carry = zero + zero
        for d in reversed(range(diagonal_blocks)):
            carry = block(first + d, carry, causal(d))
        carry = lax.fori_loop(0, first, lambda it, cr: block(first - 1 - it, cr, None), carry)
        o_ref[...] = jnp.where(head0, carry[1], carry[3])
        lt_ref[...] = jnp.where(head0, carry[0], carry[2])

    out_spec = pl.BlockSpec((tq, HEAD_PAIR), lambda p, i: (i, p))
    return pl.pallas_call(
        body, name=name, grid=(N_HEADS // 2, T // tq),
        in_specs=_attn_specs(T, tq), out_specs=[out_spec, out_spec],
        out_shape=[jax.ShapeDtypeStruct((T, SB_WIDTH), F32), jax.ShapeDtypeStruct((T, SB_WIDTH), F32)],
        scratch_shapes=[pltpu.VMEM((T // tk, HEAD_PAIR, tk), BF16), pltpu.VMEM((T, HEAD_PAIR), BF16)],
        compiler_params=_params(("arbitrary", "arbitrary")),
    )(proj, proj, proj)


def _attn_bwd(proj, do, ltot, after, *, name):
    T = proj.shape[0]
    tk = min(ATTN_K_BLOCK, T)
    tq = min(ATTN_Q_BLOCK_BWD, T)
    diagonal_blocks = tq // tk

    def body(q_ref, k_ref, v_ref, do_ref, lt_ref, after_ref, dq_ref, dk_ref, dv_ref,
             kb_scr, kt_scr, vt_scr, dkt_ref, dvt_ref):
        qi = pl.program_id(1)

        @pl.when(qi == 0)
        def _():
            kb_scr[...] = k_ref[...].astype(BF16)
            _transposed_blocks(k_ref, kt_scr, tk)
            _transposed_blocks(v_ref, vt_scr, tk)
            dkt_ref[...] = jnp.zeros_like(dkt_ref)
            dvt_ref[...] = jnp.zeros_like(dvt_ref)

        head0 = lax.broadcasted_iota(jnp.int32, (tq, HEAD_PAIR), 1) < HEAD_DIM
        q, do_, lt = q_ref[...] * ATTN_SCALE, do_ref[...], lt_ref[...]
        qs = (jnp.where(head0, q, 0.0).astype(BF16), jnp.where(head0, 0.0, q).astype(BF16))
        q_heads = (jnp.where(head0, q, 0.0), jnp.where(head0, 0.0, q))
        do_heads = (jnp.where(head0, do_, 0.0), jnp.where(head0, 0.0, do_))
        dos = tuple(d.astype(BF16) for d in do_heads)
        qts = tuple(x.T.astype(BF16) for x in q_heads)
        dots = tuple(d.T.astype(BF16) for d in do_heads)
        lts = (jnp.max(jnp.where(head0, lt, -jnp.inf), axis=1, keepdims=True),
               jnp.max(jnp.where(head0, -jnp.inf, lt), axis=1, keepdims=True))
        r = lax.broadcasted_iota(jnp.int32, (tq, tk), 0)
        c = lax.broadcasted_iota(jnp.int32, (tq, tk), 1)
        upto = (r[:tk] <= c[:tk]).astype(BF16)
        before = (r[:tk] < c[:tk]).astype(BF16)
        upto2, before2 = _rows([upto, upto]), _rows([before, before])
        causal = lambda d: (lambda rows: c[rows] + d * tk < r[rows])
        strips = _strips(tq)

        def log_terms(z, valid):
            lbs, his, los, sums = [], [], [], []
            for rows in strips:
                lb, lm = _log_sigmoids(z[rows])
                if valid is not None:
                    lm = jnp.where(valid(rows), lm, 0.0)
                hi, lo = _split_bf16(lm)
                lbs.append(lb)
                his.append(hi)
                los.append(lo)
                sums.append(jnp.sum(lm, axis=1, keepdims=True))
            return lbs, jnp.concatenate([_rows(his), _rows(los)], axis=1), _rows(sums)

        def weights(lbs, rest, lm_upto, da, valid):
            a_parts, es, his, los, sums = [], [], [], [], []
            for rows, lb in zip(strips, lbs):
                a = jnp.exp(lb + (rest[rows] - lm_upto[rows]))
                if valid is not None:
                    a = jnp.where(valid(rows), a, 0.0)
                e = da[rows] * a
                hi, lo = _split_bf16(e)
                a_parts.append(a.astype(BF16))
                es.append(e)
                his.append(hi)
                los.append(lo)
                sums.append(jnp.sum(e, axis=1, keepdims=True))
            return _rows(a_parts), es, jnp.concatenate([_rows(his), _rows(los)], axis=1), _rows(sums)

        def score_grads(lbs, es, run_e, e_before, valid):
            parts = []
            for rows, lb, e in zip(strips, lbs, es):
                beta = jnp.exp(lb)
                dz = e * (1.0 - beta) - (run_e[rows] + e_before[rows]) * beta
                if valid is not None:
                    dz = jnp.where(valid(rows), dz, 0.0)
                parts.append(dz.astype(BF16))
            return _rows(parts)

        def block(kj, carry, valid):
            off = pl.multiple_of(kj * tk, tk)
            kb, kt, vt = kb_scr[pl.ds(off, tk), :], kt_scr[kj], vt_scr[kj]
            run_lm0, run_e0, dq0, run_lm1, run_e1, dq1 = carry
            z0, da0 = _mm(qs[0], kt), _mm(dos[0], vt)
            z1, da1 = _mm(qs[1], kt), _mm(dos[1], vt)
            lbs0, split0, lm_sums0 = log_terms(z0, valid)
            lm_upto0 = _mm(split0, upto2)
            lbs1, split1, lm_sums1 = log_terms(z1, valid)
            lm_upto1 = _mm(split1, upto2)
            a0, es0, split0, e_sums0 = weights(lbs0, lts[0] - run_lm0, lm_upto0, da0, valid)
            e_before0 = _mm(split0, before2)
            a1, es1, split1, e_sums1 = weights(lbs1, lts[1] - run_lm1, lm_upto1, da1, valid)
            e_before1 = _mm(split1, before2)
            dz0 = score_grads(lbs0, es0, run_e0, e_before0, valid)
            dkt_blk = _mm(qts[0], dz0)
            dvt_blk = _mm(dots[0], a0)
            dq0 = dq0 + _mm(dz0, kb)
            dz1 = score_grads(lbs1, es1, run_e1, e_before1, valid)
            dkt_ref[kj] += dkt_blk + _mm(qts[1], dz1)
            dvt_ref[kj] += dvt_blk + _mm(dots[1], a1)
            dq1 = dq1 + _mm(dz1, kb)
            return run_lm0 + lm_sums0, run_e0 + e_sums0, dq0, run_lm1 + lm_sums1, run_e1 + e_sums1, dq1

        zero = (jnp.zeros((tq, 1), F32), jnp.zeros((tq, 1), F32), jnp.zeros((tq, HEAD_PAIR), F32))
        first = diagonal_blocks * qi
        carry = lax.fori_loop(0, first, lambda kj, cr: block(kj, cr, None), zero + zero)
        for d in range(diagonal_blocks):
            carry = block(first + d, carry, causal(d))
        dq_ref[...] = (jnp.where(head0, carry[2], carry[5]) * ATTN_SCALE).astype(BF16)

        @pl.when(qi == T // tq - 1)
        def _():
            for b in range(T // tk):
                dk_ref[b * tk:(b + 1) * tk, :] = dkt_ref[b].T.astype(BF16)
                dv_ref[b * tk:(b + 1) * tk, :] = dvt_ref[b].T.astype(BF16)

    blk = pl.BlockSpec((tq, HEAD_PAIR), lambda p, i: (i, p))
    seq = pl.BlockSpec((T, HEAD_PAIR), lambda p, i: (0, p))
    transposed = pltpu.VMEM((T // tk, HEAD_PAIR, tk), F32)
    return pl.pallas_call(
        body, name=name, grid=(N_HEADS // 2, T // tq),
        in_specs=_attn_specs(T, tq) + [blk, blk, AFTER], out_specs=[blk, seq, seq],
        out_shape=[jax.ShapeDtypeStruct((T, SB_WIDTH), BF16)] * 3,
        scratch_shapes=[pltpu.VMEM((T, HEAD_PAIR), BF16), pltpu.VMEM((T // tk, HEAD_PAIR, tk), BF16),
                        pltpu.VMEM((T // tk, HEAD_PAIR, tk), BF16), transposed, transposed],
        compiler_params=_params(("arbitrary", "arbitrary")),
    )(proj, proj, proj, do, ltot, _in_hbm(after))


def _mix_specs(T, D, tm, wbp, w_out):
    gate_col = (POOL_WIDTH + 3 * SB_WIDTH) // D
    row = lambda i: (i, 0)
    return [
        pl.BlockSpec((tm, D), row),
        pl.BlockSpec((tm, POOL_WIDTH), row),
        pl.BlockSpec((tm, SB_WIDTH), row),
        pl.BlockSpec((tm, D), lambda i: (i, gate_col)),
        pl.BlockSpec((tm, D), lambda i: (i, gate_col + 1)),
        pl.BlockSpec(wbp.shape, lambda i: (0, 0)),
        pl.BlockSpec(wbp.shape, lambda i: (0, 0)),
        pl.BlockSpec(w_out.shape, lambda i: (0, 0)),
    ]


def _mix_fwd(h, p, o, proj, wbp, wba, w_out, *, tm, name):
    T, D = h.shape
    tm = min(tm, T)

    def body(h_ref, p_ref, o_ref, glp_ref, gls_ref, wbp_ref, wba_ref, wout_ref, hout_ref, m_ref):
        halves = (pl.ds(0, tm // 2), pl.ds(tm // 2, tm // 2))
        wbp, wba, wout = wbp_ref[...], wba_ref[...], wout_ref[...]
        branches = [(_mm_nt(p_ref[rows, :].astype(BF16), wbp), _mm_nt(o_ref[rows, :].astype(BF16), wba))
                    for rows in halves]
        for rows, (yp, ys) in zip(halves, branches):
            m = (jax.nn.sigmoid(glp_ref[rows, :]) * yp + jax.nn.sigmoid(gls_ref[rows, :]) * ys).astype(BF16)
            m_ref[rows, :] = m
            hout_ref[rows, :] = h_ref[rows, :] + _mm(m, wout)

    row = lambda i: (i, 0)
    return pl.pallas_call(
        body, name=name, grid=(T // tm,),
        in_specs=_mix_specs(T, D, tm, wbp, w_out),
        out_specs=[pl.BlockSpec((tm, D), row), pl.BlockSpec((tm, D), row)],
        out_shape=[jax.ShapeDtypeStruct((T, D), F32), jax.ShapeDtypeStruct((T, D), BF16)],
        compiler_params=_params(("arbitrary",)),
    )(h, p, o, proj, proj, wbp, wba, w_out)


def _mix_bwd(dh, p, o, proj, wbp, wba, w_out, after, *, tm, name):
    T, D = dh.shape
    tm = min(tm, T)

    def body(dh_ref, p_ref, o_ref, glp_ref, gls_ref, wbp_ref, wba_ref, wout_ref, after_ref,
             dyp_ref, dys_ref, dp_ref, do_ref, dgl_ref):
        halves = (pl.ds(0, tm // 2), pl.ds(tm // 2, tm // 2))
        wbp, wba, wout = wbp_ref[...], wba_ref[...], wout_ref[...]
        products = [(_mm_nt(dh_ref[rows, :].astype(BF16), wout), _mm_nt(p_ref[rows, :].astype(BF16), wbp),
                     _mm_nt(o_ref[rows, :].astype(BF16), wba)) for rows in halves]
        for rows, (dm, yp, ys) in zip(halves, products):
            gp = jax.nn.sigmoid(glp_ref[rows, :])
            gs = jax.nn.sigmoid(gls_ref[rows, :])
            dyp = (dm * gp).astype(BF16)
            dys = (dm * gs).astype(BF16)
            dyp_ref[rows, :] = dyp
            dys_ref[rows, :] = dys
            dgl_ref[rows, :D] = (dm * yp * gp * (1.0 - gp)).astype(BF16)
            dgl_ref[rows, D:] = (dm * ys * gs * (1.0 - gs)).astype(BF16)
            dp_ref[rows, :] = _mm(dyp, wbp)
            do_ref[rows, :] = _mm(dys, wba)

    row = lambda i: (i, 0)
    return pl.pallas_call(
        body, name=name, grid=(T // tm,),
        in_specs=_mix_specs(T, D, tm, wbp, w_out) + [AFTER],
        out_specs=[pl.BlockSpec((tm, D), row), pl.BlockSpec((tm, D), row), pl.BlockSpec((tm, POOL_WIDTH), row),
                   pl.BlockSpec((tm, SB_WIDTH), row), pl.BlockSpec((tm, 2 * D), row)],
        out_shape=[jax.ShapeDtypeStruct((T, D), BF16), jax.ShapeDtypeStruct((T, D), BF16),
                   jax.ShapeDtypeStruct((T, POOL_WIDTH), F32), jax.ShapeDtypeStruct((T, SB_WIDTH), F32),
                   jax.ShapeDtypeStruct((T, 2 * D), BF16)],
        compiler_params=_params(("arbitrary",)),
    )(dh, p, o, proj, proj, wbp, wba, w_out, _in_hbm(after))


def _adamw_update(w, g, m, v):
    m_ = ADAM_B1 * m + (1.0 - ADAM_B1) * g
    v_ = ADAM_B2 * v + (1.0 - ADAM_B2) * (g * g)
    m_hat = m_ / (1.0 - ADAM_B1 ** ADAM_STEP)
    v_hat = v_ / (1.0 - ADAM_B2 ** ADAM_STEP)
    return -ADAM_LR * (m_hat / (jnp.sqrt(v_hat) + ADAM_EPS) + ADAM_WD * w), m_, v_


def _adamw(w, g, m, v, *, name):
    R, C = w.shape
    tr = _row_tile(R, C)

    def body(w_ref, g_ref, m_ref, v_ref, d_ref, nm_ref, nv_ref):
        d_ref[...], nm_ref[...], nv_ref[...] = _adamw_update(w_ref[...], g_ref[...], m_ref[...], v_ref[...])

    spec = pl.BlockSpec((tr, C), lambda i: (i, 0))
    return pl.pallas_call(
        body, name=name, grid=(R // tr,), in_specs=[spec] * 4, out_specs=[spec] * 3,
        out_shape=[jax.ShapeDtypeStruct((R, C), F32)] * 3,
        compiler_params=_params(("arbitrary",)),
    )(w, g, m, v)


def _position():
    return lax.axis_index("x"), lax.axis_index("y"), lax.axis_index("c")


def _all_gather(shards, *, name, collective_id):
    n = len(shards)
    n_copies = 9

    def body(*refs):
        ins, outs = refs[:n], refs[n:2 * n]
        send_sems, recv_sems, local_sems = refs[2 * n:]
        x, y, c = _position()
        me, sibling = (x, y, c), (x, y, 1 - c)
        x_nbr, y_nbr, diagonal = (1 - x, y, c), (x, 1 - y, c), (1 - x, 1 - y, c)
        other = lambda pos: (pos[0], pos[1], 1 - c)

        barrier = pltpu.get_barrier_semaphore()
        for peer in (sibling, x_nbr, y_nbr):
            pl.semaphore_signal(barrier, inc=1, device_id=peer, device_id_type=MESH)
        pl.semaphore_wait(barrier, 3)

        def block(a, pos, half=None):
            ref = outs[a].at[4 * pos[0] + 2 * pos[1] + pos[2]]
            rows = ref.shape[0] // 2
            return ref if half is None else ref.at[pl.ds(half * rows, rows)]

        def copy(a, k, pos, to, half=None, src=None):
            return pltpu.make_async_remote_copy(
                src_ref=block(a, pos, half) if src is None else src, dst_ref=block(a, pos, half),
                send_sem=send_sems.at[n_copies * a + k], recv_sem=recv_sems.at[n_copies * a + k],
                device_id=to, device_id_type=MESH)

        started = []
        for a in range(n):
            mine = pltpu.make_async_copy(ins[a], block(a, me), local_sems.at[a])
            mine.start()
            started.append(mine)
        sends = []
        for a in range(n):
            sends += [copy(a, 1, me, x_nbr, src=ins[a]), copy(a, 2, me, y_nbr, src=ins[a]),
                      copy(a, 0, me, sibling, src=ins[a])]
        for cp in sends:
            cp.start()

        def pass_on(copies):
            for cp in copies:
                cp.start()
                sends.append(cp)

        for a in range(n):
            copy(a, 1, x_nbr, me).wait_recv()
            pass_on([copy(a, 5, x_nbr, y_nbr, half=0), copy(a, 3, x_nbr, sibling)])
            copy(a, 2, y_nbr, me).wait_recv()
            pass_on([copy(a, 6, y_nbr, x_nbr, half=1), copy(a, 4, y_nbr, sibling)])
        for a in range(n):
            copy(a, 5, diagonal, me, half=0).wait_recv()
            pass_on([copy(a, 7, diagonal, sibling, half=0)])
            copy(a, 6, diagonal, me, half=1).wait_recv()
            pass_on([copy(a, 8, diagonal, sibling, half=1)])
        for a in range(n):
            copy(a, 0, sibling, me).wait_recv()
            copy(a, 3, other(x_nbr), me).wait_recv()
            copy(a, 4, other(y_nbr), me).wait_recv()
            copy(a, 7, other(diagonal), me, half=0).wait_recv()
            copy(a, 8, other(diagonal), me, half=1).wait_recv()
        for cp in sends:
            cp.wait_send()
        for cp in started:
            cp.wait()

    return pl.kernel(
        body, name=name,
        out_type=[jax.ShapeDtypeStruct((N_DEV,) + s.shape, s.dtype) for s in shards],
        mesh=plsc.ScalarSubcoreMesh(axis_name="sequencer", num_cores=1),
        scratch_types=[pltpu.SemaphoreType.DMA((n_copies * n,)), pltpu.SemaphoreType.DMA((n_copies * n,)),
                       pltpu.SemaphoreType.DMA((n,))],
        compiler_params=pltpu.CompilerParams(collective_id=collective_id),
    )(*shards)


def _chip_sums(group, *, name):
    n = len(group)
    shapes = [g.shape[1:] for g in group]

    def body(*refs):
        g_refs, partials, out_refs = refs[:n], refs[n:3 * n:2], refs[n + 1:3 * n:2]
        mines, theirs = refs[3 * n:5 * n:2], refs[3 * n + 1:5 * n:2]
        send_sems, recv_sems, local_sems = refs[5 * n:]
        x, y, c = _position()
        my_chip = 2 * x + y

        def swap(a, s):
            return pltpu.make_async_remote_copy(
                src_ref=g_refs[a].at[2 * s + (1 - c)], dst_ref=theirs[a].at[s],
                send_sem=send_sems.at[4 * a + s], recv_sem=recv_sems.at[4 * a + s],
                device_id=(x, y, 1 - c), device_id_type=MESH)

        def load(a, s):
            return pltpu.make_async_copy(g_refs[a].at[2 * s + c], mines[a].at[s], local_sems.at[4 * a + s])

        for a in range(n):
            for s in range(4):
                swap(a, s).start()
                load(a, s).start()

        for a, (R, C) in enumerate(shapes):
            rc = 128 if R % 128 == 0 else R

            def chip_sum(chip, rows):
                return mines[a][chip, rows, :].astype(F32) + theirs[a][chip, rows, :].astype(F32)

            for s in range(4):
                load(a, s).wait()
                swap(a, s).wait_recv()

                @pl.when(s == my_chip)
                def _():
                    @pl.loop(0, R // rc)
                    def _(t):
                        rows = pl.ds(pl.multiple_of(t * rc, rc), rc)
                        out_refs[a][rows, :] = chip_sum(s, rows)

                @pl.when(s != my_chip)
                def _():
                    @pl.loop(0, R // rc)
                    def _(t):
                        rows = pl.ds(pl.multiple_of(t * rc, rc), rc)
                        partials[a][(s ^ my_chip) - 1, rows, :] = chip_sum(s, rows).astype(BF16)

        for a in range(n):
            for s in range(4):
                swap(a, s).wait_send()

    vmem = pl.BlockSpec(memory_space=pltpu.VMEM)
    outs = pl.pallas_call(
        body, name=name,
        in_specs=[pl.BlockSpec(memory_space=pl.ANY)] * n, out_specs=[vmem] * (2 * n),
        out_shape=[shape for R, C in shapes
                   for shape in (jax.ShapeDtypeStruct((3, R, C), BF16), jax.ShapeDtypeStruct((R, C), F32))],
        scratch_shapes=[pltpu.VMEM((4, R, C), BF16) for R, C in shapes for _ in range(2)] + [
            pltpu.SemaphoreType.DMA((4 * n,)), pltpu.SemaphoreType.DMA((4 * n,)), pltpu.SemaphoreType.DMA((4 * n,))],
        compiler_params=_params(),
    )(*group)
    return [(outs[2 * a], outs[2 * a + 1]) for a in range(n)]


def _cross_chips(partials, *, name, collective_id):
    n = len(partials)

    def body(*refs):
        ins, outs = refs[:n], refs[n:2 * n]
        send_sems, recv_sems = refs[2 * n:]
        x, y, c = _position()
        my_chip = 2 * x + y
        peers = [((my_chip ^ j) // 2, (my_chip ^ j) % 2, c) for j in (1, 2, 3)]

        barrier = pltpu.get_barrier_semaphore()
        for peer in peers:
            pl.semaphore_signal(barrier, inc=1, device_id=peer, device_id_type=MESH)
        pl.semaphore_wait(barrier, 3)

        copies = [
            pltpu.make_async_remote_copy(
                src_ref=ins[a].at[j], dst_ref=outs[a].at[j],
                send_sem=send_sems.at[3 * a + j], recv_sem=recv_sems.at[3 * a + j],
                device_id=peers[j], device_id_type=MESH)
            for a in range(n) for j in range(3)]
        for cp in copies:
            cp.start()
        for cp in copies:
            cp.wait_recv()
        for cp in copies:
            cp.wait_send()

    return pl.kernel(
        body, name=name,
        out_type=[jax.ShapeDtypeStruct(p.shape, p.dtype) for p in partials],
        mesh=plsc.ScalarSubcoreMesh(axis_name="sequencer", num_cores=1),
        scratch_types=[pltpu.SemaphoreType.DMA((3 * n,)), pltpu.SemaphoreType.DMA((3 * n,))],
        compiler_params=pltpu.CompilerParams(collective_id=collective_id),
    )(*partials)


def _cross_chips_and_gather(partials, slab, *, name, collective_id):
    n = len(partials)

    def body(*refs):
        part_refs, slab_ref = refs[:n], refs[n]
        landed_refs, slabs_ref = refs[n + 1:2 * n + 1], refs[2 * n + 1]
        send_sems, recv_sems, local_sem = refs[2 * n + 2:]
        x, y, c = _position()
        me, my_chip = 4 * x + 2 * y + c, 2 * x + y
        others = [me ^ k for k in range(1, N_DEV)]
        ids = [(o // 4, (o // 2) % 2, o % 2) for o in others]

        barrier = pltpu.get_barrier_semaphore()
        for peer in ids:
            pl.semaphore_signal(barrier, inc=1, device_id=peer, device_id_type=MESH)
        pl.semaphore_wait(barrier, N_DEV - 1)

        mine = pltpu.make_async_copy(slab_ref, slabs_ref.at[me], local_sem)
        mine.start()
        sends = [
            pltpu.make_async_remote_copy(
                src_ref=part_refs[a].at[j], dst_ref=landed_refs[a].at[j],
                send_sem=send_sems.at[3 * a + j], recv_sem=recv_sems.at[3 * a + j],
                device_id=((my_chip ^ (j + 1)) // 2, (my_chip ^ (j + 1)) % 2, c), device_id_type=MESH)
            for a in range(n) for j in range(3)]
        sends += [
            pltpu.make_async_remote_copy(
                src_ref=slab_ref, dst_ref=slabs_ref.at[me],
                send_sem=send_sems.at[3 * n + k], recv_sem=recv_sems.at[3 * n + k],
                device_id=ids[k], device_id_type=MESH)
            for k in range(N_DEV - 1)]
        arrivals = sends[:3 * n] + [
            pltpu.make_async_remote_copy(
                src_ref=slab_ref, dst_ref=slabs_ref.at[others[k]],
                send_sem=send_sems.at[3 * n + k], recv_sem=recv_sems.at[3 * n + k],
                device_id=ids[k], device_id_type=MESH)
            for k in range(N_DEV - 1)]
        for cp in sends:
            cp.start()
        for cp in arrivals:
            cp.wait_recv()
        for cp in sends:
            cp.wait_send()
        mine.wait()

    n_sems = 3 * n + N_DEV - 1
    outs = pl.kernel(
        body, name=name,
        out_type=[jax.ShapeDtypeStruct(p.shape, p.dtype) for p in partials]
                 + [jax.ShapeDtypeStruct((N_DEV,) + slab.shape, slab.dtype)],
        mesh=plsc.ScalarSubcoreMesh(axis_name="sequencer", num_cores=1),
        scratch_types=[pltpu.SemaphoreType.DMA((n_sems,)), pltpu.SemaphoreType.DMA((n_sems,)), pltpu.SemaphoreType.DMA],
        compiler_params=pltpu.CompilerParams(collective_id=collective_id),
    )(*partials, slab)
    return outs[:n], outs[n]


def _sum_devices(gathered, after, *, name):
    _, R, C = gathered.shape

    def body(in_ref, after_ref, out_ref):
        total = in_ref[0]
        for d in range(1, N_DEV):
            total = total + in_ref[d]
        out_ref[...] = total

    return pl.pallas_call(
        body, name=name, grid=(1,),
        in_specs=[pl.BlockSpec((N_DEV, R, C), lambda i: (0, 0, 0)), AFTER],
        out_specs=pl.BlockSpec((R, C), lambda i: (0, 0)),
        out_shape=jax.ShapeDtypeStruct((R, C), F32),
        compiler_params=_params(("arbitrary",)),
    )(gathered, _in_hbm(after))


def _owner_sum(own, landed, after, *, name):
    R, C = own.shape
    tr = _row_tile(R, C)

    def body(own_ref, landed_ref, after_ref, out_ref):
        total = own_ref[...]
        for j in range(3):
            total = total + landed_ref[j].astype(F32)
        out_ref[...] = total

    return pl.pallas_call(
        body, name=name, grid=(R // tr,),
        in_specs=[pl.BlockSpec((tr, C), lambda i: (i, 0)), pl.BlockSpec((3, tr, C), lambda i: (0, i, 0)), AFTER],
        out_specs=pl.BlockSpec((tr, C), lambda i: (i, 0)),
        out_shape=jax.ShapeDtypeStruct((R, C), F32),
        compiler_params=_params(("arbitrary",)),
    )(own, landed, _in_hbm(after))


def _owner_sum_adamw(own, landed, w, m, v, after, *, name):
    H, R, C = w.shape
    tr = R // 2

    def body(own_ref, landed_ref, w_ref, m_ref, v_ref, after_ref, g_ref, d_ref, nm_ref, nv_ref):
        total = own_ref[...]
        for j in range(3):
            total = total + landed_ref[j].astype(F32)
        g_ref[...] = total
        d_ref[...], nm_ref[...], nv_ref[...] = _adamw_update(w_ref[...], total, m_ref[...], v_ref[...])

    spec = pl.BlockSpec((None, tr, C), lambda h, i: (h, i, 0))
    return pl.pallas_call(
        body, name=name, grid=(H, R // tr),
        in_specs=[spec, pl.BlockSpec((3, None, tr, C), lambda h, i: (0, h, i, 0)), spec, spec, spec, AFTER],
        out_specs=[spec] * 4,
        out_shape=[jax.ShapeDtypeStruct((H, R, C), F32)] * 4,
        compiler_params=_params(("arbitrary", "arbitrary")),
    )(own, landed, w, m, v, _in_hbm(after))


def _local_step(x, target, norms, pool_w_group, pool_scale, wgu1, wd1, w_in, wbp, wba, w_out, wgu2, wd2, exchange):
    n1g, nmg, n2g, nfg = norms
    D = x.shape[1]
    gu1, hid1 = _ffn_up(x, n1g, wgu1, tm=1024, name="ffn1_up")
    h1 = _ffn_down(x, hid1, wd1, tm=512, name="ffn1_down")
    un, proj = _inproj_fwd(h1, nmg, w_in, tm=1024, name="inproj_fwd")
    p = _pool_fwd(proj, pool_w_group, pool_scale, name="pool_fwd")
    o, ltot = _attn_fwd(proj, name="attn_fwd")
    h2, m = _mix_fwd(h1, p, o, proj, wbp, wba, w_out, tm=512, name="mix_fwd")
    gu2, hid2 = _ffn_up(h2, n2g, wgu2, tm=1024, name="ffn2_up")
    h3 = _ffn_down(h2, hid2, wd2, tm=512, name="ffn2_down")
    dh3, df2, loss, d_nf = _loss_bwd(h3, target, nfg, tm=256, name="loss_bwd")

    dh2, d_n2, n2, dgu2 = _ffn_bwd(dh3, df2, h2, n2g, gu2, wgu2, wd2, df2, tm=512, name="ffn2_bwd")
    d_wd2 = _wgrad_down(hid2, df2, tk=WGRAD_TOKENS, name="ffn2_wgrad_down")
    d_wgu2 = _wgrad_gate_up(n2, dgu2, tk=WGRAD_TOKENS, name="ffn2_wgrad_gate_up")
    (g_wd2, g_wgu2), token = exchange("ffn2", [d_wd2.reshape(N_DEV, FF_SHARD_PAD, D), d_wgu2])

    dyp, dys, dp, do, dgl = _mix_bwd(dh2, p, o, proj, wbp, wba, w_out, token, tm=512, name="mix_bwd")
    d_wout = _wgrad_full(m, dh2, tk=WGRAD_TOKENS, name="wgrad_out")
    d_wbp = _wgrad_full(dyp, p, tk=WGRAD_TOKENS, name="wgrad_branch_pool")
    d_wba = _wgrad_full(dys, o, tk=WGRAD_TOKENS, name="wgrad_branch_attn")
    by_owner = lambda g: g.reshape(N_DEV, g.shape[0] // N_DEV, g.shape[1])
    (g_wbp, g_wba, g_wout), token = exchange("mix", [by_owner(d_wbp), by_owner(d_wba), by_owner(d_wout)])
    dxp, d_wgroup, d_scale = _pool_bwd(dp, proj, pool_w_group, pool_scale, name="pool_bwd")
    dq, dk, dv = _attn_bwd(proj, do, ltot, token, name="attn_bwd")
    dproj_parts = [dxp, dq, dk, dv, dgl]
    dh1, df1, d_nm = _inproj_bwd(dproj_parts, dh2, h1, nmg, w_in, tm=512, name="inproj_bwd")
    d_win = _wgrad_in(dproj_parts, un, name="wgrad_in")
    d_wd1 = _wgrad_down(hid1, df1, tk=WGRAD_TOKENS, name="ffn1_wgrad_down")
    (g_win, g_wd1, replicated_early), token = exchange(
        "w_in_ffn1_down", [d_win, d_wd1.reshape(N_DEV, FF_SHARD_PAD, D), d_nm, d_n2, d_nf, d_scale, d_wgroup, loss])

    dx, d_n1, n1, dgu1 = _ffn_bwd(dh1, df1, x, n1g, gu1, wgu1, wd1, token, tm=512, name="ffn1_bwd")
    d_wgu1_a = _wgrad_gate_up(n1, dgu1, tk=WGRAD_TOKENS, name="ffn1_wgrad_gate_up_a", part=0, parts=2)
    (g_wgu1_a, replicated_late), token = exchange("ffn1_gate_up_a", [d_wgu1_a, d_n1])
    d_wgu1_b = _wgrad_gate_up(n1, dgu1, tk=WGRAD_TOKENS, name="ffn1_wgrad_gate_up_b", part=1, parts=2)
    (g_wgu1_b,), token = exchange("last", [d_wgu1_b])
    g_wgu1 = (g_wgu1_a, g_wgu1_b)

    sharded = (g_wgu1, g_wd1, g_win, g_wbp, g_wba, g_wout, g_wgu2, g_wd2)
    return dx, sharded, (replicated_late, replicated_early), token


def _hidden_major(w):
    return jnp.swapaxes(w[0], 0, 1)


def _pad_gate_up(wt):
    d = wt.shape[1]
    wt = wt.astype(BF16).reshape(2, FF_SHARD, d)
    return jnp.pad(wt, ((0, 0), (0, FF_SHARD_PAD - FF_SHARD), (0, 0))).reshape(2 * FF_SHARD_PAD, d)


def _unpad_gate_up(gt):
    d = gt.shape[1]
    return gt.reshape(2, FF_SHARD_PAD, d)[:, :FF_SHARD].reshape(2 * FF_SHARD, d)


def _pad_down(w):
    return jnp.pad(w.astype(BF16), ((0, FF_SHARD_PAD - FF_SHARD), (0, 0)))


def kernel(x, ffn1_norm, ffn1_w_gate_up, ffn1_w_down, mix_norm, w_in, pool_w_group, pool_scale, w_branch_pool, w_branch_attn, w_out, ffn2_norm, ffn2_w_gate_up, ffn2_w_down, final_norm, loss_target, m_ffn1_norm, m_ffn1_w_gate_up, m_ffn1_w_down, m_mix_norm, m_w_in, m_pool_w_group, m_pool_scale, m_w_branch_pool, m_w_branch_attn, m_w_out, m_ffn2_norm, m_ffn2_w_gate_up, m_ffn2_w_down, m_final_norm, v_ffn1_norm, v_ffn1_w_gate_up, v_ffn1_w_down, v_mix_norm, v_w_in, v_pool_w_group, v_pool_scale, v_w_branch_pool, v_w_branch_attn, v_w_out, v_ffn2_norm, v_ffn2_w_gate_up, v_ffn2_w_down, v_final_norm):
    D = x.shape[-1]
    weights = dict(ffn1_norm=ffn1_norm, ffn1_w_gate_up=ffn1_w_gate_up, ffn1_w_down=ffn1_w_down, mix_norm=mix_norm,
                   w_in=w_in, pool_w_group=pool_w_group, pool_scale=pool_scale, w_branch_pool=w_branch_pool,
                   w_branch_attn=w_branch_attn, w_out=w_out, ffn2_norm=ffn2_norm, ffn2_w_gate_up=ffn2_w_gate_up,
                   ffn2_w_down=ffn2_w_down, final_norm=final_norm)
    first = dict(ffn1_norm=m_ffn1_norm, ffn1_w_gate_up=m_ffn1_w_gate_up, ffn1_w_down=m_ffn1_w_down,
                 mix_norm=m_mix_norm, w_in=m_w_in, pool_w_group=m_pool_w_group, pool_scale=m_pool_scale,
                 w_branch_pool=m_w_branch_pool, w_branch_attn=m_w_branch_attn, w_out=m_w_out,
                 ffn2_norm=m_ffn2_norm, ffn2_w_gate_up=m_ffn2_w_gate_up, ffn2_w_down=m_ffn2_w_down,
                 final_norm=m_final_norm)
    second = dict(ffn1_norm=v_ffn1_norm, ffn1_w_gate_up=v_ffn1_w_gate_up, ffn1_w_down=v_ffn1_w_down,
                  mix_norm=v_mix_norm, w_in=v_w_in, pool_w_group=v_pool_w_group, pool_scale=v_pool_scale,
                  w_branch_pool=v_w_branch_pool, w_branch_attn=v_w_branch_attn, w_out=v_w_out,
                  ffn2_norm=v_ffn2_norm, ffn2_w_gate_up=v_ffn2_w_gate_up, ffn2_w_down=v_ffn2_w_down,
                  final_norm=v_final_norm)
    order = list(weights)

    wgu1, = _all_gather([_pad_gate_up(_hidden_major(ffn1_w_gate_up))], name="all_gather_ffn1_gate_up", collective_id=0)
    wd1, = _all_gather([_pad_down(ffn1_w_down[0])], name="all_gather_ffn1_down", collective_id=10)
    transposed = lambda w: jnp.swapaxes(w[0], 0, 1).astype(BF16)
    win_g, = _all_gather([transposed(w_in)], name="all_gather_w_in", collective_id=1)
    wbp_g, wba_g = _all_gather([transposed(w_branch_pool), transposed(w_branch_attn)],
                               name="all_gather_branches", collective_id=2)
    wout_g, = _all_gather([w_out[0].astype(BF16)], name="all_gather_w_out", collective_id=11)
    wgu2, wd2 = _all_gather([_pad_gate_up(_hidden_major(ffn2_w_gate_up)), _pad_down(ffn2_w_down[0])],
                            name="all_gather_ffn2", collective_id=3)
    whole = lambda g: g.reshape(g.shape[0] * g.shape[1], g.shape[2])
    wd1, wd2, win_g, wbp_g, wba_g, wout_g = (whole(g) for g in (wd1, wd2, win_g, wbp_g, wba_g, wout_g))

    cross_ids = {"ffn2": 4, "mix": 5, "w_in_ffn1_down": 8, "ffn1_gate_up_a": 9, "last": 7}
    small = ["ffn1_norm", "mix_norm", "ffn2_norm", "final_norm", "pool_scale", "pool_w_group"]

    def tile_rows(a):
        a = a.reshape(-1, 128)
        return jnp.pad(a, ((0, -a.shape[0] % 8), (0, 0)))

    def exchange(tag, group):
        grads = [g for g in group if g.dtype == BF16]
        extras = [tile_rows(g) for g in group if g.dtype != BF16]
        sums = _chip_sums(grads, name="chip_sums_" + tag)
        partials = [s[0] for s in sums]
        handles = []
        if extras:
            landed, slabs = _cross_chips_and_gather(partials, jnp.concatenate(extras, axis=0),
                                                    name="cross_chips_" + tag, collective_id=cross_ids[tag])
            handles = [slabs]
        else:
            landed = _cross_chips(partials, name="cross_chips_" + tag, collective_id=cross_ids[tag])
        return [(s[1], l) for s, l in zip(sums, landed)] + handles, sums[-1][1]

    norms = (ffn1_norm, mix_norm, ffn2_norm, final_norm.reshape(1, D))
    dx, sharded, (slabs_late, slabs_early), last = _local_step(
        x[0], loss_target[0], norms, pool_w_group[0], pool_scale, wgu1, wd1, win_g, wbp_g, wba_g, wout_g, wgu2, wd2,
        exchange)
    names = ["ffn1_w_gate_up", "ffn1_w_down", "w_in", "w_branch_pool", "w_branch_attn", "w_out",
             "ffn2_w_gate_up", "ffn2_w_down"]
    handles = dict(zip(names, sharded))
    grads, delta, new_m, new_v = {}, {}, {}, {}
    after = last
    for k in ("ffn2_w_down", "ffn2_w_gate_up", "w_branch_pool", "w_branch_attn", "w_out", "w_in", "ffn1_w_down",
              "ffn1_w_gate_up"):
        hidden_major = k.endswith("w_gate_up")
        view = _hidden_major if hidden_major else (lambda a: a[0])
        back = (lambda a: jnp.swapaxes(a, 0, 1)[None]) if hidden_major else (lambda a: a[None])
        if k in ("ffn2_w_down", "ffn2_w_gate_up", "w_out", "ffn1_w_down"):
            own, landed = handles[k]
            groups = 2 if hidden_major else 1
            by_group = lambda a: a.reshape(a.shape[:-2] + (groups, a.shape[-2] // groups, a.shape[-1]))
            out = _owner_sum_adamw(by_group(own), by_group(landed), by_group(view(weights[k])),
                                   by_group(view(first[k])), by_group(view(second[k])), after, name="adamw_" + k)
            after = out[1]
            grads[k], delta[k], new_m[k], new_v[k] = (back(a.reshape(-1, a.shape[-1])) for a in out)
            continue
        if isinstance(handles[k][0], tuple):
            first_half = _owner_sum(*handles[k][0], after, name="owner_sum_" + k + "_a")
            second_half = _owner_sum(*handles[k][1], first_half, name="owner_sum_" + k + "_b")
            g = jnp.concatenate([first_half[:FF_SHARD], second_half[:FF_SHARD]], axis=0)
        else:
            g = jnp.swapaxes(_owner_sum(*handles[k], after, name="owner_sum_" + k), 0, 1)
        out = _adamw(view(weights[k]), g, view(first[k]), view(second[k]), name="adamw_" + k)
        after = out[0]
        grads[k] = back(g)
        delta[k], new_m[k], new_v[k] = (back(a) for a in out)

    rows = [weights[k].size // 128 for k in small]
    padded_rows = [-(-r // 8) * 8 for r in rows]
    starts = [sum(padded_rows[:i]) for i in range(len(rows) + 1)]
    total = jnp.concatenate([_sum_devices(slabs_late, after, name="sum_replicated_late"),
                             _sum_devices(slabs_early, after, name="sum_replicated_early")], axis=0)
    loss_out = total[starts[-1], 0]
    small_w = jnp.concatenate([tile_rows(weights[k]) for k in small], axis=0)
    small_m = jnp.concatenate([tile_rows(first[k]) for k in small], axis=0)
    small_v = jnp.concatenate([tile_rows(second[k]) for k in small], axis=0)
    small_out = _adamw(small_w, total[:starts[-1]], small_m, small_v, name="adamw_replicated")
    for name_, start, n_rows in zip(small, starts, rows):
        shape = weights[name_].shape
        grads[name_] = total[start:start + n_rows].reshape(shape)
        delta[name_], new_m[name_], new_v[name_] = (a[start:start + n_rows].reshape(shape) for a in small_out)

    return (loss_out, dx[None], *[grads[k] for k in order], *[delta[k] for k in order],
            *[new_m[k] for k in order], *[new_v[k] for k in order])
```

```python
import jax
import jax.numpy as jnp
from jax import lax
from jax.experimental import pallas as pl
from jax.experimental.pallas import tpu as pltpu
from jax.experimental.pallas import tpu_sc as plsc

F32 = jnp.float32
BF16 = jnp.bfloat16
MESH = pl.DeviceIdType.MESH

RMS_EPS = 1e-6
N_DEV = 8
N_HEADS = 8
HEAD_DIM = 64
HEAD_PAIR = 2 * HEAD_DIM
POOL_WINDOWS = (2, 4, 8, 16)
POOL_GROUP = 128
POOL_WIDTH = 512
SB_WIDTH = 512
FF_SHARD = 352
FF_SHARD_PAD = 384
ATTN_K_BLOCK = 256
ATTN_Q_BLOCK_FWD = 512
ATTN_Q_BLOCK_BWD = 256
ATTN_SCALE = 0.125

ADAM_LR = 0.001
ADAM_B1 = 0.9
ADAM_B2 = 0.999
ADAM_EPS = 1e-08
ADAM_WD = 0.01
ADAM_STEP = 10

VMEM_LIMIT = 48 << 20
WGRAD_TOKENS = 2048


def _params(dims=None):
    return pltpu.CompilerParams(dimension_semantics=dims, vmem_limit_bytes=VMEM_LIMIT)


def _mm(a, b):
    return jnp.dot(a, b, preferred_element_type=F32)


def _mm_nt(a, b):
    return lax.dot_general(a, b, (((1,), (1,)), ((), ())), preferred_element_type=F32)


def _mm_tn(a, b):
    return lax.dot_general(a, b, (((0,), (0,)), ((), ())), preferred_element_type=F32)


def _row_tile(rows, cols):
    limit = max(8, (512 * 1024) // cols)
    return max(t for t in range(8, rows + 1, 8) if rows % t == 0 and (t <= limit or t == 8))


def _rstd(xf):
    return lax.rsqrt(jnp.mean(xf * xf, axis=-1, keepdims=True) + RMS_EPS)


def _rms_bwd(xf, gain, dn):
    r = _rstd(xf)
    xh = xf * r
    dgain = jnp.sum(dn * xh, axis=0, keepdims=True)
    dxh = dn * gain
    dx = r * (dxh - xh * jnp.mean(dxh * xh, axis=-1, keepdims=True))
    return dx, dgain


def _ffn_up(x, gain, wgu, *, tm, name):
    T, D = x.shape
    tm = min(tm, T)
    nb, bw = wgu.shape[0] // 2, wgu.shape[1]

    def body(x_ref, gain_ref, wg_ref, wu_ref, gu_ref, hid_ref, n_scr):
        @pl.when(pl.program_id(1) == 0)
        def _():
            xf = x_ref[...]
            n_scr[...] = (xf * _rstd(xf) * gain_ref[...]).astype(BF16)

        halves = (pl.ds(0, tm // 2), pl.ds(tm // 2, tm // 2))
        wg, wu = wg_ref[...], wu_ref[...]
        gus = [(_mm_nt(n_scr[rows, :], wg), _mm_nt(n_scr[rows, :], wu)) for rows in halves]
        for rows, (g, u) in zip(halves, gus):
            gu_ref[0, rows, :] = g.astype(BF16)
            gu_ref[1, rows, :] = u.astype(BF16)
            hid_ref[rows, :] = (g * jax.nn.sigmoid(g) * u).astype(BF16)

    return pl.pallas_call(
        body, name=name, grid=(T // tm, nb),
        in_specs=[
            pl.BlockSpec((tm, D), lambda i, j: (i, 0)),
            pl.BlockSpec((1, D), lambda i, j: (0, 0)),
            pl.BlockSpec((None, bw, D), lambda i, j: (j, 0, 0)),
            pl.BlockSpec((None, bw, D), lambda i, j: (j + nb, 0, 0)),
        ],
        out_specs=[
            pl.BlockSpec((2, tm, bw), lambda i, j: (0, i, j)),
            pl.BlockSpec((tm, bw), lambda i, j: (i, j)),
        ],
        out_shape=[jax.ShapeDtypeStruct((2, T, nb * bw), BF16), jax.ShapeDtypeStruct((T, nb * bw), BF16)],
        scratch_shapes=[pltpu.VMEM((tm, D), BF16)],
        compiler_params=_params(("arbitrary", "arbitrary")),
    )(x, gain, wgu, wgu)


def _ffn_down(x, hid, wd, *, tm, name):
    T, D = x.shape
    tm = min(tm, T)
    F = hid.shape[1]

    def body(x_ref, hid_ref, wd_ref, h_ref):
        h_ref[...] = x_ref[...] + 0.5 * _mm(hid_ref[...], wd_ref[...])

    return pl.pallas_call(
        body, name=name, grid=(T // tm,),
        in_specs=[
            pl.BlockSpec((tm, D), lambda i: (i, 0)),
            pl.BlockSpec((tm, F), lambda i: (i, 0)),
            pl.BlockSpec((F, D), lambda i: (0, 0)),
        ],
        out_specs=pl.BlockSpec((tm, D), lambda i: (i, 0)),
        out_shape=jax.ShapeDtypeStruct((T, D), F32),
        compiler_params=_params(("arbitrary",)),
    )(x, hid, wd)


AFTER = pl.BlockSpec(memory_space=pltpu.HBM)


def _in_hbm(token):
    return pltpu.with_memory_space_constraint(token, pltpu.HBM)


def _ffn_bwd(dh, df, x, gain, gu, wgu, wd, after, *, tm, name):
    T, D = x.shape
    tm = min(tm, T)
    nb, bw = wgu.shape[0] // 2, wgu.shape[1]

    def body(dh_ref, df_ref, x_ref, gain_ref, gu_ref, wg_ref, wu_ref, wd_ref, after_ref,
             dx_ref, dgain_ref, n_ref, dgu_ref, dn_acc):
        i, j = pl.program_id(0), pl.program_id(1)

        @pl.when(j == 0)
        def _():
            xf = x_ref[...]
            n_ref[...] = (xf * _rstd(xf) * gain_ref[...]).astype(BF16)
            dn_acc[...] = jnp.zeros_like(dn_acc)

        @pl.when((i == 0) & (j == 0))
        def _():
            dgain_ref[...] = jnp.zeros_like(dgain_ref)

        halves = (pl.ds(0, tm // 2), pl.ds(tm // 2, tm // 2))
        wd, wg, wu = wd_ref[...], wg_ref[...], wu_ref[...]
        dhids = [_mm_nt(df_ref[rows, :], wd) for rows in halves]
        for rows, dhid in zip(halves, dhids):
            g = gu_ref[0, rows, :].astype(F32)
            u = gu_ref[1, rows, :].astype(F32)
            s = jax.nn.sigmoid(g)
            silu = g * s
            dg = (dhid * u * (s * (1.0 + g * (1.0 - s)))).astype(BF16)
            du = (dhid * silu).astype(BF16)
            dgu_ref[0, rows, :] = dg
            dgu_ref[1, rows, :] = du
            dn_acc[rows, :] += _mm(dg, wg) + _mm(du, wu)

        @pl.when(j == nb - 1)
        def _():
            dx, dgain = _rms_bwd(x_ref[...], gain_ref[...], dn_acc[...])
            dx_ref[...] = dh_ref[...] + dx
            dgain_ref[...] += dgain

    row = lambda i, j: (i, 0)
    return pl.pallas_call(
        body, name=name, grid=(T // tm, nb),
        in_specs=[
            pl.BlockSpec((tm, D), row),
            pl.BlockSpec((tm, D), row),
            pl.BlockSpec((tm, D), row),
            pl.BlockSpec((1, D), lambda i, j: (0, 0)),
            pl.BlockSpec((2, tm, bw), lambda i, j: (0, i, j)),
            pl.BlockSpec((None, bw, D), lambda i, j: (j, 0, 0)),
            pl.BlockSpec((None, bw, D), lambda i, j: (j + nb, 0, 0)),
            pl.BlockSpec((bw, D), lambda i, j: (j, 0)),
            AFTER,
        ],
        out_specs=[
            pl.BlockSpec((tm, D), row),
            pl.BlockSpec((1, D), lambda i, j: (0, 0)),
            pl.BlockSpec((tm, D), row),
            pl.BlockSpec((2, tm, bw), lambda i, j: (0, i, j)),
        ],
        out_shape=[
            jax.ShapeDtypeStruct((T, D), F32),
            jax.ShapeDtypeStruct((1, D), F32),
            jax.ShapeDtypeStruct((T, D), BF16),
            jax.ShapeDtypeStruct((2, T, nb * bw), BF16),
        ],
        scratch_shapes=[pltpu.VMEM((tm, D), F32)],
        compiler_params=_params(("arbitrary", "arbitrary")),
    )(dh, df, x, gain, gu, wgu, wgu, wd, _in_hbm(after))


def _wgrad(a, b, *, grid, a_spec, b_spec, out_spec, out_shape, acc_shape, name):
    nk = grid[2]

    def body(a_ref, b_ref, o_ref, acc):
        k = pl.program_id(2)

        @pl.when(k == 0)
        def _():
            acc[...] = jnp.zeros_like(acc)

        acc[...] += _mm_tn(a_ref[...].astype(BF16), b_ref[...].astype(BF16))

        @pl.when(k == nk - 1)
        def _():
            o_ref[...] = acc[...].astype(o_ref.dtype)

    return pl.pallas_call(
        body, name=name, grid=grid, in_specs=[a_spec, b_spec], out_specs=out_spec,
        out_shape=jax.ShapeDtypeStruct(out_shape, BF16),
        scratch_shapes=[pltpu.VMEM(acc_shape, F32)],
        compiler_params=_params(("arbitrary", "arbitrary", "arbitrary")),
    )(a, b)


def _wgrad_gate_up(n, dgu, *, tk, name, part=0, parts=1):
    T, D = n.shape
    tk = min(tk, T)
    owner_rows = FF_SHARD_PAD * 2
    nb = dgu.shape[2] // owner_rows
    bw = owner_rows // parts
    return _wgrad(
        dgu, n, grid=(2 * nb, 1, T // tk), name=name,
        a_spec=pl.BlockSpec((None, tk, bw), lambda m, c, k: (m // nb, k, parts * (m % nb) + part)),
        b_spec=pl.BlockSpec((tk, D), lambda m, c, k: (k, 0)),
        out_spec=pl.BlockSpec((None, bw, D), lambda m, c, k: (m, 0, 0)),
        out_shape=(2 * nb, bw, D), acc_shape=(bw, D))


def _wgrad_down(hid, df, *, tk, name):
    T, D = df.shape
    tk = min(tk, T)
    bw = FF_SHARD_PAD * 2
    nb = hid.shape[1] // bw
    return _wgrad(
        hid, df, grid=(nb, 1, T // tk), name=name,
        a_spec=pl.BlockSpec((tk, bw), lambda m, c, k: (k, m)),
        b_spec=pl.BlockSpec((tk, D), lambda m, c, k: (k, 0)),
        out_spec=pl.BlockSpec((bw, D), lambda m, c, k: (m, 0)),
        out_shape=(nb * bw, D), acc_shape=(bw, D))


def _wgrad_in(dparts, un, *, name):
    T, D = un.shape
    bw = sum(p.shape[1] for p in dparts) // N_DEV
    first = [sum(p.shape[1] for p in dparts[:i]) // bw for i in range(len(dparts) + 1)]

    def body(*refs):
        dp_refs, un_ref, o_ref = refs[:-2], refs[-2], refs[-1]
        m = pl.program_id(0)
        for dp_ref, lo, hi in zip(dp_refs, first[:-1], first[1:]):
            @pl.when((m >= lo) & (m < hi))
            def _():
                o_ref[...] = _mm_tn(dp_ref[...], un_ref[...]).astype(o_ref.dtype)

    def piece_spec(lo, hi):
        return pl.BlockSpec((T, bw), lambda m: (0, jnp.clip(m - lo, 0, hi - lo - 1)))

    return pl.pallas_call(
        body, name=name, grid=(N_DEV,),
        in_specs=[piece_spec(lo, hi) for lo, hi in zip(first[:-1], first[1:])] + [pl.BlockSpec((T, D), lambda m: (0, 0))],
        out_specs=pl.BlockSpec((None, bw, D), lambda m: (m, 0, 0)),
        out_shape=jax.ShapeDtypeStruct((N_DEV, bw, D), BF16),
        compiler_params=_params(("arbitrary",)),
    )(*dparts, un)


def _wgrad_full(a, b, *, tk, name):
    T, M = a.shape
    tk = min(tk, T)
    N = b.shape[1]
    return _wgrad(
        a, b, grid=(1, 1, T // tk), name=name,
        a_spec=pl.BlockSpec((tk, M), lambda m, c, k: (k, 0)),
        b_spec=pl.BlockSpec((tk, N), lambda m, c, k: (k, 0)),
        out_spec=pl.BlockSpec((M, N), lambda m, c, k: (0, 0)), out_shape=(M, N), acc_shape=(M, N))


def _loss_bwd(h, target, gain, *, tm, name):
    T, D = h.shape
    tm = min(tm, T)

    def body(h_ref, t_ref, gain_ref, dh_ref, df_ref, loss_ref, dgain_ref):
        @pl.when(pl.program_id(0) == 0)
        def _():
            loss_ref[...] = jnp.zeros_like(loss_ref)
            dgain_ref[...] = jnp.zeros_like(dgain_ref)

        xf = h_ref[...]
        gain = gain_ref[...]
        err = xf * _rstd(xf) * gain - t_ref[...]
        loss_ref[...] += 0.5 * jnp.sum(jnp.mean(err * err, axis=-1, keepdims=True), axis=0, keepdims=True)
        dx, dgain = _rms_bwd(xf, gain, err * (1.0 / D))
        dh_ref[...] = dx
        df_ref[...] = (0.5 * dx).astype(BF16)
        dgain_ref[...] += dgain

    row = lambda i: (i, 0)
    fixed = lambda i: (0, 0)
    return pl.pallas_call(
        body, name=name, grid=(T // tm,),
        in_specs=[pl.BlockSpec((tm, D), row), pl.BlockSpec((tm, D), row), pl.BlockSpec((1, D), fixed)],
        out_specs=[pl.BlockSpec((tm, D), row), pl.BlockSpec((tm, D), row), pl.BlockSpec((1, 128), fixed),
                   pl.BlockSpec((1, D), fixed)],
        out_shape=[jax.ShapeDtypeStruct((T, D), F32), jax.ShapeDtypeStruct((T, D), BF16),
                   jax.ShapeDtypeStruct((1, 128), F32), jax.ShapeDtypeStruct((1, D), F32)],
        compiler_params=_params(("arbitrary",)),
    )(h, target, gain)


def _inproj_fwd(h, gain, w_in_t, *, tm, name):
    T, D = h.shape
    tm = min(tm, T)
    bn = D
    nb = w_in_t.shape[0] // bn

    def body(h_ref, gain_ref, wt_ref, un_ref, proj_ref):
        @pl.when(pl.program_id(1) == 0)
        def _():
            xf = h_ref[...]
            un_ref[...] = (xf * _rstd(xf) * gain_ref[...]).astype(BF16)

        proj_ref[...] = _mm_nt(un_ref[...], wt_ref[...])

    return pl.pallas_call(
        body, name=name, grid=(T // tm, nb),
        in_specs=[
            pl.BlockSpec((tm, D), lambda i, j: (i, 0)),
            pl.BlockSpec((1, D), lambda i, j: (0, 0)),
            pl.BlockSpec((bn, D), lambda i, j: (j, 0)),
        ],
        out_specs=[pl.BlockSpec((tm, D), lambda i, j: (i, 0)), pl.BlockSpec((tm, bn), lambda i, j: (i, j))],
        out_shape=[jax.ShapeDtypeStruct((T, D), BF16), jax.ShapeDtypeStruct((T, nb * bn), F32)],
        compiler_params=_params(("arbitrary", "arbitrary")),
    )(h, gain, w_in_t)


def _inproj_bwd(dparts, dh, h, gain, w_in_t, *, tm, name):
    T, D = h.shape
    tm = min(tm, T)
    n = len(dparts)
    widths = [p.shape[1] for p in dparts]
    starts = [sum(widths[:i]) for i in range(n)]

    def body(*refs):
        dp_refs = refs[:n]
        dh_ref, h_ref, gain_ref, wt_ref, dx_ref, df_ref, dgain_ref = refs[n:]

        @pl.when(pl.program_id(0) == 0)
        def _():
            dgain_ref[...] = jnp.zeros_like(dgain_ref)

        dn = sum(_mm(dp_ref[...], wt_ref[start:start + width, :])
                 for dp_ref, start, width in zip(dp_refs, starts, widths))
        dx, dgain = _rms_bwd(h_ref[...], gain_ref[...], dn)
        dh_in = dh_ref[...] + dx
        dx_ref[...] = dh_in
        df_ref[...] = (0.5 * dh_in).astype(BF16)
        dgain_ref[...] += dgain

    row = lambda i: (i, 0)
    fixed = lambda i: (0, 0)
    return pl.pallas_call(
        body, name=name, grid=(T // tm,),
        in_specs=[pl.BlockSpec((tm, width), row) for width in widths] + [
            pl.BlockSpec((tm, D), row),
            pl.BlockSpec((tm, D), row),
            pl.BlockSpec((1, D), fixed),
            pl.BlockSpec(w_in_t.shape, fixed),
        ],
        out_specs=[pl.BlockSpec((tm, D), row), pl.BlockSpec((tm, D), row), pl.BlockSpec((1, D), fixed)],
        out_shape=[jax.ShapeDtypeStruct((T, D), F32), jax.ShapeDtypeStruct((T, D), BF16),
                   jax.ShapeDtypeStruct((1, D), F32)],
        compiler_params=_params(("arbitrary",)),
    )(*dparts, dh, h, gain, w_in_t)


def _window_sum(x, row, doublings, *, backward):
    T = x.shape[0]
    s = x
    for k in range(doublings):
        sh = 1 << k
        if backward:
            s = s + jnp.where(row < T - sh, pltpu.roll(s, T - sh, 0), 0.0)
        else:
            s = s + jnp.where(row >= sh, pltpu.roll(s, sh, 0), 0.0)
    return s


def _pool_fwd(proj, w_group, scale, *, name):
    T = proj.shape[0]

    def body(xp_ref, w_ref, scale_ref, p_ref):
        row = lax.broadcasted_iota(jnp.int32, (T, POOL_GROUP), 0)
        for gi, window in enumerate(POOL_WINDOWS):
            cols = slice(gi * POOL_GROUP, (gi + 1) * POOL_GROUP)
            x = xp_ref[:, cols]
            inv_count = 1.0 / jnp.minimum(row + 1, window).astype(F32)
            yc = _window_sum(x, row, gi + 1, backward=False) * inv_count - x
            pre = _mm(yc.astype(BF16), w_ref[gi].astype(BF16))
            p_ref[:, cols] = pre * scale_ref[:, cols]

    return pl.pallas_call(
        body, name=name, grid=(1,),
        in_specs=[
            pl.BlockSpec((T, POOL_WIDTH), lambda i: (0, 0)),
            pl.BlockSpec(w_group.shape, lambda i: (0, 0, 0)),
            pl.BlockSpec((1, POOL_WIDTH), lambda i: (0, 0)),
        ],
        out_specs=pl.BlockSpec((T, POOL_WIDTH), lambda i: (0, 0)),
        out_shape=jax.ShapeDtypeStruct((T, POOL_WIDTH), F32),
        compiler_params=_params(("arbitrary",)),
    )(proj, w_group, scale)


def _pool_bwd(dp, proj, w_group, scale, *, name):
    T = proj.shape[0]

    def body(dp_ref, xp_ref, w_ref, scale_ref, dxp_ref, dw_ref, dscale_ref):
        row = lax.broadcasted_iota(jnp.int32, (T, POOL_GROUP), 0)
        for gi, window in enumerate(POOL_WINDOWS):
            cols = slice(gi * POOL_GROUP, (gi + 1) * POOL_GROUP)
            x = xp_ref[:, cols]
            inv_count = 1.0 / jnp.minimum(row + 1, window).astype(F32)
            yc = (_window_sum(x, row, gi + 1, backward=False) * inv_count - x).astype(BF16)
            w = w_ref[gi].astype(BF16)
            pre = _mm(yc, w)
            dpg = dp_ref[:, cols]
            dscale_ref[:, cols] = jnp.sum(dpg * pre, axis=0, keepdims=True)
            dpre = (dpg * scale_ref[:, cols]).astype(BF16)
            dw_ref[gi] = _mm_tn(yc, dpre)
            dyc = _mm_nt(dpre, w)
            dxp_ref[:, cols] = (_window_sum(dyc * inv_count, row, gi + 1, backward=True) - dyc).astype(BF16)

    return pl.pallas_call(
        body, name=name, grid=(1,),
        in_specs=[
            pl.BlockSpec((T, POOL_WIDTH), lambda i: (0, 0)),
            pl.BlockSpec((T, POOL_WIDTH), lambda i: (0, 0)),
            pl.BlockSpec(w_group.shape, lambda i: (0, 0, 0)),
            pl.BlockSpec((1, POOL_WIDTH), lambda i: (0, 0)),
        ],
        out_specs=[
            pl.BlockSpec((T, POOL_WIDTH), lambda i: (0, 0)),
            pl.BlockSpec(w_group.shape, lambda i: (0, 0, 0)),
            pl.BlockSpec((1, POOL_WIDTH), lambda i: (0, 0)),
        ],
        out_shape=[jax.ShapeDtypeStruct((T, POOL_WIDTH), BF16), jax.ShapeDtypeStruct(w_group.shape, F32),
                   jax.ShapeDtypeStruct((1, POOL_WIDTH), F32)],
        compiler_params=_params(("arbitrary",)),
    )(dp, proj, w_group, scale)


ATTN_STRIP = 32


def _log_sigmoids(z):
    lb = jnp.minimum(z, 0.0) - jnp.log(1.0 + jnp.exp(-jnp.abs(z)))
    return lb, lb - z


def _transposed_blocks(x_ref, blocks_scr, tq):
    for b in range(blocks_scr.shape[0]):
        blocks_scr[b] = x_ref[b * tq:(b + 1) * tq, :].T.astype(BF16)


def _split_bf16(x):
    hi = x.astype(BF16)
    return hi, (x - hi.astype(F32)).astype(BF16)


def _strips(n):
    return [slice(i, i + ATTN_STRIP) for i in range(0, n, ATTN_STRIP)]


def _rows(parts):
    return jnp.concatenate(parts, axis=0)


def _attn_specs(T, tq):
    q_col = POOL_WIDTH // HEAD_PAIR
    k_col = q_col + SB_WIDTH // HEAD_PAIR
    v_col = k_col + SB_WIDTH // HEAD_PAIR
    return [
        pl.BlockSpec((tq, HEAD_PAIR), lambda p, i: (i, q_col + p)),
        pl.BlockSpec((T, HEAD_PAIR), lambda p, i: (0, k_col + p)),
        pl.BlockSpec((T, HEAD_PAIR), lambda p, i: (0, v_col + p)),
    ]


def _attn_fwd(proj, *, name):
    T = proj.shape[0]
    tk = min(ATTN_K_BLOCK, T)
    tq = min(ATTN_Q_BLOCK_FWD, T)
    diagonal_blocks = tq // tk

    def body(q_ref, k_ref, v_ref, o_ref, lt_ref, kt_scr, vb_scr):
        qi = pl.program_id(1)

        @pl.when(qi == 0)
        def _():
            _transposed_blocks(k_ref, kt_scr, tk)
            vb_scr[...] = v_ref[...].astype(BF16)

        head0 = lax.broadcasted_iota(jnp.int32, (tq, HEAD_PAIR), 1) < HEAD_DIM
        q = q_ref[...] * ATTN_SCALE
        qs = (jnp.where(head0, q, 0.0).astype(BF16), jnp.where(head0, 0.0, q).astype(BF16))
        r = lax.broadcasted_iota(jnp.int32, (tq, tk), 0)
        c = lax.broadcasted_iota(jnp.int32, (tq, tk), 1)
        later = (r[:tk] > c[:tk]).astype(BF16)
        later2 = _rows([later, later])
        causal = lambda d: (lambda rows: c[rows] + d * tk < r[rows])
        strips = _strips(tq)

        def log_terms(z, valid):
            lbs, his, los, sums = [], [], [], []
            for rows in strips:
                lb, lm = _log_sigmoids(z[rows])
                if valid is not None:
                    lm = jnp.where(valid(rows), lm, 0.0)
                hi, lo = _split_bf16(lm)
                lbs.append(lb)
                his.append(hi)
                los.append(lo)
                sums.append(jnp.sum(lm, axis=1, keepdims=True))
            return lbs, jnp.concatenate([_rows(his), _rows(los)], axis=1), _rows(sums)

        def weights(lbs, run, after, valid):
            parts = []
            for rows, lb in zip(strips, lbs):
                a = jnp.exp(lb + run[rows] + after[rows])
                if valid is not None:
                    a = jnp.where(valid(rows), a, 0.0)
                parts.append(a.astype(BF16))
            return _rows(parts)

        def block(kj, carry, valid):
            kt = kt_scr[kj]
            vb = vb_scr[pl.ds(pl.multiple_of(kj * tk, tk), tk), :]
            run0, o0, run1, o1 = carry
            z0 = _mm(qs[0], kt)
            z1 = _mm(qs[1], kt)
            lbs0, split0, sums0 = log_terms(z0, valid)
            after0 = _mm(split0, later2)
            lbs1, split1, sums1 = log_terms(z1, valid)
            after1 = _mm(split1, later2)
            o0 = o0 + _mm(weights(lbs0, run0, after0, valid), vb)
            o1 = o1 + _mm(weights(lbs1, run1, after1, valid), vb)
            return run0 + sums0, o0, run1 + sums1, o1

        zero = (jnp.zeros((tq, 1), F32), jnp.zeros((tq, HEAD_PAIR), F32))
        first = diagonal_blocks * qi
        carry = zero + zero
        for d in reversed(range(diagonal_blocks)):
            carry = block(first + d, carry, causal(d))
        carry = lax.fori_loop(0, first, lambda it, cr: block(first - 1 - it, cr, None), carry)
        o_ref[...] = jnp.where(head0, carry[1], carry[3])
        lt_ref[...] = jnp.where(head0, carry[0], carry[2])

    out_spec = pl.BlockSpec((tq, HEAD_PAIR), lambda p, i: (i, p))
    return pl.pallas_call(
        body, name=name, grid=(N_HEADS // 2, T // tq),
        in_specs=_attn_specs(T, tq), out_specs=[out_spec, out_spec],
        out_shape=[jax.ShapeDtypeStruct((T, SB_WIDTH), F32), jax.ShapeDtypeStruct((T, SB_WIDTH), F32)],
        scratch_shapes=[pltpu.VMEM((T // tk, HEAD_PAIR, tk), BF16), pltpu.VMEM((T, HEAD_PAIR), BF16)],
        compiler_params=_params(("arbitrary", "arbitrary")),
    )(proj, proj, proj)


def _attn_bwd(proj, do, ltot, after, *, name):
    T = proj.shape[0]
    tk = min(ATTN_K_BLOCK, T)
    tq = min(ATTN_Q_BLOCK_BWD, T)
    diagonal_blocks = tq // tk

    def body(q_ref, k_ref, v_ref, do_ref, lt_ref, after_ref, dq_ref, dk_ref, dv_ref,
             kb_scr, kt_scr, vt_scr, dkt_ref, dvt_ref):
        qi = pl.program_id(1)

        @pl.when(qi == 0)
        def _():
            kb_scr[...] = k_ref[...].astype(BF16)
            _transposed_blocks(k_ref, kt_scr, tk)
            _transposed_blocks(v_ref, vt_scr, tk)
            dkt_ref[...] = jnp.zeros_like(dkt_ref)
            dvt_ref[...] = jnp.zeros_like(dvt_ref)

        head0 = lax.broadcasted_iota(jnp.int32, (tq, HEAD_PAIR), 1) < HEAD_DIM
        q, do_, lt = q_ref[...] * ATTN_SCALE, do_ref[...], lt_ref[...]
        qs = (jnp.where(head0, q, 0.0).astype(BF16), jnp.where(head0, 0.0, q).astype(BF16))
        q_heads = (jnp.where(head0, q, 0.0), jnp.where(head0, 0.0, q))
        do_heads = (jnp.where(head0, do_, 0.0), jnp.where(head0, 0.0, do_))
        dos = tuple(d.astype(BF16) for d in do_heads)
        qts = tuple(x.T.astype(BF16) for x in q_heads)
        dots = tuple(d.T.astype(BF16) for d in do_heads)
        lts = (jnp.max(jnp.where(head0, lt, -jnp.inf), axis=1, keepdims=True),
               jnp.max(jnp.where(head0, -jnp.inf, lt), axis=1, keepdims=True))
        r = lax.broadcasted_iota(jnp.int32, (tq, tk), 0)
        c = lax.broadcasted_iota(jnp.int32, (tq, tk), 1)
        upto = (r[:tk] <= c[:tk]).astype(BF16)
        before = (r[:tk] < c[:tk]).astype(BF16)
        upto2, before2 = _rows([upto, upto]), _rows([before, before])
        causal = lambda d: (lambda rows: c[rows] + d * tk < r[rows])
        strips = _strips(tq)

        def log_terms(z, valid):
            lbs, his, los, sums = [], [], [], []
            for rows in strips:
                lb, lm = _log_sigmoids(z[rows])
                if valid is not None:
                    lm = jnp.where(valid(rows), lm, 0.0)
                hi, lo = _split_bf16(lm)
                lbs.append(lb)
                his.append(hi)
                los.append(lo)
                sums.append(jnp.sum(lm, axis=1, keepdims=True))
            return lbs, jnp.concatenate([_rows(his), _rows(los)], axis=1), _rows(sums)

        def weights(lbs, rest, lm_upto, da, valid):
            a_parts, es, his, los, sums = [], [], [], [], []
            for rows, lb in zip(strips, lbs):
                a = jnp.exp(lb + (rest[rows] - lm_upto[rows]))
                if valid is not None:
                    a = jnp.where(valid(rows), a, 0.0)
                e = da[rows] * a
                hi, lo = _split_bf16(e)
                a_parts.append(a.astype(BF16))
                es.append(e)
                his.append(hi)
                los.append(lo)
                sums.append(jnp.sum(e, axis=1, keepdims=True))
            return _rows(a_parts), es, jnp.concatenate([_rows(his), _rows(los)], axis=1), _rows(sums)

        def score_grads(lbs, es, run_e, e_before, valid):
            parts = []
            for rows, lb, e in zip(strips, lbs, es):
                beta = jnp.exp(lb)
                dz = e * (1.0 - beta) - (run_e[rows] + e_before[rows]) * beta
                if valid is not None:
                    dz = jnp.where(valid(rows), dz, 0.0)
                parts.append(dz.astype(BF16))
            return _rows(parts)

        def block(kj, carry, valid):
            off = pl.multiple_of(kj * tk, tk)
            kb, kt, vt = kb_scr[pl.ds(off, tk), :], kt_scr[kj], vt_scr[kj]
            run_lm0, run_e0, dq0, run_lm1, run_e1, dq1 = carry
            z0, da0 = _mm(qs[0], kt), _mm(dos[0], vt)
            z1, da1 = _mm(qs[1], kt), _mm(dos[1], vt)
            lbs0, split0, lm_sums0 = log_terms(z0, valid)
            lm_upto0 = _mm(split0, upto2)
            lbs1, split1, lm_sums1 = log_terms(z1, valid)
            lm_upto1 = _mm(split1, upto2)
            a0, es0, split0, e_sums0 = weights(lbs0, lts[0] - run_lm0, lm_upto0, da0, valid)
            e_before0 = _mm(split0, before2)
            a1, es1, split1, e_sums1 = weights(lbs1, lts[1] - run_lm1, lm_upto1, da1, valid)
            e_before1 = _mm(split1, before2)
            dz0 = score_grads(lbs0, es0, run_e0, e_before0, valid)
            dkt_blk = _mm(qts[0], dz0)
            dvt_blk = _mm(dots[0], a0)
            dq0 = dq0 + _mm(dz0, kb)
            dz1 = score_grads(lbs1, es1, run_e1, e_before1, valid)
            dkt_ref[kj] += dkt_blk + _mm(qts[1], dz1)
            dvt_ref[kj] += dvt_blk + _mm(dots[1], a1)
            dq1 = dq1 + _mm(dz1, kb)
            return run_lm0 + lm_sums0, run_e0 + e_sums0, dq0, run_lm1 + lm_sums1, run_e1 + e_sums1, dq1

        zero = (jnp.zeros((tq, 1), F32), jnp.zeros((tq, 1), F32), jnp.zeros((tq, HEAD_PAIR), F32))
        first = diagonal_blocks * qi
        carry = lax.fori_loop(0, first, lambda kj, cr: block(kj, cr, None), zero + zero)
        for d in range(diagonal_blocks):
            carry = block(first + d, carry, causal(d))
        dq_ref[...] = (jnp.where(head0, carry[2], carry[5]) * ATTN_SCALE).astype(BF16)

        @pl.when(qi == T // tq - 1)
        def _():
            for b in range(T // tk):
                dk_ref[b * tk:(b + 1) * tk, :] = dkt_ref[b].T.astype(BF16)
                dv_ref[b * tk:(b + 1) * tk, :] = dvt_ref[b].T.astype(BF16)

    blk = pl.BlockSpec((tq, HEAD_PAIR), lambda p, i: (i, p))
    seq = pl.BlockSpec((T, HEAD_PAIR), lambda p, i: (0, p))
    transposed = pltpu.VMEM((T // tk, HEAD_PAIR, tk), F32)
    return pl.pallas_call(
        body, name=name, grid=(N_HEADS // 2, T // tq),
        in_specs=_attn_specs(T, tq) + [blk, blk, AFTER], out_specs=[blk, seq, seq],
        out_shape=[jax.ShapeDtypeStruct((T, SB_WIDTH), BF16)] * 3,
        scratch_shapes=[pltpu.VMEM((T, HEAD_PAIR), BF16), pltpu.VMEM((T // tk, HEAD_PAIR, tk), BF16),
                        pltpu.VMEM((T // tk, HEAD_PAIR, tk), BF16), transposed, transposed],
        compiler_params=_params(("arbitrary", "arbitrary")),
    )(proj, proj, proj, do, ltot, _in_hbm(after))


def _mix_specs(T, D, tm, wbp, w_out):
    gate_col = (POOL_WIDTH + 3 * SB_WIDTH) // D
    row = lambda i: (i, 0)
    return [
        pl.BlockSpec((tm, D), row),
        pl.BlockSpec((tm, POOL_WIDTH), row),
        pl.BlockSpec((tm, SB_WIDTH), row),
        pl.BlockSpec((tm, D), lambda i: (i, gate_col)),
        pl.BlockSpec((tm, D), lambda i: (i, gate_col + 1)),
        pl.BlockSpec(wbp.shape, lambda i: (0, 0)),
        pl.BlockSpec(wbp.shape, lambda i: (0, 0)),
        pl.BlockSpec(w_out.shape, lambda i: (0, 0)),
    ]


def _mix_fwd(h, p, o, proj, wbp, wba, w_out, *, tm, name):
    T, D = h.shape
    tm = min(tm, T)

    def body(h_ref, p_ref, o_ref, glp_ref, gls_ref, wbp_ref, wba_ref, wout_ref, hout_ref, m_ref):
        halves = (pl.ds(0, tm // 2), pl.ds(tm // 2, tm // 2))
        wbp, wba, wout = wbp_ref[...], wba_ref[...], wout_ref[...]
        branches = [(_mm_nt(p_ref[rows, :].astype(BF16), wbp), _mm_nt(o_ref[rows, :].astype(BF16), wba))
                    for rows in halves]
        for rows, (yp, ys) in zip(halves, branches):
            m = (jax.nn.sigmoid(glp_ref[rows, :]) * yp + jax.nn.sigmoid(gls_ref[rows, :]) * ys).astype(BF16)
            m_ref[rows, :] = m
            hout_ref[rows, :] = h_ref[rows, :] + _mm(m, wout)

    row = lambda i: (i, 0)
    return pl.pallas_call(
        body, name=name, grid=(T // tm,),
        in_specs=_mix_specs(T, D, tm, wbp, w_out),
        out_specs=[pl.BlockSpec((tm, D), row), pl.BlockSpec((tm, D), row)],
        out_shape=[jax.ShapeDtypeStruct((T, D), F32), jax.ShapeDtypeStruct((T, D), BF16)],
        compiler_params=_params(("arbitrary",)),
    )(h, p, o, proj, proj, wbp, wba, w_out)


def _mix_bwd(dh, p, o, proj, wbp, wba, w_out, after, *, tm, name):
    T, D = dh.shape
    tm = min(tm, T)

    def body(dh_ref, p_ref, o_ref, glp_ref, gls_ref, wbp_ref, wba_ref, wout_ref, after_ref,
             dyp_ref, dys_ref, dp_ref, do_ref, dgl_ref):
        halves = (pl.ds(0, tm // 2), pl.ds(tm // 2, tm // 2))
        wbp, wba, wout = wbp_ref[...], wba_ref[...], wout_ref[...]
        products = [(_mm_nt(dh_ref[rows, :].astype(BF16), wout), _mm_nt(p_ref[rows, :].astype(BF16), wbp),
                     _mm_nt(o_ref[rows, :].astype(BF16), wba)) for rows in halves]
        for rows, (dm, yp, ys) in zip(halves, products):
            gp = jax.nn.sigmoid(glp_ref[rows, :])
            gs = jax.nn.sigmoid(gls_ref[rows, :])
            dyp = (dm * gp).astype(BF16)
            dys = (dm * gs).astype(BF16)
            dyp_ref[rows, :] = dyp
            dys_ref[rows, :] = dys
            dgl_ref[rows, :D] = (dm * yp * gp * (1.0 - gp)).astype(BF16)
            dgl_ref[rows, D:] = (dm * ys * gs * (1.0 - gs)).astype(BF16)
            dp_ref[rows, :] = _mm(dyp, wbp)
            do_ref[rows, :] = _mm(dys, wba)

    row = lambda i: (i, 0)
    return pl.pallas_call(
        body, name=name, grid=(T // tm,),
        in_specs=_mix_specs(T, D, tm, wbp, w_out) + [AFTER],
        out_specs=[pl.BlockSpec((tm, D), row), pl.BlockSpec((tm, D), row), pl.BlockSpec((tm, POOL_WIDTH), row),
                   pl.BlockSpec((tm, SB_WIDTH), row), pl.BlockSpec((tm, 2 * D), row)],
        out_shape=[jax.ShapeDtypeStruct((T, D), BF16), jax.ShapeDtypeStruct((T, D), BF16),
                   jax.ShapeDtypeStruct((T, POOL_WIDTH), F32), jax.ShapeDtypeStruct((T, SB_WIDTH), F32),
                   jax.ShapeDtypeStruct((T, 2 * D), BF16)],
        compiler_params=_params(("arbitrary",)),
    )(dh, p, o, proj, proj, wbp, wba, w_out, _in_hbm(after))


def _adamw_update(w, g, m, v):
    m_ = ADAM_B1 * m + (1.0 - ADAM_B1) * g
    v_ = ADAM_B2 * v + (1.0 - ADAM_B2) * (g * g)
    m_hat = m_ / (1.0 - ADAM_B1 ** ADAM_STEP)
    v_hat = v_ / (1.0 - ADAM_B2 ** ADAM_STEP)
    return -ADAM_LR * (m_hat / (jnp.sqrt(v_hat) + ADAM_EPS) + ADAM_WD * w), m_, v_


def _adamw(w, g, m, v, *, name):
    R, C = w.shape
    tr = _row_tile(R, C)

    def body(w_ref, g_ref, m_ref, v_ref, d_ref, nm_ref, nv_ref):
        d_ref[...], nm_ref[...], nv_ref[...] = _adamw_update(w_ref[...], g_ref[...], m_ref[...], v_ref[...])

    spec = pl.BlockSpec((tr, C), lambda i: (i, 0))
    return pl.pallas_call(
        body, name=name, grid=(R // tr,), in_specs=[spec] * 4, out_specs=[spec] * 3,
        out_shape=[jax.ShapeDtypeStruct((R, C), F32)] * 3,
        compiler_params=_params(("arbitrary",)),
    )(w, g, m, v)


def _position():
    return lax.axis_index("x"), lax.axis_index("y"), lax.axis_index("c")


def _all_gather(shards, *, name, collective_id):
    n = len(shards)
    n_copies = 9

    def body(*refs):
        ins, outs = refs[:n], refs[n:2 * n]
        send_sems, recv_sems, local_sems = refs[2 * n:]
        x, y, c = _position()
        me, sibling = (x, y, c), (x, y, 1 - c)
        x_nbr, y_nbr, diagonal = (1 - x, y, c), (x, 1 - y, c), (1 - x, 1 - y, c)
        other = lambda pos: (pos[0], pos[1], 1 - c)

        barrier = pltpu.get_barrier_semaphore()
        for peer in (sibling, x_nbr, y_nbr):
            pl.semaphore_signal(barrier, inc=1, device_id=peer, device_id_type=MESH)
        pl.semaphore_wait(barrier, 3)

        def block(a, pos, half=None):
            ref = outs[a].at[4 * pos[0] + 2 * pos[1] + pos[2]]
            rows = ref.shape[0] // 2
            return ref if half is None else ref.at[pl.ds(half * rows, rows)]

        def copy(a, k, pos, to, half=None, src=None):
            return pltpu.make_async_remote_copy(
                src_ref=block(a, pos, half) if src is None else src, dst_ref=block(a, pos, half),
                send_sem=send_sems.at[n_copies * a + k], recv_sem=recv_sems.at[n_copies * a + k],
                device_id=to, device_id_type=MESH)

        started = []
        for a in range(n):
            mine = pltpu.make_async_copy(ins[a], block(a, me), local_sems.at[a])
            mine.start()
            started.append(mine)
        sends = []
        for a in range(n):
            sends += [copy(a, 1, me, x_nbr, src=ins[a]), copy(a, 2, me, y_nbr, src=ins[a]),
                      copy(a, 0, me, sibling, src=ins[a])]
        for cp in sends:
            cp.start()

        def pass_on(copies):
            for cp in copies:
                cp.start()
                sends.append(cp)

        for a in range(n):
            copy(a, 1, x_nbr, me).wait_recv()
            pass_on([copy(a, 5, x_nbr, y_nbr, half=0), copy(a, 3, x_nbr, sibling)])
            copy(a, 2, y_nbr, me).wait_recv()
            pass_on([copy(a, 6, y_nbr, x_nbr, half=1), copy(a, 4, y_nbr, sibling)])
        for a in range(n):
            copy(a, 5, diagonal, me, half=0).wait_recv()
            pass_on([copy(a, 7, diagonal, sibling, half=0)])
            copy(a, 6, diagonal, me, half=1).wait_recv()
            pass_on([copy(a, 8, diagonal, sibling, half=1)])
        for a in range(n):
            copy(a, 0, sibling, me).wait_recv()
            copy(a, 3, other(x_nbr), me).wait_recv()
            copy(a, 4, other(y_nbr), me).wait_recv()
            copy(a, 7, other(diagonal), me, half=0).wait_recv()
            copy(a, 8, other(diagonal), me, half=1).wait_recv()
        for cp in sends:
            cp.wait_send()
        for cp in started:
            cp.wait()

    return pl.kernel(
        body, name=name,
        out_type=[jax.ShapeDtypeStruct((N_DEV,) + s.shape, s.dtype) for s in shards],
        mesh=plsc.ScalarSubcoreMesh(axis_name="sequencer", num_cores=1),
        scratch_types=[pltpu.SemaphoreType.DMA((n_copies * n,)), pltpu.SemaphoreType.DMA((n_copies * n,)),
                       pltpu.SemaphoreType.DMA((n,))],
        compiler_params=pltpu.CompilerParams(collective_id=collective_id),
    )(*shards)


def _chip_sums(group, *, name):
    n = len(group)
    shapes = [g.shape[1:] for g in group]

    def body(*refs):
        g_refs, partials, out_refs = refs[:n], refs[n:3 * n:2], refs[n + 1:3 * n:2]
        mines, theirs = refs[3 * n:5 * n:2], refs[3 * n + 1:5 * n:2]
        send_sems, recv_sems, local_sems = refs[5 * n:]
        x, y, c = _position()
        my_chip = 2 * x + y

        def swap(a, s):
            return pltpu.make_async_remote_copy(
                src_ref=g_refs[a].at[2 * s + (1 - c)], dst_ref=theirs[a].at[s],
                send_sem=send_sems.at[4 * a + s], recv_sem=recv_sems.at[4 * a + s],
                device_id=(x, y, 1 - c), device_id_type=MESH)

        def load(a, s):
            return pltpu.make_async_copy(g_refs[a].at[2 * s + c], mines[a].at[s], local_sems.at[4 * a + s])

        for a in range(n):
            for s in range(4):
                swap(a, s).start()
                load(a, s).start()

        for a, (R, C) in enumerate(shapes):
            rc = 128 if R % 128 == 0 else R

            def chip_sum(chip, rows):
                return mines[a][chip, rows, :].astype(F32) + theirs[a][chip, rows, :].astype(F32)

            for s in range(4):
                load(a, s).wait()
                swap(a, s).wait_recv()

                @pl.when(s == my_chip)
                def _():
                    @pl.loop(0, R // rc)
                    def _(t):
                        rows = pl.ds(pl.multiple_of(t * rc, rc), rc)
                        out_refs[a][rows, :] = chip_sum(s, rows)

                @pl.when(s != my_chip)
                def _():
                    @pl.loop(0, R // rc)
                    def _(t):
                        rows = pl.ds(pl.multiple_of(t * rc, rc), rc)
                        partials[a][(s ^ my_chip) - 1, rows, :] = chip_sum(s, rows).astype(BF16)

        for a in range(n):
            for s in range(4):
                swap(a, s).wait_send()

    vmem = pl.BlockSpec(memory_space=pltpu.VMEM)
    outs = pl.pallas_call(
        body, name=name,
        in_specs=[pl.BlockSpec(memory_space=pl.ANY)] * n, out_specs=[vmem] * (2 * n),
        out_shape=[shape for R, C in shapes
                   for shape in (jax.ShapeDtypeStruct((3, R, C), BF16), jax.ShapeDtypeStruct((R, C), F32))],
        scratch_shapes=[pltpu.VMEM((4, R, C), BF16) for R, C in shapes for _ in range(2)] + [
            pltpu.SemaphoreType.DMA((4 * n,)), pltpu.SemaphoreType.DMA((4 * n,)), pltpu.SemaphoreType.DMA((4 * n,))],
        compiler_params=_params(),
    )(*group)
    return [(outs[2 * a], outs[2 * a + 1]) for a in range(n)]


def _cross_chips(partials, *, name, collective_id):
    n = len(partials)

    def body(*refs):
        ins, outs = refs[:n], refs[n:2 * n]
        send_sems, recv_sems = refs[2 * n:]
        x, y, c = _position()
        my_chip = 2 * x + y
        peers = [((my_chip ^ j) // 2, (my_chip ^ j) % 2, c) for j in (1, 2, 3)]

        barrier = pltpu.get_barrier_semaphore()
        for peer in peers:
            pl.semaphore_signal(barrier, inc=1, device_id=peer, device_id_type=MESH)
        pl.semaphore_wait(barrier, 3)

        copies = [
            pltpu.make_async_remote_copy(
                src_ref=ins[a].at[j], dst_ref=outs[a].at[j],
                send_sem=send_sems.at[3 * a + j], recv_sem=recv_sems.at[3 * a + j],
                device_id=peers[j], device_id_type=MESH)
            for a in range(n) for j in range(3)]
        for cp in copies:
            cp.start()
        for cp in copies:
            cp.wait_recv()
        for cp in copies:
            cp.wait_send()

    return pl.kernel(
        body, name=name,
        out_type=[jax.ShapeDtypeStruct(p.shape, p.dtype) for p in partials],
        mesh=plsc.ScalarSubcoreMesh(axis_name="sequencer", num_cores=1),
        scratch_types=[pltpu.SemaphoreType.DMA((3 * n,)), pltpu.SemaphoreType.DMA((3 * n,))],
        compiler_params=pltpu.CompilerParams(collective_id=collective_id),
    )(*partials)


def _cross_chips_and_gather(partials, slab, *, name, collective_id):
    n = len(partials)

    def body(*refs):
        part_refs, slab_ref = refs[:n], refs[n]
        landed_refs, slabs_ref = refs[n + 1:2 * n + 1], refs[2 * n + 1]
        send_sems, recv_sems, local_sem = refs[2 * n + 2:]
        x, y, c = _position()
        me, my_chip = 4 * x + 2 * y + c, 2 * x + y
        others = [me ^ k for k in range(1, N_DEV)]
        ids = [(o // 4, (o // 2) % 2, o % 2) for o in others]

        barrier = pltpu.get_barrier_semaphore()
        for peer in ids:
            pl.semaphore_signal(barrier, inc=1, device_id=peer, device_id_type=MESH)
        pl.semaphore_wait(barrier, N_DEV - 1)

        mine = pltpu.make_async_copy(slab_ref, slabs_ref.at[me], local_sem)
        mine.start()
        sends = [
            pltpu.make_async_remote_copy(
                src_ref=part_refs[a].at[j], dst_ref=landed_refs[a].at[j],
                send_sem=send_sems.at[3 * a + j], recv_sem=recv_sems.at[3 * a + j],
                device_id=((my_chip ^ (j + 1)) // 2, (my_chip ^ (j + 1)) % 2, c), device_id_type=MESH)
            for a in range(n) for j in range(3)]
        sends += [
            pltpu.make_async_remote_copy(
                src_ref=slab_ref, dst_ref=slabs_ref.at[me],
                send_sem=send_sems.at[3 * n + k], recv_sem=recv_sems.at[3 * n + k],
                device_id=ids[k], device_id_type=MESH)
            for k in range(N_DEV - 1)]
        arrivals = sends[:3 * n] + [
            pltpu.make_async_remote_copy(
                src_ref=slab_ref, dst_ref=slabs_ref.at[others[k]],
                send_sem=send_sems.at[3 * n + k], recv_sem=recv_sems.at[3 * n + k],
                device_id=ids[k], device_id_type=MESH)
            for k in range(N_DEV - 1)]
        for cp in sends:
            cp.start()
        for cp in arrivals:
            cp.wait_recv()
        for cp in sends:
            cp.wait_send()
        mine.wait()

    n_sems = 3 * n + N_DEV - 1
    outs = pl.kernel(
        body, name=name,
        out_type=[jax.ShapeDtypeStruct(p.shape, p.dtype) for p in partials]
                 + [jax.ShapeDtypeStruct((N_DEV,) + slab.shape, slab.dtype)],
        mesh=plsc.ScalarSubcoreMesh(axis_name="sequencer", num_cores=1),
        scratch_types=[pltpu.SemaphoreType.DMA((n_sems,)), pltpu.SemaphoreType.DMA((n_sems,)), pltpu.SemaphoreType.DMA],
        compiler_params=pltpu.CompilerParams(collective_id=collective_id),
    )(*partials, slab)
    return outs[:n], outs[n]


def _sum_devices(gathered, after, *, name):
    _, R, C = gathered.shape

    def body(in_ref, after_ref, out_ref):
        total = in_ref[0]
        for d in range(1, N_DEV):
            total = total + in_ref[d]
        out_ref[...] = total

    return pl.pallas_call(
        body, name=name, grid=(1,),
        in_specs=[pl.BlockSpec((N_DEV, R, C), lambda i: (0, 0, 0)), AFTER],
        out_specs=pl.BlockSpec((R, C), lambda i: (0, 0)),
        out_shape=jax.ShapeDtypeStruct((R, C), F32),
        compiler_params=_params(("arbitrary",)),
    )(gathered, _in_hbm(after))


def _owner_sum(own, landed, after, *, name):
    R, C = own.shape
    tr = _row_tile(R, C)

    def body(own_ref, landed_ref, after_ref, out_ref):
        total = own_ref[...]
        for j in range(3):
            total = total + landed_ref[j].astype(F32)
        out_ref[...] = total

    return pl.pallas_call(
        body, name=name, grid=(R // tr,),
        in_specs=[pl.BlockSpec((tr, C), lambda i: (i, 0)), pl.BlockSpec((3, tr, C), lambda i: (0, i, 0)), AFTER],
        out_specs=pl.BlockSpec((tr, C), lambda i: (i, 0)),
        out_shape=jax.ShapeDtypeStruct((R, C), F32),
        compiler_params=_params(("arbitrary",)),
    )(own, landed, _in_hbm(after))


def _owner_sum_adamw(own, landed, w, m, v, after, *, name):
    H, R, C = w.shape
    tr = R // 2

    def body(own_ref, landed_ref, w_ref, m_ref, v_ref, after_ref, g_ref, d_ref, nm_ref, nv_ref):
        total = own_ref[...]
        for j in range(3):
            total = total + landed_ref[j].astype(F32)
        g_ref[...] = total
        d_ref[...], nm_ref[...], nv_ref[...] = _adamw_update(w_ref[...], total, m_ref[...], v_ref[...])

    spec = pl.BlockSpec((None, tr, C), lambda h, i: (h, i, 0))
    return pl.pallas_call(
        body, name=name, grid=(H, R // tr),
        in_specs=[spec, pl.BlockSpec((3, None, tr, C), lambda h, i: (0, h, i, 0)), spec, spec, spec, AFTER],
        out_specs=[spec] * 4,
        out_shape=[jax.ShapeDtypeStruct((H, R, C), F32)] * 4,
        compiler_params=_params(("arbitrary", "arbitrary")),
    )(own, landed, w, m, v, _in_hbm(after))


def _local_step(x, target, norms, pool_w_group, pool_scale, wgu1, wd1, w_in, wbp, wba, w_out, wgu2, wd2, exchange):
    n1g, nmg, n2g, nfg = norms
    D = x.shape[1]
    gu1, hid1 = _ffn_up(x, n1g, wgu1, tm=1024, name="ffn1_up")
    h1 = _ffn_down(x, hid1, wd1, tm=512, name="ffn1_down")
    un, proj = _inproj_fwd(h1, nmg, w_in, tm=1024, name="inproj_fwd")
    p = _pool_fwd(proj, pool_w_group, pool_scale, name="pool_fwd")
    o, ltot = _attn_fwd(proj, name="attn_fwd")
    h2, m = _mix_fwd(h1, p, o, proj, wbp, wba, w_out, tm=512, name="mix_fwd")
    gu2, hid2 = _ffn_up(h2, n2g, wgu2, tm=1024, name="ffn2_up")
    h3 = _ffn_down(h2, hid2, wd2, tm=512, name="ffn2_down")
    dh3, df2, loss, d_nf = _loss_bwd(h3, target, nfg, tm=256, name="loss_bwd")

    dh2, d_n2, n2, dgu2 = _ffn_bwd(dh3, df2, h2, n2g, gu2, wgu2, wd2, df2, tm=512, name="ffn2_bwd")
    d_wd2 = _wgrad_down(hid2, df2, tk=WGRAD_TOKENS, name="ffn2_wgrad_down")
    d_wgu2 = _wgrad_gate_up(n2, dgu2, tk=WGRAD_TOKENS, name="ffn2_wgrad_gate_up")
    (g_wd2, g_wgu2), token = exchange("ffn2", [d_wd2.reshape(N_DEV, FF_SHARD_PAD, D), d_wgu2])

    dyp, dys, dp, do, dgl = _mix_bwd(dh2, p, o, proj, wbp, wba, w_out, token, tm=512, name="mix_bwd")
    d_wout = _wgrad_full(m, dh2, tk=WGRAD_TOKENS, name="wgrad_out")
    d_wbp = _wgrad_full(dyp, p, tk=WGRAD_TOKENS, name="wgrad_branch_pool")
    d_wba = _wgrad_full(dys, o, tk=WGRAD_TOKENS, name="wgrad_branch_attn")
    by_owner = lambda g: g.reshape(N_DEV, g.shape[0] // N_DEV, g.shape[1])
    (g_wbp, g_wba, g_wout), token = exchange("mix", [by_owner(d_wbp), by_owner(d_wba), by_owner(d_wout)])
    dxp, d_wgroup, d_scale = _pool_bwd(dp, proj, pool_w_group, pool_scale, name="pool_bwd")
    dq, dk, dv = _attn_bwd(proj, do, ltot, token, name="attn_bwd")
    dproj_parts = [dxp, dq, dk, dv, dgl]
    dh1, df1, d_nm = _inproj_bwd(dproj_parts, dh2, h1, nmg, w_in, tm=512, name="inproj_bwd")
    d_win = _wgrad_in(dproj_parts, un, name="wgrad_in")
    d_wd1 = _wgrad_down(hid1, df1, tk=WGRAD_TOKENS, name="ffn1_wgrad_down")
    (g_win, g_wd1, replicated_early), token = exchange(
        "w_in_ffn1_down", [d_win, d_wd1.reshape(N_DEV, FF_SHARD_PAD, D), d_nm, d_n2, d_nf, d_scale, d_wgroup, loss])

    dx, d_n1, n1, dgu1 = _ffn_bwd(dh1, df1, x, n1g, gu1, wgu1, wd1, token, tm=512, name="ffn1_bwd")
    d_wgu1_a = _wgrad_gate_up(n1, dgu1, tk=WGRAD_TOKENS, name="ffn1_wgrad_gate_up_a", part=0, parts=2)
    (g_wgu1_a, replicated_late), token = exchange("ffn1_gate_up_a", [d_wgu1_a, d_n1])
    d_wgu1_b = _wgrad_gate_up(n1, dgu1, tk=WGRAD_TOKENS, name="ffn1_wgrad_gate_up_b", part=1, parts=2)
    (g_wgu1_b,), token = exchange("last", [d_wgu1_b])
    g_wgu1 = (g_wgu1_a, g_wgu1_b)

    sharded = (g_wgu1, g_wd1, g_win, g_wbp, g_wba, g_wout, g_wgu2, g_wd2)
    return dx, sharded, (replicated_late, replicated_early), token


def _hidden_major(w):
    return jnp.swapaxes(w[0], 0, 1)


def _pad_gate_up(wt):
    d = wt.shape[1]
    wt = wt.astype(BF16).reshape(2, FF_SHARD, d)
    return jnp.pad(wt, ((0, 0), (0, FF_SHARD_PAD - FF_SHARD), (0, 0))).reshape(2 * FF_SHARD_PAD, d)


def _unpad_gate_up(gt):
    d = gt.shape[1]
    return gt.reshape(2, FF_SHARD_PAD, d)[:, :FF_SHARD].reshape(2 * FF_SHARD, d)


def _pad_down(w):
    return jnp.pad(w.astype(BF16), ((0, FF_SHARD_PAD - FF_SHARD), (0, 0)))


def kernel(x, ffn1_norm, ffn1_w_gate_up, ffn1_w_down, mix_norm, w_in, pool_w_group, pool_scale, w_branch_pool, w_branch_attn, w_out, ffn2_norm, ffn2_w_gate_up, ffn2_w_down, final_norm, loss_target, m_ffn1_norm, m_ffn1_w_gate_up, m_ffn1_w_down, m_mix_norm, m_w_in, m_pool_w_group, m_pool_scale, m_w_branch_pool, m_w_branch_attn, m_w_out, m_ffn2_norm, m_ffn2_w_gate_up, m_ffn2_w_down, m_final_norm, v_ffn1_norm, v_ffn1_w_gate_up, v_ffn1_w_down, v_mix_norm, v_w_in, v_pool_w_group, v_pool_scale, v_w_branch_pool, v_w_branch_attn, v_w_out, v_ffn2_norm, v_ffn2_w_gate_up, v_ffn2_w_down, v_final_norm):
    D = x.shape[-1]
    weights = dict(ffn1_norm=ffn1_norm, ffn1_w_gate_up=ffn1_w_gate_up, ffn1_w_down=ffn1_w_down, mix_norm=mix_norm,
                   w_in=w_in, pool_w_group=pool_w_group, pool_scale=pool_scale, w_branch_pool=w_branch_pool,
                   w_branch_attn=w_branch_attn, w_out=w_out, ffn2_norm=ffn2_norm, ffn2_w_gate_up=ffn2_w_gate_up,
                   ffn2_w_down=ffn2_w_down, final_norm=final_norm)
    first = dict(ffn1_norm=m_ffn1_norm, ffn1_w_gate_up=m_ffn1_w_gate_up, ffn1_w_down=m_ffn1_w_down,
                 mix_norm=m_mix_norm, w_in=m_w_in, pool_w_group=m_pool_w_group, pool_scale=m_pool_scale,
                 w_branch_pool=m_w_branch_pool, w_branch_attn=m_w_branch_attn, w_out=m_w_out,
                 ffn2_norm=m_ffn2_norm, ffn2_w_gate_up=m_ffn2_w_gate_up, ffn2_w_down=m_ffn2_w_down,
                 final_norm=m_final_norm)
    second = dict(ffn1_norm=v_ffn1_norm, ffn1_w_gate_up=v_ffn1_w_gate_up, ffn1_w_down=v_ffn1_w_down,
                  mix_norm=v_mix_norm, w_in=v_w_in, pool_w_group=v_pool_w_group, pool_scale=v_pool_scale,
                  w_branch_pool=v_w_branch_pool, w_branch_attn=v_w_branch_attn, w_out=v_w_out,
                  ffn2_norm=v_ffn2_norm, ffn2_w_gate_up=v_ffn2_w_gate_up, ffn2_w_down=v_ffn2_w_down,
                  final_norm=v_final_norm)
    order = list(weights)

    wgu1, = _all_gather([_pad_gate_up(_hidden_major(ffn1_w_gate_up))], name="all_gather_ffn1_gate_up", collective_id=0)
    wd1, = _all_gather([_pad_down(ffn1_w_down[0])], name="all_gather_ffn1_down", collective_id=10)
    transposed = lambda w: jnp.swapaxes(w[0], 0, 1).astype(BF16)
    win_g, = _all_gather([transposed(w_in)], name="all_gather_w_in", collective_id=1)
    wbp_g, wba_g = _all_gather([transposed(w_branch_pool), transposed(w_branch_attn)],
                               name="all_gather_branches", collective_id=2)
    wout_g, = _all_gather([w_out[0].astype(BF16)], name="all_gather_w_out", collective_id=11)
    wgu2, wd2 = _all_gather([_pad_gate_up(_hidden_major(ffn2_w_gate_up)), _pad_down(ffn2_w_down[0])],
                            name="all_gather_ffn2", collective_id=3)
    whole = lambda g: g.reshape(g.shape[0] * g.shape[1], g.shape[2])
    wd1, wd2, win_g, wbp_g, wba_g, wout_g = (whole(g) for g in (wd1, wd2, win_g, wbp_g, wba_g, wout_g))

    cross_ids = {"ffn2": 4, "mix": 5, "w_in_ffn1_down": 8, "ffn1_gate_up_a": 9, "last": 7}
    small = ["ffn1_norm", "mix_norm", "ffn2_norm", "final_norm", "pool_scale", "pool_w_group"]

    def tile_rows(a):
        a = a.reshape(-1, 128)
        return jnp.pad(a, ((0, -a.shape[0] % 8), (0, 0)))

    def exchange(tag, group):
        grads = [g for g in group if g.dtype == BF16]
        extras = [tile_rows(g) for g in group if g.dtype != BF16]
        sums = _chip_sums(grads, name="chip_sums_" + tag)
        partials = [s[0] for s in sums]
        handles = []
        if extras:
            landed, slabs = _cross_chips_and_gather(partials, jnp.concatenate(extras, axis=0),
                                                    name="cross_chips_" + tag, collective_id=cross_ids[tag])
            handles = [slabs]
        else:
            landed = _cross_chips(partials, name="cross_chips_" + tag, collective_id=cross_ids[tag])
        return [(s[1], l) for s, l in zip(sums, landed)] + handles, sums[-1][1]

    norms = (ffn1_norm, mix_norm, ffn2_norm, final_norm.reshape(1, D))
    dx, sharded, (slabs_late, slabs_early), last = _local_step(
        x[0], loss_target[0], norms, pool_w_group[0], pool_scale, wgu1, wd1, win_g, wbp_g, wba_g, wout_g, wgu2, wd2,
        exchange)
    names = ["ffn1_w_gate_up", "ffn1_w_down", "w_in", "w_branch_pool", "w_branch_attn", "w_out",
             "ffn2_w_gate_up", "ffn2_w_down"]
    handles = dict(zip(names, sharded))
    grads, delta, new_m, new_v = {}, {}, {}, {}
    loss_out = []

    def update_replicated(after):
        rows = [weights[k].size // 128 for k in small]
        padded_rows = [-(-r // 8) * 8 for r in rows]
        starts = [sum(padded_rows[:i]) for i in range(len(rows) + 1)]
        total = jnp.concatenate([_sum_devices(slabs_late, after, name="sum_replicated_late"),
                                 _sum_devices(slabs_early, after, name="sum_replicated_early")], axis=0)
        loss_out.append(total[starts[-1], 0])
        small_w = jnp.concatenate([tile_rows(weights[k]) for k in small], axis=0)
        small_m = jnp.concatenate([tile_rows(first[k]) for k in small], axis=0)
        small_v = jnp.concatenate([tile_rows(second[k]) for k in small], axis=0)
        small_out = _adamw(small_w, total[:starts[-1]], small_m, small_v, name="adamw_replicated")
        for name_, start, n_rows in zip(small, starts, rows):
            shape = weights[name_].shape
            grads[name_] = total[start:start + n_rows].reshape(shape)
            delta[name_], new_m[name_], new_v[name_] = (a[start:start + n_rows].reshape(shape) for a in small_out)
        return small_out[0]

    after = last
    for k in ("ffn2_w_down", "ffn2_w_gate_up", "w_branch_pool", "w_branch_attn", "w_out", "w_in", "ffn1_w_down",
              "ffn1_w_gate_up"):
        hidden_major = k.endswith("w_gate_up")
        view = _hidden_major if hidden_major else (lambda a: a[0])
        back = (lambda a: jnp.swapaxes(a, 0, 1)[None]) if hidden_major else (lambda a: a[None])
        if k in ("ffn2_w_down", "ffn2_w_gate_up", "w_out", "ffn1_w_down"):
            own, landed = handles[k]
            groups = 2 if hidden_major else 1
            by_group = lambda a: a.reshape(a.shape[:-2] + (groups, a.shape[-2] // groups, a.shape[-1]))
            out = _owner_sum_adamw(by_group(own), by_group(landed), by_group(view(weights[k])),
                                   by_group(view(first[k])), by_group(view(second[k])), after, name="adamw_" + k)
            after = out[1]
            grads[k], delta[k], new_m[k], new_v[k] = (back(a.reshape(-1, a.shape[-1])) for a in out)
            continue
        if isinstance(handles[k][0], tuple):
            first_half = _owner_sum(*handles[k][0], after, name="owner_sum_" + k + "_a")
            second_half = _owner_sum(*handles[k][1], update_replicated(first_half), name="owner_sum_" + k + "_b")
            g = jnp.concatenate([first_half[:FF_SHARD], second_half[:FF_SHARD]], axis=0)
        else:
            g = jnp.swapaxes(_owner_sum(*handles[k], after, name="owner_sum_" + k), 0, 1)
        out = _adamw(view(weights[k]), g, view(first[k]), view(second[k]), name="adamw_" + k)
        after = out[0]
        grads[k] = back(g)
        delta[k], new_m[k], new_v[k] = (back(a) for a in out)

    return (loss_out[0], dx[None], *[grads[k] for k in order], *[delta[k] for k in order],
            *[new_m[k] for k in order], *[new_v[k] for k in order])
```

```python
import jax
import jax.numpy as jnp
from jax import lax
from jax.experimental import pallas as pl
from jax.experimental.pallas import tpu as pltpu
from jax.experimental.pallas import tpu_sc as plsc

F32 = jnp.float32
BF16 = jnp.bfloat16
MESH = pl.DeviceIdType.MESH

RMS_EPS = 1e-6
N_DEV = 8
N_HEADS = 8
HEAD_DIM = 64
HEAD_PAIR = 2 * HEAD_DIM
POOL_WINDOWS = (2, 4, 8, 16)
POOL_GROUP = 128
POOL_WIDTH = 512
SB_WIDTH = 512
FF_SHARD = 352
FF_SHARD_PAD = 384
ATTN_K_BLOCK = 256
ATTN_Q_BLOCK_FWD = 512
ATTN_Q_BLOCK_BWD = 256
ATTN_SCALE = 0.125

ADAM_LR = 0.001
ADAM_B1 = 0.9
ADAM_B2 = 0.999
ADAM_EPS = 1e-08
ADAM_WD = 0.01
ADAM_STEP = 10

VMEM_LIMIT = 48 << 20
WGRAD_TOKENS = 2048


def _params(dims=None):
    return pltpu.CompilerParams(dimension_semantics=dims, vmem_limit_bytes=VMEM_LIMIT)


def _mm(a, b):
    return jnp.dot(a, b, preferred_element_type=F32)


def _mm_nt(a, b):
    return lax.dot_general(a, b, (((1,), (1,)), ((), ())), preferred_element_type=F32)


def _mm_tn(a, b):
    return lax.dot_general(a, b, (((0,), (0,)), ((), ())), preferred_element_type=F32)


def _row_tile(rows, cols):
    limit = max(8, (512 * 1024) // cols)
    return max(t for t in range(8, rows + 1, 8) if rows % t == 0 and (t <= limit or t == 8))


def _rstd(xf):
    return lax.rsqrt(jnp.mean(xf * xf, axis=-1, keepdims=True) + RMS_EPS)


def _rms_bwd(xf, gain, dn):
    r = _rstd(xf)
    xh = xf * r
    dgain = jnp.sum(dn * xh, axis=0, keepdims=True)
    dxh = dn * gain
    dx = r * (dxh - xh * jnp.mean(dxh * xh, axis=-1, keepdims=True))
    return dx, dgain


def _ffn_up(x, gain, wgu, *, tm, name):
    T, D = x.shape
    tm = min(tm, T)
    nb, bw = wgu.shape[0] // 2, wgu.shape[1]

    def body(x_ref, gain_ref, wg_ref, wu_ref, gu_ref, hid_ref, n_scr):
        @pl.when(pl.program_id(1) == 0)
        def _():
            xf = x_ref[...]
            n_scr[...] = (xf * _rstd(xf) * gain_ref[...]).astype(BF16)

        halves = (pl.ds(0, tm // 2), pl.ds(tm // 2, tm // 2))
        wg, wu = wg_ref[...], wu_ref[...]
        gus = [(_mm_nt(n_scr[rows, :], wg), _mm_nt(n_scr[rows, :], wu)) for rows in halves]
        for rows, (g, u) in zip(halves, gus):
            gu_ref[0, rows, :] = g.astype(BF16)
            gu_ref[1, rows, :] = u.astype(BF16)
            hid_ref[rows, :] = (g * jax.nn.sigmoid(g) * u).astype(BF16)

    return pl.pallas_call(
        body, name=name, grid=(T // tm, nb),
        in_specs=[
            pl.BlockSpec((tm, D), lambda i, j: (i, 0)),
            pl.BlockSpec((1, D), lambda i, j: (0, 0)),
            pl.BlockSpec((None, bw, D), lambda i, j: (j, 0, 0)),
            pl.BlockSpec((None, bw, D), lambda i, j: (j + nb, 0, 0)),
        ],
        out_specs=[
            pl.BlockSpec((2, tm, bw), lambda i, j: (0, i, j)),
            pl.BlockSpec((tm, bw), lambda i, j: (i, j)),
        ],
        out_shape=[jax.ShapeDtypeStruct((2, T, nb * bw), BF16), jax.ShapeDtypeStruct((T, nb * bw), BF16)],
        scratch_shapes=[pltpu.VMEM((tm, D), BF16)],
        compiler_params=_params(("arbitrary", "arbitrary")),
    )(x, gain, wgu, wgu)


def _ffn_down(x, hid, wd, *, tm, name):
    T, D = x.shape
    tm = min(tm, T)
    F = hid.shape[1]

    def body(x_ref, hid_ref, wd_ref, h_ref):
        h_ref[...] = x_ref[...] + 0.5 * _mm(hid_ref[...], wd_ref[...])

    return pl.pallas_call(
        body, name=name, grid=(T // tm,),
        in_specs=[
            pl.BlockSpec((tm, D), lambda i: (i, 0)),
            pl.BlockSpec((tm, F), lambda i: (i, 0)),
            pl.BlockSpec((F, D), lambda i: (0, 0)),
        ],
        out_specs=pl.BlockSpec((tm, D), lambda i: (i, 0)),
        out_shape=jax.ShapeDtypeStruct((T, D), F32),
        compiler_params=_params(("arbitrary",)),
    )(x, hid, wd)


AFTER = pl.BlockSpec(memory_space=pltpu.HBM)


def _in_hbm(token):
    return pltpu.with_memory_space_constraint(token, pltpu.HBM)


def _ffn_bwd(dh, df, x, gain, gu, wgu, wd, after, *, tm, name):
    T, D = x.shape
    tm = min(tm, T)
    nb, bw = wgu.shape[0] // 2, wgu.shape[1]

    def body(dh_ref, df_ref, x_ref, gain_ref, gu_ref, wg_ref, wu_ref, wd_ref, after_ref,
             dx_ref, dgain_ref, n_ref, dgu_ref, dn_acc):
        i, j = pl.program_id(0), pl.program_id(1)

        @pl.when(j == 0)
        def _():
            xf = x_ref[...]
            n_ref[...] = (xf * _rstd(xf) * gain_ref[...]).astype(BF16)
            dn_acc[...] = jnp.zeros_like(dn_acc)

        @pl.when((i == 0) & (j == 0))
        def _():
            dgain_ref[...] = jnp.zeros_like(dgain_ref)

        halves = (pl.ds(0, tm // 2), pl.ds(tm // 2, tm // 2))
        wd, wg, wu = wd_ref[...], wg_ref[...], wu_ref[...]
        dhids = [_mm_nt(df_ref[rows, :], wd) for rows in halves]
        for rows, dhid in zip(halves, dhids):
            g = gu_ref[0, rows, :].astype(F32)
            u = gu_ref[1, rows, :].astype(F32)
            s = jax.nn.sigmoid(g)
            silu = g * s
            dg = (dhid * u * (s * (1.0 + g * (1.0 - s)))).astype(BF16)
            du = (dhid * silu).astype(BF16)
            dgu_ref[0, rows, :] = dg
            dgu_ref[1, rows, :] = du
            dn_acc[rows, :] += _mm(dg, wg) + _mm(du, wu)

        @pl.when(j == nb - 1)
        def _():
            dx, dgain = _rms_bwd(x_ref[...], gain_ref[...], dn_acc[...])
            dx_ref[...] = dh_ref[...] + dx
            dgain_ref[...] += dgain

    row = lambda i, j: (i, 0)
    return pl.pallas_call(
        body, name=name, grid=(T // tm, nb),
        in_specs=[
            pl.BlockSpec((tm, D), row),
            pl.BlockSpec((tm, D), row),
            pl.BlockSpec((tm, D), row),
            pl.BlockSpec((1, D), lambda i, j: (0, 0)),
            pl.BlockSpec((2, tm, bw), lambda i, j: (0, i, j)),
            pl.BlockSpec((None, bw, D), lambda i, j: (j, 0, 0)),
            pl.BlockSpec((None, bw, D), lambda i, j: (j + nb, 0, 0)),
            pl.BlockSpec((bw, D), lambda i, j: (j, 0)),
            AFTER,
        ],
        out_specs=[
            pl.BlockSpec((tm, D), row),
            pl.BlockSpec((1, D), lambda i, j: (0, 0)),
            pl.BlockSpec((tm, D), row),
            pl.BlockSpec((2, tm, bw), lambda i, j: (0, i, j)),
        ],
        out_shape=[
            jax.ShapeDtypeStruct((T, D), F32),
            jax.ShapeDtypeStruct((1, D), F32),
            jax.ShapeDtypeStruct((T, D), BF16),
            jax.ShapeDtypeStruct((2, T, nb * bw), BF16),
        ],
        scratch_shapes=[pltpu.VMEM((tm, D), F32)],
        compiler_params=_params(("arbitrary", "arbitrary")),
    )(dh, df, x, gain, gu, wgu, wgu, wd, _in_hbm(after))


def _wgrad(a, b, *, grid, a_spec, b_spec, out_spec, out_shape, acc_shape, name):
    nk = grid[2]

    def body(a_ref, b_ref, o_ref, acc):
        k = pl.program_id(2)

        @pl.when(k == 0)
        def _():
            acc[...] = jnp.zeros_like(acc)

        acc[...] += _mm_tn(a_ref[...].astype(BF16), b_ref[...].astype(BF16))

        @pl.when(k == nk - 1)
        def _():
            o_ref[...] = acc[...].astype(o_ref.dtype)

    return pl.pallas_call(
        body, name=name, grid=grid, in_specs=[a_spec, b_spec], out_specs=out_spec,
        out_shape=jax.ShapeDtypeStruct(out_shape, BF16),
        scratch_shapes=[pltpu.VMEM(acc_shape, F32)],
        compiler_params=_params(("arbitrary", "arbitrary", "arbitrary")),
    )(a, b)


def _wgrad_gate_up(n, dgu, *, tk, name, part=0, parts=1):
    T, D = n.shape
    tk = min(tk, T)
    owner_rows = FF_SHARD_PAD * 2
    nb = dgu.shape[2] // owner_rows
    bw = owner_rows // parts
    return _wgrad(
        dgu, n, grid=(2 * nb, 1, T // tk), name=name,
        a_spec=pl.BlockSpec((None, tk, bw), lambda m, c, k: (m // nb, k, parts * (m % nb) + part)),
        b_spec=pl.BlockSpec((tk, D), lambda m, c, k: (k, 0)),
        out_spec=pl.BlockSpec((None, bw, D), lambda m, c, k: (m, 0, 0)),
        out_shape=(2 * nb, bw, D), acc_shape=(bw, D))


def _wgrad_down(hid, df, *, tk, name):
    T, D = df.shape
    tk = min(tk, T)
    bw = FF_SHARD_PAD * 2
    nb = hid.shape[1] // bw
    return _wgrad(
        hid, df, grid=(nb, 1, T // tk), name=name,
        a_spec=pl.BlockSpec((tk, bw), lambda m, c, k: (k, m)),
        b_spec=pl.BlockSpec((tk, D), lambda m, c, k: (k, 0)),
        out_spec=pl.BlockSpec((bw, D), lambda m, c, k: (m, 0)),
        out_shape=(nb * bw, D), acc_shape=(bw, D))


def _wgrad_in(dparts, un, *, name):
    T, D = un.shape
    bw = sum(p.shape[1] for p in dparts) // N_DEV
    first = [sum(p.shape[1] for p in dparts[:i]) // bw for i in range(len(dparts) + 1)]

    def body(*refs):
        dp_refs, un_ref, o_ref = refs[:-2], refs[-2], refs[-1]
        m = pl.program_id(0)
        for dp_ref, lo, hi in zip(dp_refs, first[:-1], first[1:]):
            @pl.when((m >= lo) & (m < hi))
            def _():
                o_ref[...] = _mm_tn(dp_ref[...], un_ref[...]).astype(o_ref.dtype)

    def piece_spec(lo, hi):
        return pl.BlockSpec((T, bw), lambda m: (0, jnp.clip(m - lo, 0, hi - lo - 1)))

    return pl.pallas_call(
        body, name=name, grid=(N_DEV,),
        in_specs=[piece_spec(lo, hi) for lo, hi in zip(first[:-1], first[1:])] + [pl.BlockSpec((T, D), lambda m: (0, 0))],
        out_specs=pl.BlockSpec((None, bw, D), lambda m: (m, 0, 0)),
        out_shape=jax.ShapeDtypeStruct((N_DEV, bw, D), BF16),
        compiler_params=_params(("arbitrary",)),
    )(*dparts, un)


def _wgrad_full(a, b, *, tk, name):
    T, M = a.shape
    tk = min(tk, T)
    N = b.shape[1]
    return _wgrad(
        a, b, grid=(1, 1, T // tk), name=name,
        a_spec=pl.BlockSpec((tk, M), lambda m, c, k: (k, 0)),
        b_spec=pl.BlockSpec((tk, N), lambda m, c, k: (k, 0)),
        out_spec=pl.BlockSpec((M, N), lambda m, c, k: (0, 0)), out_shape=(M, N), acc_shape=(M, N))


def _loss_bwd(h, target, gain, *, tm, name):
    T, D = h.shape
    tm = min(tm, T)

    def body(h_ref, t_ref, gain_ref, dh_ref, df_ref, loss_ref, dgain_ref):
        @pl.when(pl.program_id(0) == 0)
        def _():
            loss_ref[...] = jnp.zeros_like(loss_ref)
            dgain_ref[...] = jnp.zeros_like(dgain_ref)

        xf = h_ref[...]
        gain = gain_ref[...]
        err = xf * _rstd(xf) * gain - t_ref[...]
        loss_ref[...] += 0.5 * jnp.sum(jnp.mean(err * err, axis=-1, keepdims=True), axis=0, keepdims=True)
        dx, dgain = _rms_bwd(xf, gain, err * (1.0 / D))
        dh_ref[...] = dx
        df_ref[...] = (0.5 * dx).astype(BF16)
        dgain_ref[...] += dgain

    row = lambda i: (i, 0)
    fixed = lambda i: (0, 0)
    return pl.pallas_call(
        body, name=name, grid=(T // tm,),
        in_specs=[pl.BlockSpec((tm, D), row), pl.BlockSpec((tm, D), row), pl.BlockSpec((1, D), fixed)],
        out_specs=[pl.BlockSpec((tm, D), row), pl.BlockSpec((tm, D), row), pl.BlockSpec((1, 128), fixed),
                   pl.BlockSpec((1, D), fixed)],
        out_shape=[jax.ShapeDtypeStruct((T, D), F32), jax.ShapeDtypeStruct((T, D), BF16),
                   jax.ShapeDtypeStruct((1, 128), F32), jax.ShapeDtypeStruct((1, D), F32)],
        compiler_params=_params(("arbitrary",)),
    )(h, target, gain)


def _inproj_fwd(h, gain, w_in_t, *, tm, name):
    T, D = h.shape
    tm = min(tm, T)
    bn = D
    nb = w_in_t.shape[0] // bn

    def body(h_ref, gain_ref, wt_ref, un_ref, proj_ref):
        @pl.when(pl.program_id(1) == 0)
        def _():
            xf = h_ref[...]
            un_ref[...] = (xf * _rstd(xf) * gain_ref[...]).astype(BF16)

        proj_ref[...] = _mm_nt(un_ref[...], wt_ref[...])

    return pl.pallas_call(
        body, name=name, grid=(T // tm, nb),
        in_specs=[
            pl.BlockSpec((tm, D), lambda i, j: (i, 0)),
            pl.BlockSpec((1, D), lambda i, j: (0, 0)),
            pl.BlockSpec((bn, D), lambda i, j: (j, 0)),
        ],
        out_specs=[pl.BlockSpec((tm, D), lambda i, j: (i, 0)), pl.BlockSpec((tm, bn), lambda i, j: (i, j))],
        out_shape=[jax.ShapeDtypeStruct((T, D), BF16), jax.ShapeDtypeStruct((T, nb * bn), F32)],
        compiler_params=_params(("arbitrary", "arbitrary")),
    )(h, gain, w_in_t)


def _inproj_bwd(dparts, dh, h, gain, w_in_t, *, tm, name):
    T, D = h.shape
    tm = min(tm, T)
    n = len(dparts)
    widths = [p.shape[1] for p in dparts]
    starts = [sum(widths[:i]) for i in range(n)]

    def body(*refs):
        dp_refs = refs[:n]
        dh_ref, h_ref, gain_ref, wt_ref, dx_ref, df_ref, dgain_ref = refs[n:]

        @pl.when(pl.program_id(0) == 0)
        def _():
            dgain_ref[...] = jnp.zeros_like(dgain_ref)

        dn = sum(_mm(dp_ref[...], wt_ref[start:start + width, :])
                 for dp_ref, start, width in zip(dp_refs, starts, widths))
        dx, dgain = _rms_bwd(h_ref[...], gain_ref[...], dn)
        dh_in = dh_ref[...] + dx
        dx_ref[...] = dh_in
        df_ref[...] = (0.5 * dh_in).astype(BF16)
        dgain_ref[...] += dgain

    row = lambda i: (i, 0)
    fixed = lambda i: (0, 0)
    return pl.pallas_call(
        body, name=name, grid=(T // tm,),
        in_specs=[pl.BlockSpec((tm, width), row) for width in widths] + [
            pl.BlockSpec((tm, D), row),
            pl.BlockSpec((tm, D), row),
            pl.BlockSpec((1, D), fixed),
            pl.BlockSpec(w_in_t.shape, fixed),
        ],
        out_specs=[pl.BlockSpec((tm, D), row), pl.BlockSpec((tm, D), row), pl.BlockSpec((1, D), fixed)],
        out_shape=[jax.ShapeDtypeStruct((T, D), F32), jax.ShapeDtypeStruct((T, D), BF16),
                   jax.ShapeDtypeStruct((1, D), F32)],
        compiler_params=_params(("arbitrary",)),
    )(*dparts, dh, h, gain, w_in_t)


def _window_sum(x, row, doublings, *, backward):
    T = x.shape[0]
    s = x
    for k in range(doublings):
        sh = 1 << k
        if backward:
            s = s + jnp.where(row < T - sh, pltpu.roll(s, T - sh, 0), 0.0)
        else:
            s = s + jnp.where(row >= sh, pltpu.roll(s, sh, 0), 0.0)
    return s


def _pool_fwd(proj, w_group, scale, *, name):
    T = proj.shape[0]

    def body(xp_ref, w_ref, scale_ref, p_ref):
        row = lax.broadcasted_iota(jnp.int32, (T, POOL_GROUP), 0)
        for gi, window in enumerate(POOL_WINDOWS):
            cols = slice(gi * POOL_GROUP, (gi + 1) * POOL_GROUP)
            x = xp_ref[:, cols]
            inv_count = 1.0 / jnp.minimum(row + 1, window).astype(F32)
            yc = _window_sum(x, row, gi + 1, backward=False) * inv_count - x
            pre = _mm(yc.astype(BF16), w_ref[gi].astype(BF16))
            p_ref[:, cols] = pre * scale_ref[:, cols]

    return pl.pallas_call(
        body, name=name, grid=(1,),
        in_specs=[
            pl.BlockSpec((T, POOL_WIDTH), lambda i: (0, 0)),
            pl.BlockSpec(w_group.shape, lambda i: (0, 0, 0)),
            pl.BlockSpec((1, POOL_WIDTH), lambda i: (0, 0)),
        ],
        out_specs=pl.BlockSpec((T, POOL_WIDTH), lambda i: (0, 0)),
        out_shape=jax.ShapeDtypeStruct((T, POOL_WIDTH), F32),
        compiler_params=_params(("arbitrary",)),
    )(proj, w_group, scale)


def _pool_bwd(dp, proj, w_group, scale, *, name):
    T = proj.shape[0]

    def body(dp_ref, xp_ref, w_ref, scale_ref, dxp_ref, dw_ref, dscale_ref):
        row = lax.broadcasted_iota(jnp.int32, (T, POOL_GROUP), 0)
        for gi, window in enumerate(POOL_WINDOWS):
            cols = slice(gi * POOL_GROUP, (gi + 1) * POOL_GROUP)
            x = xp_ref[:, cols]
            inv_count = 1.0 / jnp.minimum(row + 1, window).astype(F32)
            yc = (_window_sum(x, row, gi + 1, backward=False) * inv_count - x).astype(BF16)
            w = w_ref[gi].astype(BF16)
            pre = _mm(yc, w)
            dpg = dp_ref[:, cols]
            dscale_ref[:, cols] = jnp.sum(dpg * pre, axis=0, keepdims=True)
            dpre = (dpg * scale_ref[:, cols]).astype(BF16)
            dw_ref[gi] = _mm_tn(yc, dpre)
            dyc = _mm_nt(dpre, w)
            dxp_ref[:, cols] = (_window_sum(dyc * inv_count, row, gi + 1, backward=True) - dyc).astype(BF16)

    return pl.pallas_call(
        body, name=name, grid=(1,),
        in_specs=[
            pl.BlockSpec((T, POOL_WIDTH), lambda i: (0, 0)),
            pl.BlockSpec((T, POOL_WIDTH), lambda i: (0, 0)),
            pl.BlockSpec(w_group.shape, lambda i: (0, 0, 0)),
            pl.BlockSpec((1, POOL_WIDTH), lambda i: (0, 0)),
        ],
        out_specs=[
            pl.BlockSpec((T, POOL_WIDTH), lambda i: (0, 0)),
            pl.BlockSpec(w_group.shape, lambda i: (0, 0, 0)),
            pl.BlockSpec((1, POOL_WIDTH), lambda i: (0, 0)),
        ],
        out_shape=[jax.ShapeDtypeStruct((T, POOL_WIDTH), BF16), jax.ShapeDtypeStruct(w_group.shape, F32),
                   jax.ShapeDtypeStruct((1, POOL_WIDTH), F32)],
        compiler_params=_params(("arbitrary",)),
    )(dp, proj, w_group, scale)


ATTN_STRIP = 32


def _log_sigmoids(z):
    lb = jnp.minimum(z, 0.0) - jnp.log(1.0 + jnp.exp(-jnp.abs(z)))
    return lb, lb - z


def _transposed_blocks(x_ref, blocks_scr, tq):
    for b in range(blocks_scr.shape[0]):
        blocks_scr[b] = x_ref[b * tq:(b + 1) * tq, :].T.astype(BF16)


def _split_bf16(x):
    hi = x.astype(BF16)
    return hi, (x - hi.astype(F32)).astype(BF16)


def _strips(n):
    return [slice(i, i + ATTN_STRIP) for i in range(0, n, ATTN_STRIP)]


def _rows(parts):
    return jnp.concatenate(parts, axis=0)


def _attn_specs(T, tq):
    q_col = POOL_WIDTH // HEAD_PAIR
    k_col = q_col + SB_WIDTH // HEAD_PAIR
    v_col = k_col + SB_WIDTH // HEAD_PAIR
    return [
        pl.BlockSpec((tq, HEAD_PAIR), lambda p, i: (i, q_col + p)),
        pl.BlockSpec((T, HEAD_PAIR), lambda p, i: (0, k_col + p)),
        pl.BlockSpec((T, HEAD_PAIR), lambda p, i: (0, v_col + p)),
    ]


def _attn_fwd(proj, *, name):
    T = proj.shape[0]
    tk = min(ATTN_K_BLOCK, T)
    tq = min(ATTN_Q_BLOCK_FWD, T)
    diagonal_blocks = tq // tk

    def body(q_ref, k_ref, v_ref, o_ref, lt_ref, kt_scr, vb_scr):
        qi = pl.program_id(1)

        @pl.when(qi == 0)
        def _():
            _transposed_blocks(k_ref, kt_scr, tk)
            vb_scr[...] = v_ref[...].astype(BF16)

        head0 = lax.broadcasted_iota(jnp.int32, (tq, HEAD_PAIR), 1) < HEAD_DIM
        q = q_ref[...] * ATTN_SCALE
        qs = (jnp.where(head0, q, 0.0).astype(BF16), jnp.where(head0, 0.0, q).astype(BF16))
        r = lax.broadcasted_iota(jnp.int32, (tq, tk), 0)
        c = lax.broadcasted_iota(jnp.int32, (tq, tk), 1)
        later = (r[:tk] > c[:tk]).astype(BF16)
        later2 = _rows([later, later])
        causal = lambda d: (lambda rows: c[rows] + d * tk < r[rows])
        strips = _strips(tq)

        def log_terms(z, valid):
            lbs, his, los, sums = [], [], [], []
            for rows in strips:
                lb, lm = _log_sigmoids(z[rows])
                if valid is not None:
                    lm = jnp.where(valid(rows), lm, 0.0)
                hi, lo = _split_bf16(lm)
                lbs.append(lb)
                his.append(hi)
                los.append(lo)
                sums.append(jnp.sum(lm, axis=1, keepdims=True))
            return lbs, jnp.concatenate([_rows(his), _rows(los)], axis=1), _rows(sums)

        def weights(lbs, run, after, valid):
            parts = []
            for rows, lb in zip(strips, lbs):
                a = jnp.exp(lb + run[rows] + after[rows])
                if valid is not None:
                    a = jnp.where(valid(rows), a, 0.0)
                parts.append(a.astype(BF16))
            return _rows(parts)

        def block(kj, carry, valid):
            kt = kt_scr[kj]
            vb = vb_scr[pl.ds(pl.multiple_of(kj * tk, tk), tk), :]
            run0, o0, run1, o1 = carry
            z0 = _mm(qs[0], kt)
            z1 = _mm(qs[1], kt)
            lbs0, split0, sums0 = log_terms(z0, valid)
            after0 = _mm(split0, later2)
            lbs1, split1, sums1 = log_terms(z1, valid)
            after1 = _mm(split1, later2)
            o0 = o0 + _mm(weights(lbs0, run0, after0, valid), vb)
            o1 = o1 + _mm(weights(lbs1, run1, after1, valid), vb)
            return run0 + sums0, o0, run1 + sums1, o1

        zero = (jnp.zeros((tq, 1), F32), jnp.zeros((tq, HEAD_PAIR), F32))
        first = diagonal_blocks * qi
        carry = zero + zero
        for d in reversed(range(diagonal_blocks)):
            carry = block(first + d, carry, causal(d))
        carry = lax.fori_loop(0, first, lambda it, cr: block(first - 1 - it, cr, None), carry)
        o_ref[...] = jnp.where(head0, carry[1], carry[3])
        lt_ref[...] = jnp.where(head0, carry[0], carry[2])

    out_spec = pl.BlockSpec((tq, HEAD_PAIR), lambda p, i: (i, p))
    return pl.pallas_call(
        body, name=name, grid=(N_HEADS // 2, T // tq),
        in_specs=_attn_specs(T, tq), out_specs=[out_spec, out_spec],
        out_shape=[jax.ShapeDtypeStruct((T, SB_WIDTH), F32), jax.ShapeDtypeStruct((T, SB_WIDTH), F32)],
        scratch_shapes=[pltpu.VMEM((T // tk, HEAD_PAIR, tk), BF16), pltpu.VMEM((T, HEAD_PAIR), BF16)],
        compiler_params=_params(("arbitrary", "arbitrary")),
    )(proj, proj, proj)


def _attn_bwd(proj, do, ltot, after, *, name):
    T = proj.shape[0]
    tk = min(ATTN_K_BLOCK, T)
    tq = min(ATTN_Q_BLOCK_BWD, T)
    diagonal_blocks = tq // tk

    def body(q_ref, k_ref, v_ref, do_ref, lt_ref, after_ref, dq_ref, dk_ref, dv_ref,
             kb_scr, kt_scr, vt_scr, dkt_ref, dvt_ref):
        qi = pl.program_id(1)

        @pl.when(qi == 0)
        def _():
            kb_scr[...] = k_ref[...].astype(BF16)
            _transposed_blocks(k_ref, kt_scr, tk)
            _transposed_blocks(v_ref, vt_scr, tk)
            dkt_ref[...] = jnp.zeros_like(dkt_ref)
            dvt_ref[...] = jnp.zeros_like(dvt_ref)

        head0 = lax.broadcasted_iota(jnp.int32, (tq, HEAD_PAIR), 1) < HEAD_DIM
        q, do_, lt = q_ref[...] * ATTN_SCALE, do_ref[...], lt_ref[...]
        qs = (jnp.where(head0, q, 0.0).astype(BF16), jnp.where(head0, 0.0, q).astype(BF16))
        q_heads = (jnp.where(head0, q, 0.0), jnp.where(head0, 0.0, q))
        do_heads = (jnp.where(head0, do_, 0.0), jnp.where(head0, 0.0, do_))
        dos = tuple(d.astype(BF16) for d in do_heads)
        qts = tuple(x.T.astype(BF16) for x in q_heads)
        dots = tuple(d.T.astype(BF16) for d in do_heads)
        lts = (jnp.max(jnp.where(head0, lt, -jnp.inf), axis=1, keepdims=True),
               jnp.max(jnp.where(head0, -jnp.inf, lt), axis=1, keepdims=True))
        r = lax.broadcasted_iota(jnp.int32, (tq, tk), 0)
        c = lax.broadcasted_iota(jnp.int32, (tq, tk), 1)
        upto = (r[:tk] <= c[:tk]).astype(BF16)
        before = (r[:tk] < c[:tk]).astype(BF16)
        upto2, before2 = _rows([upto, upto]), _rows([before, before])
        causal = lambda d: (lambda rows: c[rows] + d * tk < r[rows])
        strips = _strips(tq)

        def log_terms(z, valid):
            lbs, his, los, sums = [], [], [], []
            for rows in strips:
                lb, lm = _log_sigmoids(z[rows])
                if valid is not None:
                    lm = jnp.where(valid(rows), lm, 0.0)
                hi, lo = _split_bf16(lm)
                lbs.append(lb)
                his.append(hi)
                los.append(lo)
                sums.append(jnp.sum(lm, axis=1, keepdims=True))
            return lbs, jnp.concatenate([_rows(his), _rows(los)], axis=1), _rows(sums)

        def weights(lbs, rest, lm_upto, da, valid):
            a_parts, es, his, los, sums = [], [], [], [], []
            for rows, lb in zip(strips, lbs):
                a = jnp.exp(lb + (rest[rows] - lm_upto[rows]))
                if valid is not None:
                    a = jnp.where(valid(rows), a, 0.0)
                e = da[rows] * a
                hi, lo = _split_bf16(e)
                a_parts.append(a.astype(BF16))
                es.append(e)
                his.append(hi)
                los.append(lo)
                sums.append(jnp.sum(e, axis=1, keepdims=True))
            return _rows(a_parts), es, jnp.concatenate([_rows(his), _rows(los)], axis=1), _rows(sums)

        def score_grads(lbs, es, run_e, e_before, valid):
            parts = []
            for rows, lb, e in zip(strips, lbs, es):
                beta = jnp.exp(lb)
                dz = e * (1.0 - beta) - (run_e[rows] + e_before[rows]) * beta
                if valid is not None:
                    dz = jnp.where(valid(rows), dz, 0.0)
                parts.append(dz.astype(BF16))
            return _rows(parts)

        def block(kj, carry, valid):
            off = pl.multiple_of(kj * tk, tk)
            kb, kt, vt = kb_scr[pl.ds(off, tk), :], kt_scr[kj], vt_scr[kj]
            run_lm0, run_e0, dq0, run_lm1, run_e1, dq1 = carry
            z0, da0 = _mm(qs[0], kt), _mm(dos[0], vt)
            z1, da1 = _mm(qs[1], kt), _mm(dos[1], vt)
            lbs0, split0, lm_sums0 = log_terms(z0, valid)
            lm_upto0 = _mm(split0, upto2)
            lbs1, split1, lm_sums1 = log_terms(z1, valid)
            lm_upto1 = _mm(split1, upto2)
            a0, es0, split0, e_sums0 = weights(lbs0, lts[0] - run_lm0, lm_upto0, da0, valid)
            e_before0 = _mm(split0, before2)
            a1, es1, split1, e_sums1 = weights(lbs1, lts[1] - run_lm1, lm_upto1, da1, valid)
            e_before1 = _mm(split1, before2)
            dz0 = score_grads(lbs0, es0, run_e0, e_before0, valid)
            dkt_blk = _mm(qts[0], dz0)
            dvt_blk = _mm(dots[0], a0)
            dq0 = dq0 + _mm(dz0, kb)
            dz1 = score_grads(lbs1, es1, run_e1, e_before1, valid)
            dkt_ref[kj] += dkt_blk + _mm(qts[1], dz1)
            dvt_ref[kj] += dvt_blk + _mm(dots[1], a1)
            dq1 = dq1 + _mm(dz1, kb)
            return run_lm0 + lm_sums0, run_e0 + e_sums0, dq0, run_lm1 + lm_sums1, run_e1 + e_sums1, dq1

        zero = (jnp.zeros((tq, 1), F32), jnp.zeros((tq, 1), F32), jnp.zeros((tq, HEAD_PAIR), F32))
        first = diagonal_blocks * qi
        carry = lax.fori_loop(0, first, lambda kj, cr: block(kj, cr, None), zero + zero)
        for d in range(diagonal_blocks):
            carry = block(first + d, carry, causal(d))
        dq_ref[...] = (jnp.where(head0, carry[2], carry[5]) * ATTN_SCALE).astype(BF16)

        @pl.when(qi == T // tq - 1)
        def _():
            for b in range(T // tk):
                dk_ref[b * tk:(b + 1) * tk, :] = dkt_ref[b].T.astype(BF16)
                dv_ref[b * tk:(b + 1) * tk, :] = dvt_ref[b].T.astype(BF16)

    blk = pl.BlockSpec((tq, HEAD_PAIR), lambda p, i: (i, p))
    seq = pl.BlockSpec((T, HEAD_PAIR), lambda p, i: (0, p))
    transposed = pltpu.VMEM((T // tk, HEAD_PAIR, tk), F32)
    return pl.pallas_call(
        body, name=name, grid=(N_HEADS // 2, T // tq),
        in_specs=_attn_specs(T, tq) + [blk, blk, AFTER], out_specs=[blk, seq, seq],
        out_shape=[jax.ShapeDtypeStruct((T, SB_WIDTH), BF16)] * 3,
        scratch_shapes=[pltpu.VMEM((T, HEAD_PAIR), BF16), pltpu.VMEM((T // tk, HEAD_PAIR, tk), BF16),
                        pltpu.VMEM((T // tk, HEAD_PAIR, tk), BF16), transposed, transposed],
        compiler_params=_params(("arbitrary", "arbitrary")),
    )(proj, proj, proj, do, ltot, _in_hbm(after))


def _mix_specs(T, D, tm, wbp, w_out):
    gate_col = (POOL_WIDTH + 3 * SB_WIDTH) // D
    row = lambda i: (i, 0)
    return [
        pl.BlockSpec((tm, D), row),
        pl.BlockSpec((tm, POOL_WIDTH), row),
        pl.BlockSpec((tm, SB_WIDTH), row),
        pl.BlockSpec((tm, D), lambda i: (i, gate_col)),
        pl.BlockSpec((tm, D), lambda i: (i, gate_col + 1)),
        pl.BlockSpec(wbp.shape, lambda i: (0, 0)),
        pl.BlockSpec(wbp.shape, lambda i: (0, 0)),
        pl.BlockSpec(w_out.shape, lambda i: (0, 0)),
    ]


def _mix_fwd(h, p, o, proj, wbp, wba, w_out, *, tm, name):
    T, D = h.shape
    tm = min(tm, T)

    def body(h_ref, p_ref, o_ref, glp_ref, gls_ref, wbp_ref, wba_ref, wout_ref, hout_ref, m_ref):
        halves = (pl.ds(0, tm // 2), pl.ds(tm // 2, tm // 2))
        wbp, wba, wout = wbp_ref[...], wba_ref[...], wout_ref[...]
        branches = [(_mm_nt(p_ref[rows, :].astype(BF16), wbp), _mm_nt(o_ref[rows, :].astype(BF16), wba))
                    for rows in halves]
        for rows, (yp, ys) in zip(halves, branches):
            m = (jax.nn.sigmoid(glp_ref[rows, :]) * yp + jax.nn.sigmoid(gls_ref[rows, :]) * ys).astype(BF16)
            m_ref[rows, :] = m
            hout_ref[rows, :] = h_ref[rows, :] + _mm(m, wout)

    row = lambda i: (i, 0)
    return pl.pallas_call(
        body, name=name, grid=(T // tm,),
        in_specs=_mix_specs(T, D, tm, wbp, w_out),
        out_specs=[pl.BlockSpec((tm, D), row), pl.BlockSpec((tm, D), row)],
        out_shape=[jax.ShapeDtypeStruct((T, D), F32), jax.ShapeDtypeStruct((T, D), BF16)],
        compiler_params=_params(("arbitrary",)),
    )(h, p, o, proj, proj, wbp, wba, w_out)


def _mix_bwd(dh, p, o, proj, wbp, wba, w_out, after, *, tm, name):
    T, D = dh.shape
    tm = min(tm, T)

    def body(dh_ref, p_ref, o_ref, glp_ref, gls_ref, wbp_ref, wba_ref, wout_ref, after_ref,
             dyp_ref, dys_ref, dp_ref, do_ref, dgl_ref):
        halves = (pl.ds(0, tm // 2), pl.ds(tm // 2, tm // 2))
        wbp, wba, wout = wbp_ref[...], wba_ref[...], wout_ref[...]
        products = [(_mm_nt(dh_ref[rows, :].astype(BF16), wout), _mm_nt(p_ref[rows, :].astype(BF16), wbp),
                     _mm_nt(o_ref[rows, :].astype(BF16), wba)) for rows in halves]
        for rows, (dm, yp, ys) in zip(halves, products):
            gp = jax.nn.sigmoid(glp_ref[rows, :])
            gs = jax.nn.sigmoid(gls_ref[rows, :])
            dyp = (dm * gp).astype(BF16)
            dys = (dm * gs).astype(BF16)
            dyp_ref[rows, :] = dyp
            dys_ref[rows, :] = dys
            dgl_ref[rows, :D] = (dm * yp * gp * (1.0 - gp)).astype(BF16)
            dgl_ref[rows, D:] = (dm * ys * gs * (1.0 - gs)).astype(BF16)
            dp_ref[rows, :] = _mm(dyp, wbp)
            do_ref[rows, :] = _mm(dys, wba)

    row = lambda i: (i, 0)
    return pl.pallas_call(
        body, name=name, grid=(T // tm,),
        in_specs=_mix_specs(T, D, tm, wbp, w_out) + [AFTER],
        out_specs=[pl.BlockSpec((tm, D), row), pl.BlockSpec((tm, D), row), pl.BlockSpec((tm, POOL_WIDTH), row),
                   pl.BlockSpec((tm, SB_WIDTH), row), pl.BlockSpec((tm, 2 * D), row)],
        out_shape=[jax.ShapeDtypeStruct((T, D), BF16), jax.ShapeDtypeStruct((T, D), BF16),
                   jax.ShapeDtypeStruct((T, POOL_WIDTH), F32), jax.ShapeDtypeStruct((T, SB_WIDTH), F32),
                   jax.ShapeDtypeStruct((T, 2 * D), BF16)],
        compiler_params=_params(("arbitrary",)),
    )(dh, p, o, proj, proj, wbp, wba, w_out, _in_hbm(after))


def _adamw_update(w, g, m, v):
    m_ = ADAM_B1 * m + (1.0 - ADAM_B1) * g
    v_ = ADAM_B2 * v + (1.0 - ADAM_B2) * (g * g)
    m_hat = m_ / (1.0 - ADAM_B1 ** ADAM_STEP)
    v_hat = v_ / (1.0 - ADAM_B2 ** ADAM_STEP)
    return -ADAM_LR * (m_hat / (jnp.sqrt(v_hat) + ADAM_EPS) + ADAM_WD * w), m_, v_


def _adamw(w, g, m, v, *, name):
    R, C = w.shape
    tr = _row_tile(R, C)

    def body(w_ref, g_ref, m_ref, v_ref, d_ref, nm_ref, nv_ref):
        d_ref[...], nm_ref[...], nv_ref[...] = _adamw_update(w_ref[...], g_ref[...], m_ref[...], v_ref[...])

    spec = pl.BlockSpec((tr, C), lambda i: (i, 0))
    return pl.pallas_call(
        body, name=name, grid=(R // tr,), in_specs=[spec] * 4, out_specs=[spec] * 3,
        out_shape=[jax.ShapeDtypeStruct((R, C), F32)] * 3,
        compiler_params=_params(("arbitrary",)),
    )(w, g, m, v)


def _position():
    return lax.axis_index("x"), lax.axis_index("y"), lax.axis_index("c")


def _all_gather(shards, *, name, collective_id, padded_rows=None):
    n = len(shards)
    n_copies = 9
    padded = padded_rows is not None
    zeros = [jnp.zeros((N_DEV, s.shape[0], padded_rows - s.shape[1], s.shape[2]), s.dtype) for s in shards] if padded else []

    def body(*refs):
        ins, zero_refs, outs = refs[:n], refs[n:n + len(zeros)], refs[n + len(zeros):2 * n + len(zeros)]
        send_sems, recv_sems, local_sems = refs[2 * n + len(zeros):]
        x, y, c = _position()
        me, sibling = (x, y, c), (x, y, 1 - c)
        x_nbr, y_nbr, diagonal = (1 - x, y, c), (x, 1 - y, c), (1 - x, 1 - y, c)
        other = lambda pos: (pos[0], pos[1], 1 - c)

        barrier = pltpu.get_barrier_semaphore()
        for peer in (sibling, x_nbr, y_nbr):
            pl.semaphore_signal(barrier, inc=1, device_id=peer, device_id_type=MESH)
        pl.semaphore_wait(barrier, 3)

        def block(a, pos, half=None):
            ref = outs[a].at[4 * pos[0] + 2 * pos[1] + pos[2]]
            if padded:
                groups, real = ins[a].shape[:2]
                if half is None:
                    return ref.at[:, pl.ds(0, real), :]
                if groups == 2:
                    return ref.at[pl.ds(half, 1), pl.ds(0, real), :]
                return ref.at[:, pl.ds(half * (real // 2), real // 2), :]
            rows = ref.shape[0] // 2
            return ref if half is None else ref.at[pl.ds(half * rows, rows)]

        def copy(a, k, pos, to, half=None, src=None):
            return pltpu.make_async_remote_copy(
                src_ref=block(a, pos, half) if src is None else src, dst_ref=block(a, pos, half),
                send_sem=send_sems.at[n_copies * a + k], recv_sem=recv_sems.at[n_copies * a + k],
                device_id=to, device_id_type=MESH)

        started = []
        for a in range(n):
            mine = pltpu.make_async_copy(ins[a], block(a, me), local_sems.at[a])
            mine.start()
            started.append(mine)
        for a, zero_ref in enumerate(zero_refs):
            real = ins[a].shape[1]
            padding = pltpu.make_async_copy(zero_ref, outs[a].at[:, :, pl.ds(real, padded_rows - real), :],
                                            local_sems.at[n + a])
            padding.start()
            started.append(padding)
        sends = []
        for a in range(n):
            sends += [copy(a, 1, me, x_nbr, src=ins[a]), copy(a, 2, me, y_nbr, src=ins[a]),
                      copy(a, 0, me, sibling, src=ins[a])]
        for cp in sends:
            cp.start()

        def pass_on(copies):
            for cp in copies:
                cp.start()
                sends.append(cp)

        for a in range(n):
            copy(a, 1, x_nbr, me).wait_recv()
            pass_on([copy(a, 5, x_nbr, y_nbr, half=0), copy(a, 3, x_nbr, sibling)])
            copy(a, 2, y_nbr, me).wait_recv()
            pass_on([copy(a, 6, y_nbr, x_nbr, half=1), copy(a, 4, y_nbr, sibling)])
        for a in range(n):
            copy(a, 5, diagonal, me, half=0).wait_recv()
            pass_on([copy(a, 7, diagonal, sibling, half=0)])
            copy(a, 6, diagonal, me, half=1).wait_recv()
            pass_on([copy(a, 8, diagonal, sibling, half=1)])
        for a in range(n):
            copy(a, 0, sibling, me).wait_recv()
            copy(a, 3, other(x_nbr), me).wait_recv()
            copy(a, 4, other(y_nbr), me).wait_recv()
            copy(a, 7, other(diagonal), me, half=0).wait_recv()
            copy(a, 8, other(diagonal), me, half=1).wait_recv()
        for cp in sends:
            cp.wait_send()
        for cp in started:
            cp.wait()

    return pl.kernel(
        body, name=name,
        out_type=[jax.ShapeDtypeStruct((N_DEV, s.shape[0], padded_rows, s.shape[2]) if padded else (N_DEV,) + s.shape,
                                       s.dtype) for s in shards],
        mesh=plsc.ScalarSubcoreMesh(axis_name="sequencer", num_cores=1),
        scratch_types=[pltpu.SemaphoreType.DMA((n_copies * n,)), pltpu.SemaphoreType.DMA((n_copies * n,)),
                       pltpu.SemaphoreType.DMA((n + len(zeros),))],
        compiler_params=pltpu.CompilerParams(collective_id=collective_id),
    )(*shards, *zeros)


def _chip_sums(group, *, name):
    n = len(group)
    shapes = [g.shape[1:] for g in group]

    def body(*refs):
        g_refs, partials, out_refs = refs[:n], refs[n:3 * n:2], refs[n + 1:3 * n:2]
        mines, theirs = refs[3 * n:5 * n:2], refs[3 * n + 1:5 * n:2]
        send_sems, recv_sems, local_sems = refs[5 * n:]
        x, y, c = _position()
        my_chip = 2 * x + y

        def swap(a, s):
            return pltpu.make_async_remote_copy(
                src_ref=g_refs[a].at[2 * s + (1 - c)], dst_ref=theirs[a].at[s],
                send_sem=send_sems.at[4 * a + s], recv_sem=recv_sems.at[4 * a + s],
                device_id=(x, y, 1 - c), device_id_type=MESH)

        def load(a, s):
            return pltpu.make_async_copy(g_refs[a].at[2 * s + c], mines[a].at[s], local_sems.at[4 * a + s])

        for a in range(n):
            for s in range(4):
                swap(a, s).start()
                load(a, s).start()

        for a, (R, C) in enumerate(shapes):
            rc = 128 if R % 128 == 0 else R

            def chip_sum(chip, rows):
                return mines[a][chip, rows, :].astype(F32) + theirs[a][chip, rows, :].astype(F32)

            for s in range(4):
                load(a, s).wait()
                swap(a, s).wait_recv()

                @pl.when(s == my_chip)
                def _():
                    @pl.loop(0, R // rc)
                    def _(t):
                        rows = pl.ds(pl.multiple_of(t * rc, rc), rc)
                        out_refs[a][rows, :] = chip_sum(s, rows)

                @pl.when(s != my_chip)
                def _():
                    @pl.loop(0, R // rc)
                    def _(t):
                        rows = pl.ds(pl.multiple_of(t * rc, rc), rc)
                        partials[a][(s ^ my_chip) - 1, rows, :] = chip_sum(s, rows).astype(BF16)

        for a in range(n):
            for s in range(4):
                swap(a, s).wait_send()

    vmem = pl.BlockSpec(memory_space=pltpu.VMEM)
    outs = pl.pallas_call(
        body, name=name,
        in_specs=[pl.BlockSpec(memory_space=pl.ANY)] * n, out_specs=[vmem] * (2 * n),
        out_shape=[shape for R, C in shapes
                   for shape in (jax.ShapeDtypeStruct((3, R, C), BF16), jax.ShapeDtypeStruct((R, C), F32))],
        scratch_shapes=[pltpu.VMEM((4, R, C), BF16) for R, C in shapes for _ in range(2)] + [
            pltpu.SemaphoreType.DMA((4 * n,)), pltpu.SemaphoreType.DMA((4 * n,)), pltpu.SemaphoreType.DMA((4 * n,))],
        compiler_params=_params(),
    )(*group)
    return [(outs[2 * a], outs[2 * a + 1]) for a in range(n)]


def _cross_chips(partials, *, name, collective_id):
    n = len(partials)

    def body(*refs):
        ins, outs = refs[:n], refs[n:2 * n]
        send_sems, recv_sems = refs[2 * n:]
        x, y, c = _position()
        my_chip = 2 * x + y
        peers = [((my_chip ^ j) // 2, (my_chip ^ j) % 2, c) for j in (1, 2, 3)]

        barrier = pltpu.get_barrier_semaphore()
        for peer in peers:
            pl.semaphore_signal(barrier, inc=1, device_id=peer, device_id_type=MESH)
        pl.semaphore_wait(barrier, 3)

        copies = [
            pltpu.make_async_remote_copy(
                src_ref=ins[a].at[j], dst_ref=outs[a].at[j],
                send_sem=send_sems.at[3 * a + j], recv_sem=recv_sems.at[3 * a + j],
                device_id=peers[j], device_id_type=MESH)
            for a in range(n) for j in range(3)]
        for cp in copies:
            cp.start()
        for cp in copies:
            cp.wait_recv()
        for cp in copies:
            cp.wait_send()

    return pl.kernel(
        body, name=name,
        out_type=[jax.ShapeDtypeStruct(p.shape, p.dtype) for p in partials],
        mesh=plsc.ScalarSubcoreMesh(axis_name="sequencer", num_cores=1),
        scratch_types=[pltpu.SemaphoreType.DMA((3 * n,)), pltpu.SemaphoreType.DMA((3 * n,))],
        compiler_params=pltpu.CompilerParams(collective_id=collective_id),
    )(*partials)


def _cross_chips_and_gather(partials, slab, *, name, collective_id):
    n = len(partials)

    def body(*refs):
        part_refs, slab_ref = refs[:n], refs[n]
        landed_refs, slabs_ref = refs[n + 1:2 * n + 1], refs[2 * n + 1]
        send_sems, recv_sems, local_sem = refs[2 * n + 2:]
        x, y, c = _position()
        me, my_chip = 4 * x + 2 * y + c, 2 * x + y
        others = [me ^ k for k in range(1, N_DEV)]
        ids = [(o // 4, (o // 2) % 2, o % 2) for o in others]

        barrier = pltpu.get_barrier_semaphore()
        for peer in ids:
            pl.semaphore_signal(barrier, inc=1, device_id=peer, device_id_type=MESH)
        pl.semaphore_wait(barrier, N_DEV - 1)

        mine = pltpu.make_async_copy(slab_ref, slabs_ref.at[me], local_sem)
        mine.start()
        sends = [
            pltpu.make_async_remote_copy(
                src_ref=part_refs[a].at[j], dst_ref=landed_refs[a].at[j],
                send_sem=send_sems.at[3 * a + j], recv_sem=recv_sems.at[3 * a + j],
                device_id=((my_chip ^ (j + 1)) // 2, (my_chip ^ (j + 1)) % 2, c), device_id_type=MESH)
            for a in range(n) for j in range(3)]
        sends += [
            pltpu.make_async_remote_copy(
                src_ref=slab_ref, dst_ref=slabs_ref.at[me],
                send_sem=send_sems.at[3 * n + k], recv_sem=recv_sems.at[3 * n + k],
                device_id=ids[k], device_id_type=MESH)
            for k in range(N_DEV - 1)]
        arrivals = sends[:3 * n] + [
            pltpu.make_async_remote_copy(
                src_ref=slab_ref, dst_ref=slabs_ref.at[others[k]],
                send_sem=send_sems.at[3 * n + k], recv_sem=recv_sems.at[3 * n + k],
                device_id=ids[k], device_id_type=MESH)
            for k in range(N_DEV - 1)]
        for cp in sends:
            cp.start()
        for cp in arrivals:
            cp.wait_recv()
        for cp in sends:
            cp.wait_send()
        mine.wait()

    n_sems = 3 * n + N_DEV - 1
    outs = pl.kernel(
        body, name=name,
        out_type=[jax.ShapeDtypeStruct(p.shape, p.dtype) for p in partials]
                 + [jax.ShapeDtypeStruct((N_DEV,) + slab.shape, slab.dtype)],
        mesh=plsc.ScalarSubcoreMesh(axis_name="sequencer", num_cores=1),
        scratch_types=[pltpu.SemaphoreType.DMA((n_sems,)), pltpu.SemaphoreType.DMA((n_sems,)), pltpu.SemaphoreType.DMA],
        compiler_params=pltpu.CompilerParams(collective_id=collective_id),
    )(*partials, slab)
    return outs[:n], outs[n]


def _sum_devices(gathered, after, *, name):
    _, R, C = gathered.shape

    def body(in_ref, after_ref, out_ref):
        total = in_ref[0]
        for d in range(1, N_DEV):
            total = total + in_ref[d]
        out_ref[...] = total

    return pl.pallas_call(
        body, name=name, grid=(1,),
        in_specs=[pl.BlockSpec((N_DEV, R, C), lambda i: (0, 0, 0)), AFTER],
        out_specs=pl.BlockSpec((R, C), lambda i: (0, 0)),
        out_shape=jax.ShapeDtypeStruct((R, C), F32),
        compiler_params=_params(("arbitrary",)),
    )(gathered, _in_hbm(after))


def _owner_sum(own, landed, after, *, name):
    R, C = own.shape
    tr = _row_tile(R, C)

    def body(own_ref, landed_ref, after_ref, out_ref):
        total = own_ref[...]
        for j in range(3):
            total = total + landed_ref[j].astype(F32)
        out_ref[...] = total

    return pl.pallas_call(
        body, name=name, grid=(R // tr,),
        in_specs=[pl.BlockSpec((tr, C), lambda i: (i, 0)), pl.BlockSpec((3, tr, C), lambda i: (0, i, 0)), AFTER],
        out_specs=pl.BlockSpec((tr, C), lambda i: (i, 0)),
        out_shape=jax.ShapeDtypeStruct((R, C), F32),
        compiler_params=_params(("arbitrary",)),
    )(own, landed, _in_hbm(after))


def _owner_sum_adamw(own, landed, w, m, v, after, *, name):
    H, R, C = w.shape
    tr = R // 2

    def body(own_ref, landed_ref, w_ref, m_ref, v_ref, after_ref, g_ref, d_ref, nm_ref, nv_ref):
        total = own_ref[...]
        for j in range(3):
            total = total + landed_ref[j].astype(F32)
        g_ref[...] = total
        d_ref[...], nm_ref[...], nv_ref[...] = _adamw_update(w_ref[...], total, m_ref[...], v_ref[...])

    spec = pl.BlockSpec((None, tr, C), lambda h, i: (h, i, 0))
    return pl.pallas_call(
        body, name=name, grid=(H, R // tr),
        in_specs=[spec, pl.BlockSpec((3, None, tr, C), lambda h, i: (0, h, i, 0)), spec, spec, spec, AFTER],
        out_specs=[spec] * 4,
        out_shape=[jax.ShapeDtypeStruct((H, R, C), F32)] * 4,
        compiler_params=_params(("arbitrary", "arbitrary")),
    )(own, landed, w, m, v, _in_hbm(after))


def _local_step(x, target, norms, pool_w_group, pool_scale, wgu1, wd1, w_in, wbp, wba, w_out, wgu2, wd2, exchange):
    n1g, nmg, n2g, nfg = norms
    D = x.shape[1]
    gu1, hid1 = _ffn_up(x, n1g, wgu1, tm=1024, name="ffn1_up")
    h1 = _ffn_down(x, hid1, wd1, tm=512, name="ffn1_down")
    un, proj = _inproj_fwd(h1, nmg, w_in, tm=1024, name="inproj_fwd")
    p = _pool_fwd(proj, pool_w_group, pool_scale, name="pool_fwd")
    o, ltot = _attn_fwd(proj, name="attn_fwd")
    h2, m = _mix_fwd(h1, p, o, proj, wbp, wba, w_out, tm=512, name="mix_fwd")
    gu2, hid2 = _ffn_up(h2, n2g, wgu2, tm=1024, name="ffn2_up")
    h3 = _ffn_down(h2, hid2, wd2, tm=512, name="ffn2_down")
    dh3, df2, loss, d_nf = _loss_bwd(h3, target, nfg, tm=256, name="loss_bwd")

    dh2, d_n2, n2, dgu2 = _ffn_bwd(dh3, df2, h2, n2g, gu2, wgu2, wd2, df2, tm=512, name="ffn2_bwd")
    d_wd2 = _wgrad_down(hid2, df2, tk=WGRAD_TOKENS, name="ffn2_wgrad_down")
    d_wgu2 = _wgrad_gate_up(n2, dgu2, tk=WGRAD_TOKENS, name="ffn2_wgrad_gate_up")
    (g_wd2, g_wgu2), token = exchange("ffn2", [d_wd2.reshape(N_DEV, FF_SHARD_PAD, D), d_wgu2])

    dyp, dys, dp, do, dgl = _mix_bwd(dh2, p, o, proj, wbp, wba, w_out, token, tm=512, name="mix_bwd")
    d_wout = _wgrad_full(m, dh2, tk=WGRAD_TOKENS, name="wgrad_out")
    d_wbp = _wgrad_full(dyp, p, tk=WGRAD_TOKENS, name="wgrad_branch_pool")
    d_wba = _wgrad_full(dys, o, tk=WGRAD_TOKENS, name="wgrad_branch_attn")
    by_owner = lambda g: g.reshape(N_DEV, g.shape[0] // N_DEV, g.shape[1])
    (g_wbp, g_wba, g_wout), token = exchange("mix", [by_owner(d_wbp), by_owner(d_wba), by_owner(d_wout)])
    dxp, d_wgroup, d_scale = _pool_bwd(dp, proj, pool_w_group, pool_scale, name="pool_bwd")
    dq, dk, dv = _attn_bwd(proj, do, ltot, token, name="attn_bwd")
    dproj_parts = [dxp, dq, dk, dv, dgl]
    dh1, df1, d_nm = _inproj_bwd(dproj_parts, dh2, h1, nmg, w_in, tm=512, name="inproj_bwd")
    d_win = _wgrad_in(dproj_parts, un, name="wgrad_in")
    d_wd1 = _wgrad_down(hid1, df1, tk=WGRAD_TOKENS, name="ffn1_wgrad_down")
    (g_win, g_wd1, replicated_early), token = exchange(
        "w_in_ffn1_down", [d_win, d_wd1.reshape(N_DEV, FF_SHARD_PAD, D), d_nm, d_n2, d_nf, d_scale, d_wgroup, loss])

    dx, d_n1, n1, dgu1 = _ffn_bwd(dh1, df1, x, n1g, gu1, wgu1, wd1, token, tm=512, name="ffn1_bwd")
    d_wgu1_a = _wgrad_gate_up(n1, dgu1, tk=WGRAD_TOKENS, name="ffn1_wgrad_gate_up_a", part=0, parts=2)
    (g_wgu1_a, replicated_late), token = exchange("ffn1_gate_up_a", [d_wgu1_a, d_n1])
    d_wgu1_b = _wgrad_gate_up(n1, dgu1, tk=WGRAD_TOKENS, name="ffn1_wgrad_gate_up_b", part=1, parts=2)
    (g_wgu1_b,), token = exchange("last", [d_wgu1_b])
    g_wgu1 = (g_wgu1_a, g_wgu1_b)

    sharded = (g_wgu1, g_wd1, g_win, g_wbp, g_wba, g_wout, g_wgu2, g_wd2)
    return dx, sharded, (replicated_late, replicated_early), token


def _hidden_major(w):
    return jnp.swapaxes(w[0], 0, 1)


def _pad_gate_up(wt):
    d = wt.shape[1]
    wt = wt.astype(BF16).reshape(2, FF_SHARD, d)
    return jnp.pad(wt, ((0, 0), (0, FF_SHARD_PAD - FF_SHARD), (0, 0))).reshape(2 * FF_SHARD_PAD, d)


def _unpad_gate_up(gt):
    d = gt.shape[1]
    return gt.reshape(2, FF_SHARD_PAD, d)[:, :FF_SHARD].reshape(2 * FF_SHARD, d)


def _pad_down(w):
    return jnp.pad(w.astype(BF16), ((0, FF_SHARD_PAD - FF_SHARD), (0, 0)))


def kernel(x, ffn1_norm, ffn1_w_gate_up, ffn1_w_down, mix_norm, w_in, pool_w_group, pool_scale, w_branch_pool, w_branch_attn, w_out, ffn2_norm, ffn2_w_gate_up, ffn2_w_down, final_norm, loss_target, m_ffn1_norm, m_ffn1_w_gate_up, m_ffn1_w_down, m_mix_norm, m_w_in, m_pool_w_group, m_pool_scale, m_w_branch_pool, m_w_branch_attn, m_w_out, m_ffn2_norm, m_ffn2_w_gate_up, m_ffn2_w_down, m_final_norm, v_ffn1_norm, v_ffn1_w_gate_up, v_ffn1_w_down, v_mix_norm, v_w_in, v_pool_w_group, v_pool_scale, v_w_branch_pool, v_w_branch_attn, v_w_out, v_ffn2_norm, v_ffn2_w_gate_up, v_ffn2_w_down, v_final_norm):
    D = x.shape[-1]
    weights = dict(ffn1_norm=ffn1_norm, ffn1_w_gate_up=ffn1_w_gate_up, ffn1_w_down=ffn1_w_down, mix_norm=mix_norm,
                   w_in=w_in, pool_w_group=pool_w_group, pool_scale=pool_scale, w_branch_pool=w_branch_pool,
                   w_branch_attn=w_branch_attn, w_out=w_out, ffn2_norm=ffn2_norm, ffn2_w_gate_up=ffn2_w_gate_up,
                   ffn2_w_down=ffn2_w_down, final_norm=final_norm)
    first = dict(ffn1_norm=m_ffn1_norm, ffn1_w_gate_up=m_ffn1_w_gate_up, ffn1_w_down=m_ffn1_w_down,
                 mix_norm=m_mix_norm, w_in=m_w_in, pool_w_group=m_pool_w_group, pool_scale=m_pool_scale,
                 w_branch_pool=m_w_branch_pool, w_branch_attn=m_w_branch_attn, w_out=m_w_out,
                 ffn2_norm=m_ffn2_norm, ffn2_w_gate_up=m_ffn2_w_gate_up, ffn2_w_down=m_ffn2_w_down,
                 final_norm=m_final_norm)
    second = dict(ffn1_norm=v_ffn1_norm, ffn1_w_gate_up=v_ffn1_w_gate_up, ffn1_w_down=v_ffn1_w_down,
                  mix_norm=v_mix_norm, w_in=v_w_in, pool_w_group=v_pool_w_group, pool_scale=v_pool_scale,
                  w_branch_pool=v_w_branch_pool, w_branch_attn=v_w_branch_attn, w_out=v_w_out,
                  ffn2_norm=v_ffn2_norm, ffn2_w_gate_up=v_ffn2_w_gate_up, ffn2_w_down=v_ffn2_w_down,
                  final_norm=v_final_norm)
    order = list(weights)

    wgu1, = _all_gather([_hidden_major(ffn1_w_gate_up).astype(BF16).reshape(2, FF_SHARD, D)],
                        name="all_gather_ffn1_gate_up", collective_id=0, padded_rows=FF_SHARD_PAD)
    wgu1 = wgu1.reshape(N_DEV, 2 * FF_SHARD_PAD, D)
    wd1, = _all_gather([ffn1_w_down.astype(BF16)], name="all_gather_ffn1_down", collective_id=10,
                       padded_rows=FF_SHARD_PAD)
    wd1 = wd1.reshape(N_DEV, FF_SHARD_PAD, D)
    transposed = lambda w: jnp.swapaxes(w[0], 0, 1).astype(BF16)
    win_g, = _all_gather([transposed(w_in)], name="all_gather_w_in", collective_id=1)
    wbp_g, wba_g = _all_gather([transposed(w_branch_pool), transposed(w_branch_attn)],
                               name="all_gather_branches", collective_id=2)
    wout_g, = _all_gather([w_out[0].astype(BF16)], name="all_gather_w_out", collective_id=11)
    wgu2, wd2 = _all_gather([_pad_gate_up(_hidden_major(ffn2_w_gate_up)), _pad_down(ffn2_w_down[0])],
                            name="all_gather_ffn2", collective_id=3)
    whole = lambda g: g.reshape(g.shape[0] * g.shape[1], g.shape[2])
    wd1, wd2, win_g, wbp_g, wba_g, wout_g = (whole(g) for g in (wd1, wd2, win_g, wbp_g, wba_g, wout_g))

    cross_ids = {"ffn2": 4, "mix": 5, "w_in_ffn1_down": 8, "ffn1_gate_up_a": 9, "last": 7}
    small = ["ffn1_norm", "mix_norm", "ffn2_norm", "final_norm", "pool_scale", "pool_w_group"]

    def tile_rows(a):
        a = a.reshape(-1, 128)
        return jnp.pad(a, ((0, -a.shape[0] % 8), (0, 0)))

    def exchange(tag, group):
        grads = [g for g in group if g.dtype == BF16]
        extras = [tile_rows(g) for g in group if g.dtype != BF16]
        sums = _chip_sums(grads, name="chip_sums_" + tag)
        partials = [s[0] for s in sums]
        handles = []
        if extras:
            landed, slabs = _cross_chips_and_gather(partials, jnp.concatenate(extras, axis=0),
                                                    name="cross_chips_" + tag, collective_id=cross_ids[tag])
            handles = [slabs]
        else:
            landed = _cross_chips(partials, name="cross_chips_" + tag, collective_id=cross_ids[tag])
        return [(s[1], l) for s, l in zip(sums, landed)] + handles, sums[-1][1]

    norms = (ffn1_norm, mix_norm, ffn2_norm, final_norm.reshape(1, D))
    dx, sharded, (slabs_late, slabs_early), last = _local_step(
        x[0], loss_target[0], norms, pool_w_group[0], pool_scale, wgu1, wd1, win_g, wbp_g, wba_g, wout_g, wgu2, wd2,
        exchange)
    names = ["ffn1_w_gate_up", "ffn1_w_down", "w_in", "w_branch_pool", "w_branch_attn", "w_out",
             "ffn2_w_gate_up", "ffn2_w_down"]
    handles = dict(zip(names, sharded))
    grads, delta, new_m, new_v = {}, {}, {}, {}
    loss_out = []

    def update_replicated(after):
        rows = [weights[k].size // 128 for k in small]
        padded_rows = [-(-r // 8) * 8 for r in rows]
        starts = [sum(padded_rows[:i]) for i in range(len(rows) + 1)]
        total = jnp.concatenate([_sum_devices(slabs_late, after, name="sum_replicated_late"),
                                 _sum_devices(slabs_early, after, name="sum_replicated_early")], axis=0)
        loss_out.append(total[starts[-1], 0])
        small_w = jnp.concatenate([tile_rows(weights[k]) for k in small], axis=0)
        small_m = jnp.concatenate([tile_rows(first[k]) for k in small], axis=0)
        small_v = jnp.concatenate([tile_rows(second[k]) for k in small], axis=0)
        small_out = _adamw(small_w, total[:starts[-1]], small_m, small_v, name="adamw_replicated")
        for name_, start, n_rows in zip(small, starts, rows):
            shape = weights[name_].shape
            grads[name_] = total[start:start + n_rows].reshape(shape)
            delta[name_], new_m[name_], new_v[name_] = (a[start:start + n_rows].reshape(shape) for a in small_out)
        return small_out[0]

    after = last
    for k in ("ffn2_w_down", "ffn2_w_gate_up", "w_branch_pool", "w_branch_attn", "w_out", "w_in", "ffn1_w_down",
              "ffn1_w_gate_up"):
        hidden_major = k.endswith("w_gate_up")
        view = _hidden_major if hidden_major else (lambda a: a[0])
        back = (lambda a: jnp.swapaxes(a, 0, 1)[None]) if hidden_major else (lambda a: a[None])
        if k in ("ffn2_w_down", "ffn2_w_gate_up", "w_out", "ffn1_w_down"):
            own, landed = handles[k]
            groups = 2 if hidden_major else 1
            by_group = lambda a: a.reshape(a.shape[:-2] + (groups, a.shape[-2] // groups, a.shape[-1]))
            out = _owner_sum_adamw(by_group(own), by_group(landed), by_group(view(weights[k])),
                                   by_group(view(first[k])), by_group(view(second[k])), after, name="adamw_" + k)
            after = out[1]
            grads[k], delta[k], new_m[k], new_v[k] = (back(a.reshape(-1, a.shape[-1])) for a in out)
            continue
        if isinstance(handles[k][0], tuple):
            first_half = _owner_sum(*handles[k][0], after, name="owner_sum_" + k + "_a")
            second_half = _owner_sum(*handles[k][1], update_replicated(first_half), name="owner_sum_" + k + "_b")
            g = jnp.concatenate([first_half[:FF_SHARD], second_half[:FF_SHARD]], axis=0)
        else:
            g = jnp.swapaxes(_owner_sum(*handles[k], after, name="owner_sum_" + k), 0, 1)
        out = _adamw(view(weights[k]), g, view(first[k]), view(second[k]), name="adamw_" + k)
        after = out[0]
        grads[k] = back(g)
        delta[k], new_m[k], new_v[k] = (back(a) for a in out)

    return (loss_out[0], dx[None], *[grads[k] for k in order], *[delta[k] for k in order],
            *[new_m[k] for k in order], *[new_v[k] for k in order])
```

```python
import jax
import jax.numpy as jnp
from jax import lax
from jax.experimental import pallas as pl
from jax.experimental.pallas import tpu as pltpu
from jax.experimental.pallas import tpu_sc as plsc

F32 = jnp.float32
BF16 = jnp.bfloat16
MESH = pl.DeviceIdType.MESH

RMS_EPS = 1e-6
N_DEV = 8
N_HEADS = 8
HEAD_DIM = 64
HEAD_PAIR = 2 * HEAD_DIM
POOL_WINDOWS = (2, 4, 8, 16)
POOL_GROUP = 128
POOL_WIDTH = 512
SB_WIDTH = 512
FF_SHARD = 352
FF_SHARD_PAD = 384
ATTN_K_BLOCK = 256
ATTN_Q_BLOCK_FWD = 512
ATTN_Q_BLOCK_BWD = 256
ATTN_SCALE = 0.125

ADAM_LR = 0.001
ADAM_B1 = 0.9
ADAM_B2 = 0.999
ADAM_EPS = 1e-08
ADAM_WD = 0.01
ADAM_STEP = 10

VMEM_LIMIT = 48 << 20
WGRAD_TOKENS = 2048


def _params(dims=None):
    return pltpu.CompilerParams(dimension_semantics=dims, vmem_limit_bytes=VMEM_LIMIT)


def _mm(a, b):
    return jnp.dot(a, b, preferred_element_type=F32)


def _mm_nt(a, b):
    return lax.dot_general(a, b, (((1,), (1,)), ((), ())), preferred_element_type=F32)


def _mm_tn(a, b):
    return lax.dot_general(a, b, (((0,), (0,)), ((), ())), preferred_element_type=F32)


def _row_tile(rows, cols):
    limit = max(8, (512 * 1024) // cols)
    return max(t for t in range(8, rows + 1, 8) if rows % t == 0 and (t <= limit or t == 8))


def _rstd(xf):
    return lax.rsqrt(jnp.mean(xf * xf, axis=-1, keepdims=True) + RMS_EPS)


def _rms_bwd(xf, gain, dn):
    r = _rstd(xf)
    xh = xf * r
    dgain = jnp.sum(dn * xh, axis=0, keepdims=True)
    dxh = dn * gain
    dx = r * (dxh - xh * jnp.mean(dxh * xh, axis=-1, keepdims=True))
    return dx, dgain


def _ffn_up(x, gain, wgu, *, tm, name):
    T, D = x.shape
    tm = min(tm, T)
    nb, bw = wgu.shape[0] // 2, wgu.shape[1]

    def body(x_ref, gain_ref, wg_ref, wu_ref, gu_ref, hid_ref, n_scr):
        @pl.when(pl.program_id(1) == 0)
        def _():
            xf = x_ref[...]
            n_scr[...] = (xf * _rstd(xf) * gain_ref[...]).astype(BF16)

        halves = (pl.ds(0, tm // 2), pl.ds(tm // 2, tm // 2))
        wg, wu = wg_ref[...], wu_ref[...]
        gus = [(_mm_nt(n_scr[rows, :], wg), _mm_nt(n_scr[rows, :], wu)) for rows in halves]
        for rows, (g, u) in zip(halves, gus):
            gu_ref[0, rows, :] = g.astype(BF16)
            gu_ref[1, rows, :] = u.astype(BF16)
            hid_ref[rows, :] = (g * jax.nn.sigmoid(g) * u).astype(BF16)

    return pl.pallas_call(
        body, name=name, grid=(T // tm, nb),
        in_specs=[
            pl.BlockSpec((tm, D), lambda i, j: (i, 0)),
            pl.BlockSpec((1, D), lambda i, j: (0, 0)),
            pl.BlockSpec((None, bw, D), lambda i, j: (j, 0, 0)),
            pl.BlockSpec((None, bw, D), lambda i, j: (j + nb, 0, 0)),
        ],
        out_specs=[
            pl.BlockSpec((2, tm, bw), lambda i, j: (0, i, j)),
            pl.BlockSpec((tm, bw), lambda i, j: (i, j)),
        ],
        out_shape=[jax.ShapeDtypeStruct((2, T, nb * bw), BF16), jax.ShapeDtypeStruct((T, nb * bw), BF16)],
        scratch_shapes=[pltpu.VMEM((tm, D), BF16)],
        compiler_params=_params(("arbitrary", "arbitrary")),
    )(x, gain, wgu, wgu)


def _ffn_down(x, hid, wd, *, tm, name):
    T, D = x.shape
    tm = min(tm, T)
    F = hid.shape[1]

    def body(x_ref, hid_ref, wd_ref, h_ref):
        h_ref[...] = x_ref[...] + 0.5 * _mm(hid_ref[...], wd_ref[...])

    return pl.pallas_call(
        body, name=name, grid=(T // tm,),
        in_specs=[
            pl.BlockSpec((tm, D), lambda i: (i, 0)),
            pl.BlockSpec((tm, F), lambda i: (i, 0)),
            pl.BlockSpec((F, D), lambda i: (0, 0)),
        ],
        out_specs=pl.BlockSpec((tm, D), lambda i: (i, 0)),
        out_shape=jax.ShapeDtypeStruct((T, D), F32),
        compiler_params=_params(("arbitrary",)),
    )(x, hid, wd)


AFTER = pl.BlockSpec(memory_space=pltpu.HBM)


def _in_hbm(token):
    return pltpu.with_memory_space_constraint(token, pltpu.HBM)


def _ffn_bwd(dh, df, x, gain, gu, wgu, wd, after, *, tm, name):
    T, D = x.shape
    tm = min(tm, T)
    nb, bw = wgu.shape[0] // 2, wgu.shape[1]

    def body(dh_ref, df_ref, x_ref, gain_ref, gu_ref, wg_ref, wu_ref, wd_ref, after_ref,
             dx_ref, dgain_ref, n_ref, dgu_ref, dn_acc):
        i, j = pl.program_id(0), pl.program_id(1)

        @pl.when(j == 0)
        def _():
            xf = x_ref[...]
            n_ref[...] = (xf * _rstd(xf) * gain_ref[...]).astype(BF16)
            dn_acc[...] = jnp.zeros_like(dn_acc)

        @pl.when((i == 0) & (j == 0))
        def _():
            dgain_ref[...] = jnp.zeros_like(dgain_ref)

        halves = (pl.ds(0, tm // 2), pl.ds(tm // 2, tm // 2))
        wd, wg, wu = wd_ref[...], wg_ref[...], wu_ref[...]
        dhids = [_mm_nt(df_ref[rows, :], wd) for rows in halves]
        for rows, dhid in zip(halves, dhids):
            g = gu_ref[0, rows, :].astype(F32)
            u = gu_ref[1, rows, :].astype(F32)
            s = jax.nn.sigmoid(g)
            silu = g * s
            dg = (dhid * u * (s * (1.0 + g * (1.0 - s)))).astype(BF16)
            du = (dhid * silu).astype(BF16)
            dgu_ref[0, rows, :] = dg
            dgu_ref[1, rows, :] = du
            dn_acc[rows, :] += _mm(dg, wg) + _mm(du, wu)

        @pl.when(j == nb - 1)
        def _():
            dx, dgain = _rms_bwd(x_ref[...], gain_ref[...], dn_acc[...])
            dx_ref[...] = dh_ref[...] + dx
            dgain_ref[...] += dgain

    row = lambda i, j: (i, 0)
    return pl.pallas_call(
        body, name=name, grid=(T // tm, nb),
        in_specs=[
            pl.BlockSpec((tm, D), row),
            pl.BlockSpec((tm, D), row),
            pl.BlockSpec((tm, D), row),
            pl.BlockSpec((1, D), lambda i, j: (0, 0)),
            pl.BlockSpec((2, tm, bw), lambda i, j: (0, i, j)),
            pl.BlockSpec((None, bw, D), lambda i, j: (j, 0, 0)),
            pl.BlockSpec((None, bw, D), lambda i, j: (j + nb, 0, 0)),
            pl.BlockSpec((bw, D), lambda i, j: (j, 0)),
            AFTER,
        ],
        out_specs=[
            pl.BlockSpec((tm, D), row),
            pl.BlockSpec((1, D), lambda i, j: (0, 0)),
            pl.BlockSpec((tm, D), row),
            pl.BlockSpec((2, tm, bw), lambda i, j: (0, i, j)),
        ],
        out_shape=[
            jax.ShapeDtypeStruct((T, D), F32),
            jax.ShapeDtypeStruct((1, D), F32),
            jax.ShapeDtypeStruct((T, D), BF16),
            jax.ShapeDtypeStruct((2, T, nb * bw), BF16),
        ],
        scratch_shapes=[pltpu.VMEM((tm, D), F32)],
        compiler_params=_params(("arbitrary", "arbitrary")),
    )(dh, df, x, gain, gu, wgu, wgu, wd, _in_hbm(after))


def _wgrad(a, b, *, grid, a_spec, b_spec, out_spec, out_shape, acc_shape, name):
    nk = grid[2]

    def body(a_ref, b_ref, o_ref, acc):
        k = pl.program_id(2)

        @pl.when(k == 0)
        def _():
            acc[...] = jnp.zeros_like(acc)

        acc[...] += _mm_tn(a_ref[...].astype(BF16), b_ref[...].astype(BF16))

        @pl.when(k == nk - 1)
        def _():
            o_ref[...] = acc[...].astype(o_ref.dtype)

    return pl.pallas_call(
        body, name=name, grid=grid, in_specs=[a_spec, b_spec], out_specs=out_spec,
        out_shape=jax.ShapeDtypeStruct(out_shape, BF16),
        scratch_shapes=[pltpu.VMEM(acc_shape, F32)],
        compiler_params=_params(("arbitrary", "arbitrary", "arbitrary")),
    )(a, b)


def _wgrad_gate_up(n, dgu, *, tk, name, part=0, parts=1):
    T, D = n.shape
    tk = min(tk, T)
    owner_rows = FF_SHARD_PAD * 2
    nb = dgu.shape[2] // owner_rows
    bw = owner_rows // parts
    return _wgrad(
        dgu, n, grid=(2 * nb, 1, T // tk), name=name,
        a_spec=pl.BlockSpec((None, tk, bw), lambda m, c, k: (m // nb, k, parts * (m % nb) + part)),
        b_spec=pl.BlockSpec((tk, D), lambda m, c, k: (k, 0)),
        out_spec=pl.BlockSpec((None, bw, D), lambda m, c, k: (m, 0, 0)),
        out_shape=(2 * nb, bw, D), acc_shape=(bw, D))


def _wgrad_down(hid, df, *, tk, name):
    T, D = df.shape
    tk = min(tk, T)
    bw = FF_SHARD_PAD * 2
    nb = hid.shape[1] // bw
    return _wgrad(
        hid, df, grid=(nb, 1, T // tk), name=name,
        a_spec=pl.BlockSpec((tk, bw), lambda m, c, k: (k, m)),
        b_spec=pl.BlockSpec((tk, D), lambda m, c, k: (k, 0)),
        out_spec=pl.BlockSpec((bw, D), lambda m, c, k: (m, 0)),
        out_shape=(nb * bw, D), acc_shape=(bw, D))


def _wgrad_in(dparts, un, *, name):
    T, D = un.shape
    bw = sum(p.shape[1] for p in dparts) // N_DEV
    first = [sum(p.shape[1] for p in dparts[:i]) // bw for i in range(len(dparts) + 1)]

    def body(*refs):
        dp_refs, un_ref, o_ref = refs[:-2], refs[-2], refs[-1]
        m = pl.program_id(0)
        for dp_ref, lo, hi in zip(dp_refs, first[:-1], first[1:]):
            @pl.when((m >= lo) & (m < hi))
            def _():
                o_ref[...] = _mm_tn(dp_ref[...], un_ref[...]).astype(o_ref.dtype)

    def piece_spec(lo, hi):
        return pl.BlockSpec((T, bw), lambda m: (0, jnp.clip(m - lo, 0, hi - lo - 1)))

    return pl.pallas_call(
        body, name=name, grid=(N_DEV,),
        in_specs=[piece_spec(lo, hi) for lo, hi in zip(first[:-1], first[1:])] + [pl.BlockSpec((T, D), lambda m: (0, 0))],
        out_specs=pl.BlockSpec((None, bw, D), lambda m: (m, 0, 0)),
        out_shape=jax.ShapeDtypeStruct((N_DEV, bw, D), BF16),
        compiler_params=_params(("arbitrary",)),
    )(*dparts, un)


def _wgrad_full(a, b, *, tk, name):
    T, M = a.shape
    tk = min(tk, T)
    N = b.shape[1]
    return _wgrad(
        a, b, grid=(1, 1, T // tk), name=name,
        a_spec=pl.BlockSpec((tk, M), lambda m, c, k: (k, 0)),
        b_spec=pl.BlockSpec((tk, N), lambda m, c, k: (k, 0)),
        out_spec=pl.BlockSpec((M, N), lambda m, c, k: (0, 0)), out_shape=(M, N), acc_shape=(M, N))


def _loss_bwd(h, target, gain, *, tm, name):
    T, D = h.shape
    tm = min(tm, T)

    def body(h_ref, t_ref, gain_ref, dh_ref, df_ref, loss_ref, dgain_ref):
        @pl.when(pl.program_id(0) == 0)
        def _():
            loss_ref[...] = jnp.zeros_like(loss_ref)
            dgain_ref[...] = jnp.zeros_like(dgain_ref)

        xf = h_ref[...]
        gain = gain_ref[...]
        err = xf * _rstd(xf) * gain - t_ref[...]
        loss_ref[...] += 0.5 * jnp.sum(jnp.mean(err * err, axis=-1, keepdims=True), axis=0, keepdims=True)
        dx, dgain = _rms_bwd(xf, gain, err * (1.0 / D))
        dh_ref[...] = dx
        df_ref[...] = (0.5 * dx).astype(BF16)
        dgain_ref[...] += dgain

    row = lambda i: (i, 0)
    fixed = lambda i: (0, 0)
    return pl.pallas_call(
        body, name=name, grid=(T // tm,),
        in_specs=[pl.BlockSpec((tm, D), row), pl.BlockSpec((tm, D), row), pl.BlockSpec((1, D), fixed)],
        out_specs=[pl.BlockSpec((tm, D), row), pl.BlockSpec((tm, D), row), pl.BlockSpec((1, 128), fixed),
                   pl.BlockSpec((1, D), fixed)],
        out_shape=[jax.ShapeDtypeStruct((T, D), F32), jax.ShapeDtypeStruct((T, D), BF16),
                   jax.ShapeDtypeStruct((1, 128), F32), jax.ShapeDtypeStruct((1, D), F32)],
        compiler_params=_params(("arbitrary",)),
    )(h, target, gain)


def _inproj_fwd(h, gain, w_in_t, *, tm, name):
    T, D = h.shape
    tm = min(tm, T)
    bn = D
    nb = w_in_t.shape[0] // bn

    def body(h_ref, gain_ref, wt_ref, un_ref, proj_ref):
        @pl.when(pl.program_id(1) == 0)
        def _():
            xf = h_ref[...]
            un_ref[...] = (xf * _rstd(xf) * gain_ref[...]).astype(BF16)

        proj_ref[...] = _mm_nt(un_ref[...], wt_ref[...])

    return pl.pallas_call(
        body, name=name, grid=(T // tm, nb),
        in_specs=[
            pl.BlockSpec((tm, D), lambda i, j: (i, 0)),
            pl.BlockSpec((1, D), lambda i, j: (0, 0)),
            pl.BlockSpec((bn, D), lambda i, j: (j, 0)),
        ],
        out_specs=[pl.BlockSpec((tm, D), lambda i, j: (i, 0)), pl.BlockSpec((tm, bn), lambda i, j: (i, j))],
        out_shape=[jax.ShapeDtypeStruct((T, D), BF16), jax.ShapeDtypeStruct((T, nb * bn), F32)],
        compiler_params=_params(("arbitrary", "arbitrary")),
    )(h, gain, w_in_t)


def _inproj_bwd(dparts, dh, h, gain, w_in_t, *, tm, name):
    T, D = h.shape
    tm = min(tm, T)
    n = len(dparts)
    widths = [p.shape[1] for p in dparts]
    starts = [sum(widths[:i]) for i in range(n)]

    def body(*refs):
        dp_refs = refs[:n]
        dh_ref, h_ref, gain_ref, wt_ref, dx_ref, df_ref, dgain_ref = refs[n:]

        @pl.when(pl.program_id(0) == 0)
        def _():
            dgain_ref[...] = jnp.zeros_like(dgain_ref)

        dn = sum(_mm(dp_ref[...], wt_ref[start:start + width, :])
                 for dp_ref, start, width in zip(dp_refs, starts, widths))
        dx, dgain = _rms_bwd(h_ref[...], gain_ref[...], dn)
        dh_in = dh_ref[...] + dx
        dx_ref[...] = dh_in
        df_ref[...] = (0.5 * dh_in).astype(BF16)
        dgain_ref[...] += dgain

    row = lambda i: (i, 0)
    fixed = lambda i: (0, 0)
    return pl.pallas_call(
        body, name=name, grid=(T // tm,),
        in_specs=[pl.BlockSpec((tm, width), row) for width in widths] + [
            pl.BlockSpec((tm, D), row),
            pl.BlockSpec((tm, D), row),
            pl.BlockSpec((1, D), fixed),
            pl.BlockSpec(w_in_t.shape, fixed),
        ],
        out_specs=[pl.BlockSpec((tm, D), row), pl.BlockSpec((tm, D), row), pl.BlockSpec((1, D), fixed)],
        out_shape=[jax.ShapeDtypeStruct((T, D), F32), jax.ShapeDtypeStruct((T, D), BF16),
                   jax.ShapeDtypeStruct((1, D), F32)],
        compiler_params=_params(("arbitrary",)),
    )(*dparts, dh, h, gain, w_in_t)


def _window_sum(x, row, doublings, *, backward):
    T = x.shape[0]
    s = x
    for k in range(doublings):
        sh = 1 << k
        if backward:
            s = s + jnp.where(row < T - sh, pltpu.roll(s, T - sh, 0), 0.0)
        else:
            s = s + jnp.where(row >= sh, pltpu.roll(s, sh, 0), 0.0)
    return s


def _pool_fwd(proj, w_group, scale, *, name):
    T = proj.shape[0]

    def body(xp_ref, w_ref, scale_ref, p_ref):
        row = lax.broadcasted_iota(jnp.int32, (T, POOL_GROUP), 0)
        for gi, window in enumerate(POOL_WINDOWS):
            cols = slice(gi * POOL_GROUP, (gi + 1) * POOL_GROUP)
            x = xp_ref[:, cols]
            inv_count = 1.0 / jnp.minimum(row + 1, window).astype(F32)
            yc = _window_sum(x, row, gi + 1, backward=False) * inv_count - x
            pre = _mm(yc.astype(BF16), w_ref[gi].astype(BF16))
            p_ref[:, cols] = pre * scale_ref[:, cols]

    return pl.pallas_call(
        body, name=name, grid=(1,),
        in_specs=[
            pl.BlockSpec((T, POOL_WIDTH), lambda i: (0, 0)),
            pl.BlockSpec(w_group.shape, lambda i: (0, 0, 0)),
            pl.BlockSpec((1, POOL_WIDTH), lambda i: (0, 0)),
        ],
        out_specs=pl.BlockSpec((T, POOL_WIDTH), lambda i: (0, 0)),
        out_shape=jax.ShapeDtypeStruct((T, POOL_WIDTH), F32),
        compiler_params=_params(("arbitrary",)),
    )(proj, w_group, scale)


def _pool_bwd(dp, proj, w_group, scale, *, name):
    T = proj.shape[0]

    def body(dp_ref, xp_ref, w_ref, scale_ref, dxp_ref, dw_ref, dscale_ref):
        row = lax.broadcasted_iota(jnp.int32, (T, POOL_GROUP), 0)
        for gi, window in enumerate(POOL_WINDOWS):
            cols = slice(gi * POOL_GROUP, (gi + 1) * POOL_GROUP)
            x = xp_ref[:, cols]
            inv_count = 1.0 / jnp.minimum(row + 1, window).astype(F32)
            yc = (_window_sum(x, row, gi + 1, backward=False) * inv_count - x).astype(BF16)
            w = w_ref[gi].astype(BF16)
            pre = _mm(yc, w)
            dpg = dp_ref[:, cols]
            dscale_ref[:, cols] = jnp.sum(dpg * pre, axis=0, keepdims=True)
            dpre = (dpg * scale_ref[:, cols]).astype(BF16)
            dw_ref[gi] = _mm_tn(yc, dpre)
            dyc = _mm_nt(dpre, w)
            dxp_ref[:, cols] = (_window_sum(dyc * inv_count, row, gi + 1, backward=True) - dyc).astype(BF16)

    return pl.pallas_call(
        body, name=name, grid=(1,),
        in_specs=[
            pl.BlockSpec((T, POOL_WIDTH), lambda i: (0, 0)),
            pl.BlockSpec((T, POOL_WIDTH), lambda i: (0, 0)),
            pl.BlockSpec(w_group.shape, lambda i: (0, 0, 0)),
            pl.BlockSpec((1, POOL_WIDTH), lambda i: (0, 0)),
        ],
        out_specs=[
            pl.BlockSpec((T, POOL_WIDTH), lambda i: (0, 0)),
            pl.BlockSpec(w_group.shape, lambda i: (0, 0, 0)),
            pl.BlockSpec((1, POOL_WIDTH), lambda i: (0, 0)),
        ],
        out_shape=[jax.ShapeDtypeStruct((T, POOL_WIDTH), BF16), jax.ShapeDtypeStruct(w_group.shape, F32),
                   jax.ShapeDtypeStruct((1, POOL_WIDTH), F32)],
        compiler_params=_params(("arbitrary",)),
    )(dp, proj, w_group, scale)


ATTN_STRIP = 32


def _log_sigmoids(z):
    lb = jnp.minimum(z, 0.0) - jnp.log(1.0 + jnp.exp(-jnp.abs(z)))
    return lb, lb - z


def _transposed_blocks(x_ref, blocks_scr, tq):
    for b in range(blocks_scr.shape[0]):
        blocks_scr[b] = x_ref[b * tq:(b + 1) * tq, :].T.astype(BF16)


def _split_bf16(x):
    hi = x.astype(BF16)
    return hi, (x - hi.astype(F32)).astype(BF16)


def _strips(n):
    return [slice(i, i + ATTN_STRIP) for i in range(0, n, ATTN_STRIP)]


def _rows(parts):
    return jnp.concatenate(parts, axis=0)


def _attn_specs(T, tq):
    q_col = POOL_WIDTH // HEAD_PAIR
    k_col = q_col + SB_WIDTH // HEAD_PAIR
    v_col = k_col + SB_WIDTH // HEAD_PAIR
    return [
        pl.BlockSpec((tq, HEAD_PAIR), lambda p, i: (i, q_col + p)),
        pl.BlockSpec((T, HEAD_PAIR), lambda p, i: (0, k_col + p)),
        pl.BlockSpec((T, HEAD_PAIR), lambda p, i: (0, v_col + p)),
    ]


def _attn_fwd(proj, *, name):
    T = proj.shape[0]
    tk = min(ATTN_K_BLOCK, T)
    tq = min(ATTN_Q_BLOCK_FWD, T)
    diagonal_blocks = tq // tk

    def body(q_ref, k_ref, v_ref, o_ref, lt_ref, kt_scr, vb_scr):
        qi = pl.program_id(1)

        @pl.when(qi == 0)
        def _():
            _transposed_blocks(k_ref, kt_scr, tk)
            vb_scr[...] = v_ref[...].astype(BF16)

        head0 = lax.broadcasted_iota(jnp.int32, (tq, HEAD_PAIR), 1) < HEAD_DIM
        q = q_ref[...] * ATTN_SCALE
        qs = (jnp.where(head0, q, 0.0).astype(BF16), jnp.where(head0, 0.0, q).astype(BF16))
        r = lax.broadcasted_iota(jnp.int32, (tq, tk), 0)
        c = lax.broadcasted_iota(jnp.int32, (tq, tk), 1)
        later = (r[:tk] > c[:tk]).astype(BF16)
        later2 = _rows([later, later])
        causal = lambda d: (lambda rows: c[rows] + d * tk < r[rows])
        strips = _strips(tq)

        def log_terms(z, valid):
            lbs, his, los, sums = [], [], [], []
            for rows in strips:
                lb, lm = _log_sigmoids(z[rows])
                if valid is not None:
                    lm = jnp.where(valid(rows), lm, 0.0)
                hi, lo = _split_bf16(lm)
                lbs.append(lb)
                his.append(hi)
                los.append(lo)
                sums.append(jnp.sum(lm, axis=1, keepdims=True))
            return lbs, jnp.concatenate([_rows(his), _rows(los)], axis=1), _rows(sums)

        def weights(lbs, run, after, valid):
            parts = []
            for rows, lb in zip(strips, lbs):
                a = jnp.exp(lb + run[rows] + after[rows])
                if valid is not None:
                    a = jnp.where(valid(rows), a, 0.0)
                parts.append(a.astype(BF16))
            return _rows(parts)

        def block(kj, carry, valid):
            kt = kt_scr[kj]
            vb = vb_scr[pl.ds(pl.multiple_of(kj * tk, tk), tk), :]
            run0, o0, run1, o1 = carry
            z0 = _mm(qs[0], kt)
            z1 = _mm(qs[1], kt)
            lbs0, split0, sums0 = log_terms(z0, valid)
            after0 = _mm(split0, later2)
            lbs1, split1, sums1 = log_terms(z1, valid)
            after1 = _mm(split1, later2)
            o0 = o0 + _mm(weights(lbs0, run0, after0, valid), vb)
            o1 = o1 + _mm(weights(lbs1, run1, after1, valid), vb)
            return run0 + sums0, o0, run1 + sums1, o1

        zero = (jnp.zeros((tq, 1), F32), jnp.zeros((tq, HEAD_PAIR), F32))
        first = diagonal_blocks * qi
        carry = zero + zero
        for d in reversed(range(diagonal_blocks)):
            carry = block(first + d, carry, causal(d))
        carry = lax.fori_loop(0, first, lambda it, cr: block(first - 1 - it, cr, None), carry)
        o_ref[...] = jnp.where(head0, carry[1], carry[3])
        lt_ref[...] = jnp.where(head0, carry[0], carry[2])

    out_spec = pl.BlockSpec((tq, HEAD_PAIR), lambda p, i: (i, p))
    return pl.pallas_call(
        body, name=name, grid=(N_HEADS // 2, T // tq),
        in_specs=_attn_specs(T, tq), out_specs=[out_spec, out_spec],
        out_shape=[jax.ShapeDtypeStruct((T, SB_WIDTH), F32), jax.ShapeDtypeStruct((T, SB_WIDTH), F32)],
        scratch_shapes=[pltpu.VMEM((T // tk, HEAD_PAIR, tk), BF16), pltpu.VMEM((T, HEAD_PAIR), BF16)],
        compiler_params=_params(("arbitrary", "arbitrary")),
    )(proj, proj, proj)


def _attn_bwd(proj, do, ltot, after, *, name):
    T = proj.shape[0]
    tk = min(ATTN_K_BLOCK, T)
    tq = min(ATTN_Q_BLOCK_BWD, T)
    diagonal_blocks = tq // tk

    def body(q_ref, k_ref, v_ref, do_ref, lt_ref, after_ref, dq_ref, dk_ref, dv_ref,
             kb_scr, kt_scr, vt_scr, dkt_ref, dvt_ref):
        qi = pl.program_id(1)

        @pl.when(qi == 0)
        def _():
            kb_scr[...] = k_ref[...].astype(BF16)
            _transposed_blocks(k_ref, kt_scr, tk)
            _transposed_blocks(v_ref, vt_scr, tk)
            dkt_ref[...] = jnp.zeros_like(dkt_ref)
            dvt_ref[...] = jnp.zeros_like(dvt_ref)

        head0 = lax.broadcasted_iota(jnp.int32, (tq, HEAD_PAIR), 1) < HEAD_DIM
        q, do_, lt = q_ref[...] * ATTN_SCALE, do_ref[...], lt_ref[...]
        qs = (jnp.where(head0, q, 0.0).astype(BF16), jnp.where(head0, 0.0, q).astype(BF16))
        q_heads = (jnp.where(head0, q, 0.0), jnp.where(head0, 0.0, q))
        do_heads = (jnp.where(head0, do_, 0.0), jnp.where(head0, 0.0, do_))
        dos = tuple(d.astype(BF16) for d in do_heads)
        qts = tuple(x.T.astype(BF16) for x in q_heads)
        dots = tuple(d.T.astype(BF16) for d in do_heads)
        lts = (jnp.max(jnp.where(head0, lt, -jnp.inf), axis=1, keepdims=True),
               jnp.max(jnp.where(head0, -jnp.inf, lt), axis=1, keepdims=True))
        r = lax.broadcasted_iota(jnp.int32, (tq, tk), 0)
        c = lax.broadcasted_iota(jnp.int32, (tq, tk), 1)
        upto = (r[:tk] <= c[:tk]).astype(BF16)
        before = (r[:tk] < c[:tk]).astype(BF16)
        upto2, before2 = _rows([upto, upto]), _rows([before, before])
        causal = lambda d: (lambda rows: c[rows] + d * tk < r[rows])
        strips = _strips(tq)

        def log_terms(z, valid):
            lbs, his, los, sums = [], [], [], []
            for rows in strips:
                lb, lm = _log_sigmoids(z[rows])
                if valid is not None:
                    lm = jnp.where(valid(rows), lm, 0.0)
                hi, lo = _split_bf16(lm)
                lbs.append(lb)
                his.append(hi)
                los.append(lo)
                sums.append(jnp.sum(lm, axis=1, keepdims=True))
            return lbs, jnp.concatenate([_rows(his), _rows(los)], axis=1), _rows(sums)

        def weights(lbs, rest, lm_upto, da, valid):
            a_parts, es, his, los, sums = [], [], [], [], []
            for rows, lb in zip(strips, lbs):
                a = jnp.exp(lb + (rest[rows] - lm_upto[rows]))
                if valid is not None:
                    a = jnp.where(valid(rows), a, 0.0)
                e = da[rows] * a
                hi, lo = _split_bf16(e)
                a_parts.append(a.astype(BF16))
                es.append(e)
                his.append(hi)
                los.append(lo)
                sums.append(jnp.sum(e, axis=1, keepdims=True))
            return _rows(a_parts), es, jnp.concatenate([_rows(his), _rows(los)], axis=1), _rows(sums)

        def score_grads(lbs, es, run_e, e_before, valid):
            parts = []
            for rows, lb, e in zip(strips, lbs, es):
                beta = jnp.exp(lb)
                dz = e * (1.0 - beta) - (run_e[rows] + e_before[rows]) * beta
                if valid is not None:
                    dz = jnp.where(valid(rows), dz, 0.0)
                parts.append(dz.astype(BF16))
            return _rows(parts)

        def block(kj, carry, valid):
            off = pl.multiple_of(kj * tk, tk)
            kb, kt, vt = kb_scr[pl.ds(off, tk), :], kt_scr[kj], vt_scr[kj]
            run_lm0, run_e0, dq0, run_lm1, run_e1, dq1 = carry
            z0, da0 = _mm(qs[0], kt), _mm(dos[0], vt)
            z1, da1 = _mm(qs[1], kt), _mm(dos[1], vt)
            lbs0, split0, lm_sums0 = log_terms(z0, valid)
            lm_upto0 = _mm(split0, upto2)
            lbs1, split1, lm_sums1 = log_terms(z1, valid)
            lm_upto1 = _mm(split1, upto2)
            a0, es0, split0, e_sums0 = weights(lbs0, lts[0] - run_lm0, lm_upto0, da0, valid)
            e_before0 = _mm(split0, before2)
            a1, es1, split1, e_sums1 = weights(lbs1, lts[1] - run_lm1, lm_upto1, da1, valid)
            e_before1 = _mm(split1, before2)
            dz0 = score_grads(lbs0, es0, run_e0, e_before0, valid)
            dkt_blk = _mm(qts[0], dz0)
            dvt_blk = _mm(dots[0], a0)
            dq0 = dq0 + _mm(dz0, kb)
            dz1 = score_grads(lbs1, es1, run_e1, e_before1, valid)
            dkt_ref[kj] += dkt_blk + _mm(qts[1], dz1)
            dvt_ref[kj] += dvt_blk + _mm(dots[1], a1)
            dq1 = dq1 + _mm(dz1, kb)
            return run_lm0 + lm_sums0, run_e0 + e_sums0, dq0, run_lm1 + lm_sums1, run_e1 + e_sums1, dq1

        zero = (jnp.zeros((tq, 1), F32), jnp.zeros((tq, 1), F32), jnp.zeros((tq, HEAD_PAIR), F32))
        first = diagonal_blocks * qi
        carry = lax.fori_loop(0, first, lambda kj, cr: block(kj, cr, None), zero + zero)
        for d in range(diagonal_blocks):
            carry = block(first + d, carry, causal(d))
        dq_ref[...] = (jnp.where(head0, carry[2], carry[5]) * ATTN_SCALE).astype(BF16)

        @pl.when(qi == T // tq - 1)
        def _():
            for b in range(T // tk):
                dk_ref[b * tk:(b + 1) * tk, :] = dkt_ref[b].T.astype(BF16)
                dv_ref[b * tk:(b + 1) * tk, :] = dvt_ref[b].T.astype(BF16)

    blk = pl.BlockSpec((tq, HEAD_PAIR), lambda p, i: (i, p))
    seq = pl.BlockSpec((T, HEAD_PAIR), lambda p, i: (0, p))
    transposed = pltpu.VMEM((T // tk, HEAD_PAIR, tk), F32)
    return pl.pallas_call(
        body, name=name, grid=(N_HEADS // 2, T // tq),
        in_specs=_attn_specs(T, tq) + [blk, blk, AFTER], out_specs=[blk, seq, seq],
        out_shape=[jax.ShapeDtypeStruct((T, SB_WIDTH), BF16)] * 3,
        scratch_shapes=[pltpu.VMEM((T, HEAD_PAIR), BF16), pltpu.VMEM((T // tk, HEAD_PAIR, tk), BF16),
                        pltpu.VMEM((T // tk, HEAD_PAIR, tk), BF16), transposed, transposed],
        compiler_params=_params(("arbitrary", "arbitrary")),
    )(proj, proj, proj, do, ltot, _in_hbm(after))


def _mix_specs(T, D, tm, wbp, w_out):
    gate_col = (POOL_WIDTH + 3 * SB_WIDTH) // D
    row = lambda i: (i, 0)
    return [
        pl.BlockSpec((tm, D), row),
        pl.BlockSpec((tm, POOL_WIDTH), row),
        pl.BlockSpec((tm, SB_WIDTH), row),
        pl.BlockSpec((tm, D), lambda i: (i, gate_col)),
        pl.BlockSpec((tm, D), lambda i: (i, gate_col + 1)),
        pl.BlockSpec(wbp.shape, lambda i: (0, 0)),
        pl.BlockSpec(wbp.shape, lambda i: (0, 0)),
        pl.BlockSpec(w_out.shape, lambda i: (0, 0)),
    ]


def _mix_fwd(h, p, o, proj, wbp, wba, w_out, *, tm, name):
    T, D = h.shape
    tm = min(tm, T)

    def body(h_ref, p_ref, o_ref, glp_ref, gls_ref, wbp_ref, wba_ref, wout_ref, hout_ref, m_ref):
        halves = (pl.ds(0, tm // 2), pl.ds(tm // 2, tm // 2))
        wbp, wba, wout = wbp_ref[...], wba_ref[...], wout_ref[...]
        branches = [(_mm_nt(p_ref[rows, :].astype(BF16), wbp), _mm_nt(o_ref[rows, :].astype(BF16), wba))
                    for rows in halves]
        for rows, (yp, ys) in zip(halves, branches):
            m = (jax.nn.sigmoid(glp_ref[rows, :]) * yp + jax.nn.sigmoid(gls_ref[rows, :]) * ys).astype(BF16)
            m_ref[rows, :] = m
            hout_ref[rows, :] = h_ref[rows, :] + _mm(m, wout)

    row = lambda i: (i, 0)
    return pl.pallas_call(
        body, name=name, grid=(T // tm,),
        in_specs=_mix_specs(T, D, tm, wbp, w_out),
        out_specs=[pl.BlockSpec((tm, D), row), pl.BlockSpec((tm, D), row)],
        out_shape=[jax.ShapeDtypeStruct((T, D), F32), jax.ShapeDtypeStruct((T, D), BF16)],
        compiler_params=_params(("arbitrary",)),
    )(h, p, o, proj, proj, wbp, wba, w_out)


def _mix_bwd(dh, p, o, proj, wbp, wba, w_out, after, *, tm, name):
    T, D = dh.shape
    tm = min(tm, T)

    def body(dh_ref, p_ref, o_ref, glp_ref, gls_ref, wbp_ref, wba_ref, wout_ref, after_ref,
             dyp_ref, dys_ref, dp_ref, do_ref, dgl_ref):
        halves = (pl.ds(0, tm // 2), pl.ds(tm // 2, tm // 2))
        wbp, wba, wout = wbp_ref[...], wba_ref[...], wout_ref[...]
        products = [(_mm_nt(dh_ref[rows, :].astype(BF16), wout), _mm_nt(p_ref[rows, :].astype(BF16), wbp),
                     _mm_nt(o_ref[rows, :].astype(BF16), wba)) for rows in halves]
        for rows, (dm, yp, ys) in zip(halves, products):
            gp = jax.nn.sigmoid(glp_ref[rows, :])
            gs = jax.nn.sigmoid(gls_ref[rows, :])
            dyp = (dm * gp).astype(BF16)
            dys = (dm * gs).astype(BF16)
            dyp_ref[rows, :] = dyp
            dys_ref[rows, :] = dys
            dgl_ref[rows, :D] = (dm * yp * gp * (1.0 - gp)).astype(BF16)
            dgl_ref[rows, D:] = (dm * ys * gs * (1.0 - gs)).astype(BF16)
            dp_ref[rows, :] = _mm(dyp, wbp)
            do_ref[rows, :] = _mm(dys, wba)

    row = lambda i: (i, 0)
    return pl.pallas_call(
        body, name=name, grid=(T // tm,),
        in_specs=_mix_specs(T, D, tm, wbp, w_out) + [AFTER],
        out_specs=[pl.BlockSpec((tm, D), row), pl.BlockSpec((tm, D), row), pl.BlockSpec((tm, POOL_WIDTH), row),
                   pl.BlockSpec((tm, SB_WIDTH), row), pl.BlockSpec((tm, 2 * D), row)],
        out_shape=[jax.ShapeDtypeStruct((T, D), BF16), jax.ShapeDtypeStruct((T, D), BF16),
                   jax.ShapeDtypeStruct((T, POOL_WIDTH), F32), jax.ShapeDtypeStruct((T, SB_WIDTH), F32),
                   jax.ShapeDtypeStruct((T, 2 * D), BF16)],
        compiler_params=_params(("arbitrary",)),
    )(dh, p, o, proj, proj, wbp, wba, w_out, _in_hbm(after))


def _adamw_update(w, g, m, v):
    m_ = ADAM_B1 * m + (1.0 - ADAM_B1) * g
    v_ = ADAM_B2 * v + (1.0 - ADAM_B2) * (g * g)
    m_hat = m_ / (1.0 - ADAM_B1 ** ADAM_STEP)
    v_hat = v_ / (1.0 - ADAM_B2 ** ADAM_STEP)
    return -ADAM_LR * (m_hat / (jnp.sqrt(v_hat) + ADAM_EPS) + ADAM_WD * w), m_, v_


def _adamw(w, g, m, v, *, name):
    R, C = w.shape
    tr = _row_tile(R, C)

    def body(w_ref, g_ref, m_ref, v_ref, d_ref, nm_ref, nv_ref):
        d_ref[...], nm_ref[...], nv_ref[...] = _adamw_update(w_ref[...], g_ref[...], m_ref[...], v_ref[...])

    spec = pl.BlockSpec((tr, C), lambda i: (i, 0))
    return pl.pallas_call(
        body, name=name, grid=(R // tr,), in_specs=[spec] * 4, out_specs=[spec] * 3,
        out_shape=[jax.ShapeDtypeStruct((R, C), F32)] * 3,
        compiler_params=_params(("arbitrary",)),
    )(w, g, m, v)


def _position():
    return lax.axis_index("x"), lax.axis_index("y"), lax.axis_index("c")


def _all_gather(shards, *, name, collective_id):
    n = len(shards)
    n_copies = 9

    def body(*refs):
        ins, outs = refs[:n], refs[n:2 * n]
        send_sems, recv_sems, local_sems = refs[2 * n:]
        x, y, c = _position()
        me, sibling = (x, y, c), (x, y, 1 - c)
        x_nbr, y_nbr, diagonal = (1 - x, y, c), (x, 1 - y, c), (1 - x, 1 - y, c)
        other = lambda pos: (pos[0], pos[1], 1 - c)

        barrier = pltpu.get_barrier_semaphore()
        for peer in (sibling, x_nbr, y_nbr):
            pl.semaphore_signal(barrier, inc=1, device_id=peer, device_id_type=MESH)
        pl.semaphore_wait(barrier, 3)

        def block(a, pos, half=None):
            ref = outs[a].at[4 * pos[0] + 2 * pos[1] + pos[2]]
            rows = ref.shape[0] // 2
            return ref if half is None else ref.at[pl.ds(half * rows, rows)]

        def copy(a, k, pos, to, half=None, src=None):
            return pltpu.make_async_remote_copy(
                src_ref=block(a, pos, half) if src is None else src, dst_ref=block(a, pos, half),
                send_sem=send_sems.at[n_copies * a + k], recv_sem=recv_sems.at[n_copies * a + k],
                device_id=to, device_id_type=MESH)

        started = []
        for a in range(n):
            mine = pltpu.make_async_copy(ins[a], block(a, me), local_sems.at[a])
            mine.start()
            started.append(mine)
        sends = []
        for a in range(n):
            sends += [copy(a, 1, me, x_nbr, src=ins[a]), copy(a, 2, me, y_nbr, src=ins[a]),
                      copy(a, 0, me, sibling, src=ins[a])]
        for cp in sends:
            cp.start()

        def pass_on(copies):
            for cp in copies:
                cp.start()
                sends.append(cp)

        for a in range(n):
            copy(a, 1, x_nbr, me).wait_recv()
            pass_on([copy(a, 5, x_nbr, y_nbr, half=0), copy(a, 3, x_nbr, sibling)])
            copy(a, 2, y_nbr, me).wait_recv()
            pass_on([copy(a, 6, y_nbr, x_nbr, half=1), copy(a, 4, y_nbr, sibling)])
        for a in range(n):
            copy(a, 5, diagonal, me, half=0).wait_recv()
            pass_on([copy(a, 7, diagonal, sibling, half=0)])
            copy(a, 6, diagonal, me, half=1).wait_recv()
            pass_on([copy(a, 8, diagonal, sibling, half=1)])
        for a in range(n):
            copy(a, 0, sibling, me).wait_recv()
            copy(a, 3, other(x_nbr), me).wait_recv()
            copy(a, 4, other(y_nbr), me).wait_recv()
            copy(a, 7, other(diagonal), me, half=0).wait_recv()
            copy(a, 8, other(diagonal), me, half=1).wait_recv()
        for cp in sends:
            cp.wait_send()
        for cp in started:
            cp.wait()

    return pl.kernel(
        body, name=name,
        out_type=[jax.ShapeDtypeStruct((N_DEV,) + s.shape, s.dtype) for s in shards],
        mesh=plsc.ScalarSubcoreMesh(axis_name="sequencer", num_cores=1),
        scratch_types=[pltpu.SemaphoreType.DMA((n_copies * n,)), pltpu.SemaphoreType.DMA((n_copies * n,)),
                       pltpu.SemaphoreType.DMA((n,))],
        compiler_params=pltpu.CompilerParams(collective_id=collective_id),
    )(*shards)


def _chip_sums(group, *, name):
    n = len(group)
    shapes = [g.shape[1:] for g in group]

    def body(*refs):
        g_refs, partials, out_refs = refs[:n], refs[n:3 * n:2], refs[n + 1:3 * n:2]
        mines, theirs = refs[3 * n:5 * n:2], refs[3 * n + 1:5 * n:2]
        send_sems, recv_sems, local_sems = refs[5 * n:]
        x, y, c = _position()
        my_chip = 2 * x + y

        def swap(a, s):
            return pltpu.make_async_remote_copy(
                src_ref=g_refs[a].at[2 * s + (1 - c)], dst_ref=theirs[a].at[s],
                send_sem=send_sems.at[4 * a + s], recv_sem=recv_sems.at[4 * a + s],
                device_id=(x, y, 1 - c), device_id_type=MESH)

        def load(a, s):
            return pltpu.make_async_copy(g_refs[a].at[2 * s + c], mines[a].at[s], local_sems.at[4 * a + s])

        for a in range(n):
            for s in range(4):
                swap(a, s).start()
                load(a, s).start()

        for a, (R, C) in enumerate(shapes):
            rc = 128 if R % 128 == 0 else R

            def chip_sum(chip, rows):
                return mines[a][chip, rows, :].astype(F32) + theirs[a][chip, rows, :].astype(F32)

            for s in range(4):
                load(a, s).wait()
                swap(a, s).wait_recv()

                @pl.when(s == my_chip)
                def _():
                    @pl.loop(0, R // rc)
                    def _(t):
                        rows = pl.ds(pl.multiple_of(t * rc, rc), rc)
                        out_refs[a][rows, :] = chip_sum(s, rows)

                @pl.when(s != my_chip)
                def _():
                    @pl.loop(0, R // rc)
                    def _(t):
                        rows = pl.ds(pl.multiple_of(t * rc, rc), rc)
                        partials[a][(s ^ my_chip) - 1, rows, :] = chip_sum(s, rows).astype(BF16)

        for a in range(n):
            for s in range(4):
                swap(a, s).wait_send()

    vmem = pl.BlockSpec(memory_space=pltpu.VMEM)
    outs = pl.pallas_call(
        body, name=name,
        in_specs=[pl.BlockSpec(memory_space=pl.ANY)] * n, out_specs=[vmem] * (2 * n),
        out_shape=[shape for R, C in shapes
                   for shape in (jax.ShapeDtypeStruct((3, R, C), BF16), jax.ShapeDtypeStruct((R, C), F32))],
        scratch_shapes=[pltpu.VMEM((4, R, C), BF16) for R, C in shapes for _ in range(2)] + [
            pltpu.SemaphoreType.DMA((4 * n,)), pltpu.SemaphoreType.DMA((4 * n,)), pltpu.SemaphoreType.DMA((4 * n,))],
        compiler_params=_params(),
    )(*group)
    return [(outs[2 * a], outs[2 * a + 1]) for a in range(n)]


def _cross_chips(partials, *, name, collective_id):
    n = len(partials)

    def body(*refs):
        ins, outs = refs[:n], refs[n:2 * n]
        send_sems, recv_sems = refs[2 * n:]
        x, y, c = _position()
        my_chip = 2 * x + y
        peers = [((my_chip ^ j) // 2, (my_chip ^ j) % 2, c) for j in (1, 2, 3)]

        barrier = pltpu.get_barrier_semaphore()
        for peer in peers:
            pl.semaphore_signal(barrier, inc=1, device_id=peer, device_id_type=MESH)
        pl.semaphore_wait(barrier, 3)

        copies = [
            pltpu.make_async_remote_copy(
                src_ref=ins[a].at[j], dst_ref=outs[a].at[j],
                send_sem=send_sems.at[3 * a + j], recv_sem=recv_sems.at[3 * a + j],
                device_id=peers[j], device_id_type=MESH)
            for a in range(n) for j in range(3)]
        for cp in copies:
            cp.start()
        for cp in copies:
            cp.wait_recv()
        for cp in copies:
            cp.wait_send()

    return pl.kernel(
        body, name=name,
        out_type=[jax.ShapeDtypeStruct(p.shape, p.dtype) for p in partials],
        mesh=plsc.ScalarSubcoreMesh(axis_name="sequencer", num_cores=1),
        scratch_types=[pltpu.SemaphoreType.DMA((3 * n,)), pltpu.SemaphoreType.DMA((3 * n,))],
        compiler_params=pltpu.CompilerParams(collective_id=collective_id),
    )(*partials)


def _cross_chips_and_gather(partials, slab, *, name, collective_id):
    n = len(partials)

    def body(*refs):
        part_refs, slab_ref = refs[:n], refs[n]
        landed_refs, slabs_ref = refs[n + 1:2 * n + 1], refs[2 * n + 1]
        send_sems, recv_sems, local_sem = refs[2 * n + 2:]
        x, y, c = _position()
        me, my_chip = 4 * x + 2 * y + c, 2 * x + y
        others = [me ^ k for k in range(1, N_DEV)]
        ids = [(o // 4, (o // 2) % 2, o % 2) for o in others]

        barrier = pltpu.get_barrier_semaphore()
        for peer in ids:
            pl.semaphore_signal(barrier, inc=1, device_id=peer, device_id_type=MESH)
        pl.semaphore_wait(barrier, N_DEV - 1)

        mine = pltpu.make_async_copy(slab_ref, slabs_ref.at[me], local_sem)
        mine.start()
        sends = [
            pltpu.make_async_remote_copy(
                src_ref=part_refs[a].at[j], dst_ref=landed_refs[a].at[j],
                send_sem=send_sems.at[3 * a + j], recv_sem=recv_sems.at[3 * a + j],
                device_id=((my_chip ^ (j + 1)) // 2, (my_chip ^ (j + 1)) % 2, c), device_id_type=MESH)
            for a in range(n) for j in range(3)]
        sends += [
            pltpu.make_async_remote_copy(
                src_ref=slab_ref, dst_ref=slabs_ref.at[me],
                send_sem=send_sems.at[3 * n + k], recv_sem=recv_sems.at[3 * n + k],
                device_id=ids[k], device_id_type=MESH)
            for k in range(N_DEV - 1)]
        arrivals = sends[:3 * n] + [
            pltpu.make_async_remote_copy(
                src_ref=slab_ref, dst_ref=slabs_ref.at[others[k]],
                send_sem=send_sems.at[3 * n + k], recv_sem=recv_sems.at[3 * n + k],
                device_id=ids[k], device_id_type=MESH)
            for k in range(N_DEV - 1)]
        for cp in sends:
            cp.start()
        for cp in arrivals:
            cp.wait_recv()
        for cp in sends:
            cp.wait_send()
        mine.wait()

    n_sems = 3 * n + N_DEV - 1
    outs = pl.kernel(
        body, name=name,
        out_type=[jax.ShapeDtypeStruct(p.shape, p.dtype) for p in partials]
                 + [jax.ShapeDtypeStruct((N_DEV,) + slab.shape, slab.dtype)],
        mesh=plsc.ScalarSubcoreMesh(axis_name="sequencer", num_cores=1),
        scratch_types=[pltpu.SemaphoreType.DMA((n_sems,)), pltpu.SemaphoreType.DMA((n_sems,)), pltpu.SemaphoreType.DMA],
        compiler_params=pltpu.CompilerParams(collective_id=collective_id),
    )(*partials, slab)
    return outs[:n], outs[n]


def _sum_devices(gathered, after, *, name):
    _, R, C = gathered.shape

    def body(in_ref, after_ref, out_ref):
        total = in_ref[0]
        for d in range(1, N_DEV):
            total = total + in_ref[d]
        out_ref[...] = total

    return pl.pallas_call(
        body, name=name, grid=(1,),
        in_specs=[pl.BlockSpec((N_DEV, R, C), lambda i: (0, 0, 0)), AFTER],
        out_specs=pl.BlockSpec((R, C), lambda i: (0, 0)),
        out_shape=jax.ShapeDtypeStruct((R, C), F32),
        compiler_params=_params(("arbitrary",)),
    )(gathered, _in_hbm(after))


def _owner_sum(own, landed, after, *, name):
    R, C = own.shape
    tr = _row_tile(R, C)

    def body(own_ref, landed_ref, after_ref, out_ref):
        total = own_ref[...]
        for j in range(3):
            total = total + landed_ref[j].astype(F32)
        out_ref[...] = total

    return pl.pallas_call(
        body, name=name, grid=(R // tr,),
        in_specs=[pl.BlockSpec((tr, C), lambda i: (i, 0)), pl.BlockSpec((3, tr, C), lambda i: (0, i, 0)), AFTER],
        out_specs=pl.BlockSpec((tr, C), lambda i: (i, 0)),
        out_shape=jax.ShapeDtypeStruct((R, C), F32),
        compiler_params=_params(("arbitrary",)),
    )(own, landed, _in_hbm(after))


def _owner_sum_adamw(own, landed, w, m, v, after, *, transposed, name):
    H, R, C = w.shape
    tr = R // 2

    def body(own_ref, landed_ref, w_ref, m_ref, v_ref, after_ref, g_ref, d_ref, nm_ref, nv_ref):
        total = own_ref[...]
        for j in range(3):
            total = total + landed_ref[j].astype(F32)
        if transposed:
            total = total.T
        g_ref[...] = total
        d_ref[...], nm_ref[...], nv_ref[...] = _adamw_update(w_ref[...], total, m_ref[...], v_ref[...])

    spec = pl.BlockSpec((None, tr, C), lambda h, i: (h, i, 0))
    if transposed:
        own_spec = pl.BlockSpec((None, C, tr), lambda h, i: (h, 0, i))
        landed_spec = pl.BlockSpec((3, None, C, tr), lambda h, i: (0, h, 0, i))
    else:
        own_spec, landed_spec = spec, pl.BlockSpec((3, None, tr, C), lambda h, i: (0, h, i, 0))
    return pl.pallas_call(
        body, name=name, grid=(H, R // tr),
        in_specs=[own_spec, landed_spec, spec, spec, spec, AFTER],
        out_specs=[spec] * 4,
        out_shape=[jax.ShapeDtypeStruct((H, R, C), F32)] * 4,
        compiler_params=_params(("arbitrary", "arbitrary")),
    )(own, landed, w, m, v, _in_hbm(after))


def _local_step(x, target, norms, pool_w_group, pool_scale, wgu1, wd1, w_in, wbp, wba, w_out, wgu2, wd2, exchange):
    n1g, nmg, n2g, nfg = norms
    D = x.shape[1]
    gu1, hid1 = _ffn_up(x, n1g, wgu1, tm=1024, name="ffn1_up")
    h1 = _ffn_down(x, hid1, wd1, tm=512, name="ffn1_down")
    un, proj = _inproj_fwd(h1, nmg, w_in, tm=1024, name="inproj_fwd")
    p = _pool_fwd(proj, pool_w_group, pool_scale, name="pool_fwd")
    o, ltot = _attn_fwd(proj, name="attn_fwd")
    h2, m = _mix_fwd(h1, p, o, proj, wbp, wba, w_out, tm=512, name="mix_fwd")
    gu2, hid2 = _ffn_up(h2, n2g, wgu2, tm=1024, name="ffn2_up")
    h3 = _ffn_down(h2, hid2, wd2, tm=512, name="ffn2_down")
    dh3, df2, loss, d_nf = _loss_bwd(h3, target, nfg, tm=256, name="loss_bwd")

    dh2, d_n2, n2, dgu2 = _ffn_bwd(dh3, df2, h2, n2g, gu2, wgu2, wd2, df2, tm=512, name="ffn2_bwd")
    d_wd2 = _wgrad_down(hid2, df2, tk=WGRAD_TOKENS, name="ffn2_wgrad_down")
    d_wgu2 = _wgrad_gate_up(n2, dgu2, tk=WGRAD_TOKENS, name="ffn2_wgrad_gate_up")
    (g_wd2, g_wgu2), token = exchange("ffn2", [d_wd2.reshape(N_DEV, FF_SHARD_PAD, D), d_wgu2])

    dyp, dys, dp, do, dgl = _mix_bwd(dh2, p, o, proj, wbp, wba, w_out, token, tm=512, name="mix_bwd")
    d_wout = _wgrad_full(m, dh2, tk=WGRAD_TOKENS, name="wgrad_out")
    d_wbp = _wgrad_full(dyp, p, tk=WGRAD_TOKENS, name="wgrad_branch_pool")
    d_wba = _wgrad_full(dys, o, tk=WGRAD_TOKENS, name="wgrad_branch_attn")
    by_owner = lambda g: g.reshape(N_DEV, g.shape[0] // N_DEV, g.shape[1])
    (g_wbp, g_wba, g_wout), token = exchange("mix", [by_owner(d_wbp), by_owner(d_wba), by_owner(d_wout)])
    dxp, d_wgroup, d_scale = _pool_bwd(dp, proj, pool_w_group, pool_scale, name="pool_bwd")
    dq, dk, dv = _attn_bwd(proj, do, ltot, token, name="attn_bwd")
    dproj_parts = [dxp, dq, dk, dv, dgl]
    dh1, df1, d_nm = _inproj_bwd(dproj_parts, dh2, h1, nmg, w_in, tm=512, name="inproj_bwd")
    d_win = _wgrad_in(dproj_parts, un, name="wgrad_in")
    d_wd1 = _wgrad_down(hid1, df1, tk=WGRAD_TOKENS, name="ffn1_wgrad_down")
    (g_win, g_wd1, replicated_early), token = exchange(
        "w_in_ffn1_down", [d_win, d_wd1.reshape(N_DEV, FF_SHARD_PAD, D), d_nm, d_n2, d_nf, d_scale, d_wgroup, loss])

    dx, d_n1, n1, dgu1 = _ffn_bwd(dh1, df1, x, n1g, gu1, wgu1, wd1, token, tm=512, name="ffn1_bwd")
    d_wgu1_a = _wgrad_gate_up(n1, dgu1, tk=WGRAD_TOKENS, name="ffn1_wgrad_gate_up_a", part=0, parts=2)
    (g_wgu1_a, replicated_late), token = exchange("ffn1_gate_up_a", [d_wgu1_a, d_n1])
    d_wgu1_b = _wgrad_gate_up(n1, dgu1, tk=WGRAD_TOKENS, name="ffn1_wgrad_gate_up_b", part=1, parts=2)
    (g_wgu1_b,), token = exchange("last", [d_wgu1_b])
    g_wgu1 = (g_wgu1_a, g_wgu1_b)

    sharded = (g_wgu1, g_wd1, g_win, g_wbp, g_wba, g_wout, g_wgu2, g_wd2)
    return dx, sharded, (replicated_late, replicated_early), token


def _hidden_major(w):
    return jnp.swapaxes(w[0], 0, 1)


def _pad_gate_up(wt):
    d = wt.shape[1]
    wt = wt.astype(BF16).reshape(2, FF_SHARD, d)
    return jnp.pad(wt, ((0, 0), (0, FF_SHARD_PAD - FF_SHARD), (0, 0))).reshape(2 * FF_SHARD_PAD, d)


def _unpad_gate_up(gt):
    d = gt.shape[1]
    return gt.reshape(2, FF_SHARD_PAD, d)[:, :FF_SHARD].reshape(2 * FF_SHARD, d)


def _pad_down(w):
    return jnp.pad(w.astype(BF16), ((0, FF_SHARD_PAD - FF_SHARD), (0, 0)))


def kernel(x, ffn1_norm, ffn1_w_gate_up, ffn1_w_down, mix_norm, w_in, pool_w_group, pool_scale, w_branch_pool, w_branch_attn, w_out, ffn2_norm, ffn2_w_gate_up, ffn2_w_down, final_norm, loss_target, m_ffn1_norm, m_ffn1_w_gate_up, m_ffn1_w_down, m_mix_norm, m_w_in, m_pool_w_group, m_pool_scale, m_w_branch_pool, m_w_branch_attn, m_w_out, m_ffn2_norm, m_ffn2_w_gate_up, m_ffn2_w_down, m_final_norm, v_ffn1_norm, v_ffn1_w_gate_up, v_ffn1_w_down, v_mix_norm, v_w_in, v_pool_w_group, v_pool_scale, v_w_branch_pool, v_w_branch_attn, v_w_out, v_ffn2_norm, v_ffn2_w_gate_up, v_ffn2_w_down, v_final_norm):
    D = x.shape[-1]
    weights = dict(ffn1_norm=ffn1_norm, ffn1_w_gate_up=ffn1_w_gate_up, ffn1_w_down=ffn1_w_down, mix_norm=mix_norm,
                   w_in=w_in, pool_w_group=pool_w_group, pool_scale=pool_scale, w_branch_pool=w_branch_pool,
                   w_branch_attn=w_branch_attn, w_out=w_out, ffn2_norm=ffn2_norm, ffn2_w_gate_up=ffn2_w_gate_up,
                   ffn2_w_down=ffn2_w_down, final_norm=final_norm)
    first = dict(ffn1_norm=m_ffn1_norm, ffn1_w_gate_up=m_ffn1_w_gate_up, ffn1_w_down=m_ffn1_w_down,
                 mix_norm=m_mix_norm, w_in=m_w_in, pool_w_group=m_pool_w_group, pool_scale=m_pool_scale,
                 w_branch_pool=m_w_branch_pool, w_branch_attn=m_w_branch_attn, w_out=m_w_out,
                 ffn2_norm=m_ffn2_norm, ffn2_w_gate_up=m_ffn2_w_gate_up, ffn2_w_down=m_ffn2_w_down,
                 final_norm=m_final_norm)
    second = dict(ffn1_norm=v_ffn1_norm, ffn1_w_gate_up=v_ffn1_w_gate_up, ffn1_w_down=v_ffn1_w_down,
                  mix_norm=v_mix_norm, w_in=v_w_in, pool_w_group=v_pool_w_group, pool_scale=v_pool_scale,
                  w_branch_pool=v_w_branch_pool, w_branch_attn=v_w_branch_attn, w_out=v_w_out,
                  ffn2_norm=v_ffn2_norm, ffn2_w_gate_up=v_ffn2_w_gate_up, ffn2_w_down=v_ffn2_w_down,
                  final_norm=v_final_norm)
    order = list(weights)

    wgu1, = _all_gather([_pad_gate_up(_hidden_major(ffn1_w_gate_up))], name="all_gather_ffn1_gate_up", collective_id=0)
    wd1, = _all_gather([_pad_down(ffn1_w_down[0])], name="all_gather_ffn1_down", collective_id=10)
    transposed = lambda w: jnp.swapaxes(w[0], 0, 1).astype(BF16)
    win_g, = _all_gather([transposed(w_in)], name="all_gather_w_in", collective_id=1)
    wbp_g, wba_g = _all_gather([transposed(w_branch_pool), transposed(w_branch_attn)],
                               name="all_gather_branches", collective_id=2)
    wout_g, = _all_gather([w_out[0].astype(BF16)], name="all_gather_w_out", collective_id=11)
    wgu2, wd2 = _all_gather([_pad_gate_up(_hidden_major(ffn2_w_gate_up)), _pad_down(ffn2_w_down[0])],
                            name="all_gather_ffn2", collective_id=3)
    whole = lambda g: g.reshape(g.shape[0] * g.shape[1], g.shape[2])
    wd1, wd2, win_g, wbp_g, wba_g, wout_g = (whole(g) for g in (wd1, wd2, win_g, wbp_g, wba_g, wout_g))

    cross_ids = {"ffn2": 4, "mix": 5, "w_in_ffn1_down": 8, "ffn1_gate_up_a": 9, "last": 7}
    small = ["ffn1_norm", "mix_norm", "ffn2_norm", "final_norm", "pool_scale", "pool_w_group"]

    def tile_rows(a):
        a = a.reshape(-1, 128)
        return jnp.pad(a, ((0, -a.shape[0] % 8), (0, 0)))

    def exchange(tag, group):
        grads = [g for g in group if g.dtype == BF16]
        extras = [tile_rows(g) for g in group if g.dtype != BF16]
        sums = _chip_sums(grads, name="chip_sums_" + tag)
        partials = [s[0] for s in sums]
        handles = []
        if extras:
            landed, slabs = _cross_chips_and_gather(partials, jnp.concatenate(extras, axis=0),
                                                    name="cross_chips_" + tag, collective_id=cross_ids[tag])
            handles = [slabs]
        else:
            landed = _cross_chips(partials, name="cross_chips_" + tag, collective_id=cross_ids[tag])
        return [(s[1], l) for s, l in zip(sums, landed)] + handles, sums[-1][1]

    norms = (ffn1_norm, mix_norm, ffn2_norm, final_norm.reshape(1, D))
    dx, sharded, (slabs_late, slabs_early), last = _local_step(
        x[0], loss_target[0], norms, pool_w_group[0], pool_scale, wgu1, wd1, win_g, wbp_g, wba_g, wout_g, wgu2, wd2,
        exchange)
    names = ["ffn1_w_gate_up", "ffn1_w_down", "w_in", "w_branch_pool", "w_branch_attn", "w_out",
             "ffn2_w_gate_up", "ffn2_w_down"]
    handles = dict(zip(names, sharded))
    grads, delta, new_m, new_v = {}, {}, {}, {}
    loss_out = []

    def update_replicated(after):
        rows = [weights[k].size // 128 for k in small]
        padded_rows = [-(-r // 8) * 8 for r in rows]
        starts = [sum(padded_rows[:i]) for i in range(len(rows) + 1)]
        total = jnp.concatenate([_sum_devices(slabs_late, after, name="sum_replicated_late"),
                                 _sum_devices(slabs_early, after, name="sum_replicated_early")], axis=0)
        loss_out.append(total[starts[-1], 0])
        small_w = jnp.concatenate([tile_rows(weights[k]) for k in small], axis=0)
        small_m = jnp.concatenate([tile_rows(first[k]) for k in small], axis=0)
        small_v = jnp.concatenate([tile_rows(second[k]) for k in small], axis=0)
        small_out = _adamw(small_w, total[:starts[-1]], small_m, small_v, name="adamw_replicated")
        for name_, start, n_rows in zip(small, starts, rows):
            shape = weights[name_].shape
            grads[name_] = total[start:start + n_rows].reshape(shape)
            delta[name_], new_m[name_], new_v[name_] = (a[start:start + n_rows].reshape(shape) for a in small_out)
        return small_out[0]

    after = last
    for k in ("ffn2_w_down", "ffn2_w_gate_up", "w_branch_pool", "w_branch_attn", "w_out", "w_in", "ffn1_w_down",
              "ffn1_w_gate_up"):
        hidden_major = k.endswith("w_gate_up")
        view = _hidden_major if hidden_major else (lambda a: a[0])
        back = (lambda a: jnp.swapaxes(a, 0, 1)[None]) if hidden_major else (lambda a: a[None])
        if not isinstance(handles[k][0], tuple):
            own, landed = handles[k]
            groups = 2 if hidden_major else 1
            by_group = lambda a: a.reshape(a.shape[:-2] + (groups, a.shape[-2] // groups, a.shape[-1]))
            out = _owner_sum_adamw(by_group(own), by_group(landed), by_group(view(weights[k])),
                                   by_group(view(first[k])), by_group(view(second[k])), after, name="adamw_" + k,
                                   transposed=k in ("w_in", "w_branch_pool", "w_branch_attn"))
            after = out[1]
            grads[k], delta[k], new_m[k], new_v[k] = (back(a.reshape(-1, a.shape[-1])) for a in out)
            continue
        first_half = _owner_sum(*handles[k][0], after, name="owner_sum_" + k + "_a")
        second_half = _owner_sum(*handles[k][1], update_replicated(first_half), name="owner_sum_" + k + "_b")
        g = jnp.concatenate([first_half[:FF_SHARD], second_half[:FF_SHARD]], axis=0)
        out = _adamw(view(weights[k]), g, view(first[k]), view(second[k]), name="adamw_" + k)
        after = out[0]
        grads[k] = back(g)
        delta[k], new_m[k], new_v[k] = (back(a) for a in out)

    return (loss_out[0], dx[None], *[grads[k] for k in order], *[delta[k] for k in order],
            *[new_m[k] for k in order], *[new_v[k] for k in order])
```

```python
import jax
import jax.numpy as jnp
from jax import lax
from jax.experimental import pallas as pl
from jax.experimental.pallas import tpu as pltpu
from jax.experimental.pallas import tpu_sc as plsc

F32 = jnp.float32
BF16 = jnp.bfloat16
MESH = pl.DeviceIdType.MESH

RMS_EPS = 1e-6
N_DEV = 8
N_HEADS = 8
HEAD_DIM = 64
HEAD_PAIR = 2 * HEAD_DIM
POOL_WINDOWS = (2, 4, 8, 16)
POOL_GROUP = 128
POOL_WIDTH = 512
SB_WIDTH = 512
FF_SHARD = 352
FF_SHARD_PAD = 384
ATTN_K_BLOCK = 256
ATTN_Q_BLOCK_FWD = 512
ATTN_Q_BLOCK_BWD = 256
ATTN_SCALE = 0.125

ADAM_LR = 0.001
ADAM_B1 = 0.9
ADAM_B2 = 0.999
ADAM_EPS = 1e-08
ADAM_WD = 0.01
ADAM_STEP = 10

VMEM_LIMIT = 48 << 20
WGRAD_TOKENS = 2048


def _params(dims=None):
    return pltpu.CompilerParams(dimension_semantics=dims, vmem_limit_bytes=VMEM_LIMIT)


def _mm(a, b):
    return jnp.dot(a, b, preferred_element_type=F32)


def _mm_nt(a, b):
    return lax.dot_general(a, b, (((1,), (1,)), ((), ())), preferred_element_type=F32)


def _mm_tn(a, b):
    return lax.dot_general(a, b, (((0,), (0,)), ((), ())), preferred_element_type=F32)


def _row_tile(rows, cols):
    limit = max(8, (512 * 1024) // cols)
    return max(t for t in range(8, rows + 1, 8) if rows % t == 0 and (t <= limit or t == 8))


def _rstd(xf):
    return lax.rsqrt(jnp.mean(xf * xf, axis=-1, keepdims=True) + RMS_EPS)


def _rms_bwd(xf, gain, dn):
    r = _rstd(xf)
    xh = xf * r
    dgain = jnp.sum(dn * xh, axis=0, keepdims=True)
    dxh = dn * gain
    dx = r * (dxh - xh * jnp.mean(dxh * xh, axis=-1, keepdims=True))
    return dx, dgain


def _ffn_up(x, gain, wgu, *, tm, name):
    T, D = x.shape
    tm = min(tm, T)
    nb, bw = wgu.shape[0] // 2, wgu.shape[1]

    def body(x_ref, gain_ref, wg_ref, wu_ref, gu_ref, hid_ref, n_scr):
        @pl.when(pl.program_id(1) == 0)
        def _():
            xf = x_ref[...]
            n_scr[...] = (xf * _rstd(xf) * gain_ref[...]).astype(BF16)

        halves = (pl.ds(0, tm // 2), pl.ds(tm // 2, tm // 2))
        wg, wu = wg_ref[...], wu_ref[...]
        gus = [(_mm_nt(n_scr[rows, :], wg), _mm_nt(n_scr[rows, :], wu)) for rows in halves]
        for rows, (g, u) in zip(halves, gus):
            gu_ref[0, rows, :] = g.astype(BF16)
            gu_ref[1, rows, :] = u.astype(BF16)
            hid_ref[rows, :] = (g * jax.nn.sigmoid(g) * u).astype(BF16)

    return pl.pallas_call(
        body, name=name, grid=(T // tm, nb),
        in_specs=[
            pl.BlockSpec((tm, D), lambda i, j: (i, 0)),
            pl.BlockSpec((1, D), lambda i, j: (0, 0)),
            pl.BlockSpec((None, bw, D), lambda i, j: (j, 0, 0)),
            pl.BlockSpec((None, bw, D), lambda i, j: (j + nb, 0, 0)),
        ],
        out_specs=[
            pl.BlockSpec((2, tm, bw), lambda i, j: (0, i, j)),
            pl.BlockSpec((tm, bw), lambda i, j: (i, j)),
        ],
        out_shape=[jax.ShapeDtypeStruct((2, T, nb * bw), BF16), jax.ShapeDtypeStruct((T, nb * bw), BF16)],
        scratch_shapes=[pltpu.VMEM((tm, D), BF16)],
        compiler_params=_params(("arbitrary", "arbitrary")),
    )(x, gain, wgu, wgu)


def _ffn_down(x, hid, wd, *, tm, name):
    T, D = x.shape
    tm = min(tm, T)
    F = hid.shape[1]

    def body(x_ref, hid_ref, wd_ref, h_ref):
        h_ref[...] = x_ref[...] + 0.5 * _mm(hid_ref[...], wd_ref[...])

    return pl.pallas_call(
        body, name=name, grid=(T // tm,),
        in_specs=[
            pl.BlockSpec((tm, D), lambda i: (i, 0)),
            pl.BlockSpec((tm, F), lambda i: (i, 0)),
            pl.BlockSpec((F, D), lambda i: (0, 0)),
        ],
        out_specs=pl.BlockSpec((tm, D), lambda i: (i, 0)),
        out_shape=jax.ShapeDtypeStruct((T, D), F32),
        compiler_params=_params(("arbitrary",)),
    )(x, hid, wd)


AFTER = pl.BlockSpec(memory_space=pltpu.HBM)


def _in_hbm(token):
    return pltpu.with_memory_space_constraint(token, pltpu.HBM)


def _ffn_bwd(dh, df, x, gain, gu, wgu, wd, after, *, tm, name):
    T, D = x.shape
    tm = min(tm, T)
    nb, bw = wgu.shape[0] // 2, wgu.shape[1]

    def body(dh_ref, df_ref, x_ref, gain_ref, gu_ref, wg_ref, wu_ref, wd_ref, after_ref,
             dx_ref, dgain_ref, n_ref, dgu_ref, dn_acc):
        i, j = pl.program_id(0), pl.program_id(1)

        @pl.when(j == 0)
        def _():
            xf = x_ref[...]
            n_ref[...] = (xf * _rstd(xf) * gain_ref[...]).astype(BF16)
            dn_acc[...] = jnp.zeros_like(dn_acc)

        @pl.when((i == 0) & (j == 0))
        def _():
            dgain_ref[...] = jnp.zeros_like(dgain_ref)

        halves = (pl.ds(0, tm // 2), pl.ds(tm // 2, tm // 2))
        wd, wg, wu = wd_ref[...], wg_ref[...], wu_ref[...]
        dhids = [_mm_nt(df_ref[rows, :], wd) for rows in halves]
        for rows, dhid in zip(halves, dhids):
            g = gu_ref[0, rows, :].astype(F32)
            u = gu_ref[1, rows, :].astype(F32)
            s = jax.nn.sigmoid(g)
            silu = g * s
            dg = (dhid * u * (s * (1.0 + g * (1.0 - s)))).astype(BF16)
            du = (dhid * silu).astype(BF16)
            dgu_ref[0, rows, :] = dg
            dgu_ref[1, rows, :] = du
            dn_acc[rows, :] += _mm(dg, wg) + _mm(du, wu)

        @pl.when(j == nb - 1)
        def _():
            dx, dgain = _rms_bwd(x_ref[...], gain_ref[...], dn_acc[...])
            dx_ref[...] = dh_ref[...] + dx
            dgain_ref[...] += dgain

    row = lambda i, j: (i, 0)
    return pl.pallas_call(
        body, name=name, grid=(T // tm, nb),
        in_specs=[
            pl.BlockSpec((tm, D), row),
            pl.BlockSpec((tm, D), row),
            pl.BlockSpec((tm, D), row),
            pl.BlockSpec((1, D), lambda i, j: (0, 0)),
            pl.BlockSpec((2, tm, bw), lambda i, j: (0, i, j)),
            pl.BlockSpec((None, bw, D), lambda i, j: (j, 0, 0)),
            pl.BlockSpec((None, bw, D), lambda i, j: (j + nb, 0, 0)),
            pl.BlockSpec((bw, D), lambda i, j: (j, 0)),
            AFTER,
        ],
        out_specs=[
            pl.BlockSpec((tm, D), row),
            pl.BlockSpec((1, D), lambda i, j: (0, 0)),
            pl.BlockSpec((tm, D), row),
            pl.BlockSpec((2, tm, bw), lambda i, j: (0, i, j)),
        ],
        out_shape=[
            jax.ShapeDtypeStruct((T, D), F32),
            jax.ShapeDtypeStruct((1, D), F32),
            jax.ShapeDtypeStruct((T, D), BF16),
            jax.ShapeDtypeStruct((2, T, nb * bw), BF16),
        ],
        scratch_shapes=[pltpu.VMEM((tm, D), F32)],
        compiler_params=_params(("arbitrary", "arbitrary")),
    )(dh, df, x, gain, gu, wgu, wgu, wd, _in_hbm(after))


def _wgrad(a, b, *, grid, a_spec, b_spec, out_spec, out_shape, acc_shape, name):
    nk = grid[2]

    def body(a_ref, b_ref, o_ref, acc):
        k = pl.program_id(2)

        @pl.when(k == 0)
        def _():
            acc[...] = jnp.zeros_like(acc)

        acc[...] += _mm_tn(a_ref[...].astype(BF16), b_ref[...].astype(BF16))

        @pl.when(k == nk - 1)
        def _():
            o_ref[...] = acc[...].astype(o_ref.dtype)

    return pl.pallas_call(
        body, name=name, grid=grid, in_specs=[a_spec, b_spec], out_specs=out_spec,
        out_shape=jax.ShapeDtypeStruct(out_shape, BF16),
        scratch_shapes=[pltpu.VMEM(acc_shape, F32)],
        compiler_params=_params(("arbitrary", "arbitrary", "arbitrary")),
    )(a, b)


def _wgrad_gate_up(n, dgu, *, tk, name, part=0, parts=1):
    T, D = n.shape
    tk = min(tk, T)
    owner_rows = FF_SHARD_PAD * 2
    nb = dgu.shape[2] // owner_rows
    bw = owner_rows // parts
    return _wgrad(
        dgu, n, grid=(2 * nb, 1, T // tk), name=name,
        a_spec=pl.BlockSpec((None, tk, bw), lambda m, c, k: (m // nb, k, parts * (m % nb) + part)),
        b_spec=pl.BlockSpec((tk, D), lambda m, c, k: (k, 0)),
        out_spec=pl.BlockSpec((None, bw, D), lambda m, c, k: (m, 0, 0)),
        out_shape=(2 * nb, bw, D), acc_shape=(bw, D))


def _wgrad_down(hid, df, *, tk, name):
    T, D = df.shape
    tk = min(tk, T)
    bw = FF_SHARD_PAD * 2
    nb = hid.shape[1] // bw
    return _wgrad(
        hid, df, grid=(nb, 1, T // tk), name=name,
        a_spec=pl.BlockSpec((tk, bw), lambda m, c, k: (k, m)),
        b_spec=pl.BlockSpec((tk, D), lambda m, c, k: (k, 0)),
        out_spec=pl.BlockSpec((bw, D), lambda m, c, k: (m, 0)),
        out_shape=(nb * bw, D), acc_shape=(bw, D))


def _wgrad_in(dparts, un, *, name):
    T, D = un.shape
    bw = sum(p.shape[1] for p in dparts) // N_DEV
    first = [sum(p.shape[1] for p in dparts[:i]) // bw for i in range(len(dparts) + 1)]

    def body(*refs):
        dp_refs, un_ref, o_ref = refs[:-2], refs[-2], refs[-1]
        m = pl.program_id(0)
        for dp_ref, lo, hi in zip(dp_refs, first[:-1], first[1:]):
            @pl.when((m >= lo) & (m < hi))
            def _():
                o_ref[...] = _mm_tn(dp_ref[...], un_ref[...]).astype(o_ref.dtype)

    def piece_spec(lo, hi):
        return pl.BlockSpec((T, bw), lambda m: (0, jnp.clip(m - lo, 0, hi - lo - 1)))

    return pl.pallas_call(
        body, name=name, grid=(N_DEV,),
        in_specs=[piece_spec(lo, hi) for lo, hi in zip(first[:-1], first[1:])] + [pl.BlockSpec((T, D), lambda m: (0, 0))],
        out_specs=pl.BlockSpec((None, bw, D), lambda m: (m, 0, 0)),
        out_shape=jax.ShapeDtypeStruct((N_DEV, bw, D), BF16),
        compiler_params=_params(("arbitrary",)),
    )(*dparts, un)


def _wgrad_full(a, b, *, tk, name):
    T, M = a.shape
    tk = min(tk, T)
    N = b.shape[1]
    return _wgrad(
        a, b, grid=(1, 1, T // tk), name=name,
        a_spec=pl.BlockSpec((tk, M), lambda m, c, k: (k, 0)),
        b_spec=pl.BlockSpec((tk, N), lambda m, c, k: (k, 0)),
        out_spec=pl.BlockSpec((M, N), lambda m, c, k: (0, 0)), out_shape=(M, N), acc_shape=(M, N))


def _loss_bwd(h, target, gain, *, tm, name):
    T, D = h.shape
    tm = min(tm, T)

    def body(h_ref, t_ref, gain_ref, dh_ref, df_ref, loss_ref, dgain_ref):
        @pl.when(pl.program_id(0) == 0)
        def _():
            loss_ref[...] = jnp.zeros_like(loss_ref)
            dgain_ref[...] = jnp.zeros_like(dgain_ref)

        xf = h_ref[...]
        gain = gain_ref[...]
        err = xf * _rstd(xf) * gain - t_ref[...]
        loss_ref[...] += 0.5 * jnp.sum(jnp.mean(err * err, axis=-1, keepdims=True), axis=0, keepdims=True)
        dx, dgain = _rms_bwd(xf, gain, err * (1.0 / D))
        dh_ref[...] = dx
        df_ref[...] = (0.5 * dx).astype(BF16)
        dgain_ref[...] += dgain

    row = lambda i: (i, 0)
    fixed = lambda i: (0, 0)
    return pl.pallas_call(
        body, name=name, grid=(T // tm,),
        in_specs=[pl.BlockSpec((tm, D), row), pl.BlockSpec((tm, D), row), pl.BlockSpec((1, D), fixed)],
        out_specs=[pl.BlockSpec((tm, D), row), pl.BlockSpec((tm, D), row), pl.BlockSpec((1, 128), fixed),
                   pl.BlockSpec((1, D), fixed)],
        out_shape=[jax.ShapeDtypeStruct((T, D), F32), jax.ShapeDtypeStruct((T, D), BF16),
                   jax.ShapeDtypeStruct((1, 128), F32), jax.ShapeDtypeStruct((1, D), F32)],
        compiler_params=_params(("arbitrary",)),
    )(h, target, gain)


def _inproj_fwd(h, gain, w_in_t, *, tm, name):
    T, D = h.shape
    tm = min(tm, T)
    bn = D
    nb = w_in_t.shape[0] // bn

    def body(h_ref, gain_ref, wt_ref, un_ref, proj_ref):
        @pl.when(pl.program_id(1) == 0)
        def _():
            xf = h_ref[...]
            un_ref[...] = (xf * _rstd(xf) * gain_ref[...]).astype(BF16)

        proj_ref[...] = _mm_nt(un_ref[...], wt_ref[...])

    return pl.pallas_call(
        body, name=name, grid=(T // tm, nb),
        in_specs=[
            pl.BlockSpec((tm, D), lambda i, j: (i, 0)),
            pl.BlockSpec((1, D), lambda i, j: (0, 0)),
            pl.BlockSpec((bn, D), lambda i, j: (j, 0)),
        ],
        out_specs=[pl.BlockSpec((tm, D), lambda i, j: (i, 0)), pl.BlockSpec((tm, bn), lambda i, j: (i, j))],
        out_shape=[jax.ShapeDtypeStruct((T, D), BF16), jax.ShapeDtypeStruct((T, nb * bn), F32)],
        compiler_params=_params(("arbitrary", "arbitrary")),
    )(h, gain, w_in_t)


def _inproj_bwd(dparts, dh, h, gain, w_in_t, *, tm, name):
    T, D = h.shape
    tm = min(tm, T)
    n = len(dparts)
    widths = [p.shape[1] for p in dparts]
    starts = [sum(widths[:i]) for i in range(n)]

    def body(*refs):
        dp_refs = refs[:n]
        dh_ref, h_ref, gain_ref, wt_ref, dx_ref, df_ref, dgain_ref = refs[n:]

        @pl.when(pl.program_id(0) == 0)
        def _():
            dgain_ref[...] = jnp.zeros_like(dgain_ref)

        dn = sum(_mm(dp_ref[...], wt_ref[start:start + width, :])
                 for dp_ref, start, width in zip(dp_refs, starts, widths))
        dx, dgain = _rms_bwd(h_ref[...], gain_ref[...], dn)
        dh_in = dh_ref[...] + dx
        dx_ref[...] = dh_in
        df_ref[...] = (0.5 * dh_in).astype(BF16)
        dgain_ref[...] += dgain

    row = lambda i: (i, 0)
    fixed = lambda i: (0, 0)
    return pl.pallas_call(
        body, name=name, grid=(T // tm,),
        in_specs=[pl.BlockSpec((tm, width), row) for width in widths] + [
            pl.BlockSpec((tm, D), row),
            pl.BlockSpec((tm, D), row),
            pl.BlockSpec((1, D), fixed),
            pl.BlockSpec(w_in_t.shape, fixed),
        ],
        out_specs=[pl.BlockSpec((tm, D), row), pl.BlockSpec((tm, D), row), pl.BlockSpec((1, D), fixed)],
        out_shape=[jax.ShapeDtypeStruct((T, D), F32), jax.ShapeDtypeStruct((T, D), BF16),
                   jax.ShapeDtypeStruct((1, D), F32)],
        compiler_params=_params(("arbitrary",)),
    )(*dparts, dh, h, gain, w_in_t)


def _window_sum(x, row, doublings, *, backward):
    T = x.shape[0]
    s = x
    for k in range(doublings):
        sh = 1 << k
        if backward:
            s = s + jnp.where(row < T - sh, pltpu.roll(s, T - sh, 0), 0.0)
        else:
            s = s + jnp.where(row >= sh, pltpu.roll(s, sh, 0), 0.0)
    return s


def _pool_fwd(proj, w_group, scale, *, name):
    T = proj.shape[0]

    def body(xp_ref, w_ref, scale_ref, p_ref):
        row = lax.broadcasted_iota(jnp.int32, (T, POOL_GROUP), 0)
        for gi, window in enumerate(POOL_WINDOWS):
            cols = slice(gi * POOL_GROUP, (gi + 1) * POOL_GROUP)
            x = xp_ref[:, cols]
            inv_count = 1.0 / jnp.minimum(row + 1, window).astype(F32)
            yc = _window_sum(x, row, gi + 1, backward=False) * inv_count - x
            pre = _mm(yc.astype(BF16), w_ref[gi].astype(BF16))
            p_ref[:, cols] = pre * scale_ref[:, cols]

    return pl.pallas_call(
        body, name=name, grid=(1,),
        in_specs=[
            pl.BlockSpec((T, POOL_WIDTH), lambda i: (0, 0)),
            pl.BlockSpec(w_group.shape, lambda i: (0, 0, 0)),
            pl.BlockSpec((1, POOL_WIDTH), lambda i: (0, 0)),
        ],
        out_specs=pl.BlockSpec((T, POOL_WIDTH), lambda i: (0, 0)),
        out_shape=jax.ShapeDtypeStruct((T, POOL_WIDTH), F32),
        compiler_params=_params(("arbitrary",)),
    )(proj, w_group, scale)


def _pool_bwd(dp, proj, w_group, scale, *, name):
    T = proj.shape[0]

    def body(dp_ref, xp_ref, w_ref, scale_ref, dxp_ref, dw_ref, dscale_ref):
        row = lax.broadcasted_iota(jnp.int32, (T, POOL_GROUP), 0)
        for gi, window in enumerate(POOL_WINDOWS):
            cols = slice(gi * POOL_GROUP, (gi + 1) * POOL_GROUP)
            x = xp_ref[:, cols]
            inv_count = 1.0 / jnp.minimum(row + 1, window).astype(F32)
            yc = (_window_sum(x, row, gi + 1, backward=False) * inv_count - x).astype(BF16)
            w = w_ref[gi].astype(BF16)
            pre = _mm(yc, w)
            dpg = dp_ref[:, cols]
            dscale_ref[:, cols] = jnp.sum(dpg * pre, axis=0, keepdims=True)
            dpre = (dpg * scale_ref[:, cols]).astype(BF16)
            dw_ref[gi] = _mm_tn(yc, dpre)
            dyc = _mm_nt(dpre, w)
            dxp_ref[:, cols] = (_window_sum(dyc * inv_count, row, gi + 1, backward=True) - dyc).astype(BF16)

    return pl.pallas_call(
        body, name=name, grid=(1,),
        in_specs=[
            pl.BlockSpec((T, POOL_WIDTH), lambda i: (0, 0)),
            pl.BlockSpec((T, POOL_WIDTH), lambda i: (0, 0)),
            pl.BlockSpec(w_group.shape, lambda i: (0, 0, 0)),
            pl.BlockSpec((1, POOL_WIDTH), lambda i: (0, 0)),
        ],
        out_specs=[
            pl.BlockSpec((T, POOL_WIDTH), lambda i: (0, 0)),
            pl.BlockSpec(w_group.shape, lambda i: (0, 0, 0)),
            pl.BlockSpec((1, POOL_WIDTH), lambda i: (0, 0)),
        ],
        out_shape=[jax.ShapeDtypeStruct((T, POOL_WIDTH), BF16), jax.ShapeDtypeStruct(w_group.shape, F32),
                   jax.ShapeDtypeStruct((1, POOL_WIDTH), F32)],
        compiler_params=_params(("arbitrary",)),
    )(dp, proj, w_group, scale)


ATTN_STRIP = 32


def _log_sigmoids(z):
    lb = jnp.minimum(z, 0.0) - jnp.log(1.0 + jnp.exp(-jnp.abs(z)))
    return lb, lb - z


def _transposed_blocks(x_ref, blocks_scr, tq):
    for b in range(blocks_scr.shape[0]):
        blocks_scr[b] = x_ref[b * tq:(b + 1) * tq, :].T.astype(BF16)


def _split_bf16(x):
    hi = x.astype(BF16)
    return hi, (x - hi.astype(F32)).astype(BF16)


def _strips(n):
    return [slice(i, i + ATTN_STRIP) for i in range(0, n, ATTN_STRIP)]


def _rows(parts):
    return jnp.concatenate(parts, axis=0)


def _attn_specs(T, tq):
    q_col = POOL_WIDTH // HEAD_PAIR
    k_col = q_col + SB_WIDTH // HEAD_PAIR
    v_col = k_col + SB_WIDTH // HEAD_PAIR
    return [
        pl.BlockSpec((tq, HEAD_PAIR), lambda p, i: (i, q_col + p)),
        pl.BlockSpec((T, HEAD_PAIR), lambda p, i: (0, k_col + p)),
        pl.BlockSpec((T, HEAD_PAIR), lambda p, i: (0, v_col + p)),
    ]


def _attn_fwd(proj, *, name):
    T = proj.shape[0]
    tk = min(ATTN_K_BLOCK, T)
    tq = min(ATTN_Q_BLOCK_FWD, T)
    diagonal_blocks = tq // tk

    def body(q_ref, k_ref, v_ref, o_ref, lt_ref, kt_scr, vb_scr):
        qi = pl.program_id(1)

        @pl.when(qi == 0)
        def _():
            _transposed_blocks(k_ref, kt_scr, tk)
            vb_scr[...] = v_ref[...].astype(BF16)

        head0 = lax.broadcasted_iota(jnp.int32, (tq, HEAD_PAIR), 1) < HEAD_DIM
        q = q_ref[...] * ATTN_SCALE
        qs = (jnp.where(head0, q, 0.0).astype(BF16), jnp.where(head0, 0.0, q).astype(BF16))
        r = lax.broadcasted_iota(jnp.int32, (tq, tk), 0)
        c = lax.broadcasted_iota(jnp.int32, (tq, tk), 1)
        later = (r[:tk] > c[:tk]).astype(BF16)
        later2 = _rows([later, later])
        causal = lambda d: (lambda rows: c[rows] + d * tk < r[rows])
        strips = _strips(tq)

        def log_terms(z, valid):
            lbs, his, los, sums = [], [], [], []
            for rows in strips:
                lb, lm = _log_sigmoids(z[rows])
                if valid is not None:
                    lm = jnp.where(valid(rows), lm, 0.0)
                hi, lo = _split_bf16(lm)
                lbs.append(lb)
                his.append(hi)
                los.append(lo)
                sums.append(jnp.sum(lm, axis=1, keepdims=True))
            return lbs, jnp.concatenate([_rows(his), _rows(los)], axis=1), _rows(sums)

        def weights(lbs, run, after, valid):
            parts = []
            for rows, lb in zip(strips, lbs):
                a = jnp.exp(lb + run[rows] + after[rows])
                if valid is not None:
                    a = jnp.where(valid(rows), a, 0.0)
                parts.append(a.astype(BF16))
            return _rows(parts)

        def block(kj, carry, valid):
            kt = kt_scr[kj]
            vb = vb_scr[pl.ds(pl.multiple_of(kj * tk, tk), tk), :]
            run0, o0, run1, o1 = carry
            z0 = _mm(qs[0], kt)
            z1 = _mm(qs[1], kt)
            lbs0, split0, sums0 = log_terms(z0, valid)
            after0 = _mm(split0, later2)
            lbs1, split1, sums1 = log_terms(z1, valid)
            after1 = _mm(split1, later2)
            o0 = o0 + _mm(weights(lbs0, run0, after0, valid), vb)
            o1 = o1 + _mm(weights(lbs1, run1, after1, valid), vb)
            return run0 + sums0, o0, run1 + sums1, o1

        zero = (jnp.zeros((tq, 1), F32), jnp.zeros((tq, HEAD_PAIR), F32))
        first = diagonal_blocks * qi
        carry = zero + zero
        for d in reversed(range(diagonal_blocks)):
            carry = block(first + d, carry, causal(d))
        carry = lax.fori_loop(0, first, lambda it, cr: block(first - 1 - it, cr, None), carry)
        o_ref[...] = jnp.where(head0, carry[1], carry[3])
        lt_ref[...] = jnp.where(head0, carry[0], carry[2])

    out_spec = pl.BlockSpec((tq, HEAD_PAIR), lambda p, i: (i, p))
    return pl.pallas_call(
        body, name=name, grid=(N_HEADS // 2, T // tq),
        in_specs=_attn_specs(T, tq), out_specs=[out_spec, out_spec],
        out_shape=[jax.ShapeDtypeStruct((T, SB_WIDTH), F32), jax.ShapeDtypeStruct((T, SB_WIDTH), F32)],
        scratch_shapes=[pltpu.VMEM((T // tk, HEAD_PAIR, tk), BF16), pltpu.VMEM((T, HEAD_PAIR), BF16)],
        compiler_params=_params(("arbitrary", "arbitrary")),
    )(proj, proj, proj)


def _attn_bwd(proj, do, ltot, after, *, name):
    T = proj.shape[0]
    tk = min(ATTN_K_BLOCK, T)
    tq = min(ATTN_Q_BLOCK_BWD, T)
    diagonal_blocks = tq // tk

    def body(q_ref, k_ref, v_ref, do_ref, lt_ref, after_ref, dq_ref, dk_ref, dv_ref,
             kb_scr, kt_scr, vt_scr, dkt_ref, dvt_ref):
        qi = pl.program_id(1)

        @pl.when(qi == 0)
        def _():
            kb_scr[...] = k_ref[...].astype(BF16)
            _transposed_blocks(k_ref, kt_scr, tk)
            _transposed_blocks(v_ref, vt_scr, tk)
            dkt_ref[...] = jnp.zeros_like(dkt_ref)
            dvt_ref[...] = jnp.zeros_like(dvt_ref)

        head0 = lax.broadcasted_iota(jnp.int32, (tq, HEAD_PAIR), 1) < HEAD_DIM
        q, do_, lt = q_ref[...] * ATTN_SCALE, do_ref[...], lt_ref[...]
        qs = (jnp.where(head0, q, 0.0).astype(BF16), jnp.where(head0, 0.0, q).astype(BF16))
        q_heads = (jnp.where(head0, q, 0.0), jnp.where(head0, 0.0, q))
        do_heads = (jnp.where(head0, do_, 0.0), jnp.where(head0, 0.0, do_))
        dos = tuple(d.astype(BF16) for d in do_heads)
        qts = tuple(x.T.astype(BF16) for x in q_heads)
        dots = tuple(d.T.astype(BF16) for d in do_heads)
        lts = (jnp.max(jnp.where(head0, lt, -jnp.inf), axis=1, keepdims=True),
               jnp.max(jnp.where(head0, -jnp.inf, lt), axis=1, keepdims=True))
        r = lax.broadcasted_iota(jnp.int32, (tq, tk), 0)
        c = lax.broadcasted_iota(jnp.int32, (tq, tk), 1)
        upto = (r[:tk] <= c[:tk]).astype(BF16)
        before = (r[:tk] < c[:tk]).astype(BF16)
        upto2, before2 = _rows([upto, upto]), _rows([before, before])
        causal = lambda d: (lambda rows: c[rows] + d * tk < r[rows])
        strips = _strips(tq)

        def log_terms(z, valid):
            lbs, his, los, sums = [], [], [], []
            for rows in strips:
                lb, lm = _log_sigmoids(z[rows])
                if valid is not None:
                    lm = jnp.where(valid(rows), lm, 0.0)
                hi, lo = _split_bf16(lm)
                lbs.append(lb)
                his.append(hi)
                los.append(lo)
                sums.append(jnp.sum(lm, axis=1, keepdims=True))
            return lbs, jnp.concatenate([_rows(his), _rows(los)], axis=1), _rows(sums)

        def weights(lbs, rest, lm_upto, da, valid):
            a_parts, es, his, los, sums = [], [], [], [], []
            for rows, lb in zip(strips, lbs):
                a = jnp.exp(lb + (rest[rows] - lm_upto[rows]))
                if valid is not None:
                    a = jnp.where(valid(rows), a, 0.0)
                e = da[rows] * a
                hi, lo = _split_bf16(e)
                a_parts.append(a.astype(BF16))
                es.append(e)
                his.append(hi)
                los.append(lo)
                sums.append(jnp.sum(e, axis=1, keepdims=True))
            return _rows(a_parts), es, jnp.concatenate([_rows(his), _rows(los)], axis=1), _rows(sums)

        def score_grads(lbs, es, run_e, e_before, valid):
            parts = []
            for rows, lb, e in zip(strips, lbs, es):
                beta = jnp.exp(lb)
                dz = e * (1.0 - beta) - (run_e[rows] + e_before[rows]) * beta
                if valid is not None:
                    dz = jnp.where(valid(rows), dz, 0.0)
                parts.append(dz.astype(BF16))
            return _rows(parts)

        def block(kj, carry, valid):
            off = pl.multiple_of(kj * tk, tk)
            kb, kt, vt = kb_scr[pl.ds(off, tk), :], kt_scr[kj], vt_scr[kj]
            run_lm0, run_e0, dq0, run_lm1, run_e1, dq1 = carry
            z0, da0 = _mm(qs[0], kt), _mm(dos[0], vt)
            z1, da1 = _mm(qs[1], kt), _mm(dos[1], vt)
            lbs0, split0, lm_sums0 = log_terms(z0, valid)
            lm_upto0 = _mm(split0, upto2)
            lbs1, split1, lm_sums1 = log_terms(z1, valid)
            lm_upto1 = _mm(split1, upto2)
            a0, es0, split0, e_sums0 = weights(lbs0, lts[0] - run_lm0, lm_upto0, da0, valid)
            e_before0 = _mm(split0, before2)
            a1, es1, split1, e_sums1 = weights(lbs1, lts[1] - run_lm1, lm_upto1, da1, valid)
            e_before1 = _mm(split1, before2)
            dz0 = score_grads(lbs0, es0, run_e0, e_before0, valid)
            dkt_blk = _mm(qts[0], dz0)
            dvt_blk = _mm(dots[0], a0)
            dq0 = dq0 + _mm(dz0, kb)
            dz1 = score_grads(lbs1, es1, run_e1, e_before1, valid)
            dkt_ref[kj] += dkt_blk + _mm(qts[1], dz1)
            dvt_ref[kj] += dvt_blk + _mm(dots[1], a1)
            dq1 = dq1 + _mm(dz1, kb)
            return run_lm0 + lm_sums0, run_e0 + e_sums0, dq0, run_lm1 + lm_sums1, run_e1 + e_sums1, dq1

        zero = (jnp.zeros((tq, 1), F32), jnp.zeros((tq, 1), F32), jnp.zeros((tq, HEAD_PAIR), F32))
        first = diagonal_blocks * qi
        carry = lax.fori_loop(0, first, lambda kj, cr: block(kj, cr, None), zero + zero)
        for d in range(diagonal_blocks):
            carry = block(first + d, carry, causal(d))
        dq_ref[...] = (jnp.where(head0, carry[2], carry[5]) * ATTN_SCALE).astype(BF16)

        @pl.when(qi == T // tq - 1)
        def _():
            for b in range(T // tk):
                dk_ref[b * tk:(b + 1) * tk, :] = dkt_ref[b].T.astype(BF16)
                dv_ref[b * tk:(b + 1) * tk, :] = dvt_ref[b].T.astype(BF16)

    blk = pl.BlockSpec((tq, HEAD_PAIR), lambda p, i: (i, p))
    seq = pl.BlockSpec((T, HEAD_PAIR), lambda p, i: (0, p))
    transposed = pltpu.VMEM((T // tk, HEAD_PAIR, tk), F32)
    return pl.pallas_call(
        body, name=name, grid=(N_HEADS // 2, T // tq),
        in_specs=_attn_specs(T, tq) + [blk, blk, AFTER], out_specs=[blk, seq, seq],
        out_shape=[jax.ShapeDtypeStruct((T, SB_WIDTH), BF16)] * 3,
        scratch_shapes=[pltpu.VMEM((T, HEAD_PAIR), BF16), pltpu.VMEM((T // tk, HEAD_PAIR, tk), BF16),
                        pltpu.VMEM((T // tk, HEAD_PAIR, tk), BF16), transposed, transposed],
        compiler_params=_params(("arbitrary", "arbitrary")),
    )(proj, proj, proj, do, ltot, _in_hbm(after))


def _mix_specs(T, D, tm, wbp, w_out):
    gate_col = (POOL_WIDTH + 3 * SB_WIDTH) // D
    row = lambda i: (i, 0)
    return [
        pl.BlockSpec((tm, D), row),
        pl.BlockSpec((tm, POOL_WIDTH), row),
        pl.BlockSpec((tm, SB_WIDTH), row),
        pl.BlockSpec((tm, D), lambda i: (i, gate_col)),
        pl.BlockSpec((tm, D), lambda i: (i, gate_col + 1)),
        pl.BlockSpec(wbp.shape, lambda i: (0, 0)),
        pl.BlockSpec(wbp.shape, lambda i: (0, 0)),
        pl.BlockSpec(w_out.shape, lambda i: (0, 0)),
    ]


def _mix_fwd(h, p, o, proj, wbp, wba, w_out, *, tm, name):
    T, D = h.shape
    tm = min(tm, T)

    def body(h_ref, p_ref, o_ref, glp_ref, gls_ref, wbp_ref, wba_ref, wout_ref, hout_ref, m_ref):
        halves = (pl.ds(0, tm // 2), pl.ds(tm // 2, tm // 2))
        wbp, wba, wout = wbp_ref[...], wba_ref[...], wout_ref[...]
        branches = [(_mm_nt(p_ref[rows, :].astype(BF16), wbp), _mm_nt(o_ref[rows, :].astype(BF16), wba))
                    for rows in halves]
        for rows, (yp, ys) in zip(halves, branches):
            m = (jax.nn.sigmoid(glp_ref[rows, :]) * yp + jax.nn.sigmoid(gls_ref[rows, :]) * ys).astype(BF16)
            m_ref[rows, :] = m
            hout_ref[rows, :] = h_ref[rows, :] + _mm(m, wout)

    row = lambda i: (i, 0)
    return pl.pallas_call(
        body, name=name, grid=(T // tm,),
        in_specs=_mix_specs(T, D, tm, wbp, w_out),
        out_specs=[pl.BlockSpec((tm, D), row), pl.BlockSpec((tm, D), row)],
        out_shape=[jax.ShapeDtypeStruct((T, D), F32), jax.ShapeDtypeStruct((T, D), BF16)],
        compiler_params=_params(("arbitrary",)),
    )(h, p, o, proj, proj, wbp, wba, w_out)


def _mix_bwd(dh, p, o, proj, wbp, wba, w_out, after, *, tm, name):
    T, D = dh.shape
    tm = min(tm, T)

    def body(dh_ref, p_ref, o_ref, glp_ref, gls_ref, wbp_ref, wba_ref, wout_ref, after_ref,
             dyp_ref, dys_ref, dp_ref, do_ref, dgl_ref):
        halves = (pl.ds(0, tm // 2), pl.ds(tm // 2, tm // 2))
        wbp, wba, wout = wbp_ref[...], wba_ref[...], wout_ref[...]
        products = [(_mm_nt(dh_ref[rows, :].astype(BF16), wout), _mm_nt(p_ref[rows, :].astype(BF16), wbp),
                     _mm_nt(o_ref[rows, :].astype(BF16), wba)) for rows in halves]
        for rows, (dm, yp, ys) in zip(halves, products):
            gp = jax.nn.sigmoid(glp_ref[rows, :])
            gs = jax.nn.sigmoid(gls_ref[rows, :])
            dyp = (dm * gp).astype(BF16)
            dys = (dm * gs).astype(BF16)
            dyp_ref[rows, :] = dyp
            dys_ref[rows, :] = dys
            dgl_ref[rows, :D] = (dm * yp * gp * (1.0 - gp)).astype(BF16)
            dgl_ref[rows, D:] = (dm * ys * gs * (1.0 - gs)).astype(BF16)
            dp_ref[rows, :] = _mm(dyp, wbp)
            do_ref[rows, :] = _mm(dys, wba)

    row = lambda i: (i, 0)
    return pl.pallas_call(
        body, name=name, grid=(T // tm,),
        in_specs=_mix_specs(T, D, tm, wbp, w_out) + [AFTER],
        out_specs=[pl.BlockSpec((tm, D), row), pl.BlockSpec((tm, D), row), pl.BlockSpec((tm, POOL_WIDTH), row),
                   pl.BlockSpec((tm, SB_WIDTH), row), pl.BlockSpec((tm, 2 * D), row)],
        out_shape=[jax.ShapeDtypeStruct((T, D), BF16), jax.ShapeDtypeStruct((T, D), BF16),
                   jax.ShapeDtypeStruct((T, POOL_WIDTH), F32), jax.ShapeDtypeStruct((T, SB_WIDTH), F32),
                   jax.ShapeDtypeStruct((T, 2 * D), BF16)],
        compiler_params=_params(("arbitrary",)),
    )(dh, p, o, proj, proj, wbp, wba, w_out, _in_hbm(after))


def _adamw_update(w, g, m, v):
    m_ = ADAM_B1 * m + (1.0 - ADAM_B1) * g
    v_ = ADAM_B2 * v + (1.0 - ADAM_B2) * (g * g)
    m_hat = m_ / (1.0 - ADAM_B1 ** ADAM_STEP)
    v_hat = v_ / (1.0 - ADAM_B2 ** ADAM_STEP)
    return -ADAM_LR * (m_hat / (jnp.sqrt(v_hat) + ADAM_EPS) + ADAM_WD * w), m_, v_


def _adamw(w, g, m, v, *, name):
    R, C = w.shape
    tr = _row_tile(R, C)

    def body(w_ref, g_ref, m_ref, v_ref, d_ref, nm_ref, nv_ref):
        d_ref[...], nm_ref[...], nv_ref[...] = _adamw_update(w_ref[...], g_ref[...], m_ref[...], v_ref[...])

    spec = pl.BlockSpec((tr, C), lambda i: (i, 0))
    return pl.pallas_call(
        body, name=name, grid=(R // tr,), in_specs=[spec] * 4, out_specs=[spec] * 3,
        out_shape=[jax.ShapeDtypeStruct((R, C), F32)] * 3,
        compiler_params=_params(("arbitrary",)),
    )(w, g, m, v)


def _position():
    return lax.axis_index("x"), lax.axis_index("y"), lax.axis_index("c")


def _all_gather(shards, *, name, collective_id):
    n = len(shards)
    n_copies = 9

    def body(*refs):
        ins, outs = refs[:n], refs[n:2 * n]
        send_sems, recv_sems, local_sems = refs[2 * n:]
        x, y, c = _position()
        me, sibling = (x, y, c), (x, y, 1 - c)
        x_nbr, y_nbr, diagonal = (1 - x, y, c), (x, 1 - y, c), (1 - x, 1 - y, c)
        other = lambda pos: (pos[0], pos[1], 1 - c)

        barrier = pltpu.get_barrier_semaphore()
        for peer in (sibling, x_nbr, y_nbr):
            pl.semaphore_signal(barrier, inc=1, device_id=peer, device_id_type=MESH)
        pl.semaphore_wait(barrier, 3)

        def block(a, pos, half=None):
            ref = outs[a].at[4 * pos[0] + 2 * pos[1] + pos[2]]
            rows = ref.shape[0] // 2
            return ref if half is None else ref.at[pl.ds(half * rows, rows)]

        def copy(a, k, pos, to, half=None, src=None):
            return pltpu.make_async_remote_copy(
                src_ref=block(a, pos, half) if src is None else src, dst_ref=block(a, pos, half),
                send_sem=send_sems.at[n_copies * a + k], recv_sem=recv_sems.at[n_copies * a + k],
                device_id=to, device_id_type=MESH)

        started = []
        for a in range(n):
            mine = pltpu.make_async_copy(ins[a], block(a, me), local_sems.at[a])
            mine.start()
            started.append(mine)
        sends = []
        for a in range(n):
            sends += [copy(a, 1, me, x_nbr, src=ins[a]), copy(a, 2, me, y_nbr, src=ins[a]),
                      copy(a, 0, me, sibling, src=ins[a])]
        for cp in sends:
            cp.start()

        def pass_on(copies):
            for cp in copies:
                cp.start()
                sends.append(cp)

        for a in range(n):
            copy(a, 1, x_nbr, me).wait_recv()
            pass_on([copy(a, 5, x_nbr, y_nbr, half=0), copy(a, 3, x_nbr, sibling)])
            copy(a, 2, y_nbr, me).wait_recv()
            pass_on([copy(a, 6, y_nbr, x_nbr, half=1), copy(a, 4, y_nbr, sibling)])
        for a in range(n):
            copy(a, 5, diagonal, me, half=0).wait_recv()
            pass_on([copy(a, 7, diagonal, sibling, half=0)])
            copy(a, 6, diagonal, me, half=1).wait_recv()
            pass_on([copy(a, 8, diagonal, sibling, half=1)])
        for a in range(n):
            copy(a, 0, sibling, me).wait_recv()
            copy(a, 3, other(x_nbr), me).wait_recv()
            copy(a, 4, other(y_nbr), me).wait_recv()
            copy(a, 7, other(diagonal), me, half=0).wait_recv()
            copy(a, 8, other(diagonal), me, half=1).wait_recv()
        for cp in sends:
            cp.wait_send()
        for cp in started:
            cp.wait()

    return pl.kernel(
        body, name=name,
        out_type=[jax.ShapeDtypeStruct((N_DEV,) + s.shape, s.dtype) for s in shards],
        mesh=plsc.ScalarSubcoreMesh(axis_name="sequencer", num_cores=1),
        scratch_types=[pltpu.SemaphoreType.DMA((n_copies * n,)), pltpu.SemaphoreType.DMA((n_copies * n,)),
                       pltpu.SemaphoreType.DMA((n,))],
        compiler_params=pltpu.CompilerParams(collective_id=collective_id),
    )(*shards)


def _chip_sums(group, *, name):
    n = len(group)
    shapes = [g.shape[1:] for g in group]

    def body(*refs):
        g_refs, partials, out_refs = refs[:n], refs[n:3 * n:2], refs[n + 1:3 * n:2]
        mines, theirs = refs[3 * n:5 * n:2], refs[3 * n + 1:5 * n:2]
        send_sems, recv_sems, local_sems = refs[5 * n:]
        x, y, c = _position()
        my_chip = 2 * x + y

        def swap(a, s):
            return pltpu.make_async_remote_copy(
                src_ref=g_refs[a].at[2 * s + (1 - c)], dst_ref=theirs[a].at[s],
                send_sem=send_sems.at[4 * a + s], recv_sem=recv_sems.at[4 * a + s],
                device_id=(x, y, 1 - c), device_id_type=MESH)

        def load(a, s):
            return pltpu.make_async_copy(g_refs[a].at[2 * s + c], mines[a].at[s], local_sems.at[4 * a + s])

        for a in range(n):
            for s in range(4):
                swap(a, s).start()
                load(a, s).start()

        for a, (R, C) in enumerate(shapes):
            rc = 128 if R % 128 == 0 else R

            def chip_sum(chip, rows):
                return mines[a][chip, rows, :].astype(F32) + theirs[a][chip, rows, :].astype(F32)

            for s in range(4):
                load(a, s).wait()
                swap(a, s).wait_recv()

                @pl.when(s == my_chip)
                def _():
                    @pl.loop(0, R // rc)
                    def _(t):
                        rows = pl.ds(pl.multiple_of(t * rc, rc), rc)
                        out_refs[a][rows, :] = chip_sum(s, rows)

                @pl.when(s != my_chip)
                def _():
                    @pl.loop(0, R // rc)
                    def _(t):
                        rows = pl.ds(pl.multiple_of(t * rc, rc), rc)
                        partials[a][(s ^ my_chip) - 1, rows, :] = chip_sum(s, rows).astype(BF16)

        for a in range(n):
            for s in range(4):
                swap(a, s).wait_send()

    vmem = pl.BlockSpec(memory_space=pltpu.VMEM)
    outs = pl.pallas_call(
        body, name=name,
        in_specs=[pl.BlockSpec(memory_space=pl.ANY)] * n, out_specs=[vmem] * (2 * n),
        out_shape=[shape for R, C in shapes
                   for shape in (jax.ShapeDtypeStruct((3, R, C), BF16), jax.ShapeDtypeStruct((R, C), F32))],
        scratch_shapes=[pltpu.VMEM((4, R, C), BF16) for R, C in shapes for _ in range(2)] + [
            pltpu.SemaphoreType.DMA((4 * n,)), pltpu.SemaphoreType.DMA((4 * n,)), pltpu.SemaphoreType.DMA((4 * n,))],
        compiler_params=_params(),
    )(*group)
    return [(outs[2 * a], outs[2 * a + 1]) for a in range(n)]


def _cross_chips(partials, *, name, collective_id):
    n = len(partials)

    def body(*refs):
        ins, outs = refs[:n], refs[n:2 * n]
        send_sems, recv_sems = refs[2 * n:]
        x, y, c = _position()
        my_chip = 2 * x + y
        peers = [((my_chip ^ j) // 2, (my_chip ^ j) % 2, c) for j in (1, 2, 3)]

        barrier = pltpu.get_barrier_semaphore()
        for peer in peers:
            pl.semaphore_signal(barrier, inc=1, device_id=peer, device_id_type=MESH)
        pl.semaphore_wait(barrier, 3)

        copies = [
            pltpu.make_async_remote_copy(
                src_ref=ins[a].at[j], dst_ref=outs[a].at[j],
                send_sem=send_sems.at[3 * a + j], recv_sem=recv_sems.at[3 * a + j],
                device_id=peers[j], device_id_type=MESH)
            for a in range(n) for j in range(3)]
        for cp in copies:
            cp.start()
        for cp in copies:
            cp.wait_recv()
        for cp in copies:
            cp.wait_send()

    return pl.kernel(
        body, name=name,
        out_type=[jax.ShapeDtypeStruct(p.shape, p.dtype) for p in partials],
        mesh=plsc.ScalarSubcoreMesh(axis_name="sequencer", num_cores=1),
        scratch_types=[pltpu.SemaphoreType.DMA((3 * n,)), pltpu.SemaphoreType.DMA((3 * n,))],
        compiler_params=pltpu.CompilerParams(collective_id=collective_id),
    )(*partials)


def _cross_chips_and_gather(partials, slab, *, name, collective_id):
    n = len(partials)

    def body(*refs):
        part_refs, slab_ref = refs[:n], refs[n]
        landed_refs, slabs_ref = refs[n + 1:2 * n + 1], refs[2 * n + 1]
        send_sems, recv_sems, local_sem = refs[2 * n + 2:]
        x, y, c = _position()
        me, my_chip = 4 * x + 2 * y + c, 2 * x + y
        others = [me ^ k for k in range(1, N_DEV)]
        ids = [(o // 4, (o // 2) % 2, o % 2) for o in others]

        barrier = pltpu.get_barrier_semaphore()
        for peer in ids:
            pl.semaphore_signal(barrier, inc=1, device_id=peer, device_id_type=MESH)
        pl.semaphore_wait(barrier, N_DEV - 1)

        mine = pltpu.make_async_copy(slab_ref, slabs_ref.at[me], local_sem)
        mine.start()
        sends = [
            pltpu.make_async_remote_copy(
                src_ref=part_refs[a].at[j], dst_ref=landed_refs[a].at[j],
                send_sem=send_sems.at[3 * a + j], recv_sem=recv_sems.at[3 * a + j],
                device_id=((my_chip ^ (j + 1)) // 2, (my_chip ^ (j + 1)) % 2, c), device_id_type=MESH)
            for a in range(n) for j in range(3)]
        sends += [
            pltpu.make_async_remote_copy(
                src_ref=slab_ref, dst_ref=slabs_ref.at[me],
                send_sem=send_sems.at[3 * n + k], recv_sem=recv_sems.at[3 * n + k],
                device_id=ids[k], device_id_type=MESH)
            for k in range(N_DEV - 1)]
        arrivals = sends[:3 * n] + [
            pltpu.make_async_remote_copy(
                src_ref=slab_ref, dst_ref=slabs_ref.at[others[k]],
                send_sem=send_sems.at[3 * n + k], recv_sem=recv_sems.at[3 * n + k],
                device_id=ids[k], device_id_type=MESH)
            for k in range(N_DEV - 1)]
        for cp in sends:
            cp.start()
        for cp in arrivals:
            cp.wait_recv()
        for cp in sends:
            cp.wait_send()
        mine.wait()

    n_sems = 3 * n + N_DEV - 1
    outs = pl.kernel(
        body, name=name,
        out_type=[jax.ShapeDtypeStruct(p.shape, p.dtype) for p in partials]
                 + [jax.ShapeDtypeStruct((N_DEV,) + slab.shape, slab.dtype)],
        mesh=plsc.ScalarSubcoreMesh(axis_name="sequencer", num_cores=1),
        scratch_types=[pltpu.SemaphoreType.DMA((n_sems,)), pltpu.SemaphoreType.DMA((n_sems,)), pltpu.SemaphoreType.DMA],
        compiler_params=pltpu.CompilerParams(collective_id=collective_id),
    )(*partials, slab)
    return outs[:n], outs[n]


def _sum_devices(gathered, after, *, name):
    _, R, C = gathered.shape

    def body(in_ref, after_ref, out_ref):
        total = in_ref[0]
        for d in range(1, N_DEV):
            total = total + in_ref[d]
        out_ref[...] = total

    return pl.pallas_call(
        body, name=name, grid=(1,),
        in_specs=[pl.BlockSpec((N_DEV, R, C), lambda i: (0, 0, 0)), AFTER],
        out_specs=pl.BlockSpec((R, C), lambda i: (0, 0)),
        out_shape=jax.ShapeDtypeStruct((R, C), F32),
        compiler_params=_params(("arbitrary",)),
    )(gathered, _in_hbm(after))


def _owner_sum_adamw(own, landed, w, m, v, after, *, transposed, name, group=None, into=()):
    H, R, C = w.shape
    tr = R // 2
    first_group = 0 if group is None else group

    def body(own_ref, landed_ref, w_ref, m_ref, v_ref, after_ref, *rest):
        g_ref, d_ref, nm_ref, nv_ref = rest[len(into):]
        total = own_ref[...]
        for j in range(3):
            total = total + landed_ref[j].astype(F32)
        if transposed:
            total = total.T
        g_ref[...] = total
        d_ref[...], nm_ref[...], nv_ref[...] = _adamw_update(w_ref[...], total, m_ref[...], v_ref[...])

    spec = pl.BlockSpec((None, tr, C), lambda h, i: (first_group + h, i, 0))
    if transposed:
        own_spec = pl.BlockSpec((None, C, tr), lambda h, i: (h, 0, i))
        landed_spec = pl.BlockSpec((3, None, C, tr), lambda h, i: (0, h, 0, i))
    else:
        own_spec = pl.BlockSpec((None, tr, C), lambda h, i: (h, i, 0))
        landed_spec = pl.BlockSpec((3, None, tr, C), lambda h, i: (0, h, i, 0))
    n_in = 6
    return pl.pallas_call(
        body, name=name, grid=(own.shape[0], R // tr),
        in_specs=[own_spec, landed_spec, spec, spec, spec, AFTER] + [pl.BlockSpec(memory_space=pl.ANY)] * len(into),
        out_specs=[spec] * 4,
        out_shape=[jax.ShapeDtypeStruct((H, R, C), F32)] * 4,
        input_output_aliases={n_in + j: j for j in range(len(into))},
        compiler_params=_params(("arbitrary", "arbitrary")),
    )(own, landed, w, m, v, _in_hbm(after), *into)


def _local_step(x, target, norms, pool_w_group, pool_scale, wgu1, wd1, w_in, wbp, wba, w_out, wgu2, wd2, exchange):
    n1g, nmg, n2g, nfg = norms
    D = x.shape[1]
    gu1, hid1 = _ffn_up(x, n1g, wgu1, tm=1024, name="ffn1_up")
    h1 = _ffn_down(x, hid1, wd1, tm=512, name="ffn1_down")
    un, proj = _inproj_fwd(h1, nmg, w_in, tm=1024, name="inproj_fwd")
    p = _pool_fwd(proj, pool_w_group, pool_scale, name="pool_fwd")
    o, ltot = _attn_fwd(proj, name="attn_fwd")
    h2, m = _mix_fwd(h1, p, o, proj, wbp, wba, w_out, tm=512, name="mix_fwd")
    gu2, hid2 = _ffn_up(h2, n2g, wgu2, tm=1024, name="ffn2_up")
    h3 = _ffn_down(h2, hid2, wd2, tm=512, name="ffn2_down")
    dh3, df2, loss, d_nf = _loss_bwd(h3, target, nfg, tm=256, name="loss_bwd")

    dh2, d_n2, n2, dgu2 = _ffn_bwd(dh3, df2, h2, n2g, gu2, wgu2, wd2, df2, tm=512, name="ffn2_bwd")
    d_wd2 = _wgrad_down(hid2, df2, tk=WGRAD_TOKENS, name="ffn2_wgrad_down")
    d_wgu2 = _wgrad_gate_up(n2, dgu2, tk=WGRAD_TOKENS, name="ffn2_wgrad_gate_up")
    (g_wd2, g_wgu2), token = exchange("ffn2", [d_wd2.reshape(N_DEV, FF_SHARD_PAD, D), d_wgu2])

    dyp, dys, dp, do, dgl = _mix_bwd(dh2, p, o, proj, wbp, wba, w_out, token, tm=512, name="mix_bwd")
    d_wout = _wgrad_full(m, dh2, tk=WGRAD_TOKENS, name="wgrad_out")
    d_wbp = _wgrad_full(dyp, p, tk=WGRAD_TOKENS, name="wgrad_branch_pool")
    d_wba = _wgrad_full(dys, o, tk=WGRAD_TOKENS, name="wgrad_branch_attn")
    by_owner = lambda g: g.reshape(N_DEV, g.shape[0] // N_DEV, g.shape[1])
    (g_wbp, g_wba, g_wout), token = exchange("mix", [by_owner(d_wbp), by_owner(d_wba), by_owner(d_wout)])
    dxp, d_wgroup, d_scale = _pool_bwd(dp, proj, pool_w_group, pool_scale, name="pool_bwd")
    dq, dk, dv = _attn_bwd(proj, do, ltot, token, name="attn_bwd")
    dproj_parts = [dxp, dq, dk, dv, dgl]
    dh1, df1, d_nm = _inproj_bwd(dproj_parts, dh2, h1, nmg, w_in, tm=512, name="inproj_bwd")
    d_win = _wgrad_in(dproj_parts, un, name="wgrad_in")
    d_wd1 = _wgrad_down(hid1, df1, tk=WGRAD_TOKENS, name="ffn1_wgrad_down")
    (g_win, g_wd1, replicated_early), token = exchange(
        "w_in_ffn1_down", [d_win, d_wd1.reshape(N_DEV, FF_SHARD_PAD, D), d_nm, d_n2, d_nf, d_scale, d_wgroup, loss])

    dx, d_n1, n1, dgu1 = _ffn_bwd(dh1, df1, x, n1g, gu1, wgu1, wd1, token, tm=512, name="ffn1_bwd")
    d_wgu1_a = _wgrad_gate_up(n1, dgu1, tk=WGRAD_TOKENS, name="ffn1_wgrad_gate_up_a", part=0, parts=2)
    (g_wgu1_a, replicated_late), token = exchange("ffn1_gate_up_a", [d_wgu1_a, d_n1])
    d_wgu1_b = _wgrad_gate_up(n1, dgu1, tk=WGRAD_TOKENS, name="ffn1_wgrad_gate_up_b", part=1, parts=2)
    (g_wgu1_b,), token = exchange("last", [d_wgu1_b])
    g_wgu1 = (g_wgu1_a, g_wgu1_b)

    sharded = (g_wgu1, g_wd1, g_win, g_wbp, g_wba, g_wout, g_wgu2, g_wd2)
    return dx, sharded, (replicated_late, replicated_early), token


def _hidden_major(w):
    return jnp.swapaxes(w[0], 0, 1)


def _pad_gate_up(wt):
    d = wt.shape[1]
    wt = wt.astype(BF16).reshape(2, FF_SHARD, d)
    return jnp.pad(wt, ((0, 0), (0, FF_SHARD_PAD - FF_SHARD), (0, 0))).reshape(2 * FF_SHARD_PAD, d)


def _unpad_gate_up(gt):
    d = gt.shape[1]
    return gt.reshape(2, FF_SHARD_PAD, d)[:, :FF_SHARD].reshape(2 * FF_SHARD, d)


def _pad_down(w):
    return jnp.pad(w.astype(BF16), ((0, FF_SHARD_PAD - FF_SHARD), (0, 0)))


def kernel(x, ffn1_norm, ffn1_w_gate_up, ffn1_w_down, mix_norm, w_in, pool_w_group, pool_scale, w_branch_pool, w_branch_attn, w_out, ffn2_norm, ffn2_w_gate_up, ffn2_w_down, final_norm, loss_target, m_ffn1_norm, m_ffn1_w_gate_up, m_ffn1_w_down, m_mix_norm, m_w_in, m_pool_w_group, m_pool_scale, m_w_branch_pool, m_w_branch_attn, m_w_out, m_ffn2_norm, m_ffn2_w_gate_up, m_ffn2_w_down, m_final_norm, v_ffn1_norm, v_ffn1_w_gate_up, v_ffn1_w_down, v_mix_norm, v_w_in, v_pool_w_group, v_pool_scale, v_w_branch_pool, v_w_branch_attn, v_w_out, v_ffn2_norm, v_ffn2_w_gate_up, v_ffn2_w_down, v_final_norm):
    D = x.shape[-1]
    weights = dict(ffn1_norm=ffn1_norm, ffn1_w_gate_up=ffn1_w_gate_up, ffn1_w_down=ffn1_w_down, mix_norm=mix_norm,
                   w_in=w_in, pool_w_group=pool_w_group, pool_scale=pool_scale, w_branch_pool=w_branch_pool,
                   w_branch_attn=w_branch_attn, w_out=w_out, ffn2_norm=ffn2_norm, ffn2_w_gate_up=ffn2_w_gate_up,
                   ffn2_w_down=ffn2_w_down, final_norm=final_norm)
    first = dict(ffn1_norm=m_ffn1_norm, ffn1_w_gate_up=m_ffn1_w_gate_up, ffn1_w_down=m_ffn1_w_down,
                 mix_norm=m_mix_norm, w_in=m_w_in, pool_w_group=m_pool_w_group, pool_scale=m_pool_scale,
                 w_branch_pool=m_w_branch_pool, w_branch_attn=m_w_branch_attn, w_out=m_w_out,
                 ffn2_norm=m_ffn2_norm, ffn2_w_gate_up=m_ffn2_w_gate_up, ffn2_w_down=m_ffn2_w_down,
                 final_norm=m_final_norm)
    second = dict(ffn1_norm=v_ffn1_norm, ffn1_w_gate_up=v_ffn1_w_gate_up, ffn1_w_down=v_ffn1_w_down,
                  mix_norm=v_mix_norm, w_in=v_w_in, pool_w_group=v_pool_w_group, pool_scale=v_pool_scale,
                  w_branch_pool=v_w_branch_pool, w_branch_attn=v_w_branch_attn, w_out=v_w_out,
                  ffn2_norm=v_ffn2_norm, ffn2_w_gate_up=v_ffn2_w_gate_up, ffn2_w_down=v_ffn2_w_down,
                  final_norm=v_final_norm)
    order = list(weights)

    wgu1, = _all_gather([_pad_gate_up(_hidden_major(ffn1_w_gate_up))], name="all_gather_ffn1_gate_up", collective_id=0)
    wd1, = _all_gather([_pad_down(ffn1_w_down[0])], name="all_gather_ffn1_down", collective_id=10)
    transposed = lambda w: jnp.swapaxes(w[0], 0, 1).astype(BF16)
    win_g, = _all_gather([transposed(w_in)], name="all_gather_w_in", collective_id=1)
    wbp_g, wba_g = _all_gather([transposed(w_branch_pool), transposed(w_branch_attn)],
                               name="all_gather_branches", collective_id=2)
    wout_g, = _all_gather([w_out[0].astype(BF16)], name="all_gather_w_out", collective_id=11)
    wgu2, wd2 = _all_gather([_pad_gate_up(_hidden_major(ffn2_w_gate_up)), _pad_down(ffn2_w_down[0])],
                            name="all_gather_ffn2", collective_id=3)
    whole = lambda g: g.reshape(g.shape[0] * g.shape[1], g.shape[2])
    wd1, wd2, win_g, wbp_g, wba_g, wout_g = (whole(g) for g in (wd1, wd2, win_g, wbp_g, wba_g, wout_g))

    cross_ids = {"ffn2": 4, "mix": 5, "w_in_ffn1_down": 8, "ffn1_gate_up_a": 9, "last": 7}
    small = ["ffn1_norm", "mix_norm", "ffn2_norm", "final_norm", "pool_scale", "pool_w_group"]

    def tile_rows(a):
        a = a.reshape(-1, 128)
        return jnp.pad(a, ((0, -a.shape[0] % 8), (0, 0)))

    def exchange(tag, group):
        grads = [g for g in group if g.dtype == BF16]
        extras = [tile_rows(g) for g in group if g.dtype != BF16]
        sums = _chip_sums(grads, name="chip_sums_" + tag)
        partials = [s[0] for s in sums]
        handles = []
        if extras:
            landed, slabs = _cross_chips_and_gather(partials, jnp.concatenate(extras, axis=0),
                                                    name="cross_chips_" + tag, collective_id=cross_ids[tag])
            handles = [slabs]
        else:
            landed = _cross_chips(partials, name="cross_chips_" + tag, collective_id=cross_ids[tag])
        return [(s[1], l) for s, l in zip(sums, landed)] + handles, sums[-1][1]

    norms = (ffn1_norm, mix_norm, ffn2_norm, final_norm.reshape(1, D))
    dx, sharded, (slabs_late, slabs_early), last = _local_step(
        x[0], loss_target[0], norms, pool_w_group[0], pool_scale, wgu1, wd1, win_g, wbp_g, wba_g, wout_g, wgu2, wd2,
        exchange)
    names = ["ffn1_w_gate_up", "ffn1_w_down", "w_in", "w_branch_pool", "w_branch_attn", "w_out",
             "ffn2_w_gate_up", "ffn2_w_down"]
    handles = dict(zip(names, sharded))
    grads, delta, new_m, new_v = {}, {}, {}, {}
    loss_out = []

    def update_replicated(after):
        rows = [weights[k].size // 128 for k in small]
        padded_rows = [-(-r // 8) * 8 for r in rows]
        starts = [sum(padded_rows[:i]) for i in range(len(rows) + 1)]
        total = jnp.concatenate([_sum_devices(slabs_late, after, name="sum_replicated_late"),
                                 _sum_devices(slabs_early, after, name="sum_replicated_early")], axis=0)
        loss_out.append(total[starts[-1], 0])
        small_w = jnp.concatenate([tile_rows(weights[k]) for k in small], axis=0)
        small_m = jnp.concatenate([tile_rows(first[k]) for k in small], axis=0)
        small_v = jnp.concatenate([tile_rows(second[k]) for k in small], axis=0)
        small_out = _adamw(small_w, total[:starts[-1]], small_m, small_v, name="adamw_replicated")
        for name_, start, n_rows in zip(small, starts, rows):
            shape = weights[name_].shape
            grads[name_] = total[start:start + n_rows].reshape(shape)
            delta[name_], new_m[name_], new_v[name_] = (a[start:start + n_rows].reshape(shape) for a in small_out)
        return small_out[0]

    after = last
    for k in ("ffn2_w_down", "ffn2_w_gate_up", "w_branch_pool", "w_branch_attn", "w_out", "w_in", "ffn1_w_down",
              "ffn1_w_gate_up"):
        hidden_major = k.endswith("w_gate_up")
        view = _hidden_major if hidden_major else (lambda a: a[0])
        back = (lambda a: jnp.swapaxes(a, 0, 1)[None]) if hidden_major else (lambda a: a[None])
        groups = 2 if hidden_major else 1
        by_group = lambda a: a.reshape(a.shape[:-2] + (groups, a.shape[-2] // groups, a.shape[-1]))
        state = [by_group(view(a[k])) for a in (weights, first, second)]
        if isinstance(handles[k][0], tuple):
            (own_a, landed_a), (own_b, landed_b) = handles[k]
            out = _owner_sum_adamw(own_a[None], landed_a[:, None], *state, after, name="adamw_" + k + "_a",
                                   transposed=False, group=0)
            out = _owner_sum_adamw(own_b[None], landed_b[:, None], *state, update_replicated(out[1]),
                                   name="adamw_" + k + "_b", transposed=False, group=1, into=out)
        else:
            own, landed = handles[k]
            out = _owner_sum_adamw(by_group(own), by_group(landed), *state, after, name="adamw_" + k,
                                   transposed=k in ("w_in", "w_branch_pool", "w_branch_attn"))
        after = out[1]
        grads[k], delta[k], new_m[k], new_v[k] = (back(a.reshape(-1, a.shape[-1])) for a in out)

    return (loss_out[0], dx[None], *[grads[k] for k in order], *[delta[k] for k in order],
            *[new_m[k] for k in order], *[new_v[k] for k in order])
```

```python
import jax
import jax.numpy as jnp
from jax import lax
from jax.experimental import pallas as pl
from jax.experimental.pallas import tpu as pltpu
from jax.experimental.pallas import tpu_sc as plsc

F32 = jnp.float32
BF16 = jnp.bfloat16
MESH = pl.DeviceIdType.MESH

RMS_EPS = 1e-6
N_DEV = 8
N_HEADS = 8
HEAD_DIM = 64
HEAD_PAIR = 2 * HEAD_DIM
POOL_WINDOWS = (2, 4, 8, 16)
POOL_GROUP = 128
POOL_WIDTH = 512
SB_WIDTH = 512
FF_SHARD = 352
FF_SHARD_PAD = 384
ATTN_K_BLOCK = 256
ATTN_Q_BLOCK_FWD = 512
ATTN_Q_BLOCK_BWD = 256
ATTN_SCALE = 0.125

ADAM_LR = 0.001
ADAM_B1 = 0.9
ADAM_B2 = 0.999
ADAM_EPS = 1e-08
ADAM_WD = 0.01
ADAM_STEP = 10

VMEM_LIMIT = 48 << 20
WGRAD_TOKENS = 2048


def _params(dims=None):
    return pltpu.CompilerParams(dimension_semantics=dims, vmem_limit_bytes=VMEM_LIMIT)


def _mm(a, b):
    return jnp.dot(a, b, preferred_element_type=F32)


def _mm_nt(a, b):
    return lax.dot_general(a, b, (((1,), (1,)), ((), ())), preferred_element_type=F32)


def _mm_tn(a, b):
    return lax.dot_general(a, b, (((0,), (0,)), ((), ())), preferred_element_type=F32)


def _row_tile(rows, cols):
    limit = max(8, (512 * 1024) // cols)
    return max(t for t in range(8, rows + 1, 8) if rows % t == 0 and (t <= limit or t == 8))


def _rstd(xf):
    return lax.rsqrt(jnp.mean(xf * xf, axis=-1, keepdims=True) + RMS_EPS)


def _rms_bwd(xf, gain, dn):
    r = _rstd(xf)
    xh = xf * r
    dgain = jnp.sum(dn * xh, axis=0, keepdims=True)
    dxh = dn * gain
    dx = r * (dxh - xh * jnp.mean(dxh * xh, axis=-1, keepdims=True))
    return dx, dgain


def _ffn_up(x, gain, wgu, *, tm, name):
    T, D = x.shape
    tm = min(tm, T)
    nb, bw = wgu.shape[0] // 2, wgu.shape[1]

    def body(x_ref, gain_ref, wg_ref, wu_ref, gu_ref, hid_ref, n_scr):
        @pl.when(pl.program_id(1) == 0)
        def _():
            xf = x_ref[...]
            n_scr[...] = (xf * _rstd(xf) * gain_ref[...]).astype(BF16)

        halves = (pl.ds(0, tm // 2), pl.ds(tm // 2, tm // 2))
        wg, wu = wg_ref[...], wu_ref[...]
        gus = [(_mm_nt(n_scr[rows, :], wg), _mm_nt(n_scr[rows, :], wu)) for rows in halves]
        for rows, (g, u) in zip(halves, gus):
            gu_ref[0, rows, :] = g.astype(BF16)
            gu_ref[1, rows, :] = u.astype(BF16)
            hid_ref[rows, :] = (g * jax.nn.sigmoid(g) * u).astype(BF16)

    return pl.pallas_call(
        body, name=name, grid=(T // tm, nb),
        in_specs=[
            pl.BlockSpec((tm, D), lambda i, j: (i, 0)),
            pl.BlockSpec((1, D), lambda i, j: (0, 0)),
            pl.BlockSpec((None, bw, D), lambda i, j: (j, 0, 0)),
            pl.BlockSpec((None, bw, D), lambda i, j: (j + nb, 0, 0)),
        ],
        out_specs=[
            pl.BlockSpec((2, tm, bw), lambda i, j: (0, i, j)),
            pl.BlockSpec((tm, bw), lambda i, j: (i, j)),
        ],
        out_shape=[jax.ShapeDtypeStruct((2, T, nb * bw), BF16), jax.ShapeDtypeStruct((T, nb * bw), BF16)],
        scratch_shapes=[pltpu.VMEM((tm, D), BF16)],
        compiler_params=_params(("arbitrary", "arbitrary")),
    )(x, gain, wgu, wgu)


def _ffn_down(x, hid, wd, *, tm, name):
    T, D = x.shape
    tm = min(tm, T)
    F = hid.shape[1]

    def body(x_ref, hid_ref, wd_ref, h_ref):
        h_ref[...] = x_ref[...] + 0.5 * _mm(hid_ref[...], wd_ref[...])

    return pl.pallas_call(
        body, name=name, grid=(T // tm,),
        in_specs=[
            pl.BlockSpec((tm, D), lambda i: (i, 0)),
            pl.BlockSpec((tm, F), lambda i: (i, 0)),
            pl.BlockSpec((F, D), lambda i: (0, 0)),
        ],
        out_specs=pl.BlockSpec((tm, D), lambda i: (i, 0)),
        out_shape=jax.ShapeDtypeStruct((T, D), F32),
        compiler_params=_params(("arbitrary",)),
    )(x, hid, wd)


AFTER = pl.BlockSpec(memory_space=pltpu.HBM)


def _in_hbm(token):
    return pltpu.with_memory_space_constraint(token, pltpu.HBM)


def _ffn_bwd(dh, df, x, gain, gu, wgu, wd, after, *, tm, name):
    T, D = x.shape
    tm = min(tm, T)
    nb, bw = wgu.shape[0] // 2, wgu.shape[1]

    def body(dh_ref, df_ref, x_ref, gain_ref, gu_ref, wg_ref, wu_ref, wd_ref, after_ref,
             dx_ref, dgain_ref, n_ref, dgu_ref, dn_acc):
        i, j = pl.program_id(0), pl.program_id(1)

        @pl.when(j == 0)
        def _():
            xf = x_ref[...]
            n_ref[...] = (xf * _rstd(xf) * gain_ref[...]).astype(BF16)
            dn_acc[...] = jnp.zeros_like(dn_acc)

        @pl.when((i == 0) & (j == 0))
        def _():
            dgain_ref[...] = jnp.zeros_like(dgain_ref)

        halves = (pl.ds(0, tm // 2), pl.ds(tm // 2, tm // 2))
        wd, wg, wu = wd_ref[...], wg_ref[...], wu_ref[...]
        dhids = [_mm_nt(df_ref[rows, :], wd) for rows in halves]
        for rows, dhid in zip(halves, dhids):
            g = gu_ref[0, rows, :].astype(F32)
            u = gu_ref[1, rows, :].astype(F32)
            s = jax.nn.sigmoid(g)
            silu = g * s
            dg = (dhid * u * (s * (1.0 + g * (1.0 - s)))).astype(BF16)
            du = (dhid * silu).astype(BF16)
            dgu_ref[0, rows, :] = dg
            dgu_ref[1, rows, :] = du
            dn_acc[rows, :] += _mm(dg, wg) + _mm(du, wu)

        @pl.when(j == nb - 1)
        def _():
            dx, dgain = _rms_bwd(x_ref[...], gain_ref[...], dn_acc[...])
            dx_ref[...] = dh_ref[...] + dx
            dgain_ref[...] += dgain

    row = lambda i, j: (i, 0)
    return pl.pallas_call(
        body, name=name, grid=(T // tm, nb),
        in_specs=[
            pl.BlockSpec((tm, D), row),
            pl.BlockSpec((tm, D), row),
            pl.BlockSpec((tm, D), row),
            pl.BlockSpec((1, D), lambda i, j: (0, 0)),
            pl.BlockSpec((2, tm, bw), lambda i, j: (0, i, j)),
            pl.BlockSpec((None, bw, D), lambda i, j: (j, 0, 0)),
            pl.BlockSpec((None, bw, D), lambda i, j: (j + nb, 0, 0)),
            pl.BlockSpec((bw, D), lambda i, j: (j, 0)),
            AFTER,
        ],
        out_specs=[
            pl.BlockSpec((tm, D), row),
            pl.BlockSpec((1, D), lambda i, j: (0, 0)),
            pl.BlockSpec((tm, D), row),
            pl.BlockSpec((2, tm, bw), lambda i, j: (0, i, j)),
        ],
        out_shape=[
            jax.ShapeDtypeStruct((T, D), F32),
            jax.ShapeDtypeStruct((1, D), F32),
            jax.ShapeDtypeStruct((T, D), BF16),
            jax.ShapeDtypeStruct((2, T, nb * bw), BF16),
        ],
        scratch_shapes=[pltpu.VMEM((tm, D), F32)],
        compiler_params=_params(("arbitrary", "arbitrary")),
    )(dh, df, x, gain, gu, wgu, wgu, wd, _in_hbm(after))


def _wgrad(a, b, *, grid, a_spec, b_spec, out_spec, out_shape, acc_shape, name):
    nk = grid[2]

    def body(a_ref, b_ref, o_ref, acc):
        k = pl.program_id(2)

        @pl.when(k == 0)
        def _():
            acc[...] = jnp.zeros_like(acc)

        acc[...] += _mm_tn(a_ref[...].astype(BF16), b_ref[...].astype(BF16))

        @pl.when(k == nk - 1)
        def _():
            o_ref[...] = acc[...].astype(o_ref.dtype)

    return pl.pallas_call(
        body, name=name, grid=grid, in_specs=[a_spec, b_spec], out_specs=out_spec,
        out_shape=jax.ShapeDtypeStruct(out_shape, BF16),
        scratch_shapes=[pltpu.VMEM(acc_shape, F32)],
        compiler_params=_params(("arbitrary", "arbitrary", "arbitrary")),
    )(a, b)


def _wgrad_gate_up(n, dgu, *, tk, name, part=0, parts=1):
    T, D = n.shape
    tk = min(tk, T)
    owner_rows = FF_SHARD_PAD * 2
    nb = dgu.shape[2] // owner_rows
    bw = owner_rows // parts
    return _wgrad(
        dgu, n, grid=(2 * nb, 1, T // tk), name=name,
        a_spec=pl.BlockSpec((None, tk, bw), lambda m, c, k: (m // nb, k, parts * (m % nb) + part)),
        b_spec=pl.BlockSpec((tk, D), lambda m, c, k: (k, 0)),
        out_spec=pl.BlockSpec((None, bw, D), lambda m, c, k: (m, 0, 0)),
        out_shape=(2 * nb, bw, D), acc_shape=(bw, D))


def _wgrad_down(hid, df, *, tk, name):
    T, D = df.shape
    tk = min(tk, T)
    bw = FF_SHARD_PAD * 2
    nb = hid.shape[1] // bw
    return _wgrad(
        hid, df, grid=(nb, 1, T // tk), name=name,
        a_spec=pl.BlockSpec((tk, bw), lambda m, c, k: (k, m)),
        b_spec=pl.BlockSpec((tk, D), lambda m, c, k: (k, 0)),
        out_spec=pl.BlockSpec((bw, D), lambda m, c, k: (m, 0)),
        out_shape=(nb * bw, D), acc_shape=(bw, D))


def _wgrad_in(dparts, un, *, name):
    T, D = un.shape
    bw = sum(p.shape[1] for p in dparts) // N_DEV
    first = [sum(p.shape[1] for p in dparts[:i]) // bw for i in range(len(dparts) + 1)]

    def body(*refs):
        dp_refs, un_ref, o_ref = refs[:-2], refs[-2], refs[-1]
        m = pl.program_id(0)
        for dp_ref, lo, hi in zip(dp_refs, first[:-1], first[1:]):
            @pl.when((m >= lo) & (m < hi))
            def _():
                o_ref[...] = _mm_tn(dp_ref[...], un_ref[...]).astype(o_ref.dtype)

    def piece_spec(lo, hi):
        return pl.BlockSpec((T, bw), lambda m: (0, jnp.clip(m - lo, 0, hi - lo - 1)))

    return pl.pallas_call(
        body, name=name, grid=(N_DEV,),
        in_specs=[piece_spec(lo, hi) for lo, hi in zip(first[:-1], first[1:])] + [pl.BlockSpec((T, D), lambda m: (0, 0))],
        out_specs=pl.BlockSpec((None, bw, D), lambda m: (m, 0, 0)),
        out_shape=jax.ShapeDtypeStruct((N_DEV, bw, D), BF16),
        compiler_params=_params(("arbitrary",)),
    )(*dparts, un)


def _wgrad_full(a, b, *, tk, name):
    T, M = a.shape
    tk = min(tk, T)
    N = b.shape[1]
    return _wgrad(
        a, b, grid=(1, 1, T // tk), name=name,
        a_spec=pl.BlockSpec((tk, M), lambda m, c, k: (k, 0)),
        b_spec=pl.BlockSpec((tk, N), lambda m, c, k: (k, 0)),
        out_spec=pl.BlockSpec((M, N), lambda m, c, k: (0, 0)), out_shape=(M, N), acc_shape=(M, N))


def _loss_bwd(h, target, gain, *, tm, name):
    T, D = h.shape
    tm = min(tm, T)

    def body(h_ref, t_ref, gain_ref, dh_ref, df_ref, loss_ref, dgain_ref):
        @pl.when(pl.program_id(0) == 0)
        def _():
            loss_ref[...] = jnp.zeros_like(loss_ref)
            dgain_ref[...] = jnp.zeros_like(dgain_ref)

        xf = h_ref[...]
        gain = gain_ref[...]
        err = xf * _rstd(xf) * gain - t_ref[...]
        loss_ref[...] += 0.5 * jnp.sum(jnp.mean(err * err, axis=-1, keepdims=True), axis=0, keepdims=True)
        dx, dgain = _rms_bwd(xf, gain, err * (1.0 / D))
        dh_ref[...] = dx
        df_ref[...] = (0.5 * dx).astype(BF16)
        dgain_ref[...] += dgain

    row = lambda i: (i, 0)
    fixed = lambda i: (0, 0)
    return pl.pallas_call(
        body, name=name, grid=(T // tm,),
        in_specs=[pl.BlockSpec((tm, D), row), pl.BlockSpec((tm, D), row), pl.BlockSpec((1, D), fixed)],
        out_specs=[pl.BlockSpec((tm, D), row), pl.BlockSpec((tm, D), row), pl.BlockSpec((1, 128), fixed),
                   pl.BlockSpec((1, D), fixed)],
        out_shape=[jax.ShapeDtypeStruct((T, D), F32), jax.ShapeDtypeStruct((T, D), BF16),
                   jax.ShapeDtypeStruct((1, 128), F32), jax.ShapeDtypeStruct((1, D), F32)],
        compiler_params=_params(("arbitrary",)),
    )(h, target, gain)


def _inproj_fwd(h, gain, w_in_t, *, tm, name):
    T, D = h.shape
    tm = min(tm, T)
    bn = D
    nb = w_in_t.shape[0] // bn

    def body(h_ref, gain_ref, wt_ref, un_ref, proj_ref):
        @pl.when(pl.program_id(1) == 0)
        def _():
            xf = h_ref[...]
            un_ref[...] = (xf * _rstd(xf) * gain_ref[...]).astype(BF16)

        proj_ref[...] = _mm_nt(un_ref[...], wt_ref[...])

    return pl.pallas_call(
        body, name=name, grid=(T // tm, nb),
        in_specs=[
            pl.BlockSpec((tm, D), lambda i, j: (i, 0)),
            pl.BlockSpec((1, D), lambda i, j: (0, 0)),
            pl.BlockSpec((bn, D), lambda i, j: (j, 0)),
        ],
        out_specs=[pl.BlockSpec((tm, D), lambda i, j: (i, 0)), pl.BlockSpec((tm, bn), lambda i, j: (i, j))],
        out_shape=[jax.ShapeDtypeStruct((T, D), BF16), jax.ShapeDtypeStruct((T, nb * bn), F32)],
        compiler_params=_params(("arbitrary", "arbitrary")),
    )(h, gain, w_in_t)


def _inproj_bwd(dparts, dh, h, gain, w_in_t, *, tm, name):
    T, D = h.shape
    tm = min(tm, T)
    n = len(dparts)
    widths = [p.shape[1] for p in dparts]
    starts = [sum(widths[:i]) for i in range(n)]

    def body(*refs):
        dp_refs = refs[:n]
        dh_ref, h_ref, gain_ref, wt_ref, dx_ref, df_ref, dgain_ref = refs[n:]

        @pl.when(pl.program_id(0) == 0)
        def _():
            dgain_ref[...] = jnp.zeros_like(dgain_ref)

        dn = sum(_mm(dp_ref[...], wt_ref[start:start + width, :])
                 for dp_ref, start, width in zip(dp_refs, starts, widths))
        dx, dgain = _rms_bwd(h_ref[...], gain_ref[...], dn)
        dh_in = dh_ref[...] + dx
        dx_ref[...] = dh_in
        df_ref[...] = (0.5 * dh_in).astype(BF16)
        dgain_ref[...] += dgain

    row = lambda i: (i, 0)
    fixed = lambda i: (0, 0)
    return pl.pallas_call(
        body, name=name, grid=(T // tm,),
        in_specs=[pl.BlockSpec((tm, width), row) for width in widths] + [
            pl.BlockSpec((tm, D), row),
            pl.BlockSpec((tm, D), row),
            pl.BlockSpec((1, D), fixed),
            pl.BlockSpec(w_in_t.shape, fixed),
        ],
        out_specs=[pl.BlockSpec((tm, D), row), pl.BlockSpec((tm, D), row), pl.BlockSpec((1, D), fixed)],
        out_shape=[jax.ShapeDtypeStruct((T, D), F32), jax.ShapeDtypeStruct((T, D), BF16),
                   jax.ShapeDtypeStruct((1, D), F32)],
        compiler_params=_params(("arbitrary",)),
    )(*dparts, dh, h, gain, w_in_t)


def _window_sum(x, row, doublings, *, backward):
    T = x.shape[0]
    s = x
    for k in range(doublings):
        sh = 1 << k
        if backward:
            s = s + jnp.where(row < T - sh, pltpu.roll(s, T - sh, 0), 0.0)
        else:
            s = s + jnp.where(row >= sh, pltpu.roll(s, sh, 0), 0.0)
    return s


def _pool_fwd(proj, w_group, scale, *, name):
    T = proj.shape[0]

    def body(xp_ref, w_ref, scale_ref, p_ref):
        row = lax.broadcasted_iota(jnp.int32, (T, POOL_GROUP), 0)
        for gi, window in enumerate(POOL_WINDOWS):
            cols = slice(gi * POOL_GROUP, (gi + 1) * POOL_GROUP)
            x = xp_ref[:, cols]
            inv_count = 1.0 / jnp.minimum(row + 1, window).astype(F32)
            yc = _window_sum(x, row, gi + 1, backward=False) * inv_count - x
            pre = _mm(yc.astype(BF16), w_ref[gi].astype(BF16))
            p_ref[:, cols] = pre * scale_ref[:, cols]

    return pl.pallas_call(
        body, name=name, grid=(1,),
        in_specs=[
            pl.BlockSpec((T, POOL_WIDTH), lambda i: (0, 0)),
            pl.BlockSpec(w_group.shape, lambda i: (0, 0, 0)),
            pl.BlockSpec((1, POOL_WIDTH), lambda i: (0, 0)),
        ],
        out_specs=pl.BlockSpec((T, POOL_WIDTH), lambda i: (0, 0)),
        out_shape=jax.ShapeDtypeStruct((T, POOL_WIDTH), F32),
        compiler_params=_params(("arbitrary",)),
    )(proj, w_group, scale)


def _pool_bwd(dp, proj, w_group, scale, *, name):
    T = proj.shape[0]

    def body(dp_ref, xp_ref, w_ref, scale_ref, dxp_ref, dw_ref, dscale_ref):
        row = lax.broadcasted_iota(jnp.int32, (T, POOL_GROUP), 0)
        for gi, window in enumerate(POOL_WINDOWS):
            cols = slice(gi * POOL_GROUP, (gi + 1) * POOL_GROUP)
            x = xp_ref[:, cols]
            inv_count = 1.0 / jnp.minimum(row + 1, window).astype(F32)
            yc = (_window_sum(x, row, gi + 1, backward=False) * inv_count - x).astype(BF16)
            w = w_ref[gi].astype(BF16)
            pre = _mm(yc, w)
            dpg = dp_ref[:, cols]
            dscale_ref[:, cols] = jnp.sum(dpg * pre, axis=0, keepdims=True)
            dpre = (dpg * scale_ref[:, cols]).astype(BF16)
            dw_ref[gi] = _mm_tn(yc, dpre)
            dyc = _mm_nt(dpre, w)
            dxp_ref[:, cols] = (_window_sum(dyc * inv_count, row, gi + 1, backward=True) - dyc).astype(BF16)

    return pl.pallas_call(
        body, name=name, grid=(1,),
        in_specs=[
            pl.BlockSpec((T, POOL_WIDTH), lambda i: (0, 0)),
            pl.BlockSpec((T, POOL_WIDTH), lambda i: (0, 0)),
            pl.BlockSpec(w_group.shape, lambda i: (0, 0, 0)),
            pl.BlockSpec((1, POOL_WIDTH), lambda i: (0, 0)),
        ],
        out_specs=[
            pl.BlockSpec((T, POOL_WIDTH), lambda i: (0, 0)),
            pl.BlockSpec(w_group.shape, lambda i: (0, 0, 0)),
            pl.BlockSpec((1, POOL_WIDTH), lambda i: (0, 0)),
        ],
        out_shape=[jax.ShapeDtypeStruct((T, POOL_WIDTH), BF16), jax.ShapeDtypeStruct(w_group.shape, F32),
                   jax.ShapeDtypeStruct((1, POOL_WIDTH), F32)],
        compiler_params=_params(("arbitrary",)),
    )(dp, proj, w_group, scale)


ATTN_STRIP = 32


def _log_sigmoids(z):
    lb = jnp.minimum(z, 0.0) - jnp.log(1.0 + jnp.exp(-jnp.abs(z)))
    return lb, lb - z


def _transposed_blocks(x_ref, blocks_scr, tq):
    for b in range(blocks_scr.shape[0]):
        blocks_scr[b] = x_ref[b * tq:(b + 1) * tq, :].T.astype(BF16)


def _split_bf16(x):
    hi = x.astype(BF16)
    return hi, (x - hi.astype(F32)).astype(BF16)


def _strips(n):
    return [slice(i, i + ATTN_STRIP) for i in range(0, n, ATTN_STRIP)]


def _rows(parts):
    return jnp.concatenate(parts, axis=0)


def _attn_specs(T, tq):
    q_col = POOL_WIDTH // HEAD_PAIR
    k_col = q_col + SB_WIDTH // HEAD_PAIR
    v_col = k_col + SB_WIDTH // HEAD_PAIR
    return [
        pl.BlockSpec((tq, HEAD_PAIR), lambda p, i: (i, q_col + p)),
        pl.BlockSpec((T, HEAD_PAIR), lambda p, i: (0, k_col + p)),
        pl.BlockSpec((T, HEAD_PAIR), lambda p, i: (0, v_col + p)),
    ]


def _attn_fwd(proj, *, name):
    T = proj.shape[0]
    tk = min(ATTN_K_BLOCK, T)
    tq = min(ATTN_Q_BLOCK_FWD, T)
    diagonal_blocks = tq // tk

    def body(q_ref, k_ref, v_ref, o_ref, lt_ref, kt_scr, vb_scr):
        qi = pl.program_id(1)

        @pl.when(qi == 0)
        def _():
            _transposed_blocks(k_ref, kt_scr, tk)
            vb_scr[...] = v_ref[...].astype(BF16)

        head0 = lax.broadcasted_iota(jnp.int32, (tq, HEAD_PAIR), 1) < HEAD_DIM
        q = q_ref[...] * ATTN_SCALE
        qs = (jnp.where(head0, q, 0.0).astype(BF16), jnp.where(head0, 0.0, q).astype(BF16))
        r = lax.broadcasted_iota(jnp.int32, (tq, tk), 0)
        c = lax.broadcasted_iota(jnp.int32, (tq, tk), 1)
        later = (r[:tk] > c[:tk]).astype(BF16)
        later2 = _rows([later, later])
        causal = lambda d: (lambda rows: c[rows] + d * tk < r[rows])
        strips = _strips(tq)

        def log_terms(z, valid):
            lbs, his, los, sums = [], [], [], []
            for rows in strips:
                lb, lm = _log_sigmoids(z[rows])
                if valid is not None:
                    lm = jnp.where(valid(rows), lm, 0.0)
                hi, lo = _split_bf16(lm)
                lbs.append(lb)
                his.append(hi)
                los.append(lo)
                sums.append(jnp.sum(lm, axis=1, keepdims=True))
            return lbs, jnp.concatenate([_rows(his), _rows(los)], axis=1), _rows(sums)

        def weights(lbs, run, after, valid):
            parts = []
            for rows, lb in zip(strips, lbs):
                a = jnp.exp(lb + run[rows] + after[rows])
                if valid is not None:
                    a = jnp.where(valid(rows), a, 0.0)
                parts.append(a.astype(BF16))
            return _rows(parts)

        def block(kj, carry, valid):
            kt = kt_scr[kj]
            vb = vb_scr[pl.ds(pl.multiple_of(kj * tk, tk), tk), :]
            run0, o0, run1, o1 = carry
            z0 = _mm(qs[0], kt)
            z1 = _mm(qs[1], kt)
            lbs0, split0, sums0 = log_terms(z0, valid)
            after0 = _mm(split0, later2)
            lbs1, split1, sums1 = log_terms(z1, valid)
            after1 = _mm(split1, later2)
            o0 = o0 + _mm(weights(lbs0, run0, after0, valid), vb)
            o1 = o1 + _mm(weights(lbs1, run1, after1, valid), vb)
            return run0 + sums0, o0, run1 + sums1, o1

        zero = (jnp.zeros((tq, 1), F32), jnp.zeros((tq, HEAD_PAIR), F32))
        first = diagonal_blocks * qi
        carry = zero + zero
        for d in reversed(range(diagonal_blocks)):
            carry = block(first + d, carry, causal(d))
        carry = lax.fori_loop(0, first, lambda it, cr: block(first - 1 - it, cr, None), carry)
        o_ref[...] = jnp.where(head0, carry[1], carry[3])
        lt_ref[...] = jnp.where(head0, carry[0], carry[2])

    out_spec = pl.BlockSpec((tq, HEAD_PAIR), lambda p, i: (i, p))
    return pl.pallas_call(
        body, name=name, grid=(N_HEADS // 2, T // tq),
        in_specs=_attn_specs(T, tq), out_specs=[out_spec, out_spec],
        out_shape=[jax.ShapeDtypeStruct((T, SB_WIDTH), F32), jax.ShapeDtypeStruct((T, SB_WIDTH), F32)],
        scratch_shapes=[pltpu.VMEM((T // tk, HEAD_PAIR, tk), BF16), pltpu.VMEM((T, HEAD_PAIR), BF16)],
        compiler_params=_params(("arbitrary", "arbitrary")),
    )(proj, proj, proj)


def _attn_bwd(proj, do, ltot, after, *, name):
    T = proj.shape[0]
    tk = min(ATTN_K_BLOCK, T)
    tq = min(ATTN_Q_BLOCK_BWD, T)
    diagonal_blocks = tq // tk

    def body(q_ref, k_ref, v_ref, do_ref, lt_ref, after_ref, dq_ref, dk_ref, dv_ref,
             kb_scr, kt_scr, vt_scr, dkt_ref, dvt_ref):
        qi = pl.program_id(1)

        @pl.when(qi == 0)
        def _():
            kb_scr[...] = k_ref[...].astype(BF16)
            _transposed_blocks(k_ref, kt_scr, tk)
            _transposed_blocks(v_ref, vt_scr, tk)
            dkt_ref[...] = jnp.zeros_like(dkt_ref)
            dvt_ref[...] = jnp.zeros_like(dvt_ref)

        head0 = lax.broadcasted_iota(jnp.int32, (tq, HEAD_PAIR), 1) < HEAD_DIM
        q, do_, lt = q_ref[...] * ATTN_SCALE, do_ref[...], lt_ref[...]
        qs = (jnp.where(head0, q, 0.0).astype(BF16), jnp.where(head0, 0.0, q).astype(BF16))
        q_heads = (jnp.where(head0, q, 0.0), jnp.where(head0, 0.0, q))
        do_heads = (jnp.where(head0, do_, 0.0), jnp.where(head0, 0.0, do_))
        dos = tuple(d.astype(BF16) for d in do_heads)
        qts = tuple(x.T.astype(BF16) for x in q_heads)
        dots = tuple(d.T.astype(BF16) for d in do_heads)
        lts = (jnp.max(jnp.where(head0, lt, -jnp.inf), axis=1, keepdims=True),
               jnp.max(jnp.where(head0, -jnp.inf, lt), axis=1, keepdims=True))
        r = lax.broadcasted_iota(jnp.int32, (tq, tk), 0)
        c = lax.broadcasted_iota(jnp.int32, (tq, tk), 1)
        upto = (r[:tk] <= c[:tk]).astype(BF16)
        before = (r[:tk] < c[:tk]).astype(BF16)
        upto2, before2 = _rows([upto, upto]), _rows([before, before])
        causal = lambda d: (lambda rows: c[rows] + d * tk < r[rows])
        strips = _strips(tq)

        def log_terms(z, valid):
            lbs, his, los, sums = [], [], [], []
            for rows in strips:
                lb, lm = _log_sigmoids(z[rows])
                if valid is not None:
                    lm = jnp.where(valid(rows), lm, 0.0)
                hi, lo = _split_bf16(lm)
                lbs.append(lb)
                his.append(hi)
                los.append(lo)
                sums.append(jnp.sum(lm, axis=1, keepdims=True))
            return lbs, jnp.concatenate([_rows(his), _rows(los)], axis=1), _rows(sums)

        def weights(lbs, rest, lm_upto, da, valid):
            a_parts, es, his, los, sums = [], [], [], [], []
            for rows, lb in zip(strips, lbs):
                a = jnp.exp(lb + (rest[rows] - lm_upto[rows]))
                if valid is not None:
                    a = jnp.where(valid(rows), a, 0.0)
                e = da[rows] * a
                hi, lo = _split_bf16(e)
                a_parts.append(a.astype(BF16))
                es.append(e)
                his.append(hi)
                los.append(lo)
                sums.append(jnp.sum(e, axis=1, keepdims=True))
            return _rows(a_parts), es, jnp.concatenate([_rows(his), _rows(los)], axis=1), _rows(sums)

        def score_grads(lbs, es, run_e, e_before, valid):
            parts = []
            for rows, lb, e in zip(strips, lbs, es):
                beta = jnp.exp(lb)
                dz = e * (1.0 - beta) - (run_e[rows] + e_before[rows]) * beta
                if valid is not None:
                    dz = jnp.where(valid(rows), dz, 0.0)
                parts.append(dz.astype(BF16))
            return _rows(parts)

        def block(kj, carry, valid):
            off = pl.multiple_of(kj * tk, tk)
            kb, kt, vt = kb_scr[pl.ds(off, tk), :], kt_scr[kj], vt_scr[kj]
            run_lm0, run_e0, dq0, run_lm1, run_e1, dq1 = carry
            z0, da0 = _mm(qs[0], kt), _mm(dos[0], vt)
            z1, da1 = _mm(qs[1], kt), _mm(dos[1], vt)
            lbs0, split0, lm_sums0 = log_terms(z0, valid)
            lm_upto0 = _mm(split0, upto2)
            lbs1, split1, lm_sums1 = log_terms(z1, valid)
            lm_upto1 = _mm(split1, upto2)
            a0, es0, split0, e_sums0 = weights(lbs0, lts[0] - run_lm0, lm_upto0, da0, valid)
            e_before0 = _mm(split0, before2)
            a1, es1, split1, e_sums1 = weights(lbs1, lts[1] - run_lm1, lm_upto1, da1, valid)
            e_before1 = _mm(split1, before2)
            dz0 = score_grads(lbs0, es0, run_e0, e_before0, valid)
            dkt_blk = _mm(qts[0], dz0)
            dvt_blk = _mm(dots[0], a0)
            dq0 = dq0 + _mm(dz0, kb)
            dz1 = score_grads(lbs1, es1, run_e1, e_before1, valid)
            dkt_ref[kj] += dkt_blk + _mm(qts[1], dz1)
            dvt_ref[kj] += dvt_blk + _mm(dots[1], a1)
            dq1 = dq1 + _mm(dz1, kb)
            return run_lm0 + lm_sums0, run_e0 + e_sums0, dq0, run_lm1 + lm_sums1, run_e1 + e_sums1, dq1

        zero = (jnp.zeros((tq, 1), F32), jnp.zeros((tq, 1), F32), jnp.zeros((tq, HEAD_PAIR), F32))
        first = diagonal_blocks * qi
        carry = lax.fori_loop(0, first, lambda kj, cr: block(kj, cr, None), zero + zero)
        for d in range(diagonal_blocks):
            carry = block(first + d, carry, causal(d))
        dq_ref[...] = (jnp.where(head0, carry[2], carry[5]) * ATTN_SCALE).astype(BF16)

        @pl.when(qi == T // tq - 1)
        def _():
            for b in range(T // tk):
                dk_ref[b * tk:(b + 1) * tk, :] = dkt_ref[b].T.astype(BF16)
                dv_ref[b * tk:(b + 1) * tk, :] = dvt_ref[b].T.astype(BF16)

    blk = pl.BlockSpec((tq, HEAD_PAIR), lambda p, i: (i, p))
    seq = pl.BlockSpec((T, HEAD_PAIR), lambda p, i: (0, p))
    transposed = pltpu.VMEM((T // tk, HEAD_PAIR, tk), F32)
    return pl.pallas_call(
        body, name=name, grid=(N_HEADS // 2, T // tq),
        in_specs=_attn_specs(T, tq) + [blk, blk, AFTER], out_specs=[blk, seq, seq],
        out_shape=[jax.ShapeDtypeStruct((T, SB_WIDTH), BF16)] * 3,
        scratch_shapes=[pltpu.VMEM((T, HEAD_PAIR), BF16), pltpu.VMEM((T // tk, HEAD_PAIR, tk), BF16),
                        pltpu.VMEM((T // tk, HEAD_PAIR, tk), BF16), transposed, transposed],
        compiler_params=_params(("arbitrary", "arbitrary")),
    )(proj, proj, proj, do, ltot, _in_hbm(after))


def _mix_specs(T, D, tm, wbp, w_out):
    gate_col = (POOL_WIDTH + 3 * SB_WIDTH) // D
    row = lambda i: (i, 0)
    return [
        pl.BlockSpec((tm, D), row),
        pl.BlockSpec((tm, POOL_WIDTH), row),
        pl.BlockSpec((tm, SB_WIDTH), row),
        pl.BlockSpec((tm, D), lambda i: (i, gate_col)),
        pl.BlockSpec((tm, D), lambda i: (i, gate_col + 1)),
        pl.BlockSpec(wbp.shape, lambda i: (0, 0)),
        pl.BlockSpec(wbp.shape, lambda i: (0, 0)),
        pl.BlockSpec(w_out.shape, lambda i: (0, 0)),
    ]


def _mix_fwd(h, p, o, proj, wbp, wba, w_out, *, tm, name):
    T, D = h.shape
    tm = min(tm, T)

    def body(h_ref, p_ref, o_ref, glp_ref, gls_ref, wbp_ref, wba_ref, wout_ref, hout_ref, m_ref):
        halves = (pl.ds(0, tm // 2), pl.ds(tm // 2, tm // 2))
        wbp, wba, wout = wbp_ref[...], wba_ref[...], wout_ref[...]
        branches = [(_mm_nt(p_ref[rows, :].astype(BF16), wbp), _mm_nt(o_ref[rows, :].astype(BF16), wba))
                    for rows in halves]
        for rows, (yp, ys) in zip(halves, branches):
            m = (jax.nn.sigmoid(glp_ref[rows, :]) * yp + jax.nn.sigmoid(gls_ref[rows, :]) * ys).astype(BF16)
            m_ref[rows, :] = m
            hout_ref[rows, :] = h_ref[rows, :] + _mm(m, wout)

    row = lambda i: (i, 0)
    return pl.pallas_call(
        body, name=name, grid=(T // tm,),
        in_specs=_mix_specs(T, D, tm, wbp, w_out),
        out_specs=[pl.BlockSpec((tm, D), row), pl.BlockSpec((tm, D), row)],
        out_shape=[jax.ShapeDtypeStruct((T, D), F32), jax.ShapeDtypeStruct((T, D), BF16)],
        compiler_params=_params(("arbitrary",)),
    )(h, p, o, proj, proj, wbp, wba, w_out)


def _mix_bwd(dh, p, o, proj, wbp, wba, w_out, after, *, tm, name):
    T, D = dh.shape
    tm = min(tm, T)

    def body(dh_ref, p_ref, o_ref, glp_ref, gls_ref, wbp_ref, wba_ref, wout_ref, after_ref,
             dyp_ref, dys_ref, dp_ref, do_ref, dgl_ref):
        halves = (pl.ds(0, tm // 2), pl.ds(tm // 2, tm // 2))
        wbp, wba, wout = wbp_ref[...], wba_ref[...], wout_ref[...]
        products = [(_mm_nt(dh_ref[rows, :].astype(BF16), wout), _mm_nt(p_ref[rows, :].astype(BF16), wbp),
                     _mm_nt(o_ref[rows, :].astype(BF16), wba)) for rows in halves]
        for rows, (dm, yp, ys) in zip(halves, products):
            gp = jax.nn.sigmoid(glp_ref[rows, :])
            gs = jax.nn.sigmoid(gls_ref[rows, :])
            dyp = (dm * gp).astype(BF16)
            dys = (dm * gs).astype(BF16)
            dyp_ref[rows, :] = dyp
            dys_ref[rows, :] = dys
            dgl_ref[rows, :D] = (dm * yp * gp * (1.0 - gp)).astype(BF16)
            dgl_ref[rows, D:] = (dm * ys * gs * (1.0 - gs)).astype(BF16)
            dp_ref[rows, :] = _mm(dyp, wbp)
            do_ref[rows, :] = _mm(dys, wba)

    row = lambda i: (i, 0)
    return pl.pallas_call(
        body, name=name, grid=(T // tm,),
        in_specs=_mix_specs(T, D, tm, wbp, w_out) + [AFTER],
        out_specs=[pl.BlockSpec((tm, D), row), pl.BlockSpec((tm, D), row), pl.BlockSpec((tm, POOL_WIDTH), row),
                   pl.BlockSpec((tm, SB_WIDTH), row), pl.BlockSpec((tm, 2 * D), row)],
        out_shape=[jax.ShapeDtypeStruct((T, D), BF16), jax.ShapeDtypeStruct((T, D), BF16),
                   jax.ShapeDtypeStruct((T, POOL_WIDTH), F32), jax.ShapeDtypeStruct((T, SB_WIDTH), F32),
                   jax.ShapeDtypeStruct((T, 2 * D), BF16)],
        compiler_params=_params(("arbitrary",)),
    )(dh, p, o, proj, proj, wbp, wba, w_out, _in_hbm(after))


def _adamw_update(w, g, m, v):
    m_ = ADAM_B1 * m + (1.0 - ADAM_B1) * g
    v_ = ADAM_B2 * v + (1.0 - ADAM_B2) * (g * g)
    m_hat = m_ / (1.0 - ADAM_B1 ** ADAM_STEP)
    v_hat = v_ / (1.0 - ADAM_B2 ** ADAM_STEP)
    return -ADAM_LR * (m_hat / (jnp.sqrt(v_hat) + ADAM_EPS) + ADAM_WD * w), m_, v_


def _adamw(w, g, m, v, *, name):
    R, C = w.shape
    tr = _row_tile(R, C)

    def body(w_ref, g_ref, m_ref, v_ref, d_ref, nm_ref, nv_ref):
        d_ref[...], nm_ref[...], nv_ref[...] = _adamw_update(w_ref[...], g_ref[...], m_ref[...], v_ref[...])

    spec = pl.BlockSpec((tr, C), lambda i: (i, 0))
    return pl.pallas_call(
        body, name=name, grid=(R // tr,), in_specs=[spec] * 4, out_specs=[spec] * 3,
        out_shape=[jax.ShapeDtypeStruct((R, C), F32)] * 3,
        compiler_params=_params(("arbitrary",)),
    )(w, g, m, v)


def _position():
    return lax.axis_index("x"), lax.axis_index("y"), lax.axis_index("c")


def _all_gather(shards, *, name, collective_id):
    n = len(shards)
    n_copies = 9

    def body(*refs):
        ins, outs = refs[:n], refs[n:2 * n]
        send_sems, recv_sems, local_sems = refs[2 * n:]
        x, y, c = _position()
        me, sibling = (x, y, c), (x, y, 1 - c)
        x_nbr, y_nbr, diagonal = (1 - x, y, c), (x, 1 - y, c), (1 - x, 1 - y, c)
        other = lambda pos: (pos[0], pos[1], 1 - c)

        barrier = pltpu.get_barrier_semaphore()
        for peer in (sibling, x_nbr, y_nbr):
            pl.semaphore_signal(barrier, inc=1, device_id=peer, device_id_type=MESH)
        pl.semaphore_wait(barrier, 3)

        def block(a, pos, half=None):
            ref = outs[a].at[4 * pos[0] + 2 * pos[1] + pos[2]]
            rows = ref.shape[0] // 2
            return ref if half is None else ref.at[pl.ds(half * rows, rows)]

        def copy(a, k, pos, to, half=None, src=None):
            return pltpu.make_async_remote_copy(
                src_ref=block(a, pos, half) if src is None else src, dst_ref=block(a, pos, half),
                send_sem=send_sems.at[n_copies * a + k], recv_sem=recv_sems.at[n_copies * a + k],
                device_id=to, device_id_type=MESH)

        started = []
        for a in range(n):
            mine = pltpu.make_async_copy(ins[a], block(a, me), local_sems.at[a])
            mine.start()
            started.append(mine)
        sends = []
        for a in range(n):
            sends += [copy(a, 1, me, x_nbr, src=ins[a]), copy(a, 2, me, y_nbr, src=ins[a]),
                      copy(a, 0, me, sibling, src=ins[a])]
        for cp in sends:
            cp.start()

        def pass_on(copies):
            for cp in copies:
                cp.start()
                sends.append(cp)

        for a in range(n):
            copy(a, 1, x_nbr, me).wait_recv()
            pass_on([copy(a, 5, x_nbr, y_nbr, half=0), copy(a, 3, x_nbr, sibling)])
            copy(a, 2, y_nbr, me).wait_recv()
            pass_on([copy(a, 6, y_nbr, x_nbr, half=1), copy(a, 4, y_nbr, sibling)])
        for a in range(n):
            copy(a, 5, diagonal, me, half=0).wait_recv()
            pass_on([copy(a, 7, diagonal, sibling, half=0)])
            copy(a, 6, diagonal, me, half=1).wait_recv()
            pass_on([copy(a, 8, diagonal, sibling, half=1)])
        for a in range(n):
            copy(a, 0, sibling, me).wait_recv()
            copy(a, 3, other(x_nbr), me).wait_recv()
            copy(a, 4, other(y_nbr), me).wait_recv()
            copy(a, 7, other(diagonal), me, half=0).wait_recv()
            copy(a, 8, other(diagonal), me, half=1).wait_recv()
        for cp in sends:
            cp.wait_send()
        for cp in started:
            cp.wait()

    return pl.kernel(
        body, name=name,
        out_type=[jax.ShapeDtypeStruct((N_DEV,) + s.shape, s.dtype) for s in shards],
        mesh=plsc.ScalarSubcoreMesh(axis_name="sequencer", num_cores=1),
        scratch_types=[pltpu.SemaphoreType.DMA((n_copies * n,)), pltpu.SemaphoreType.DMA((n_copies * n,)),
                       pltpu.SemaphoreType.DMA((n,))],
        compiler_params=pltpu.CompilerParams(collective_id=collective_id),
    )(*shards)


def _chip_sums(group, *, name):
    n = len(group)
    shapes = [g.shape[1:] for g in group]

    def body(*refs):
        g_refs, partials, out_refs = refs[:n], refs[n:3 * n:2], refs[n + 1:3 * n:2]
        mines, theirs = refs[3 * n:5 * n:2], refs[3 * n + 1:5 * n:2]
        send_sems, recv_sems, local_sems = refs[5 * n:]
        x, y, c = _position()
        my_chip = 2 * x + y

        def swap(a, s):
            return pltpu.make_async_remote_copy(
                src_ref=g_refs[a].at[2 * s + (1 - c)], dst_ref=theirs[a].at[s],
                send_sem=send_sems.at[4 * a + s], recv_sem=recv_sems.at[4 * a + s],
                device_id=(x, y, 1 - c), device_id_type=MESH)

        def load(a, s):
            return pltpu.make_async_copy(g_refs[a].at[2 * s + c], mines[a].at[s], local_sems.at[4 * a + s])

        for a in range(n):
            for s in range(4):
                swap(a, s).start()
                load(a, s).start()

        for a, (R, C) in enumerate(shapes):
            rc = 128 if R % 128 == 0 else R

            def chip_sum(chip, rows):
                return mines[a][chip, rows, :].astype(F32) + theirs[a][chip, rows, :].astype(F32)

            for s in range(4):
                load(a, s).wait()
                swap(a, s).wait_recv()

                @pl.when(s == my_chip)
                def _():
                    @pl.loop(0, R // rc)
                    def _(t):
                        rows = pl.ds(pl.multiple_of(t * rc, rc), rc)
                        out_refs[a][rows, :] = chip_sum(s, rows)

                @pl.when(s != my_chip)
                def _():
                    @pl.loop(0, R // rc)
                    def _(t):
                        rows = pl.ds(pl.multiple_of(t * rc, rc), rc)
                        partials[a][(s ^ my_chip) - 1, rows, :] = chip_sum(s, rows).astype(BF16)

        for a in range(n):
            for s in range(4):
                swap(a, s).wait_send()

    vmem = pl.BlockSpec(memory_space=pltpu.VMEM)
    outs = pl.pallas_call(
        body, name=name,
        in_specs=[pl.BlockSpec(memory_space=pl.ANY)] * n, out_specs=[vmem] * (2 * n),
        out_shape=[shape for R, C in shapes
                   for shape in (jax.ShapeDtypeStruct((3, R, C), BF16), jax.ShapeDtypeStruct((R, C), F32))],
        scratch_shapes=[pltpu.VMEM((4, R, C), BF16) for R, C in shapes for _ in range(2)] + [
            pltpu.SemaphoreType.DMA((4 * n,)), pltpu.SemaphoreType.DMA((4 * n,)), pltpu.SemaphoreType.DMA((4 * n,))],
        compiler_params=_params(),
    )(*group)
    return [(outs[2 * a], outs[2 * a + 1]) for a in range(n)]


def _cross_chips(partials, *, name, collective_id):
    n = len(partials)

    def body(*refs):
        ins, outs = refs[:n], refs[n:2 * n]
        send_sems, recv_sems = refs[2 * n:]
        x, y, c = _position()
        my_chip = 2 * x + y
        peers = [((my_chip ^ j) // 2, (my_chip ^ j) % 2, c) for j in (1, 2, 3)]

        barrier = pltpu.get_barrier_semaphore()
        for peer in peers:
            pl.semaphore_signal(barrier, inc=1, device_id=peer, device_id_type=MESH)
        pl.semaphore_wait(barrier, 3)

        copies = [
            pltpu.make_async_remote_copy(
                src_ref=ins[a].at[j], dst_ref=outs[a].at[j],
                send_sem=send_sems.at[3 * a + j], recv_sem=recv_sems.at[3 * a + j],
                device_id=peers[j], device_id_type=MESH)
            for a in range(n) for j in range(3)]
        for cp in copies:
            cp.start()
        for cp in copies:
            cp.wait_recv()
        for cp in copies:
            cp.wait_send()

    return pl.kernel(
        body, name=name,
        out_type=[jax.ShapeDtypeStruct(p.shape, p.dtype) for p in partials],
        mesh=plsc.ScalarSubcoreMesh(axis_name="sequencer", num_cores=1),
        scratch_types=[pltpu.SemaphoreType.DMA((3 * n,)), pltpu.SemaphoreType.DMA((3 * n,))],
        compiler_params=pltpu.CompilerParams(collective_id=collective_id),
    )(*partials)


def _cross_chips_and_gather(partials, slab, *, name, collective_id):
    n = len(partials)

    def body(*refs):
        part_refs, slab_ref = refs[:n], refs[n]
        landed_refs, slabs_ref = refs[n + 1:2 * n + 1], refs[2 * n + 1]
        send_sems, recv_sems, local_sem = refs[2 * n + 2:]
        x, y, c = _position()
        me, my_chip = 4 * x + 2 * y + c, 2 * x + y
        others = [me ^ k for k in range(1, N_DEV)]
        ids = [(o // 4, (o // 2) % 2, o % 2) for o in others]

        barrier = pltpu.get_barrier_semaphore()
        for peer in ids:
            pl.semaphore_signal(barrier, inc=1, device_id=peer, device_id_type=MESH)
        pl.semaphore_wait(barrier, N_DEV - 1)

        mine = pltpu.make_async_copy(slab_ref, slabs_ref.at[me], local_sem)
        mine.start()
        sends = [
            pltpu.make_async_remote_copy(
                src_ref=part_refs[a].at[j], dst_ref=landed_refs[a].at[j],
                send_sem=send_sems.at[3 * a + j], recv_sem=recv_sems.at[3 * a + j],
                device_id=((my_chip ^ (j + 1)) // 2, (my_chip ^ (j + 1)) % 2, c), device_id_type=MESH)
            for a in range(n) for j in range(3)]
        sends += [
            pltpu.make_async_remote_copy(
                src_ref=slab_ref, dst_ref=slabs_ref.at[me],
                send_sem=send_sems.at[3 * n + k], recv_sem=recv_sems.at[3 * n + k],
                device_id=ids[k], device_id_type=MESH)
            for k in range(N_DEV - 1)]
        arrivals = sends[:3 * n] + [
            pltpu.make_async_remote_copy(
                src_ref=slab_ref, dst_ref=slabs_ref.at[others[k]],
                send_sem=send_sems.at[3 * n + k], recv_sem=recv_sems.at[3 * n + k],
                device_id=ids[k], device_id_type=MESH)
            for k in range(N_DEV - 1)]
        for cp in sends:
            cp.start()
        for cp in arrivals:
            cp.wait_recv()
        for cp in sends:
            cp.wait_send()
        mine.wait()

    n_sems = 3 * n + N_DEV - 1
    outs = pl.kernel(
        body, name=name,
        out_type=[jax.ShapeDtypeStruct(p.shape, p.dtype) for p in partials]
                 + [jax.ShapeDtypeStruct((N_DEV,) + slab.shape, slab.dtype)],
        mesh=plsc.ScalarSubcoreMesh(axis_name="sequencer", num_cores=1),
        scratch_types=[pltpu.SemaphoreType.DMA((n_sems,)), pltpu.SemaphoreType.DMA((n_sems,)), pltpu.SemaphoreType.DMA],
        compiler_params=pltpu.CompilerParams(collective_id=collective_id),
    )(*partials, slab)
    return outs[:n], outs[n]


def _sum_devices(gathered, after, *, name):
    _, R, C = gathered.shape

    def body(in_ref, after_ref, out_ref):
        total = in_ref[0]
        for d in range(1, N_DEV):
            total = total + in_ref[d]
        out_ref[...] = total

    return pl.pallas_call(
        body, name=name, grid=(1,),
        in_specs=[pl.BlockSpec((N_DEV, R, C), lambda i: (0, 0, 0)), AFTER],
        out_specs=pl.BlockSpec((R, C), lambda i: (0, 0)),
        out_shape=jax.ShapeDtypeStruct((R, C), F32),
        compiler_params=_params(("arbitrary",)),
    )(gathered, _in_hbm(after))


def _owner_sum_adamw(own, landed, w, m, v, after, *, transposed, name, group=None, into=()):
    H, R, C = w.shape
    tr = R // 2
    first_group = 0 if group is None else group

    def body(own_ref, landed_ref, w_ref, m_ref, v_ref, after_ref, *rest):
        g_ref, d_ref, nm_ref, nv_ref = rest[len(into):]
        total = own_ref[...]
        for j in range(3):
            total = total + landed_ref[j].astype(F32)
        if transposed:
            total = total.T
        g_ref[...] = total
        d_ref[...], nm_ref[...], nv_ref[...] = _adamw_update(w_ref[...], total, m_ref[...], v_ref[...])

    spec = pl.BlockSpec((None, tr, C), lambda h, i: (first_group + h, i, 0))
    if transposed:
        own_spec = pl.BlockSpec((None, C, tr), lambda h, i: (h, 0, i))
        landed_spec = pl.BlockSpec((3, None, C, tr), lambda h, i: (0, h, 0, i))
    else:
        own_spec = pl.BlockSpec((None, tr, C), lambda h, i: (h, i, 0))
        landed_spec = pl.BlockSpec((3, None, tr, C), lambda h, i: (0, h, i, 0))
    n_in = 6
    return pl.pallas_call(
        body, name=name, grid=(own.shape[0], R // tr),
        in_specs=[own_spec, landed_spec, spec, spec, spec, AFTER] + [pl.BlockSpec(memory_space=pl.ANY)] * len(into),
        out_specs=[spec] * 4,
        out_shape=[jax.ShapeDtypeStruct((H, R, C), F32)] * 4,
        input_output_aliases={n_in + j: j for j in range(len(into))},
        compiler_params=_params(("arbitrary", "arbitrary")),
    )(*[_in_hbm(a) for a in (own, landed, w, m, v, after)], *into)


def _local_step(x, target, norms, pool_w_group, pool_scale, wgu1, wd1, w_in, wbp, wba, w_out, wgu2, wd2, exchange):
    n1g, nmg, n2g, nfg = norms
    D = x.shape[1]
    gu1, hid1 = _ffn_up(x, n1g, wgu1, tm=1024, name="ffn1_up")
    h1 = _ffn_down(x, hid1, wd1, tm=512, name="ffn1_down")
    un, proj = _inproj_fwd(h1, nmg, w_in, tm=1024, name="inproj_fwd")
    p = _pool_fwd(proj, pool_w_group, pool_scale, name="pool_fwd")
    o, ltot = _attn_fwd(proj, name="attn_fwd")
    h2, m = _mix_fwd(h1, p, o, proj, wbp, wba, w_out, tm=512, name="mix_fwd")
    gu2, hid2 = _ffn_up(h2, n2g, wgu2, tm=1024, name="ffn2_up")
    h3 = _ffn_down(h2, hid2, wd2, tm=512, name="ffn2_down")
    dh3, df2, loss, d_nf = _loss_bwd(h3, target, nfg, tm=256, name="loss_bwd")

    dh2, d_n2, n2, dgu2 = _ffn_bwd(dh3, df2, h2, n2g, gu2, wgu2, wd2, df2, tm=512, name="ffn2_bwd")
    d_wd2 = _wgrad_down(hid2, df2, tk=WGRAD_TOKENS, name="ffn2_wgrad_down")
    d_wgu2 = _wgrad_gate_up(n2, dgu2, tk=WGRAD_TOKENS, name="ffn2_wgrad_gate_up")
    (g_wd2, g_wgu2), token = exchange("ffn2", [d_wd2.reshape(N_DEV, FF_SHARD_PAD, D), d_wgu2])

    dyp, dys, dp, do, dgl = _mix_bwd(dh2, p, o, proj, wbp, wba, w_out, token, tm=512, name="mix_bwd")
    d_wout = _wgrad_full(m, dh2, tk=WGRAD_TOKENS, name="wgrad_out")
    d_wbp = _wgrad_full(dyp, p, tk=WGRAD_TOKENS, name="wgrad_branch_pool")
    d_wba = _wgrad_full(dys, o, tk=WGRAD_TOKENS, name="wgrad_branch_attn")
    by_owner = lambda g: g.reshape(N_DEV, g.shape[0] // N_DEV, g.shape[1])
    (g_wbp, g_wba, g_wout), token = exchange("mix", [by_owner(d_wbp), by_owner(d_wba), by_owner(d_wout)])
    dxp, d_wgroup, d_scale = _pool_bwd(dp, proj, pool_w_group, pool_scale, name="pool_bwd")
    dq, dk, dv = _attn_bwd(proj, do, ltot, token, name="attn_bwd")
    dproj_parts = [dxp, dq, dk, dv, dgl]
    dh1, df1, d_nm = _inproj_bwd(dproj_parts, dh2, h1, nmg, w_in, tm=512, name="inproj_bwd")
    d_win = _wgrad_in(dproj_parts, un, name="wgrad_in")
    d_wd1 = _wgrad_down(hid1, df1, tk=WGRAD_TOKENS, name="ffn1_wgrad_down")
    (g_win, g_wd1, replicated_early), token = exchange(
        "w_in_ffn1_down", [d_win, d_wd1.reshape(N_DEV, FF_SHARD_PAD, D), d_nm, d_n2, d_nf, d_scale, d_wgroup, loss])

    dx, d_n1, n1, dgu1 = _ffn_bwd(dh1, df1, x, n1g, gu1, wgu1, wd1, token, tm=512, name="ffn1_bwd")
    d_wgu1_a = _wgrad_gate_up(n1, dgu1, tk=WGRAD_TOKENS, name="ffn1_wgrad_gate_up_a", part=0, parts=2)
    (g_wgu1_a, replicated_late), token = exchange("ffn1_gate_up_a", [d_wgu1_a, d_n1])
    d_wgu1_b = _wgrad_gate_up(n1, dgu1, tk=WGRAD_TOKENS, name="ffn1_wgrad_gate_up_b", part=1, parts=2)
    (g_wgu1_b,), token = exchange("last", [d_wgu1_b])
    g_wgu1 = (g_wgu1_a, g_wgu1_b)

    sharded = (g_wgu1, g_wd1, g_win, g_wbp, g_wba, g_wout, g_wgu2, g_wd2)
    return dx, sharded, (replicated_late, replicated_early), token


def _hidden_major(w):
    return jnp.swapaxes(w[0], 0, 1)


def _pad_gate_up(wt):
    d = wt.shape[1]
    wt = wt.astype(BF16).reshape(2, FF_SHARD, d)
    return jnp.pad(wt, ((0, 0), (0, FF_SHARD_PAD - FF_SHARD), (0, 0))).reshape(2 * FF_SHARD_PAD, d)


def _unpad_gate_up(gt):
    d = gt.shape[1]
    return gt.reshape(2, FF_SHARD_PAD, d)[:, :FF_SHARD].reshape(2 * FF_SHARD, d)


def _pad_down(w):
    return jnp.pad(w.astype(BF16), ((0, FF_SHARD_PAD - FF_SHARD), (0, 0)))


def kernel(x, ffn1_norm, ffn1_w_gate_up, ffn1_w_down, mix_norm, w_in, pool_w_group, pool_scale, w_branch_pool, w_branch_attn, w_out, ffn2_norm, ffn2_w_gate_up, ffn2_w_down, final_norm, loss_target, m_ffn1_norm, m_ffn1_w_gate_up, m_ffn1_w_down, m_mix_norm, m_w_in, m_pool_w_group, m_pool_scale, m_w_branch_pool, m_w_branch_attn, m_w_out, m_ffn2_norm, m_ffn2_w_gate_up, m_ffn2_w_down, m_final_norm, v_ffn1_norm, v_ffn1_w_gate_up, v_ffn1_w_down, v_mix_norm, v_w_in, v_pool_w_group, v_pool_scale, v_w_branch_pool, v_w_branch_attn, v_w_out, v_ffn2_norm, v_ffn2_w_gate_up, v_ffn2_w_down, v_final_norm):
    D = x.shape[-1]
    weights = dict(ffn1_norm=ffn1_norm, ffn1_w_gate_up=ffn1_w_gate_up, ffn1_w_down=ffn1_w_down, mix_norm=mix_norm,
                   w_in=w_in, pool_w_group=pool_w_group, pool_scale=pool_scale, w_branch_pool=w_branch_pool,
                   w_branch_attn=w_branch_attn, w_out=w_out, ffn2_norm=ffn2_norm, ffn2_w_gate_up=ffn2_w_gate_up,
                   ffn2_w_down=ffn2_w_down, final_norm=final_norm)
    first = dict(ffn1_norm=m_ffn1_norm, ffn1_w_gate_up=m_ffn1_w_gate_up, ffn1_w_down=m_ffn1_w_down,
                 mix_norm=m_mix_norm, w_in=m_w_in, pool_w_group=m_pool_w_group, pool_scale=m_pool_scale,
                 w_branch_pool=m_w_branch_pool, w_branch_attn=m_w_branch_attn, w_out=m_w_out,
                 ffn2_norm=m_ffn2_norm, ffn2_w_gate_up=m_ffn2_w_gate_up, ffn2_w_down=m_ffn2_w_down,
                 final_norm=m_final_norm)
    second = dict(ffn1_norm=v_ffn1_norm, ffn1_w_gate_up=v_ffn1_w_gate_up, ffn1_w_down=v_ffn1_w_down,
                  mix_norm=v_mix_norm, w_in=v_w_in, pool_w_group=v_pool_w_group, pool_scale=v_pool_scale,
                  w_branch_pool=v_w_branch_pool, w_branch_attn=v_w_branch_attn, w_out=v_w_out,
                  ffn2_norm=v_ffn2_norm, ffn2_w_gate_up=v_ffn2_w_gate_up, ffn2_w_down=v_ffn2_w_down,
                  final_norm=v_final_norm)
    order = list(weights)

    wgu1, = _all_gather([_pad_gate_up(_hidden_major(ffn1_w_gate_up))], name="all_gather_ffn1_gate_up", collective_id=0)
    wd1, = _all_gather([_pad_down(ffn1_w_down[0])], name="all_gather_ffn1_down", collective_id=10)
    transposed = lambda w: jnp.swapaxes(w[0], 0, 1).astype(BF16)
    win_g, = _all_gather([transposed(w_in)], name="all_gather_w_in", collective_id=1)
    wbp_g, wba_g = _all_gather([transposed(w_branch_pool), transposed(w_branch_attn)],
                               name="all_gather_branches", collective_id=2)
    wout_g, = _all_gather([w_out[0].astype(BF16)], name="all_gather_w_out", collective_id=11)
    wgu2, wd2 = _all_gather([_pad_gate_up(_hidden_major(ffn2_w_gate_up)), _pad_down(ffn2_w_down[0])],
                            name="all_gather_ffn2", collective_id=3)
    whole = lambda g: g.reshape(g.shape[0] * g.shape[1], g.shape[2])
    wd1, wd2, win_g, wbp_g, wba_g, wout_g = (whole(g) for g in (wd1, wd2, win_g, wbp_g, wba_g, wout_g))

    cross_ids = {"ffn2": 4, "mix": 5, "w_in_ffn1_down": 8, "ffn1_gate_up_a": 9, "last": 7}
    small = ["ffn1_norm", "mix_norm", "ffn2_norm", "final_norm", "pool_scale", "pool_w_group"]

    def tile_rows(a):
        a = a.reshape(-1, 128)
        return jnp.pad(a, ((0, -a.shape[0] % 8), (0, 0)))

    def exchange(tag, group):
        grads = [g for g in group if g.dtype == BF16]
        extras = [tile_rows(g) for g in group if g.dtype != BF16]
        sums = _chip_sums(grads, name="chip_sums_" + tag)
        partials = [s[0] for s in sums]
        handles = []
        if extras:
            landed, slabs = _cross_chips_and_gather(partials, jnp.concatenate(extras, axis=0),
                                                    name="cross_chips_" + tag, collective_id=cross_ids[tag])
            handles = [slabs]
        else:
            landed = _cross_chips(partials, name="cross_chips_" + tag, collective_id=cross_ids[tag])
        return [(s[1], l) for s, l in zip(sums, landed)] + handles, sums[-1][1]

    norms = (ffn1_norm, mix_norm, ffn2_norm, final_norm.reshape(1, D))
    dx, sharded, (slabs_late, slabs_early), last = _local_step(
        x[0], loss_target[0], norms, pool_w_group[0], pool_scale, wgu1, wd1, win_g, wbp_g, wba_g, wout_g, wgu2, wd2,
        exchange)
    names = ["ffn1_w_gate_up", "ffn1_w_down", "w_in", "w_branch_pool", "w_branch_attn", "w_out",
             "ffn2_w_gate_up", "ffn2_w_down"]
    handles = dict(zip(names, sharded))
    grads, delta, new_m, new_v = {}, {}, {}, {}
    loss_out = []

    def update_replicated(after):
        rows = [weights[k].size // 128 for k in small]
        padded_rows = [-(-r // 8) * 8 for r in rows]
        starts = [sum(padded_rows[:i]) for i in range(len(rows) + 1)]
        total = jnp.concatenate([_sum_devices(slabs_late, after, name="sum_replicated_late"),
                                 _sum_devices(slabs_early, after, name="sum_replicated_early")], axis=0)
        loss_out.append(total[starts[-1], 0])
        small_w = jnp.concatenate([tile_rows(weights[k]) for k in small], axis=0)
        small_m = jnp.concatenate([tile_rows(first[k]) for k in small], axis=0)
        small_v = jnp.concatenate([tile_rows(second[k]) for k in small], axis=0)
        small_out = _adamw(small_w, total[:starts[-1]], small_m, small_v, name="adamw_replicated")
        for name_, start, n_rows in zip(small, starts, rows):
            shape = weights[name_].shape
            grads[name_] = total[start:start + n_rows].reshape(shape)
            delta[name_], new_m[name_], new_v[name_] = (a[start:start + n_rows].reshape(shape) for a in small_out)
        return small_out[0]

    after = last
    for k in ("ffn2_w_down", "ffn2_w_gate_up", "w_branch_pool", "w_branch_attn", "w_out", "w_in", "ffn1_w_down",
              "ffn1_w_gate_up"):
        hidden_major = k.endswith("w_gate_up")
        view = _hidden_major if hidden_major else (lambda a: a[0])
        back = (lambda a: jnp.swapaxes(a, 0, 1)[None]) if hidden_major else (lambda a: a[None])
        groups = 2 if hidden_major else 1
        by_group = lambda a: a.reshape(a.shape[:-2] + (groups, a.shape[-2] // groups, a.shape[-1]))
        state = [by_group(view(a[k])) for a in (weights, first, second)]
        if isinstance(handles[k][0], tuple):
            (own_a, landed_a), (own_b, landed_b) = handles[k]
            out = _owner_sum_adamw(own_a[None], landed_a[:, None], *state, after, name="adamw_" + k + "_a",
                                   transposed=False, group=0)
            out = _owner_sum_adamw(own_b[None], landed_b[:, None], *state, update_replicated(out[1]),
                                   name="adamw_" + k + "_b", transposed=False, group=1, into=out)
        else:
            own, landed = handles[k]
            out = _owner_sum_adamw(by_group(own), by_group(landed), *state, after, name="adamw_" + k,
                                   transposed=k in ("w_in", "w_branch_pool", "w_branch_attn"))
        after = out[1]
        grads[k], delta[k], new_m[k], new_v[k] = (back(a.reshape(-1, a.shape[-1])) for a in out)

    return (loss_out[0], dx[None], *[grads[k] for k in order], *[delta[k] for k in order],
            *[new_m[k] for k in order], *[new_v[k] for k in order])
```

```python
import jax
import jax.numpy as jnp
from jax import lax
from jax.experimental import pallas as pl
from jax.experimental.pallas import tpu as pltpu
from jax.experimental.pallas import tpu_sc as plsc

F32 = jnp.float32
BF16 = jnp.bfloat16
MESH = pl.DeviceIdType.MESH

RMS_EPS = 1e-6
N_DEV = 8
N_HEADS = 8
HEAD_DIM = 64
HEAD_PAIR = 2 * HEAD_DIM
POOL_WINDOWS = (2, 4, 8, 16)
POOL_GROUP = 128
POOL_WIDTH = 512
SB_WIDTH = 512
FF_SHARD = 352
FF_SHARD_PAD = 384
ATTN_K_BLOCK = 256
ATTN_Q_BLOCK_FWD = 512
ATTN_Q_BLOCK_BWD = 256
ATTN_SCALE = 0.125

ADAM_LR = 0.001
ADAM_B1 = 0.9
ADAM_B2 = 0.999
ADAM_EPS = 1e-08
ADAM_WD = 0.01
ADAM_STEP = 10

VMEM_LIMIT = 48 << 20
WGRAD_TOKENS = 2048


def _params(dims=None):
    return pltpu.CompilerParams(dimension_semantics=dims, vmem_limit_bytes=VMEM_LIMIT)


def _mm(a, b):
    return jnp.dot(a, b, preferred_element_type=F32)


def _mm_nt(a, b):
    return lax.dot_general(a, b, (((1,), (1,)), ((), ())), preferred_element_type=F32)


def _mm_tn(a, b):
    return lax.dot_general(a, b, (((0,), (0,)), ((), ())), preferred_element_type=F32)


def _row_tile(rows, cols):
    limit = max(8, (512 * 1024) // cols)
    return max(t for t in range(8, rows + 1, 8) if rows % t == 0 and (t <= limit or t == 8))


def _rstd(xf):
    return lax.rsqrt(jnp.mean(xf * xf, axis=-1, keepdims=True) + RMS_EPS)


def _rms_bwd(xf, gain, dn):
    r = _rstd(xf)
    xh = xf * r
    dgain = jnp.sum(dn * xh, axis=0, keepdims=True)
    dxh = dn * gain
    dx = r * (dxh - xh * jnp.mean(dxh * xh, axis=-1, keepdims=True))
    return dx, dgain


def _ffn_up(x, gain, wgu, *, tm, name):
    T, D = x.shape
    tm = min(tm, T)
    nb, bw = wgu.shape[0] // 2, wgu.shape[1]

    def body(x_ref, gain_ref, wg_ref, wu_ref, gu_ref, hid_ref, n_scr):
        @pl.when(pl.program_id(1) == 0)
        def _():
            xf = x_ref[...]
            n_scr[...] = (xf * _rstd(xf) * gain_ref[...]).astype(BF16)

        halves = (pl.ds(0, tm // 2), pl.ds(tm // 2, tm // 2))
        wg, wu = wg_ref[...], wu_ref[...]
        gus = [(_mm_nt(n_scr[rows, :], wg), _mm_nt(n_scr[rows, :], wu)) for rows in halves]
        for rows, (g, u) in zip(halves, gus):
            gu_ref[0, rows, :] = g.astype(BF16)
            gu_ref[1, rows, :] = u.astype(BF16)
            hid_ref[rows, :] = (g * jax.nn.sigmoid(g) * u).astype(BF16)

    return pl.pallas_call(
        body, name=name, grid=(T // tm, nb),
        in_specs=[
            pl.BlockSpec((tm, D), lambda i, j: (i, 0)),
            pl.BlockSpec((1, D), lambda i, j: (0, 0)),
            pl.BlockSpec((None, bw, D), lambda i, j: (j, 0, 0)),
            pl.BlockSpec((None, bw, D), lambda i, j: (j + nb, 0, 0)),
        ],
        out_specs=[
            pl.BlockSpec((2, tm, bw), lambda i, j: (0, i, j)),
            pl.BlockSpec((tm, bw), lambda i, j: (i, j)),
        ],
        out_shape=[jax.ShapeDtypeStruct((2, T, nb * bw), BF16), jax.ShapeDtypeStruct((T, nb * bw), BF16)],
        scratch_shapes=[pltpu.VMEM((tm, D), BF16)],
        compiler_params=_params(("arbitrary", "arbitrary")),
    )(x, gain, wgu, wgu)


def _ffn_down(x, hid, wd, *, tm, name):
    T, D = x.shape
    tm = min(tm, T)
    F = hid.shape[1]

    def body(x_ref, hid_ref, wd_ref, h_ref):
        h_ref[...] = x_ref[...] + 0.5 * _mm(hid_ref[...], wd_ref[...])

    return pl.pallas_call(
        body, name=name, grid=(T // tm,),
        in_specs=[
            pl.BlockSpec((tm, D), lambda i: (i, 0)),
            pl.BlockSpec((tm, F), lambda i: (i, 0)),
            pl.BlockSpec((F, D), lambda i: (0, 0)),
        ],
        out_specs=pl.BlockSpec((tm, D), lambda i: (i, 0)),
        out_shape=jax.ShapeDtypeStruct((T, D), F32),
        compiler_params=_params(("arbitrary",)),
    )(x, hid, wd)


AFTER = pl.BlockSpec(memory_space=pltpu.HBM)


def _in_hbm(token):
    return pltpu.with_memory_space_constraint(token, pltpu.HBM)


def _ffn_bwd(dh, df, x, gain, gu, wgu, wd, after, *, tm, name):
    T, D = x.shape
    tm = min(tm, T)
    nb, bw = wgu.shape[0] // 2, wgu.shape[1]

    def body(dh_ref, df_ref, x_ref, gain_ref, gu_ref, wg_ref, wu_ref, wd_ref, after_ref,
             dx_ref, dgain_ref, n_ref, dgu_ref, dn_acc):
        i, j = pl.program_id(0), pl.program_id(1)

        @pl.when(j == 0)
        def _():
            xf = x_ref[...]
            n_ref[...] = (xf * _rstd(xf) * gain_ref[...]).astype(BF16)
            dn_acc[...] = jnp.zeros_like(dn_acc)

        @pl.when((i == 0) & (j == 0))
        def _():
            dgain_ref[...] = jnp.zeros_like(dgain_ref)

        halves = (pl.ds(0, tm // 2), pl.ds(tm // 2, tm // 2))
        wd, wg, wu = wd_ref[...], wg_ref[...], wu_ref[...]
        dhids = [_mm_nt(df_ref[rows, :], wd) for rows in halves]
        for rows, dhid in zip(halves, dhids):
            g = gu_ref[0, rows, :].astype(F32)
            u = gu_ref[1, rows, :].astype(F32)
            s = jax.nn.sigmoid(g)
            silu = g * s
            dg = (dhid * u * (s * (1.0 + g * (1.0 - s)))).astype(BF16)
            du = (dhid * silu).astype(BF16)
            dgu_ref[0, rows, :] = dg
            dgu_ref[1, rows, :] = du
            dn_acc[rows, :] += _mm(dg, wg) + _mm(du, wu)

        @pl.when(j == nb - 1)
        def _():
            dx, dgain = _rms_bwd(x_ref[...], gain_ref[...], dn_acc[...])
            dx_ref[...] = dh_ref[...] + dx
            dgain_ref[...] += dgain

    row = lambda i, j: (i, 0)
    return pl.pallas_call(
        body, name=name, grid=(T // tm, nb),
        in_specs=[
            pl.BlockSpec((tm, D), row),
            pl.BlockSpec((tm, D), row),
            pl.BlockSpec((tm, D), row),
            pl.BlockSpec((1, D), lambda i, j: (0, 0)),
            pl.BlockSpec((2, tm, bw), lambda i, j: (0, i, j)),
            pl.BlockSpec((None, bw, D), lambda i, j: (j, 0, 0)),
            pl.BlockSpec((None, bw, D), lambda i, j: (j + nb, 0, 0)),
            pl.BlockSpec((bw, D), lambda i, j: (j, 0)),
            AFTER,
        ],
        out_specs=[
            pl.BlockSpec((tm, D), row),
            pl.BlockSpec((1, D), lambda i, j: (0, 0)),
            pl.BlockSpec((tm, D), row),
            pl.BlockSpec((2, tm, bw), lambda i, j: (0, i, j)),
        ],
        out_shape=[
            jax.ShapeDtypeStruct((T, D), F32),
            jax.ShapeDtypeStruct((1, D), F32),
            jax.ShapeDtypeStruct((T, D), BF16),
            jax.ShapeDtypeStruct((2, T, nb * bw), BF16),
        ],
        scratch_shapes=[pltpu.VMEM((tm, D), F32)],
        compiler_params=_params(("arbitrary", "arbitrary")),
    )(dh, df, x, gain, gu, wgu, wgu, wd, _in_hbm(after))


def _wgrad(a, b, *, grid, a_spec, b_spec, out_spec, out_shape, acc_shape, name):
    nk = grid[2]

    def body(a_ref, b_ref, o_ref, acc):
        k = pl.program_id(2)

        @pl.when(k == 0)
        def _():
            acc[...] = jnp.zeros_like(acc)

        acc[...] += _mm_tn(a_ref[...].astype(BF16), b_ref[...].astype(BF16))

        @pl.when(k == nk - 1)
        def _():
            o_ref[...] = acc[...].astype(o_ref.dtype)

    return pl.pallas_call(
        body, name=name, grid=grid, in_specs=[a_spec, b_spec], out_specs=out_spec,
        out_shape=jax.ShapeDtypeStruct(out_shape, BF16),
        scratch_shapes=[pltpu.VMEM(acc_shape, F32)],
        compiler_params=_params(("arbitrary", "arbitrary", "arbitrary")),
    )(a, b)


def _wgrad_gate_up(n, dgu, *, tk, name, part=0, parts=1):
    T, D = n.shape
    tk = min(tk, T)
    owner_rows = FF_SHARD_PAD * 2
    nb = dgu.shape[2] // owner_rows
    bw = owner_rows // parts
    return _wgrad(
        dgu, n, grid=(2 * nb, 1, T // tk), name=name,
        a_spec=pl.BlockSpec((None, tk, bw), lambda m, c, k: (m // nb, k, parts * (m % nb) + part)),
        b_spec=pl.BlockSpec((tk, D), lambda m, c, k: (k, 0)),
        out_spec=pl.BlockSpec((None, bw, D), lambda m, c, k: (m, 0, 0)),
        out_shape=(2 * nb, bw, D), acc_shape=(bw, D))


def _wgrad_down(hid, df, *, tk, name):
    T, D = df.shape
    tk = min(tk, T)
    bw = FF_SHARD_PAD * 2
    nb = hid.shape[1] // bw
    return _wgrad(
        hid, df, grid=(nb, 1, T // tk), name=name,
        a_spec=pl.BlockSpec((tk, bw), lambda m, c, k: (k, m)),
        b_spec=pl.BlockSpec((tk, D), lambda m, c, k: (k, 0)),
        out_spec=pl.BlockSpec((bw, D), lambda m, c, k: (m, 0)),
        out_shape=(nb * bw, D), acc_shape=(bw, D))


def _wgrad_in(dparts, un, *, name):
    T, D = un.shape
    bw = sum(p.shape[1] for p in dparts) // N_DEV
    first = [sum(p.shape[1] for p in dparts[:i]) // bw for i in range(len(dparts) + 1)]

    def body(*refs):
        dp_refs, un_ref, o_ref = refs[:-2], refs[-2], refs[-1]
        m = pl.program_id(0)
        for dp_ref, lo, hi in zip(dp_refs, first[:-1], first[1:]):
            @pl.when((m >= lo) & (m < hi))
            def _():
                o_ref[...] = _mm_tn(dp_ref[...], un_ref[...]).astype(o_ref.dtype)

    def piece_spec(lo, hi):
        return pl.BlockSpec((T, bw), lambda m: (0, jnp.clip(m - lo, 0, hi - lo - 1)))

    return pl.pallas_call(
        body, name=name, grid=(N_DEV,),
        in_specs=[piece_spec(lo, hi) for lo, hi in zip(first[:-1], first[1:])] + [pl.BlockSpec((T, D), lambda m: (0, 0))],
        out_specs=pl.BlockSpec((None, bw, D), lambda m: (m, 0, 0)),
        out_shape=jax.ShapeDtypeStruct((N_DEV, bw, D), BF16),
        compiler_params=_params(("arbitrary",)),
    )(*dparts, un)


def _wgrad_full(a, b, *, tk, name):
    T, M = a.shape
    tk = min(tk, T)
    N = b.shape[1]
    return _wgrad(
        a, b, grid=(1, 1, T // tk), name=name,
        a_spec=pl.BlockSpec((tk, M), lambda m, c, k: (k, 0)),
        b_spec=pl.BlockSpec((tk, N), lambda m, c, k: (k, 0)),
        out_spec=pl.BlockSpec((M, N), lambda m, c, k: (0, 0)), out_shape=(M, N), acc_shape=(M, N))


def _loss_bwd(h, target, gain, *, tm, name):
    T, D = h.shape
    tm = min(tm, T)

    def body(h_ref, t_ref, gain_ref, dh_ref, df_ref, loss_ref, dgain_ref):
        @pl.when(pl.program_id(0) == 0)
        def _():
            loss_ref[...] = jnp.zeros_like(loss_ref)
            dgain_ref[...] = jnp.zeros_like(dgain_ref)

        xf = h_ref[...]
        gain = gain_ref[...]
        err = xf * _rstd(xf) * gain - t_ref[...]
        loss_ref[...] += 0.5 * jnp.sum(jnp.mean(err * err, axis=-1, keepdims=True), axis=0, keepdims=True)
        dx, dgain = _rms_bwd(xf, gain, err * (1.0 / D))
        dh_ref[...] = dx
        df_ref[...] = (0.5 * dx).astype(BF16)
        dgain_ref[...] += dgain

    row = lambda i: (i, 0)
    fixed = lambda i: (0, 0)
    return pl.pallas_call(
        body, name=name, grid=(T // tm,),
        in_specs=[pl.BlockSpec((tm, D), row), pl.BlockSpec((tm, D), row), pl.BlockSpec((1, D), fixed)],
        out_specs=[pl.BlockSpec((tm, D), row), pl.BlockSpec((tm, D), row), pl.BlockSpec((1, 128), fixed),
                   pl.BlockSpec((1, D), fixed)],
        out_shape=[jax.ShapeDtypeStruct((T, D), F32), jax.ShapeDtypeStruct((T, D), BF16),
                   jax.ShapeDtypeStruct((1, 128), F32), jax.ShapeDtypeStruct((1, D), F32)],
        compiler_params=_params(("arbitrary",)),
    )(h, target, gain)


def _inproj_fwd(h, gain, w_in_t, *, tm, name):
    T, D = h.shape
    tm = min(tm, T)
    bn = D
    nb = w_in_t.shape[0] // bn

    def body(h_ref, gain_ref, wt_ref, un_ref, proj_ref):
        @pl.when(pl.program_id(1) == 0)
        def _():
            xf = h_ref[...]
            un_ref[...] = (xf * _rstd(xf) * gain_ref[...]).astype(BF16)

        proj_ref[...] = _mm_nt(un_ref[...], wt_ref[...])

    return pl.pallas_call(
        body, name=name, grid=(T // tm, nb),
        in_specs=[
            pl.BlockSpec((tm, D), lambda i, j: (i, 0)),
            pl.BlockSpec((1, D), lambda i, j: (0, 0)),
            pl.BlockSpec((bn, D), lambda i, j: (j, 0)),
        ],
        out_specs=[pl.BlockSpec((tm, D), lambda i, j: (i, 0)), pl.BlockSpec((tm, bn), lambda i, j: (i, j))],
        out_shape=[jax.ShapeDtypeStruct((T, D), BF16), jax.ShapeDtypeStruct((T, nb * bn), F32)],
        compiler_params=_params(("arbitrary", "arbitrary")),
    )(h, gain, w_in_t)


def _inproj_bwd(dparts, dh, h, gain, w_in_t, *, tm, name):
    T, D = h.shape
    tm = min(tm, T)
    n = len(dparts)
    widths = [p.shape[1] for p in dparts]
    starts = [sum(widths[:i]) for i in range(n)]

    def body(*refs):
        dp_refs = refs[:n]
        dh_ref, h_ref, gain_ref, wt_ref, dx_ref, df_ref, dgain_ref = refs[n:]

        @pl.when(pl.program_id(0) == 0)
        def _():
            dgain_ref[...] = jnp.zeros_like(dgain_ref)

        dn = sum(_mm(dp_ref[...], wt_ref[start:start + width, :])
                 for dp_ref, start, width in zip(dp_refs, starts, widths))
        dx, dgain = _rms_bwd(h_ref[...], gain_ref[...], dn)
        dh_in = dh_ref[...] + dx
        dx_ref[...] = dh_in
        df_ref[...] = (0.5 * dh_in).astype(BF16)
        dgain_ref[...] += dgain

    row = lambda i: (i, 0)
    fixed = lambda i: (0, 0)
    return pl.pallas_call(
        body, name=name, grid=(T // tm,),
        in_specs=[pl.BlockSpec((tm, width), row) for width in widths] + [
            pl.BlockSpec((tm, D), row),
            pl.BlockSpec((tm, D), row),
            pl.BlockSpec((1, D), fixed),
            pl.BlockSpec(w_in_t.shape, fixed),
        ],
        out_specs=[pl.BlockSpec((tm, D), row), pl.BlockSpec((tm, D), row), pl.BlockSpec((1, D), fixed)],
        out_shape=[jax.ShapeDtypeStruct((T, D), F32), jax.ShapeDtypeStruct((T, D), BF16),
                   jax.ShapeDtypeStruct((1, D), F32)],
        compiler_params=_params(("arbitrary",)),
    )(*dparts, dh, h, gain, w_in_t)


def _window_sum(x, row, doublings, *, backward):
    T = x.shape[0]
    s = x
    for k in range(doublings):
        sh = 1 << k
        if backward:
            s = s + jnp.where(row < T - sh, pltpu.roll(s, T - sh, 0), 0.0)
        else:
            s = s + jnp.where(row >= sh, pltpu.roll(s, sh, 0), 0.0)
    return s


def _pool_fwd(proj, w_group, scale, *, name):
    T = proj.shape[0]

    def body(xp_ref, w_ref, scale_ref, p_ref):
        row = lax.broadcasted_iota(jnp.int32, (T, POOL_GROUP), 0)
        for gi, window in enumerate(POOL_WINDOWS):
            cols = slice(gi * POOL_GROUP, (gi + 1) * POOL_GROUP)
            x = xp_ref[:, cols]
            inv_count = 1.0 / jnp.minimum(row + 1, window).astype(F32)
            yc = _window_sum(x, row, gi + 1, backward=False) * inv_count - x
            pre = _mm(yc.astype(BF16), w_ref[gi].astype(BF16))
            p_ref[:, cols] = pre * scale_ref[:, cols]

    return pl.pallas_call(
        body, name=name, grid=(1,),
        in_specs=[
            pl.BlockSpec((T, POOL_WIDTH), lambda i: (0, 0)),
            pl.BlockSpec(w_group.shape, lambda i: (0, 0, 0)),
            pl.BlockSpec((1, POOL_WIDTH), lambda i: (0, 0)),
        ],
        out_specs=pl.BlockSpec((T, POOL_WIDTH), lambda i: (0, 0)),
        out_shape=jax.ShapeDtypeStruct((T, POOL_WIDTH), F32),
        compiler_params=_params(("arbitrary",)),
    )(proj, w_group, scale)


def _pool_bwd(dp, proj, w_group, scale, *, name):
    T = proj.shape[0]

    def body(dp_ref, xp_ref, w_ref, scale_ref, dxp_ref, dw_ref, dscale_ref):
        row = lax.broadcasted_iota(jnp.int32, (T, POOL_GROUP), 0)
        for gi, window in enumerate(POOL_WINDOWS):
            cols = slice(gi * POOL_GROUP, (gi + 1) * POOL_GROUP)
            x = xp_ref[:, cols]
            inv_count = 1.0 / jnp.minimum(row + 1, window).astype(F32)
            yc = (_window_sum(x, row, gi + 1, backward=False) * inv_count - x).astype(BF16)
            w = w_ref[gi].astype(BF16)
            pre = _mm(yc, w)
            dpg = dp_ref[:, cols]
            dscale_ref[:, cols] = jnp.sum(dpg * pre, axis=0, keepdims=True)
            dpre = (dpg * scale_ref[:, cols]).astype(BF16)
            dw_ref[gi] = _mm_tn(yc, dpre)
            dyc = _mm_nt(dpre, w)
            dxp_ref[:, cols] = (_window_sum(dyc * inv_count, row, gi + 1, backward=True) - dyc).astype(BF16)

    return pl.pallas_call(
        body, name=name, grid=(1,),
        in_specs=[
            pl.BlockSpec((T, POOL_WIDTH), lambda i: (0, 0)),
            pl.BlockSpec((T, POOL_WIDTH), lambda i: (0, 0)),
            pl.BlockSpec(w_group.shape, lambda i: (0, 0, 0)),
            pl.BlockSpec((1, POOL_WIDTH), lambda i: (0, 0)),
        ],
        out_specs=[
            pl.BlockSpec((T, POOL_WIDTH), lambda i: (0, 0)),
            pl.BlockSpec(w_group.shape, lambda i: (0, 0, 0)),
            pl.BlockSpec((1, POOL_WIDTH), lambda i: (0, 0)),
        ],
        out_shape=[jax.ShapeDtypeStruct((T, POOL_WIDTH), BF16), jax.ShapeDtypeStruct(w_group.shape, F32),
                   jax.ShapeDtypeStruct((1, POOL_WIDTH), F32)],
        compiler_params=_params(("arbitrary",)),
    )(dp, proj, w_group, scale)


ATTN_STRIP = 32


def _log_sigmoids(z):
    lb = jnp.minimum(z, 0.0) - jnp.log(1.0 + jnp.exp(-jnp.abs(z)))
    return lb, lb - z


def _transposed_blocks(x_ref, blocks_scr, tq):
    for b in range(blocks_scr.shape[0]):
        blocks_scr[b] = x_ref[b * tq:(b + 1) * tq, :].T.astype(BF16)


def _split_bf16(x):
    hi = x.astype(BF16)
    return hi, (x - hi.astype(F32)).astype(BF16)


def _strips(n):
    return [slice(i, i + ATTN_STRIP) for i in range(0, n, ATTN_STRIP)]


def _rows(parts):
    return jnp.concatenate(parts, axis=0)


def _attn_specs(T, tq):
    q_col = POOL_WIDTH // HEAD_PAIR
    k_col = q_col + SB_WIDTH // HEAD_PAIR
    v_col = k_col + SB_WIDTH // HEAD_PAIR
    return [
        pl.BlockSpec((tq, HEAD_PAIR), lambda p, i: (i, q_col + p)),
        pl.BlockSpec((T, HEAD_PAIR), lambda p, i: (0, k_col + p)),
        pl.BlockSpec((T, HEAD_PAIR), lambda p, i: (0, v_col + p)),
    ]


def _attn_fwd(proj, *, name):
    T = proj.shape[0]
    tk = min(ATTN_K_BLOCK, T)
    tq = min(ATTN_Q_BLOCK_FWD, T)
    diagonal_blocks = tq // tk

    def body(q_ref, k_ref, v_ref, o_ref, lt_ref, kt_scr, vb_scr):
        qi = pl.program_id(1)

        @pl.when(qi == 0)
        def _():
            _transposed_blocks(k_ref, kt_scr, tk)
            vb_scr[...] = v_ref[...].astype(BF16)

        head0 = lax.broadcasted_iota(jnp.int32, (tq, HEAD_PAIR), 1) < HEAD_DIM
        q = q_ref[...] * ATTN_SCALE
        qs = (jnp.where(head0, q, 0.0).astype(BF16), jnp.where(head0, 0.0, q).astype(BF16))
        r = lax.broadcasted_iota(jnp.int32, (tq, tk), 0)
        c = lax.broadcasted_iota(jnp.int32, (tq, tk), 1)
        later = (r[:tk] > c[:tk]).astype(BF16)
        later2 = _rows([later, later])
        causal = lambda d: (lambda rows: c[rows] + d * tk < r[rows])
        strips = _strips(tq)

        def log_terms(z, valid):
            lbs, his, los, sums = [], [], [], []
            for rows in strips:
                lb, lm = _log_sigmoids(z[rows])
                if valid is not None:
                    lm = jnp.where(valid(rows), lm, 0.0)
                hi, lo = _split_bf16(lm)
                lbs.append(lb)
                his.append(hi)
                los.append(lo)
                sums.append(jnp.sum(lm, axis=1, keepdims=True))
            return lbs, jnp.concatenate([_rows(his), _rows(los)], axis=1), _rows(sums)

        def weights(lbs, run, after, valid):
            parts = []
            for rows, lb in zip(strips, lbs):
                a = jnp.exp(lb + run[rows] + after[rows])
                if valid is not None:
                    a = jnp.where(valid(rows), a, 0.0)
                parts.append(a.astype(BF16))
            return _rows(parts)

        def block(kj, carry, valid):
            kt = kt_scr[kj]
            vb = vb_scr[pl.ds(pl.multiple_of(kj * tk, tk), tk), :]
            run0, o0, run1, o1 = carry
            z0 = _mm(qs[0], kt)
            z1 = _mm(qs[1], kt)
            lbs0, split0, sums0 = log_terms(z0, valid)
            after0 = _mm(split0, later2)
            lbs1, split1, sums1 = log_terms(z1, valid)
            after1 = _mm(split1, later2)
            o0 = o0 + _mm(weights(lbs0, run0, after0, valid), vb)
            o1 = o1 + _mm(weights(lbs1, run1, after1, valid), vb)
            return run0 + sums0, o0, run1 + sums1, o1

        zero = (jnp.zeros((tq, 1), F32), jnp.zeros((tq, HEAD_PAIR), F32))
        first = diagonal_blocks * qi
        carry = zero + zero
        for d in reversed(range(diagonal_blocks)):
            carry = block(first + d, carry, causal(d))
        carry = lax.fori_loop(0, first, lambda it, cr: block(first - 1 - it, cr, None), carry)
        o_ref[...] = jnp.where(head0, carry[1], carry[3])
        lt_ref[...] = jnp.where(head0, carry[0], carry[2])

    out_spec = pl.BlockSpec((tq, HEAD_PAIR), lambda p, i: (i, p))
    return pl.pallas_call(
        body, name=name, grid=(N_HEADS // 2, T // tq),
        in_specs=_attn_specs(T, tq), out_specs=[out_spec, out_spec],
        out_shape=[jax.ShapeDtypeStruct((T, SB_WIDTH), F32), jax.ShapeDtypeStruct((T, SB_WIDTH), F32)],
        scratch_shapes=[pltpu.VMEM((T // tk, HEAD_PAIR, tk), BF16), pltpu.VMEM((T, HEAD_PAIR), BF16)],
        compiler_params=_params(("arbitrary", "arbitrary")),
    )(proj, proj, proj)


def _attn_bwd(proj, do, ltot, after, *, name):
    T = proj.shape[0]
    tk = min(ATTN_K_BLOCK, T)
    tq = min(ATTN_Q_BLOCK_BWD, T)
    diagonal_blocks = tq // tk

    def body(q_ref, k_ref, v_ref, do_ref, lt_ref, after_ref, dq_ref, dk_ref, dv_ref,
             kb_scr, kt_scr, vt_scr, dkt_ref, dvt_ref):
        qi = pl.program_id(1)

        @pl.when(qi == 0)
        def _():
            kb_scr[...] = k_ref[...].astype(BF16)
            _transposed_blocks(k_ref, kt_scr, tk)
            _transposed_blocks(v_ref, vt_scr, tk)
            dkt_ref[...] = jnp.zeros_like(dkt_ref)
            dvt_ref[...] = jnp.zeros_like(dvt_ref)

        head0 = lax.broadcasted_iota(jnp.int32, (tq, HEAD_PAIR), 1) < HEAD_DIM
        q, do_, lt = q_ref[...] * ATTN_SCALE, do_ref[...], lt_ref[...]
        qs = (jnp.where(head0, q, 0.0).astype(BF16), jnp.where(head0, 0.0, q).astype(BF16))
        q_heads = (jnp.where(head0, q, 0.0), jnp.where(head0, 0.0, q))
        do_heads = (jnp.where(head0, do_, 0.0), jnp.where(head0, 0.0, do_))
        dos = tuple(d.astype(BF16) for d in do_heads)
        qts = tuple(x.T.astype(BF16) for x in q_heads)
        dots = tuple(d.T.astype(BF16) for d in do_heads)
        lts = (jnp.max(jnp.where(head0, lt, -jnp.inf), axis=1, keepdims=True),
               jnp.max(jnp.where(head0, -jnp.inf, lt), axis=1, keepdims=True))
        r = lax.broadcasted_iota(jnp.int32, (tq, tk), 0)
        c = lax.broadcasted_iota(jnp.int32, (tq, tk), 1)
        upto = (r[:tk] <= c[:tk]).astype(BF16)
        before = (r[:tk] < c[:tk]).astype(BF16)
        upto2, before2 = _rows([upto, upto]), _rows([before, before])
        causal = lambda d: (lambda rows: c[rows] + d * tk < r[rows])
        strips = _strips(tq)

        def log_terms(z, valid):
            lbs, his, los, sums = [], [], [], []
            for rows in strips:
                lb, lm = _log_sigmoids(z[rows])
                if valid is not None:
                    lm = jnp.where(valid(rows), lm, 0.0)
                hi, lo = _split_bf16(lm)
                lbs.append(lb)
                his.append(hi)
                los.append(lo)
                sums.append(jnp.sum(lm, axis=1, keepdims=True))
            return lbs, jnp.concatenate([_rows(his), _rows(los)], axis=1), _rows(sums)

        def weights(lbs, rest, lm_upto, da, valid):
            a_parts, es, his, los, sums = [], [], [], [], []
            for rows, lb in zip(strips, lbs):
                a = jnp.exp(lb + (rest[rows] - lm_upto[rows]))
                if valid is not None:
                    a = jnp.where(valid(rows), a, 0.0)
                e = da[rows] * a
                hi, lo = _split_bf16(e)
                a_parts.append(a.astype(BF16))
                es.append(e)
                his.append(hi)
                los.append(lo)
                sums.append(jnp.sum(e, axis=1, keepdims=True))
            return _rows(a_parts), es, jnp.concatenate([_rows(his), _rows(los)], axis=1), _rows(sums)

        def score_grads(lbs, es, run_e, e_before, valid):
            parts = []
            for rows, lb, e in zip(strips, lbs, es):
                beta = jnp.exp(lb)
                dz = e * (1.0 - beta) - (run_e[rows] + e_before[rows]) * beta
                if valid is not None:
                    dz = jnp.where(valid(rows), dz, 0.0)
                parts.append(dz.astype(BF16))
            return _rows(parts)

        def block(kj, carry, valid):
            off = pl.multiple_of(kj * tk, tk)
            kb, kt, vt = kb_scr[pl.ds(off, tk), :], kt_scr[kj], vt_scr[kj]
            run_lm0, run_e0, dq0, run_lm1, run_e1, dq1 = carry
            z0, da0 = _mm(qs[0], kt), _mm(dos[0], vt)
            z1, da1 = _mm(qs[1], kt), _mm(dos[1], vt)
            lbs0, split0, lm_sums0 = log_terms(z0, valid)
            lm_upto0 = _mm(split0, upto2)
            lbs1, split1, lm_sums1 = log_terms(z1, valid)
            lm_upto1 = _mm(split1, upto2)
            a0, es0, split0, e_sums0 = weights(lbs0, lts[0] - run_lm0, lm_upto0, da0, valid)
            e_before0 = _mm(split0, before2)
            a1, es1, split1, e_sums1 = weights(lbs1, lts[1] - run_lm1, lm_upto1, da1, valid)
            e_before1 = _mm(split1, before2)
            dz0 = score_grads(lbs0, es0, run_e0, e_before0, valid)
            dkt_blk = _mm(qts[0], dz0)
            dvt_blk = _mm(dots[0], a0)
            dq0 = dq0 + _mm(dz0, kb)
            dz1 = score_grads(lbs1, es1, run_e1, e_before1, valid)
            dkt_ref[kj] += dkt_blk + _mm(qts[1], dz1)
            dvt_ref[kj] += dvt_blk + _mm(dots[1], a1)
            dq1 = dq1 + _mm(dz1, kb)
            return run_lm0 + lm_sums0, run_e0 + e_sums0, dq0, run_lm1 + lm_sums1, run_e1 + e_sums1, dq1

        zero = (jnp.zeros((tq, 1), F32), jnp.zeros((tq, 1), F32), jnp.zeros((tq, HEAD_PAIR), F32))
        first = diagonal_blocks * qi
        carry = lax.fori_loop(0, first, lambda kj, cr: block(kj, cr, None), zero + zero)
        for d in range(diagonal_blocks):
            carry = block(first + d, carry, causal(d))
        dq_ref[...] = (jnp.where(head0, carry[2], carry[5]) * ATTN_SCALE).astype(BF16)

        @pl.when(qi == T // tq - 1)
        def _():
            for b in range(T // tk):
                dk_ref[b * tk:(b + 1) * tk, :] = dkt_ref[b].T.astype(BF16)
                dv_ref[b * tk:(b + 1) * tk, :] = dvt_ref[b].T.astype(BF16)

    blk = pl.BlockSpec((tq, HEAD_PAIR), lambda p, i: (i, p))
    seq = pl.BlockSpec((T, HEAD_PAIR), lambda p, i: (0, p))
    transposed = pltpu.VMEM((T // tk, HEAD_PAIR, tk), F32)
    return pl.pallas_call(
        body, name=name, grid=(N_HEADS // 2, T // tq),
        in_specs=_attn_specs(T, tq) + [blk, blk, AFTER], out_specs=[blk, seq, seq],
        out_shape=[jax.ShapeDtypeStruct((T, SB_WIDTH), BF16)] * 3,
        scratch_shapes=[pltpu.VMEM((T, HEAD_PAIR), BF16), pltpu.VMEM((T // tk, HEAD_PAIR, tk), BF16),
                        pltpu.VMEM((T // tk, HEAD_PAIR, tk), BF16), transposed, transposed],
        compiler_params=_params(("arbitrary", "arbitrary")),
    )(proj, proj, proj, do, ltot, _in_hbm(after))


def _mix_specs(T, D, tm, wbp, w_out):
    gate_col = (POOL_WIDTH + 3 * SB_WIDTH) // D
    row = lambda i: (i, 0)
    return [
        pl.BlockSpec((tm, D), row),
        pl.BlockSpec((tm, POOL_WIDTH), row),
        pl.BlockSpec((tm, SB_WIDTH), row),
        pl.BlockSpec((tm, D), lambda i: (i, gate_col)),
        pl.BlockSpec((tm, D), lambda i: (i, gate_col + 1)),
        pl.BlockSpec(wbp.shape, lambda i: (0, 0)),
        pl.BlockSpec(wbp.shape, lambda i: (0, 0)),
        pl.BlockSpec(w_out.shape, lambda i: (0, 0)),
    ]


def _mix_fwd(h, p, o, proj, wbp, wba, w_out, *, tm, name):
    T, D = h.shape
    tm = min(tm, T)

    def body(h_ref, p_ref, o_ref, glp_ref, gls_ref, wbp_ref, wba_ref, wout_ref, hout_ref, m_ref):
        halves = (pl.ds(0, tm // 2), pl.ds(tm // 2, tm // 2))
        wbp, wba, wout = wbp_ref[...], wba_ref[...], wout_ref[...]
        branches = [(_mm_nt(p_ref[rows, :].astype(BF16), wbp), _mm_nt(o_ref[rows, :].astype(BF16), wba))
                    for rows in halves]
        for rows, (yp, ys) in zip(halves, branches):
            m = (jax.nn.sigmoid(glp_ref[rows, :]) * yp + jax.nn.sigmoid(gls_ref[rows, :]) * ys).astype(BF16)
            m_ref[rows, :] = m
            hout_ref[rows, :] = h_ref[rows, :] + _mm(m, wout)

    row = lambda i: (i, 0)
    return pl.pallas_call(
        body, name=name, grid=(T // tm,),
        in_specs=_mix_specs(T, D, tm, wbp, w_out),
        out_specs=[pl.BlockSpec((tm, D), row), pl.BlockSpec((tm, D), row)],
        out_shape=[jax.ShapeDtypeStruct((T, D), F32), jax.ShapeDtypeStruct((T, D), BF16)],
        compiler_params=_params(("arbitrary",)),
    )(h, p, o, proj, proj, wbp, wba, w_out)


def _mix_bwd(dh, p, o, proj, wbp, wba, w_out, after, *, tm, name):
    T, D = dh.shape
    tm = min(tm, T)

    def body(dh_ref, p_ref, o_ref, glp_ref, gls_ref, wbp_ref, wba_ref, wout_ref, after_ref,
             dyp_ref, dys_ref, dp_ref, do_ref, dgl_ref):
        halves = (pl.ds(0, tm // 2), pl.ds(tm // 2, tm // 2))
        wbp, wba, wout = wbp_ref[...], wba_ref[...], wout_ref[...]
        products = [(_mm_nt(dh_ref[rows, :].astype(BF16), wout), _mm_nt(p_ref[rows, :].astype(BF16), wbp),
                     _mm_nt(o_ref[rows, :].astype(BF16), wba)) for rows in halves]
        for rows, (dm, yp, ys) in zip(halves, products):
            gp = jax.nn.sigmoid(glp_ref[rows, :])
            gs = jax.nn.sigmoid(gls_ref[rows, :])
            dyp = (dm * gp).astype(BF16)
            dys = (dm * gs).astype(BF16)
            dyp_ref[rows, :] = dyp
            dys_ref[rows, :] = dys
            dgl_ref[rows, :D] = (dm * yp * gp * (1.0 - gp)).astype(BF16)
            dgl_ref[rows, D:] = (dm * ys * gs * (1.0 - gs)).astype(BF16)
            dp_ref[rows, :] = _mm(dyp, wbp)
            do_ref[rows, :] = _mm(dys, wba)

    row = lambda i: (i, 0)
    return pl.pallas_call(
        body, name=name, grid=(T // tm,),
        in_specs=_mix_specs(T, D, tm, wbp, w_out) + [AFTER],
        out_specs=[pl.BlockSpec((tm, D), row), pl.BlockSpec((tm, D), row), pl.BlockSpec((tm, POOL_WIDTH), row),
                   pl.BlockSpec((tm, SB_WIDTH), row), pl.BlockSpec((tm, 2 * D), row)],
        out_shape=[jax.ShapeDtypeStruct((T, D), BF16), jax.ShapeDtypeStruct((T, D), BF16),
                   jax.ShapeDtypeStruct((T, POOL_WIDTH), F32), jax.ShapeDtypeStruct((T, SB_WIDTH), F32),
                   jax.ShapeDtypeStruct((T, 2 * D), BF16)],
        compiler_params=_params(("arbitrary",)),
    )(dh, p, o, proj, proj, wbp, wba, w_out, _in_hbm(after))


def _adamw_update(w, g, m, v):
    m_ = ADAM_B1 * m + (1.0 - ADAM_B1) * g
    v_ = ADAM_B2 * v + (1.0 - ADAM_B2) * (g * g)
    m_hat = m_ / (1.0 - ADAM_B1 ** ADAM_STEP)
    v_hat = v_ / (1.0 - ADAM_B2 ** ADAM_STEP)
    return -ADAM_LR * (m_hat / (jnp.sqrt(v_hat) + ADAM_EPS) + ADAM_WD * w), m_, v_


def _adamw(w, g, m, v, *, name):
    R, C = w.shape
    tr = _row_tile(R, C)

    def body(w_ref, g_ref, m_ref, v_ref, d_ref, nm_ref, nv_ref):
        d_ref[...], nm_ref[...], nv_ref[...] = _adamw_update(w_ref[...], g_ref[...], m_ref[...], v_ref[...])

    spec = pl.BlockSpec((tr, C), lambda i: (i, 0))
    return pl.pallas_call(
        body, name=name, grid=(R // tr,), in_specs=[spec] * 4, out_specs=[spec] * 3,
        out_shape=[jax.ShapeDtypeStruct((R, C), F32)] * 3,
        compiler_params=_params(("arbitrary",)),
    )(w, g, m, v)


def _position():
    return lax.axis_index("x"), lax.axis_index("y"), lax.axis_index("c")


def _all_gather(shards, *, name, collective_id):
    n = len(shards)
    n_copies = 9

    def body(*refs):
        ins, outs = refs[:n], refs[n:2 * n]
        send_sems, recv_sems, local_sems = refs[2 * n:]
        x, y, c = _position()
        me, sibling = (x, y, c), (x, y, 1 - c)
        x_nbr, y_nbr, diagonal = (1 - x, y, c), (x, 1 - y, c), (1 - x, 1 - y, c)
        other = lambda pos: (pos[0], pos[1], 1 - c)

        barrier = pltpu.get_barrier_semaphore()
        for peer in (sibling, x_nbr, y_nbr):
            pl.semaphore_signal(barrier, inc=1, device_id=peer, device_id_type=MESH)
        pl.semaphore_wait(barrier, 3)

        def block(a, pos, half=None):
            ref = outs[a].at[4 * pos[0] + 2 * pos[1] + pos[2]]
            rows = ref.shape[0] // 2
            return ref if half is None else ref.at[pl.ds(half * rows, rows)]

        def copy(a, k, pos, to, half=None, src=None):
            return pltpu.make_async_remote_copy(
                src_ref=block(a, pos, half) if src is None else src, dst_ref=block(a, pos, half),
                send_sem=send_sems.at[n_copies * a + k], recv_sem=recv_sems.at[n_copies * a + k],
                device_id=to, device_id_type=MESH)

        started = []
        for a in range(n):
            mine = pltpu.make_async_copy(ins[a], block(a, me), local_sems.at[a])
            mine.start()
            started.append(mine)
        sends = []
        for a in range(n):
            sends += [copy(a, 1, me, x_nbr, src=ins[a]), copy(a, 2, me, y_nbr, src=ins[a]),
                      copy(a, 0, me, sibling, src=ins[a])]
        for cp in sends:
            cp.start()

        def pass_on(copies):
            for cp in copies:
                cp.start()
                sends.append(cp)

        for a in range(n):
            copy(a, 1, x_nbr, me).wait_recv()
            pass_on([copy(a, 5, x_nbr, y_nbr, half=0), copy(a, 3, x_nbr, sibling)])
            copy(a, 2, y_nbr, me).wait_recv()
            pass_on([copy(a, 6, y_nbr, x_nbr, half=1), copy(a, 4, y_nbr, sibling)])
        for a in range(n):
            copy(a, 5, diagonal, me, half=0).wait_recv()
            pass_on([copy(a, 7, diagonal, sibling, half=0)])
            copy(a, 6, diagonal, me, half=1).wait_recv()
            pass_on([copy(a, 8, diagonal, sibling, half=1)])
        for a in range(n):
            copy(a, 0, sibling, me).wait_recv()
            copy(a, 3, other(x_nbr), me).wait_recv()
            copy(a, 4, other(y_nbr), me).wait_recv()
            copy(a, 7, other(diagonal), me, half=0).wait_recv()
            copy(a, 8, other(diagonal), me, half=1).wait_recv()
        for cp in sends:
            cp.wait_send()
        for cp in started:
            cp.wait()

    return pl.kernel(
        body, name=name,
        out_type=[jax.ShapeDtypeStruct((N_DEV,) + s.shape, s.dtype) for s in shards],
        mesh=plsc.ScalarSubcoreMesh(axis_name="sequencer", num_cores=1),
        scratch_types=[pltpu.SemaphoreType.DMA((n_copies * n,)), pltpu.SemaphoreType.DMA((n_copies * n,)),
                       pltpu.SemaphoreType.DMA((n,))],
        compiler_params=pltpu.CompilerParams(collective_id=collective_id),
    )(*shards)


def _chip_sums(group, *, name):
    n = len(group)
    shapes = [g.shape[1:] for g in group]

    def body(*refs):
        g_refs, partials, out_refs = refs[:n], refs[n:3 * n:2], refs[n + 1:3 * n:2]
        mines, theirs = refs[3 * n:5 * n:2], refs[3 * n + 1:5 * n:2]
        send_sems, recv_sems, local_sems = refs[5 * n:]
        x, y, c = _position()
        my_chip = 2 * x + y

        def swap(a, s):
            return pltpu.make_async_remote_copy(
                src_ref=g_refs[a].at[2 * s + (1 - c)], dst_ref=theirs[a].at[s],
                send_sem=send_sems.at[4 * a + s], recv_sem=recv_sems.at[4 * a + s],
                device_id=(x, y, 1 - c), device_id_type=MESH)

        def load(a, s):
            return pltpu.make_async_copy(g_refs[a].at[2 * s + c], mines[a].at[s], local_sems.at[4 * a + s])

        for a in range(n):
            for s in range(4):
                swap(a, s).start()
                load(a, s).start()

        for a, (R, C) in enumerate(shapes):
            rc = 128 if R % 128 == 0 else R

            def chip_sum(chip, rows):
                return mines[a][chip, rows, :].astype(F32) + theirs[a][chip, rows, :].astype(F32)

            for s in range(4):
                load(a, s).wait()
                swap(a, s).wait_recv()

                @pl.when(s == my_chip)
                def _():
                    @pl.loop(0, R // rc)
                    def _(t):
                        rows = pl.ds(pl.multiple_of(t * rc, rc), rc)
                        out_refs[a][rows, :] = chip_sum(s, rows)

                @pl.when(s != my_chip)
                def _():
                    @pl.loop(0, R // rc)
                    def _(t):
                        rows = pl.ds(pl.multiple_of(t * rc, rc), rc)
                        partials[a][(s ^ my_chip) - 1, rows, :] = chip_sum(s, rows).astype(BF16)

        for a in range(n):
            for s in range(4):
                swap(a, s).wait_send()

    vmem = pl.BlockSpec(memory_space=pltpu.VMEM)
    outs = pl.pallas_call(
        body, name=name,
        in_specs=[pl.BlockSpec(memory_space=pl.ANY)] * n, out_specs=[vmem] * (2 * n),
        out_shape=[shape for R, C in shapes
                   for shape in (jax.ShapeDtypeStruct((3, R, C), BF16), jax.ShapeDtypeStruct((R, C), F32))],
        scratch_shapes=[pltpu.VMEM((4, R, C), BF16) for R, C in shapes for _ in range(2)] + [
            pltpu.SemaphoreType.DMA((4 * n,)), pltpu.SemaphoreType.DMA((4 * n,)), pltpu.SemaphoreType.DMA((4 * n,))],
        compiler_params=_params(),
    )(*group)
    return [(outs[2 * a], outs[2 * a + 1]) for a in range(n)]


def _cross_chips(partials, *, name, collective_id):
    n = len(partials)

    def body(*refs):
        ins, outs = refs[:n], refs[n:2 * n]
        send_sems, recv_sems = refs[2 * n:]
        x, y, c = _position()
        my_chip = 2 * x + y
        peers = [((my_chip ^ j) // 2, (my_chip ^ j) % 2, c) for j in (1, 2, 3)]

        barrier = pltpu.get_barrier_semaphore()
        for peer in peers:
            pl.semaphore_signal(barrier, inc=1, device_id=peer, device_id_type=MESH)
        pl.semaphore_wait(barrier, 3)

        copies = [
            pltpu.make_async_remote_copy(
                src_ref=ins[a].at[j], dst_ref=outs[a].at[j],
                send_sem=send_sems.at[3 * a + j], recv_sem=recv_sems.at[3 * a + j],
                device_id=peers[j], device_id_type=MESH)
            for a in range(n) for j in range(3)]
        for cp in copies:
            cp.start()
        for cp in copies:
            cp.wait_recv()
        for cp in copies:
            cp.wait_send()

    return pl.kernel(
        body, name=name,
        out_type=[jax.ShapeDtypeStruct(p.shape, p.dtype) for p in partials],
        mesh=plsc.ScalarSubcoreMesh(axis_name="sequencer", num_cores=1),
        scratch_types=[pltpu.SemaphoreType.DMA((3 * n,)), pltpu.SemaphoreType.DMA((3 * n,))],
        compiler_params=pltpu.CompilerParams(collective_id=collective_id),
    )(*partials)


def _cross_chips_and_gather(partials, slab, *, name, collective_id):
    n = len(partials)

    def body(*refs):
        part_refs, slab_ref = refs[:n], refs[n]
        landed_refs, slabs_ref = refs[n + 1:2 * n + 1], refs[2 * n + 1]
        send_sems, recv_sems, local_sem = refs[2 * n + 2:]
        x, y, c = _position()
        me, my_chip = 4 * x + 2 * y + c, 2 * x + y
        others = [me ^ k for k in range(1, N_DEV)]
        ids = [(o // 4, (o // 2) % 2, o % 2) for o in others]

        barrier = pltpu.get_barrier_semaphore()
        for peer in ids:
            pl.semaphore_signal(barrier, inc=1, device_id=peer, device_id_type=MESH)
        pl.semaphore_wait(barrier, N_DEV - 1)

        mine = pltpu.make_async_copy(slab_ref, slabs_ref.at[me], local_sem)
        mine.start()
        sends = [
            pltpu.make_async_remote_copy(
                src_ref=part_refs[a].at[j], dst_ref=landed_refs[a].at[j],
                send_sem=send_sems.at[3 * a + j], recv_sem=recv_sems.at[3 * a + j],
                device_id=((my_chip ^ (j + 1)) // 2, (my_chip ^ (j + 1)) % 2, c), device_id_type=MESH)
            for a in range(n) for j in range(3)]
        sends += [
            pltpu.make_async_remote_copy(
                src_ref=slab_ref, dst_ref=slabs_ref.at[me],
                send_sem=send_sems.at[3 * n + k], recv_sem=recv_sems.at[3 * n + k],
                device_id=ids[k], device_id_type=MESH)
            for k in range(N_DEV - 1)]
        arrivals = sends[:3 * n] + [
            pltpu.make_async_remote_copy(
                src_ref=slab_ref, dst_ref=slabs_ref.at[others[k]],
                send_sem=send_sems.at[3 * n + k], recv_sem=recv_sems.at[3 * n + k],
                device_id=ids[k], device_id_type=MESH)
            for k in range(N_DEV - 1)]
        for cp in sends:
            cp.start()
        for cp in arrivals:
            cp.wait_recv()
        for cp in sends:
            cp.wait_send()
        mine.wait()

    n_sems = 3 * n + N_DEV - 1
    outs = pl.kernel(
        body, name=name,
        out_type=[jax.ShapeDtypeStruct(p.shape, p.dtype) for p in partials]
                 + [jax.ShapeDtypeStruct((N_DEV,) + slab.shape, slab.dtype)],
        mesh=plsc.ScalarSubcoreMesh(axis_name="sequencer", num_cores=1),
        scratch_types=[pltpu.SemaphoreType.DMA((n_sems,)), pltpu.SemaphoreType.DMA((n_sems,)), pltpu.SemaphoreType.DMA],
        compiler_params=pltpu.CompilerParams(collective_id=collective_id),
    )(*partials, slab)
    return outs[:n], outs[n]


def _sum_devices(gathered, after, *, name):
    _, R, C = gathered.shape

    def body(in_ref, after_ref, out_ref):
        total = in_ref[0]
        for d in range(1, N_DEV):
            total = total + in_ref[d]
        out_ref[...] = total

    return pl.pallas_call(
        body, name=name, grid=(1,),
        in_specs=[pl.BlockSpec((N_DEV, R, C), lambda i: (0, 0, 0)), AFTER],
        out_specs=pl.BlockSpec((R, C), lambda i: (0, 0)),
        out_shape=jax.ShapeDtypeStruct((R, C), F32),
        compiler_params=_params(("arbitrary",)),
    )(gathered, _in_hbm(after))


def _owner_sum_adamw(own, landed, w, m, v, after, *, transposed, name, group=None, into=()):
    H, R, C = w.shape
    tr = R // 2
    first_group = 0 if group is None else group

    def body(own_ref, landed_ref, w_ref, m_ref, v_ref, after_ref, *rest):
        g_ref, d_ref, nm_ref, nv_ref = rest[len(into):]
        total = own_ref[...]
        for j in range(3):
            total = total + landed_ref[j].astype(F32)
        if transposed:
            total = total.T
        g_ref[...] = total
        d_ref[...], nm_ref[...], nv_ref[...] = _adamw_update(w_ref[...], total, m_ref[...], v_ref[...])

    spec = pl.BlockSpec((None, tr, C), lambda h, i: (first_group + h, i, 0))
    if transposed:
        own_spec = pl.BlockSpec((None, C, tr), lambda h, i: (h, 0, i))
        landed_spec = pl.BlockSpec((3, None, C, tr), lambda h, i: (0, h, 0, i))
    else:
        own_spec = pl.BlockSpec((None, tr, C), lambda h, i: (h, i, 0))
        landed_spec = pl.BlockSpec((3, None, tr, C), lambda h, i: (0, h, i, 0))
    n_in = 6
    return pl.pallas_call(
        body, name=name, grid=(own.shape[0], R // tr),
        in_specs=[own_spec, landed_spec, spec, spec, spec, AFTER] + [pl.BlockSpec(memory_space=pl.ANY)] * len(into),
        out_specs=[spec] * 4,
        out_shape=[jax.ShapeDtypeStruct((H, R, C), F32)] * 4,
        input_output_aliases={n_in + j: j for j in range(len(into))},
        compiler_params=_params(("arbitrary", "arbitrary")),
    )(own, landed, w, m, v, _in_hbm(after), *into)


def _local_step(x, target, norms, pool_w_group, pool_scale, wgu1, wd1, w_in, wbp, wba, w_out, wgu2, wd2, exchange):
    n1g, nmg, n2g, nfg = norms
    D = x.shape[1]
    gu1, hid1 = _ffn_up(x, n1g, wgu1, tm=1024, name="ffn1_up")
    h1 = _ffn_down(x, hid1, wd1, tm=512, name="ffn1_down")
    un, proj = _inproj_fwd(h1, nmg, w_in, tm=1024, name="inproj_fwd")
    p = _pool_fwd(proj, pool_w_group, pool_scale, name="pool_fwd")
    o, ltot = _attn_fwd(proj, name="attn_fwd")
    h2, m = _mix_fwd(h1, p, o, proj, wbp, wba, w_out, tm=512, name="mix_fwd")
    gu2, hid2 = _ffn_up(h2, n2g, wgu2, tm=1024, name="ffn2_up")
    h3 = _ffn_down(h2, hid2, wd2, tm=512, name="ffn2_down")
    dh3, df2, loss, d_nf = _loss_bwd(h3, target, nfg, tm=512, name="loss_bwd")

    dh2, d_n2, n2, dgu2 = _ffn_bwd(dh3, df2, h2, n2g, gu2, wgu2, wd2, df2, tm=512, name="ffn2_bwd")
    d_wd2 = _wgrad_down(hid2, df2, tk=WGRAD_TOKENS, name="ffn2_wgrad_down")
    d_wgu2 = _wgrad_gate_up(n2, dgu2, tk=WGRAD_TOKENS, name="ffn2_wgrad_gate_up")
    (g_wd2, g_wgu2), token = exchange("ffn2", [d_wd2.reshape(N_DEV, FF_SHARD_PAD, D), d_wgu2])

    dyp, dys, dp, do, dgl = _mix_bwd(dh2, p, o, proj, wbp, wba, w_out, token, tm=512, name="mix_bwd")
    d_wout = _wgrad_full(m, dh2, tk=WGRAD_TOKENS, name="wgrad_out")
    d_wbp = _wgrad_full(dyp, p, tk=WGRAD_TOKENS, name="wgrad_branch_pool")
    d_wba = _wgrad_full(dys, o, tk=WGRAD_TOKENS, name="wgrad_branch_attn")
    by_owner = lambda g: g.reshape(N_DEV, g.shape[0] // N_DEV, g.shape[1])
    (g_wbp, g_wba, g_wout), token = exchange("mix", [by_owner(d_wbp), by_owner(d_wba), by_owner(d_wout)])
    dxp, d_wgroup, d_scale = _pool_bwd(dp, proj, pool_w_group, pool_scale, name="pool_bwd")
    dq, dk, dv = _attn_bwd(proj, do, ltot, token, name="attn_bwd")
    dproj_parts = [dxp, dq, dk, dv, dgl]
    dh1, df1, d_nm = _inproj_bwd(dproj_parts, dh2, h1, nmg, w_in, tm=512, name="inproj_bwd")
    d_win = _wgrad_in(dproj_parts, un, name="wgrad_in")
    d_wd1 = _wgrad_down(hid1, df1, tk=WGRAD_TOKENS, name="ffn1_wgrad_down")
    (g_win, g_wd1, replicated_early), token = exchange(
        "w_in_ffn1_down", [d_win, d_wd1.reshape(N_DEV, FF_SHARD_PAD, D), d_nm, d_n2, d_nf, d_scale, d_wgroup, loss])

    dx, d_n1, n1, dgu1 = _ffn_bwd(dh1, df1, x, n1g, gu1, wgu1, wd1, token, tm=512, name="ffn1_bwd")
    d_wgu1_a = _wgrad_gate_up(n1, dgu1, tk=WGRAD_TOKENS, name="ffn1_wgrad_gate_up_a", part=0, parts=2)
    (g_wgu1_a, replicated_late), token = exchange("ffn1_gate_up_a", [d_wgu1_a, d_n1])
    d_wgu1_b = _wgrad_gate_up(n1, dgu1, tk=WGRAD_TOKENS, name="ffn1_wgrad_gate_up_b", part=1, parts=2)
    (g_wgu1_b,), token = exchange("last", [d_wgu1_b])
    g_wgu1 = (g_wgu1_a, g_wgu1_b)

    sharded = (g_wgu1, g_wd1, g_win, g_wbp, g_wba, g_wout, g_wgu2, g_wd2)
    return dx, sharded, (replicated_late, replicated_early), token


def _hidden_major(w):
    return jnp.swapaxes(w[0], 0, 1)


def _pad_gate_up(wt):
    d = wt.shape[1]
    wt = wt.astype(BF16).reshape(2, FF_SHARD, d)
    return jnp.pad(wt, ((0, 0), (0, FF_SHARD_PAD - FF_SHARD), (0, 0))).reshape(2 * FF_SHARD_PAD, d)


def _unpad_gate_up(gt):
    d = gt.shape[1]
    return gt.reshape(2, FF_SHARD_PAD, d)[:, :FF_SHARD].reshape(2 * FF_SHARD, d)


def _pad_down(w):
    return jnp.pad(w.astype(BF16), ((0, FF_SHARD_PAD - FF_SHARD), (0, 0)))


def kernel(x, ffn1_norm, ffn1_w_gate_up, ffn1_w_down, mix_norm, w_in, pool_w_group, pool_scale, w_branch_pool, w_branch_attn, w_out, ffn2_norm, ffn2_w_gate_up, ffn2_w_down, final_norm, loss_target, m_ffn1_norm, m_ffn1_w_gate_up, m_ffn1_w_down, m_mix_norm, m_w_in, m_pool_w_group, m_pool_scale, m_w_branch_pool, m_w_branch_attn, m_w_out, m_ffn2_norm, m_ffn2_w_gate_up, m_ffn2_w_down, m_final_norm, v_ffn1_norm, v_ffn1_w_gate_up, v_ffn1_w_down, v_mix_norm, v_w_in, v_pool_w_group, v_pool_scale, v_w_branch_pool, v_w_branch_attn, v_w_out, v_ffn2_norm, v_ffn2_w_gate_up, v_ffn2_w_down, v_final_norm):
    D = x.shape[-1]
    weights = dict(ffn1_norm=ffn1_norm, ffn1_w_gate_up=ffn1_w_gate_up, ffn1_w_down=ffn1_w_down, mix_norm=mix_norm,
                   w_in=w_in, pool_w_group=pool_w_group, pool_scale=pool_scale, w_branch_pool=w_branch_pool,
                   w_branch_attn=w_branch_attn, w_out=w_out, ffn2_norm=ffn2_norm, ffn2_w_gate_up=ffn2_w_gate_up,
                   ffn2_w_down=ffn2_w_down, final_norm=final_norm)
    first = dict(ffn1_norm=m_ffn1_norm, ffn1_w_gate_up=m_ffn1_w_gate_up, ffn1_w_down=m_ffn1_w_down,
                 mix_norm=m_mix_norm, w_in=m_w_in, pool_w_group=m_pool_w_group, pool_scale=m_pool_scale,
                 w_branch_pool=m_w_branch_pool, w_branch_attn=m_w_branch_attn, w_out=m_w_out,
                 ffn2_norm=m_ffn2_norm, ffn2_w_gate_up=m_ffn2_w_gate_up, ffn2_w_down=m_ffn2_w_down,
                 final_norm=m_final_norm)
    second = dict(ffn1_norm=v_ffn1_norm, ffn1_w_gate_up=v_ffn1_w_gate_up, ffn1_w_down=v_ffn1_w_down,
                  mix_norm=v_mix_norm, w_in=v_w_in, pool_w_group=v_pool_w_group, pool_scale=v_pool_scale,
                  w_branch_pool=v_w_branch_pool, w_branch_attn=v_w_branch_attn, w_out=v_w_out,
                  ffn2_norm=v_ffn2_norm, ffn2_w_gate_up=v_ffn2_w_gate_up, ffn2_w_down=v_ffn2_w_down,
                  final_norm=v_final_norm)
    order = list(weights)

    wgu1, = _all_gather([_pad_gate_up(_hidden_major(ffn1_w_gate_up))], name="all_gather_ffn1_gate_up", collective_id=0)
    wd1, = _all_gather([_pad_down(ffn1_w_down[0])], name="all_gather_ffn1_down", collective_id=10)
    transposed = lambda w: jnp.swapaxes(w[0], 0, 1).astype(BF16)
    win_g, = _all_gather([transposed(w_in)], name="all_gather_w_in", collective_id=1)
    wbp_g, wba_g = _all_gather([transposed(w_branch_pool), transposed(w_branch_attn)],
                               name="all_gather_branches", collective_id=2)
    wout_g, = _all_gather([w_out[0].astype(BF16)], name="all_gather_w_out", collective_id=11)
    wgu2, wd2 = _all_gather([_pad_gate_up(_hidden_major(ffn2_w_gate_up)), _pad_down(ffn2_w_down[0])],
                            name="all_gather_ffn2", collective_id=3)
    whole = lambda g: g.reshape(g.shape[0] * g.shape[1], g.shape[2])
    wd1, wd2, win_g, wbp_g, wba_g, wout_g = (whole(g) for g in (wd1, wd2, win_g, wbp_g, wba_g, wout_g))

    cross_ids = {"ffn2": 4, "mix": 5, "w_in_ffn1_down": 8, "ffn1_gate_up_a": 9, "last": 7}
    small = ["ffn1_norm", "mix_norm", "ffn2_norm", "final_norm", "pool_scale", "pool_w_group"]

    def tile_rows(a):
        a = a.reshape(-1, 128)
        return jnp.pad(a, ((0, -a.shape[0] % 8), (0, 0)))

    def exchange(tag, group):
        grads = [g for g in group if g.dtype == BF16]
        extras = [tile_rows(g) for g in group if g.dtype != BF16]
        sums = _chip_sums(grads, name="chip_sums_" + tag)
        partials = [s[0] for s in sums]
        handles = []
        if extras:
            landed, slabs = _cross_chips_and_gather(partials, jnp.concatenate(extras, axis=0),
                                                    name="cross_chips_" + tag, collective_id=cross_ids[tag])
            handles = [slabs]
        else:
            landed = _cross_chips(partials, name="cross_chips_" + tag, collective_id=cross_ids[tag])
        return [(s[1], l) for s, l in zip(sums, landed)] + handles, sums[-1][1]

    norms = (ffn1_norm, mix_norm, ffn2_norm, final_norm.reshape(1, D))
    dx, sharded, (slabs_late, slabs_early), last = _local_step(
        x[0], loss_target[0], norms, pool_w_group[0], pool_scale, wgu1, wd1, win_g, wbp_g, wba_g, wout_g, wgu2, wd2,
        exchange)
    names = ["ffn1_w_gate_up", "ffn1_w_down", "w_in", "w_branch_pool", "w_branch_attn", "w_out",
             "ffn2_w_gate_up", "ffn2_w_down"]
    handles = dict(zip(names, sharded))
    grads, delta, new_m, new_v = {}, {}, {}, {}
    loss_out = []

    def update_replicated(after):
        rows = [weights[k].size // 128 for k in small]
        padded_rows = [-(-r // 8) * 8 for r in rows]
        starts = [sum(padded_rows[:i]) for i in range(len(rows) + 1)]
        total = jnp.concatenate([_sum_devices(slabs_late, after, name="sum_replicated_late"),
                                 _sum_devices(slabs_early, after, name="sum_replicated_early")], axis=0)
        loss_out.append(total[starts[-1], 0])
        small_w = jnp.concatenate([tile_rows(weights[k]) for k in small], axis=0)
        small_m = jnp.concatenate([tile_rows(first[k]) for k in small], axis=0)
        small_v = jnp.concatenate([tile_rows(second[k]) for k in small], axis=0)
        small_out = _adamw(small_w, total[:starts[-1]], small_m, small_v, name="adamw_replicated")
        for name_, start, n_rows in zip(small, starts, rows):
            shape = weights[name_].shape
            grads[name_] = total[start:start + n_rows].reshape(shape)
            delta[name_], new_m[name_], new_v[name_] = (a[start:start + n_rows].reshape(shape) for a in small_out)
        return small_out[0]

    after = last
    for k in ("ffn2_w_down", "ffn2_w_gate_up", "w_branch_pool", "w_branch_attn", "w_out", "w_in", "ffn1_w_down",
              "ffn1_w_gate_up"):
        hidden_major = k.endswith("w_gate_up")
        view = _hidden_major if hidden_major else (lambda a: a[0])
        back = (lambda a: jnp.swapaxes(a, 0, 1)[None]) if hidden_major else (lambda a: a[None])
        groups = 2 if hidden_major else 1
        by_group = lambda a: a.reshape(a.shape[:-2] + (groups, a.shape[-2] // groups, a.shape[-1]))
        state = [by_group(view(a[k])) for a in (weights, first, second)]
        if isinstance(handles[k][0], tuple):
            (own_a, landed_a), (own_b, landed_b) = handles[k]
            out = _owner_sum_adamw(own_a[None], landed_a[:, None], *state, after, name="adamw_" + k + "_a",
                                   transposed=False, group=0)
            out = _owner_sum_adamw(own_b[None], landed_b[:, None], *state, update_replicated(out[1]),
                                   name="adamw_" + k + "_b", transposed=False, group=1, into=out)
        else:
            own, landed = handles[k]
            out = _owner_sum_adamw(by_group(own), by_group(landed), *state, after, name="adamw_" + k,
                                   transposed=k in ("w_in", "w_branch_pool", "w_branch_attn"))
        after = out[1]
        grads[k], delta[k], new_m[k], new_v[k] = (back(a.reshape(-1, a.shape[-1])) for a in out)

    return (loss_out[0], dx[None], *[grads[k] for k in order], *[delta[k] for k in order],
            *[new_m[k] for k in order], *[new_v[k] for k in order])
```

```python
import jax
import jax.numpy as jnp
from jax import lax
from jax.experimental import pallas as pl
from jax.experimental.pallas import tpu as pltpu
from jax.experimental.pallas import tpu_sc as plsc

F32 = jnp.float32
BF16 = jnp.bfloat16
MESH = pl.DeviceIdType.MESH

RMS_EPS = 1e-6
N_DEV = 8
N_HEADS = 8
HEAD_DIM = 64
HEAD_PAIR = 2 * HEAD_DIM
POOL_WINDOWS = (2, 4, 8, 16)
POOL_GROUP = 128
POOL_WIDTH = 512
SB_WIDTH = 512
FF_SHARD = 352
FF_SHARD_PAD = 384
ATTN_K_BLOCK = 256
ATTN_Q_BLOCK_FWD = 512
ATTN_Q_BLOCK_BWD = 256
ATTN_SCALE = 0.125

ADAM_LR = 0.001
ADAM_B1 = 0.9
ADAM_B2 = 0.999
ADAM_EPS = 1e-08
ADAM_WD = 0.01
ADAM_STEP = 10

VMEM_LIMIT = 48 << 20
WGRAD_TOKENS = 2048


def _params(dims=None):
    return pltpu.CompilerParams(dimension_semantics=dims, vmem_limit_bytes=VMEM_LIMIT)


def _mm(a, b):
    return jnp.dot(a, b, preferred_element_type=F32)


def _mm_nt(a, b):
    return lax.dot_general(a, b, (((1,), (1,)), ((), ())), preferred_element_type=F32)


def _mm_tn(a, b):
    return lax.dot_general(a, b, (((0,), (0,)), ((), ())), preferred_element_type=F32)


def _row_tile(rows, cols):
    limit = max(8, (512 * 1024) // cols)
    return max(t for t in range(8, rows + 1, 8) if rows % t == 0 and (t <= limit or t == 8))


def _rstd(xf):
    return lax.rsqrt(jnp.mean(xf * xf, axis=-1, keepdims=True) + RMS_EPS)


def _rms_bwd(xf, gain, dn):
    r = _rstd(xf)
    xh = xf * r
    dgain = jnp.sum(dn * xh, axis=0, keepdims=True)
    dxh = dn * gain
    dx = r * (dxh - xh * jnp.mean(dxh * xh, axis=-1, keepdims=True))
    return dx, dgain


def _ffn_up(x, gain, wgu, *, tm, name):
    T, D = x.shape
    tm = min(tm, T)
    nb, bw = wgu.shape[0] // 2, wgu.shape[1]

    def body(x_ref, gain_ref, wg_ref, wu_ref, gu_ref, hid_ref, n_scr):
        @pl.when(pl.program_id(1) == 0)
        def _():
            xf = x_ref[...]
            n_scr[...] = (xf * _rstd(xf) * gain_ref[...]).astype(BF16)

        halves = (pl.ds(0, tm // 2), pl.ds(tm // 2, tm // 2))
        wg, wu = wg_ref[...], wu_ref[...]
        gus = [(_mm_nt(n_scr[rows, :], wg), _mm_nt(n_scr[rows, :], wu)) for rows in halves]
        for rows, (g, u) in zip(halves, gus):
            gu_ref[0, rows, :] = g.astype(BF16)
            gu_ref[1, rows, :] = u.astype(BF16)
            hid_ref[rows, :] = (g * jax.nn.sigmoid(g) * u).astype(BF16)

    return pl.pallas_call(
        body, name=name, grid=(T // tm, nb),
        in_specs=[
            pl.BlockSpec((tm, D), lambda i, j: (i, 0)),
            pl.BlockSpec((1, D), lambda i, j: (0, 0)),
            pl.BlockSpec((None, bw, D), lambda i, j: (j, 0, 0)),
            pl.BlockSpec((None, bw, D), lambda i, j: (j + nb, 0, 0)),
        ],
        out_specs=[
            pl.BlockSpec((2, tm, bw), lambda i, j: (0, i, j)),
            pl.BlockSpec((tm, bw), lambda i, j: (i, j)),
        ],
        out_shape=[jax.ShapeDtypeStruct((2, T, nb * bw), BF16), jax.ShapeDtypeStruct((T, nb * bw), BF16)],
        scratch_shapes=[pltpu.VMEM((tm, D), BF16)],
        compiler_params=_params(("arbitrary", "arbitrary")),
    )(x, gain, wgu, wgu)


def _ffn_down(x, hid, wd, *, tm, name):
    T, D = x.shape
    tm = min(tm, T)
    F = hid.shape[1]

    def body(x_ref, hid_ref, wd_ref, h_ref):
        h_ref[...] = x_ref[...] + 0.5 * _mm(hid_ref[...], wd_ref[...])

    return pl.pallas_call(
        body, name=name, grid=(T // tm,),
        in_specs=[
            pl.BlockSpec((tm, D), lambda i: (i, 0)),
            pl.BlockSpec((tm, F), lambda i: (i, 0)),
            pl.BlockSpec((F, D), lambda i: (0, 0)),
        ],
        out_specs=pl.BlockSpec((tm, D), lambda i: (i, 0)),
        out_shape=jax.ShapeDtypeStruct((T, D), F32),
        compiler_params=_params(("arbitrary",)),
    )(x, hid, wd)


AFTER = pl.BlockSpec(memory_space=pltpu.HBM)


def _in_hbm(token):
    return pltpu.with_memory_space_constraint(token, pltpu.HBM)


def _ffn_bwd(dh, df, x, gain, gu, wgu, wd, after, *, tm, name):
    T, D = x.shape
    tm = min(tm, T)
    nb, bw = wgu.shape[0] // 2, wgu.shape[1]

    def body(dh_ref, df_ref, x_ref, gain_ref, gu_ref, wg_ref, wu_ref, wd_ref, after_ref,
             dx_ref, dgain_ref, n_ref, dgu_ref, dn_acc):
        i, j = pl.program_id(0), pl.program_id(1)

        @pl.when(j == 0)
        def _():
            xf = x_ref[...]
            n_ref[...] = (xf * _rstd(xf) * gain_ref[...]).astype(BF16)
            dn_acc[...] = jnp.zeros_like(dn_acc)

        @pl.when((i == 0) & (j == 0))
        def _():
            dgain_ref[...] = jnp.zeros_like(dgain_ref)

        halves = (pl.ds(0, tm // 2), pl.ds(tm // 2, tm // 2))
        wd, wg, wu = wd_ref[...], wg_ref[...], wu_ref[...]
        dhids = [_mm_nt(df_ref[rows, :], wd) for rows in halves]
        for rows, dhid in zip(halves, dhids):
            g = gu_ref[0, rows, :].astype(F32)
            u = gu_ref[1, rows, :].astype(F32)
            s = jax.nn.sigmoid(g)
            silu = g * s
            dg = (dhid * u * (s * (1.0 + g * (1.0 - s)))).astype(BF16)
            du = (dhid * silu).astype(BF16)
            dgu_ref[0, rows, :] = dg
            dgu_ref[1, rows, :] = du
            dn_acc[rows, :] += _mm(dg, wg) + _mm(du, wu)

        @pl.when(j == nb - 1)
        def _():
            dx, dgain = _rms_bwd(x_ref[...], gain_ref[...], dn_acc[...])
            dx_ref[...] = dh_ref[...] + dx
            dgain_ref[...] += dgain

    row = lambda i, j: (i, 0)
    return pl.pallas_call(
        body, name=name, grid=(T // tm, nb),
        in_specs=[
            pl.BlockSpec((tm, D), row),
            pl.BlockSpec((tm, D), row),
            pl.BlockSpec((tm, D), row),
            pl.BlockSpec((1, D), lambda i, j: (0, 0)),
            pl.BlockSpec((2, tm, bw), lambda i, j: (0, i, j)),
            pl.BlockSpec((None, bw, D), lambda i, j: (j, 0, 0)),
            pl.BlockSpec((None, bw, D), lambda i, j: (j + nb, 0, 0)),
            pl.BlockSpec((bw, D), lambda i, j: (j, 0)),
            AFTER,
        ],
        out_specs=[
            pl.BlockSpec((tm, D), row),
            pl.BlockSpec((1, D), lambda i, j: (0, 0)),
            pl.BlockSpec((tm, D), row),
            pl.BlockSpec((2, tm, bw), lambda i, j: (0, i, j)),
        ],
        out_shape=[
            jax.ShapeDtypeStruct((T, D), F32),
            jax.ShapeDtypeStruct((1, D), F32),
            jax.ShapeDtypeStruct((T, D), BF16),
            jax.ShapeDtypeStruct((2, T, nb * bw), BF16),
        ],
        scratch_shapes=[pltpu.VMEM((tm, D), F32)],
        compiler_params=_params(("arbitrary", "arbitrary")),
    )(dh, df, x, gain, gu, wgu, wgu, wd, _in_hbm(after))


def _wgrad(a, b, *, grid, a_spec, b_spec, out_spec, out_shape, acc_shape, name):
    nk = grid[2]

    def body(a_ref, b_ref, o_ref, acc):
        k = pl.program_id(2)

        @pl.when(k == 0)
        def _():
            acc[...] = jnp.zeros_like(acc)

        acc[...] += _mm_tn(a_ref[...].astype(BF16), b_ref[...].astype(BF16))

        @pl.when(k == nk - 1)
        def _():
            o_ref[...] = acc[...].astype(o_ref.dtype)

    return pl.pallas_call(
        body, name=name, grid=grid, in_specs=[a_spec, b_spec], out_specs=out_spec,
        out_shape=jax.ShapeDtypeStruct(out_shape, BF16),
        scratch_shapes=[pltpu.VMEM(acc_shape, F32)],
        compiler_params=_params(("arbitrary", "arbitrary", "arbitrary")),
    )(a, b)


def _wgrad_gate_up(n, dgu, *, tk, name, part=0, parts=1):
    T, D = n.shape
    tk = min(tk, T)
    owner_rows = FF_SHARD_PAD * 2
    nb = dgu.shape[2] // owner_rows
    bw = owner_rows // parts
    return _wgrad(
        dgu, n, grid=(2 * nb, 1, T // tk), name=name,
        a_spec=pl.BlockSpec((None, tk, bw), lambda m, c, k: (m // nb, k, parts * (m % nb) + part)),
        b_spec=pl.BlockSpec((tk, D), lambda m, c, k: (k, 0)),
        out_spec=pl.BlockSpec((None, bw, D), lambda m, c, k: (m, 0, 0)),
        out_shape=(2 * nb, bw, D), acc_shape=(bw, D))


def _wgrad_down(hid, df, *, tk, name):
    T, D = df.shape
    tk = min(tk, T)
    bw = FF_SHARD_PAD * 2
    nb = hid.shape[1] // bw
    return _wgrad(
        hid, df, grid=(nb, 1, T // tk), name=name,
        a_spec=pl.BlockSpec((tk, bw), lambda m, c, k: (k, m)),
        b_spec=pl.BlockSpec((tk, D), lambda m, c, k: (k, 0)),
        out_spec=pl.BlockSpec((bw, D), lambda m, c, k: (m, 0)),
        out_shape=(nb * bw, D), acc_shape=(bw, D))


def _wgrad_in(dparts, un, *, name):
    T, D = un.shape
    bw = sum(p.shape[1] for p in dparts) // N_DEV
    first = [sum(p.shape[1] for p in dparts[:i]) // bw for i in range(len(dparts) + 1)]

    def body(*refs):
        dp_refs, un_ref, o_ref = refs[:-2], refs[-2], refs[-1]
        m = pl.program_id(0)
        for dp_ref, lo, hi in zip(dp_refs, first[:-1], first[1:]):
            @pl.when((m >= lo) & (m < hi))
            def _():
                o_ref[...] = _mm_tn(dp_ref[...], un_ref[...]).astype(o_ref.dtype)

    def piece_spec(lo, hi):
        return pl.BlockSpec((T, bw), lambda m: (0, jnp.clip(m - lo, 0, hi - lo - 1)))

    return pl.pallas_call(
        body, name=name, grid=(N_DEV,),
        in_specs=[piece_spec(lo, hi) for lo, hi in zip(first[:-1], first[1:])] + [pl.BlockSpec((T, D), lambda m: (0, 0))],
        out_specs=pl.BlockSpec((None, bw, D), lambda m: (m, 0, 0)),
        out_shape=jax.ShapeDtypeStruct((N_DEV, bw, D), BF16),
        compiler_params=_params(("arbitrary",)),
    )(*dparts, un)


def _wgrad_full(a, b, *, tk, name):
    T, M = a.shape
    tk = min(tk, T)
    N = b.shape[1]
    return _wgrad(
        a, b, grid=(1, 1, T // tk), name=name,
        a_spec=pl.BlockSpec((tk, M), lambda m, c, k: (k, 0)),
        b_spec=pl.BlockSpec((tk, N), lambda m, c, k: (k, 0)),
        out_spec=pl.BlockSpec((M, N), lambda m, c, k: (0, 0)), out_shape=(M, N), acc_shape=(M, N))


def _loss_bwd(h, target, gain, *, tm, name):
    T, D = h.shape
    tm = min(tm, T)

    def body(h_ref, t_ref, gain_ref, dh_ref, df_ref, loss_ref, dgain_ref):
        @pl.when(pl.program_id(0) == 0)
        def _():
            loss_ref[...] = jnp.zeros_like(loss_ref)
            dgain_ref[...] = jnp.zeros_like(dgain_ref)

        xf = h_ref[...]
        gain = gain_ref[...]
        err = xf * _rstd(xf) * gain - t_ref[...]
        loss_ref[...] += 0.5 * jnp.sum(jnp.mean(err * err, axis=-1, keepdims=True), axis=0, keepdims=True)
        dx, dgain = _rms_bwd(xf, gain, err * (1.0 / D))
        dh_ref[...] = dx
        df_ref[...] = (0.5 * dx).astype(BF16)
        dgain_ref[...] += dgain

    row = lambda i: (i, 0)
    fixed = lambda i: (0, 0)
    return pl.pallas_call(
        body, name=name, grid=(T // tm,),
        in_specs=[pl.BlockSpec((tm, D), row), pl.BlockSpec((tm, D), row), pl.BlockSpec((1, D), fixed)],
        out_specs=[pl.BlockSpec((tm, D), row), pl.BlockSpec((tm, D), row), pl.BlockSpec((1, 128), fixed),
                   pl.BlockSpec((1, D), fixed)],
        out_shape=[jax.ShapeDtypeStruct((T, D), F32), jax.ShapeDtypeStruct((T, D), BF16),
                   jax.ShapeDtypeStruct((1, 128), F32), jax.ShapeDtypeStruct((1, D), F32)],
        compiler_params=_params(("arbitrary",)),
    )(h, target, gain)


def _inproj_fwd(h, gain, w_in_t, *, tm, name):
    T, D = h.shape
    tm = min(tm, T)
    bn = D
    nb = w_in_t.shape[0] // bn

    def body(h_ref, gain_ref, wt_ref, un_ref, proj_ref):
        @pl.when(pl.program_id(1) == 0)
        def _():
            xf = h_ref[...]
            un_ref[...] = (xf * _rstd(xf) * gain_ref[...]).astype(BF16)

        proj_ref[...] = _mm_nt(un_ref[...], wt_ref[...])

    return pl.pallas_call(
        body, name=name, grid=(T // tm, nb),
        in_specs=[
            pl.BlockSpec((tm, D), lambda i, j: (i, 0)),
            pl.BlockSpec((1, D), lambda i, j: (0, 0)),
            pl.BlockSpec((bn, D), lambda i, j: (j, 0)),
        ],
        out_specs=[pl.BlockSpec((tm, D), lambda i, j: (i, 0)), pl.BlockSpec((tm, bn), lambda i, j: (i, j))],
        out_shape=[jax.ShapeDtypeStruct((T, D), BF16), jax.ShapeDtypeStruct((T, nb * bn), F32)],
        compiler_params=_params(("arbitrary", "arbitrary")),
    )(h, gain, w_in_t)


def _inproj_bwd(dparts, dh, h, gain, w_in_t, *, tm, name):
    T, D = h.shape
    tm = min(tm, T)
    n = len(dparts)
    widths = [p.shape[1] for p in dparts]
    starts = [sum(widths[:i]) for i in range(n)]

    def body(*refs):
        dp_refs = refs[:n]
        dh_ref, h_ref, gain_ref, wt_ref, dx_ref, df_ref, dgain_ref = refs[n:]

        @pl.when(pl.program_id(0) == 0)
        def _():
            dgain_ref[...] = jnp.zeros_like(dgain_ref)

        dn = sum(_mm(dp_ref[...], wt_ref[start:start + width, :])
                 for dp_ref, start, width in zip(dp_refs, starts, widths))
        dx, dgain = _rms_bwd(h_ref[...], gain_ref[...], dn)
        dh_in = dh_ref[...] + dx
        dx_ref[...] = dh_in
        df_ref[...] = (0.5 * dh_in).astype(BF16)
        dgain_ref[...] += dgain

    row = lambda i: (i, 0)
    fixed = lambda i: (0, 0)
    return pl.pallas_call(
        body, name=name, grid=(T // tm,),
        in_specs=[pl.BlockSpec((tm, width), row) for width in widths] + [
            pl.BlockSpec((tm, D), row),
            pl.BlockSpec((tm, D), row),
            pl.BlockSpec((1, D), fixed),
            pl.BlockSpec(w_in_t.shape, fixed),
        ],
        out_specs=[pl.BlockSpec((tm, D), row), pl.BlockSpec((tm, D), row), pl.BlockSpec((1, D), fixed)],
        out_shape=[jax.ShapeDtypeStruct((T, D), F32), jax.ShapeDtypeStruct((T, D), BF16),
                   jax.ShapeDtypeStruct((1, D), F32)],
        compiler_params=_params(("arbitrary",)),
    )(*dparts, dh, h, gain, w_in_t)


def _window_sum(x, row, doublings, *, backward):
    T = x.shape[0]
    s = x
    for k in range(doublings):
        sh = 1 << k
        if backward:
            s = s + jnp.where(row < T - sh, pltpu.roll(s, T - sh, 0), 0.0)
        else:
            s = s + jnp.where(row >= sh, pltpu.roll(s, sh, 0), 0.0)
    return s


def _pool_fwd(proj, w_group, scale, *, name):
    T = proj.shape[0]

    def body(xp_ref, w_ref, scale_ref, p_ref):
        row = lax.broadcasted_iota(jnp.int32, (T, POOL_GROUP), 0)
        for gi, window in enumerate(POOL_WINDOWS):
            cols = slice(gi * POOL_GROUP, (gi + 1) * POOL_GROUP)
            x = xp_ref[:, cols]
            inv_count = 1.0 / jnp.minimum(row + 1, window).astype(F32)
            yc = _window_sum(x, row, gi + 1, backward=False) * inv_count - x
            pre = _mm(yc.astype(BF16), w_ref[gi].astype(BF16))
            p_ref[:, cols] = pre * scale_ref[:, cols]

    return pl.pallas_call(
        body, name=name, grid=(1,),
        in_specs=[
            pl.BlockSpec((T, POOL_WIDTH), lambda i: (0, 0)),
            pl.BlockSpec(w_group.shape, lambda i: (0, 0, 0)),
            pl.BlockSpec((1, POOL_WIDTH), lambda i: (0, 0)),
        ],
        out_specs=pl.BlockSpec((T, POOL_WIDTH), lambda i: (0, 0)),
        out_shape=jax.ShapeDtypeStruct((T, POOL_WIDTH), F32),
        compiler_params=_params(("arbitrary",)),
    )(proj, w_group, scale)


def _pool_bwd(dp, proj, w_group, scale, *, name):
    T = proj.shape[0]

    def body(dp_ref, xp_ref, w_ref, scale_ref, dxp_ref, dw_ref, dscale_ref):
        row = lax.broadcasted_iota(jnp.int32, (T, POOL_GROUP), 0)
        for gi, window in enumerate(POOL_WINDOWS):
            cols = slice(gi * POOL_GROUP, (gi + 1) * POOL_GROUP)
            x = xp_ref[:, cols]
            inv_count = 1.0 / jnp.minimum(row + 1, window).astype(F32)
            yc = (_window_sum(x, row, gi + 1, backward=False) * inv_count - x).astype(BF16)
            w = w_ref[gi].astype(BF16)
            pre = _mm(yc, w)
            dpg = dp_ref[:, cols]
            dscale_ref[:, cols] = jnp.sum(dpg * pre, axis=0, keepdims=True)
            dpre = (dpg * scale_ref[:, cols]).astype(BF16)
            dw_ref[gi] = _mm_tn(yc, dpre)
            dyc = _mm_nt(dpre, w)
            dxp_ref[:, cols] = (_window_sum(dyc * inv_count, row, gi + 1, backward=True) - dyc).astype(BF16)

    return pl.pallas_call(
        body, name=name, grid=(1,),
        in_specs=[
            pl.BlockSpec((T, POOL_WIDTH), lambda i: (0, 0)),
            pl.BlockSpec((T, POOL_WIDTH), lambda i: (0, 0)),
            pl.BlockSpec(w_group.shape, lambda i: (0, 0, 0)),
            pl.BlockSpec((1, POOL_WIDTH), lambda i: (0, 0)),
        ],
        out_specs=[
            pl.BlockSpec((T, POOL_WIDTH), lambda i: (0, 0)),
            pl.BlockSpec(w_group.shape, lambda i: (0, 0, 0)),
            pl.BlockSpec((1, POOL_WIDTH), lambda i: (0, 0)),
        ],
        out_shape=[jax.ShapeDtypeStruct((T, POOL_WIDTH), BF16), jax.ShapeDtypeStruct(w_group.shape, F32),
                   jax.ShapeDtypeStruct((1, POOL_WIDTH), F32)],
        compiler_params=_params(("arbitrary",)),
    )(dp, proj, w_group, scale)


ATTN_STRIP = 32


def _log_sigmoids(z):
    lb = jnp.minimum(z, 0.0) - jnp.log(1.0 + jnp.exp(-jnp.abs(z)))
    return lb, lb - z


def _transposed_blocks(x_ref, blocks_scr, tq):
    for b in range(blocks_scr.shape[0]):
        blocks_scr[b] = x_ref[b * tq:(b + 1) * tq, :].T.astype(BF16)


def _split_bf16(x):
    hi = x.astype(BF16)
    return hi, (x - hi.astype(F32)).astype(BF16)


def _strips(n):
    return [slice(i, i + ATTN_STRIP) for i in range(0, n, ATTN_STRIP)]


def _rows(parts):
    return jnp.concatenate(parts, axis=0)


def _attn_specs(T, tq):
    q_col = POOL_WIDTH // HEAD_PAIR
    k_col = q_col + SB_WIDTH // HEAD_PAIR
    v_col = k_col + SB_WIDTH // HEAD_PAIR
    return [
        pl.BlockSpec((tq, HEAD_PAIR), lambda p, i: (i, q_col + p)),
        pl.BlockSpec((T, HEAD_PAIR), lambda p, i: (0, k_col + p)),
        pl.BlockSpec((T, HEAD_PAIR), lambda p, i: (0, v_col + p)),
    ]


def _attn_fwd(proj, *, name):
    T = proj.shape[0]
    tk = min(ATTN_K_BLOCK, T)
    tq = min(ATTN_Q_BLOCK_FWD, T)
    diagonal_blocks = tq // tk

    def body(q_ref, k_ref, v_ref, o_ref, lt_ref, kt_scr, vb_scr):
        qi = pl.program_id(1)

        @pl.when(qi == 0)
        def _():
            _transposed_blocks(k_ref, kt_scr, tk)
            vb_scr[...] = v_ref[...].astype(BF16)

        head0 = lax.broadcasted_iota(jnp.int32, (tq, HEAD_PAIR), 1) < HEAD_DIM
        q = q_ref[...] * ATTN_SCALE
        qs = (jnp.where(head0, q, 0.0).astype(BF16), jnp.where(head0, 0.0, q).astype(BF16))
        r = lax.broadcasted_iota(jnp.int32, (tq, tk), 0)
        c = lax.broadcasted_iota(jnp.int32, (tq, tk), 1)
        later = (r[:tk] > c[:tk]).astype(BF16)
        later2 = _rows([later, later])
        causal = lambda d: (lambda rows: c[rows] + d * tk < r[rows])
        strips = _strips(tq)

        def log_terms(z, valid):
            lbs, his, los, sums = [], [], [], []
            for rows in strips:
                lb, lm = _log_sigmoids(z[rows])
                if valid is not None:
                    lm = jnp.where(valid(rows), lm, 0.0)
                hi, lo = _split_bf16(lm)
                lbs.append(lb)
                his.append(hi)
                los.append(lo)
                sums.append(jnp.sum(lm, axis=1, keepdims=True))
            return lbs, jnp.concatenate([_rows(his), _rows(los)], axis=1), _rows(sums)

        def weights(lbs, run, after, valid):
            parts = []
            for rows, lb in zip(strips, lbs):
                a = jnp.exp(lb + run[rows] + after[rows])
                if valid is not None:
                    a = jnp.where(valid(rows), a, 0.0)
                parts.append(a.astype(BF16))
            return _rows(parts)

        def block(kj, carry, valid):
            kt = kt_scr[kj]
            vb = vb_scr[pl.ds(pl.multiple_of(kj * tk, tk), tk), :]
            run0, o0, run1, o1 = carry
            z0 = _mm(qs[0], kt)
            z1 = _mm(qs[1], kt)
            lbs0, split0, sums0 = log_terms(z0, valid)
            after0 = _mm(split0, later2)
            lbs1, split1, sums1 = log_terms(z1, valid)
            after1 = _mm(split1, later2)
            o0 = o0 + _mm(weights(lbs0, run0, after0, valid), vb)
            o1 = o1 + _mm(weights(lbs1, run1, after1, valid), vb)
            return run0 + sums0, o0, run1 + sums1, o1

        zero = (jnp.zeros((tq, 1), F32), jnp.zeros((tq, HEAD_PAIR), F32))
        first = diagonal_blocks * qi
        carry = zero + zero
        for d in reversed(range(diagonal_blocks)):
            carry = block(first + d, carry, causal(d))
        carry = lax.fori_loop(0, first, lambda it, cr: block(first - 1 - it, cr, None), carry)
        o_ref[...] = jnp.where(head0, carry[1], carry[3])
        lt_ref[...] = jnp.where(head0, carry[0], carry[2])

    out_spec = pl.BlockSpec((tq, HEAD_PAIR), lambda p, i: (i, p))
    return pl.pallas_call(
        body, name=name, grid=(N_HEADS // 2, T // tq),
        in_specs=_attn_specs(T, tq), out_specs=[out_spec, out_spec],
        out_shape=[jax.ShapeDtypeStruct((T, SB_WIDTH), F32), jax.ShapeDtypeStruct((T, SB_WIDTH), F32)],
        scratch_shapes=[pltpu.VMEM((T // tk, HEAD_PAIR, tk), BF16), pltpu.VMEM((T, HEAD_PAIR), BF16)],
        compiler_params=_params(("arbitrary", "arbitrary")),
    )(proj, proj, proj)


def _attn_bwd(proj, do, ltot, after, *, name):
    T = proj.shape[0]
    tk = min(ATTN_K_BLOCK, T)
    tq = min(ATTN_Q_BLOCK_BWD, T)
    diagonal_blocks = tq // tk

    def body(q_ref, k_ref, v_ref, do_ref, lt_ref, after_ref, dq_ref, dk_ref, dv_ref,
             kb_scr, kt_scr, vt_scr, dkt_ref, dvt_ref):
        qi = pl.program_id(1)

        @pl.when(qi == 0)
        def _():
            kb_scr[...] = k_ref[...].astype(BF16)
            _transposed_blocks(k_ref, kt_scr, tk)
            _transposed_blocks(v_ref, vt_scr, tk)
            dkt_ref[...] = jnp.zeros_like(dkt_ref)
            dvt_ref[...] = jnp.zeros_like(dvt_ref)

        head0 = lax.broadcasted_iota(jnp.int32, (tq, HEAD_PAIR), 1) < HEAD_DIM
        q, do_, lt = q_ref[...] * ATTN_SCALE, do_ref[...], lt_ref[...]
        qs = (jnp.where(head0, q, 0.0).astype(BF16), jnp.where(head0, 0.0, q).astype(BF16))
        q_heads = (jnp.where(head0, q, 0.0), jnp.where(head0, 0.0, q))
        do_heads = (jnp.where(head0, do_, 0.0), jnp.where(head0, 0.0, do_))
        dos = tuple(d.astype(BF16) for d in do_heads)
        qts = tuple(x.T.astype(BF16) for x in q_heads)
        dots = tuple(d.T.astype(BF16) for d in do_heads)
        lts = (jnp.max(jnp.where(head0, lt, -jnp.inf), axis=1, keepdims=True),
               jnp.max(jnp.where(head0, -jnp.inf, lt), axis=1, keepdims=True))
        r = lax.broadcasted_iota(jnp.int32, (tq, tk), 0)
        c = lax.broadcasted_iota(jnp.int32, (tq, tk), 1)
        upto = (r[:tk] <= c[:tk]).astype(BF16)
        before = (r[:tk] < c[:tk]).astype(BF16)
        upto2, before2 = _rows([upto, upto]), _rows([before, before])
        causal = lambda d: (lambda rows: c[rows] + d * tk < r[rows])
        strips = _strips(tq)

        def log_terms(z, valid):
            lbs, his, los, sums = [], [], [], []
            for rows in strips:
                lb, lm = _log_sigmoids(z[rows])
                if valid is not None:
                    lm = jnp.where(valid(rows), lm, 0.0)
                hi, lo = _split_bf16(lm)
                lbs.append(lb)
                his.append(hi)
                los.append(lo)
                sums.append(jnp.sum(lm, axis=1, keepdims=True))
            return lbs, jnp.concatenate([_rows(his), _rows(los)], axis=1), _rows(sums)

        def weights(lbs, rest, lm_upto, da, valid):
            a_parts, es, his, los, sums = [], [], [], [], []
            for rows, lb in zip(strips, lbs):
                a = jnp.exp(lb + (rest[rows] - lm_upto[rows]))
                if valid is not None:
                    a = jnp.where(valid(rows), a, 0.0)
                e = da[rows] * a
                hi, lo = _split_bf16(e)
                a_parts.append(a.astype(BF16))
                es.append(e)
                his.append(hi)
                los.append(lo)
                sums.append(jnp.sum(e, axis=1, keepdims=True))
            return _rows(a_parts), es, jnp.concatenate([_rows(his), _rows(los)], axis=1), _rows(sums)

        def score_grads(lbs, es, run_e, e_before, valid):
            parts = []
            for rows, lb, e in zip(strips, lbs, es):
                beta = jnp.exp(lb)
                dz = e * (1.0 - beta) - (run_e[rows] + e_before[rows]) * beta
                if valid is not None:
                    dz = jnp.where(valid(rows), dz, 0.0)
                parts.append(dz.astype(BF16))
            return _rows(parts)

        def block(kj, carry, valid):
            off = pl.multiple_of(kj * tk, tk)
            kb, kt, vt = kb_scr[pl.ds(off, tk), :], kt_scr[kj], vt_scr[kj]
            run_lm0, run_e0, dq0, run_lm1, run_e1, dq1 = carry
            z0, da0 = _mm(qs[0], kt), _mm(dos[0], vt)
            z1, da1 = _mm(qs[1], kt), _mm(dos[1], vt)
            lbs0, split0, lm_sums0 = log_terms(z0, valid)
            lm_upto0 = _mm(split0, upto2)
            lbs1, split1, lm_sums1 = log_terms(z1, valid)
            lm_upto1 = _mm(split1, upto2)
            a0, es0, split0, e_sums0 = weights(lbs0, lts[0] - run_lm0, lm_upto0, da0, valid)
            e_before0 = _mm(split0, before2)
            a1, es1, split1, e_sums1 = weights(lbs1, lts[1] - run_lm1, lm_upto1, da1, valid)
            e_before1 = _mm(split1, before2)
            dz0 = score_grads(lbs0, es0, run_e0, e_before0, valid)
            dkt_blk = _mm(qts[0], dz0)
            dvt_blk = _mm(dots[0], a0)
            dq0 = dq0 + _mm(dz0, kb)
            dz1 = score_grads(lbs1, es1, run_e1, e_before1, valid)
            dkt_ref[kj] += dkt_blk + _mm(qts[1], dz1)
            dvt_ref[kj] += dvt_blk + _mm(dots[1], a1)
            dq1 = dq1 + _mm(dz1, kb)
            return run_lm0 + lm_sums0, run_e0 + e_sums0, dq0, run_lm1 + lm_sums1, run_e1 + e_sums1, dq1

        zero = (jnp.zeros((tq, 1), F32), jnp.zeros((tq, 1), F32), jnp.zeros((tq, HEAD_PAIR), F32))
        first = diagonal_blocks * qi
        carry = lax.fori_loop(0, first, lambda kj, cr: block(kj, cr, None), zero + zero)
        for d in range(diagonal_blocks):
            carry = block(first + d, carry, causal(d))
        dq_ref[...] = (jnp.where(head0, carry[2], carry[5]) * ATTN_SCALE).astype(BF16)

        @pl.when(qi == T // tq - 1)
        def _():
            for b in range(T // tk):
                dk_ref[b * tk:(b + 1) * tk, :] = dkt_ref[b].T.astype(BF16)
                dv_ref[b * tk:(b + 1) * tk, :] = dvt_ref[b].T.astype(BF16)

    blk = pl.BlockSpec((tq, HEAD_PAIR), lambda p, i: (i, p))
    seq = pl.BlockSpec((T, HEAD_PAIR), lambda p, i: (0, p))
    transposed = pltpu.VMEM((T // tk, HEAD_PAIR, tk), F32)
    return pl.pallas_call(
        body, name=name, grid=(N_HEADS // 2, T // tq),
        in_specs=_attn_specs(T, tq) + [blk, blk, AFTER], out_specs=[blk, seq, seq],
        out_shape=[jax.ShapeDtypeStruct((T, SB_WIDTH), BF16)] * 3,
        scratch_shapes=[pltpu.VMEM((T, HEAD_PAIR), BF16), pltpu.VMEM((T // tk, HEAD_PAIR, tk), BF16),
                        pltpu.VMEM((T // tk, HEAD_PAIR, tk), BF16), transposed, transposed],
        compiler_params=_params(("arbitrary", "arbitrary")),
    )(proj, proj, proj, do, ltot, _in_hbm(after))


def _mix_specs(T, D, tm, wbp, w_out):
    gate_col = (POOL_WIDTH + 3 * SB_WIDTH) // D
    row = lambda i: (i, 0)
    return [
        pl.BlockSpec((tm, D), row),
        pl.BlockSpec((tm, POOL_WIDTH), row),
        pl.BlockSpec((tm, SB_WIDTH), row),
        pl.BlockSpec((tm, D), lambda i: (i, gate_col)),
        pl.BlockSpec((tm, D), lambda i: (i, gate_col + 1)),
        pl.BlockSpec(wbp.shape, lambda i: (0, 0)),
        pl.BlockSpec(wbp.shape, lambda i: (0, 0)),
        pl.BlockSpec(w_out.shape, lambda i: (0, 0)),
    ]


def _mix_fwd(h, p, o, proj, wbp, wba, w_out, *, tm, name):
    T, D = h.shape
    tm = min(tm, T)

    def body(h_ref, p_ref, o_ref, glp_ref, gls_ref, wbp_ref, wba_ref, wout_ref, hout_ref, m_ref):
        halves = (pl.ds(0, tm // 2), pl.ds(tm // 2, tm // 2))
        wbp, wba, wout = wbp_ref[...], wba_ref[...], wout_ref[...]
        branches = [(_mm_nt(p_ref[rows, :].astype(BF16), wbp), _mm_nt(o_ref[rows, :].astype(BF16), wba))
                    for rows in halves]
        for rows, (yp, ys) in zip(halves, branches):
            m = (jax.nn.sigmoid(glp_ref[rows, :]) * yp + jax.nn.sigmoid(gls_ref[rows, :]) * ys).astype(BF16)
            m_ref[rows, :] = m
            hout_ref[rows, :] = h_ref[rows, :] + _mm(m, wout)

    row = lambda i: (i, 0)
    return pl.pallas_call(
        body, name=name, grid=(T // tm,),
        in_specs=_mix_specs(T, D, tm, wbp, w_out),
        out_specs=[pl.BlockSpec((tm, D), row), pl.BlockSpec((tm, D), row)],
        out_shape=[jax.ShapeDtypeStruct((T, D), F32), jax.ShapeDtypeStruct((T, D), BF16)],
        compiler_params=_params(("arbitrary",)),
    )(h, p, o, proj, proj, wbp, wba, w_out)


def _mix_bwd(dh, p, o, proj, wbp, wba, w_out, after, *, tm, name):
    T, D = dh.shape
    tm = min(tm, T)

    def body(dh_ref, p_ref, o_ref, glp_ref, gls_ref, wbp_ref, wba_ref, wout_ref, after_ref,
             dyp_ref, dys_ref, dp_ref, do_ref, dgl_ref):
        halves = (pl.ds(0, tm // 2), pl.ds(tm // 2, tm // 2))
        wbp, wba, wout = wbp_ref[...], wba_ref[...], wout_ref[...]
        products = [(_mm_nt(dh_ref[rows, :].astype(BF16), wout), _mm_nt(p_ref[rows, :].astype(BF16), wbp),
                     _mm_nt(o_ref[rows, :].astype(BF16), wba)) for rows in halves]
        for rows, (dm, yp, ys) in zip(halves, products):
            gp = jax.nn.sigmoid(glp_ref[rows, :])
            gs = jax.nn.sigmoid(gls_ref[rows, :])
            dyp = (dm * gp).astype(BF16)
            dys = (dm * gs).astype(BF16)
            dyp_ref[rows, :] = dyp
            dys_ref[rows, :] = dys
            dgl_ref[rows, :D] = (dm * yp * gp * (1.0 - gp)).astype(BF16)
            dgl_ref[rows, D:] = (dm * ys * gs * (1.0 - gs)).astype(BF16)
            dp_ref[rows, :] = _mm(dyp, wbp)
            do_ref[rows, :] = _mm(dys, wba)

    row = lambda i: (i, 0)
    return pl.pallas_call(
        body, name=name, grid=(T // tm,),
        in_specs=_mix_specs(T, D, tm, wbp, w_out) + [AFTER],
        out_specs=[pl.BlockSpec((tm, D), row), pl.BlockSpec((tm, D), row), pl.BlockSpec((tm, POOL_WIDTH), row),
                   pl.BlockSpec((tm, SB_WIDTH), row), pl.BlockSpec((tm, 2 * D), row)],
        out_shape=[jax.ShapeDtypeStruct((T, D), BF16), jax.ShapeDtypeStruct((T, D), BF16),
                   jax.ShapeDtypeStruct((T, POOL_WIDTH), F32), jax.ShapeDtypeStruct((T, SB_WIDTH), F32),
                   jax.ShapeDtypeStruct((T, 2 * D), BF16)],
        compiler_params=_params(("arbitrary",)),
    )(dh, p, o, proj, proj, wbp, wba, w_out, _in_hbm(after))


def _adamw_update(w, g, m, v):
    m_ = ADAM_B1 * m + (1.0 - ADAM_B1) * g
    v_ = ADAM_B2 * v + (1.0 - ADAM_B2) * (g * g)
    m_hat = m_ / (1.0 - ADAM_B1 ** ADAM_STEP)
    v_hat = v_ / (1.0 - ADAM_B2 ** ADAM_STEP)
    return -ADAM_LR * (m_hat / (jnp.sqrt(v_hat) + ADAM_EPS) + ADAM_WD * w), m_, v_


def _adamw(w, g, m, v, *, name):
    R, C = w.shape
    tr = _row_tile(R, C)

    def body(w_ref, g_ref, m_ref, v_ref, d_ref, nm_ref, nv_ref):
        d_ref[...], nm_ref[...], nv_ref[...] = _adamw_update(w_ref[...], g_ref[...], m_ref[...], v_ref[...])

    spec = pl.BlockSpec((tr, C), lambda i: (i, 0))
    return pl.pallas_call(
        body, name=name, grid=(R // tr,), in_specs=[spec] * 4, out_specs=[spec] * 3,
        out_shape=[jax.ShapeDtypeStruct((R, C), F32)] * 3,
        compiler_params=_params(("arbitrary",)),
    )(w, g, m, v)


def _position():
    return lax.axis_index("x"), lax.axis_index("y"), lax.axis_index("c")


def _all_gather(shards, *, name, collective_id):
    n = len(shards)
    n_copies = 9

    def body(*refs):
        ins, outs = refs[:n], refs[n:2 * n]
        send_sems, recv_sems, local_sems = refs[2 * n:]
        x, y, c = _position()
        me, sibling = (x, y, c), (x, y, 1 - c)
        x_nbr, y_nbr, diagonal = (1 - x, y, c), (x, 1 - y, c), (1 - x, 1 - y, c)
        other = lambda pos: (pos[0], pos[1], 1 - c)

        barrier = pltpu.get_barrier_semaphore()
        for peer in (sibling, x_nbr, y_nbr):
            pl.semaphore_signal(barrier, inc=1, device_id=peer, device_id_type=MESH)
        pl.semaphore_wait(barrier, 3)

        def block(a, pos, half=None):
            ref = outs[a].at[4 * pos[0] + 2 * pos[1] + pos[2]]
            rows = ref.shape[0] // 2
            return ref if half is None else ref.at[pl.ds(half * rows, rows)]

        def copy(a, k, pos, to, half=None, src=None):
            return pltpu.make_async_remote_copy(
                src_ref=block(a, pos, half) if src is None else src, dst_ref=block(a, pos, half),
                send_sem=send_sems.at[n_copies * a + k], recv_sem=recv_sems.at[n_copies * a + k],
                device_id=to, device_id_type=MESH)

        started = []
        for a in range(n):
            mine = pltpu.make_async_copy(ins[a], block(a, me), local_sems.at[a])
            mine.start()
            started.append(mine)
        sends = []
        for a in range(n):
            sends += [copy(a, 1, me, x_nbr, src=ins[a]), copy(a, 2, me, y_nbr, src=ins[a]),
                      copy(a, 0, me, sibling, src=ins[a])]
        for cp in sends:
            cp.start()

        def pass_on(copies):
            for cp in copies:
                cp.start()
                sends.append(cp)

        for a in range(n):
            copy(a, 1, x_nbr, me).wait_recv()
            pass_on([copy(a, 5, x_nbr, y_nbr, half=0), copy(a, 3, x_nbr, sibling)])
            copy(a, 2, y_nbr, me).wait_recv()
            pass_on([copy(a, 6, y_nbr, x_nbr, half=1), copy(a, 4, y_nbr, sibling)])
        for a in range(n):
            copy(a, 5, diagonal, me, half=0).wait_recv()
            pass_on([copy(a, 7, diagonal, sibling, half=0)])
            copy(a, 6, diagonal, me, half=1).wait_recv()
            pass_on([copy(a, 8, diagonal, sibling, half=1)])
        for a in range(n):
            copy(a, 0, sibling, me).wait_recv()
            copy(a, 3, other(x_nbr), me).wait_recv()
            copy(a, 4, other(y_nbr), me).wait_recv()
            copy(a, 7, other(diagonal), me, half=0).wait_recv()
            copy(a, 8, other(diagonal), me, half=1).wait_recv()
        for cp in sends:
            cp.wait_send()
        for cp in started:
            cp.wait()

    return pl.kernel(
        body, name=name,
        out_type=[jax.ShapeDtypeStruct((N_DEV,) + s.shape, s.dtype) for s in shards],
        mesh=plsc.ScalarSubcoreMesh(axis_name="sequencer", num_cores=1),
        scratch_types=[pltpu.SemaphoreType.DMA((n_copies * n,)), pltpu.SemaphoreType.DMA((n_copies * n,)),
                       pltpu.SemaphoreType.DMA((n,))],
        compiler_params=pltpu.CompilerParams(collective_id=collective_id),
    )(*shards)


def _chip_sums(group, *, name):
    n = len(group)
    shapes = [g.shape[1:] for g in group]

    def body(*refs):
        g_refs, partials_out, own_out = refs[:n], refs[n:3 * n:2], refs[n + 1:3 * n:2]
        mines, theirs = refs[3 * n:7 * n:4], refs[3 * n + 1:7 * n:4]
        partials, out_refs = refs[3 * n + 2:7 * n:4], refs[3 * n + 3:7 * n:4]
        send_sems, recv_sems, local_sems, store_sems = refs[7 * n:]
        x, y, c = _position()
        my_chip = 2 * x + y

        def swap(a, s):
            return pltpu.make_async_remote_copy(
                src_ref=g_refs[a].at[2 * s + (1 - c)], dst_ref=theirs[a].at[s],
                send_sem=send_sems.at[4 * a + s], recv_sem=recv_sems.at[4 * a + s],
                device_id=(x, y, 1 - c), device_id_type=MESH)

        def load(a, s):
            return pltpu.make_async_copy(g_refs[a].at[2 * s + c], mines[a].at[s], local_sems.at[4 * a + s])

        def store(a, s, own):
            if own:
                return pltpu.make_async_copy(out_refs[a], own_out[a], store_sems.at[4 * a + s])
            j = (s ^ my_chip) - 1
            return pltpu.make_async_copy(partials[a].at[j], partials_out[a].at[j], store_sems.at[4 * a + s])

        for a in range(n):
            for s in range(4):
                swap(a, s).start()
                load(a, s).start()

        for a, (R, C) in enumerate(shapes):
            rc = 128 if R % 128 == 0 else R

            def chip_sum(chip, rows):
                return mines[a][chip, rows, :].astype(F32) + theirs[a][chip, rows, :].astype(F32)

            for s in range(4):
                load(a, s).wait()
                swap(a, s).wait_recv()

                @pl.when(s == my_chip)
                def _():
                    @pl.loop(0, R // rc)
                    def _(t):
                        rows = pl.ds(pl.multiple_of(t * rc, rc), rc)
                        out_refs[a][rows, :] = chip_sum(s, rows)
                    store(a, s, True).start()

                @pl.when(s != my_chip)
                def _():
                    @pl.loop(0, R // rc)
                    def _(t):
                        rows = pl.ds(pl.multiple_of(t * rc, rc), rc)
                        partials[a][(s ^ my_chip) - 1, rows, :] = chip_sum(s, rows).astype(BF16)
                    store(a, s, False).start()

        for a in range(n):
            for s in range(4):
                swap(a, s).wait_send()
                pl.when(s == my_chip)(store(a, s, True).wait)
                pl.when(s != my_chip)(store(a, s, False).wait)

    hbm = pl.BlockSpec(memory_space=pl.ANY)
    outs = pl.pallas_call(
        body, name=name,
        in_specs=[hbm] * n, out_specs=[hbm] * (2 * n),
        out_shape=[shape for R, C in shapes
                   for shape in (jax.ShapeDtypeStruct((3, R, C), BF16), jax.ShapeDtypeStruct((R, C), F32))],
        scratch_shapes=[scratch for R, C in shapes for scratch in (
            pltpu.VMEM((4, R, C), BF16), pltpu.VMEM((4, R, C), BF16), pltpu.VMEM((3, R, C), BF16),
            pltpu.VMEM((R, C), F32))] + [pltpu.SemaphoreType.DMA((4 * n,))] * 4,
        compiler_params=_params(),
    )(*group)
    return [(outs[2 * a], outs[2 * a + 1]) for a in range(n)]


def _cross_chips(partials, *, name, collective_id):
    n = len(partials)

    def body(*refs):
        ins, outs = refs[:n], refs[n:2 * n]
        send_sems, recv_sems = refs[2 * n:]
        x, y, c = _position()
        my_chip = 2 * x + y
        peers = [((my_chip ^ j) // 2, (my_chip ^ j) % 2, c) for j in (1, 2, 3)]

        barrier = pltpu.get_barrier_semaphore()
        for peer in peers:
            pl.semaphore_signal(barrier, inc=1, device_id=peer, device_id_type=MESH)
        pl.semaphore_wait(barrier, 3)

        copies = [
            pltpu.make_async_remote_copy(
                src_ref=ins[a].at[j], dst_ref=outs[a].at[j],
                send_sem=send_sems.at[3 * a + j], recv_sem=recv_sems.at[3 * a + j],
                device_id=peers[j], device_id_type=MESH)
            for a in range(n) for j in range(3)]
        for cp in copies:
            cp.start()
        for cp in copies:
            cp.wait_recv()
        for cp in copies:
            cp.wait_send()

    return pl.kernel(
        body, name=name,
        out_type=[jax.ShapeDtypeStruct(p.shape, p.dtype) for p in partials],
        mesh=plsc.ScalarSubcoreMesh(axis_name="sequencer", num_cores=1),
        scratch_types=[pltpu.SemaphoreType.DMA((3 * n,)), pltpu.SemaphoreType.DMA((3 * n,))],
        compiler_params=pltpu.CompilerParams(collective_id=collective_id),
    )(*partials)


def _cross_chips_and_gather(partials, slab, *, name, collective_id):
    n = len(partials)

    def body(*refs):
        part_refs, slab_ref = refs[:n], refs[n]
        landed_refs, slabs_ref = refs[n + 1:2 * n + 1], refs[2 * n + 1]
        send_sems, recv_sems, local_sem = refs[2 * n + 2:]
        x, y, c = _position()
        me, my_chip = 4 * x + 2 * y + c, 2 * x + y
        others = [me ^ k for k in range(1, N_DEV)]
        ids = [(o // 4, (o // 2) % 2, o % 2) for o in others]

        barrier = pltpu.get_barrier_semaphore()
        for peer in ids:
            pl.semaphore_signal(barrier, inc=1, device_id=peer, device_id_type=MESH)
        pl.semaphore_wait(barrier, N_DEV - 1)

        mine = pltpu.make_async_copy(slab_ref, slabs_ref.at[me], local_sem)
        mine.start()
        sends = [
            pltpu.make_async_remote_copy(
                src_ref=part_refs[a].at[j], dst_ref=landed_refs[a].at[j],
                send_sem=send_sems.at[3 * a + j], recv_sem=recv_sems.at[3 * a + j],
                device_id=((my_chip ^ (j + 1)) // 2, (my_chip ^ (j + 1)) % 2, c), device_id_type=MESH)
            for a in range(n) for j in range(3)]
        sends += [
            pltpu.make_async_remote_copy(
                src_ref=slab_ref, dst_ref=slabs_ref.at[me],
                send_sem=send_sems.at[3 * n + k], recv_sem=recv_sems.at[3 * n + k],
                device_id=ids[k], device_id_type=MESH)
            for k in range(N_DEV - 1)]
        arrivals = sends[:3 * n] + [
            pltpu.make_async_remote_copy(
                src_ref=slab_ref, dst_ref=slabs_ref.at[others[k]],
                send_sem=send_sems.at[3 * n + k], recv_sem=recv_sems.at[3 * n + k],
                device_id=ids[k], device_id_type=MESH)
            for k in range(N_DEV - 1)]
        for cp in sends:
            cp.start()
        for cp in arrivals:
            cp.wait_recv()
        for cp in sends:
            cp.wait_send()
        mine.wait()

    n_sems = 3 * n + N_DEV - 1
    outs = pl.kernel(
        body, name=name,
        out_type=[jax.ShapeDtypeStruct(p.shape, p.dtype) for p in partials]
                 + [jax.ShapeDtypeStruct((N_DEV,) + slab.shape, slab.dtype)],
        mesh=plsc.ScalarSubcoreMesh(axis_name="sequencer", num_cores=1),
        scratch_types=[pltpu.SemaphoreType.DMA((n_sems,)), pltpu.SemaphoreType.DMA((n_sems,)), pltpu.SemaphoreType.DMA],
        compiler_params=pltpu.CompilerParams(collective_id=collective_id),
    )(*partials, slab)
    return outs[:n], outs[n]


def _sum_devices(gathered, after, *, name):
    _, R, C = gathered.shape

    def body(in_ref, after_ref, out_ref):
        total = in_ref[0]
        for d in range(1, N_DEV):
            total = total + in_ref[d]
        out_ref[...] = total

    return pl.pallas_call(
        body, name=name, grid=(1,),
        in_specs=[pl.BlockSpec((N_DEV, R, C), lambda i: (0, 0, 0)), AFTER],
        out_specs=pl.BlockSpec((R, C), lambda i: (0, 0)),
        out_shape=jax.ShapeDtypeStruct((R, C), F32),
        compiler_params=_params(("arbitrary",)),
    )(gathered, _in_hbm(after))


def _owner_sum_adamw(own, landed, w, m, v, after, *, transposed, name, group=None, into=()):
    H, R, C = w.shape
    tr = R // 2
    first_group = 0 if group is None else group

    def body(own_ref, landed_ref, w_ref, m_ref, v_ref, after_ref, *rest):
        g_ref, d_ref, nm_ref, nv_ref = rest[len(into):]
        total = own_ref[...]
        for j in range(3):
            total = total + landed_ref[j].astype(F32)
        if transposed:
            total = total.T
        g_ref[...] = total
        d_ref[...], nm_ref[...], nv_ref[...] = _adamw_update(w_ref[...], total, m_ref[...], v_ref[...])

    spec = pl.BlockSpec((None, tr, C), lambda h, i: (first_group + h, i, 0))
    if transposed:
        own_spec = pl.BlockSpec((None, C, tr), lambda h, i: (h, 0, i))
        landed_spec = pl.BlockSpec((3, None, C, tr), lambda h, i: (0, h, 0, i))
    else:
        own_spec = pl.BlockSpec((None, tr, C), lambda h, i: (h, i, 0))
        landed_spec = pl.BlockSpec((3, None, tr, C), lambda h, i: (0, h, i, 0))
    n_in = 6
    return pl.pallas_call(
        body, name=name, grid=(own.shape[0], R // tr),
        in_specs=[own_spec, landed_spec, spec, spec, spec, AFTER] + [pl.BlockSpec(memory_space=pl.ANY)] * len(into),
        out_specs=[spec] * 4,
        out_shape=[jax.ShapeDtypeStruct((H, R, C), F32)] * 4,
        input_output_aliases={n_in + j: j for j in range(len(into))},
        compiler_params=_params(("arbitrary", "arbitrary")),
    )(own, landed, w, m, v, _in_hbm(after), *into)


def _local_step(x, target, norms, pool_w_group, pool_scale, wgu1, wd1, w_in, wbp, wba, w_out, wgu2, wd2, exchange):
    n1g, nmg, n2g, nfg = norms
    D = x.shape[1]
    gu1, hid1 = _ffn_up(x, n1g, wgu1, tm=1024, name="ffn1_up")
    h1 = _ffn_down(x, hid1, wd1, tm=512, name="ffn1_down")
    un, proj = _inproj_fwd(h1, nmg, w_in, tm=1024, name="inproj_fwd")
    p = _pool_fwd(proj, pool_w_group, pool_scale, name="pool_fwd")
    o, ltot = _attn_fwd(proj, name="attn_fwd")
    h2, m = _mix_fwd(h1, p, o, proj, wbp, wba, w_out, tm=512, name="mix_fwd")
    gu2, hid2 = _ffn_up(h2, n2g, wgu2, tm=1024, name="ffn2_up")
    h3 = _ffn_down(h2, hid2, wd2, tm=512, name="ffn2_down")
    dh3, df2, loss, d_nf = _loss_bwd(h3, target, nfg, tm=512, name="loss_bwd")

    dh2, d_n2, n2, dgu2 = _ffn_bwd(dh3, df2, h2, n2g, gu2, wgu2, wd2, df2, tm=512, name="ffn2_bwd")
    d_wd2 = _wgrad_down(hid2, df2, tk=WGRAD_TOKENS, name="ffn2_wgrad_down")
    d_wgu2 = _wgrad_gate_up(n2, dgu2, tk=WGRAD_TOKENS, name="ffn2_wgrad_gate_up")
    (g_wd2, g_wgu2), token = exchange("ffn2", [d_wd2.reshape(N_DEV, FF_SHARD_PAD, D), d_wgu2])

    dyp, dys, dp, do, dgl = _mix_bwd(dh2, p, o, proj, wbp, wba, w_out, token, tm=512, name="mix_bwd")
    d_wout = _wgrad_full(m, dh2, tk=WGRAD_TOKENS, name="wgrad_out")
    d_wbp = _wgrad_full(dyp, p, tk=WGRAD_TOKENS, name="wgrad_branch_pool")
    d_wba = _wgrad_full(dys, o, tk=WGRAD_TOKENS, name="wgrad_branch_attn")
    by_owner = lambda g: g.reshape(N_DEV, g.shape[0] // N_DEV, g.shape[1])
    (g_wbp, g_wba, g_wout), token = exchange("mix", [by_owner(d_wbp), by_owner(d_wba), by_owner(d_wout)])
    dxp, d_wgroup, d_scale = _pool_bwd(dp, proj, pool_w_group, pool_scale, name="pool_bwd")
    dq, dk, dv = _attn_bwd(proj, do, ltot, token, name="attn_bwd")
    dproj_parts = [dxp, dq, dk, dv, dgl]
    dh1, df1, d_nm = _inproj_bwd(dproj_parts, dh2, h1, nmg, w_in, tm=512, name="inproj_bwd")
    d_win = _wgrad_in(dproj_parts, un, name="wgrad_in")
    d_wd1 = _wgrad_down(hid1, df1, tk=WGRAD_TOKENS, name="ffn1_wgrad_down")
    (g_win, g_wd1, replicated_early), token = exchange(
        "w_in_ffn1_down", [d_win, d_wd1.reshape(N_DEV, FF_SHARD_PAD, D), d_nm, d_n2, d_nf, d_scale, d_wgroup, loss])

    dx, d_n1, n1, dgu1 = _ffn_bwd(dh1, df1, x, n1g, gu1, wgu1, wd1, token, tm=512, name="ffn1_bwd")
    d_wgu1_a = _wgrad_gate_up(n1, dgu1, tk=WGRAD_TOKENS, name="ffn1_wgrad_gate_up_a", part=0, parts=2)
    (g_wgu1_a, replicated_late), token = exchange("ffn1_gate_up_a", [d_wgu1_a, d_n1])
    d_wgu1_b = _wgrad_gate_up(n1, dgu1, tk=WGRAD_TOKENS, name="ffn1_wgrad_gate_up_b", part=1, parts=2)
    (g_wgu1_b,), token = exchange("last", [d_wgu1_b])
    g_wgu1 = (g_wgu1_a, g_wgu1_b)

    sharded = (g_wgu1, g_wd1, g_win, g_wbp, g_wba, g_wout, g_wgu2, g_wd2)
    return dx, sharded, (replicated_late, replicated_early), token


def _hidden_major(w):
    return jnp.swapaxes(w[0], 0, 1)


def _pad_gate_up(wt):
    d = wt.shape[1]
    wt = wt.astype(BF16).reshape(2, FF_SHARD, d)
    return jnp.pad(wt, ((0, 0), (0, FF_SHARD_PAD - FF_SHARD), (0, 0))).reshape(2 * FF_SHARD_PAD, d)


def _unpad_gate_up(gt):
    d = gt.shape[1]
    return gt.reshape(2, FF_SHARD_PAD, d)[:, :FF_SHARD].reshape(2 * FF_SHARD, d)


def _pad_down(w):
    return jnp.pad(w.astype(BF16), ((0, FF_SHARD_PAD - FF_SHARD), (0, 0)))


def kernel(x, ffn1_norm, ffn1_w_gate_up, ffn1_w_down, mix_norm, w_in, pool_w_group, pool_scale, w_branch_pool, w_branch_attn, w_out, ffn2_norm, ffn2_w_gate_up, ffn2_w_down, final_norm, loss_target, m_ffn1_norm, m_ffn1_w_gate_up, m_ffn1_w_down, m_mix_norm, m_w_in, m_pool_w_group, m_pool_scale, m_w_branch_pool, m_w_branch_attn, m_w_out, m_ffn2_norm, m_ffn2_w_gate_up, m_ffn2_w_down, m_final_norm, v_ffn1_norm, v_ffn1_w_gate_up, v_ffn1_w_down, v_mix_norm, v_w_in, v_pool_w_group, v_pool_scale, v_w_branch_pool, v_w_branch_attn, v_w_out, v_ffn2_norm, v_ffn2_w_gate_up, v_ffn2_w_down, v_final_norm):
    D = x.shape[-1]
    weights = dict(ffn1_norm=ffn1_norm, ffn1_w_gate_up=ffn1_w_gate_up, ffn1_w_down=ffn1_w_down, mix_norm=mix_norm,
                   w_in=w_in, pool_w_group=pool_w_group, pool_scale=pool_scale, w_branch_pool=w_branch_pool,
                   w_branch_attn=w_branch_attn, w_out=w_out, ffn2_norm=ffn2_norm, ffn2_w_gate_up=ffn2_w_gate_up,
                   ffn2_w_down=ffn2_w_down, final_norm=final_norm)
    first = dict(ffn1_norm=m_ffn1_norm, ffn1_w_gate_up=m_ffn1_w_gate_up, ffn1_w_down=m_ffn1_w_down,
                 mix_norm=m_mix_norm, w_in=m_w_in, pool_w_group=m_pool_w_group, pool_scale=m_pool_scale,
                 w_branch_pool=m_w_branch_pool, w_branch_attn=m_w_branch_attn, w_out=m_w_out,
                 ffn2_norm=m_ffn2_norm, ffn2_w_gate_up=m_ffn2_w_gate_up, ffn2_w_down=m_ffn2_w_down,
                 final_norm=m_final_norm)
    second = dict(ffn1_norm=v_ffn1_norm, ffn1_w_gate_up=v_ffn1_w_gate_up, ffn1_w_down=v_ffn1_w_down,
                  mix_norm=v_mix_norm, w_in=v_w_in, pool_w_group=v_pool_w_group, pool_scale=v_pool_scale,
                  w_branch_pool=v_w_branch_pool, w_branch_attn=v_w_branch_attn, w_out=v_w_out,
                  ffn2_norm=v_ffn2_norm, ffn2_w_gate_up=v_ffn2_w_gate_up, ffn2_w_down=v_ffn2_w_down,
                  final_norm=v_final_norm)
    order = list(weights)

    wgu1, = _all_gather([_pad_gate_up(_hidden_major(ffn1_w_gate_up))], name="all_gather_ffn1_gate_up", collective_id=0)
    wd1, = _all_gather([_pad_down(ffn1_w_down[0])], name="all_gather_ffn1_down", collective_id=10)
    transposed = lambda w: jnp.swapaxes(w[0], 0, 1).astype(BF16)
    win_g, = _all_gather([transposed(w_in)], name="all_gather_w_in", collective_id=1)
    wbp_g, wba_g = _all_gather([transposed(w_branch_pool), transposed(w_branch_attn)],
                               name="all_gather_branches", collective_id=2)
    wout_g, = _all_gather([w_out[0].astype(BF16)], name="all_gather_w_out", collective_id=11)
    wgu2, wd2 = _all_gather([_pad_gate_up(_hidden_major(ffn2_w_gate_up)), _pad_down(ffn2_w_down[0])],
                            name="all_gather_ffn2", collective_id=3)
    whole = lambda g: g.reshape(g.shape[0] * g.shape[1], g.shape[2])
    wd1, wd2, win_g, wbp_g, wba_g, wout_g = (whole(g) for g in (wd1, wd2, win_g, wbp_g, wba_g, wout_g))

    cross_ids = {"ffn2": 4, "mix": 5, "w_in_ffn1_down": 8, "ffn1_gate_up_a": 9, "last": 7}
    small = ["ffn1_norm", "mix_norm", "ffn2_norm", "final_norm", "pool_scale", "pool_w_group"]

    def tile_rows(a):
        a = a.reshape(-1, 128)
        return jnp.pad(a, ((0, -a.shape[0] % 8), (0, 0)))

    def exchange(tag, group):
        grads = [g for g in group if g.dtype == BF16]
        extras = [tile_rows(g) for g in group if g.dtype != BF16]
        sums = _chip_sums(grads, name="chip_sums_" + tag)
        partials = [s[0] for s in sums]
        handles = []
        if extras:
            landed, slabs = _cross_chips_and_gather(partials, jnp.concatenate(extras, axis=0),
                                                    name="cross_chips_" + tag, collective_id=cross_ids[tag])
            handles = [slabs]
        else:
            landed = _cross_chips(partials, name="cross_chips_" + tag, collective_id=cross_ids[tag])
        return [(s[1], l) for s, l in zip(sums, landed)] + handles, sums[-1][1]

    norms = (ffn1_norm, mix_norm, ffn2_norm, final_norm.reshape(1, D))
    dx, sharded, (slabs_late, slabs_early), last = _local_step(
        x[0], loss_target[0], norms, pool_w_group[0], pool_scale, wgu1, wd1, win_g, wbp_g, wba_g, wout_g, wgu2, wd2,
        exchange)
    names = ["ffn1_w_gate_up", "ffn1_w_down", "w_in", "w_branch_pool", "w_branch_attn", "w_out",
             "ffn2_w_gate_up", "ffn2_w_down"]
    handles = dict(zip(names, sharded))
    grads, delta, new_m, new_v = {}, {}, {}, {}
    loss_out = []

    def update_replicated(after):
        rows = [weights[k].size // 128 for k in small]
        padded_rows = [-(-r // 8) * 8 for r in rows]
        starts = [sum(padded_rows[:i]) for i in range(len(rows) + 1)]
        total = jnp.concatenate([_sum_devices(slabs_late, after, name="sum_replicated_late"),
                                 _sum_devices(slabs_early, after, name="sum_replicated_early")], axis=0)
        loss_out.append(total[starts[-1], 0])
        small_w = jnp.concatenate([tile_rows(weights[k]) for k in small], axis=0)
        small_m = jnp.concatenate([tile_rows(first[k]) for k in small], axis=0)
        small_v = jnp.concatenate([tile_rows(second[k]) for k in small], axis=0)
        small_out = _adamw(small_w, total[:starts[-1]], small_m, small_v, name="adamw_replicated")
        for name_, start, n_rows in zip(small, starts, rows):
            shape = weights[name_].shape
            grads[name_] = total[start:start + n_rows].reshape(shape)
            delta[name_], new_m[name_], new_v[name_] = (a[start:start + n_rows].reshape(shape) for a in small_out)
        return small_out[0]

    after = last
    for k in ("ffn2_w_down", "ffn2_w_gate_up", "w_branch_pool", "w_branch_attn", "w_out", "w_in", "ffn1_w_down",
              "ffn1_w_gate_up"):
        hidden_major = k.endswith("w_gate_up")
        view = _hidden_major if hidden_major else (lambda a: a[0])
        back = (lambda a: jnp.swapaxes(a, 0, 1)[None]) if hidden_major else (lambda a: a[None])
        groups = 2 if hidden_major else 1
        by_group = lambda a: a.reshape(a.shape[:-2] + (groups, a.shape[-2] // groups, a.shape[-1]))
        state = [by_group(view(a[k])) for a in (weights, first, second)]
        if isinstance(handles[k][0], tuple):
            (own_a, landed_a), (own_b, landed_b) = handles[k]
            out = _owner_sum_adamw(own_a[None], landed_a[:, None], *state, after, name="adamw_" + k + "_a",
                                   transposed=False, group=0)
            out = _owner_sum_adamw(own_b[None], landed_b[:, None], *state, update_replicated(out[1]),
                                   name="adamw_" + k + "_b", transposed=False, group=1, into=out)
        else:
            own, landed = handles[k]
            out = _owner_sum_adamw(by_group(own), by_group(landed), *state, after, name="adamw_" + k,
                                   transposed=k in ("w_in", "w_branch_pool", "w_branch_attn"))
        after = out[1]
        grads[k], delta[k], new_m[k], new_v[k] = (back(a.reshape(-1, a.shape[-1])) for a in out)

    return (loss_out[0], dx[None], *[grads[k] for k in order], *[delta[k] for k in order],
            *[new_m[k] for k in order], *[new_v[k] for k in order])
```

```python
import jax
import jax.numpy as jnp
from jax import lax
from jax.experimental import pallas as pl
from jax.experimental.pallas import tpu as pltpu
from jax.experimental.pallas import tpu_sc as plsc

F32 = jnp.float32
BF16 = jnp.bfloat16
MESH = pl.DeviceIdType.MESH

RMS_EPS = 1e-6
N_DEV = 8
N_HEADS = 8
HEAD_DIM = 64
HEAD_PAIR = 2 * HEAD_DIM
POOL_WINDOWS = (2, 4, 8, 16)
POOL_GROUP = 128
POOL_WIDTH = 512
SB_WIDTH = 512
FF_SHARD = 352
FF_SHARD_PAD = 384
ATTN_K_BLOCK = 256
ATTN_Q_BLOCK_FWD = 512
ATTN_Q_BLOCK_BWD = 256
ATTN_SCALE = 0.125

ADAM_LR = 0.001
ADAM_B1 = 0.9
ADAM_B2 = 0.999
ADAM_EPS = 1e-08
ADAM_WD = 0.01
ADAM_STEP = 10

VMEM_LIMIT = 48 << 20
WGRAD_TOKENS = 2048


def _params(dims=None):
    return pltpu.CompilerParams(dimension_semantics=dims, vmem_limit_bytes=VMEM_LIMIT)


def _mm(a, b):
    return jnp.dot(a, b, preferred_element_type=F32)


def _mm_nt(a, b):
    return lax.dot_general(a, b, (((1,), (1,)), ((), ())), preferred_element_type=F32)


def _mm_tn(a, b):
    return lax.dot_general(a, b, (((0,), (0,)), ((), ())), preferred_element_type=F32)


def _row_tile(rows, cols):
    limit = max(8, (512 * 1024) // cols)
    return max(t for t in range(8, rows + 1, 8) if rows % t == 0 and (t <= limit or t == 8))


def _rstd(xf):
    return lax.rsqrt(jnp.mean(xf * xf, axis=-1, keepdims=True) + RMS_EPS)


def _rms_bwd(xf, gain, dn):
    r = _rstd(xf)
    xh = xf * r
    dgain = jnp.sum(dn * xh, axis=0, keepdims=True)
    dxh = dn * gain
    dx = r * (dxh - xh * jnp.mean(dxh * xh, axis=-1, keepdims=True))
    return dx, dgain


def _ffn_up(x, gain, wgu, *, tm, name):
    T, D = x.shape
    tm = min(tm, T)
    nb, bw = wgu.shape[0] // 2, wgu.shape[1]

    def body(x_ref, gain_ref, wg_ref, wu_ref, gu_ref, hid_ref, n_scr):
        @pl.when(pl.program_id(1) == 0)
        def _():
            xf = x_ref[...]
            n_scr[...] = (xf * _rstd(xf) * gain_ref[...]).astype(BF16)

        halves = (pl.ds(0, tm // 2), pl.ds(tm // 2, tm // 2))
        wg, wu = wg_ref[...], wu_ref[...]
        gus = [(_mm_nt(n_scr[rows, :], wg), _mm_nt(n_scr[rows, :], wu)) for rows in halves]
        for rows, (g, u) in zip(halves, gus):
            gu_ref[0, rows, :] = g.astype(BF16)
            gu_ref[1, rows, :] = u.astype(BF16)
            hid_ref[rows, :] = (g * jax.nn.sigmoid(g) * u).astype(BF16)

    return pl.pallas_call(
        body, name=name, grid=(T // tm, nb),
        in_specs=[
            pl.BlockSpec((tm, D), lambda i, j: (i, 0)),
            pl.BlockSpec((1, D), lambda i, j: (0, 0)),
            pl.BlockSpec((None, bw, D), lambda i, j: (j, 0, 0)),
            pl.BlockSpec((None, bw, D), lambda i, j: (j + nb, 0, 0)),
        ],
        out_specs=[
            pl.BlockSpec((2, tm, bw), lambda i, j: (0, i, j)),
            pl.BlockSpec((tm, bw), lambda i, j: (i, j)),
        ],
        out_shape=[jax.ShapeDtypeStruct((2, T, nb * bw), BF16), jax.ShapeDtypeStruct((T, nb * bw), BF16)],
        scratch_shapes=[pltpu.VMEM((tm, D), BF16)],
        compiler_params=_params(("arbitrary", "arbitrary")),
    )(x, gain, wgu, wgu)


def _ffn_down(x, hid, wd, *, tm, name):
    T, D = x.shape
    tm = min(tm, T)
    F = hid.shape[1]

    def body(x_ref, hid_ref, wd_ref, h_ref):
        h_ref[...] = x_ref[...] + 0.5 * _mm(hid_ref[...], wd_ref[...])

    return pl.pallas_call(
        body, name=name, grid=(T // tm,),
        in_specs=[
            pl.BlockSpec((tm, D), lambda i: (i, 0)),
            pl.BlockSpec((tm, F), lambda i: (i, 0)),
            pl.BlockSpec((F, D), lambda i: (0, 0)),
        ],
        out_specs=pl.BlockSpec((tm, D), lambda i: (i, 0)),
        out_shape=jax.ShapeDtypeStruct((T, D), F32),
        compiler_params=_params(("arbitrary",)),
    )(x, hid, wd)


AFTER = pl.BlockSpec(memory_space=pltpu.HBM)


def _in_hbm(token):
    return pltpu.with_memory_space_constraint(token, pltpu.HBM)


def _ffn_bwd(dh, df, x, gain, gu, wgu, wd, after, *, tm, name):
    T, D = x.shape
    tm = min(tm, T)
    nb, bw = wgu.shape[0] // 2, wgu.shape[1]

    def body(dh_ref, df_ref, x_ref, gain_ref, gu_ref, wg_ref, wu_ref, wd_ref, after_ref,
             dx_ref, dgain_ref, n_ref, dgu_ref, dn_acc):
        i, j = pl.program_id(0), pl.program_id(1)

        @pl.when(j == 0)
        def _():
            xf = x_ref[...]
            n_ref[...] = (xf * _rstd(xf) * gain_ref[...]).astype(BF16)
            dn_acc[...] = jnp.zeros_like(dn_acc)

        @pl.when((i == 0) & (j == 0))
        def _():
            dgain_ref[...] = jnp.zeros_like(dgain_ref)

        halves = (pl.ds(0, tm // 2), pl.ds(tm // 2, tm // 2))
        wd, wg, wu = wd_ref[...], wg_ref[...], wu_ref[...]
        dhids = [_mm_nt(df_ref[rows, :], wd) for rows in halves]
        for rows, dhid in zip(halves, dhids):
            g = gu_ref[0, rows, :].astype(F32)
            u = gu_ref[1, rows, :].astype(F32)
            s = jax.nn.sigmoid(g)
            silu = g * s
            dg = (dhid * u * (s * (1.0 + g * (1.0 - s)))).astype(BF16)
            du = (dhid * silu).astype(BF16)
            dgu_ref[0, rows, :] = dg
            dgu_ref[1, rows, :] = du
            dn_acc[rows, :] += _mm(dg, wg) + _mm(du, wu)

        @pl.when(j == nb - 1)
        def _():
            dx, dgain = _rms_bwd(x_ref[...], gain_ref[...], dn_acc[...])
            dx_ref[...] = dh_ref[...] + dx
            dgain_ref[...] += dgain

    row = lambda i, j: (i, 0)
    return pl.pallas_call(
        body, name=name, grid=(T // tm, nb),
        in_specs=[
            pl.BlockSpec((tm, D), row),
            pl.BlockSpec((tm, D), row),
            pl.BlockSpec((tm, D), row),
            pl.BlockSpec((1, D), lambda i, j: (0, 0)),
            pl.BlockSpec((2, tm, bw), lambda i, j: (0, i, j)),
            pl.BlockSpec((None, bw, D), lambda i, j: (j, 0, 0)),
            pl.BlockSpec((None, bw, D), lambda i, j: (j + nb, 0, 0)),
            pl.BlockSpec((bw, D), lambda i, j: (j, 0)),
            AFTER,
        ],
        out_specs=[
            pl.BlockSpec((tm, D), row),
            pl.BlockSpec((1, D), lambda i, j: (0, 0)),
            pl.BlockSpec((tm, D), row),
            pl.BlockSpec((2, tm, bw), lambda i, j: (0, i, j)),
        ],
        out_shape=[
            jax.ShapeDtypeStruct((T, D), F32),
            jax.ShapeDtypeStruct((1, D), F32),
            jax.ShapeDtypeStruct((T, D), BF16),
            jax.ShapeDtypeStruct((2, T, nb * bw), BF16),
        ],
        scratch_shapes=[pltpu.VMEM((tm, D), F32)],
        compiler_params=_params(("arbitrary", "arbitrary")),
    )(dh, df, x, gain, gu, wgu, wgu, wd, _in_hbm(after))


def _wgrad(a, b, *, grid, a_spec, b_spec, out_spec, out_shape, acc_shape, name):
    nk = grid[2]

    def body(a_ref, b_ref, o_ref, acc):
        k = pl.program_id(2)

        @pl.when(k == 0)
        def _():
            acc[...] = jnp.zeros_like(acc)

        acc[...] += _mm_tn(a_ref[...].astype(BF16), b_ref[...].astype(BF16))

        @pl.when(k == nk - 1)
        def _():
            o_ref[...] = acc[...].astype(o_ref.dtype)

    return pl.pallas_call(
        body, name=name, grid=grid, in_specs=[a_spec, b_spec], out_specs=out_spec,
        out_shape=jax.ShapeDtypeStruct(out_shape, BF16),
        scratch_shapes=[pltpu.VMEM(acc_shape, F32)],
        compiler_params=_params(("arbitrary", "arbitrary", "arbitrary")),
    )(a, b)


def _wgrad_gate_up(n, dgu, *, tk, name, part=0, parts=1):
    T, D = n.shape
    tk = min(tk, T)
    owner_rows = FF_SHARD_PAD * 2
    nb = dgu.shape[2] // owner_rows
    bw = owner_rows // parts
    return _wgrad(
        dgu, n, grid=(2 * nb, 1, T // tk), name=name,
        a_spec=pl.BlockSpec((None, tk, bw), lambda m, c, k: (m // nb, k, parts * (m % nb) + part)),
        b_spec=pl.BlockSpec((tk, D), lambda m, c, k: (k, 0)),
        out_spec=pl.BlockSpec((None, bw, D), lambda m, c, k: (m, 0, 0)),
        out_shape=(2 * nb, bw, D), acc_shape=(bw, D))


def _wgrad_down(hid, df, *, tk, name):
    T, D = df.shape
    tk = min(tk, T)
    bw = FF_SHARD_PAD * 2
    nb = hid.shape[1] // bw
    return _wgrad(
        hid, df, grid=(nb, 1, T // tk), name=name,
        a_spec=pl.BlockSpec((tk, bw), lambda m, c, k: (k, m)),
        b_spec=pl.BlockSpec((tk, D), lambda m, c, k: (k, 0)),
        out_spec=pl.BlockSpec((bw, D), lambda m, c, k: (m, 0)),
        out_shape=(nb * bw, D), acc_shape=(bw, D))


def _wgrad_in(dparts, un, *, name):
    T, D = un.shape
    bw = sum(p.shape[1] for p in dparts) // N_DEV
    first = [sum(p.shape[1] for p in dparts[:i]) // bw for i in range(len(dparts) + 1)]

    def body(*refs):
        dp_refs, un_ref, o_ref = refs[:-2], refs[-2], refs[-1]
        m = pl.program_id(0)
        for dp_ref, lo, hi in zip(dp_refs, first[:-1], first[1:]):
            @pl.when((m >= lo) & (m < hi))
            def _():
                o_ref[...] = _mm_tn(dp_ref[...], un_ref[...]).astype(o_ref.dtype)

    def piece_spec(lo, hi):
        return pl.BlockSpec((T, bw), lambda m: (0, jnp.clip(m - lo, 0, hi - lo - 1)))

    return pl.pallas_call(
        body, name=name, grid=(N_DEV,),
        in_specs=[piece_spec(lo, hi) for lo, hi in zip(first[:-1], first[1:])] + [pl.BlockSpec((T, D), lambda m: (0, 0))],
        out_specs=pl.BlockSpec((None, bw, D), lambda m: (m, 0, 0)),
        out_shape=jax.ShapeDtypeStruct((N_DEV, bw, D), BF16),
        compiler_params=_params(("arbitrary",)),
    )(*dparts, un)


def _wgrad_full(a, b, *, tk, name):
    T, M = a.shape
    tk = min(tk, T)
    N = b.shape[1]
    return _wgrad(
        a, b, grid=(1, 1, T // tk), name=name,
        a_spec=pl.BlockSpec((tk, M), lambda m, c, k: (k, 0)),
        b_spec=pl.BlockSpec((tk, N), lambda m, c, k: (k, 0)),
        out_spec=pl.BlockSpec((M, N), lambda m, c, k: (0, 0)), out_shape=(M, N), acc_shape=(M, N))


def _loss_bwd(h, target, gain, *, tm, name):
    T, D = h.shape
    tm = min(tm, T)

    def body(h_ref, t_ref, gain_ref, dh_ref, df_ref, loss_ref, dgain_ref):
        @pl.when(pl.program_id(0) == 0)
        def _():
            loss_ref[...] = jnp.zeros_like(loss_ref)
            dgain_ref[...] = jnp.zeros_like(dgain_ref)

        xf = h_ref[...]
        gain = gain_ref[...]
        err = xf * _rstd(xf) * gain - t_ref[...]
        loss_ref[...] += 0.5 * jnp.sum(jnp.mean(err * err, axis=-1, keepdims=True), axis=0, keepdims=True)
        dx, dgain = _rms_bwd(xf, gain, err * (1.0 / D))
        dh_ref[...] = dx
        df_ref[...] = (0.5 * dx).astype(BF16)
        dgain_ref[...] += dgain

    row = lambda i: (i, 0)
    fixed = lambda i: (0, 0)
    return pl.pallas_call(
        body, name=name, grid=(T // tm,),
        in_specs=[pl.BlockSpec((tm, D), row), pl.BlockSpec((tm, D), row), pl.BlockSpec((1, D), fixed)],
        out_specs=[pl.BlockSpec((tm, D), row), pl.BlockSpec((tm, D), row), pl.BlockSpec((1, 128), fixed),
                   pl.BlockSpec((1, D), fixed)],
        out_shape=[jax.ShapeDtypeStruct((T, D), F32), jax.ShapeDtypeStruct((T, D), BF16),
                   jax.ShapeDtypeStruct((1, 128), F32), jax.ShapeDtypeStruct((1, D), F32)],
        compiler_params=_params(("arbitrary",)),
    )(h, target, gain)


def _inproj_fwd(h, gain, w_in_t, *, tm, name):
    T, D = h.shape
    tm = min(tm, T)
    bn = D
    nb = w_in_t.shape[0] // bn

    def body(h_ref, gain_ref, wt_ref, un_ref, proj_ref):
        @pl.when(pl.program_id(1) == 0)
        def _():
            xf = h_ref[...]
            un_ref[...] = (xf * _rstd(xf) * gain_ref[...]).astype(BF16)

        proj_ref[...] = _mm_nt(un_ref[...], wt_ref[...])

    return pl.pallas_call(
        body, name=name, grid=(T // tm, nb),
        in_specs=[
            pl.BlockSpec((tm, D), lambda i, j: (i, 0)),
            pl.BlockSpec((1, D), lambda i, j: (0, 0)),
            pl.BlockSpec((bn, D), lambda i, j: (j, 0)),
        ],
        out_specs=[pl.BlockSpec((tm, D), lambda i, j: (i, 0)), pl.BlockSpec((tm, bn), lambda i, j: (i, j))],
        out_shape=[jax.ShapeDtypeStruct((T, D), BF16), jax.ShapeDtypeStruct((T, nb * bn), F32)],
        compiler_params=_params(("arbitrary", "arbitrary")),
    )(h, gain, w_in_t)


def _inproj_bwd(dparts, dh, h, gain, w_in_t, *, tm, name):
    T, D = h.shape
    tm = min(tm, T)
    n = len(dparts)
    widths = [p.shape[1] for p in dparts]
    starts = [sum(widths[:i]) for i in range(n)]

    def body(*refs):
        dp_refs = refs[:n]
        dh_ref, h_ref, gain_ref, wt_ref, dx_ref, df_ref, dgain_ref = refs[n:]

        @pl.when(pl.program_id(0) == 0)
        def _():
            dgain_ref[...] = jnp.zeros_like(dgain_ref)

        dn = sum(_mm(dp_ref[...], wt_ref[start:start + width, :])
                 for dp_ref, start, width in zip(dp_refs, starts, widths))
        dx, dgain = _rms_bwd(h_ref[...], gain_ref[...], dn)
        dh_in = dh_ref[...] + dx
        dx_ref[...] = dh_in
        df_ref[...] = (0.5 * dh_in).astype(BF16)
        dgain_ref[...] += dgain

    row = lambda i: (i, 0)
    fixed = lambda i: (0, 0)
    return pl.pallas_call(
        body, name=name, grid=(T // tm,),
        in_specs=[pl.BlockSpec((tm, width), row) for width in widths] + [
            pl.BlockSpec((tm, D), row),
            pl.BlockSpec((tm, D), row),
            pl.BlockSpec((1, D), fixed),
            pl.BlockSpec(w_in_t.shape, fixed),
        ],
        out_specs=[pl.BlockSpec((tm, D), row), pl.BlockSpec((tm, D), row), pl.BlockSpec((1, D), fixed)],
        out_shape=[jax.ShapeDtypeStruct((T, D), F32), jax.ShapeDtypeStruct((T, D), BF16),
                   jax.ShapeDtypeStruct((1, D), F32)],
        compiler_params=_params(("arbitrary",)),
    )(*dparts, dh, h, gain, w_in_t)


def _window_sum(x, row, doublings, *, backward):
    T = x.shape[0]
    s = x
    for k in range(doublings):
        sh = 1 << k
        if backward:
            s = s + jnp.where(row < T - sh, pltpu.roll(s, T - sh, 0), 0.0)
        else:
            s = s + jnp.where(row >= sh, pltpu.roll(s, sh, 0), 0.0)
    return s


def _pool_fwd(proj, w_group, scale, *, name):
    T = proj.shape[0]

    def body(xp_ref, w_ref, scale_ref, p_ref):
        row = lax.broadcasted_iota(jnp.int32, (T, POOL_GROUP), 0)
        for gi, window in enumerate(POOL_WINDOWS):
            cols = slice(gi * POOL_GROUP, (gi + 1) * POOL_GROUP)
            x = xp_ref[:, cols]
            inv_count = 1.0 / jnp.minimum(row + 1, window).astype(F32)
            yc = _window_sum(x, row, gi + 1, backward=False) * inv_count - x
            pre = _mm(yc.astype(BF16), w_ref[gi].astype(BF16))
            p_ref[:, cols] = pre * scale_ref[:, cols]

    return pl.pallas_call(
        body, name=name, grid=(1,),
        in_specs=[
            pl.BlockSpec((T, POOL_WIDTH), lambda i: (0, 0)),
            pl.BlockSpec(w_group.shape, lambda i: (0, 0, 0)),
            pl.BlockSpec((1, POOL_WIDTH), lambda i: (0, 0)),
        ],
        out_specs=pl.BlockSpec((T, POOL_WIDTH), lambda i: (0, 0)),
        out_shape=jax.ShapeDtypeStruct((T, POOL_WIDTH), F32),
        compiler_params=_params(("arbitrary",)),
    )(proj, w_group, scale)


def _pool_bwd(dp, proj, w_group, scale, *, name):
    T = proj.shape[0]

    def body(dp_ref, xp_ref, w_ref, scale_ref, dxp_ref, dw_ref, dscale_ref):
        row = lax.broadcasted_iota(jnp.int32, (T, POOL_GROUP), 0)
        for gi, window in enumerate(POOL_WINDOWS):
            cols = slice(gi * POOL_GROUP, (gi + 1) * POOL_GROUP)
            x = xp_ref[:, cols]
            inv_count = 1.0 / jnp.minimum(row + 1, window).astype(F32)
            yc = (_window_sum(x, row, gi + 1, backward=False) * inv_count - x).astype(BF16)
            w = w_ref[gi].astype(BF16)
            pre = _mm(yc, w)
            dpg = dp_ref[:, cols]
            dscale_ref[:, cols] = jnp.sum(dpg * pre, axis=0, keepdims=True)
            dpre = (dpg * scale_ref[:, cols]).astype(BF16)
            dw_ref[gi] = _mm_tn(yc, dpre)
            dyc = _mm_nt(dpre, w)
            dxp_ref[:, cols] = (_window_sum(dyc * inv_count, row, gi + 1, backward=True) - dyc).astype(BF16)

    return pl.pallas_call(
        body, name=name, grid=(1,),
        in_specs=[
            pl.BlockSpec((T, POOL_WIDTH), lambda i: (0, 0)),
            pl.BlockSpec((T, POOL_WIDTH), lambda i: (0, 0)),
            pl.BlockSpec(w_group.shape, lambda i: (0, 0, 0)),
            pl.BlockSpec((1, POOL_WIDTH), lambda i: (0, 0)),
        ],
        out_specs=[
            pl.BlockSpec((T, POOL_WIDTH), lambda i: (0, 0)),
            pl.BlockSpec(w_group.shape, lambda i: (0, 0, 0)),
            pl.BlockSpec((1, POOL_WIDTH), lambda i: (0, 0)),
        ],
        out_shape=[jax.ShapeDtypeStruct((T, POOL_WIDTH), BF16), jax.ShapeDtypeStruct(w_group.shape, F32),
                   jax.ShapeDtypeStruct((1, POOL_WIDTH), F32)],
        compiler_params=_params(("arbitrary",)),
    )(dp, proj, w_group, scale)


ATTN_STRIP = 32


def _log_sigmoids(z):
    lb = jnp.minimum(z, 0.0) - jnp.log(1.0 + jnp.exp(-jnp.abs(z)))
    return lb, lb - z


def _transposed_blocks(x_ref, blocks_scr, tq):
    for b in range(blocks_scr.shape[0]):
        blocks_scr[b] = x_ref[b * tq:(b + 1) * tq, :].T.astype(BF16)


def _split_bf16(x):
    hi = x.astype(BF16)
    return hi, (x - hi.astype(F32)).astype(BF16)


def _strips(n):
    return [slice(i, i + ATTN_STRIP) for i in range(0, n, ATTN_STRIP)]


def _rows(parts):
    return jnp.concatenate(parts, axis=0)


def _attn_specs(T, tq):
    q_col = POOL_WIDTH // HEAD_PAIR
    k_col = q_col + SB_WIDTH // HEAD_PAIR
    v_col = k_col + SB_WIDTH // HEAD_PAIR
    return [
        pl.BlockSpec((tq, HEAD_PAIR), lambda p, i: (i, q_col + p)),
        pl.BlockSpec((T, HEAD_PAIR), lambda p, i: (0, k_col + p)),
        pl.BlockSpec((T, HEAD_PAIR), lambda p, i: (0, v_col + p)),
    ]


def _attn_fwd(proj, *, name):
    T = proj.shape[0]
    tk = min(ATTN_K_BLOCK, T)
    tq = min(ATTN_Q_BLOCK_FWD, T)
    diagonal_blocks = tq // tk

    def body(q_ref, k_ref, v_ref, o_ref, lt_ref, kt_scr, vb_scr):
        qi = pl.program_id(1)

        @pl.when(qi == 0)
        def _():
            _transposed_blocks(k_ref, kt_scr, tk)
            vb_scr[...] = v_ref[...].astype(BF16)

        head0 = lax.broadcasted_iota(jnp.int32, (tq, HEAD_PAIR), 1) < HEAD_DIM
        q = q_ref[...] * ATTN_SCALE
        qs = (jnp.where(head0, q, 0.0).astype(BF16), jnp.where(head0, 0.0, q).astype(BF16))
        r = lax.broadcasted_iota(jnp.int32, (tq, tk), 0)
        c = lax.broadcasted_iota(jnp.int32, (tq, tk), 1)
        later = (r[:tk] > c[:tk]).astype(BF16)
        later2 = _rows([later, later])
        causal = lambda d: (lambda rows: c[rows] + d * tk < r[rows])
        strips = _strips(tq)

        def log_terms(z, valid):
            lbs, his, los, sums = [], [], [], []
            for rows in strips:
                lb, lm = _log_sigmoids(z[rows])
                if valid is not None:
                    lm = jnp.where(valid(rows), lm, 0.0)
                hi, lo = _split_bf16(lm)
                lbs.append(lb)
                his.append(hi)
                los.append(lo)
                sums.append(jnp.sum(lm, axis=1, keepdims=True))
            return lbs, jnp.concatenate([_rows(his), _rows(los)], axis=1), _rows(sums)

        def weights(lbs, run, after, valid):
            parts = []
            for rows, lb in zip(strips, lbs):
                a = jnp.exp(lb + run[rows] + after[rows])
                if valid is not None:
                    a = jnp.where(valid(rows), a, 0.0)
                parts.append(a.astype(BF16))
            return _rows(parts)

        def block(kj, carry, valid):
            kt = kt_scr[kj]
            vb = vb_scr[pl.ds(pl.multiple_of(kj * tk, tk), tk), :]
            run0, o0, run1, o1 = carry
            z0 = _mm(qs[0], kt)
            z1 = _mm(qs[1], kt)
            lbs0, split0, sums0 = log_terms(z0, valid)
            after0 = _mm(split0, later2)
            lbs1, split1, sums1 = log_terms(z1, valid)
            after1 = _mm(split1, later2)
            o0 = o0 + _mm(weights(lbs0, run0, after0, valid), vb)
            o1 = o1 + _mm(weights(lbs1, run1, after1, valid), vb)
            return run0 + sums0, o0, run1 + sums1, o1

        zero = (jnp.zeros((tq, 1), F32), jnp.zeros((tq, HEAD_PAIR), F32))
        first = diagonal_blocks * qi
        carry = zero + zero
        for d in reversed(range(diagonal_blocks)):
            carry = block(first + d, carry, causal(d))
        carry = lax.fori_loop(0, first, lambda it, cr: block(first - 1 - it, cr, None), carry)
        o_ref[...] = jnp.where(head0, carry[1], carry[3])
        lt_ref[...] = jnp.where(head0, carry[0], carry[2])

    out_spec = pl.BlockSpec((tq, HEAD_PAIR), lambda p, i: (i, p))
    return pl.pallas_call(
        body, name=name, grid=(N_HEADS // 2, T // tq),
        in_specs=_attn_specs(T, tq), out_specs=[out_spec, out_spec],
        out_shape=[jax.ShapeDtypeStruct((T, SB_WIDTH), F32), jax.ShapeDtypeStruct((T, SB_WIDTH), F32)],
        scratch_shapes=[pltpu.VMEM((T // tk, HEAD_PAIR, tk), BF16), pltpu.VMEM((T, HEAD_PAIR), BF16)],
        compiler_params=_params(("arbitrary", "arbitrary")),
    )(proj, proj, proj)


def _attn_bwd(proj, do, ltot, after, *, name):
    T = proj.shape[0]
    tk = min(ATTN_K_BLOCK, T)
    tq = min(ATTN_Q_BLOCK_BWD, T)
    diagonal_blocks = tq // tk

    def body(q_ref, k_ref, v_ref, do_ref, lt_ref, after_ref, dq_ref, dk_ref, dv_ref,
             kb_scr, kt_scr, vt_scr, dkt_ref, dvt_ref):
        qi = pl.program_id(1)

        @pl.when(qi == 0)
        def _():
            kb_scr[...] = k_ref[...].astype(BF16)
            _transposed_blocks(k_ref, kt_scr, tk)
            _transposed_blocks(v_ref, vt_scr, tk)
            dkt_ref[...] = jnp.zeros_like(dkt_ref)
            dvt_ref[...] = jnp.zeros_like(dvt_ref)

        head0 = lax.broadcasted_iota(jnp.int32, (tq, HEAD_PAIR), 1) < HEAD_DIM
        q, do_, lt = q_ref[...] * ATTN_SCALE, do_ref[...], lt_ref[...]
        qs = (jnp.where(head0, q, 0.0).astype(BF16), jnp.where(head0, 0.0, q).astype(BF16))
        q_heads = (jnp.where(head0, q, 0.0), jnp.where(head0, 0.0, q))
        do_heads = (jnp.where(head0, do_, 0.0), jnp.where(head0, 0.0, do_))
        dos = tuple(d.astype(BF16) for d in do_heads)
        qts = tuple(x.T.astype(BF16) for x in q_heads)
        dots = tuple(d.T.astype(BF16) for d in do_heads)
        lts = (jnp.max(jnp.where(head0, lt, -jnp.inf), axis=1, keepdims=True),
               jnp.max(jnp.where(head0, -jnp.inf, lt), axis=1, keepdims=True))
        r = lax.broadcasted_iota(jnp.int32, (tq, tk), 0)
        c = lax.broadcasted_iota(jnp.int32, (tq, tk), 1)
        upto = (r[:tk] <= c[:tk]).astype(BF16)
        before = (r[:tk] < c[:tk]).astype(BF16)
        upto2, before2 = _rows([upto, upto]), _rows([before, before])
        causal = lambda d: (lambda rows: c[rows] + d * tk < r[rows])
        strips = _strips(tq)

        def log_terms(z, valid):
            lbs, his, los, sums = [], [], [], []
            for rows in strips:
                lb, lm = _log_sigmoids(z[rows])
                if valid is not None:
                    lm = jnp.where(valid(rows), lm, 0.0)
                hi, lo = _split_bf16(lm)
                lbs.append(lb)
                his.append(hi)
                los.append(lo)
                sums.append(jnp.sum(lm, axis=1, keepdims=True))
            return lbs, jnp.concatenate([_rows(his), _rows(los)], axis=1), _rows(sums)

        def weights(lbs, rest, lm_upto, da, valid):
            a_parts, es, his, los, sums = [], [], [], [], []
            for rows, lb in zip(strips, lbs):
                a = jnp.exp(lb + (rest[rows] - lm_upto[rows]))
                if valid is not None:
                    a = jnp.where(valid(rows), a, 0.0)
                e = da[rows] * a
                hi, lo = _split_bf16(e)
                a_parts.append(a.astype(BF16))
                es.append(e)
                his.append(hi)
                los.append(lo)
                sums.append(jnp.sum(e, axis=1, keepdims=True))
            return _rows(a_parts), es, jnp.concatenate([_rows(his), _rows(los)], axis=1), _rows(sums)

        def score_grads(lbs, es, run_e, e_before, valid):
            parts = []
            for rows, lb, e in zip(strips, lbs, es):
                beta = jnp.exp(lb)
                dz = e * (1.0 - beta) - (run_e[rows] + e_before[rows]) * beta
                if valid is not None:
                    dz = jnp.where(valid(rows), dz, 0.0)
                parts.append(dz.astype(BF16))
            return _rows(parts)

        def block(kj, carry, valid):
            off = pl.multiple_of(kj * tk, tk)
            kb, kt, vt = kb_scr[pl.ds(off, tk), :], kt_scr[kj], vt_scr[kj]
            run_lm0, run_e0, dq0, run_lm1, run_e1, dq1 = carry
            z0, da0 = _mm(qs[0], kt), _mm(dos[0], vt)
            z1, da1 = _mm(qs[1], kt), _mm(dos[1], vt)
            lbs0, split0, lm_sums0 = log_terms(z0, valid)
            lm_upto0 = _mm(split0, upto2)
            lbs1, split1, lm_sums1 = log_terms(z1, valid)
            lm_upto1 = _mm(split1, upto2)
            a0, es0, split0, e_sums0 = weights(lbs0, lts[0] - run_lm0, lm_upto0, da0, valid)
            e_before0 = _mm(split0, before2)
            a1, es1, split1, e_sums1 = weights(lbs1, lts[1] - run_lm1, lm_upto1, da1, valid)
            e_before1 = _mm(split1, before2)
            dz0 = score_grads(lbs0, es0, run_e0, e_before0, valid)
            dkt_blk = _mm(qts[0], dz0)
            dvt_blk = _mm(dots[0], a0)
            dq0 = dq0 + _mm(dz0, kb)
            dz1 = score_grads(lbs1, es1, run_e1, e_before1, valid)
            dkt_ref[kj] += dkt_blk + _mm(qts[1], dz1)
            dvt_ref[kj] += dvt_blk + _mm(dots[1], a1)
            dq1 = dq1 + _mm(dz1, kb)
            return run_lm0 + lm_sums0, run_e0 + e_sums0, dq0, run_lm1 + lm_sums1, run_e1 + e_sums1, dq1

        zero = (jnp.zeros((tq, 1), F32), jnp.zeros((tq, 1), F32), jnp.zeros((tq, HEAD_PAIR), F32))
        first = diagonal_blocks * qi
        carry = lax.fori_loop(0, first, lambda kj, cr: block(kj, cr, None), zero + zero)
        for d in range(diagonal_blocks):
            carry = block(first + d, carry, causal(d))
        dq_ref[...] = (jnp.where(head0, carry[2], carry[5]) * ATTN_SCALE).astype(BF16)

        @pl.when(qi == T // tq - 1)
        def _():
            for b in range(T // tk):
                dk_ref[b * tk:(b + 1) * tk, :] = dkt_ref[b].T.astype(BF16)
                dv_ref[b * tk:(b + 1) * tk, :] = dvt_ref[b].T.astype(BF16)

    blk = pl.BlockSpec((tq, HEAD_PAIR), lambda p, i: (i, p))
    seq = pl.BlockSpec((T, HEAD_PAIR), lambda p, i: (0, p))
    transposed = pltpu.VMEM((T // tk, HEAD_PAIR, tk), F32)
    return pl.pallas_call(
        body, name=name, grid=(N_HEADS // 2, T // tq),
        in_specs=_attn_specs(T, tq) + [blk, blk, AFTER], out_specs=[blk, seq, seq],
        out_shape=[jax.ShapeDtypeStruct((T, SB_WIDTH), BF16)] * 3,
        scratch_shapes=[pltpu.VMEM((T, HEAD_PAIR), BF16), pltpu.VMEM((T // tk, HEAD_PAIR, tk), BF16),
                        pltpu.VMEM((T // tk, HEAD_PAIR, tk), BF16), transposed, transposed],
        compiler_params=_params(("arbitrary", "arbitrary")),
    )(proj, proj, proj, do, ltot, _in_hbm(after))


def _mix_specs(T, D, tm, wbp, w_out):
    gate_col = (POOL_WIDTH + 3 * SB_WIDTH) // D
    row = lambda i: (i, 0)
    return [
        pl.BlockSpec((tm, D), row),
        pl.BlockSpec((tm, POOL_WIDTH), row),
        pl.BlockSpec((tm, SB_WIDTH), row),
        pl.BlockSpec((tm, D), lambda i: (i, gate_col)),
        pl.BlockSpec((tm, D), lambda i: (i, gate_col + 1)),
        pl.BlockSpec(wbp.shape, lambda i: (0, 0)),
        pl.BlockSpec(wbp.shape, lambda i: (0, 0)),
        pl.BlockSpec(w_out.shape, lambda i: (0, 0)),
    ]


def _mix_fwd(h, p, o, proj, wbp, wba, w_out, *, tm, name):
    T, D = h.shape
    tm = min(tm, T)

    def body(h_ref, p_ref, o_ref, glp_ref, gls_ref, wbp_ref, wba_ref, wout_ref, hout_ref, m_ref):
        halves = (pl.ds(0, tm // 2), pl.ds(tm // 2, tm // 2))
        wbp, wba, wout = wbp_ref[...], wba_ref[...], wout_ref[...]
        branches = [(_mm_nt(p_ref[rows, :].astype(BF16), wbp), _mm_nt(o_ref[rows, :].astype(BF16), wba))
                    for rows in halves]
        for rows, (yp, ys) in zip(halves, branches):
            m = (jax.nn.sigmoid(glp_ref[rows, :]) * yp + jax.nn.sigmoid(gls_ref[rows, :]) * ys).astype(BF16)
            m_ref[rows, :] = m
            hout_ref[rows, :] = h_ref[rows, :] + _mm(m, wout)

    row = lambda i: (i, 0)
    return pl.pallas_call(
        body, name=name, grid=(T // tm,),
        in_specs=_mix_specs(T, D, tm, wbp, w_out),
        out_specs=[pl.BlockSpec((tm, D), row), pl.BlockSpec((tm, D), row)],
        out_shape=[jax.ShapeDtypeStruct((T, D), F32), jax.ShapeDtypeStruct((T, D), BF16)],
        compiler_params=_params(("arbitrary",)),
    )(h, p, o, proj, proj, wbp, wba, w_out)


def _mix_bwd(dh, p, o, proj, wbp, wba, w_out, after, *, tm, name):
    T, D = dh.shape
    tm = min(tm, T)

    def body(dh_ref, p_ref, o_ref, glp_ref, gls_ref, wbp_ref, wba_ref, wout_ref, after_ref,
             dyp_ref, dys_ref, dp_ref, do_ref, dgl_ref):
        halves = (pl.ds(0, tm // 2), pl.ds(tm // 2, tm // 2))
        wbp, wba, wout = wbp_ref[...], wba_ref[...], wout_ref[...]
        products = [(_mm_nt(dh_ref[rows, :].astype(BF16), wout), _mm_nt(p_ref[rows, :].astype(BF16), wbp),
                     _mm_nt(o_ref[rows, :].astype(BF16), wba)) for rows in halves]
        for rows, (dm, yp, ys) in zip(halves, products):
            gp = jax.nn.sigmoid(glp_ref[rows, :])
            gs = jax.nn.sigmoid(gls_ref[rows, :])
            dyp = (dm * gp).astype(BF16)
            dys = (dm * gs).astype(BF16)
            dyp_ref[rows, :] = dyp
            dys_ref[rows, :] = dys
            dgl_ref[rows, :D] = (dm * yp * gp * (1.0 - gp)).astype(BF16)
            dgl_ref[rows, D:] = (dm * ys * gs * (1.0 - gs)).astype(BF16)
            dp_ref[rows, :] = _mm(dyp, wbp)
            do_ref[rows, :] = _mm(dys, wba)

    row = lambda i: (i, 0)
    return pl.pallas_call(
        body, name=name, grid=(T // tm,),
        in_specs=_mix_specs(T, D, tm, wbp, w_out) + [AFTER],
        out_specs=[pl.BlockSpec((tm, D), row), pl.BlockSpec((tm, D), row), pl.BlockSpec((tm, POOL_WIDTH), row),
                   pl.BlockSpec((tm, SB_WIDTH), row), pl.BlockSpec((tm, 2 * D), row)],
        out_shape=[jax.ShapeDtypeStruct((T, D), BF16), jax.ShapeDtypeStruct((T, D), BF16),
                   jax.ShapeDtypeStruct((T, POOL_WIDTH), F32), jax.ShapeDtypeStruct((T, SB_WIDTH), F32),
                   jax.ShapeDtypeStruct((T, 2 * D), BF16)],
        compiler_params=_params(("arbitrary",)),
    )(dh, p, o, proj, proj, wbp, wba, w_out, _in_hbm(after))


def _adamw_update(w, g, m, v):
    m_ = ADAM_B1 * m + (1.0 - ADAM_B1) * g
    v_ = ADAM_B2 * v + (1.0 - ADAM_B2) * (g * g)
    m_hat = m_ / (1.0 - ADAM_B1 ** ADAM_STEP)
    v_hat = v_ / (1.0 - ADAM_B2 ** ADAM_STEP)
    return -ADAM_LR * (m_hat / (jnp.sqrt(v_hat) + ADAM_EPS) + ADAM_WD * w), m_, v_


def _adamw(w, g, m, v, *, name):
    R, C = w.shape
    tr = _row_tile(R, C)

    def body(w_ref, g_ref, m_ref, v_ref, d_ref, nm_ref, nv_ref):
        d_ref[...], nm_ref[...], nv_ref[...] = _adamw_update(w_ref[...], g_ref[...], m_ref[...], v_ref[...])

    spec = pl.BlockSpec((tr, C), lambda i: (i, 0))
    return pl.pallas_call(
        body, name=name, grid=(R // tr,), in_specs=[spec] * 4, out_specs=[spec] * 3,
        out_shape=[jax.ShapeDtypeStruct((R, C), F32)] * 3,
        compiler_params=_params(("arbitrary",)),
    )(w, g, m, v)


def _position():
    return lax.axis_index("x"), lax.axis_index("y"), lax.axis_index("c")


def _all_gather(shards, *, name, collective_id):
    n = len(shards)
    n_copies = 9

    def body(*refs):
        ins, outs = refs[:n], refs[n:2 * n]
        send_sems, recv_sems, local_sems = refs[2 * n:]
        x, y, c = _position()
        me, sibling = (x, y, c), (x, y, 1 - c)
        x_nbr, y_nbr, diagonal = (1 - x, y, c), (x, 1 - y, c), (1 - x, 1 - y, c)
        other = lambda pos: (pos[0], pos[1], 1 - c)

        barrier = pltpu.get_barrier_semaphore()
        for peer in (sibling, x_nbr, y_nbr):
            pl.semaphore_signal(barrier, inc=1, device_id=peer, device_id_type=MESH)
        pl.semaphore_wait(barrier, 3)

        def block(a, pos, half=None):
            ref = outs[a].at[4 * pos[0] + 2 * pos[1] + pos[2]]
            rows = ref.shape[0] // 2
            return ref if half is None else ref.at[pl.ds(half * rows, rows)]

        def copy(a, k, pos, to, half=None, src=None):
            return pltpu.make_async_remote_copy(
                src_ref=block(a, pos, half) if src is None else src, dst_ref=block(a, pos, half),
                send_sem=send_sems.at[n_copies * a + k], recv_sem=recv_sems.at[n_copies * a + k],
                device_id=to, device_id_type=MESH)

        started = []
        for a in range(n):
            mine = pltpu.make_async_copy(ins[a], block(a, me), local_sems.at[a])
            mine.start()
            started.append(mine)
        sends = []
        for a in range(n):
            sends += [copy(a, 1, me, x_nbr, src=ins[a]), copy(a, 2, me, y_nbr, src=ins[a]),
                      copy(a, 0, me, sibling, src=ins[a])]
        for cp in sends:
            cp.start()

        def pass_on(copies):
            for cp in copies:
                cp.start()
                sends.append(cp)

        for a in range(n):
            copy(a, 1, x_nbr, me).wait_recv()
            pass_on([copy(a, 5, x_nbr, y_nbr, half=0), copy(a, 3, x_nbr, sibling)])
            copy(a, 2, y_nbr, me).wait_recv()
            pass_on([copy(a, 6, y_nbr, x_nbr, half=1), copy(a, 4, y_nbr, sibling)])
        for a in range(n):
            copy(a, 5, diagonal, me, half=0).wait_recv()
            pass_on([copy(a, 7, diagonal, sibling, half=0)])
            copy(a, 6, diagonal, me, half=1).wait_recv()
            pass_on([copy(a, 8, diagonal, sibling, half=1)])
        for a in range(n):
            copy(a, 0, sibling, me).wait_recv()
            copy(a, 3, other(x_nbr), me).wait_recv()
            copy(a, 4, other(y_nbr), me).wait_recv()
            copy(a, 7, other(diagonal), me, half=0).wait_recv()
            copy(a, 8, other(diagonal), me, half=1).wait_recv()
        for cp in sends:
            cp.wait_send()
        for cp in started:
            cp.wait()

    return pl.kernel(
        body, name=name,
        out_type=[jax.ShapeDtypeStruct((N_DEV,) + s.shape, s.dtype) for s in shards],
        mesh=plsc.ScalarSubcoreMesh(axis_name="sequencer", num_cores=1),
        scratch_types=[pltpu.SemaphoreType.DMA((n_copies * n,)), pltpu.SemaphoreType.DMA((n_copies * n,)),
                       pltpu.SemaphoreType.DMA((n,))],
        compiler_params=pltpu.CompilerParams(collective_id=collective_id),
    )(*shards)


def _chip_sums(group, *, name):
    n = len(group)
    shapes = [g.shape[1:] for g in group]

    def body(*refs):
        g_refs, partials_out, own_out = refs[:n], refs[n:3 * n:2], refs[n + 1:3 * n:2]
        mines, theirs = refs[3 * n:7 * n:4], refs[3 * n + 1:7 * n:4]
        partials, out_refs = refs[3 * n + 2:7 * n:4], refs[3 * n + 3:7 * n:4]
        send_sems, recv_sems, local_sems, store_sems = refs[7 * n:]
        x, y, c = _position()
        my_chip = 2 * x + y

        def swap(a, s):
            return pltpu.make_async_remote_copy(
                src_ref=g_refs[a].at[2 * s + (1 - c)], dst_ref=theirs[a].at[s],
                send_sem=send_sems.at[4 * a + s], recv_sem=recv_sems.at[4 * a + s],
                device_id=(x, y, 1 - c), device_id_type=MESH)

        def load(a, s):
            return pltpu.make_async_copy(g_refs[a].at[2 * s + c], mines[a].at[s], local_sems.at[4 * a + s])

        def store(a, s, own):
            if own:
                return pltpu.make_async_copy(out_refs[a], own_out[a], store_sems.at[4 * a + s])
            j = (s ^ my_chip) - 1
            return pltpu.make_async_copy(partials[a].at[j], partials_out[a].at[j], store_sems.at[4 * a + s])

        for a in range(n):
            for s in range(4):
                swap(a, s).start()
                load(a, s).start(priority=1)

        for a, (R, C) in enumerate(shapes):
            rc = 128 if R % 128 == 0 else R

            def chip_sum(chip, rows):
                return mines[a][chip, rows, :].astype(F32) + theirs[a][chip, rows, :].astype(F32)

            for s in range(4):
                load(a, s).wait()
                swap(a, s).wait_recv()

                @pl.when(s == my_chip)
                def _():
                    @pl.loop(0, R // rc)
                    def _(t):
                        rows = pl.ds(pl.multiple_of(t * rc, rc), rc)
                        out_refs[a][rows, :] = chip_sum(s, rows)
                    store(a, s, True).start()

                @pl.when(s != my_chip)
                def _():
                    @pl.loop(0, R // rc)
                    def _(t):
                        rows = pl.ds(pl.multiple_of(t * rc, rc), rc)
                        partials[a][(s ^ my_chip) - 1, rows, :] = chip_sum(s, rows).astype(BF16)
                    store(a, s, False).start()

        for a in range(n):
            for s in range(4):
                swap(a, s).wait_send()
                pl.when(s == my_chip)(store(a, s, True).wait)
                pl.when(s != my_chip)(store(a, s, False).wait)

    hbm = pl.BlockSpec(memory_space=pl.ANY)
    outs = pl.pallas_call(
        body, name=name,
        in_specs=[hbm] * n, out_specs=[hbm] * (2 * n),
        out_shape=[shape for R, C in shapes
                   for shape in (jax.ShapeDtypeStruct((3, R, C), BF16), jax.ShapeDtypeStruct((R, C), F32))],
        scratch_shapes=[scratch for R, C in shapes for scratch in (
            pltpu.VMEM((4, R, C), BF16), pltpu.VMEM((4, R, C), BF16), pltpu.VMEM((3, R, C), BF16),
            pltpu.VMEM((R, C), F32))] + [pltpu.SemaphoreType.DMA((4 * n,))] * 4,
        compiler_params=_params(),
    )(*group)
    return [(outs[2 * a], outs[2 * a + 1]) for a in range(n)]


def _cross_chips(partials, *, name, collective_id):
    n = len(partials)

    def body(*refs):
        ins, outs = refs[:n], refs[n:2 * n]
        send_sems, recv_sems = refs[2 * n:]
        x, y, c = _position()
        my_chip = 2 * x + y
        peers = [((my_chip ^ j) // 2, (my_chip ^ j) % 2, c) for j in (1, 2, 3)]

        barrier = pltpu.get_barrier_semaphore()
        for peer in peers:
            pl.semaphore_signal(barrier, inc=1, device_id=peer, device_id_type=MESH)
        pl.semaphore_wait(barrier, 3)

        copies = [
            pltpu.make_async_remote_copy(
                src_ref=ins[a].at[j], dst_ref=outs[a].at[j],
                send_sem=send_sems.at[3 * a + j], recv_sem=recv_sems.at[3 * a + j],
                device_id=peers[j], device_id_type=MESH)
            for a in range(n) for j in range(3)]
        for cp in copies:
            cp.start()
        for cp in copies:
            cp.wait_recv()
        for cp in copies:
            cp.wait_send()

    return pl.kernel(
        body, name=name,
        out_type=[jax.ShapeDtypeStruct(p.shape, p.dtype) for p in partials],
        mesh=plsc.ScalarSubcoreMesh(axis_name="sequencer", num_cores=1),
        scratch_types=[pltpu.SemaphoreType.DMA((3 * n,)), pltpu.SemaphoreType.DMA((3 * n,))],
        compiler_params=pltpu.CompilerParams(collective_id=collective_id),
    )(*partials)


def _cross_chips_and_gather(partials, slab, *, name, collective_id):
    n = len(partials)

    def body(*refs):
        part_refs, slab_ref = refs[:n], refs[n]
        landed_refs, slabs_ref = refs[n + 1:2 * n + 1], refs[2 * n + 1]
        send_sems, recv_sems, local_sem = refs[2 * n + 2:]
        x, y, c = _position()
        me, my_chip = 4 * x + 2 * y + c, 2 * x + y
        others = [me ^ k for k in range(1, N_DEV)]
        ids = [(o // 4, (o // 2) % 2, o % 2) for o in others]

        barrier = pltpu.get_barrier_semaphore()
        for peer in ids:
            pl.semaphore_signal(barrier, inc=1, device_id=peer, device_id_type=MESH)
        pl.semaphore_wait(barrier, N_DEV - 1)

        mine = pltpu.make_async_copy(slab_ref, slabs_ref.at[me], local_sem)
        mine.start()
        sends = [
            pltpu.make_async_remote_copy(
                src_ref=part_refs[a].at[j], dst_ref=landed_refs[a].at[j],
                send_sem=send_sems.at[3 * a + j], recv_sem=recv_sems.at[3 * a + j],
                device_id=((my_chip ^ (j + 1)) // 2, (my_chip ^ (j + 1)) % 2, c), device_id_type=MESH)
            for a in range(n) for j in range(3)]
        sends += [
            pltpu.make_async_remote_copy(
                src_ref=slab_ref, dst_ref=slabs_ref.at[me],
                send_sem=send_sems.at[3 * n + k], recv_sem=recv_sems.at[3 * n + k],
                device_id=ids[k], device_id_type=MESH)
            for k in range(N_DEV - 1)]
        arrivals = sends[:3 * n] + [
            pltpu.make_async_remote_copy(
                src_ref=slab_ref, dst_ref=slabs_ref.at[others[k]],
                send_sem=send_sems.at[3 * n + k], recv_sem=recv_sems.at[3 * n + k],
                device_id=ids[k], device_id_type=MESH)
            for k in range(N_DEV - 1)]
        for cp in sends:
            cp.start()
        for cp in arrivals:
            cp.wait_recv()
        for cp in sends:
            cp.wait_send()
        mine.wait()

    n_sems = 3 * n + N_DEV - 1
    outs = pl.kernel(
        body, name=name,
        out_type=[jax.ShapeDtypeStruct(p.shape, p.dtype) for p in partials]
                 + [jax.ShapeDtypeStruct((N_DEV,) + slab.shape, slab.dtype)],
        mesh=plsc.ScalarSubcoreMesh(axis_name="sequencer", num_cores=1),
        scratch_types=[pltpu.SemaphoreType.DMA((n_sems,)), pltpu.SemaphoreType.DMA((n_sems,)), pltpu.SemaphoreType.DMA],
        compiler_params=pltpu.CompilerParams(collective_id=collective_id),
    )(*partials, slab)
    return outs[:n], outs[n]


def _sum_devices(gathered, after, *, name):
    _, R, C = gathered.shape

    def body(in_ref, after_ref, out_ref):
        total = in_ref[0]
        for d in range(1, N_DEV):
            total = total + in_ref[d]
        out_ref[...] = total

    return pl.pallas_call(
        body, name=name, grid=(1,),
        in_specs=[pl.BlockSpec((N_DEV, R, C), lambda i: (0, 0, 0)), AFTER],
        out_specs=pl.BlockSpec((R, C), lambda i: (0, 0)),
        out_shape=jax.ShapeDtypeStruct((R, C), F32),
        compiler_params=_params(("arbitrary",)),
    )(gathered, _in_hbm(after))


def _owner_sum_adamw(own, landed, w, m, v, after, *, transposed, name, group=None, into=()):
    H, R, C = w.shape
    tr = R // 2
    first_group = 0 if group is None else group

    def body(own_ref, landed_ref, w_ref, m_ref, v_ref, after_ref, *rest):
        g_ref, d_ref, nm_ref, nv_ref = rest[len(into):]
        total = own_ref[...]
        for j in range(3):
            total = total + landed_ref[j].astype(F32)
        if transposed:
            total = total.T
        g_ref[...] = total
        d_ref[...], nm_ref[...], nv_ref[...] = _adamw_update(w_ref[...], total, m_ref[...], v_ref[...])

    spec = pl.BlockSpec((None, tr, C), lambda h, i: (first_group + h, i, 0))
    if transposed:
        own_spec = pl.BlockSpec((None, C, tr), lambda h, i: (h, 0, i))
        landed_spec = pl.BlockSpec((3, None, C, tr), lambda h, i: (0, h, 0, i))
    else:
        own_spec = pl.BlockSpec((None, tr, C), lambda h, i: (h, i, 0))
        landed_spec = pl.BlockSpec((3, None, tr, C), lambda h, i: (0, h, i, 0))
    n_in = 6
    return pl.pallas_call(
        body, name=name, grid=(own.shape[0], R // tr),
        in_specs=[own_spec, landed_spec, spec, spec, spec, AFTER] + [pl.BlockSpec(memory_space=pl.ANY)] * len(into),
        out_specs=[spec] * 4,
        out_shape=[jax.ShapeDtypeStruct((H, R, C), F32)] * 4,
        input_output_aliases={n_in + j: j for j in range(len(into))},
        compiler_params=_params(("arbitrary", "arbitrary")),
    )(own, landed, w, m, v, _in_hbm(after), *into)


def _local_step(x, target, norms, pool_w_group, pool_scale, wgu1, wd1, w_in, wbp, wba, w_out, wgu2, wd2, exchange):
    n1g, nmg, n2g, nfg = norms
    D = x.shape[1]
    gu1, hid1 = _ffn_up(x, n1g, wgu1, tm=1024, name="ffn1_up")
    h1 = _ffn_down(x, hid1, wd1, tm=512, name="ffn1_down")
    un, proj = _inproj_fwd(h1, nmg, w_in, tm=1024, name="inproj_fwd")
    p = _pool_fwd(proj, pool_w_group, pool_scale, name="pool_fwd")
    o, ltot = _attn_fwd(proj, name="attn_fwd")
    h2, m = _mix_fwd(h1, p, o, proj, wbp, wba, w_out, tm=512, name="mix_fwd")
    gu2, hid2 = _ffn_up(h2, n2g, wgu2, tm=1024, name="ffn2_up")
    h3 = _ffn_down(h2, hid2, wd2, tm=512, name="ffn2_down")
    dh3, df2, loss, d_nf = _loss_bwd(h3, target, nfg, tm=512, name="loss_bwd")

    dh2, d_n2, n2, dgu2 = _ffn_bwd(dh3, df2, h2, n2g, gu2, wgu2, wd2, df2, tm=512, name="ffn2_bwd")
    d_wd2 = _wgrad_down(hid2, df2, tk=WGRAD_TOKENS, name="ffn2_wgrad_down")
    d_wgu2 = _wgrad_gate_up(n2, dgu2, tk=WGRAD_TOKENS, name="ffn2_wgrad_gate_up")
    (g_wd2, g_wgu2), token = exchange("ffn2", [d_wd2.reshape(N_DEV, FF_SHARD_PAD, D), d_wgu2])

    dyp, dys, dp, do, dgl = _mix_bwd(dh2, p, o, proj, wbp, wba, w_out, token, tm=512, name="mix_bwd")
    d_wout = _wgrad_full(m, dh2, tk=WGRAD_TOKENS, name="wgrad_out")
    d_wbp = _wgrad_full(dyp, p, tk=WGRAD_TOKENS, name="wgrad_branch_pool")
    d_wba = _wgrad_full(dys, o, tk=WGRAD_TOKENS, name="wgrad_branch_attn")
    by_owner = lambda g: g.reshape(N_DEV, g.shape[0] // N_DEV, g.shape[1])
    (g_wbp, g_wba, g_wout), token = exchange("mix", [by_owner(d_wbp), by_owner(d_wba), by_owner(d_wout)])
    dxp, d_wgroup, d_scale = _pool_bwd(dp, proj, pool_w_group, pool_scale, name="pool_bwd")
    dq, dk, dv = _attn_bwd(proj, do, ltot, token, name="attn_bwd")
    dproj_parts = [dxp, dq, dk, dv, dgl]
    dh1, df1, d_nm = _inproj_bwd(dproj_parts, dh2, h1, nmg, w_in, tm=512, name="inproj_bwd")
    d_win = _wgrad_in(dproj_parts, un, name="wgrad_in")
    d_wd1 = _wgrad_down(hid1, df1, tk=WGRAD_TOKENS, name="ffn1_wgrad_down")
    (g_win, g_wd1, replicated_early), token = exchange(
        "w_in_ffn1_down", [d_win, d_wd1.reshape(N_DEV, FF_SHARD_PAD, D), d_nm, d_n2, d_nf, d_scale, d_wgroup, loss])

    dx, d_n1, n1, dgu1 = _ffn_bwd(dh1, df1, x, n1g, gu1, wgu1, wd1, token, tm=512, name="ffn1_bwd")
    d_wgu1_a = _wgrad_gate_up(n1, dgu1, tk=WGRAD_TOKENS, name="ffn1_wgrad_gate_up_a", part=0, parts=2)
    (g_wgu1_a, replicated_late), token = exchange("ffn1_gate_up_a", [d_wgu1_a, d_n1])
    d_wgu1_b = _wgrad_gate_up(n1, dgu1, tk=WGRAD_TOKENS, name="ffn1_wgrad_gate_up_b", part=1, parts=2)
    (g_wgu1_b,), token = exchange("last", [d_wgu1_b])
    g_wgu1 = (g_wgu1_a, g_wgu1_b)

    sharded = (g_wgu1, g_wd1, g_win, g_wbp, g_wba, g_wout, g_wgu2, g_wd2)
    return dx, sharded, (replicated_late, replicated_early), token


def _hidden_major(w):
    return jnp.swapaxes(w[0], 0, 1)


def _pad_gate_up(wt):
    d = wt.shape[1]
    wt = wt.astype(BF16).reshape(2, FF_SHARD, d)
    return jnp.pad(wt, ((0, 0), (0, FF_SHARD_PAD - FF_SHARD), (0, 0))).reshape(2 * FF_SHARD_PAD, d)


def _unpad_gate_up(gt):
    d = gt.shape[1]
    return gt.reshape(2, FF_SHARD_PAD, d)[:, :FF_SHARD].reshape(2 * FF_SHARD, d)


def _pad_down(w):
    return jnp.pad(w.astype(BF16), ((0, FF_SHARD_PAD - FF_SHARD), (0, 0)))


def kernel(x, ffn1_norm, ffn1_w_gate_up, ffn1_w_down, mix_norm, w_in, pool_w_group, pool_scale, w_branch_pool, w_branch_attn, w_out, ffn2_norm, ffn2_w_gate_up, ffn2_w_down, final_norm, loss_target, m_ffn1_norm, m_ffn1_w_gate_up, m_ffn1_w_down, m_mix_norm, m_w_in, m_pool_w_group, m_pool_scale, m_w_branch_pool, m_w_branch_attn, m_w_out, m_ffn2_norm, m_ffn2_w_gate_up, m_ffn2_w_down, m_final_norm, v_ffn1_norm, v_ffn1_w_gate_up, v_ffn1_w_down, v_mix_norm, v_w_in, v_pool_w_group, v_pool_scale, v_w_branch_pool, v_w_branch_attn, v_w_out, v_ffn2_norm, v_ffn2_w_gate_up, v_ffn2_w_down, v_final_norm):
    D = x.shape[-1]
    weights = dict(ffn1_norm=ffn1_norm, ffn1_w_gate_up=ffn1_w_gate_up, ffn1_w_down=ffn1_w_down, mix_norm=mix_norm,
                   w_in=w_in, pool_w_group=pool_w_group, pool_scale=pool_scale, w_branch_pool=w_branch_pool,
                   w_branch_attn=w_branch_attn, w_out=w_out, ffn2_norm=ffn2_norm, ffn2_w_gate_up=ffn2_w_gate_up,
                   ffn2_w_down=ffn2_w_down, final_norm=final_norm)
    first = dict(ffn1_norm=m_ffn1_norm, ffn1_w_gate_up=m_ffn1_w_gate_up, ffn1_w_down=m_ffn1_w_down,
                 mix_norm=m_mix_norm, w_in=m_w_in, pool_w_group=m_pool_w_group, pool_scale=m_pool_scale,
                 w_branch_pool=m_w_branch_pool, w_branch_attn=m_w_branch_attn, w_out=m_w_out,
                 ffn2_norm=m_ffn2_norm, ffn2_w_gate_up=m_ffn2_w_gate_up, ffn2_w_down=m_ffn2_w_down,
                 final_norm=m_final_norm)
    second = dict(ffn1_norm=v_ffn1_norm, ffn1_w_gate_up=v_ffn1_w_gate_up, ffn1_w_down=v_ffn1_w_down,
                  mix_norm=v_mix_norm, w_in=v_w_in, pool_w_group=v_pool_w_group, pool_scale=v_pool_scale,
                  w_branch_pool=v_w_branch_pool, w_branch_attn=v_w_branch_attn, w_out=v_w_out,
                  ffn2_norm=v_ffn2_norm, ffn2_w_gate_up=v_ffn2_w_gate_up, ffn2_w_down=v_ffn2_w_down,
                  final_norm=v_final_norm)
    order = list(weights)

    wgu1, = _all_gather([_pad_gate_up(_hidden_major(ffn1_w_gate_up))], name="all_gather_ffn1_gate_up", collective_id=0)
    wd1, = _all_gather([_pad_down(ffn1_w_down[0])], name="all_gather_ffn1_down", collective_id=10)
    transposed = lambda w: jnp.swapaxes(w[0], 0, 1).astype(BF16)
    win_g, = _all_gather([transposed(w_in)], name="all_gather_w_in", collective_id=1)
    wbp_g, wba_g = _all_gather([transposed(w_branch_pool), transposed(w_branch_attn)],
                               name="all_gather_branches", collective_id=2)
    wout_g, = _all_gather([w_out[0].astype(BF16)], name="all_gather_w_out", collective_id=11)
    wgu2, wd2 = _all_gather([_pad_gate_up(_hidden_major(ffn2_w_gate_up)), _pad_down(ffn2_w_down[0])],
                            name="all_gather_ffn2", collective_id=3)
    whole = lambda g: g.reshape(g.shape[0] * g.shape[1], g.shape[2])
    wd1, wd2, win_g, wbp_g, wba_g, wout_g = (whole(g) for g in (wd1, wd2, win_g, wbp_g, wba_g, wout_g))

    cross_ids = {"ffn2": 4, "mix": 5, "w_in_ffn1_down": 8, "ffn1_gate_up_a": 9, "last": 7}
    small = ["ffn1_norm", "mix_norm", "ffn2_norm", "final_norm", "pool_scale", "pool_w_group"]

    def tile_rows(a):
        a = a.reshape(-1, 128)
        return jnp.pad(a, ((0, -a.shape[0] % 8), (0, 0)))

    def exchange(tag, group):
        grads = [g for g in group if g.dtype == BF16]
        extras = [tile_rows(g) for g in group if g.dtype != BF16]
        sums = _chip_sums(grads, name="chip_sums_" + tag)
        partials = [s[0] for s in sums]
        handles = []
        if extras:
            landed, slabs = _cross_chips_and_gather(partials, jnp.concatenate(extras, axis=0),
                                                    name="cross_chips_" + tag, collective_id=cross_ids[tag])
            handles = [slabs]
        else:
            landed = _cross_chips(partials, name="cross_chips_" + tag, collective_id=cross_ids[tag])
        return [(s[1], l) for s, l in zip(sums, landed)] + handles, sums[-1][1]

    norms = (ffn1_norm, mix_norm, ffn2_norm, final_norm.reshape(1, D))
    dx, sharded, (slabs_late, slabs_early), last = _local_step(
        x[0], loss_target[0], norms, pool_w_group[0], pool_scale, wgu1, wd1, win_g, wbp_g, wba_g, wout_g, wgu2, wd2,
        exchange)
    names = ["ffn1_w_gate_up", "ffn1_w_down", "w_in", "w_branch_pool", "w_branch_attn", "w_out",
             "ffn2_w_gate_up", "ffn2_w_down"]
    handles = dict(zip(names, sharded))
    grads, delta, new_m, new_v = {}, {}, {}, {}
    loss_out = []

    def update_replicated(after):
        rows = [weights[k].size // 128 for k in small]
        padded_rows = [-(-r // 8) * 8 for r in rows]
        starts = [sum(padded_rows[:i]) for i in range(len(rows) + 1)]
        total = jnp.concatenate([_sum_devices(slabs_late, after, name="sum_replicated_late"),
                                 _sum_devices(slabs_early, after, name="sum_replicated_early")], axis=0)
        loss_out.append(total[starts[-1], 0])
        small_w = jnp.concatenate([tile_rows(weights[k]) for k in small], axis=0)
        small_m = jnp.concatenate([tile_rows(first[k]) for k in small], axis=0)
        small_v = jnp.concatenate([tile_rows(second[k]) for k in small], axis=0)
        small_out = _adamw(small_w, total[:starts[-1]], small_m, small_v, name="adamw_replicated")
        for name_, start, n_rows in zip(small, starts, rows):
            shape = weights[name_].shape
            grads[name_] = total[start:start + n_rows].reshape(shape)
            delta[name_], new_m[name_], new_v[name_] = (a[start:start + n_rows].reshape(shape) for a in small_out)
        return small_out[0]

    after = last
    for k in ("ffn2_w_down", "ffn2_w_gate_up", "w_branch_pool", "w_branch_attn", "w_out", "w_in", "ffn1_w_down",
              "ffn1_w_gate_up"):
        hidden_major = k.endswith("w_gate_up")
        view = _hidden_major if hidden_major else (lambda a: a[0])
        back = (lambda a: jnp.swapaxes(a, 0, 1)[None]) if hidden_major else (lambda a: a[None])
        groups = 2 if hidden_major else 1
        by_group = lambda a: a.reshape(a.shape[:-2] + (groups, a.shape[-2] // groups, a.shape[-1]))
        state = [by_group(view(a[k])) for a in (weights, first, second)]
        if isinstance(handles[k][0], tuple):
            (own_a, landed_a), (own_b, landed_b) = handles[k]
            out = _owner_sum_adamw(own_a[None], landed_a[:, None], *state, after, name="adamw_" + k + "_a",
                                   transposed=False, group=0)
            out = _owner_sum_adamw(own_b[None], landed_b[:, None], *state, update_replicated(out[1]),
                                   name="adamw_" + k + "_b", transposed=False, group=1, into=out)
        else:
            own, landed = handles[k]
            out = _owner_sum_adamw(by_group(own), by_group(landed), *state, after, name="adamw_" + k,
                                   transposed=k in ("w_in", "w_branch_pool", "w_branch_attn"))
        after = out[1]
        grads[k], delta[k], new_m[k], new_v[k] = (back(a.reshape(-1, a.shape[-1])) for a in out)

    return (loss_out[0], dx[None], *[grads[k] for k in order], *[delta[k] for k in order],
            *[new_m[k] for k in order], *[new_v[k] for k in order])
```

```python
import jax
import jax.numpy as jnp
from jax import lax
from jax.experimental import pallas as pl
from jax.experimental.pallas import tpu as pltpu
from jax.experimental.pallas import tpu_sc as plsc

F32 = jnp.float32
BF16 = jnp.bfloat16
MESH = pl.DeviceIdType.MESH

RMS_EPS = 1e-6
N_DEV = 8
N_HEADS = 8
HEAD_DIM = 64
HEAD_PAIR = 2 * HEAD_DIM
POOL_WINDOWS = (2, 4, 8, 16)
POOL_GROUP = 128
POOL_WIDTH = 512
SB_WIDTH = 512
FF_SHARD = 352
FF_SHARD_PAD = 384
ATTN_K_BLOCK = 256
ATTN_Q_BLOCK_FWD = 512
ATTN_Q_BLOCK_BWD = 256
ATTN_SCALE = 0.125

ADAM_LR = 0.001
ADAM_B1 = 0.9
ADAM_B2 = 0.999
ADAM_EPS = 1e-08
ADAM_WD = 0.01
ADAM_STEP = 10

VMEM_LIMIT = 48 << 20
WGRAD_TOKENS = 2048


def _params(dims=None):
    return pltpu.CompilerParams(dimension_semantics=dims, vmem_limit_bytes=VMEM_LIMIT)


def _mm(a, b):
    return jnp.dot(a, b, preferred_element_type=F32)


def _mm_nt(a, b):
    return lax.dot_general(a, b, (((1,), (1,)), ((), ())), preferred_element_type=F32)


def _mm_tn(a, b):
    return lax.dot_general(a, b, (((0,), (0,)), ((), ())), preferred_element_type=F32)


def _row_tile(rows, cols):
    limit = max(8, (512 * 1024) // cols)
    return max(t for t in range(8, rows + 1, 8) if rows % t == 0 and (t <= limit or t == 8))


def _rstd(xf):
    return lax.rsqrt(jnp.mean(xf * xf, axis=-1, keepdims=True) + RMS_EPS)


def _rms_bwd(xf, gain, dn):
    r = _rstd(xf)
    xh = xf * r
    dgain = jnp.sum(dn * xh, axis=0, keepdims=True)
    dxh = dn * gain
    dx = r * (dxh - xh * jnp.mean(dxh * xh, axis=-1, keepdims=True))
    return dx, dgain


def _ffn_up(x, gain, wgu, *, tm, name):
    T, D = x.shape
    tm = min(tm, T)
    nb, bw = wgu.shape[0] // 2, wgu.shape[1]

    def body(x_ref, gain_ref, wg_ref, wu_ref, gu_ref, hid_ref, n_scr):
        @pl.when(pl.program_id(1) == 0)
        def _():
            xf = x_ref[...]
            n_scr[...] = (xf * _rstd(xf) * gain_ref[...]).astype(BF16)

        halves = (pl.ds(0, tm // 2), pl.ds(tm // 2, tm // 2))
        wg, wu = wg_ref[...], wu_ref[...]
        gus = [(_mm_nt(n_scr[rows, :], wg), _mm_nt(n_scr[rows, :], wu)) for rows in halves]
        for rows, (g, u) in zip(halves, gus):
            gu_ref[0, rows, :] = g.astype(BF16)
            gu_ref[1, rows, :] = u.astype(BF16)
            hid_ref[rows, :] = (g * jax.nn.sigmoid(g) * u).astype(BF16)

    return pl.pallas_call(
        body, name=name, grid=(T // tm, nb),
        in_specs=[
            pl.BlockSpec((tm, D), lambda i, j: (i, 0)),
            pl.BlockSpec((1, D), lambda i, j: (0, 0)),
            pl.BlockSpec((None, bw, D), lambda i, j: (j, 0, 0)),
            pl.BlockSpec((None, bw, D), lambda i, j: (j + nb, 0, 0)),
        ],
        out_specs=[
            pl.BlockSpec((2, tm, bw), lambda i, j: (0, i, j)),
            pl.BlockSpec((tm, bw), lambda i, j: (i, j)),
        ],
        out_shape=[jax.ShapeDtypeStruct((2, T, nb * bw), BF16), jax.ShapeDtypeStruct((T, nb * bw), BF16)],
        scratch_shapes=[pltpu.VMEM((tm, D), BF16)],
        compiler_params=_params(("arbitrary", "arbitrary")),
    )(x, gain, wgu, wgu)


def _ffn_down(x, hid, wd, *, tm, name):
    T, D = x.shape
    tm = min(tm, T)
    F = hid.shape[1]

    def body(x_ref, hid_ref, wd_ref, h_ref):
        h_ref[...] = x_ref[...] + 0.5 * _mm(hid_ref[...], wd_ref[...])

    return pl.pallas_call(
        body, name=name, grid=(T // tm,),
        in_specs=[
            pl.BlockSpec((tm, D), lambda i: (i, 0)),
            pl.BlockSpec((tm, F), lambda i: (i, 0)),
            pl.BlockSpec((F, D), lambda i: (0, 0)),
        ],
        out_specs=pl.BlockSpec((tm, D), lambda i: (i, 0)),
        out_shape=jax.ShapeDtypeStruct((T, D), F32),
        compiler_params=_params(("arbitrary",)),
    )(x, hid, wd)


AFTER = pl.BlockSpec(memory_space=pltpu.HBM)


def _in_hbm(token):
    return pltpu.with_memory_space_constraint(token, pltpu.HBM)


def _ffn_bwd(dh, df, x, gain, gu, wgu, wd, after, *, tm, name):
    T, D = x.shape
    tm = min(tm, T)
    nb, bw = wgu.shape[0] // 2, wgu.shape[1]

    def body(dh_ref, df_ref, x_ref, gain_ref, gu_ref, wg_ref, wu_ref, wd_ref, after_ref,
             dx_ref, dgain_ref, n_ref, dgu_ref, dn_acc):
        i, j = pl.program_id(0), pl.program_id(1)

        @pl.when(j == 0)
        def _():
            xf = x_ref[...]
            n_ref[...] = (xf * _rstd(xf) * gain_ref[...]).astype(BF16)
            dn_acc[...] = jnp.zeros_like(dn_acc)

        @pl.when((i == 0) & (j == 0))
        def _():
            dgain_ref[...] = jnp.zeros_like(dgain_ref)

        halves = (pl.ds(0, tm // 2), pl.ds(tm // 2, tm // 2))
        wd, wg, wu = wd_ref[...], wg_ref[...], wu_ref[...]
        dhids = [_mm_nt(df_ref[rows, :], wd) for rows in halves]
        for rows, dhid in zip(halves, dhids):
            g = gu_ref[0, rows, :].astype(F32)
            u = gu_ref[1, rows, :].astype(F32)
            s = jax.nn.sigmoid(g)
            silu = g * s
            dg = (dhid * u * (s * (1.0 + g * (1.0 - s)))).astype(BF16)
            du = (dhid * silu).astype(BF16)
            dgu_ref[0, rows, :] = dg
            dgu_ref[1, rows, :] = du
            dn_acc[rows, :] += _mm(dg, wg) + _mm(du, wu)

        @pl.when(j == nb - 1)
        def _():
            dx, dgain = _rms_bwd(x_ref[...], gain_ref[...], dn_acc[...])
            dx_ref[...] = dh_ref[...] + dx
            dgain_ref[...] += dgain

    row = lambda i, j: (i, 0)
    return pl.pallas_call(
        body, name=name, grid=(T // tm, nb),
        in_specs=[
            pl.BlockSpec((tm, D), row),
            pl.BlockSpec((tm, D), row),
            pl.BlockSpec((tm, D), row),
            pl.BlockSpec((1, D), lambda i, j: (0, 0)),
            pl.BlockSpec((2, tm, bw), lambda i, j: (0, i, j)),
            pl.BlockSpec((None, bw, D), lambda i, j: (j, 0, 0)),
            pl.BlockSpec((None, bw, D), lambda i, j: (j + nb, 0, 0)),
            pl.BlockSpec((bw, D), lambda i, j: (j, 0)),
            AFTER,
        ],
        out_specs=[
            pl.BlockSpec((tm, D), row),
            pl.BlockSpec((1, D), lambda i, j: (0, 0)),
            pl.BlockSpec((tm, D), row),
            pl.BlockSpec((2, tm, bw), lambda i, j: (0, i, j)),
        ],
        out_shape=[
            jax.ShapeDtypeStruct((T, D), F32),
            jax.ShapeDtypeStruct((1, D), F32),
            jax.ShapeDtypeStruct((T, D), BF16),
            jax.ShapeDtypeStruct((2, T, nb * bw), BF16),
        ],
        scratch_shapes=[pltpu.VMEM((tm, D), F32)],
        compiler_params=_params(("arbitrary", "arbitrary")),
    )(dh, df, x, gain, gu, wgu, wgu, wd, _in_hbm(after))


def _wgrad(a, b, *, grid, a_spec, b_spec, out_spec, out_shape, acc_shape, name):
    nk = grid[2]

    def body(a_ref, b_ref, o_ref, acc):
        k = pl.program_id(2)

        @pl.when(k == 0)
        def _():
            acc[...] = jnp.zeros_like(acc)

        acc[...] += _mm_tn(a_ref[...].astype(BF16), b_ref[...].astype(BF16))

        @pl.when(k == nk - 1)
        def _():
            o_ref[...] = acc[...].astype(o_ref.dtype)

    return pl.pallas_call(
        body, name=name, grid=grid, in_specs=[a_spec, b_spec], out_specs=out_spec,
        out_shape=jax.ShapeDtypeStruct(out_shape, BF16),
        scratch_shapes=[pltpu.VMEM(acc_shape, F32)],
        compiler_params=_params(("arbitrary", "arbitrary", "arbitrary")),
    )(a, b)


def _wgrad_gate_up(n, dgu, *, tk, name, part=0, parts=1):
    T, D = n.shape
    tk = min(tk, T)
    owner_rows = FF_SHARD_PAD * 2
    nb = dgu.shape[2] // owner_rows
    bw = owner_rows // parts
    return _wgrad(
        dgu, n, grid=(2 * nb, 1, T // tk), name=name,
        a_spec=pl.BlockSpec((None, tk, bw), lambda m, c, k: (m // nb, k, parts * (m % nb) + part)),
        b_spec=pl.BlockSpec((tk, D), lambda m, c, k: (k, 0)),
        out_spec=pl.BlockSpec((None, bw, D), lambda m, c, k: (m, 0, 0)),
        out_shape=(2 * nb, bw, D), acc_shape=(bw, D))


def _wgrad_down(hid, df, *, tk, name):
    T, D = df.shape
    tk = min(tk, T)
    bw = FF_SHARD_PAD * 2
    nb = hid.shape[1] // bw
    return _wgrad(
        hid, df, grid=(nb, 1, T // tk), name=name,
        a_spec=pl.BlockSpec((tk, bw), lambda m, c, k: (k, m)),
        b_spec=pl.BlockSpec((tk, D), lambda m, c, k: (k, 0)),
        out_spec=pl.BlockSpec((bw, D), lambda m, c, k: (m, 0)),
        out_shape=(nb * bw, D), acc_shape=(bw, D))


def _wgrad_in(dparts, un, *, name):
    T, D = un.shape
    bw = sum(p.shape[1] for p in dparts) // N_DEV
    first = [sum(p.shape[1] for p in dparts[:i]) // bw for i in range(len(dparts) + 1)]

    def body(*refs):
        dp_refs, un_ref, o_ref = refs[:-2], refs[-2], refs[-1]
        m = pl.program_id(0)
        for dp_ref, lo, hi in zip(dp_refs, first[:-1], first[1:]):
            @pl.when((m >= lo) & (m < hi))
            def _():
                o_ref[...] = _mm_tn(dp_ref[...], un_ref[...]).astype(o_ref.dtype)

    def piece_spec(lo, hi):
        return pl.BlockSpec((T, bw), lambda m: (0, jnp.clip(m - lo, 0, hi - lo - 1)))

    return pl.pallas_call(
        body, name=name, grid=(N_DEV,),
        in_specs=[piece_spec(lo, hi) for lo, hi in zip(first[:-1], first[1:])] + [pl.BlockSpec((T, D), lambda m: (0, 0))],
        out_specs=pl.BlockSpec((None, bw, D), lambda m: (m, 0, 0)),
        out_shape=jax.ShapeDtypeStruct((N_DEV, bw, D), BF16),
        compiler_params=_params(("arbitrary",)),
    )(*dparts, un)


def _wgrad_full(a, b, *, tk, name):
    T, M = a.shape
    tk = min(tk, T)
    N = b.shape[1]
    return _wgrad(
        a, b, grid=(1, 1, T // tk), name=name,
        a_spec=pl.BlockSpec((tk, M), lambda m, c, k: (k, 0)),
        b_spec=pl.BlockSpec((tk, N), lambda m, c, k: (k, 0)),
        out_spec=pl.BlockSpec((M, N), lambda m, c, k: (0, 0)), out_shape=(M, N), acc_shape=(M, N))


def _loss_bwd(h, target, gain, *, tm, name):
    T, D = h.shape
    tm = min(tm, T)

    def body(h_ref, t_ref, gain_ref, dh_ref, df_ref, loss_ref, dgain_ref):
        @pl.when(pl.program_id(0) == 0)
        def _():
            loss_ref[...] = jnp.zeros_like(loss_ref)
            dgain_ref[...] = jnp.zeros_like(dgain_ref)

        xf = h_ref[...]
        gain = gain_ref[...]
        err = xf * _rstd(xf) * gain - t_ref[...]
        loss_ref[...] += 0.5 * jnp.sum(jnp.mean(err * err, axis=-1, keepdims=True), axis=0, keepdims=True)
        dx, dgain = _rms_bwd(xf, gain, err * (1.0 / D))
        dh_ref[...] = dx
        df_ref[...] = (0.5 * dx).astype(BF16)
        dgain_ref[...] += dgain

    row = lambda i: (i, 0)
    fixed = lambda i: (0, 0)
    return pl.pallas_call(
        body, name=name, grid=(T // tm,),
        in_specs=[pl.BlockSpec((tm, D), row), pl.BlockSpec((tm, D), row), pl.BlockSpec((1, D), fixed)],
        out_specs=[pl.BlockSpec((tm, D), row), pl.BlockSpec((tm, D), row), pl.BlockSpec((1, 128), fixed),
                   pl.BlockSpec((1, D), fixed)],
        out_shape=[jax.ShapeDtypeStruct((T, D), F32), jax.ShapeDtypeStruct((T, D), BF16),
                   jax.ShapeDtypeStruct((1, 128), F32), jax.ShapeDtypeStruct((1, D), F32)],
        compiler_params=_params(("arbitrary",)),
    )(h, target, gain)


def _inproj_fwd(h, gain, w_in_t, *, tm, name):
    T, D = h.shape
    tm = min(tm, T)
    bn = D
    nb = w_in_t.shape[0] // bn

    def body(h_ref, gain_ref, wt_ref, un_ref, proj_ref):
        @pl.when(pl.program_id(1) == 0)
        def _():
            xf = h_ref[...]
            un_ref[...] = (xf * _rstd(xf) * gain_ref[...]).astype(BF16)

        proj_ref[...] = _mm_nt(un_ref[...], wt_ref[...])

    return pl.pallas_call(
        body, name=name, grid=(T // tm, nb),
        in_specs=[
            pl.BlockSpec((tm, D), lambda i, j: (i, 0)),
            pl.BlockSpec((1, D), lambda i, j: (0, 0)),
            pl.BlockSpec((bn, D), lambda i, j: (j, 0)),
        ],
        out_specs=[pl.BlockSpec((tm, D), lambda i, j: (i, 0)), pl.BlockSpec((tm, bn), lambda i, j: (i, j))],
        out_shape=[jax.ShapeDtypeStruct((T, D), BF16), jax.ShapeDtypeStruct((T, nb * bn), F32)],
        compiler_params=_params(("arbitrary", "arbitrary")),
    )(h, gain, w_in_t)


def _inproj_bwd(dparts, dh, h, gain, w_in_t, *, tm, name):
    T, D = h.shape
    tm = min(tm, T)
    n = len(dparts)
    widths = [p.shape[1] for p in dparts]
    starts = [sum(widths[:i]) for i in range(n)]

    def body(*refs):
        dp_refs = refs[:n]
        dh_ref, h_ref, gain_ref, wt_ref, dx_ref, df_ref, dgain_ref = refs[n:]

        @pl.when(pl.program_id(0) == 0)
        def _():
            dgain_ref[...] = jnp.zeros_like(dgain_ref)

        dn = sum(_mm(dp_ref[...], wt_ref[start:start + width, :])
                 for dp_ref, start, width in zip(dp_refs, starts, widths))
        dx, dgain = _rms_bwd(h_ref[...], gain_ref[...], dn)
        dh_in = dh_ref[...] + dx
        dx_ref[...] = dh_in
        df_ref[...] = (0.5 * dh_in).astype(BF16)
        dgain_ref[...] += dgain

    row = lambda i: (i, 0)
    fixed = lambda i: (0, 0)
    return pl.pallas_call(
        body, name=name, grid=(T // tm,),
        in_specs=[pl.BlockSpec((tm, width), row) for width in widths] + [
            pl.BlockSpec((tm, D), row),
            pl.BlockSpec((tm, D), row),
            pl.BlockSpec((1, D), fixed),
            pl.BlockSpec(w_in_t.shape, fixed),
        ],
        out_specs=[pl.BlockSpec((tm, D), row), pl.BlockSpec((tm, D), row), pl.BlockSpec((1, D), fixed)],
        out_shape=[jax.ShapeDtypeStruct((T, D), F32), jax.ShapeDtypeStruct((T, D), BF16),
                   jax.ShapeDtypeStruct((1, D), F32)],
        compiler_params=_params(("arbitrary",)),
    )(*dparts, dh, h, gain, w_in_t)


def _window_sum(x, row, doublings, *, backward):
    T = x.shape[0]
    s = x
    for k in range(doublings):
        sh = 1 << k
        if backward:
            s = s + jnp.where(row < T - sh, pltpu.roll(s, T - sh, 0), 0.0)
        else:
            s = s + jnp.where(row >= sh, pltpu.roll(s, sh, 0), 0.0)
    return s


def _pool_fwd(proj, w_group, scale, *, name):
    T = proj.shape[0]

    def body(xp_ref, w_ref, scale_ref, p_ref):
        row = lax.broadcasted_iota(jnp.int32, (T, POOL_GROUP), 0)
        for gi, window in enumerate(POOL_WINDOWS):
            cols = slice(gi * POOL_GROUP, (gi + 1) * POOL_GROUP)
            x = xp_ref[:, cols]
            inv_count = 1.0 / jnp.minimum(row + 1, window).astype(F32)
            yc = _window_sum(x, row, gi + 1, backward=False) * inv_count - x
            pre = _mm(yc.astype(BF16), w_ref[gi].astype(BF16))
            p_ref[:, cols] = pre * scale_ref[:, cols]

    return pl.pallas_call(
        body, name=name, grid=(1,),
        in_specs=[
            pl.BlockSpec((T, POOL_WIDTH), lambda i: (0, 0)),
            pl.BlockSpec(w_group.shape, lambda i: (0, 0, 0)),
            pl.BlockSpec((1, POOL_WIDTH), lambda i: (0, 0)),
        ],
        out_specs=pl.BlockSpec((T, POOL_WIDTH), lambda i: (0, 0)),
        out_shape=jax.ShapeDtypeStruct((T, POOL_WIDTH), F32),
        compiler_params=_params(("arbitrary",)),
    )(proj, w_group, scale)


def _pool_bwd(dp, proj, w_group, scale, *, name):
    T = proj.shape[0]

    def body(dp_ref, xp_ref, w_ref, scale_ref, dxp_ref, dw_ref, dscale_ref):
        row = lax.broadcasted_iota(jnp.int32, (T, POOL_GROUP), 0)
        for gi, window in enumerate(POOL_WINDOWS):
            cols = slice(gi * POOL_GROUP, (gi + 1) * POOL_GROUP)
            x = xp_ref[:, cols]
            inv_count = 1.0 / jnp.minimum(row + 1, window).astype(F32)
            yc = (_window_sum(x, row, gi + 1, backward=False) * inv_count - x).astype(BF16)
            w = w_ref[gi].astype(BF16)
            pre = _mm(yc, w)
            dpg = dp_ref[:, cols]
            dscale_ref[:, cols] = jnp.sum(dpg * pre, axis=0, keepdims=True)
            dpre = (dpg * scale_ref[:, cols]).astype(BF16)
            dw_ref[gi] = _mm_tn(yc, dpre)
            dyc = _mm_nt(dpre, w)
            dxp_ref[:, cols] = (_window_sum(dyc * inv_count, row, gi + 1, backward=True) - dyc).astype(BF16)

    return pl.pallas_call(
        body, name=name, grid=(1,),
        in_specs=[
            pl.BlockSpec((T, POOL_WIDTH), lambda i: (0, 0)),
            pl.BlockSpec((T, POOL_WIDTH), lambda i: (0, 0)),
            pl.BlockSpec(w_group.shape, lambda i: (0, 0, 0)),
            pl.BlockSpec((1, POOL_WIDTH), lambda i: (0, 0)),
        ],
        out_specs=[
            pl.BlockSpec((T, POOL_WIDTH), lambda i: (0, 0)),
            pl.BlockSpec(w_group.shape, lambda i: (0, 0, 0)),
            pl.BlockSpec((1, POOL_WIDTH), lambda i: (0, 0)),
        ],
        out_shape=[jax.ShapeDtypeStruct((T, POOL_WIDTH), BF16), jax.ShapeDtypeStruct(w_group.shape, F32),
                   jax.ShapeDtypeStruct((1, POOL_WIDTH), F32)],
        compiler_params=_params(("arbitrary",)),
    )(dp, proj, w_group, scale)


ATTN_STRIP = 32


def _log_sigmoids(z):
    lb = jnp.minimum(z, 0.0) - jnp.log(1.0 + jnp.exp(-jnp.abs(z)))
    return lb, lb - z


def _transposed_blocks(x_ref, blocks_scr, tq):
    for b in range(blocks_scr.shape[0]):
        blocks_scr[b] = x_ref[b * tq:(b + 1) * tq, :].T.astype(BF16)


def _split_bf16(x):
    hi = x.astype(BF16)
    return hi, (x - hi.astype(F32)).astype(BF16)


def _strips(n):
    return [slice(i, i + ATTN_STRIP) for i in range(0, n, ATTN_STRIP)]


def _rows(parts):
    return jnp.concatenate(parts, axis=0)


def _attn_specs(T, tq):
    q_col = POOL_WIDTH // HEAD_PAIR
    k_col = q_col + SB_WIDTH // HEAD_PAIR
    v_col = k_col + SB_WIDTH // HEAD_PAIR
    return [
        pl.BlockSpec((tq, HEAD_PAIR), lambda p, i: (i, q_col + p)),
        pl.BlockSpec((T, HEAD_PAIR), lambda p, i: (0, k_col + p)),
        pl.BlockSpec((T, HEAD_PAIR), lambda p, i: (0, v_col + p)),
    ]


def _attn_fwd(proj, *, name):
    T = proj.shape[0]
    tk = min(ATTN_K_BLOCK, T)
    tq = min(ATTN_Q_BLOCK_FWD, T)
    diagonal_blocks = tq // tk

    def body(q_ref, k_ref, v_ref, o_ref, lt_ref, kt_scr, vb_scr):
        qi = pl.program_id(1)

        @pl.when(qi == 0)
        def _():
            _transposed_blocks(k_ref, kt_scr, tk)
            vb_scr[...] = v_ref[...].astype(BF16)

        head0 = lax.broadcasted_iota(jnp.int32, (tq, HEAD_PAIR), 1) < HEAD_DIM
        q = q_ref[...] * ATTN_SCALE
        qs = (jnp.where(head0, q, 0.0).astype(BF16), jnp.where(head0, 0.0, q).astype(BF16))
        r = lax.broadcasted_iota(jnp.int32, (tq, tk), 0)
        c = lax.broadcasted_iota(jnp.int32, (tq, tk), 1)
        later = (r[:tk] > c[:tk]).astype(BF16)
        later2 = _rows([later, later])
        causal = lambda d: (lambda rows: c[rows] + d * tk < r[rows])
        strips = _strips(tq)

        def log_terms(z, valid):
            lbs, his, los, sums = [], [], [], []
            for rows in strips:
                lb, lm = _log_sigmoids(z[rows])
                if valid is not None:
                    lm = jnp.where(valid(rows), lm, 0.0)
                hi, lo = _split_bf16(lm)
                lbs.append(lb)
                his.append(hi)
                los.append(lo)
                sums.append(jnp.sum(lm, axis=1, keepdims=True))
            return lbs, jnp.concatenate([_rows(his), _rows(los)], axis=1), _rows(sums)

        def weights(lbs, run, after, valid):
            parts = []
            for rows, lb in zip(strips, lbs):
                a = jnp.exp(lb + run[rows] + after[rows])
                if valid is not None:
                    a = jnp.where(valid(rows), a, 0.0)
                parts.append(a.astype(BF16))
            return _rows(parts)

        def block(kj, carry, valid):
            kt = kt_scr[kj]
            vb = vb_scr[pl.ds(pl.multiple_of(kj * tk, tk), tk), :]
            run0, o0, run1, o1 = carry
            z0 = _mm(qs[0], kt)
            z1 = _mm(qs[1], kt)
            lbs0, split0, sums0 = log_terms(z0, valid)
            after0 = _mm(split0, later2)
            lbs1, split1, sums1 = log_terms(z1, valid)
            after1 = _mm(split1, later2)
            o0 = o0 + _mm(weights(lbs0, run0, after0, valid), vb)
            o1 = o1 + _mm(weights(lbs1, run1, after1, valid), vb)
            return run0 + sums0, o0, run1 + sums1, o1

        zero = (jnp.zeros((tq, 1), F32), jnp.zeros((tq, HEAD_PAIR), F32))
        first = diagonal_blocks * qi
        carry = zero + zero
        for d in reversed(range(diagonal_blocks)):
            carry = block(first + d, carry, causal(d))
        carry = lax.fori_loop(0, first, lambda it, cr: block(first - 1 - it, cr, None), carry)
        o_ref[...] = jnp.where(head0, carry[1], carry[3])
        lt_ref[...] = jnp.where(head0, carry[0], carry[2])

    out_spec = pl.BlockSpec((tq, HEAD_PAIR), lambda p, i: (i, p))
    return pl.pallas_call(
        body, name=name, grid=(N_HEADS // 2, T // tq),
        in_specs=_attn_specs(T, tq), out_specs=[out_spec, out_spec],
        out_shape=[jax.ShapeDtypeStruct((T, SB_WIDTH), F32), jax.ShapeDtypeStruct((T, SB_WIDTH), F32)],
        scratch_shapes=[pltpu.VMEM((T // tk, HEAD_PAIR, tk), BF16), pltpu.VMEM((T, HEAD_PAIR), BF16)],
        compiler_params=_params(("arbitrary", "arbitrary")),
    )(proj, proj, proj)


def _attn_bwd(proj, do, ltot, after, *, name):
    T = proj.shape[0]
    tk = min(ATTN_K_BLOCK, T)
    tq = min(ATTN_Q_BLOCK_BWD, T)
    diagonal_blocks = tq // tk

    def body(q_ref, k_ref, v_ref, do_ref, lt_ref, after_ref, dq_ref, dk_ref, dv_ref,
             kb_scr, kt_scr, vt_scr, dkt_ref, dvt_ref):
        qi = pl.program_id(1)

        @pl.when(qi == 0)
        def _():
            kb_scr[...] = k_ref[...].astype(BF16)
            _transposed_blocks(k_ref, kt_scr, tk)
            _transposed_blocks(v_ref, vt_scr, tk)
            dkt_ref[...] = jnp.zeros_like(dkt_ref)
            dvt_ref[...] = jnp.zeros_like(dvt_ref)

        head0 = lax.broadcasted_iota(jnp.int32, (tq, HEAD_PAIR), 1) < HEAD_DIM
        q, do_, lt = q_ref[...] * ATTN_SCALE, do_ref[...], lt_ref[...]
        qs = (jnp.where(head0, q, 0.0).astype(BF16), jnp.where(head0, 0.0, q).astype(BF16))
        q_heads = (jnp.where(head0, q, 0.0), jnp.where(head0, 0.0, q))
        do_heads = (jnp.where(head0, do_, 0.0), jnp.where(head0, 0.0, do_))
        dos = tuple(d.astype(BF16) for d in do_heads)
        qts = tuple(x.T.astype(BF16) for x in q_heads)
        dots = tuple(d.T.astype(BF16) for d in do_heads)
        lts = (jnp.max(jnp.where(head0, lt, -jnp.inf), axis=1, keepdims=True),
               jnp.max(jnp.where(head0, -jnp.inf, lt), axis=1, keepdims=True))
        r = lax.broadcasted_iota(jnp.int32, (tq, tk), 0)
        c = lax.broadcasted_iota(jnp.int32, (tq, tk), 1)
        upto = (r[:tk] <= c[:tk]).astype(BF16)
        before = (r[:tk] < c[:tk]).astype(BF16)
        upto2, before2 = _rows([upto, upto]), _rows([before, before])
        causal = lambda d: (lambda rows: c[rows] + d * tk < r[rows])
        strips = _strips(tq)

        def log_terms(z, valid):
            lbs, his, los, sums = [], [], [], []
            for rows in strips:
                lb, lm = _log_sigmoids(z[rows])
                if valid is not None:
                    lm = jnp.where(valid(rows), lm, 0.0)
                hi, lo = _split_bf16(lm)
                lbs.append(lb)
                his.append(hi)
                los.append(lo)
                sums.append(jnp.sum(lm, axis=1, keepdims=True))
            return lbs, jnp.concatenate([_rows(his), _rows(los)], axis=1), _rows(sums)

        def weights(lbs, rest, lm_upto, da, valid):
            a_parts, es, his, los, sums = [], [], [], [], []
            for rows, lb in zip(strips, lbs):
                a = jnp.exp(lb + (rest[rows] - lm_upto[rows]))
                if valid is not None:
                    a = jnp.where(valid(rows), a, 0.0)
                e = da[rows] * a
                hi, lo = _split_bf16(e)
                a_parts.append(a.astype(BF16))
                es.append(e)
                his.append(hi)
                los.append(lo)
                sums.append(jnp.sum(e, axis=1, keepdims=True))
            return _rows(a_parts), es, jnp.concatenate([_rows(his), _rows(los)], axis=1), _rows(sums)

        def score_grads(lbs, es, run_e, e_before, valid):
            parts = []
            for rows, lb, e in zip(strips, lbs, es):
                beta = jnp.exp(lb)
                dz = e * (1.0 - beta) - (run_e[rows] + e_before[rows]) * beta
                if valid is not None:
                    dz = jnp.where(valid(rows), dz, 0.0)
                parts.append(dz.astype(BF16))
            return _rows(parts)

        def block(kj, carry, valid):
            off = pl.multiple_of(kj * tk, tk)
            kb, kt, vt = kb_scr[pl.ds(off, tk), :], kt_scr[kj], vt_scr[kj]
            run_lm0, run_e0, dq0, run_lm1, run_e1, dq1 = carry
            z0, da0 = _mm(qs[0], kt), _mm(dos[0], vt)
            z1, da1 = _mm(qs[1], kt), _mm(dos[1], vt)
            lbs0, split0, lm_sums0 = log_terms(z0, valid)
            lm_upto0 = _mm(split0, upto2)
            lbs1, split1, lm_sums1 = log_terms(z1, valid)
            lm_upto1 = _mm(split1, upto2)
            a0, es0, split0, e_sums0 = weights(lbs0, lts[0] - run_lm0, lm_upto0, da0, valid)
            e_before0 = _mm(split0, before2)
            a1, es1, split1, e_sums1 = weights(lbs1, lts[1] - run_lm1, lm_upto1, da1, valid)
            e_before1 = _mm(split1, before2)
            dz0 = score_grads(lbs0, es0, run_e0, e_before0, valid)
            dkt_blk = _mm(qts[0], dz0)
            dvt_blk = _mm(dots[0], a0)
            dq0 = dq0 + _mm(dz0, kb)
            dz1 = score_grads(lbs1, es1, run_e1, e_before1, valid)
            dkt_ref[kj] += dkt_blk + _mm(qts[1], dz1)
            dvt_ref[kj] += dvt_blk + _mm(dots[1], a1)
            dq1 = dq1 + _mm(dz1, kb)
            return run_lm0 + lm_sums0, run_e0 + e_sums0, dq0, run_lm1 + lm_sums1, run_e1 + e_sums1, dq1

        zero = (jnp.zeros((tq, 1), F32), jnp.zeros((tq, 1), F32), jnp.zeros((tq, HEAD_PAIR), F32))
        first = diagonal_blocks * qi
        carry = lax.fori_loop(0, first, lambda kj, cr: block(kj, cr, None), zero + zero)
        for d in range(diagonal_blocks):
            carry = block(first + d, carry, causal(d))
        dq_ref[...] = (jnp.where(head0, carry[2], carry[5]) * ATTN_SCALE).astype(BF16)

        @pl.when(qi == T // tq - 1)
        def _():
            for b in range(T // tk):
                dk_ref[b * tk:(b + 1) * tk, :] = dkt_ref[b].T.astype(BF16)
                dv_ref[b * tk:(b + 1) * tk, :] = dvt_ref[b].T.astype(BF16)

    blk = pl.BlockSpec((tq, HEAD_PAIR), lambda p, i: (i, p))
    seq = pl.BlockSpec((T, HEAD_PAIR), lambda p, i: (0, p))
    transposed = pltpu.VMEM((T // tk, HEAD_PAIR, tk), F32)
    return pl.pallas_call(
        body, name=name, grid=(N_HEADS // 2, T // tq),
        in_specs=_attn_specs(T, tq) + [blk, blk, AFTER], out_specs=[blk, seq, seq],
        out_shape=[jax.ShapeDtypeStruct((T, SB_WIDTH), BF16)] * 3,
        scratch_shapes=[pltpu.VMEM((T, HEAD_PAIR), BF16), pltpu.VMEM((T // tk, HEAD_PAIR, tk), BF16),
                        pltpu.VMEM((T // tk, HEAD_PAIR, tk), BF16), transposed, transposed],
        compiler_params=_params(("arbitrary", "arbitrary")),
    )(proj, proj, proj, do, ltot, _in_hbm(after))


def _mix_specs(T, D, tm, wbp, w_out):
    gate_col = (POOL_WIDTH + 3 * SB_WIDTH) // D
    row = lambda i: (i, 0)
    return [
        pl.BlockSpec((tm, D), row),
        pl.BlockSpec((tm, POOL_WIDTH), row),
        pl.BlockSpec((tm, SB_WIDTH), row),
        pl.BlockSpec((tm, D), lambda i: (i, gate_col)),
        pl.BlockSpec((tm, D), lambda i: (i, gate_col + 1)),
        pl.BlockSpec(wbp.shape, lambda i: (0, 0)),
        pl.BlockSpec(wbp.shape, lambda i: (0, 0)),
        pl.BlockSpec(w_out.shape, lambda i: (0, 0)),
    ]


def _mix_fwd(h, p, o, proj, wbp, wba, w_out, *, tm, name):
    T, D = h.shape
    tm = min(tm, T)

    def body(h_ref, p_ref, o_ref, glp_ref, gls_ref, wbp_ref, wba_ref, wout_ref, hout_ref, m_ref):
        halves = (pl.ds(0, tm // 2), pl.ds(tm // 2, tm // 2))
        wbp, wba, wout = wbp_ref[...], wba_ref[...], wout_ref[...]
        branches = [(_mm_nt(p_ref[rows, :].astype(BF16), wbp), _mm_nt(o_ref[rows, :].astype(BF16), wba))
                    for rows in halves]
        for rows, (yp, ys) in zip(halves, branches):
            m = (jax.nn.sigmoid(glp_ref[rows, :]) * yp + jax.nn.sigmoid(gls_ref[rows, :]) * ys).astype(BF16)
            m_ref[rows, :] = m
            hout_ref[rows, :] = h_ref[rows, :] + _mm(m, wout)

    row = lambda i: (i, 0)
    return pl.pallas_call(
        body, name=name, grid=(T // tm,),
        in_specs=_mix_specs(T, D, tm, wbp, w_out),
        out_specs=[pl.BlockSpec((tm, D), row), pl.BlockSpec((tm, D), row)],
        out_shape=[jax.ShapeDtypeStruct((T, D), F32), jax.ShapeDtypeStruct((T, D), BF16)],
        compiler_params=_params(("arbitrary",)),
    )(h, p, o, proj, proj, wbp, wba, w_out)


def _mix_bwd(dh, p, o, proj, wbp, wba, w_out, after, *, tm, name):
    T, D = dh.shape
    tm = min(tm, T)

    def body(dh_ref, p_ref, o_ref, glp_ref, gls_ref, wbp_ref, wba_ref, wout_ref, after_ref,
             dyp_ref, dys_ref, dp_ref, do_ref, dgl_ref):
        halves = (pl.ds(0, tm // 2), pl.ds(tm // 2, tm // 2))
        wbp, wba, wout = wbp_ref[...], wba_ref[...], wout_ref[...]
        products = [(_mm_nt(dh_ref[rows, :].astype(BF16), wout), _mm_nt(p_ref[rows, :].astype(BF16), wbp),
                     _mm_nt(o_ref[rows, :].astype(BF16), wba)) for rows in halves]
        for rows, (dm, yp, ys) in zip(halves, products):
            gp = jax.nn.sigmoid(glp_ref[rows, :])
            gs = jax.nn.sigmoid(gls_ref[rows, :])
            dyp = (dm * gp).astype(BF16)
            dys = (dm * gs).astype(BF16)
            dyp_ref[rows, :] = dyp
            dys_ref[rows, :] = dys
            dgl_ref[rows, :D] = (dm * yp * gp * (1.0 - gp)).astype(BF16)
            dgl_ref[rows, D:] = (dm * ys * gs * (1.0 - gs)).astype(BF16)
            dp_ref[rows, :] = _mm(dyp, wbp)
            do_ref[rows, :] = _mm(dys, wba)

    row = lambda i: (i, 0)
    return pl.pallas_call(
        body, name=name, grid=(T // tm,),
        in_specs=_mix_specs(T, D, tm, wbp, w_out) + [AFTER],
        out_specs=[pl.BlockSpec((tm, D), row), pl.BlockSpec((tm, D), row), pl.BlockSpec((tm, POOL_WIDTH), row),
                   pl.BlockSpec((tm, SB_WIDTH), row), pl.BlockSpec((tm, 2 * D), row)],
        out_shape=[jax.ShapeDtypeStruct((T, D), BF16), jax.ShapeDtypeStruct((T, D), BF16),
                   jax.ShapeDtypeStruct((T, POOL_WIDTH), F32), jax.ShapeDtypeStruct((T, SB_WIDTH), F32),
                   jax.ShapeDtypeStruct((T, 2 * D), BF16)],
        compiler_params=_params(("arbitrary",)),
    )(dh, p, o, proj, proj, wbp, wba, w_out, _in_hbm(after))


def _adamw_update(w, g, m, v):
    m_ = ADAM_B1 * m + (1.0 - ADAM_B1) * g
    v_ = ADAM_B2 * v + (1.0 - ADAM_B2) * (g * g)
    m_hat = m_ / (1.0 - ADAM_B1 ** ADAM_STEP)
    v_hat = v_ / (1.0 - ADAM_B2 ** ADAM_STEP)
    return -ADAM_LR * (m_hat / (jnp.sqrt(v_hat) + ADAM_EPS) + ADAM_WD * w), m_, v_


def _adamw(w, g, m, v, *, name):
    R, C = w.shape
    tr = _row_tile(R, C)

    def body(w_ref, g_ref, m_ref, v_ref, d_ref, nm_ref, nv_ref):
        d_ref[...], nm_ref[...], nv_ref[...] = _adamw_update(w_ref[...], g_ref[...], m_ref[...], v_ref[...])

    spec = pl.BlockSpec((tr, C), lambda i: (i, 0))
    return pl.pallas_call(
        body, name=name, grid=(R // tr,), in_specs=[spec] * 4, out_specs=[spec] * 3,
        out_shape=[jax.ShapeDtypeStruct((R, C), F32)] * 3,
        compiler_params=_params(("arbitrary",)),
    )(w, g, m, v)


def _position():
    return lax.axis_index("x"), lax.axis_index("y"), lax.axis_index("c")


def _all_gather(shards, *, name, collective_id):
    n = len(shards)
    n_copies = 9

    def body(*refs):
        ins, outs = refs[:n], refs[n:2 * n]
        send_sems, recv_sems, local_sems = refs[2 * n:]
        x, y, c = _position()
        me, sibling = (x, y, c), (x, y, 1 - c)
        x_nbr, y_nbr, diagonal = (1 - x, y, c), (x, 1 - y, c), (1 - x, 1 - y, c)
        other = lambda pos: (pos[0], pos[1], 1 - c)

        barrier = pltpu.get_barrier_semaphore()
        for peer in (sibling, x_nbr, y_nbr):
            pl.semaphore_signal(barrier, inc=1, device_id=peer, device_id_type=MESH)
        pl.semaphore_wait(barrier, 3)

        def block(a, pos, half=None):
            ref = outs[a].at[4 * pos[0] + 2 * pos[1] + pos[2]]
            rows = ref.shape[0] // 2
            return ref if half is None else ref.at[pl.ds(half * rows, rows)]

        def copy(a, k, pos, to, half=None, src=None):
            return pltpu.make_async_remote_copy(
                src_ref=block(a, pos, half) if src is None else src, dst_ref=block(a, pos, half),
                send_sem=send_sems.at[n_copies * a + k], recv_sem=recv_sems.at[n_copies * a + k],
                device_id=to, device_id_type=MESH)

        started = []
        for a in range(n):
            mine = pltpu.make_async_copy(ins[a], block(a, me), local_sems.at[a])
            mine.start()
            started.append(mine)
        sends = []
        for a in range(n):
            sends += [copy(a, 1, me, x_nbr, src=ins[a]), copy(a, 2, me, y_nbr, src=ins[a]),
                      copy(a, 0, me, sibling, src=ins[a])]
        for cp in sends:
            cp.start()

        def pass_on(copies):
            for cp in copies:
                cp.start()
                sends.append(cp)

        for a in range(n):
            copy(a, 1, x_nbr, me).wait_recv()
            pass_on([copy(a, 5, x_nbr, y_nbr, half=0), copy(a, 3, x_nbr, sibling)])
            copy(a, 2, y_nbr, me).wait_recv()
            pass_on([copy(a, 6, y_nbr, x_nbr, half=1), copy(a, 4, y_nbr, sibling)])
        for a in range(n):
            copy(a, 5, diagonal, me, half=0).wait_recv()
            pass_on([copy(a, 7, diagonal, sibling, half=0)])
            copy(a, 6, diagonal, me, half=1).wait_recv()
            pass_on([copy(a, 8, diagonal, sibling, half=1)])
        for a in range(n):
            copy(a, 0, sibling, me).wait_recv()
            copy(a, 3, other(x_nbr), me).wait_recv()
            copy(a, 4, other(y_nbr), me).wait_recv()
            copy(a, 7, other(diagonal), me, half=0).wait_recv()
            copy(a, 8, other(diagonal), me, half=1).wait_recv()
        for cp in sends:
            cp.wait_send()
        for cp in started:
            cp.wait()

    return pl.kernel(
        body, name=name,
        out_type=[jax.ShapeDtypeStruct((N_DEV,) + s.shape, s.dtype) for s in shards],
        mesh=plsc.ScalarSubcoreMesh(axis_name="sequencer", num_cores=1),
        scratch_types=[pltpu.SemaphoreType.DMA((n_copies * n,)), pltpu.SemaphoreType.DMA((n_copies * n,)),
                       pltpu.SemaphoreType.DMA((n,))],
        compiler_params=pltpu.CompilerParams(collective_id=collective_id),
    )(*shards)


def _chip_sums(group, *, name):
    n = len(group)
    shapes = [g.shape[1:] for g in group]

    def body(*refs):
        g_refs, partials_out, own_out = refs[:n], refs[n:3 * n:2], refs[n + 1:3 * n:2]
        mines, theirs = refs[3 * n:7 * n:4], refs[3 * n + 1:7 * n:4]
        partials, out_refs = refs[3 * n + 2:7 * n:4], refs[3 * n + 3:7 * n:4]
        send_sems, recv_sems, local_sems, store_sems = refs[7 * n:]
        x, y, c = _position()
        my_chip = 2 * x + y

        def swap(a, s):
            return pltpu.make_async_remote_copy(
                src_ref=g_refs[a].at[2 * s + (1 - c)], dst_ref=theirs[a].at[s],
                send_sem=send_sems.at[4 * a + s], recv_sem=recv_sems.at[4 * a + s],
                device_id=(x, y, 1 - c), device_id_type=MESH)

        def load(a, s):
            return pltpu.make_async_copy(g_refs[a].at[2 * s + c], mines[a].at[s], local_sems.at[4 * a + s])

        def store(a, s, own):
            if own:
                return pltpu.make_async_copy(out_refs[a], own_out[a], store_sems.at[4 * a + s])
            j = (s ^ my_chip) - 1
            return pltpu.make_async_copy(partials[a].at[j], partials_out[a].at[j], store_sems.at[4 * a + s])

        for a in range(n):
            for s in range(4):
                swap(a, s).start()
                load(a, s).start(priority=1)

        for a, (R, C) in enumerate(shapes):
            rc = 128 if R % 128 == 0 else R

            def chip_sum(chip, rows):
                return mines[a][chip, rows, :].astype(F32) + theirs[a][chip, rows, :].astype(F32)

            for s in range(4):
                load(a, s).wait()
                swap(a, s).wait_recv()

                @pl.when(s == my_chip)
                def _():
                    @pl.loop(0, R // rc)
                    def _(t):
                        rows = pl.ds(pl.multiple_of(t * rc, rc), rc)
                        out_refs[a][rows, :] = chip_sum(s, rows)
                    store(a, s, True).start(priority=1)

                @pl.when(s != my_chip)
                def _():
                    @pl.loop(0, R // rc)
                    def _(t):
                        rows = pl.ds(pl.multiple_of(t * rc, rc), rc)
                        partials[a][(s ^ my_chip) - 1, rows, :] = chip_sum(s, rows).astype(BF16)
                    store(a, s, False).start(priority=1)

        for a in range(n):
            for s in range(4):
                swap(a, s).wait_send()
                pl.when(s == my_chip)(store(a, s, True).wait)
                pl.when(s != my_chip)(store(a, s, False).wait)

    hbm = pl.BlockSpec(memory_space=pl.ANY)
    outs = pl.pallas_call(
        body, name=name,
        in_specs=[hbm] * n, out_specs=[hbm] * (2 * n),
        out_shape=[shape for R, C in shapes
                   for shape in (jax.ShapeDtypeStruct((3, R, C), BF16), jax.ShapeDtypeStruct((R, C), F32))],
        scratch_shapes=[scratch for R, C in shapes for scratch in (
            pltpu.VMEM((4, R, C), BF16), pltpu.VMEM((4, R, C), BF16), pltpu.VMEM((3, R, C), BF16),
            pltpu.VMEM((R, C), F32))] + [pltpu.SemaphoreType.DMA((4 * n,))] * 4,
        compiler_params=_params(),
    )(*group)
    return [(outs[2 * a], outs[2 * a + 1]) for a in range(n)]


def _cross_chips(partials, *, name, collective_id):
    n = len(partials)

    def body(*refs):
        ins, outs = refs[:n], refs[n:2 * n]
        send_sems, recv_sems = refs[2 * n:]
        x, y, c = _position()
        my_chip = 2 * x + y
        peers = [((my_chip ^ j) // 2, (my_chip ^ j) % 2, c) for j in (1, 2, 3)]

        barrier = pltpu.get_barrier_semaphore()
        for peer in peers:
            pl.semaphore_signal(barrier, inc=1, device_id=peer, device_id_type=MESH)
        pl.semaphore_wait(barrier, 3)

        copies = [
            pltpu.make_async_remote_copy(
                src_ref=ins[a].at[j], dst_ref=outs[a].at[j],
                send_sem=send_sems.at[3 * a + j], recv_sem=recv_sems.at[3 * a + j],
                device_id=peers[j], device_id_type=MESH)
            for a in range(n) for j in range(3)]
        for cp in copies:
            cp.start()
        for cp in copies:
            cp.wait_recv()
        for cp in copies:
            cp.wait_send()

    return pl.kernel(
        body, name=name,
        out_type=[jax.ShapeDtypeStruct(p.shape, p.dtype) for p in partials],
        mesh=plsc.ScalarSubcoreMesh(axis_name="sequencer", num_cores=1),
        scratch_types=[pltpu.SemaphoreType.DMA((3 * n,)), pltpu.SemaphoreType.DMA((3 * n,))],
        compiler_params=pltpu.CompilerParams(collective_id=collective_id),
    )(*partials)


def _cross_chips_and_gather(partials, slab, *, name, collective_id):
    n = len(partials)

    def body(*refs):
        part_refs, slab_ref = refs[:n], refs[n]
        landed_refs, slabs_ref = refs[n + 1:2 * n + 1], refs[2 * n + 1]
        send_sems, recv_sems, local_sem = refs[2 * n + 2:]
        x, y, c = _position()
        me, my_chip = 4 * x + 2 * y + c, 2 * x + y
        others = [me ^ k for k in range(1, N_DEV)]
        ids = [(o // 4, (o // 2) % 2, o % 2) for o in others]

        barrier = pltpu.get_barrier_semaphore()
        for peer in ids:
            pl.semaphore_signal(barrier, inc=1, device_id=peer, device_id_type=MESH)
        pl.semaphore_wait(barrier, N_DEV - 1)

        mine = pltpu.make_async_copy(slab_ref, slabs_ref.at[me], local_sem)
        mine.start()
        sends = [
            pltpu.make_async_remote_copy(
                src_ref=part_refs[a].at[j], dst_ref=landed_refs[a].at[j],
                send_sem=send_sems.at[3 * a + j], recv_sem=recv_sems.at[3 * a + j],
                device_id=((my_chip ^ (j + 1)) // 2, (my_chip ^ (j + 1)) % 2, c), device_id_type=MESH)
            for a in range(n) for j in range(3)]
        sends += [
            pltpu.make_async_remote_copy(
                src_ref=slab_ref, dst_ref=slabs_ref.at[me],
                send_sem=send_sems.at[3 * n + k], recv_sem=recv_sems.at[3 * n + k],
                device_id=ids[k], device_id_type=MESH)
            for k in range(N_DEV - 1)]
        arrivals = sends[:3 * n] + [
            pltpu.make_async_remote_copy(
                src_ref=slab_ref, dst_ref=slabs_ref.at[others[k]],
                send_sem=send_sems.at[3 * n + k], recv_sem=recv_sems.at[3 * n + k],
                device_id=ids[k], device_id_type=MESH)
            for k in range(N_DEV - 1)]
        for cp in sends:
            cp.start()
        for cp in arrivals:
            cp.wait_recv()
        for cp in sends:
            cp.wait_send()
        mine.wait()

    n_sems = 3 * n + N_DEV - 1
    outs = pl.kernel(
        body, name=name,
        out_type=[jax.ShapeDtypeStruct(p.shape, p.dtype) for p in partials]
                 + [jax.ShapeDtypeStruct((N_DEV,) + slab.shape, slab.dtype)],
        mesh=plsc.ScalarSubcoreMesh(axis_name="sequencer", num_cores=1),
        scratch_types=[pltpu.SemaphoreType.DMA((n_sems,)), pltpu.SemaphoreType.DMA((n_sems,)), pltpu.SemaphoreType.DMA],
        compiler_params=pltpu.CompilerParams(collective_id=collective_id),
    )(*partials, slab)
    return outs[:n], outs[n]


def _sum_devices(gathered, after, *, name):
    _, R, C = gathered.shape

    def body(in_ref, after_ref, out_ref):
        total = in_ref[0]
        for d in range(1, N_DEV):
            total = total + in_ref[d]
        out_ref[...] = total

    return pl.pallas_call(
        body, name=name, grid=(1,),
        in_specs=[pl.BlockSpec((N_DEV, R, C), lambda i: (0, 0, 0)), AFTER],
        out_specs=pl.BlockSpec((R, C), lambda i: (0, 0)),
        out_shape=jax.ShapeDtypeStruct((R, C), F32),
        compiler_params=_params(("arbitrary",)),
    )(gathered, _in_hbm(after))


def _owner_sum_adamw(own, landed, w, m, v, after, *, transposed, name, group=None, into=()):
    H, R, C = w.shape
    tr = R // 2
    first_group = 0 if group is None else group

    def body(own_ref, landed_ref, w_ref, m_ref, v_ref, after_ref, *rest):
        g_ref, d_ref, nm_ref, nv_ref = rest[len(into):]
        total = own_ref[...]
        for j in range(3):
            total = total + landed_ref[j].astype(F32)
        if transposed:
            total = total.T
        g_ref[...] = total
        d_ref[...], nm_ref[...], nv_ref[...] = _adamw_update(w_ref[...], total, m_ref[...], v_ref[...])

    spec = pl.BlockSpec((None, tr, C), lambda h, i: (first_group + h, i, 0))
    if transposed:
        own_spec = pl.BlockSpec((None, C, tr), lambda h, i: (h, 0, i))
        landed_spec = pl.BlockSpec((3, None, C, tr), lambda h, i: (0, h, 0, i))
    else:
        own_spec = pl.BlockSpec((None, tr, C), lambda h, i: (h, i, 0))
        landed_spec = pl.BlockSpec((3, None, tr, C), lambda h, i: (0, h, i, 0))
    n_in = 6
    return pl.pallas_call(
        body, name=name, grid=(own.shape[0], R // tr),
        in_specs=[own_spec, landed_spec, spec, spec, spec, AFTER] + [pl.BlockSpec(memory_space=pl.ANY)] * len(into),
        out_specs=[spec] * 4,
        out_shape=[jax.ShapeDtypeStruct((H, R, C), F32)] * 4,
        input_output_aliases={n_in + j: j for j in range(len(into))},
        compiler_params=_params(("arbitrary", "arbitrary")),
    )(own, landed, w, m, v, _in_hbm(after), *into)


def _local_step(x, target, norms, pool_w_group, pool_scale, wgu1, wd1, w_in, wbp, wba, w_out, wgu2, wd2, exchange):
    n1g, nmg, n2g, nfg = norms
    D = x.shape[1]
    gu1, hid1 = _ffn_up(x, n1g, wgu1, tm=1024, name="ffn1_up")
    h1 = _ffn_down(x, hid1, wd1, tm=512, name="ffn1_down")
    un, proj = _inproj_fwd(h1, nmg, w_in, tm=1024, name="inproj_fwd")
    p = _pool_fwd(proj, pool_w_group, pool_scale, name="pool_fwd")
    o, ltot = _attn_fwd(proj, name="attn_fwd")
    h2, m = _mix_fwd(h1, p, o, proj, wbp, wba, w_out, tm=512, name="mix_fwd")
    gu2, hid2 = _ffn_up(h2, n2g, wgu2, tm=1024, name="ffn2_up")
    h3 = _ffn_down(h2, hid2, wd2, tm=512, name="ffn2_down")
    dh3, df2, loss, d_nf = _loss_bwd(h3, target, nfg, tm=512, name="loss_bwd")

    dh2, d_n2, n2, dgu2 = _ffn_bwd(dh3, df2, h2, n2g, gu2, wgu2, wd2, df2, tm=512, name="ffn2_bwd")
    d_wd2 = _wgrad_down(hid2, df2, tk=WGRAD_TOKENS, name="ffn2_wgrad_down")
    d_wgu2 = _wgrad_gate_up(n2, dgu2, tk=WGRAD_TOKENS, name="ffn2_wgrad_gate_up")
    (g_wd2, g_wgu2), token = exchange("ffn2", [d_wd2.reshape(N_DEV, FF_SHARD_PAD, D), d_wgu2])

    dyp, dys, dp, do, dgl = _mix_bwd(dh2, p, o, proj, wbp, wba, w_out, token, tm=512, name="mix_bwd")
    d_wout = _wgrad_full(m, dh2, tk=WGRAD_TOKENS, name="wgrad_out")
    d_wbp = _wgrad_full(dyp, p, tk=WGRAD_TOKENS, name="wgrad_branch_pool")
    d_wba = _wgrad_full(dys, o, tk=WGRAD_TOKENS, name="wgrad_branch_attn")
    by_owner = lambda g: g.reshape(N_DEV, g.shape[0] // N_DEV, g.shape[1])
    (g_wbp, g_wba, g_wout), token = exchange("mix", [by_owner(d_wbp), by_owner(d_wba), by_owner(d_wout)])
    dxp, d_wgroup, d_scale = _pool_bwd(dp, proj, pool_w_group, pool_scale, name="pool_bwd")
    dq, dk, dv = _attn_bwd(proj, do, ltot, token, name="attn_bwd")
    dproj_parts = [dxp, dq, dk, dv, dgl]
    dh1, df1, d_nm = _inproj_bwd(dproj_parts, dh2, h1, nmg, w_in, tm=512, name="inproj_bwd")
    d_win = _wgrad_in(dproj_parts, un, name="wgrad_in")
    d_wd1 = _wgrad_down(hid1, df1, tk=WGRAD_TOKENS, name="ffn1_wgrad_down")
    (g_win, g_wd1, replicated_early), token = exchange(
        "w_in_ffn1_down", [d_win, d_wd1.reshape(N_DEV, FF_SHARD_PAD, D), d_nm, d_n2, d_nf, d_scale, d_wgroup, loss])

    dx, d_n1, n1, dgu1 = _ffn_bwd(dh1, df1, x, n1g, gu1, wgu1, wd1, token, tm=512, name="ffn1_bwd")
    d_wgu1_a = _wgrad_gate_up(n1, dgu1, tk=WGRAD_TOKENS, name="ffn1_wgrad_gate_up_a", part=0, parts=2)
    (g_wgu1_a, replicated_late), token = exchange("ffn1_gate_up_a", [d_wgu1_a, d_n1])
    d_wgu1_b = _wgrad_gate_up(n1, dgu1, tk=WGRAD_TOKENS, name="ffn1_wgrad_gate_up_b", part=1, parts=2)
    (g_wgu1_b,), token = exchange("last", [d_wgu1_b])
    g_wgu1 = (g_wgu1_a, g_wgu1_b)

    sharded = (g_wgu1, g_wd1, g_win, g_wbp, g_wba, g_wout, g_wgu2, g_wd2)
    return dx, sharded, (replicated_late, replicated_early), token


def _hidden_major(w):
    return jnp.swapaxes(w[0], 0, 1)


def _pad_gate_up(wt):
    d = wt.shape[1]
    wt = wt.astype(BF16).reshape(2, FF_SHARD, d)
    return jnp.pad(wt, ((0, 0), (0, FF_SHARD_PAD - FF_SHARD), (0, 0))).reshape(2 * FF_SHARD_PAD, d)


def _unpad_gate_up(gt):
    d = gt.shape[1]
    return gt.reshape(2, FF_SHARD_PAD, d)[:, :FF_SHARD].reshape(2 * FF_SHARD, d)


def _pad_down(w):
    return jnp.pad(w.astype(BF16), ((0, FF_SHARD_PAD - FF_SHARD), (0, 0)))


def kernel(x, ffn1_norm, ffn1_w_gate_up, ffn1_w_down, mix_norm, w_in, pool_w_group, pool_scale, w_branch_pool, w_branch_attn, w_out, ffn2_norm, ffn2_w_gate_up, ffn2_w_down, final_norm, loss_target, m_ffn1_norm, m_ffn1_w_gate_up, m_ffn1_w_down, m_mix_norm, m_w_in, m_pool_w_group, m_pool_scale, m_w_branch_pool, m_w_branch_attn, m_w_out, m_ffn2_norm, m_ffn2_w_gate_up, m_ffn2_w_down, m_final_norm, v_ffn1_norm, v_ffn1_w_gate_up, v_ffn1_w_down, v_mix_norm, v_w_in, v_pool_w_group, v_pool_scale, v_w_branch_pool, v_w_branch_attn, v_w_out, v_ffn2_norm, v_ffn2_w_gate_up, v_ffn2_w_down, v_final_norm):
    D = x.shape[-1]
    weights = dict(ffn1_norm=ffn1_norm, ffn1_w_gate_up=ffn1_w_gate_up, ffn1_w_down=ffn1_w_down, mix_norm=mix_norm,
                   w_in=w_in, pool_w_group=pool_w_group, pool_scale=pool_scale, w_branch_pool=w_branch_pool,
                   w_branch_attn=w_branch_attn, w_out=w_out, ffn2_norm=ffn2_norm, ffn2_w_gate_up=ffn2_w_gate_up,
                   ffn2_w_down=ffn2_w_down, final_norm=final_norm)
    first = dict(ffn1_norm=m_ffn1_norm, ffn1_w_gate_up=m_ffn1_w_gate_up, ffn1_w_down=m_ffn1_w_down,
                 mix_norm=m_mix_norm, w_in=m_w_in, pool_w_group=m_pool_w_group, pool_scale=m_pool_scale,
                 w_branch_pool=m_w_branch_pool, w_branch_attn=m_w_branch_attn, w_out=m_w_out,
                 ffn2_norm=m_ffn2_norm, ffn2_w_gate_up=m_ffn2_w_gate_up, ffn2_w_down=m_ffn2_w_down,
                 final_norm=m_final_norm)
    second = dict(ffn1_norm=v_ffn1_norm, ffn1_w_gate_up=v_ffn1_w_gate_up, ffn1_w_down=v_ffn1_w_down,
                  mix_norm=v_mix_norm, w_in=v_w_in, pool_w_group=v_pool_w_group, pool_scale=v_pool_scale,
                  w_branch_pool=v_w_branch_pool, w_branch_attn=v_w_branch_attn, w_out=v_w_out,
                  ffn2_norm=v_ffn2_norm, ffn2_w_gate_up=v_ffn2_w_gate_up, ffn2_w_down=v_ffn2_w_down,
                  final_norm=v_final_norm)
    order = list(weights)

    wgu1, = _all_gather([_pad_gate_up(_hidden_major(ffn1_w_gate_up))], name="all_gather_ffn1_gate_up", collective_id=0)
    wd1, = _all_gather([_pad_down(ffn1_w_down[0])], name="all_gather_ffn1_down", collective_id=10)
    transposed = lambda w: jnp.swapaxes(w[0], 0, 1).astype(BF16)
    win_g, = _all_gather([transposed(w_in)], name="all_gather_w_in", collective_id=1)
    wbp_g, wba_g = _all_gather([transposed(w_branch_pool), transposed(w_branch_attn)],
                               name="all_gather_branches", collective_id=2)
    wout_g, = _all_gather([w_out[0].astype(BF16)], name="all_gather_w_out", collective_id=11)
    wgu2, wd2 = _all_gather([_pad_gate_up(_hidden_major(ffn2_w_gate_up)), _pad_down(ffn2_w_down[0])],
                            name="all_gather_ffn2", collective_id=3)
    whole = lambda g: g.reshape(g.shape[0] * g.shape[1], g.shape[2])
    wd1, wd2, win_g, wbp_g, wba_g, wout_g = (whole(g) for g in (wd1, wd2, win_g, wbp_g, wba_g, wout_g))

    cross_ids = {"ffn2": 4, "mix": 5, "w_in_ffn1_down": 8, "ffn1_gate_up_a": 9, "last": 7}
    small = ["ffn1_norm", "mix_norm", "ffn2_norm", "final_norm", "pool_scale", "pool_w_group"]

    def tile_rows(a):
        a = a.reshape(-1, 128)
        return jnp.pad(a, ((0, -a.shape[0] % 8), (0, 0)))

    def exchange(tag, group):
        grads = [g for g in group if g.dtype == BF16]
        extras = [tile_rows(g) for g in group if g.dtype != BF16]
        sums = _chip_sums(grads, name="chip_sums_" + tag)
        partials = [s[0] for s in sums]
        handles = []
        if extras:
            landed, slabs = _cross_chips_and_gather(partials, jnp.concatenate(extras, axis=0),
                                                    name="cross_chips_" + tag, collective_id=cross_ids[tag])
            handles = [slabs]
        else:
            landed = _cross_chips(partials, name="cross_chips_" + tag, collective_id=cross_ids[tag])
        return [(s[1], l) for s, l in zip(sums, landed)] + handles, sums[-1][1]

    norms = (ffn1_norm, mix_norm, ffn2_norm, final_norm.reshape(1, D))
    dx, sharded, (slabs_late, slabs_early), last = _local_step(
        x[0], loss_target[0], norms, pool_w_group[0], pool_scale, wgu1, wd1, win_g, wbp_g, wba_g, wout_g, wgu2, wd2,
        exchange)
    names = ["ffn1_w_gate_up", "ffn1_w_down", "w_in", "w_branch_pool", "w_branch_attn", "w_out",
             "ffn2_w_gate_up", "ffn2_w_down"]
    handles = dict(zip(names, sharded))
    grads, delta, new_m, new_v = {}, {}, {}, {}
    loss_out = []

    def update_replicated(after):
        rows = [weights[k].size // 128 for k in small]
        padded_rows = [-(-r // 8) * 8 for r in rows]
        starts = [sum(padded_rows[:i]) for i in range(len(rows) + 1)]
        total = jnp.concatenate([_sum_devices(slabs_late, after, name="sum_replicated_late"),
                                 _sum_devices(slabs_early, after, name="sum_replicated_early")], axis=0)
        loss_out.append(total[starts[-1], 0])
        small_w = jnp.concatenate([tile_rows(weights[k]) for k in small], axis=0)
        small_m = jnp.concatenate([tile_rows(first[k]) for k in small], axis=0)
        small_v = jnp.concatenate([tile_rows(second[k]) for k in small], axis=0)
        small_out = _adamw(small_w, total[:starts[-1]], small_m, small_v, name="adamw_replicated")
        for name_, start, n_rows in zip(small, starts, rows):
            shape = weights[name_].shape
            grads[name_] = total[start:start + n_rows].reshape(shape)
            delta[name_], new_m[name_], new_v[name_] = (a[start:start + n_rows].reshape(shape) for a in small_out)
        return small_out[0]

    after = last
    for k in ("ffn2_w_down", "ffn2_w_gate_up", "w_branch_pool", "w_branch_attn", "w_out", "w_in", "ffn1_w_down",
              "ffn1_w_gate_up"):
        hidden_major = k.endswith("w_gate_up")
        view = _hidden_major if hidden_major else (lambda a: a[0])
        back = (lambda a: jnp.swapaxes(a, 0, 1)[None]) if hidden_major else (lambda a: a[None])
        groups = 2 if hidden_major else 1
        by_group = lambda a: a.reshape(a.shape[:-2] + (groups, a.shape[-2] // groups, a.shape[-1]))
        state = [by_group(view(a[k])) for a in (weights, first, second)]
        if isinstance(handles[k][0], tuple):
            (own_a, landed_a), (own_b, landed_b) = handles[k]
            out = _owner_sum_adamw(own_a[None], landed_a[:, None], *state, after, name="adamw_" + k + "_a",
                                   transposed=False, group=0)
            out = _owner_sum_adamw(own_b[None], landed_b[:, None], *state, update_replicated(out[1]),
                                   name="adamw_" + k + "_b", transposed=False, group=1, into=out)
        else:
            own, landed = handles[k]
            out = _owner_sum_adamw(by_group(own), by_group(landed), *state, after, name="adamw_" + k,
                                   transposed=k in ("w_in", "w_branch_pool", "w_branch_attn"))
        after = out[1]
        grads[k], delta[k], new_m[k], new_v[k] = (back(a.reshape(-1, a.shape[-1])) for a in out)

    return (loss_out[0], dx[None], *[grads[k] for k in order], *[delta[k] for k in order],
            *[new_m[k] for k in order], *[new_v[k] for k in order])
```

```python
import jax
import jax.numpy as jnp
from jax import lax
from jax.experimental import pallas as pl
from jax.experimental.pallas import tpu as pltpu
from jax.experimental.pallas import tpu_sc as plsc

F32 = jnp.float32
BF16 = jnp.bfloat16
MESH = pl.DeviceIdType.MESH

RMS_EPS = 1e-6
N_DEV = 8
N_HEADS = 8
HEAD_DIM = 64
HEAD_PAIR = 2 * HEAD_DIM
POOL_WINDOWS = (2, 4, 8, 16)
POOL_GROUP = 128
POOL_WIDTH = 512
SB_WIDTH = 512
FF_SHARD = 352
FF_SHARD_PAD = 384
ATTN_K_BLOCK = 256
ATTN_Q_BLOCK_FWD = 512
ATTN_Q_BLOCK_BWD = 256
ATTN_SCALE = 0.125

ADAM_LR = 0.001
ADAM_B1 = 0.9
ADAM_B2 = 0.999
ADAM_EPS = 1e-08
ADAM_WD = 0.01
ADAM_STEP = 10

VMEM_LIMIT = 48 << 20
WGRAD_TOKENS = 2048


def _params(dims=None):
    return pltpu.CompilerParams(dimension_semantics=dims, vmem_limit_bytes=VMEM_LIMIT)


def _mm(a, b):
    return jnp.dot(a, b, preferred_element_type=F32)


def _mm_nt(a, b):
    return lax.dot_general(a, b, (((1,), (1,)), ((), ())), preferred_element_type=F32)


def _mm_tn(a, b):
    return lax.dot_general(a, b, (((0,), (0,)), ((), ())), preferred_element_type=F32)


def _row_tile(rows, cols):
    limit = max(8, (512 * 1024) // cols)
    return max(t for t in range(8, rows + 1, 8) if rows % t == 0 and (t <= limit or t == 8))


def _rstd(xf):
    return lax.rsqrt(jnp.mean(xf * xf, axis=-1, keepdims=True) + RMS_EPS)


def _rms_bwd(xf, gain, dn):
    r = _rstd(xf)
    xh = xf * r
    dgain = jnp.sum(dn * xh, axis=0, keepdims=True)
    dxh = dn * gain
    dx = r * (dxh - xh * jnp.mean(dxh * xh, axis=-1, keepdims=True))
    return dx, dgain


def _ffn_up(x, gain, wgu, *, tm, name):
    T, D = x.shape
    tm = min(tm, T)
    nb, bw = wgu.shape[0] // 2, wgu.shape[1]

    def body(x_ref, gain_ref, wg_ref, wu_ref, gu_ref, hid_ref, n_scr):
        @pl.when(pl.program_id(1) == 0)
        def _():
            xf = x_ref[...]
            n_scr[...] = (xf * _rstd(xf) * gain_ref[...]).astype(BF16)

        halves = (pl.ds(0, tm // 2), pl.ds(tm // 2, tm // 2))
        wg, wu = wg_ref[...], wu_ref[...]
        gus = [(_mm_nt(n_scr[rows, :], wg), _mm_nt(n_scr[rows, :], wu)) for rows in halves]
        for rows, (g, u) in zip(halves, gus):
            gu_ref[0, rows, :] = g.astype(BF16)
            gu_ref[1, rows, :] = u.astype(BF16)
            hid_ref[rows, :] = (g * jax.nn.sigmoid(g) * u).astype(BF16)

    return pl.pallas_call(
        body, name=name, grid=(T // tm, nb),
        in_specs=[
            pl.BlockSpec((tm, D), lambda i, j: (i, 0)),
            pl.BlockSpec((1, D), lambda i, j: (0, 0)),
            pl.BlockSpec((None, bw, D), lambda i, j: (j, 0, 0)),
            pl.BlockSpec((None, bw, D), lambda i, j: (j + nb, 0, 0)),
        ],
        out_specs=[
            pl.BlockSpec((2, tm, bw), lambda i, j: (0, i, j)),
            pl.BlockSpec((tm, bw), lambda i, j: (i, j)),
        ],
        out_shape=[jax.ShapeDtypeStruct((2, T, nb * bw), BF16), jax.ShapeDtypeStruct((T, nb * bw), BF16)],
        scratch_shapes=[pltpu.VMEM((tm, D), BF16)],
        compiler_params=_params(("arbitrary", "arbitrary")),
    )(x, gain, wgu, wgu)


def _ffn_down(x, hid, wd, *, tm, name):
    T, D = x.shape
    tm = min(tm, T)
    F = hid.shape[1]

    def body(x_ref, hid_ref, wd_ref, h_ref):
        h_ref[...] = x_ref[...] + 0.5 * _mm(hid_ref[...], wd_ref[...])

    return pl.pallas_call(
        body, name=name, grid=(T // tm,),
        in_specs=[
            pl.BlockSpec((tm, D), lambda i: (i, 0)),
            pl.BlockSpec((tm, F), lambda i: (i, 0)),
            pl.BlockSpec((F, D), lambda i: (0, 0)),
        ],
        out_specs=pl.BlockSpec((tm, D), lambda i: (i, 0)),
        out_shape=jax.ShapeDtypeStruct((T, D), F32),
        compiler_params=_params(("arbitrary",)),
    )(x, hid, wd)


AFTER = pl.BlockSpec(memory_space=pltpu.HBM)


def _in_hbm(token):
    return pltpu.with_memory_space_constraint(token, pltpu.HBM)


def _ffn_bwd(dh, df, x, gain, gu, wgu, wd, after, *, tm, name):
    T, D = x.shape
    tm = min(tm, T)
    nb, bw = wgu.shape[0] // 2, wgu.shape[1]

    def body(dh_ref, df_ref, x_ref, gain_ref, gu_ref, wg_ref, wu_ref, wd_ref, after_ref,
             dx_ref, dgain_ref, n_ref, dgu_ref, dn_acc):
        i, j = pl.program_id(0), pl.program_id(1)

        @pl.when(j == 0)
        def _():
            xf = x_ref[...]
            n_ref[...] = (xf * _rstd(xf) * gain_ref[...]).astype(BF16)
            dn_acc[...] = jnp.zeros_like(dn_acc)

        @pl.when((i == 0) & (j == 0))
        def _():
            dgain_ref[...] = jnp.zeros_like(dgain_ref)

        halves = (pl.ds(0, tm // 2), pl.ds(tm // 2, tm // 2))
        wd, wg, wu = wd_ref[...], wg_ref[...], wu_ref[...]
        dhids = [_mm_nt(df_ref[rows, :], wd) for rows in halves]
        for rows, dhid in zip(halves, dhids):
            g = gu_ref[0, rows, :].astype(F32)
            u = gu_ref[1, rows, :].astype(F32)
            s = jax.nn.sigmoid(g)
            silu = g * s
            dg = (dhid * u * (s * (1.0 + g * (1.0 - s)))).astype(BF16)
            du = (dhid * silu).astype(BF16)
            dgu_ref[0, rows, :] = dg
            dgu_ref[1, rows, :] = du
            dn_acc[rows, :] += _mm(dg, wg) + _mm(du, wu)

        @pl.when(j == nb - 1)
        def _():
            dx, dgain = _rms_bwd(x_ref[...], gain_ref[...], dn_acc[...])
            dx_ref[...] = dh_ref[...] + dx
            dgain_ref[...] += dgain

    row = lambda i, j: (i, 0)
    return pl.pallas_call(
        body, name=name, grid=(T // tm, nb),
        in_specs=[
            pl.BlockSpec((tm, D), row),
            pl.BlockSpec((tm, D), row),
            pl.BlockSpec((tm, D), row),
            pl.BlockSpec((1, D), lambda i, j: (0, 0)),
            pl.BlockSpec((2, tm, bw), lambda i, j: (0, i, j)),
            pl.BlockSpec((None, bw, D), lambda i, j: (j, 0, 0)),
            pl.BlockSpec((None, bw, D), lambda i, j: (j + nb, 0, 0)),
            pl.BlockSpec((bw, D), lambda i, j: (j, 0)),
            AFTER,
        ],
        out_specs=[
            pl.BlockSpec((tm, D), row),
            pl.BlockSpec((1, D), lambda i, j: (0, 0)),
            pl.BlockSpec((tm, D), row),
            pl.BlockSpec((2, tm, bw), lambda i, j: (0, i, j)),
        ],
        out_shape=[
            jax.ShapeDtypeStruct((T, D), F32),
            jax.ShapeDtypeStruct((1, D), F32),
            jax.ShapeDtypeStruct((T, D), BF16),
            jax.ShapeDtypeStruct((2, T, nb * bw), BF16),
        ],
        scratch_shapes=[pltpu.VMEM((tm, D), F32)],
        compiler_params=_params(("arbitrary", "arbitrary")),
    )(dh, df, x, gain, gu, wgu, wgu, wd, _in_hbm(after))


def _wgrad(a, b, *, grid, a_spec, b_spec, out_spec, out_shape, acc_shape, name):
    nk = grid[2]

    def body(a_ref, b_ref, o_ref, acc):
        k = pl.program_id(2)

        @pl.when(k == 0)
        def _():
            acc[...] = jnp.zeros_like(acc)

        acc[...] += _mm_tn(a_ref[...].astype(BF16), b_ref[...].astype(BF16))

        @pl.when(k == nk - 1)
        def _():
            o_ref[...] = acc[...].astype(o_ref.dtype)

    return pl.pallas_call(
        body, name=name, grid=grid, in_specs=[a_spec, b_spec], out_specs=out_spec,
        out_shape=jax.ShapeDtypeStruct(out_shape, BF16),
        scratch_shapes=[pltpu.VMEM(acc_shape, F32)],
        compiler_params=_params(("arbitrary", "arbitrary", "arbitrary")),
    )(a, b)


def _wgrad_gate_up(n, dgu, *, tk, name, part=0, parts=1):
    T, D = n.shape
    tk = min(tk, T)
    owner_rows = FF_SHARD_PAD * 2
    nb = dgu.shape[2] // owner_rows
    bw = owner_rows // parts
    return _wgrad(
        dgu, n, grid=(2 * nb, 1, T // tk), name=name,
        a_spec=pl.BlockSpec((None, tk, bw), lambda m, c, k: (m // nb, k, parts * (m % nb) + part)),
        b_spec=pl.BlockSpec((tk, D), lambda m, c, k: (k, 0)),
        out_spec=pl.BlockSpec((None, bw, D), lambda m, c, k: (m, 0, 0)),
        out_shape=(2 * nb, bw, D), acc_shape=(bw, D))


def _wgrad_down(hid, df, *, tk, name):
    T, D = df.shape
    tk = min(tk, T)
    bw = FF_SHARD_PAD * 2
    nb = hid.shape[1] // bw
    return _wgrad(
        hid, df, grid=(nb, 1, T // tk), name=name,
        a_spec=pl.BlockSpec((tk, bw), lambda m, c, k: (k, m)),
        b_spec=pl.BlockSpec((tk, D), lambda m, c, k: (k, 0)),
        out_spec=pl.BlockSpec((bw, D), lambda m, c, k: (m, 0)),
        out_shape=(nb * bw, D), acc_shape=(bw, D))


def _wgrad_in(dparts, un, *, name):
    T, D = un.shape
    bw = sum(p.shape[1] for p in dparts) // N_DEV
    first = [sum(p.shape[1] for p in dparts[:i]) // bw for i in range(len(dparts) + 1)]

    def body(*refs):
        dp_refs, un_ref, o_ref = refs[:-2], refs[-2], refs[-1]
        m = pl.program_id(0)
        for dp_ref, lo, hi in zip(dp_refs, first[:-1], first[1:]):
            @pl.when((m >= lo) & (m < hi))
            def _():
                o_ref[...] = _mm_tn(dp_ref[...], un_ref[...]).astype(o_ref.dtype)

    def piece_spec(lo, hi):
        return pl.BlockSpec((T, bw), lambda m: (0, jnp.clip(m - lo, 0, hi - lo - 1)))

    return pl.pallas_call(
        body, name=name, grid=(N_DEV,),
        in_specs=[piece_spec(lo, hi) for lo, hi in zip(first[:-1], first[1:])] + [pl.BlockSpec((T, D), lambda m: (0, 0))],
        out_specs=pl.BlockSpec((None, bw, D), lambda m: (m, 0, 0)),
        out_shape=jax.ShapeDtypeStruct((N_DEV, bw, D), BF16),
        compiler_params=_params(("arbitrary",)),
    )(*dparts, un)


def _wgrad_full(a, b, *, tk, name):
    T, M = a.shape
    tk = min(tk, T)
    N = b.shape[1]
    return _wgrad(
        a, b, grid=(1, 1, T // tk), name=name,
        a_spec=pl.BlockSpec((tk, M), lambda m, c, k: (k, 0)),
        b_spec=pl.BlockSpec((tk, N), lambda m, c, k: (k, 0)),
        out_spec=pl.BlockSpec((M, N), lambda m, c, k: (0, 0)), out_shape=(M, N), acc_shape=(M, N))


def _loss_bwd(h, target, gain, *, tm, name):
    T, D = h.shape
    tm = min(tm, T)

    def body(h_ref, t_ref, gain_ref, dh_ref, df_ref, loss_ref, dgain_ref):
        @pl.when(pl.program_id(0) == 0)
        def _():
            loss_ref[...] = jnp.zeros_like(loss_ref)
            dgain_ref[...] = jnp.zeros_like(dgain_ref)

        xf = h_ref[...]
        gain = gain_ref[...]
        err = xf * _rstd(xf) * gain - t_ref[...]
        loss_ref[...] += 0.5 * jnp.sum(jnp.mean(err * err, axis=-1, keepdims=True), axis=0, keepdims=True)
        dx, dgain = _rms_bwd(xf, gain, err * (1.0 / D))
        dh_ref[...] = dx
        df_ref[...] = (0.5 * dx).astype(BF16)
        dgain_ref[...] += dgain

    row = lambda i: (i, 0)
    fixed = lambda i: (0, 0)
    return pl.pallas_call(
        body, name=name, grid=(T // tm,),
        in_specs=[pl.BlockSpec((tm, D), row), pl.BlockSpec((tm, D), row), pl.BlockSpec((1, D), fixed)],
        out_specs=[pl.BlockSpec((tm, D), row), pl.BlockSpec((tm, D), row), pl.BlockSpec((1, 128), fixed),
                   pl.BlockSpec((1, D), fixed)],
        out_shape=[jax.ShapeDtypeStruct((T, D), F32), jax.ShapeDtypeStruct((T, D), BF16),
                   jax.ShapeDtypeStruct((1, 128), F32), jax.ShapeDtypeStruct((1, D), F32)],
        compiler_params=_params(("arbitrary",)),
    )(h, target, gain)


def _inproj_fwd(h, gain, w_in_t, *, tm, name):
    T, D = h.shape
    tm = min(tm, T)
    bn = D
    nb = w_in_t.shape[0] // bn

    def body(h_ref, gain_ref, wt_ref, un_ref, proj_ref):
        @pl.when(pl.program_id(1) == 0)
        def _():
            xf = h_ref[...]
            un_ref[...] = (xf * _rstd(xf) * gain_ref[...]).astype(BF16)

        proj_ref[...] = _mm_nt(un_ref[...], wt_ref[...])

    return pl.pallas_call(
        body, name=name, grid=(T // tm, nb),
        in_specs=[
            pl.BlockSpec((tm, D), lambda i, j: (i, 0)),
            pl.BlockSpec((1, D), lambda i, j: (0, 0)),
            pl.BlockSpec((bn, D), lambda i, j: (j, 0)),
        ],
        out_specs=[pl.BlockSpec((tm, D), lambda i, j: (i, 0)), pl.BlockSpec((tm, bn), lambda i, j: (i, j))],
        out_shape=[jax.ShapeDtypeStruct((T, D), BF16), jax.ShapeDtypeStruct((T, nb * bn), F32)],
        compiler_params=_params(("arbitrary", "arbitrary")),
    )(h, gain, w_in_t)


def _inproj_bwd(dparts, dh, h, gain, w_in_t, *, tm, name):
    T, D = h.shape
    tm = min(tm, T)
    n = len(dparts)
    widths = [p.shape[1] for p in dparts]
    starts = [sum(widths[:i]) for i in range(n)]

    def body(*refs):
        dp_refs = refs[:n]
        dh_ref, h_ref, gain_ref, wt_ref, dx_ref, df_ref, dgain_ref = refs[n:]

        @pl.when(pl.program_id(0) == 0)
        def _():
            dgain_ref[...] = jnp.zeros_like(dgain_ref)

        dn = sum(_mm(dp_ref[...], wt_ref[start:start + width, :])
                 for dp_ref, start, width in zip(dp_refs, starts, widths))
        dx, dgain = _rms_bwd(h_ref[...], gain_ref[...], dn)
        dh_in = dh_ref[...] + dx
        dx_ref[...] = dh_in
        df_ref[...] = (0.5 * dh_in).astype(BF16)
        dgain_ref[...] += dgain

    row = lambda i: (i, 0)
    fixed = lambda i: (0, 0)
    return pl.pallas_call(
        body, name=name, grid=(T // tm,),
        in_specs=[pl.BlockSpec((tm, width), row) for width in widths] + [
            pl.BlockSpec((tm, D), row),
            pl.BlockSpec((tm, D), row),
            pl.BlockSpec((1, D), fixed),
            pl.BlockSpec(w_in_t.shape, fixed),
        ],
        out_specs=[pl.BlockSpec((tm, D), row), pl.BlockSpec((tm, D), row), pl.BlockSpec((1, D), fixed)],
        out_shape=[jax.ShapeDtypeStruct((T, D), F32), jax.ShapeDtypeStruct((T, D), BF16),
                   jax.ShapeDtypeStruct((1, D), F32)],
        compiler_params=_params(("arbitrary",)),
    )(*dparts, dh, h, gain, w_in_t)


def _window_sum(x, row, doublings, *, backward):
    T = x.shape[0]
    s = x
    for k in range(doublings):
        sh = 1 << k
        if backward:
            s = s + jnp.where(row < T - sh, pltpu.roll(s, T - sh, 0), 0.0)
        else:
            s = s + jnp.where(row >= sh, pltpu.roll(s, sh, 0), 0.0)
    return s


def _pool_fwd(proj, w_group, scale, *, name):
    T = proj.shape[0]

    def body(xp_ref, w_ref, scale_ref, p_ref):
        row = lax.broadcasted_iota(jnp.int32, (T, POOL_GROUP), 0)
        for gi, window in enumerate(POOL_WINDOWS):
            cols = slice(gi * POOL_GROUP, (gi + 1) * POOL_GROUP)
            x = xp_ref[:, cols]
            inv_count = 1.0 / jnp.minimum(row + 1, window).astype(F32)
            yc = _window_sum(x, row, gi + 1, backward=False) * inv_count - x
            pre = _mm(yc.astype(BF16), w_ref[gi].astype(BF16))
            p_ref[:, cols] = pre * scale_ref[:, cols]

    return pl.pallas_call(
        body, name=name, grid=(1,),
        in_specs=[
            pl.BlockSpec((T, POOL_WIDTH), lambda i: (0, 0)),
            pl.BlockSpec(w_group.shape, lambda i: (0, 0, 0)),
            pl.BlockSpec((1, POOL_WIDTH), lambda i: (0, 0)),
        ],
        out_specs=pl.BlockSpec((T, POOL_WIDTH), lambda i: (0, 0)),
        out_shape=jax.ShapeDtypeStruct((T, POOL_WIDTH), F32),
        compiler_params=_params(("arbitrary",)),
    )(proj, w_group, scale)


def _pool_bwd(dp, proj, w_group, scale, *, name):
    T = proj.shape[0]

    def body(dp_ref, xp_ref, w_ref, scale_ref, dxp_ref, dw_ref, dscale_ref):
        row = lax.broadcasted_iota(jnp.int32, (T, POOL_GROUP), 0)
        for gi, window in enumerate(POOL_WINDOWS):
            cols = slice(gi * POOL_GROUP, (gi + 1) * POOL_GROUP)
            x = xp_ref[:, cols]
            inv_count = 1.0 / jnp.minimum(row + 1, window).astype(F32)
            yc = (_window_sum(x, row, gi + 1, backward=False) * inv_count - x).astype(BF16)
            w = w_ref[gi].astype(BF16)
            pre = _mm(yc, w)
            dpg = dp_ref[:, cols]
            dscale_ref[:, cols] = jnp.sum(dpg * pre, axis=0, keepdims=True)
            dpre = (dpg * scale_ref[:, cols]).astype(BF16)
            dw_ref[gi] = _mm_tn(yc, dpre)
            dyc = _mm_nt(dpre, w)
            dxp_ref[:, cols] = (_window_sum(dyc * inv_count, row, gi + 1, backward=True) - dyc).astype(BF16)

    return pl.pallas_call(
        body, name=name, grid=(1,),
        in_specs=[
            pl.BlockSpec((T, POOL_WIDTH), lambda i: (0, 0)),
            pl.BlockSpec((T, POOL_WIDTH), lambda i: (0, 0)),
            pl.BlockSpec(w_group.shape, lambda i: (0, 0, 0)),
            pl.BlockSpec((1, POOL_WIDTH), lambda i: (0, 0)),
        ],
        out_specs=[
            pl.BlockSpec((T, POOL_WIDTH), lambda i: (0, 0)),
            pl.BlockSpec(w_group.shape, lambda i: (0, 0, 0)),
            pl.BlockSpec((1, POOL_WIDTH), lambda i: (0, 0)),
        ],
        out_shape=[jax.ShapeDtypeStruct((T, POOL_WIDTH), BF16), jax.ShapeDtypeStruct(w_group.shape, F32),
                   jax.ShapeDtypeStruct((1, POOL_WIDTH), F32)],
        compiler_params=_params(("arbitrary",)),
    )(dp, proj, w_group, scale)


ATTN_STRIP = 32


def _log_sigmoids(z):
    lb = jnp.minimum(z, 0.0) - jnp.log(1.0 + jnp.exp(-jnp.abs(z)))
    return lb, lb - z


def _transposed_blocks(x_ref, blocks_scr, tq):
    for b in range(blocks_scr.shape[0]):
        blocks_scr[b] = x_ref[b * tq:(b + 1) * tq, :].T.astype(BF16)


def _split_bf16(x):
    hi = x.astype(BF16)
    return hi, (x - hi.astype(F32)).astype(BF16)


def _strips(n):
    return [slice(i, i + ATTN_STRIP) for i in range(0, n, ATTN_STRIP)]


def _rows(parts):
    return jnp.concatenate(parts, axis=0)


def _attn_specs(T, tq):
    q_col = POOL_WIDTH // HEAD_PAIR
    k_col = q_col + SB_WIDTH // HEAD_PAIR
    v_col = k_col + SB_WIDTH // HEAD_PAIR
    return [
        pl.BlockSpec((tq, HEAD_PAIR), lambda p, i: (i, q_col + p)),
        pl.BlockSpec((T, HEAD_PAIR), lambda p, i: (0, k_col + p)),
        pl.BlockSpec((T, HEAD_PAIR), lambda p, i: (0, v_col + p)),
    ]


def _attn_fwd(proj, *, name):
    T = proj.shape[0]
    tk = min(ATTN_K_BLOCK, T)
    tq = min(ATTN_Q_BLOCK_FWD, T)
    diagonal_blocks = tq // tk

    def body(q_ref, k_ref, v_ref, o_ref, lt_ref, kt_scr, vb_scr):
        qi = pl.program_id(1)

        @pl.when(qi == 0)
        def _():
            _transposed_blocks(k_ref, kt_scr, tk)
            vb_scr[...] = v_ref[...].astype(BF16)

        head0 = lax.broadcasted_iota(jnp.int32, (tq, HEAD_PAIR), 1) < HEAD_DIM
        q = q_ref[...] * ATTN_SCALE
        qs = (jnp.where(head0, q, 0.0).astype(BF16), jnp.where(head0, 0.0, q).astype(BF16))
        r = lax.broadcasted_iota(jnp.int32, (tq, tk), 0)
        c = lax.broadcasted_iota(jnp.int32, (tq, tk), 1)
        later = (r[:tk] > c[:tk]).astype(BF16)
        later2 = _rows([later, later])
        causal = lambda d: (lambda rows: c[rows] + d * tk < r[rows])
        strips = _strips(tq)

        def log_terms(z, valid):
            lbs, his, los, sums = [], [], [], []
            for rows in strips:
                lb, lm = _log_sigmoids(z[rows])
                if valid is not None:
                    lm = jnp.where(valid(rows), lm, 0.0)
                hi, lo = _split_bf16(lm)
                lbs.append(lb)
                his.append(hi)
                los.append(lo)
                sums.append(jnp.sum(lm, axis=1, keepdims=True))
            return lbs, jnp.concatenate([_rows(his), _rows(los)], axis=1), _rows(sums)

        def weights(lbs, run, after, valid):
            parts = []
            for rows, lb in zip(strips, lbs):
                a = jnp.exp(lb + run[rows] + after[rows])
                if valid is not None:
                    a = jnp.where(valid(rows), a, 0.0)
                parts.append(a.astype(BF16))
            return _rows(parts)

        def block(kj, carry, valid):
            kt = kt_scr[kj]
            vb = vb_scr[pl.ds(pl.multiple_of(kj * tk, tk), tk), :]
            run0, o0, run1, o1 = carry
            z0 = _mm(qs[0], kt)
            z1 = _mm(qs[1], kt)
            lbs0, split0, sums0 = log_terms(z0, valid)
            after0 = _mm(split0, later2)
            lbs1, split1, sums1 = log_terms(z1, valid)
            after1 = _mm(split1, later2)
            o0 = o0 + _mm(weights(lbs0, run0, after0, valid), vb)
            o1 = o1 + _mm(weights(lbs1, run1, after1, valid), vb)
            return run0 + sums0, o0, run1 + sums1, o1

        zero = (jnp.zeros((tq, 1), F32), jnp.zeros((tq, HEAD_PAIR), F32))
        first = diagonal_blocks * qi
        carry = zero + zero
        for d in reversed(range(diagonal_blocks)):
            carry = block(first + d, carry, causal(d))
        carry = lax.fori_loop(0, first, lambda it, cr: block(first - 1 - it, cr, None), carry)
        o_ref[...] = jnp.where(head0, carry[1], carry[3])
        lt_ref[...] = jnp.where(head0, carry[0], carry[2])

    out_spec = pl.BlockSpec((tq, HEAD_PAIR), lambda p, i: (i, p))
    return pl.pallas_call(
        body, name=name, grid=(N_HEADS // 2, T // tq),
        in_specs=_attn_specs(T, tq), out_specs=[out_spec, out_spec],
        out_shape=[jax.ShapeDtypeStruct((T, SB_WIDTH), F32), jax.ShapeDtypeStruct((T, SB_WIDTH), F32)],
        scratch_shapes=[pltpu.VMEM((T // tk, HEAD_PAIR, tk), BF16), pltpu.VMEM((T, HEAD_PAIR), BF16)],
        compiler_params=_params(("arbitrary", "arbitrary")),
    )(proj, proj, proj)


def _attn_bwd(proj, do, ltot, after, *, name):
    T = proj.shape[0]
    tk = min(ATTN_K_BLOCK, T)
    tq = min(ATTN_Q_BLOCK_BWD, T)
    diagonal_blocks = tq // tk

    def body(q_ref, k_ref, v_ref, do_ref, lt_ref, after_ref, dq_ref, dk_ref, dv_ref,
             kb_scr, kt_scr, vt_scr, dkt_ref, dvt_ref):
        qi = pl.program_id(1)

        @pl.when(qi == 0)
        def _():
            kb_scr[...] = k_ref[...].astype(BF16)
            _transposed_blocks(k_ref, kt_scr, tk)
            _transposed_blocks(v_ref, vt_scr, tk)
            dkt_ref[...] = jnp.zeros_like(dkt_ref)
            dvt_ref[...] = jnp.zeros_like(dvt_ref)

        head0 = lax.broadcasted_iota(jnp.int32, (tq, HEAD_PAIR), 1) < HEAD_DIM
        q, do_, lt = q_ref[...] * ATTN_SCALE, do_ref[...], lt_ref[...]
        qs = (jnp.where(head0, q, 0.0).astype(BF16), jnp.where(head0, 0.0, q).astype(BF16))
        q_heads = (jnp.where(head0, q, 0.0), jnp.where(head0, 0.0, q))
        do_heads = (jnp.where(head0, do_, 0.0), jnp.where(head0, 0.0, do_))
        dos = tuple(d.astype(BF16) for d in do_heads)
        qts = tuple(x.T.astype(BF16) for x in q_heads)
        dots = tuple(d.T.astype(BF16) for d in do_heads)
        lts = (jnp.max(jnp.where(head0, lt, -jnp.inf), axis=1, keepdims=True),
               jnp.max(jnp.where(head0, -jnp.inf, lt), axis=1, keepdims=True))
        r = lax.broadcasted_iota(jnp.int32, (tq, tk), 0)
        c = lax.broadcasted_iota(jnp.int32, (tq, tk), 1)
        upto = (r[:tk] <= c[:tk]).astype(BF16)
        before = (r[:tk] < c[:tk]).astype(BF16)
        upto2, before2 = _rows([upto, upto]), _rows([before, before])
        causal = lambda d: (lambda rows: c[rows] + d * tk < r[rows])
        strips = _strips(tq)

        def log_terms(z, valid):
            lbs, his, los, sums = [], [], [], []
            for rows in strips:
                lb, lm = _log_sigmoids(z[rows])
                if valid is not None:
                    lm = jnp.where(valid(rows), lm, 0.0)
                hi, lo = _split_bf16(lm)
                lbs.append(lb)
                his.append(hi)
                los.append(lo)
                sums.append(jnp.sum(lm, axis=1, keepdims=True))
            return lbs, jnp.concatenate([_rows(his), _rows(los)], axis=1), _rows(sums)

        def weights(lbs, rest, lm_upto, da, valid):
            a_parts, es, his, los, sums = [], [], [], [], []
            for rows, lb in zip(strips, lbs):
                a = jnp.exp(lb + (rest[rows] - lm_upto[rows]))
                if valid is not None:
                    a = jnp.where(valid(rows), a, 0.0)
                e = da[rows] * a
                hi, lo = _split_bf16(e)
                a_parts.append(a.astype(BF16))
                es.append(e)
                his.append(hi)
                los.append(lo)
                sums.append(jnp.sum(e, axis=1, keepdims=True))
            return _rows(a_parts), es, jnp.concatenate([_rows(his), _rows(los)], axis=1), _rows(sums)

        def score_grads(lbs, es, run_e, e_before, valid):
            parts = []
            for rows, lb, e in zip(strips, lbs, es):
                beta = jnp.exp(lb)
                dz = e * (1.0 - beta) - (run_e[rows] + e_before[rows]) * beta
                if valid is not None:
                    dz = jnp.where(valid(rows), dz, 0.0)
                parts.append(dz.astype(BF16))
            return _rows(parts)

        def block(kj, carry, valid):
            off = pl.multiple_of(kj * tk, tk)
            kb, kt, vt = kb_scr[pl.ds(off, tk), :], kt_scr[kj], vt_scr[kj]
            run_lm0, run_e0, dq0, run_lm1, run_e1, dq1 = carry
            z0, da0 = _mm(qs[0], kt), _mm(dos[0], vt)
            z1, da1 = _mm(qs[1], kt), _mm(dos[1], vt)
            lbs0, split0, lm_sums0 = log_terms(z0, valid)
            lm_upto0 = _mm(split0, upto2)
            lbs1, split1, lm_sums1 = log_terms(z1, valid)
            lm_upto1 = _mm(split1, upto2)
            a0, es0, split0, e_sums0 = weights(lbs0, lts[0] - run_lm0, lm_upto0, da0, valid)
            e_before0 = _mm(split0, before2)
            a1, es1, split1, e_sums1 = weights(lbs1, lts[1] - run_lm1, lm_upto1, da1, valid)
            e_before1 = _mm(split1, before2)
            dz0 = score_grads(lbs0, es0, run_e0, e_before0, valid)
            dkt_blk = _mm(qts[0], dz0)
            dvt_blk = _mm(dots[0], a0)
            dq0 = dq0 + _mm(dz0, kb)
            dz1 = score_grads(lbs1, es1, run_e1, e_before1, valid)
            dkt_ref[kj] += dkt_blk + _mm(qts[1], dz1)
            dvt_ref[kj] += dvt_blk + _mm(dots[1], a1)
            dq1 = dq1 + _mm(dz1, kb)
            return run_lm0 + lm_sums0, run_e0 + e_sums0, dq0, run_lm1 + lm_sums1, run_e1 + e_sums1, dq1

        zero = (jnp.zeros((tq, 1), F32), jnp.zeros((tq, 1), F32), jnp.zeros((tq, HEAD_PAIR), F32))
        first = diagonal_blocks * qi
        carry = lax.fori_loop(0, first, lambda kj, cr: block(kj, cr, None), zero + zero)
        for d in range(diagonal_blocks):
            carry = block(first + d, carry, causal(d))
        dq_ref[...] = (jnp.where(head0, carry[2], carry[5]) * ATTN_SCALE).astype(BF16)

        @pl.when(qi == T // tq - 1)
        def _():
            for b in range(T // tk):
                dk_ref[b * tk:(b + 1) * tk, :] = dkt_ref[b].T.astype(BF16)
                dv_ref[b * tk:(b + 1) * tk, :] = dvt_ref[b].T.astype(BF16)

    blk = pl.BlockSpec((tq, HEAD_PAIR), lambda p, i: (i, p))
    seq = pl.BlockSpec((T, HEAD_PAIR), lambda p, i: (0, p))
    transposed = pltpu.VMEM((T // tk, HEAD_PAIR, tk), F32)
    return pl.pallas_call(
        body, name=name, grid=(N_HEADS // 2, T // tq),
        in_specs=_attn_specs(T, tq) + [blk, blk, AFTER], out_specs=[blk, seq, seq],
        out_shape=[jax.ShapeDtypeStruct((T, SB_WIDTH), BF16)] * 3,
        scratch_shapes=[pltpu.VMEM((T, HEAD_PAIR), BF16), pltpu.VMEM((T // tk, HEAD_PAIR, tk), BF16),
                        pltpu.VMEM((T // tk, HEAD_PAIR, tk), BF16), transposed, transposed],
        compiler_params=_params(("arbitrary", "arbitrary")),
    )(proj, proj, proj, do, ltot, _in_hbm(after))


def _mix_specs(T, D, tm, wbp, w_out):
    gate_col = (POOL_WIDTH + 3 * SB_WIDTH) // D
    row = lambda i: (i, 0)
    return [
        pl.BlockSpec((tm, D), row),
        pl.BlockSpec((tm, POOL_WIDTH), row),
        pl.BlockSpec((tm, SB_WIDTH), row),
        pl.BlockSpec((tm, D), lambda i: (i, gate_col)),
        pl.BlockSpec((tm, D), lambda i: (i, gate_col + 1)),
        pl.BlockSpec(wbp.shape, lambda i: (0, 0)),
        pl.BlockSpec(wbp.shape, lambda i: (0, 0)),
        pl.BlockSpec(w_out.shape, lambda i: (0, 0)),
    ]


def _mix_fwd(h, p, o, proj, wbp, wba, w_out, *, tm, name):
    T, D = h.shape
    tm = min(tm, T)

    def body(h_ref, p_ref, o_ref, glp_ref, gls_ref, wbp_ref, wba_ref, wout_ref, hout_ref, m_ref):
        halves = (pl.ds(0, tm // 2), pl.ds(tm // 2, tm // 2))
        wbp, wba, wout = wbp_ref[...], wba_ref[...], wout_ref[...]
        branches = [(_mm_nt(p_ref[rows, :].astype(BF16), wbp), _mm_nt(o_ref[rows, :].astype(BF16), wba))
                    for rows in halves]
        for rows, (yp, ys) in zip(halves, branches):
            m = (jax.nn.sigmoid(glp_ref[rows, :]) * yp + jax.nn.sigmoid(gls_ref[rows, :]) * ys).astype(BF16)
            m_ref[rows, :] = m
            hout_ref[rows, :] = h_ref[rows, :] + _mm(m, wout)

    row = lambda i: (i, 0)
    return pl.pallas_call(
        body, name=name, grid=(T // tm,),
        in_specs=_mix_specs(T, D, tm, wbp, w_out),
        out_specs=[pl.BlockSpec((tm, D), row), pl.BlockSpec((tm, D), row)],
        out_shape=[jax.ShapeDtypeStruct((T, D), F32), jax.ShapeDtypeStruct((T, D), BF16)],
        compiler_params=_params(("arbitrary",)),
    )(h, p, o, proj, proj, wbp, wba, w_out)


def _mix_bwd(dh, p, o, proj, wbp, wba, w_out, after, *, tm, name):
    T, D = dh.shape
    tm = min(tm, T)

    def body(dh_ref, p_ref, o_ref, glp_ref, gls_ref, wbp_ref, wba_ref, wout_ref, after_ref,
             dyp_ref, dys_ref, dp_ref, do_ref, dgl_ref):
        halves = (pl.ds(0, tm // 2), pl.ds(tm // 2, tm // 2))
        wbp, wba, wout = wbp_ref[...], wba_ref[...], wout_ref[...]
        products = [(_mm_nt(dh_ref[rows, :].astype(BF16), wout), _mm_nt(p_ref[rows, :].astype(BF16), wbp),
                     _mm_nt(o_ref[rows, :].astype(BF16), wba)) for rows in halves]
        for rows, (dm, yp, ys) in zip(halves, products):
            gp = jax.nn.sigmoid(glp_ref[rows, :])
            gs = jax.nn.sigmoid(gls_ref[rows, :])
            dyp = (dm * gp).astype(BF16)
            dys = (dm * gs).astype(BF16)
            dyp_ref[rows, :] = dyp
            dys_ref[rows, :] = dys
            dgl_ref[rows, :D] = (dm * yp * gp * (1.0 - gp)).astype(BF16)
            dgl_ref[rows, D:] = (dm * ys * gs * (1.0 - gs)).astype(BF16)
            dp_ref[rows, :] = _mm(dyp, wbp)
            do_ref[rows, :] = _mm(dys, wba)

    row = lambda i: (i, 0)
    return pl.pallas_call(
        body, name=name, grid=(T // tm,),
        in_specs=_mix_specs(T, D, tm, wbp, w_out) + [AFTER],
        out_specs=[pl.BlockSpec((tm, D), row), pl.BlockSpec((tm, D), row), pl.BlockSpec((tm, POOL_WIDTH), row),
                   pl.BlockSpec((tm, SB_WIDTH), row), pl.BlockSpec((tm, 2 * D), row)],
        out_shape=[jax.ShapeDtypeStruct((T, D), BF16), jax.ShapeDtypeStruct((T, D), BF16),
                   jax.ShapeDtypeStruct((T, POOL_WIDTH), F32), jax.ShapeDtypeStruct((T, SB_WIDTH), F32),
                   jax.ShapeDtypeStruct((T, 2 * D), BF16)],
        compiler_params=_params(("arbitrary",)),
    )(dh, p, o, proj, proj, wbp, wba, w_out, _in_hbm(after))


def _adamw_update(w, g, m, v):
    m_ = ADAM_B1 * m + (1.0 - ADAM_B1) * g
    v_ = ADAM_B2 * v + (1.0 - ADAM_B2) * (g * g)
    m_hat = m_ / (1.0 - ADAM_B1 ** ADAM_STEP)
    v_hat = v_ / (1.0 - ADAM_B2 ** ADAM_STEP)
    return -ADAM_LR * (m_hat / (jnp.sqrt(v_hat) + ADAM_EPS) + ADAM_WD * w), m_, v_


def _adamw(w, g, m, v, *, name):
    R, C = w.shape
    tr = _row_tile(R, C)

    def body(w_ref, g_ref, m_ref, v_ref, d_ref, nm_ref, nv_ref):
        d_ref[...], nm_ref[...], nv_ref[...] = _adamw_update(w_ref[...], g_ref[...], m_ref[...], v_ref[...])

    spec = pl.BlockSpec((tr, C), lambda i: (i, 0))
    return pl.pallas_call(
        body, name=name, grid=(R // tr,), in_specs=[spec] * 4, out_specs=[spec] * 3,
        out_shape=[jax.ShapeDtypeStruct((R, C), F32)] * 3,
        compiler_params=_params(("arbitrary",)),
    )(w, g, m, v)


def _position():
    return lax.axis_index("x"), lax.axis_index("y"), lax.axis_index("c")


def _all_gather(shards, *, name, collective_id):
    n = len(shards)
    n_copies = 9

    def body(*refs):
        ins, outs = refs[:n], refs[n:2 * n]
        send_sems, recv_sems, local_sems = refs[2 * n:]
        x, y, c = _position()
        me, sibling = (x, y, c), (x, y, 1 - c)
        x_nbr, y_nbr, diagonal = (1 - x, y, c), (x, 1 - y, c), (1 - x, 1 - y, c)
        other = lambda pos: (pos[0], pos[1], 1 - c)

        barrier = pltpu.get_barrier_semaphore()
        for peer in (sibling, x_nbr, y_nbr):
            pl.semaphore_signal(barrier, inc=1, device_id=peer, device_id_type=MESH)
        pl.semaphore_wait(barrier, 3)

        def block(a, pos, half=None):
            ref = outs[a].at[4 * pos[0] + 2 * pos[1] + pos[2]]
            rows = ref.shape[0] // 2
            return ref if half is None else ref.at[pl.ds(half * rows, rows)]

        def copy(a, k, pos, to, half=None, src=None):
            return pltpu.make_async_remote_copy(
                src_ref=block(a, pos, half) if src is None else src, dst_ref=block(a, pos, half),
                send_sem=send_sems.at[n_copies * a + k], recv_sem=recv_sems.at[n_copies * a + k],
                device_id=to, device_id_type=MESH)

        started = []
        for a in range(n):
            mine = pltpu.make_async_copy(ins[a], block(a, me), local_sems.at[a])
            mine.start()
            started.append(mine)
        sends = []
        for a in range(n):
            sends += [copy(a, 1, me, x_nbr, src=ins[a]), copy(a, 2, me, y_nbr, src=ins[a]),
                      copy(a, 0, me, sibling, src=ins[a])]
        for cp in sends:
            cp.start()

        def pass_on(copies):
            for cp in copies:
                cp.start()
                sends.append(cp)

        for a in range(n):
            copy(a, 1, x_nbr, me).wait_recv()
            pass_on([copy(a, 5, x_nbr, y_nbr, half=0), copy(a, 3, x_nbr, sibling)])
            copy(a, 2, y_nbr, me).wait_recv()
            pass_on([copy(a, 6, y_nbr, x_nbr, half=1), copy(a, 4, y_nbr, sibling)])
        for a in range(n):
            copy(a, 5, diagonal, me, half=0).wait_recv()
            pass_on([copy(a, 7, diagonal, sibling, half=0)])
            copy(a, 6, diagonal, me, half=1).wait_recv()
            pass_on([copy(a, 8, diagonal, sibling, half=1)])
        for a in range(n):
            copy(a, 0, sibling, me).wait_recv()
            copy(a, 3, other(x_nbr), me).wait_recv()
            copy(a, 4, other(y_nbr), me).wait_recv()
            copy(a, 7, other(diagonal), me, half=0).wait_recv()
            copy(a, 8, other(diagonal), me, half=1).wait_recv()
        for cp in sends:
            cp.wait_send()
        for cp in started:
            cp.wait()

    return pl.kernel(
        body, name=name,
        out_type=[jax.ShapeDtypeStruct((N_DEV,) + s.shape, s.dtype) for s in shards],
        mesh=plsc.ScalarSubcoreMesh(axis_name="sequencer", num_cores=1),
        scratch_types=[pltpu.SemaphoreType.DMA((n_copies * n,)), pltpu.SemaphoreType.DMA((n_copies * n,)),
                       pltpu.SemaphoreType.DMA((n,))],
        compiler_params=pltpu.CompilerParams(collective_id=collective_id),
    )(*shards)


def _chip_sums(group, *, name):
    n = len(group)
    shapes = [g.shape[1:] for g in group]

    def body(*refs):
        g_refs, partials_out, own_out = refs[:n], refs[n:3 * n:2], refs[n + 1:3 * n:2]
        mines, theirs = refs[3 * n:7 * n:4], refs[3 * n + 1:7 * n:4]
        partials, out_refs = refs[3 * n + 2:7 * n:4], refs[3 * n + 3:7 * n:4]
        send_sems, recv_sems, local_sems, store_sems = refs[7 * n:]
        x, y, c = _position()
        my_chip = 2 * x + y

        def swap(a, s):
            return pltpu.make_async_remote_copy(
                src_ref=g_refs[a].at[2 * s + (1 - c)], dst_ref=theirs[a].at[s],
                send_sem=send_sems.at[4 * a + s], recv_sem=recv_sems.at[4 * a + s],
                device_id=(x, y, 1 - c), device_id_type=MESH)

        def load(a, s):
            return pltpu.make_async_copy(g_refs[a].at[2 * s + c], mines[a].at[s], local_sems.at[4 * a + s])

        def store(a, s, own):
            if own:
                return pltpu.make_async_copy(out_refs[a], own_out[a], store_sems.at[4 * a + s])
            j = (s ^ my_chip) - 1
            return pltpu.make_async_copy(partials[a].at[j], partials_out[a].at[j], store_sems.at[4 * a + s])

        for a in range(n):
            for s in range(4):
                swap(a, s).start()
                load(a, s).start(priority=s % 2)

        for a, (R, C) in enumerate(shapes):
            rc = 128 if R % 128 == 0 else R

            def chip_sum(chip, rows):
                return mines[a][chip, rows, :].astype(F32) + theirs[a][chip, rows, :].astype(F32)

            for s in range(4):
                load(a, s).wait()
                swap(a, s).wait_recv()

                @pl.when(s == my_chip)
                def _():
                    @pl.loop(0, R // rc)
                    def _(t):
                        rows = pl.ds(pl.multiple_of(t * rc, rc), rc)
                        out_refs[a][rows, :] = chip_sum(s, rows)
                    store(a, s, True).start()

                @pl.when(s != my_chip)
                def _():
                    @pl.loop(0, R // rc)
                    def _(t):
                        rows = pl.ds(pl.multiple_of(t * rc, rc), rc)
                        partials[a][(s ^ my_chip) - 1, rows, :] = chip_sum(s, rows).astype(BF16)
                    store(a, s, False).start()

        for a in range(n):
            for s in range(4):
                swap(a, s).wait_send()
                pl.when(s == my_chip)(store(a, s, True).wait)
                pl.when(s != my_chip)(store(a, s, False).wait)

    hbm = pl.BlockSpec(memory_space=pl.ANY)
    outs = pl.pallas_call(
        body, name=name,
        in_specs=[hbm] * n, out_specs=[hbm] * (2 * n),
        out_shape=[shape for R, C in shapes
                   for shape in (jax.ShapeDtypeStruct((3, R, C), BF16), jax.ShapeDtypeStruct((R, C), F32))],
        scratch_shapes=[scratch for R, C in shapes for scratch in (
            pltpu.VMEM((4, R, C), BF16), pltpu.VMEM((4, R, C), BF16), pltpu.VMEM((3, R, C), BF16),
            pltpu.VMEM((R, C), F32))] + [pltpu.SemaphoreType.DMA((4 * n,))] * 4,
        compiler_params=_params(),
    )(*group)
    return [(outs[2 * a], outs[2 * a + 1]) for a in range(n)]


def _cross_chips(partials, *, name, collective_id):
    n = len(partials)

    def body(*refs):
        ins, outs = refs[:n], refs[n:2 * n]
        send_sems, recv_sems = refs[2 * n:]
        x, y, c = _position()
        my_chip = 2 * x + y
        peers = [((my_chip ^ j) // 2, (my_chip ^ j) % 2, c) for j in (1, 2, 3)]

        barrier = pltpu.get_barrier_semaphore()
        for peer in peers:
            pl.semaphore_signal(barrier, inc=1, device_id=peer, device_id_type=MESH)
        pl.semaphore_wait(barrier, 3)

        copies = [
            pltpu.make_async_remote_copy(
                src_ref=ins[a].at[j], dst_ref=outs[a].at[j],
                send_sem=send_sems.at[3 * a + j], recv_sem=recv_sems.at[3 * a + j],
                device_id=peers[j], device_id_type=MESH)
            for a in range(n) for j in range(3)]
        for cp in copies:
            cp.start()
        for cp in copies:
            cp.wait_recv()
        for cp in copies:
            cp.wait_send()

    return pl.kernel(
        body, name=name,
        out_type=[jax.ShapeDtypeStruct(p.shape, p.dtype) for p in partials],
        mesh=plsc.ScalarSubcoreMesh(axis_name="sequencer", num_cores=1),
        scratch_types=[pltpu.SemaphoreType.DMA((3 * n,)), pltpu.SemaphoreType.DMA((3 * n,))],
        compiler_params=pltpu.CompilerParams(collective_id=collective_id),
    )(*partials)


def _cross_chips_and_gather(partials, slab, *, name, collective_id):
    n = len(partials)

    def body(*refs):
        part_refs, slab_ref = refs[:n], refs[n]
        landed_refs, slabs_ref = refs[n + 1:2 * n + 1], refs[2 * n + 1]
        send_sems, recv_sems, local_sem = refs[2 * n + 2:]
        x, y, c = _position()
        me, my_chip = 4 * x + 2 * y + c, 2 * x + y
        others = [me ^ k for k in range(1, N_DEV)]
        ids = [(o // 4, (o // 2) % 2, o % 2) for o in others]

        barrier = pltpu.get_barrier_semaphore()
        for peer in ids:
            pl.semaphore_signal(barrier, inc=1, device_id=peer, device_id_type=MESH)
        pl.semaphore_wait(barrier, N_DEV - 1)

        mine = pltpu.make_async_copy(slab_ref, slabs_ref.at[me], local_sem)
        mine.start()
        sends = [
            pltpu.make_async_remote_copy(
                src_ref=part_refs[a].at[j], dst_ref=landed_refs[a].at[j],
                send_sem=send_sems.at[3 * a + j], recv_sem=recv_sems.at[3 * a + j],
                device_id=((my_chip ^ (j + 1)) // 2, (my_chip ^ (j + 1)) % 2, c), device_id_type=MESH)
            for a in range(n) for j in range(3)]
        sends += [
            pltpu.make_async_remote_copy(
                src_ref=slab_ref, dst_ref=slabs_ref.at[me],
                send_sem=send_sems.at[3 * n + k], recv_sem=recv_sems.at[3 * n + k],
                device_id=ids[k], device_id_type=MESH)
            for k in range(N_DEV - 1)]
        arrivals = sends[:3 * n] + [
            pltpu.make_async_remote_copy(
                src_ref=slab_ref, dst_ref=slabs_ref.at[others[k]],
                send_sem=send_sems.at[3 * n + k], recv_sem=recv_sems.at[3 * n + k],
                device_id=ids[k], device_id_type=MESH)
            for k in range(N_DEV - 1)]
        for cp in sends:
            cp.start()
        for cp in arrivals:
            cp.wait_recv()
        for cp in sends:
            cp.wait_send()
        mine.wait()

    n_sems = 3 * n + N_DEV - 1
    outs = pl.kernel(
        body, name=name,
        out_type=[jax.ShapeDtypeStruct(p.shape, p.dtype) for p in partials]
                 + [jax.ShapeDtypeStruct((N_DEV,) + slab.shape, slab.dtype)],
        mesh=plsc.ScalarSubcoreMesh(axis_name="sequencer", num_cores=1),
        scratch_types=[pltpu.SemaphoreType.DMA((n_sems,)), pltpu.SemaphoreType.DMA((n_sems,)), pltpu.SemaphoreType.DMA],
        compiler_params=pltpu.CompilerParams(collective_id=collective_id),
    )(*partials, slab)
    return outs[:n], outs[n]


def _sum_devices(gathered, after, *, name):
    _, R, C = gathered.shape

    def body(in_ref, after_ref, out_ref):
        total = in_ref[0]
        for d in range(1, N_DEV):
            total = total + in_ref[d]
        out_ref[...] = total

    return pl.pallas_call(
        body, name=name, grid=(1,),
        in_specs=[pl.BlockSpec((N_DEV, R, C), lambda i: (0, 0, 0)), AFTER],
        out_specs=pl.BlockSpec((R, C), lambda i: (0, 0)),
        out_shape=jax.ShapeDtypeStruct((R, C), F32),
        compiler_params=_params(("arbitrary",)),
    )(gathered, _in_hbm(after))


def _owner_sum_adamw(own, landed, w, m, v, after, *, transposed, name, group=None, into=()):
    H, R, C = w.shape
    tr = R // 2
    first_group = 0 if group is None else group

    def body(own_ref, landed_ref, w_ref, m_ref, v_ref, after_ref, *rest):
        g_ref, d_ref, nm_ref, nv_ref = rest[len(into):]
        total = own_ref[...]
        for j in range(3):
            total = total + landed_ref[j].astype(F32)
        if transposed:
            total = total.T
        g_ref[...] = total
        d_ref[...], nm_ref[...], nv_ref[...] = _adamw_update(w_ref[...], total, m_ref[...], v_ref[...])

    spec = pl.BlockSpec((None, tr, C), lambda h, i: (first_group + h, i, 0))
    if transposed:
        own_spec = pl.BlockSpec((None, C, tr), lambda h, i: (h, 0, i))
        landed_spec = pl.BlockSpec((3, None, C, tr), lambda h, i: (0, h, 0, i))
    else:
        own_spec = pl.BlockSpec((None, tr, C), lambda h, i: (h, i, 0))
        landed_spec = pl.BlockSpec((3, None, tr, C), lambda h, i: (0, h, i, 0))
    n_in = 6
    return pl.pallas_call(
        body, name=name, grid=(own.shape[0], R // tr),
        in_specs=[own_spec, landed_spec, spec, spec, spec, AFTER] + [pl.BlockSpec(memory_space=pl.ANY)] * len(into),
        out_specs=[spec] * 4,
        out_shape=[jax.ShapeDtypeStruct((H, R, C), F32)] * 4,
        input_output_aliases={n_in + j: j for j in range(len(into))},
        compiler_params=_params(("arbitrary", "arbitrary")),
    )(own, landed, w, m, v, _in_hbm(after), *into)


def _local_step(x, target, norms, pool_w_group, pool_scale, wgu1, wd1, w_in, wbp, wba, w_out, wgu2, wd2, exchange):
    n1g, nmg, n2g, nfg = norms
    D = x.shape[1]
    gu1, hid1 = _ffn_up(x, n1g, wgu1, tm=1024, name="ffn1_up")
    h1 = _ffn_down(x, hid1, wd1, tm=512, name="ffn1_down")
    un, proj = _inproj_fwd(h1, nmg, w_in, tm=1024, name="inproj_fwd")
    p = _pool_fwd(proj, pool_w_group, pool_scale, name="pool_fwd")
    o, ltot = _attn_fwd(proj, name="attn_fwd")
    h2, m = _mix_fwd(h1, p, o, proj, wbp, wba, w_out, tm=512, name="mix_fwd")
    gu2, hid2 = _ffn_up(h2, n2g, wgu2, tm=1024, name="ffn2_up")
    h3 = _ffn_down(h2, hid2, wd2, tm=512, name="ffn2_down")
    dh3, df2, loss, d_nf = _loss_bwd(h3, target, nfg, tm=512, name="loss_bwd")

    dh2, d_n2, n2, dgu2 = _ffn_bwd(dh3, df2, h2, n2g, gu2, wgu2, wd2, df2, tm=512, name="ffn2_bwd")
    d_wd2 = _wgrad_down(hid2, df2, tk=WGRAD_TOKENS, name="ffn2_wgrad_down")
    d_wgu2 = _wgrad_gate_up(n2, dgu2, tk=WGRAD_TOKENS, name="ffn2_wgrad_gate_up")
    (g_wd2, g_wgu2), token = exchange("ffn2", [d_wd2.reshape(N_DEV, FF_SHARD_PAD, D), d_wgu2])

    dyp, dys, dp, do, dgl = _mix_bwd(dh2, p, o, proj, wbp, wba, w_out, token, tm=512, name="mix_bwd")
    d_wout = _wgrad_full(m, dh2, tk=WGRAD_TOKENS, name="wgrad_out")
    d_wbp = _wgrad_full(dyp, p, tk=WGRAD_TOKENS, name="wgrad_branch_pool")
    d_wba = _wgrad_full(dys, o, tk=WGRAD_TOKENS, name="wgrad_branch_attn")
    by_owner = lambda g: g.reshape(N_DEV, g.shape[0] // N_DEV, g.shape[1])
    (g_wbp, g_wba, g_wout), token = exchange("mix", [by_owner(d_wbp), by_owner(d_wba), by_owner(d_wout)])
    dxp, d_wgroup, d_scale = _pool_bwd(dp, proj, pool_w_group, pool_scale, name="pool_bwd")
    dq, dk, dv = _attn_bwd(proj, do, ltot, token, name="attn_bwd")
    dproj_parts = [dxp, dq, dk, dv, dgl]
    dh1, df1, d_nm = _inproj_bwd(dproj_parts, dh2, h1, nmg, w_in, tm=512, name="inproj_bwd")
    d_win = _wgrad_in(dproj_parts, un, name="wgrad_in")
    d_wd1 = _wgrad_down(hid1, df1, tk=WGRAD_TOKENS, name="ffn1_wgrad_down")
    (g_win, g_wd1, replicated_early), token = exchange(
        "w_in_ffn1_down", [d_win, d_wd1.reshape(N_DEV, FF_SHARD_PAD, D), d_nm, d_n2, d_nf, d_scale, d_wgroup, loss])

    dx, d_n1, n1, dgu1 = _ffn_bwd(dh1, df1, x, n1g, gu1, wgu1, wd1, token, tm=512, name="ffn1_bwd")
    d_wgu1_a = _wgrad_gate_up(n1, dgu1, tk=WGRAD_TOKENS, name="ffn1_wgrad_gate_up_a", part=0, parts=2)
    (g_wgu1_a, replicated_late), token = exchange("ffn1_gate_up_a", [d_wgu1_a, d_n1])
    d_wgu1_b = _wgrad_gate_up(n1, dgu1, tk=WGRAD_TOKENS, name="ffn1_wgrad_gate_up_b", part=1, parts=2)
    (g_wgu1_b,), token = exchange("last", [d_wgu1_b])
    g_wgu1 = (g_wgu1_a, g_wgu1_b)

    sharded = (g_wgu1, g_wd1, g_win, g_wbp, g_wba, g_wout, g_wgu2, g_wd2)
    return dx, sharded, (replicated_late, replicated_early), token


def _hidden_major(w):
    return jnp.swapaxes(w[0], 0, 1)


def _pad_gate_up(wt):
    d = wt.shape[1]
    wt = wt.astype(BF16).reshape(2, FF_SHARD, d)
    return jnp.pad(wt, ((0, 0), (0, FF_SHARD_PAD - FF_SHARD), (0, 0))).reshape(2 * FF_SHARD_PAD, d)


def _unpad_gate_up(gt):
    d = gt.shape[1]
    return gt.reshape(2, FF_SHARD_PAD, d)[:, :FF_SHARD].reshape(2 * FF_SHARD, d)


def _pad_down(w):
    return jnp.pad(w.astype(BF16), ((0, FF_SHARD_PAD - FF_SHARD), (0, 0)))


def kernel(x, ffn1_norm, ffn1_w_gate_up, ffn1_w_down, mix_norm, w_in, pool_w_group, pool_scale, w_branch_pool, w_branch_attn, w_out, ffn2_norm, ffn2_w_gate_up, ffn2_w_down, final_norm, loss_target, m_ffn1_norm, m_ffn1_w_gate_up, m_ffn1_w_down, m_mix_norm, m_w_in, m_pool_w_group, m_pool_scale, m_w_branch_pool, m_w_branch_attn, m_w_out, m_ffn2_norm, m_ffn2_w_gate_up, m_ffn2_w_down, m_final_norm, v_ffn1_norm, v_ffn1_w_gate_up, v_ffn1_w_down, v_mix_norm, v_w_in, v_pool_w_group, v_pool_scale, v_w_branch_pool, v_w_branch_attn, v_w_out, v_ffn2_norm, v_ffn2_w_gate_up, v_ffn2_w_down, v_final_norm):
    D = x.shape[-1]
    weights = dict(ffn1_norm=ffn1_norm, ffn1_w_gate_up=ffn1_w_gate_up, ffn1_w_down=ffn1_w_down, mix_norm=mix_norm,
                   w_in=w_in, pool_w_group=pool_w_group, pool_scale=pool_scale, w_branch_pool=w_branch_pool,
                   w_branch_attn=w_branch_attn, w_out=w_out, ffn2_norm=ffn2_norm, ffn2_w_gate_up=ffn2_w_gate_up,
                   ffn2_w_down=ffn2_w_down, final_norm=final_norm)
    first = dict(ffn1_norm=m_ffn1_norm, ffn1_w_gate_up=m_ffn1_w_gate_up, ffn1_w_down=m_ffn1_w_down,
                 mix_norm=m_mix_norm, w_in=m_w_in, pool_w_group=m_pool_w_group, pool_scale=m_pool_scale,
                 w_branch_pool=m_w_branch_pool, w_branch_attn=m_w_branch_attn, w_out=m_w_out,
                 ffn2_norm=m_ffn2_norm, ffn2_w_gate_up=m_ffn2_w_gate_up, ffn2_w_down=m_ffn2_w_down,
                 final_norm=m_final_norm)
    second = dict(ffn1_norm=v_ffn1_norm, ffn1_w_gate_up=v_ffn1_w_gate_up, ffn1_w_down=v_ffn1_w_down,
                  mix_norm=v_mix_norm, w_in=v_w_in, pool_w_group=v_pool_w_group, pool_scale=v_pool_scale,
                  w_branch_pool=v_w_branch_pool, w_branch_attn=v_w_branch_attn, w_out=v_w_out,
                  ffn2_norm=v_ffn2_norm, ffn2_w_gate_up=v_ffn2_w_gate_up, ffn2_w_down=v_ffn2_w_down,
                  final_norm=v_final_norm)
    order = list(weights)

    wgu1, = _all_gather([_pad_gate_up(_hidden_major(ffn1_w_gate_up))], name="all_gather_ffn1_gate_up", collective_id=0)
    wd1, = _all_gather([_pad_down(ffn1_w_down[0])], name="all_gather_ffn1_down", collective_id=10)
    transposed = lambda w: jnp.swapaxes(w[0], 0, 1).astype(BF16)
    win_g, = _all_gather([transposed(w_in)], name="all_gather_w_in", collective_id=1)
    wbp_g, wba_g = _all_gather([transposed(w_branch_pool), transposed(w_branch_attn)],
                               name="all_gather_branches", collective_id=2)
    wout_g, = _all_gather([w_out[0].astype(BF16)], name="all_gather_w_out", collective_id=11)
    wgu2, wd2 = _all_gather([_pad_gate_up(_hidden_major(ffn2_w_gate_up)), _pad_down(ffn2_w_down[0])],
                            name="all_gather_ffn2", collective_id=3)
    whole = lambda g: g.reshape(g.shape[0] * g.shape[1], g.shape[2])
    wd1, wd2, win_g, wbp_g, wba_g, wout_g = (whole(g) for g in (wd1, wd2, win_g, wbp_g, wba_g, wout_g))

    cross_ids = {"ffn2": 4, "mix": 5, "w_in_ffn1_down": 8, "ffn1_gate_up_a": 9, "last": 7}
    small = ["ffn1_norm", "mix_norm", "ffn2_norm", "final_norm", "pool_scale", "pool_w_group"]

    def tile_rows(a):
        a = a.reshape(-1, 128)
        return jnp.pad(a, ((0, -a.shape[0] % 8), (0, 0)))

    def exchange(tag, group):
        grads = [g for g in group if g.dtype == BF16]
        extras = [tile_rows(g) for g in group if g.dtype != BF16]
        sums = _chip_sums(grads, name="chip_sums_" + tag)
        partials = [s[0] for s in sums]
        handles = []
        if extras:
            landed, slabs = _cross_chips_and_gather(partials, jnp.concatenate(extras, axis=0),
                                                    name="cross_chips_" + tag, collective_id=cross_ids[tag])
            handles = [slabs]
        else:
            landed = _cross_chips(partials, name="cross_chips_" + tag, collective_id=cross_ids[tag])
        return [(s[1], l) for s, l in zip(sums, landed)] + handles, sums[-1][1]

    norms = (ffn1_norm, mix_norm, ffn2_norm, final_norm.reshape(1, D))
    dx, sharded, (slabs_late, slabs_early), last = _local_step(
        x[0], loss_target[0], norms, pool_w_group[0], pool_scale, wgu1, wd1, win_g, wbp_g, wba_g, wout_g, wgu2, wd2,
        exchange)
    names = ["ffn1_w_gate_up", "ffn1_w_down", "w_in", "w_branch_pool", "w_branch_attn", "w_out",
             "ffn2_w_gate_up", "ffn2_w_down"]
    handles = dict(zip(names, sharded))
    grads, delta, new_m, new_v = {}, {}, {}, {}
    loss_out = []

    def update_replicated(after):
        rows = [weights[k].size // 128 for k in small]
        padded_rows = [-(-r // 8) * 8 for r in rows]
        starts = [sum(padded_rows[:i]) for i in range(len(rows) + 1)]
        total = jnp.concatenate([_sum_devices(slabs_late, after, name="sum_replicated_late"),
                                 _sum_devices(slabs_early, after, name="sum_replicated_early")], axis=0)
        loss_out.append(total[starts[-1], 0])
        small_w = jnp.concatenate([tile_rows(weights[k]) for k in small], axis=0)
        small_m = jnp.concatenate([tile_rows(first[k]) for k in small], axis=0)
        small_v = jnp.concatenate([tile_rows(second[k]) for k in small], axis=0)
        small_out = _adamw(small_w, total[:starts[-1]], small_m, small_v, name="adamw_replicated")
        for name_, start, n_rows in zip(small, starts, rows):
            shape = weights[name_].shape
            grads[name_] = total[start:start + n_rows].reshape(shape)
            delta[name_], new_m[name_], new_v[name_] = (a[start:start + n_rows].reshape(shape) for a in small_out)
        return small_out[0]

    after = last
    for k in ("ffn2_w_down", "ffn2_w_gate_up", "w_branch_pool", "w_branch_attn", "w_out", "w_in", "ffn1_w_down",
              "ffn1_w_gate_up"):
        hidden_major = k.endswith("w_gate_up")
        view = _hidden_major if hidden_major else (lambda a: a[0])
        back = (lambda a: jnp.swapaxes(a, 0, 1)[None]) if hidden_major else (lambda a: a[None])
        groups = 2 if hidden_major else 1
        by_group = lambda a: a.reshape(a.shape[:-2] + (groups, a.shape[-2] // groups, a.shape[-1]))
        state = [by_group(view(a[k])) for a in (weights, first, second)]
        if isinstance(handles[k][0], tuple):
            (own_a, landed_a), (own_b, landed_b) = handles[k]
            out = _owner_sum_adamw(own_a[None], landed_a[:, None], *state, after, name="adamw_" + k + "_a",
                                   transposed=False, group=0)
            out = _owner_sum_adamw(own_b[None], landed_b[:, None], *state, update_replicated(out[1]),
                                   name="adamw_" + k + "_b", transposed=False, group=1, into=out)
        else:
            own, landed = handles[k]
            out = _owner_sum_adamw(by_group(own), by_group(landed), *state, after, name="adamw_" + k,
                                   transposed=k in ("w_in", "w_branch_pool", "w_branch_attn"))
        after = out[1]
        grads[k], delta[k], new_m[k], new_v[k] = (back(a.reshape(-1, a.shape[-1])) for a in out)

    return (loss_out[0], dx[None], *[grads[k] for k in order], *[delta[k] for k in order],
            *[new_m[k] for k in order], *[new_v[k] for k in order])
```
